```python
import math
import jax, jax.numpy as jnp
from jax import lax
import numpy as np

D_MODEL = 1024
BATCH = 8
SEQ = 4096
DEPTH = 4

A_HEADS = 8
A_HEAD_DIM = 64
A_WIDTH = A_HEADS * A_HEAD_DIM
DILATED_PATTERNS = ((128, 1), (512, 4), (2048, 16))
MLA_HEADS = 4
QK_NOPE = 128
QK_ROPE = 64
V_DIM = 128
Q_LORA = 256
KV_LORA = 128
MLA_WIDTH = MLA_HEADS * V_DIM
MIX_WIDTH = A_WIDTH + MLA_WIDTH
IN_COLS = 3 * A_WIDTH + Q_LORA + KV_LORA + QK_ROPE
Q_BLOCK = 128
D_FF = 4 * D_MODEL
ROPE_THETA = 10000.0
ALPHA = (2.0 * DEPTH) ** 0.25
BETA_INIT = (8.0 * DEPTH) ** -0.25
LN_EPS = 1e-5
RMS_EPS = 1e-6

kernel_name = 'hybrid_dilated_mla_deepnorm'


def _layer_norm(x, g, b):
    x32 = x.astype(jnp.float32)
    mu = jnp.mean(x32, -1, keepdims=True)
    var = jnp.mean(jnp.square(x32 - mu), -1, keepdims=True)
    y = (x32 - mu) * lax.rsqrt(var + LN_EPS)
    return (y * g.astype(jnp.float32) + b.astype(jnp.float32)).astype(x.dtype)


def _rms_norm(x, g):
    x32 = x.astype(jnp.float32)
    y = x32 * lax.rsqrt(jnp.mean(jnp.square(x32), -1, keepdims=True) + RMS_EPS)
    return (y * g.astype(jnp.float32)).astype(x.dtype)


def _rope(x, pos):
    half = x.shape[-1] // 2
    inv_freq = ROPE_THETA ** (-jnp.arange(half, dtype=jnp.float32) / half)
    ang = pos.astype(jnp.float32)[:, None] * inv_freq[None, :]
    cos = jnp.cos(ang)[None, :, None, :]
    sin = jnp.sin(ang)[None, :, None, :]
    x1 = x[..., :half].astype(jnp.float32)
    x2 = x[..., half:].astype(jnp.float32)
    return jnp.concatenate([x1 * cos - x2 * sin, x2 * cos + x1 * sin], -1).astype(x.dtype)


def _dilated_window_attention(q, k, v, window, dilation):
    B, S, H, D = q.shape
    span = window // dilation
    L = S // dilation
    nb = -(-L // span)
    Lp = nb * span

    def to_sub(t):
        t = t.reshape(B, L, dilation, H, D).transpose(0, 2, 1, 3, 4).reshape(B * dilation, L, H, D)
        t = jnp.pad(t, ((0, 0), (0, Lp - L), (0, 0), (0, 0)))
        return t.reshape(B * dilation, nb, span, H, D)

    def band(tb):
        prev = jnp.pad(tb[:, :-1], ((0, 0), (1, 0), (0, 0), (0, 0), (0, 0)))
        return jnp.concatenate([prev, tb], axis=2)

    qb = to_sub(q)
    kb = band(to_sub(k))
    vb = band(to_sub(v))
    s = jnp.einsum('znqhd,znkhd->znhqk', qb, kb).astype(jnp.float32) * (D ** -0.5)
    qi = jnp.arange(span)[:, None]
    kj = jnp.arange(2 * span)[None, :]
    dist = span + qi - kj
    in_band = (dist >= 0) & (dist <= span)
    has_prev = (jnp.arange(nb)[:, None, None] > 0) | (kj >= span)[None]
    mask = (in_band[None] & has_prev)[None, :, None]
    s = jnp.where(mask, s, -jnp.inf)
    m = jnp.max(s, -1, keepdims=True)
    p = jnp.exp(s - m)
    den = jnp.sum(p, -1, keepdims=True)
    o = jnp.einsum('znhqk,znkhd->znqhd', (p / den).astype(v.dtype), vb)
    lse = (m + jnp.log(den))[..., 0].transpose(0, 1, 3, 2)

    def from_sub(t):
        t = t.reshape((B * dilation, Lp) + t.shape[3:])[:, :L]
        t = t.reshape((B, dilation, L) + t.shape[2:])
        t = jnp.moveaxis(t, 1, 2)
        return t.reshape((B, S) + t.shape[3:])

    return from_sub(o), from_sub(lse)


def _dilated_mixture(q, k, v):
    outs, lses = [], []
    for window, dilation in DILATED_PATTERNS:
        o, lse = _dilated_window_attention(q, k, v, window, dilation)
        outs.append(o)
        lses.append(lse)
    w = jax.nn.softmax(jnp.stack(lses), axis=0)
    o = jnp.sum(w[..., None] * jnp.stack(outs).astype(jnp.float32), axis=0)
    return o.astype(q.dtype)


def _mla_attention(c_q, c_kv, k_r, q_a_g, kv_a_g, w_uq, w_ukv, pos):
    B, S, _ = c_q.shape
    q = jnp.einsum('bsr,re->bse', _rms_norm(c_q, q_a_g), w_uq).reshape(B, S, MLA_HEADS, QK_NOPE + QK_ROPE)
    q_nope, q_pe = q[..., :QK_NOPE], q[..., QK_NOPE:]
    q_pe = _rope(q_pe, pos)
    kv = jnp.einsum('bsr,re->bse', _rms_norm(c_kv, kv_a_g), w_ukv).reshape(B, S, MLA_HEADS, QK_NOPE + V_DIM)
    k_nope, v = kv[..., :QK_NOPE], kv[..., QK_NOPE:]
    k_pe = _rope(k_r[:, :, None, :], pos)
    q_full = jnp.concatenate([q_nope, q_pe], -1)
    k_full = jnp.concatenate([k_nope, jnp.broadcast_to(k_pe, (B, S, MLA_HEADS, QK_ROPE))], -1)
    scale = (QK_NOPE + QK_ROPE) ** -0.5
    nq = S // Q_BLOCK
    qb = q_full.reshape(B, nq, Q_BLOCK, MLA_HEADS, QK_NOPE + QK_ROPE).transpose(1, 0, 2, 3, 4)
    kpos = jnp.arange(S)

    def one_block(args):
        i, qblk = args
        s = jnp.einsum('bqhd,bkhd->bhqk', qblk, k_full).astype(jnp.float32) * scale
        qpos = i * Q_BLOCK + jnp.arange(Q_BLOCK)
        s = jnp.where(kpos[None, :] <= qpos[:, None], s, -jnp.inf)
        p = jax.nn.softmax(s, axis=-1)
        return jnp.einsum('bhqk,bkhd->bqhd', p.astype(v.dtype), v)

    o = lax.map(one_block, (jnp.arange(nq), qb))
    return o.transpose(1, 0, 2, 3, 4).reshape(B, S, MLA_WIDTH)


def _fwd_setup_inputs(seed: int = 0) -> dict:
    key = jax.random.key(seed)
    ks = jax.random.split(key, 16)

    def nrm(k, shape, scale):
        return jax.random.normal(k, shape, jnp.float32) * scale

    return {
        'x': nrm(ks[0], (BATCH, SEQ, D_MODEL), 1.0),
        'w_in': nrm(ks[1], (DEPTH, D_MODEL, IN_COLS), D_MODEL ** -0.5),
        'q_a_norm': 1.0 + nrm(ks[2], (DEPTH, Q_LORA), 0.02),
        'kv_a_norm': 1.0 + nrm(ks[3], (DEPTH, KV_LORA), 0.02),
        'w_uq': nrm(ks[4], (DEPTH, Q_LORA, MLA_HEADS * (QK_NOPE + QK_ROPE)), Q_LORA ** -0.5),
        'w_ukv': nrm(ks[5], (DEPTH, KV_LORA, MLA_HEADS * (QK_NOPE + V_DIM)), KV_LORA ** -0.5),
        'a_out_norm': 1.0 + nrm(ks[6], (DEPTH, A_WIDTH), 0.02),
        'b_out_norm': 1.0 + nrm(ks[7], (DEPTH, MLA_WIDTH), 0.02),
        'w_o': nrm(ks[8], (DEPTH, MIX_WIDTH, D_MODEL), MIX_WIDTH ** -0.5 * BETA_INIT),
        'ln1_g': 1.0 + nrm(ks[9], (DEPTH, D_MODEL), 0.02),
        'ln1_b': nrm(ks[10], (DEPTH, D_MODEL), 0.02),
        'w_ff1': nrm(ks[11], (DEPTH, D_MODEL, D_FF), D_MODEL ** -0.5),
        'w_ff2': nrm(ks[12], (DEPTH, D_FF, D_MODEL), D_FF ** -0.5 * BETA_INIT),
        'ln2_g': 1.0 + nrm(ks[13], (DEPTH, D_MODEL), 0.02),
        'ln2_b': nrm(ks[14], (DEPTH, D_MODEL), 0.02),
    }


def _fwd_reference(x, w_in, q_a_norm, kv_a_norm, w_uq, w_ukv, a_out_norm, b_out_norm, w_o,
              ln1_g, ln1_b, w_ff1, w_ff2, ln2_g, ln2_b):
    B, S, _ = x.shape
    pos = jnp.arange(S, dtype=jnp.int32)
    splits = [A_WIDTH, 2 * A_WIDTH, 3 * A_WIDTH, 3 * A_WIDTH + Q_LORA, 3 * A_WIDTH + Q_LORA + KV_LORA]
    for l in range(DEPTH):
        h = jnp.einsum('bsd,de->bse', x, w_in[l])
        qa, ka, va, c_q, c_kv, k_r = jnp.split(h, splits, axis=-1)
        qa = _rope(qa.reshape(B, S, A_HEADS, A_HEAD_DIM), pos)
        ka = _rope(ka.reshape(B, S, A_HEADS, A_HEAD_DIM), pos)
        va = va.reshape(B, S, A_HEADS, A_HEAD_DIM)
        a_out = _dilated_mixture(qa, ka, va).reshape(B, S, A_WIDTH)
        b_out = _mla_attention(c_q, c_kv, k_r, q_a_norm[l], kv_a_norm[l], w_uq[l], w_ukv[l], pos)
        mixed = jnp.concatenate([_rms_norm(a_out, a_out_norm[l]), _rms_norm(b_out, b_out_norm[l])], -1)
        y = jnp.einsum('bse,ed->bsd', mixed, w_o[l])
        x = _layer_norm(ALPHA * x + y, ln1_g[l], ln1_b[l])
        f = jnp.square(jax.nn.relu(jnp.einsum('bsd,df->bsf', x, w_ff1[l])))
        f = jnp.einsum('bsf,fd->bsd', f, w_ff2[l])
        x = _layer_norm(ALPHA * x + f, ln2_g[l], ln2_b[l])
    return x


import jax as _jax
import jax.numpy as _jnp

TWIN_FORMAT = 'train_step'
FWD_PARAMS = ['x', 'w_in', 'q_a_norm', 'kv_a_norm', 'w_uq', 'w_ukv', 'a_out_norm', 'b_out_norm', 'w_o', 'ln1_g', 'ln1_b', 'w_ff1', 'w_ff2', 'ln2_g', 'ln2_b']
TWIN_WEIGHTS = ['w_in', 'q_a_norm', 'kv_a_norm', 'w_uq', 'w_ukv', 'a_out_norm', 'b_out_norm', 'w_o', 'ln1_g', 'ln1_b', 'w_ff1', 'w_ff2', 'ln2_g', 'ln2_b']
TWIN_DIFF_INPUT = 'x'
TWIN_INPUTS = ['x', 'w_in', 'q_a_norm', 'kv_a_norm', 'w_uq', 'w_ukv', 'a_out_norm', 'b_out_norm', 'w_o', 'ln1_g', 'ln1_b', 'w_ff1', 'w_ff2', 'ln2_g', 'ln2_b', 'loss_target', 'm_w_in', 'm_q_a_norm', 'm_kv_a_norm', 'm_w_uq', 'm_w_ukv', 'm_a_out_norm', 'm_b_out_norm', 'm_w_o', 'm_ln1_g', 'm_ln1_b', 'm_w_ff1', 'm_w_ff2', 'm_ln2_g', 'm_ln2_b', 'v_w_in', 'v_q_a_norm', 'v_kv_a_norm', 'v_w_uq', 'v_w_ukv', 'v_a_out_norm', 'v_b_out_norm', 'v_w_o', 'v_ln1_g', 'v_ln1_b', 'v_w_ff1', 'v_w_ff2', 'v_ln2_g', 'v_ln2_b']
TWIN_OUTPUTS = ['loss', 'grad_x', 'grad_w_in', 'grad_q_a_norm', 'grad_kv_a_norm', 'grad_w_uq', 'grad_w_ukv', 'grad_a_out_norm', 'grad_b_out_norm', 'grad_w_o', 'grad_ln1_g', 'grad_ln1_b', 'grad_w_ff1', 'grad_w_ff2', 'grad_ln2_g', 'grad_ln2_b', 'delta_w_in', 'delta_q_a_norm', 'delta_kv_a_norm', 'delta_w_uq', 'delta_w_ukv', 'delta_a_out_norm', 'delta_b_out_norm', 'delta_w_o', 'delta_ln1_g', 'delta_ln1_b', 'delta_w_ff1', 'delta_w_ff2', 'delta_ln2_g', 'delta_ln2_b', 'new_m_w_in', 'new_m_q_a_norm', 'new_m_kv_a_norm', 'new_m_w_uq', 'new_m_w_ukv', 'new_m_a_out_norm', 'new_m_b_out_norm', 'new_m_w_o', 'new_m_ln1_g', 'new_m_ln1_b', 'new_m_w_ff1', 'new_m_w_ff2', 'new_m_ln2_g', 'new_m_ln2_b', 'new_v_w_in', 'new_v_q_a_norm', 'new_v_kv_a_norm', 'new_v_w_uq', 'new_v_w_ukv', 'new_v_a_out_norm', 'new_v_b_out_norm', 'new_v_w_o', 'new_v_ln1_g', 'new_v_ln1_b', 'new_v_w_ff1', 'new_v_w_ff2', 'new_v_ln2_g', 'new_v_ln2_b']
TWIN_LEAF_KINDS = {'loss': 'loss', 'grad_x': 'grad_x', 'grad_w_in': 'grad_w', 'grad_q_a_norm': 'grad_w', 'grad_kv_a_norm': 'grad_w', 'grad_w_uq': 'grad_w', 'grad_w_ukv': 'grad_w', 'grad_a_out_norm': 'grad_w', 'grad_b_out_norm': 'grad_w', 'grad_w_o': 'grad_w', 'grad_ln1_g': 'grad_w', 'grad_ln1_b': 'grad_w', 'grad_w_ff1': 'grad_w', 'grad_w_ff2': 'grad_w', 'grad_ln2_g': 'grad_w', 'grad_ln2_b': 'grad_w', 'delta_w_in': 'delta_w', 'delta_q_a_norm': 'delta_w', 'delta_kv_a_norm': 'delta_w', 'delta_w_uq': 'delta_w', 'delta_w_ukv': 'delta_w', 'delta_a_out_norm': 'delta_w', 'delta_b_out_norm': 'delta_w', 'delta_w_o': 'delta_w', 'delta_ln1_g': 'delta_w', 'delta_ln1_b': 'delta_w', 'delta_w_ff1': 'delta_w', 'delta_w_ff2': 'delta_w', 'delta_ln2_g': 'delta_w', 'delta_ln2_b': 'delta_w', 'new_m_w_in': 'new_m', 'new_m_q_a_norm': 'new_m', 'new_m_kv_a_norm': 'new_m', 'new_m_w_uq': 'new_m', 'new_m_w_ukv': 'new_m', 'new_m_a_out_norm': 'new_m', 'new_m_b_out_norm': 'new_m', 'new_m_w_o': 'new_m', 'new_m_ln1_g': 'new_m', 'new_m_ln1_b': 'new_m', 'new_m_w_ff1': 'new_m', 'new_m_w_ff2': 'new_m', 'new_m_ln2_g': 'new_m', 'new_m_ln2_b': 'new_m', 'new_v_w_in': 'new_v', 'new_v_q_a_norm': 'new_v', 'new_v_kv_a_norm': 'new_v', 'new_v_w_uq': 'new_v', 'new_v_w_ukv': 'new_v', 'new_v_a_out_norm': 'new_v', 'new_v_b_out_norm': 'new_v', 'new_v_w_o': 'new_v', 'new_v_ln1_g': 'new_v', 'new_v_ln1_b': 'new_v', 'new_v_w_ff1': 'new_v', 'new_v_w_ff2': 'new_v', 'new_v_ln2_g': 'new_v', 'new_v_ln2_b': 'new_v'}


def _forward(args):
    return _fwd_reference(*[args[k] for k in FWD_PARAMS])


def _output_shape():
    out = _jax.eval_shape(lambda: _forward(_fwd_setup_inputs(0)))
    return out.shape, out.dtype

N_MICROBATCH = 1
ADAM_LR = 0.001
ADAM_B1 = 0.9
ADAM_B2 = 0.999
ADAM_EPS = 1e-08
ADAM_WD = 0.01
ADAM_STEP = 10
PER_EXAMPLE_BATCH_AXIS = {'x': 0, 'loss_target': 0}
SHARED_INPUTS = []
_WEIGHT_DTYPES = {'w_in': _jnp.float32, 'q_a_norm': _jnp.float32, 'kv_a_norm': _jnp.float32, 'w_uq': _jnp.float32, 'w_ukv': _jnp.float32, 'a_out_norm': _jnp.float32, 'b_out_norm': _jnp.float32, 'w_o': _jnp.float32, 'ln1_g': _jnp.float32, 'ln1_b': _jnp.float32, 'w_ff1': _jnp.float32, 'w_ff2': _jnp.float32, 'ln2_g': _jnp.float32, 'ln2_b': _jnp.float32}
MOMENT_SCALE = {'w_in': 7.188339e-02, 'q_a_norm': 4.055174e-02, 'kv_a_norm': 2.599530e-01, 'w_uq': 2.374070e-02, 'w_ukv': 7.481219e-02, 'a_out_norm': 9.429545e-02, 'b_out_norm': 1.055042e-01, 'w_o': 2.485725e-01, 'ln1_g': 7.335729e-01, 'ln1_b': 9.935210e-01, 'w_ff1': 3.144566e-02, 'w_ff2': 2.580929e-01, 'ln2_g': 1.633103e+01, 'ln2_b': 5.734986e+00}


def _to_microbatches(a, axis):
    t = _jnp.moveaxis(a, axis, 0)
    t = t.reshape((N_MICROBATCH, t.shape[0] // N_MICROBATCH) + t.shape[1:])
    return _jnp.moveaxis(t, 1, axis + 1)


def setup_inputs(seed: int = 0) -> dict:
    inp = _fwd_setup_inputs(seed)
    key = _jax.random.fold_in(_jax.random.key(seed), 7919)
    shape, _ = _output_shape()
    out = dict(inp)
    out["loss_target"] = _jax.random.normal(_jax.random.fold_in(key, 0), shape, _jnp.float32)
    for i, name in enumerate(TWIN_WEIGHTS):
        w = inp[name].astype(_jnp.float32)
        if MOMENT_SCALE is None:
            s = _jnp.sqrt(_jnp.mean(_jnp.square(w)) + 1e-30)
        else:
            s = MOMENT_SCALE[name]
        km, kv = _jax.random.split(_jax.random.fold_in(key, i + 1))
        out[name] = w
        out["m_" + name] = s * _jax.random.normal(km, w.shape, _jnp.float32)
        out["v_" + name] = (s * s) * _jax.random.uniform(kv, w.shape, _jnp.float32, 0.5, 1.5)
    if N_MICROBATCH > 1:
        for name, axis in PER_EXAMPLE_BATCH_AXIS.items():
            out[name] = _to_microbatches(out[name], axis)
    return {'x': out['x'], 'w_in': out['w_in'], 'q_a_norm': out['q_a_norm'], 'kv_a_norm': out['kv_a_norm'], 'w_uq': out['w_uq'], 'w_ukv': out['w_ukv'], 'a_out_norm': out['a_out_norm'], 'b_out_norm': out['b_out_norm'], 'w_o': out['w_o'], 'ln1_g': out['ln1_g'], 'ln1_b': out['ln1_b'], 'w_ff1': out['w_ff1'], 'w_ff2': out['w_ff2'], 'ln2_g': out['ln2_g'], 'ln2_b': out['ln2_b'], 'loss_target': out['loss_target'], 'm_w_in': out['m_w_in'], 'm_q_a_norm': out['m_q_a_norm'], 'm_kv_a_norm': out['m_kv_a_norm'], 'm_w_uq': out['m_w_uq'], 'm_w_ukv': out['m_w_ukv'], 'm_a_out_norm': out['m_a_out_norm'], 'm_b_out_norm': out['m_b_out_norm'], 'm_w_o': out['m_w_o'], 'm_ln1_g': out['m_ln1_g'], 'm_ln1_b': out['m_ln1_b'], 'm_w_ff1': out['m_w_ff1'], 'm_w_ff2': out['m_w_ff2'], 'm_ln2_g': out['m_ln2_g'], 'm_ln2_b': out['m_ln2_b'], 'v_w_in': out['v_w_in'], 'v_q_a_norm': out['v_q_a_norm'], 'v_kv_a_norm': out['v_kv_a_norm'], 'v_w_uq': out['v_w_uq'], 'v_w_ukv': out['v_w_ukv'], 'v_a_out_norm': out['v_a_out_norm'], 'v_b_out_norm': out['v_b_out_norm'], 'v_w_o': out['v_w_o'], 'v_ln1_g': out['v_ln1_g'], 'v_ln1_b': out['v_ln1_b'], 'v_w_ff1': out['v_w_ff1'], 'v_w_ff2': out['v_w_ff2'], 'v_ln2_g': out['v_ln2_g'], 'v_ln2_b': out['v_ln2_b']}


def _loss(weights, diff, rest, loss_target):
    with _jax.named_scope("forward"):
        args = {**rest, TWIN_DIFF_INPUT: diff, **{k: w.astype(_WEIGHT_DTYPES[k]) for k, w in weights.items()}}
        y = _forward(args)
    with _jax.named_scope("loss_head"):
        err = _jnp.square(y.astype(_jnp.float32) - loss_target)
        return 0.5 * _jnp.sum(_jnp.mean(err, axis=-1)) if err.ndim else 0.5 * err


def _adamw(w, g, m, v):
    m = ADAM_B1 * m + (1.0 - ADAM_B1) * g
    v = ADAM_B2 * v + (1.0 - ADAM_B2) * _jnp.square(g)
    m_hat = m / (1.0 - ADAM_B1 ** ADAM_STEP)
    v_hat = v / (1.0 - ADAM_B2 ** ADAM_STEP)
    delta = -ADAM_LR * (m_hat / (_jnp.sqrt(v_hat) + ADAM_EPS) + ADAM_WD * w)
    return delta, m, v


def reference(x, w_in, q_a_norm, kv_a_norm, w_uq, w_ukv, a_out_norm, b_out_norm, w_o, ln1_g, ln1_b, w_ff1, w_ff2, ln2_g, ln2_b, loss_target, m_w_in, m_q_a_norm, m_kv_a_norm, m_w_uq, m_w_ukv, m_a_out_norm, m_b_out_norm, m_w_o, m_ln1_g, m_ln1_b, m_w_ff1, m_w_ff2, m_ln2_g, m_ln2_b, v_w_in, v_q_a_norm, v_kv_a_norm, v_w_uq, v_w_ukv, v_a_out_norm, v_b_out_norm, v_w_o, v_ln1_g, v_ln1_b, v_w_ff1, v_w_ff2, v_ln2_g, v_ln2_b):
    given = dict(x=x, w_in=w_in, q_a_norm=q_a_norm, kv_a_norm=kv_a_norm, w_uq=w_uq, w_ukv=w_ukv, a_out_norm=a_out_norm, b_out_norm=b_out_norm, w_o=w_o, ln1_g=ln1_g, ln1_b=ln1_b, w_ff1=w_ff1, w_ff2=w_ff2, ln2_g=ln2_g, ln2_b=ln2_b, loss_target=loss_target, m_w_in=m_w_in, m_q_a_norm=m_q_a_norm, m_kv_a_norm=m_kv_a_norm, m_w_uq=m_w_uq, m_w_ukv=m_w_ukv, m_a_out_norm=m_a_out_norm, m_b_out_norm=m_b_out_norm, m_w_o=m_w_o, m_ln1_g=m_ln1_g, m_ln1_b=m_ln1_b, m_w_ff1=m_w_ff1, m_w_ff2=m_w_ff2, m_ln2_g=m_ln2_g, m_ln2_b=m_ln2_b, v_w_in=v_w_in, v_q_a_norm=v_q_a_norm, v_kv_a_norm=v_kv_a_norm, v_w_uq=v_w_uq, v_w_ukv=v_w_ukv, v_a_out_norm=v_a_out_norm, v_b_out_norm=v_b_out_norm, v_w_o=v_w_o, v_ln1_g=v_ln1_g, v_ln1_b=v_ln1_b, v_w_ff1=v_w_ff1, v_w_ff2=v_w_ff2, v_ln2_g=v_ln2_g, v_ln2_b=v_ln2_b)
    weights = {n: given[n] for n in TWIN_WEIGHTS}
    shared = {n: given[n] for n in SHARED_INPUTS}
    per_example = {n: given[n] for n in ['x']}
    grad_fn = _jax.value_and_grad(_loss, argnums=(0, 1))

    def one_microbatch(ex, loss_target):
        ex = dict(ex)
        diff = ex.pop(TWIN_DIFF_INPUT)
        return grad_fn(weights, diff, {**shared, **ex}, loss_target)

    if N_MICROBATCH == 1:
        loss, (grad_w, grad_x) = one_microbatch(per_example, given["loss_target"])
    else:
        def body(carry, xs):
            loss_sum, grad_sum = carry
            l_k, (gw_k, gx_k) = one_microbatch(xs[0], xs[1])
            with _jax.named_scope("update"):
                return (loss_sum + l_k, _jax.tree.map(_jnp.add, grad_sum, gw_k)), gx_k

        init = (_jnp.zeros((), _jnp.float32), _jax.tree.map(_jnp.zeros_like, weights))
        (loss, grad_w), grad_x = _jax.lax.scan(body, init, (per_example, given["loss_target"]))
    with _jax.named_scope("update"):
        delta_w, new_m, new_v = {}, {}, {}
        for n in TWIN_WEIGHTS:
            delta_w[n], new_m[n], new_v[n] = _adamw(weights[n], grad_w[n], given["m_" + n], given["v_" + n])
    return (loss, grad_x, *[grad_w[n] for n in TWIN_WEIGHTS], *[delta_w[n] for n in TWIN_WEIGHTS],
            *[new_m[n] for n in TWIN_WEIGHTS], *[new_v[n] for n in TWIN_WEIGHTS])
```

```python
import functools

import jax
import jax.numpy as jnp
from jax import lax
from jax.experimental import pallas as pl
from jax.experimental.pallas import tpu as pltpu

F32, BF16 = jnp.float32, jnp.bfloat16

D_MODEL = 1024
DEPTH = 4
A_HEAD_DIM = 64
A_HEADS = 8
A_WIDTH = 512
PATTERNS = ((128, 1), (512, 4), (2048, 16))
SPAN = 128
MLA_HEADS = 4
QK_NOPE = 128
QK_ROPE = 64
V_DIM = 128
Q_LORA = 256
KV_LORA = 128
MLA_WIDTH = 512
QK_PAD = 256
IN_COLS = 1984
IN_PAD = 2048
UQ_PAD = 1024
D_FF = 4096
ROPE_THETA = 10000.0
ALPHA = (2.0 * DEPTH) ** 0.25
LN_EPS = 1e-5
RMS_EPS = 1e-6
MLA_SCALE = (QK_NOPE + QK_ROPE) ** -0.5
A_SCALE = A_HEAD_DIM ** -0.5

ADAM_LR, ADAM_B1, ADAM_B2, ADAM_EPS, ADAM_WD, ADAM_STEP = 0.001, 0.9, 0.999, 1e-08, 0.01, 10

VMEM_LIMIT_BYTES = 56 * 1024 * 1024
NEG = -1e30
N_CHIPS, N_DEV = 4, 8
LANES = 128

PIECE_SHAPES = (
    ("w_in", (512, 496)), ("w_uq", (128, 192)), ("w_ukv", (64, 256)),
    ("w_o", (128, 1024)), ("w_ff1", (512, 1024)), ("w_ff2", (512, 1024)),
)
PIECE_ROWS = sum(r * c for _, (r, c) in PIECE_SHAPES) // LANES
SMALL_NAMES = ("q_a_norm", "kv_a_norm", "a_out_norm", "b_out_norm", "ln1_g", "ln1_b", "ln2_g", "ln2_b")
SMALL_SIZES = (256, 128, 512, 512, 1024, 1024, 1024, 1024)
SMALL_ROWS = 176


def _params(sem):
    return pltpu.CompilerParams(dimension_semantics=sem, vmem_limit_bytes=VMEM_LIMIT_BYTES)


def _rowwise(name, fn, ins, outs, consts=(), reds=(), ts=512):
    n_rows = ins[0].shape[-2]
    ts = min(ts, n_rows)
    assert n_rows % ts == 0
    n_in, n_c, n_o, n_r = len(ins), len(consts), len(outs), len(reds)

    def tile_spec(shape):
        if len(shape) == 2:
            return pl.BlockSpec((ts, shape[1]), lambda i: (i, 0))
        return pl.BlockSpec((shape[0], ts, shape[2]), lambda i: (0, i, 0))

    def whole_spec(shape):
        return pl.BlockSpec(shape, lambda i: (0,) * len(shape))

    def body(*refs):
        in_refs, c_refs = refs[:n_in], refs[n_in:n_in + n_c]
        o_refs, r_refs = refs[n_in + n_c:n_in + n_c + n_o], refs[n_in + n_c + n_o:]
        res = fn(*[r[...] for r in in_refs], *[r[...] for r in c_refs])
        res = tuple(res) if isinstance(res, (tuple, list)) else (res,)
        for r, v in zip(o_refs, res[:n_o]):
            if len(r.shape) == 3:
                for g in range(r.shape[0]):
                    r[g] = v[g].astype(r.dtype)
            else:
                r[...] = v.astype(r.dtype)
        if n_r:
            i = pl.program_id(0)

            @pl.when(i == 0)
            def _():
                for r, v in zip(r_refs, res[n_o:]):
                    r[...] = v

            @pl.when(i > 0)
            def _():
                for r, v in zip(r_refs, res[n_o:]):
                    r[...] += v

    out = pl.pallas_call(
        body, name=name, grid=(n_rows // ts,),
        in_specs=[tile_spec(a.shape) for a in ins] + [whole_spec(c.shape) for c in consts],
        out_specs=[tile_spec(o.shape) for o in outs] + [whole_spec(r.shape) for r in reds],
        out_shape=list(outs) + list(reds),
        compiler_params=_params(("arbitrary",)),
    )(*ins, *consts)
    return out


def _sds(shape, dtype=F32):
    return jax.ShapeDtypeStruct(tuple(shape), dtype)


def _mm(name, a, b, out_dtypes, *, ta=False, tb=False, tm=512, tn=512, tk=512, epilogue=None, extras=(), vecs=()):
    (k_dim, m_dim) = a.shape if ta else a.shape[::-1]
    (n_dim, k2) = b.shape if tb else b.shape[::-1]
    assert k_dim == k2
    tm, tn, tk = min(tm, m_dim), min(tn, n_dim), min(tk, k_dim)
    assert m_dim % tm == 0 and n_dim % tn == 0 and k_dim % tk == 0
    nk = k_dim // tk
    a_spec = pl.BlockSpec((tk, tm), lambda i, j, k: (k, i)) if ta else pl.BlockSpec((tm, tk), lambda i, j, k: (i, k))
    b_spec = pl.BlockSpec((tn, tk), lambda i, j, k: (j, k)) if tb else pl.BlockSpec((tk, tn), lambda i, j, k: (k, j))
    dims = (((0 if ta else 1,), (1 if tb else 0,)), ((), ()))
    n_e, n_v, n_o = len(extras), len(vecs), len(out_dtypes)

    def body(a_ref, b_ref, *refs):
        e_refs, v_refs = refs[:n_e], refs[n_e:n_e + n_v]
        o_refs = refs[n_e + n_v:n_e + n_v + n_o]
        part = lax.dot_general(a_ref[...].astype(BF16), b_ref[...].astype(BF16), dims, preferred_element_type=F32)

        def finish(acc):
            outs = epilogue(acc, *[r[...] for r in e_refs], *[r[...] for r in v_refs]) if epilogue else (acc,)
            for r, v in zip(o_refs, outs):
                r[...] = v.astype(r.dtype)

        if nk == 1:
            finish(part)
        else:
            acc_ref = refs[-1]
            k = pl.program_id(2)

            @pl.when(k == 0)
            def _():
                acc_ref[...] = part

            @pl.when(k > 0)
            def _():
                acc_ref[...] += part

            @pl.when(k == nk - 1)
            def _():
                finish(acc_ref[...])

    tile = pl.BlockSpec((tm, tn), lambda i, j, k: (i, j))
    out = pl.pallas_call(
        body, name=name, grid=(m_dim // tm, n_dim // tn, nk),
        in_specs=[a_spec, b_spec] + [tile] * n_e + [pl.BlockSpec((1, tn), lambda i, j, k: (0, j))] * n_v,
        out_specs=[tile] * n_o,
        out_shape=[_sds((m_dim, n_dim), d) for d in out_dtypes],
        scratch_shapes=[pltpu.VMEM((tm, tn), F32)] if nk > 1 else [],
        compiler_params=_params(("parallel", "parallel", "arbitrary")),
    )(a, b, *extras, *vecs)
    return out


def _swap_halves(x):
    width = x.shape[1]
    lane = lax.broadcasted_iota(jnp.int32, x.shape, 1)
    return jnp.where((lane % 64) < 32, pltpu.roll(x, width - 32, 1), pltpu.roll(x, 32, 1))


def _rope(x, cos, sin_signed):
    return x * cos + _swap_halves(x) * sin_signed


def _rope_t(d, cos, sin_signed):
    return d * cos - _swap_halves(d) * sin_signed


def _rms(x, g):
    rstd = lax.rsqrt(jnp.mean(x * x, axis=-1, keepdims=True) + RMS_EPS)
    return x * rstd * g


def _rms_bwd(x, g, dy):
    rstd = lax.rsqrt(jnp.mean(x * x, axis=-1, keepdims=True) + RMS_EPS)
    xh = x * rstd
    dyg = dy * g
    dx = rstd * (dyg - xh * jnp.mean(dyg * xh, axis=-1, keepdims=True))
    return dx, jnp.sum(dy * xh, axis=0, keepdims=True)


def _layer_norm(r, g, b):
    mu = jnp.mean(r, axis=-1, keepdims=True)
    xc = r - mu
    var = jnp.mean(xc * xc, axis=-1, keepdims=True)
    return xc * lax.rsqrt(var + LN_EPS) * g + b


def _dot(a, b, dims):
    return lax.dot_general(a, b, (dims, ((), ())), preferred_element_type=F32)


_NN, _NT, _TN = ((1,), (0,)), ((1,), (1,)), ((0,), (0,))


def _in_proj(x_bf, w_in_p, cos, sin):
    n_rows = x_bf.shape[0]
    ts = 512

    def body(x_ref, w_ref, cos_ref, sin_ref, qa_ref, ka_ref, va_ref, cq_ref, kpe_ref, ckv_ref):
        xb = x_ref[...]
        cos_v, sin_v = cos_ref[...], sin_ref[...]

        def seg(lo, hi):
            return _dot(xb, w_ref[:, lo:hi], _NN)

        qa_ref[...] = (_rope(seg(0, 512), cos_v, sin_v) * A_SCALE).astype(BF16)
        ka_ref[...] = _rope(seg(512, 1024), cos_v, sin_v).astype(BF16)
        va_ref[...] = seg(1024, 1536).astype(BF16)
        cq_ref[...] = seg(1536, 1792)
        kpe_ref[...] = _rope(seg(1792, 1920), cos_v[:, :LANES], sin_v[:, :LANES]).astype(BF16)
        ckv_ref[...] = seg(1920, 2048)

    row = lambda c: pl.BlockSpec((ts, c), lambda i: (i, 0))
    return pl.pallas_call(
        body, name="in_proj", grid=(n_rows // ts,),
        in_specs=[row(D_MODEL), pl.BlockSpec((D_MODEL, IN_PAD), lambda i: (0, 0)), row(A_WIDTH), row(A_WIDTH)],
        out_specs=[row(A_WIDTH), row(A_WIDTH), row(A_WIDTH), row(Q_LORA), row(LANES), row(KV_LORA)],
        out_shape=[_sds((n_rows, A_WIDTH), BF16), _sds((n_rows, A_WIDTH), BF16), _sds((n_rows, A_WIDTH), BF16),
                   _sds((n_rows, Q_LORA)), _sds((n_rows, LANES), BF16), _sds((n_rows, KV_LORA))],
        compiler_params=_params(("parallel",)),
    )(x_bf, w_in_p, cos, sin)


def _band_masks(m):
    qi = lax.broadcasted_iota(jnp.int32, (SPAN, SPAN), 0)
    kj = lax.broadcasted_iota(jnp.int32, (SPAN, SPAN), 1)
    return (kj >= qi) & (m > 0), kj <= qi


def _dil_view(t, dil):
    rows, width = t.shape
    return t.reshape(rows // dil, dil * width)


def _dil_fwd(q, k, v, dil):
    n_rows = q.shape[0]
    nb = n_rows // dil // SPAN
    qv, kv_, vv = _dil_view(q, dil), _dil_view(k, dil), _dil_view(v, dil)

    def body(q_ref, kp_ref, kc_ref, vp_ref, vc_ref, o_ref, lse_ref):
        mask_p, mask_c = _band_masks(pl.program_id(1))
        for h in range(A_HEADS):
            sl = slice(h * A_HEAD_DIM, (h + 1) * A_HEAD_DIM)
            qh = q_ref[:, sl]
            sp = jnp.where(mask_p, _dot(qh, kp_ref[:, sl], _NT), NEG)
            sc = jnp.where(mask_c, _dot(qh, kc_ref[:, sl], _NT), NEG)
            mx = jnp.maximum(jnp.max(sp, axis=-1, keepdims=True), jnp.max(sc, axis=-1, keepdims=True))
            pp, pc = jnp.exp(sp - mx), jnp.exp(sc - mx)
            den = jnp.sum(pp, axis=-1, keepdims=True) + jnp.sum(pc, axis=-1, keepdims=True)
            inv = 1.0 / den
            o = _dot((pp * inv).astype(BF16), vp_ref[:, sl], _NN) + _dot((pc * inv).astype(BF16), vc_ref[:, sl], _NN)
            o_ref[:, sl] = o
            lse_ref[:, sl] = jnp.broadcast_to(mx + jnp.log(den), (SPAN, A_HEAD_DIM))

    cur = pl.BlockSpec((SPAN, A_WIDTH), lambda r, m: (m, r))
    prev = pl.BlockSpec((SPAN, A_WIDTH), lambda r, m: (jnp.maximum(m - 1, 0), r))
    o, lse = pl.pallas_call(
        body, name=f"dil_fwd_d{dil}", grid=(dil, nb),
        in_specs=[cur, prev, cur, prev, cur],
        out_specs=[cur, cur],
        out_shape=[_sds(qv.shape), _sds(qv.shape)],
        compiler_params=_params(("parallel", "arbitrary")),
    )(qv, kv_, kv_, vv, vv)
    return o.reshape(n_rows, A_WIDTH), lse.reshape(n_rows, A_WIDTH)


def _mix_fwd(os_, ls_, b_out, a_g, b_g):
    def fn(o1, o2, o3, l1, l2, l3, b, ag, bg):
        mx = jnp.maximum(jnp.maximum(l1, l2), l3)
        e1, e2, e3 = jnp.exp(l1 - mx), jnp.exp(l2 - mx), jnp.exp(l3 - mx)
        a = (e1 * o1 + e2 * o2 + e3 * o3) / (e1 + e2 + e3)
        return a, jnp.concatenate([_rms(a, ag), _rms(b, bg)], axis=1)

    n_rows = b_out.shape[0]
    return _rowwise("mix_fwd", fn, [*os_, *ls_, b_out], [_sds((n_rows, A_WIDTH)), _sds((n_rows, 2 * A_WIDTH), BF16)],
                    consts=[a_g, b_g])


def _mla_q_prep(cq, g, w_uq_p, cos, sin):
    def fn(cq_v, cos_v, sin_v, g_v, w_v):
        cqn = _rms(cq_v, g_v).astype(BF16)
        q = _dot(cqn, w_v, _NN)
        qf = [jnp.concatenate([q[:, h * QK_NOPE:(h + 1) * QK_NOPE],
                               _rope(q[:, 512 + h * LANES:512 + (h + 1) * LANES], cos_v[:, :LANES], sin_v[:, :LANES])], axis=1)
              for h in range(MLA_HEADS)]
        return cqn, qf

    n_rows = cq.shape[0]
    return _rowwise("mla_q_prep", fn, [cq, cos, sin], [_sds((n_rows, Q_LORA), BF16), _sds((MLA_HEADS, n_rows, QK_PAD), BF16)],
                    consts=[g, w_uq_p])


def _mla_kv_prep(ckv, kpe, g, w_ukv_p):
    def fn(ckv_v, kpe_v, g_v, w_v):
        ckvn = _rms(ckv_v, g_v).astype(BF16)
        kvv = _dot(ckvn, w_v, _NN)
        kf = [jnp.concatenate([kvv[:, h * QK_NOPE:(h + 1) * QK_NOPE], kpe_v.astype(F32)], axis=1) for h in range(MLA_HEADS)]
        return ckvn, kf, kvv[:, 512:]

    n_rows = ckv.shape[0]
    return _rowwise("mla_kv_prep", fn, [ckv, kpe],
                    [_sds((n_rows, KV_LORA), BF16), _sds((MLA_HEADS, n_rows, QK_PAD), BF16), _sds((n_rows, MLA_WIDTH), BF16)],
                    consts=[g, w_ukv_p])


MLA_TQ = 512


def _mla_fwd(qf, kf, v):
    n_rows = v.shape[0]
    t = min(MLA_TQ, n_rows)
    nq = n_rows // t

    def body(q_ref, k_ref, v_ref, o_ref, lse_ref, m_sc, l_sc, acc_sc):
        i, j = pl.program_id(1), pl.program_id(2)

        @pl.when(j == 0)
        def _():
            m_sc[...] = jnp.full(m_sc.shape, NEG, F32)
            l_sc[...] = jnp.zeros(l_sc.shape, F32)
            acc_sc[...] = jnp.zeros(acc_sc.shape, F32)

        @pl.when(j <= i)
        def _():
            s = _dot(q_ref[...], k_ref[...], _NT) * MLA_SCALE
            row = lax.broadcasted_iota(jnp.int32, (t, t), 0)
            col = lax.broadcasted_iota(jnp.int32, (t, t), 1)
            s = jnp.where((j < i) | (col <= row), s, NEG)
            m_prev = m_sc[...]
            m_new = jnp.maximum(m_prev, jnp.max(s, axis=-1, keepdims=True))
            alpha = jnp.exp(m_prev - m_new)
            p = jnp.exp(s - m_new)
            l_sc[...] = alpha * l_sc[...] + jnp.sum(p, axis=-1, keepdims=True)
            acc_sc[...] = alpha * acc_sc[...] + _dot(p.astype(BF16), v_ref[...], _NN)
            m_sc[...] = m_new

        @pl.when(j == i)
        def _():
            o_ref[...] = acc_sc[...] / l_sc[...]
            lse_ref[...] = jnp.broadcast_to(m_sc[...] + jnp.log(l_sc[...]), (t, V_DIM))

    return pl.pallas_call(
        body, name="mla_fwd", grid=(MLA_HEADS, nq, nq),
        in_specs=[pl.BlockSpec((None, t, QK_PAD), lambda h, i, j: (h, i, 0)),
                  pl.BlockSpec((None, t, QK_PAD), lambda h, i, j: (h, jnp.minimum(j, i), 0)),
                  pl.BlockSpec((t, V_DIM), lambda h, i, j: (jnp.minimum(j, i), h))],
        out_specs=[pl.BlockSpec((t, V_DIM), lambda h, i, j: (i, h)), pl.BlockSpec((t, V_DIM), lambda h, i, j: (i, h))],
        out_shape=[_sds((n_rows, MLA_WIDTH)), _sds((n_rows, MLA_WIDTH))],
        scratch_shapes=[pltpu.VMEM((t, 1), F32), pltpu.VMEM((t, 1), F32), pltpu.VMEM((t, V_DIM), F32)],
        compiler_params=_params(("parallel", "parallel", "arbitrary")),
    )(qf, kf, v)


def _mm_res_ln(name, a, w, xres, g, b, tk=512):
    def epi(acc, xr, g_v, b_v):
        r = ALPHA * xr + acc
        y = _layer_norm(r, g_v, b_v)
        return y, y, r

    return _mm(name, a, w, (F32, BF16, F32), tm=512, tn=D_MODEL, tk=tk, epilogue=epi, extras=[xres], vecs=[g, b])


def _mm_relu2(x_bf, w):
    def epi(acc):
        r = jnp.maximum(acc, 0.0)
        return acc, r * r

    return _mm("ff1", x_bf, w, (F32, BF16), tm=512, tn=1024, tk=D_MODEL, epilogue=epi)


def _loss_fn(y, t):
    def fn(y_v, t_v):
        d = y_v - t_v
        part = jnp.sum(jnp.sum(d * d, axis=1, keepdims=True), axis=0, keepdims=True)
        return d * (1.0 / D_MODEL), part

    dy, part = _rowwise("loss", fn, [y, t], [_sds(y.shape)], reds=[_sds((1, 1))])
    return part * (0.5 / D_MODEL), dy


def _ln_bwd(name, dy, r, g):
    def fn(dy_v, r_v, g_v):
        mu = jnp.mean(r_v, axis=-1, keepdims=True)
        xc = r_v - mu
        rstd = lax.rsqrt(jnp.mean(xc * xc, axis=-1, keepdims=True) + LN_EPS)
        xh = xc * rstd
        dxh = dy_v * g_v
        dr = rstd * (dxh - jnp.mean(dxh, axis=-1, keepdims=True) - xh * jnp.mean(dxh * xh, axis=-1, keepdims=True))
        return dr, dr, jnp.sum(dy_v * xh, axis=0, keepdims=True), jnp.sum(dy_v, axis=0, keepdims=True)

    return _rowwise(name, fn, [dy, r], [_sds(dy.shape), _sds(dy.shape, BF16)], consts=[g],
                    reds=[_sds((1, D_MODEL)), _sds((1, D_MODEL))])


def _head_sum_matrix():
    i = lax.broadcasted_iota(jnp.int32, (A_WIDTH, A_WIDTH), 0) // A_HEAD_DIM
    j = lax.broadcasted_iota(jnp.int32, (A_WIDTH, A_WIDTH), 1) // A_HEAD_DIM
    return (i == j).astype(BF16)


def _mix_bwd(dmixed, a_out, b_out, ls_, a_g, b_g):
    def fn(dm, a, b, l1, l2, l3, ag, bg, e_mat):
        da, dga = _rms_bwd(a, ag, dm[:, :A_WIDTH])
        db, dgb = _rms_bwd(b, bg, dm[:, A_WIDTH:])
        t = da * a
        t_hi = t.astype(BF16)
        t_lo = (t - t_hi.astype(F32)).astype(BF16)
        tsum = _dot(t_hi, e_mat, _NN) + _dot(t_lo, e_mat, _NN)
        mx = jnp.maximum(jnp.maximum(l1, l2), l3)
        e1, e2, e3 = jnp.exp(l1 - mx), jnp.exp(l2 - mx), jnp.exp(l3 - mx)
        inv = 1.0 / (e1 + e2 + e3)
        w1, w2, w3 = e1 * inv, e2 * inv, e3 * inv
        tb = db * b
        delta_b = jnp.concatenate(
            [jnp.broadcast_to(jnp.sum(tb[:, h * V_DIM:(h + 1) * V_DIM], axis=-1, keepdims=True), (tb.shape[0], V_DIM))
             for h in range(MLA_HEADS)], axis=1)
        return w1 * da, w2 * da, w3 * da, w1 * tsum, w2 * tsum, w3 * tsum, db, delta_b, dga, dgb

    n_rows = a_out.shape[0]
    wide = (n_rows, A_WIDTH)
    return _rowwise("mix_bwd", fn, [dmixed, a_out, b_out, *ls_],
                    [_sds(wide, BF16)] * 3 + [_sds(wide)] * 3 + [_sds(wide, BF16), _sds(wide)],
                    consts=[a_g, b_g, _head_sum_matrix()], reds=[_sds((1, A_WIDTH)), _sds((1, MLA_WIDTH))], ts=256)


def _dil_bwd(q, k, v, lse, do, dl, dil):
    n_rows = q.shape[0]
    nb = n_rows // dil // SPAN
    qv, kv_, vv = _dil_view(q, dil), _dil_view(k, dil), _dil_view(v, dil)
    lv, dov, dlv = _dil_view(lse, dil), _dil_view(do, dil), _dil_view(dl, dil)

    def body(q_ref, kp_ref, kc_ref, vp_ref, vc_ref, lse_ref, do_ref, dl_ref, dq_ref, dk_ref, dv_ref, ck, cv):
        m = pl.program_id(1)

        @pl.when(m == 0)
        def _():
            ck[...] = jnp.zeros(ck.shape, F32)
            cv[...] = jnp.zeros(cv.shape, F32)

        @pl.when(m < nb)
        def _():
            mask_p, mask_c = _band_masks(m)
            for h in range(A_HEADS):
                sl = slice(h * A_HEAD_DIM, (h + 1) * A_HEAD_DIM)
                qh, doh = q_ref[:, sl], do_ref[:, sl]
                kp, kc, vp, vc = kp_ref[:, sl], kc_ref[:, sl], vp_ref[:, sl], vc_ref[:, sl]
                lse_h = lse_ref[:, h * A_HEAD_DIM:h * A_HEAD_DIM + 1]
                dl_h = dl_ref[:, h * A_HEAD_DIM:h * A_HEAD_DIM + 1]
                pp = jnp.exp(jnp.where(mask_p, _dot(qh, kp, _NT), NEG) - lse_h)
                pc = jnp.exp(jnp.where(mask_c, _dot(qh, kc, _NT), NEG) - lse_h)
                dsp = (pp * (_dot(doh, vp, _NT) - dl_h)).astype(BF16)
                dsc = (pc * (_dot(doh, vc, _NT) - dl_h)).astype(BF16)
                dq_ref[:, sl] = _dot(dsp, kp, _NN) + _dot(dsc, kc, _NN)
                dk_ref[:, sl] = ck[:, sl] + _dot(dsp, qh, _TN)
                dv_ref[:, sl] = cv[:, sl] + _dot(pp.astype(BF16), doh, _TN)
                ck[:, sl] = _dot(dsc, qh, _TN)
                cv[:, sl] = _dot(pc.astype(BF16), doh, _TN)

        @pl.when(m == nb)
        def _():
            dk_ref[...] = ck[...]
            dv_ref[...] = cv[...]

    last = nb - 1
    cur = pl.BlockSpec((SPAN, A_WIDTH), lambda r, m: (jnp.minimum(m, last), r))
    prev = pl.BlockSpec((SPAN, A_WIDTH), lambda r, m: (jnp.clip(m - 1, 0, last), r))
    dq, dk, dv = pl.pallas_call(
        body, name=f"dil_bwd_d{dil}", grid=(dil, nb + 1),
        in_specs=[cur, prev, cur, prev, cur, cur, cur, cur],
        out_specs=[cur, prev, prev],
        out_shape=[_sds(qv.shape)] * 3,
        scratch_shapes=[pltpu.VMEM((SPAN, A_WIDTH), F32), pltpu.VMEM((SPAN, A_WIDTH), F32)],
        compiler_params=_params(("parallel", "arbitrary")),
    )(qv, kv_, kv_, vv, vv, lv, dov, dlv)
    return dq.reshape(n_rows, A_WIDTH), dk.reshape(n_rows, A_WIDTH), dv.reshape(n_rows, A_WIDTH)


def _mla_bwd(qf, kf, v, lse, do, delta):
    n_rows = v.shape[0]
    t = min(MLA_TQ, n_rows)
    nq = n_rows // t

    def body(q_ref, k_ref, v_ref, lse_ref, do_ref, dl_ref, dq_ref, dk_ref, dv_ref, dk_sc, dv_sc):
        j, i = pl.program_id(1), pl.program_id(2)

        @pl.when((j == 0) & (i == 0))
        def _():
            dq_ref[...] = jnp.zeros(dq_ref.shape, F32)

        @pl.when(i == j)
        def _():
            dk_sc[...] = jnp.zeros(dk_sc.shape, F32)
            dv_sc[...] = jnp.zeros(dv_sc.shape, F32)

        @pl.when(i >= j)
        def _():
            q, k, dob = q_ref[...], k_ref[...], do_ref[...]
            s = _dot(q, k, _NT) * MLA_SCALE
            row = lax.broadcasted_iota(jnp.int32, (t, t), 0)
            col = lax.broadcasted_iota(jnp.int32, (t, t), 1)
            s = jnp.where((j < i) | (col <= row), s, NEG)
            p = jnp.exp(s - lse_ref[:, :1])
            dp = _dot(dob, v_ref[...], _NT)
            ds = (p * (dp - dl_ref[:, :1]) * MLA_SCALE).astype(BF16)
            dv_sc[...] += _dot(p.astype(BF16), dob, _TN)
            dk_sc[...] += _dot(ds, q, _TN)
            rows = pl.ds(pl.multiple_of(i * t, t), t)
            dq_ref[rows, :] += _dot(ds, k, _NN)

        @pl.when(i == nq - 1)
        def _():
            dk_ref[...] = dk_sc[...]
            dv_ref[...] = dv_sc[...]

    qrow = lambda h, j, i: (jnp.maximum(i, j), h)
    return pl.pallas_call(
        body, name="mla_bwd", grid=(MLA_HEADS, nq, nq),
        in_specs=[pl.BlockSpec((None, t, QK_PAD), lambda h, j, i: (h, jnp.maximum(i, j), 0)),
                  pl.BlockSpec((None, t, QK_PAD), lambda h, j, i: (h, j, 0)),
                  pl.BlockSpec((t, V_DIM), lambda h, j, i: (j, h)),
                  pl.BlockSpec((t, V_DIM), qrow), pl.BlockSpec((t, V_DIM), qrow), pl.BlockSpec((t, V_DIM), qrow)],
        out_specs=[pl.BlockSpec((None, n_rows, QK_PAD), lambda h, j, i: (h, 0, 0)),
                   pl.BlockSpec((None, t, QK_PAD), lambda h, j, i: (h, j, 0)),
                   pl.BlockSpec((t, V_DIM), lambda h, j, i: (j, h))],
        out_shape=[_sds((MLA_HEADS, n_rows, QK_PAD)), _sds((MLA_HEADS, n_rows, QK_PAD)), _sds((n_rows, MLA_WIDTH))],
        scratch_shapes=[pltpu.VMEM((t, QK_PAD), F32), pltpu.VMEM((t, V_DIM), F32)],
        compiler_params=_params(("parallel", "arbitrary", "arbitrary")),
    )(qf, kf, v, lse, do, delta)


def _mla_bwd_prep(dqf, dkf, dv, cos, sin):
    def fn(dqf_v, dkf_v, dv_v, cos_v, sin_v):
        dq = jnp.concatenate([dqf_v[h][:, :QK_NOPE] for h in range(MLA_HEADS)]
                             + [_rope_t(dqf_v[h][:, QK_NOPE:], cos_v[:, :LANES], sin_v[:, :LANES]) for h in range(MLA_HEADS)], axis=1)
        dkv = jnp.concatenate([dkf_v[h][:, :QK_NOPE] for h in range(MLA_HEADS)] + [dv_v], axis=1)
        dkpe = dkf_v[0][:, QK_NOPE:] + dkf_v[1][:, QK_NOPE:] + dkf_v[2][:, QK_NOPE:] + dkf_v[3][:, QK_NOPE:]
        return dq, dkv, dkpe

    n_rows = dv.shape[0]
    return _rowwise("mla_bwd_prep", fn, [dqf, dkf, dv, cos, sin],
                    [_sds((n_rows, UQ_PAD), BF16), _sds((n_rows, 1024), BF16), _sds((n_rows, LANES))], ts=256)


def _assemble_dh(dqs, dks, dvs, dcqn, cq, dckvn, ckv, dkpe, cos, sin, gq, gkv):
    def fn(q1, q2, q3, k1, k2, k3, v1, v2, v3, dcqn_v, cq_v, dckvn_v, ckv_v, dkpe_v, cos_v, sin_v, gq_v, gkv_v):
        dqa = _rope_t(q1 + q2 + q3, cos_v, sin_v) * A_SCALE
        dka = _rope_t(k1 + k2 + k3, cos_v, sin_v)
        dcq, dgq = _rms_bwd(cq_v, gq_v, dcqn_v)
        dckv, dgkv = _rms_bwd(ckv_v, gkv_v, dckvn_v)
        dkr = _rope_t(dkpe_v, cos_v[:, :LANES], sin_v[:, :LANES])
        return jnp.concatenate([dqa, dka, v1 + v2 + v3, dcq, dkr, dckv], axis=1), dgq, dgkv

    n_rows = cq.shape[0]
    return _rowwise("assemble_dh", fn, [*dqs, *dks, *dvs, dcqn, cq, dckvn, ckv, dkpe, cos, sin],
                    [_sds((n_rows, IN_PAD), BF16)], consts=[gq, gkv],
                    reds=[_sds((1, Q_LORA)), _sds((1, KV_LORA))], ts=256)


def _layer_fwd(x_f32, x_bf, w, sm, cos, sin):
    qa, ka, va, cq, kpe, ckv = _in_proj(x_bf, w["w_in"], cos, sin)
    os_, ls_ = [], []
    for _, dil in PATTERNS:
        o, lse = _dil_fwd(qa, ka, va, dil)
        os_.append(o)
        ls_.append(lse)
    cqn, qf = _mla_q_prep(cq, sm["q_a_norm"], w["w_uq"], cos, sin)
    ckvn, kf, v = _mla_kv_prep(ckv, kpe, sm["kv_a_norm"], w["w_ukv"])
    b_out, b_lse = _mla_fwd(qf, kf, v)
    a_out, mixed = _mix_fwd(os_, ls_, b_out, sm["a_out_norm"], sm["b_out_norm"])
    x1, x1_bf, r1 = _mm_res_ln("wo_ln1", mixed, w["w_o"], x_f32, sm["ln1_g"], sm["ln1_b"])
    u, f = _mm_relu2(x1_bf, w["w_ff1"])
    x2, x2_bf, r2 = _mm_res_ln("ff2_ln2", f, w["w_ff2"], x1, sm["ln2_g"], sm["ln2_b"])
    saved = dict(x_bf=x_bf, qa=qa, ka=ka, va=va, cq=cq, ckv=ckv, ls=ls_, cqn=cqn, qf=qf, ckvn=ckvn, kf=kf, v=v,
                 b_out=b_out, b_lse=b_lse, a_out=a_out, mixed=mixed, x1_bf=x1_bf, r1=r1, u=u, f=f, r2=r2)
    return x2, x2_bf, saved


def _layer_bwd(dx2, w, sm, sv, cos, sin):
    dr2, dr2_bf, dg2, db2 = _ln_bwd("ln2_bwd", dx2, sv["r2"], sm["ln2_g"])
    (dw_ff2,) = _mm("dw_ff2", sv["f"], dr2_bf, (F32,), ta=True, tm=1024, tn=1024, tk=512)
    (du,) = _mm("d_u", dr2_bf, w["w_ff2"], (BF16,), tb=True, tm=512, tn=1024, tk=D_MODEL,
                epilogue=lambda acc, u: (acc * (2.0 * jnp.maximum(u, 0.0)),), extras=[sv["u"]])
    (dw_ff1,) = _mm("dw_ff1", sv["x1_bf"], du, (F32,), ta=True, tm=1024, tn=1024, tk=512)
    (dy1,) = _mm("d_x1", du, w["w_ff1"], (F32,), tb=True, tm=512, tn=1024, tk=1024,
                 epilogue=lambda acc, d: (acc + ALPHA * d,), extras=[dr2])
    dr1, dr1_bf, dg1, db1 = _ln_bwd("ln1_bwd", dy1, sv["r1"], sm["ln1_g"])
    (dw_o,) = _mm("dw_o", sv["mixed"], dr1_bf, (F32,), ta=True, tm=1024, tn=1024, tk=512)
    (dmixed,) = _mm("d_mixed", dr1_bf, w["w_o"], (F32,), tb=True, tm=512, tn=1024, tk=D_MODEL)
    do1, do2, do3, dl1, dl2, dl3, do_b, delta_b, dga, dgb = _mix_bwd(
        dmixed, sv["a_out"], sv["b_out"], sv["ls"], sm["a_out_norm"], sm["b_out_norm"])
    dqs, dks, dvs = [], [], []
    for (_, dil), lse, do, dl in zip(PATTERNS, sv["ls"], (do1, do2, do3), (dl1, dl2, dl3)):
        dq, dk, dv = _dil_bwd(sv["qa"], sv["ka"], sv["va"], lse, do, dl, dil)
        dqs.append(dq)
        dks.append(dk)
        dvs.append(dv)
    dqf, dkf, dv_b = _mla_bwd(sv["qf"], sv["kf"], sv["v"], sv["b_lse"], do_b, delta_b)
    dq_b, dkv_b, dkpe = _mla_bwd_prep(dqf, dkf, dv_b, cos, sin)
    (dw_uq,) = _mm("dw_uq", sv["cqn"], dq_b, (F32,), ta=True, tm=Q_LORA, tn=1024, tk=512)
    (dcqn,) = _mm("d_cqn", dq_b, w["w_uq"], (F32,), tb=True, tm=512, tn=Q_LORA, tk=UQ_PAD)
    (dw_ukv,) = _mm("dw_ukv", sv["ckvn"], dkv_b, (F32,), ta=True, tm=KV_LORA, tn=1024, tk=512)
    (dckvn,) = _mm("d_ckvn", dkv_b, w["w_ukv"], (F32,), tb=True, tm=512, tn=KV_LORA, tk=1024)
    dh, dgq, dgkv = _assemble_dh(dqs, dks, dvs, dcqn, sv["cq"], dckvn, sv["ckv"], dkpe, cos, sin,
                                 sm["q_a_norm"], sm["kv_a_norm"])
    (dw_in,) = _mm("dw_in", sv["x_bf"], dh, (F32,), ta=True, tm=1024, tn=1024, tk=512)
    (dx,) = _mm("d_x", dh, w["w_in"], (F32,), tb=True, tm=512, tn=1024, tk=1024,
                epilogue=lambda acc, d: (acc + ALPHA * d,), extras=[dr1])
    dws = dict(w_in=dw_in, w_uq=dw_uq, w_ukv=dw_ukv, w_o=dw_o, w_ff1=dw_ff1, w_ff2=dw_ff2)
    dsm = dict(q_a_norm=dgq, kv_a_norm=dgkv, a_out_norm=dga, b_out_norm=dgb, ln1_g=dg1, ln1_b=db1, ln2_g=dg2, ln2_b=db2)
    return dx, dws, dsm


def _pad_w_in(w):
    return jnp.concatenate([w[:, :1792], w[:, 1920:1984], jnp.zeros((w.shape[0], 64), w.dtype), w[:, 1792:1920]], axis=1)


def _unpad_w_in(w):
    return jnp.concatenate([w[:, :1792], w[:, 1920:2048], w[:, 1792:1856]], axis=1)


def _pad_w_uq(w):
    w4 = w.reshape(Q_LORA, MLA_HEADS, QK_NOPE + QK_ROPE)
    nope = w4[:, :, :QK_NOPE].reshape(Q_LORA, MLA_HEADS * QK_NOPE)
    rope = jnp.pad(w4[:, :, QK_NOPE:], ((0, 0), (0, 0), (0, LANES - QK_ROPE))).reshape(Q_LORA, MLA_HEADS * LANES)
    return jnp.concatenate([nope, rope], axis=1)


def _unpad_w_uq(w):
    nope = w[:, :512].reshape(Q_LORA, MLA_HEADS, QK_NOPE)
    rope = w[:, 512:].reshape(Q_LORA, MLA_HEADS, LANES)[:, :, :QK_ROPE]
    return jnp.concatenate([nope, rope], axis=2).reshape(Q_LORA, MLA_HEADS * (QK_NOPE + QK_ROPE))


def _perm_w_ukv(w):
    w4 = w.reshape(KV_LORA, MLA_HEADS, QK_NOPE + V_DIM)
    return jnp.concatenate([w4[:, :, :QK_NOPE].reshape(KV_LORA, 512), w4[:, :, QK_NOPE:].reshape(KV_LORA, 512)], axis=1)


def _unperm_w_ukv(w):
    k = w[:, :512].reshape(KV_LORA, MLA_HEADS, QK_NOPE)
    v = w[:, 512:].reshape(KV_LORA, MLA_HEADS, V_DIM)
    return jnp.concatenate([k, v], axis=2).reshape(KV_LORA, 1024)


def _rope_tables(n_rows):
    half = A_HEAD_DIM // 2
    inv_freq = ROPE_THETA ** (-jnp.arange(half, dtype=F32) / half)
    ang = jnp.arange(n_rows, dtype=F32)[:, None] * inv_freq[None, :]
    cos = jnp.tile(jnp.cos(ang), (1, 2 * A_HEADS))
    sin = jnp.tile(jnp.concatenate([-jnp.sin(ang), jnp.sin(ang)], axis=1), (1, A_HEADS))
    return cos, sin


COL_SHARDED = ("w_in", "w_uq", "w_ukv", "w_ff1")


def _to_pieces(name, full):
    rows, cols = full.shape
    if name in COL_SHARDED:
        t = full.reshape(2, rows // 2, N_CHIPS, cols // N_CHIPS).transpose(2, 0, 1, 3)
    else:
        t = full.reshape(N_CHIPS, 2, rows // 8, cols)
    return t.reshape(N_CHIPS, 2, -1)


def _from_pieces(name, pieces, rows, cols):
    if name in COL_SHARDED:
        return pieces.reshape(N_CHIPS, 2, rows // 2, cols // N_CHIPS).transpose(1, 2, 0, 3).reshape(rows, cols)
    return pieces.reshape(rows, cols)


FULL_SHAPES = dict(w_in=(1024, 1984), w_uq=(256, 768), w_ukv=(128, 1024), w_o=(1024, 1024), w_ff1=(1024, 4096),
                   w_ff2=(4096, 1024))
W_NAMES = tuple(n for n, _ in PIECE_SHAPES)


def _shard_half(name, shard, half):
    rows = shard.shape[0] // 2
    return lax.dynamic_slice_in_dim(shard, half * rows, rows, axis=0).reshape(-1)


def _shard_from_halves(name, halves):
    out, off = {}, 0
    for nm, (r, c) in PIECE_SHAPES:
        out[nm] = halves[:, off:off + r * c].reshape(2 * r, c)
        off += r * c
    return out


MESH = pl.DeviceIdType.MESH
ANY = pl.BlockSpec(memory_space=pl.ANY)


def _all_gather_pieces(name, block):
    rows, lanes = block.shape

    def body(x_ref, out_ref, send_sems, recv_sems, local_sem):
        x, y, c = lax.axis_index("x"), lax.axis_index("y"), lax.axis_index("c")
        me, sibling = (x, y, c), (x, y, 1 - c)
        chips = [(1 - x, y), (x, 1 - y), (1 - x, 1 - y)]

        def slot(px, py, pc):
            return out_ref.at[4 * px + 2 * py + pc]

        def copy(k, blk, to, src=None):
            return pltpu.make_async_remote_copy(
                src_ref=slot(*blk) if src is None else src, dst_ref=slot(*blk),
                send_sem=send_sems.at[k], recv_sem=recv_sems.at[k], device_id=to, device_id_type=MESH)

        mine = pltpu.make_async_copy(x_ref, slot(*me), local_sem)
        mine.start()
        first = [copy(0, me, sibling, src=x_ref)]
        first += [copy(1 + j, me, (*chip, c), src=x_ref) for j, chip in enumerate(chips)]
        for cp in first:
            cp.start()
        passed = [copy(4 + j, (*chip, c), sibling) for j, chip in enumerate(chips)]
        for j, chip in enumerate(chips):
            copy(1 + j, (*chip, c), me).wait_recv()
            passed[j].start()
        copy(0, sibling, me).wait_recv()
        for j, chip in enumerate(chips):
            copy(4 + j, (*chip, 1 - c), me).wait_recv()
        for cp in first + passed:
            cp.wait_send()
        mine.wait()

    return pl.pallas_call(
        body, name=name, out_shape=_sds((N_DEV, rows, lanes), block.dtype),
        in_specs=[ANY], out_specs=ANY,
        scratch_shapes=[pltpu.SemaphoreType.DMA((7,)), pltpu.SemaphoreType.DMA((7,)), pltpu.SemaphoreType.DMA],
    )(block)


def _swap_with_sibling(name, send):
    def body(s_ref, got_ref, send_sems, recv_sems):
        sibling = (lax.axis_index("x"), lax.axis_index("y"), 1 - lax.axis_index("c"))
        cps = [pltpu.make_async_remote_copy(src_ref=s_ref.at[j], dst_ref=got_ref.at[j], send_sem=send_sems.at[j],
                                            recv_sem=recv_sems.at[j], device_id=sibling, device_id_type=MESH)
               for j in range(N_CHIPS)]
        for cp in cps:
            cp.start()
        for cp in cps:
            cp.wait()

    return pl.pallas_call(
        body, name=name, out_shape=_sds(send.shape, send.dtype), in_specs=[ANY], out_specs=ANY,
        scratch_shapes=[pltpu.SemaphoreType.DMA((N_CHIPS,)), pltpu.SemaphoreType.DMA((N_CHIPS,))],
    )(send)


def _scatter_to_chips(name, h):
    def body(h_ref, got_ref, send_sems, recv_sems, local_sem):
        x, y, c = lax.axis_index("x"), lax.axis_index("y"), lax.axis_index("c")
        mine = 2 * x + y
        chips = [(1 - x, y), (x, 1 - y), (1 - x, 1 - y)]
        own = pltpu.make_async_copy(h_ref.at[mine], got_ref.at[mine], local_sem)
        own.start()
        cps = [pltpu.make_async_remote_copy(src_ref=h_ref.at[2 * px + py], dst_ref=got_ref.at[mine],
                                            send_sem=send_sems.at[t], recv_sem=recv_sems.at[t],
                                            device_id=(px, py, c), device_id_type=MESH)
               for t, (px, py) in enumerate(chips)]
        for cp in cps:
            cp.start()
        for t, (px, py) in enumerate(chips):
            pltpu.make_async_remote_copy(src_ref=h_ref.at[mine], dst_ref=got_ref.at[2 * px + py],
                                         send_sem=send_sems.at[t], recv_sem=recv_sems.at[t],
                                         device_id=(px, py, c), device_id_type=MESH).wait_recv()
        for cp in cps:
            cp.wait_send()
        own.wait()

    return pl.pallas_call(
        body, name=name, out_shape=_sds(h.shape, h.dtype), in_specs=[ANY], out_specs=ANY,
        scratch_shapes=[pltpu.SemaphoreType.DMA((3,)), pltpu.SemaphoreType.DMA((3,)), pltpu.SemaphoreType.DMA],
    )(h)


def _share_with_sibling(name, r):
    def body(r_ref, out_ref, send_sem, recv_sem, local_sem):
        c = lax.axis_index("c")
        sibling = (lax.axis_index("x"), lax.axis_index("y"), 1 - c)
        own = pltpu.make_async_copy(r_ref, out_ref.at[c], local_sem)
        own.start()
        cp = pltpu.make_async_remote_copy(src_ref=r_ref, dst_ref=out_ref.at[c], send_sem=send_sem, recv_sem=recv_sem,
                                          device_id=sibling, device_id_type=MESH)
        cp.start()
        pltpu.make_async_remote_copy(src_ref=r_ref, dst_ref=out_ref.at[1 - c], send_sem=send_sem, recv_sem=recv_sem,
                                     device_id=sibling, device_id_type=MESH).wait_recv()
        cp.wait_send()
        own.wait()

    return pl.pallas_call(
        body, name=name, out_shape=_sds((2,) + r.shape, r.dtype), in_specs=[ANY], out_specs=ANY,
        scratch_shapes=[pltpu.SemaphoreType.DMA, pltpu.SemaphoreType.DMA, pltpu.SemaphoreType.DMA],
    )(r)


def _add_n(name, parts):
    def fn(*vs):
        acc = vs[0]
        for v in vs[1:]:
            acc = acc + v
        return acc

    return _rowwise(name, fn, parts, [_sds(parts[0].shape)], ts=1920)[0]


def _reduce_layer_grads(tag, dws, c):
    pieces = jnp.concatenate([_to_pieces(n, dws[n]) for n in W_NAMES], axis=2)
    keep = lax.dynamic_index_in_dim(pieces, c, axis=1, keepdims=False).reshape(N_CHIPS * PIECE_ROWS, LANES)
    send = lax.dynamic_index_in_dim(pieces, 1 - c, axis=1, keepdims=False).reshape(N_CHIPS, PIECE_ROWS, LANES)
    got = _swap_with_sibling(f"rs_pair_{tag}", send)
    pair = _add_n(f"rs_pair_sum_{tag}", [keep, got.reshape(N_CHIPS * PIECE_ROWS, LANES)])
    parts = _scatter_to_chips(f"rs_chips_{tag}", pair.reshape(N_CHIPS, PIECE_ROWS, LANES))
    mine = _rowwise(f"rs_chips_sum_{tag}", lambda p: p[0] + p[1] + p[2] + p[3], [parts],
                    [_sds((PIECE_ROWS, LANES))], ts=1920)[0]
    both = _share_with_sibling(f"rs_share_{tag}", mine)
    return _shard_from_halves(tag, both.reshape(2, PIECE_ROWS * LANES))


def _gather_layer_weights(tag, shards, c):
    block = jnp.concatenate([_shard_half(n, shards[n], c) for n in W_NAMES]).astype(BF16).reshape(PIECE_ROWS, LANES)
    allp = _all_gather_pieces(f"ag_{tag}", block).reshape(N_CHIPS, 2, PIECE_ROWS * LANES)
    full, off = {}, 0
    for nm, (r, cc) in PIECE_SHAPES:
        full[nm] = _from_pieces(nm, allp[:, :, off:off + r * cc], *FULL_SHAPES[nm])
        off += r * cc
    full["w_in"] = _pad_w_in(full["w_in"])
    full["w_uq"] = _pad_w_uq(full["w_uq"])
    full["w_ukv"] = _perm_w_ukv(full["w_ukv"])
    return full


def _all_reduce_small(vec):
    rows, lanes = vec.shape

    def gather_body(x_ref, out_ref, send_sems, recv_sems):
        x, y, c = lax.axis_index("x"), lax.axis_index("y"), lax.axis_index("c")
        me, sibling = (x, y, c), (x, y, 1 - c)
        chips = [(1 - x, y), (x, 1 - y), (1 - x, 1 - y)]

        def slot(px, py, pc):
            return out_ref.at[4 * px + 2 * py + pc]

        def copy(k, blk, to, src=None):
            return pltpu.make_async_remote_copy(
                src_ref=slot(*blk) if src is None else src, dst_ref=slot(*blk),
                send_sem=send_sems.at[k], recv_sem=recv_sems.at[k], device_id=to, device_id_type=MESH)

        out_ref[4 * x + 2 * y + c] = x_ref[...]
        first = [copy(0, me, sibling, src=x_ref)]
        first += [copy(1 + j, me, (*chip, c), src=x_ref) for j, chip in enumerate(chips)]
        for cp in first:
            cp.start()
        passed = [copy(4 + j, (*chip, c), sibling) for j, chip in enumerate(chips)]
        for j, chip in enumerate(chips):
            copy(1 + j, (*chip, c), me).wait_recv()
            passed[j].start()
        copy(0, sibling, me).wait_recv()
        for j, chip in enumerate(chips):
            copy(4 + j, (*chip, 1 - c), me).wait_recv()
        for cp in first + passed:
            cp.wait_send()

    vmem = pl.BlockSpec(memory_space=pltpu.VMEM)
    allv = pl.pallas_call(
        gather_body, name="small_all_gather", out_shape=_sds((N_DEV, rows, lanes)),
        in_specs=[vmem], out_specs=vmem,
        scratch_shapes=[pltpu.SemaphoreType.DMA((7,)), pltpu.SemaphoreType.DMA((7,))],
    )(vec)

    def sum_body(a_ref, o_ref):
        acc = a_ref[0]
        for d in range(1, N_DEV):
            acc = acc + a_ref[d]
        o_ref[...] = acc

    return pl.pallas_call(sum_body, name="small_sum", out_shape=_sds((rows, lanes)), in_specs=[vmem], out_specs=vmem)(allv)


def _adamw(name, w, g, m, v, ts=512):
    def fn(w_v, g_v, m_v, v_v):
        m_n = ADAM_B1 * m_v + (1.0 - ADAM_B1) * g_v
        v_n = ADAM_B2 * v_v + (1.0 - ADAM_B2) * (g_v * g_v)
        m_hat = m_n / (1.0 - ADAM_B1 ** ADAM_STEP)
        v_hat = v_n / (1.0 - ADAM_B2 ** ADAM_STEP)
        delta = -ADAM_LR * (m_hat / (jnp.sqrt(v_hat) + ADAM_EPS) + ADAM_WD * w_v)
        return delta, m_n, v_n

    return _rowwise(name, fn, [w, g, m, v], [_sds(w.shape)] * 3, ts=ts)


def _pack_small(per_layer):
    flat = jnp.concatenate([per_layer[l][n].reshape(-1) for l in range(DEPTH) for n in SMALL_NAMES])
    return jnp.pad(flat, (0, SMALL_ROWS * LANES - flat.shape[0])).reshape(SMALL_ROWS, LANES)


def _unpack_small(packed):
    flat = packed.reshape(-1)
    per = sum(SMALL_SIZES)
    out = {}
    for n, size, off in zip(SMALL_NAMES, SMALL_SIZES, [sum(SMALL_SIZES[:i]) for i in range(len(SMALL_SIZES))]):
        out[n] = jnp.stack([flat[l * per + off:l * per + off + size] for l in range(DEPTH)])
    return out


def _fwd_bwd(x, target, weights, smalls, on_layer_grads):
    cos, sin = _rope_tables(x.shape[0])
    h_f32, h_bf = x, x.astype(BF16)
    saved = []
    for l in range(len(weights)):
        h_f32, h_bf, sv = _layer_fwd(h_f32, h_bf, weights[l], smalls[l], cos, sin)
        saved.append(sv)
    loss_part, dh = _loss_fn(h_f32, target)
    small_grads = [None] * len(weights)
    for l in reversed(range(len(weights))):
        dh, dws, small_grads[l] = _layer_bwd(dh, weights[l], smalls[l], saved[l], cos, sin)
        dws["w_in"] = _unpad_w_in(dws["w_in"])
        dws["w_uq"] = _unpad_w_uq(dws["w_uq"])
        dws["w_ukv"] = _unperm_w_ukv(dws["w_ukv"])
        on_layer_grads(l, dws)
    return loss_part, dh, small_grads


def kernel(x, w_in, q_a_norm, kv_a_norm, w_uq, w_ukv, a_out_norm, b_out_norm, w_o, ln1_g, ln1_b, w_ff1, w_ff2, ln2_g, ln2_b, loss_target, m_w_in, m_q_a_norm, m_kv_a_norm, m_w_uq, m_w_ukv, m_a_out_norm, m_b_out_norm, m_w_o, m_ln1_g, m_ln1_b, m_w_ff1, m_w_ff2, m_ln2_g, m_ln2_b, v_w_in, v_q_a_norm, v_kv_a_norm, v_w_uq, v_w_ukv, v_a_out_norm, v_b_out_norm, v_w_o, v_ln1_g, v_ln1_b, v_w_ff1, v_w_ff2, v_ln2_g, v_ln2_b):
    c = lax.axis_index("c")
    big = dict(w_in=w_in, w_uq=w_uq, w_ukv=w_ukv, w_o=w_o, w_ff1=w_ff1, w_ff2=w_ff2)
    big_m = dict(w_in=m_w_in, w_uq=m_w_uq, w_ukv=m_w_ukv, w_o=m_w_o, w_ff1=m_w_ff1, w_ff2=m_w_ff2)
    big_v = dict(w_in=v_w_in, w_uq=v_w_uq, w_ukv=v_w_ukv, w_o=v_w_o, w_ff1=v_w_ff1, w_ff2=v_w_ff2)
    small = dict(q_a_norm=q_a_norm, kv_a_norm=kv_a_norm, a_out_norm=a_out_norm, b_out_norm=b_out_norm,
                 ln1_g=ln1_g, ln1_b=ln1_b, ln2_g=ln2_g, ln2_b=ln2_b)
    small_m = dict(q_a_norm=m_q_a_norm, kv_a_norm=m_kv_a_norm, a_out_norm=m_a_out_norm, b_out_norm=m_b_out_norm,
                   ln1_g=m_ln1_g, ln1_b=m_ln1_b, ln2_g=m_ln2_g, ln2_b=m_ln2_b)
    small_v = dict(q_a_norm=v_q_a_norm, kv_a_norm=v_kv_a_norm, a_out_norm=v_a_out_norm, b_out_norm=v_b_out_norm,
                   ln1_g=v_ln1_g, ln1_b=v_ln1_b, ln2_g=v_ln2_g, ln2_b=v_ln2_b)

    weights = [_gather_layer_weights(f"l{l}", {n: big[n][l] for n in W_NAMES}, c) for l in range(DEPTH)]
    smalls = [{n: small[n][l][None, :] for n in SMALL_NAMES} for l in range(DEPTH)]
    shard_grads = [None] * DEPTH

    def reduce_layer(l, dws):
        shard_grads[l] = _reduce_layer_grads(f"l{l}", dws, c)

    loss_part, grad_x, small_grads = _fwd_bwd(x[0], loss_target[0], weights, smalls, reduce_layer)
    loss = lax.psum(loss_part[0, 0], ("x", "y", "c"))
    grad_x = grad_x[None]

    g_small_packed = _all_reduce_small(_pack_small(small_grads))
    pack_in = lambda d: _pack_small([{n: d[n][l] for n in SMALL_NAMES} for l in range(DEPTH)])
    sd, sm_, sv_ = _adamw("adamw_small", pack_in(small), g_small_packed, pack_in(small_m), pack_in(small_v), ts=SMALL_ROWS)
    g_small, d_small, m_small, v_small = (_unpack_small(t) for t in (g_small_packed, sd, sm_, sv_))

    g_big, d_big, m_big, v_big = {}, {}, {}, {}
    for n in W_NAMES:
        g = jnp.stack([shard_grads[l][n] for l in range(DEPTH)])
        shape = g.shape
        flat = lambda t: t.reshape(shape[0] * shape[1], shape[2])
        d_, m_, v_ = _adamw(f"adamw_{n}", flat(big[n]), flat(g), flat(big_m[n]), flat(big_v[n]))
        g_big[n], d_big[n], m_big[n], v_big[n] = g, d_.reshape(shape), m_.reshape(shape), v_.reshape(shape)

    order = ("w_in", "q_a_norm", "kv_a_norm", "w_uq", "w_ukv", "a_out_norm", "b_out_norm", "w_o", "ln1_g", "ln1_b",
             "w_ff1", "w_ff2", "ln2_g", "ln2_b")
    pick = lambda bigd, smalld: [bigd[n] if n in bigd else smalld[n] for n in order]
    return (loss, grad_x, *pick(g_big, g_small), *pick(d_big, d_small), *pick(m_big, m_small), *pick(v_big, v_small))
```

```python
import math

import jax
import jax.numpy as jnp
from jax import lax
from jax.experimental import pallas as pl
from jax.experimental.pallas import tpu as pltpu

F32, BF16 = jnp.float32, jnp.bfloat16

D_MODEL = 1024
DEPTH = 4
A_HEAD_DIM = 64
A_HEADS = 8
A_WIDTH = 512
PATTERNS = ((128, 1), (512, 4), (2048, 16))
SPAN = 128
MLA_HEADS = 4
QK_NOPE = 128
QK_ROPE = 64
V_DIM = 128
Q_LORA = 256
KV_LORA = 128
MLA_WIDTH = 512
QK_PAD = 256
IN_COLS = 1984
IN_PAD = 2048
UQ_PAD = 1024
D_FF = 4096
ROPE_THETA = 10000.0
ALPHA = (2.0 * DEPTH) ** 0.25
LN_EPS = 1e-5
RMS_EPS = 1e-6
MLA_SCALE = (QK_NOPE + QK_ROPE) ** -0.5
A_SCALE = A_HEAD_DIM ** -0.5

ADAM_LR, ADAM_B1, ADAM_B2, ADAM_EPS, ADAM_WD, ADAM_STEP = 0.001, 0.9, 0.999, 1e-08, 0.01, 10

VMEM_LIMIT_BYTES = 56 * 1024 * 1024
NEG = -1e30
N_CHIPS, N_DEV = 4, 8
LANES = 128

PIECE_SHAPES = (
    ("w_in", (512, 496)), ("w_uq", (128, 192)), ("w_ukv", (64, 256)),
    ("w_o", (128, 1024)), ("w_ff1", (512, 1024)), ("w_ff2", (512, 1024)),
)
PIECE_ROWS = sum(r * c for _, (r, c) in PIECE_SHAPES) // LANES
SMALL_NAMES = ("q_a_norm", "kv_a_norm", "a_out_norm", "b_out_norm", "ln1_g", "ln1_b", "ln2_g", "ln2_b")
SMALL_SIZES = (256, 128, 512, 512, 1024, 1024, 1024, 1024)
SMALL_ROWS = 176


def _params(sem):
    return pltpu.CompilerParams(dimension_semantics=sem, vmem_limit_bytes=VMEM_LIMIT_BYTES)


def _rowwise(name, fn, ins, outs, consts=(), reds=(), ts=512):
    n_rows = ins[0].shape[-2]
    ts = min(ts, n_rows)
    assert n_rows % ts == 0
    n_in, n_c, n_o, n_r = len(ins), len(consts), len(outs), len(reds)

    def tile_spec(shape):
        if len(shape) == 2:
            return pl.BlockSpec((ts, shape[1]), lambda i: (i, 0))
        return pl.BlockSpec((shape[0], ts, shape[2]), lambda i: (0, i, 0))

    def whole_spec(shape):
        return pl.BlockSpec(shape, lambda i: (0,) * len(shape))

    def body(*refs):
        in_refs, c_refs = refs[:n_in], refs[n_in:n_in + n_c]
        o_refs, r_refs = refs[n_in + n_c:n_in + n_c + n_o], refs[n_in + n_c + n_o:]
        res = fn(*[r[...] for r in in_refs], *[r[...] for r in c_refs])
        res = tuple(res) if isinstance(res, (tuple, list)) else (res,)
        for r, v in zip(o_refs, res[:n_o]):
            if len(r.shape) == 3:
                for g in range(r.shape[0]):
                    r[g] = v[g].astype(r.dtype)
            else:
                r[...] = v.astype(r.dtype)
        if n_r:
            i = pl.program_id(0)

            @pl.when(i == 0)
            def _():
                for r, v in zip(r_refs, res[n_o:]):
                    r[...] = v

            @pl.when(i > 0)
            def _():
                for r, v in zip(r_refs, res[n_o:]):
                    r[...] += v

    out = pl.pallas_call(
        body, name=name, grid=(n_rows // ts,),
        in_specs=[tile_spec(a.shape) for a in ins] + [whole_spec(c.shape) for c in consts],
        out_specs=[tile_spec(o.shape) for o in outs] + [whole_spec(r.shape) for r in reds],
        out_shape=list(outs) + list(reds),
        compiler_params=_params(("arbitrary",)),
    )(*ins, *consts)
    return out


def _sds(shape, dtype=F32):
    return jax.ShapeDtypeStruct(tuple(shape), dtype)


def _mm(name, a, b, out_dtypes, *, ta=False, tb=False, tm=512, tn=512, tk=512, epilogue=None, extras=(), vecs=()):
    (k_dim, m_dim) = a.shape if ta else a.shape[::-1]
    (n_dim, k2) = b.shape if tb else b.shape[::-1]
    assert k_dim == k2
    tm, tn, tk = min(tm, m_dim), min(tn, n_dim), min(tk, k_dim)
    assert m_dim % tm == 0 and n_dim % tn == 0 and k_dim % tk == 0
    nk = k_dim // tk
    a_spec = pl.BlockSpec((tk, tm), lambda i, j, k: (k, i)) if ta else pl.BlockSpec((tm, tk), lambda i, j, k: (i, k))
    b_spec = pl.BlockSpec((tn, tk), lambda i, j, k: (j, k)) if tb else pl.BlockSpec((tk, tn), lambda i, j, k: (k, j))
    dims = (((0 if ta else 1,), (1 if tb else 0,)), ((), ()))
    n_e, n_v, n_o = len(extras), len(vecs), len(out_dtypes)

    def body(a_ref, b_ref, *refs):
        e_refs, v_refs = refs[:n_e], refs[n_e:n_e + n_v]
        o_refs = refs[n_e + n_v:n_e + n_v + n_o]
        part = lax.dot_general(a_ref[...].astype(BF16), b_ref[...].astype(BF16), dims, preferred_element_type=F32)

        def finish(acc):
            outs = epilogue(acc, *[r[...] for r in e_refs], *[r[...] for r in v_refs]) if epilogue else (acc,)
            for r, v in zip(o_refs, outs):
                r[...] = v.astype(r.dtype)

        if nk == 1:
            finish(part)
        else:
            acc_ref = refs[-1]
            k = pl.program_id(2)

            @pl.when(k == 0)
            def _():
                acc_ref[...] = part

            @pl.when(k > 0)
            def _():
                acc_ref[...] += part

            @pl.when(k == nk - 1)
            def _():
                finish(acc_ref[...])

    tile = pl.BlockSpec((tm, tn), lambda i, j, k: (i, j))
    out = pl.pallas_call(
        body, name=name, grid=(m_dim // tm, n_dim // tn, nk),
        in_specs=[a_spec, b_spec] + [tile] * n_e + [pl.BlockSpec((1, tn), lambda i, j, k: (0, j))] * n_v,
        out_specs=[tile] * n_o,
        out_shape=[_sds((m_dim, n_dim), d) for d in out_dtypes],
        scratch_shapes=[pltpu.VMEM((tm, tn), F32)] if nk > 1 else [],
        compiler_params=_params(("parallel", "parallel", "arbitrary")),
    )(a, b, *extras, *vecs)
    return out


def _swap_halves(x):
    width = x.shape[1]
    lane = lax.broadcasted_iota(jnp.int32, x.shape, 1)
    return jnp.where((lane % 64) < 32, pltpu.roll(x, width - 32, 1), pltpu.roll(x, 32, 1))


def _rope(x, cos, sin_signed):
    return x * cos + _swap_halves(x) * sin_signed


def _rope_t(d, cos, sin_signed):
    return d * cos - _swap_halves(d) * sin_signed


def _rms(x, g):
    rstd = lax.rsqrt(jnp.mean(x * x, axis=-1, keepdims=True) + RMS_EPS)
    return x * rstd * g


def _rms_bwd(x, g, dy):
    rstd = lax.rsqrt(jnp.mean(x * x, axis=-1, keepdims=True) + RMS_EPS)
    xh = x * rstd
    dyg = dy * g
    dx = rstd * (dyg - xh * jnp.mean(dyg * xh, axis=-1, keepdims=True))
    return dx, jnp.sum(dy * xh, axis=0, keepdims=True)


def _layer_norm(r, g, b):
    mu = jnp.mean(r, axis=-1, keepdims=True)
    xc = r - mu
    var = jnp.mean(xc * xc, axis=-1, keepdims=True)
    return xc * lax.rsqrt(var + LN_EPS) * g + b


def _dot(a, b, dims):
    return lax.dot_general(a, b, (dims, ((), ())), preferred_element_type=F32)


_NN, _NT, _TN = ((1,), (0,)), ((1,), (1,)), ((0,), (0,))


def _in_proj(x_bf, w_in_p, cos, sin):
    n_rows = x_bf.shape[0]
    ts = 512

    def body(x_ref, w_ref, cos_ref, sin_ref, qa_ref, ka_ref, va_ref, cq_ref, kpe_ref, ckv_ref):
        xb = x_ref[...]
        cos_v, sin_v = cos_ref[...], sin_ref[...]

        def seg(lo, hi):
            return _dot(xb, w_ref[:, lo:hi], _NN)

        qa_ref[...] = (_rope(seg(0, 512), cos_v, sin_v) * A_SCALE).astype(BF16)
        ka_ref[...] = _rope(seg(512, 1024), cos_v, sin_v).astype(BF16)
        va_ref[...] = seg(1024, 1536).astype(BF16)
        cq_ref[...] = seg(1536, 1792)
        kpe_ref[...] = _rope(seg(1792, 1920), cos_v[:, :LANES], sin_v[:, :LANES]).astype(BF16)
        ckv_ref[...] = seg(1920, 2048)

    row = lambda c: pl.BlockSpec((ts, c), lambda i: (i, 0))
    return pl.pallas_call(
        body, name="in_proj", grid=(n_rows // ts,),
        in_specs=[row(D_MODEL), pl.BlockSpec((D_MODEL, IN_PAD), lambda i: (0, 0)), row(A_WIDTH), row(A_WIDTH)],
        out_specs=[row(A_WIDTH), row(A_WIDTH), row(A_WIDTH), row(Q_LORA), row(LANES), row(KV_LORA)],
        out_shape=[_sds((n_rows, A_WIDTH), BF16), _sds((n_rows, A_WIDTH), BF16), _sds((n_rows, A_WIDTH), BF16),
                   _sds((n_rows, Q_LORA)), _sds((n_rows, LANES), BF16), _sds((n_rows, KV_LORA))],
        compiler_params=_params(("parallel",)),
    )(x_bf, w_in_p, cos, sin)


def _band_mask(m, heads):
    qi = lax.broadcasted_iota(jnp.int32, (heads * SPAN, 2 * SPAN), 0) % SPAN
    kj = lax.broadcasted_iota(jnp.int32, (heads * SPAN, 2 * SPAN), 1)
    return ((kj < SPAN) & (kj >= qi) & (m > 0)) | ((kj >= SPAN) & ((kj - SPAN) <= qi))


def _even_lanes(rows):
    return lax.broadcasted_iota(jnp.int32, (rows, LANES), 1) < A_HEAD_DIM


def _dil_view(t, dil):
    rows, width = t.shape
    return t.reshape(rows // dil, dil * width)


def _dil_fwd(q, k, v, dil):
    n_rows = q.shape[0]
    nb = n_rows // dil // SPAN
    qv, kv_, vv = _dil_view(q, dil), _dil_view(k, dil), _dil_view(v, dil)

    def body(q_ref, kp_ref, kc_ref, vp_ref, vc_ref, o_ref, lse_ref):
        mask = _band_mask(pl.program_id(1), 1)
        even, even2 = _even_lanes(SPAN), _even_lanes(2 * SPAN)
        for p in range(A_HEADS // 2):
            sl = slice(p * LANES, (p + 1) * LANES)
            q2 = q_ref[:, sl]
            kcat = jnp.concatenate([kp_ref[:, sl], kc_ref[:, sl]], axis=0)
            vcat = jnp.concatenate([vp_ref[:, sl], vc_ref[:, sl]], axis=0)
            zero, one = jnp.zeros_like(q2), jnp.ones_like(vcat)
            res, lses = [], []
            for first in (True, False):
                qh = jnp.where(even, q2, zero) if first else jnp.where(even, zero, q2)
                vh = jnp.where(even2, vcat, one) if first else jnp.where(even2, one, vcat)
                s = jnp.where(mask, _dot(qh, kcat, _NT), NEG)
                mx = jnp.max(s, axis=-1, keepdims=True)
                r = _dot(jnp.exp(s - mx).astype(BF16), vh, _NN)
                den = pltpu.roll(r, A_HEAD_DIM, 1)
                res.append(r / den)
                lses.append(mx + jnp.log(den))
            o_ref[:, sl] = jnp.where(even, res[0], res[1])
            lse_ref[:, sl] = jnp.where(even, lses[0], lses[1])

    cur = pl.BlockSpec((SPAN, A_WIDTH), lambda r, m: (m, r))
    prev = pl.BlockSpec((SPAN, A_WIDTH), lambda r, m: (jnp.maximum(m - 1, 0), r))
    o, lse = pl.pallas_call(
        body, name=f"dil_fwd_d{dil}", grid=(dil, nb),
        in_specs=[cur, prev, cur, prev, cur],
        out_specs=[cur, cur],
        out_shape=[_sds(qv.shape), _sds(qv.shape)],
        compiler_params=_params(("parallel", "arbitrary")),
    )(qv, kv_, kv_, vv, vv)
    return o.reshape(n_rows, A_WIDTH), lse.reshape(n_rows, A_WIDTH)


def _mix_fwd(os_, ls_, b_out, a_g, b_g):
    def fn(o1, o2, o3, l1, l2, l3, b, ag, bg):
        mx = jnp.maximum(jnp.maximum(l1, l2), l3)
        e1, e2, e3 = jnp.exp(l1 - mx), jnp.exp(l2 - mx), jnp.exp(l3 - mx)
        a = (e1 * o1 + e2 * o2 + e3 * o3) / (e1 + e2 + e3)
        return a, jnp.concatenate([_rms(a, ag), _rms(b, bg)], axis=1)

    n_rows = b_out.shape[0]
    return _rowwise("mix_fwd", fn, [*os_, *ls_, b_out], [_sds((n_rows, A_WIDTH)), _sds((n_rows, 2 * A_WIDTH), BF16)],
                    consts=[a_g, b_g])


def _mla_q_prep(cq, g, w_uq_p, cos, sin):
    def fn(cq_v, cos_v, sin_v, g_v, w_v):
        cqn = _rms(cq_v, g_v).astype(BF16)
        q = _dot(cqn, w_v, _NN)
        qf = [jnp.concatenate([q[:, h * QK_NOPE:(h + 1) * QK_NOPE],
                               _rope(q[:, 512 + h * LANES:512 + (h + 1) * LANES], cos_v[:, :LANES], sin_v[:, :LANES])], axis=1)
              for h in range(MLA_HEADS)]
        return cqn, qf

    n_rows = cq.shape[0]
    return _rowwise("mla_q_prep", fn, [cq, cos, sin], [_sds((n_rows, Q_LORA), BF16), _sds((MLA_HEADS, n_rows, QK_PAD), BF16)],
                    consts=[g, w_uq_p])


def _mla_kv_prep(ckv, kpe, g, w_ukv_p):
    def fn(ckv_v, kpe_v, g_v, w_v):
        ckvn = _rms(ckv_v, g_v).astype(BF16)
        kvv = _dot(ckvn, w_v, _NN)
        kf = [jnp.concatenate([kvv[:, h * QK_NOPE:(h + 1) * QK_NOPE], kpe_v.astype(F32)], axis=1) for h in range(MLA_HEADS)]
        return ckvn, kf, kvv[:, 512:]

    n_rows = ckv.shape[0]
    return _rowwise("mla_kv_prep", fn, [ckv, kpe],
                    [_sds((n_rows, KV_LORA), BF16), _sds((MLA_HEADS, n_rows, QK_PAD), BF16), _sds((n_rows, MLA_WIDTH), BF16)],
                    consts=[g, w_ukv_p])


MLA_FWD_TILES = (1024, 2048)
MLA_BWD_TILES = (1024, 1024)


def _mla_fwd(qf, kf, v):
    n_rows = v.shape[0]
    tq, tk = min(MLA_FWD_TILES[0], n_rows), min(MLA_FWD_TILES[1], n_rows)
    nq, nk = n_rows // tq, n_rows // tk
    exp2_scale = MLA_SCALE * math.log2(math.e)

    def last_k(i):
        return jnp.right_shift(i * tq + tq - 1, int(math.log2(tk)))

    def body(q_ref, k_ref, v_ref, o_ref, lse_ref, m_sc, l_sc, acc_sc):
        i, j = pl.program_id(1), pl.program_id(2)

        @pl.when(j == 0)
        def _():
            m_sc[...] = jnp.full(m_sc.shape, NEG, F32)
            l_sc[...] = jnp.zeros(l_sc.shape, F32)
            acc_sc[...] = jnp.zeros(acc_sc.shape, F32)

        def step(masked):
            s = _dot(q_ref[...], k_ref[...], _NT)
            if masked:
                row = lax.broadcasted_iota(jnp.int32, (tq, tk), 0) + i * tq
                col = lax.broadcasted_iota(jnp.int32, (tq, tk), 1) + j * tk
                s = jnp.where(col <= row, s, NEG)
            m_prev = m_sc[...]
            m_new = jnp.maximum(m_prev, jnp.max(s, axis=-1, keepdims=True))
            alpha = jnp.exp2((m_prev - m_new) * exp2_scale)
            p = jnp.exp2((s - m_new) * exp2_scale)
            l_sc[...] = alpha * l_sc[...] + jnp.sum(p, axis=-1, keepdims=True)
            acc_sc[...] = alpha * acc_sc[...] + _dot(p.astype(BF16), v_ref[...], _NN)
            m_sc[...] = m_new

        active = j * tk <= i * tq + tq - 1
        crosses = (j + 1) * tk - 1 > i * tq

        @pl.when(active & jnp.logical_not(crosses))
        def _():
            step(False)

        @pl.when(active & crosses)
        def _():
            step(True)

        @pl.when(j == last_k(i))
        def _():
            o_ref[...] = acc_sc[...] / l_sc[...]
            lse_ref[...] = jnp.broadcast_to(m_sc[...] * MLA_SCALE + jnp.log(l_sc[...]), (tq, V_DIM))

    return pl.pallas_call(
        body, name="mla_fwd", grid=(MLA_HEADS, nq, nk),
        in_specs=[pl.BlockSpec((None, tq, QK_PAD), lambda h, i, j: (h, i, 0)),
                  pl.BlockSpec((None, tk, QK_PAD), lambda h, i, j: (h, jnp.minimum(j, last_k(i)), 0)),
                  pl.BlockSpec((tk, V_DIM), lambda h, i, j: (jnp.minimum(j, last_k(i)), h))],
        out_specs=[pl.BlockSpec((tq, V_DIM), lambda h, i, j: (i, h)), pl.BlockSpec((tq, V_DIM), lambda h, i, j: (i, h))],
        out_shape=[_sds((n_rows, MLA_WIDTH)), _sds((n_rows, MLA_WIDTH))],
        scratch_shapes=[pltpu.VMEM((tq, 1), F32), pltpu.VMEM((tq, 1), F32), pltpu.VMEM((tq, V_DIM), F32)],
        compiler_params=_params(("parallel", "parallel", "arbitrary")),
    )(qf, kf, v)


def _mm_res_ln(name, a, w, xres, g, b):
    def epi(acc, xr, g_v, b_v):
        r = ALPHA * xr + acc
        y = _layer_norm(r, g_v, b_v)
        return y, y, r

    return _mm(name, a, w, (F32, BF16, F32), tm=1024, tn=D_MODEL, tk=1024, epilogue=epi, extras=[xres], vecs=[g, b])


def _mm_relu2(x_bf, w):
    def epi(acc):
        r = jnp.maximum(acc, 0.0)
        return (r * r,)

    return _mm("ff1", x_bf, w, (BF16,), tm=1024, tn=2048, tk=D_MODEL, epilogue=epi)[0]


def _loss_fn(y, t):
    def fn(y_v, t_v):
        d = y_v - t_v
        part = jnp.sum(jnp.sum(d * d, axis=1, keepdims=True), axis=0, keepdims=True)
        return d * (1.0 / D_MODEL), part

    dy, part = _rowwise("loss", fn, [y, t], [_sds(y.shape)], reds=[_sds((1, 1))])
    return part * (0.5 / D_MODEL), dy


def _ln_bwd(name, dy, r, g):
    def fn(dy_v, r_v, g_v):
        mu = jnp.mean(r_v, axis=-1, keepdims=True)
        xc = r_v - mu
        rstd = lax.rsqrt(jnp.mean(xc * xc, axis=-1, keepdims=True) + LN_EPS)
        xh = xc * rstd
        dxh = dy_v * g_v
        dr = rstd * (dxh - jnp.mean(dxh, axis=-1, keepdims=True) - xh * jnp.mean(dxh * xh, axis=-1, keepdims=True))
        return dr, dr, jnp.sum(dy_v * xh, axis=0, keepdims=True), jnp.sum(dy_v, axis=0, keepdims=True)

    return _rowwise(name, fn, [dy, r], [_sds(dy.shape), _sds(dy.shape, BF16)], consts=[g],
                    reds=[_sds((1, D_MODEL)), _sds((1, D_MODEL))])


def _head_sum_matrix():
    i = lax.broadcasted_iota(jnp.int32, (A_WIDTH, A_WIDTH), 0) // A_HEAD_DIM
    j = lax.broadcasted_iota(jnp.int32, (A_WIDTH, A_WIDTH), 1) // A_HEAD_DIM
    return (i == j).astype(BF16)


def _mix_bwd(dmixed, a_out, b_out, ls_, a_g, b_g):
    def fn(dm, a, b, l1, l2, l3, ag, bg, e_mat):
        da, dga = _rms_bwd(a, ag, dm[:, :A_WIDTH])
        db, dgb = _rms_bwd(b, bg, dm[:, A_WIDTH:])
        t = da * a
        t_hi = t.astype(BF16)
        t_lo = (t - t_hi.astype(F32)).astype(BF16)
        tsum = _dot(t_hi, e_mat, _NN) + _dot(t_lo, e_mat, _NN)
        mx = jnp.maximum(jnp.maximum(l1, l2), l3)
        e1, e2, e3 = jnp.exp(l1 - mx), jnp.exp(l2 - mx), jnp.exp(l3 - mx)
        inv = 1.0 / (e1 + e2 + e3)
        w1, w2, w3 = e1 * inv, e2 * inv, e3 * inv
        tb = db * b
        delta_b = jnp.concatenate(
            [jnp.broadcast_to(jnp.sum(tb[:, h * V_DIM:(h + 1) * V_DIM], axis=-1, keepdims=True), (tb.shape[0], V_DIM))
             for h in range(MLA_HEADS)], axis=1)
        return w1 * da, w2 * da, w3 * da, w1 * tsum, w2 * tsum, w3 * tsum, db, delta_b, dga, dgb

    n_rows = a_out.shape[0]
    wide = (n_rows, A_WIDTH)
    return _rowwise("mix_bwd", fn, [dmixed, a_out, b_out, *ls_],
                    [_sds(wide, BF16)] * 3 + [_sds(wide)] * 3 + [_sds(wide, BF16), _sds(wide)],
                    consts=[a_g, b_g, _head_sum_matrix()], reds=[_sds((1, A_WIDTH)), _sds((1, MLA_WIDTH))], ts=256)


def _dil_bwd(q, k, v, lse, do, dl, dil):
    n_rows = q.shape[0]
    nb = n_rows // dil // SPAN
    qv, kv_, vv = _dil_view(q, dil), _dil_view(k, dil), _dil_view(v, dil)
    lv, dov, dlv = _dil_view(lse, dil), _dil_view(do, dil), _dil_view(dl, dil)

    def body(q_ref, kp_ref, kc_ref, vp_ref, vc_ref, lse_ref, do_ref, dl_ref, dq_ref, dk_ref, dv_ref, ck, cv):
        m = pl.program_id(1)

        @pl.when(m == 0)
        def _():
            ck[...] = jnp.zeros(ck.shape, F32)
            cv[...] = jnp.zeros(cv.shape, F32)

        @pl.when(m < nb)
        def _():
            mask, even = _band_mask(m, 2), _even_lanes(SPAN)
            for p in range(A_HEADS // 2):
                sl = slice(p * LANES, (p + 1) * LANES)
                q2, do2 = q_ref[:, sl], do_ref[:, sl]
                zero = jnp.zeros_like(q2)
                qcat = jnp.concatenate([jnp.where(even, q2, zero), jnp.where(even, zero, q2)], axis=0)
                docat = jnp.concatenate([jnp.where(even, do2, zero), jnp.where(even, zero, do2)], axis=0)
                kcat = jnp.concatenate([kp_ref[:, sl], kc_ref[:, sl]], axis=0)
                vcat = jnp.concatenate([vp_ref[:, sl], vc_ref[:, sl]], axis=0)
                lse2, dl2 = lse_ref[:, sl], dl_ref[:, sl]
                lse_c = jnp.concatenate([lse2[:, :1], lse2[:, A_HEAD_DIM:A_HEAD_DIM + 1]], axis=0)
                dl_c = jnp.concatenate([dl2[:, :1], dl2[:, A_HEAD_DIM:A_HEAD_DIM + 1]], axis=0)
                pr = jnp.exp(jnp.where(mask, _dot(qcat, kcat, _NT), NEG) - lse_c)
                ds = (pr * (_dot(docat, vcat, _NT) - dl_c)).astype(BF16)
                dq = _dot(ds, kcat, _NN)
                dq_ref[:, sl] = jnp.where(even, dq[:SPAN], dq[SPAN:])
                dk2 = _dot(ds, qcat, _TN)
                dv2 = _dot(pr.astype(BF16), docat, _TN)
                dk_ref[:, sl] = ck[:, sl] + dk2[:SPAN]
                dv_ref[:, sl] = cv[:, sl] + dv2[:SPAN]
                ck[:, sl] = dk2[SPAN:]
                cv[:, sl] = dv2[SPAN:]

        @pl.when(m == nb)
        def _():
            dk_ref[...] = ck[...]
            dv_ref[...] = cv[...]

    last = nb - 1
    cur = pl.BlockSpec((SPAN, A_WIDTH), lambda r, m: (jnp.minimum(m, last), r))
    prev = pl.BlockSpec((SPAN, A_WIDTH), lambda r, m: (jnp.clip(m - 1, 0, last), r))
    dq, dk, dv = pl.pallas_call(
        body, name=f"dil_bwd_d{dil}", grid=(dil, nb + 1),
        in_specs=[cur, prev, cur, prev, cur, cur, cur, cur],
        out_specs=[cur, prev, prev],
        out_shape=[_sds(qv.shape)] * 3,
        scratch_shapes=[pltpu.VMEM((SPAN, A_WIDTH), F32), pltpu.VMEM((SPAN, A_WIDTH), F32)],
        compiler_params=_params(("parallel", "arbitrary")),
    )(qv, kv_, kv_, vv, vv, lv, dov, dlv)
    return dq.reshape(n_rows, A_WIDTH), dk.reshape(n_rows, A_WIDTH), dv.reshape(n_rows, A_WIDTH)


def _mla_bwd(qf, kf, v, lse, do, delta):
    n_rows = v.shape[0]
    tq, tk = min(MLA_BWD_TILES[0], n_rows), min(MLA_BWD_TILES[1], n_rows)
    nq, nk = n_rows // tq, n_rows // tk

    def first_q(j):
        return jnp.right_shift(j * tk, int(math.log2(tq)))

    def body(q_ref, k_ref, v_ref, lse_ref, do_ref, dl_ref, dq_ref, dk_ref, dv_ref, dk_sc, dv_sc):
        j, i = pl.program_id(1), pl.program_id(2)

        @pl.when((j == 0) & (i == 0))
        def _():
            dq_ref[...] = jnp.zeros(dq_ref.shape, F32)

        @pl.when(i == first_q(j))
        def _():
            dk_sc[...] = jnp.zeros(dk_sc.shape, F32)
            dv_sc[...] = jnp.zeros(dv_sc.shape, F32)

        def step(masked):
            q, k, dob = q_ref[...], k_ref[...], do_ref[...]
            s = _dot(q, k, _NT) * MLA_SCALE
            if masked:
                row = lax.broadcasted_iota(jnp.int32, (tq, tk), 0) + i * tq
                col = lax.broadcasted_iota(jnp.int32, (tq, tk), 1) + j * tk
                s = jnp.where(col <= row, s, NEG)
            p = jnp.exp(s - lse_ref[:, :1])
            dp = _dot(dob, v_ref[...], _NT)
            ds = (p * (dp - dl_ref[:, :1]) * MLA_SCALE).astype(BF16)
            dv_sc[...] += _dot(p.astype(BF16), dob, _TN)
            dk_sc[...] += _dot(ds, q, _TN)
            rows = pl.ds(pl.multiple_of(i * tq, tq), tq)
            dq_ref[rows, :] += _dot(ds, k, _NN)

        active = i >= first_q(j)
        crosses = (j + 1) * tk - 1 > i * tq

        @pl.when(active & jnp.logical_not(crosses))
        def _():
            step(False)

        @pl.when(active & crosses)
        def _():
            step(True)

        @pl.when(i == nq - 1)
        def _():
            dk_ref[...] = dk_sc[...]
            dv_ref[...] = dv_sc[...]

    qrow = lambda h, j, i: (jnp.maximum(i, first_q(j)), h)
    return pl.pallas_call(
        body, name="mla_bwd", grid=(MLA_HEADS, nk, nq),
        in_specs=[pl.BlockSpec((None, tq, QK_PAD), lambda h, j, i: (h, jnp.maximum(i, first_q(j)), 0)),
                  pl.BlockSpec((None, tk, QK_PAD), lambda h, j, i: (h, j, 0)),
                  pl.BlockSpec((tk, V_DIM), lambda h, j, i: (j, h)),
                  pl.BlockSpec((tq, V_DIM), qrow), pl.BlockSpec((tq, V_DIM), qrow), pl.BlockSpec((tq, V_DIM), qrow)],
        out_specs=[pl.BlockSpec((None, n_rows, QK_PAD), lambda h, j, i: (h, 0, 0)),
                   pl.BlockSpec((None, tk, QK_PAD), lambda h, j, i: (h, j, 0)),
                   pl.BlockSpec((tk, V_DIM), lambda h, j, i: (j, h))],
        out_shape=[_sds((MLA_HEADS, n_rows, QK_PAD)), _sds((MLA_HEADS, n_rows, QK_PAD)), _sds((n_rows, MLA_WIDTH))],
        scratch_shapes=[pltpu.VMEM((tk, QK_PAD), F32), pltpu.VMEM((tk, V_DIM), F32)],
        compiler_params=_params(("parallel", "arbitrary", "arbitrary")),
    )(qf, kf, v, lse, do, delta)


def _mla_bwd_prep(dqf, dkf, dv, cos, sin):
    def fn(dqf_v, dkf_v, dv_v, cos_v, sin_v):
        dq = jnp.concatenate([dqf_v[h][:, :QK_NOPE] for h in range(MLA_HEADS)]
                             + [_rope_t(dqf_v[h][:, QK_NOPE:], cos_v[:, :LANES], sin_v[:, :LANES]) for h in range(MLA_HEADS)], axis=1)
        dkv = jnp.concatenate([dkf_v[h][:, :QK_NOPE] for h in range(MLA_HEADS)] + [dv_v], axis=1)
        dkpe = dkf_v[0][:, QK_NOPE:] + dkf_v[1][:, QK_NOPE:] + dkf_v[2][:, QK_NOPE:] + dkf_v[3][:, QK_NOPE:]
        return dq, dkv, dkpe

    n_rows = dv.shape[0]
    return _rowwise("mla_bwd_prep", fn, [dqf, dkf, dv, cos, sin],
                    [_sds((n_rows, UQ_PAD), BF16), _sds((n_rows, 1024), BF16), _sds((n_rows, LANES))], ts=256)


def _assemble_dh(dqs, dks, dvs, dcqn, cq, dckvn, ckv, dkpe, cos, sin, gq, gkv):
    def fn(q1, q2, q3, k1, k2, k3, v1, v2, v3, dcqn_v, cq_v, dckvn_v, ckv_v, dkpe_v, cos_v, sin_v, gq_v, gkv_v):
        dqa = _rope_t(q1 + q2 + q3, cos_v, sin_v) * A_SCALE
        dka = _rope_t(k1 + k2 + k3, cos_v, sin_v)
        dcq, dgq = _rms_bwd(cq_v, gq_v, dcqn_v)
        dckv, dgkv = _rms_bwd(ckv_v, gkv_v, dckvn_v)
        dkr = _rope_t(dkpe_v, cos_v[:, :LANES], sin_v[:, :LANES])
        return jnp.concatenate([dqa, dka, v1 + v2 + v3, dcq, dkr, dckv], axis=1), dgq, dgkv

    n_rows = cq.shape[0]
    return _rowwise("assemble_dh", fn, [*dqs, *dks, *dvs, dcqn, cq, dckvn, ckv, dkpe, cos, sin],
                    [_sds((n_rows, IN_PAD), BF16)], consts=[gq, gkv],
                    reds=[_sds((1, Q_LORA)), _sds((1, KV_LORA))], ts=256)


def _layer_fwd(x_f32, x_bf, w, sm, cos, sin):
    qa, ka, va, cq, kpe, ckv = _in_proj(x_bf, w["w_in"], cos, sin)
    os_, ls_ = [], []
    for _, dil in PATTERNS:
        o, lse = _dil_fwd(qa, ka, va, dil)
        os_.append(o)
        ls_.append(lse)
    cqn, qf = _mla_q_prep(cq, sm["q_a_norm"], w["w_uq"], cos, sin)
    ckvn, kf, v = _mla_kv_prep(ckv, kpe, sm["kv_a_norm"], w["w_ukv"])
    b_out, b_lse = _mla_fwd(qf, kf, v)
    a_out, mixed = _mix_fwd(os_, ls_, b_out, sm["a_out_norm"], sm["b_out_norm"])
    x1, x1_bf, r1 = _mm_res_ln("wo_ln1", mixed, w["w_o"], x_f32, sm["ln1_g"], sm["ln1_b"])
    f = _mm_relu2(x1_bf, w["w_ff1"])
    x2, x2_bf, r2 = _mm_res_ln("ff2_ln2", f, w["w_ff2"], x1, sm["ln2_g"], sm["ln2_b"])
    saved = dict(x_bf=x_bf, qa=qa, ka=ka, va=va, cq=cq, ckv=ckv, ls=ls_, cqn=cqn, qf=qf, ckvn=ckvn, kf=kf, v=v,
                 b_out=b_out, b_lse=b_lse, a_out=a_out, mixed=mixed, x1_bf=x1_bf, r1=r1, f=f, r2=r2)
    return x2, x2_bf, saved


def _layer_bwd(dx2, w, sm, sv, cos, sin):
    dr2, dr2_bf, dg2, db2 = _ln_bwd("ln2_bwd", dx2, sv["r2"], sm["ln2_g"])
    (dw_ff2,) = _mm("dw_ff2", sv["f"], dr2_bf, (F32,), ta=True, tm=1024, tn=1024, tk=2048)
    (du,) = _mm("d_u", dr2_bf, w["w_ff2"], (BF16,), tb=True, tm=1024, tn=2048, tk=D_MODEL,
                epilogue=lambda acc, f: (acc * (2.0 * jnp.sqrt(f.astype(F32))),), extras=[sv["f"]])
    (dw_ff1,) = _mm("dw_ff1", sv["x1_bf"], du, (F32,), ta=True, tm=1024, tn=1024, tk=2048)
    (dy1,) = _mm("d_x1", du, w["w_ff1"], (F32,), tb=True, tm=1024, tn=1024, tk=2048,
                 epilogue=lambda acc, d: (acc + ALPHA * d,), extras=[dr2])
    dr1, dr1_bf, dg1, db1 = _ln_bwd("ln1_bwd", dy1, sv["r1"], sm["ln1_g"])
    (dw_o,) = _mm("dw_o", sv["mixed"], dr1_bf, (F32,), ta=True, tm=1024, tn=1024, tk=2048)
    (dmixed,) = _mm("d_mixed", dr1_bf, w["w_o"], (F32,), tb=True, tm=1024, tn=1024, tk=D_MODEL)
    do1, do2, do3, dl1, dl2, dl3, do_b, delta_b, dga, dgb = _mix_bwd(
        dmixed, sv["a_out"], sv["b_out"], sv["ls"], sm["a_out_norm"], sm["b_out_norm"])
    dqs, dks, dvs = [], [], []
    for (_, dil), lse, do, dl in zip(PATTERNS, sv["ls"], (do1, do2, do3), (dl1, dl2, dl3)):
        dq, dk, dv = _dil_bwd(sv["qa"], sv["ka"], sv["va"], lse, do, dl, dil)
        dqs.append(dq)
        dks.append(dk)
        dvs.append(dv)
    dqf, dkf, dv_b = _mla_bwd(sv["qf"], sv["kf"], sv["v"], sv["b_lse"], do_b, delta_b)
    dq_b, dkv_b, dkpe = _mla_bwd_prep(dqf, dkf, dv_b, cos, sin)
    (dw_uq,) = _mm("dw_uq", sv["cqn"], dq_b, (F32,), ta=True, tm=Q_LORA, tn=1024, tk=2048)
    (dcqn,) = _mm("d_cqn", dq_b, w["w_uq"], (F32,), tb=True, tm=1024, tn=Q_LORA, tk=UQ_PAD)
    (dw_ukv,) = _mm("dw_ukv", sv["ckvn"], dkv_b, (F32,), ta=True, tm=KV_LORA, tn=1024, tk=2048)
    (dckvn,) = _mm("d_ckvn", dkv_b, w["w_ukv"], (F32,), tb=True, tm=1024, tn=KV_LORA, tk=1024)
    dh, dgq, dgkv = _assemble_dh(dqs, dks, dvs, dcqn, sv["cq"], dckvn, sv["ckv"], dkpe, cos, sin,
                                 sm["q_a_norm"], sm["kv_a_norm"])
    (dw_in,) = _mm("dw_in", sv["x_bf"], dh, (F32,), ta=True, tm=1024, tn=1024, tk=2048)
    (dx,) = _mm("d_x", dh, w["w_in"], (F32,), tb=True, tm=1024, tn=1024, tk=2048,
                epilogue=lambda acc, d: (acc + ALPHA * d,), extras=[dr1])
    dws = dict(w_in=dw_in, w_uq=dw_uq, w_ukv=dw_ukv, w_o=dw_o, w_ff1=dw_ff1, w_ff2=dw_ff2)
    dsm = dict(q_a_norm=dgq, kv_a_norm=dgkv, a_out_norm=dga, b_out_norm=dgb, ln1_g=dg1, ln1_b=db1, ln2_g=dg2, ln2_b=db2)
    return dx, dws, dsm


def _pad_w_in(w):
    return jnp.concatenate([w[:, :1792], w[:, 1920:1984], jnp.zeros((w.shape[0], 64), w.dtype), w[:, 1792:1920]], axis=1)


def _unpad_w_in(w):
    return jnp.concatenate([w[:, :1792], w[:, 1920:2048], w[:, 1792:1856]], axis=1)


def _pad_w_uq(w):
    w4 = w.reshape(Q_LORA, MLA_HEADS, QK_NOPE + QK_ROPE)
    nope = w4[:, :, :QK_NOPE].reshape(Q_LORA, MLA_HEADS * QK_NOPE)
    rope = jnp.pad(w4[:, :, QK_NOPE:], ((0, 0), (0, 0), (0, LANES - QK_ROPE))).reshape(Q_LORA, MLA_HEADS * LANES)
    return jnp.concatenate([nope, rope], axis=1)


def _unpad_w_uq(w):
    nope = w[:, :512].reshape(Q_LORA, MLA_HEADS, QK_NOPE)
    rope = w[:, 512:].reshape(Q_LORA, MLA_HEADS, LANES)[:, :, :QK_ROPE]
    return jnp.concatenate([nope, rope], axis=2).reshape(Q_LORA, MLA_HEADS * (QK_NOPE + QK_ROPE))


def _perm_w_ukv(w):
    w4 = w.reshape(KV_LORA, MLA_HEADS, QK_NOPE + V_DIM)
    return jnp.concatenate([w4[:, :, :QK_NOPE].reshape(KV_LORA, 512), w4[:, :, QK_NOPE:].reshape(KV_LORA, 512)], axis=1)


def _unperm_w_ukv(w):
    k = w[:, :512].reshape(KV_LORA, MLA_HEADS, QK_NOPE)
    v = w[:, 512:].reshape(KV_LORA, MLA_HEADS, V_DIM)
    return jnp.concatenate([k, v], axis=2).reshape(KV_LORA, 1024)


def _rope_tables(n_rows):
    half = A_HEAD_DIM // 2
    inv_freq = ROPE_THETA ** (-jnp.arange(half, dtype=F32) / half)
    ang = jnp.arange(n_rows, dtype=F32)[:, None] * inv_freq[None, :]
    cos = jnp.tile(jnp.cos(ang), (1, 2 * A_HEADS))
    sin = jnp.tile(jnp.concatenate([-jnp.sin(ang), jnp.sin(ang)], axis=1), (1, A_HEADS))
    return cos, sin


COL_SHARDED = ("w_in", "w_uq", "w_ukv", "w_ff1")


def _to_pieces(name, full):
    rows, cols = full.shape
    if name in COL_SHARDED:
        t = full.reshape(2, rows // 2, N_CHIPS, cols // N_CHIPS).transpose(2, 0, 1, 3)
    else:
        t = full.reshape(N_CHIPS, 2, rows // 8, cols)
    return t.reshape(N_CHIPS, 2, -1)


def _from_pieces(name, pieces, rows, cols):
    if name in COL_SHARDED:
        return pieces.reshape(N_CHIPS, 2, rows // 2, cols // N_CHIPS).transpose(1, 2, 0, 3).reshape(rows, cols)
    return pieces.reshape(rows, cols)


FULL_SHAPES = dict(w_in=(1024, 1984), w_uq=(256, 768), w_ukv=(128, 1024), w_o=(1024, 1024), w_ff1=(1024, 4096),
                   w_ff2=(4096, 1024))
W_NAMES = tuple(n for n, _ in PIECE_SHAPES)


def _shard_half(name, shard, half):
    rows = shard.shape[0] // 2
    return lax.dynamic_slice_in_dim(shard, half * rows, rows, axis=0).reshape(-1)


def _shard_from_halves(name, halves):
    out, off = {}, 0
    for nm, (r, c) in PIECE_SHAPES:
        out[nm] = halves[:, off:off + r * c].reshape(2 * r, c)
        off += r * c
    return out


MESH = pl.DeviceIdType.MESH
ANY = pl.BlockSpec(memory_space=pl.ANY)


def _all_gather_pieces(name, block):
    rows, lanes = block.shape

    def body(x_ref, out_ref, send_sems, recv_sems, local_sem):
        x, y, c = lax.axis_index("x"), lax.axis_index("y"), lax.axis_index("c")
        me, sibling = (x, y, c), (x, y, 1 - c)
        chips = [(1 - x, y), (x, 1 - y), (1 - x, 1 - y)]

        def slot(px, py, pc):
            return out_ref.at[4 * px + 2 * py + pc]

        def copy(k, blk, to, src=None):
            return pltpu.make_async_remote_copy(
                src_ref=slot(*blk) if src is None else src, dst_ref=slot(*blk),
                send_sem=send_sems.at[k], recv_sem=recv_sems.at[k], device_id=to, device_id_type=MESH)

        mine = pltpu.make_async_copy(x_ref, slot(*me), local_sem)
        mine.start()
        first = [copy(0, me, sibling, src=x_ref)]
        first += [copy(1 + j, me, (*chip, c), src=x_ref) for j, chip in enumerate(chips)]
        for cp in first:
            cp.start()
        passed = [copy(4 + j, (*chip, c), sibling) for j, chip in enumerate(chips)]
        for j, chip in enumerate(chips):
            copy(1 + j, (*chip, c), me).wait_recv()
            passed[j].start()
        copy(0, sibling, me).wait_recv()
        for j, chip in enumerate(chips):
            copy(4 + j, (*chip, 1 - c), me).wait_recv()
        for cp in first + passed:
            cp.wait_send()
        mine.wait()

    return pl.pallas_call(
        body, name=name, out_shape=_sds((N_DEV, rows, lanes), block.dtype),
        in_specs=[ANY], out_specs=ANY,
        scratch_shapes=[pltpu.SemaphoreType.DMA((7,)), pltpu.SemaphoreType.DMA((7,)), pltpu.SemaphoreType.DMA],
    )(block)


def _swap_with_sibling(name, send):
    def body(s_ref, got_ref, send_sems, recv_sems):
        sibling = (lax.axis_index("x"), lax.axis_index("y"), 1 - lax.axis_index("c"))
        cps = [pltpu.make_async_remote_copy(src_ref=s_ref.at[j], dst_ref=got_ref.at[j], send_sem=send_sems.at[j],
                                            recv_sem=recv_sems.at[j], device_id=sibling, device_id_type=MESH)
               for j in range(N_CHIPS)]
        for cp in cps:
            cp.start()
        for cp in cps:
            cp.wait()

    return pl.pallas_call(
        body, name=name, out_shape=_sds(send.shape, send.dtype), in_specs=[ANY], out_specs=ANY,
        scratch_shapes=[pltpu.SemaphoreType.DMA((N_CHIPS,)), pltpu.SemaphoreType.DMA((N_CHIPS,))],
    )(send)


def _scatter_to_chips(name, h):
    def body(h_ref, got_ref, send_sems, recv_sems, local_sem):
        x, y, c = lax.axis_index("x"), lax.axis_index("y"), lax.axis_index("c")
        mine = 2 * x + y
        chips = [(1 - x, y), (x, 1 - y), (1 - x, 1 - y)]
        own = pltpu.make_async_copy(h_ref.at[mine], got_ref.at[mine], local_sem)
        own.start()
        cps = [pltpu.make_async_remote_copy(src_ref=h_ref.at[2 * px + py], dst_ref=got_ref.at[mine],
                                            send_sem=send_sems.at[t], recv_sem=recv_sems.at[t],
                                            device_id=(px, py, c), device_id_type=MESH)
               for t, (px, py) in enumerate(chips)]
        for cp in cps:
            cp.start()
        for t, (px, py) in enumerate(chips):
            pltpu.make_async_remote_copy(src_ref=h_ref.at[mine], dst_ref=got_ref.at[2 * px + py],
                                         send_sem=send_sems.at[t], recv_sem=recv_sems.at[t],
                                         device_id=(px, py, c), device_id_type=MESH).wait_recv()
        for cp in cps:
            cp.wait_send()
        own.wait()

    return pl.pallas_call(
        body, name=name, out_shape=_sds(h.shape, h.dtype), in_specs=[ANY], out_specs=ANY,
        scratch_shapes=[pltpu.SemaphoreType.DMA((3,)), pltpu.SemaphoreType.DMA((3,)), pltpu.SemaphoreType.DMA],
    )(h)


def _share_with_sibling(name, r):
    def body(r_ref, got_ref, send_sem, recv_sem):
        sibling = (lax.axis_index("x"), lax.axis_index("y"), 1 - lax.axis_index("c"))
        cp = pltpu.make_async_remote_copy(src_ref=r_ref, dst_ref=got_ref, send_sem=send_sem, recv_sem=recv_sem,
                                          device_id=sibling, device_id_type=MESH)
        cp.start()
        cp.wait()

    return pl.pallas_call(
        body, name=name, out_shape=_sds(r.shape, r.dtype), in_specs=[ANY], out_specs=ANY,
        scratch_shapes=[pltpu.SemaphoreType.DMA, pltpu.SemaphoreType.DMA],
    )(r)


def _reduce_layer_grads(tag, dws, c, chip):
    pieces = jnp.concatenate([_to_pieces(n, dws[n]) for n in W_NAMES], axis=2)
    keep = lax.dynamic_index_in_dim(pieces, c, axis=1, keepdims=False).reshape(N_CHIPS * PIECE_ROWS, LANES)
    send = lax.dynamic_index_in_dim(pieces, 1 - c, axis=1, keepdims=False).reshape(N_CHIPS, PIECE_ROWS, LANES)
    got = _swap_with_sibling(f"rs_pair_{tag}", send)
    pair, pair_bf = _rowwise(f"rs_pair_sum_{tag}", lambda a, b: (a + b, a + b),
                             [keep, got.reshape(N_CHIPS * PIECE_ROWS, LANES)],
                             [_sds(keep.shape), _sds(keep.shape, BF16)], ts=1920)
    parts = _scatter_to_chips(f"rs_chips_{tag}", pair_bf.reshape(N_CHIPS, PIECE_ROWS, LANES))
    own = lax.dynamic_index_in_dim(pair.reshape(N_CHIPS, PIECE_ROWS, LANES), chip, axis=0, keepdims=False)
    others = jnp.broadcast_to((jnp.arange(N_CHIPS) != chip).astype(F32)[:, None, None], (N_CHIPS, 1, LANES))
    mine = _rowwise(f"rs_chips_sum_{tag}",
                    lambda o, p, w: o + w[0] * p[0] + w[1] * p[1] + w[2] * p[2] + w[3] * p[3],
                    [own, parts], [_sds((PIECE_ROWS, LANES))], consts=[others], ts=1920)[0]
    theirs = _share_with_sibling(f"rs_share_{tag}", mine)
    both = jnp.where(c == 0, jnp.stack([mine, theirs]), jnp.stack([theirs, mine]))
    return _shard_from_halves(tag, both.reshape(2, PIECE_ROWS * LANES))


def _gather_layer_weights(tag, shards, c):
    block = jnp.concatenate([_shard_half(n, shards[n], c) for n in W_NAMES]).astype(BF16).reshape(PIECE_ROWS, LANES)
    allp = _all_gather_pieces(f"ag_{tag}", block).reshape(N_CHIPS, 2, PIECE_ROWS * LANES)
    full, off = {}, 0
    for nm, (r, cc) in PIECE_SHAPES:
        full[nm] = _from_pieces(nm, allp[:, :, off:off + r * cc], *FULL_SHAPES[nm])
        off += r * cc
    full["w_in"] = _pad_w_in(full["w_in"])
    full["w_uq"] = _pad_w_uq(full["w_uq"])
    full["w_ukv"] = _perm_w_ukv(full["w_ukv"])
    return full


def _all_reduce_small(vec):
    rows, lanes = vec.shape

    def gather_body(x_ref, out_ref, send_sems, recv_sems):
        x, y, c = lax.axis_index("x"), lax.axis_index("y"), lax.axis_index("c")
        me, sibling = (x, y, c), (x, y, 1 - c)
        chips = [(1 - x, y), (x, 1 - y), (1 - x, 1 - y)]

        def slot(px, py, pc):
            return out_ref.at[4 * px + 2 * py + pc]

        def copy(k, blk, to, src=None):
            return pltpu.make_async_remote_copy(
                src_ref=slot(*blk) if src is None else src, dst_ref=slot(*blk),
                send_sem=send_sems.at[k], recv_sem=recv_sems.at[k], device_id=to, device_id_type=MESH)

        out_ref[4 * x + 2 * y + c] = x_ref[...]
        first = [copy(0, me, sibling, src=x_ref)]
        first += [copy(1 + j, me, (*chip, c), src=x_ref) for j, chip in enumerate(chips)]
        for cp in first:
            cp.start()
        passed = [copy(4 + j, (*chip, c), sibling) for j, chip in enumerate(chips)]
        for j, chip in enumerate(chips):
            copy(1 + j, (*chip, c), me).wait_recv()
            passed[j].start()
        copy(0, sibling, me).wait_recv()
        for j, chip in enumerate(chips):
            copy(4 + j, (*chip, 1 - c), me).wait_recv()
        for cp in first + passed:
            cp.wait_send()

    vmem = pl.BlockSpec(memory_space=pltpu.VMEM)
    allv = pl.pallas_call(
        gather_body, name="small_all_gather", out_shape=_sds((N_DEV, rows, lanes)),
        in_specs=[vmem], out_specs=vmem,
        scratch_shapes=[pltpu.SemaphoreType.DMA((7,)), pltpu.SemaphoreType.DMA((7,))],
    )(vec)

    def sum_body(a_ref, o_ref):
        acc = a_ref[0]
        for d in range(1, N_DEV):
            acc = acc + a_ref[d]
        o_ref[...] = acc

    return pl.pallas_call(sum_body, name="small_sum", out_shape=_sds((rows, lanes)), in_specs=[vmem], out_specs=vmem)(allv)


def _adamw(name, w, g, m, v, ts=512):
    def fn(w_v, g_v, m_v, v_v):
        m_n = ADAM_B1 * m_v + (1.0 - ADAM_B1) * g_v
        v_n = ADAM_B2 * v_v + (1.0 - ADAM_B2) * (g_v * g_v)
        m_hat = m_n / (1.0 - ADAM_B1 ** ADAM_STEP)
        v_hat = v_n / (1.0 - ADAM_B2 ** ADAM_STEP)
        delta = -ADAM_LR * (m_hat / (jnp.sqrt(v_hat) + ADAM_EPS) + ADAM_WD * w_v)
        return delta, m_n, v_n

    return _rowwise(name, fn, [w, g, m, v], [_sds(w.shape)] * 3, ts=ts)


def _pack_small(per_layer):
    flat = jnp.concatenate([per_layer[l][n].reshape(-1) for l in range(DEPTH) for n in SMALL_NAMES])
    return jnp.pad(flat, (0, SMALL_ROWS * LANES - flat.shape[0])).reshape(SMALL_ROWS, LANES)


def _unpack_small(packed):
    flat = packed.reshape(-1)
    per = sum(SMALL_SIZES)
    out = {}
    for n, size, off in zip(SMALL_NAMES, SMALL_SIZES, [sum(SMALL_SIZES[:i]) for i in range(len(SMALL_SIZES))]):
        out[n] = jnp.stack([flat[l * per + off:l * per + off + size] for l in range(DEPTH)])
    return out


def _fwd_bwd(x, target, weights, smalls, on_layer_grads):
    cos, sin = _rope_tables(x.shape[0])
    h_f32, h_bf = x, x.astype(BF16)
    saved = []
    for l in range(len(weights)):
        h_f32, h_bf, sv = _layer_fwd(h_f32, h_bf, weights[l], smalls[l], cos, sin)
        saved.append(sv)
    loss_part, dh = _loss_fn(h_f32, target)
    small_grads = [None] * len(weights)
    for l in reversed(range(len(weights))):
        dh, dws, small_grads[l] = _layer_bwd(dh, weights[l], smalls[l], saved[l], cos, sin)
        dws["w_in"] = _unpad_w_in(dws["w_in"])
        dws["w_uq"] = _unpad_w_uq(dws["w_uq"])
        dws["w_ukv"] = _unperm_w_ukv(dws["w_ukv"])
        on_layer_grads(l, dws)
    return loss_part, dh, small_grads


def kernel(x, w_in, q_a_norm, kv_a_norm, w_uq, w_ukv, a_out_norm, b_out_norm, w_o, ln1_g, ln1_b, w_ff1, w_ff2, ln2_g, ln2_b, loss_target, m_w_in, m_q_a_norm, m_kv_a_norm, m_w_uq, m_w_ukv, m_a_out_norm, m_b_out_norm, m_w_o, m_ln1_g, m_ln1_b, m_w_ff1, m_w_ff2, m_ln2_g, m_ln2_b, v_w_in, v_q_a_norm, v_kv_a_norm, v_w_uq, v_w_ukv, v_a_out_norm, v_b_out_norm, v_w_o, v_ln1_g, v_ln1_b, v_w_ff1, v_w_ff2, v_ln2_g, v_ln2_b):
    c = lax.axis_index("c")
    chip = 2 * lax.axis_index("x") + lax.axis_index("y")
    big = dict(w_in=w_in, w_uq=w_uq, w_ukv=w_ukv, w_o=w_o, w_ff1=w_ff1, w_ff2=w_ff2)
    big_m = dict(w_in=m_w_in, w_uq=m_w_uq, w_ukv=m_w_ukv, w_o=m_w_o, w_ff1=m_w_ff1, w_ff2=m_w_ff2)
    big_v = dict(w_in=v_w_in, w_uq=v_w_uq, w_ukv=v_w_ukv, w_o=v_w_o, w_ff1=v_w_ff1, w_ff2=v_w_ff2)
    small = dict(q_a_norm=q_a_norm, kv_a_norm=kv_a_norm, a_out_norm=a_out_norm, b_out_norm=b_out_norm,
                 ln1_g=ln1_g, ln1_b=ln1_b, ln2_g=ln2_g, ln2_b=ln2_b)
    small_m = dict(q_a_norm=m_q_a_norm, kv_a_norm=m_kv_a_norm, a_out_norm=m_a_out_norm, b_out_norm=m_b_out_norm,
                   ln1_g=m_ln1_g, ln1_b=m_ln1_b, ln2_g=m_ln2_g, ln2_b=m_ln2_b)
    small_v = dict(q_a_norm=v_q_a_norm, kv_a_norm=v_kv_a_norm, a_out_norm=v_a_out_norm, b_out_norm=v_b_out_norm,
                   ln1_g=v_ln1_g, ln1_b=v_ln1_b, ln2_g=v_ln2_g, ln2_b=v_ln2_b)

    weights = [_gather_layer_weights(f"l{l}", {n: big[n][l] for n in W_NAMES}, c) for l in range(DEPTH)]
    smalls = [{n: small[n][l][None, :] for n in SMALL_NAMES} for l in range(DEPTH)]
    shard_grads = [None] * DEPTH

    def reduce_layer(l, dws):
        shard_grads[l] = _reduce_layer_grads(f"l{l}", dws, c, chip)

    loss_part, grad_x, small_grads = _fwd_bwd(x[0], loss_target[0], weights, smalls, reduce_layer)
    loss = lax.psum(loss_part[0, 0], ("x", "y", "c"))
    grad_x = grad_x[None]

    g_small_packed = _all_reduce_small(_pack_small(small_grads))
    pack_in = lambda d: _pack_small([{n: d[n][l] for n in SMALL_NAMES} for l in range(DEPTH)])
    sd, sm_, sv_ = _adamw("adamw_small", pack_in(small), g_small_packed, pack_in(small_m), pack_in(small_v), ts=SMALL_ROWS)
    g_small, d_small, m_small, v_small = (_unpack_small(t) for t in (g_small_packed, sd, sm_, sv_))

    g_big, d_big, m_big, v_big = {}, {}, {}, {}
    for n in W_NAMES:
        g = jnp.stack([shard_grads[l][n] for l in range(DEPTH)])
        shape = g.shape
        flat = lambda t: t.reshape(shape[0] * shape[1], shape[2])
        d_, m_, v_ = _adamw(f"adamw_{n}", flat(big[n]), flat(g), flat(big_m[n]), flat(big_v[n]))
        g_big[n], d_big[n], m_big[n], v_big[n] = g, d_.reshape(shape), m_.reshape(shape), v_.reshape(shape)

    order = ("w_in", "q_a_norm", "kv_a_norm", "w_uq", "w_ukv", "a_out_norm", "b_out_norm", "w_o", "ln1_g", "ln1_b",
             "w_ff1", "w_ff2", "ln2_g", "ln2_b")
    pick = lambda bigd, smalld: [bigd[n] if n in bigd else smalld[n] for n in order]
    return (loss, grad_x, *pick(g_big, g_small), *pick(d_big, d_small), *pick(m_big, m_small), *pick(v_big, v_small))
```

```python
import math

import jax
import jax.numpy as jnp
from jax import lax
from jax.experimental import pallas as pl
from jax.experimental.pallas import tpu as pltpu

F32, BF16 = jnp.float32, jnp.bfloat16

D_MODEL = 1024
DEPTH = 4
A_HEAD_DIM = 64
A_HEADS = 8
A_WIDTH = 512
PATTERNS = ((128, 1), (512, 4), (2048, 16))
SPAN = 128
MLA_HEADS = 4
QK_NOPE = 128
QK_ROPE = 64
V_DIM = 128
Q_LORA = 256
KV_LORA = 128
MLA_WIDTH = 512
QK_PAD = 256
IN_COLS = 1984
IN_PAD = 2048
UQ_PAD = 1024
D_FF = 4096
ROPE_THETA = 10000.0
ALPHA = (2.0 * DEPTH) ** 0.25
LN_EPS = 1e-5
RMS_EPS = 1e-6
MLA_SCALE = (QK_NOPE + QK_ROPE) ** -0.5
A_SCALE = A_HEAD_DIM ** -0.5

ADAM_LR, ADAM_B1, ADAM_B2, ADAM_EPS, ADAM_WD, ADAM_STEP = 0.001, 0.9, 0.999, 1e-08, 0.01, 10

VMEM_LIMIT_BYTES = 56 * 1024 * 1024
NEG = -1e30
N_CHIPS, N_DEV = 4, 8
LANES = 128

SMALL_NAMES = ("q_a_norm", "kv_a_norm", "a_out_norm", "b_out_norm", "ln1_g", "ln1_b", "ln2_g", "ln2_b")
SMALL_SIZES = (256, 128, 512, 512, 1024, 1024, 1024, 1024)
SMALL_ROWS = 176


def _params(sem):
    return pltpu.CompilerParams(dimension_semantics=sem, vmem_limit_bytes=VMEM_LIMIT_BYTES)


def _rowwise(name, fn, ins, outs, consts=(), reds=(), ts=512):
    n_rows = ins[0].shape[-2]
    ts = min(ts, n_rows)
    assert n_rows % ts == 0
    n_in, n_c, n_o, n_r = len(ins), len(consts), len(outs), len(reds)

    def tile_spec(shape):
        if len(shape) == 2:
            return pl.BlockSpec((ts, shape[1]), lambda i: (i, 0))
        return pl.BlockSpec((shape[0], ts, shape[2]), lambda i: (0, i, 0))

    def whole_spec(shape):
        return pl.BlockSpec(shape, lambda i: (0,) * len(shape))

    def body(*refs):
        in_refs, c_refs = refs[:n_in], refs[n_in:n_in + n_c]
        o_refs, r_refs = refs[n_in + n_c:n_in + n_c + n_o], refs[n_in + n_c + n_o:]
        res = fn(*[r[...] for r in in_refs], *[r[...] for r in c_refs])
        res = tuple(res) if isinstance(res, (tuple, list)) else (res,)
        for r, v in zip(o_refs, res[:n_o]):
            if len(r.shape) == 3:
                for g in range(r.shape[0]):
                    r[g] = v[g].astype(r.dtype)
            else:
                r[...] = v.astype(r.dtype)
        if n_r:
            i = pl.program_id(0)

            @pl.when(i == 0)
            def _():
                for r, v in zip(r_refs, res[n_o:]):
                    r[...] = v

            @pl.when(i > 0)
            def _():
                for r, v in zip(r_refs, res[n_o:]):
                    r[...] += v

    out = pl.pallas_call(
        body, name=name, grid=(n_rows // ts,),
        in_specs=[tile_spec(a.shape) for a in ins] + [whole_spec(c.shape) for c in consts],
        out_specs=[tile_spec(o.shape) for o in outs] + [whole_spec(r.shape) for r in reds],
        out_shape=list(outs) + list(reds),
        compiler_params=_params(("arbitrary",)),
    )(*ins, *consts)
    return out


def _sds(shape, dtype=F32):
    return jax.ShapeDtypeStruct(tuple(shape), dtype)


def _mm(name, a, b, out_dtypes, *, ta=False, tb=False, tm=512, tn=512, tk=512, epilogue=None, extras=(), vecs=()):
    (k_dim, m_dim) = a.shape if ta else a.shape[::-1]
    (n_dim, k2) = b.shape if tb else b.shape[::-1]
    assert k_dim == k2
    tm, tn, tk = min(tm, m_dim), min(tn, n_dim), min(tk, k_dim)
    assert m_dim % tm == 0 and n_dim % tn == 0 and k_dim % tk == 0
    nk = k_dim // tk
    a_spec = pl.BlockSpec((tk, tm), lambda i, j, k: (k, i)) if ta else pl.BlockSpec((tm, tk), lambda i, j, k: (i, k))
    b_spec = pl.BlockSpec((tn, tk), lambda i, j, k: (j, k)) if tb else pl.BlockSpec((tk, tn), lambda i, j, k: (k, j))
    dims = (((0 if ta else 1,), (1 if tb else 0,)), ((), ()))
    n_e, n_v, n_o = len(extras), len(vecs), len(out_dtypes)

    def body(a_ref, b_ref, *refs):
        e_refs, v_refs = refs[:n_e], refs[n_e:n_e + n_v]
        o_refs = refs[n_e + n_v:n_e + n_v + n_o]
        part = lax.dot_general(a_ref[...].astype(BF16), b_ref[...].astype(BF16), dims, preferred_element_type=F32)

        def finish(acc):
            outs = epilogue(acc, *[r[...] for r in e_refs], *[r[...] for r in v_refs]) if epilogue else (acc,)
            for r, v in zip(o_refs, outs):
                r[...] = v.astype(r.dtype)

        if nk == 1:
            finish(part)
        else:
            acc_ref = refs[-1]
            k = pl.program_id(2)

            @pl.when(k == 0)
            def _():
                acc_ref[...] = part

            @pl.when(k > 0)
            def _():
                acc_ref[...] += part

            @pl.when(k == nk - 1)
            def _():
                finish(acc_ref[...])

    tile = pl.BlockSpec((tm, tn), lambda i, j, k: (i, j))
    out = pl.pallas_call(
        body, name=name, grid=(m_dim // tm, n_dim // tn, nk),
        in_specs=[a_spec, b_spec] + [tile] * n_e + [pl.BlockSpec((1, tn), lambda i, j, k: (0, j))] * n_v,
        out_specs=[tile] * n_o,
        out_shape=[_sds((m_dim, n_dim), d) for d in out_dtypes],
        scratch_shapes=[pltpu.VMEM((tm, tn), F32)] if nk > 1 else [],
        compiler_params=_params(("parallel", "parallel", "arbitrary")),
    )(a, b, *extras, *vecs)
    return out


def _swap_halves(x):
    width = x.shape[1]
    lane = lax.broadcasted_iota(jnp.int32, x.shape, 1)
    return jnp.where((lane % 64) < 32, pltpu.roll(x, width - 32, 1), pltpu.roll(x, 32, 1))


def _rope(x, cos, sin_signed):
    return x * cos + _swap_halves(x) * sin_signed


def _rope_t(d, cos, sin_signed):
    return d * cos - _swap_halves(d) * sin_signed


def _rms(x, g):
    rstd = lax.rsqrt(jnp.mean(x * x, axis=-1, keepdims=True) + RMS_EPS)
    return x * rstd * g


def _rms_bwd(x, g, dy):
    rstd = lax.rsqrt(jnp.mean(x * x, axis=-1, keepdims=True) + RMS_EPS)
    xh = x * rstd
    dyg = dy * g
    dx = rstd * (dyg - xh * jnp.mean(dyg * xh, axis=-1, keepdims=True))
    return dx, jnp.sum(dy * xh, axis=0, keepdims=True)


def _layer_norm(r, g, b):
    mu = jnp.mean(r, axis=-1, keepdims=True)
    xc = r - mu
    var = jnp.mean(xc * xc, axis=-1, keepdims=True)
    return xc * lax.rsqrt(var + LN_EPS) * g + b


def _dot(a, b, dims):
    return lax.dot_general(a, b, (dims, ((), ())), preferred_element_type=F32)


_NN, _NT, _TN = ((1,), (0,)), ((1,), (1,)), ((0,), (0,))


def _in_proj(x_bf, w_in_p, cos, sin):
    n_rows = x_bf.shape[0]
    ts = 512

    def body(x_ref, w_ref, cos_ref, sin_ref, qa_ref, ka_ref, va_ref, cq_ref, kpe_ref, ckv_ref):
        xb = x_ref[...]
        cos_v, sin_v = cos_ref[...], sin_ref[...]

        def seg(lo, hi):
            return _dot(xb, w_ref[:, lo:hi], _NN)

        qa_ref[...] = (_rope(seg(0, 512), cos_v, sin_v) * A_SCALE).astype(BF16)
        ka_ref[...] = _rope(seg(512, 1024), cos_v, sin_v).astype(BF16)
        va_ref[...] = seg(1024, 1536).astype(BF16)
        cq_ref[...] = seg(1536, 1792)
        kpe_ref[...] = _rope(seg(1792, 1920), cos_v[:, :LANES], sin_v[:, :LANES]).astype(BF16)
        ckv_ref[...] = seg(1920, 2048)

    row = lambda c: pl.BlockSpec((ts, c), lambda i: (i, 0))
    return pl.pallas_call(
        body, name="in_proj", grid=(n_rows // ts,),
        in_specs=[row(D_MODEL), pl.BlockSpec((D_MODEL, IN_PAD), lambda i: (0, 0)), row(A_WIDTH), row(A_WIDTH)],
        out_specs=[row(A_WIDTH), row(A_WIDTH), row(A_WIDTH), row(Q_LORA), row(LANES), row(KV_LORA)],
        out_shape=[_sds((n_rows, A_WIDTH), BF16), _sds((n_rows, A_WIDTH), BF16), _sds((n_rows, A_WIDTH), BF16),
                   _sds((n_rows, Q_LORA)), _sds((n_rows, LANES), BF16), _sds((n_rows, KV_LORA))],
        compiler_params=_params(("parallel",)),
    )(x_bf, w_in_p, cos, sin)


def _band_mask(m, heads):
    qi = lax.broadcasted_iota(jnp.int32, (heads * SPAN, 2 * SPAN), 0) % SPAN
    kj = lax.broadcasted_iota(jnp.int32, (heads * SPAN, 2 * SPAN), 1)
    return ((kj < SPAN) & (kj >= qi) & (m > 0)) | ((kj >= SPAN) & ((kj - SPAN) <= qi))


def _even_lanes(rows):
    return lax.broadcasted_iota(jnp.int32, (rows, LANES), 1) < A_HEAD_DIM


def _dil_view(t, dil):
    rows, width = t.shape
    return t.reshape(rows // dil, dil * width)


def _dil_fwd(q, k, v, dil):
    n_rows = q.shape[0]
    nb = n_rows // dil // SPAN
    qv, kv_, vv = _dil_view(q, dil), _dil_view(k, dil), _dil_view(v, dil)

    def body(q_ref, kp_ref, kc_ref, vp_ref, vc_ref, o_ref, lse_ref):
        mask = _band_mask(pl.program_id(1), 1)
        even, even2 = _even_lanes(SPAN), _even_lanes(2 * SPAN)
        for p in range(A_HEADS // 2):
            sl = slice(p * LANES, (p + 1) * LANES)
            q2 = q_ref[:, sl]
            kcat = jnp.concatenate([kp_ref[:, sl], kc_ref[:, sl]], axis=0)
            vcat = jnp.concatenate([vp_ref[:, sl], vc_ref[:, sl]], axis=0)
            zero, one = jnp.zeros_like(q2), jnp.ones_like(vcat)
            res, lses = [], []
            for first in (True, False):
                qh = jnp.where(even, q2, zero) if first else jnp.where(even, zero, q2)
                vh = jnp.where(even2, vcat, one) if first else jnp.where(even2, one, vcat)
                s = jnp.where(mask, _dot(qh, kcat, _NT), NEG)
                mx = jnp.max(s, axis=-1, keepdims=True)
                r = _dot(jnp.exp(s - mx).astype(BF16), vh, _NN)
                den = pltpu.roll(r, A_HEAD_DIM, 1)
                res.append(r / den)
                lses.append(mx + jnp.log(den))
            o_ref[:, sl] = jnp.where(even, res[0], res[1])
            lse_ref[:, sl] = jnp.where(even, lses[0], lses[1])

    cur = pl.BlockSpec((SPAN, A_WIDTH), lambda r, m: (m, r))
    prev = pl.BlockSpec((SPAN, A_WIDTH), lambda r, m: (jnp.maximum(m - 1, 0), r))
    o, lse = pl.pallas_call(
        body, name=f"dil_fwd_d{dil}", grid=(dil, nb),
        in_specs=[cur, prev, cur, prev, cur],
        out_specs=[cur, cur],
        out_shape=[_sds(qv.shape), _sds(qv.shape)],
        compiler_params=_params(("parallel", "arbitrary")),
    )(qv, kv_, kv_, vv, vv)
    return o.reshape(n_rows, A_WIDTH), lse.reshape(n_rows, A_WIDTH)


def _mix_fwd(os_, ls_, b_out, a_g, b_g):
    def fn(o1, o2, o3, l1, l2, l3, b, ag, bg):
        mx = jnp.maximum(jnp.maximum(l1, l2), l3)
        e1, e2, e3 = jnp.exp(l1 - mx), jnp.exp(l2 - mx), jnp.exp(l3 - mx)
        a = (e1 * o1 + e2 * o2 + e3 * o3) / (e1 + e2 + e3)
        return a, jnp.concatenate([_rms(a, ag), _rms(b, bg)], axis=1)

    n_rows = b_out.shape[0]
    return _rowwise("mix_fwd", fn, [*os_, *ls_, b_out], [_sds((n_rows, A_WIDTH)), _sds((n_rows, 2 * A_WIDTH), BF16)],
                    consts=[a_g, b_g])


def _mla_q_prep(cq, g, w_uq_p, cos, sin):
    def fn(cq_v, cos_v, sin_v, g_v, w_v):
        cqn = _rms(cq_v, g_v).astype(BF16)
        q = _dot(cqn, w_v, _NN)
        qf = [jnp.concatenate([q[:, h * QK_NOPE:(h + 1) * QK_NOPE],
                               _rope(q[:, 512 + h * LANES:512 + (h + 1) * LANES], cos_v[:, :LANES], sin_v[:, :LANES])], axis=1)
              for h in range(MLA_HEADS)]
        return cqn, qf

    n_rows = cq.shape[0]
    return _rowwise("mla_q_prep", fn, [cq, cos, sin], [_sds((n_rows, Q_LORA), BF16), _sds((MLA_HEADS, n_rows, QK_PAD), BF16)],
                    consts=[g, w_uq_p])


def _mla_kv_prep(ckv, kpe, g, w_ukv_p):
    def fn(ckv_v, kpe_v, g_v, w_v):
        ckvn = _rms(ckv_v, g_v).astype(BF16)
        kvv = _dot(ckvn, w_v, _NN)
        kf = [jnp.concatenate([kvv[:, h * QK_NOPE:(h + 1) * QK_NOPE], kpe_v.astype(F32)], axis=1) for h in range(MLA_HEADS)]
        return ckvn, kf, kvv[:, 512:]

    n_rows = ckv.shape[0]
    return _rowwise("mla_kv_prep", fn, [ckv, kpe],
                    [_sds((n_rows, KV_LORA), BF16), _sds((MLA_HEADS, n_rows, QK_PAD), BF16), _sds((n_rows, MLA_WIDTH), BF16)],
                    consts=[g, w_ukv_p])


MLA_FWD_TILES = (1024, 2048)
MLA_BWD_TILES = (1024, 1024)


def _mla_fwd(qf, kf, v):
    n_rows = v.shape[0]
    tq, tk = min(MLA_FWD_TILES[0], n_rows), min(MLA_FWD_TILES[1], n_rows)
    nq, nk = n_rows // tq, n_rows // tk
    exp2_scale = MLA_SCALE * math.log2(math.e)

    def last_k(i):
        return jnp.right_shift(i * tq + tq - 1, int(math.log2(tk)))

    def body(q_ref, k_ref, v_ref, o_ref, lse_ref, m_sc, l_sc, acc_sc):
        i, j = pl.program_id(1), pl.program_id(2)

        @pl.when(j == 0)
        def _():
            m_sc[...] = jnp.full(m_sc.shape, NEG, F32)
            l_sc[...] = jnp.zeros(l_sc.shape, F32)
            acc_sc[...] = jnp.zeros(acc_sc.shape, F32)

        def step(masked):
            s = _dot(q_ref[...], k_ref[...], _NT)
            if masked:
                row = lax.broadcasted_iota(jnp.int32, (tq, tk), 0) + i * tq
                col = lax.broadcasted_iota(jnp.int32, (tq, tk), 1) + j * tk
                s = jnp.where(col <= row, s, NEG)
            m_prev = m_sc[...]
            m_new = jnp.maximum(m_prev, jnp.max(s, axis=-1, keepdims=True))
            alpha = jnp.exp2((m_prev - m_new) * exp2_scale)
            p = jnp.exp2((s - m_new) * exp2_scale)
            l_sc[...] = alpha * l_sc[...] + jnp.sum(p, axis=-1, keepdims=True)
            acc_sc[...] = alpha * acc_sc[...] + _dot(p.astype(BF16), v_ref[...], _NN)
            m_sc[...] = m_new

        active = j * tk <= i * tq + tq - 1
        crosses = (j + 1) * tk - 1 > i * tq

        @pl.when(active & jnp.logical_not(crosses))
        def _():
            step(False)

        @pl.when(active & crosses)
        def _():
            step(True)

        @pl.when(j == last_k(i))
        def _():
            o_ref[...] = acc_sc[...] / l_sc[...]
            lse_ref[...] = jnp.broadcast_to(m_sc[...] * MLA_SCALE + jnp.log(l_sc[...]), (tq, V_DIM))

    return pl.pallas_call(
        body, name="mla_fwd", grid=(MLA_HEADS, nq, nk),
        in_specs=[pl.BlockSpec((None, tq, QK_PAD), lambda h, i, j: (h, i, 0)),
                  pl.BlockSpec((None, tk, QK_PAD), lambda h, i, j: (h, jnp.minimum(j, last_k(i)), 0)),
                  pl.BlockSpec((tk, V_DIM), lambda h, i, j: (jnp.minimum(j, last_k(i)), h))],
        out_specs=[pl.BlockSpec((tq, V_DIM), lambda h, i, j: (i, h)), pl.BlockSpec((tq, V_DIM), lambda h, i, j: (i, h))],
        out_shape=[_sds((n_rows, MLA_WIDTH)), _sds((n_rows, MLA_WIDTH))],
        scratch_shapes=[pltpu.VMEM((tq, 1), F32), pltpu.VMEM((tq, 1), F32), pltpu.VMEM((tq, V_DIM), F32)],
        compiler_params=_params(("parallel", "parallel", "arbitrary")),
    )(qf, kf, v)


def _mm_res_ln(name, a, w, xres, g, b):
    def epi(acc, xr, g_v, b_v):
        r = ALPHA * xr + acc
        y = _layer_norm(r, g_v, b_v)
        return y, y, r

    return _mm(name, a, w, (F32, BF16, F32), tm=1024, tn=D_MODEL, tk=1024, epilogue=epi, extras=[xres], vecs=[g, b])


def _mm_relu2(x_bf, w):
    def epi(acc):
        r = jnp.maximum(acc, 0.0)
        return (r * r,)

    return _mm("ff1", x_bf, w, (BF16,), tm=1024, tn=2048, tk=D_MODEL, epilogue=epi)[0]


def _loss_fn(y, t):
    def fn(y_v, t_v):
        d = y_v - t_v
        part = jnp.sum(jnp.sum(d * d, axis=1, keepdims=True), axis=0, keepdims=True)
        return d * (1.0 / D_MODEL), part

    dy, part = _rowwise("loss", fn, [y, t], [_sds(y.shape)], reds=[_sds((1, 1))])
    return part * (0.5 / D_MODEL), dy


def _ln_bwd(name, dy, r, g):
    def fn(dy_v, r_v, g_v):
        mu = jnp.mean(r_v, axis=-1, keepdims=True)
        xc = r_v - mu
        rstd = lax.rsqrt(jnp.mean(xc * xc, axis=-1, keepdims=True) + LN_EPS)
        xh = xc * rstd
        dxh = dy_v * g_v
        dr = rstd * (dxh - jnp.mean(dxh, axis=-1, keepdims=True) - xh * jnp.mean(dxh * xh, axis=-1, keepdims=True))
        return dr, dr, jnp.sum(dy_v * xh, axis=0, keepdims=True), jnp.sum(dy_v, axis=0, keepdims=True)

    return _rowwise(name, fn, [dy, r], [_sds(dy.shape), _sds(dy.shape, BF16)], consts=[g],
                    reds=[_sds((1, D_MODEL)), _sds((1, D_MODEL))])


def _head_sum_matrix():
    i = lax.broadcasted_iota(jnp.int32, (A_WIDTH, A_WIDTH), 0) // A_HEAD_DIM
    j = lax.broadcasted_iota(jnp.int32, (A_WIDTH, A_WIDTH), 1) // A_HEAD_DIM
    return (i == j).astype(BF16)


def _mix_bwd(dmixed, a_out, b_out, ls_, a_g, b_g):
    def fn(dm, a, b, l1, l2, l3, ag, bg, e_mat):
        da, dga = _rms_bwd(a, ag, dm[:, :A_WIDTH])
        db, dgb = _rms_bwd(b, bg, dm[:, A_WIDTH:])
        t = da * a
        t_hi = t.astype(BF16)
        t_lo = (t - t_hi.astype(F32)).astype(BF16)
        tsum = _dot(t_hi, e_mat, _NN) + _dot(t_lo, e_mat, _NN)
        mx = jnp.maximum(jnp.maximum(l1, l2), l3)
        e1, e2, e3 = jnp.exp(l1 - mx), jnp.exp(l2 - mx), jnp.exp(l3 - mx)
        inv = 1.0 / (e1 + e2 + e3)
        w1, w2, w3 = e1 * inv, e2 * inv, e3 * inv
        tb = db * b
        delta_b = jnp.concatenate(
            [jnp.broadcast_to(jnp.sum(tb[:, h * V_DIM:(h + 1) * V_DIM], axis=-1, keepdims=True), (tb.shape[0], V_DIM))
             for h in range(MLA_HEADS)], axis=1)
        return w1 * da, w2 * da, w3 * da, w1 * tsum, w2 * tsum, w3 * tsum, db, delta_b, dga, dgb

    n_rows = a_out.shape[0]
    wide = (n_rows, A_WIDTH)
    return _rowwise("mix_bwd", fn, [dmixed, a_out, b_out, *ls_],
                    [_sds(wide, BF16)] * 3 + [_sds(wide)] * 3 + [_sds(wide, BF16), _sds(wide)],
                    consts=[a_g, b_g, _head_sum_matrix()], reds=[_sds((1, A_WIDTH)), _sds((1, MLA_WIDTH))], ts=256)


def _dil_bwd(q, k, v, lse, do, dl, dil):
    n_rows = q.shape[0]
    nb = n_rows // dil // SPAN
    qv, kv_, vv = _dil_view(q, dil), _dil_view(k, dil), _dil_view(v, dil)
    lv, dov, dlv = _dil_view(lse, dil), _dil_view(do, dil), _dil_view(dl, dil)

    def body(q_ref, kp_ref, kc_ref, vp_ref, vc_ref, lse_ref, do_ref, dl_ref, dq_ref, dk_ref, dv_ref, ck, cv):
        m = pl.program_id(1)

        @pl.when(m == 0)
        def _():
            ck[...] = jnp.zeros(ck.shape, F32)
            cv[...] = jnp.zeros(cv.shape, F32)

        @pl.when(m < nb)
        def _():
            mask, even = _band_mask(m, 2), _even_lanes(SPAN)
            for p in range(A_HEADS // 2):
                sl = slice(p * LANES, (p + 1) * LANES)
                q2, do2 = q_ref[:, sl], do_ref[:, sl]
                zero = jnp.zeros_like(q2)
                qcat = jnp.concatenate([jnp.where(even, q2, zero), jnp.where(even, zero, q2)], axis=0)
                docat = jnp.concatenate([jnp.where(even, do2, zero), jnp.where(even, zero, do2)], axis=0)
                kcat = jnp.concatenate([kp_ref[:, sl], kc_ref[:, sl]], axis=0)
                vcat = jnp.concatenate([vp_ref[:, sl], vc_ref[:, sl]], axis=0)
                lse2, dl2 = lse_ref[:, sl], dl_ref[:, sl]
                lse_c = jnp.concatenate([lse2[:, :1], lse2[:, A_HEAD_DIM:A_HEAD_DIM + 1]], axis=0)
                dl_c = jnp.concatenate([dl2[:, :1], dl2[:, A_HEAD_DIM:A_HEAD_DIM + 1]], axis=0)
                pr = jnp.exp(jnp.where(mask, _dot(qcat, kcat, _NT), NEG) - lse_c)
                ds = (pr * (_dot(docat, vcat, _NT) - dl_c)).astype(BF16)
                dq = _dot(ds, kcat, _NN)
                dq_ref[:, sl] = jnp.where(even, dq[:SPAN], dq[SPAN:])
                dk2 = _dot(ds, qcat, _TN)
                dv2 = _dot(pr.astype(BF16), docat, _TN)
                dk_ref[:, sl] = ck[:, sl] + dk2[:SPAN]
                dv_ref[:, sl] = cv[:, sl] + dv2[:SPAN]
                ck[:, sl] = dk2[SPAN:]
                cv[:, sl] = dv2[SPAN:]

        @pl.when(m == nb)
        def _():
            dk_ref[...] = ck[...]
            dv_ref[...] = cv[...]

    last = nb - 1
    cur = pl.BlockSpec((SPAN, A_WIDTH), lambda r, m: (jnp.minimum(m, last), r))
    prev = pl.BlockSpec((SPAN, A_WIDTH), lambda r, m: (jnp.clip(m - 1, 0, last), r))
    dq, dk, dv = pl.pallas_call(
        body, name=f"dil_bwd_d{dil}", grid=(dil, nb + 1),
        in_specs=[cur, prev, cur, prev, cur, cur, cur, cur],
        out_specs=[cur, prev, prev],
        out_shape=[_sds(qv.shape)] * 3,
        scratch_shapes=[pltpu.VMEM((SPAN, A_WIDTH), F32), pltpu.VMEM((SPAN, A_WIDTH), F32)],
        compiler_params=_params(("parallel", "arbitrary")),
    )(qv, kv_, kv_, vv, vv, lv, dov, dlv)
    return dq.reshape(n_rows, A_WIDTH), dk.reshape(n_rows, A_WIDTH), dv.reshape(n_rows, A_WIDTH)


def _mla_bwd(qf, kf, v, lse, do, delta):
    n_rows = v.shape[0]
    tq, tk = min(MLA_BWD_TILES[0], n_rows), min(MLA_BWD_TILES[1], n_rows)
    nq, nk = n_rows // tq, n_rows // tk

    def first_q(j):
        return jnp.right_shift(j * tk, int(math.log2(tq)))

    def body(q_ref, k_ref, v_ref, lse_ref, do_ref, dl_ref, dq_ref, dk_ref, dv_ref, dk_sc, dv_sc):
        j, i = pl.program_id(1), pl.program_id(2)

        @pl.when((j == 0) & (i == 0))
        def _():
            dq_ref[...] = jnp.zeros(dq_ref.shape, F32)

        @pl.when(i == first_q(j))
        def _():
            dk_sc[...] = jnp.zeros(dk_sc.shape, F32)
            dv_sc[...] = jnp.zeros(dv_sc.shape, F32)

        def step(masked):
            q, k, dob = q_ref[...], k_ref[...], do_ref[...]
            s = _dot(q, k, _NT) * MLA_SCALE
            if masked:
                row = lax.broadcasted_iota(jnp.int32, (tq, tk), 0) + i * tq
                col = lax.broadcasted_iota(jnp.int32, (tq, tk), 1) + j * tk
                s = jnp.where(col <= row, s, NEG)
            p = jnp.exp(s - lse_ref[:, :1])
            dp = _dot(dob, v_ref[...], _NT)
            ds = (p * (dp - dl_ref[:, :1]) * MLA_SCALE).astype(BF16)
            dv_sc[...] += _dot(p.astype(BF16), dob, _TN)
            dk_sc[...] += _dot(ds, q, _TN)
            rows = pl.ds(pl.multiple_of(i * tq, tq), tq)
            dq_ref[rows, :] += _dot(ds, k, _NN)

        active = i >= first_q(j)
        crosses = (j + 1) * tk - 1 > i * tq

        @pl.when(active & jnp.logical_not(crosses))
        def _():
            step(False)

        @pl.when(active & crosses)
        def _():
            step(True)

        @pl.when(i == nq - 1)
        def _():
            dk_ref[...] = dk_sc[...]
            dv_ref[...] = dv_sc[...]

    qrow = lambda h, j, i: (jnp.maximum(i, first_q(j)), h)
    return pl.pallas_call(
        body, name="mla_bwd", grid=(MLA_HEADS, nk, nq),
        in_specs=[pl.BlockSpec((None, tq, QK_PAD), lambda h, j, i: (h, jnp.maximum(i, first_q(j)), 0)),
                  pl.BlockSpec((None, tk, QK_PAD), lambda h, j, i: (h, j, 0)),
                  pl.BlockSpec((tk, V_DIM), lambda h, j, i: (j, h)),
                  pl.BlockSpec((tq, V_DIM), qrow), pl.BlockSpec((tq, V_DIM), qrow), pl.BlockSpec((tq, V_DIM), qrow)],
        out_specs=[pl.BlockSpec((None, n_rows, QK_PAD), lambda h, j, i: (h, 0, 0)),
                   pl.BlockSpec((None, tk, QK_PAD), lambda h, j, i: (h, j, 0)),
                   pl.BlockSpec((tk, V_DIM), lambda h, j, i: (j, h))],
        out_shape=[_sds((MLA_HEADS, n_rows, QK_PAD)), _sds((MLA_HEADS, n_rows, QK_PAD)), _sds((n_rows, MLA_WIDTH))],
        scratch_shapes=[pltpu.VMEM((tk, QK_PAD), F32), pltpu.VMEM((tk, V_DIM), F32)],
        compiler_params=_params(("parallel", "arbitrary", "arbitrary")),
    )(qf, kf, v, lse, do, delta)


def _mla_bwd_prep(dqf, dkf, dv, cos, sin):
    def fn(dqf_v, dkf_v, dv_v, cos_v, sin_v):
        dq = jnp.concatenate([dqf_v[h][:, :QK_NOPE] for h in range(MLA_HEADS)]
                             + [_rope_t(dqf_v[h][:, QK_NOPE:], cos_v[:, :LANES], sin_v[:, :LANES]) for h in range(MLA_HEADS)], axis=1)
        dkv = jnp.concatenate([dkf_v[h][:, :QK_NOPE] for h in range(MLA_HEADS)] + [dv_v], axis=1)
        dkpe = dkf_v[0][:, QK_NOPE:] + dkf_v[1][:, QK_NOPE:] + dkf_v[2][:, QK_NOPE:] + dkf_v[3][:, QK_NOPE:]
        return dq, dkv, dkpe

    n_rows = dv.shape[0]
    return _rowwise("mla_bwd_prep", fn, [dqf, dkf, dv, cos, sin],
                    [_sds((n_rows, UQ_PAD), BF16), _sds((n_rows, 1024), BF16), _sds((n_rows, LANES))], ts=256)


def _assemble_dh(dqs, dks, dvs, dcqn, cq, dckvn, ckv, dkpe, cos, sin, gq, gkv):
    def fn(q1, q2, q3, k1, k2, k3, v1, v2, v3, dcqn_v, cq_v, dckvn_v, ckv_v, dkpe_v, cos_v, sin_v, gq_v, gkv_v):
        dqa = _rope_t(q1 + q2 + q3, cos_v, sin_v) * A_SCALE
        dka = _rope_t(k1 + k2 + k3, cos_v, sin_v)
        dcq, dgq = _rms_bwd(cq_v, gq_v, dcqn_v)
        dckv, dgkv = _rms_bwd(ckv_v, gkv_v, dckvn_v)
        dkr = _rope_t(dkpe_v, cos_v[:, :LANES], sin_v[:, :LANES])
        return jnp.concatenate([dqa, dka, v1 + v2 + v3, dcq, dkr, dckv], axis=1), dgq, dgkv

    n_rows = cq.shape[0]
    return _rowwise("assemble_dh", fn, [*dqs, *dks, *dvs, dcqn, cq, dckvn, ckv, dkpe, cos, sin],
                    [_sds((n_rows, IN_PAD), BF16)], consts=[gq, gkv],
                    reds=[_sds((1, Q_LORA)), _sds((1, KV_LORA))], ts=256)


def _layer_fwd(x_f32, x_bf, w, sm, cos, sin):
    qa, ka, va, cq, kpe, ckv = _in_proj(x_bf, w["w_in"], cos, sin)
    os_, ls_ = [], []
    for _, dil in PATTERNS:
        o, lse = _dil_fwd(qa, ka, va, dil)
        os_.append(o)
        ls_.append(lse)
    cqn, qf = _mla_q_prep(cq, sm["q_a_norm"], w["w_uq"], cos, sin)
    ckvn, kf, v = _mla_kv_prep(ckv, kpe, sm["kv_a_norm"], w["w_ukv"])
    b_out, b_lse = _mla_fwd(qf, kf, v)
    a_out, mixed = _mix_fwd(os_, ls_, b_out, sm["a_out_norm"], sm["b_out_norm"])
    x1, x1_bf, r1 = _mm_res_ln("wo_ln1", mixed, w["w_o"], x_f32, sm["ln1_g"], sm["ln1_b"])
    f = _mm_relu2(x1_bf, w["w_ff1"])
    x2, x2_bf, r2 = _mm_res_ln("ff2_ln2", f, w["w_ff2"], x1, sm["ln2_g"], sm["ln2_b"])
    saved = dict(x_bf=x_bf, qa=qa, ka=ka, va=va, cq=cq, ckv=ckv, ls=ls_, cqn=cqn, qf=qf, ckvn=ckvn, kf=kf, v=v,
                 b_out=b_out, b_lse=b_lse, a_out=a_out, mixed=mixed, x1_bf=x1_bf, r1=r1, f=f, r2=r2)
    return x2, x2_bf, saved


def _layer_bwd(dx2, w, sm, sv, cos, sin):
    dr2, dr2_bf, dg2, db2 = _ln_bwd("ln2_bwd", dx2, sv["r2"], sm["ln2_g"])
    (dw_ff2,) = _mm("dw_ff2", sv["f"], dr2_bf, (F32,), ta=True, tm=1024, tn=1024, tk=2048)
    (du,) = _mm("d_u", dr2_bf, w["w_ff2"], (BF16,), tb=True, tm=1024, tn=2048, tk=D_MODEL,
                epilogue=lambda acc, f: (acc * (2.0 * jnp.sqrt(f.astype(F32))),), extras=[sv["f"]])
    (dw_ff1,) = _mm("dw_ff1", sv["x1_bf"], du, (F32,), ta=True, tm=1024, tn=1024, tk=2048)
    (dy1,) = _mm("d_x1", du, w["w_ff1"], (F32,), tb=True, tm=1024, tn=1024, tk=2048,
                 epilogue=lambda acc, d: (acc + ALPHA * d,), extras=[dr2])
    dr1, dr1_bf, dg1, db1 = _ln_bwd("ln1_bwd", dy1, sv["r1"], sm["ln1_g"])
    (dw_o,) = _mm("dw_o", sv["mixed"], dr1_bf, (F32,), ta=True, tm=1024, tn=1024, tk=2048)
    (dmixed,) = _mm("d_mixed", dr1_bf, w["w_o"], (F32,), tb=True, tm=1024, tn=1024, tk=D_MODEL)
    do1, do2, do3, dl1, dl2, dl3, do_b, delta_b, dga, dgb = _mix_bwd(
        dmixed, sv["a_out"], sv["b_out"], sv["ls"], sm["a_out_norm"], sm["b_out_norm"])
    dqs, dks, dvs = [], [], []
    for (_, dil), lse, do, dl in zip(PATTERNS, sv["ls"], (do1, do2, do3), (dl1, dl2, dl3)):
        dq, dk, dv = _dil_bwd(sv["qa"], sv["ka"], sv["va"], lse, do, dl, dil)
        dqs.append(dq)
        dks.append(dk)
        dvs.append(dv)
    dqf, dkf, dv_b = _mla_bwd(sv["qf"], sv["kf"], sv["v"], sv["b_lse"], do_b, delta_b)
    dq_b, dkv_b, dkpe = _mla_bwd_prep(dqf, dkf, dv_b, cos, sin)
    (dw_uq,) = _mm("dw_uq", sv["cqn"], dq_b, (F32,), ta=True, tm=Q_LORA, tn=1024, tk=2048)
    (dcqn,) = _mm("d_cqn", dq_b, w["w_uq"], (F32,), tb=True, tm=1024, tn=Q_LORA, tk=UQ_PAD)
    (dw_ukv,) = _mm("dw_ukv", sv["ckvn"], dkv_b, (F32,), ta=True, tm=KV_LORA, tn=1024, tk=2048)
    (dckvn,) = _mm("d_ckvn", dkv_b, w["w_ukv"], (F32,), tb=True, tm=1024, tn=KV_LORA, tk=1024)
    dh, dgq, dgkv = _assemble_dh(dqs, dks, dvs, dcqn, sv["cq"], dckvn, sv["ckv"], dkpe, cos, sin,
                                 sm["q_a_norm"], sm["kv_a_norm"])
    (dw_in,) = _mm("dw_in", sv["x_bf"], dh, (F32,), ta=True, tm=1024, tn=1024, tk=2048)
    (dx,) = _mm("d_x", dh, w["w_in"], (F32,), tb=True, tm=1024, tn=1024, tk=2048,
                epilogue=lambda acc, d: (acc + ALPHA * d,), extras=[dr1])
    dws = dict(w_in=dw_in, w_uq=dw_uq, w_ukv=dw_ukv, w_o=dw_o, w_ff1=dw_ff1, w_ff2=dw_ff2)
    dsm = dict(q_a_norm=dgq, kv_a_norm=dgkv, a_out_norm=dga, b_out_norm=dgb, ln1_g=dg1, ln1_b=db1, ln2_g=dg2, ln2_b=db2)
    return dx, dws, dsm


def _pad_w_in(w):
    return jnp.concatenate([w[:, :1792], w[:, 1920:1984], jnp.zeros((w.shape[0], 64), w.dtype), w[:, 1792:1920]], axis=1)


def _unpad_w_in(w):
    return jnp.concatenate([w[:, :1792], w[:, 1920:2048], w[:, 1792:1856]], axis=1)


def _rope_tables(n_rows):
    half = A_HEAD_DIM // 2
    inv_freq = ROPE_THETA ** (-jnp.arange(half, dtype=F32) / half)
    ang = jnp.arange(n_rows, dtype=F32)[:, None] * inv_freq[None, :]
    cos = jnp.tile(jnp.cos(ang), (1, 2 * A_HEADS))
    sin = jnp.tile(jnp.concatenate([-jnp.sin(ang), jnp.sin(ang)], axis=1), (1, A_HEADS))
    return cos, sin


W_NAMES = ("w_in", "w_uq", "w_ukv", "w_o", "w_ff1", "w_ff2")
PIECE = dict(w_in=(512, 512), w_uq=(128, 256), w_ukv=(64, 256), w_o=(128, 1024), w_ff1=(512, 1024), w_ff2=(512, 1024))
SHARD_COLS = dict(w_in=496, w_uq=192, w_ukv=256, w_o=1024, w_ff1=1024, w_ff2=1024)


def _col_pieces(t, pad_to):
    rows, _, cols = t.shape
    t = jnp.pad(t, ((0, 0), (0, 0), (0, pad_to - cols)))
    return t.reshape(2, rows // 2, N_CHIPS, pad_to).transpose(2, 0, 1, 3)


def _grad_pieces(dws):
    uq, ukv = dws["w_uq"], dws["w_ukv"]
    uq = jnp.concatenate([uq[:, :512].reshape(Q_LORA, MLA_HEADS, LANES), uq[:, 512:].reshape(Q_LORA, MLA_HEADS, LANES)], axis=2)
    ukv = jnp.concatenate([ukv[:, :512].reshape(KV_LORA, MLA_HEADS, LANES), ukv[:, 512:].reshape(KV_LORA, MLA_HEADS, LANES)], axis=2)
    return dict(
        w_in=_col_pieces(_unpad_w_in(dws["w_in"]).reshape(D_MODEL, N_CHIPS, SHARD_COLS["w_in"]), 512),
        w_uq=_col_pieces(uq, 256), w_ukv=_col_pieces(ukv, 256),
        w_o=dws["w_o"].reshape(N_CHIPS, 2, 128, D_MODEL),
        w_ff1=_col_pieces(dws["w_ff1"].reshape(D_MODEL, N_CHIPS, 1024), 1024),
        w_ff2=dws["w_ff2"].reshape(N_CHIPS, 2, 512, D_MODEL))


def _weights_from_pieces(p):
    def cols(t):
        return t.transpose(1, 2, 0, 3).reshape(2 * t.shape[2], N_CHIPS, t.shape[3])

    uq, ukv = cols(p["w_uq"]), cols(p["w_ukv"])
    return dict(
        w_in=_pad_w_in(cols(p["w_in"])[:, :, :SHARD_COLS["w_in"]].reshape(D_MODEL, IN_COLS)),
        w_uq=jnp.concatenate([uq[:, :, :LANES].reshape(Q_LORA, 512), uq[:, :, LANES:].reshape(Q_LORA, 512)], axis=1),
        w_ukv=jnp.concatenate([ukv[:, :, :LANES].reshape(KV_LORA, 512), ukv[:, :, LANES:].reshape(KV_LORA, 512)], axis=1),
        w_o=p["w_o"].reshape(D_MODEL, D_MODEL),
        w_ff1=cols(p["w_ff1"]).reshape(D_MODEL, D_FF),
        w_ff2=p["w_ff2"].reshape(D_FF, D_MODEL))


def _my_piece(name, shard, half):
    rows, cols = shard.shape[0] // 2, shard.shape[1]
    t = lax.dynamic_slice_in_dim(shard, half * rows, rows, axis=0).astype(BF16)
    return jnp.pad(t, ((0, 0), (0, PIECE[name][1] - cols)))


MESH = pl.DeviceIdType.MESH
ANY = pl.BlockSpec(memory_space=pl.ANY)


def _comm_call(name, body, ins, out_shapes, n_sems):
    return pl.pallas_call(
        body, name=name, out_shape=list(out_shapes), in_specs=[ANY] * len(ins), out_specs=[ANY] * len(out_shapes),
        scratch_shapes=[pltpu.SemaphoreType.DMA((n_sems,)), pltpu.SemaphoreType.DMA((n_sems,))],
    )(*ins)


def _all_gather_pieces(name, blocks):
    n = len(blocks)

    def body(*refs):
        x_refs, out_refs, send_sems, recv_sems = refs[:n], refs[n:2 * n], refs[2 * n], refs[2 * n + 1]
        x, y, c = lax.axis_index("x"), lax.axis_index("y"), lax.axis_index("c")
        me, sibling = (x, y, c), (x, y, 1 - c)
        chips = [(1 - x, y), (x, 1 - y), (1 - x, 1 - y)]

        def copy(t, k, blk, to, src=None):
            px, py, pc = blk
            slot = out_refs[t].at[4 * px + 2 * py + pc]
            return pltpu.make_async_remote_copy(
                src_ref=slot if src is None else src, dst_ref=slot,
                send_sem=send_sems.at[7 * t + k], recv_sem=recv_sems.at[7 * t + k], device_id=to, device_id_type=MESH)

        sent = []
        for t in range(n):
            sent.append(copy(t, 0, me, sibling, src=x_refs[t]))
            sent += [copy(t, 1 + j, me, (*chip, c), src=x_refs[t]) for j, chip in enumerate(chips)]
        for cp in sent:
            cp.start()
        for j, chip in enumerate(chips):
            for t in range(n):
                copy(t, 1 + j, (*chip, c), me).wait_recv()
                sent.append(copy(t, 4 + j, (*chip, c), sibling))
                sent[-1].start()
        for t in range(n):
            copy(t, 0, sibling, me).wait_recv()
        for j, chip in enumerate(chips):
            for t in range(n):
                copy(t, 4 + j, (*chip, 1 - c), me).wait_recv()
        for cp in sent:
            cp.wait_send()

    return _comm_call(name, body, blocks, [_sds((N_DEV,) + b.shape, b.dtype) for b in blocks], 7 * n)


def _swap_with_sibling(name, gs):
    n = len(gs)

    def body(*refs):
        g_refs, got_refs, send_sems, recv_sems = refs[:n], refs[n:2 * n], refs[2 * n], refs[2 * n + 1]
        c = lax.axis_index("c")
        sibling = (lax.axis_index("x"), lax.axis_index("y"), 1 - c)
        cps = [pltpu.make_async_remote_copy(src_ref=g_refs[t].at[k, 1 - c], dst_ref=got_refs[t].at[k],
                                            send_sem=send_sems.at[N_CHIPS * t + k], recv_sem=recv_sems.at[N_CHIPS * t + k],
                                            device_id=sibling, device_id_type=MESH)
               for t in range(n) for k in range(N_CHIPS)]
        for cp in cps:
            cp.start()
        for cp in cps:
            cp.wait()

    return _comm_call(name, body, gs, [_sds((N_CHIPS,) + g.shape[2:], g.dtype) for g in gs], N_CHIPS * n)


def _scatter_to_chips(name, hs):
    n = len(hs)

    def body(*refs):
        h_refs, got_refs, send_sems, recv_sems = refs[:n], refs[n:2 * n], refs[2 * n], refs[2 * n + 1]
        x, y, c = lax.axis_index("x"), lax.axis_index("y"), lax.axis_index("c")
        chips = [(1 - x, y), (x, 1 - y), (1 - x, 1 - y)]
        cps = [pltpu.make_async_remote_copy(src_ref=h_refs[t].at[2 * px + py], dst_ref=got_refs[t].at[rel],
                                            send_sem=send_sems.at[3 * t + rel], recv_sem=recv_sems.at[3 * t + rel],
                                            device_id=(px, py, c), device_id_type=MESH)
               for t in range(n) for rel, (px, py) in enumerate(chips)]
        for cp in cps:
            cp.start()
        for cp in cps:
            cp.wait()

    return _comm_call(name, body, hs, [_sds((3,) + h.shape[1:], h.dtype) for h in hs], 3 * n)


def _share_with_sibling(name, rs):
    n = len(rs)

    def body(*refs):
        r_refs, got_refs, send_sems, recv_sems = refs[:n], refs[n:2 * n], refs[2 * n], refs[2 * n + 1]
        sibling = (lax.axis_index("x"), lax.axis_index("y"), 1 - lax.axis_index("c"))
        cps = [pltpu.make_async_remote_copy(src_ref=r_refs[t], dst_ref=got_refs[t], send_sem=send_sems.at[t],
                                            recv_sem=recv_sems.at[t], device_id=sibling, device_id_type=MESH)
               for t in range(n)]
        for cp in cps:
            cp.start()
        for cp in cps:
            cp.wait()

    return _comm_call(name, body, rs, [_sds(r.shape, r.dtype) for r in rs], n)


SUM_ROWS = 256


def _pair_sum(name, g, got, c_arr):
    _, _, pr, pc = g.shape
    tr = min(pr, SUM_ROWS)

    def body(c_ref, g_ref, got_ref, o32_ref, o16_ref):
        s = g_ref[...] + got_ref[...]
        o32_ref[...] = s
        o16_ref[...] = s.astype(BF16)

    part = pl.BlockSpec((None, tr, pc), lambda k, i, c_ref: (k, i, 0))
    return pl.pallas_call(
        body, name=name, out_shape=[_sds((N_CHIPS, pr, pc)), _sds((N_CHIPS, pr, pc), BF16)],
        grid_spec=pltpu.PrefetchScalarGridSpec(
            num_scalar_prefetch=1, grid=(N_CHIPS, pr // tr),
            in_specs=[pl.BlockSpec((None, None, tr, pc), lambda k, i, c_ref: (k, c_ref[0], i, 0)), part],
            out_specs=[part, part]),
        compiler_params=_params(("parallel", "parallel")),
    )(c_arr, g, got)


def _chips_sum(name, h32, got, chip_arr):
    _, pr, pc = h32.shape
    tr = min(pr, SUM_ROWS)

    def body(chip_ref, h_ref, got_ref, o_ref):
        o_ref[...] = h_ref[...] + got_ref[0] + got_ref[1] + got_ref[2]

    return pl.pallas_call(
        body, name=name, out_shape=_sds((pr, pc)),
        grid_spec=pltpu.PrefetchScalarGridSpec(
            num_scalar_prefetch=1, grid=(pr // tr,),
            in_specs=[pl.BlockSpec((None, tr, pc), lambda i, chip_ref: (chip_ref[0], i, 0)),
                      pl.BlockSpec((3, tr, pc), lambda i, chip_ref: (0, i, 0))],
            out_specs=pl.BlockSpec((tr, pc), lambda i, chip_ref: (i, 0))),
        compiler_params=_params(("parallel",)),
    )(chip_arr, h32, got)


def _reduce_layer_grads(tag, dws, c, chip):
    gp = _grad_pieces(dws)
    c_arr, chip_arr = jnp.reshape(c, (1,)).astype(jnp.int32), jnp.reshape(chip, (1,)).astype(jnp.int32)
    gots = _swap_with_sibling(f"rs_pair_{tag}", [gp[n] for n in W_NAMES])
    sums = [_pair_sum(f"rs_pair_sum_{n}_{tag}", gp[n], got, c_arr) for n, got in zip(W_NAMES, gots)]
    parts = _scatter_to_chips(f"rs_chips_{tag}", [s16 for _, s16 in sums])
    mine = [_chips_sum(f"rs_chips_sum_{n}_{tag}", s32, part, chip_arr) for n, (s32, _), part in zip(W_NAMES, sums, parts)]
    theirs = _share_with_sibling(f"rs_share_{tag}", mine)
    out = {}
    for n, a, b in zip(W_NAMES, mine, theirs):
        both = jnp.where(c == 0, jnp.concatenate([a, b], axis=0), jnp.concatenate([b, a], axis=0))
        out[n] = both[:, :SHARD_COLS[n]]
    return out


def _gather_layer_weights(tag, shards, c, dev):
    blocks = [_my_piece(n, shards[n], c) for n in W_NAMES]
    gathered = _all_gather_pieces(f"ag_{tag}", blocks)
    pieces = {n: lax.dynamic_update_index_in_dim(g, b, dev, 0).reshape((N_CHIPS, 2) + b.shape)
              for n, g, b in zip(W_NAMES, gathered, blocks)}
    return _weights_from_pieces(pieces)


def _all_reduce_small(vec):
    rows, lanes = vec.shape

    def gather_body(x_ref, out_ref, send_sems, recv_sems):
        x, y, c = lax.axis_index("x"), lax.axis_index("y"), lax.axis_index("c")
        me, sibling = (x, y, c), (x, y, 1 - c)
        chips = [(1 - x, y), (x, 1 - y), (1 - x, 1 - y)]

        def slot(px, py, pc):
            return out_ref.at[4 * px + 2 * py + pc]

        def copy(k, blk, to, src=None):
            return pltpu.make_async_remote_copy(
                src_ref=slot(*blk) if src is None else src, dst_ref=slot(*blk),
                send_sem=send_sems.at[k], recv_sem=recv_sems.at[k], device_id=to, device_id_type=MESH)

        out_ref[4 * x + 2 * y + c] = x_ref[...]
        first = [copy(0, me, sibling, src=x_ref)]
        first += [copy(1 + j, me, (*chip, c), src=x_ref) for j, chip in enumerate(chips)]
        for cp in first:
            cp.start()
        passed = [copy(4 + j, (*chip, c), sibling) for j, chip in enumerate(chips)]
        for j, chip in enumerate(chips):
            copy(1 + j, (*chip, c), me).wait_recv()
            passed[j].start()
        copy(0, sibling, me).wait_recv()
        for j, chip in enumerate(chips):
            copy(4 + j, (*chip, 1 - c), me).wait_recv()
        for cp in first + passed:
            cp.wait_send()

    vmem = pl.BlockSpec(memory_space=pltpu.VMEM)
    allv = pl.pallas_call(
        gather_body, name="small_all_gather", out_shape=_sds((N_DEV, rows, lanes)),
        in_specs=[vmem], out_specs=vmem,
        scratch_shapes=[pltpu.SemaphoreType.DMA((7,)), pltpu.SemaphoreType.DMA((7,))],
    )(vec)

    def sum_body(a_ref, o_ref):
        acc = a_ref[0]
        for d in range(1, N_DEV):
            acc = acc + a_ref[d]
        o_ref[...] = acc

    return pl.pallas_call(sum_body, name="small_sum", out_shape=_sds((rows, lanes)), in_specs=[vmem], out_specs=vmem)(allv)


def _adamw(name, w, g, m, v, ts=512):
    def fn(w_v, g_v, m_v, v_v):
        m_n = ADAM_B1 * m_v + (1.0 - ADAM_B1) * g_v
        v_n = ADAM_B2 * v_v + (1.0 - ADAM_B2) * (g_v * g_v)
        m_hat = m_n / (1.0 - ADAM_B1 ** ADAM_STEP)
        v_hat = v_n / (1.0 - ADAM_B2 ** ADAM_STEP)
        delta = -ADAM_LR * (m_hat / (jnp.sqrt(v_hat) + ADAM_EPS) + ADAM_WD * w_v)
        return delta, m_n, v_n

    return _rowwise(name, fn, [w, g, m, v], [_sds(w.shape)] * 3, ts=ts)


def _pack_small(per_layer):
    flat = jnp.concatenate([per_layer[l][n].reshape(-1) for l in range(DEPTH) for n in SMALL_NAMES])
    return jnp.pad(flat, (0, SMALL_ROWS * LANES - flat.shape[0])).reshape(SMALL_ROWS, LANES)


def _unpack_small(packed):
    flat = packed.reshape(-1)
    per = sum(SMALL_SIZES)
    out = {}
    for n, size, off in zip(SMALL_NAMES, SMALL_SIZES, [sum(SMALL_SIZES[:i]) for i in range(len(SMALL_SIZES))]):
        out[n] = jnp.stack([flat[l * per + off:l * per + off + size] for l in range(DEPTH)])
    return out


def _fwd_bwd(x, target, weights, smalls, on_layer_grads):
    cos, sin = _rope_tables(x.shape[0])
    h_f32, h_bf = x, x.astype(BF16)
    saved = []
    for l in range(len(weights)):
        h_f32, h_bf, sv = _layer_fwd(h_f32, h_bf, weights[l], smalls[l], cos, sin)
        saved.append(sv)
    loss_part, dh = _loss_fn(h_f32, target)
    small_grads = [None] * len(weights)
    for l in reversed(range(len(weights))):
        dh, dws, small_grads[l] = _layer_bwd(dh, weights[l], smalls[l], saved[l], cos, sin)
        on_layer_grads(l, dws)
    return loss_part, dh, small_grads


def kernel(x, w_in, q_a_norm, kv_a_norm, w_uq, w_ukv, a_out_norm, b_out_norm, w_o, ln1_g, ln1_b, w_ff1, w_ff2, ln2_g, ln2_b, loss_target, m_w_in, m_q_a_norm, m_kv_a_norm, m_w_uq, m_w_ukv, m_a_out_norm, m_b_out_norm, m_w_o, m_ln1_g, m_ln1_b, m_w_ff1, m_w_ff2, m_ln2_g, m_ln2_b, v_w_in, v_q_a_norm, v_kv_a_norm, v_w_uq, v_w_ukv, v_a_out_norm, v_b_out_norm, v_w_o, v_ln1_g, v_ln1_b, v_w_ff1, v_w_ff2, v_ln2_g, v_ln2_b):
    c = lax.axis_index("c")
    chip = 2 * lax.axis_index("x") + lax.axis_index("y")
    big = dict(w_in=w_in, w_uq=w_uq, w_ukv=w_ukv, w_o=w_o, w_ff1=w_ff1, w_ff2=w_ff2)
    big_m = dict(w_in=m_w_in, w_uq=m_w_uq, w_ukv=m_w_ukv, w_o=m_w_o, w_ff1=m_w_ff1, w_ff2=m_w_ff2)
    big_v = dict(w_in=v_w_in, w_uq=v_w_uq, w_ukv=v_w_ukv, w_o=v_w_o, w_ff1=v_w_ff1, w_ff2=v_w_ff2)
    small = dict(q_a_norm=q_a_norm, kv_a_norm=kv_a_norm, a_out_norm=a_out_norm, b_out_norm=b_out_norm,
                 ln1_g=ln1_g, ln1_b=ln1_b, ln2_g=ln2_g, ln2_b=ln2_b)
    small_m = dict(q_a_norm=m_q_a_norm, kv_a_norm=m_kv_a_norm, a_out_norm=m_a_out_norm, b_out_norm=m_b_out_norm,
                   ln1_g=m_ln1_g, ln1_b=m_ln1_b, ln2_g=m_ln2_g, ln2_b=m_ln2_b)
    small_v = dict(q_a_norm=v_q_a_norm, kv_a_norm=v_kv_a_norm, a_out_norm=v_a_out_norm, b_out_norm=v_b_out_norm,
                   ln1_g=v_ln1_g, ln1_b=v_ln1_b, ln2_g=v_ln2_g, ln2_b=v_ln2_b)

    weights = [_gather_layer_weights(f"l{l}", {n: big[n][l] for n in W_NAMES}, c, 2 * chip + c) for l in range(DEPTH)]
    smalls = [{n: small[n][l][None, :] for n in SMALL_NAMES} for l in range(DEPTH)]
    shard_grads = [None] * DEPTH

    def reduce_layer(l, dws):
        shard_grads[l] = _reduce_layer_grads(f"l{l}", dws, c, chip)

    loss_part, grad_x, small_grads = _fwd_bwd(x[0], loss_target[0], weights, smalls, reduce_layer)
    loss = lax.psum(loss_part[0, 0], ("x", "y", "c"))
    grad_x = grad_x[None]

    g_small_packed = _all_reduce_small(_pack_small(small_grads))
    pack_in = lambda d: _pack_small([{n: d[n][l] for n in SMALL_NAMES} for l in range(DEPTH)])
    sd, sm_, sv_ = _adamw("adamw_small", pack_in(small), g_small_packed, pack_in(small_m), pack_in(small_v), ts=SMALL_ROWS)
    g_small, d_small, m_small, v_small = (_unpack_small(t) for t in (g_small_packed, sd, sm_, sv_))

    g_big, d_big, m_big, v_big = {}, {}, {}, {}
    for n in W_NAMES:
        g = jnp.stack([shard_grads[l][n] for l in range(DEPTH)])
        shape = g.shape
        flat = lambda t: t.reshape(shape[0] * shape[1], shape[2])
        d_, m_, v_ = _adamw(f"adamw_{n}", flat(big[n]), flat(g), flat(big_m[n]), flat(big_v[n]))
        g_big[n], d_big[n], m_big[n], v_big[n] = g, d_.reshape(shape), m_.reshape(shape), v_.reshape(shape)

    order = ("w_in", "q_a_norm", "kv_a_norm", "w_uq", "w_ukv", "a_out_norm", "b_out_norm", "w_o", "ln1_g", "ln1_b",
             "w_ff1", "w_ff2", "ln2_g", "ln2_b")
    pick = lambda bigd, smalld: [bigd[n] if n in bigd else smalld[n] for n in order]
    return (loss, grad_x, *pick(g_big, g_small), *pick(d_big, d_small), *pick(m_big, m_small), *pick(v_big, v_small))
```

```python
import math

import jax
import jax.numpy as jnp
from jax import lax
from jax.experimental import pallas as pl
from jax.experimental.pallas import tpu as pltpu

F32, BF16 = jnp.float32, jnp.bfloat16

D_MODEL = 1024
DEPTH = 4
A_HEAD_DIM = 64
A_HEADS = 8
A_WIDTH = 512
PATTERNS = ((128, 1), (512, 4), (2048, 16))
SPAN = 128
MLA_HEADS = 4
QK_NOPE = 128
QK_ROPE = 64
V_DIM = 128
Q_LORA = 256
KV_LORA = 128
MLA_WIDTH = 512
QK_PAD = 256
IN_COLS = 1984
IN_PAD = 2048
UQ_PAD = 1024
D_FF = 4096
ROPE_THETA = 10000.0
ALPHA = (2.0 * DEPTH) ** 0.25
LN_EPS = 1e-5
RMS_EPS = 1e-6
MLA_SCALE = (QK_NOPE + QK_ROPE) ** -0.5
A_SCALE = A_HEAD_DIM ** -0.5

ADAM_LR, ADAM_B1, ADAM_B2, ADAM_EPS, ADAM_WD, ADAM_STEP = 0.001, 0.9, 0.999, 1e-08, 0.01, 10

VMEM_LIMIT_BYTES = 56 * 1024 * 1024
NEG = -1e30
N_CHIPS, N_DEV = 4, 8
LANES = 128

SMALL_NAMES = ("q_a_norm", "kv_a_norm", "a_out_norm", "b_out_norm", "ln1_g", "ln1_b", "ln2_g", "ln2_b")
SMALL_SIZES = (256, 128, 512, 512, 1024, 1024, 1024, 1024)
SMALL_ROWS = 176


def _params(sem):
    return pltpu.CompilerParams(dimension_semantics=sem, vmem_limit_bytes=VMEM_LIMIT_BYTES)


def _rowwise(name, fn, ins, outs, consts=(), reds=(), ts=512):
    n_rows = ins[0].shape[-2]
    ts = min(ts, n_rows)
    assert n_rows % ts == 0
    n_in, n_c, n_o, n_r = len(ins), len(consts), len(outs), len(reds)

    def tile_spec(shape):
        if len(shape) == 2:
            return pl.BlockSpec((ts, shape[1]), lambda i: (i, 0))
        return pl.BlockSpec((shape[0], ts, shape[2]), lambda i: (0, i, 0))

    def whole_spec(shape):
        return pl.BlockSpec(shape, lambda i: (0,) * len(shape))

    def body(*refs):
        in_refs, c_refs = refs[:n_in], refs[n_in:n_in + n_c]
        o_refs, r_refs = refs[n_in + n_c:n_in + n_c + n_o], refs[n_in + n_c + n_o:]
        res = fn(*[r[...] for r in in_refs], *[r[...] for r in c_refs])
        res = tuple(res) if isinstance(res, (tuple, list)) else (res,)
        for r, v in zip(o_refs, res[:n_o]):
            if len(r.shape) == 3:
                for g in range(r.shape[0]):
                    r[g] = v[g].astype(r.dtype)
            else:
                r[...] = v.astype(r.dtype)
        if n_r:
            i = pl.program_id(0)

            @pl.when(i == 0)
            def _():
                for r, v in zip(r_refs, res[n_o:]):
                    r[...] = v

            @pl.when(i > 0)
            def _():
                for r, v in zip(r_refs, res[n_o:]):
                    r[...] += v

    out = pl.pallas_call(
        body, name=name, grid=(n_rows // ts,),
        in_specs=[tile_spec(a.shape) for a in ins] + [whole_spec(c.shape) for c in consts],
        out_specs=[tile_spec(o.shape) for o in outs] + [whole_spec(r.shape) for r in reds],
        out_shape=list(outs) + list(reds),
        compiler_params=_params(("arbitrary",)),
    )(*ins, *consts)
    return out


def _sds(shape, dtype=F32):
    return jax.ShapeDtypeStruct(tuple(shape), dtype)


def _hosted(comm, grid):
    if comm is None:
        return [], [], [], [], (lambda *a: None), (lambda *a: None)

    def edge(which, at, ins, outs, sems):
        ids = [pl.program_id(d) for d in range(len(grid))]
        hit = ids[0] == at[0]
        for i, v in zip(ids[1:], at[1:]):
            hit = hit & (i == v)

        @pl.when(hit)
        def _():
            which(ins, outs, *sems)

    begin = lambda ins, outs, sems: edge(comm.start, [0] * len(grid), ins, outs, sems)
    end = lambda ins, outs, sems: edge(comm.finish, [g - 1 for g in grid], ins, outs, sems)
    return comm.ins, [ANY] * len(comm.ins), comm.out_shapes, [ANY] * len(comm.out_shapes), begin, end


def _mm(name, a, b, out_dtypes, *, ta=False, tb=False, tm=512, tn=512, tk=512, epilogue=None, extras=(), vecs=(),
        comm=None):
    (k_dim, m_dim) = a.shape if ta else a.shape[::-1]
    (n_dim, k2) = b.shape if tb else b.shape[::-1]
    assert k_dim == k2
    tm, tn, tk = min(tm, m_dim), min(tn, n_dim), min(tk, k_dim)
    assert m_dim % tm == 0 and n_dim % tn == 0 and k_dim % tk == 0
    nk = k_dim // tk
    grid = (m_dim // tm, n_dim // tn, nk)
    a_spec = pl.BlockSpec((tk, tm), lambda i, j, k: (k, i)) if ta else pl.BlockSpec((tm, tk), lambda i, j, k: (i, k))
    b_spec = pl.BlockSpec((tn, tk), lambda i, j, k: (j, k)) if tb else pl.BlockSpec((tk, tn), lambda i, j, k: (k, j))
    dims = (((0 if ta else 1,), (1 if tb else 0,)), ((), ()))
    n_e, n_v, n_o = len(extras), len(vecs), len(out_dtypes)
    c_ins, c_in_specs, c_outs, c_out_specs, begin, end = _hosted(comm, grid)
    n_ci, n_co = len(c_ins), len(c_outs)

    def body(a_ref, b_ref, *refs):
        e_refs, v_refs, ci_refs = refs[:n_e], refs[n_e:n_e + n_v], refs[n_e + n_v:n_e + n_v + n_ci]
        o_refs = refs[n_e + n_v + n_ci:n_e + n_v + n_ci + n_o]
        co_refs = refs[n_e + n_v + n_ci + n_o:n_e + n_v + n_ci + n_o + n_co]
        scratch = refs[n_e + n_v + n_ci + n_o + n_co:]
        sems = scratch[-2:]
        begin(ci_refs, co_refs, sems)
        part = lax.dot_general(a_ref[...].astype(BF16), b_ref[...].astype(BF16), dims, preferred_element_type=F32)

        def finish(acc):
            outs = epilogue(acc, *[r[...] for r in e_refs], *[r[...] for r in v_refs]) if epilogue else (acc,)
            for r, v in zip(o_refs, outs):
                r[...] = v.astype(r.dtype)

        if nk == 1:
            finish(part)
        else:
            acc_ref = scratch[0]
            k = pl.program_id(2)

            @pl.when(k == 0)
            def _():
                acc_ref[...] = part

            @pl.when(k > 0)
            def _():
                acc_ref[...] += part

            @pl.when(k == nk - 1)
            def _():
                finish(acc_ref[...])
        end(ci_refs, co_refs, sems)

    tile = pl.BlockSpec((tm, tn), lambda i, j, k: (i, j))
    out = pl.pallas_call(
        body, name=name, grid=grid,
        in_specs=[a_spec, b_spec] + [tile] * n_e + [pl.BlockSpec((1, tn), lambda i, j, k: (0, j))] * n_v + c_in_specs,
        out_specs=[tile] * n_o + c_out_specs,
        out_shape=[_sds((m_dim, n_dim), d) for d in out_dtypes] + c_outs,
        scratch_shapes=([pltpu.VMEM((tm, tn), F32)] if nk > 1 else []) + (comm.sem_scratch() if comm else []),
        compiler_params=_params(("arbitrary",) * 3 if comm else ("parallel", "parallel", "arbitrary")),
    )(a, b, *extras, *vecs, *c_ins)
    return out


def _swap_halves(x):
    width = x.shape[1]
    lane = lax.broadcasted_iota(jnp.int32, x.shape, 1)
    return jnp.where((lane % 64) < 32, pltpu.roll(x, width - 32, 1), pltpu.roll(x, 32, 1))


def _rope(x, cos, sin_signed):
    return x * cos + _swap_halves(x) * sin_signed


def _rope_t(d, cos, sin_signed):
    return d * cos - _swap_halves(d) * sin_signed


def _rms(x, g):
    rstd = lax.rsqrt(jnp.mean(x * x, axis=-1, keepdims=True) + RMS_EPS)
    return x * rstd * g


def _rms_bwd(x, g, dy):
    rstd = lax.rsqrt(jnp.mean(x * x, axis=-1, keepdims=True) + RMS_EPS)
    xh = x * rstd
    dyg = dy * g
    dx = rstd * (dyg - xh * jnp.mean(dyg * xh, axis=-1, keepdims=True))
    return dx, jnp.sum(dy * xh, axis=0, keepdims=True)


def _layer_norm(r, g, b):
    mu = jnp.mean(r, axis=-1, keepdims=True)
    xc = r - mu
    var = jnp.mean(xc * xc, axis=-1, keepdims=True)
    return xc * lax.rsqrt(var + LN_EPS) * g + b


def _dot(a, b, dims):
    return lax.dot_general(a, b, (dims, ((), ())), preferred_element_type=F32)


_NN, _NT, _TN = ((1,), (0,)), ((1,), (1,)), ((0,), (0,))


def _in_proj(x_bf, w_in_p, cos, sin):
    n_rows = x_bf.shape[0]
    ts = 512

    def body(x_ref, w_ref, cos_ref, sin_ref, qa_ref, ka_ref, va_ref, cq_ref, kpe_ref, ckv_ref):
        xb = x_ref[...]
        cos_v, sin_v = cos_ref[...], sin_ref[...]

        def seg(lo, hi):
            return _dot(xb, w_ref[:, lo:hi], _NN)

        qa_ref[...] = (_rope(seg(0, 512), cos_v, sin_v) * A_SCALE).astype(BF16)
        ka_ref[...] = _rope(seg(512, 1024), cos_v, sin_v).astype(BF16)
        va_ref[...] = seg(1024, 1536).astype(BF16)
        cq_ref[...] = seg(1536, 1792)
        kpe_ref[...] = _rope(seg(1792, 1920), cos_v[:, :LANES], sin_v[:, :LANES]).astype(BF16)
        ckv_ref[...] = seg(1920, 2048)

    row = lambda c: pl.BlockSpec((ts, c), lambda i: (i, 0))
    return pl.pallas_call(
        body, name="in_proj", grid=(n_rows // ts,),
        in_specs=[row(D_MODEL), pl.BlockSpec((D_MODEL, IN_PAD), lambda i: (0, 0)), row(A_WIDTH), row(A_WIDTH)],
        out_specs=[row(A_WIDTH), row(A_WIDTH), row(A_WIDTH), row(Q_LORA), row(LANES), row(KV_LORA)],
        out_shape=[_sds((n_rows, A_WIDTH), BF16), _sds((n_rows, A_WIDTH), BF16), _sds((n_rows, A_WIDTH), BF16),
                   _sds((n_rows, Q_LORA)), _sds((n_rows, LANES), BF16), _sds((n_rows, KV_LORA))],
        compiler_params=_params(("parallel",)),
    )(x_bf, w_in_p, cos, sin)


def _band_mask(m, heads):
    qi = lax.broadcasted_iota(jnp.int32, (heads * SPAN, 2 * SPAN), 0) % SPAN
    kj = lax.broadcasted_iota(jnp.int32, (heads * SPAN, 2 * SPAN), 1)
    return ((kj < SPAN) & (kj >= qi) & (m > 0)) | ((kj >= SPAN) & ((kj - SPAN) <= qi))


def _even_lanes(rows):
    return lax.broadcasted_iota(jnp.int32, (rows, LANES), 1) < A_HEAD_DIM


def _dil_view(t, dil):
    rows, width = t.shape
    return t.reshape(rows // dil, dil * width)


def _dil_fwd(q, k, v, dil):
    n_rows = q.shape[0]
    nb = n_rows // dil // SPAN
    qv, kv_, vv = _dil_view(q, dil), _dil_view(k, dil), _dil_view(v, dil)

    def body(q_ref, kp_ref, kc_ref, vp_ref, vc_ref, o_ref, lse_ref):
        mask = _band_mask(pl.program_id(1), 1)
        even, even2 = _even_lanes(SPAN), _even_lanes(2 * SPAN)
        for p in range(A_HEADS // 2):
            sl = slice(p * LANES, (p + 1) * LANES)
            q2 = q_ref[:, sl]
            kcat = jnp.concatenate([kp_ref[:, sl], kc_ref[:, sl]], axis=0)
            vcat = jnp.concatenate([vp_ref[:, sl], vc_ref[:, sl]], axis=0)
            zero, one = jnp.zeros_like(q2), jnp.ones_like(vcat)
            res, lses = [], []
            for first in (True, False):
                qh = jnp.where(even, q2, zero) if first else jnp.where(even, zero, q2)
                vh = jnp.where(even2, vcat, one) if first else jnp.where(even2, one, vcat)
                s = jnp.where(mask, _dot(qh, kcat, _NT), NEG)
                mx = jnp.max(s, axis=-1, keepdims=True)
                r = _dot(jnp.exp(s - mx).astype(BF16), vh, _NN)
                den = pltpu.roll(r, A_HEAD_DIM, 1)
                res.append(r / den)
                lses.append(mx + jnp.log(den))
            o_ref[:, sl] = jnp.where(even, res[0], res[1])
            lse_ref[:, sl] = jnp.where(even, lses[0], lses[1])

    cur = pl.BlockSpec((SPAN, A_WIDTH), lambda r, m: (m, r))
    prev = pl.BlockSpec((SPAN, A_WIDTH), lambda r, m: (jnp.maximum(m - 1, 0), r))
    o, lse = pl.pallas_call(
        body, name=f"dil_fwd_d{dil}", grid=(dil, nb),
        in_specs=[cur, prev, cur, prev, cur],
        out_specs=[cur, cur],
        out_shape=[_sds(qv.shape), _sds(qv.shape)],
        compiler_params=_params(("parallel", "arbitrary")),
    )(qv, kv_, kv_, vv, vv)
    return o.reshape(n_rows, A_WIDTH), lse.reshape(n_rows, A_WIDTH)


def _mix_fwd(os_, ls_, b_out, a_g, b_g):
    def fn(o1, o2, o3, l1, l2, l3, b, ag, bg):
        mx = jnp.maximum(jnp.maximum(l1, l2), l3)
        e1, e2, e3 = jnp.exp(l1 - mx), jnp.exp(l2 - mx), jnp.exp(l3 - mx)
        a = (e1 * o1 + e2 * o2 + e3 * o3) / (e1 + e2 + e3)
        return a, jnp.concatenate([_rms(a, ag), _rms(b, bg)], axis=1)

    n_rows = b_out.shape[0]
    return _rowwise("mix_fwd", fn, [*os_, *ls_, b_out], [_sds((n_rows, A_WIDTH)), _sds((n_rows, 2 * A_WIDTH), BF16)],
                    consts=[a_g, b_g])


def _mla_q_prep(cq, g, w_uq_p, cos, sin):
    def fn(cq_v, cos_v, sin_v, g_v, w_v):
        cqn = _rms(cq_v, g_v).astype(BF16)
        q = _dot(cqn, w_v, _NN)
        qf = [jnp.concatenate([q[:, h * QK_NOPE:(h + 1) * QK_NOPE],
                               _rope(q[:, 512 + h * LANES:512 + (h + 1) * LANES], cos_v[:, :LANES], sin_v[:, :LANES])], axis=1)
              for h in range(MLA_HEADS)]
        return cqn, qf

    n_rows = cq.shape[0]
    return _rowwise("mla_q_prep", fn, [cq, cos, sin], [_sds((n_rows, Q_LORA), BF16), _sds((MLA_HEADS, n_rows, QK_PAD), BF16)],
                    consts=[g, w_uq_p])


def _mla_kv_prep(ckv, kpe, g, w_ukv_p):
    def fn(ckv_v, kpe_v, g_v, w_v):
        ckvn = _rms(ckv_v, g_v).astype(BF16)
        kvv = _dot(ckvn, w_v, _NN)
        kf = [jnp.concatenate([kvv[:, h * QK_NOPE:(h + 1) * QK_NOPE], kpe_v.astype(F32)], axis=1) for h in range(MLA_HEADS)]
        return ckvn, kf, kvv[:, 512:]

    n_rows = ckv.shape[0]
    return _rowwise("mla_kv_prep", fn, [ckv, kpe],
                    [_sds((n_rows, KV_LORA), BF16), _sds((MLA_HEADS, n_rows, QK_PAD), BF16), _sds((n_rows, MLA_WIDTH), BF16)],
                    consts=[g, w_ukv_p])


MLA_FWD_TILES = (1024, 2048)
MLA_BWD_TILES = (1024, 1024)


def _mla_fwd(qf, kf, v, comm=None):
    n_rows = v.shape[0]
    tq, tk = min(MLA_FWD_TILES[0], n_rows), min(MLA_FWD_TILES[1], n_rows)
    nq, nk = n_rows // tq, n_rows // tk
    grid = (MLA_HEADS, nq, nk)
    exp2_scale = MLA_SCALE * math.log2(math.e)
    c_ins, c_in_specs, c_outs, c_out_specs, begin, end = _hosted(comm, grid)
    n_ci, n_co = len(c_ins), len(c_outs)

    def last_k(i):
        return jnp.right_shift(i * tq + tq - 1, int(math.log2(tk)))

    def body(q_ref, k_ref, v_ref, *refs):
        ci_refs, (o_ref, lse_ref), co_refs = refs[:n_ci], refs[n_ci:n_ci + 2], refs[n_ci + 2:n_ci + 2 + n_co]
        m_sc, l_sc, acc_sc = refs[n_ci + 2 + n_co:n_ci + 5 + n_co]
        sems = refs[n_ci + 5 + n_co:]
        i, j = pl.program_id(1), pl.program_id(2)
        begin(ci_refs, co_refs, sems)

        @pl.when(j == 0)
        def _():
            m_sc[...] = jnp.full(m_sc.shape, NEG, F32)
            l_sc[...] = jnp.zeros(l_sc.shape, F32)
            acc_sc[...] = jnp.zeros(acc_sc.shape, F32)

        def step(masked):
            s = _dot(q_ref[...], k_ref[...], _NT)
            if masked:
                row = lax.broadcasted_iota(jnp.int32, (tq, tk), 0) + i * tq
                col = lax.broadcasted_iota(jnp.int32, (tq, tk), 1) + j * tk
                s = jnp.where(col <= row, s, NEG)
            m_prev = m_sc[...]
            m_new = jnp.maximum(m_prev, jnp.max(s, axis=-1, keepdims=True))
            alpha = jnp.exp2((m_prev - m_new) * exp2_scale)
            p = jnp.exp2((s - m_new) * exp2_scale)
            l_sc[...] = alpha * l_sc[...] + jnp.sum(p, axis=-1, keepdims=True)
            acc_sc[...] = alpha * acc_sc[...] + _dot(p.astype(BF16), v_ref[...], _NN)
            m_sc[...] = m_new

        active = j * tk <= i * tq + tq - 1
        crosses = (j + 1) * tk - 1 > i * tq

        @pl.when(active & jnp.logical_not(crosses))
        def _():
            step(False)

        @pl.when(active & crosses)
        def _():
            step(True)

        @pl.when(j == last_k(i))
        def _():
            o_ref[...] = acc_sc[...] / l_sc[...]
            lse_ref[...] = jnp.broadcast_to(m_sc[...] * MLA_SCALE + jnp.log(l_sc[...]), (tq, V_DIM))

        end(ci_refs, co_refs, sems)

    return pl.pallas_call(
        body, name="mla_fwd", grid=grid,
        in_specs=[pl.BlockSpec((None, tq, QK_PAD), lambda h, i, j: (h, i, 0)),
                  pl.BlockSpec((None, tk, QK_PAD), lambda h, i, j: (h, jnp.minimum(j, last_k(i)), 0)),
                  pl.BlockSpec((tk, V_DIM), lambda h, i, j: (jnp.minimum(j, last_k(i)), h))] + c_in_specs,
        out_specs=[pl.BlockSpec((tq, V_DIM), lambda h, i, j: (i, h)), pl.BlockSpec((tq, V_DIM), lambda h, i, j: (i, h))]
        + c_out_specs,
        out_shape=[_sds((n_rows, MLA_WIDTH)), _sds((n_rows, MLA_WIDTH))] + c_outs,
        scratch_shapes=[pltpu.VMEM((tq, 1), F32), pltpu.VMEM((tq, 1), F32), pltpu.VMEM((tq, V_DIM), F32)]
        + (comm.sem_scratch() if comm else []),
        compiler_params=_params(("arbitrary",) * 3 if comm else ("parallel", "parallel", "arbitrary")),
    )(qf, kf, v, *c_ins)


def _mm_res_ln(name, a, w, xres, g, b):
    def epi(acc, xr, g_v, b_v):
        r = ALPHA * xr + acc
        y = _layer_norm(r, g_v, b_v)
        return y, y, r

    return _mm(name, a, w, (F32, BF16, F32), tm=1024, tn=D_MODEL, tk=1024, epilogue=epi, extras=[xres], vecs=[g, b])


def _mm_relu2(x_bf, w):
    def epi(acc):
        r = jnp.maximum(acc, 0.0)
        return (r * r,)

    return _mm("ff1", x_bf, w, (BF16,), tm=1024, tn=2048, tk=D_MODEL, epilogue=epi)[0]


def _loss_fn(y, t):
    def fn(y_v, t_v):
        d = y_v - t_v
        part = jnp.sum(jnp.sum(d * d, axis=1, keepdims=True), axis=0, keepdims=True)
        return d * (1.0 / D_MODEL), part

    dy, part = _rowwise("loss", fn, [y, t], [_sds(y.shape)], reds=[_sds((1, 1))])
    return part * (0.5 / D_MODEL), dy


def _ln_bwd(name, dy, r, g):
    def fn(dy_v, r_v, g_v):
        mu = jnp.mean(r_v, axis=-1, keepdims=True)
        xc = r_v - mu
        rstd = lax.rsqrt(jnp.mean(xc * xc, axis=-1, keepdims=True) + LN_EPS)
        xh = xc * rstd
        dxh = dy_v * g_v
        dr = rstd * (dxh - jnp.mean(dxh, axis=-1, keepdims=True) - xh * jnp.mean(dxh * xh, axis=-1, keepdims=True))
        return dr, dr, jnp.sum(dy_v * xh, axis=0, keepdims=True), jnp.sum(dy_v, axis=0, keepdims=True)

    return _rowwise(name, fn, [dy, r], [_sds(dy.shape), _sds(dy.shape, BF16)], consts=[g],
                    reds=[_sds((1, D_MODEL)), _sds((1, D_MODEL))])


def _head_sum_matrix():
    i = lax.broadcasted_iota(jnp.int32, (A_WIDTH, A_WIDTH), 0) // A_HEAD_DIM
    j = lax.broadcasted_iota(jnp.int32, (A_WIDTH, A_WIDTH), 1) // A_HEAD_DIM
    return (i == j).astype(BF16)


def _mix_bwd(dmixed, a_out, b_out, ls_, a_g, b_g):
    def fn(dm, a, b, l1, l2, l3, ag, bg, e_mat):
        da, dga = _rms_bwd(a, ag, dm[:, :A_WIDTH])
        db, dgb = _rms_bwd(b, bg, dm[:, A_WIDTH:])
        t = da * a
        t_hi = t.astype(BF16)
        t_lo = (t - t_hi.astype(F32)).astype(BF16)
        tsum = _dot(t_hi, e_mat, _NN) + _dot(t_lo, e_mat, _NN)
        mx = jnp.maximum(jnp.maximum(l1, l2), l3)
        e1, e2, e3 = jnp.exp(l1 - mx), jnp.exp(l2 - mx), jnp.exp(l3 - mx)
        inv = 1.0 / (e1 + e2 + e3)
        w1, w2, w3 = e1 * inv, e2 * inv, e3 * inv
        tb = db * b
        delta_b = jnp.concatenate(
            [jnp.broadcast_to(jnp.sum(tb[:, h * V_DIM:(h + 1) * V_DIM], axis=-1, keepdims=True), (tb.shape[0], V_DIM))
             for h in range(MLA_HEADS)], axis=1)
        return w1 * da, w2 * da, w3 * da, w1 * tsum, w2 * tsum, w3 * tsum, db, delta_b, dga, dgb

    n_rows = a_out.shape[0]
    wide = (n_rows, A_WIDTH)
    return _rowwise("mix_bwd", fn, [dmixed, a_out, b_out, *ls_],
                    [_sds(wide, BF16)] * 3 + [_sds(wide)] * 3 + [_sds(wide, BF16), _sds(wide)],
                    consts=[a_g, b_g, _head_sum_matrix()], reds=[_sds((1, A_WIDTH)), _sds((1, MLA_WIDTH))], ts=256)


def _dil_bwd(q, k, v, lse, do, dl, dil):
    n_rows = q.shape[0]
    nb = n_rows // dil // SPAN
    qv, kv_, vv = _dil_view(q, dil), _dil_view(k, dil), _dil_view(v, dil)
    lv, dov, dlv = _dil_view(lse, dil), _dil_view(do, dil), _dil_view(dl, dil)

    def body(q_ref, kp_ref, kc_ref, vp_ref, vc_ref, lse_ref, do_ref, dl_ref, dq_ref, dk_ref, dv_ref, ck, cv):
        m = pl.program_id(1)

        @pl.when(m == 0)
        def _():
            ck[...] = jnp.zeros(ck.shape, F32)
            cv[...] = jnp.zeros(cv.shape, F32)

        @pl.when(m < nb)
        def _():
            mask, even = _band_mask(m, 2), _even_lanes(SPAN)
            for p in range(A_HEADS // 2):
                sl = slice(p * LANES, (p + 1) * LANES)
                q2, do2 = q_ref[:, sl], do_ref[:, sl]
                zero = jnp.zeros_like(q2)
                qcat = jnp.concatenate([jnp.where(even, q2, zero), jnp.where(even, zero, q2)], axis=0)
                docat = jnp.concatenate([jnp.where(even, do2, zero), jnp.where(even, zero, do2)], axis=0)
                kcat = jnp.concatenate([kp_ref[:, sl], kc_ref[:, sl]], axis=0)
                vcat = jnp.concatenate([vp_ref[:, sl], vc_ref[:, sl]], axis=0)
                lse2, dl2 = lse_ref[:, sl], dl_ref[:, sl]
                lse_c = jnp.concatenate([lse2[:, :1], lse2[:, A_HEAD_DIM:A_HEAD_DIM + 1]], axis=0)
                dl_c = jnp.concatenate([dl2[:, :1], dl2[:, A_HEAD_DIM:A_HEAD_DIM + 1]], axis=0)
                pr = jnp.exp(jnp.where(mask, _dot(qcat, kcat, _NT), NEG) - lse_c)
                ds = (pr * (_dot(docat, vcat, _NT) - dl_c)).astype(BF16)
                dq = _dot(ds, kcat, _NN)
                dq_ref[:, sl] = jnp.where(even, dq[:SPAN], dq[SPAN:])
                dk2 = _dot(ds, qcat, _TN)
                dv2 = _dot(pr.astype(BF16), docat, _TN)
                dk_ref[:, sl] = ck[:, sl] + dk2[:SPAN]
                dv_ref[:, sl] = cv[:, sl] + dv2[:SPAN]
                ck[:, sl] = dk2[SPAN:]
                cv[:, sl] = dv2[SPAN:]

        @pl.when(m == nb)
        def _():
            dk_ref[...] = ck[...]
            dv_ref[...] = cv[...]

    last = nb - 1
    cur = pl.BlockSpec((SPAN, A_WIDTH), lambda r, m: (jnp.minimum(m, last), r))
    prev = pl.BlockSpec((SPAN, A_WIDTH), lambda r, m: (jnp.clip(m - 1, 0, last), r))
    dq, dk, dv = pl.pallas_call(
        body, name=f"dil_bwd_d{dil}", grid=(dil, nb + 1),
        in_specs=[cur, prev, cur, prev, cur, cur, cur, cur],
        out_specs=[cur, prev, prev],
        out_shape=[_sds(qv.shape)] * 3,
        scratch_shapes=[pltpu.VMEM((SPAN, A_WIDTH), F32), pltpu.VMEM((SPAN, A_WIDTH), F32)],
        compiler_params=_params(("parallel", "arbitrary")),
    )(qv, kv_, kv_, vv, vv, lv, dov, dlv)
    return dq.reshape(n_rows, A_WIDTH), dk.reshape(n_rows, A_WIDTH), dv.reshape(n_rows, A_WIDTH)


def _mla_bwd(qf, kf, v, lse, do, delta, comm=None):
    n_rows = v.shape[0]
    tq, tk = min(MLA_BWD_TILES[0], n_rows), min(MLA_BWD_TILES[1], n_rows)
    nq, nk = n_rows // tq, n_rows // tk

    def first_q(j):
        return jnp.right_shift(j * tk, int(math.log2(tq)))

    grid = (MLA_HEADS, nk, nq)
    c_ins, c_in_specs, c_outs, c_out_specs, begin, end = _hosted(comm, grid)
    n_ci, n_co = len(c_ins), len(c_outs)

    def body(q_ref, k_ref, v_ref, lse_ref, do_ref, dl_ref, *refs):
        ci_refs, (dq_ref, dk_ref, dv_ref), co_refs = refs[:n_ci], refs[n_ci:n_ci + 3], refs[n_ci + 3:n_ci + 3 + n_co]
        dk_sc, dv_sc = refs[n_ci + 3 + n_co:n_ci + 5 + n_co]
        sems = refs[n_ci + 5 + n_co:]
        j, i = pl.program_id(1), pl.program_id(2)
        begin(ci_refs, co_refs, sems)

        @pl.when((j == 0) & (i == 0))
        def _():
            dq_ref[...] = jnp.zeros(dq_ref.shape, F32)

        @pl.when(i == first_q(j))
        def _():
            dk_sc[...] = jnp.zeros(dk_sc.shape, F32)
            dv_sc[...] = jnp.zeros(dv_sc.shape, F32)

        def step(masked):
            q, k, dob = q_ref[...], k_ref[...], do_ref[...]
            s = _dot(q, k, _NT) * MLA_SCALE
            if masked:
                row = lax.broadcasted_iota(jnp.int32, (tq, tk), 0) + i * tq
                col = lax.broadcasted_iota(jnp.int32, (tq, tk), 1) + j * tk
                s = jnp.where(col <= row, s, NEG)
            p = jnp.exp(s - lse_ref[:, :1])
            dp = _dot(dob, v_ref[...], _NT)
            ds = (p * (dp - dl_ref[:, :1]) * MLA_SCALE).astype(BF16)
            dv_sc[...] += _dot(p.astype(BF16), dob, _TN)
            dk_sc[...] += _dot(ds, q, _TN)
            rows = pl.ds(pl.multiple_of(i * tq, tq), tq)
            dq_ref[rows, :] += _dot(ds, k, _NN)

        active = i >= first_q(j)
        crosses = (j + 1) * tk - 1 > i * tq

        @pl.when(active & jnp.logical_not(crosses))
        def _():
            step(False)

        @pl.when(active & crosses)
        def _():
            step(True)

        @pl.when(i == nq - 1)
        def _():
            dk_ref[...] = dk_sc[...]
            dv_ref[...] = dv_sc[...]

        end(ci_refs, co_refs, sems)

    qrow = lambda h, j, i: (jnp.maximum(i, first_q(j)), h)
    return pl.pallas_call(
        body, name="mla_bwd", grid=grid,
        in_specs=[pl.BlockSpec((None, tq, QK_PAD), lambda h, j, i: (h, jnp.maximum(i, first_q(j)), 0)),
                  pl.BlockSpec((None, tk, QK_PAD), lambda h, j, i: (h, j, 0)),
                  pl.BlockSpec((tk, V_DIM), lambda h, j, i: (j, h)),
                  pl.BlockSpec((tq, V_DIM), qrow), pl.BlockSpec((tq, V_DIM), qrow), pl.BlockSpec((tq, V_DIM), qrow)]
        + c_in_specs,
        out_specs=[pl.BlockSpec((None, n_rows, QK_PAD), lambda h, j, i: (h, 0, 0)),
                   pl.BlockSpec((None, tk, QK_PAD), lambda h, j, i: (h, j, 0)),
                   pl.BlockSpec((tk, V_DIM), lambda h, j, i: (j, h))] + c_out_specs,
        out_shape=[_sds((MLA_HEADS, n_rows, QK_PAD)), _sds((MLA_HEADS, n_rows, QK_PAD)), _sds((n_rows, MLA_WIDTH))] + c_outs,
        scratch_shapes=[pltpu.VMEM((tk, QK_PAD), F32), pltpu.VMEM((tk, V_DIM), F32)] + (comm.sem_scratch() if comm else []),
        compiler_params=_params(("arbitrary",) * 3 if comm else ("parallel", "arbitrary", "arbitrary")),
    )(qf, kf, v, lse, do, delta, *c_ins)


def _mla_bwd_prep(dqf, dkf, dv, cos, sin):
    def fn(dqf_v, dkf_v, dv_v, cos_v, sin_v):
        dq = jnp.concatenate([dqf_v[h][:, :QK_NOPE] for h in range(MLA_HEADS)]
                             + [_rope_t(dqf_v[h][:, QK_NOPE:], cos_v[:, :LANES], sin_v[:, :LANES]) for h in range(MLA_HEADS)], axis=1)
        dkv = jnp.concatenate([dkf_v[h][:, :QK_NOPE] for h in range(MLA_HEADS)] + [dv_v], axis=1)
        dkpe = dkf_v[0][:, QK_NOPE:] + dkf_v[1][:, QK_NOPE:] + dkf_v[2][:, QK_NOPE:] + dkf_v[3][:, QK_NOPE:]
        return dq, dkv, dkpe

    n_rows = dv.shape[0]
    return _rowwise("mla_bwd_prep", fn, [dqf, dkf, dv, cos, sin],
                    [_sds((n_rows, UQ_PAD), BF16), _sds((n_rows, 1024), BF16), _sds((n_rows, LANES))], ts=256)


def _assemble_dh(dqs, dks, dvs, dcqn, cq, dckvn, ckv, dkpe, cos, sin, gq, gkv):
    def fn(q1, q2, q3, k1, k2, k3, v1, v2, v3, dcqn_v, cq_v, dckvn_v, ckv_v, dkpe_v, cos_v, sin_v, gq_v, gkv_v):
        dqa = _rope_t(q1 + q2 + q3, cos_v, sin_v) * A_SCALE
        dka = _rope_t(k1 + k2 + k3, cos_v, sin_v)
        dcq, dgq = _rms_bwd(cq_v, gq_v, dcqn_v)
        dckv, dgkv = _rms_bwd(ckv_v, gkv_v, dckvn_v)
        dkr = _rope_t(dkpe_v, cos_v[:, :LANES], sin_v[:, :LANES])
        return jnp.concatenate([dqa, dka, v1 + v2 + v3, dcq, dkr, dckv], axis=1), dgq, dgkv

    n_rows = cq.shape[0]
    return _rowwise("assemble_dh", fn, [*dqs, *dks, *dvs, dcqn, cq, dckvn, ckv, dkpe, cos, sin],
                    [_sds((n_rows, IN_PAD), BF16)], consts=[gq, gkv],
                    reds=[_sds((1, Q_LORA)), _sds((1, KV_LORA))], ts=256)


def _layer_fwd(x_f32, x_bf, w, sm, cos, sin, behind=None):
    qa, ka, va, cq, kpe, ckv = _in_proj(x_bf, w["w_in"], cos, sin)
    os_, ls_ = [], []
    for _, dil in PATTERNS:
        o, lse = _dil_fwd(qa, ka, va, dil)
        os_.append(o)
        ls_.append(lse)
    cqn, qf = _mla_q_prep(cq, sm["q_a_norm"], w["w_uq"], cos, sin)
    ckvn, kf, v = _mla_kv_prep(ckv, kpe, sm["kv_a_norm"], w["w_ukv"])
    b_out, b_lse, *behind_outs = _mla_fwd(qf, kf, v, comm=behind)
    a_out, mixed = _mix_fwd(os_, ls_, b_out, sm["a_out_norm"], sm["b_out_norm"])
    x1, x1_bf, r1 = _mm_res_ln("wo_ln1", mixed, w["w_o"], x_f32, sm["ln1_g"], sm["ln1_b"])
    f = _mm_relu2(x1_bf, w["w_ff1"])
    x2, x2_bf, r2 = _mm_res_ln("ff2_ln2", f, w["w_ff2"], x1, sm["ln2_g"], sm["ln2_b"])
    saved = dict(x_bf=x_bf, qa=qa, ka=ka, va=va, cq=cq, ckv=ckv, ls=ls_, cqn=cqn, qf=qf, ckvn=ckvn, kf=kf, v=v,
                 b_out=b_out, b_lse=b_lse, a_out=a_out, mixed=mixed, x1_bf=x1_bf, r1=r1, f=f, r2=r2)
    return x2, x2_bf, saved, behind_outs


def _layer_bwd(dx2, w, sm, sv, cos, sin, pending=None):
    dr2, dr2_bf, dg2, db2 = _ln_bwd("ln2_bwd", dx2, sv["r2"], sm["ln2_g"])
    (dw_ff2,) = _mm("dw_ff2", sv["f"], dr2_bf, (F32,), ta=True, tm=1024, tn=1024, tk=2048)
    du, *swapped = _mm("d_u", dr2_bf, w["w_ff2"], (BF16,), tb=True, tm=1024, tn=2048, tk=D_MODEL,
                       epilogue=lambda acc, f: (acc * (2.0 * jnp.sqrt(f.astype(F32))),), extras=[sv["f"]],
                       comm=pending.swap_comm() if pending else None)
    if pending:
        pending.after_swap(swapped)
    (dw_ff1,) = _mm("dw_ff1", sv["x1_bf"], du, (F32,), ta=True, tm=1024, tn=1024, tk=2048)
    (dy1,) = _mm("d_x1", du, w["w_ff1"], (F32,), tb=True, tm=1024, tn=1024, tk=2048,
                 epilogue=lambda acc, d: (acc + ALPHA * d,), extras=[dr2])
    dr1, dr1_bf, dg1, db1 = _ln_bwd("ln1_bwd", dy1, sv["r1"], sm["ln1_g"])
    (dw_o,) = _mm("dw_o", sv["mixed"], dr1_bf, (F32,), ta=True, tm=1024, tn=1024, tk=2048)
    (dmixed,) = _mm("d_mixed", dr1_bf, w["w_o"], (F32,), tb=True, tm=1024, tn=1024, tk=D_MODEL)
    do1, do2, do3, dl1, dl2, dl3, do_b, delta_b, dga, dgb = _mix_bwd(
        dmixed, sv["a_out"], sv["b_out"], sv["ls"], sm["a_out_norm"], sm["b_out_norm"])
    dqs, dks, dvs = [], [], []
    for (_, dil), lse, do, dl in zip(PATTERNS, sv["ls"], (do1, do2, do3), (dl1, dl2, dl3)):
        dq, dk, dv = _dil_bwd(sv["qa"], sv["ka"], sv["va"], lse, do, dl, dil)
        dqs.append(dq)
        dks.append(dk)
        dvs.append(dv)
    dqf, dkf, dv_b, *scattered = _mla_bwd(sv["qf"], sv["kf"], sv["v"], sv["b_lse"], do_b, delta_b,
                                          comm=pending.scatter_comm() if pending else None)
    if pending:
        pending.after_scatter(scattered)
    dq_b, dkv_b, dkpe = _mla_bwd_prep(dqf, dkf, dv_b, cos, sin)
    (dw_uq,) = _mm("dw_uq", sv["cqn"], dq_b, (F32,), ta=True, tm=Q_LORA, tn=1024, tk=2048)
    (dcqn,) = _mm("d_cqn", dq_b, w["w_uq"], (F32,), tb=True, tm=1024, tn=Q_LORA, tk=UQ_PAD)
    (dw_ukv,) = _mm("dw_ukv", sv["ckvn"], dkv_b, (F32,), ta=True, tm=KV_LORA, tn=1024, tk=2048)
    (dckvn,) = _mm("d_ckvn", dkv_b, w["w_ukv"], (F32,), tb=True, tm=1024, tn=KV_LORA, tk=1024)
    dh, dgq, dgkv = _assemble_dh(dqs, dks, dvs, dcqn, sv["cq"], dckvn, sv["ckv"], dkpe, cos, sin,
                                 sm["q_a_norm"], sm["kv_a_norm"])
    (dw_in,) = _mm("dw_in", sv["x_bf"], dh, (F32,), ta=True, tm=1024, tn=1024, tk=2048)
    (dx,) = _mm("d_x", dh, w["w_in"], (F32,), tb=True, tm=1024, tn=1024, tk=2048,
                epilogue=lambda acc, d: (acc + ALPHA * d,), extras=[dr1])
    dws = dict(w_in=dw_in, w_uq=dw_uq, w_ukv=dw_ukv, w_o=dw_o, w_ff1=dw_ff1, w_ff2=dw_ff2)
    dsm = dict(q_a_norm=dgq, kv_a_norm=dgkv, a_out_norm=dga, b_out_norm=dgb, ln1_g=dg1, ln1_b=db1, ln2_g=dg2, ln2_b=db2)
    return dx, dws, dsm


def _pad_w_in(w):
    return jnp.concatenate([w[:, :1792], w[:, 1920:1984], jnp.zeros((w.shape[0], 64), w.dtype), w[:, 1792:1920]], axis=1)


def _unpad_w_in(w):
    return jnp.concatenate([w[:, :1792], w[:, 1920:2048], w[:, 1792:1856]], axis=1)


def _rope_tables(n_rows):
    half = A_HEAD_DIM // 2
    inv_freq = ROPE_THETA ** (-jnp.arange(half, dtype=F32) / half)
    ang = jnp.arange(n_rows, dtype=F32)[:, None] * inv_freq[None, :]
    cos = jnp.tile(jnp.cos(ang), (1, 2 * A_HEADS))
    sin = jnp.tile(jnp.concatenate([-jnp.sin(ang), jnp.sin(ang)], axis=1), (1, A_HEADS))
    return cos, sin


W_NAMES = ("w_in", "w_uq", "w_ukv", "w_o", "w_ff1", "w_ff2")
PIECE = dict(w_in=(512, 512), w_uq=(128, 256), w_ukv=(64, 256), w_o=(128, 1024), w_ff1=(512, 1024), w_ff2=(512, 1024))
SHARD_COLS = dict(w_in=496, w_uq=192, w_ukv=256, w_o=1024, w_ff1=1024, w_ff2=1024)


def _col_pieces(t, pad_to):
    rows, _, cols = t.shape
    t = jnp.pad(t, ((0, 0), (0, 0), (0, pad_to - cols)))
    return t.reshape(2, rows // 2, N_CHIPS, pad_to).transpose(2, 0, 1, 3)


def _grad_pieces(dws):
    uq, ukv = dws["w_uq"], dws["w_ukv"]
    uq = jnp.concatenate([uq[:, :512].reshape(Q_LORA, MLA_HEADS, LANES), uq[:, 512:].reshape(Q_LORA, MLA_HEADS, LANES)], axis=2)
    ukv = jnp.concatenate([ukv[:, :512].reshape(KV_LORA, MLA_HEADS, LANES), ukv[:, 512:].reshape(KV_LORA, MLA_HEADS, LANES)], axis=2)
    return dict(
        w_in=_col_pieces(_unpad_w_in(dws["w_in"]).reshape(D_MODEL, N_CHIPS, SHARD_COLS["w_in"]), 512),
        w_uq=_col_pieces(uq, 256), w_ukv=_col_pieces(ukv, 256),
        w_o=dws["w_o"].reshape(N_CHIPS, 2, 128, D_MODEL),
        w_ff1=_col_pieces(dws["w_ff1"].reshape(D_MODEL, N_CHIPS, 1024), 1024),
        w_ff2=dws["w_ff2"].reshape(N_CHIPS, 2, 512, D_MODEL))


def _weights_from_pieces(p):
    def cols(t):
        return t.transpose(1, 2, 0, 3).reshape(2 * t.shape[2], N_CHIPS, t.shape[3])

    uq, ukv = cols(p["w_uq"]), cols(p["w_ukv"])
    return dict(
        w_in=_pad_w_in(cols(p["w_in"])[:, :, :SHARD_COLS["w_in"]].reshape(D_MODEL, IN_COLS)),
        w_uq=jnp.concatenate([uq[:, :, :LANES].reshape(Q_LORA, 512), uq[:, :, LANES:].reshape(Q_LORA, 512)], axis=1),
        w_ukv=jnp.concatenate([ukv[:, :, :LANES].reshape(KV_LORA, 512), ukv[:, :, LANES:].reshape(KV_LORA, 512)], axis=1),
        w_o=p["w_o"].reshape(D_MODEL, D_MODEL),
        w_ff1=cols(p["w_ff1"]).reshape(D_MODEL, D_FF),
        w_ff2=p["w_ff2"].reshape(D_FF, D_MODEL))


def _my_piece(name, shard, half):
    rows, cols = shard.shape[0] // 2, shard.shape[1]
    t = lax.dynamic_slice_in_dim(shard, half * rows, rows, axis=0).astype(BF16)
    return jnp.pad(t, ((0, 0), (0, PIECE[name][1] - cols)))


MESH = pl.DeviceIdType.MESH
ANY = pl.BlockSpec(memory_space=pl.ANY)


class _Comm:
    def __init__(self, ins, out_shapes, n_sems, start, finish):
        self.ins, self.out_shapes, self.n_sems, self.start, self.finish = list(ins), list(out_shapes), n_sems, start, finish

    def sem_scratch(self):
        return [pltpu.SemaphoreType.DMA((self.n_sems,)), pltpu.SemaphoreType.DMA((self.n_sems,))]


def _comm_call(name, comm):
    n_in, n_out = len(comm.ins), len(comm.out_shapes)

    def body(*refs):
        ins, outs, sems = refs[:n_in], refs[n_in:n_in + n_out], refs[n_in + n_out:]
        comm.start(ins, outs, *sems)
        comm.finish(ins, outs, *sems)

    return pl.pallas_call(body, name=name, out_shape=comm.out_shapes, in_specs=[ANY] * n_in, out_specs=[ANY] * n_out,
                          scratch_shapes=comm.sem_scratch())(*comm.ins)


def _all_gather_comm(blocks):
    n = len(blocks)

    def plan(x_refs, out_refs, send_sems, recv_sems):
        x, y, c = lax.axis_index("x"), lax.axis_index("y"), lax.axis_index("c")
        me, sibling = (x, y, c), (x, y, 1 - c)
        chips = [(1 - x, y), (x, 1 - y), (1 - x, 1 - y)]

        def copy(t, k, blk, to, src=None):
            px, py, pc = blk
            slot = out_refs[t].at[4 * px + 2 * py + pc]
            return pltpu.make_async_remote_copy(
                src_ref=slot if src is None else src, dst_ref=slot,
                send_sem=send_sems.at[7 * t + k], recv_sem=recv_sems.at[7 * t + k], device_id=to, device_id_type=MESH)

        first = []
        for t in range(n):
            first.append(copy(t, 0, me, sibling, src=x_refs[t]))
            first += [copy(t, 1 + j, me, (*chip, c), src=x_refs[t]) for j, chip in enumerate(chips)]
        return me, sibling, chips, c, copy, first

    def start(x_refs, out_refs, send_sems, recv_sems):
        for cp in plan(x_refs, out_refs, send_sems, recv_sems)[-1]:
            cp.start()

    def finish(x_refs, out_refs, send_sems, recv_sems):
        me, sibling, chips, c, copy, sent = plan(x_refs, out_refs, send_sems, recv_sems)
        for j, chip in enumerate(chips):
            for t in range(n):
                copy(t, 1 + j, (*chip, c), me).wait_recv()
                sent.append(copy(t, 4 + j, (*chip, c), sibling))
                sent[-1].start()
        for t in range(n):
            copy(t, 0, sibling, me).wait_recv()
        for j, chip in enumerate(chips):
            for t in range(n):
                copy(t, 4 + j, (*chip, 1 - c), me).wait_recv()
        for cp in sent:
            cp.wait_send()

    return _Comm(blocks, [_sds((N_DEV,) + b.shape, b.dtype) for b in blocks], 7 * n, start, finish)


def _simple_comm(ins, out_shapes, n_sems, copies):
    def start(in_refs, out_refs, send_sems, recv_sems):
        for cp in copies(in_refs, out_refs, send_sems, recv_sems):
            cp.start()

    def finish(in_refs, out_refs, send_sems, recv_sems):
        for cp in copies(in_refs, out_refs, send_sems, recv_sems):
            cp.wait()

    return _Comm(ins, out_shapes, n_sems, start, finish)


def _swap_comm(gs):
    def copies(g_refs, got_refs, send_sems, recv_sems):
        c = lax.axis_index("c")
        sibling = (lax.axis_index("x"), lax.axis_index("y"), 1 - c)
        return [pltpu.make_async_remote_copy(src_ref=g_refs[t].at[k, 1 - c], dst_ref=got_refs[t].at[k],
                                             send_sem=send_sems.at[N_CHIPS * t + k], recv_sem=recv_sems.at[N_CHIPS * t + k],
                                             device_id=sibling, device_id_type=MESH)
                for t in range(len(gs)) for k in range(N_CHIPS)]

    return _simple_comm(gs, [_sds((N_CHIPS,) + g.shape[2:], g.dtype) for g in gs], N_CHIPS * len(gs), copies)


def _scatter_comm(hs):
    def copies(h_refs, got_refs, send_sems, recv_sems):
        x, y, c = lax.axis_index("x"), lax.axis_index("y"), lax.axis_index("c")
        chips = [(1 - x, y), (x, 1 - y), (1 - x, 1 - y)]
        return [pltpu.make_async_remote_copy(src_ref=h_refs[t].at[2 * px + py], dst_ref=got_refs[t].at[rel],
                                             send_sem=send_sems.at[3 * t + rel], recv_sem=recv_sems.at[3 * t + rel],
                                             device_id=(px, py, c), device_id_type=MESH)
                for t in range(len(hs)) for rel, (px, py) in enumerate(chips)]

    return _simple_comm(hs, [_sds((3,) + h.shape[1:], h.dtype) for h in hs], 3 * len(hs), copies)


def _share_comm(rs):
    def copies(r_refs, got_refs, send_sems, recv_sems):
        sibling = (lax.axis_index("x"), lax.axis_index("y"), 1 - lax.axis_index("c"))
        return [pltpu.make_async_remote_copy(src_ref=r_refs[t], dst_ref=got_refs[t], send_sem=send_sems.at[t],
                                             recv_sem=recv_sems.at[t], device_id=sibling, device_id_type=MESH)
                for t in range(len(rs))]

    return _simple_comm(rs, [_sds(r.shape, r.dtype) for r in rs], len(rs), copies)


SUM_ROWS = 256


def _pair_sum(name, g, got, c_arr):
    _, _, pr, pc = g.shape
    tr = min(pr, SUM_ROWS)

    def body(c_ref, g_ref, got_ref, o32_ref, o16_ref):
        s = g_ref[...] + got_ref[...]
        o32_ref[...] = s
        o16_ref[...] = s.astype(BF16)

    part = pl.BlockSpec((None, tr, pc), lambda k, i, c_ref: (k, i, 0))
    return pl.pallas_call(
        body, name=name, out_shape=[_sds((N_CHIPS, pr, pc)), _sds((N_CHIPS, pr, pc), BF16)],
        grid_spec=pltpu.PrefetchScalarGridSpec(
            num_scalar_prefetch=1, grid=(N_CHIPS, pr // tr),
            in_specs=[pl.BlockSpec((None, None, tr, pc), lambda k, i, c_ref: (k, c_ref[0], i, 0)), part],
            out_specs=[part, part]),
        compiler_params=_params(("parallel", "parallel")),
    )(c_arr, g, got)


def _chips_sum(name, h32, got, chip_arr):
    _, pr, pc = h32.shape
    tr = min(pr, SUM_ROWS)

    def body(chip_ref, h_ref, got_ref, o_ref):
        o_ref[...] = h_ref[...] + got_ref[0] + got_ref[1] + got_ref[2]

    return pl.pallas_call(
        body, name=name, out_shape=_sds((pr, pc)),
        grid_spec=pltpu.PrefetchScalarGridSpec(
            num_scalar_prefetch=1, grid=(pr // tr,),
            in_specs=[pl.BlockSpec((None, tr, pc), lambda i, chip_ref: (chip_ref[0], i, 0)),
                      pl.BlockSpec((3, tr, pc), lambda i, chip_ref: (0, i, 0))],
            out_specs=pl.BlockSpec((tr, pc), lambda i, chip_ref: (i, 0))),
        compiler_params=_params(("parallel",)),
    )(chip_arr, h32, got)


class _LayerReduce:
    def __init__(self, tag, dws, c, chip):
        self.tag, self.c, self.gp = tag, c, _grad_pieces(dws)
        self.c_arr, self.chip_arr = jnp.reshape(c, (1,)).astype(jnp.int32), jnp.reshape(chip, (1,)).astype(jnp.int32)
        self.shards = None

    def swap_comm(self):
        return _swap_comm([self.gp[n] for n in W_NAMES])

    def after_swap(self, gots):
        self.sums = [_pair_sum(f"rs_pair_sum_{n}_{self.tag}", self.gp[n], got, self.c_arr) for n, got in zip(W_NAMES, gots)]

    def scatter_comm(self):
        return _scatter_comm([s16 for _, s16 in self.sums])

    def after_scatter(self, parts):
        mine = [_chips_sum(f"rs_chips_sum_{n}_{self.tag}", s32, part, self.chip_arr)
                for n, (s32, _), part in zip(W_NAMES, self.sums, parts)]
        theirs = _comm_call(f"rs_share_{self.tag}", _share_comm(mine))
        self.shards = {}
        for n, a, b in zip(W_NAMES, mine, theirs):
            both = jnp.where(self.c == 0, jnp.concatenate([a, b], axis=0), jnp.concatenate([b, a], axis=0))
            self.shards[n] = both[:, :SHARD_COLS[n]]

    def run_alone(self):
        self.after_swap(_comm_call(f"rs_pair_{self.tag}", self.swap_comm()))
        self.after_scatter(_comm_call(f"rs_chips_{self.tag}", self.scatter_comm()))


class _LayerWeights:
    def __init__(self, shards, c, dev):
        self.shards, self.c, self.dev = shards, c, dev

    def comm(self, l):
        self.blocks = [_my_piece(n, self.shards[l][n], self.c) for n in W_NAMES]
        return _all_gather_comm(self.blocks)

    def take(self, l, gathered):
        pieces = {n: lax.dynamic_update_index_in_dim(g, b, self.dev, 0).reshape((N_CHIPS, 2) + b.shape)
                  for n, g, b in zip(W_NAMES, gathered, self.blocks)}
        return _weights_from_pieces(pieces)

    def first(self):
        return self.take(0, _comm_call("ag_l0", self.comm(0)))


def _all_reduce_small(vec):
    rows, lanes = vec.shape

    def gather_body(x_ref, out_ref, send_sems, recv_sems):
        x, y, c = lax.axis_index("x"), lax.axis_index("y"), lax.axis_index("c")
        me, sibling = (x, y, c), (x, y, 1 - c)
        chips = [(1 - x, y), (x, 1 - y), (1 - x, 1 - y)]

        def slot(px, py, pc):
            return out_ref.at[4 * px + 2 * py + pc]

        def copy(k, blk, to, src=None):
            return pltpu.make_async_remote_copy(
                src_ref=slot(*blk) if src is None else src, dst_ref=slot(*blk),
                send_sem=send_sems.at[k], recv_sem=recv_sems.at[k], device_id=to, device_id_type=MESH)

        out_ref[4 * x + 2 * y + c] = x_ref[...]
        first = [copy(0, me, sibling, src=x_ref)]
        first += [copy(1 + j, me, (*chip, c), src=x_ref) for j, chip in enumerate(chips)]
        for cp in first:
            cp.start()
        passed = [copy(4 + j, (*chip, c), sibling) for j, chip in enumerate(chips)]
        for j, chip in enumerate(chips):
            copy(1 + j, (*chip, c), me).wait_recv()
            passed[j].start()
        copy(0, sibling, me).wait_recv()
        for j, chip in enumerate(chips):
            copy(4 + j, (*chip, 1 - c), me).wait_recv()
        for cp in first + passed:
            cp.wait_send()

    vmem = pl.BlockSpec(memory_space=pltpu.VMEM)
    allv = pl.pallas_call(
        gather_body, name="small_all_gather", out_shape=_sds((N_DEV, rows, lanes)),
        in_specs=[vmem], out_specs=vmem,
        scratch_shapes=[pltpu.SemaphoreType.DMA((7,)), pltpu.SemaphoreType.DMA((7,))],
    )(vec)

    def sum_body(a_ref, o_ref):
        acc = a_ref[0]
        for d in range(1, N_DEV):
            acc = acc + a_ref[d]
        o_ref[...] = acc

    return pl.pallas_call(sum_body, name="small_sum", out_shape=_sds((rows, lanes)), in_specs=[vmem], out_specs=vmem)(allv)


def _adamw(name, w, g, m, v, ts=512):
    def fn(w_v, g_v, m_v, v_v):
        m_n = ADAM_B1 * m_v + (1.0 - ADAM_B1) * g_v
        v_n = ADAM_B2 * v_v + (1.0 - ADAM_B2) * (g_v * g_v)
        m_hat = m_n / (1.0 - ADAM_B1 ** ADAM_STEP)
        v_hat = v_n / (1.0 - ADAM_B2 ** ADAM_STEP)
        delta = -ADAM_LR * (m_hat / (jnp.sqrt(v_hat) + ADAM_EPS) + ADAM_WD * w_v)
        return delta, m_n, v_n

    return _rowwise(name, fn, [w, g, m, v], [_sds(w.shape)] * 3, ts=ts)


def _pack_small(per_layer):
    flat = jnp.concatenate([per_layer[l][n].reshape(-1) for l in range(DEPTH) for n in SMALL_NAMES])
    return jnp.pad(flat, (0, SMALL_ROWS * LANES - flat.shape[0])).reshape(SMALL_ROWS, LANES)


def _unpack_small(packed):
    flat = packed.reshape(-1)
    per = sum(SMALL_SIZES)
    out = {}
    for n, size, off in zip(SMALL_NAMES, SMALL_SIZES, [sum(SMALL_SIZES[:i]) for i in range(len(SMALL_SIZES))]):
        out[n] = jnp.stack([flat[l * per + off:l * per + off + size] for l in range(DEPTH)])
    return out


def _fwd_bwd(x, target, layer_weights, smalls, on_layer_grads):
    depth = len(smalls)
    cos, sin = _rope_tables(x.shape[0])
    h_f32, h_bf = x, x.astype(BF16)
    saved, weights = [], [layer_weights.first()]
    for l in range(depth):
        behind = layer_weights.comm(l + 1) if l + 1 < depth else None
        h_f32, h_bf, sv, brought = _layer_fwd(h_f32, h_bf, weights[l], smalls[l], cos, sin, behind)
        saved.append(sv)
        if l + 1 < depth:
            weights.append(layer_weights.take(l + 1, brought))
    loss_part, dh = _loss_fn(h_f32, target)
    small_grads, pending = [None] * depth, None
    for l in reversed(range(depth)):
        dh, dws, small_grads[l] = _layer_bwd(dh, weights[l], smalls[l], saved[l], cos, sin, pending)
        pending = on_layer_grads(l, dws)
    return loss_part, dh, small_grads, pending


def kernel(x, w_in, q_a_norm, kv_a_norm, w_uq, w_ukv, a_out_norm, b_out_norm, w_o, ln1_g, ln1_b, w_ff1, w_ff2, ln2_g, ln2_b, loss_target, m_w_in, m_q_a_norm, m_kv_a_norm, m_w_uq, m_w_ukv, m_a_out_norm, m_b_out_norm, m_w_o, m_ln1_g, m_ln1_b, m_w_ff1, m_w_ff2, m_ln2_g, m_ln2_b, v_w_in, v_q_a_norm, v_kv_a_norm, v_w_uq, v_w_ukv, v_a_out_norm, v_b_out_norm, v_w_o, v_ln1_g, v_ln1_b, v_w_ff1, v_w_ff2, v_ln2_g, v_ln2_b):
    c = lax.axis_index("c")
    chip = 2 * lax.axis_index("x") + lax.axis_index("y")
    big = dict(w_in=w_in, w_uq=w_uq, w_ukv=w_ukv, w_o=w_o, w_ff1=w_ff1, w_ff2=w_ff2)
    big_m = dict(w_in=m_w_in, w_uq=m_w_uq, w_ukv=m_w_ukv, w_o=m_w_o, w_ff1=m_w_ff1, w_ff2=m_w_ff2)
    big_v = dict(w_in=v_w_in, w_uq=v_w_uq, w_ukv=v_w_ukv, w_o=v_w_o, w_ff1=v_w_ff1, w_ff2=v_w_ff2)
    small = dict(q_a_norm=q_a_norm, kv_a_norm=kv_a_norm, a_out_norm=a_out_norm, b_out_norm=b_out_norm,
                 ln1_g=ln1_g, ln1_b=ln1_b, ln2_g=ln2_g, ln2_b=ln2_b)
    small_m = dict(q_a_norm=m_q_a_norm, kv_a_norm=m_kv_a_norm, a_out_norm=m_a_out_norm, b_out_norm=m_b_out_norm,
                   ln1_g=m_ln1_g, ln1_b=m_ln1_b, ln2_g=m_ln2_g, ln2_b=m_ln2_b)
    small_v = dict(q_a_norm=v_q_a_norm, kv_a_norm=v_kv_a_norm, a_out_norm=v_a_out_norm, b_out_norm=v_b_out_norm,
                   ln1_g=v_ln1_g, ln1_b=v_ln1_b, ln2_g=v_ln2_g, ln2_b=v_ln2_b)

    layer_weights = _LayerWeights([{n: big[n][l] for n in W_NAMES} for l in range(DEPTH)], c, 2 * chip + c)
    smalls = [{n: small[n][l][None, :] for n in SMALL_NAMES} for l in range(DEPTH)]
    reductions = [None] * DEPTH

    def reduce_layer(l, dws):
        reductions[l] = _LayerReduce(f"l{l}", dws, c, chip)
        return reductions[l]

    loss_part, grad_x, small_grads, last = _fwd_bwd(x[0], loss_target[0], layer_weights, smalls, reduce_layer)
    last.run_alone()
    shard_grads = [r.shards for r in reductions]
    loss = lax.psum(loss_part[0, 0], ("x", "y", "c"))
    grad_x = grad_x[None]

    g_small_packed = _all_reduce_small(_pack_small(small_grads))
    pack_in = lambda d: _pack_small([{n: d[n][l] for n in SMALL_NAMES} for l in range(DEPTH)])
    sd, sm_, sv_ = _adamw("adamw_small", pack_in(small), g_small_packed, pack_in(small_m), pack_in(small_v), ts=SMALL_ROWS)
    g_small, d_small, m_small, v_small = (_unpack_small(t) for t in (g_small_packed, sd, sm_, sv_))

    g_big, d_big, m_big, v_big = {}, {}, {}, {}
    for n in W_NAMES:
        g = jnp.stack([shard_grads[l][n] for l in range(DEPTH)])
        shape = g.shape
        flat = lambda t: t.reshape(shape[0] * shape[1], shape[2])
        d_, m_, v_ = _adamw(f"adamw_{n}", flat(big[n]), flat(g), flat(big_m[n]), flat(big_v[n]))
        g_big[n], d_big[n], m_big[n], v_big[n] = g, d_.reshape(shape), m_.reshape(shape), v_.reshape(shape)

    order = ("w_in", "q_a_norm", "kv_a_norm", "w_uq", "w_ukv", "a_out_norm", "b_out_norm", "w_o", "ln1_g", "ln1_b",
             "w_ff1", "w_ff2", "ln2_g", "ln2_b")
    pick = lambda bigd, smalld: [bigd[n] if n in bigd else smalld[n] for n in order]
    return (loss, grad_x, *pick(g_big, g_small), *pick(d_big, d_small), *pick(m_big, m_small), *pick(v_big, v_small))
```

```python
import math

import jax
import jax.numpy as jnp
from jax import lax
from jax.experimental import pallas as pl
from jax.experimental.pallas import tpu as pltpu

F32, BF16 = jnp.float32, jnp.bfloat16

D_MODEL = 1024
DEPTH = 4
A_HEAD_DIM = 64
A_HEADS = 8
A_WIDTH = 512
PATTERNS = ((128, 1), (512, 4), (2048, 16))
SPAN = 128
MLA_HEADS = 4
QK_NOPE = 128
QK_ROPE = 64
V_DIM = 128
Q_LORA = 256
KV_LORA = 128
MLA_WIDTH = 512
QK_PAD = 256
IN_COLS = 1984
IN_PAD = 2048
UQ_PAD = 1024
D_FF = 4096
ROPE_THETA = 10000.0
ALPHA = (2.0 * DEPTH) ** 0.25
LN_EPS = 1e-5
RMS_EPS = 1e-6
MLA_SCALE = (QK_NOPE + QK_ROPE) ** -0.5
A_SCALE = A_HEAD_DIM ** -0.5

ADAM_LR, ADAM_B1, ADAM_B2, ADAM_EPS, ADAM_WD, ADAM_STEP = 0.001, 0.9, 0.999, 1e-08, 0.01, 10

VMEM_LIMIT_BYTES = 56 * 1024 * 1024
NEG = -1e30
N_CHIPS, N_DEV = 4, 8
LANES = 128

SMALL_NAMES = ("q_a_norm", "kv_a_norm", "a_out_norm", "b_out_norm", "ln1_g", "ln1_b", "ln2_g", "ln2_b")
SMALL_SIZES = (256, 128, 512, 512, 1024, 1024, 1024, 1024)
SMALL_ROWS = 176


def _params(sem):
    return pltpu.CompilerParams(dimension_semantics=sem, vmem_limit_bytes=VMEM_LIMIT_BYTES)


def _rowwise(name, fn, ins, outs, consts=(), reds=(), ts=512):
    n_rows = ins[0].shape[-2]
    ts = min(ts, n_rows)
    assert n_rows % ts == 0
    n_in, n_c, n_o, n_r = len(ins), len(consts), len(outs), len(reds)

    def tile_spec(shape):
        if len(shape) == 2:
            return pl.BlockSpec((ts, shape[1]), lambda i: (i, 0))
        return pl.BlockSpec((shape[0], ts, shape[2]), lambda i: (0, i, 0))

    def whole_spec(shape):
        return pl.BlockSpec(shape, lambda i: (0,) * len(shape))

    def body(*refs):
        in_refs, c_refs = refs[:n_in], refs[n_in:n_in + n_c]
        o_refs, r_refs = refs[n_in + n_c:n_in + n_c + n_o], refs[n_in + n_c + n_o:]
        res = fn(*[r[...] for r in in_refs], *[r[...] for r in c_refs])
        res = tuple(res) if isinstance(res, (tuple, list)) else (res,)
        for r, v in zip(o_refs, res[:n_o]):
            if len(r.shape) == 3:
                for g in range(r.shape[0]):
                    r[g] = v[g].astype(r.dtype)
            else:
                r[...] = v.astype(r.dtype)
        if n_r:
            i = pl.program_id(0)

            @pl.when(i == 0)
            def _():
                for r, v in zip(r_refs, res[n_o:]):
                    r[...] = v

            @pl.when(i > 0)
            def _():
                for r, v in zip(r_refs, res[n_o:]):
                    r[...] += v

    out = pl.pallas_call(
        body, name=name, grid=(n_rows // ts,),
        in_specs=[tile_spec(a.shape) for a in ins] + [whole_spec(c.shape) for c in consts],
        out_specs=[tile_spec(o.shape) for o in outs] + [whole_spec(r.shape) for r in reds],
        out_shape=list(outs) + list(reds),
        compiler_params=_params(("arbitrary",)),
    )(*ins, *consts)
    return out


def _sds(shape, dtype=F32):
    return jax.ShapeDtypeStruct(tuple(shape), dtype)


def _hosted(comm, grid):
    if comm is None:
        return [], [], [], [], (lambda *a: None), (lambda *a: None)

    def edge(which, at, ins, outs, sems):
        ids = [pl.program_id(d) for d in range(len(grid))]
        hit = ids[0] == at[0]
        for i, v in zip(ids[1:], at[1:]):
            hit = hit & (i == v)

        @pl.when(hit)
        def _():
            which(ins, outs, *sems)

    begin = lambda ins, outs, sems: edge(comm.start, [0] * len(grid), ins, outs, sems)
    end = lambda ins, outs, sems: edge(comm.finish, [g - 1 for g in grid], ins, outs, sems)
    return comm.ins, [ANY] * len(comm.ins), comm.out_shapes, [ANY] * len(comm.out_shapes), begin, end


def _mm(name, a, b, out_dtypes, *, ta=False, tb=False, tm=512, tn=512, tk=512, epilogue=None, extras=(), vecs=(),
        comm=None):
    (k_dim, m_dim) = a.shape if ta else a.shape[::-1]
    (n_dim, k2) = b.shape if tb else b.shape[::-1]
    assert k_dim == k2
    tm, tn, tk = min(tm, m_dim), min(tn, n_dim), min(tk, k_dim)
    assert m_dim % tm == 0 and n_dim % tn == 0 and k_dim % tk == 0
    nk = k_dim // tk
    grid = (m_dim // tm, n_dim // tn, nk)
    a_spec = pl.BlockSpec((tk, tm), lambda i, j, k: (k, i)) if ta else pl.BlockSpec((tm, tk), lambda i, j, k: (i, k))
    b_spec = pl.BlockSpec((tn, tk), lambda i, j, k: (j, k)) if tb else pl.BlockSpec((tk, tn), lambda i, j, k: (k, j))
    dims = (((0 if ta else 1,), (1 if tb else 0,)), ((), ()))
    n_e, n_v, n_o = len(extras), len(vecs), len(out_dtypes)
    c_ins, c_in_specs, c_outs, c_out_specs, begin, end = _hosted(comm, grid)
    n_ci, n_co = len(c_ins), len(c_outs)

    def body(a_ref, b_ref, *refs):
        e_refs, v_refs, ci_refs = refs[:n_e], refs[n_e:n_e + n_v], refs[n_e + n_v:n_e + n_v + n_ci]
        o_refs = refs[n_e + n_v + n_ci:n_e + n_v + n_ci + n_o]
        co_refs = refs[n_e + n_v + n_ci + n_o:n_e + n_v + n_ci + n_o + n_co]
        scratch = refs[n_e + n_v + n_ci + n_o + n_co:]
        sems = scratch[-2:]
        begin(ci_refs, co_refs, sems)
        part = lax.dot_general(a_ref[...].astype(BF16), b_ref[...].astype(BF16), dims, preferred_element_type=F32)

        def finish(acc):
            outs = epilogue(acc, *[r[...] for r in e_refs], *[r[...] for r in v_refs]) if epilogue else (acc,)
            for r, v in zip(o_refs, outs):
                r[...] = v.astype(r.dtype)

        if nk == 1:
            finish(part)
        else:
            acc_ref = scratch[0]
            k = pl.program_id(2)

            @pl.when(k == 0)
            def _():
                acc_ref[...] = part

            @pl.when(k > 0)
            def _():
                acc_ref[...] += part

            @pl.when(k == nk - 1)
            def _():
                finish(acc_ref[...])
        end(ci_refs, co_refs, sems)

    tile = pl.BlockSpec((tm, tn), lambda i, j, k: (i, j))
    out = pl.pallas_call(
        body, name=name, grid=grid,
        in_specs=[a_spec, b_spec] + [tile] * n_e + [pl.BlockSpec((1, tn), lambda i, j, k: (0, j))] * n_v + c_in_specs,
        out_specs=[tile] * n_o + c_out_specs,
        out_shape=[_sds((m_dim, n_dim), d) for d in out_dtypes] + c_outs,
        scratch_shapes=([pltpu.VMEM((tm, tn), F32)] if nk > 1 else []) + (comm.sem_scratch() if comm else []),
        compiler_params=_params(("arbitrary",) * 3 if comm else ("parallel", "parallel", "arbitrary")),
    )(a, b, *extras, *vecs, *c_ins)
    return out


def _swap_halves(x):
    width = x.shape[1]
    lane = lax.broadcasted_iota(jnp.int32, x.shape, 1)
    return jnp.where((lane % 64) < 32, pltpu.roll(x, width - 32, 1), pltpu.roll(x, 32, 1))


def _rope(x, cos, sin_signed):
    return x * cos + _swap_halves(x) * sin_signed


def _rope_t(d, cos, sin_signed):
    return d * cos - _swap_halves(d) * sin_signed


def _rms(x, g):
    rstd = lax.rsqrt(jnp.mean(x * x, axis=-1, keepdims=True) + RMS_EPS)
    return x * rstd * g


def _rms_bwd(x, g, dy):
    rstd = lax.rsqrt(jnp.mean(x * x, axis=-1, keepdims=True) + RMS_EPS)
    xh = x * rstd
    dyg = dy * g
    dx = rstd * (dyg - xh * jnp.mean(dyg * xh, axis=-1, keepdims=True))
    return dx, jnp.sum(dy * xh, axis=0, keepdims=True)


def _layer_norm(r, g, b):
    mu = jnp.mean(r, axis=-1, keepdims=True)
    xc = r - mu
    var = jnp.mean(xc * xc, axis=-1, keepdims=True)
    return xc * lax.rsqrt(var + LN_EPS) * g + b


def _dot(a, b, dims):
    return lax.dot_general(a, b, (dims, ((), ())), preferred_element_type=F32)


_NN, _NT, _TN = ((1,), (0,)), ((1,), (1,)), ((0,), (0,))


def _in_proj(x_bf, w_in_p, cos, sin):
    n_rows = x_bf.shape[0]
    ts = 512

    def body(x_ref, w_ref, cos_ref, sin_ref, qkv_ref, cq_ref, kpe_ref, ckv_ref):
        xb = x_ref[...]
        cos_v, sin_v = cos_ref[...], sin_ref[...]

        def seg(lo, hi):
            return _dot(xb, w_ref[:, lo:hi], _NN)

        qkv_ref[:, 0:512] = (_rope(seg(0, 512), cos_v, sin_v) * A_SCALE).astype(BF16)
        qkv_ref[:, 512:1024] = _rope(seg(512, 1024), cos_v, sin_v).astype(BF16)
        qkv_ref[:, 1024:1536] = seg(1024, 1536).astype(BF16)
        cq_ref[...] = seg(1536, 1792)
        kpe_ref[...] = _rope(seg(1792, 1920), cos_v[:, :LANES], sin_v[:, :LANES]).astype(BF16)
        ckv_ref[...] = seg(1920, 2048)

    row = lambda c: pl.BlockSpec((ts, c), lambda i: (i, 0))
    return pl.pallas_call(
        body, name="in_proj", grid=(n_rows // ts,),
        in_specs=[row(D_MODEL), pl.BlockSpec((D_MODEL, IN_PAD), lambda i: (0, 0)), row(A_WIDTH), row(A_WIDTH)],
        out_specs=[row(3 * A_WIDTH), row(Q_LORA), row(LANES), row(KV_LORA)],
        out_shape=[_sds((n_rows, 3 * A_WIDTH), BF16), _sds((n_rows, Q_LORA)), _sds((n_rows, LANES), BF16),
                   _sds((n_rows, KV_LORA))],
        compiler_params=_params(("parallel",)),
    )(x_bf, w_in_p, cos, sin)


def _band_mask(m, heads):
    qi = lax.broadcasted_iota(jnp.int32, (heads * SPAN, 2 * SPAN), 0) % SPAN
    kj = lax.broadcasted_iota(jnp.int32, (heads * SPAN, 2 * SPAN), 1)
    return ((kj < SPAN) & (kj >= qi) & (m > 0)) | ((kj >= SPAN) & ((kj - SPAN) <= qi))


def _even_lanes(rows):
    return lax.broadcasted_iota(jnp.int32, (rows, LANES), 1) < A_HEAD_DIM


def _dil_view(t, dil):
    rows, width = t.shape
    return t.reshape(rows // dil, dil * width)


def _dil_fwd(qkv_view, dil):
    nb = qkv_view.shape[0] // SPAN

    def body(q_ref, kp_ref, kc_ref, vp_ref, vc_ref, ol_ref):
        mask = _band_mask(pl.program_id(1), 1)
        even, even2 = _even_lanes(SPAN), _even_lanes(2 * SPAN)
        for p in range(A_HEADS // 2):
            sl = slice(p * LANES, (p + 1) * LANES)
            q2 = q_ref[:, sl]
            kcat = jnp.concatenate([kp_ref[:, sl], kc_ref[:, sl]], axis=0)
            vcat = jnp.concatenate([vp_ref[:, sl], vc_ref[:, sl]], axis=0)
            zero, one = jnp.zeros_like(q2), jnp.ones_like(vcat)
            res, lses = [], []
            for first in (True, False):
                qh = jnp.where(even, q2, zero) if first else jnp.where(even, zero, q2)
                vh = jnp.where(even2, vcat, one) if first else jnp.where(even2, one, vcat)
                s = jnp.where(mask, _dot(qh, kcat, _NT), NEG)
                mx = jnp.max(s, axis=-1, keepdims=True)
                r = _dot(jnp.exp(s - mx).astype(BF16), vh, _NN)
                den = pltpu.roll(r, A_HEAD_DIM, 1)
                res.append(r / den)
                lses.append(mx + jnp.log(den))
            ol_ref[:, sl] = jnp.where(even, res[0], res[1])
            ol_ref[:, A_WIDTH + p * LANES:A_WIDTH + (p + 1) * LANES] = jnp.where(even, lses[0], lses[1])

    def cur(part, parts):
        return pl.BlockSpec((SPAN, A_WIDTH), lambda r, m: (m, parts * r + part))

    def prev(part, parts):
        return pl.BlockSpec((SPAN, A_WIDTH), lambda r, m: (jnp.maximum(m - 1, 0), parts * r + part))

    return pl.pallas_call(
        body, name=f"dil_fwd_d{dil}", grid=(dil, nb),
        in_specs=[cur(0, 3), prev(1, 3), cur(1, 3), prev(2, 3), cur(2, 3)],
        out_specs=pl.BlockSpec((SPAN, 2 * A_WIDTH), lambda r, m: (m, r)),
        out_shape=_sds((qkv_view.shape[0], dil * 2 * A_WIDTH)),
        compiler_params=_params(("parallel", "arbitrary")),
    )(qkv_view, qkv_view, qkv_view, qkv_view, qkv_view)


def _mix_fwd(ols, b_out, a_g, b_g):
    def fn(ol1, ol2, ol3, b, ag, bg):
        (o1, l1), (o2, l2), (o3, l3) = [(t[:, :A_WIDTH], t[:, A_WIDTH:]) for t in (ol1, ol2, ol3)]
        mx = jnp.maximum(jnp.maximum(l1, l2), l3)
        e1, e2, e3 = jnp.exp(l1 - mx), jnp.exp(l2 - mx), jnp.exp(l3 - mx)
        a = (e1 * o1 + e2 * o2 + e3 * o3) / (e1 + e2 + e3)
        return a, jnp.concatenate([_rms(a, ag), _rms(b, bg)], axis=1)

    n_rows = b_out.shape[0]
    return _rowwise("mix_fwd", fn, [*ols, b_out], [_sds((n_rows, A_WIDTH)), _sds((n_rows, 2 * A_WIDTH), BF16)],
                    consts=[a_g, b_g])


def _mla_q_prep(cq, g, w_uq_p, cos, sin):
    def fn(cq_v, cos_v, sin_v, g_v, w_v):
        cqn = _rms(cq_v, g_v).astype(BF16)
        q = _dot(cqn, w_v, _NN)
        qf = [jnp.concatenate([q[:, h * QK_NOPE:(h + 1) * QK_NOPE],
                               _rope(q[:, 512 + h * LANES:512 + (h + 1) * LANES], cos_v[:, :LANES], sin_v[:, :LANES])], axis=1)
              for h in range(MLA_HEADS)]
        return cqn, qf

    n_rows = cq.shape[0]
    return _rowwise("mla_q_prep", fn, [cq, cos, sin], [_sds((n_rows, Q_LORA), BF16), _sds((MLA_HEADS, n_rows, QK_PAD), BF16)],
                    consts=[g, w_uq_p])


def _mla_kv_prep(ckv, kpe, g, w_ukv_p):
    def fn(ckv_v, kpe_v, g_v, w_v):
        ckvn = _rms(ckv_v, g_v).astype(BF16)
        kvv = _dot(ckvn, w_v, _NN)
        kf = [jnp.concatenate([kvv[:, h * QK_NOPE:(h + 1) * QK_NOPE], kpe_v.astype(F32)], axis=1) for h in range(MLA_HEADS)]
        return ckvn, kf, kvv[:, 512:]

    n_rows = ckv.shape[0]
    return _rowwise("mla_kv_prep", fn, [ckv, kpe],
                    [_sds((n_rows, KV_LORA), BF16), _sds((MLA_HEADS, n_rows, QK_PAD), BF16), _sds((n_rows, MLA_WIDTH), BF16)],
                    consts=[g, w_ukv_p])


MLA_FWD_TILES = (1024, 2048)
MLA_BWD_TILES = (1024, 1024)


def _mla_fwd(qf, kf, v, comm=None):
    n_rows = v.shape[0]
    tq, tk = min(MLA_FWD_TILES[0], n_rows), min(MLA_FWD_TILES[1], n_rows)
    nq, nk = n_rows // tq, n_rows // tk
    grid = (MLA_HEADS, nq, nk)
    exp2_scale = MLA_SCALE * math.log2(math.e)
    c_ins, c_in_specs, c_outs, c_out_specs, begin, end = _hosted(comm, grid)
    n_ci, n_co = len(c_ins), len(c_outs)

    def last_k(i):
        return jnp.right_shift(i * tq + tq - 1, int(math.log2(tk)))

    def body(q_ref, k_ref, v_ref, *refs):
        ci_refs, (o_ref, lse_ref), co_refs = refs[:n_ci], refs[n_ci:n_ci + 2], refs[n_ci + 2:n_ci + 2 + n_co]
        m_sc, l_sc, acc_sc = refs[n_ci + 2 + n_co:n_ci + 5 + n_co]
        sems = refs[n_ci + 5 + n_co:]
        i, j = pl.program_id(1), pl.program_id(2)
        begin(ci_refs, co_refs, sems)

        @pl.when(j == 0)
        def _():
            m_sc[...] = jnp.full(m_sc.shape, NEG, F32)
            l_sc[...] = jnp.zeros(l_sc.shape, F32)
            acc_sc[...] = jnp.zeros(acc_sc.shape, F32)

        def step(masked):
            s = _dot(q_ref[...], k_ref[...], _NT)
            if masked:
                row = lax.broadcasted_iota(jnp.int32, (tq, tk), 0) + i * tq
                col = lax.broadcasted_iota(jnp.int32, (tq, tk), 1) + j * tk
                s = jnp.where(col <= row, s, NEG)
            m_prev = m_sc[...]
            m_new = jnp.maximum(m_prev, jnp.max(s, axis=-1, keepdims=True))
            alpha = jnp.exp2((m_prev - m_new) * exp2_scale)
            p = jnp.exp2((s - m_new) * exp2_scale)
            l_sc[...] = alpha * l_sc[...] + jnp.sum(p, axis=-1, keepdims=True)
            acc_sc[...] = alpha * acc_sc[...] + _dot(p.astype(BF16), v_ref[...], _NN)
            m_sc[...] = m_new

        active = j * tk <= i * tq + tq - 1
        crosses = (j + 1) * tk - 1 > i * tq

        @pl.when(active & jnp.logical_not(crosses))
        def _():
            step(False)

        @pl.when(active & crosses)
        def _():
            step(True)

        @pl.when(j == last_k(i))
        def _():
            o_ref[...] = acc_sc[...] / l_sc[...]
            lse_ref[...] = jnp.broadcast_to(m_sc[...] * MLA_SCALE + jnp.log(l_sc[...]), (tq, V_DIM))

        end(ci_refs, co_refs, sems)

    return pl.pallas_call(
        body, name="mla_fwd", grid=grid,
        in_specs=[pl.BlockSpec((None, tq, QK_PAD), lambda h, i, j: (h, i, 0)),
                  pl.BlockSpec((None, tk, QK_PAD), lambda h, i, j: (h, jnp.minimum(j, last_k(i)), 0)),
                  pl.BlockSpec((tk, V_DIM), lambda h, i, j: (jnp.minimum(j, last_k(i)), h))] + c_in_specs,
        out_specs=[pl.BlockSpec((tq, V_DIM), lambda h, i, j: (i, h)), pl.BlockSpec((tq, V_DIM), lambda h, i, j: (i, h))]
        + c_out_specs,
        out_shape=[_sds((n_rows, MLA_WIDTH)), _sds((n_rows, MLA_WIDTH))] + c_outs,
        scratch_shapes=[pltpu.VMEM((tq, 1), F32), pltpu.VMEM((tq, 1), F32), pltpu.VMEM((tq, V_DIM), F32)]
        + (comm.sem_scratch() if comm else []),
        compiler_params=_params(("arbitrary",) * 3 if comm else ("parallel", "parallel", "arbitrary")),
    )(qf, kf, v, *c_ins)


def _mm_res_ln(name, a, w, xres, g, b):
    def epi(acc, xr, g_v, b_v):
        r = ALPHA * xr + acc
        y = _layer_norm(r, g_v, b_v)
        return y, y, r

    return _mm(name, a, w, (F32, BF16, F32), tm=1024, tn=D_MODEL, tk=1024, epilogue=epi, extras=[xres], vecs=[g, b])


def _mm_relu2(x_bf, w):
    def epi(acc):
        r = jnp.maximum(acc, 0.0)
        return (r * r,)

    return _mm("ff1", x_bf, w, (BF16,), tm=1024, tn=2048, tk=D_MODEL, epilogue=epi)[0]


def _loss_fn(y, t):
    def fn(y_v, t_v):
        d = y_v - t_v
        part = jnp.sum(jnp.sum(d * d, axis=1, keepdims=True), axis=0, keepdims=True)
        return d * (1.0 / D_MODEL), part

    dy, part = _rowwise("loss", fn, [y, t], [_sds(y.shape)], reds=[_sds((1, 1))])
    return part * (0.5 / D_MODEL), dy


def _ln_bwd(name, dy, r, g):
    def fn(dy_v, r_v, g_v):
        mu = jnp.mean(r_v, axis=-1, keepdims=True)
        xc = r_v - mu
        rstd = lax.rsqrt(jnp.mean(xc * xc, axis=-1, keepdims=True) + LN_EPS)
        xh = xc * rstd
        dxh = dy_v * g_v
        dr = rstd * (dxh - jnp.mean(dxh, axis=-1, keepdims=True) - xh * jnp.mean(dxh * xh, axis=-1, keepdims=True))
        return dr, dr, jnp.sum(dy_v * xh, axis=0, keepdims=True), jnp.sum(dy_v, axis=0, keepdims=True)

    return _rowwise(name, fn, [dy, r], [_sds(dy.shape), _sds(dy.shape, BF16)], consts=[g],
                    reds=[_sds((1, D_MODEL)), _sds((1, D_MODEL))])


def _head_sum_matrix():
    i = lax.broadcasted_iota(jnp.int32, (A_WIDTH, A_WIDTH), 0) // A_HEAD_DIM
    j = lax.broadcasted_iota(jnp.int32, (A_WIDTH, A_WIDTH), 1) // A_HEAD_DIM
    return (i == j).astype(BF16)


def _mix_bwd(dmixed, a_out, b_out, ols, a_g, b_g):
    def fn(dm, a, b, ol1, ol2, ol3, ag, bg, e_mat):
        l1, l2, l3 = ol1[:, A_WIDTH:], ol2[:, A_WIDTH:], ol3[:, A_WIDTH:]
        da, dga = _rms_bwd(a, ag, dm[:, :A_WIDTH])
        db, dgb = _rms_bwd(b, bg, dm[:, A_WIDTH:])
        t = da * a
        t_hi = t.astype(BF16)
        t_lo = (t - t_hi.astype(F32)).astype(BF16)
        tsum = _dot(t_hi, e_mat, _NN) + _dot(t_lo, e_mat, _NN)
        mx = jnp.maximum(jnp.maximum(l1, l2), l3)
        e1, e2, e3 = jnp.exp(l1 - mx), jnp.exp(l2 - mx), jnp.exp(l3 - mx)
        inv = 1.0 / (e1 + e2 + e3)
        w1, w2, w3 = e1 * inv, e2 * inv, e3 * inv
        tb = db * b
        delta_b = jnp.concatenate(
            [jnp.broadcast_to(jnp.sum(tb[:, h * V_DIM:(h + 1) * V_DIM], axis=-1, keepdims=True), (tb.shape[0], V_DIM))
             for h in range(MLA_HEADS)], axis=1)
        dods = [jnp.concatenate([w * da, w * tsum], axis=1) for w in (w1, w2, w3)]
        return (*dods, db, delta_b, dga, dgb)

    n_rows = a_out.shape[0]
    wide = (n_rows, A_WIDTH)
    return _rowwise("mix_bwd", fn, [dmixed, a_out, b_out, *ols],
                    [_sds((n_rows, 2 * A_WIDTH))] * 3 + [_sds(wide, BF16), _sds(wide)],
                    consts=[a_g, b_g, _head_sum_matrix()], reds=[_sds((1, A_WIDTH)), _sds((1, MLA_WIDTH))], ts=256)


def _dil_bwd(qkv_view, ol_view, dod_view, dil):
    nb = qkv_view.shape[0] // SPAN

    def body(q_ref, kp_ref, kc_ref, vp_ref, vc_ref, lse_ref, do_ref, dl_ref, out_ref, cq, ck, cv):
        m = pl.program_id(1)

        @pl.when(m == 0)
        def _():
            cq[...] = jnp.zeros(cq.shape, F32)
            ck[...] = jnp.zeros(ck.shape, F32)
            cv[...] = jnp.zeros(cv.shape, F32)

        @pl.when(m < nb)
        def _():
            mask, even = _band_mask(m, 2), _even_lanes(SPAN)
            for p in range(A_HEADS // 2):
                sl = slice(p * LANES, (p + 1) * LANES)
                sl_k = slice(A_WIDTH + p * LANES, A_WIDTH + (p + 1) * LANES)
                sl_v = slice(2 * A_WIDTH + p * LANES, 2 * A_WIDTH + (p + 1) * LANES)
                q2, do2 = q_ref[:, sl], do_ref[:, sl].astype(BF16)
                zero = jnp.zeros_like(q2)
                qcat = jnp.concatenate([jnp.where(even, q2, zero), jnp.where(even, zero, q2)], axis=0)
                docat = jnp.concatenate([jnp.where(even, do2, zero), jnp.where(even, zero, do2)], axis=0)
                kcat = jnp.concatenate([kp_ref[:, sl], kc_ref[:, sl]], axis=0)
                vcat = jnp.concatenate([vp_ref[:, sl], vc_ref[:, sl]], axis=0)
                lse2, dl2 = lse_ref[:, sl], dl_ref[:, sl]
                lse_c = jnp.concatenate([lse2[:, :1], lse2[:, A_HEAD_DIM:A_HEAD_DIM + 1]], axis=0)
                dl_c = jnp.concatenate([dl2[:, :1], dl2[:, A_HEAD_DIM:A_HEAD_DIM + 1]], axis=0)
                pr = jnp.exp(jnp.where(mask, _dot(qcat, kcat, _NT), NEG) - lse_c)
                ds = (pr * (_dot(docat, vcat, _NT) - dl_c)).astype(BF16)
                dq = _dot(ds, kcat, _NN)
                dk2 = _dot(ds, qcat, _TN)
                dv2 = _dot(pr.astype(BF16), docat, _TN)
                out_ref[:, sl] = cq[:, sl]
                out_ref[:, sl_k] = ck[:, sl] + dk2[:SPAN]
                out_ref[:, sl_v] = cv[:, sl] + dv2[:SPAN]
                cq[:, sl] = jnp.where(even, dq[:SPAN], dq[SPAN:])
                ck[:, sl] = dk2[SPAN:]
                cv[:, sl] = dv2[SPAN:]

        @pl.when(m == nb)
        def _():
            out_ref[:, 0:A_WIDTH] = cq[...]
            out_ref[:, A_WIDTH:2 * A_WIDTH] = ck[...]
            out_ref[:, 2 * A_WIDTH:3 * A_WIDTH] = cv[...]

    last = nb - 1

    def cur(part, parts):
        return pl.BlockSpec((SPAN, A_WIDTH), lambda r, m: (jnp.minimum(m, last), parts * r + part))

    def prev(part, parts):
        return pl.BlockSpec((SPAN, A_WIDTH), lambda r, m: (jnp.clip(m - 1, 0, last), parts * r + part))

    carry = pltpu.VMEM((SPAN, A_WIDTH), F32)
    return pl.pallas_call(
        body, name=f"dil_bwd_d{dil}", grid=(dil, nb + 1),
        in_specs=[cur(0, 3), prev(1, 3), cur(1, 3), prev(2, 3), cur(2, 3), cur(1, 2), cur(0, 2), cur(1, 2)],
        out_specs=pl.BlockSpec((SPAN, 3 * A_WIDTH), lambda r, m: (jnp.clip(m - 1, 0, last), r)),
        out_shape=_sds(qkv_view.shape),
        scratch_shapes=[carry, carry, carry],
        compiler_params=_params(("parallel", "arbitrary")),
    )(qkv_view, qkv_view, qkv_view, qkv_view, qkv_view, ol_view, dod_view, dod_view)


def _mla_bwd(qf, kf, v, lse, do, delta, comm=None):
    n_rows = v.shape[0]
    tq, tk = min(MLA_BWD_TILES[0], n_rows), min(MLA_BWD_TILES[1], n_rows)
    nq, nk = n_rows // tq, n_rows // tk

    def first_q(j):
        return jnp.right_shift(j * tk, int(math.log2(tq)))

    grid = (MLA_HEADS, nk, nq)
    c_ins, c_in_specs, c_outs, c_out_specs, begin, end = _hosted(comm, grid)
    n_ci, n_co = len(c_ins), len(c_outs)

    def body(q_ref, k_ref, v_ref, lse_ref, do_ref, dl_ref, *refs):
        ci_refs, (dq_ref, dk_ref, dv_ref), co_refs = refs[:n_ci], refs[n_ci:n_ci + 3], refs[n_ci + 3:n_ci + 3 + n_co]
        dk_sc, dv_sc = refs[n_ci + 3 + n_co:n_ci + 5 + n_co]
        sems = refs[n_ci + 5 + n_co:]
        j, i = pl.program_id(1), pl.program_id(2)
        begin(ci_refs, co_refs, sems)

        @pl.when((j == 0) & (i == 0))
        def _():
            dq_ref[...] = jnp.zeros(dq_ref.shape, F32)

        @pl.when(i == first_q(j))
        def _():
            dk_sc[...] = jnp.zeros(dk_sc.shape, F32)
            dv_sc[...] = jnp.zeros(dv_sc.shape, F32)

        def step(masked):
            q, k, dob = q_ref[...], k_ref[...], do_ref[...]
            s = _dot(q, k, _NT) * MLA_SCALE
            if masked:
                row = lax.broadcasted_iota(jnp.int32, (tq, tk), 0) + i * tq
                col = lax.broadcasted_iota(jnp.int32, (tq, tk), 1) + j * tk
                s = jnp.where(col <= row, s, NEG)
            p = jnp.exp(s - lse_ref[:, :1])
            dp = _dot(dob, v_ref[...], _NT)
            ds = (p * (dp - dl_ref[:, :1]) * MLA_SCALE).astype(BF16)
            dv_sc[...] += _dot(p.astype(BF16), dob, _TN)
            dk_sc[...] += _dot(ds, q, _TN)
            rows = pl.ds(pl.multiple_of(i * tq, tq), tq)
            dq_ref[rows, :] += _dot(ds, k, _NN)

        active = i >= first_q(j)
        crosses = (j + 1) * tk - 1 > i * tq

        @pl.when(active & jnp.logical_not(crosses))
        def _():
            step(False)

        @pl.when(active & crosses)
        def _():
            step(True)

        @pl.when(i == nq - 1)
        def _():
            dk_ref[...] = dk_sc[...]
            dv_ref[...] = dv_sc[...]

        end(ci_refs, co_refs, sems)

    qrow = lambda h, j, i: (jnp.maximum(i, first_q(j)), h)
    return pl.pallas_call(
        body, name="mla_bwd", grid=grid,
        in_specs=[pl.BlockSpec((None, tq, QK_PAD), lambda h, j, i: (h, jnp.maximum(i, first_q(j)), 0)),
                  pl.BlockSpec((None, tk, QK_PAD), lambda h, j, i: (h, j, 0)),
                  pl.BlockSpec((tk, V_DIM), lambda h, j, i: (j, h)),
                  pl.BlockSpec((tq, V_DIM), qrow), pl.BlockSpec((tq, V_DIM), qrow), pl.BlockSpec((tq, V_DIM), qrow)]
        + c_in_specs,
        out_specs=[pl.BlockSpec((None, n_rows, QK_PAD), lambda h, j, i: (h, 0, 0)),
                   pl.BlockSpec((None, tk, QK_PAD), lambda h, j, i: (h, j, 0)),
                   pl.BlockSpec((tk, V_DIM), lambda h, j, i: (j, h))] + c_out_specs,
        out_shape=[_sds((MLA_HEADS, n_rows, QK_PAD)), _sds((MLA_HEADS, n_rows, QK_PAD)), _sds((n_rows, MLA_WIDTH))] + c_outs,
        scratch_shapes=[pltpu.VMEM((tk, QK_PAD), F32), pltpu.VMEM((tk, V_DIM), F32)] + (comm.sem_scratch() if comm else []),
        compiler_params=_params(("arbitrary",) * 3 if comm else ("parallel", "arbitrary", "arbitrary")),
    )(qf, kf, v, lse, do, delta, *c_ins)


def _mla_bwd_prep(dqf, dkf, dv, cos, sin):
    def fn(dqf_v, dkf_v, dv_v, cos_v, sin_v):
        dq = jnp.concatenate([dqf_v[h][:, :QK_NOPE] for h in range(MLA_HEADS)]
                             + [_rope_t(dqf_v[h][:, QK_NOPE:], cos_v[:, :LANES], sin_v[:, :LANES]) for h in range(MLA_HEADS)], axis=1)
        dkv = jnp.concatenate([dkf_v[h][:, :QK_NOPE] for h in range(MLA_HEADS)] + [dv_v], axis=1)
        dkpe = dkf_v[0][:, QK_NOPE:] + dkf_v[1][:, QK_NOPE:] + dkf_v[2][:, QK_NOPE:] + dkf_v[3][:, QK_NOPE:]
        return dq, dkv, dkpe

    n_rows = dv.shape[0]
    return _rowwise("mla_bwd_prep", fn, [dqf, dkf, dv, cos, sin],
                    [_sds((n_rows, UQ_PAD), BF16), _sds((n_rows, 1024), BF16), _sds((n_rows, LANES))], ts=256)


def _assemble_dh(dqkvs, dcqn, cq, dckvn, ckv, dkpe, cos, sin, gq, gkv):
    def fn(g1, g2, g3, dcqn_v, cq_v, dckvn_v, ckv_v, dkpe_v, cos_v, sin_v, gq_v, gkv_v):
        g = g1 + g2 + g3
        dqa = _rope_t(g[:, :A_WIDTH], cos_v, sin_v) * A_SCALE
        dka = _rope_t(g[:, A_WIDTH:2 * A_WIDTH], cos_v, sin_v)
        dcq, dgq = _rms_bwd(cq_v, gq_v, dcqn_v)
        dckv, dgkv = _rms_bwd(ckv_v, gkv_v, dckvn_v)
        dkr = _rope_t(dkpe_v, cos_v[:, :LANES], sin_v[:, :LANES])
        return jnp.concatenate([dqa, dka, g[:, 2 * A_WIDTH:], dcq, dkr, dckv], axis=1), dgq, dgkv

    n_rows = cq.shape[0]
    return _rowwise("assemble_dh", fn, [*dqkvs, dcqn, cq, dckvn, ckv, dkpe, cos, sin],
                    [_sds((n_rows, IN_PAD), BF16)], consts=[gq, gkv],
                    reds=[_sds((1, Q_LORA)), _sds((1, KV_LORA))], ts=256)


def _layer_fwd(x_f32, x_bf, w, sm, cos, sin, behind=None):
    qkv, cq, kpe, ckv = _in_proj(x_bf, w["w_in"], cos, sin)
    n_rows = qkv.shape[0]
    qkv_views = [_dil_view(qkv, dil) for _, dil in PATTERNS]
    ol_views = [_dil_fwd(view, dil) for view, (_, dil) in zip(qkv_views, PATTERNS)]
    ols = [t.reshape(n_rows, 2 * A_WIDTH) for t in ol_views]
    cqn, qf = _mla_q_prep(cq, sm["q_a_norm"], w["w_uq"], cos, sin)
    ckvn, kf, v = _mla_kv_prep(ckv, kpe, sm["kv_a_norm"], w["w_ukv"])
    b_out, b_lse, *behind_outs = _mla_fwd(qf, kf, v, comm=behind)
    a_out, mixed = _mix_fwd(ols, b_out, sm["a_out_norm"], sm["b_out_norm"])
    x1, x1_bf, r1 = _mm_res_ln("wo_ln1", mixed, w["w_o"], x_f32, sm["ln1_g"], sm["ln1_b"])
    f = _mm_relu2(x1_bf, w["w_ff1"])
    x2, x2_bf, r2 = _mm_res_ln("ff2_ln2", f, w["w_ff2"], x1, sm["ln2_g"], sm["ln2_b"])
    saved = dict(x_bf=x_bf, qkv_views=qkv_views, ol_views=ol_views, ols=ols, cq=cq, ckv=ckv, cqn=cqn, qf=qf, ckvn=ckvn,
                 kf=kf, v=v, b_out=b_out, b_lse=b_lse, a_out=a_out, mixed=mixed, x1_bf=x1_bf, r1=r1, f=f, r2=r2)
    return x2, x2_bf, saved, behind_outs


def _layer_bwd(dx2, w, sm, sv, cos, sin, pending=None):
    dr2, dr2_bf, dg2, db2 = _ln_bwd("ln2_bwd", dx2, sv["r2"], sm["ln2_g"])
    (dw_ff2,) = _mm("dw_ff2", sv["f"], dr2_bf, (F32,), ta=True, tm=1024, tn=1024, tk=2048)
    du, *swapped = _mm("d_u", dr2_bf, w["w_ff2"], (BF16,), tb=True, tm=1024, tn=2048, tk=D_MODEL,
                       epilogue=lambda acc, f: (acc * (2.0 * jnp.sqrt(f.astype(F32))),), extras=[sv["f"]],
                       comm=pending.swap_comm() if pending else None)
    if pending:
        pending.after_swap(swapped)
    (dw_ff1,) = _mm("dw_ff1", sv["x1_bf"], du, (F32,), ta=True, tm=1024, tn=1024, tk=2048)
    (dy1,) = _mm("d_x1", du, w["w_ff1"], (F32,), tb=True, tm=1024, tn=1024, tk=2048,
                 epilogue=lambda acc, d: (acc + ALPHA * d,), extras=[dr2])
    dr1, dr1_bf, dg1, db1 = _ln_bwd("ln1_bwd", dy1, sv["r1"], sm["ln1_g"])
    (dw_o,) = _mm("dw_o", sv["mixed"], dr1_bf, (F32,), ta=True, tm=1024, tn=1024, tk=2048)
    (dmixed,) = _mm("d_mixed", dr1_bf, w["w_o"], (F32,), tb=True, tm=1024, tn=1024, tk=D_MODEL)
    dod1, dod2, dod3, do_b, delta_b, dga, dgb = _mix_bwd(
        dmixed, sv["a_out"], sv["b_out"], sv["ols"], sm["a_out_norm"], sm["b_out_norm"])
    n_rows = dmixed.shape[0]
    dqkvs = [_dil_bwd(qkv_view, ol_view, _dil_view(dod, dil), dil).reshape(n_rows, 3 * A_WIDTH)
             for (_, dil), qkv_view, ol_view, dod in zip(PATTERNS, sv["qkv_views"], sv["ol_views"], (dod1, dod2, dod3))]
    dqf, dkf, dv_b, *scattered = _mla_bwd(sv["qf"], sv["kf"], sv["v"], sv["b_lse"], do_b, delta_b,
                                          comm=pending.scatter_comm() if pending else None)
    if pending:
        pending.after_scatter(scattered)
    dq_b, dkv_b, dkpe = _mla_bwd_prep(dqf, dkf, dv_b, cos, sin)
    (dw_uq,) = _mm("dw_uq", sv["cqn"], dq_b, (F32,), ta=True, tm=Q_LORA, tn=1024, tk=2048)
    (dcqn,) = _mm("d_cqn", dq_b, w["w_uq"], (F32,), tb=True, tm=1024, tn=Q_LORA, tk=UQ_PAD)
    (dw_ukv,) = _mm("dw_ukv", sv["ckvn"], dkv_b, (F32,), ta=True, tm=KV_LORA, tn=1024, tk=2048)
    (dckvn,) = _mm("d_ckvn", dkv_b, w["w_ukv"], (F32,), tb=True, tm=1024, tn=KV_LORA, tk=1024)
    dh, dgq, dgkv = _assemble_dh(dqkvs, dcqn, sv["cq"], dckvn, sv["ckv"], dkpe, cos, sin,
                                 sm["q_a_norm"], sm["kv_a_norm"])
    (dw_in,) = _mm("dw_in", sv["x_bf"], dh, (F32,), ta=True, tm=1024, tn=1024, tk=2048)
    (dx,) = _mm("d_x", dh, w["w_in"], (F32,), tb=True, tm=1024, tn=1024, tk=2048,
                epilogue=lambda acc, d: (acc + ALPHA * d,), extras=[dr1])
    dws = dict(w_in=dw_in, w_uq=dw_uq, w_ukv=dw_ukv, w_o=dw_o, w_ff1=dw_ff1, w_ff2=dw_ff2)
    dsm = dict(q_a_norm=dgq, kv_a_norm=dgkv, a_out_norm=dga, b_out_norm=dgb, ln1_g=dg1, ln1_b=db1, ln2_g=dg2, ln2_b=db2)
    return dx, dws, dsm


def _pad_w_in(w):
    return jnp.concatenate([w[:, :1792], w[:, 1920:1984], jnp.zeros((w.shape[0], 64), w.dtype), w[:, 1792:1920]], axis=1)


def _unpad_w_in(w):
    return jnp.concatenate([w[:, :1792], w[:, 1920:2048], w[:, 1792:1856]], axis=1)


def _rope_tables(n_rows):
    half = A_HEAD_DIM // 2
    inv_freq = ROPE_THETA ** (-jnp.arange(half, dtype=F32) / half)
    ang = jnp.arange(n_rows, dtype=F32)[:, None] * inv_freq[None, :]
    cos = jnp.tile(jnp.cos(ang), (1, 2 * A_HEADS))
    sin = jnp.tile(jnp.concatenate([-jnp.sin(ang), jnp.sin(ang)], axis=1), (1, A_HEADS))
    return cos, sin


W_NAMES = ("w_in", "w_uq", "w_ukv", "w_o", "w_ff1", "w_ff2")
PIECE = dict(w_in=(512, 512), w_uq=(128, 256), w_ukv=(64, 256), w_o=(128, 1024), w_ff1=(512, 1024), w_ff2=(512, 1024))
SHARD_COLS = dict(w_in=496, w_uq=192, w_ukv=256, w_o=1024, w_ff1=1024, w_ff2=1024)


def _col_pieces(t, pad_to):
    rows, _, cols = t.shape
    t = jnp.pad(t, ((0, 0), (0, 0), (0, pad_to - cols)))
    return t.reshape(2, rows // 2, N_CHIPS, pad_to).transpose(2, 0, 1, 3)


def _grad_pieces(dws):
    uq, ukv = dws["w_uq"], dws["w_ukv"]
    uq = jnp.concatenate([uq[:, :512].reshape(Q_LORA, MLA_HEADS, LANES), uq[:, 512:].reshape(Q_LORA, MLA_HEADS, LANES)], axis=2)
    ukv = jnp.concatenate([ukv[:, :512].reshape(KV_LORA, MLA_HEADS, LANES), ukv[:, 512:].reshape(KV_LORA, MLA_HEADS, LANES)], axis=2)
    return dict(
        w_in=_col_pieces(_unpad_w_in(dws["w_in"]).reshape(D_MODEL, N_CHIPS, SHARD_COLS["w_in"]), 512),
        w_uq=_col_pieces(uq, 256), w_ukv=_col_pieces(ukv, 256),
        w_o=dws["w_o"].reshape(N_CHIPS, 2, 128, D_MODEL),
        w_ff1=_col_pieces(dws["w_ff1"].reshape(D_MODEL, N_CHIPS, 1024), 1024),
        w_ff2=dws["w_ff2"].reshape(N_CHIPS, 2, 512, D_MODEL))


def _weights_from_pieces(p):
    def cols(t):
        return t.transpose(1, 2, 0, 3).reshape(2 * t.shape[2], N_CHIPS, t.shape[3])

    uq, ukv = cols(p["w_uq"]), cols(p["w_ukv"])
    return dict(
        w_in=_pad_w_in(cols(p["w_in"])[:, :, :SHARD_COLS["w_in"]].reshape(D_MODEL, IN_COLS)),
        w_uq=jnp.concatenate([uq[:, :, :LANES].reshape(Q_LORA, 512), uq[:, :, LANES:].reshape(Q_LORA, 512)], axis=1),
        w_ukv=jnp.concatenate([ukv[:, :, :LANES].reshape(KV_LORA, 512), ukv[:, :, LANES:].reshape(KV_LORA, 512)], axis=1),
        w_o=p["w_o"].reshape(D_MODEL, D_MODEL),
        w_ff1=cols(p["w_ff1"]).reshape(D_MODEL, D_FF),
        w_ff2=p["w_ff2"].reshape(D_FF, D_MODEL))


def _my_piece(name, shard, half):
    rows, cols = shard.shape[0] // 2, shard.shape[1]
    t = lax.dynamic_slice_in_dim(shard, half * rows, rows, axis=0).astype(BF16)
    return jnp.pad(t, ((0, 0), (0, PIECE[name][1] - cols)))


MESH = pl.DeviceIdType.MESH
ANY = pl.BlockSpec(memory_space=pl.ANY)


class _Comm:
    def __init__(self, ins, out_shapes, n_sems, start, finish):
        self.ins, self.out_shapes, self.n_sems, self.start, self.finish = list(ins), list(out_shapes), n_sems, start, finish

    def sem_scratch(self):
        return [pltpu.SemaphoreType.DMA((self.n_sems,)), pltpu.SemaphoreType.DMA((self.n_sems,))]


def _comm_call(name, comm):
    n_in, n_out = len(comm.ins), len(comm.out_shapes)

    def body(*refs):
        ins, outs, sems = refs[:n_in], refs[n_in:n_in + n_out], refs[n_in + n_out:]
        comm.start(ins, outs, *sems)
        comm.finish(ins, outs, *sems)

    return pl.pallas_call(body, name=name, out_shape=comm.out_shapes, in_specs=[ANY] * n_in, out_specs=[ANY] * n_out,
                          scratch_shapes=comm.sem_scratch())(*comm.ins)


def _all_gather_comm(blocks):
    n = len(blocks)

    def plan(x_refs, out_refs, send_sems, recv_sems):
        x, y, c = lax.axis_index("x"), lax.axis_index("y"), lax.axis_index("c")
        me, sibling = (x, y, c), (x, y, 1 - c)
        chips = [(1 - x, y), (x, 1 - y), (1 - x, 1 - y)]

        def copy(t, k, blk, to, src=None):
            px, py, pc = blk
            slot = out_refs[t].at[4 * px + 2 * py + pc]
            return pltpu.make_async_remote_copy(
                src_ref=slot if src is None else src, dst_ref=slot,
                send_sem=send_sems.at[7 * t + k], recv_sem=recv_sems.at[7 * t + k], device_id=to, device_id_type=MESH)

        first = []
        for t in range(n):
            first.append(copy(t, 0, me, sibling, src=x_refs[t]))
            first += [copy(t, 1 + j, me, (*chip, c), src=x_refs[t]) for j, chip in enumerate(chips)]
        return me, sibling, chips, c, copy, first

    def start(x_refs, out_refs, send_sems, recv_sems):
        for cp in plan(x_refs, out_refs, send_sems, recv_sems)[-1]:
            cp.start()

    def finish(x_refs, out_refs, send_sems, recv_sems):
        me, sibling, chips, c, copy, sent = plan(x_refs, out_refs, send_sems, recv_sems)
        for j, chip in enumerate(chips):
            for t in range(n):
                copy(t, 1 + j, (*chip, c), me).wait_recv()
                sent.append(copy(t, 4 + j, (*chip, c), sibling))
                sent[-1].start()
        for t in range(n):
            copy(t, 0, sibling, me).wait_recv()
        for j, chip in enumerate(chips):
            for t in range(n):
                copy(t, 4 + j, (*chip, 1 - c), me).wait_recv()
        for cp in sent:
            cp.wait_send()

    return _Comm(blocks, [_sds((N_DEV,) + b.shape, b.dtype) for b in blocks], 7 * n, start, finish)


def _simple_comm(ins, out_shapes, n_sems, copies):
    def start(in_refs, out_refs, send_sems, recv_sems):
        for cp in copies(in_refs, out_refs, send_sems, recv_sems):
            cp.start()

    def finish(in_refs, out_refs, send_sems, recv_sems):
        for cp in copies(in_refs, out_refs, send_sems, recv_sems):
            cp.wait()

    return _Comm(ins, out_shapes, n_sems, start, finish)


def _swap_comm(gs):
    def copies(g_refs, got_refs, send_sems, recv_sems):
        c = lax.axis_index("c")
        sibling = (lax.axis_index("x"), lax.axis_index("y"), 1 - c)
        return [pltpu.make_async_remote_copy(src_ref=g_refs[t].at[k, 1 - c], dst_ref=got_refs[t].at[k],
                                             send_sem=send_sems.at[N_CHIPS * t + k], recv_sem=recv_sems.at[N_CHIPS * t + k],
                                             device_id=sibling, device_id_type=MESH)
                for t in range(len(gs)) for k in range(N_CHIPS)]

    return _simple_comm(gs, [_sds((N_CHIPS,) + g.shape[2:], g.dtype) for g in gs], N_CHIPS * len(gs), copies)


def _scatter_comm(hs):
    def copies(h_refs, got_refs, send_sems, recv_sems):
        x, y, c = lax.axis_index("x"), lax.axis_index("y"), lax.axis_index("c")
        chips = [(1 - x, y), (x, 1 - y), (1 - x, 1 - y)]
        return [pltpu.make_async_remote_copy(src_ref=h_refs[t].at[2 * px + py], dst_ref=got_refs[t].at[rel],
                                             send_sem=send_sems.at[3 * t + rel], recv_sem=recv_sems.at[3 * t + rel],
                                             device_id=(px, py, c), device_id_type=MESH)
                for t in range(len(hs)) for rel, (px, py) in enumerate(chips)]

    return _simple_comm(hs, [_sds((3,) + h.shape[1:], h.dtype) for h in hs], 3 * len(hs), copies)


def _share_comm(rs):
    def copies(r_refs, got_refs, send_sems, recv_sems):
        sibling = (lax.axis_index("x"), lax.axis_index("y"), 1 - lax.axis_index("c"))
        return [pltpu.make_async_remote_copy(src_ref=r_refs[t], dst_ref=got_refs[t], send_sem=send_sems.at[t],
                                             recv_sem=recv_sems.at[t], device_id=sibling, device_id_type=MESH)
                for t in range(len(rs))]

    return _simple_comm(rs, [_sds(r.shape, r.dtype) for r in rs], len(rs), copies)


SUM_STEPS = 4


def _pair_sums(name, gs, gots, c_arr):
    n = len(gs)
    dims = [(g.shape[2] // SUM_STEPS, g.shape[3]) for g in gs]

    def body(c_ref, *refs):
        for t in range(n):
            s = refs[t][...] + refs[n + t][...]
            refs[2 * n + t][...] = s
            refs[3 * n + t][...] = s.astype(BF16)

    def part(tr, pc):
        return pl.BlockSpec((None, tr, pc), lambda k, i, c_ref: (k, i, 0))

    def kept(tr, pc):
        return pl.BlockSpec((None, None, tr, pc), lambda k, i, c_ref: (k, c_ref[0], i, 0))

    outs = pl.pallas_call(
        body, name=name,
        out_shape=[_sds((N_CHIPS,) + g.shape[2:]) for g in gs] + [_sds((N_CHIPS,) + g.shape[2:], BF16) for g in gs],
        grid_spec=pltpu.PrefetchScalarGridSpec(
            num_scalar_prefetch=1, grid=(N_CHIPS, SUM_STEPS),
            in_specs=[kept(*d) for d in dims] + [part(*d) for d in dims],
            out_specs=[part(*d) for d in dims] * 2),
        compiler_params=_params(("parallel", "parallel")),
    )(c_arr, *gs, *gots)
    return outs[:n], outs[n:]


def _chips_sums(name, h32s, gots, chip_arr):
    n = len(h32s)
    dims = [(h.shape[1] // SUM_STEPS, h.shape[2]) for h in h32s]

    def body(chip_ref, *refs):
        for t in range(n):
            got_ref = refs[n + t]
            refs[2 * n + t][...] = refs[t][...] + got_ref[0] + got_ref[1] + got_ref[2]

    return pl.pallas_call(
        body, name=name, out_shape=[_sds(h.shape[1:]) for h in h32s],
        grid_spec=pltpu.PrefetchScalarGridSpec(
            num_scalar_prefetch=1, grid=(SUM_STEPS,),
            in_specs=[pl.BlockSpec((None, tr, pc), lambda i, chip_ref: (chip_ref[0], i, 0)) for tr, pc in dims]
            + [pl.BlockSpec((3, tr, pc), lambda i, chip_ref: (0, i, 0)) for tr, pc in dims],
            out_specs=[pl.BlockSpec((tr, pc), lambda i, chip_ref: (i, 0)) for tr, pc in dims]),
        compiler_params=_params(("parallel",)),
    )(chip_arr, *h32s, *gots)


class _LayerReduce:
    def __init__(self, tag, dws, c, chip):
        self.tag, self.c, self.gp = tag, c, _grad_pieces(dws)
        self.c_arr, self.chip_arr = jnp.reshape(c, (1,)).astype(jnp.int32), jnp.reshape(chip, (1,)).astype(jnp.int32)
        self.shards = None

    def swap_comm(self):
        return _swap_comm([self.gp[n] for n in W_NAMES])

    def after_swap(self, gots):
        self.s32, self.s16 = _pair_sums(f"rs_pair_sum_{self.tag}", [self.gp[n] for n in W_NAMES], gots, self.c_arr)

    def scatter_comm(self):
        return _scatter_comm(self.s16)

    def after_scatter(self, parts):
        mine = _chips_sums(f"rs_chips_sum_{self.tag}", self.s32, parts, self.chip_arr)
        theirs = _comm_call(f"rs_share_{self.tag}", _share_comm(mine))
        self.shards = {}
        for n, a, b in zip(W_NAMES, mine, theirs):
            both = jnp.where(self.c == 0, jnp.concatenate([a, b], axis=0), jnp.concatenate([b, a], axis=0))
            self.shards[n] = both[:, :SHARD_COLS[n]]

    def run_alone(self):
        self.after_swap(_comm_call(f"rs_pair_{self.tag}", self.swap_comm()))
        self.after_scatter(_comm_call(f"rs_chips_{self.tag}", self.scatter_comm()))


class _LayerWeights:
    def __init__(self, shards, c, dev):
        self.shards, self.c, self.dev = shards, c, dev

    def comm(self, l):
        self.blocks = [_my_piece(n, self.shards[l][n], self.c) for n in W_NAMES]
        return _all_gather_comm(self.blocks)

    def take(self, l, gathered):
        pieces = {n: lax.dynamic_update_index_in_dim(g, b, self.dev, 0).reshape((N_CHIPS, 2) + b.shape)
                  for n, g, b in zip(W_NAMES, gathered, self.blocks)}
        return _weights_from_pieces(pieces)

    def first(self):
        return self.take(0, _comm_call("ag_l0", self.comm(0)))


def _all_reduce_small(vec):
    rows, lanes = vec.shape

    def gather_body(x_ref, out_ref, send_sems, recv_sems):
        x, y, c = lax.axis_index("x"), lax.axis_index("y"), lax.axis_index("c")
        me, sibling = (x, y, c), (x, y, 1 - c)
        chips = [(1 - x, y), (x, 1 - y), (1 - x, 1 - y)]

        def slot(px, py, pc):
            return out_ref.at[4 * px + 2 * py + pc]

        def copy(k, blk, to, src=None):
            return pltpu.make_async_remote_copy(
                src_ref=slot(*blk) if src is None else src, dst_ref=slot(*blk),
                send_sem=send_sems.at[k], recv_sem=recv_sems.at[k], device_id=to, device_id_type=MESH)

        out_ref[4 * x + 2 * y + c] = x_ref[...]
        first = [copy(0, me, sibling, src=x_ref)]
        first += [copy(1 + j, me, (*chip, c), src=x_ref) for j, chip in enumerate(chips)]
        for cp in first:
            cp.start()
        passed = [copy(4 + j, (*chip, c), sibling) for j, chip in enumerate(chips)]
        for j, chip in enumerate(chips):
            copy(1 + j, (*chip, c), me).wait_recv()
            passed[j].start()
        copy(0, sibling, me).wait_recv()
        for j, chip in enumerate(chips):
            copy(4 + j, (*chip, 1 - c), me).wait_recv()
        for cp in first + passed:
            cp.wait_send()

    vmem = pl.BlockSpec(memory_space=pltpu.VMEM)
    allv = pl.pallas_call(
        gather_body, name="small_all_gather", out_shape=_sds((N_DEV, rows, lanes)),
        in_specs=[vmem], out_specs=vmem,
        scratch_shapes=[pltpu.SemaphoreType.DMA((7,)), pltpu.SemaphoreType.DMA((7,))],
    )(vec)

    def sum_body(a_ref, o_ref):
        acc = a_ref[0]
        for d in range(1, N_DEV):
            acc = acc + a_ref[d]
        o_ref[...] = acc

    return pl.pallas_call(sum_body, name="small_sum", out_shape=_sds((rows, lanes)), in_specs=[vmem], out_specs=vmem)(allv)


def _adamw(name, w, g, m, v, ts=512):
    def fn(w_v, g_v, m_v, v_v):
        m_n = ADAM_B1 * m_v + (1.0 - ADAM_B1) * g_v
        v_n = ADAM_B2 * v_v + (1.0 - ADAM_B2) * (g_v * g_v)
        m_hat = m_n / (1.0 - ADAM_B1 ** ADAM_STEP)
        v_hat = v_n / (1.0 - ADAM_B2 ** ADAM_STEP)
        delta = -ADAM_LR * (m_hat / (jnp.sqrt(v_hat) + ADAM_EPS) + ADAM_WD * w_v)
        return delta, m_n, v_n

    return _rowwise(name, fn, [w, g, m, v], [_sds(w.shape)] * 3, ts=ts)


def _pack_small(per_layer):
    flat = jnp.concatenate([per_layer[l][n].reshape(-1) for l in range(DEPTH) for n in SMALL_NAMES])
    return jnp.pad(flat, (0, SMALL_ROWS * LANES - flat.shape[0])).reshape(SMALL_ROWS, LANES)


def _unpack_small(packed):
    flat = packed.reshape(-1)
    per = sum(SMALL_SIZES)
    out = {}
    for n, size, off in zip(SMALL_NAMES, SMALL_SIZES, [sum(SMALL_SIZES[:i]) for i in range(len(SMALL_SIZES))]):
        out[n] = jnp.stack([flat[l * per + off:l * per + off + size] for l in range(DEPTH)])
    return out


def _fwd_bwd(x, target, layer_weights, smalls, on_layer_grads):
    depth = len(smalls)
    cos, sin = _rope_tables(x.shape[0])
    h_f32, h_bf = x, x.astype(BF16)
    saved, weights = [], [layer_weights.first()]
    for l in range(depth):
        behind = layer_weights.comm(l + 1) if l + 1 < depth else None
        h_f32, h_bf, sv, brought = _layer_fwd(h_f32, h_bf, weights[l], smalls[l], cos, sin, behind)
        saved.append(sv)
        if l + 1 < depth:
            weights.append(layer_weights.take(l + 1, brought))
    loss_part, dh = _loss_fn(h_f32, target)
    small_grads, pending = [None] * depth, None
    for l in reversed(range(depth)):
        dh, dws, small_grads[l] = _layer_bwd(dh, weights[l], smalls[l], saved[l], cos, sin, pending)
        pending = on_layer_grads(l, dws)
    return loss_part, dh, small_grads, pending


def kernel(x, w_in, q_a_norm, kv_a_norm, w_uq, w_ukv, a_out_norm, b_out_norm, w_o, ln1_g, ln1_b, w_ff1, w_ff2, ln2_g, ln2_b, loss_target, m_w_in, m_q_a_norm, m_kv_a_norm, m_w_uq, m_w_ukv, m_a_out_norm, m_b_out_norm, m_w_o, m_ln1_g, m_ln1_b, m_w_ff1, m_w_ff2, m_ln2_g, m_ln2_b, v_w_in, v_q_a_norm, v_kv_a_norm, v_w_uq, v_w_ukv, v_a_out_norm, v_b_out_norm, v_w_o, v_ln1_g, v_ln1_b, v_w_ff1, v_w_ff2, v_ln2_g, v_ln2_b):
    c = lax.axis_index("c")
    chip = 2 * lax.axis_index("x") + lax.axis_index("y")
    big = dict(w_in=w_in, w_uq=w_uq, w_ukv=w_ukv, w_o=w_o, w_ff1=w_ff1, w_ff2=w_ff2)
    big_m = dict(w_in=m_w_in, w_uq=m_w_uq, w_ukv=m_w_ukv, w_o=m_w_o, w_ff1=m_w_ff1, w_ff2=m_w_ff2)
    big_v = dict(w_in=v_w_in, w_uq=v_w_uq, w_ukv=v_w_ukv, w_o=v_w_o, w_ff1=v_w_ff1, w_ff2=v_w_ff2)
    small = dict(q_a_norm=q_a_norm, kv_a_norm=kv_a_norm, a_out_norm=a_out_norm, b_out_norm=b_out_norm,
                 ln1_g=ln1_g, ln1_b=ln1_b, ln2_g=ln2_g, ln2_b=ln2_b)
    small_m = dict(q_a_norm=m_q_a_norm, kv_a_norm=m_kv_a_norm, a_out_norm=m_a_out_norm, b_out_norm=m_b_out_norm,
                   ln1_g=m_ln1_g, ln1_b=m_ln1_b, ln2_g=m_ln2_g, ln2_b=m_ln2_b)
    small_v = dict(q_a_norm=v_q_a_norm, kv_a_norm=v_kv_a_norm, a_out_norm=v_a_out_norm, b_out_norm=v_b_out_norm,
                   ln1_g=v_ln1_g, ln1_b=v_ln1_b, ln2_g=v_ln2_g, ln2_b=v_ln2_b)

    layer_weights = _LayerWeights([{n: big[n][l] for n in W_NAMES} for l in range(DEPTH)], c, 2 * chip + c)
    smalls = [{n: small[n][l][None, :] for n in SMALL_NAMES} for l in range(DEPTH)]
    reductions = [None] * DEPTH

    def reduce_layer(l, dws):
        reductions[l] = _LayerReduce(f"l{l}", dws, c, chip)
        return reductions[l]

    loss_part, grad_x, small_grads, last = _fwd_bwd(x[0], loss_target[0], layer_weights, smalls, reduce_layer)
    last.run_alone()
    shard_grads = [r.shards for r in reductions]
    loss = lax.psum(loss_part[0, 0], ("x", "y", "c"))
    grad_x = grad_x[None]

    g_small_packed = _all_reduce_small(_pack_small(small_grads))
    pack_in = lambda d: _pack_small([{n: d[n][l] for n in SMALL_NAMES} for l in range(DEPTH)])
    sd, sm_, sv_ = _adamw("adamw_small", pack_in(small), g_small_packed, pack_in(small_m), pack_in(small_v), ts=SMALL_ROWS)
    g_small, d_small, m_small, v_small = (_unpack_small(t) for t in (g_small_packed, sd, sm_, sv_))

    g_big, d_big, m_big, v_big = {}, {}, {}, {}
    for n in W_NAMES:
        g = jnp.stack([shard_grads[l][n] for l in range(DEPTH)])
        shape = g.shape
        flat = lambda t: t.reshape(shape[0] * shape[1], shape[2])
        d_, m_, v_ = _adamw(f"adamw_{n}", flat(big[n]), flat(g), flat(big_m[n]), flat(big_v[n]))
        g_big[n], d_big[n], m_big[n], v_big[n] = g, d_.reshape(shape), m_.reshape(shape), v_.reshape(shape)

    order = ("w_in", "q_a_norm", "kv_a_norm", "w_uq", "w_ukv", "a_out_norm", "b_out_norm", "w_o", "ln1_g", "ln1_b",
             "w_ff1", "w_ff2", "ln2_g", "ln2_b")
    pick = lambda bigd, smalld: [bigd[n] if n in bigd else smalld[n] for n in order]
    return (loss, grad_x, *pick(g_big, g_small), *pick(d_big, d_small), *pick(m_big, m_small), *pick(v_big, v_small))
```

```python
import math

import jax
import jax.numpy as jnp
from jax import lax
from jax.experimental import pallas as pl
from jax.experimental.pallas import tpu as pltpu

F32, BF16 = jnp.float32, jnp.bfloat16

D_MODEL = 1024
DEPTH = 4
A_HEAD_DIM = 64
A_HEADS = 8
A_WIDTH = 512
PATTERNS = ((128, 1), (512, 4), (2048, 16))
SPAN = 128
MLA_HEADS = 4
QK_NOPE = 128
QK_ROPE = 64
V_DIM = 128
Q_LORA = 256
KV_LORA = 128
MLA_WIDTH = 512
QK_PAD = 256
IN_COLS = 1984
IN_PAD = 2048
UQ_PAD = 1024
D_FF = 4096
ROPE_THETA = 10000.0
ALPHA = (2.0 * DEPTH) ** 0.25
LN_EPS = 1e-5
RMS_EPS = 1e-6
MLA_SCALE = (QK_NOPE + QK_ROPE) ** -0.5
A_SCALE = A_HEAD_DIM ** -0.5

ADAM_LR, ADAM_B1, ADAM_B2, ADAM_EPS, ADAM_WD, ADAM_STEP = 0.001, 0.9, 0.999, 1e-08, 0.01, 10

VMEM_LIMIT_BYTES = 56 * 1024 * 1024
NEG = -1e30
N_CHIPS, N_DEV = 4, 8
LANES = 128

SMALL_NAMES = ("q_a_norm", "kv_a_norm", "a_out_norm", "b_out_norm", "ln1_g", "ln1_b", "ln2_g", "ln2_b")
SMALL_SIZES = (256, 128, 512, 512, 1024, 1024, 1024, 1024)
SMALL_ROWS = 176


def _params(sem):
    return pltpu.CompilerParams(dimension_semantics=sem, vmem_limit_bytes=VMEM_LIMIT_BYTES)


def _interleave(parts, scratch):
    d, (n, width) = len(parts), parts[0].shape
    for r in range(d):
        for s in range(width // LANES):
            scratch.at[s][pl.ds(r, n, stride=d), :] = parts[r][:, s * LANES:(s + 1) * LANES]
    return jnp.concatenate([scratch[s] for s in range(width // LANES)], axis=1)


def _deinterleave(x, d, scratch):
    n, width = x.shape[0] // d, x.shape[1]
    for s in range(width // LANES):
        scratch[s] = x[:, s * LANES:(s + 1) * LANES]
    return [jnp.concatenate([scratch.at[s][pl.ds(r, n, stride=d), :] for s in range(width // LANES)], axis=1) for r in range(d)]


def _rowwise(name, fn, ins, outs, consts=(), reds=(), ts=512, in_dils=None, out_dils=None):
    in_dils = list(in_dils or [1] * len(ins))
    out_dils = list(out_dils or [1] * len(outs))
    n_rows = ins[0].shape[-2] * in_dils[0]
    ts = min(ts, n_rows)
    assert n_rows % ts == 0
    n_in, n_c, n_o, n_r = len(ins), len(consts), len(outs), len(reds)
    viewed = [(a.shape, d) for a, d in zip(list(ins) + list(outs), in_dils + out_dils) if d > 1]

    def tile_spec(shape, d=1):
        if len(shape) == 2:
            return pl.BlockSpec((ts // d, shape[1]), lambda i: (i, 0))
        return pl.BlockSpec((shape[0], ts, shape[2]), lambda i: (0, i, 0))

    def whole_spec(shape):
        return pl.BlockSpec(shape, lambda i: (0,) * len(shape))

    def body(*refs):
        in_refs, c_refs = refs[:n_in], refs[n_in:n_in + n_c]
        o_refs = refs[n_in + n_c:n_in + n_c + n_o]
        r_refs = refs[n_in + n_c + n_o:n_in + n_c + n_o + n_r]
        scratches = list(refs[n_in + n_c + n_o + n_r:])
        vals = []
        for r, d in zip(in_refs, in_dils):
            if d == 1:
                vals.append(r[...])
            else:
                width = r.shape[1] // d
                vals.append(_interleave([r[:, k * width:(k + 1) * width] for k in range(d)], scratches.pop(0)))
        res = fn(*vals, *[r[...] for r in c_refs])
        res = tuple(res) if isinstance(res, (tuple, list)) else (res,)
        for r, v, d in zip(o_refs, res[:n_o], out_dils):
            if len(r.shape) == 3:
                for g in range(r.shape[0]):
                    r[g] = v[g].astype(r.dtype)
            elif d == 1:
                r[...] = v.astype(r.dtype)
            else:
                width = r.shape[1] // d
                for k, part in enumerate(_deinterleave(v, d, scratches.pop(0))):
                    r[:, k * width:(k + 1) * width] = part.astype(r.dtype)
        if n_r:
            i = pl.program_id(0)

            @pl.when(i == 0)
            def _():
                for r, v in zip(r_refs, res[n_o:]):
                    r[...] = v

            @pl.when(i > 0)
            def _():
                for r, v in zip(r_refs, res[n_o:]):
                    r[...] += v

    out = pl.pallas_call(
        body, name=name, grid=(n_rows // ts,),
        in_specs=[tile_spec(a.shape, d) for a, d in zip(ins, in_dils)] + [whole_spec(c.shape) for c in consts],
        out_specs=[tile_spec(o.shape, d) for o, d in zip(outs, out_dils)] + [whole_spec(r.shape) for r in reds],
        out_shape=list(outs) + list(reds),
        scratch_shapes=[pltpu.VMEM((shape[1] // d // LANES, ts, LANES), F32) for shape, d in viewed],
        compiler_params=_params(("arbitrary",)),
    )(*ins, *consts)
    return out


def _sds(shape, dtype=F32):
    return jax.ShapeDtypeStruct(tuple(shape), dtype)


def _hosted(comm, grid):
    if comm is None:
        return [], [], [], [], (lambda *a: None), (lambda *a: None)

    def edge(which, at, ins, outs, sems):
        ids = [pl.program_id(d) for d in range(len(grid))]
        hit = ids[0] == at[0]
        for i, v in zip(ids[1:], at[1:]):
            hit = hit & (i == v)

        @pl.when(hit)
        def _():
            which(ins, outs, *sems)

    begin = lambda ins, outs, sems: edge(comm.start, [0] * len(grid), ins, outs, sems)
    end = lambda ins, outs, sems: edge(comm.finish, [g - 1 for g in grid], ins, outs, sems)
    return comm.ins, [ANY] * len(comm.ins), comm.out_shapes, [ANY] * len(comm.out_shapes), begin, end


def _mm(name, a, b, out_dtypes, *, ta=False, tb=False, tm=512, tn=512, tk=512, epilogue=None, extras=(), vecs=(),
        comm=None):
    (k_dim, m_dim) = a.shape if ta else a.shape[::-1]
    (n_dim, k2) = b.shape if tb else b.shape[::-1]
    assert k_dim == k2
    tm, tn, tk = min(tm, m_dim), min(tn, n_dim), min(tk, k_dim)
    assert m_dim % tm == 0 and n_dim % tn == 0 and k_dim % tk == 0
    nk = k_dim // tk
    grid = (m_dim // tm, n_dim // tn, nk)
    a_spec = pl.BlockSpec((tk, tm), lambda i, j, k: (k, i)) if ta else pl.BlockSpec((tm, tk), lambda i, j, k: (i, k))
    b_spec = pl.BlockSpec((tn, tk), lambda i, j, k: (j, k)) if tb else pl.BlockSpec((tk, tn), lambda i, j, k: (k, j))
    dims = (((0 if ta else 1,), (1 if tb else 0,)), ((), ()))
    n_e, n_v, n_o = len(extras), len(vecs), len(out_dtypes)
    c_ins, c_in_specs, c_outs, c_out_specs, begin, end = _hosted(comm, grid)
    n_ci, n_co = len(c_ins), len(c_outs)

    def body(a_ref, b_ref, *refs):
        e_refs, v_refs, ci_refs = refs[:n_e], refs[n_e:n_e + n_v], refs[n_e + n_v:n_e + n_v + n_ci]
        o_refs = refs[n_e + n_v + n_ci:n_e + n_v + n_ci + n_o]
        co_refs = refs[n_e + n_v + n_ci + n_o:n_e + n_v + n_ci + n_o + n_co]
        scratch = refs[n_e + n_v + n_ci + n_o + n_co:]
        sems = scratch[-2:]
        begin(ci_refs, co_refs, sems)
        part = lax.dot_general(a_ref[...].astype(BF16), b_ref[...].astype(BF16), dims, preferred_element_type=F32)

        def finish(acc):
            outs = epilogue(acc, *[r[...] for r in e_refs], *[r[...] for r in v_refs]) if epilogue else (acc,)
            for r, v in zip(o_refs, outs):
                r[...] = v.astype(r.dtype)

        if nk == 1:
            finish(part)
        else:
            acc_ref = scratch[0]
            k = pl.program_id(2)

            @pl.when(k == 0)
            def _():
                acc_ref[...] = part

            @pl.when(k > 0)
            def _():
                acc_ref[...] += part

            @pl.when(k == nk - 1)
            def _():
                finish(acc_ref[...])
        end(ci_refs, co_refs, sems)

    tile = pl.BlockSpec((tm, tn), lambda i, j, k: (i, j))
    out = pl.pallas_call(
        body, name=name, grid=grid,
        in_specs=[a_spec, b_spec] + [tile] * n_e + [pl.BlockSpec((1, tn), lambda i, j, k: (0, j))] * n_v + c_in_specs,
        out_specs=[tile] * n_o + c_out_specs,
        out_shape=[_sds((m_dim, n_dim), d) for d in out_dtypes] + c_outs,
        scratch_shapes=([pltpu.VMEM((tm, tn), F32)] if nk > 1 else []) + (comm.sem_scratch() if comm else []),
        compiler_params=_params(("arbitrary",) * 3 if comm else ("parallel", "parallel", "arbitrary")),
    )(a, b, *extras, *vecs, *c_ins)
    return out


def _swap_halves(x):
    width = x.shape[1]
    lane = lax.broadcasted_iota(jnp.int32, x.shape, 1)
    return jnp.where((lane % 64) < 32, pltpu.roll(x, width - 32, 1), pltpu.roll(x, 32, 1))


def _rope(x, cos, sin_signed):
    return x * cos + _swap_halves(x) * sin_signed


def _rope_t(d, cos, sin_signed):
    return d * cos - _swap_halves(d) * sin_signed


def _rms(x, g):
    rstd = lax.rsqrt(jnp.mean(x * x, axis=-1, keepdims=True) + RMS_EPS)
    return x * rstd * g


def _rms_bwd(x, g, dy):
    rstd = lax.rsqrt(jnp.mean(x * x, axis=-1, keepdims=True) + RMS_EPS)
    xh = x * rstd
    dyg = dy * g
    dx = rstd * (dyg - xh * jnp.mean(dyg * xh, axis=-1, keepdims=True))
    return dx, jnp.sum(dy * xh, axis=0, keepdims=True)


def _layer_norm(r, g, b):
    mu = jnp.mean(r, axis=-1, keepdims=True)
    xc = r - mu
    var = jnp.mean(xc * xc, axis=-1, keepdims=True)
    return xc * lax.rsqrt(var + LN_EPS) * g + b


def _dot(a, b, dims):
    return lax.dot_general(a, b, (dims, ((), ())), preferred_element_type=F32)


_NN, _NT, _TN = ((1,), (0,)), ((1,), (1,)), ((0,), (0,))


def _in_proj(x_bf, w_in_p, cos, sin):
    n_rows = x_bf.shape[0]
    ts = 512
    dils = [dil for _, dil in PATTERNS]
    n_p = len(dils)

    def body(x_ref, w_ref, cos_ref, sin_ref, *refs):
        view_refs, (cq_ref, kpe_ref, ckv_ref, scratch) = refs[:n_p], refs[n_p:]
        xb = x_ref[...]
        cos_v, sin_v = cos_ref[...], sin_ref[...]

        def seg(lo, hi):
            return _dot(xb, w_ref[:, lo:hi], _NN)

        def put(part, val):
            for ref, dil in zip(view_refs, dils):
                pieces = [val] if dil == 1 else _deinterleave(val, dil, scratch)
                for r, piece in enumerate(pieces):
                    lo = (3 * r + part) * A_WIDTH
                    ref[:, lo:lo + A_WIDTH] = piece.astype(BF16)

        put(0, _rope(seg(0, 512), cos_v, sin_v) * A_SCALE)
        put(1, _rope(seg(512, 1024), cos_v, sin_v))
        put(2, seg(1024, 1536))
        cq_ref[...] = seg(1536, 1792)
        kpe_ref[...] = _rope(seg(1792, 1920), cos_v[:, :LANES], sin_v[:, :LANES]).astype(BF16)
        ckv_ref[...] = seg(1920, 2048)

    row = lambda c, d=1: pl.BlockSpec((ts // d, c * d), lambda i: (i, 0))
    return pl.pallas_call(
        body, name="in_proj", grid=(n_rows // ts,),
        in_specs=[row(D_MODEL), pl.BlockSpec((D_MODEL, IN_PAD), lambda i: (0, 0)), row(A_WIDTH), row(A_WIDTH)],
        out_specs=[row(3 * A_WIDTH, d) for d in dils] + [row(Q_LORA), row(LANES), row(KV_LORA)],
        out_shape=[_sds((n_rows // d, d * 3 * A_WIDTH), BF16) for d in dils]
        + [_sds((n_rows, Q_LORA)), _sds((n_rows, LANES), BF16), _sds((n_rows, KV_LORA))],
        scratch_shapes=[pltpu.VMEM((A_WIDTH // LANES, ts, LANES), F32)],
        compiler_params=_params(("parallel",)),
    )(x_bf, w_in_p, cos, sin)


def _band_mask(m, heads):
    qi = lax.broadcasted_iota(jnp.int32, (heads * SPAN, 2 * SPAN), 0) % SPAN
    kj = lax.broadcasted_iota(jnp.int32, (heads * SPAN, 2 * SPAN), 1)
    return ((kj < SPAN) & (kj >= qi) & (m > 0)) | ((kj >= SPAN) & ((kj - SPAN) <= qi))


def _even_lanes(rows):
    return lax.broadcasted_iota(jnp.int32, (rows, LANES), 1) < A_HEAD_DIM


def _dil_fwd(qkv_view, dil):
    nb = qkv_view.shape[0] // SPAN

    def body(q_ref, kp_ref, kc_ref, vp_ref, vc_ref, ol_ref):
        mask = _band_mask(pl.program_id(1), 1)
        even, even2 = _even_lanes(SPAN), _even_lanes(2 * SPAN)
        for p in range(A_HEADS // 2):
            sl = slice(p * LANES, (p + 1) * LANES)
            q2 = q_ref[:, sl]
            kcat = jnp.concatenate([kp_ref[:, sl], kc_ref[:, sl]], axis=0)
            vcat = jnp.concatenate([vp_ref[:, sl], vc_ref[:, sl]], axis=0)
            zero, one = jnp.zeros_like(q2), jnp.ones_like(vcat)
            res, lses = [], []
            for first in (True, False):
                qh = jnp.where(even, q2, zero) if first else jnp.where(even, zero, q2)
                vh = jnp.where(even2, vcat, one) if first else jnp.where(even2, one, vcat)
                s = jnp.where(mask, _dot(qh, kcat, _NT), NEG)
                mx = jnp.max(s, axis=-1, keepdims=True)
                r = _dot(jnp.exp(s - mx).astype(BF16), vh, _NN)
                den = pltpu.roll(r, A_HEAD_DIM, 1)
                res.append(r / den)
                lses.append(mx + jnp.log(den))
            ol_ref[:, sl] = jnp.where(even, res[0], res[1])
            ol_ref[:, A_WIDTH + p * LANES:A_WIDTH + (p + 1) * LANES] = jnp.where(even, lses[0], lses[1])

    def cur(part, parts):
        return pl.BlockSpec((SPAN, A_WIDTH), lambda r, m: (m, parts * r + part))

    def prev(part, parts):
        return pl.BlockSpec((SPAN, A_WIDTH), lambda r, m: (jnp.maximum(m - 1, 0), parts * r + part))

    return pl.pallas_call(
        body, name=f"dil_fwd_d{dil}", grid=(dil, nb),
        in_specs=[cur(0, 3), prev(1, 3), cur(1, 3), prev(2, 3), cur(2, 3)],
        out_specs=pl.BlockSpec((SPAN, 2 * A_WIDTH), lambda r, m: (m, r)),
        out_shape=_sds((qkv_view.shape[0], dil * 2 * A_WIDTH)),
        compiler_params=_params(("parallel", "arbitrary")),
    )(qkv_view, qkv_view, qkv_view, qkv_view, qkv_view)


def _mix_fwd(ols, b_out, a_g, b_g):
    def fn(ol1, ol2, ol3, b, ag, bg):
        (o1, l1), (o2, l2), (o3, l3) = [(t[:, :A_WIDTH], t[:, A_WIDTH:]) for t in (ol1, ol2, ol3)]
        mx = jnp.maximum(jnp.maximum(l1, l2), l3)
        e1, e2, e3 = jnp.exp(l1 - mx), jnp.exp(l2 - mx), jnp.exp(l3 - mx)
        a = (e1 * o1 + e2 * o2 + e3 * o3) / (e1 + e2 + e3)
        return a, jnp.concatenate([_rms(a, ag), _rms(b, bg)], axis=1)

    n_rows = b_out.shape[0]
    return _rowwise("mix_fwd", fn, [*ols, b_out], [_sds((n_rows, A_WIDTH)), _sds((n_rows, 2 * A_WIDTH), BF16)],
                    consts=[a_g, b_g], in_dils=[dil for _, dil in PATTERNS] + [1])


def _mla_q_prep(cq, g, w_uq_p, cos, sin):
    def fn(cq_v, cos_v, sin_v, g_v, w_v):
        cqn = _rms(cq_v, g_v).astype(BF16)
        q = _dot(cqn, w_v, _NN)
        qf = [jnp.concatenate([q[:, h * QK_NOPE:(h + 1) * QK_NOPE],
                               _rope(q[:, 512 + h * LANES:512 + (h + 1) * LANES], cos_v[:, :LANES], sin_v[:, :LANES])], axis=1)
              for h in range(MLA_HEADS)]
        return cqn, qf

    n_rows = cq.shape[0]
    return _rowwise("mla_q_prep", fn, [cq, cos, sin], [_sds((n_rows, Q_LORA), BF16), _sds((MLA_HEADS, n_rows, QK_PAD), BF16)],
                    consts=[g, w_uq_p])


def _mla_kv_prep(ckv, kpe, g, w_ukv_p):
    def fn(ckv_v, kpe_v, g_v, w_v):
        ckvn = _rms(ckv_v, g_v).astype(BF16)
        kvv = _dot(ckvn, w_v, _NN)
        kf = [jnp.concatenate([kvv[:, h * QK_NOPE:(h + 1) * QK_NOPE], kpe_v.astype(F32)], axis=1) for h in range(MLA_HEADS)]
        return ckvn, kf, kvv[:, 512:]

    n_rows = ckv.shape[0]
    return _rowwise("mla_kv_prep", fn, [ckv, kpe],
                    [_sds((n_rows, KV_LORA), BF16), _sds((MLA_HEADS, n_rows, QK_PAD), BF16), _sds((n_rows, MLA_WIDTH), BF16)],
                    consts=[g, w_ukv_p])


MLA_FWD_TILES = (1024, 2048)
MLA_BWD_TILES = (1024, 1024)


def _mla_fwd(qf, kf, v, comm=None):
    n_rows = v.shape[0]
    tq, tk = min(MLA_FWD_TILES[0], n_rows), min(MLA_FWD_TILES[1], n_rows)
    nq, nk = n_rows // tq, n_rows // tk
    grid = (MLA_HEADS, nq, nk)
    exp2_scale = MLA_SCALE * math.log2(math.e)
    c_ins, c_in_specs, c_outs, c_out_specs, begin, end = _hosted(comm, grid)
    n_ci, n_co = len(c_ins), len(c_outs)

    def last_k(i):
        return jnp.right_shift(i * tq + tq - 1, int(math.log2(tk)))

    def body(q_ref, k_ref, v_ref, *refs):
        ci_refs, (o_ref, lse_ref), co_refs = refs[:n_ci], refs[n_ci:n_ci + 2], refs[n_ci + 2:n_ci + 2 + n_co]
        m_sc, l_sc, acc_sc = refs[n_ci + 2 + n_co:n_ci + 5 + n_co]
        sems = refs[n_ci + 5 + n_co:]
        i, j = pl.program_id(1), pl.program_id(2)
        begin(ci_refs, co_refs, sems)

        @pl.when(j == 0)
        def _():
            m_sc[...] = jnp.full(m_sc.shape, NEG, F32)
            l_sc[...] = jnp.zeros(l_sc.shape, F32)
            acc_sc[...] = jnp.zeros(acc_sc.shape, F32)

        def step(masked):
            s = _dot(q_ref[...], k_ref[...], _NT)
            if masked:
                row = lax.broadcasted_iota(jnp.int32, (tq, tk), 0) + i * tq
                col = lax.broadcasted_iota(jnp.int32, (tq, tk), 1) + j * tk
                s = jnp.where(col <= row, s, NEG)
            m_prev = m_sc[...]
            m_new = jnp.maximum(m_prev, jnp.max(s, axis=-1, keepdims=True))
            alpha = jnp.exp2((m_prev - m_new) * exp2_scale)
            p = jnp.exp2((s - m_new) * exp2_scale)
            l_sc[...] = alpha * l_sc[...] + jnp.sum(p, axis=-1, keepdims=True)
            acc_sc[...] = alpha * acc_sc[...] + _dot(p.astype(BF16), v_ref[...], _NN)
            m_sc[...] = m_new

        active = j * tk <= i * tq + tq - 1
        crosses = (j + 1) * tk - 1 > i * tq

        @pl.when(active & jnp.logical_not(crosses))
        def _():
            step(False)

        @pl.when(active & crosses)
        def _():
            step(True)

        @pl.when(j == last_k(i))
        def _():
            o_ref[...] = acc_sc[...] / l_sc[...]
            lse_ref[...] = jnp.broadcast_to(m_sc[...] * MLA_SCALE + jnp.log(l_sc[...]), (tq, V_DIM))

        end(ci_refs, co_refs, sems)

    return pl.pallas_call(
        body, name="mla_fwd", grid=grid,
        in_specs=[pl.BlockSpec((None, tq, QK_PAD), lambda h, i, j: (h, i, 0)),
                  pl.BlockSpec((None, tk, QK_PAD), lambda h, i, j: (h, jnp.minimum(j, last_k(i)), 0)),
                  pl.BlockSpec((tk, V_DIM), lambda h, i, j: (jnp.minimum(j, last_k(i)), h))] + c_in_specs,
        out_specs=[pl.BlockSpec((tq, V_DIM), lambda h, i, j: (i, h)), pl.BlockSpec((tq, V_DIM), lambda h, i, j: (i, h))]
        + c_out_specs,
        out_shape=[_sds((n_rows, MLA_WIDTH)), _sds((n_rows, MLA_WIDTH))] + c_outs,
        scratch_shapes=[pltpu.VMEM((tq, 1), F32), pltpu.VMEM((tq, 1), F32), pltpu.VMEM((tq, V_DIM), F32)]
        + (comm.sem_scratch() if comm else []),
        compiler_params=_params(("arbitrary",) * 3 if comm else ("parallel", "parallel", "arbitrary")),
    )(qf, kf, v, *c_ins)


def _mm_res_ln(name, a, w, xres, g, b):
    def epi(acc, xr, g_v, b_v):
        r = ALPHA * xr + acc
        y = _layer_norm(r, g_v, b_v)
        return y, y, r

    return _mm(name, a, w, (F32, BF16, F32), tm=1024, tn=D_MODEL, tk=1024, epilogue=epi, extras=[xres], vecs=[g, b])


def _mm_relu2(x_bf, w):
    def epi(acc):
        r = jnp.maximum(acc, 0.0)
        return (r * r,)

    return _mm("ff1", x_bf, w, (BF16,), tm=1024, tn=2048, tk=D_MODEL, epilogue=epi)[0]


def _loss_fn(y, t):
    def fn(y_v, t_v):
        d = y_v - t_v
        part = jnp.sum(jnp.sum(d * d, axis=1, keepdims=True), axis=0, keepdims=True)
        return d * (1.0 / D_MODEL), part

    dy, part = _rowwise("loss", fn, [y, t], [_sds(y.shape)], reds=[_sds((1, 1))])
    return part * (0.5 / D_MODEL), dy


def _ln_bwd(name, dy, r, g):
    def fn(dy_v, r_v, g_v):
        mu = jnp.mean(r_v, axis=-1, keepdims=True)
        xc = r_v - mu
        rstd = lax.rsqrt(jnp.mean(xc * xc, axis=-1, keepdims=True) + LN_EPS)
        xh = xc * rstd
        dxh = dy_v * g_v
        dr = rstd * (dxh - jnp.mean(dxh, axis=-1, keepdims=True) - xh * jnp.mean(dxh * xh, axis=-1, keepdims=True))
        return dr, dr, jnp.sum(dy_v * xh, axis=0, keepdims=True), jnp.sum(dy_v, axis=0, keepdims=True)

    return _rowwise(name, fn, [dy, r], [_sds(dy.shape), _sds(dy.shape, BF16)], consts=[g],
                    reds=[_sds((1, D_MODEL)), _sds((1, D_MODEL))])


def _head_sum_matrix():
    i = lax.broadcasted_iota(jnp.int32, (A_WIDTH, A_WIDTH), 0) // A_HEAD_DIM
    j = lax.broadcasted_iota(jnp.int32, (A_WIDTH, A_WIDTH), 1) // A_HEAD_DIM
    return (i == j).astype(BF16)


def _mix_bwd(dmixed, a_out, b_out, ols, a_g, b_g):
    def fn(dm, a, b, ol1, ol2, ol3, ag, bg, e_mat):
        l1, l2, l3 = ol1[:, A_WIDTH:], ol2[:, A_WIDTH:], ol3[:, A_WIDTH:]
        da, dga = _rms_bwd(a, ag, dm[:, :A_WIDTH])
        db, dgb = _rms_bwd(b, bg, dm[:, A_WIDTH:])
        t = da * a
        t_hi = t.astype(BF16)
        t_lo = (t - t_hi.astype(F32)).astype(BF16)
        tsum = _dot(t_hi, e_mat, _NN) + _dot(t_lo, e_mat, _NN)
        mx = jnp.maximum(jnp.maximum(l1, l2), l3)
        e1, e2, e3 = jnp.exp(l1 - mx), jnp.exp(l2 - mx), jnp.exp(l3 - mx)
        inv = 1.0 / (e1 + e2 + e3)
        w1, w2, w3 = e1 * inv, e2 * inv, e3 * inv
        tb = db * b
        delta_b = jnp.concatenate(
            [jnp.broadcast_to(jnp.sum(tb[:, h * V_DIM:(h + 1) * V_DIM], axis=-1, keepdims=True), (tb.shape[0], V_DIM))
             for h in range(MLA_HEADS)], axis=1)
        dods = [jnp.concatenate([w * da, w * tsum], axis=1) for w in (w1, w2, w3)]
        return (*dods, db, delta_b, dga, dgb)

    n_rows = a_out.shape[0]
    wide = (n_rows, A_WIDTH)
    dils = [dil for _, dil in PATTERNS]
    return _rowwise("mix_bwd", fn, [dmixed, a_out, b_out, *ols],
                    [_sds((n_rows // d, d * 2 * A_WIDTH)) for d in dils] + [_sds(wide, BF16), _sds(wide)],
                    consts=[a_g, b_g, _head_sum_matrix()], reds=[_sds((1, A_WIDTH)), _sds((1, MLA_WIDTH))], ts=256,
                    in_dils=[1, 1, 1] + dils, out_dils=dils + [1, 1])


def _dil_bwd(qkv_view, ol_view, dod_view, dil):
    nb = qkv_view.shape[0] // SPAN

    def body(q_ref, kp_ref, kc_ref, vp_ref, vc_ref, lse_ref, do_ref, dl_ref, out_ref, cq, ck, cv):
        m = pl.program_id(1)

        @pl.when(m == 0)
        def _():
            cq[...] = jnp.zeros(cq.shape, F32)
            ck[...] = jnp.zeros(ck.shape, F32)
            cv[...] = jnp.zeros(cv.shape, F32)

        @pl.when(m < nb)
        def _():
            mask, even = _band_mask(m, 2), _even_lanes(SPAN)
            for p in range(A_HEADS // 2):
                sl = slice(p * LANES, (p + 1) * LANES)
                sl_k = slice(A_WIDTH + p * LANES, A_WIDTH + (p + 1) * LANES)
                sl_v = slice(2 * A_WIDTH + p * LANES, 2 * A_WIDTH + (p + 1) * LANES)
                q2, do2 = q_ref[:, sl], do_ref[:, sl].astype(BF16)
                zero = jnp.zeros_like(q2)
                qcat = jnp.concatenate([jnp.where(even, q2, zero), jnp.where(even, zero, q2)], axis=0)
                docat = jnp.concatenate([jnp.where(even, do2, zero), jnp.where(even, zero, do2)], axis=0)
                kcat = jnp.concatenate([kp_ref[:, sl], kc_ref[:, sl]], axis=0)
                vcat = jnp.concatenate([vp_ref[:, sl], vc_ref[:, sl]], axis=0)
                lse2, dl2 = lse_ref[:, sl], dl_ref[:, sl]
                lse_c = jnp.concatenate([lse2[:, :1], lse2[:, A_HEAD_DIM:A_HEAD_DIM + 1]], axis=0)
                dl_c = jnp.concatenate([dl2[:, :1], dl2[:, A_HEAD_DIM:A_HEAD_DIM + 1]], axis=0)
                pr = jnp.exp(jnp.where(mask, _dot(qcat, kcat, _NT), NEG) - lse_c)
                ds = (pr * (_dot(docat, vcat, _NT) - dl_c)).astype(BF16)
                dq = _dot(ds, kcat, _NN)
                dk2 = _dot(ds, qcat, _TN)
                dv2 = _dot(pr.astype(BF16), docat, _TN)
                out_ref[:, sl] = cq[:, sl]
                out_ref[:, sl_k] = ck[:, sl] + dk2[:SPAN]
                out_ref[:, sl_v] = cv[:, sl] + dv2[:SPAN]
                cq[:, sl] = jnp.where(even, dq[:SPAN], dq[SPAN:])
                ck[:, sl] = dk2[SPAN:]
                cv[:, sl] = dv2[SPAN:]

        @pl.when(m == nb)
        def _():
            out_ref[:, 0:A_WIDTH] = cq[...]
            out_ref[:, A_WIDTH:2 * A_WIDTH] = ck[...]
            out_ref[:, 2 * A_WIDTH:3 * A_WIDTH] = cv[...]

    last = nb - 1

    def cur(part, parts):
        return pl.BlockSpec((SPAN, A_WIDTH), lambda r, m: (jnp.minimum(m, last), parts * r + part))

    def prev(part, parts):
        return pl.BlockSpec((SPAN, A_WIDTH), lambda r, m: (jnp.clip(m - 1, 0, last), parts * r + part))

    carry = pltpu.VMEM((SPAN, A_WIDTH), F32)
    return pl.pallas_call(
        body, name=f"dil_bwd_d{dil}", grid=(dil, nb + 1),
        in_specs=[cur(0, 3), prev(1, 3), cur(1, 3), prev(2, 3), cur(2, 3), cur(1, 2), cur(0, 2), cur(1, 2)],
        out_specs=pl.BlockSpec((SPAN, 3 * A_WIDTH), lambda r, m: (jnp.clip(m - 1, 0, last), r)),
        out_shape=_sds(qkv_view.shape),
        scratch_shapes=[carry, carry, carry],
        compiler_params=_params(("parallel", "arbitrary")),
    )(qkv_view, qkv_view, qkv_view, qkv_view, qkv_view, ol_view, dod_view, dod_view)


def _mla_bwd(qf, kf, v, lse, do, delta, comm=None):
    n_rows = v.shape[0]
    tq, tk = min(MLA_BWD_TILES[0], n_rows), min(MLA_BWD_TILES[1], n_rows)
    nq, nk = n_rows // tq, n_rows // tk

    def first_q(j):
        return jnp.right_shift(j * tk, int(math.log2(tq)))

    grid = (MLA_HEADS, nk, nq)
    c_ins, c_in_specs, c_outs, c_out_specs, begin, end = _hosted(comm, grid)
    n_ci, n_co = len(c_ins), len(c_outs)

    def body(q_ref, k_ref, v_ref, lse_ref, do_ref, dl_ref, *refs):
        ci_refs, (dq_ref, dk_ref, dv_ref), co_refs = refs[:n_ci], refs[n_ci:n_ci + 3], refs[n_ci + 3:n_ci + 3 + n_co]
        dk_sc, dv_sc = refs[n_ci + 3 + n_co:n_ci + 5 + n_co]
        sems = refs[n_ci + 5 + n_co:]
        j, i = pl.program_id(1), pl.program_id(2)
        begin(ci_refs, co_refs, sems)

        @pl.when((j == 0) & (i == 0))
        def _():
            dq_ref[...] = jnp.zeros(dq_ref.shape, F32)

        @pl.when(i == first_q(j))
        def _():
            dk_sc[...] = jnp.zeros(dk_sc.shape, F32)
            dv_sc[...] = jnp.zeros(dv_sc.shape, F32)

        def step(masked):
            q, k, dob = q_ref[...], k_ref[...], do_ref[...]
            s = _dot(q, k, _NT) * MLA_SCALE
            if masked:
                row = lax.broadcasted_iota(jnp.int32, (tq, tk), 0) + i * tq
                col = lax.broadcasted_iota(jnp.int32, (tq, tk), 1) + j * tk
                s = jnp.where(col <= row, s, NEG)
            p = jnp.exp(s - lse_ref[:, :1])
            dp = _dot(dob, v_ref[...], _NT)
            ds = (p * (dp - dl_ref[:, :1]) * MLA_SCALE).astype(BF16)
            dv_sc[...] += _dot(p.astype(BF16), dob, _TN)
            dk_sc[...] += _dot(ds, q, _TN)
            rows = pl.ds(pl.multiple_of(i * tq, tq), tq)
            dq_ref[rows, :] += _dot(ds, k, _NN)

        active = i >= first_q(j)
        crosses = (j + 1) * tk - 1 > i * tq

        @pl.when(active & jnp.logical_not(crosses))
        def _():
            step(False)

        @pl.when(active & crosses)
        def _():
            step(True)

        @pl.when(i == nq - 1)
        def _():
            dk_ref[...] = dk_sc[...]
            dv_ref[...] = dv_sc[...]

        end(ci_refs, co_refs, sems)

    qrow = lambda h, j, i: (jnp.maximum(i, first_q(j)), h)
    return pl.pallas_call(
        body, name="mla_bwd", grid=grid,
        in_specs=[pl.BlockSpec((None, tq, QK_PAD), lambda h, j, i: (h, jnp.maximum(i, first_q(j)), 0)),
                  pl.BlockSpec((None, tk, QK_PAD), lambda h, j, i: (h, j, 0)),
                  pl.BlockSpec((tk, V_DIM), lambda h, j, i: (j, h)),
                  pl.BlockSpec((tq, V_DIM), qrow), pl.BlockSpec((tq, V_DIM), qrow), pl.BlockSpec((tq, V_DIM), qrow)]
        + c_in_specs,
        out_specs=[pl.BlockSpec((None, n_rows, QK_PAD), lambda h, j, i: (h, 0, 0)),
                   pl.BlockSpec((None, tk, QK_PAD), lambda h, j, i: (h, j, 0)),
                   pl.BlockSpec((tk, V_DIM), lambda h, j, i: (j, h))] + c_out_specs,
        out_shape=[_sds((MLA_HEADS, n_rows, QK_PAD)), _sds((MLA_HEADS, n_rows, QK_PAD)), _sds((n_rows, MLA_WIDTH))] + c_outs,
        scratch_shapes=[pltpu.VMEM((tk, QK_PAD), F32), pltpu.VMEM((tk, V_DIM), F32)] + (comm.sem_scratch() if comm else []),
        compiler_params=_params(("arbitrary",) * 3 if comm else ("parallel", "arbitrary", "arbitrary")),
    )(qf, kf, v, lse, do, delta, *c_ins)


def _mla_bwd_prep(dqf, dkf, dv, cos, sin):
    def fn(dqf_v, dkf_v, dv_v, cos_v, sin_v):
        dq = jnp.concatenate([dqf_v[h][:, :QK_NOPE] for h in range(MLA_HEADS)]
                             + [_rope_t(dqf_v[h][:, QK_NOPE:], cos_v[:, :LANES], sin_v[:, :LANES]) for h in range(MLA_HEADS)], axis=1)
        dkv = jnp.concatenate([dkf_v[h][:, :QK_NOPE] for h in range(MLA_HEADS)] + [dv_v], axis=1)
        dkpe = dkf_v[0][:, QK_NOPE:] + dkf_v[1][:, QK_NOPE:] + dkf_v[2][:, QK_NOPE:] + dkf_v[3][:, QK_NOPE:]
        return dq, dkv, dkpe

    n_rows = dv.shape[0]
    return _rowwise("mla_bwd_prep", fn, [dqf, dkf, dv, cos, sin],
                    [_sds((n_rows, UQ_PAD), BF16), _sds((n_rows, 1024), BF16), _sds((n_rows, LANES))], ts=256)


def _assemble_dh(dqkvs, dcqn, cq, dckvn, ckv, dkpe, cos, sin, gq, gkv):
    def fn(g1, g2, g3, dcqn_v, cq_v, dckvn_v, ckv_v, dkpe_v, cos_v, sin_v, gq_v, gkv_v):
        g = g1 + g2 + g3
        dqa = _rope_t(g[:, :A_WIDTH], cos_v, sin_v) * A_SCALE
        dka = _rope_t(g[:, A_WIDTH:2 * A_WIDTH], cos_v, sin_v)
        dcq, dgq = _rms_bwd(cq_v, gq_v, dcqn_v)
        dckv, dgkv = _rms_bwd(ckv_v, gkv_v, dckvn_v)
        dkr = _rope_t(dkpe_v, cos_v[:, :LANES], sin_v[:, :LANES])
        return jnp.concatenate([dqa, dka, g[:, 2 * A_WIDTH:], dcq, dkr, dckv], axis=1), dgq, dgkv

    n_rows = cq.shape[0]
    return _rowwise("assemble_dh", fn, [*dqkvs, dcqn, cq, dckvn, ckv, dkpe, cos, sin],
                    [_sds((n_rows, IN_PAD), BF16)], consts=[gq, gkv],
                    reds=[_sds((1, Q_LORA)), _sds((1, KV_LORA))], ts=256,
                    in_dils=[dil for _, dil in PATTERNS] + [1] * 7)


def _layer_fwd(x_f32, x_bf, w, sm, cos, sin, behind=None):
    *qkv_views, cq, kpe, ckv = _in_proj(x_bf, w["w_in"], cos, sin)
    ol_views = [_dil_fwd(view, dil) for view, (_, dil) in zip(qkv_views, PATTERNS)]
    cqn, qf = _mla_q_prep(cq, sm["q_a_norm"], w["w_uq"], cos, sin)
    ckvn, kf, v = _mla_kv_prep(ckv, kpe, sm["kv_a_norm"], w["w_ukv"])
    b_out, b_lse, *behind_outs = _mla_fwd(qf, kf, v, comm=behind)
    a_out, mixed = _mix_fwd(ol_views, b_out, sm["a_out_norm"], sm["b_out_norm"])
    x1, x1_bf, r1 = _mm_res_ln("wo_ln1", mixed, w["w_o"], x_f32, sm["ln1_g"], sm["ln1_b"])
    f = _mm_relu2(x1_bf, w["w_ff1"])
    x2, x2_bf, r2 = _mm_res_ln("ff2_ln2", f, w["w_ff2"], x1, sm["ln2_g"], sm["ln2_b"])
    saved = dict(x_bf=x_bf, qkv_views=qkv_views, ol_views=ol_views, cq=cq, ckv=ckv, cqn=cqn, qf=qf, ckvn=ckvn,
                 kf=kf, v=v, b_out=b_out, b_lse=b_lse, a_out=a_out, mixed=mixed, x1_bf=x1_bf, r1=r1, f=f, r2=r2)
    return x2, x2_bf, saved, behind_outs


def _layer_bwd(dx2, w, sm, sv, cos, sin, pending=None):
    dr2, dr2_bf, dg2, db2 = _ln_bwd("ln2_bwd", dx2, sv["r2"], sm["ln2_g"])
    (dw_ff2,) = _mm("dw_ff2", sv["f"], dr2_bf, (F32,), ta=True, tm=1024, tn=1024, tk=2048)
    du, *swapped = _mm("d_u", dr2_bf, w["w_ff2"], (BF16,), tb=True, tm=1024, tn=2048, tk=D_MODEL,
                       epilogue=lambda acc, f: (acc * (2.0 * jnp.sqrt(f.astype(F32))),), extras=[sv["f"]],
                       comm=pending.swap_comm() if pending else None)
    if pending:
        pending.after_swap(swapped)
    (dw_ff1,) = _mm("dw_ff1", sv["x1_bf"], du, (F32,), ta=True, tm=1024, tn=1024, tk=2048)
    (dy1,) = _mm("d_x1", du, w["w_ff1"], (F32,), tb=True, tm=1024, tn=1024, tk=2048,
                 epilogue=lambda acc, d: (acc + ALPHA * d,), extras=[dr2])
    dr1, dr1_bf, dg1, db1 = _ln_bwd("ln1_bwd", dy1, sv["r1"], sm["ln1_g"])
    (dw_o,) = _mm("dw_o", sv["mixed"], dr1_bf, (F32,), ta=True, tm=1024, tn=1024, tk=2048)
    (dmixed,) = _mm("d_mixed", dr1_bf, w["w_o"], (F32,), tb=True, tm=1024, tn=1024, tk=D_MODEL)
    dod1, dod2, dod3, do_b, delta_b, dga, dgb = _mix_bwd(
        dmixed, sv["a_out"], sv["b_out"], sv["ol_views"], sm["a_out_norm"], sm["b_out_norm"])
    dqkvs = [_dil_bwd(qkv_view, ol_view, dod_view, dil)
             for (_, dil), qkv_view, ol_view, dod_view in zip(PATTERNS, sv["qkv_views"], sv["ol_views"], (dod1, dod2, dod3))]
    dqf, dkf, dv_b, *scattered = _mla_bwd(sv["qf"], sv["kf"], sv["v"], sv["b_lse"], do_b, delta_b,
                                          comm=pending.scatter_comm() if pending else None)
    if pending:
        pending.after_scatter(scattered)
    dq_b, dkv_b, dkpe = _mla_bwd_prep(dqf, dkf, dv_b, cos, sin)
    (dw_uq,) = _mm("dw_uq", sv["cqn"], dq_b, (F32,), ta=True, tm=Q_LORA, tn=1024, tk=2048)
    (dcqn,) = _mm("d_cqn", dq_b, w["w_uq"], (F32,), tb=True, tm=1024, tn=Q_LORA, tk=UQ_PAD)
    (dw_ukv,) = _mm("dw_ukv", sv["ckvn"], dkv_b, (F32,), ta=True, tm=KV_LORA, tn=1024, tk=2048)
    (dckvn,) = _mm("d_ckvn", dkv_b, w["w_ukv"], (F32,), tb=True, tm=1024, tn=KV_LORA, tk=1024)
    dh, dgq, dgkv = _assemble_dh(dqkvs, dcqn, sv["cq"], dckvn, sv["ckv"], dkpe, cos, sin,
                                 sm["q_a_norm"], sm["kv_a_norm"])
    (dw_in,) = _mm("dw_in", sv["x_bf"], dh, (F32,), ta=True, tm=1024, tn=1024, tk=2048)
    (dx,) = _mm("d_x", dh, w["w_in"], (F32,), tb=True, tm=1024, tn=1024, tk=2048,
                epilogue=lambda acc, d: (acc + ALPHA * d,), extras=[dr1])
    dws = dict(w_in=dw_in, w_uq=dw_uq, w_ukv=dw_ukv, w_o=dw_o, w_ff1=dw_ff1, w_ff2=dw_ff2)
    dsm = dict(q_a_norm=dgq, kv_a_norm=dgkv, a_out_norm=dga, b_out_norm=dgb, ln1_g=dg1, ln1_b=db1, ln2_g=dg2, ln2_b=db2)
    return dx, dws, dsm


def _pad_w_in(w):
    return jnp.concatenate([w[:, :1792], w[:, 1920:1984], jnp.zeros((w.shape[0], 64), w.dtype), w[:, 1792:1920]], axis=1)


def _unpad_w_in(w):
    return jnp.concatenate([w[:, :1792], w[:, 1920:2048], w[:, 1792:1856]], axis=1)


def _rope_tables(n_rows):
    half = A_HEAD_DIM // 2
    inv_freq = ROPE_THETA ** (-jnp.arange(half, dtype=F32) / half)
    ang = jnp.arange(n_rows, dtype=F32)[:, None] * inv_freq[None, :]
    cos = jnp.tile(jnp.cos(ang), (1, 2 * A_HEADS))
    sin = jnp.tile(jnp.concatenate([-jnp.sin(ang), jnp.sin(ang)], axis=1), (1, A_HEADS))
    return cos, sin


W_NAMES = ("w_in", "w_uq", "w_ukv", "w_o", "w_ff1", "w_ff2")
PIECE = dict(w_in=(512, 512), w_uq=(128, 256), w_ukv=(64, 256), w_o=(128, 1024), w_ff1=(512, 1024), w_ff2=(512, 1024))
SHARD_COLS = dict(w_in=496, w_uq=192, w_ukv=256, w_o=1024, w_ff1=1024, w_ff2=1024)


def _col_pieces(t, pad_to):
    rows, _, cols = t.shape
    t = jnp.pad(t, ((0, 0), (0, 0), (0, pad_to - cols)))
    return t.reshape(2, rows // 2, N_CHIPS, pad_to).transpose(2, 0, 1, 3)


def _grad_pieces(dws):
    uq, ukv = dws["w_uq"], dws["w_ukv"]
    uq = jnp.concatenate([uq[:, :512].reshape(Q_LORA, MLA_HEADS, LANES), uq[:, 512:].reshape(Q_LORA, MLA_HEADS, LANES)], axis=2)
    ukv = jnp.concatenate([ukv[:, :512].reshape(KV_LORA, MLA_HEADS, LANES), ukv[:, 512:].reshape(KV_LORA, MLA_HEADS, LANES)], axis=2)
    return dict(
        w_in=_col_pieces(_unpad_w_in(dws["w_in"]).reshape(D_MODEL, N_CHIPS, SHARD_COLS["w_in"]), 512),
        w_uq=_col_pieces(uq, 256), w_ukv=_col_pieces(ukv, 256),
        w_o=dws["w_o"].reshape(N_CHIPS, 2, 128, D_MODEL),
        w_ff1=_col_pieces(dws["w_ff1"].reshape(D_MODEL, N_CHIPS, 1024), 1024),
        w_ff2=dws["w_ff2"].reshape(N_CHIPS, 2, 512, D_MODEL))


def _weights_from_pieces(p):
    def cols(t):
        return t.transpose(1, 2, 0, 3).reshape(2 * t.shape[2], N_CHIPS, t.shape[3])

    uq, ukv = cols(p["w_uq"]), cols(p["w_ukv"])
    return dict(
        w_in=_pad_w_in(cols(p["w_in"])[:, :, :SHARD_COLS["w_in"]].reshape(D_MODEL, IN_COLS)),
        w_uq=jnp.concatenate([uq[:, :, :LANES].reshape(Q_LORA, 512), uq[:, :, LANES:].reshape(Q_LORA, 512)], axis=1),
        w_ukv=jnp.concatenate([ukv[:, :, :LANES].reshape(KV_LORA, 512), ukv[:, :, LANES:].reshape(KV_LORA, 512)], axis=1),
        w_o=p["w_o"].reshape(D_MODEL, D_MODEL),
        w_ff1=cols(p["w_ff1"]).reshape(D_MODEL, D_FF),
        w_ff2=p["w_ff2"].reshape(D_FF, D_MODEL))


def _my_piece(name, shard, half):
    rows, cols = shard.shape[0] // 2, shard.shape[1]
    t = lax.dynamic_slice_in_dim(shard, half * rows, rows, axis=0).astype(BF16)
    return jnp.pad(t, ((0, 0), (0, PIECE[name][1] - cols)))


MESH = pl.DeviceIdType.MESH
ANY = pl.BlockSpec(memory_space=pl.ANY)


class _Comm:
    def __init__(self, ins, out_shapes, n_sems, start, finish):
        self.ins, self.out_shapes, self.n_sems, self.start, self.finish = list(ins), list(out_shapes), n_sems, start, finish

    def sem_scratch(self):
        return [pltpu.SemaphoreType.DMA((self.n_sems,)), pltpu.SemaphoreType.DMA((self.n_sems,))]


def _comm_call(name, comm):
    n_in, n_out = len(comm.ins), len(comm.out_shapes)

    def body(*refs):
        ins, outs, sems = refs[:n_in], refs[n_in:n_in + n_out], refs[n_in + n_out:]
        comm.start(ins, outs, *sems)
        comm.finish(ins, outs, *sems)

    return pl.pallas_call(body, name=name, out_shape=comm.out_shapes, in_specs=[ANY] * n_in, out_specs=[ANY] * n_out,
                          scratch_shapes=comm.sem_scratch())(*comm.ins)


def _all_gather_comm(blocks):
    n = len(blocks)

    def plan(x_refs, out_refs, send_sems, recv_sems):
        x, y, c = lax.axis_index("x"), lax.axis_index("y"), lax.axis_index("c")
        me, sibling = (x, y, c), (x, y, 1 - c)
        chips = [(1 - x, y), (x, 1 - y), (1 - x, 1 - y)]

        def copy(t, k, blk, to, src=None):
            px, py, pc = blk
            slot = out_refs[t].at[4 * px + 2 * py + pc]
            return pltpu.make_async_remote_copy(
                src_ref=slot if src is None else src, dst_ref=slot,
                send_sem=send_sems.at[7 * t + k], recv_sem=recv_sems.at[7 * t + k], device_id=to, device_id_type=MESH)

        first = []
        for t in range(n):
            first.append(copy(t, 0, me, sibling, src=x_refs[t]))
            first += [copy(t, 1 + j, me, (*chip, c), src=x_refs[t]) for j, chip in enumerate(chips)]
        return me, sibling, chips, c, copy, first

    def start(x_refs, out_refs, send_sems, recv_sems):
        for cp in plan(x_refs, out_refs, send_sems, recv_sems)[-1]:
            cp.start()

    def finish(x_refs, out_refs, send_sems, recv_sems):
        me, sibling, chips, c, copy, sent = plan(x_refs, out_refs, send_sems, recv_sems)
        for j, chip in enumerate(chips):
            for t in range(n):
                copy(t, 1 + j, (*chip, c), me).wait_recv()
                sent.append(copy(t, 4 + j, (*chip, c), sibling))
                sent[-1].start()
        for t in range(n):
            copy(t, 0, sibling, me).wait_recv()
        for j, chip in enumerate(chips):
            for t in range(n):
                copy(t, 4 + j, (*chip, 1 - c), me).wait_recv()
        for cp in sent:
            cp.wait_send()

    return _Comm(blocks, [_sds((N_DEV,) + b.shape, b.dtype) for b in blocks], 7 * n, start, finish)


def _simple_comm(ins, out_shapes, n_sems, copies):
    def start(in_refs, out_refs, send_sems, recv_sems):
        for cp in copies(in_refs, out_refs, send_sems, recv_sems):
            cp.start()

    def finish(in_refs, out_refs, send_sems, recv_sems):
        for cp in copies(in_refs, out_refs, send_sems, recv_sems):
            cp.wait()

    return _Comm(ins, out_shapes, n_sems, start, finish)


def _swap_comm(gs):
    def copies(g_refs, got_refs, send_sems, recv_sems):
        c = lax.axis_index("c")
        sibling = (lax.axis_index("x"), lax.axis_index("y"), 1 - c)
        return [pltpu.make_async_remote_copy(src_ref=g_refs[t].at[k, 1 - c], dst_ref=got_refs[t].at[k],
                                             send_sem=send_sems.at[N_CHIPS * t + k], recv_sem=recv_sems.at[N_CHIPS * t + k],
                                             device_id=sibling, device_id_type=MESH)
                for t in range(len(gs)) for k in range(N_CHIPS)]

    return _simple_comm(gs, [_sds((N_CHIPS,) + g.shape[2:], g.dtype) for g in gs], N_CHIPS * len(gs), copies)


def _scatter_comm(hs):
    def copies(h_refs, got_refs, send_sems, recv_sems):
        x, y, c = lax.axis_index("x"), lax.axis_index("y"), lax.axis_index("c")
        chips = [(1 - x, y), (x, 1 - y), (1 - x, 1 - y)]
        return [pltpu.make_async_remote_copy(src_ref=h_refs[t].at[2 * px + py], dst_ref=got_refs[t].at[rel],
                                             send_sem=send_sems.at[3 * t + rel], recv_sem=recv_sems.at[3 * t + rel],
                                             device_id=(px, py, c), device_id_type=MESH)
                for t in range(len(hs)) for rel, (px, py) in enumerate(chips)]

    return _simple_comm(hs, [_sds((3,) + h.shape[1:], h.dtype) for h in hs], 3 * len(hs), copies)


def _share_comm(rs):
    def copies(r_refs, got_refs, send_sems, recv_sems):
        sibling = (lax.axis_index("x"), lax.axis_index("y"), 1 - lax.axis_index("c"))
        return [pltpu.make_async_remote_copy(src_ref=r_refs[t], dst_ref=got_refs[t], send_sem=send_sems.at[t],
                                             recv_sem=recv_sems.at[t], device_id=sibling, device_id_type=MESH)
                for t in range(len(rs))]

    return _simple_comm(rs, [_sds(r.shape, r.dtype) for r in rs], len(rs), copies)


SUM_STEPS = 4


def _pair_sums(name, gs, gots, c_arr):
    n = len(gs)
    dims = [(g.shape[2] // SUM_STEPS, g.shape[3]) for g in gs]

    def body(c_ref, *refs):
        for t in range(n):
            s = refs[t][...] + refs[n + t][...]
            refs[2 * n + t][...] = s
            refs[3 * n + t][...] = s.astype(BF16)

    def part(tr, pc):
        return pl.BlockSpec((None, tr, pc), lambda k, i, c_ref: (k, i, 0))

    def kept(tr, pc):
        return pl.BlockSpec((None, None, tr, pc), lambda k, i, c_ref: (k, c_ref[0], i, 0))

    outs = pl.pallas_call(
        body, name=name,
        out_shape=[_sds((N_CHIPS,) + g.shape[2:]) for g in gs] + [_sds((N_CHIPS,) + g.shape[2:], BF16) for g in gs],
        grid_spec=pltpu.PrefetchScalarGridSpec(
            num_scalar_prefetch=1, grid=(N_CHIPS, SUM_STEPS),
            in_specs=[kept(*d) for d in dims] + [part(*d) for d in dims],
            out_specs=[part(*d) for d in dims] * 2),
        compiler_params=_params(("parallel", "parallel")),
    )(c_arr, *gs, *gots)
    return outs[:n], outs[n:]


def _chips_sums(name, h32s, gots, chip_arr):
    n = len(h32s)
    dims = [(h.shape[1] // SUM_STEPS, h.shape[2]) for h in h32s]

    def body(chip_ref, *refs):
        for t in range(n):
            got_ref = refs[n + t]
            refs[2 * n + t][...] = refs[t][...] + got_ref[0] + got_ref[1] + got_ref[2]

    return pl.pallas_call(
        body, name=name, out_shape=[_sds(h.shape[1:]) for h in h32s],
        grid_spec=pltpu.PrefetchScalarGridSpec(
            num_scalar_prefetch=1, grid=(SUM_STEPS,),
            in_specs=[pl.BlockSpec((None, tr, pc), lambda i, chip_ref: (chip_ref[0], i, 0)) for tr, pc in dims]
            + [pl.BlockSpec((3, tr, pc), lambda i, chip_ref: (0, i, 0)) for tr, pc in dims],
            out_specs=[pl.BlockSpec((tr, pc), lambda i, chip_ref: (i, 0)) for tr, pc in dims]),
        compiler_params=_params(("parallel",)),
    )(chip_arr, *h32s, *gots)


class _LayerReduce:
    def __init__(self, tag, dws, c, chip):
        self.tag, self.c, self.gp = tag, c, _grad_pieces(dws)
        self.c_arr, self.chip_arr = jnp.reshape(c, (1,)).astype(jnp.int32), jnp.reshape(chip, (1,)).astype(jnp.int32)
        self.shards = None

    def swap_comm(self):
        return _swap_comm([self.gp[n] for n in W_NAMES])

    def after_swap(self, gots):
        self.s32, self.s16 = _pair_sums(f"rs_pair_sum_{self.tag}", [self.gp[n] for n in W_NAMES], gots, self.c_arr)

    def scatter_comm(self):
        return _scatter_comm(self.s16)

    def after_scatter(self, parts):
        mine = _chips_sums(f"rs_chips_sum_{self.tag}", self.s32, parts, self.chip_arr)
        theirs = _comm_call(f"rs_share_{self.tag}", _share_comm(mine))
        self.shards = {}
        for n, a, b in zip(W_NAMES, mine, theirs):
            both = jnp.where(self.c == 0, jnp.concatenate([a, b], axis=0), jnp.concatenate([b, a], axis=0))
            self.shards[n] = both[:, :SHARD_COLS[n]]

    def run_alone(self):
        self.after_swap(_comm_call(f"rs_pair_{self.tag}", self.swap_comm()))
        self.after_scatter(_comm_call(f"rs_chips_{self.tag}", self.scatter_comm()))


class _LayerWeights:
    def __init__(self, shards, c, dev):
        self.shards, self.c, self.dev = shards, c, dev

    def comm(self, l):
        self.blocks = [_my_piece(n, self.shards[l][n], self.c) for n in W_NAMES]
        return _all_gather_comm(self.blocks)

    def take(self, l, gathered):
        pieces = {n: lax.dynamic_update_index_in_dim(g, b, self.dev, 0).reshape((N_CHIPS, 2) + b.shape)
                  for n, g, b in zip(W_NAMES, gathered, self.blocks)}
        return _weights_from_pieces(pieces)

    def first(self):
        return self.take(0, _comm_call("ag_l0", self.comm(0)))


def _all_reduce_small(vec):
    rows, lanes = vec.shape

    def gather_body(x_ref, out_ref, send_sems, recv_sems):
        x, y, c = lax.axis_index("x"), lax.axis_index("y"), lax.axis_index("c")
        me, sibling = (x, y, c), (x, y, 1 - c)
        chips = [(1 - x, y), (x, 1 - y), (1 - x, 1 - y)]

        def slot(px, py, pc):
            return out_ref.at[4 * px + 2 * py + pc]

        def copy(k, blk, to, src=None):
            return pltpu.make_async_remote_copy(
                src_ref=slot(*blk) if src is None else src, dst_ref=slot(*blk),
                send_sem=send_sems.at[k], recv_sem=recv_sems.at[k], device_id=to, device_id_type=MESH)

        out_ref[4 * x + 2 * y + c] = x_ref[...]
        first = [copy(0, me, sibling, src=x_ref)]
        first += [copy(1 + j, me, (*chip, c), src=x_ref) for j, chip in enumerate(chips)]
        for cp in first:
            cp.start()
        passed = [copy(4 + j, (*chip, c), sibling) for j, chip in enumerate(chips)]
        for j, chip in enumerate(chips):
            copy(1 + j, (*chip, c), me).wait_recv()
            passed[j].start()
        copy(0, sibling, me).wait_recv()
        for j, chip in enumerate(chips):
            copy(4 + j, (*chip, 1 - c), me).wait_recv()
        for cp in first + passed:
            cp.wait_send()

    vmem = pl.BlockSpec(memory_space=pltpu.VMEM)
    allv = pl.pallas_call(
        gather_body, name="small_all_gather", out_shape=_sds((N_DEV, rows, lanes)),
        in_specs=[vmem], out_specs=vmem,
        scratch_shapes=[pltpu.SemaphoreType.DMA((7,)), pltpu.SemaphoreType.DMA((7,))],
    )(vec)

    def sum_body(a_ref, o_ref):
        acc = a_ref[0]
        for d in range(1, N_DEV):
            acc = acc + a_ref[d]
        o_ref[...] = acc

    return pl.pallas_call(sum_body, name="small_sum", out_shape=_sds((rows, lanes)), in_specs=[vmem], out_specs=vmem)(allv)


def _adamw(name, w, g, m, v, ts=512):
    def fn(w_v, g_v, m_v, v_v):
        m_n = ADAM_B1 * m_v + (1.0 - ADAM_B1) * g_v
        v_n = ADAM_B2 * v_v + (1.0 - ADAM_B2) * (g_v * g_v)
        m_hat = m_n / (1.0 - ADAM_B1 ** ADAM_STEP)
        v_hat = v_n / (1.0 - ADAM_B2 ** ADAM_STEP)
        delta = -ADAM_LR * (m_hat / (jnp.sqrt(v_hat) + ADAM_EPS) + ADAM_WD * w_v)
        return delta, m_n, v_n

    return _rowwise(name, fn, [w, g, m, v], [_sds(w.shape)] * 3, ts=ts)


def _pack_small(per_layer):
    flat = jnp.concatenate([per_layer[l][n].reshape(-1) for l in range(DEPTH) for n in SMALL_NAMES])
    return jnp.pad(flat, (0, SMALL_ROWS * LANES - flat.shape[0])).reshape(SMALL_ROWS, LANES)


def _unpack_small(packed):
    flat = packed.reshape(-1)
    per = sum(SMALL_SIZES)
    out = {}
    for n, size, off in zip(SMALL_NAMES, SMALL_SIZES, [sum(SMALL_SIZES[:i]) for i in range(len(SMALL_SIZES))]):
        out[n] = jnp.stack([flat[l * per + off:l * per + off + size] for l in range(DEPTH)])
    return out


def _fwd_bwd(x, target, layer_weights, smalls, on_layer_grads):
    depth = len(smalls)
    cos, sin = _rope_tables(x.shape[0])
    h_f32, h_bf = x, x.astype(BF16)
    saved, weights = [], [layer_weights.first()]
    for l in range(depth):
        behind = layer_weights.comm(l + 1) if l + 1 < depth else None
        h_f32, h_bf, sv, brought = _layer_fwd(h_f32, h_bf, weights[l], smalls[l], cos, sin, behind)
        saved.append(sv)
        if l + 1 < depth:
            weights.append(layer_weights.take(l + 1, brought))
    loss_part, dh = _loss_fn(h_f32, target)
    small_grads, pending = [None] * depth, None
    for l in reversed(range(depth)):
        dh, dws, small_grads[l] = _layer_bwd(dh, weights[l], smalls[l], saved[l], cos, sin, pending)
        pending = on_layer_grads(l, dws)
    return loss_part, dh, small_grads, pending


def kernel(x, w_in, q_a_norm, kv_a_norm, w_uq, w_ukv, a_out_norm, b_out_norm, w_o, ln1_g, ln1_b, w_ff1, w_ff2, ln2_g, ln2_b, loss_target, m_w_in, m_q_a_norm, m_kv_a_norm, m_w_uq, m_w_ukv, m_a_out_norm, m_b_out_norm, m_w_o, m_ln1_g, m_ln1_b, m_w_ff1, m_w_ff2, m_ln2_g, m_ln2_b, v_w_in, v_q_a_norm, v_kv_a_norm, v_w_uq, v_w_ukv, v_a_out_norm, v_b_out_norm, v_w_o, v_ln1_g, v_ln1_b, v_w_ff1, v_w_ff2, v_ln2_g, v_ln2_b):
    c = lax.axis_index("c")
    chip = 2 * lax.axis_index("x") + lax.axis_index("y")
    big = dict(w_in=w_in, w_uq=w_uq, w_ukv=w_ukv, w_o=w_o, w_ff1=w_ff1, w_ff2=w_ff2)
    big_m = dict(w_in=m_w_in, w_uq=m_w_uq, w_ukv=m_w_ukv, w_o=m_w_o, w_ff1=m_w_ff1, w_ff2=m_w_ff2)
    big_v = dict(w_in=v_w_in, w_uq=v_w_uq, w_ukv=v_w_ukv, w_o=v_w_o, w_ff1=v_w_ff1, w_ff2=v_w_ff2)
    small = dict(q_a_norm=q_a_norm, kv_a_norm=kv_a_norm, a_out_norm=a_out_norm, b_out_norm=b_out_norm,
                 ln1_g=ln1_g, ln1_b=ln1_b, ln2_g=ln2_g, ln2_b=ln2_b)
    small_m = dict(q_a_norm=m_q_a_norm, kv_a_norm=m_kv_a_norm, a_out_norm=m_a_out_norm, b_out_norm=m_b_out_norm,
                   ln1_g=m_ln1_g, ln1_b=m_ln1_b, ln2_g=m_ln2_g, ln2_b=m_ln2_b)
    small_v = dict(q_a_norm=v_q_a_norm, kv_a_norm=v_kv_a_norm, a_out_norm=v_a_out_norm, b_out_norm=v_b_out_norm,
                   ln1_g=v_ln1_g, ln1_b=v_ln1_b, ln2_g=v_ln2_g, ln2_b=v_ln2_b)

    layer_weights = _LayerWeights([{n: big[n][l] for n in W_NAMES} for l in range(DEPTH)], c, 2 * chip + c)
    smalls = [{n: small[n][l][None, :] for n in SMALL_NAMES} for l in range(DEPTH)]
    reductions = [None] * DEPTH

    def reduce_layer(l, dws):
        reductions[l] = _LayerReduce(f"l{l}", dws, c, chip)
        return reductions[l]

    loss_part, grad_x, small_grads, last = _fwd_bwd(x[0], loss_target[0], layer_weights, smalls, reduce_layer)
    last.run_alone()
    shard_grads = [r.shards for r in reductions]
    loss = lax.psum(loss_part[0, 0], ("x", "y", "c"))
    grad_x = grad_x[None]

    g_small_packed = _all_reduce_small(_pack_small(small_grads))
    pack_in = lambda d: _pack_small([{n: d[n][l] for n in SMALL_NAMES} for l in range(DEPTH)])
    sd, sm_, sv_ = _adamw("adamw_small", pack_in(small), g_small_packed, pack_in(small_m), pack_in(small_v), ts=SMALL_ROWS)
    g_small, d_small, m_small, v_small = (_unpack_small(t) for t in (g_small_packed, sd, sm_, sv_))

    g_big, d_big, m_big, v_big = {}, {}, {}, {}
    for n in W_NAMES:
        g = jnp.stack([shard_grads[l][n] for l in range(DEPTH)])
        shape = g.shape
        flat = lambda t: t.reshape(shape[0] * shape[1], shape[2])
        d_, m_, v_ = _adamw(f"adamw_{n}", flat(big[n]), flat(g), flat(big_m[n]), flat(big_v[n]))
        g_big[n], d_big[n], m_big[n], v_big[n] = g, d_.reshape(shape), m_.reshape(shape), v_.reshape(shape)

    order = ("w_in", "q_a_norm", "kv_a_norm", "w_uq", "w_ukv", "a_out_norm", "b_out_norm", "w_o", "ln1_g", "ln1_b",
             "w_ff1", "w_ff2", "ln2_g", "ln2_b")
    pick = lambda bigd, smalld: [bigd[n] if n in bigd else smalld[n] for n in order]
    return (loss, grad_x, *pick(g_big, g_small), *pick(d_big, d_small), *pick(m_big, m_small), *pick(v_big, v_small))
```

```python
import math

import jax
import jax.numpy as jnp
from jax import lax
from jax.experimental import pallas as pl
from jax.experimental.pallas import tpu as pltpu

F32, BF16 = jnp.float32, jnp.bfloat16

D_MODEL = 1024
DEPTH = 4
A_HEAD_DIM = 64
A_HEADS = 8
A_WIDTH = 512
PATTERNS = ((128, 1), (512, 4), (2048, 16))
SPAN = 128
MLA_HEADS = 4
QK_NOPE = 128
QK_ROPE = 64
V_DIM = 128
Q_LORA = 256
KV_LORA = 128
MLA_WIDTH = 512
QK_PAD = 256
IN_COLS = 1984
IN_PAD = 2048
UQ_PAD = 1024
D_FF = 4096
ROPE_THETA = 10000.0
ALPHA = (2.0 * DEPTH) ** 0.25
LN_EPS = 1e-5
RMS_EPS = 1e-6
MLA_SCALE = (QK_NOPE + QK_ROPE) ** -0.5
A_SCALE = A_HEAD_DIM ** -0.5

ADAM_LR, ADAM_B1, ADAM_B2, ADAM_EPS, ADAM_WD, ADAM_STEP = 0.001, 0.9, 0.999, 1e-08, 0.01, 10

VMEM_LIMIT_BYTES = 56 * 1024 * 1024
NEG = -1e30
N_CHIPS, N_DEV = 4, 8
LANES = 128

SMALL_NAMES = ("q_a_norm", "kv_a_norm", "a_out_norm", "b_out_norm", "ln1_g", "ln1_b", "ln2_g", "ln2_b")
SMALL_SIZES = (256, 128, 512, 512, 1024, 1024, 1024, 1024)
SMALL_ROWS = 176


def _params(sem):
    return pltpu.CompilerParams(dimension_semantics=sem, vmem_limit_bytes=VMEM_LIMIT_BYTES)


def _interleave(parts, scratch):
    d, (n, width) = len(parts), parts[0].shape
    for r in range(d):
        for s in range(width // LANES):
            scratch.at[s][pl.ds(r, n, stride=d), :] = parts[r][:, s * LANES:(s + 1) * LANES]
    return jnp.concatenate([scratch[s] for s in range(width // LANES)], axis=1)


def _deinterleave(x, d, scratch):
    n, width = x.shape[0] // d, x.shape[1]
    for s in range(width // LANES):
        scratch[s] = x[:, s * LANES:(s + 1) * LANES]
    return [jnp.concatenate([scratch.at[s][pl.ds(r, n, stride=d), :] for s in range(width // LANES)], axis=1) for r in range(d)]


def _rowwise(name, fn, ins, outs, consts=(), reds=(), ts=512, in_dils=None, out_dils=None):
    in_dils = list(in_dils or [1] * len(ins))
    out_dils = list(out_dils or [1] * len(outs))
    n_rows = ins[0].shape[-2] * in_dils[0]
    ts = min(ts, n_rows)
    assert n_rows % ts == 0
    n_in, n_c, n_o, n_r = len(ins), len(consts), len(outs), len(reds)
    viewed = [(a.shape, d) for a, d in zip(list(ins) + list(outs), in_dils + out_dils) if d > 1]

    def tile_spec(shape, d=1):
        if len(shape) == 2:
            return pl.BlockSpec((ts // d, shape[1]), lambda i: (i, 0))
        return pl.BlockSpec((shape[0], ts, shape[2]), lambda i: (0, i, 0))

    def whole_spec(shape):
        return pl.BlockSpec(shape, lambda i: (0,) * len(shape))

    def body(*refs):
        in_refs, c_refs = refs[:n_in], refs[n_in:n_in + n_c]
        o_refs = refs[n_in + n_c:n_in + n_c + n_o]
        r_refs = refs[n_in + n_c + n_o:n_in + n_c + n_o + n_r]
        scratches = list(refs[n_in + n_c + n_o + n_r:])
        vals = []
        for r, d in zip(in_refs, in_dils):
            if d == 1:
                vals.append(r[...])
            else:
                width = r.shape[1] // d
                vals.append(_interleave([r[:, k * width:(k + 1) * width] for k in range(d)], scratches.pop(0)))
        res = fn(*vals, *[r[...] for r in c_refs])
        res = tuple(res) if isinstance(res, (tuple, list)) else (res,)
        for r, v, d in zip(o_refs, res[:n_o], out_dils):
            if len(r.shape) == 3:
                for g in range(r.shape[0]):
                    r[g] = v[g].astype(r.dtype)
            elif d == 1:
                r[...] = v.astype(r.dtype)
            else:
                width = r.shape[1] // d
                for k, part in enumerate(_deinterleave(v, d, scratches.pop(0))):
                    r[:, k * width:(k + 1) * width] = part.astype(r.dtype)
        if n_r:
            i = pl.program_id(0)

            @pl.when(i == 0)
            def _():
                for r, v in zip(r_refs, res[n_o:]):
                    r[...] = v

            @pl.when(i > 0)
            def _():
                for r, v in zip(r_refs, res[n_o:]):
                    r[...] += v

    out = pl.pallas_call(
        body, name=name, grid=(n_rows // ts,),
        in_specs=[tile_spec(a.shape, d) for a, d in zip(ins, in_dils)] + [whole_spec(c.shape) for c in consts],
        out_specs=[tile_spec(o.shape, d) for o, d in zip(outs, out_dils)] + [whole_spec(r.shape) for r in reds],
        out_shape=list(outs) + list(reds),
        scratch_shapes=[pltpu.VMEM((shape[1] // d // LANES, ts, LANES), F32) for shape, d in viewed],
        compiler_params=_params(("arbitrary",)),
    )(*ins, *consts)
    return out


def _sds(shape, dtype=F32):
    return jax.ShapeDtypeStruct(tuple(shape), dtype)


def _hosted(comm, grid):
    if comm is None:
        return [], [], [], [], (lambda *a: None), (lambda *a: None)

    def edge(which, at, ins, outs, sems):
        ids = [pl.program_id(d) for d in range(len(grid))]
        hit = ids[0] == at[0]
        for i, v in zip(ids[1:], at[1:]):
            hit = hit & (i == v)

        @pl.when(hit)
        def _():
            which(ins, outs, *sems)

    begin = lambda ins, outs, sems: edge(comm.start, [0] * len(grid), ins, outs, sems)
    end = lambda ins, outs, sems: edge(comm.finish, [g - 1 for g in grid], ins, outs, sems)
    return comm.ins, [ANY] * len(comm.ins), comm.out_shapes, [ANY] * len(comm.out_shapes), begin, end


def _mm(name, a, b, out_dtypes, *, ta=False, tb=False, tm=512, tn=512, tk=512, epilogue=None, extras=(), vecs=(),
        comm=None, b_pieces=False, out_pieces=False):
    (k_dim, m_dim) = a.shape if ta else a.shape[::-1]
    if b_pieces:
        b_rows, b_cols = 2 * b.shape[2], N_CHIPS * b.shape[3]
        (n_dim, k2) = (b_rows, b_cols) if tb else (b_cols, b_rows)
    else:
        (n_dim, k2) = b.shape if tb else b.shape[::-1]
    assert k_dim == k2
    tm, tn, tk = min(tm, m_dim), min(tn, n_dim), min(tk, k_dim)
    assert m_dim % tm == 0 and n_dim % tn == 0 and k_dim % tk == 0
    nk = k_dim // tk
    grid = (m_dim // tm, n_dim // tn, nk)
    a_spec = pl.BlockSpec((tk, tm), lambda i, j, k: (k, i)) if ta else pl.BlockSpec((tm, tk), lambda i, j, k: (i, k))
    if b_pieces:
        assert (tn, tk) == (b_rows, b.shape[3]) if tb else (tk, tn) == (b_rows, b.shape[3])
        b_spec = pl.BlockSpec((None,) + b.shape[1:], (lambda i, j, k: (k, 0, 0, 0)) if tb else (lambda i, j, k: (j, 0, 0, 0)))
    else:
        b_spec = pl.BlockSpec((tn, tk), lambda i, j, k: (j, k)) if tb else pl.BlockSpec((tk, tn), lambda i, j, k: (k, j))
    assert not out_pieces or (tm == m_dim and len(out_dtypes) == 1)
    dims = (((0 if ta else 1,), (1 if tb else 0,)), ((), ()))
    n_e, n_v, n_o = len(extras), len(vecs), len(out_dtypes)
    c_ins, c_in_specs, c_outs, c_out_specs, begin, end = _hosted(comm, grid)
    n_ci, n_co = len(c_ins), len(c_outs)

    def body(a_ref, b_ref, *refs):
        e_refs, v_refs, ci_refs = refs[:n_e], refs[n_e:n_e + n_v], refs[n_e + n_v:n_e + n_v + n_ci]
        o_refs = refs[n_e + n_v + n_ci:n_e + n_v + n_ci + n_o]
        co_refs = refs[n_e + n_v + n_ci + n_o:n_e + n_v + n_ci + n_o + n_co]
        scratch = refs[n_e + n_v + n_ci + n_o + n_co:]
        sems = scratch[-2:]
        begin(ci_refs, co_refs, sems)
        b_val = b_ref[...].reshape(b_rows, b.shape[3]) if b_pieces else b_ref[...]
        part = lax.dot_general(a_ref[...].astype(BF16), b_val.astype(BF16), dims, preferred_element_type=F32)

        def finish(acc):
            outs = epilogue(acc, *[r[...] for r in e_refs], *[r[...] for r in v_refs]) if epilogue else (acc,)
            for r, v in zip(o_refs, outs):
                r[...] = (v.reshape(2, tm // 2, tn) if out_pieces else v).astype(r.dtype)

        if nk == 1:
            finish(part)
        else:
            acc_ref = scratch[0]
            k = pl.program_id(2)

            @pl.when(k == 0)
            def _():
                acc_ref[...] = part

            @pl.when(k > 0)
            def _():
                acc_ref[...] += part

            @pl.when(k == nk - 1)
            def _():
                finish(acc_ref[...])
        end(ci_refs, co_refs, sems)

    tile = pl.BlockSpec((tm, tn), lambda i, j, k: (i, j))
    if out_pieces:
        out_tile, out_shape = pl.BlockSpec((None, 2, tm // 2, tn), lambda i, j, k: (j, 0, 0, 0)), (n_dim // tn, 2, tm // 2, tn)
    else:
        out_tile, out_shape = tile, (m_dim, n_dim)
    out = pl.pallas_call(
        body, name=name, grid=grid,
        in_specs=[a_spec, b_spec] + [tile] * n_e + [pl.BlockSpec((1, tn), lambda i, j, k: (0, j))] * n_v + c_in_specs,
        out_specs=[out_tile] * n_o + c_out_specs,
        out_shape=[_sds(out_shape, d) for d in out_dtypes] + c_outs,
        scratch_shapes=([pltpu.VMEM((tm, tn), F32)] if nk > 1 else []) + (comm.sem_scratch() if comm else []),
        compiler_params=_params(("arbitrary",) * 3 if comm else ("parallel", "parallel", "arbitrary")),
    )(a, b, *extras, *vecs, *c_ins)
    return out


def _swap_halves(x):
    width = x.shape[1]
    lane = lax.broadcasted_iota(jnp.int32, x.shape, 1)
    return jnp.where((lane % 64) < 32, pltpu.roll(x, width - 32, 1), pltpu.roll(x, 32, 1))


def _rope(x, cos, sin_signed):
    return x * cos + _swap_halves(x) * sin_signed


def _rope_t(d, cos, sin_signed):
    return d * cos - _swap_halves(d) * sin_signed


def _rms(x, g):
    rstd = lax.rsqrt(jnp.mean(x * x, axis=-1, keepdims=True) + RMS_EPS)
    return x * rstd * g


def _rms_bwd(x, g, dy):
    rstd = lax.rsqrt(jnp.mean(x * x, axis=-1, keepdims=True) + RMS_EPS)
    xh = x * rstd
    dyg = dy * g
    dx = rstd * (dyg - xh * jnp.mean(dyg * xh, axis=-1, keepdims=True))
    return dx, jnp.sum(dy * xh, axis=0, keepdims=True)


def _layer_norm(r, g, b):
    mu = jnp.mean(r, axis=-1, keepdims=True)
    xc = r - mu
    var = jnp.mean(xc * xc, axis=-1, keepdims=True)
    return xc * lax.rsqrt(var + LN_EPS) * g + b


def _dot(a, b, dims):
    return lax.dot_general(a, b, (dims, ((), ())), preferred_element_type=F32)


_NN, _NT, _TN = ((1,), (0,)), ((1,), (1,)), ((0,), (0,))


def _in_proj(x_bf, w_in_p, cos, sin):
    n_rows = x_bf.shape[0]
    ts = 512
    dils = [dil for _, dil in PATTERNS]
    n_p = len(dils)

    def body(x_ref, w_ref, cos_ref, sin_ref, *refs):
        view_refs, (cq_ref, kpe_ref, ckv_ref, scratch) = refs[:n_p], refs[n_p:]
        xb = x_ref[...]
        cos_v, sin_v = cos_ref[...], sin_ref[...]

        def seg(lo, hi):
            return _dot(xb, w_ref[:, lo:hi], _NN)

        def put(part, val):
            for ref, dil in zip(view_refs, dils):
                pieces = [val] if dil == 1 else _deinterleave(val, dil, scratch)
                for r, piece in enumerate(pieces):
                    lo = (3 * r + part) * A_WIDTH
                    ref[:, lo:lo + A_WIDTH] = piece.astype(BF16)

        put(0, _rope(seg(0, 512), cos_v, sin_v) * A_SCALE)
        put(1, _rope(seg(512, 1024), cos_v, sin_v))
        put(2, seg(1024, 1536))
        cq_ref[...] = seg(1536, 1792)
        kpe_ref[...] = _rope(seg(1792, 1920), cos_v[:, :LANES], sin_v[:, :LANES]).astype(BF16)
        ckv_ref[...] = seg(1920, 2048)

    row = lambda c, d=1: pl.BlockSpec((ts // d, c * d), lambda i: (i, 0))
    return pl.pallas_call(
        body, name="in_proj", grid=(n_rows // ts,),
        in_specs=[row(D_MODEL), pl.BlockSpec((D_MODEL, IN_PAD), lambda i: (0, 0)), row(A_WIDTH), row(A_WIDTH)],
        out_specs=[row(3 * A_WIDTH, d) for d in dils] + [row(Q_LORA), row(LANES), row(KV_LORA)],
        out_shape=[_sds((n_rows // d, d * 3 * A_WIDTH), BF16) for d in dils]
        + [_sds((n_rows, Q_LORA)), _sds((n_rows, LANES), BF16), _sds((n_rows, KV_LORA))],
        scratch_shapes=[pltpu.VMEM((A_WIDTH // LANES, ts, LANES), F32)],
        compiler_params=_params(("parallel",)),
    )(x_bf, w_in_p, cos, sin)


def _band_mask(m, heads):
    qi = lax.broadcasted_iota(jnp.int32, (heads * SPAN, 2 * SPAN), 0) % SPAN
    kj = lax.broadcasted_iota(jnp.int32, (heads * SPAN, 2 * SPAN), 1)
    return ((kj < SPAN) & (kj >= qi) & (m > 0)) | ((kj >= SPAN) & ((kj - SPAN) <= qi))


def _even_lanes(rows):
    return lax.broadcasted_iota(jnp.int32, (rows, LANES), 1) < A_HEAD_DIM


DIL_GROUP = 4


def _dil_fwd(qkv_view, dil):
    nb = qkv_view.shape[0] // SPAN
    group = min(dil, DIL_GROUP)

    def body(cur_ref, prev_ref, ol_ref):
        mask = _band_mask(pl.program_id(1), 1)
        even, even2 = _even_lanes(SPAN), _even_lanes(2 * SPAN)
        for g in range(group):
            for p in range(A_HEADS // 2):
                q_sl, k_sl, v_sl = (slice((3 * g + t) * A_WIDTH + p * LANES, (3 * g + t) * A_WIDTH + (p + 1) * LANES)
                                    for t in range(3))
                q2 = cur_ref[:, q_sl]
                kcat = jnp.concatenate([prev_ref[:, k_sl], cur_ref[:, k_sl]], axis=0)
                vcat = jnp.concatenate([prev_ref[:, v_sl], cur_ref[:, v_sl]], axis=0)
                zero, one = jnp.zeros_like(q2), jnp.ones_like(vcat)
                res, lses = [], []
                for first in (True, False):
                    qh = jnp.where(even, q2, zero) if first else jnp.where(even, zero, q2)
                    vh = jnp.where(even2, vcat, one) if first else jnp.where(even2, one, vcat)
                    s = jnp.where(mask, _dot(qh, kcat, _NT), NEG)
                    mx = jnp.max(s, axis=-1, keepdims=True)
                    r = _dot(jnp.exp(s - mx).astype(BF16), vh, _NN)
                    den = pltpu.roll(r, A_HEAD_DIM, 1)
                    res.append(r / den)
                    lses.append(mx + jnp.log(den))
                o_lo = 2 * g * A_WIDTH + p * LANES
                ol_ref[:, o_lo:o_lo + LANES] = jnp.where(even, res[0], res[1])
                ol_ref[:, o_lo + A_WIDTH:o_lo + A_WIDTH + LANES] = jnp.where(even, lses[0], lses[1])

    return pl.pallas_call(
        body, name=f"dil_fwd_d{dil}", grid=(dil // group, nb),
        in_specs=[pl.BlockSpec((SPAN, group * 3 * A_WIDTH), lambda r, m: (m, r)),
                  pl.BlockSpec((SPAN, group * 3 * A_WIDTH), lambda r, m: (jnp.maximum(m - 1, 0), r))],
        out_specs=pl.BlockSpec((SPAN, group * 2 * A_WIDTH), lambda r, m: (m, r)),
        out_shape=_sds((qkv_view.shape[0], dil * 2 * A_WIDTH)),
        compiler_params=_params(("parallel", "arbitrary")),
    )(qkv_view, qkv_view)


def _mix_fwd(ols, b_out, a_g, b_g):
    def fn(ol1, ol2, ol3, b, ag, bg):
        (o1, l1), (o2, l2), (o3, l3) = [(t[:, :A_WIDTH], t[:, A_WIDTH:]) for t in (ol1, ol2, ol3)]
        mx = jnp.maximum(jnp.maximum(l1, l2), l3)
        e1, e2, e3 = jnp.exp(l1 - mx), jnp.exp(l2 - mx), jnp.exp(l3 - mx)
        a = (e1 * o1 + e2 * o2 + e3 * o3) / (e1 + e2 + e3)
        return a, jnp.concatenate([_rms(a, ag), _rms(b, bg)], axis=1)

    n_rows = b_out.shape[0]
    return _rowwise("mix_fwd", fn, [*ols, b_out], [_sds((n_rows, A_WIDTH)), _sds((n_rows, 2 * A_WIDTH), BF16)],
                    consts=[a_g, b_g], in_dils=[dil for _, dil in PATTERNS] + [1])


def _mla_q_prep(cq, g, w_uq_p, cos, sin):
    def fn(cq_v, cos_v, sin_v, g_v, w_v):
        cqn = _rms(cq_v, g_v).astype(BF16)
        q = _dot(cqn, w_v, _NN)
        qf = [jnp.concatenate([q[:, h * QK_NOPE:(h + 1) * QK_NOPE],
                               _rope(q[:, 512 + h * LANES:512 + (h + 1) * LANES], cos_v[:, :LANES], sin_v[:, :LANES])], axis=1)
              for h in range(MLA_HEADS)]
        return cqn, qf

    n_rows = cq.shape[0]
    return _rowwise("mla_q_prep", fn, [cq, cos, sin], [_sds((n_rows, Q_LORA), BF16), _sds((MLA_HEADS, n_rows, QK_PAD), BF16)],
                    consts=[g, w_uq_p])


def _mla_kv_prep(ckv, kpe, g, w_ukv_p):
    def fn(ckv_v, kpe_v, g_v, w_v):
        ckvn = _rms(ckv_v, g_v).astype(BF16)
        kvv = _dot(ckvn, w_v, _NN)
        kf = [jnp.concatenate([kvv[:, h * QK_NOPE:(h + 1) * QK_NOPE], kpe_v.astype(F32)], axis=1) for h in range(MLA_HEADS)]
        return ckvn, kf, kvv[:, 512:]

    n_rows = ckv.shape[0]
    return _rowwise("mla_kv_prep", fn, [ckv, kpe],
                    [_sds((n_rows, KV_LORA), BF16), _sds((MLA_HEADS, n_rows, QK_PAD), BF16), _sds((n_rows, MLA_WIDTH), BF16)],
                    consts=[g, w_ukv_p])


MLA_FWD_TILES = (1024, 2048)
MLA_BWD_TILES = (1024, 1024)


def _mla_fwd(qf, kf, v, comm=None):
    n_rows = v.shape[0]
    tq, tk = min(MLA_FWD_TILES[0], n_rows), min(MLA_FWD_TILES[1], n_rows)
    nq, nk = n_rows // tq, n_rows // tk
    grid = (MLA_HEADS, nq, nk)
    exp2_scale = MLA_SCALE * math.log2(math.e)
    c_ins, c_in_specs, c_outs, c_out_specs, begin, end = _hosted(comm, grid)
    n_ci, n_co = len(c_ins), len(c_outs)

    def last_k(i):
        return jnp.right_shift(i * tq + tq - 1, int(math.log2(tk)))

    def body(q_ref, k_ref, v_ref, *refs):
        ci_refs, (o_ref, lse_ref), co_refs = refs[:n_ci], refs[n_ci:n_ci + 2], refs[n_ci + 2:n_ci + 2 + n_co]
        m_sc, l_sc, acc_sc = refs[n_ci + 2 + n_co:n_ci + 5 + n_co]
        sems = refs[n_ci + 5 + n_co:]
        i, j = pl.program_id(1), pl.program_id(2)
        begin(ci_refs, co_refs, sems)

        @pl.when(j == 0)
        def _():
            m_sc[...] = jnp.full(m_sc.shape, NEG, F32)
            l_sc[...] = jnp.zeros(l_sc.shape, F32)
            acc_sc[...] = jnp.zeros(acc_sc.shape, F32)

        def step(masked):
            s = _dot(q_ref[...], k_ref[...], _NT)
            if masked:
                row = lax.broadcasted_iota(jnp.int32, (tq, tk), 0) + i * tq
                col = lax.broadcasted_iota(jnp.int32, (tq, tk), 1) + j * tk
                s = jnp.where(col <= row, s, NEG)
            m_prev = m_sc[...]
            m_new = jnp.maximum(m_prev, jnp.max(s, axis=-1, keepdims=True))
            alpha = jnp.exp2((m_prev - m_new) * exp2_scale)
            p = jnp.exp2((s - m_new) * exp2_scale)
            l_sc[...] = alpha * l_sc[...] + jnp.sum(p, axis=-1, keepdims=True)
            acc_sc[...] = alpha * acc_sc[...] + _dot(p.astype(BF16), v_ref[...], _NN)
            m_sc[...] = m_new

        active = j * tk <= i * tq + tq - 1
        crosses = (j + 1) * tk - 1 > i * tq

        @pl.when(active & jnp.logical_not(crosses))
        def _():
            step(False)

        @pl.when(active & crosses)
        def _():
            step(True)

        @pl.when(j == last_k(i))
        def _():
            o_ref[...] = acc_sc[...] / l_sc[...]
            lse_ref[...] = jnp.broadcast_to(m_sc[...] * MLA_SCALE + jnp.log(l_sc[...]), (tq, V_DIM))

        end(ci_refs, co_refs, sems)

    return pl.pallas_call(
        body, name="mla_fwd", grid=grid,
        in_specs=[pl.BlockSpec((None, tq, QK_PAD), lambda h, i, j: (h, i, 0)),
                  pl.BlockSpec((None, tk, QK_PAD), lambda h, i, j: (h, jnp.minimum(j, last_k(i)), 0)),
                  pl.BlockSpec((tk, V_DIM), lambda h, i, j: (jnp.minimum(j, last_k(i)), h))] + c_in_specs,
        out_specs=[pl.BlockSpec((tq, V_DIM), lambda h, i, j: (i, h)), pl.BlockSpec((tq, V_DIM), lambda h, i, j: (i, h))]
        + c_out_specs,
        out_shape=[_sds((n_rows, MLA_WIDTH)), _sds((n_rows, MLA_WIDTH))] + c_outs,
        scratch_shapes=[pltpu.VMEM((tq, 1), F32), pltpu.VMEM((tq, 1), F32), pltpu.VMEM((tq, V_DIM), F32)]
        + (comm.sem_scratch() if comm else []),
        compiler_params=_params(("arbitrary",) * 3 if comm else ("parallel", "parallel", "arbitrary")),
    )(qf, kf, v, *c_ins)


def _mm_res_ln(name, a, w, xres, g, b):
    def epi(acc, xr, g_v, b_v):
        r = ALPHA * xr + acc
        y = _layer_norm(r, g_v, b_v)
        return y, y, r

    return _mm(name, a, w, (F32, BF16, F32), tm=1024, tn=D_MODEL, tk=1024, epilogue=epi, extras=[xres], vecs=[g, b])


def _mm_relu2(x_bf, w):
    def epi(acc):
        r = jnp.maximum(acc, 0.0)
        return (r * r,)

    return _mm("ff1", x_bf, w, (BF16,), tm=1024, tn=1024, tk=D_MODEL, epilogue=epi, b_pieces=True)[0]


def _loss_fn(y, t):
    def fn(y_v, t_v):
        d = y_v - t_v
        part = jnp.sum(jnp.sum(d * d, axis=1, keepdims=True), axis=0, keepdims=True)
        return d * (1.0 / D_MODEL), part

    dy, part = _rowwise("loss", fn, [y, t], [_sds(y.shape)], reds=[_sds((1, 1))])
    return part * (0.5 / D_MODEL), dy


def _ln_bwd(name, dy, r, g):
    def fn(dy_v, r_v, g_v):
        mu = jnp.mean(r_v, axis=-1, keepdims=True)
        xc = r_v - mu
        rstd = lax.rsqrt(jnp.mean(xc * xc, axis=-1, keepdims=True) + LN_EPS)
        xh = xc * rstd
        dxh = dy_v * g_v
        dr = rstd * (dxh - jnp.mean(dxh, axis=-1, keepdims=True) - xh * jnp.mean(dxh * xh, axis=-1, keepdims=True))
        return dr, dr, jnp.sum(dy_v * xh, axis=0, keepdims=True), jnp.sum(dy_v, axis=0, keepdims=True)

    return _rowwise(name, fn, [dy, r], [_sds(dy.shape), _sds(dy.shape, BF16)], consts=[g],
                    reds=[_sds((1, D_MODEL)), _sds((1, D_MODEL))])


def _head_sum_matrix():
    i = lax.broadcasted_iota(jnp.int32, (A_WIDTH, A_WIDTH), 0) // A_HEAD_DIM
    j = lax.broadcasted_iota(jnp.int32, (A_WIDTH, A_WIDTH), 1) // A_HEAD_DIM
    return (i == j).astype(BF16)


def _mix_bwd(dmixed, a_out, b_out, ols, a_g, b_g):
    def fn(dm, a, b, ol1, ol2, ol3, ag, bg, e_mat):
        l1, l2, l3 = ol1[:, A_WIDTH:], ol2[:, A_WIDTH:], ol3[:, A_WIDTH:]
        da, dga = _rms_bwd(a, ag, dm[:, :A_WIDTH])
        db, dgb = _rms_bwd(b, bg, dm[:, A_WIDTH:])
        t = da * a
        t_hi = t.astype(BF16)
        t_lo = (t - t_hi.astype(F32)).astype(BF16)
        tsum = _dot(t_hi, e_mat, _NN) + _dot(t_lo, e_mat, _NN)
        mx = jnp.maximum(jnp.maximum(l1, l2), l3)
        e1, e2, e3 = jnp.exp(l1 - mx), jnp.exp(l2 - mx), jnp.exp(l3 - mx)
        inv = 1.0 / (e1 + e2 + e3)
        w1, w2, w3 = e1 * inv, e2 * inv, e3 * inv
        tb = db * b
        delta_b = jnp.concatenate(
            [jnp.broadcast_to(jnp.sum(tb[:, h * V_DIM:(h + 1) * V_DIM], axis=-1, keepdims=True), (tb.shape[0], V_DIM))
             for h in range(MLA_HEADS)], axis=1)
        dods = [jnp.concatenate([w * da, w * tsum], axis=1) for w in (w1, w2, w3)]
        return (*dods, db, delta_b, dga, dgb)

    n_rows = a_out.shape[0]
    wide = (n_rows, A_WIDTH)
    dils = [dil for _, dil in PATTERNS]
    return _rowwise("mix_bwd", fn, [dmixed, a_out, b_out, *ols],
                    [_sds((n_rows // d, d * 2 * A_WIDTH)) for d in dils] + [_sds(wide, BF16), _sds(wide)],
                    consts=[a_g, b_g, _head_sum_matrix()], reds=[_sds((1, A_WIDTH)), _sds((1, MLA_WIDTH))], ts=256,
                    in_dils=[1, 1, 1] + dils, out_dils=dils + [1, 1])


def _dil_bwd(qkv_view, ol_view, dod_view, dil):
    nb = qkv_view.shape[0] // SPAN
    group = min(dil, DIL_GROUP)

    def body(cur_ref, prev_ref, ol_ref, dod_ref, out_ref, carry):
        m = pl.program_id(1)

        @pl.when(m == 0)
        def _():
            carry[...] = jnp.zeros(carry.shape, F32)

        @pl.when(m < nb)
        def _():
            mask, even = _band_mask(m, 2), _even_lanes(SPAN)
            for g in range(group):
                for p in range(A_HEADS // 2):
                    q_sl, k_sl, v_sl = (slice((3 * g + t) * A_WIDTH + p * LANES, (3 * g + t) * A_WIDTH + (p + 1) * LANES)
                                        for t in range(3))
                    do_sl = slice(2 * g * A_WIDTH + p * LANES, 2 * g * A_WIDTH + (p + 1) * LANES)
                    l_sl = slice((2 * g + 1) * A_WIDTH + p * LANES, (2 * g + 1) * A_WIDTH + (p + 1) * LANES)
                    q2, do2 = cur_ref[:, q_sl], dod_ref[:, do_sl].astype(BF16)
                    zero = jnp.zeros_like(q2)
                    qcat = jnp.concatenate([jnp.where(even, q2, zero), jnp.where(even, zero, q2)], axis=0)
                    docat = jnp.concatenate([jnp.where(even, do2, zero), jnp.where(even, zero, do2)], axis=0)
                    kcat = jnp.concatenate([prev_ref[:, k_sl], cur_ref[:, k_sl]], axis=0)
                    vcat = jnp.concatenate([prev_ref[:, v_sl], cur_ref[:, v_sl]], axis=0)
                    lse2, dl2 = ol_ref[:, l_sl], dod_ref[:, l_sl]
                    lse_c = jnp.concatenate([lse2[:, :1], lse2[:, A_HEAD_DIM:A_HEAD_DIM + 1]], axis=0)
                    dl_c = jnp.concatenate([dl2[:, :1], dl2[:, A_HEAD_DIM:A_HEAD_DIM + 1]], axis=0)
                    pr = jnp.exp(jnp.where(mask, _dot(qcat, kcat, _NT), NEG) - lse_c)
                    ds = (pr * (_dot(docat, vcat, _NT) - dl_c)).astype(BF16)
                    dq = _dot(ds, kcat, _NN)
                    dk2 = _dot(ds, qcat, _TN)
                    dv2 = _dot(pr.astype(BF16), docat, _TN)
                    out_ref[:, q_sl] = carry[:, q_sl]
                    out_ref[:, k_sl] = carry[:, k_sl] + dk2[:SPAN]
                    out_ref[:, v_sl] = carry[:, v_sl] + dv2[:SPAN]
                    carry[:, q_sl] = jnp.where(even, dq[:SPAN], dq[SPAN:])
                    carry[:, k_sl] = dk2[SPAN:]
                    carry[:, v_sl] = dv2[SPAN:]

        @pl.when(m == nb)
        def _():
            out_ref[...] = carry[...]

    last = nb - 1

    def cur(width):
        return pl.BlockSpec((SPAN, group * width), lambda r, m: (jnp.minimum(m, last), r))

    def prev(width):
        return pl.BlockSpec((SPAN, group * width), lambda r, m: (jnp.clip(m - 1, 0, last), r))

    return pl.pallas_call(
        body, name=f"dil_bwd_d{dil}", grid=(dil // group, nb + 1),
        in_specs=[cur(3 * A_WIDTH), prev(3 * A_WIDTH), cur(2 * A_WIDTH), cur(2 * A_WIDTH)],
        out_specs=prev(3 * A_WIDTH),
        out_shape=_sds(qkv_view.shape),
        scratch_shapes=[pltpu.VMEM((SPAN, group * 3 * A_WIDTH), F32)],
        compiler_params=_params(("parallel", "arbitrary")),
    )(qkv_view, qkv_view, ol_view, dod_view)


def _mla_bwd(qf, kf, v, lse, do, delta, comm=None):
    n_rows = v.shape[0]
    tq, tk = min(MLA_BWD_TILES[0], n_rows), min(MLA_BWD_TILES[1], n_rows)
    nq, nk = n_rows // tq, n_rows // tk

    def first_q(j):
        return jnp.right_shift(j * tk, int(math.log2(tq)))

    grid = (MLA_HEADS, nk, nq)
    c_ins, c_in_specs, c_outs, c_out_specs, begin, end = _hosted(comm, grid)
    n_ci, n_co = len(c_ins), len(c_outs)

    def body(q_ref, k_ref, v_ref, lse_ref, do_ref, dl_ref, *refs):
        ci_refs, (dq_ref, dk_ref, dv_ref), co_refs = refs[:n_ci], refs[n_ci:n_ci + 3], refs[n_ci + 3:n_ci + 3 + n_co]
        dk_sc, dv_sc = refs[n_ci + 3 + n_co:n_ci + 5 + n_co]
        sems = refs[n_ci + 5 + n_co:]
        j, i = pl.program_id(1), pl.program_id(2)
        begin(ci_refs, co_refs, sems)

        @pl.when((j == 0) & (i == 0))
        def _():
            dq_ref[...] = jnp.zeros(dq_ref.shape, F32)

        @pl.when(i == first_q(j))
        def _():
            dk_sc[...] = jnp.zeros(dk_sc.shape, F32)
            dv_sc[...] = jnp.zeros(dv_sc.shape, F32)

        def step(masked):
            q, k, dob = q_ref[...], k_ref[...], do_ref[...]
            s = _dot(q, k, _NT) * MLA_SCALE
            if masked:
                row = lax.broadcasted_iota(jnp.int32, (tq, tk), 0) + i * tq
                col = lax.broadcasted_iota(jnp.int32, (tq, tk), 1) + j * tk
                s = jnp.where(col <= row, s, NEG)
            p = jnp.exp(s - lse_ref[:, :1])
            dp = _dot(dob, v_ref[...], _NT)
            ds = (p * (dp - dl_ref[:, :1]) * MLA_SCALE).astype(BF16)
            dv_sc[...] += _dot(p.astype(BF16), dob, _TN)
            dk_sc[...] += _dot(ds, q, _TN)
            rows = pl.ds(pl.multiple_of(i * tq, tq), tq)
            dq_ref[rows, :] += _dot(ds, k, _NN)

        active = i >= first_q(j)
        crosses = (j + 1) * tk - 1 > i * tq

        @pl.when(active & jnp.logical_not(crosses))
        def _():
            step(False)

        @pl.when(active & crosses)
        def _():
            step(True)

        @pl.when(i == nq - 1)
        def _():
            dk_ref[...] = dk_sc[...]
            dv_ref[...] = dv_sc[...]

        end(ci_refs, co_refs, sems)

    qrow = lambda h, j, i: (jnp.maximum(i, first_q(j)), h)
    return pl.pallas_call(
        body, name="mla_bwd", grid=grid,
        in_specs=[pl.BlockSpec((None, tq, QK_PAD), lambda h, j, i: (h, jnp.maximum(i, first_q(j)), 0)),
                  pl.BlockSpec((None, tk, QK_PAD), lambda h, j, i: (h, j, 0)),
                  pl.BlockSpec((tk, V_DIM), lambda h, j, i: (j, h)),
                  pl.BlockSpec((tq, V_DIM), qrow), pl.BlockSpec((tq, V_DIM), qrow), pl.BlockSpec((tq, V_DIM), qrow)]
        + c_in_specs,
        out_specs=[pl.BlockSpec((None, n_rows, QK_PAD), lambda h, j, i: (h, 0, 0)),
                   pl.BlockSpec((None, tk, QK_PAD), lambda h, j, i: (h, j, 0)),
                   pl.BlockSpec((tk, V_DIM), lambda h, j, i: (j, h))] + c_out_specs,
        out_shape=[_sds((MLA_HEADS, n_rows, QK_PAD)), _sds((MLA_HEADS, n_rows, QK_PAD)), _sds((n_rows, MLA_WIDTH))] + c_outs,
        scratch_shapes=[pltpu.VMEM((tk, QK_PAD), F32), pltpu.VMEM((tk, V_DIM), F32)] + (comm.sem_scratch() if comm else []),
        compiler_params=_params(("arbitrary",) * 3 if comm else ("parallel", "arbitrary", "arbitrary")),
    )(qf, kf, v, lse, do, delta, *c_ins)


def _mla_bwd_prep(dqf, dkf, dv, cos, sin):
    def fn(dqf_v, dkf_v, dv_v, cos_v, sin_v):
        dq = jnp.concatenate([dqf_v[h][:, :QK_NOPE] for h in range(MLA_HEADS)]
                             + [_rope_t(dqf_v[h][:, QK_NOPE:], cos_v[:, :LANES], sin_v[:, :LANES]) for h in range(MLA_HEADS)], axis=1)
        dkv = jnp.concatenate([dkf_v[h][:, :QK_NOPE] for h in range(MLA_HEADS)] + [dv_v], axis=1)
        dkpe = dkf_v[0][:, QK_NOPE:] + dkf_v[1][:, QK_NOPE:] + dkf_v[2][:, QK_NOPE:] + dkf_v[3][:, QK_NOPE:]
        return dq, dkv, dkpe

    n_rows = dv.shape[0]
    return _rowwise("mla_bwd_prep", fn, [dqf, dkf, dv, cos, sin],
                    [_sds((n_rows, UQ_PAD), BF16), _sds((n_rows, 1024), BF16), _sds((n_rows, LANES))], ts=256)


def _assemble_dh(dqkvs, dcqn, cq, dckvn, ckv, dkpe, cos, sin, gq, gkv):
    def fn(g1, g2, g3, dcqn_v, cq_v, dckvn_v, ckv_v, dkpe_v, cos_v, sin_v, gq_v, gkv_v):
        g = g1 + g2 + g3
        dqa = _rope_t(g[:, :A_WIDTH], cos_v, sin_v) * A_SCALE
        dka = _rope_t(g[:, A_WIDTH:2 * A_WIDTH], cos_v, sin_v)
        dcq, dgq = _rms_bwd(cq_v, gq_v, dcqn_v)
        dckv, dgkv = _rms_bwd(ckv_v, gkv_v, dckvn_v)
        dkr = _rope_t(dkpe_v, cos_v[:, :LANES], sin_v[:, :LANES])
        return jnp.concatenate([dqa, dka, g[:, 2 * A_WIDTH:], dcq, dkr, dckv], axis=1), dgq, dgkv

    n_rows = cq.shape[0]
    return _rowwise("assemble_dh", fn, [*dqkvs, dcqn, cq, dckvn, ckv, dkpe, cos, sin],
                    [_sds((n_rows, IN_PAD), BF16)], consts=[gq, gkv],
                    reds=[_sds((1, Q_LORA)), _sds((1, KV_LORA))], ts=256,
                    in_dils=[dil for _, dil in PATTERNS] + [1] * 7)


def _layer_fwd(x_f32, x_bf, w, sm, cos, sin, behind=None):
    *qkv_views, cq, kpe, ckv = _in_proj(x_bf, w["w_in"], cos, sin)
    ol_views = [_dil_fwd(view, dil) for view, (_, dil) in zip(qkv_views, PATTERNS)]
    cqn, qf = _mla_q_prep(cq, sm["q_a_norm"], w["w_uq"], cos, sin)
    ckvn, kf, v = _mla_kv_prep(ckv, kpe, sm["kv_a_norm"], w["w_ukv"])
    b_out, b_lse, *behind_outs = _mla_fwd(qf, kf, v, comm=behind)
    a_out, mixed = _mix_fwd(ol_views, b_out, sm["a_out_norm"], sm["b_out_norm"])
    x1, x1_bf, r1 = _mm_res_ln("wo_ln1", mixed, w["w_o"], x_f32, sm["ln1_g"], sm["ln1_b"])
    f = _mm_relu2(x1_bf, w["w_ff1"])
    x2, x2_bf, r2 = _mm_res_ln("ff2_ln2", f, w["w_ff2"], x1, sm["ln2_g"], sm["ln2_b"])
    saved = dict(x_bf=x_bf, qkv_views=qkv_views, ol_views=ol_views, cq=cq, ckv=ckv, cqn=cqn, qf=qf, ckvn=ckvn,
                 kf=kf, v=v, b_out=b_out, b_lse=b_lse, a_out=a_out, mixed=mixed, x1_bf=x1_bf, r1=r1, f=f, r2=r2)
    return x2, x2_bf, saved, behind_outs


def _layer_bwd(dx2, w, sm, sv, cos, sin, pending=None):
    dr2, dr2_bf, dg2, db2 = _ln_bwd("ln2_bwd", dx2, sv["r2"], sm["ln2_g"])
    (dw_ff2,) = _mm("dw_ff2", sv["f"], dr2_bf, (F32,), ta=True, tm=1024, tn=1024, tk=2048)
    du, *swapped = _mm("d_u", dr2_bf, w["w_ff2"], (BF16,), tb=True, tm=1024, tn=2048, tk=D_MODEL,
                       epilogue=lambda acc, f: (acc * (2.0 * jnp.sqrt(f.astype(F32))),), extras=[sv["f"]],
                       comm=pending.swap_comm() if pending else None)
    if pending:
        pending.after_swap(swapped)
    (dw_ff1,) = _mm("dw_ff1", sv["x1_bf"], du, (F32,), ta=True, tm=1024, tn=1024, tk=2048, out_pieces=True)
    (dy1,) = _mm("d_x1", du, w["w_ff1"], (F32,), tb=True, tm=1024, tn=1024, tk=1024,
                 epilogue=lambda acc, d: (acc + ALPHA * d,), extras=[dr2], b_pieces=True)
    dr1, dr1_bf, dg1, db1 = _ln_bwd("ln1_bwd", dy1, sv["r1"], sm["ln1_g"])
    (dw_o,) = _mm("dw_o", sv["mixed"], dr1_bf, (F32,), ta=True, tm=1024, tn=1024, tk=2048)
    (dmixed,) = _mm("d_mixed", dr1_bf, w["w_o"], (F32,), tb=True, tm=1024, tn=1024, tk=D_MODEL)
    dod1, dod2, dod3, do_b, delta_b, dga, dgb = _mix_bwd(
        dmixed, sv["a_out"], sv["b_out"], sv["ol_views"], sm["a_out_norm"], sm["b_out_norm"])
    dqkvs = [_dil_bwd(qkv_view, ol_view, dod_view, dil)
             for (_, dil), qkv_view, ol_view, dod_view in zip(PATTERNS, sv["qkv_views"], sv["ol_views"], (dod1, dod2, dod3))]
    dqf, dkf, dv_b, *scattered = _mla_bwd(sv["qf"], sv["kf"], sv["v"], sv["b_lse"], do_b, delta_b,
                                          comm=pending.scatter_comm() if pending else None)
    if pending:
        pending.after_scatter(scattered)
    dq_b, dkv_b, dkpe = _mla_bwd_prep(dqf, dkf, dv_b, cos, sin)
    (dw_uq,) = _mm("dw_uq", sv["cqn"], dq_b, (F32,), ta=True, tm=Q_LORA, tn=1024, tk=2048)
    (dcqn,) = _mm("d_cqn", dq_b, w["w_uq"], (F32,), tb=True, tm=1024, tn=Q_LORA, tk=UQ_PAD)
    (dw_ukv,) = _mm("dw_ukv", sv["ckvn"], dkv_b, (F32,), ta=True, tm=KV_LORA, tn=1024, tk=2048)
    (dckvn,) = _mm("d_ckvn", dkv_b, w["w_ukv"], (F32,), tb=True, tm=1024, tn=KV_LORA, tk=1024)
    dh, dgq, dgkv = _assemble_dh(dqkvs, dcqn, sv["cq"], dckvn, sv["ckv"], dkpe, cos, sin,
                                 sm["q_a_norm"], sm["kv_a_norm"])
    (dw_in,) = _mm("dw_in", sv["x_bf"], dh, (F32,), ta=True, tm=1024, tn=1024, tk=2048)
    (dx,) = _mm("d_x", dh, w["w_in"], (F32,), tb=True, tm=1024, tn=1024, tk=2048,
                epilogue=lambda acc, d: (acc + ALPHA * d,), extras=[dr1])
    dws = dict(w_in=dw_in, w_uq=dw_uq, w_ukv=dw_ukv, w_o=dw_o, w_ff1=dw_ff1, w_ff2=dw_ff2)
    dsm = dict(q_a_norm=dgq, kv_a_norm=dgkv, a_out_norm=dga, b_out_norm=dgb, ln1_g=dg1, ln1_b=db1, ln2_g=dg2, ln2_b=db2)
    return dx, dws, dsm


def _pad_w_in(w):
    return jnp.concatenate([w[:, :1792], w[:, 1920:1984], jnp.zeros((w.shape[0], 64), w.dtype), w[:, 1792:1920]], axis=1)


def _unpad_w_in(w):
    return jnp.concatenate([w[:, :1792], w[:, 1920:2048], w[:, 1792:1856]], axis=1)


def _rope_tables(n_rows):
    half = A_HEAD_DIM // 2
    inv_freq = ROPE_THETA ** (-jnp.arange(half, dtype=F32) / half)
    ang = jnp.arange(n_rows, dtype=F32)[:, None] * inv_freq[None, :]
    cos = jnp.tile(jnp.cos(ang), (1, 2 * A_HEADS))
    sin = jnp.tile(jnp.concatenate([-jnp.sin(ang), jnp.sin(ang)], axis=1), (1, A_HEADS))
    return cos, sin


W_NAMES = ("w_in", "w_uq", "w_ukv", "w_o", "w_ff1", "w_ff2")
PIECE = dict(w_in=(512, 512), w_uq=(128, 256), w_ukv=(64, 256), w_o=(128, 1024), w_ff1=(512, 1024), w_ff2=(512, 1024))
SHARD_COLS = dict(w_in=496, w_uq=192, w_ukv=256, w_o=1024, w_ff1=1024, w_ff2=1024)


def _col_pieces(t, pad_to):
    rows, _, cols = t.shape
    t = jnp.pad(t, ((0, 0), (0, 0), (0, pad_to - cols)))
    return t.reshape(2, rows // 2, N_CHIPS, pad_to).transpose(2, 0, 1, 3)


def _grad_pieces(dws):
    uq, ukv = dws["w_uq"], dws["w_ukv"]
    uq = jnp.concatenate([uq[:, :512].reshape(Q_LORA, MLA_HEADS, LANES), uq[:, 512:].reshape(Q_LORA, MLA_HEADS, LANES)], axis=2)
    ukv = jnp.concatenate([ukv[:, :512].reshape(KV_LORA, MLA_HEADS, LANES), ukv[:, 512:].reshape(KV_LORA, MLA_HEADS, LANES)], axis=2)
    return dict(
        w_in=_col_pieces(_unpad_w_in(dws["w_in"]).reshape(D_MODEL, N_CHIPS, SHARD_COLS["w_in"]), 512),
        w_uq=_col_pieces(uq, 256), w_ukv=_col_pieces(ukv, 256),
        w_o=dws["w_o"].reshape(N_CHIPS, 2, 128, D_MODEL),
        w_ff1=dws["w_ff1"],
        w_ff2=dws["w_ff2"].reshape(N_CHIPS, 2, 512, D_MODEL))


def _weights_from_pieces(p):
    def cols(t):
        return t.transpose(1, 2, 0, 3).reshape(2 * t.shape[2], N_CHIPS, t.shape[3])

    uq, ukv = cols(p["w_uq"]), cols(p["w_ukv"])
    return dict(
        w_in=_pad_w_in(cols(p["w_in"])[:, :, :SHARD_COLS["w_in"]].reshape(D_MODEL, IN_COLS)),
        w_uq=jnp.concatenate([uq[:, :, :LANES].reshape(Q_LORA, 512), uq[:, :, LANES:].reshape(Q_LORA, 512)], axis=1),
        w_ukv=jnp.concatenate([ukv[:, :, :LANES].reshape(KV_LORA, 512), ukv[:, :, LANES:].reshape(KV_LORA, 512)], axis=1),
        w_o=p["w_o"].reshape(D_MODEL, D_MODEL),
        w_ff1=p["w_ff1"],
        w_ff2=p["w_ff2"].reshape(D_FF, D_MODEL))


def _my_piece(name, shard, half):
    rows, cols = shard.shape[0] // 2, shard.shape[1]
    t = lax.dynamic_slice_in_dim(shard, half * rows, rows, axis=0).astype(BF16)
    return jnp.pad(t, ((0, 0), (0, PIECE[name][1] - cols)))


MESH = pl.DeviceIdType.MESH
ANY = pl.BlockSpec(memory_space=pl.ANY)


class _Comm:
    def __init__(self, ins, out_shapes, n_sems, start, finish):
        self.ins, self.out_shapes, self.n_sems, self.start, self.finish = list(ins), list(out_shapes), n_sems, start, finish

    def sem_scratch(self):
        return [pltpu.SemaphoreType.DMA((self.n_sems,)), pltpu.SemaphoreType.DMA((self.n_sems,))]


def _comm_call(name, comm):
    n_in, n_out = len(comm.ins), len(comm.out_shapes)

    def body(*refs):
        ins, outs, sems = refs[:n_in], refs[n_in:n_in + n_out], refs[n_in + n_out:]
        comm.start(ins, outs, *sems)
        comm.finish(ins, outs, *sems)

    return pl.pallas_call(body, name=name, out_shape=comm.out_shapes, in_specs=[ANY] * n_in, out_specs=[ANY] * n_out,
                          scratch_shapes=comm.sem_scratch())(*comm.ins)


def _all_gather_comm(blocks):
    n = len(blocks)

    def plan(x_refs, out_refs, send_sems, recv_sems):
        x, y, c = lax.axis_index("x"), lax.axis_index("y"), lax.axis_index("c")
        me, sibling = (x, y, c), (x, y, 1 - c)
        chips = [(1 - x, y), (x, 1 - y), (1 - x, 1 - y)]

        def copy(t, k, blk, to, src=None):
            px, py, pc = blk
            slot = out_refs[t].at[4 * px + 2 * py + pc]
            return pltpu.make_async_remote_copy(
                src_ref=slot if src is None else src, dst_ref=slot,
                send_sem=send_sems.at[7 * t + k], recv_sem=recv_sems.at[7 * t + k], device_id=to, device_id_type=MESH)

        first = []
        for t in range(n):
            first.append(copy(t, 0, me, sibling, src=x_refs[t]))
            first += [copy(t, 1 + j, me, (*chip, c), src=x_refs[t]) for j, chip in enumerate(chips)]
        return me, sibling, chips, c, copy, first

    def start(x_refs, out_refs, send_sems, recv_sems):
        for cp in plan(x_refs, out_refs, send_sems, recv_sems)[-1]:
            cp.start()

    def finish(x_refs, out_refs, send_sems, recv_sems):
        me, sibling, chips, c, copy, sent = plan(x_refs, out_refs, send_sems, recv_sems)
        for j, chip in enumerate(chips):
            for t in range(n):
                copy(t, 1 + j, (*chip, c), me).wait_recv()
                sent.append(copy(t, 4 + j, (*chip, c), sibling))
                sent[-1].start()
        for t in range(n):
            copy(t, 0, sibling, me).wait_recv()
        for j, chip in enumerate(chips):
            for t in range(n):
                copy(t, 4 + j, (*chip, 1 - c), me).wait_recv()
        for cp in sent:
            cp.wait_send()

    return _Comm(blocks, [_sds((N_DEV,) + b.shape, b.dtype) for b in blocks], 7 * n, start, finish)


def _simple_comm(ins, out_shapes, n_sems, copies):
    def start(in_refs, out_refs, send_sems, recv_sems):
        for cp in copies(in_refs, out_refs, send_sems, recv_sems):
            cp.start()

    def finish(in_refs, out_refs, send_sems, recv_sems):
        for cp in copies(in_refs, out_refs, send_sems, recv_sems):
            cp.wait()

    return _Comm(ins, out_shapes, n_sems, start, finish)


def _swap_comm(gs):
    def copies(g_refs, got_refs, send_sems, recv_sems):
        c = lax.axis_index("c")
        sibling = (lax.axis_index("x"), lax.axis_index("y"), 1 - c)
        return [pltpu.make_async_remote_copy(src_ref=g_refs[t].at[k, 1 - c], dst_ref=got_refs[t].at[k],
                                             send_sem=send_sems.at[N_CHIPS * t + k], recv_sem=recv_sems.at[N_CHIPS * t + k],
                                             device_id=sibling, device_id_type=MESH)
                for t in range(len(gs)) for k in range(N_CHIPS)]

    return _simple_comm(gs, [_sds((N_CHIPS,) + g.shape[2:], g.dtype) for g in gs], N_CHIPS * len(gs), copies)


def _scatter_comm(hs):
    def copies(h_refs, got_refs, send_sems, recv_sems):
        x, y, c = lax.axis_index("x"), lax.axis_index("y"), lax.axis_index("c")
        chips = [(1 - x, y), (x, 1 - y), (1 - x, 1 - y)]
        return [pltpu.make_async_remote_copy(src_ref=h_refs[t].at[2 * px + py], dst_ref=got_refs[t].at[rel],
                                             send_sem=send_sems.at[3 * t + rel], recv_sem=recv_sems.at[3 * t + rel],
                                             device_id=(px, py, c), device_id_type=MESH)
                for t in range(len(hs)) for rel, (px, py) in enumerate(chips)]

    return _simple_comm(hs, [_sds((3,) + h.shape[1:], h.dtype) for h in hs], 3 * len(hs), copies)


def _share_comm(rs):
    def copies(r_refs, got_refs, send_sems, recv_sems):
        sibling = (lax.axis_index("x"), lax.axis_index("y"), 1 - lax.axis_index("c"))
        return [pltpu.make_async_remote_copy(src_ref=r_refs[t], dst_ref=got_refs[t], send_sem=send_sems.at[t],
                                             recv_sem=recv_sems.at[t], device_id=sibling, device_id_type=MESH)
                for t in range(len(rs))]

    return _simple_comm(rs, [_sds(r.shape, r.dtype) for r in rs], len(rs), copies)


SUM_STEPS = 4


def _pair_sums(name, gs, gots, c_arr):
    n = len(gs)
    dims = [(g.shape[2] // SUM_STEPS, g.shape[3]) for g in gs]

    def body(c_ref, *refs):
        for t in range(n):
            s = refs[t][...] + refs[n + t][...]
            refs[2 * n + t][...] = s
            refs[3 * n + t][...] = s.astype(BF16)

    def part(tr, pc):
        return pl.BlockSpec((None, tr, pc), lambda k, i, c_ref: (k, i, 0))

    def kept(tr, pc):
        return pl.BlockSpec((None, None, tr, pc), lambda k, i, c_ref: (k, c_ref[0], i, 0))

    outs = pl.pallas_call(
        body, name=name,
        out_shape=[_sds((N_CHIPS,) + g.shape[2:]) for g in gs] + [_sds((N_CHIPS,) + g.shape[2:], BF16) for g in gs],
        grid_spec=pltpu.PrefetchScalarGridSpec(
            num_scalar_prefetch=1, grid=(N_CHIPS, SUM_STEPS),
            in_specs=[kept(*d) for d in dims] + [part(*d) for d in dims],
            out_specs=[part(*d) for d in dims] * 2),
        compiler_params=_params(("parallel", "parallel")),
    )(c_arr, *gs, *gots)
    return outs[:n], outs[n:]


def _chips_sums(name, h32s, gots, chip_arr):
    n = len(h32s)
    dims = [(h.shape[1] // SUM_STEPS, h.shape[2]) for h in h32s]

    def body(chip_ref, *refs):
        for t in range(n):
            got_ref = refs[n + t]
            refs[2 * n + t][...] = refs[t][...] + got_ref[0] + got_ref[1] + got_ref[2]

    return pl.pallas_call(
        body, name=name, out_shape=[_sds(h.shape[1:]) for h in h32s],
        grid_spec=pltpu.PrefetchScalarGridSpec(
            num_scalar_prefetch=1, grid=(SUM_STEPS,),
            in_specs=[pl.BlockSpec((None, tr, pc), lambda i, chip_ref: (chip_ref[0], i, 0)) for tr, pc in dims]
            + [pl.BlockSpec((3, tr, pc), lambda i, chip_ref: (0, i, 0)) for tr, pc in dims],
            out_specs=[pl.BlockSpec((tr, pc), lambda i, chip_ref: (i, 0)) for tr, pc in dims]),
        compiler_params=_params(("parallel",)),
    )(chip_arr, *h32s, *gots)


class _LayerReduce:
    def __init__(self, tag, dws, c, chip):
        self.tag, self.c, self.gp = tag, c, _grad_pieces(dws)
        self.c_arr, self.chip_arr = jnp.reshape(c, (1,)).astype(jnp.int32), jnp.reshape(chip, (1,)).astype(jnp.int32)
        self.shards = None

    def swap_comm(self):
        return _swap_comm([self.gp[n] for n in W_NAMES])

    def after_swap(self, gots):
        self.s32, self.s16 = _pair_sums(f"rs_pair_sum_{self.tag}", [self.gp[n] for n in W_NAMES], gots, self.c_arr)

    def scatter_comm(self):
        return _scatter_comm(self.s16)

    def after_scatter(self, parts):
        mine = _chips_sums(f"rs_chips_sum_{self.tag}", self.s32, parts, self.chip_arr)
        theirs = _comm_call(f"rs_share_{self.tag}", _share_comm(mine))
        self.shards = {}
        for n, a, b in zip(W_NAMES, mine, theirs):
            both = jnp.where(self.c == 0, jnp.concatenate([a, b], axis=0), jnp.concatenate([b, a], axis=0))
            self.shards[n] = both[:, :SHARD_COLS[n]]

    def run_alone(self):
        self.after_swap(_comm_call(f"rs_pair_{self.tag}", self.swap_comm()))
        self.after_scatter(_comm_call(f"rs_chips_{self.tag}", self.scatter_comm()))


class _LayerWeights:
    def __init__(self, shards, c, dev):
        self.shards, self.c, self.dev = shards, c, dev

    def comm(self, l):
        self.blocks = [_my_piece(n, self.shards[l][n], self.c) for n in W_NAMES]
        return _all_gather_comm(self.blocks)

    def take(self, l, gathered):
        pieces = {n: lax.dynamic_update_index_in_dim(g, b, self.dev, 0).reshape((N_CHIPS, 2) + b.shape)
                  for n, g, b in zip(W_NAMES, gathered, self.blocks)}
        return _weights_from_pieces(pieces)

    def first(self):
        return self.take(0, _comm_call("ag_l0", self.comm(0)))


def _all_reduce_small(vec):
    rows, lanes = vec.shape

    def gather_body(x_ref, out_ref, send_sems, recv_sems):
        x, y, c = lax.axis_index("x"), lax.axis_index("y"), lax.axis_index("c")
        me, sibling = (x, y, c), (x, y, 1 - c)
        chips = [(1 - x, y), (x, 1 - y), (1 - x, 1 - y)]

        def slot(px, py, pc):
            return out_ref.at[4 * px + 2 * py + pc]

        def copy(k, blk, to, src=None):
            return pltpu.make_async_remote_copy(
                src_ref=slot(*blk) if src is None else src, dst_ref=slot(*blk),
                send_sem=send_sems.at[k], recv_sem=recv_sems.at[k], device_id=to, device_id_type=MESH)

        out_ref[4 * x + 2 * y + c] = x_ref[...]
        first = [copy(0, me, sibling, src=x_ref)]
        first += [copy(1 + j, me, (*chip, c), src=x_ref) for j, chip in enumerate(chips)]
        for cp in first:
            cp.start()
        passed = [copy(4 + j, (*chip, c), sibling) for j, chip in enumerate(chips)]
        for j, chip in enumerate(chips):
            copy(1 + j, (*chip, c), me).wait_recv()
            passed[j].start()
        copy(0, sibling, me).wait_recv()
        for j, chip in enumerate(chips):
            copy(4 + j, (*chip, 1 - c), me).wait_recv()
        for cp in first + passed:
            cp.wait_send()

    vmem = pl.BlockSpec(memory_space=pltpu.VMEM)
    allv = pl.pallas_call(
        gather_body, name="small_all_gather", out_shape=_sds((N_DEV, rows, lanes)),
        in_specs=[vmem], out_specs=vmem,
        scratch_shapes=[pltpu.SemaphoreType.DMA((7,)), pltpu.SemaphoreType.DMA((7,))],
    )(vec)

    def sum_body(a_ref, o_ref):
        acc = a_ref[0]
        for d in range(1, N_DEV):
            acc = acc + a_ref[d]
        o_ref[...] = acc

    return pl.pallas_call(sum_body, name="small_sum", out_shape=_sds((rows, lanes)), in_specs=[vmem], out_specs=vmem)(allv)


def _adamw(name, w, g, m, v, ts=512):
    def fn(w_v, g_v, m_v, v_v):
        m_n = ADAM_B1 * m_v + (1.0 - ADAM_B1) * g_v
        v_n = ADAM_B2 * v_v + (1.0 - ADAM_B2) * (g_v * g_v)
        m_hat = m_n / (1.0 - ADAM_B1 ** ADAM_STEP)
        v_hat = v_n / (1.0 - ADAM_B2 ** ADAM_STEP)
        delta = -ADAM_LR * (m_hat / (jnp.sqrt(v_hat) + ADAM_EPS) + ADAM_WD * w_v)
        return delta, m_n, v_n

    return _rowwise(name, fn, [w, g, m, v], [_sds(w.shape)] * 3, ts=ts)


def _pack_small(per_layer):
    flat = jnp.concatenate([per_layer[l][n].reshape(-1) for l in range(DEPTH) for n in SMALL_NAMES])
    return jnp.pad(flat, (0, SMALL_ROWS * LANES - flat.shape[0])).reshape(SMALL_ROWS, LANES)


def _unpack_small(packed):
    flat = packed.reshape(-1)
    per = sum(SMALL_SIZES)
    out = {}
    for n, size, off in zip(SMALL_NAMES, SMALL_SIZES, [sum(SMALL_SIZES[:i]) for i in range(len(SMALL_SIZES))]):
        out[n] = jnp.stack([flat[l * per + off:l * per + off + size] for l in range(DEPTH)])
    return out


def _fwd_bwd(x, target, layer_weights, smalls, on_layer_grads):
    depth = len(smalls)
    cos, sin = _rope_tables(x.shape[0])
    h_f32, h_bf = x, x.astype(BF16)
    saved, weights = [], [layer_weights.first()]
    for l in range(depth):
        behind = layer_weights.comm(l + 1) if l + 1 < depth else None
        h_f32, h_bf, sv, brought = _layer_fwd(h_f32, h_bf, weights[l], smalls[l], cos, sin, behind)
        saved.append(sv)
        if l + 1 < depth:
            weights.append(layer_weights.take(l + 1, brought))
    loss_part, dh = _loss_fn(h_f32, target)
    small_grads, pending = [None] * depth, None
    for l in reversed(range(depth)):
        dh, dws, small_grads[l] = _layer_bwd(dh, weights[l], smalls[l], saved[l], cos, sin, pending)
        pending = on_layer_grads(l, dws)
    return loss_part, dh, small_grads, pending


def kernel(x, w_in, q_a_norm, kv_a_norm, w_uq, w_ukv, a_out_norm, b_out_norm, w_o, ln1_g, ln1_b, w_ff1, w_ff2, ln2_g, ln2_b, loss_target, m_w_in, m_q_a_norm, m_kv_a_norm, m_w_uq, m_w_ukv, m_a_out_norm, m_b_out_norm, m_w_o, m_ln1_g, m_ln1_b, m_w_ff1, m_w_ff2, m_ln2_g, m_ln2_b, v_w_in, v_q_a_norm, v_kv_a_norm, v_w_uq, v_w_ukv, v_a_out_norm, v_b_out_norm, v_w_o, v_ln1_g, v_ln1_b, v_w_ff1, v_w_ff2, v_ln2_g, v_ln2_b):
    c = lax.axis_index("c")
    chip = 2 * lax.axis_index("x") + lax.axis_index("y")
    big = dict(w_in=w_in, w_uq=w_uq, w_ukv=w_ukv, w_o=w_o, w_ff1=w_ff1, w_ff2=w_ff2)
    big_m = dict(w_in=m_w_in, w_uq=m_w_uq, w_ukv=m_w_ukv, w_o=m_w_o, w_ff1=m_w_ff1, w_ff2=m_w_ff2)
    big_v = dict(w_in=v_w_in, w_uq=v_w_uq, w_ukv=v_w_ukv, w_o=v_w_o, w_ff1=v_w_ff1, w_ff2=v_w_ff2)
    small = dict(q_a_norm=q_a_norm, kv_a_norm=kv_a_norm, a_out_norm=a_out_norm, b_out_norm=b_out_norm,
                 ln1_g=ln1_g, ln1_b=ln1_b, ln2_g=ln2_g, ln2_b=ln2_b)
    small_m = dict(q_a_norm=m_q_a_norm, kv_a_norm=m_kv_a_norm, a_out_norm=m_a_out_norm, b_out_norm=m_b_out_norm,
                   ln1_g=m_ln1_g, ln1_b=m_ln1_b, ln2_g=m_ln2_g, ln2_b=m_ln2_b)
    small_v = dict(q_a_norm=v_q_a_norm, kv_a_norm=v_kv_a_norm, a_out_norm=v_a_out_norm, b_out_norm=v_b_out_norm,
                   ln1_g=v_ln1_g, ln1_b=v_ln1_b, ln2_g=v_ln2_g, ln2_b=v_ln2_b)

    layer_weights = _LayerWeights([{n: big[n][l] for n in W_NAMES} for l in range(DEPTH)], c, 2 * chip + c)
    smalls = [{n: small[n][l][None, :] for n in SMALL_NAMES} for l in range(DEPTH)]
    reductions = [None] * DEPTH

    def reduce_layer(l, dws):
        reductions[l] = _LayerReduce(f"l{l}", dws, c, chip)
        return reductions[l]

    loss_part, grad_x, small_grads, last = _fwd_bwd(x[0], loss_target[0], layer_weights, smalls, reduce_layer)
    last.run_alone()
    shard_grads = [r.shards for r in reductions]
    loss = lax.psum(loss_part[0, 0], ("x", "y", "c"))
    grad_x = grad_x[None]

    g_small_packed = _all_reduce_small(_pack_small(small_grads))
    pack_in = lambda d: _pack_small([{n: d[n][l] for n in SMALL_NAMES} for l in range(DEPTH)])
    sd, sm_, sv_ = _adamw("adamw_small", pack_in(small), g_small_packed, pack_in(small_m), pack_in(small_v), ts=SMALL_ROWS)
    g_small, d_small, m_small, v_small = (_unpack_small(t) for t in (g_small_packed, sd, sm_, sv_))

    g_big, d_big, m_big, v_big = {}, {}, {}, {}
    for n in W_NAMES:
        g = jnp.stack([shard_grads[l][n] for l in range(DEPTH)])
        shape = g.shape
        flat = lambda t: t.reshape(shape[0] * shape[1], shape[2])
        d_, m_, v_ = _adamw(f"adamw_{n}", flat(big[n]), flat(g), flat(big_m[n]), flat(big_v[n]))
        g_big[n], d_big[n], m_big[n], v_big[n] = g, d_.reshape(shape), m_.reshape(shape), v_.reshape(shape)

    order = ("w_in", "q_a_norm", "kv_a_norm", "w_uq", "w_ukv", "a_out_norm", "b_out_norm", "w_o", "ln1_g", "ln1_b",
             "w_ff1", "w_ff2", "ln2_g", "ln2_b")
    pick = lambda bigd, smalld: [bigd[n] if n in bigd else smalld[n] for n in order]
    return (loss, grad_x, *pick(g_big, g_small), *pick(d_big, d_small), *pick(m_big, m_small), *pick(v_big, v_small))
```

```python
import math

import jax
import jax.numpy as jnp
from jax import lax
from jax.experimental import pallas as pl
from jax.experimental.pallas import tpu as pltpu

F32, BF16 = jnp.float32, jnp.bfloat16

D_MODEL = 1024
DEPTH = 4
A_HEAD_DIM = 64
A_HEADS = 8
A_WIDTH = 512
PATTERNS = ((128, 1), (512, 4), (2048, 16))
SPAN = 128
MLA_HEADS = 4
QK_NOPE = 128
QK_ROPE = 64
V_DIM = 128
Q_LORA = 256
KV_LORA = 128
MLA_WIDTH = 512
QK_PAD = 256
IN_COLS = 1984
IN_PAD = 2048
UQ_PAD = 1024
D_FF = 4096
ROPE_THETA = 10000.0
ALPHA = (2.0 * DEPTH) ** 0.25
LN_EPS = 1e-5
RMS_EPS = 1e-6
MLA_SCALE = (QK_NOPE + QK_ROPE) ** -0.5
A_SCALE = A_HEAD_DIM ** -0.5

ADAM_LR, ADAM_B1, ADAM_B2, ADAM_EPS, ADAM_WD, ADAM_STEP = 0.001, 0.9, 0.999, 1e-08, 0.01, 10

VMEM_LIMIT_BYTES = 56 * 1024 * 1024
NEG = -1e30
N_CHIPS, N_DEV = 4, 8
LANES = 128

SMALL_NAMES = ("q_a_norm", "kv_a_norm", "a_out_norm", "b_out_norm", "ln1_g", "ln1_b", "ln2_g", "ln2_b")
SMALL_SIZES = (256, 128, 512, 512, 1024, 1024, 1024, 1024)
SMALL_ROWS = 176


def _params(sem):
    return pltpu.CompilerParams(dimension_semantics=sem, vmem_limit_bytes=VMEM_LIMIT_BYTES)


def _interleave(parts, scratch):
    d, (n, width) = len(parts), parts[0].shape
    for r in range(d):
        for s in range(width // LANES):
            scratch.at[s][pl.ds(r, n, stride=d), :] = parts[r][:, s * LANES:(s + 1) * LANES]
    return jnp.concatenate([scratch[s] for s in range(width // LANES)], axis=1)


def _deinterleave(x, d, scratch):
    n, width = x.shape[0] // d, x.shape[1]
    for s in range(width // LANES):
        scratch[s] = x[:, s * LANES:(s + 1) * LANES]
    return [jnp.concatenate([scratch.at[s][pl.ds(r, n, stride=d), :] for s in range(width // LANES)], axis=1) for r in range(d)]


def _rowwise(name, fn, ins, outs, consts=(), reds=(), ts=512, in_dils=None, out_dils=None):
    in_dils = list(in_dils or [1] * len(ins))
    out_dils = list(out_dils or [1] * len(outs))
    n_rows = ins[0].shape[-2] * in_dils[0]
    ts = min(ts, n_rows)
    assert n_rows % ts == 0
    n_in, n_c, n_o, n_r = len(ins), len(consts), len(outs), len(reds)
    viewed = [(a.shape, d) for a, d in zip(list(ins) + list(outs), in_dils + out_dils) if d > 1]

    def tile_spec(shape, d=1):
        if len(shape) == 2:
            return pl.BlockSpec((ts // d, shape[1]), lambda i: (i, 0))
        return pl.BlockSpec((shape[0], ts, shape[2]), lambda i: (0, i, 0))

    def whole_spec(shape):
        return pl.BlockSpec(shape, lambda i: (0,) * len(shape))

    def body(*refs):
        in_refs, c_refs = refs[:n_in], refs[n_in:n_in + n_c]
        o_refs = refs[n_in + n_c:n_in + n_c + n_o]
        r_refs = refs[n_in + n_c + n_o:n_in + n_c + n_o + n_r]
        scratches = list(refs[n_in + n_c + n_o + n_r:])
        vals = []
        for r, d in zip(in_refs, in_dils):
            if d == 1:
                vals.append(r[...])
            else:
                width = r.shape[1] // d
                vals.append(_interleave([r[:, k * width:(k + 1) * width] for k in range(d)], scratches.pop(0)))
        res = fn(*vals, *[r[...] for r in c_refs])
        res = tuple(res) if isinstance(res, (tuple, list)) else (res,)
        for r, v, d in zip(o_refs, res[:n_o], out_dils):
            if len(r.shape) == 3:
                for g in range(r.shape[0]):
                    r[g] = v[g].astype(r.dtype)
            elif d == 1:
                r[...] = v.astype(r.dtype)
            else:
                width = r.shape[1] // d
                for k, part in enumerate(_deinterleave(v, d, scratches.pop(0))):
                    r[:, k * width:(k + 1) * width] = part.astype(r.dtype)
        if n_r:
            i = pl.program_id(0)

            @pl.when(i == 0)
            def _():
                for r, v in zip(r_refs, res[n_o:]):
                    r[...] = v

            @pl.when(i > 0)
            def _():
                for r, v in zip(r_refs, res[n_o:]):
                    r[...] += v

    out = pl.pallas_call(
        body, name=name, grid=(n_rows // ts,),
        in_specs=[tile_spec(a.shape, d) for a, d in zip(ins, in_dils)] + [whole_spec(c.shape) for c in consts],
        out_specs=[tile_spec(o.shape, d) for o, d in zip(outs, out_dils)] + [whole_spec(r.shape) for r in reds],
        out_shape=list(outs) + list(reds),
        scratch_shapes=[pltpu.VMEM((shape[1] // d // LANES, ts, LANES), F32) for shape, d in viewed],
        compiler_params=_params(("arbitrary",)),
    )(*ins, *consts)
    return out


def _sds(shape, dtype=F32):
    return jax.ShapeDtypeStruct(tuple(shape), dtype)


def _hosted(comm, grid):
    if comm is None:
        return [], [], [], [], (lambda *a: None), (lambda *a: None)

    def edge(which, at, ins, outs, sems):
        ids = [pl.program_id(d) for d in range(len(grid))]
        hit = ids[0] == at[0]
        for i, v in zip(ids[1:], at[1:]):
            hit = hit & (i == v)

        @pl.when(hit)
        def _():
            which(ins, outs, *sems)

    begin = lambda ins, outs, sems: edge(comm.start, [0] * len(grid), ins, outs, sems)
    end = lambda ins, outs, sems: edge(comm.finish, [g - 1 for g in grid], ins, outs, sems)
    return comm.ins, [ANY] * len(comm.ins), comm.out_shapes, [ANY] * len(comm.out_shapes), begin, end


def _mm(name, a, b, out_dtypes, *, ta=False, tb=False, tm=512, tn=512, tk=512, epilogue=None, extras=(), vecs=(),
        comm=None, b_pieces=False, out_pieces=False):
    (k_dim, m_dim) = a.shape if ta else a.shape[::-1]
    if b_pieces:
        b_rows, b_cols = 2 * b.shape[2], N_CHIPS * b.shape[3]
        (n_dim, k2) = (b_rows, b_cols) if tb else (b_cols, b_rows)
    else:
        (n_dim, k2) = b.shape if tb else b.shape[::-1]
    assert k_dim == k2
    tm, tn, tk = min(tm, m_dim), min(tn, n_dim), min(tk, k_dim)
    assert m_dim % tm == 0 and n_dim % tn == 0 and k_dim % tk == 0
    nk = k_dim // tk
    grid = (m_dim // tm, n_dim // tn, nk)
    a_spec = pl.BlockSpec((tk, tm), lambda i, j, k: (k, i)) if ta else pl.BlockSpec((tm, tk), lambda i, j, k: (i, k))
    if b_pieces:
        assert (tn, tk) == (b_rows, b.shape[3]) if tb else (tk, tn) == (b_rows, b.shape[3])
        b_spec = pl.BlockSpec((None,) + b.shape[1:], (lambda i, j, k: (k, 0, 0, 0)) if tb else (lambda i, j, k: (j, 0, 0, 0)))
    else:
        b_spec = pl.BlockSpec((tn, tk), lambda i, j, k: (j, k)) if tb else pl.BlockSpec((tk, tn), lambda i, j, k: (k, j))
    assert not out_pieces or (tm == m_dim and len(out_dtypes) == 1)
    dims = (((0 if ta else 1,), (1 if tb else 0,)), ((), ()))
    n_e, n_v, n_o = len(extras), len(vecs), len(out_dtypes)
    c_ins, c_in_specs, c_outs, c_out_specs, begin, end = _hosted(comm, grid)
    n_ci, n_co = len(c_ins), len(c_outs)

    def body(a_ref, b_ref, *refs):
        e_refs, v_refs, ci_refs = refs[:n_e], refs[n_e:n_e + n_v], refs[n_e + n_v:n_e + n_v + n_ci]
        o_refs = refs[n_e + n_v + n_ci:n_e + n_v + n_ci + n_o]
        co_refs = refs[n_e + n_v + n_ci + n_o:n_e + n_v + n_ci + n_o + n_co]
        scratch = refs[n_e + n_v + n_ci + n_o + n_co:]
        sems = scratch[-2:]
        begin(ci_refs, co_refs, sems)
        b_val = b_ref[...].reshape(b_rows, b.shape[3]) if b_pieces else b_ref[...]
        part = lax.dot_general(a_ref[...].astype(BF16), b_val.astype(BF16), dims, preferred_element_type=F32)

        def finish(acc):
            outs = epilogue(acc, *[r[...] for r in e_refs], *[r[...] for r in v_refs]) if epilogue else (acc,)
            for r, v in zip(o_refs, outs):
                r[...] = (v.reshape(2, tm // 2, tn) if out_pieces else v).astype(r.dtype)

        if nk == 1:
            finish(part)
        else:
            acc_ref = scratch[0]
            k = pl.program_id(2)

            @pl.when(k == 0)
            def _():
                acc_ref[...] = part

            @pl.when(k > 0)
            def _():
                acc_ref[...] += part

            @pl.when(k == nk - 1)
            def _():
                finish(acc_ref[...])
        end(ci_refs, co_refs, sems)

    tile = pl.BlockSpec((tm, tn), lambda i, j, k: (i, j))
    if out_pieces:
        out_tile, out_shape = pl.BlockSpec((None, 2, tm // 2, tn), lambda i, j, k: (j, 0, 0, 0)), (n_dim // tn, 2, tm // 2, tn)
    else:
        out_tile, out_shape = tile, (m_dim, n_dim)
    out = pl.pallas_call(
        body, name=name, grid=grid,
        in_specs=[a_spec, b_spec] + [tile] * n_e + [pl.BlockSpec((1, tn), lambda i, j, k: (0, j))] * n_v + c_in_specs,
        out_specs=[out_tile] * n_o + c_out_specs,
        out_shape=[_sds(out_shape, d) for d in out_dtypes] + c_outs,
        scratch_shapes=([pltpu.VMEM((tm, tn), F32)] if nk > 1 else []) + (comm.sem_scratch() if comm else []),
        compiler_params=_params(("arbitrary",) * 3 if comm else ("parallel", "parallel", "arbitrary")),
    )(a, b, *extras, *vecs, *c_ins)
    return out


def _swap_halves(x):
    width = x.shape[1]
    lane = lax.broadcasted_iota(jnp.int32, x.shape, 1)
    return jnp.where((lane % 64) < 32, pltpu.roll(x, width - 32, 1), pltpu.roll(x, 32, 1))


def _rope(x, cos, sin_signed):
    return x * cos + _swap_halves(x) * sin_signed


def _rope_t(d, cos, sin_signed):
    return d * cos - _swap_halves(d) * sin_signed


def _rms(x, g):
    rstd = lax.rsqrt(jnp.mean(x * x, axis=-1, keepdims=True) + RMS_EPS)
    return x * rstd * g


def _rms_bwd(x, g, dy):
    rstd = lax.rsqrt(jnp.mean(x * x, axis=-1, keepdims=True) + RMS_EPS)
    xh = x * rstd
    dyg = dy * g
    dx = rstd * (dyg - xh * jnp.mean(dyg * xh, axis=-1, keepdims=True))
    return dx, jnp.sum(dy * xh, axis=0, keepdims=True)


def _layer_norm(r, g, b):
    mu = jnp.mean(r, axis=-1, keepdims=True)
    xc = r - mu
    var = jnp.mean(xc * xc, axis=-1, keepdims=True)
    return xc * lax.rsqrt(var + LN_EPS) * g + b


def _dot(a, b, dims):
    return lax.dot_general(a, b, (dims, ((), ())), preferred_element_type=F32)


_NN, _NT, _TN = ((1,), (0,)), ((1,), (1,)), ((0,), (0,))


def _in_proj(x_bf, w_in_p, cos, sin):
    n_rows = x_bf.shape[0]
    ts = 512
    dils = [dil for _, dil in PATTERNS]
    n_p = len(dils)

    def body(x_ref, w_ref, cos_ref, sin_ref, *refs):
        view_refs, (cq_ref, kpe_ref, ckv_ref, scratch) = refs[:n_p], refs[n_p:]
        xb = x_ref[...]
        cos_v, sin_v = cos_ref[...], sin_ref[...]

        def seg(lo, hi):
            return _dot(xb, w_ref[:, lo:hi], _NN)

        def put(part, val):
            for ref, dil in zip(view_refs, dils):
                pieces = [val] if dil == 1 else _deinterleave(val, dil, scratch)
                for r, piece in enumerate(pieces):
                    lo = (3 * r + part) * A_WIDTH
                    ref[:, lo:lo + A_WIDTH] = piece.astype(BF16)

        put(0, _rope(seg(0, 512), cos_v, sin_v) * A_SCALE)
        put(1, _rope(seg(512, 1024), cos_v, sin_v))
        put(2, seg(1024, 1536))
        cq_ref[...] = seg(1536, 1792)
        kpe_ref[...] = _rope(seg(1792, 1920), cos_v[:, :LANES], sin_v[:, :LANES]).astype(BF16)
        ckv_ref[...] = seg(1920, 2048)

    row = lambda c, d=1: pl.BlockSpec((ts // d, c * d), lambda i: (i, 0))
    return pl.pallas_call(
        body, name="in_proj", grid=(n_rows // ts,),
        in_specs=[row(D_MODEL), pl.BlockSpec((D_MODEL, IN_PAD), lambda i: (0, 0)), row(A_WIDTH), row(A_WIDTH)],
        out_specs=[row(3 * A_WIDTH, d) for d in dils] + [row(Q_LORA), row(LANES), row(KV_LORA)],
        out_shape=[_sds((n_rows // d, d * 3 * A_WIDTH), BF16) for d in dils]
        + [_sds((n_rows, Q_LORA)), _sds((n_rows, LANES), BF16), _sds((n_rows, KV_LORA))],
        scratch_shapes=[pltpu.VMEM((A_WIDTH // LANES, ts, LANES), F32)],
        compiler_params=_params(("parallel",)),
    )(x_bf, w_in_p, cos, sin)


def _band_mask(m, heads):
    qi = lax.broadcasted_iota(jnp.int32, (heads * SPAN, 2 * SPAN), 0) % SPAN
    kj = lax.broadcasted_iota(jnp.int32, (heads * SPAN, 2 * SPAN), 1)
    return ((kj < SPAN) & (kj >= qi) & (m > 0)) | ((kj >= SPAN) & ((kj - SPAN) <= qi))


def _even_lanes(rows):
    return lax.broadcasted_iota(jnp.int32, (rows, LANES), 1) < A_HEAD_DIM


DIL_GROUP = 4


def _dil_fwd(qkv_view, dil, comm=None):
    nb = qkv_view.shape[0] // SPAN
    group = min(dil, DIL_GROUP)
    grid = (dil // group, nb)
    c_ins, c_in_specs, c_outs, c_out_specs, begin, end = _hosted(comm, grid)
    n_ci, n_co = len(c_ins), len(c_outs)

    def body(cur_ref, prev_ref, *refs):
        ci_refs, ol_ref, co_refs, sems = refs[:n_ci], refs[n_ci], refs[n_ci + 1:n_ci + 1 + n_co], refs[n_ci + 1 + n_co:]
        begin(ci_refs, co_refs, sems)
        mask = _band_mask(pl.program_id(1), 1)
        even, even2 = _even_lanes(SPAN), _even_lanes(2 * SPAN)
        for g in range(group):
            for p in range(A_HEADS // 2):
                q_sl, k_sl, v_sl = (slice((3 * g + t) * A_WIDTH + p * LANES, (3 * g + t) * A_WIDTH + (p + 1) * LANES)
                                    for t in range(3))
                q2 = cur_ref[:, q_sl]
                kcat = jnp.concatenate([prev_ref[:, k_sl], cur_ref[:, k_sl]], axis=0)
                vcat = jnp.concatenate([prev_ref[:, v_sl], cur_ref[:, v_sl]], axis=0)
                zero, one = jnp.zeros_like(q2), jnp.ones_like(vcat)
                res, lses = [], []
                for first in (True, False):
                    qh = jnp.where(even, q2, zero) if first else jnp.where(even, zero, q2)
                    vh = jnp.where(even2, vcat, one) if first else jnp.where(even2, one, vcat)
                    s = jnp.where(mask, _dot(qh, kcat, _NT), NEG)
                    mx = jnp.max(s, axis=-1, keepdims=True)
                    r = _dot(jnp.exp(s - mx).astype(BF16), vh, _NN)
                    den = pltpu.roll(r, A_HEAD_DIM, 1)
                    res.append(r / den)
                    lses.append(mx + jnp.log(den))
                o_lo = 2 * g * A_WIDTH + p * LANES
                ol_ref[:, o_lo:o_lo + LANES] = jnp.where(even, res[0], res[1])
                ol_ref[:, o_lo + A_WIDTH:o_lo + A_WIDTH + LANES] = jnp.where(even, lses[0], lses[1])
        end(ci_refs, co_refs, sems)

    return pl.pallas_call(
        body, name=f"dil_fwd_d{dil}", grid=grid,
        in_specs=[pl.BlockSpec((SPAN, group * 3 * A_WIDTH), lambda r, m: (m, r)),
                  pl.BlockSpec((SPAN, group * 3 * A_WIDTH), lambda r, m: (jnp.maximum(m - 1, 0), r))] + c_in_specs,
        out_specs=[pl.BlockSpec((SPAN, group * 2 * A_WIDTH), lambda r, m: (m, r))] + c_out_specs,
        out_shape=[_sds((qkv_view.shape[0], dil * 2 * A_WIDTH))] + c_outs,
        scratch_shapes=comm.sem_scratch() if comm else [],
        compiler_params=_params(("arbitrary", "arbitrary") if comm else ("parallel", "arbitrary")),
    )(qkv_view, qkv_view, *c_ins)


def _mix_fwd(ols, b_out, a_g, b_g):
    def fn(ol1, ol2, ol3, b, ag, bg):
        (o1, l1), (o2, l2), (o3, l3) = [(t[:, :A_WIDTH], t[:, A_WIDTH:]) for t in (ol1, ol2, ol3)]
        mx = jnp.maximum(jnp.maximum(l1, l2), l3)
        e1, e2, e3 = jnp.exp(l1 - mx), jnp.exp(l2 - mx), jnp.exp(l3 - mx)
        a = (e1 * o1 + e2 * o2 + e3 * o3) / (e1 + e2 + e3)
        return a, jnp.concatenate([_rms(a, ag), _rms(b, bg)], axis=1)

    n_rows = b_out.shape[0]
    return _rowwise("mix_fwd", fn, [*ols, b_out], [_sds((n_rows, A_WIDTH)), _sds((n_rows, 2 * A_WIDTH), BF16)],
                    consts=[a_g, b_g], in_dils=[dil for _, dil in PATTERNS] + [1])


def _mla_q_prep(cq, g, w_uq_p, cos, sin):
    def fn(cq_v, cos_v, sin_v, g_v, w_v):
        cqn = _rms(cq_v, g_v).astype(BF16)
        q = _dot(cqn, w_v, _NN)
        qf = [jnp.concatenate([q[:, h * QK_NOPE:(h + 1) * QK_NOPE],
                               _rope(q[:, 512 + h * LANES:512 + (h + 1) * LANES], cos_v[:, :LANES], sin_v[:, :LANES])], axis=1)
              for h in range(MLA_HEADS)]
        return cqn, qf

    n_rows = cq.shape[0]
    return _rowwise("mla_q_prep", fn, [cq, cos, sin], [_sds((n_rows, Q_LORA), BF16), _sds((MLA_HEADS, n_rows, QK_PAD), BF16)],
                    consts=[g, w_uq_p])


def _mla_kv_prep(ckv, kpe, g, w_ukv_p):
    def fn(ckv_v, kpe_v, g_v, w_v):
        ckvn = _rms(ckv_v, g_v).astype(BF16)
        kvv = _dot(ckvn, w_v, _NN)
        kf = [jnp.concatenate([kvv[:, h * QK_NOPE:(h + 1) * QK_NOPE], kpe_v.astype(F32)], axis=1) for h in range(MLA_HEADS)]
        return ckvn, kf, kvv[:, 512:]

    n_rows = ckv.shape[0]
    return _rowwise("mla_kv_prep", fn, [ckv, kpe],
                    [_sds((n_rows, KV_LORA), BF16), _sds((MLA_HEADS, n_rows, QK_PAD), BF16), _sds((n_rows, MLA_WIDTH), BF16)],
                    consts=[g, w_ukv_p])


MLA_FWD_TILES = (1024, 2048)
MLA_BWD_TILES = (1024, 1024)


def _mla_fwd(qf, kf, v, comm=None):
    n_rows = v.shape[0]
    tq, tk = min(MLA_FWD_TILES[0], n_rows), min(MLA_FWD_TILES[1], n_rows)
    nq, nk = n_rows // tq, n_rows // tk
    grid = (MLA_HEADS, nq, nk)
    exp2_scale = MLA_SCALE * math.log2(math.e)
    c_ins, c_in_specs, c_outs, c_out_specs, begin, end = _hosted(comm, grid)
    n_ci, n_co = len(c_ins), len(c_outs)

    def last_k(i):
        return jnp.right_shift(i * tq + tq - 1, int(math.log2(tk)))

    def body(q_ref, k_ref, v_ref, *refs):
        ci_refs, (o_ref, lse_ref), co_refs = refs[:n_ci], refs[n_ci:n_ci + 2], refs[n_ci + 2:n_ci + 2 + n_co]
        m_sc, l_sc, acc_sc = refs[n_ci + 2 + n_co:n_ci + 5 + n_co]
        sems = refs[n_ci + 5 + n_co:]
        i, j = pl.program_id(1), pl.program_id(2)
        begin(ci_refs, co_refs, sems)

        @pl.when(j == 0)
        def _():
            m_sc[...] = jnp.full(m_sc.shape, NEG, F32)
            l_sc[...] = jnp.zeros(l_sc.shape, F32)
            acc_sc[...] = jnp.zeros(acc_sc.shape, F32)

        def step(masked):
            s = _dot(q_ref[...], k_ref[...], _NT)
            if masked:
                row = lax.broadcasted_iota(jnp.int32, (tq, tk), 0) + i * tq
                col = lax.broadcasted_iota(jnp.int32, (tq, tk), 1) + j * tk
                s = jnp.where(col <= row, s, NEG)
            m_prev = m_sc[...]
            m_new = jnp.maximum(m_prev, jnp.max(s, axis=-1, keepdims=True))
            alpha = jnp.exp2((m_prev - m_new) * exp2_scale)
            p = jnp.exp2((s - m_new) * exp2_scale)
            l_sc[...] = alpha * l_sc[...] + jnp.sum(p, axis=-1, keepdims=True)
            acc_sc[...] = alpha * acc_sc[...] + _dot(p.astype(BF16), v_ref[...], _NN)
            m_sc[...] = m_new

        active = j * tk <= i * tq + tq - 1
        crosses = (j + 1) * tk - 1 > i * tq

        @pl.when(active & jnp.logical_not(crosses))
        def _():
            step(False)

        @pl.when(active & crosses)
        def _():
            step(True)

        @pl.when(j == last_k(i))
        def _():
            o_ref[...] = acc_sc[...] / l_sc[...]
            lse_ref[...] = jnp.broadcast_to(m_sc[...] * MLA_SCALE + jnp.log(l_sc[...]), (tq, V_DIM))

        end(ci_refs, co_refs, sems)

    return pl.pallas_call(
        body, name="mla_fwd", grid=grid,
        in_specs=[pl.BlockSpec((None, tq, QK_PAD), lambda h, i, j: (h, i, 0)),
                  pl.BlockSpec((None, tk, QK_PAD), lambda h, i, j: (h, jnp.minimum(j, last_k(i)), 0)),
                  pl.BlockSpec((tk, V_DIM), lambda h, i, j: (jnp.minimum(j, last_k(i)), h))] + c_in_specs,
        out_specs=[pl.BlockSpec((tq, V_DIM), lambda h, i, j: (i, h)), pl.BlockSpec((tq, V_DIM), lambda h, i, j: (i, h))]
        + c_out_specs,
        out_shape=[_sds((n_rows, MLA_WIDTH)), _sds((n_rows, MLA_WIDTH))] + c_outs,
        scratch_shapes=[pltpu.VMEM((tq, 1), F32), pltpu.VMEM((tq, 1), F32), pltpu.VMEM((tq, V_DIM), F32)]
        + (comm.sem_scratch() if comm else []),
        compiler_params=_params(("arbitrary",) * 3 if comm else ("parallel", "parallel", "arbitrary")),
    )(qf, kf, v, *c_ins)


def _mm_res_ln(name, a, w, xres, g, b):
    def epi(acc, xr, g_v, b_v):
        r = ALPHA * xr + acc
        y = _layer_norm(r, g_v, b_v)
        return y, y, r

    return _mm(name, a, w, (F32, BF16, F32), tm=1024, tn=D_MODEL, tk=1024, epilogue=epi, extras=[xres], vecs=[g, b])


def _mm_relu2(x_bf, w):
    def epi(acc):
        r = jnp.maximum(acc, 0.0)
        return (r * r,)

    return _mm("ff1", x_bf, w, (BF16,), tm=1024, tn=1024, tk=D_MODEL, epilogue=epi, b_pieces=True)[0]


def _loss_fn(y, t):
    def fn(y_v, t_v):
        d = y_v - t_v
        part = jnp.sum(jnp.sum(d * d, axis=1, keepdims=True), axis=0, keepdims=True)
        return d * (1.0 / D_MODEL), part

    dy, part = _rowwise("loss", fn, [y, t], [_sds(y.shape)], reds=[_sds((1, 1))])
    return part * (0.5 / D_MODEL), dy


def _ln_bwd(name, dy, r, g):
    def fn(dy_v, r_v, g_v):
        mu = jnp.mean(r_v, axis=-1, keepdims=True)
        xc = r_v - mu
        rstd = lax.rsqrt(jnp.mean(xc * xc, axis=-1, keepdims=True) + LN_EPS)
        xh = xc * rstd
        dxh = dy_v * g_v
        dr = rstd * (dxh - jnp.mean(dxh, axis=-1, keepdims=True) - xh * jnp.mean(dxh * xh, axis=-1, keepdims=True))
        return dr, dr, jnp.sum(dy_v * xh, axis=0, keepdims=True), jnp.sum(dy_v, axis=0, keepdims=True)

    return _rowwise(name, fn, [dy, r], [_sds(dy.shape), _sds(dy.shape, BF16)], consts=[g],
                    reds=[_sds((1, D_MODEL)), _sds((1, D_MODEL))])


def _head_sum_matrix():
    i = lax.broadcasted_iota(jnp.int32, (A_WIDTH, A_WIDTH), 0) // A_HEAD_DIM
    j = lax.broadcasted_iota(jnp.int32, (A_WIDTH, A_WIDTH), 1) // A_HEAD_DIM
    return (i == j).astype(BF16)


def _mix_bwd(dmixed, a_out, b_out, ols, a_g, b_g):
    def fn(dm, a, b, ol1, ol2, ol3, ag, bg, e_mat):
        l1, l2, l3 = ol1[:, A_WIDTH:], ol2[:, A_WIDTH:], ol3[:, A_WIDTH:]
        da, dga = _rms_bwd(a, ag, dm[:, :A_WIDTH])
        db, dgb = _rms_bwd(b, bg, dm[:, A_WIDTH:])
        t = da * a
        t_hi = t.astype(BF16)
        t_lo = (t - t_hi.astype(F32)).astype(BF16)
        tsum = _dot(t_hi, e_mat, _NN) + _dot(t_lo, e_mat, _NN)
        mx = jnp.maximum(jnp.maximum(l1, l2), l3)
        e1, e2, e3 = jnp.exp(l1 - mx), jnp.exp(l2 - mx), jnp.exp(l3 - mx)
        inv = 1.0 / (e1 + e2 + e3)
        w1, w2, w3 = e1 * inv, e2 * inv, e3 * inv
        tb = db * b
        delta_b = jnp.concatenate(
            [jnp.broadcast_to(jnp.sum(tb[:, h * V_DIM:(h + 1) * V_DIM], axis=-1, keepdims=True), (tb.shape[0], V_DIM))
             for h in range(MLA_HEADS)], axis=1)
        dods = [jnp.concatenate([w * da, w * tsum], axis=1) for w in (w1, w2, w3)]
        return (*dods, db, delta_b, dga, dgb)

    n_rows = a_out.shape[0]
    wide = (n_rows, A_WIDTH)
    dils = [dil for _, dil in PATTERNS]
    return _rowwise("mix_bwd", fn, [dmixed, a_out, b_out, *ols],
                    [_sds((n_rows // d, d * 2 * A_WIDTH)) for d in dils] + [_sds(wide, BF16), _sds(wide)],
                    consts=[a_g, b_g, _head_sum_matrix()], reds=[_sds((1, A_WIDTH)), _sds((1, MLA_WIDTH))], ts=256,
                    in_dils=[1, 1, 1] + dils, out_dils=dils + [1, 1])


def _dil_bwd(qkv_view, ol_view, dod_view, dil):
    nb = qkv_view.shape[0] // SPAN
    group = min(dil, DIL_GROUP)

    def body(cur_ref, prev_ref, ol_ref, dod_ref, out_ref, carry):
        m = pl.program_id(1)

        @pl.when(m == 0)
        def _():
            carry[...] = jnp.zeros(carry.shape, F32)

        @pl.when(m < nb)
        def _():
            mask, even = _band_mask(m, 2), _even_lanes(SPAN)
            for g in range(group):
                for p in range(A_HEADS // 2):
                    q_sl, k_sl, v_sl = (slice((3 * g + t) * A_WIDTH + p * LANES, (3 * g + t) * A_WIDTH + (p + 1) * LANES)
                                        for t in range(3))
                    do_sl = slice(2 * g * A_WIDTH + p * LANES, 2 * g * A_WIDTH + (p + 1) * LANES)
                    l_sl = slice((2 * g + 1) * A_WIDTH + p * LANES, (2 * g + 1) * A_WIDTH + (p + 1) * LANES)
                    q2, do2 = cur_ref[:, q_sl], dod_ref[:, do_sl].astype(BF16)
                    zero = jnp.zeros_like(q2)
                    qcat = jnp.concatenate([jnp.where(even, q2, zero), jnp.where(even, zero, q2)], axis=0)
                    docat = jnp.concatenate([jnp.where(even, do2, zero), jnp.where(even, zero, do2)], axis=0)
                    kcat = jnp.concatenate([prev_ref[:, k_sl], cur_ref[:, k_sl]], axis=0)
                    vcat = jnp.concatenate([prev_ref[:, v_sl], cur_ref[:, v_sl]], axis=0)
                    lse2, dl2 = ol_ref[:, l_sl], dod_ref[:, l_sl]
                    lse_c = jnp.concatenate([lse2[:, :1], lse2[:, A_HEAD_DIM:A_HEAD_DIM + 1]], axis=0)
                    dl_c = jnp.concatenate([dl2[:, :1], dl2[:, A_HEAD_DIM:A_HEAD_DIM + 1]], axis=0)
                    pr = jnp.exp(jnp.where(mask, _dot(qcat, kcat, _NT), NEG) - lse_c)
                    ds = (pr * (_dot(docat, vcat, _NT) - dl_c)).astype(BF16)
                    dq = _dot(ds, kcat, _NN)
                    dk2 = _dot(ds, qcat, _TN)
                    dv2 = _dot(pr.astype(BF16), docat, _TN)
                    out_ref[:, q_sl] = carry[:, q_sl]
                    out_ref[:, k_sl] = carry[:, k_sl] + dk2[:SPAN]
                    out_ref[:, v_sl] = carry[:, v_sl] + dv2[:SPAN]
                    carry[:, q_sl] = jnp.where(even, dq[:SPAN], dq[SPAN:])
                    carry[:, k_sl] = dk2[SPAN:]
                    carry[:, v_sl] = dv2[SPAN:]

        @pl.when(m == nb)
        def _():
            out_ref[...] = carry[...]

    last = nb - 1

    def cur(width):
        return pl.BlockSpec((SPAN, group * width), lambda r, m: (jnp.minimum(m, last), r))

    def prev(width):
        return pl.BlockSpec((SPAN, group * width), lambda r, m: (jnp.clip(m - 1, 0, last), r))

    return pl.pallas_call(
        body, name=f"dil_bwd_d{dil}", grid=(dil // group, nb + 1),
        in_specs=[cur(3 * A_WIDTH), prev(3 * A_WIDTH), cur(2 * A_WIDTH), cur(2 * A_WIDTH)],
        out_specs=prev(3 * A_WIDTH),
        out_shape=_sds(qkv_view.shape),
        scratch_shapes=[pltpu.VMEM((SPAN, group * 3 * A_WIDTH), F32)],
        compiler_params=_params(("parallel", "arbitrary")),
    )(qkv_view, qkv_view, ol_view, dod_view)


def _mla_bwd(qf, kf, v, lse, do, delta, comm=None):
    n_rows = v.shape[0]
    tq, tk = min(MLA_BWD_TILES[0], n_rows), min(MLA_BWD_TILES[1], n_rows)
    nq, nk = n_rows // tq, n_rows // tk

    def first_q(j):
        return jnp.right_shift(j * tk, int(math.log2(tq)))

    grid = (MLA_HEADS, nk, nq)
    c_ins, c_in_specs, c_outs, c_out_specs, begin, end = _hosted(comm, grid)
    n_ci, n_co = len(c_ins), len(c_outs)

    def body(q_ref, k_ref, v_ref, lse_ref, do_ref, dl_ref, *refs):
        ci_refs, (dq_ref, dk_ref, dv_ref), co_refs = refs[:n_ci], refs[n_ci:n_ci + 3], refs[n_ci + 3:n_ci + 3 + n_co]
        dk_sc, dv_sc = refs[n_ci + 3 + n_co:n_ci + 5 + n_co]
        sems = refs[n_ci + 5 + n_co:]
        j, i = pl.program_id(1), pl.program_id(2)
        begin(ci_refs, co_refs, sems)

        @pl.when((j == 0) & (i == 0))
        def _():
            dq_ref[...] = jnp.zeros(dq_ref.shape, F32)

        @pl.when(i == first_q(j))
        def _():
            dk_sc[...] = jnp.zeros(dk_sc.shape, F32)
            dv_sc[...] = jnp.zeros(dv_sc.shape, F32)

        def step(masked):
            q, k, dob = q_ref[...], k_ref[...], do_ref[...]
            s = _dot(q, k, _NT) * MLA_SCALE
            if masked:
                row = lax.broadcasted_iota(jnp.int32, (tq, tk), 0) + i * tq
                col = lax.broadcasted_iota(jnp.int32, (tq, tk), 1) + j * tk
                s = jnp.where(col <= row, s, NEG)
            p = jnp.exp(s - lse_ref[:, :1])
            dp = _dot(dob, v_ref[...], _NT)
            ds = (p * (dp - dl_ref[:, :1]) * MLA_SCALE).astype(BF16)
            dv_sc[...] += _dot(p.astype(BF16), dob, _TN)
            dk_sc[...] += _dot(ds, q, _TN)
            rows = pl.ds(pl.multiple_of(i * tq, tq), tq)
            dq_ref[rows, :] += _dot(ds, k, _NN)

        active = i >= first_q(j)
        crosses = (j + 1) * tk - 1 > i * tq

        @pl.when(active & jnp.logical_not(crosses))
        def _():
            step(False)

        @pl.when(active & crosses)
        def _():
            step(True)

        @pl.when(i == nq - 1)
        def _():
            dk_ref[...] = dk_sc[...]
            dv_ref[...] = dv_sc[...]

        end(ci_refs, co_refs, sems)

    qrow = lambda h, j, i: (jnp.maximum(i, first_q(j)), h)
    return pl.pallas_call(
        body, name="mla_bwd", grid=grid,
        in_specs=[pl.BlockSpec((None, tq, QK_PAD), lambda h, j, i: (h, jnp.maximum(i, first_q(j)), 0)),
                  pl.BlockSpec((None, tk, QK_PAD), lambda h, j, i: (h, j, 0)),
                  pl.BlockSpec((tk, V_DIM), lambda h, j, i: (j, h)),
                  pl.BlockSpec((tq, V_DIM), qrow), pl.BlockSpec((tq, V_DIM), qrow), pl.BlockSpec((tq, V_DIM), qrow)]
        + c_in_specs,
        out_specs=[pl.BlockSpec((None, n_rows, QK_PAD), lambda h, j, i: (h, 0, 0)),
                   pl.BlockSpec((None, tk, QK_PAD), lambda h, j, i: (h, j, 0)),
                   pl.BlockSpec((tk, V_DIM), lambda h, j, i: (j, h))] + c_out_specs,
        out_shape=[_sds((MLA_HEADS, n_rows, QK_PAD)), _sds((MLA_HEADS, n_rows, QK_PAD)), _sds((n_rows, MLA_WIDTH))] + c_outs,
        scratch_shapes=[pltpu.VMEM((tk, QK_PAD), F32), pltpu.VMEM((tk, V_DIM), F32)] + (comm.sem_scratch() if comm else []),
        compiler_params=_params(("arbitrary",) * 3 if comm else ("parallel", "arbitrary", "arbitrary")),
    )(qf, kf, v, lse, do, delta, *c_ins)


def _mla_bwd_prep(dqf, dkf, dv, cos, sin):
    def fn(dqf_v, dkf_v, dv_v, cos_v, sin_v):
        dq = jnp.concatenate([dqf_v[h][:, :QK_NOPE] for h in range(MLA_HEADS)]
                             + [_rope_t(dqf_v[h][:, QK_NOPE:], cos_v[:, :LANES], sin_v[:, :LANES]) for h in range(MLA_HEADS)], axis=1)
        dkv = jnp.concatenate([dkf_v[h][:, :QK_NOPE] for h in range(MLA_HEADS)] + [dv_v], axis=1)
        dkpe = dkf_v[0][:, QK_NOPE:] + dkf_v[1][:, QK_NOPE:] + dkf_v[2][:, QK_NOPE:] + dkf_v[3][:, QK_NOPE:]
        return dq, dkv, dkpe

    n_rows = dv.shape[0]
    return _rowwise("mla_bwd_prep", fn, [dqf, dkf, dv, cos, sin],
                    [_sds((n_rows, UQ_PAD), BF16), _sds((n_rows, 1024), BF16), _sds((n_rows, LANES))], ts=256)


def _assemble_dh(dqkvs, dcqn, cq, dckvn, ckv, dkpe, cos, sin, gq, gkv):
    def fn(g1, g2, g3, dcqn_v, cq_v, dckvn_v, ckv_v, dkpe_v, cos_v, sin_v, gq_v, gkv_v):
        g = g1 + g2 + g3
        dqa = _rope_t(g[:, :A_WIDTH], cos_v, sin_v) * A_SCALE
        dka = _rope_t(g[:, A_WIDTH:2 * A_WIDTH], cos_v, sin_v)
        dcq, dgq = _rms_bwd(cq_v, gq_v, dcqn_v)
        dckv, dgkv = _rms_bwd(ckv_v, gkv_v, dckvn_v)
        dkr = _rope_t(dkpe_v, cos_v[:, :LANES], sin_v[:, :LANES])
        return jnp.concatenate([dqa, dka, g[:, 2 * A_WIDTH:], dcq, dkr, dckv], axis=1), dgq, dgkv

    n_rows = cq.shape[0]
    return _rowwise("assemble_dh", fn, [*dqkvs, dcqn, cq, dckvn, ckv, dkpe, cos, sin],
                    [_sds((n_rows, IN_PAD), BF16)], consts=[gq, gkv],
                    reds=[_sds((1, Q_LORA)), _sds((1, KV_LORA))], ts=256,
                    in_dils=[dil for _, dil in PATTERNS] + [1] * 7)


def _layer_fwd(x_f32, x_bf, w, sm, cos, sin, behind=None, rest=None):
    *qkv_views, cq, kpe, ckv = _in_proj(x_bf, w["w_in"], cos, sin)
    ol_views, brought = [], []
    for i, (view, (_, dil)) in enumerate(zip(qkv_views, PATTERNS)):
        ol, *got = _dil_fwd(view, dil, comm=rest[0][i] if rest else None)
        ol_views.append(ol)
        brought.append(got)
    if rest:
        w = {**w, **rest[1](brought)}
    cqn, qf = _mla_q_prep(cq, sm["q_a_norm"], w["w_uq"], cos, sin)
    ckvn, kf, v = _mla_kv_prep(ckv, kpe, sm["kv_a_norm"], w["w_ukv"])
    b_out, b_lse, *behind_outs = _mla_fwd(qf, kf, v, comm=behind)
    a_out, mixed = _mix_fwd(ol_views, b_out, sm["a_out_norm"], sm["b_out_norm"])
    x1, x1_bf, r1 = _mm_res_ln("wo_ln1", mixed, w["w_o"], x_f32, sm["ln1_g"], sm["ln1_b"])
    f = _mm_relu2(x1_bf, w["w_ff1"])
    x2, x2_bf, r2 = _mm_res_ln("ff2_ln2", f, w["w_ff2"], x1, sm["ln2_g"], sm["ln2_b"])
    saved = dict(w=w, x_bf=x_bf, qkv_views=qkv_views, ol_views=ol_views, cq=cq, ckv=ckv, cqn=cqn, qf=qf, ckvn=ckvn,
                 kf=kf, v=v, b_out=b_out, b_lse=b_lse, a_out=a_out, mixed=mixed, x1_bf=x1_bf, r1=r1, f=f, r2=r2)
    return x2, x2_bf, saved, behind_outs


def _layer_bwd(dx2, w, sm, sv, cos, sin, reduce_of, pending=None):
    dr2, dr2_bf, dg2, db2 = _ln_bwd("ln2_bwd", dx2, sv["r2"], sm["ln2_g"])
    (dw_ff2,) = _mm("dw_ff2", sv["f"], dr2_bf, (F32,), ta=True, tm=1024, tn=1024, tk=2048)
    du, *swapped = _mm("d_u", dr2_bf, w["w_ff2"], (BF16,), tb=True, tm=1024, tn=2048, tk=D_MODEL,
                       epilogue=lambda acc, f: (acc * (2.0 * jnp.sqrt(f.astype(F32))),), extras=[sv["f"]],
                       comm=_swap_comm(pending.pieces) if pending else None)
    if pending:
        pending.after_swap(swapped)
    (dw_ff1,) = _mm("dw_ff1", sv["x1_bf"], du, (F32,), ta=True, tm=1024, tn=1024, tk=2048, out_pieces=True)
    (dy1,) = _mm("d_x1", du, w["w_ff1"], (F32,), tb=True, tm=1024, tn=1024, tk=1024,
                 epilogue=lambda acc, d: (acc + ALPHA * d,), extras=[dr2], b_pieces=True)
    dr1, dr1_bf, dg1, db1 = _ln_bwd("ln1_bwd", dy1, sv["r1"], sm["ln1_g"])
    (dw_o,) = _mm("dw_o", sv["mixed"], dr1_bf, (F32,), ta=True, tm=1024, tn=1024, tk=2048)
    early = reduce_of(dict(w_o=dw_o, w_ff1=dw_ff1, w_ff2=dw_ff2))
    dmixed, *swapped = _mm("d_mixed", dr1_bf, w["w_o"], (F32,), tb=True, tm=1024, tn=1024, tk=D_MODEL,
                           comm=_swap_comm(early.pieces) if early else None)
    if early:
        early.after_swap(swapped)
    dod1, dod2, dod3, do_b, delta_b, dga, dgb = _mix_bwd(
        dmixed, sv["a_out"], sv["b_out"], sv["ol_views"], sm["a_out_norm"], sm["b_out_norm"])
    dqkvs = [_dil_bwd(qkv_view, ol_view, dod_view, dil)
             for (_, dil), qkv_view, ol_view, dod_view in zip(PATTERNS, sv["qkv_views"], sv["ol_views"], (dod1, dod2, dod3))]
    to_scatter = (early.s16 if early else []) + (pending.s16 if pending else [])
    dqf, dkf, dv_b, *scattered = _mla_bwd(sv["qf"], sv["kf"], sv["v"], sv["b_lse"], do_b, delta_b,
                                          comm=_scatter_comm(to_scatter) if to_scatter else None)
    if early:
        early.after_scatter(scattered[:len(early.s16)])
    if pending:
        pending.after_scatter(scattered[len(scattered) - len(pending.s16):])
    dq_b, dkv_b, dkpe = _mla_bwd_prep(dqf, dkf, dv_b, cos, sin)
    (dw_uq,) = _mm("dw_uq", sv["cqn"], dq_b, (F32,), ta=True, tm=Q_LORA, tn=1024, tk=2048)
    (dcqn,) = _mm("d_cqn", dq_b, w["w_uq"], (F32,), tb=True, tm=1024, tn=Q_LORA, tk=UQ_PAD)
    (dw_ukv,) = _mm("dw_ukv", sv["ckvn"], dkv_b, (F32,), ta=True, tm=KV_LORA, tn=1024, tk=2048)
    (dckvn,) = _mm("d_ckvn", dkv_b, w["w_ukv"], (F32,), tb=True, tm=1024, tn=KV_LORA, tk=1024)
    dh, dgq, dgkv = _assemble_dh(dqkvs, dcqn, sv["cq"], dckvn, sv["ckv"], dkpe, cos, sin,
                                 sm["q_a_norm"], sm["kv_a_norm"])
    (dw_in,) = _mm("dw_in", sv["x_bf"], dh, (F32,), ta=True, tm=1024, tn=1024, tk=2048)
    (dx,) = _mm("d_x", dh, w["w_in"], (F32,), tb=True, tm=1024, tn=1024, tk=2048,
                epilogue=lambda acc, d: (acc + ALPHA * d,), extras=[dr1])
    late = reduce_of(dict(w_in=dw_in, w_uq=dw_uq, w_ukv=dw_ukv))
    dsm = dict(q_a_norm=dgq, kv_a_norm=dgkv, a_out_norm=dga, b_out_norm=dgb, ln1_g=dg1, ln1_b=db1, ln2_g=dg2, ln2_b=db2)
    return dx, dsm, late


def _pad_w_in(w):
    return jnp.concatenate([w[:, :1792], w[:, 1920:1984], jnp.zeros((w.shape[0], 64), w.dtype), w[:, 1792:1920]], axis=1)


def _unpad_w_in(w):
    return jnp.concatenate([w[:, :1792], w[:, 1920:2048], w[:, 1792:1856]], axis=1)


def _rope_tables(n_rows):
    half = A_HEAD_DIM // 2
    inv_freq = ROPE_THETA ** (-jnp.arange(half, dtype=F32) / half)
    ang = jnp.arange(n_rows, dtype=F32)[:, None] * inv_freq[None, :]
    cos = jnp.tile(jnp.cos(ang), (1, 2 * A_HEADS))
    sin = jnp.tile(jnp.concatenate([-jnp.sin(ang), jnp.sin(ang)], axis=1), (1, A_HEADS))
    return cos, sin


W_NAMES = ("w_in", "w_uq", "w_ukv", "w_o", "w_ff1", "w_ff2")
PIECE = dict(w_in=(512, 512), w_uq=(128, 256), w_ukv=(64, 256), w_o=(128, 1024), w_ff1=(512, 1024), w_ff2=(512, 1024))
SHARD_COLS = dict(w_in=496, w_uq=192, w_ukv=256, w_o=1024, w_ff1=1024, w_ff2=1024)


def _col_pieces(t, pad_to):
    rows, _, cols = t.shape
    t = jnp.pad(t, ((0, 0), (0, 0), (0, pad_to - cols)))
    return t.reshape(2, rows // 2, N_CHIPS, pad_to).transpose(2, 0, 1, 3)


def _grad_pieces(dws):
    def heads(t, rows):
        return jnp.concatenate([t[:, :512].reshape(rows, MLA_HEADS, LANES), t[:, 512:].reshape(rows, MLA_HEADS, LANES)], axis=2)

    make = dict(
        w_in=lambda t: _col_pieces(_unpad_w_in(t).reshape(D_MODEL, N_CHIPS, SHARD_COLS["w_in"]), 512),
        w_uq=lambda t: _col_pieces(heads(t, Q_LORA), 256),
        w_ukv=lambda t: _col_pieces(heads(t, KV_LORA), 256),
        w_o=lambda t: t.reshape(N_CHIPS, 2, 128, D_MODEL),
        w_ff1=lambda t: t,
        w_ff2=lambda t: t.reshape(N_CHIPS, 2, 512, D_MODEL))
    return {n: make[n](t) for n, t in dws.items()}


def _weights_from_pieces(p):
    def cols(t):
        return t.transpose(1, 2, 0, 3).reshape(2 * t.shape[2], N_CHIPS, t.shape[3])

    def heads(t, rows):
        return jnp.concatenate([t[:, :, :LANES].reshape(rows, 512), t[:, :, LANES:].reshape(rows, 512)], axis=1)

    make = dict(
        w_in=lambda t: _pad_w_in(cols(t)[:, :, :SHARD_COLS["w_in"]].reshape(D_MODEL, IN_COLS)),
        w_uq=lambda t: heads(cols(t), Q_LORA),
        w_ukv=lambda t: heads(cols(t), KV_LORA),
        w_o=lambda t: t.reshape(D_MODEL, D_MODEL),
        w_ff1=lambda t: t,
        w_ff2=lambda t: t.reshape(D_FF, D_MODEL))
    return {n: make[n](t) for n, t in p.items()}


def _my_piece(name, shard, half):
    rows, cols = shard.shape[0] // 2, shard.shape[1]
    t = lax.dynamic_slice_in_dim(shard, half * rows, rows, axis=0).astype(BF16)
    return jnp.pad(t, ((0, 0), (0, PIECE[name][1] - cols)))


MESH = pl.DeviceIdType.MESH
ANY = pl.BlockSpec(memory_space=pl.ANY)


class _Comm:
    def __init__(self, ins, out_shapes, n_sems, start, finish):
        self.ins, self.out_shapes, self.n_sems, self.start, self.finish = list(ins), list(out_shapes), n_sems, start, finish

    def sem_scratch(self):
        return [pltpu.SemaphoreType.DMA((self.n_sems,)), pltpu.SemaphoreType.DMA((self.n_sems,))]


def _comm_call(name, comm):
    n_in, n_out = len(comm.ins), len(comm.out_shapes)

    def body(*refs):
        ins, outs, sems = refs[:n_in], refs[n_in:n_in + n_out], refs[n_in + n_out:]
        comm.start(ins, outs, *sems)
        comm.finish(ins, outs, *sems)

    return pl.pallas_call(body, name=name, out_shape=comm.out_shapes, in_specs=[ANY] * n_in, out_specs=[ANY] * n_out,
                          scratch_shapes=comm.sem_scratch())(*comm.ins)


def _all_gather_comm(blocks):
    n = len(blocks)

    def plan(x_refs, out_refs, send_sems, recv_sems):
        x, y, c = lax.axis_index("x"), lax.axis_index("y"), lax.axis_index("c")
        me, sibling = (x, y, c), (x, y, 1 - c)
        chips = [(1 - x, y), (x, 1 - y), (1 - x, 1 - y)]

        def copy(t, k, blk, to, src=None):
            px, py, pc = blk
            slot = out_refs[t].at[4 * px + 2 * py + pc]
            return pltpu.make_async_remote_copy(
                src_ref=slot if src is None else src, dst_ref=slot,
                send_sem=send_sems.at[7 * t + k], recv_sem=recv_sems.at[7 * t + k], device_id=to, device_id_type=MESH)

        first = []
        for t in range(n):
            first.append(copy(t, 0, me, sibling, src=x_refs[t]))
            first += [copy(t, 1 + j, me, (*chip, c), src=x_refs[t]) for j, chip in enumerate(chips)]
        return me, sibling, chips, c, copy, first

    def start(x_refs, out_refs, send_sems, recv_sems):
        for cp in plan(x_refs, out_refs, send_sems, recv_sems)[-1]:
            cp.start()

    def finish(x_refs, out_refs, send_sems, recv_sems):
        me, sibling, chips, c, copy, sent = plan(x_refs, out_refs, send_sems, recv_sems)
        for j, chip in enumerate(chips):
            for t in range(n):
                copy(t, 1 + j, (*chip, c), me).wait_recv()
                sent.append(copy(t, 4 + j, (*chip, c), sibling))
                sent[-1].start()
        for t in range(n):
            copy(t, 0, sibling, me).wait_recv()
        for j, chip in enumerate(chips):
            for t in range(n):
                copy(t, 4 + j, (*chip, 1 - c), me).wait_recv()
        for cp in sent:
            cp.wait_send()

    return _Comm(blocks, [_sds((N_DEV,) + b.shape, b.dtype) for b in blocks], 7 * n, start, finish)


def _simple_comm(ins, out_shapes, n_sems, copies):
    def start(in_refs, out_refs, send_sems, recv_sems):
        for cp in copies(in_refs, out_refs, send_sems, recv_sems):
            cp.start()

    def finish(in_refs, out_refs, send_sems, recv_sems):
        for cp in copies(in_refs, out_refs, send_sems, recv_sems):
            cp.wait()

    return _Comm(ins, out_shapes, n_sems, start, finish)


def _swap_comm(gs):
    def copies(g_refs, got_refs, send_sems, recv_sems):
        c = lax.axis_index("c")
        sibling = (lax.axis_index("x"), lax.axis_index("y"), 1 - c)
        return [pltpu.make_async_remote_copy(src_ref=g_refs[t].at[k, 1 - c], dst_ref=got_refs[t].at[k],
                                             send_sem=send_sems.at[N_CHIPS * t + k], recv_sem=recv_sems.at[N_CHIPS * t + k],
                                             device_id=sibling, device_id_type=MESH)
                for t in range(len(gs)) for k in range(N_CHIPS)]

    return _simple_comm(gs, [_sds((N_CHIPS,) + g.shape[2:], g.dtype) for g in gs], N_CHIPS * len(gs), copies)


def _scatter_comm(hs):
    def copies(h_refs, got_refs, send_sems, recv_sems):
        x, y, c = lax.axis_index("x"), lax.axis_index("y"), lax.axis_index("c")
        chips = [(1 - x, y), (x, 1 - y), (1 - x, 1 - y)]
        return [pltpu.make_async_remote_copy(src_ref=h_refs[t].at[2 * px + py], dst_ref=got_refs[t].at[rel],
                                             send_sem=send_sems.at[3 * t + rel], recv_sem=recv_sems.at[3 * t + rel],
                                             device_id=(px, py, c), device_id_type=MESH)
                for t in range(len(hs)) for rel, (px, py) in enumerate(chips)]

    return _simple_comm(hs, [_sds((3,) + h.shape[1:], h.dtype) for h in hs], 3 * len(hs), copies)


def _share_comm(rs):
    def copies(r_refs, got_refs, send_sems, recv_sems):
        sibling = (lax.axis_index("x"), lax.axis_index("y"), 1 - lax.axis_index("c"))
        return [pltpu.make_async_remote_copy(src_ref=r_refs[t], dst_ref=got_refs[t], send_sem=send_sems.at[t],
                                             recv_sem=recv_sems.at[t], device_id=sibling, device_id_type=MESH)
                for t in range(len(rs))]

    return _simple_comm(rs, [_sds(r.shape, r.dtype) for r in rs], len(rs), copies)


SUM_STEPS = 4


def _pair_sums(name, gs, gots, c_arr):
    n = len(gs)
    dims = [(g.shape[2] // SUM_STEPS, g.shape[3]) for g in gs]

    def body(c_ref, *refs):
        for t in range(n):
            s = refs[t][...] + refs[n + t][...]
            refs[2 * n + t][...] = s
            refs[3 * n + t][...] = s.astype(BF16)

    def part(tr, pc):
        return pl.BlockSpec((None, tr, pc), lambda k, i, c_ref: (k, i, 0))

    def kept(tr, pc):
        return pl.BlockSpec((None, None, tr, pc), lambda k, i, c_ref: (k, c_ref[0], i, 0))

    outs = pl.pallas_call(
        body, name=name,
        out_shape=[_sds((N_CHIPS,) + g.shape[2:]) for g in gs] + [_sds((N_CHIPS,) + g.shape[2:], BF16) for g in gs],
        grid_spec=pltpu.PrefetchScalarGridSpec(
            num_scalar_prefetch=1, grid=(N_CHIPS, SUM_STEPS),
            in_specs=[kept(*d) for d in dims] + [part(*d) for d in dims],
            out_specs=[part(*d) for d in dims] * 2),
        compiler_params=_params(("parallel", "parallel")),
    )(c_arr, *gs, *gots)
    return list(outs[:n]), list(outs[n:])


def _chips_sums(name, h32s, gots, chip_arr):
    n = len(h32s)
    dims = [(h.shape[1] // SUM_STEPS, h.shape[2]) for h in h32s]

    def body(chip_ref, *refs):
        for t in range(n):
            got_ref = refs[n + t]
            refs[2 * n + t][...] = refs[t][...] + got_ref[0] + got_ref[1] + got_ref[2]

    return pl.pallas_call(
        body, name=name, out_shape=[_sds(h.shape[1:]) for h in h32s],
        grid_spec=pltpu.PrefetchScalarGridSpec(
            num_scalar_prefetch=1, grid=(SUM_STEPS,),
            in_specs=[pl.BlockSpec((None, tr, pc), lambda i, chip_ref: (chip_ref[0], i, 0)) for tr, pc in dims]
            + [pl.BlockSpec((3, tr, pc), lambda i, chip_ref: (0, i, 0)) for tr, pc in dims],
            out_specs=[pl.BlockSpec((tr, pc), lambda i, chip_ref: (i, 0)) for tr, pc in dims]),
        compiler_params=_params(("parallel",)),
    )(chip_arr, *h32s, *gots)


class _Reduce:
    def __init__(self, tag, dws, c, chip):
        gp = _grad_pieces(dws)
        self.tag, self.c, self.names, self.pieces = tag, c, list(gp), list(gp.values())
        self.c_arr, self.chip_arr = jnp.reshape(c, (1,)).astype(jnp.int32), jnp.reshape(chip, (1,)).astype(jnp.int32)
        self.shards = None

    def after_swap(self, gots):
        self.s32, self.s16 = _pair_sums(f"rs_pair_sum_{self.tag}", self.pieces, gots, self.c_arr)

    def after_scatter(self, parts):
        mine = _chips_sums(f"rs_chips_sum_{self.tag}", self.s32, parts, self.chip_arr)
        theirs = _comm_call(f"rs_share_{self.tag}", _share_comm(mine))
        self.shards = {}
        for n, a, b in zip(self.names, mine, theirs):
            both = jnp.where(self.c == 0, jnp.concatenate([a, b], axis=0), jnp.concatenate([b, a], axis=0))
            self.shards[n] = both[:, :SHARD_COLS[n]]

    def run_alone(self):
        self.after_swap(_comm_call(f"rs_pair_{self.tag}", _swap_comm(self.pieces)))
        self.after_scatter(_comm_call(f"rs_chips_{self.tag}", _scatter_comm(self.s16)))


class _LayerWeights:
    def __init__(self, shards, c, dev):
        self.shards, self.c, self.dev = shards, c, dev

    def comm(self, l, names=W_NAMES):
        return _all_gather_comm([_my_piece(n, self.shards[l][n], self.c) for n in names])

    def take(self, l, gathered, names=W_NAMES):
        pieces = {}
        for n, g in zip(names, gathered):
            mine = _my_piece(n, self.shards[l][n], self.c)
            pieces[n] = lax.dynamic_update_index_in_dim(g, mine, self.dev, 0).reshape((N_CHIPS, 2) + mine.shape)
        return _weights_from_pieces(pieces)

    def first(self):
        now, later = ("w_in", "w_uq", "w_ukv"), ("w_ff1", "w_ff2", "w_o")
        weights = self.take(0, _comm_call("ag_l0", self.comm(0, now)), now)
        comms = [self.comm(0, (n,)) for n in later]

        def take_rest(results):
            rest = {}
            for n, gathered in zip(later, results):
                rest.update(self.take(0, gathered, (n,)))
            return rest

        return weights, (comms, take_rest)


def _all_reduce_small(vec):
    rows, lanes = vec.shape

    def gather_body(x_ref, out_ref, send_sems, recv_sems):
        x, y, c = lax.axis_index("x"), lax.axis_index("y"), lax.axis_index("c")
        me, sibling = (x, y, c), (x, y, 1 - c)
        chips = [(1 - x, y), (x, 1 - y), (1 - x, 1 - y)]

        def slot(px, py, pc):
            return out_ref.at[4 * px + 2 * py + pc]

        def copy(k, blk, to, src=None):
            return pltpu.make_async_remote_copy(
                src_ref=slot(*blk) if src is None else src, dst_ref=slot(*blk),
                send_sem=send_sems.at[k], recv_sem=recv_sems.at[k], device_id=to, device_id_type=MESH)

        out_ref[4 * x + 2 * y + c] = x_ref[...]
        first = [copy(0, me, sibling, src=x_ref)]
        first += [copy(1 + j, me, (*chip, c), src=x_ref) for j, chip in enumerate(chips)]
        for cp in first:
            cp.start()
        passed = [copy(4 + j, (*chip, c), sibling) for j, chip in enumerate(chips)]
        for j, chip in enumerate(chips):
            copy(1 + j, (*chip, c), me).wait_recv()
            passed[j].start()
        copy(0, sibling, me).wait_recv()
        for j, chip in enumerate(chips):
            copy(4 + j, (*chip, 1 - c), me).wait_recv()
        for cp in first + passed:
            cp.wait_send()

    vmem = pl.BlockSpec(memory_space=pltpu.VMEM)
    allv = pl.pallas_call(
        gather_body, name="small_all_gather", out_shape=_sds((N_DEV, rows, lanes)),
        in_specs=[vmem], out_specs=vmem,
        scratch_shapes=[pltpu.SemaphoreType.DMA((7,)), pltpu.SemaphoreType.DMA((7,))],
    )(vec)

    def sum_body(a_ref, o_ref):
        acc = a_ref[0]
        for d in range(1, N_DEV):
            acc = acc + a_ref[d]
        o_ref[...] = acc

    return pl.pallas_call(sum_body, name="small_sum", out_shape=_sds((rows, lanes)), in_specs=[vmem], out_specs=vmem)(allv)


def _adamw(name, w, g, m, v, ts=512):
    def fn(w_v, g_v, m_v, v_v):
        m_n = ADAM_B1 * m_v + (1.0 - ADAM_B1) * g_v
        v_n = ADAM_B2 * v_v + (1.0 - ADAM_B2) * (g_v * g_v)
        m_hat = m_n / (1.0 - ADAM_B1 ** ADAM_STEP)
        v_hat = v_n / (1.0 - ADAM_B2 ** ADAM_STEP)
        delta = -ADAM_LR * (m_hat / (jnp.sqrt(v_hat) + ADAM_EPS) + ADAM_WD * w_v)
        return delta, m_n, v_n

    return _rowwise(name, fn, [w, g, m, v], [_sds(w.shape)] * 3, ts=ts)


def _pack_small(per_layer):
    flat = jnp.concatenate([per_layer[l][n].reshape(-1) for l in range(DEPTH) for n in SMALL_NAMES])
    return jnp.pad(flat, (0, SMALL_ROWS * LANES - flat.shape[0])).reshape(SMALL_ROWS, LANES)


def _unpack_small(packed):
    flat = packed.reshape(-1)
    per = sum(SMALL_SIZES)
    out = {}
    for n, size, off in zip(SMALL_NAMES, SMALL_SIZES, [sum(SMALL_SIZES[:i]) for i in range(len(SMALL_SIZES))]):
        out[n] = jnp.stack([flat[l * per + off:l * per + off + size] for l in range(DEPTH)])
    return out


def _fwd_bwd(x, target, layer_weights, smalls, on_layer_grads):
    depth = len(smalls)
    cos, sin = _rope_tables(x.shape[0])
    h_f32, h_bf = x, x.astype(BF16)
    saved = []
    w, rest = layer_weights.first()
    for l in range(depth):
        behind = layer_weights.comm(l + 1) if l + 1 < depth else None
        h_f32, h_bf, sv, brought = _layer_fwd(h_f32, h_bf, w, smalls[l], cos, sin, behind, rest)
        saved.append(sv)
        if l + 1 < depth:
            w, rest = layer_weights.take(l + 1, brought), None
    loss_part, dh = _loss_fn(h_f32, target)
    small_grads, pending = [None] * depth, None
    for l in reversed(range(depth)):
        dh, small_grads[l], pending = _layer_bwd(dh, saved[l]["w"], smalls[l], saved[l], cos, sin,
                                                 lambda dws, l=l: on_layer_grads(l, dws), pending)
    return loss_part, dh, small_grads, pending


def kernel(x, w_in, q_a_norm, kv_a_norm, w_uq, w_ukv, a_out_norm, b_out_norm, w_o, ln1_g, ln1_b, w_ff1, w_ff2, ln2_g, ln2_b, loss_target, m_w_in, m_q_a_norm, m_kv_a_norm, m_w_uq, m_w_ukv, m_a_out_norm, m_b_out_norm, m_w_o, m_ln1_g, m_ln1_b, m_w_ff1, m_w_ff2, m_ln2_g, m_ln2_b, v_w_in, v_q_a_norm, v_kv_a_norm, v_w_uq, v_w_ukv, v_a_out_norm, v_b_out_norm, v_w_o, v_ln1_g, v_ln1_b, v_w_ff1, v_w_ff2, v_ln2_g, v_ln2_b):
    c = lax.axis_index("c")
    chip = 2 * lax.axis_index("x") + lax.axis_index("y")
    big = dict(w_in=w_in, w_uq=w_uq, w_ukv=w_ukv, w_o=w_o, w_ff1=w_ff1, w_ff2=w_ff2)
    big_m = dict(w_in=m_w_in, w_uq=m_w_uq, w_ukv=m_w_ukv, w_o=m_w_o, w_ff1=m_w_ff1, w_ff2=m_w_ff2)
    big_v = dict(w_in=v_w_in, w_uq=v_w_uq, w_ukv=v_w_ukv, w_o=v_w_o, w_ff1=v_w_ff1, w_ff2=v_w_ff2)
    small = dict(q_a_norm=q_a_norm, kv_a_norm=kv_a_norm, a_out_norm=a_out_norm, b_out_norm=b_out_norm,
                 ln1_g=ln1_g, ln1_b=ln1_b, ln2_g=ln2_g, ln2_b=ln2_b)
    small_m = dict(q_a_norm=m_q_a_norm, kv_a_norm=m_kv_a_norm, a_out_norm=m_a_out_norm, b_out_norm=m_b_out_norm,
                   ln1_g=m_ln1_g, ln1_b=m_ln1_b, ln2_g=m_ln2_g, ln2_b=m_ln2_b)
    small_v = dict(q_a_norm=v_q_a_norm, kv_a_norm=v_kv_a_norm, a_out_norm=v_a_out_norm, b_out_norm=v_b_out_norm,
                   ln1_g=v_ln1_g, ln1_b=v_ln1_b, ln2_g=v_ln2_g, ln2_b=v_ln2_b)

    layer_weights = _LayerWeights([{n: big[n][l] for n in W_NAMES} for l in range(DEPTH)], c, 2 * chip + c)
    smalls = [{n: small[n][l][None, :] for n in SMALL_NAMES} for l in range(DEPTH)]
    reductions = [[] for _ in range(DEPTH)]

    def reduce_layer(l, dws):
        reductions[l].append(_Reduce(f"l{l}_{len(reductions[l])}", dws, c, chip))
        return reductions[l][-1]

    loss_part, grad_x, small_grads, last = _fwd_bwd(x[0], loss_target[0], layer_weights, smalls, reduce_layer)
    last.run_alone()
    shard_grads = [{n: g for r in rs for n, g in r.shards.items()} for rs in reductions]
    loss = lax.psum(loss_part[0, 0], ("x", "y", "c"))
    grad_x = grad_x[None]

    g_small_packed = _all_reduce_small(_pack_small(small_grads))
    pack_in = lambda d: _pack_small([{n: d[n][l] for n in SMALL_NAMES} for l in range(DEPTH)])
    sd, sm_, sv_ = _adamw("adamw_small", pack_in(small), g_small_packed, pack_in(small_m), pack_in(small_v), ts=SMALL_ROWS)
    g_small, d_small, m_small, v_small = (_unpack_small(t) for t in (g_small_packed, sd, sm_, sv_))

    g_big, d_big, m_big, v_big = {}, {}, {}, {}
    for n in W_NAMES:
        g = jnp.stack([shard_grads[l][n] for l in range(DEPTH)])
        shape = g.shape
        flat = lambda t: t.reshape(shape[0] * shape[1], shape[2])
        d_, m_, v_ = _adamw(f"adamw_{n}", flat(big[n]), flat(g), flat(big_m[n]), flat(big_v[n]))
        g_big[n], d_big[n], m_big[n], v_big[n] = g, d_.reshape(shape), m_.reshape(shape), v_.reshape(shape)

    order = ("w_in", "q_a_norm", "kv_a_norm", "w_uq", "w_ukv", "a_out_norm", "b_out_norm", "w_o", "ln1_g", "ln1_b",
             "w_ff1", "w_ff2", "ln2_g", "ln2_b")
    pick = lambda bigd, smalld: [bigd[n] if n in bigd else smalld[n] for n in order]
    return (loss, grad_x, *pick(g_big, g_small), *pick(d_big, d_small), *pick(m_big, m_small), *pick(v_big, v_small))
```

```python
import math

import jax
import jax.numpy as jnp
from jax import lax
from jax.experimental import pallas as pl
from jax.experimental.pallas import tpu as pltpu

F32, BF16 = jnp.float32, jnp.bfloat16

D_MODEL = 1024
DEPTH = 4
A_HEAD_DIM = 64
A_HEADS = 8
A_WIDTH = 512
PATTERNS = ((128, 1), (512, 4), (2048, 16))
SPAN = 128
MLA_HEADS = 4
QK_NOPE = 128
QK_ROPE = 64
V_DIM = 128
Q_LORA = 256
KV_LORA = 128
MLA_WIDTH = 512
QK_PAD = 256
IN_COLS = 1984
IN_PAD = 2048
UQ_PAD = 1024
D_FF = 4096
ROPE_THETA = 10000.0
ALPHA = (2.0 * DEPTH) ** 0.25
LN_EPS = 1e-5
RMS_EPS = 1e-6
MLA_SCALE = (QK_NOPE + QK_ROPE) ** -0.5
A_SCALE = A_HEAD_DIM ** -0.5

ADAM_LR, ADAM_B1, ADAM_B2, ADAM_EPS, ADAM_WD, ADAM_STEP = 0.001, 0.9, 0.999, 1e-08, 0.01, 10

VMEM_LIMIT_BYTES = 56 * 1024 * 1024
NEG = -1e30
N_CHIPS, N_DEV = 4, 8
LANES = 128

SMALL_NAMES = ("q_a_norm", "kv_a_norm", "a_out_norm", "b_out_norm", "ln1_g", "ln1_b", "ln2_g", "ln2_b")
SMALL_SIZES = (256, 128, 512, 512, 1024, 1024, 1024, 1024)
SMALL_ROWS = 176


def _params(sem):
    return pltpu.CompilerParams(dimension_semantics=sem, vmem_limit_bytes=VMEM_LIMIT_BYTES)


def _interleave(parts, scratch):
    d, (n, width) = len(parts), parts[0].shape
    for r in range(d):
        for s in range(width // LANES):
            scratch.at[s][pl.ds(r, n, stride=d), :] = parts[r][:, s * LANES:(s + 1) * LANES]
    return jnp.concatenate([scratch[s] for s in range(width // LANES)], axis=1)


def _deinterleave(x, d, scratch):
    n, width = x.shape[0] // d, x.shape[1]
    for s in range(width // LANES):
        scratch[s] = x[:, s * LANES:(s + 1) * LANES]
    return [jnp.concatenate([scratch.at[s][pl.ds(r, n, stride=d), :] for s in range(width // LANES)], axis=1) for r in range(d)]


def _rowwise(name, fn, ins, outs, consts=(), reds=(), ts=512, in_dils=None, out_dils=None):
    in_dils = list(in_dils or [1] * len(ins))
    out_dils = list(out_dils or [1] * len(outs))
    n_rows = ins[0].shape[-2] * in_dils[0]
    ts = min(ts, n_rows)
    assert n_rows % ts == 0
    n_in, n_c, n_o, n_r = len(ins), len(consts), len(outs), len(reds)
    viewed = [(a.shape, d) for a, d in zip(list(ins) + list(outs), in_dils + out_dils) if d > 1]

    def tile_spec(shape, d=1):
        if len(shape) == 2:
            return pl.BlockSpec((ts // d, shape[1]), lambda i: (i, 0))
        return pl.BlockSpec((shape[0], ts, shape[2]), lambda i: (0, i, 0))

    def whole_spec(shape):
        return pl.BlockSpec(shape, lambda i: (0,) * len(shape))

    def body(*refs):
        in_refs, c_refs = refs[:n_in], refs[n_in:n_in + n_c]
        o_refs = refs[n_in + n_c:n_in + n_c + n_o]
        r_refs = refs[n_in + n_c + n_o:n_in + n_c + n_o + n_r]
        scratches = list(refs[n_in + n_c + n_o + n_r:])
        vals = []
        for r, d in zip(in_refs, in_dils):
            if d == 1:
                vals.append(r[...])
            else:
                width = r.shape[1] // d
                vals.append(_interleave([r[:, k * width:(k + 1) * width] for k in range(d)], scratches.pop(0)))
        res = fn(*vals, *[r[...] for r in c_refs])
        res = tuple(res) if isinstance(res, (tuple, list)) else (res,)
        for r, v, d in zip(o_refs, res[:n_o], out_dils):
            if len(r.shape) == 3:
                for g in range(r.shape[0]):
                    r[g] = v[g].astype(r.dtype)
            elif d == 1:
                r[...] = v.astype(r.dtype)
            else:
                width = r.shape[1] // d
                for k, part in enumerate(_deinterleave(v, d, scratches.pop(0))):
                    r[:, k * width:(k + 1) * width] = part.astype(r.dtype)
        if n_r:
            i = pl.program_id(0)

            @pl.when(i == 0)
            def _():
                for r, v in zip(r_refs, res[n_o:]):
                    r[...] = v

            @pl.when(i > 0)
            def _():
                for r, v in zip(r_refs, res[n_o:]):
                    r[...] += v

    out = pl.pallas_call(
        body, name=name, grid=(n_rows // ts,),
        in_specs=[tile_spec(a.shape, d) for a, d in zip(ins, in_dils)] + [whole_spec(c.shape) for c in consts],
        out_specs=[tile_spec(o.shape, d) for o, d in zip(outs, out_dils)] + [whole_spec(r.shape) for r in reds],
        out_shape=list(outs) + list(reds),
        scratch_shapes=[pltpu.VMEM((shape[1] // d // LANES, ts, LANES), F32) for shape, d in viewed],
        compiler_params=_params(("arbitrary",)),
    )(*ins, *consts)
    return out


def _sds(shape, dtype=F32):
    return jax.ShapeDtypeStruct(tuple(shape), dtype)


def _hosted(comm, grid):
    if comm is None:
        return [], [], [], [], (lambda *a: None), (lambda *a: None)

    def edge(which, at, ins, outs, sems):
        ids = [pl.program_id(d) for d in range(len(grid))]
        hit = ids[0] == at[0]
        for i, v in zip(ids[1:], at[1:]):
            hit = hit & (i == v)

        @pl.when(hit)
        def _():
            which(ins, outs, *sems)

    begin = lambda ins, outs, sems: edge(comm.start, [0] * len(grid), ins, outs, sems)
    end = lambda ins, outs, sems: edge(comm.finish, [g - 1 for g in grid], ins, outs, sems)
    return comm.ins, [ANY] * len(comm.ins), comm.out_shapes, [ANY] * len(comm.out_shapes), begin, end


def _mm(name, a, b, out_dtypes, *, ta=False, tb=False, tm=512, tn=512, tk=512, epilogue=None, extras=(), vecs=(),
        comm=None, b_pieces=False, out_pieces=False):
    (k_dim, m_dim) = a.shape if ta else a.shape[::-1]
    if b_pieces:
        b_rows, b_cols = 2 * b.shape[2], N_CHIPS * b.shape[3]
        (n_dim, k2) = (b_rows, b_cols) if tb else (b_cols, b_rows)
    else:
        (n_dim, k2) = b.shape if tb else b.shape[::-1]
    assert k_dim == k2
    tm, tn, tk = min(tm, m_dim), min(tn, n_dim), min(tk, k_dim)
    assert m_dim % tm == 0 and n_dim % tn == 0 and k_dim % tk == 0
    nk = k_dim // tk
    grid = (m_dim // tm, n_dim // tn, nk)
    a_spec = pl.BlockSpec((tk, tm), lambda i, j, k: (k, i)) if ta else pl.BlockSpec((tm, tk), lambda i, j, k: (i, k))
    if b_pieces:
        assert tn == b_rows and tk % b.shape[3] == 0 if tb else (tk, tn) == (b_rows, b.shape[3])
        n_bp = tk // b.shape[3] if tb else 1
        b_spec = pl.BlockSpec((n_bp,) + b.shape[1:], (lambda i, j, k: (k, 0, 0, 0)) if tb else (lambda i, j, k: (j, 0, 0, 0)))
    else:
        b_spec = pl.BlockSpec((tn, tk), lambda i, j, k: (j, k)) if tb else pl.BlockSpec((tk, tn), lambda i, j, k: (k, j))
    assert not out_pieces or (tm == m_dim and len(out_dtypes) == 1)
    dims = (((0 if ta else 1,), (1 if tb else 0,)), ((), ()))
    n_e, n_v, n_o = len(extras), len(vecs), len(out_dtypes)
    c_ins, c_in_specs, c_outs, c_out_specs, begin, end = _hosted(comm, grid)
    n_ci, n_co = len(c_ins), len(c_outs)

    def body(a_ref, b_ref, *refs):
        e_refs, v_refs, ci_refs = refs[:n_e], refs[n_e:n_e + n_v], refs[n_e + n_v:n_e + n_v + n_ci]
        o_refs = refs[n_e + n_v + n_ci:n_e + n_v + n_ci + n_o]
        co_refs = refs[n_e + n_v + n_ci + n_o:n_e + n_v + n_ci + n_o + n_co]
        scratch = refs[n_e + n_v + n_ci + n_o + n_co:]
        sems = scratch[-2:]
        begin(ci_refs, co_refs, sems)
        if b_pieces:
            pc = b.shape[3]
            part = None
            for cc in range(n_bp):
                a_val = a_ref[:, cc * pc:(cc + 1) * pc] if tb else a_ref[...]
                term = lax.dot_general(a_val.astype(BF16), b_ref[cc].reshape(b_rows, pc).astype(BF16), dims,
                                       preferred_element_type=F32)
                part = term if part is None else part + term
        else:
            part = lax.dot_general(a_ref[...].astype(BF16), b_ref[...].astype(BF16), dims, preferred_element_type=F32)

        def finish(acc):
            outs = epilogue(acc, *[r[...] for r in e_refs], *[r[...] for r in v_refs]) if epilogue else (acc,)
            for r, v in zip(o_refs, outs):
                r[...] = (v.reshape(2, tm // 2, tn) if out_pieces else v).astype(r.dtype)

        if nk == 1:
            finish(part)
        else:
            acc_ref = scratch[0]
            k = pl.program_id(2)

            @pl.when(k == 0)
            def _():
                acc_ref[...] = part

            @pl.when(k > 0)
            def _():
                acc_ref[...] += part

            @pl.when(k == nk - 1)
            def _():
                finish(acc_ref[...])
        end(ci_refs, co_refs, sems)

    tile = pl.BlockSpec((tm, tn), lambda i, j, k: (i, j))
    if out_pieces:
        out_tile, out_shape = pl.BlockSpec((None, 2, tm // 2, tn), lambda i, j, k: (j, 0, 0, 0)), (n_dim // tn, 2, tm // 2, tn)
    else:
        out_tile, out_shape = tile, (m_dim, n_dim)
    out = pl.pallas_call(
        body, name=name, grid=grid,
        in_specs=[a_spec, b_spec] + [tile] * n_e + [pl.BlockSpec((1, tn), lambda i, j, k: (0, j))] * n_v + c_in_specs,
        out_specs=[out_tile] * n_o + c_out_specs,
        out_shape=[_sds(out_shape, d) for d in out_dtypes] + c_outs,
        scratch_shapes=([pltpu.VMEM((tm, tn), F32)] if nk > 1 else []) + (comm.sem_scratch() if comm else []),
        compiler_params=_params(("arbitrary",) * 3 if comm else ("parallel", "parallel", "arbitrary")),
    )(a, b, *extras, *vecs, *c_ins)
    return out


def _swap_halves(x):
    width = x.shape[1]
    lane = lax.broadcasted_iota(jnp.int32, x.shape, 1)
    return jnp.where((lane % 64) < 32, pltpu.roll(x, width - 32, 1), pltpu.roll(x, 32, 1))


def _rope(x, cos, sin_signed):
    return x * cos + _swap_halves(x) * sin_signed


def _rope_t(d, cos, sin_signed):
    return d * cos - _swap_halves(d) * sin_signed


def _rms(x, g):
    rstd = lax.rsqrt(jnp.mean(x * x, axis=-1, keepdims=True) + RMS_EPS)
    return x * rstd * g


def _rms_bwd(x, g, dy):
    rstd = lax.rsqrt(jnp.mean(x * x, axis=-1, keepdims=True) + RMS_EPS)
    xh = x * rstd
    dyg = dy * g
    dx = rstd * (dyg - xh * jnp.mean(dyg * xh, axis=-1, keepdims=True))
    return dx, jnp.sum(dy * xh, axis=0, keepdims=True)


def _layer_norm(r, g, b):
    mu = jnp.mean(r, axis=-1, keepdims=True)
    xc = r - mu
    var = jnp.mean(xc * xc, axis=-1, keepdims=True)
    return xc * lax.rsqrt(var + LN_EPS) * g + b


def _dot(a, b, dims):
    return lax.dot_general(a, b, (dims, ((), ())), preferred_element_type=F32)


_NN, _NT, _TN = ((1,), (0,)), ((1,), (1,)), ((0,), (0,))


def _in_proj(x_bf, w_in_p, cos, sin):
    n_rows = x_bf.shape[0]
    ts = 512
    dils = [dil for _, dil in PATTERNS]
    n_p = len(dils)

    def body(x_ref, w_ref, cos_ref, sin_ref, *refs):
        view_refs, (cq_ref, kpe_ref, ckv_ref, scratch) = refs[:n_p], refs[n_p:]
        xb = x_ref[...]
        cos_v, sin_v = cos_ref[...], sin_ref[...]

        def seg(lo, hi):
            return _dot(xb, w_ref[:, lo:hi], _NN)

        def put(part, val):
            for ref, dil in zip(view_refs, dils):
                pieces = [val] if dil == 1 else _deinterleave(val, dil, scratch)
                for r, piece in enumerate(pieces):
                    lo = (3 * r + part) * A_WIDTH
                    ref[:, lo:lo + A_WIDTH] = piece.astype(BF16)

        put(0, _rope(seg(0, 512), cos_v, sin_v) * A_SCALE)
        put(1, _rope(seg(512, 1024), cos_v, sin_v))
        put(2, seg(1024, 1536))
        cq_ref[...] = seg(1536, 1792)
        kpe_ref[...] = _rope(seg(1792, 1920), cos_v[:, :LANES], sin_v[:, :LANES]).astype(BF16)
        ckv_ref[...] = seg(1920, 2048)

    row = lambda c, d=1: pl.BlockSpec((ts // d, c * d), lambda i: (i, 0))
    return pl.pallas_call(
        body, name="in_proj", grid=(n_rows // ts,),
        in_specs=[row(D_MODEL), pl.BlockSpec((D_MODEL, IN_PAD), lambda i: (0, 0)), row(A_WIDTH), row(A_WIDTH)],
        out_specs=[row(3 * A_WIDTH, d) for d in dils] + [row(Q_LORA), row(LANES), row(KV_LORA)],
        out_shape=[_sds((n_rows // d, d * 3 * A_WIDTH), BF16) for d in dils]
        + [_sds((n_rows, Q_LORA)), _sds((n_rows, LANES), BF16), _sds((n_rows, KV_LORA))],
        scratch_shapes=[pltpu.VMEM((A_WIDTH // LANES, ts, LANES), F32)],
        compiler_params=_params(("parallel",)),
    )(x_bf, w_in_p, cos, sin)


def _band_mask(m, heads):
    qi = lax.broadcasted_iota(jnp.int32, (heads * SPAN, 2 * SPAN), 0) % SPAN
    kj = lax.broadcasted_iota(jnp.int32, (heads * SPAN, 2 * SPAN), 1)
    return ((kj < SPAN) & (kj >= qi) & (m > 0)) | ((kj >= SPAN) & ((kj - SPAN) <= qi))


def _even_lanes(rows):
    return lax.broadcasted_iota(jnp.int32, (rows, LANES), 1) < A_HEAD_DIM


DIL_GROUP = 4


def _dil_fwd(qkv_view, dil, comm=None):
    nb = qkv_view.shape[0] // SPAN
    group = min(dil, DIL_GROUP)
    grid = (dil // group, nb)
    c_ins, c_in_specs, c_outs, c_out_specs, begin, end = _hosted(comm, grid)
    n_ci, n_co = len(c_ins), len(c_outs)

    def body(cur_ref, prev_ref, *refs):
        ci_refs, ol_ref, co_refs, sems = refs[:n_ci], refs[n_ci], refs[n_ci + 1:n_ci + 1 + n_co], refs[n_ci + 1 + n_co:]
        begin(ci_refs, co_refs, sems)
        mask = _band_mask(pl.program_id(1), 1)
        even, even2 = _even_lanes(SPAN), _even_lanes(2 * SPAN)
        for g in range(group):
            for p in range(A_HEADS // 2):
                q_sl, k_sl, v_sl = (slice((3 * g + t) * A_WIDTH + p * LANES, (3 * g + t) * A_WIDTH + (p + 1) * LANES)
                                    for t in range(3))
                q2 = cur_ref[:, q_sl]
                kcat = jnp.concatenate([prev_ref[:, k_sl], cur_ref[:, k_sl]], axis=0)
                vcat = jnp.concatenate([prev_ref[:, v_sl], cur_ref[:, v_sl]], axis=0)
                zero, one = jnp.zeros_like(q2), jnp.ones_like(vcat)
                res, lses = [], []
                for first in (True, False):
                    qh = jnp.where(even, q2, zero) if first else jnp.where(even, zero, q2)
                    vh = jnp.where(even2, vcat, one) if first else jnp.where(even2, one, vcat)
                    s = jnp.where(mask, _dot(qh, kcat, _NT), NEG)
                    mx = jnp.max(s, axis=-1, keepdims=True)
                    r = _dot(jnp.exp(s - mx).astype(BF16), vh, _NN)
                    den = pltpu.roll(r, A_HEAD_DIM, 1)
                    res.append(r / den)
                    lses.append(mx + jnp.log(den))
                o_lo = 2 * g * A_WIDTH + p * LANES
                ol_ref[:, o_lo:o_lo + LANES] = jnp.where(even, res[0], res[1])
                ol_ref[:, o_lo + A_WIDTH:o_lo + A_WIDTH + LANES] = jnp.where(even, lses[0], lses[1])
        end(ci_refs, co_refs, sems)

    return pl.pallas_call(
        body, name=f"dil_fwd_d{dil}", grid=grid,
        in_specs=[pl.BlockSpec((SPAN, group * 3 * A_WIDTH), lambda r, m: (m, r)),
                  pl.BlockSpec((SPAN, group * 3 * A_WIDTH), lambda r, m: (jnp.maximum(m - 1, 0), r))] + c_in_specs,
        out_specs=[pl.BlockSpec((SPAN, group * 2 * A_WIDTH), lambda r, m: (m, r))] + c_out_specs,
        out_shape=[_sds((qkv_view.shape[0], dil * 2 * A_WIDTH))] + c_outs,
        scratch_shapes=comm.sem_scratch() if comm else [],
        compiler_params=_params(("arbitrary", "arbitrary") if comm else ("parallel", "arbitrary")),
    )(qkv_view, qkv_view, *c_ins)


def _mix_fwd(ols, b_out, a_g, b_g):
    def fn(ol1, ol2, ol3, b, ag, bg):
        (o1, l1), (o2, l2), (o3, l3) = [(t[:, :A_WIDTH], t[:, A_WIDTH:]) for t in (ol1, ol2, ol3)]
        mx = jnp.maximum(jnp.maximum(l1, l2), l3)
        e1, e2, e3 = jnp.exp(l1 - mx), jnp.exp(l2 - mx), jnp.exp(l3 - mx)
        a = (e1 * o1 + e2 * o2 + e3 * o3) / (e1 + e2 + e3)
        return a, jnp.concatenate([_rms(a, ag), _rms(b, bg)], axis=1)

    n_rows = b_out.shape[0]
    return _rowwise("mix_fwd", fn, [*ols, b_out], [_sds((n_rows, A_WIDTH)), _sds((n_rows, 2 * A_WIDTH), BF16)],
                    consts=[a_g, b_g], in_dils=[dil for _, dil in PATTERNS] + [1])


def _mla_q_prep(cq, g, w_uq_p, cos, sin):
    def fn(cq_v, cos_v, sin_v, g_v, w_v):
        cqn = _rms(cq_v, g_v).astype(BF16)
        q = _dot(cqn, w_v, _NN)
        qf = [jnp.concatenate([q[:, h * QK_NOPE:(h + 1) * QK_NOPE],
                               _rope(q[:, 512 + h * LANES:512 + (h + 1) * LANES], cos_v[:, :LANES], sin_v[:, :LANES])], axis=1)
              for h in range(MLA_HEADS)]
        return cqn, qf

    n_rows = cq.shape[0]
    return _rowwise("mla_q_prep", fn, [cq, cos, sin], [_sds((n_rows, Q_LORA), BF16), _sds((MLA_HEADS, n_rows, QK_PAD), BF16)],
                    consts=[g, w_uq_p])


def _mla_kv_prep(ckv, kpe, g, w_ukv_p):
    def fn(ckv_v, kpe_v, g_v, w_v):
        ckvn = _rms(ckv_v, g_v).astype(BF16)
        kvv = _dot(ckvn, w_v, _NN)
        kf = [jnp.concatenate([kvv[:, h * QK_NOPE:(h + 1) * QK_NOPE], kpe_v.astype(F32)], axis=1) for h in range(MLA_HEADS)]
        return ckvn, kf, kvv[:, 512:]

    n_rows = ckv.shape[0]
    return _rowwise("mla_kv_prep", fn, [ckv, kpe],
                    [_sds((n_rows, KV_LORA), BF16), _sds((MLA_HEADS, n_rows, QK_PAD), BF16), _sds((n_rows, MLA_WIDTH), BF16)],
                    consts=[g, w_ukv_p])


MLA_FWD_TILES = (1024, 2048)
MLA_BWD_TILES = (1024, 1024)


def _mla_fwd(qf, kf, v, comm=None):
    n_rows = v.shape[0]
    tq, tk = min(MLA_FWD_TILES[0], n_rows), min(MLA_FWD_TILES[1], n_rows)
    nq, nk = n_rows // tq, n_rows // tk
    grid = (MLA_HEADS, nq, nk)
    exp2_scale = MLA_SCALE * math.log2(math.e)
    c_ins, c_in_specs, c_outs, c_out_specs, begin, end = _hosted(comm, grid)
    n_ci, n_co = len(c_ins), len(c_outs)

    def last_k(i):
        return jnp.right_shift(i * tq + tq - 1, int(math.log2(tk)))

    def body(q_ref, k_ref, v_ref, *refs):
        ci_refs, (o_ref, lse_ref), co_refs = refs[:n_ci], refs[n_ci:n_ci + 2], refs[n_ci + 2:n_ci + 2 + n_co]
        m_sc, l_sc, acc_sc = refs[n_ci + 2 + n_co:n_ci + 5 + n_co]
        sems = refs[n_ci + 5 + n_co:]
        i, j = pl.program_id(1), pl.program_id(2)
        begin(ci_refs, co_refs, sems)

        @pl.when(j == 0)
        def _():
            m_sc[...] = jnp.full(m_sc.shape, NEG, F32)
            l_sc[...] = jnp.zeros(l_sc.shape, F32)
            acc_sc[...] = jnp.zeros(acc_sc.shape, F32)

        def step(masked):
            s = _dot(q_ref[...], k_ref[...], _NT)
            if masked:
                row = lax.broadcasted_iota(jnp.int32, (tq, tk), 0) + i * tq
                col = lax.broadcasted_iota(jnp.int32, (tq, tk), 1) + j * tk
                s = jnp.where(col <= row, s, NEG)
            m_prev = m_sc[...]
            m_new = jnp.maximum(m_prev, jnp.max(s, axis=-1, keepdims=True))
            alpha = jnp.exp2((m_prev - m_new) * exp2_scale)
            p = jnp.exp2((s - m_new) * exp2_scale)
            l_sc[...] = alpha * l_sc[...] + jnp.sum(p, axis=-1, keepdims=True)
            acc_sc[...] = alpha * acc_sc[...] + _dot(p.astype(BF16), v_ref[...], _NN)
            m_sc[...] = m_new

        active = j * tk <= i * tq + tq - 1
        crosses = (j + 1) * tk - 1 > i * tq

        @pl.when(active & jnp.logical_not(crosses))
        def _():
            step(False)

        @pl.when(active & crosses)
        def _():
            step(True)

        @pl.when(j == last_k(i))
        def _():
            o_ref[...] = acc_sc[...] / l_sc[...]
            lse_ref[...] = jnp.broadcast_to(m_sc[...] * MLA_SCALE + jnp.log(l_sc[...]), (tq, V_DIM))

        end(ci_refs, co_refs, sems)

    return pl.pallas_call(
        body, name="mla_fwd", grid=grid,
        in_specs=[pl.BlockSpec((None, tq, QK_PAD), lambda h, i, j: (h, i, 0)),
                  pl.BlockSpec((None, tk, QK_PAD), lambda h, i, j: (h, jnp.minimum(j, last_k(i)), 0)),
                  pl.BlockSpec((tk, V_DIM), lambda h, i, j: (jnp.minimum(j, last_k(i)), h))] + c_in_specs,
        out_specs=[pl.BlockSpec((tq, V_DIM), lambda h, i, j: (i, h)), pl.BlockSpec((tq, V_DIM), lambda h, i, j: (i, h))]
        + c_out_specs,
        out_shape=[_sds((n_rows, MLA_WIDTH)), _sds((n_rows, MLA_WIDTH))] + c_outs,
        scratch_shapes=[pltpu.VMEM((tq, 1), F32), pltpu.VMEM((tq, 1), F32), pltpu.VMEM((tq, V_DIM), F32)]
        + (comm.sem_scratch() if comm else []),
        compiler_params=_params(("arbitrary",) * 3 if comm else ("parallel", "parallel", "arbitrary")),
    )(qf, kf, v, *c_ins)


def _mm_res_ln(name, a, w, xres, g, b):
    def epi(acc, xr, g_v, b_v):
        r = ALPHA * xr + acc
        y = _layer_norm(r, g_v, b_v)
        return y, y, r

    return _mm(name, a, w, (F32, BF16, F32), tm=1024, tn=D_MODEL, tk=1024, epilogue=epi, extras=[xres], vecs=[g, b])


def _mm_relu2(x_bf, w):
    def epi(acc):
        r = jnp.maximum(acc, 0.0)
        return (r * r,)

    return _mm("ff1", x_bf, w, (BF16,), tm=1024, tn=1024, tk=D_MODEL, epilogue=epi, b_pieces=True)[0]


def _loss_fn(y, t):
    def fn(y_v, t_v):
        d = y_v - t_v
        part = jnp.sum(jnp.sum(d * d, axis=1, keepdims=True), axis=0, keepdims=True)
        return d * (1.0 / D_MODEL), part

    dy, part = _rowwise("loss", fn, [y, t], [_sds(y.shape)], reds=[_sds((1, 1))])
    return part * (0.5 / D_MODEL), dy


def _ln_bwd(name, dy, r, g):
    def fn(dy_v, r_v, g_v):
        mu = jnp.mean(r_v, axis=-1, keepdims=True)
        xc = r_v - mu
        rstd = lax.rsqrt(jnp.mean(xc * xc, axis=-1, keepdims=True) + LN_EPS)
        xh = xc * rstd
        dxh = dy_v * g_v
        dr = rstd * (dxh - jnp.mean(dxh, axis=-1, keepdims=True) - xh * jnp.mean(dxh * xh, axis=-1, keepdims=True))
        return dr, dr, jnp.sum(dy_v * xh, axis=0, keepdims=True), jnp.sum(dy_v, axis=0, keepdims=True)

    return _rowwise(name, fn, [dy, r], [_sds(dy.shape), _sds(dy.shape, BF16)], consts=[g],
                    reds=[_sds((1, D_MODEL)), _sds((1, D_MODEL))])


def _head_sum_matrix():
    i = lax.broadcasted_iota(jnp.int32, (A_WIDTH, A_WIDTH), 0) // A_HEAD_DIM
    j = lax.broadcasted_iota(jnp.int32, (A_WIDTH, A_WIDTH), 1) // A_HEAD_DIM
    return (i == j).astype(BF16)


def _mix_bwd(dmixed, a_out, b_out, ols, a_g, b_g):
    def fn(dm, a, b, ol1, ol2, ol3, ag, bg, e_mat):
        l1, l2, l3 = ol1[:, A_WIDTH:], ol2[:, A_WIDTH:], ol3[:, A_WIDTH:]
        da, dga = _rms_bwd(a, ag, dm[:, :A_WIDTH])
        db, dgb = _rms_bwd(b, bg, dm[:, A_WIDTH:])
        t = da * a
        t_hi = t.astype(BF16)
        t_lo = (t - t_hi.astype(F32)).astype(BF16)
        tsum = _dot(t_hi, e_mat, _NN) + _dot(t_lo, e_mat, _NN)
        mx = jnp.maximum(jnp.maximum(l1, l2), l3)
        e1, e2, e3 = jnp.exp(l1 - mx), jnp.exp(l2 - mx), jnp.exp(l3 - mx)
        inv = 1.0 / (e1 + e2 + e3)
        w1, w2, w3 = e1 * inv, e2 * inv, e3 * inv
        tb = db * b
        delta_b = jnp.concatenate(
            [jnp.broadcast_to(jnp.sum(tb[:, h * V_DIM:(h + 1) * V_DIM], axis=-1, keepdims=True), (tb.shape[0], V_DIM))
             for h in range(MLA_HEADS)], axis=1)
        dods = [jnp.concatenate([w * da, w * tsum], axis=1) for w in (w1, w2, w3)]
        return (*dods, db, delta_b, dga, dgb)

    n_rows = a_out.shape[0]
    wide = (n_rows, A_WIDTH)
    dils = [dil for _, dil in PATTERNS]
    return _rowwise("mix_bwd", fn, [dmixed, a_out, b_out, *ols],
                    [_sds((n_rows // d, d * 2 * A_WIDTH)) for d in dils] + [_sds(wide, BF16), _sds(wide)],
                    consts=[a_g, b_g, _head_sum_matrix()], reds=[_sds((1, A_WIDTH)), _sds((1, MLA_WIDTH))], ts=256,
                    in_dils=[1, 1, 1] + dils, out_dils=dils + [1, 1])


def _dil_bwd(qkv_view, ol_view, dod_view, dil, comm=None):
    nb = qkv_view.shape[0] // SPAN
    group = min(dil, DIL_GROUP)
    grid = (dil // group, nb + 1)
    c_ins, c_in_specs, c_outs, c_out_specs, begin, end = _hosted(comm, grid)
    n_ci, n_co = len(c_ins), len(c_outs)

    def body(cur_ref, prev_ref, ol_ref, dod_ref, *refs):
        ci_refs, out_ref, co_refs = refs[:n_ci], refs[n_ci], refs[n_ci + 1:n_ci + 1 + n_co]
        carry, sems = refs[n_ci + 1 + n_co], refs[n_ci + 2 + n_co:]
        m = pl.program_id(1)
        begin(ci_refs, co_refs, sems)

        @pl.when(m == 0)
        def _():
            carry[...] = jnp.zeros(carry.shape, F32)

        @pl.when(m < nb)
        def _():
            mask, even = _band_mask(m, 2), _even_lanes(SPAN)
            for g in range(group):
                for p in range(A_HEADS // 2):
                    q_sl, k_sl, v_sl = (slice((3 * g + t) * A_WIDTH + p * LANES, (3 * g + t) * A_WIDTH + (p + 1) * LANES)
                                        for t in range(3))
                    do_sl = slice(2 * g * A_WIDTH + p * LANES, 2 * g * A_WIDTH + (p + 1) * LANES)
                    l_sl = slice((2 * g + 1) * A_WIDTH + p * LANES, (2 * g + 1) * A_WIDTH + (p + 1) * LANES)
                    q2, do2 = cur_ref[:, q_sl], dod_ref[:, do_sl].astype(BF16)
                    zero = jnp.zeros_like(q2)
                    qcat = jnp.concatenate([jnp.where(even, q2, zero), jnp.where(even, zero, q2)], axis=0)
                    docat = jnp.concatenate([jnp.where(even, do2, zero), jnp.where(even, zero, do2)], axis=0)
                    kcat = jnp.concatenate([prev_ref[:, k_sl], cur_ref[:, k_sl]], axis=0)
                    vcat = jnp.concatenate([prev_ref[:, v_sl], cur_ref[:, v_sl]], axis=0)
                    lse2, dl2 = ol_ref[:, l_sl], dod_ref[:, l_sl]
                    lse_c = jnp.concatenate([lse2[:, :1], lse2[:, A_HEAD_DIM:A_HEAD_DIM + 1]], axis=0)
                    dl_c = jnp.concatenate([dl2[:, :1], dl2[:, A_HEAD_DIM:A_HEAD_DIM + 1]], axis=0)
                    pr = jnp.exp(jnp.where(mask, _dot(qcat, kcat, _NT), NEG) - lse_c)
                    ds = (pr * (_dot(docat, vcat, _NT) - dl_c)).astype(BF16)
                    dq = _dot(ds, kcat, _NN)
                    dk2 = _dot(ds, qcat, _TN)
                    dv2 = _dot(pr.astype(BF16), docat, _TN)
                    out_ref[:, q_sl] = carry[:, q_sl]
                    out_ref[:, k_sl] = carry[:, k_sl] + dk2[:SPAN]
                    out_ref[:, v_sl] = carry[:, v_sl] + dv2[:SPAN]
                    carry[:, q_sl] = jnp.where(even, dq[:SPAN], dq[SPAN:])
                    carry[:, k_sl] = dk2[SPAN:]
                    carry[:, v_sl] = dv2[SPAN:]

        @pl.when(m == nb)
        def _():
            out_ref[...] = carry[...]

        end(ci_refs, co_refs, sems)

    last = nb - 1

    def cur(width):
        return pl.BlockSpec((SPAN, group * width), lambda r, m: (jnp.minimum(m, last), r))

    def prev(width):
        return pl.BlockSpec((SPAN, group * width), lambda r, m: (jnp.clip(m - 1, 0, last), r))

    return pl.pallas_call(
        body, name=f"dil_bwd_d{dil}", grid=grid,
        in_specs=[cur(3 * A_WIDTH), prev(3 * A_WIDTH), cur(2 * A_WIDTH), cur(2 * A_WIDTH)] + c_in_specs,
        out_specs=[prev(3 * A_WIDTH)] + c_out_specs,
        out_shape=[_sds(qkv_view.shape)] + c_outs,
        scratch_shapes=[pltpu.VMEM((SPAN, group * 3 * A_WIDTH), F32)] + (comm.sem_scratch() if comm else []),
        compiler_params=_params(("arbitrary", "arbitrary") if comm else ("parallel", "arbitrary")),
    )(qkv_view, qkv_view, ol_view, dod_view, *c_ins)


def _mla_bwd(qf, kf, v, lse, do, delta, comm=None):
    n_rows = v.shape[0]
    tq, tk = min(MLA_BWD_TILES[0], n_rows), min(MLA_BWD_TILES[1], n_rows)
    nq, nk = n_rows // tq, n_rows // tk

    def first_q(j):
        return jnp.right_shift(j * tk, int(math.log2(tq)))

    grid = (MLA_HEADS, nk, nq)
    c_ins, c_in_specs, c_outs, c_out_specs, begin, end = _hosted(comm, grid)
    n_ci, n_co = len(c_ins), len(c_outs)

    def body(q_ref, k_ref, v_ref, lse_ref, do_ref, dl_ref, *refs):
        ci_refs, (dq_ref, dk_ref, dv_ref), co_refs = refs[:n_ci], refs[n_ci:n_ci + 3], refs[n_ci + 3:n_ci + 3 + n_co]
        dk_sc, dv_sc = refs[n_ci + 3 + n_co:n_ci + 5 + n_co]
        sems = refs[n_ci + 5 + n_co:]
        j, i = pl.program_id(1), pl.program_id(2)
        begin(ci_refs, co_refs, sems)

        @pl.when((j == 0) & (i == 0))
        def _():
            dq_ref[...] = jnp.zeros(dq_ref.shape, F32)

        @pl.when(i == first_q(j))
        def _():
            dk_sc[...] = jnp.zeros(dk_sc.shape, F32)
            dv_sc[...] = jnp.zeros(dv_sc.shape, F32)

        def step(masked):
            q, k, dob = q_ref[...], k_ref[...], do_ref[...]
            s = _dot(q, k, _NT) * MLA_SCALE
            if masked:
                row = lax.broadcasted_iota(jnp.int32, (tq, tk), 0) + i * tq
                col = lax.broadcasted_iota(jnp.int32, (tq, tk), 1) + j * tk
                s = jnp.where(col <= row, s, NEG)
            p = jnp.exp(s - lse_ref[:, :1])
            dp = _dot(dob, v_ref[...], _NT)
            ds = (p * (dp - dl_ref[:, :1]) * MLA_SCALE).astype(BF16)
            dv_sc[...] += _dot(p.astype(BF16), dob, _TN)
            dk_sc[...] += _dot(ds, q, _TN)
            rows = pl.ds(pl.multiple_of(i * tq, tq), tq)
            dq_ref[rows, :] += _dot(ds, k, _NN)

        active = i >= first_q(j)
        crosses = (j + 1) * tk - 1 > i * tq

        @pl.when(active & jnp.logical_not(crosses))
        def _():
            step(False)

        @pl.when(active & crosses)
        def _():
            step(True)

        @pl.when(i == nq - 1)
        def _():
            dk_ref[...] = dk_sc[...]
            dv_ref[...] = dv_sc[...]

        end(ci_refs, co_refs, sems)

    qrow = lambda h, j, i: (jnp.maximum(i, first_q(j)), h)
    return pl.pallas_call(
        body, name="mla_bwd", grid=grid,
        in_specs=[pl.BlockSpec((None, tq, QK_PAD), lambda h, j, i: (h, jnp.maximum(i, first_q(j)), 0)),
                  pl.BlockSpec((None, tk, QK_PAD), lambda h, j, i: (h, j, 0)),
                  pl.BlockSpec((tk, V_DIM), lambda h, j, i: (j, h)),
                  pl.BlockSpec((tq, V_DIM), qrow), pl.BlockSpec((tq, V_DIM), qrow), pl.BlockSpec((tq, V_DIM), qrow)]
        + c_in_specs,
        out_specs=[pl.BlockSpec((None, n_rows, QK_PAD), lambda h, j, i: (h, 0, 0)),
                   pl.BlockSpec((None, tk, QK_PAD), lambda h, j, i: (h, j, 0)),
                   pl.BlockSpec((tk, V_DIM), lambda h, j, i: (j, h))] + c_out_specs,
        out_shape=[_sds((MLA_HEADS, n_rows, QK_PAD)), _sds((MLA_HEADS, n_rows, QK_PAD)), _sds((n_rows, MLA_WIDTH))] + c_outs,
        scratch_shapes=[pltpu.VMEM((tk, QK_PAD), F32), pltpu.VMEM((tk, V_DIM), F32)] + (comm.sem_scratch() if comm else []),
        compiler_params=_params(("arbitrary",) * 3 if comm else ("parallel", "arbitrary", "arbitrary")),
    )(qf, kf, v, lse, do, delta, *c_ins)


def _mla_bwd_prep(dqf, dkf, dv, cos, sin):
    def fn(dqf_v, dkf_v, dv_v, cos_v, sin_v):
        dq = jnp.concatenate([dqf_v[h][:, :QK_NOPE] for h in range(MLA_HEADS)]
                             + [_rope_t(dqf_v[h][:, QK_NOPE:], cos_v[:, :LANES], sin_v[:, :LANES]) for h in range(MLA_HEADS)], axis=1)
        dkv = jnp.concatenate([dkf_v[h][:, :QK_NOPE] for h in range(MLA_HEADS)] + [dv_v], axis=1)
        dkpe = dkf_v[0][:, QK_NOPE:] + dkf_v[1][:, QK_NOPE:] + dkf_v[2][:, QK_NOPE:] + dkf_v[3][:, QK_NOPE:]
        return dq, dkv, dkpe

    n_rows = dv.shape[0]
    return _rowwise("mla_bwd_prep", fn, [dqf, dkf, dv, cos, sin],
                    [_sds((n_rows, UQ_PAD), BF16), _sds((n_rows, 1024), BF16), _sds((n_rows, LANES))], ts=256)


def _assemble_dh(dqkvs, dcqn, cq, dckvn, ckv, dkpe, cos, sin, gq, gkv):
    def fn(g1, g2, g3, dcqn_v, cq_v, dckvn_v, ckv_v, dkpe_v, cos_v, sin_v, gq_v, gkv_v):
        g = g1 + g2 + g3
        dqa = _rope_t(g[:, :A_WIDTH], cos_v, sin_v) * A_SCALE
        dka = _rope_t(g[:, A_WIDTH:2 * A_WIDTH], cos_v, sin_v)
        dcq, dgq = _rms_bwd(cq_v, gq_v, dcqn_v)
        dckv, dgkv = _rms_bwd(ckv_v, gkv_v, dckvn_v)
        dkr = _rope_t(dkpe_v, cos_v[:, :LANES], sin_v[:, :LANES])
        return jnp.concatenate([dqa, dka, g[:, 2 * A_WIDTH:], dcq, dkr, dckv], axis=1), dgq, dgkv

    n_rows = cq.shape[0]
    return _rowwise("assemble_dh", fn, [*dqkvs, dcqn, cq, dckvn, ckv, dkpe, cos, sin],
                    [_sds((n_rows, IN_PAD), BF16)], consts=[gq, gkv],
                    reds=[_sds((1, Q_LORA)), _sds((1, KV_LORA))], ts=256,
                    in_dils=[dil for _, dil in PATTERNS] + [1] * 7)


def _layer_fwd(x_f32, x_bf, w, sm, cos, sin, behind=None, rest=None):
    *qkv_views, cq, kpe, ckv = _in_proj(x_bf, w["w_in"], cos, sin)
    ol_views, brought = [], []
    for i, (view, (_, dil)) in enumerate(zip(qkv_views, PATTERNS)):
        ol, *got = _dil_fwd(view, dil, comm=rest[0][i] if rest else None)
        ol_views.append(ol)
        brought.append(got)
    if rest:
        w = {**w, **rest[1](brought)}
    cqn, qf = _mla_q_prep(cq, sm["q_a_norm"], w["w_uq"], cos, sin)
    ckvn, kf, v = _mla_kv_prep(ckv, kpe, sm["kv_a_norm"], w["w_ukv"])
    b_out, b_lse, *behind_outs = _mla_fwd(qf, kf, v, comm=behind)
    a_out, mixed = _mix_fwd(ol_views, b_out, sm["a_out_norm"], sm["b_out_norm"])
    x1, x1_bf, r1 = _mm_res_ln("wo_ln1", mixed, w["w_o"], x_f32, sm["ln1_g"], sm["ln1_b"])
    f = _mm_relu2(x1_bf, w["w_ff1"])
    x2, x2_bf, r2 = _mm_res_ln("ff2_ln2", f, w["w_ff2"], x1, sm["ln2_g"], sm["ln2_b"])
    saved = dict(w=w, x_bf=x_bf, qkv_views=qkv_views, ol_views=ol_views, cq=cq, ckv=ckv, cqn=cqn, qf=qf, ckvn=ckvn,
                 kf=kf, v=v, b_out=b_out, b_lse=b_lse, a_out=a_out, mixed=mixed, x1_bf=x1_bf, r1=r1, f=f, r2=r2)
    return x2, x2_bf, saved, behind_outs


def _layer_bwd(dx2, w, sm, sv, cos, sin, reduce_of, pending=None):
    dr2, dr2_bf, dg2, db2 = _ln_bwd("ln2_bwd", dx2, sv["r2"], sm["ln2_g"])
    (dw_ff2,) = _mm("dw_ff2", sv["f"], dr2_bf, (F32,), ta=True, tm=1024, tn=1024, tk=2048)
    du, *swapped = _mm("d_u", dr2_bf, w["w_ff2"], (BF16,), tb=True, tm=1024, tn=2048, tk=D_MODEL,
                       epilogue=lambda acc, f: (acc * (2.0 * jnp.sqrt(f.astype(F32))),), extras=[sv["f"]],
                       comm=_swap_comm(pending.pieces) if pending else None)
    if pending:
        pending.after_swap(swapped)
    (dw_ff1,) = _mm("dw_ff1", sv["x1_bf"], du, (F32,), ta=True, tm=1024, tn=1024, tk=2048, out_pieces=True)
    (dy1,) = _mm("d_x1", du, w["w_ff1"], (F32,), tb=True, tm=1024, tn=1024, tk=2048,
                 epilogue=lambda acc, d: (acc + ALPHA * d,), extras=[dr2], b_pieces=True)
    dr1, dr1_bf, dg1, db1 = _ln_bwd("ln1_bwd", dy1, sv["r1"], sm["ln1_g"])
    (dw_o,) = _mm("dw_o", sv["mixed"], dr1_bf, (F32,), ta=True, tm=1024, tn=1024, tk=2048)
    early = reduce_of(dict(w_o=dw_o, w_ff1=dw_ff1, w_ff2=dw_ff2))
    (dmixed,) = _mm("d_mixed", dr1_bf, w["w_o"], (F32,), tb=True, tm=1024, tn=1024, tk=D_MODEL)
    dod1, dod2, dod3, do_b, delta_b, dga, dgb = _mix_bwd(
        dmixed, sv["a_out"], sv["b_out"], sv["ol_views"], sm["a_out_norm"], sm["b_out_norm"])
    dqkvs = []
    for i, ((_, dil), qkv_view, ol_view, dod_view) in enumerate(zip(PATTERNS, sv["qkv_views"], sv["ol_views"], (dod1, dod2, dod3))):
        swap_here = early is not None and i == 0
        dqkv, *swapped = _dil_bwd(qkv_view, ol_view, dod_view, dil, comm=_swap_comm(early.pieces) if swap_here else None)
        dqkvs.append(dqkv)
        if swap_here:
            early.after_swap(swapped)
    to_scatter = (early.s16 if early else []) + (pending.s16 if pending else [])
    dqf, dkf, dv_b, *scattered = _mla_bwd(sv["qf"], sv["kf"], sv["v"], sv["b_lse"], do_b, delta_b,
                                          comm=_scatter_comm(to_scatter) if to_scatter else None)
    if early:
        early.after_scatter(scattered[:len(early.s16)])
    if pending:
        pending.after_scatter(scattered[len(scattered) - len(pending.s16):])
    dq_b, dkv_b, dkpe = _mla_bwd_prep(dqf, dkf, dv_b, cos, sin)
    (dw_uq,) = _mm("dw_uq", sv["cqn"], dq_b, (F32,), ta=True, tm=Q_LORA, tn=1024, tk=2048)
    (dcqn,) = _mm("d_cqn", dq_b, w["w_uq"], (F32,), tb=True, tm=1024, tn=Q_LORA, tk=UQ_PAD)
    (dw_ukv,) = _mm("dw_ukv", sv["ckvn"], dkv_b, (F32,), ta=True, tm=KV_LORA, tn=1024, tk=2048)
    (dckvn,) = _mm("d_ckvn", dkv_b, w["w_ukv"], (F32,), tb=True, tm=1024, tn=KV_LORA, tk=1024)
    dh, dgq, dgkv = _assemble_dh(dqkvs, dcqn, sv["cq"], dckvn, sv["ckv"], dkpe, cos, sin,
                                 sm["q_a_norm"], sm["kv_a_norm"])
    (dw_in,) = _mm("dw_in", sv["x_bf"], dh, (F32,), ta=True, tm=1024, tn=1024, tk=2048)
    (dx,) = _mm("d_x", dh, w["w_in"], (F32,), tb=True, tm=1024, tn=1024, tk=2048,
                epilogue=lambda acc, d: (acc + ALPHA * d,), extras=[dr1])
    late = reduce_of(dict(w_in=dw_in, w_uq=dw_uq, w_ukv=dw_ukv))
    dsm = dict(q_a_norm=dgq, kv_a_norm=dgkv, a_out_norm=dga, b_out_norm=dgb, ln1_g=dg1, ln1_b=db1, ln2_g=dg2, ln2_b=db2)
    return dx, dsm, late


def _pad_w_in(w):
    return jnp.concatenate([w[:, :1792], w[:, 1920:1984], jnp.zeros((w.shape[0], 64), w.dtype), w[:, 1792:1920]], axis=1)


def _unpad_w_in(w):
    return jnp.concatenate([w[:, :1792], w[:, 1920:2048], w[:, 1792:1856]], axis=1)


def _rope_tables(n_rows):
    half = A_HEAD_DIM // 2
    inv_freq = ROPE_THETA ** (-jnp.arange(half, dtype=F32) / half)
    ang = jnp.arange(n_rows, dtype=F32)[:, None] * inv_freq[None, :]
    cos = jnp.tile(jnp.cos(ang), (1, 2 * A_HEADS))
    sin = jnp.tile(jnp.concatenate([-jnp.sin(ang), jnp.sin(ang)], axis=1), (1, A_HEADS))
    return cos, sin


W_NAMES = ("w_in", "w_uq", "w_ukv", "w_o", "w_ff1", "w_ff2")
PIECE = dict(w_in=(512, 512), w_uq=(128, 256), w_ukv=(64, 256), w_o=(128, 1024), w_ff1=(512, 1024), w_ff2=(512, 1024))
SHARD_COLS = dict(w_in=496, w_uq=192, w_ukv=256, w_o=1024, w_ff1=1024, w_ff2=1024)


def _col_pieces(t, pad_to):
    rows, _, cols = t.shape
    t = jnp.pad(t, ((0, 0), (0, 0), (0, pad_to - cols)))
    return t.reshape(2, rows // 2, N_CHIPS, pad_to).transpose(2, 0, 1, 3)


def _grad_pieces(dws):
    def heads(t, rows):
        return jnp.concatenate([t[:, :512].reshape(rows, MLA_HEADS, LANES), t[:, 512:].reshape(rows, MLA_HEADS, LANES)], axis=2)

    make = dict(
        w_in=lambda t: _col_pieces(_unpad_w_in(t).reshape(D_MODEL, N_CHIPS, SHARD_COLS["w_in"]), 512),
        w_uq=lambda t: _col_pieces(heads(t, Q_LORA), 256),
        w_ukv=lambda t: _col_pieces(heads(t, KV_LORA), 256),
        w_o=lambda t: t.reshape(N_CHIPS, 2, 128, D_MODEL),
        w_ff1=lambda t: t,
        w_ff2=lambda t: t.reshape(N_CHIPS, 2, 512, D_MODEL))
    return {n: make[n](t) for n, t in dws.items()}


def _weights_from_pieces(p):
    def cols(t):
        return t.transpose(1, 2, 0, 3).reshape(2 * t.shape[2], N_CHIPS, t.shape[3])

    def heads(t, rows):
        return jnp.concatenate([t[:, :, :LANES].reshape(rows, 512), t[:, :, LANES:].reshape(rows, 512)], axis=1)

    make = dict(
        w_in=lambda t: _pad_w_in(cols(t)[:, :, :SHARD_COLS["w_in"]].reshape(D_MODEL, IN_COLS)),
        w_uq=lambda t: heads(cols(t), Q_LORA),
        w_ukv=lambda t: heads(cols(t), KV_LORA),
        w_o=lambda t: t.reshape(D_MODEL, D_MODEL),
        w_ff1=lambda t: t,
        w_ff2=lambda t: t.reshape(D_FF, D_MODEL))
    return {n: make[n](t) for n, t in p.items()}


def _my_piece(name, shard, half):
    rows, cols = shard.shape[0] // 2, shard.shape[1]
    t = lax.dynamic_slice_in_dim(shard, half * rows, rows, axis=0).astype(BF16)
    return jnp.pad(t, ((0, 0), (0, PIECE[name][1] - cols)))


MESH = pl.DeviceIdType.MESH
ANY = pl.BlockSpec(memory_space=pl.ANY)


class _Comm:
    def __init__(self, ins, out_shapes, n_sems, start, finish):
        self.ins, self.out_shapes, self.n_sems, self.start, self.finish = list(ins), list(out_shapes), n_sems, start, finish

    def sem_scratch(self):
        return [pltpu.SemaphoreType.DMA((self.n_sems,)), pltpu.SemaphoreType.DMA((self.n_sems,))]


def _comm_call(name, comm):
    n_in, n_out = len(comm.ins), len(comm.out_shapes)

    def body(*refs):
        ins, outs, sems = refs[:n_in], refs[n_in:n_in + n_out], refs[n_in + n_out:]
        comm.start(ins, outs, *sems)
        comm.finish(ins, outs, *sems)

    return pl.pallas_call(body, name=name, out_shape=comm.out_shapes, in_specs=[ANY] * n_in, out_specs=[ANY] * n_out,
                          scratch_shapes=comm.sem_scratch())(*comm.ins)


def _all_gather_comm(blocks):
    n = len(blocks)

    def plan(x_refs, out_refs, send_sems, recv_sems):
        x, y, c = lax.axis_index("x"), lax.axis_index("y"), lax.axis_index("c")
        me, sibling = (x, y, c), (x, y, 1 - c)
        chips = [(1 - x, y), (x, 1 - y), (1 - x, 1 - y)]

        def copy(t, k, blk, to, src=None):
            px, py, pc = blk
            slot = out_refs[t].at[4 * px + 2 * py + pc]
            return pltpu.make_async_remote_copy(
                src_ref=slot if src is None else src, dst_ref=slot,
                send_sem=send_sems.at[7 * t + k], recv_sem=recv_sems.at[7 * t + k], device_id=to, device_id_type=MESH)

        first = []
        for t in range(n):
            first.append(copy(t, 0, me, sibling, src=x_refs[t]))
            first += [copy(t, 1 + j, me, (*chip, c), src=x_refs[t]) for j, chip in enumerate(chips)]
        return me, sibling, chips, c, copy, first

    def start(x_refs, out_refs, send_sems, recv_sems):
        for cp in plan(x_refs, out_refs, send_sems, recv_sems)[-1]:
            cp.start()

    def finish(x_refs, out_refs, send_sems, recv_sems):
        me, sibling, chips, c, copy, sent = plan(x_refs, out_refs, send_sems, recv_sems)
        for j, chip in enumerate(chips):
            for t in range(n):
                copy(t, 1 + j, (*chip, c), me).wait_recv()
                sent.append(copy(t, 4 + j, (*chip, c), sibling))
                sent[-1].start()
        for t in range(n):
            copy(t, 0, sibling, me).wait_recv()
        for j, chip in enumerate(chips):
            for t in range(n):
                copy(t, 4 + j, (*chip, 1 - c), me).wait_recv()
        for cp in sent:
            cp.wait_send()

    return _Comm(blocks, [_sds((N_DEV,) + b.shape, b.dtype) for b in blocks], 7 * n, start, finish)


def _simple_comm(ins, out_shapes, n_sems, copies):
    def start(in_refs, out_refs, send_sems, recv_sems):
        for cp in copies(in_refs, out_refs, send_sems, recv_sems):
            cp.start()

    def finish(in_refs, out_refs, send_sems, recv_sems):
        for cp in copies(in_refs, out_refs, send_sems, recv_sems):
            cp.wait()

    return _Comm(ins, out_shapes, n_sems, start, finish)


def _swap_comm(gs):
    def copies(g_refs, got_refs, send_sems, recv_sems):
        c = lax.axis_index("c")
        sibling = (lax.axis_index("x"), lax.axis_index("y"), 1 - c)
        return [pltpu.make_async_remote_copy(src_ref=g_refs[t].at[k, 1 - c], dst_ref=got_refs[t].at[k],
                                             send_sem=send_sems.at[N_CHIPS * t + k], recv_sem=recv_sems.at[N_CHIPS * t + k],
                                             device_id=sibling, device_id_type=MESH)
                for t in range(len(gs)) for k in range(N_CHIPS)]

    return _simple_comm(gs, [_sds((N_CHIPS,) + g.shape[2:], g.dtype) for g in gs], N_CHIPS * len(gs), copies)


def _scatter_comm(hs):
    def copies(h_refs, got_refs, send_sems, recv_sems):
        x, y, c = lax.axis_index("x"), lax.axis_index("y"), lax.axis_index("c")
        chips = [(1 - x, y), (x, 1 - y), (1 - x, 1 - y)]
        return [pltpu.make_async_remote_copy(src_ref=h_refs[t].at[2 * px + py], dst_ref=got_refs[t].at[rel],
                                             send_sem=send_sems.at[3 * t + rel], recv_sem=recv_sems.at[3 * t + rel],
                                             device_id=(px, py, c), device_id_type=MESH)
                for t in range(len(hs)) for rel, (px, py) in enumerate(chips)]

    return _simple_comm(hs, [_sds((3,) + h.shape[1:], h.dtype) for h in hs], 3 * len(hs), copies)


def _share_comm(rs):
    def copies(r_refs, got_refs, send_sems, recv_sems):
        sibling = (lax.axis_index("x"), lax.axis_index("y"), 1 - lax.axis_index("c"))
        return [pltpu.make_async_remote_copy(src_ref=r_refs[t], dst_ref=got_refs[t], send_sem=send_sems.at[t],
                                             recv_sem=recv_sems.at[t], device_id=sibling, device_id_type=MESH)
                for t in range(len(rs))]

    return _simple_comm(rs, [_sds(r.shape, r.dtype) for r in rs], len(rs), copies)


SUM_STEPS = 4


def _pair_sums(name, gs, gots, c_arr):
    n = len(gs)
    dims = [(g.shape[2] // SUM_STEPS, g.shape[3]) for g in gs]

    def body(c_ref, *refs):
        for t in range(n):
            s = refs[t][...] + refs[n + t][...]
            refs[2 * n + t][...] = s
            refs[3 * n + t][...] = s.astype(BF16)

    def part(tr, pc):
        return pl.BlockSpec((None, tr, pc), lambda k, i, c_ref: (k, i, 0))

    def kept(tr, pc):
        return pl.BlockSpec((None, None, tr, pc), lambda k, i, c_ref: (k, c_ref[0], i, 0))

    outs = pl.pallas_call(
        body, name=name,
        out_shape=[_sds((N_CHIPS,) + g.shape[2:]) for g in gs] + [_sds((N_CHIPS,) + g.shape[2:], BF16) for g in gs],
        grid_spec=pltpu.PrefetchScalarGridSpec(
            num_scalar_prefetch=1, grid=(N_CHIPS, SUM_STEPS),
            in_specs=[kept(*d) for d in dims] + [part(*d) for d in dims],
            out_specs=[part(*d) for d in dims] * 2),
        compiler_params=_params(("parallel", "parallel")),
    )(c_arr, *gs, *gots)
    return list(outs[:n]), list(outs[n:])


def _chips_sums(name, h32s, gots, chip_arr):
    n = len(h32s)
    dims = [(h.shape[1] // SUM_STEPS, h.shape[2]) for h in h32s]

    def body(chip_ref, *refs):
        for t in range(n):
            got_ref = refs[n + t]
            refs[2 * n + t][...] = refs[t][...] + got_ref[0] + got_ref[1] + got_ref[2]

    return pl.pallas_call(
        body, name=name, out_shape=[_sds(h.shape[1:]) for h in h32s],
        grid_spec=pltpu.PrefetchScalarGridSpec(
            num_scalar_prefetch=1, grid=(SUM_STEPS,),
            in_specs=[pl.BlockSpec((None, tr, pc), lambda i, chip_ref: (chip_ref[0], i, 0)) for tr, pc in dims]
            + [pl.BlockSpec((3, tr, pc), lambda i, chip_ref: (0, i, 0)) for tr, pc in dims],
            out_specs=[pl.BlockSpec((tr, pc), lambda i, chip_ref: (i, 0)) for tr, pc in dims]),
        compiler_params=_params(("parallel",)),
    )(chip_arr, *h32s, *gots)


class _Reduce:
    def __init__(self, tag, dws, c, chip):
        gp = _grad_pieces(dws)
        self.tag, self.c, self.names, self.pieces = tag, c, list(gp), list(gp.values())
        self.c_arr, self.chip_arr = jnp.reshape(c, (1,)).astype(jnp.int32), jnp.reshape(chip, (1,)).astype(jnp.int32)
        self.shards = None

    def after_swap(self, gots):
        self.s32, self.s16 = _pair_sums(f"rs_pair_sum_{self.tag}", self.pieces, gots, self.c_arr)

    def after_scatter(self, parts):
        mine = _chips_sums(f"rs_chips_sum_{self.tag}", self.s32, parts, self.chip_arr)
        theirs = _comm_call(f"rs_share_{self.tag}", _share_comm(mine))
        self.shards = {}
        for n, a, b in zip(self.names, mine, theirs):
            both = jnp.where(self.c == 0, jnp.concatenate([a, b], axis=0), jnp.concatenate([b, a], axis=0))
            self.shards[n] = both[:, :SHARD_COLS[n]]

    def run_alone(self):
        self.after_swap(_comm_call(f"rs_pair_{self.tag}", _swap_comm(self.pieces)))
        self.after_scatter(_comm_call(f"rs_chips_{self.tag}", _scatter_comm(self.s16)))


class _LayerWeights:
    def __init__(self, shards, c, dev):
        self.shards, self.c, self.dev = shards, c, dev

    def comm(self, l, names=W_NAMES):
        return _all_gather_comm([_my_piece(n, self.shards[l][n], self.c) for n in names])

    def take(self, l, gathered, names=W_NAMES):
        pieces = {}
        for n, g in zip(names, gathered):
            mine = _my_piece(n, self.shards[l][n], self.c)
            pieces[n] = lax.dynamic_update_index_in_dim(g, mine, self.dev, 0).reshape((N_CHIPS, 2) + mine.shape)
        return _weights_from_pieces(pieces)

    def first(self):
        now, later = ("w_in", "w_uq", "w_ukv"), ("w_ff1", "w_ff2", "w_o")
        weights = self.take(0, _comm_call("ag_l0", self.comm(0, now)), now)
        comms = [self.comm(0, (n,)) for n in later]

        def take_rest(results):
            rest = {}
            for n, gathered in zip(later, results):
                rest.update(self.take(0, gathered, (n,)))
            return rest

        return weights, (comms, take_rest)


def _all_reduce_small(vec):
    rows, lanes = vec.shape

    def gather_body(x_ref, out_ref, send_sems, recv_sems):
        x, y, c = lax.axis_index("x"), lax.axis_index("y"), lax.axis_index("c")
        me, sibling = (x, y, c), (x, y, 1 - c)
        chips = [(1 - x, y), (x, 1 - y), (1 - x, 1 - y)]

        def slot(px, py, pc):
            return out_ref.at[4 * px + 2 * py + pc]

        def copy(k, blk, to, src=None):
            return pltpu.make_async_remote_copy(
                src_ref=slot(*blk) if src is None else src, dst_ref=slot(*blk),
                send_sem=send_sems.at[k], recv_sem=recv_sems.at[k], device_id=to, device_id_type=MESH)

        out_ref[4 * x + 2 * y + c] = x_ref[...]
        first = [copy(0, me, sibling, src=x_ref)]
        first += [copy(1 + j, me, (*chip, c), src=x_ref) for j, chip in enumerate(chips)]
        for cp in first:
            cp.start()
        passed = [copy(4 + j, (*chip, c), sibling) for j, chip in enumerate(chips)]
        for j, chip in enumerate(chips):
            copy(1 + j, (*chip, c), me).wait_recv()
            passed[j].start()
        copy(0, sibling, me).wait_recv()
        for j, chip in enumerate(chips):
            copy(4 + j, (*chip, 1 - c), me).wait_recv()
        for cp in first + passed:
            cp.wait_send()

    vmem = pl.BlockSpec(memory_space=pltpu.VMEM)
    allv = pl.pallas_call(
        gather_body, name="small_all_gather", out_shape=_sds((N_DEV, rows, lanes)),
        in_specs=[vmem], out_specs=vmem,
        scratch_shapes=[pltpu.SemaphoreType.DMA((7,)), pltpu.SemaphoreType.DMA((7,))],
    )(vec)

    def sum_body(a_ref, o_ref):
        acc = a_ref[0]
        for d in range(1, N_DEV):
            acc = acc + a_ref[d]
        o_ref[...] = acc

    return pl.pallas_call(sum_body, name="small_sum", out_shape=_sds((rows, lanes)), in_specs=[vmem], out_specs=vmem)(allv)


def _adamw(name, w, g, m, v, ts=512):
    def fn(w_v, g_v, m_v, v_v):
        m_n = ADAM_B1 * m_v + (1.0 - ADAM_B1) * g_v
        v_n = ADAM_B2 * v_v + (1.0 - ADAM_B2) * (g_v * g_v)
        m_hat = m_n / (1.0 - ADAM_B1 ** ADAM_STEP)
        v_hat = v_n / (1.0 - ADAM_B2 ** ADAM_STEP)
        delta = -ADAM_LR * (m_hat / (jnp.sqrt(v_hat) + ADAM_EPS) + ADAM_WD * w_v)
        return delta, m_n, v_n

    return _rowwise(name, fn, [w, g, m, v], [_sds(w.shape)] * 3, ts=ts)


def _pack_small(per_layer):
    flat = jnp.concatenate([per_layer[l][n].reshape(-1) for l in range(DEPTH) for n in SMALL_NAMES])
    return jnp.pad(flat, (0, SMALL_ROWS * LANES - flat.shape[0])).reshape(SMALL_ROWS, LANES)


def _unpack_small(packed):
    flat = packed.reshape(-1)
    per = sum(SMALL_SIZES)
    out = {}
    for n, size, off in zip(SMALL_NAMES, SMALL_SIZES, [sum(SMALL_SIZES[:i]) for i in range(len(SMALL_SIZES))]):
        out[n] = jnp.stack([flat[l * per + off:l * per + off + size] for l in range(DEPTH)])
    return out


def _fwd_bwd(x, target, layer_weights, smalls, on_layer_grads):
    depth = len(smalls)
    cos, sin = _rope_tables(x.shape[0])
    h_f32, h_bf = x, x.astype(BF16)
    saved = []
    w, rest = layer_weights.first()
    for l in range(depth):
        behind = layer_weights.comm(l + 1) if l + 1 < depth else None
        h_f32, h_bf, sv, brought = _layer_fwd(h_f32, h_bf, w, smalls[l], cos, sin, behind, rest)
        saved.append(sv)
        if l + 1 < depth:
            w, rest = layer_weights.take(l + 1, brought), None
    loss_part, dh = _loss_fn(h_f32, target)
    small_grads, pending = [None] * depth, None
    for l in reversed(range(depth)):
        dh, small_grads[l], pending = _layer_bwd(dh, saved[l]["w"], smalls[l], saved[l], cos, sin,
                                                 lambda dws, l=l: on_layer_grads(l, dws), pending)
    return loss_part, dh, small_grads, pending


def kernel(x, w_in, q_a_norm, kv_a_norm, w_uq, w_ukv, a_out_norm, b_out_norm, w_o, ln1_g, ln1_b, w_ff1, w_ff2, ln2_g, ln2_b, loss_target, m_w_in, m_q_a_norm, m_kv_a_norm, m_w_uq, m_w_ukv, m_a_out_norm, m_b_out_norm, m_w_o, m_ln1_g, m_ln1_b, m_w_ff1, m_w_ff2, m_ln2_g, m_ln2_b, v_w_in, v_q_a_norm, v_kv_a_norm, v_w_uq, v_w_ukv, v_a_out_norm, v_b_out_norm, v_w_o, v_ln1_g, v_ln1_b, v_w_ff1, v_w_ff2, v_ln2_g, v_ln2_b):
    c = lax.axis_index("c")
    chip = 2 * lax.axis_index("x") + lax.axis_index("y")
    big = dict(w_in=w_in, w_uq=w_uq, w_ukv=w_ukv, w_o=w_o, w_ff1=w_ff1, w_ff2=w_ff2)
    big_m = dict(w_in=m_w_in, w_uq=m_w_uq, w_ukv=m_w_ukv, w_o=m_w_o, w_ff1=m_w_ff1, w_ff2=m_w_ff2)
    big_v = dict(w_in=v_w_in, w_uq=v_w_uq, w_ukv=v_w_ukv, w_o=v_w_o, w_ff1=v_w_ff1, w_ff2=v_w_ff2)
    small = dict(q_a_norm=q_a_norm, kv_a_norm=kv_a_norm, a_out_norm=a_out_norm, b_out_norm=b_out_norm,
                 ln1_g=ln1_g, ln1_b=ln1_b, ln2_g=ln2_g, ln2_b=ln2_b)
    small_m = dict(q_a_norm=m_q_a_norm, kv_a_norm=m_kv_a_norm, a_out_norm=m_a_out_norm, b_out_norm=m_b_out_norm,
                   ln1_g=m_ln1_g, ln1_b=m_ln1_b, ln2_g=m_ln2_g, ln2_b=m_ln2_b)
    small_v = dict(q_a_norm=v_q_a_norm, kv_a_norm=v_kv_a_norm, a_out_norm=v_a_out_norm, b_out_norm=v_b_out_norm,
                   ln1_g=v_ln1_g, ln1_b=v_ln1_b, ln2_g=v_ln2_g, ln2_b=v_ln2_b)

    layer_weights = _LayerWeights([{n: big[n][l] for n in W_NAMES} for l in range(DEPTH)], c, 2 * chip + c)
    smalls = [{n: small[n][l][None, :] for n in SMALL_NAMES} for l in range(DEPTH)]
    reductions = [[] for _ in range(DEPTH)]

    def reduce_layer(l, dws):
        reductions[l].append(_Reduce(f"l{l}_{len(reductions[l])}", dws, c, chip))
        return reductions[l][-1]

    loss_part, grad_x, small_grads, last = _fwd_bwd(x[0], loss_target[0], layer_weights, smalls, reduce_layer)
    last.run_alone()
    shard_grads = [{n: g for r in rs for n, g in r.shards.items()} for rs in reductions]
    loss = lax.psum(loss_part[0, 0], ("x", "y", "c"))
    grad_x = grad_x[None]

    g_small_packed = _all_reduce_small(_pack_small(small_grads))
    pack_in = lambda d: _pack_small([{n: d[n][l] for n in SMALL_NAMES} for l in range(DEPTH)])
    sd, sm_, sv_ = _adamw("adamw_small", pack_in(small), g_small_packed, pack_in(small_m), pack_in(small_v), ts=SMALL_ROWS)
    g_small, d_small, m_small, v_small = (_unpack_small(t) for t in (g_small_packed, sd, sm_, sv_))

    g_big, d_big, m_big, v_big = {}, {}, {}, {}
    for n in W_NAMES:
        g = jnp.stack([shard_grads[l][n] for l in range(DEPTH)])
        shape = g.shape
        flat = lambda t: t.reshape(shape[0] * shape[1], shape[2])
        d_, m_, v_ = _adamw(f"adamw_{n}", flat(big[n]), flat(g), flat(big_m[n]), flat(big_v[n]))
        g_big[n], d_big[n], m_big[n], v_big[n] = g, d_.reshape(shape), m_.reshape(shape), v_.reshape(shape)

    order = ("w_in", "q_a_norm", "kv_a_norm", "w_uq", "w_ukv", "a_out_norm", "b_out_norm", "w_o", "ln1_g", "ln1_b",
             "w_ff1", "w_ff2", "ln2_g", "ln2_b")
    pick = lambda bigd, smalld: [bigd[n] if n in bigd else smalld[n] for n in order]
    return (loss, grad_x, *pick(g_big, g_small), *pick(d_big, d_small), *pick(m_big, m_small), *pick(v_big, v_small))
```

```python
import math

import jax
import jax.numpy as jnp
from jax import lax
from jax.experimental import pallas as pl
from jax.experimental.pallas import tpu as pltpu

F32, BF16 = jnp.float32, jnp.bfloat16

D_MODEL = 1024
DEPTH = 4
A_HEAD_DIM = 64
A_HEADS = 8
A_WIDTH = 512
PATTERNS = ((128, 1), (512, 4), (2048, 16))
SPAN = 128
MLA_HEADS = 4
QK_NOPE = 128
QK_ROPE = 64
V_DIM = 128
Q_LORA = 256
KV_LORA = 128
MLA_WIDTH = 512
QK_PAD = 256
IN_COLS = 1984
IN_PAD = 2048
UQ_PAD = 1024
D_FF = 4096
ROPE_THETA = 10000.0
ALPHA = (2.0 * DEPTH) ** 0.25
LN_EPS = 1e-5
RMS_EPS = 1e-6
MLA_SCALE = (QK_NOPE + QK_ROPE) ** -0.5
A_SCALE = A_HEAD_DIM ** -0.5

ADAM_LR, ADAM_B1, ADAM_B2, ADAM_EPS, ADAM_WD, ADAM_STEP = 0.001, 0.9, 0.999, 1e-08, 0.01, 10

VMEM_LIMIT_BYTES = 56 * 1024 * 1024
NEG = -1e30
N_CHIPS, N_DEV = 4, 8
LANES = 128

SMALL_NAMES = ("q_a_norm", "kv_a_norm", "a_out_norm", "b_out_norm", "ln1_g", "ln1_b", "ln2_g", "ln2_b")
SMALL_SIZES = (256, 128, 512, 512, 1024, 1024, 1024, 1024)
SMALL_ROWS = 176


def _params(sem):
    return pltpu.CompilerParams(dimension_semantics=sem, vmem_limit_bytes=VMEM_LIMIT_BYTES)


FAST_STRIDE = 4


def _interleave(parts, scratch):
    d, (n, width) = len(parts), parts[0].shape
    if d > FAST_STRIDE and d % FAST_STRIDE == 0:
        inner = [_interleave(parts[r::FAST_STRIDE], scratch) for r in range(FAST_STRIDE)]
        return _interleave(inner, scratch)
    for r in range(d):
        for s in range(width // LANES):
            scratch.at[s][pl.ds(r, n, stride=d), :] = parts[r][:, s * LANES:(s + 1) * LANES]
    return jnp.concatenate([scratch.at[s][pl.ds(0, n * d), :] for s in range(width // LANES)], axis=1)


def _deinterleave(x, d, scratch):
    n, width = x.shape[0] // d, x.shape[1]
    if d > FAST_STRIDE and d % FAST_STRIDE == 0:
        outer = _deinterleave(x, FAST_STRIDE, scratch)
        inner = [_deinterleave(y, d // FAST_STRIDE, scratch) for y in outer]
        return [inner[r % FAST_STRIDE][r // FAST_STRIDE] for r in range(d)]
    for s in range(width // LANES):
        scratch.at[s][pl.ds(0, n * d), :] = x[:, s * LANES:(s + 1) * LANES]
    return [jnp.concatenate([scratch.at[s][pl.ds(r, n, stride=d), :] for s in range(width // LANES)], axis=1) for r in range(d)]


def _rowwise(name, fn, ins, outs, consts=(), reds=(), ts=512, in_dils=None, out_dils=None):
    in_dils = list(in_dils or [1] * len(ins))
    out_dils = list(out_dils or [1] * len(outs))
    n_rows = ins[0].shape[-2] * in_dils[0]
    ts = min(ts, n_rows)
    assert n_rows % ts == 0
    n_in, n_c, n_o, n_r = len(ins), len(consts), len(outs), len(reds)
    viewed = [(a.shape, d) for a, d in zip(list(ins) + list(outs), in_dils + out_dils) if d > 1]

    def tile_spec(shape, d=1):
        if len(shape) == 2:
            return pl.BlockSpec((ts // d, shape[1]), lambda i: (i, 0))
        return pl.BlockSpec((shape[0], ts, shape[2]), lambda i: (0, i, 0))

    def whole_spec(shape):
        return pl.BlockSpec(shape, lambda i: (0,) * len(shape))

    def body(*refs):
        in_refs, c_refs = refs[:n_in], refs[n_in:n_in + n_c]
        o_refs = refs[n_in + n_c:n_in + n_c + n_o]
        r_refs = refs[n_in + n_c + n_o:n_in + n_c + n_o + n_r]
        scratches = list(refs[n_in + n_c + n_o + n_r:])
        vals = []
        for r, d in zip(in_refs, in_dils):
            if d == 1:
                vals.append(r[...])
            else:
                width = r.shape[1] // d
                vals.append(_interleave([r[:, k * width:(k + 1) * width] for k in range(d)], scratches.pop(0)))
        res = fn(*vals, *[r[...] for r in c_refs])
        res = tuple(res) if isinstance(res, (tuple, list)) else (res,)
        for r, v, d in zip(o_refs, res[:n_o], out_dils):
            if len(r.shape) == 3:
                for g in range(r.shape[0]):
                    r[g] = v[g].astype(r.dtype)
            elif d == 1:
                r[...] = v.astype(r.dtype)
            else:
                width = r.shape[1] // d
                for k, part in enumerate(_deinterleave(v, d, scratches.pop(0))):
                    r[:, k * width:(k + 1) * width] = part.astype(r.dtype)
        if n_r:
            i = pl.program_id(0)

            @pl.when(i == 0)
            def _():
                for r, v in zip(r_refs, res[n_o:]):
                    r[...] = v

            @pl.when(i > 0)
            def _():
                for r, v in zip(r_refs, res[n_o:]):
                    r[...] += v

    out = pl.pallas_call(
        body, name=name, grid=(n_rows // ts,),
        in_specs=[tile_spec(a.shape, d) for a, d in zip(ins, in_dils)] + [whole_spec(c.shape) for c in consts],
        out_specs=[tile_spec(o.shape, d) for o, d in zip(outs, out_dils)] + [whole_spec(r.shape) for r in reds],
        out_shape=list(outs) + list(reds),
        scratch_shapes=[pltpu.VMEM((shape[1] // d // LANES, ts, LANES), F32) for shape, d in viewed],
        compiler_params=_params(("arbitrary",)),
    )(*ins, *consts)
    return out


def _sds(shape, dtype=F32):
    return jax.ShapeDtypeStruct(tuple(shape), dtype)


def _hosted(comm, grid):
    if comm is None:
        return [], [], [], [], (lambda *a: None), (lambda *a: None)

    def edge(which, at, ins, outs, sems):
        ids = [pl.program_id(d) for d in range(len(grid))]
        hit = ids[0] == at[0]
        for i, v in zip(ids[1:], at[1:]):
            hit = hit & (i == v)

        @pl.when(hit)
        def _():
            which(ins, outs, *sems)

    begin = lambda ins, outs, sems: edge(comm.start, [0] * len(grid), ins, outs, sems)
    end = lambda ins, outs, sems: edge(comm.finish, [g - 1 for g in grid], ins, outs, sems)
    return comm.ins, [ANY] * len(comm.ins), comm.out_shapes, [ANY] * len(comm.out_shapes), begin, end


def _mm(name, a, b, out_dtypes, *, ta=False, tb=False, tm=512, tn=512, tk=512, epilogue=None, extras=(), vecs=(),
        comm=None, b_pieces=False, out_pieces=False):
    (k_dim, m_dim) = a.shape if ta else a.shape[::-1]
    if b_pieces:
        b_rows, b_cols = 2 * b.shape[2], N_CHIPS * b.shape[3]
        (n_dim, k2) = (b_rows, b_cols) if tb else (b_cols, b_rows)
    else:
        (n_dim, k2) = b.shape if tb else b.shape[::-1]
    assert k_dim == k2
    tm, tn, tk = min(tm, m_dim), min(tn, n_dim), min(tk, k_dim)
    assert m_dim % tm == 0 and n_dim % tn == 0 and k_dim % tk == 0
    nk = k_dim // tk
    grid = (m_dim // tm, n_dim // tn, nk)
    a_spec = pl.BlockSpec((tk, tm), lambda i, j, k: (k, i)) if ta else pl.BlockSpec((tm, tk), lambda i, j, k: (i, k))
    if b_pieces:
        assert tn == b_rows and tk % b.shape[3] == 0 if tb else (tk, tn) == (b_rows, b.shape[3])
        n_bp = tk // b.shape[3] if tb else 1
        b_spec = pl.BlockSpec((n_bp,) + b.shape[1:], (lambda i, j, k: (k, 0, 0, 0)) if tb else (lambda i, j, k: (j, 0, 0, 0)))
    else:
        b_spec = pl.BlockSpec((tn, tk), lambda i, j, k: (j, k)) if tb else pl.BlockSpec((tk, tn), lambda i, j, k: (k, j))
    assert not out_pieces or (tm == m_dim and len(out_dtypes) == 1)
    dims = (((0 if ta else 1,), (1 if tb else 0,)), ((), ()))
    n_e, n_v, n_o = len(extras), len(vecs), len(out_dtypes)
    c_ins, c_in_specs, c_outs, c_out_specs, begin, end = _hosted(comm, grid)
    n_ci, n_co = len(c_ins), len(c_outs)

    def body(a_ref, b_ref, *refs):
        e_refs, v_refs, ci_refs = refs[:n_e], refs[n_e:n_e + n_v], refs[n_e + n_v:n_e + n_v + n_ci]
        o_refs = refs[n_e + n_v + n_ci:n_e + n_v + n_ci + n_o]
        co_refs = refs[n_e + n_v + n_ci + n_o:n_e + n_v + n_ci + n_o + n_co]
        scratch = refs[n_e + n_v + n_ci + n_o + n_co:]
        sems = scratch[-2:]
        begin(ci_refs, co_refs, sems)
        if b_pieces:
            pc = b.shape[3]
            part = None
            for cc in range(n_bp):
                a_val = a_ref[:, cc * pc:(cc + 1) * pc] if tb else a_ref[...]
                term = lax.dot_general(a_val.astype(BF16), b_ref[cc].reshape(b_rows, pc).astype(BF16), dims,
                                       preferred_element_type=F32)
                part = term if part is None else part + term
        else:
            part = lax.dot_general(a_ref[...].astype(BF16), b_ref[...].astype(BF16), dims, preferred_element_type=F32)

        def finish(acc):
            outs = epilogue(acc, *[r[...] for r in e_refs], *[r[...] for r in v_refs]) if epilogue else (acc,)
            for r, v in zip(o_refs, outs):
                r[...] = (v.reshape(2, tm // 2, tn) if out_pieces else v).astype(r.dtype)

        if nk == 1:
            finish(part)
        else:
            acc_ref = scratch[0]
            k = pl.program_id(2)

            @pl.when(k == 0)
            def _():
                acc_ref[...] = part

            @pl.when(k > 0)
            def _():
                acc_ref[...] += part

            @pl.when(k == nk - 1)
            def _():
                finish(acc_ref[...])
        end(ci_refs, co_refs, sems)

    tile = pl.BlockSpec((tm, tn), lambda i, j, k: (i, j))
    if out_pieces:
        out_tile, out_shape = pl.BlockSpec((None, 2, tm // 2, tn), lambda i, j, k: (j, 0, 0, 0)), (n_dim // tn, 2, tm // 2, tn)
    else:
        out_tile, out_shape = tile, (m_dim, n_dim)
    out = pl.pallas_call(
        body, name=name, grid=grid,
        in_specs=[a_spec, b_spec] + [tile] * n_e + [pl.BlockSpec((1, tn), lambda i, j, k: (0, j))] * n_v + c_in_specs,
        out_specs=[out_tile] * n_o + c_out_specs,
        out_shape=[_sds(out_shape, d) for d in out_dtypes] + c_outs,
        scratch_shapes=([pltpu.VMEM((tm, tn), F32)] if nk > 1 else []) + (comm.sem_scratch() if comm else []),
        compiler_params=_params(("arbitrary",) * 3 if comm else ("parallel", "parallel", "arbitrary")),
    )(a, b, *extras, *vecs, *c_ins)
    return out


def _swap_halves(x):
    width = x.shape[1]
    lane = lax.broadcasted_iota(jnp.int32, x.shape, 1)
    return jnp.where((lane % 64) < 32, pltpu.roll(x, width - 32, 1), pltpu.roll(x, 32, 1))


def _rope(x, cos, sin_signed):
    return x * cos + _swap_halves(x) * sin_signed


def _rope_t(d, cos, sin_signed):
    return d * cos - _swap_halves(d) * sin_signed


def _rms(x, g):
    rstd = lax.rsqrt(jnp.mean(x * x, axis=-1, keepdims=True) + RMS_EPS)
    return x * rstd * g


def _rms_bwd(x, g, dy):
    rstd = lax.rsqrt(jnp.mean(x * x, axis=-1, keepdims=True) + RMS_EPS)
    xh = x * rstd
    dyg = dy * g
    dx = rstd * (dyg - xh * jnp.mean(dyg * xh, axis=-1, keepdims=True))
    return dx, jnp.sum(dy * xh, axis=0, keepdims=True)


def _layer_norm(r, g, b):
    mu = jnp.mean(r, axis=-1, keepdims=True)
    xc = r - mu
    var = jnp.mean(xc * xc, axis=-1, keepdims=True)
    return xc * lax.rsqrt(var + LN_EPS) * g + b


def _dot(a, b, dims):
    return lax.dot_general(a, b, (dims, ((), ())), preferred_element_type=F32)


_NN, _NT, _TN = ((1,), (0,)), ((1,), (1,)), ((0,), (0,))


def _in_proj(x_bf, w_in_p, cos, sin):
    n_rows = x_bf.shape[0]
    ts = 512
    dils = [dil for _, dil in PATTERNS]
    n_p = len(dils)

    def body(x_ref, w_ref, cos_ref, sin_ref, *refs):
        view_refs, (cq_ref, kpe_ref, ckv_ref, scratch) = refs[:n_p], refs[n_p:]
        xb = x_ref[...]
        cos_v, sin_v = cos_ref[...], sin_ref[...]

        def seg(lo, hi):
            return _dot(xb, w_ref[:, lo:hi], _NN)

        def put(part, val):
            for ref, dil in zip(view_refs, dils):
                pieces = [val] if dil == 1 else _deinterleave(val, dil, scratch)
                for r, piece in enumerate(pieces):
                    lo = (3 * r + part) * A_WIDTH
                    ref[:, lo:lo + A_WIDTH] = piece.astype(BF16)

        put(0, _rope(seg(0, 512), cos_v, sin_v) * A_SCALE)
        put(1, _rope(seg(512, 1024), cos_v, sin_v))
        put(2, seg(1024, 1536))
        cq_ref[...] = seg(1536, 1792)
        kpe_ref[...] = _rope(seg(1792, 1920), cos_v[:, :LANES], sin_v[:, :LANES]).astype(BF16)
        ckv_ref[...] = seg(1920, 2048)

    row = lambda c, d=1: pl.BlockSpec((ts // d, c * d), lambda i: (i, 0))
    return pl.pallas_call(
        body, name="in_proj", grid=(n_rows // ts,),
        in_specs=[row(D_MODEL), pl.BlockSpec((D_MODEL, IN_PAD), lambda i: (0, 0)), row(A_WIDTH), row(A_WIDTH)],
        out_specs=[row(3 * A_WIDTH, d) for d in dils] + [row(Q_LORA), row(LANES), row(KV_LORA)],
        out_shape=[_sds((n_rows // d, d * 3 * A_WIDTH), BF16) for d in dils]
        + [_sds((n_rows, Q_LORA)), _sds((n_rows, LANES), BF16), _sds((n_rows, KV_LORA))],
        scratch_shapes=[pltpu.VMEM((A_WIDTH // LANES, ts, LANES), F32)],
        compiler_params=_params(("parallel",)),
    )(x_bf, w_in_p, cos, sin)


def _band_mask(m, heads):
    qi = lax.broadcasted_iota(jnp.int32, (heads * SPAN, 2 * SPAN), 0) % SPAN
    kj = lax.broadcasted_iota(jnp.int32, (heads * SPAN, 2 * SPAN), 1)
    return ((kj < SPAN) & (kj >= qi) & (m > 0)) | ((kj >= SPAN) & ((kj - SPAN) <= qi))


def _even_lanes(rows):
    return lax.broadcasted_iota(jnp.int32, (rows, LANES), 1) < A_HEAD_DIM


DIL_GROUP = 4


def _dil_fwd(qkv_view, dil, comm=None):
    nb = qkv_view.shape[0] // SPAN
    group = min(dil, DIL_GROUP)
    grid = (dil // group, nb)
    c_ins, c_in_specs, c_outs, c_out_specs, begin, end = _hosted(comm, grid)
    n_ci, n_co = len(c_ins), len(c_outs)

    def body(cur_ref, prev_ref, *refs):
        ci_refs, ol_ref, co_refs, sems = refs[:n_ci], refs[n_ci], refs[n_ci + 1:n_ci + 1 + n_co], refs[n_ci + 1 + n_co:]
        begin(ci_refs, co_refs, sems)
        mask = _band_mask(pl.program_id(1), 1)
        even, even2 = _even_lanes(SPAN), _even_lanes(2 * SPAN)
        for g in range(group):
            for p in range(A_HEADS // 2):
                q_sl, k_sl, v_sl = (slice((3 * g + t) * A_WIDTH + p * LANES, (3 * g + t) * A_WIDTH + (p + 1) * LANES)
                                    for t in range(3))
                q2 = cur_ref[:, q_sl]
                kcat = jnp.concatenate([prev_ref[:, k_sl], cur_ref[:, k_sl]], axis=0)
                vcat = jnp.concatenate([prev_ref[:, v_sl], cur_ref[:, v_sl]], axis=0)
                zero, one = jnp.zeros_like(q2), jnp.ones_like(vcat)
                res, lses = [], []
                for first in (True, False):
                    qh = jnp.where(even, q2, zero) if first else jnp.where(even, zero, q2)
                    vh = jnp.where(even2, vcat, one) if first else jnp.where(even2, one, vcat)
                    s = jnp.where(mask, _dot(qh, kcat, _NT), NEG)
                    mx = jnp.max(s, axis=-1, keepdims=True)
                    r = _dot(jnp.exp(s - mx).astype(BF16), vh, _NN)
                    den = pltpu.roll(r, A_HEAD_DIM, 1)
                    res.append(r / den)
                    lses.append(mx + jnp.log(den))
                o_lo = 2 * g * A_WIDTH + p * LANES
                ol_ref[:, o_lo:o_lo + LANES] = jnp.where(even, res[0], res[1])
                ol_ref[:, o_lo + A_WIDTH:o_lo + A_WIDTH + LANES] = jnp.where(even, lses[0], lses[1])
        end(ci_refs, co_refs, sems)

    return pl.pallas_call(
        body, name=f"dil_fwd_d{dil}", grid=grid,
        in_specs=[pl.BlockSpec((SPAN, group * 3 * A_WIDTH), lambda r, m: (m, r)),
                  pl.BlockSpec((SPAN, group * 3 * A_WIDTH), lambda r, m: (jnp.maximum(m - 1, 0), r))] + c_in_specs,
        out_specs=[pl.BlockSpec((SPAN, group * 2 * A_WIDTH), lambda r, m: (m, r))] + c_out_specs,
        out_shape=[_sds((qkv_view.shape[0], dil * 2 * A_WIDTH))] + c_outs,
        scratch_shapes=comm.sem_scratch() if comm else [],
        compiler_params=_params(("arbitrary", "arbitrary") if comm else ("parallel", "arbitrary")),
    )(qkv_view, qkv_view, *c_ins)


def _mix_fwd(ols, b_out, a_g, b_g):
    def fn(ol1, ol2, ol3, b, ag, bg):
        (o1, l1), (o2, l2), (o3, l3) = [(t[:, :A_WIDTH], t[:, A_WIDTH:]) for t in (ol1, ol2, ol3)]
        mx = jnp.maximum(jnp.maximum(l1, l2), l3)
        e1, e2, e3 = jnp.exp(l1 - mx), jnp.exp(l2 - mx), jnp.exp(l3 - mx)
        a = (e1 * o1 + e2 * o2 + e3 * o3) / (e1 + e2 + e3)
        return a, jnp.concatenate([_rms(a, ag), _rms(b, bg)], axis=1)

    n_rows = b_out.shape[0]
    return _rowwise("mix_fwd", fn, [*ols, b_out], [_sds((n_rows, A_WIDTH)), _sds((n_rows, 2 * A_WIDTH), BF16)],
                    consts=[a_g, b_g], in_dils=[dil for _, dil in PATTERNS] + [1])


def _mla_q_prep(cq, g, w_uq_p, cos, sin):
    def fn(cq_v, cos_v, sin_v, g_v, w_v):
        cqn = _rms(cq_v, g_v).astype(BF16)
        q = _dot(cqn, w_v, _NN)
        qf = [jnp.concatenate([q[:, h * QK_NOPE:(h + 1) * QK_NOPE],
                               _rope(q[:, 512 + h * LANES:512 + (h + 1) * LANES], cos_v[:, :LANES], sin_v[:, :LANES])], axis=1)
              for h in range(MLA_HEADS)]
        return cqn, qf

    n_rows = cq.shape[0]
    return _rowwise("mla_q_prep", fn, [cq, cos, sin], [_sds((n_rows, Q_LORA), BF16), _sds((MLA_HEADS, n_rows, QK_PAD), BF16)],
                    consts=[g, w_uq_p])


def _mla_kv_prep(ckv, kpe, g, w_ukv_p):
    def fn(ckv_v, kpe_v, g_v, w_v):
        ckvn = _rms(ckv_v, g_v).astype(BF16)
        kvv = _dot(ckvn, w_v, _NN)
        kf = [jnp.concatenate([kvv[:, h * QK_NOPE:(h + 1) * QK_NOPE], kpe_v.astype(F32)], axis=1) for h in range(MLA_HEADS)]
        return ckvn, kf, kvv[:, 512:]

    n_rows = ckv.shape[0]
    return _rowwise("mla_kv_prep", fn, [ckv, kpe],
                    [_sds((n_rows, KV_LORA), BF16), _sds((MLA_HEADS, n_rows, QK_PAD), BF16), _sds((n_rows, MLA_WIDTH), BF16)],
                    consts=[g, w_ukv_p])


MLA_FWD_TILES = (1024, 2048)
MLA_BWD_TILES = (1024, 1024)


def _mla_fwd(qf, kf, v, comm=None):
    n_rows = v.shape[0]
    tq, tk = min(MLA_FWD_TILES[0], n_rows), min(MLA_FWD_TILES[1], n_rows)
    nq, nk = n_rows // tq, n_rows // tk
    grid = (MLA_HEADS, nq, nk)
    exp2_scale = MLA_SCALE * math.log2(math.e)
    c_ins, c_in_specs, c_outs, c_out_specs, begin, end = _hosted(comm, grid)
    n_ci, n_co = len(c_ins), len(c_outs)

    def last_k(i):
        return jnp.right_shift(i * tq + tq - 1, int(math.log2(tk)))

    def body(q_ref, k_ref, v_ref, *refs):
        ci_refs, (o_ref, lse_ref), co_refs = refs[:n_ci], refs[n_ci:n_ci + 2], refs[n_ci + 2:n_ci + 2 + n_co]
        m_sc, l_sc, acc_sc = refs[n_ci + 2 + n_co:n_ci + 5 + n_co]
        sems = refs[n_ci + 5 + n_co:]
        i, j = pl.program_id(1), pl.program_id(2)
        begin(ci_refs, co_refs, sems)

        @pl.when(j == 0)
        def _():
            m_sc[...] = jnp.full(m_sc.shape, NEG, F32)
            l_sc[...] = jnp.zeros(l_sc.shape, F32)
            acc_sc[...] = jnp.zeros(acc_sc.shape, F32)

        def step(masked):
            s = _dot(q_ref[...], k_ref[...], _NT)
            if masked:
                row = lax.broadcasted_iota(jnp.int32, (tq, tk), 0) + i * tq
                col = lax.broadcasted_iota(jnp.int32, (tq, tk), 1) + j * tk
                s = jnp.where(col <= row, s, NEG)
            m_prev = m_sc[...]
            m_new = jnp.maximum(m_prev, jnp.max(s, axis=-1, keepdims=True))
            alpha = jnp.exp2((m_prev - m_new) * exp2_scale)
            p = jnp.exp2((s - m_new) * exp2_scale)
            l_sc[...] = alpha * l_sc[...] + jnp.sum(p, axis=-1, keepdims=True)
            acc_sc[...] = alpha * acc_sc[...] + _dot(p.astype(BF16), v_ref[...], _NN)
            m_sc[...] = m_new

        active = j * tk <= i * tq + tq - 1
        crosses = (j + 1) * tk - 1 > i * tq

        @pl.when(active & jnp.logical_not(crosses))
        def _():
            step(False)

        @pl.when(active & crosses)
        def _():
            step(True)

        @pl.when(j == last_k(i))
        def _():
            o_ref[...] = acc_sc[...] / l_sc[...]
            lse_ref[...] = jnp.broadcast_to(m_sc[...] * MLA_SCALE + jnp.log(l_sc[...]), (tq, V_DIM))

        end(ci_refs, co_refs, sems)

    return pl.pallas_call(
        body, name="mla_fwd", grid=grid,
        in_specs=[pl.BlockSpec((None, tq, QK_PAD), lambda h, i, j: (h, i, 0)),
                  pl.BlockSpec((None, tk, QK_PAD), lambda h, i, j: (h, jnp.minimum(j, last_k(i)), 0)),
                  pl.BlockSpec((tk, V_DIM), lambda h, i, j: (jnp.minimum(j, last_k(i)), h))] + c_in_specs,
        out_specs=[pl.BlockSpec((tq, V_DIM), lambda h, i, j: (i, h)), pl.BlockSpec((tq, V_DIM), lambda h, i, j: (i, h))]
        + c_out_specs,
        out_shape=[_sds((n_rows, MLA_WIDTH)), _sds((n_rows, MLA_WIDTH))] + c_outs,
        scratch_shapes=[pltpu.VMEM((tq, 1), F32), pltpu.VMEM((tq, 1), F32), pltpu.VMEM((tq, V_DIM), F32)]
        + (comm.sem_scratch() if comm else []),
        compiler_params=_params(("arbitrary",) * 3 if comm else ("parallel", "parallel", "arbitrary")),
    )(qf, kf, v, *c_ins)


def _mm_res_ln(name, a, w, xres, g, b):
    def epi(acc, xr, g_v, b_v):
        r = ALPHA * xr + acc
        y = _layer_norm(r, g_v, b_v)
        return y, y, r

    return _mm(name, a, w, (F32, BF16, F32), tm=1024, tn=D_MODEL, tk=1024, epilogue=epi, extras=[xres], vecs=[g, b])


def _mm_relu2(x_bf, w, comm=None):
    def epi(acc):
        r = jnp.maximum(acc, 0.0)
        return (r * r,)

    return _mm("ff1", x_bf, w, (BF16,), tm=1024, tn=1024, tk=D_MODEL, epilogue=epi, b_pieces=True, comm=comm)


def _loss_fn(y, t):
    def fn(y_v, t_v):
        d = y_v - t_v
        part = jnp.sum(jnp.sum(d * d, axis=1, keepdims=True), axis=0, keepdims=True)
        return d * (1.0 / D_MODEL), part

    dy, part = _rowwise("loss", fn, [y, t], [_sds(y.shape)], reds=[_sds((1, 1))])
    return part * (0.5 / D_MODEL), dy


def _ln_bwd(name, dy, r, g):
    def fn(dy_v, r_v, g_v):
        mu = jnp.mean(r_v, axis=-1, keepdims=True)
        xc = r_v - mu
        rstd = lax.rsqrt(jnp.mean(xc * xc, axis=-1, keepdims=True) + LN_EPS)
        xh = xc * rstd
        dxh = dy_v * g_v
        dr = rstd * (dxh - jnp.mean(dxh, axis=-1, keepdims=True) - xh * jnp.mean(dxh * xh, axis=-1, keepdims=True))
        return dr, dr, jnp.sum(dy_v * xh, axis=0, keepdims=True), jnp.sum(dy_v, axis=0, keepdims=True)

    return _rowwise(name, fn, [dy, r], [_sds(dy.shape), _sds(dy.shape, BF16)], consts=[g],
                    reds=[_sds((1, D_MODEL)), _sds((1, D_MODEL))])


def _head_sum_matrix():
    i = lax.broadcasted_iota(jnp.int32, (A_WIDTH, A_WIDTH), 0) // A_HEAD_DIM
    j = lax.broadcasted_iota(jnp.int32, (A_WIDTH, A_WIDTH), 1) // A_HEAD_DIM
    return (i == j).astype(BF16)


def _mix_bwd(dmixed, a_out, b_out, ols, a_g, b_g):
    def fn(dm, a, b, ol1, ol2, ol3, ag, bg, e_mat):
        l1, l2, l3 = ol1[:, A_WIDTH:], ol2[:, A_WIDTH:], ol3[:, A_WIDTH:]
        da, dga = _rms_bwd(a, ag, dm[:, :A_WIDTH])
        db, dgb = _rms_bwd(b, bg, dm[:, A_WIDTH:])
        t = da * a
        t_hi = t.astype(BF16)
        t_lo = (t - t_hi.astype(F32)).astype(BF16)
        tsum = _dot(t_hi, e_mat, _NN) + _dot(t_lo, e_mat, _NN)
        mx = jnp.maximum(jnp.maximum(l1, l2), l3)
        e1, e2, e3 = jnp.exp(l1 - mx), jnp.exp(l2 - mx), jnp.exp(l3 - mx)
        inv = 1.0 / (e1 + e2 + e3)
        w1, w2, w3 = e1 * inv, e2 * inv, e3 * inv
        tb = db * b
        delta_b = jnp.concatenate(
            [jnp.broadcast_to(jnp.sum(tb[:, h * V_DIM:(h + 1) * V_DIM], axis=-1, keepdims=True), (tb.shape[0], V_DIM))
             for h in range(MLA_HEADS)], axis=1)
        dods = [jnp.concatenate([w * da, w * tsum], axis=1) for w in (w1, w2, w3)]
        return (*dods, db, delta_b, dga, dgb)

    n_rows = a_out.shape[0]
    wide = (n_rows, A_WIDTH)
    dils = [dil for _, dil in PATTERNS]
    return _rowwise("mix_bwd", fn, [dmixed, a_out, b_out, *ols],
                    [_sds((n_rows // d, d * 2 * A_WIDTH)) for d in dils] + [_sds(wide, BF16), _sds(wide)],
                    consts=[a_g, b_g, _head_sum_matrix()], reds=[_sds((1, A_WIDTH)), _sds((1, MLA_WIDTH))], ts=256,
                    in_dils=[1, 1, 1] + dils, out_dils=dils + [1, 1])


def _dil_bwd(qkv_view, ol_view, dod_view, dil, comm=None):
    nb = qkv_view.shape[0] // SPAN
    group = min(dil, DIL_GROUP)
    grid = (dil // group, nb + 1)
    c_ins, c_in_specs, c_outs, c_out_specs, begin, end = _hosted(comm, grid)
    n_ci, n_co = len(c_ins), len(c_outs)

    def body(cur_ref, prev_ref, ol_ref, dod_ref, *refs):
        ci_refs, out_ref, co_refs = refs[:n_ci], refs[n_ci], refs[n_ci + 1:n_ci + 1 + n_co]
        carry, sems = refs[n_ci + 1 + n_co], refs[n_ci + 2 + n_co:]
        m = pl.program_id(1)
        begin(ci_refs, co_refs, sems)

        @pl.when(m == 0)
        def _():
            carry[...] = jnp.zeros(carry.shape, F32)

        @pl.when(m < nb)
        def _():
            mask, even = _band_mask(m, 2), _even_lanes(SPAN)
            for g in range(group):
                for p in range(A_HEADS // 2):
                    q_sl, k_sl, v_sl = (slice((3 * g + t) * A_WIDTH + p * LANES, (3 * g + t) * A_WIDTH + (p + 1) * LANES)
                                        for t in range(3))
                    do_sl = slice(2 * g * A_WIDTH + p * LANES, 2 * g * A_WIDTH + (p + 1) * LANES)
                    l_sl = slice((2 * g + 1) * A_WIDTH + p * LANES, (2 * g + 1) * A_WIDTH + (p + 1) * LANES)
                    q2, do2 = cur_ref[:, q_sl], dod_ref[:, do_sl].astype(BF16)
                    zero = jnp.zeros_like(q2)
                    qcat = jnp.concatenate([jnp.where(even, q2, zero), jnp.where(even, zero, q2)], axis=0)
                    docat = jnp.concatenate([jnp.where(even, do2, zero), jnp.where(even, zero, do2)], axis=0)
                    kcat = jnp.concatenate([prev_ref[:, k_sl], cur_ref[:, k_sl]], axis=0)
                    vcat = jnp.concatenate([prev_ref[:, v_sl], cur_ref[:, v_sl]], axis=0)
                    lse2, dl2 = ol_ref[:, l_sl], dod_ref[:, l_sl]
                    lse_c = jnp.concatenate([lse2[:, :1], lse2[:, A_HEAD_DIM:A_HEAD_DIM + 1]], axis=0)
                    dl_c = jnp.concatenate([dl2[:, :1], dl2[:, A_HEAD_DIM:A_HEAD_DIM + 1]], axis=0)
                    pr = jnp.exp(jnp.where(mask, _dot(qcat, kcat, _NT), NEG) - lse_c)
                    ds = (pr * (_dot(docat, vcat, _NT) - dl_c)).astype(BF16)
                    dq = _dot(ds, kcat, _NN)
                    dk2 = _dot(ds, qcat, _TN)
                    dv2 = _dot(pr.astype(BF16), docat, _TN)
                    out_ref[:, q_sl] = carry[:, q_sl]
                    out_ref[:, k_sl] = carry[:, k_sl] + dk2[:SPAN]
                    out_ref[:, v_sl] = carry[:, v_sl] + dv2[:SPAN]
                    carry[:, q_sl] = jnp.where(even, dq[:SPAN], dq[SPAN:])
                    carry[:, k_sl] = dk2[SPAN:]
                    carry[:, v_sl] = dv2[SPAN:]

        @pl.when(m == nb)
        def _():
            out_ref[...] = carry[...]

        end(ci_refs, co_refs, sems)

    last = nb - 1

    def cur(width):
        return pl.BlockSpec((SPAN, group * width), lambda r, m: (jnp.minimum(m, last), r))

    def prev(width):
        return pl.BlockSpec((SPAN, group * width), lambda r, m: (jnp.clip(m - 1, 0, last), r))

    return pl.pallas_call(
        body, name=f"dil_bwd_d{dil}", grid=grid,
        in_specs=[cur(3 * A_WIDTH), prev(3 * A_WIDTH), cur(2 * A_WIDTH), cur(2 * A_WIDTH)] + c_in_specs,
        out_specs=[prev(3 * A_WIDTH)] + c_out_specs,
        out_shape=[_sds(qkv_view.shape)] + c_outs,
        scratch_shapes=[pltpu.VMEM((SPAN, group * 3 * A_WIDTH), F32)] + (comm.sem_scratch() if comm else []),
        compiler_params=_params(("arbitrary", "arbitrary") if comm else ("parallel", "arbitrary")),
    )(qkv_view, qkv_view, ol_view, dod_view, *c_ins)


def _mla_bwd(qf, kf, v, lse, do, delta, comm=None):
    n_rows = v.shape[0]
    tq, tk = min(MLA_BWD_TILES[0], n_rows), min(MLA_BWD_TILES[1], n_rows)
    nq, nk = n_rows // tq, n_rows // tk

    def first_q(j):
        return jnp.right_shift(j * tk, int(math.log2(tq)))

    grid = (MLA_HEADS, nk, nq)
    c_ins, c_in_specs, c_outs, c_out_specs, begin, end = _hosted(comm, grid)
    n_ci, n_co = len(c_ins), len(c_outs)

    def body(q_ref, k_ref, v_ref, lse_ref, do_ref, dl_ref, *refs):
        ci_refs, (dq_ref, dk_ref, dv_ref), co_refs = refs[:n_ci], refs[n_ci:n_ci + 3], refs[n_ci + 3:n_ci + 3 + n_co]
        dk_sc, dv_sc = refs[n_ci + 3 + n_co:n_ci + 5 + n_co]
        sems = refs[n_ci + 5 + n_co:]
        j, i = pl.program_id(1), pl.program_id(2)
        begin(ci_refs, co_refs, sems)

        @pl.when((j == 0) & (i == 0))
        def _():
            dq_ref[...] = jnp.zeros(dq_ref.shape, F32)

        @pl.when(i == first_q(j))
        def _():
            dk_sc[...] = jnp.zeros(dk_sc.shape, F32)
            dv_sc[...] = jnp.zeros(dv_sc.shape, F32)

        def step(masked):
            q, k, dob = q_ref[...], k_ref[...], do_ref[...]
            s = _dot(q, k, _NT) * MLA_SCALE
            if masked:
                row = lax.broadcasted_iota(jnp.int32, (tq, tk), 0) + i * tq
                col = lax.broadcasted_iota(jnp.int32, (tq, tk), 1) + j * tk
                s = jnp.where(col <= row, s, NEG)
            p = jnp.exp(s - lse_ref[:, :1])
            dp = _dot(dob, v_ref[...], _NT)
            ds = (p * (dp - dl_ref[:, :1]) * MLA_SCALE).astype(BF16)
            dv_sc[...] += _dot(p.astype(BF16), dob, _TN)
            dk_sc[...] += _dot(ds, q, _TN)
            rows = pl.ds(pl.multiple_of(i * tq, tq), tq)
            dq_ref[rows, :] += _dot(ds, k, _NN)

        active = i >= first_q(j)
        crosses = (j + 1) * tk - 1 > i * tq

        @pl.when(active & jnp.logical_not(crosses))
        def _():
            step(False)

        @pl.when(active & crosses)
        def _():
            step(True)

        @pl.when(i == nq - 1)
        def _():
            dk_ref[...] = dk_sc[...]
            dv_ref[...] = dv_sc[...]

        end(ci_refs, co_refs, sems)

    qrow = lambda h, j, i: (jnp.maximum(i, first_q(j)), h)
    return pl.pallas_call(
        body, name="mla_bwd", grid=grid,
        in_specs=[pl.BlockSpec((None, tq, QK_PAD), lambda h, j, i: (h, jnp.maximum(i, first_q(j)), 0)),
                  pl.BlockSpec((None, tk, QK_PAD), lambda h, j, i: (h, j, 0)),
                  pl.BlockSpec((tk, V_DIM), lambda h, j, i: (j, h)),
                  pl.BlockSpec((tq, V_DIM), qrow), pl.BlockSpec((tq, V_DIM), qrow), pl.BlockSpec((tq, V_DIM), qrow)]
        + c_in_specs,
        out_specs=[pl.BlockSpec((None, n_rows, QK_PAD), lambda h, j, i: (h, 0, 0)),
                   pl.BlockSpec((None, tk, QK_PAD), lambda h, j, i: (h, j, 0)),
                   pl.BlockSpec((tk, V_DIM), lambda h, j, i: (j, h))] + c_out_specs,
        out_shape=[_sds((MLA_HEADS, n_rows, QK_PAD)), _sds((MLA_HEADS, n_rows, QK_PAD)), _sds((n_rows, MLA_WIDTH))] + c_outs,
        scratch_shapes=[pltpu.VMEM((tk, QK_PAD), F32), pltpu.VMEM((tk, V_DIM), F32)] + (comm.sem_scratch() if comm else []),
        compiler_params=_params(("arbitrary",) * 3 if comm else ("parallel", "arbitrary", "arbitrary")),
    )(qf, kf, v, lse, do, delta, *c_ins)


def _mla_bwd_prep(dqf, dkf, dv, cos, sin):
    def fn(dqf_v, dkf_v, dv_v, cos_v, sin_v):
        dq = jnp.concatenate([dqf_v[h][:, :QK_NOPE] for h in range(MLA_HEADS)]
                             + [_rope_t(dqf_v[h][:, QK_NOPE:], cos_v[:, :LANES], sin_v[:, :LANES]) for h in range(MLA_HEADS)], axis=1)
        dkv = jnp.concatenate([dkf_v[h][:, :QK_NOPE] for h in range(MLA_HEADS)] + [dv_v], axis=1)
        dkpe = dkf_v[0][:, QK_NOPE:] + dkf_v[1][:, QK_NOPE:] + dkf_v[2][:, QK_NOPE:] + dkf_v[3][:, QK_NOPE:]
        return dq, dkv, dkpe

    n_rows = dv.shape[0]
    return _rowwise("mla_bwd_prep", fn, [dqf, dkf, dv, cos, sin],
                    [_sds((n_rows, UQ_PAD), BF16), _sds((n_rows, 1024), BF16), _sds((n_rows, LANES))], ts=256)


def _assemble_dh(dqkvs, dcqn, cq, dckvn, ckv, dkpe, cos, sin, gq, gkv):
    def fn(g1, g2, g3, dcqn_v, cq_v, dckvn_v, ckv_v, dkpe_v, cos_v, sin_v, gq_v, gkv_v):
        g = g1 + g2 + g3
        dqa = _rope_t(g[:, :A_WIDTH], cos_v, sin_v) * A_SCALE
        dka = _rope_t(g[:, A_WIDTH:2 * A_WIDTH], cos_v, sin_v)
        dcq, dgq = _rms_bwd(cq_v, gq_v, dcqn_v)
        dckv, dgkv = _rms_bwd(ckv_v, gkv_v, dckvn_v)
        dkr = _rope_t(dkpe_v, cos_v[:, :LANES], sin_v[:, :LANES])
        return jnp.concatenate([dqa, dka, g[:, 2 * A_WIDTH:], dcq, dkr, dckv], axis=1), dgq, dgkv

    n_rows = cq.shape[0]
    return _rowwise("assemble_dh", fn, [*dqkvs, dcqn, cq, dckvn, ckv, dkpe, cos, sin],
                    [_sds((n_rows, IN_PAD), BF16)], consts=[gq, gkv],
                    reds=[_sds((1, Q_LORA)), _sds((1, KV_LORA))], ts=256,
                    in_dils=[dil for _, dil in PATTERNS] + [1] * 7)


def _layer_fwd(x_f32, x_bf, w, sm, cos, sin, behind=(None, None), rest=None):
    *qkv_views, cq, kpe, ckv = _in_proj(x_bf, w["w_in"], cos, sin)
    ol_views, brought = [], []
    for i, (view, (_, dil)) in enumerate(zip(qkv_views, PATTERNS)):
        ol, *got = _dil_fwd(view, dil, comm=rest[0][i] if rest else None)
        ol_views.append(ol)
        brought.append(got)
    if rest:
        w = {**w, **rest[1](brought)}
    cqn, qf = _mla_q_prep(cq, sm["q_a_norm"], w["w_uq"], cos, sin)
    ckvn, kf, v = _mla_kv_prep(ckv, kpe, sm["kv_a_norm"], w["w_ukv"])
    b_out, b_lse, *behind_attn = _mla_fwd(qf, kf, v, comm=behind[0])
    a_out, mixed = _mix_fwd(ol_views, b_out, sm["a_out_norm"], sm["b_out_norm"])
    x1, x1_bf, r1 = _mm_res_ln("wo_ln1", mixed, w["w_o"], x_f32, sm["ln1_g"], sm["ln1_b"])
    f, *behind_ff1 = _mm_relu2(x1_bf, w["w_ff1"], comm=behind[1])
    x2, x2_bf, r2 = _mm_res_ln("ff2_ln2", f, w["w_ff2"], x1, sm["ln2_g"], sm["ln2_b"])
    saved = dict(w=w, x_bf=x_bf, qkv_views=qkv_views, ol_views=ol_views, cq=cq, ckv=ckv, cqn=cqn, qf=qf, ckvn=ckvn,
                 kf=kf, v=v, b_out=b_out, b_lse=b_lse, a_out=a_out, mixed=mixed, x1_bf=x1_bf, r1=r1, f=f, r2=r2)
    return x2, x2_bf, saved, (behind_attn, behind_ff1)


def _layer_bwd(dx2, w, sm, sv, cos, sin, reduce_of, pending=None):
    dr2, dr2_bf, dg2, db2 = _ln_bwd("ln2_bwd", dx2, sv["r2"], sm["ln2_g"])
    (dw_ff2,) = _mm("dw_ff2", sv["f"], dr2_bf, (F32,), ta=True, tm=1024, tn=1024, tk=2048)
    du, *swapped = _mm("d_u", dr2_bf, w["w_ff2"], (BF16,), tb=True, tm=1024, tn=2048, tk=D_MODEL,
                       epilogue=lambda acc, f: (acc * (2.0 * jnp.sqrt(f.astype(F32))),), extras=[sv["f"]],
                       comm=_swap_comm(pending.pieces) if pending else None)
    if pending:
        pending.after_swap(swapped)
    (dw_ff1,) = _mm("dw_ff1", sv["x1_bf"], du, (F32,), ta=True, tm=1024, tn=1024, tk=2048, out_pieces=True)
    (dy1,) = _mm("d_x1", du, w["w_ff1"], (F32,), tb=True, tm=1024, tn=1024, tk=2048,
                 epilogue=lambda acc, d: (acc + ALPHA * d,), extras=[dr2], b_pieces=True)
    dr1, dr1_bf, dg1, db1 = _ln_bwd("ln1_bwd", dy1, sv["r1"], sm["ln1_g"])
    (dw_o,) = _mm("dw_o", sv["mixed"], dr1_bf, (F32,), ta=True, tm=1024, tn=1024, tk=2048)
    early = reduce_of(dict(w_o=dw_o, w_ff1=dw_ff1, w_ff2=dw_ff2))
    (dmixed,) = _mm("d_mixed", dr1_bf, w["w_o"], (F32,), tb=True, tm=1024, tn=1024, tk=D_MODEL)
    dod1, dod2, dod3, do_b, delta_b, dga, dgb = _mix_bwd(
        dmixed, sv["a_out"], sv["b_out"], sv["ol_views"], sm["a_out_norm"], sm["b_out_norm"])
    dqkvs = []
    for i, ((_, dil), qkv_view, ol_view, dod_view) in enumerate(zip(PATTERNS, sv["qkv_views"], sv["ol_views"], (dod1, dod2, dod3))):
        swap_here = early is not None and i == 0
        dqkv, *swapped = _dil_bwd(qkv_view, ol_view, dod_view, dil, comm=_swap_comm(early.pieces) if swap_here else None)
        dqkvs.append(dqkv)
        if swap_here:
            early.after_swap(swapped)
    to_scatter = (early.s16 if early else []) + (pending.s16 if pending else [])
    dqf, dkf, dv_b, *scattered = _mla_bwd(sv["qf"], sv["kf"], sv["v"], sv["b_lse"], do_b, delta_b,
                                          comm=_scatter_comm(to_scatter) if to_scatter else None)
    if early:
        early.after_scatter(scattered[:len(early.s16)])
    if pending:
        pending.after_scatter(scattered[len(scattered) - len(pending.s16):])
    dq_b, dkv_b, dkpe = _mla_bwd_prep(dqf, dkf, dv_b, cos, sin)
    (dw_uq,) = _mm("dw_uq", sv["cqn"], dq_b, (F32,), ta=True, tm=Q_LORA, tn=1024, tk=2048)
    (dcqn,) = _mm("d_cqn", dq_b, w["w_uq"], (F32,), tb=True, tm=1024, tn=Q_LORA, tk=UQ_PAD)
    (dw_ukv,) = _mm("dw_ukv", sv["ckvn"], dkv_b, (F32,), ta=True, tm=KV_LORA, tn=1024, tk=2048)
    (dckvn,) = _mm("d_ckvn", dkv_b, w["w_ukv"], (F32,), tb=True, tm=1024, tn=KV_LORA, tk=1024)
    dh, dgq, dgkv = _assemble_dh(dqkvs, dcqn, sv["cq"], dckvn, sv["ckv"], dkpe, cos, sin,
                                 sm["q_a_norm"], sm["kv_a_norm"])
    (dw_in,) = _mm("dw_in", sv["x_bf"], dh, (F32,), ta=True, tm=1024, tn=1024, tk=2048)
    (dx,) = _mm("d_x", dh, w["w_in"], (F32,), tb=True, tm=1024, tn=1024, tk=2048,
                epilogue=lambda acc, d: (acc + ALPHA * d,), extras=[dr1])
    late = reduce_of(dict(w_in=dw_in, w_uq=dw_uq, w_ukv=dw_ukv))
    dsm = dict(q_a_norm=dgq, kv_a_norm=dgkv, a_out_norm=dga, b_out_norm=dgb, ln1_g=dg1, ln1_b=db1, ln2_g=dg2, ln2_b=db2)
    return dx, dsm, late


def _pad_w_in(w):
    return jnp.concatenate([w[:, :1792], w[:, 1920:1984], jnp.zeros((w.shape[0], 64), w.dtype), w[:, 1792:1920]], axis=1)


def _unpad_w_in(w):
    return jnp.concatenate([w[:, :1792], w[:, 1920:2048], w[:, 1792:1856]], axis=1)


def _rope_tables(n_rows):
    half = A_HEAD_DIM // 2
    inv_freq = ROPE_THETA ** (-jnp.arange(half, dtype=F32) / half)
    ang = jnp.arange(n_rows, dtype=F32)[:, None] * inv_freq[None, :]
    cos = jnp.tile(jnp.cos(ang), (1, 2 * A_HEADS))
    sin = jnp.tile(jnp.concatenate([-jnp.sin(ang), jnp.sin(ang)], axis=1), (1, A_HEADS))
    return cos, sin


W_NAMES = ("w_in", "w_uq", "w_ukv", "w_o", "w_ff1", "w_ff2")
PIECE = dict(w_in=(512, 512), w_uq=(128, 256), w_ukv=(64, 256), w_o=(128, 1024), w_ff1=(512, 1024), w_ff2=(512, 1024))
SHARD_COLS = dict(w_in=496, w_uq=192, w_ukv=256, w_o=1024, w_ff1=1024, w_ff2=1024)


def _col_pieces(t, pad_to):
    rows, _, cols = t.shape
    t = jnp.pad(t, ((0, 0), (0, 0), (0, pad_to - cols)))
    return t.reshape(2, rows // 2, N_CHIPS, pad_to).transpose(2, 0, 1, 3)


def _grad_pieces(dws):
    def heads(t, rows):
        return jnp.concatenate([t[:, :512].reshape(rows, MLA_HEADS, LANES), t[:, 512:].reshape(rows, MLA_HEADS, LANES)], axis=2)

    make = dict(
        w_in=lambda t: _col_pieces(_unpad_w_in(t).reshape(D_MODEL, N_CHIPS, SHARD_COLS["w_in"]), 512),
        w_uq=lambda t: _col_pieces(heads(t, Q_LORA), 256),
        w_ukv=lambda t: _col_pieces(heads(t, KV_LORA), 256),
        w_o=lambda t: t.reshape(N_CHIPS, 2, 128, D_MODEL),
        w_ff1=lambda t: t,
        w_ff2=lambda t: t.reshape(N_CHIPS, 2, 512, D_MODEL))
    return {n: make[n](t) for n, t in dws.items()}


def _weights_from_pieces(p):
    def cols(t):
        return t.transpose(1, 2, 0, 3).reshape(2 * t.shape[2], N_CHIPS, t.shape[3])

    def heads(t, rows):
        return jnp.concatenate([t[:, :, :LANES].reshape(rows, 512), t[:, :, LANES:].reshape(rows, 512)], axis=1)

    make = dict(
        w_in=lambda t: _pad_w_in(cols(t)[:, :, :SHARD_COLS["w_in"]].reshape(D_MODEL, IN_COLS)),
        w_uq=lambda t: heads(cols(t), Q_LORA),
        w_ukv=lambda t: heads(cols(t), KV_LORA),
        w_o=lambda t: t.reshape(D_MODEL, D_MODEL),
        w_ff1=lambda t: t,
        w_ff2=lambda t: t.reshape(D_FF, D_MODEL))
    return {n: make[n](t) for n, t in p.items()}


def _my_piece(name, shard, half):
    rows, cols = shard.shape[0] // 2, shard.shape[1]
    t = lax.dynamic_slice_in_dim(shard, half * rows, rows, axis=0).astype(BF16)
    return jnp.pad(t, ((0, 0), (0, PIECE[name][1] - cols)))


MESH = pl.DeviceIdType.MESH
ANY = pl.BlockSpec(memory_space=pl.ANY)


class _Comm:
    def __init__(self, ins, out_shapes, n_sems, start, finish):
        self.ins, self.out_shapes, self.n_sems, self.start, self.finish = list(ins), list(out_shapes), n_sems, start, finish

    def sem_scratch(self):
        return [pltpu.SemaphoreType.DMA((self.n_sems,)), pltpu.SemaphoreType.DMA((self.n_sems,))]


def _comm_call(name, comm):
    n_in, n_out = len(comm.ins), len(comm.out_shapes)

    def body(*refs):
        ins, outs, sems = refs[:n_in], refs[n_in:n_in + n_out], refs[n_in + n_out:]
        comm.start(ins, outs, *sems)
        comm.finish(ins, outs, *sems)

    return pl.pallas_call(body, name=name, out_shape=comm.out_shapes, in_specs=[ANY] * n_in, out_specs=[ANY] * n_out,
                          scratch_shapes=comm.sem_scratch())(*comm.ins)


def _all_gather_comm(blocks):
    n = len(blocks)

    def plan(x_refs, out_refs, send_sems, recv_sems):
        x, y, c = lax.axis_index("x"), lax.axis_index("y"), lax.axis_index("c")
        me, sibling = (x, y, c), (x, y, 1 - c)
        chips = [(1 - x, y), (x, 1 - y), (1 - x, 1 - y)]

        def copy(t, k, blk, to, src=None):
            px, py, pc = blk
            slot = out_refs[t].at[4 * px + 2 * py + pc]
            return pltpu.make_async_remote_copy(
                src_ref=slot if src is None else src, dst_ref=slot,
                send_sem=send_sems.at[7 * t + k], recv_sem=recv_sems.at[7 * t + k], device_id=to, device_id_type=MESH)

        first = []
        for t in range(n):
            first.append(copy(t, 0, me, sibling, src=x_refs[t]))
            first += [copy(t, 1 + j, me, (*chip, c), src=x_refs[t]) for j, chip in enumerate(chips)]
        return me, sibling, chips, c, copy, first

    def start(x_refs, out_refs, send_sems, recv_sems):
        for cp in plan(x_refs, out_refs, send_sems, recv_sems)[-1]:
            cp.start()

    def finish(x_refs, out_refs, send_sems, recv_sems):
        me, sibling, chips, c, copy, sent = plan(x_refs, out_refs, send_sems, recv_sems)
        for j, chip in enumerate(chips):
            for t in range(n):
                copy(t, 1 + j, (*chip, c), me).wait_recv()
                sent.append(copy(t, 4 + j, (*chip, c), sibling))
                sent[-1].start()
        for t in range(n):
            copy(t, 0, sibling, me).wait_recv()
        for j, chip in enumerate(chips):
            for t in range(n):
                copy(t, 4 + j, (*chip, 1 - c), me).wait_recv()
        for cp in sent:
            cp.wait_send()

    return _Comm(blocks, [_sds((N_DEV,) + b.shape, b.dtype) for b in blocks], 7 * n, start, finish)


def _simple_comm(ins, out_shapes, n_sems, copies):
    def start(in_refs, out_refs, send_sems, recv_sems):
        for cp in copies(in_refs, out_refs, send_sems, recv_sems):
            cp.start()

    def finish(in_refs, out_refs, send_sems, recv_sems):
        for cp in copies(in_refs, out_refs, send_sems, recv_sems):
            cp.wait()

    return _Comm(ins, out_shapes, n_sems, start, finish)


def _swap_comm(gs):
    def copies(g_refs, got_refs, send_sems, recv_sems):
        c = lax.axis_index("c")
        sibling = (lax.axis_index("x"), lax.axis_index("y"), 1 - c)
        return [pltpu.make_async_remote_copy(src_ref=g_refs[t].at[k, 1 - c], dst_ref=got_refs[t].at[k],
                                             send_sem=send_sems.at[N_CHIPS * t + k], recv_sem=recv_sems.at[N_CHIPS * t + k],
                                             device_id=sibling, device_id_type=MESH)
                for t in range(len(gs)) for k in range(N_CHIPS)]

    return _simple_comm(gs, [_sds((N_CHIPS,) + g.shape[2:], g.dtype) for g in gs], N_CHIPS * len(gs), copies)


def _scatter_comm(hs):
    def copies(h_refs, got_refs, send_sems, recv_sems):
        x, y, c = lax.axis_index("x"), lax.axis_index("y"), lax.axis_index("c")
        chips = [(1 - x, y), (x, 1 - y), (1 - x, 1 - y)]
        return [pltpu.make_async_remote_copy(src_ref=h_refs[t].at[2 * px + py], dst_ref=got_refs[t].at[rel],
                                             send_sem=send_sems.at[3 * t + rel], recv_sem=recv_sems.at[3 * t + rel],
                                             device_id=(px, py, c), device_id_type=MESH)
                for t in range(len(hs)) for rel, (px, py) in enumerate(chips)]

    return _simple_comm(hs, [_sds((3,) + h.shape[1:], h.dtype) for h in hs], 3 * len(hs), copies)


def _share_comm(rs):
    def copies(r_refs, got_refs, send_sems, recv_sems):
        sibling = (lax.axis_index("x"), lax.axis_index("y"), 1 - lax.axis_index("c"))
        return [pltpu.make_async_remote_copy(src_ref=r_refs[t], dst_ref=got_refs[t], send_sem=send_sems.at[t],
                                             recv_sem=recv_sems.at[t], device_id=sibling, device_id_type=MESH)
                for t in range(len(rs))]

    return _simple_comm(rs, [_sds(r.shape, r.dtype) for r in rs], len(rs), copies)


SUM_STEPS = 4


def _pair_sums(name, gs, gots, c_arr):
    n = len(gs)
    dims = [(g.shape[2] // SUM_STEPS, g.shape[3]) for g in gs]

    def body(c_ref, *refs):
        for t in range(n):
            s = refs[t][...] + refs[n + t][...]
            refs[2 * n + t][...] = s
            refs[3 * n + t][...] = s.astype(BF16)

    def part(tr, pc):
        return pl.BlockSpec((None, tr, pc), lambda k, i, c_ref: (k, i, 0))

    def kept(tr, pc):
        return pl.BlockSpec((None, None, tr, pc), lambda k, i, c_ref: (k, c_ref[0], i, 0))

    outs = pl.pallas_call(
        body, name=name,
        out_shape=[_sds((N_CHIPS,) + g.shape[2:]) for g in gs] + [_sds((N_CHIPS,) + g.shape[2:], BF16) for g in gs],
        grid_spec=pltpu.PrefetchScalarGridSpec(
            num_scalar_prefetch=1, grid=(N_CHIPS, SUM_STEPS),
            in_specs=[kept(*d) for d in dims] + [part(*d) for d in dims],
            out_specs=[part(*d) for d in dims] * 2),
        compiler_params=_params(("parallel", "parallel")),
    )(c_arr, *gs, *gots)
    return list(outs[:n]), list(outs[n:])


def _chips_sums(name, h32s, gots, chip_arr):
    n = len(h32s)
    dims = [(h.shape[1] // SUM_STEPS, h.shape[2]) for h in h32s]

    def body(chip_ref, *refs):
        for t in range(n):
            got_ref = refs[n + t]
            refs[2 * n + t][...] = refs[t][...] + got_ref[0] + got_ref[1] + got_ref[2]

    return pl.pallas_call(
        body, name=name, out_shape=[_sds(h.shape[1:]) for h in h32s],
        grid_spec=pltpu.PrefetchScalarGridSpec(
            num_scalar_prefetch=1, grid=(SUM_STEPS,),
            in_specs=[pl.BlockSpec((None, tr, pc), lambda i, chip_ref: (chip_ref[0], i, 0)) for tr, pc in dims]
            + [pl.BlockSpec((3, tr, pc), lambda i, chip_ref: (0, i, 0)) for tr, pc in dims],
            out_specs=[pl.BlockSpec((tr, pc), lambda i, chip_ref: (i, 0)) for tr, pc in dims]),
        compiler_params=_params(("parallel",)),
    )(chip_arr, *h32s, *gots)


class _Reduce:
    def __init__(self, tag, dws, c, chip):
        gp = _grad_pieces(dws)
        self.tag, self.c, self.names, self.pieces = tag, c, list(gp), list(gp.values())
        self.c_arr, self.chip_arr = jnp.reshape(c, (1,)).astype(jnp.int32), jnp.reshape(chip, (1,)).astype(jnp.int32)
        self.shards = None

    def after_swap(self, gots):
        self.s32, self.s16 = _pair_sums(f"rs_pair_sum_{self.tag}", self.pieces, gots, self.c_arr)

    def after_scatter(self, parts):
        mine = _chips_sums(f"rs_chips_sum_{self.tag}", self.s32, parts, self.chip_arr)
        theirs = _comm_call(f"rs_share_{self.tag}", _share_comm(mine))
        self.shards = {}
        for n, a, b in zip(self.names, mine, theirs):
            both = jnp.where(self.c == 0, jnp.concatenate([a, b], axis=0), jnp.concatenate([b, a], axis=0))
            self.shards[n] = both[:, :SHARD_COLS[n]]

    def run_alone(self):
        self.after_swap(_comm_call(f"rs_pair_{self.tag}", _swap_comm(self.pieces)))
        self.after_scatter(_comm_call(f"rs_chips_{self.tag}", _scatter_comm(self.s16)))


class _LayerWeights:
    def __init__(self, shards, c, dev):
        self.shards, self.c, self.dev = shards, c, dev

    def comm(self, l, names=W_NAMES):
        return _all_gather_comm([_my_piece(n, self.shards[l][n], self.c) for n in names])

    def take(self, l, gathered, names=W_NAMES):
        pieces = {}
        for n, g in zip(names, gathered):
            mine = _my_piece(n, self.shards[l][n], self.c)
            pieces[n] = lax.dynamic_update_index_in_dim(g, mine, self.dev, 0).reshape((N_CHIPS, 2) + mine.shape)
        return _weights_from_pieces(pieces)

    def first(self):
        now, later = ("w_in", "w_uq", "w_ukv"), ("w_ff1", "w_ff2", "w_o")
        weights = self.take(0, _comm_call("ag_l0", self.comm(0, now)), now)
        comms = [self.comm(0, (n,)) for n in later]

        def take_rest(results):
            rest = {}
            for n, gathered in zip(later, results):
                rest.update(self.take(0, gathered, (n,)))
            return rest

        return weights, (comms, take_rest)


def _all_reduce_small(vec):
    rows, lanes = vec.shape

    def gather_body(x_ref, out_ref, send_sems, recv_sems):
        x, y, c = lax.axis_index("x"), lax.axis_index("y"), lax.axis_index("c")
        me, sibling = (x, y, c), (x, y, 1 - c)
        chips = [(1 - x, y), (x, 1 - y), (1 - x, 1 - y)]

        def slot(px, py, pc):
            return out_ref.at[4 * px + 2 * py + pc]

        def copy(k, blk, to, src=None):
            return pltpu.make_async_remote_copy(
                src_ref=slot(*blk) if src is None else src, dst_ref=slot(*blk),
                send_sem=send_sems.at[k], recv_sem=recv_sems.at[k], device_id=to, device_id_type=MESH)

        out_ref[4 * x + 2 * y + c] = x_ref[...]
        first = [copy(0, me, sibling, src=x_ref)]
        first += [copy(1 + j, me, (*chip, c), src=x_ref) for j, chip in enumerate(chips)]
        for cp in first:
            cp.start()
        passed = [copy(4 + j, (*chip, c), sibling) for j, chip in enumerate(chips)]
        for j, chip in enumerate(chips):
            copy(1 + j, (*chip, c), me).wait_recv()
            passed[j].start()
        copy(0, sibling, me).wait_recv()
        for j, chip in enumerate(chips):
            copy(4 + j, (*chip, 1 - c), me).wait_recv()
        for cp in first + passed:
            cp.wait_send()

    vmem = pl.BlockSpec(memory_space=pltpu.VMEM)
    allv = pl.pallas_call(
        gather_body, name="small_all_gather", out_shape=_sds((N_DEV, rows, lanes)),
        in_specs=[vmem], out_specs=vmem,
        scratch_shapes=[pltpu.SemaphoreType.DMA((7,)), pltpu.SemaphoreType.DMA((7,))],
    )(vec)

    def sum_body(a_ref, o_ref):
        acc = a_ref[0]
        for d in range(1, N_DEV):
            acc = acc + a_ref[d]
        o_ref[...] = acc

    return pl.pallas_call(sum_body, name="small_sum", out_shape=_sds((rows, lanes)), in_specs=[vmem], out_specs=vmem)(allv)


def _adamw(name, w, g, m, v, ts=512):
    def fn(w_v, g_v, m_v, v_v):
        m_n = ADAM_B1 * m_v + (1.0 - ADAM_B1) * g_v
        v_n = ADAM_B2 * v_v + (1.0 - ADAM_B2) * (g_v * g_v)
        m_hat = m_n / (1.0 - ADAM_B1 ** ADAM_STEP)
        v_hat = v_n / (1.0 - ADAM_B2 ** ADAM_STEP)
        delta = -ADAM_LR * (m_hat / (jnp.sqrt(v_hat) + ADAM_EPS) + ADAM_WD * w_v)
        return delta, m_n, v_n

    return _rowwise(name, fn, [w, g, m, v], [_sds(w.shape)] * 3, ts=ts)


def _pack_small(per_layer):
    flat = jnp.concatenate([per_layer[l][n].reshape(-1) for l in range(DEPTH) for n in SMALL_NAMES])
    return jnp.pad(flat, (0, SMALL_ROWS * LANES - flat.shape[0])).reshape(SMALL_ROWS, LANES)


def _unpack_small(packed):
    flat = packed.reshape(-1)
    per = sum(SMALL_SIZES)
    out = {}
    for n, size, off in zip(SMALL_NAMES, SMALL_SIZES, [sum(SMALL_SIZES[:i]) for i in range(len(SMALL_SIZES))]):
        out[n] = jnp.stack([flat[l * per + off:l * per + off + size] for l in range(DEPTH)])
    return out


NEXT_WEIGHTS_BEHIND = (("w_ff1", "w_ff2"), ("w_in", "w_uq", "w_ukv", "w_o"))


def _fwd_bwd(x, target, layer_weights, smalls, on_layer_grads):
    depth = len(smalls)
    cos, sin = _rope_tables(x.shape[0])
    h_f32, h_bf = x, x.astype(BF16)
    saved = []
    w, rest = layer_weights.first()
    for l in range(depth):
        more = l + 1 < depth
        behind = [layer_weights.comm(l + 1, names) if more else None for names in NEXT_WEIGHTS_BEHIND]
        h_f32, h_bf, sv, brought = _layer_fwd(h_f32, h_bf, w, smalls[l], cos, sin, behind, rest)
        saved.append(sv)
        if more:
            w, rest = {}, None
            for names, got in zip(NEXT_WEIGHTS_BEHIND, brought):
                w.update(layer_weights.take(l + 1, got, names))
    loss_part, dh = _loss_fn(h_f32, target)
    small_grads, pending = [None] * depth, None
    for l in reversed(range(depth)):
        dh, small_grads[l], pending = _layer_bwd(dh, saved[l]["w"], smalls[l], saved[l], cos, sin,
                                                 lambda dws, l=l: on_layer_grads(l, dws), pending)
    return loss_part, dh, small_grads, pending


def kernel(x, w_in, q_a_norm, kv_a_norm, w_uq, w_ukv, a_out_norm, b_out_norm, w_o, ln1_g, ln1_b, w_ff1, w_ff2, ln2_g, ln2_b, loss_target, m_w_in, m_q_a_norm, m_kv_a_norm, m_w_uq, m_w_ukv, m_a_out_norm, m_b_out_norm, m_w_o, m_ln1_g, m_ln1_b, m_w_ff1, m_w_ff2, m_ln2_g, m_ln2_b, v_w_in, v_q_a_norm, v_kv_a_norm, v_w_uq, v_w_ukv, v_a_out_norm, v_b_out_norm, v_w_o, v_ln1_g, v_ln1_b, v_w_ff1, v_w_ff2, v_ln2_g, v_ln2_b):
    c = lax.axis_index("c")
    chip = 2 * lax.axis_index("x") + lax.axis_index("y")
    big = dict(w_in=w_in, w_uq=w_uq, w_ukv=w_ukv, w_o=w_o, w_ff1=w_ff1, w_ff2=w_ff2)
    big_m = dict(w_in=m_w_in, w_uq=m_w_uq, w_ukv=m_w_ukv, w_o=m_w_o, w_ff1=m_w_ff1, w_ff2=m_w_ff2)
    big_v = dict(w_in=v_w_in, w_uq=v_w_uq, w_ukv=v_w_ukv, w_o=v_w_o, w_ff1=v_w_ff1, w_ff2=v_w_ff2)
    small = dict(q_a_norm=q_a_norm, kv_a_norm=kv_a_norm, a_out_norm=a_out_norm, b_out_norm=b_out_norm,
                 ln1_g=ln1_g, ln1_b=ln1_b, ln2_g=ln2_g, ln2_b=ln2_b)
    small_m = dict(q_a_norm=m_q_a_norm, kv_a_norm=m_kv_a_norm, a_out_norm=m_a_out_norm, b_out_norm=m_b_out_norm,
                   ln1_g=m_ln1_g, ln1_b=m_ln1_b, ln2_g=m_ln2_g, ln2_b=m_ln2_b)
    small_v = dict(q_a_norm=v_q_a_norm, kv_a_norm=v_kv_a_norm, a_out_norm=v_a_out_norm, b_out_norm=v_b_out_norm,
                   ln1_g=v_ln1_g, ln1_b=v_ln1_b, ln2_g=v_ln2_g, ln2_b=v_ln2_b)

    layer_weights = _LayerWeights([{n: big[n][l] for n in W_NAMES} for l in range(DEPTH)], c, 2 * chip + c)
    smalls = [{n: small[n][l][None, :] for n in SMALL_NAMES} for l in range(DEPTH)]
    reductions = [[] for _ in range(DEPTH)]

    def reduce_layer(l, dws):
        reductions[l].append(_Reduce(f"l{l}_{len(reductions[l])}", dws, c, chip))
        return reductions[l][-1]

    loss_part, grad_x, small_grads, last = _fwd_bwd(x[0], loss_target[0], layer_weights, smalls, reduce_layer)
    last.run_alone()
    shard_grads = [{n: g for r in rs for n, g in r.shards.items()} for rs in reductions]
    loss = lax.psum(loss_part[0, 0], ("x", "y", "c"))
    grad_x = grad_x[None]

    g_small_packed = _all_reduce_small(_pack_small(small_grads))
    pack_in = lambda d: _pack_small([{n: d[n][l] for n in SMALL_NAMES} for l in range(DEPTH)])
    sd, sm_, sv_ = _adamw("adamw_small", pack_in(small), g_small_packed, pack_in(small_m), pack_in(small_v), ts=SMALL_ROWS)
    g_small, d_small, m_small, v_small = (_unpack_small(t) for t in (g_small_packed, sd, sm_, sv_))

    g_big, d_big, m_big, v_big = {}, {}, {}, {}
    for n in W_NAMES:
        g = jnp.stack([shard_grads[l][n] for l in range(DEPTH)])
        shape = g.shape
        flat = lambda t: t.reshape(shape[0] * shape[1], shape[2])
        d_, m_, v_ = _adamw(f"adamw_{n}", flat(big[n]), flat(g), flat(big_m[n]), flat(big_v[n]))
        g_big[n], d_big[n], m_big[n], v_big[n] = g, d_.reshape(shape), m_.reshape(shape), v_.reshape(shape)

    order = ("w_in", "q_a_norm", "kv_a_norm", "w_uq", "w_ukv", "a_out_norm", "b_out_norm", "w_o", "ln1_g", "ln1_b",
             "w_ff1", "w_ff2", "ln2_g", "ln2_b")
    pick = lambda bigd, smalld: [bigd[n] if n in bigd else smalld[n] for n in order]
    return (loss, grad_x, *pick(g_big, g_small), *pick(d_big, d_small), *pick(m_big, m_small), *pick(v_big, v_small))
```

```python
import math

import jax
import jax.numpy as jnp
from jax import lax
from jax.experimental import pallas as pl
from jax.experimental.pallas import tpu as pltpu

F32, BF16 = jnp.float32, jnp.bfloat16

D_MODEL = 1024
DEPTH = 4
A_HEAD_DIM = 64
A_HEADS = 8
A_WIDTH = 512
PATTERNS = ((128, 1), (512, 4), (2048, 16))
SPAN = 128
MLA_HEADS = 4
QK_NOPE = 128
QK_ROPE = 64
V_DIM = 128
Q_LORA = 256
KV_LORA = 128
MLA_WIDTH = 512
QK_PAD = 256
IN_COLS = 1984
IN_PAD = 2048
UQ_PAD = 1024
D_FF = 4096
ROPE_THETA = 10000.0
ALPHA = (2.0 * DEPTH) ** 0.25
LN_EPS = 1e-5
RMS_EPS = 1e-6
MLA_SCALE = (QK_NOPE + QK_ROPE) ** -0.5
A_SCALE = A_HEAD_DIM ** -0.5

ADAM_LR, ADAM_B1, ADAM_B2, ADAM_EPS, ADAM_WD, ADAM_STEP = 0.001, 0.9, 0.999, 1e-08, 0.01, 10

VMEM_LIMIT_BYTES = 56 * 1024 * 1024
NEG = -1e30
N_CHIPS, N_DEV = 4, 8
LANES = 128

SMALL_NAMES = ("q_a_norm", "kv_a_norm", "a_out_norm", "b_out_norm", "ln1_g", "ln1_b", "ln2_g", "ln2_b")
SMALL_SIZES = (256, 128, 512, 512, 1024, 1024, 1024, 1024)
SMALL_ROWS = 176


def _params(sem):
    return pltpu.CompilerParams(dimension_semantics=sem, vmem_limit_bytes=VMEM_LIMIT_BYTES)


FAST_STRIDE = 4


def _interleave(parts, scratch):
    d, (n, width) = len(parts), parts[0].shape
    if d > FAST_STRIDE and d % FAST_STRIDE == 0:
        inner = [_interleave(parts[r::FAST_STRIDE], scratch) for r in range(FAST_STRIDE)]
        return _interleave(inner, scratch)
    for r in range(d):
        for s in range(width // LANES):
            scratch.at[s][pl.ds(r, n, stride=d), :] = parts[r][:, s * LANES:(s + 1) * LANES]
    return jnp.concatenate([scratch.at[s][pl.ds(0, n * d), :] for s in range(width // LANES)], axis=1)


def _deinterleave(x, d, scratch):
    n, width = x.shape[0] // d, x.shape[1]
    if d > FAST_STRIDE and d % FAST_STRIDE == 0:
        outer = _deinterleave(x, FAST_STRIDE, scratch)
        inner = [_deinterleave(y, d // FAST_STRIDE, scratch) for y in outer]
        return [inner[r % FAST_STRIDE][r // FAST_STRIDE] for r in range(d)]
    for s in range(width // LANES):
        scratch.at[s][pl.ds(0, n * d), :] = x[:, s * LANES:(s + 1) * LANES]
    return [jnp.concatenate([scratch.at[s][pl.ds(r, n, stride=d), :] for s in range(width // LANES)], axis=1) for r in range(d)]


def _rowwise(name, fn, ins, outs, consts=(), reds=(), ts=512, in_dils=None, out_dils=None, comm=None):
    in_dils = list(in_dils or [1] * len(ins))
    out_dils = list(out_dils or [1] * len(outs))
    n_rows = ins[0].shape[-2] * in_dils[0]
    ts = min(ts, n_rows)
    assert n_rows % ts == 0
    n_in, n_c, n_o, n_r = len(ins), len(consts), len(outs), len(reds)
    viewed = [(a.shape, d) for a, d in zip(list(ins) + list(outs), in_dils + out_dils) if d > 1]
    grid = (n_rows // ts,)
    h_ins, h_in_specs, h_outs, h_out_specs, begin, end = _hosted(comm, grid)
    n_hi, n_ho = len(h_ins), len(h_outs)

    def tile_spec(shape, d=1):
        if len(shape) == 2:
            return pl.BlockSpec((ts // d, shape[1]), lambda i: (i, 0))
        return pl.BlockSpec((shape[0], ts, shape[2]), lambda i: (0, i, 0))

    def whole_spec(shape):
        return pl.BlockSpec(shape, lambda i: (0,) * len(shape))

    def body(*refs):
        in_refs, c_refs, hi_refs = refs[:n_in], refs[n_in:n_in + n_c], refs[n_in + n_c:n_in + n_c + n_hi]
        refs = refs[n_in + n_c + n_hi:]
        o_refs, r_refs, ho_refs = refs[:n_o], refs[n_o:n_o + n_r], refs[n_o + n_r:n_o + n_r + n_ho]
        scratches = list(refs[n_o + n_r + n_ho:])
        sems = [scratches.pop(), scratches.pop()][::-1] if comm else []
        begin(hi_refs, ho_refs, sems)
        vals = []
        for r, d in zip(in_refs, in_dils):
            if d == 1:
                vals.append(r[...])
            else:
                width = r.shape[1] // d
                vals.append(_interleave([r[:, k * width:(k + 1) * width] for k in range(d)], scratches.pop(0)))
        res = fn(*vals, *[r[...] for r in c_refs])
        res = tuple(res) if isinstance(res, (tuple, list)) else (res,)
        for r, v, d in zip(o_refs, res[:n_o], out_dils):
            if len(r.shape) == 3:
                for g in range(r.shape[0]):
                    r[g] = v[g].astype(r.dtype)
            elif d == 1:
                r[...] = v.astype(r.dtype)
            else:
                width = r.shape[1] // d
                for k, part in enumerate(_deinterleave(v, d, scratches.pop(0))):
                    r[:, k * width:(k + 1) * width] = part.astype(r.dtype)
        if n_r:
            i = pl.program_id(0)

            @pl.when(i == 0)
            def _():
                for r, v in zip(r_refs, res[n_o:]):
                    r[...] = v

            @pl.when(i > 0)
            def _():
                for r, v in zip(r_refs, res[n_o:]):
                    r[...] += v
        end(hi_refs, ho_refs, sems)

    out = pl.pallas_call(
        body, name=name, grid=grid,
        in_specs=[tile_spec(a.shape, d) for a, d in zip(ins, in_dils)] + [whole_spec(c.shape) for c in consts] + h_in_specs,
        out_specs=[tile_spec(o.shape, d) for o, d in zip(outs, out_dils)] + [whole_spec(r.shape) for r in reds] + h_out_specs,
        out_shape=list(outs) + list(reds) + h_outs,
        scratch_shapes=[pltpu.VMEM((shape[1] // d // LANES, ts, LANES), F32) for shape, d in viewed]
        + (comm.sem_scratch() if comm else []),
        compiler_params=_params(("arbitrary",)),
    )(*ins, *consts, *h_ins)
    return out


def _sds(shape, dtype=F32):
    return jax.ShapeDtypeStruct(tuple(shape), dtype)


def _hosted(comm, grid):
    if comm is None:
        return [], [], [], [], (lambda *a: None), (lambda *a: None)

    def edge(which, at, ins, outs, sems):
        ids = [pl.program_id(d) for d in range(len(grid))]
        hit = ids[0] == at[0]
        for i, v in zip(ids[1:], at[1:]):
            hit = hit & (i == v)

        @pl.when(hit)
        def _():
            which(ins, outs, *sems)

    begin = lambda ins, outs, sems: edge(comm.start, [0] * len(grid), ins, outs, sems)
    end = lambda ins, outs, sems: edge(comm.finish, [g - 1 for g in grid], ins, outs, sems)
    return comm.ins, [ANY] * len(comm.ins), comm.out_shapes, [ANY] * len(comm.out_shapes), begin, end


def _mm(name, a, b, out_dtypes, *, ta=False, tb=False, tm=512, tn=512, tk=512, epilogue=None, extras=(), vecs=(),
        comm=None, b_pieces=False, out_pieces=False):
    (k_dim, m_dim) = a.shape if ta else a.shape[::-1]
    if b_pieces:
        b_rows, b_cols = 2 * b.shape[2], N_CHIPS * b.shape[3]
        (n_dim, k2) = (b_rows, b_cols) if tb else (b_cols, b_rows)
    else:
        (n_dim, k2) = b.shape if tb else b.shape[::-1]
    assert k_dim == k2
    tm, tn, tk = min(tm, m_dim), min(tn, n_dim), min(tk, k_dim)
    assert m_dim % tm == 0 and n_dim % tn == 0 and k_dim % tk == 0
    nk = k_dim // tk
    grid = (m_dim // tm, n_dim // tn, nk)
    a_spec = pl.BlockSpec((tk, tm), lambda i, j, k: (k, i)) if ta else pl.BlockSpec((tm, tk), lambda i, j, k: (i, k))
    if b_pieces:
        assert tn == b_rows and tk % b.shape[3] == 0 if tb else (tk, tn) == (b_rows, b.shape[3])
        n_bp = tk // b.shape[3] if tb else 1
        b_spec = pl.BlockSpec((n_bp,) + b.shape[1:], (lambda i, j, k: (k, 0, 0, 0)) if tb else (lambda i, j, k: (j, 0, 0, 0)))
    else:
        b_spec = pl.BlockSpec((tn, tk), lambda i, j, k: (j, k)) if tb else pl.BlockSpec((tk, tn), lambda i, j, k: (k, j))
    assert not out_pieces or (tm == m_dim and len(out_dtypes) == 1)
    dims = (((0 if ta else 1,), (1 if tb else 0,)), ((), ()))
    n_e, n_v, n_o = len(extras), len(vecs), len(out_dtypes)
    c_ins, c_in_specs, c_outs, c_out_specs, begin, end = _hosted(comm, grid)
    n_ci, n_co = len(c_ins), len(c_outs)

    def body(a_ref, b_ref, *refs):
        e_refs, v_refs, ci_refs = refs[:n_e], refs[n_e:n_e + n_v], refs[n_e + n_v:n_e + n_v + n_ci]
        o_refs = refs[n_e + n_v + n_ci:n_e + n_v + n_ci + n_o]
        co_refs = refs[n_e + n_v + n_ci + n_o:n_e + n_v + n_ci + n_o + n_co]
        scratch = refs[n_e + n_v + n_ci + n_o + n_co:]
        sems = scratch[-2:]
        begin(ci_refs, co_refs, sems)
        if b_pieces:
            pc = b.shape[3]
            part = None
            for cc in range(n_bp):
                a_val = a_ref[:, cc * pc:(cc + 1) * pc] if tb else a_ref[...]
                term = lax.dot_general(a_val.astype(BF16), b_ref[cc].reshape(b_rows, pc).astype(BF16), dims,
                                       preferred_element_type=F32)
                part = term if part is None else part + term
        else:
            part = lax.dot_general(a_ref[...].astype(BF16), b_ref[...].astype(BF16), dims, preferred_element_type=F32)

        def finish(acc):
            outs = epilogue(acc, *[r[...] for r in e_refs], *[r[...] for r in v_refs]) if epilogue else (acc,)
            for r, v in zip(o_refs, outs):
                r[...] = (v.reshape(2, tm // 2, tn) if out_pieces else v).astype(r.dtype)

        if nk == 1:
            finish(part)
        else:
            acc_ref = scratch[0]
            k = pl.program_id(2)

            @pl.when(k == 0)
            def _():
                acc_ref[...] = part

            @pl.when(k > 0)
            def _():
                acc_ref[...] += part

            @pl.when(k == nk - 1)
            def _():
                finish(acc_ref[...])
        end(ci_refs, co_refs, sems)

    tile = pl.BlockSpec((tm, tn), lambda i, j, k: (i, j))
    if out_pieces:
        out_tile, out_shape = pl.BlockSpec((None, 2, tm // 2, tn), lambda i, j, k: (j, 0, 0, 0)), (n_dim // tn, 2, tm // 2, tn)
    else:
        out_tile, out_shape = tile, (m_dim, n_dim)
    out = pl.pallas_call(
        body, name=name, grid=grid,
        in_specs=[a_spec, b_spec] + [tile] * n_e + [pl.BlockSpec((1, tn), lambda i, j, k: (0, j))] * n_v + c_in_specs,
        out_specs=[out_tile] * n_o + c_out_specs,
        out_shape=[_sds(out_shape, d) for d in out_dtypes] + c_outs,
        scratch_shapes=([pltpu.VMEM((tm, tn), F32)] if nk > 1 else []) + (comm.sem_scratch() if comm else []),
        compiler_params=_params(("arbitrary",) * 3 if comm else ("parallel", "parallel", "arbitrary")),
    )(a, b, *extras, *vecs, *c_ins)
    return out


def _swap_halves(x):
    width = x.shape[1]
    lane = lax.broadcasted_iota(jnp.int32, x.shape, 1)
    return jnp.where((lane % 64) < 32, pltpu.roll(x, width - 32, 1), pltpu.roll(x, 32, 1))


def _rope(x, cos, sin_signed):
    return x * cos + _swap_halves(x) * sin_signed


def _rope_t(d, cos, sin_signed):
    return d * cos - _swap_halves(d) * sin_signed


def _rms(x, g):
    rstd = lax.rsqrt(jnp.mean(x * x, axis=-1, keepdims=True) + RMS_EPS)
    return x * rstd * g


def _rms_bwd(x, g, dy):
    rstd = lax.rsqrt(jnp.mean(x * x, axis=-1, keepdims=True) + RMS_EPS)
    xh = x * rstd
    dyg = dy * g
    dx = rstd * (dyg - xh * jnp.mean(dyg * xh, axis=-1, keepdims=True))
    return dx, jnp.sum(dy * xh, axis=0, keepdims=True)


def _layer_norm(r, g, b):
    mu = jnp.mean(r, axis=-1, keepdims=True)
    xc = r - mu
    var = jnp.mean(xc * xc, axis=-1, keepdims=True)
    return xc * lax.rsqrt(var + LN_EPS) * g + b


def _dot(a, b, dims):
    return lax.dot_general(a, b, (dims, ((), ())), preferred_element_type=F32)


_NN, _NT, _TN = ((1,), (0,)), ((1,), (1,)), ((0,), (0,))


def _in_proj(x_bf, w_in_p, cos, sin):
    n_rows = x_bf.shape[0]
    ts = 512
    dils = [dil for _, dil in PATTERNS]
    n_p = len(dils)

    def body(x_ref, w_ref, cos_ref, sin_ref, *refs):
        view_refs, (cq_ref, kpe_ref, ckv_ref, scratch) = refs[:n_p], refs[n_p:]
        xb = x_ref[...]
        cos_v, sin_v = cos_ref[...], sin_ref[...]

        def seg(lo, hi):
            return _dot(xb, w_ref[:, lo:hi], _NN)

        def put(part, val):
            for ref, dil in zip(view_refs, dils):
                pieces = [val] if dil == 1 else _deinterleave(val, dil, scratch)
                for r, piece in enumerate(pieces):
                    lo = (3 * r + part) * A_WIDTH
                    ref[:, lo:lo + A_WIDTH] = piece.astype(BF16)

        put(0, _rope(seg(0, 512), cos_v, sin_v) * A_SCALE)
        put(1, _rope(seg(512, 1024), cos_v, sin_v))
        put(2, seg(1024, 1536))
        cq_ref[...] = seg(1536, 1792)
        kpe_ref[...] = _rope(seg(1792, 1920), cos_v[:, :LANES], sin_v[:, :LANES]).astype(BF16)
        ckv_ref[...] = seg(1920, 2048)

    row = lambda c, d=1: pl.BlockSpec((ts // d, c * d), lambda i: (i, 0))
    return pl.pallas_call(
        body, name="in_proj", grid=(n_rows // ts,),
        in_specs=[row(D_MODEL), pl.BlockSpec((D_MODEL, IN_PAD), lambda i: (0, 0)), row(A_WIDTH), row(A_WIDTH)],
        out_specs=[row(3 * A_WIDTH, d) for d in dils] + [row(Q_LORA), row(LANES), row(KV_LORA)],
        out_shape=[_sds((n_rows // d, d * 3 * A_WIDTH), BF16) for d in dils]
        + [_sds((n_rows, Q_LORA)), _sds((n_rows, LANES), BF16), _sds((n_rows, KV_LORA))],
        scratch_shapes=[pltpu.VMEM((A_WIDTH // LANES, ts, LANES), F32)],
        compiler_params=_params(("parallel",)),
    )(x_bf, w_in_p, cos, sin)


def _band_mask(m, heads):
    qi = lax.broadcasted_iota(jnp.int32, (heads * SPAN, 2 * SPAN), 0) % SPAN
    kj = lax.broadcasted_iota(jnp.int32, (heads * SPAN, 2 * SPAN), 1)
    return ((kj < SPAN) & (kj >= qi) & (m > 0)) | ((kj >= SPAN) & ((kj - SPAN) <= qi))


def _even_lanes(rows):
    return lax.broadcasted_iota(jnp.int32, (rows, LANES), 1) < A_HEAD_DIM


DIL_GROUP = 4


def _dil_fwd(qkv_view, dil, comm=None):
    nb = qkv_view.shape[0] // SPAN
    group = min(dil, DIL_GROUP)
    grid = (dil // group, nb)
    c_ins, c_in_specs, c_outs, c_out_specs, begin, end = _hosted(comm, grid)
    n_ci, n_co = len(c_ins), len(c_outs)

    def body(cur_ref, prev_ref, *refs):
        ci_refs, ol_ref, co_refs, sems = refs[:n_ci], refs[n_ci], refs[n_ci + 1:n_ci + 1 + n_co], refs[n_ci + 1 + n_co:]
        begin(ci_refs, co_refs, sems)
        mask = _band_mask(pl.program_id(1), 1)
        even, even2 = _even_lanes(SPAN), _even_lanes(2 * SPAN)
        for g in range(group):
            for p in range(A_HEADS // 2):
                q_sl, k_sl, v_sl = (slice((3 * g + t) * A_WIDTH + p * LANES, (3 * g + t) * A_WIDTH + (p + 1) * LANES)
                                    for t in range(3))
                q2 = cur_ref[:, q_sl]
                kcat = jnp.concatenate([prev_ref[:, k_sl], cur_ref[:, k_sl]], axis=0)
                vcat = jnp.concatenate([prev_ref[:, v_sl], cur_ref[:, v_sl]], axis=0)
                zero, one = jnp.zeros_like(q2), jnp.ones_like(vcat)
                res, lses = [], []
                for first in (True, False):
                    qh = jnp.where(even, q2, zero) if first else jnp.where(even, zero, q2)
                    vh = jnp.where(even2, vcat, one) if first else jnp.where(even2, one, vcat)
                    s = jnp.where(mask, _dot(qh, kcat, _NT), NEG)
                    mx = jnp.max(s, axis=-1, keepdims=True)
                    r = _dot(jnp.exp(s - mx).astype(BF16), vh, _NN)
                    den = pltpu.roll(r, A_HEAD_DIM, 1)
                    res.append(r / den)
                    lses.append(mx + jnp.log(den))
                o_lo = 2 * g * A_WIDTH + p * LANES
                ol_ref[:, o_lo:o_lo + LANES] = jnp.where(even, res[0], res[1])
                ol_ref[:, o_lo + A_WIDTH:o_lo + A_WIDTH + LANES] = jnp.where(even, lses[0], lses[1])
        end(ci_refs, co_refs, sems)

    return pl.pallas_call(
        body, name=f"dil_fwd_d{dil}", grid=grid,
        in_specs=[pl.BlockSpec((SPAN, group * 3 * A_WIDTH), lambda r, m: (m, r)),
                  pl.BlockSpec((SPAN, group * 3 * A_WIDTH), lambda r, m: (jnp.maximum(m - 1, 0), r))] + c_in_specs,
        out_specs=[pl.BlockSpec((SPAN, group * 2 * A_WIDTH), lambda r, m: (m, r))] + c_out_specs,
        out_shape=[_sds((qkv_view.shape[0], dil * 2 * A_WIDTH))] + c_outs,
        scratch_shapes=comm.sem_scratch() if comm else [],
        compiler_params=_params(("arbitrary", "arbitrary") if comm else ("parallel", "arbitrary")),
    )(qkv_view, qkv_view, *c_ins)


def _mix_fwd(ols, b_out, a_g, b_g):
    def fn(ol1, ol2, ol3, b, ag, bg):
        (o1, l1), (o2, l2), (o3, l3) = [(t[:, :A_WIDTH], t[:, A_WIDTH:]) for t in (ol1, ol2, ol3)]
        mx = jnp.maximum(jnp.maximum(l1, l2), l3)
        e1, e2, e3 = jnp.exp(l1 - mx), jnp.exp(l2 - mx), jnp.exp(l3 - mx)
        a = (e1 * o1 + e2 * o2 + e3 * o3) / (e1 + e2 + e3)
        return a, jnp.concatenate([_rms(a, ag), _rms(b, bg)], axis=1)

    n_rows = b_out.shape[0]
    return _rowwise("mix_fwd", fn, [*ols, b_out], [_sds((n_rows, A_WIDTH)), _sds((n_rows, 2 * A_WIDTH), BF16)],
                    consts=[a_g, b_g], in_dils=[dil for _, dil in PATTERNS] + [1])


def _mla_q_prep(cq, g, w_uq_p, cos, sin):
    def fn(cq_v, cos_v, sin_v, g_v, w_v):
        cqn = _rms(cq_v, g_v).astype(BF16)
        q = _dot(cqn, w_v, _NN)
        qf = [jnp.concatenate([q[:, h * QK_NOPE:(h + 1) * QK_NOPE],
                               _rope(q[:, 512 + h * LANES:512 + (h + 1) * LANES], cos_v[:, :LANES], sin_v[:, :LANES])], axis=1)
              for h in range(MLA_HEADS)]
        return cqn, qf

    n_rows = cq.shape[0]
    return _rowwise("mla_q_prep", fn, [cq, cos, sin], [_sds((n_rows, Q_LORA), BF16), _sds((MLA_HEADS, n_rows, QK_PAD), BF16)],
                    consts=[g, w_uq_p])


def _mla_kv_prep(ckv, kpe, g, w_ukv_p):
    def fn(ckv_v, kpe_v, g_v, w_v):
        ckvn = _rms(ckv_v, g_v).astype(BF16)
        kvv = _dot(ckvn, w_v, _NN)
        kf = [jnp.concatenate([kvv[:, h * QK_NOPE:(h + 1) * QK_NOPE], kpe_v.astype(F32)], axis=1) for h in range(MLA_HEADS)]
        return ckvn, kf, kvv[:, 512:]

    n_rows = ckv.shape[0]
    return _rowwise("mla_kv_prep", fn, [ckv, kpe],
                    [_sds((n_rows, KV_LORA), BF16), _sds((MLA_HEADS, n_rows, QK_PAD), BF16), _sds((n_rows, MLA_WIDTH), BF16)],
                    consts=[g, w_ukv_p])


MLA_FWD_TILES = (1024, 2048)
MLA_BWD_TILES = (1024, 1024)


def _mla_fwd(qf, kf, v, comm=None):
    n_rows = v.shape[0]
    tq, tk = min(MLA_FWD_TILES[0], n_rows), min(MLA_FWD_TILES[1], n_rows)
    nq, nk = n_rows // tq, n_rows // tk
    grid = (MLA_HEADS, nq, nk)
    exp2_scale = MLA_SCALE * math.log2(math.e)
    c_ins, c_in_specs, c_outs, c_out_specs, begin, end = _hosted(comm, grid)
    n_ci, n_co = len(c_ins), len(c_outs)

    def last_k(i):
        return jnp.right_shift(i * tq + tq - 1, int(math.log2(tk)))

    def body(q_ref, k_ref, v_ref, *refs):
        ci_refs, (o_ref, lse_ref), co_refs = refs[:n_ci], refs[n_ci:n_ci + 2], refs[n_ci + 2:n_ci + 2 + n_co]
        m_sc, l_sc, acc_sc = refs[n_ci + 2 + n_co:n_ci + 5 + n_co]
        sems = refs[n_ci + 5 + n_co:]
        i, j = pl.program_id(1), pl.program_id(2)
        begin(ci_refs, co_refs, sems)

        @pl.when(j == 0)
        def _():
            m_sc[...] = jnp.full(m_sc.shape, NEG, F32)
            l_sc[...] = jnp.zeros(l_sc.shape, F32)
            acc_sc[...] = jnp.zeros(acc_sc.shape, F32)

        def step(masked):
            s = _dot(q_ref[...], k_ref[...], _NT)
            if masked:
                row = lax.broadcasted_iota(jnp.int32, (tq, tk), 0) + i * tq
                col = lax.broadcasted_iota(jnp.int32, (tq, tk), 1) + j * tk
                s = jnp.where(col <= row, s, NEG)
            m_prev = m_sc[...]
            m_new = jnp.maximum(m_prev, jnp.max(s, axis=-1, keepdims=True))
            alpha = jnp.exp2((m_prev - m_new) * exp2_scale)
            p = jnp.exp2((s - m_new) * exp2_scale)
            l_sc[...] = alpha * l_sc[...] + jnp.sum(p, axis=-1, keepdims=True)
            acc_sc[...] = alpha * acc_sc[...] + _dot(p.astype(BF16), v_ref[...], _NN)
            m_sc[...] = m_new

        active = j * tk <= i * tq + tq - 1
        crosses = (j + 1) * tk - 1 > i * tq

        @pl.when(active & jnp.logical_not(crosses))
        def _():
            step(False)

        @pl.when(active & crosses)
        def _():
            step(True)

        @pl.when(j == last_k(i))
        def _():
            o_ref[...] = acc_sc[...] / l_sc[...]
            lse_ref[...] = jnp.broadcast_to(m_sc[...] * MLA_SCALE + jnp.log(l_sc[...]), (tq, V_DIM))

        end(ci_refs, co_refs, sems)

    return pl.pallas_call(
        body, name="mla_fwd", grid=grid,
        in_specs=[pl.BlockSpec((None, tq, QK_PAD), lambda h, i, j: (h, i, 0)),
                  pl.BlockSpec((None, tk, QK_PAD), lambda h, i, j: (h, jnp.minimum(j, last_k(i)), 0)),
                  pl.BlockSpec((tk, V_DIM), lambda h, i, j: (jnp.minimum(j, last_k(i)), h))] + c_in_specs,
        out_specs=[pl.BlockSpec((tq, V_DIM), lambda h, i, j: (i, h)), pl.BlockSpec((tq, V_DIM), lambda h, i, j: (i, h))]
        + c_out_specs,
        out_shape=[_sds((n_rows, MLA_WIDTH)), _sds((n_rows, MLA_WIDTH))] + c_outs,
        scratch_shapes=[pltpu.VMEM((tq, 1), F32), pltpu.VMEM((tq, 1), F32), pltpu.VMEM((tq, V_DIM), F32)]
        + (comm.sem_scratch() if comm else []),
        compiler_params=_params(("arbitrary",) * 3 if comm else ("parallel", "parallel", "arbitrary")),
    )(qf, kf, v, *c_ins)


def _mm_res_ln(name, a, w, xres, g, b):
    def epi(acc, xr, g_v, b_v):
        r = ALPHA * xr + acc
        y = _layer_norm(r, g_v, b_v)
        return y, y, r

    return _mm(name, a, w, (F32, BF16, F32), tm=1024, tn=D_MODEL, tk=1024, epilogue=epi, extras=[xres], vecs=[g, b])


def _mm_relu2(x_bf, w, comm=None):
    def epi(acc):
        r = jnp.maximum(acc, 0.0)
        return (r * r,)

    return _mm("ff1", x_bf, w, (BF16,), tm=1024, tn=1024, tk=D_MODEL, epilogue=epi, b_pieces=True, comm=comm)


def _loss_fn(y, t):
    def fn(y_v, t_v):
        d = y_v - t_v
        part = jnp.sum(jnp.sum(d * d, axis=1, keepdims=True), axis=0, keepdims=True)
        return d * (1.0 / D_MODEL), part

    dy, part = _rowwise("loss", fn, [y, t], [_sds(y.shape)], reds=[_sds((1, 1))])
    return part * (0.5 / D_MODEL), dy


def _ln_bwd(name, dy, r, g):
    def fn(dy_v, r_v, g_v):
        mu = jnp.mean(r_v, axis=-1, keepdims=True)
        xc = r_v - mu
        rstd = lax.rsqrt(jnp.mean(xc * xc, axis=-1, keepdims=True) + LN_EPS)
        xh = xc * rstd
        dxh = dy_v * g_v
        dr = rstd * (dxh - jnp.mean(dxh, axis=-1, keepdims=True) - xh * jnp.mean(dxh * xh, axis=-1, keepdims=True))
        return dr, dr, jnp.sum(dy_v * xh, axis=0, keepdims=True), jnp.sum(dy_v, axis=0, keepdims=True)

    return _rowwise(name, fn, [dy, r], [_sds(dy.shape), _sds(dy.shape, BF16)], consts=[g],
                    reds=[_sds((1, D_MODEL)), _sds((1, D_MODEL))])


def _head_sum_matrix():
    i = lax.broadcasted_iota(jnp.int32, (A_WIDTH, A_WIDTH), 0) // A_HEAD_DIM
    j = lax.broadcasted_iota(jnp.int32, (A_WIDTH, A_WIDTH), 1) // A_HEAD_DIM
    return (i == j).astype(BF16)


def _mix_bwd(dmixed, a_out, b_out, ols, a_g, b_g):
    def fn(dm, a, b, ol1, ol2, ol3, ag, bg, e_mat):
        l1, l2, l3 = ol1[:, A_WIDTH:], ol2[:, A_WIDTH:], ol3[:, A_WIDTH:]
        da, dga = _rms_bwd(a, ag, dm[:, :A_WIDTH])
        db, dgb = _rms_bwd(b, bg, dm[:, A_WIDTH:])
        t = da * a
        t_hi = t.astype(BF16)
        t_lo = (t - t_hi.astype(F32)).astype(BF16)
        tsum = _dot(t_hi, e_mat, _NN) + _dot(t_lo, e_mat, _NN)
        mx = jnp.maximum(jnp.maximum(l1, l2), l3)
        e1, e2, e3 = jnp.exp(l1 - mx), jnp.exp(l2 - mx), jnp.exp(l3 - mx)
        inv = 1.0 / (e1 + e2 + e3)
        w1, w2, w3 = e1 * inv, e2 * inv, e3 * inv
        tb = db * b
        delta_b = jnp.concatenate(
            [jnp.broadcast_to(jnp.sum(tb[:, h * V_DIM:(h + 1) * V_DIM], axis=-1, keepdims=True), (tb.shape[0], V_DIM))
             for h in range(MLA_HEADS)], axis=1)
        dods = [jnp.concatenate([w * da, w * tsum], axis=1) for w in (w1, w2, w3)]
        return (*dods, db, delta_b, dga, dgb)

    n_rows = a_out.shape[0]
    wide = (n_rows, A_WIDTH)
    dils = [dil for _, dil in PATTERNS]
    return _rowwise("mix_bwd", fn, [dmixed, a_out, b_out, *ols],
                    [_sds((n_rows // d, d * 2 * A_WIDTH)) for d in dils] + [_sds(wide, BF16), _sds(wide)],
                    consts=[a_g, b_g, _head_sum_matrix()], reds=[_sds((1, A_WIDTH)), _sds((1, MLA_WIDTH))], ts=256,
                    in_dils=[1, 1, 1] + dils, out_dils=dils + [1, 1])


def _dil_bwd(qkv_view, ol_view, dod_view, dil, comm=None):
    nb = qkv_view.shape[0] // SPAN
    group = min(dil, DIL_GROUP)
    grid = (dil // group, nb + 1)
    c_ins, c_in_specs, c_outs, c_out_specs, begin, end = _hosted(comm, grid)
    n_ci, n_co = len(c_ins), len(c_outs)

    def body(cur_ref, prev_ref, ol_ref, dod_ref, *refs):
        ci_refs, out_ref, co_refs = refs[:n_ci], refs[n_ci], refs[n_ci + 1:n_ci + 1 + n_co]
        carry, sems = refs[n_ci + 1 + n_co], refs[n_ci + 2 + n_co:]
        m = pl.program_id(1)
        begin(ci_refs, co_refs, sems)

        @pl.when(m == 0)
        def _():
            carry[...] = jnp.zeros(carry.shape, F32)

        @pl.when(m < nb)
        def _():
            mask, even = _band_mask(m, 2), _even_lanes(SPAN)
            for g in range(group):
                for p in range(A_HEADS // 2):
                    q_sl, k_sl, v_sl = (slice((3 * g + t) * A_WIDTH + p * LANES, (3 * g + t) * A_WIDTH + (p + 1) * LANES)
                                        for t in range(3))
                    do_sl = slice(2 * g * A_WIDTH + p * LANES, 2 * g * A_WIDTH + (p + 1) * LANES)
                    l_sl = slice((2 * g + 1) * A_WIDTH + p * LANES, (2 * g + 1) * A_WIDTH + (p + 1) * LANES)
                    q2, do2 = cur_ref[:, q_sl], dod_ref[:, do_sl].astype(BF16)
                    zero = jnp.zeros_like(q2)
                    qcat = jnp.concatenate([jnp.where(even, q2, zero), jnp.where(even, zero, q2)], axis=0)
                    docat = jnp.concatenate([jnp.where(even, do2, zero), jnp.where(even, zero, do2)], axis=0)
                    kcat = jnp.concatenate([prev_ref[:, k_sl], cur_ref[:, k_sl]], axis=0)
                    vcat = jnp.concatenate([prev_ref[:, v_sl], cur_ref[:, v_sl]], axis=0)
                    lse2, dl2 = ol_ref[:, l_sl], dod_ref[:, l_sl]
                    lse_c = jnp.concatenate([lse2[:, :1], lse2[:, A_HEAD_DIM:A_HEAD_DIM + 1]], axis=0)
                    dl_c = jnp.concatenate([dl2[:, :1], dl2[:, A_HEAD_DIM:A_HEAD_DIM + 1]], axis=0)
                    pr = jnp.exp(jnp.where(mask, _dot(qcat, kcat, _NT), NEG) - lse_c)
                    ds = (pr * (_dot(docat, vcat, _NT) - dl_c)).astype(BF16)
                    dq = _dot(ds, kcat, _NN)
                    dk2 = _dot(ds, qcat, _TN)
                    dv2 = _dot(pr.astype(BF16), docat, _TN)
                    out_ref[:, q_sl] = carry[:, q_sl]
                    out_ref[:, k_sl] = carry[:, k_sl] + dk2[:SPAN]
                    out_ref[:, v_sl] = carry[:, v_sl] + dv2[:SPAN]
                    carry[:, q_sl] = jnp.where(even, dq[:SPAN], dq[SPAN:])
                    carry[:, k_sl] = dk2[SPAN:]
                    carry[:, v_sl] = dv2[SPAN:]

        @pl.when(m == nb)
        def _():
            out_ref[...] = carry[...]

        end(ci_refs, co_refs, sems)

    last = nb - 1

    def cur(width):
        return pl.BlockSpec((SPAN, group * width), lambda r, m: (jnp.minimum(m, last), r))

    def prev(width):
        return pl.BlockSpec((SPAN, group * width), lambda r, m: (jnp.clip(m - 1, 0, last), r))

    return pl.pallas_call(
        body, name=f"dil_bwd_d{dil}", grid=grid,
        in_specs=[cur(3 * A_WIDTH), prev(3 * A_WIDTH), cur(2 * A_WIDTH), cur(2 * A_WIDTH)] + c_in_specs,
        out_specs=[prev(3 * A_WIDTH)] + c_out_specs,
        out_shape=[_sds(qkv_view.shape)] + c_outs,
        scratch_shapes=[pltpu.VMEM((SPAN, group * 3 * A_WIDTH), F32)] + (comm.sem_scratch() if comm else []),
        compiler_params=_params(("arbitrary", "arbitrary") if comm else ("parallel", "arbitrary")),
    )(qkv_view, qkv_view, ol_view, dod_view, *c_ins)


def _mla_bwd(qf, kf, v, lse, do, delta, comm=None):
    n_rows = v.shape[0]
    tq, tk = min(MLA_BWD_TILES[0], n_rows), min(MLA_BWD_TILES[1], n_rows)
    nq, nk = n_rows // tq, n_rows // tk

    def first_q(j):
        return jnp.right_shift(j * tk, int(math.log2(tq)))

    grid = (MLA_HEADS, nk, nq)
    c_ins, c_in_specs, c_outs, c_out_specs, begin, end = _hosted(comm, grid)
    n_ci, n_co = len(c_ins), len(c_outs)

    def body(q_ref, k_ref, v_ref, lse_ref, do_ref, dl_ref, *refs):
        ci_refs, (dq_ref, dk_ref, dv_ref), co_refs = refs[:n_ci], refs[n_ci:n_ci + 3], refs[n_ci + 3:n_ci + 3 + n_co]
        dk_sc, dv_sc = refs[n_ci + 3 + n_co:n_ci + 5 + n_co]
        sems = refs[n_ci + 5 + n_co:]
        j, i = pl.program_id(1), pl.program_id(2)
        begin(ci_refs, co_refs, sems)

        @pl.when((j == 0) & (i == 0))
        def _():
            dq_ref[...] = jnp.zeros(dq_ref.shape, F32)

        @pl.when(i == first_q(j))
        def _():
            dk_sc[...] = jnp.zeros(dk_sc.shape, F32)
            dv_sc[...] = jnp.zeros(dv_sc.shape, F32)

        def step(masked):
            q, k, dob = q_ref[...], k_ref[...], do_ref[...]
            s = _dot(q, k, _NT) * MLA_SCALE
            if masked:
                row = lax.broadcasted_iota(jnp.int32, (tq, tk), 0) + i * tq
                col = lax.broadcasted_iota(jnp.int32, (tq, tk), 1) + j * tk
                s = jnp.where(col <= row, s, NEG)
            p = jnp.exp(s - lse_ref[:, :1])
            dp = _dot(dob, v_ref[...], _NT)
            ds = (p * (dp - dl_ref[:, :1]) * MLA_SCALE).astype(BF16)
            dv_sc[...] += _dot(p.astype(BF16), dob, _TN)
            dk_sc[...] += _dot(ds, q, _TN)
            rows = pl.ds(pl.multiple_of(i * tq, tq), tq)
            dq_ref[rows, :] += _dot(ds, k, _NN)

        active = i >= first_q(j)
        crosses = (j + 1) * tk - 1 > i * tq

        @pl.when(active & jnp.logical_not(crosses))
        def _():
            step(False)

        @pl.when(active & crosses)
        def _():
            step(True)

        @pl.when(i == nq - 1)
        def _():
            dk_ref[...] = dk_sc[...]
            dv_ref[...] = dv_sc[...]

        end(ci_refs, co_refs, sems)

    qrow = lambda h, j, i: (jnp.maximum(i, first_q(j)), h)
    return pl.pallas_call(
        body, name="mla_bwd", grid=grid,
        in_specs=[pl.BlockSpec((None, tq, QK_PAD), lambda h, j, i: (h, jnp.maximum(i, first_q(j)), 0)),
                  pl.BlockSpec((None, tk, QK_PAD), lambda h, j, i: (h, j, 0)),
                  pl.BlockSpec((tk, V_DIM), lambda h, j, i: (j, h)),
                  pl.BlockSpec((tq, V_DIM), qrow), pl.BlockSpec((tq, V_DIM), qrow), pl.BlockSpec((tq, V_DIM), qrow)]
        + c_in_specs,
        out_specs=[pl.BlockSpec((None, n_rows, QK_PAD), lambda h, j, i: (h, 0, 0)),
                   pl.BlockSpec((None, tk, QK_PAD), lambda h, j, i: (h, j, 0)),
                   pl.BlockSpec((tk, V_DIM), lambda h, j, i: (j, h))] + c_out_specs,
        out_shape=[_sds((MLA_HEADS, n_rows, QK_PAD)), _sds((MLA_HEADS, n_rows, QK_PAD)), _sds((n_rows, MLA_WIDTH))] + c_outs,
        scratch_shapes=[pltpu.VMEM((tk, QK_PAD), F32), pltpu.VMEM((tk, V_DIM), F32)] + (comm.sem_scratch() if comm else []),
        compiler_params=_params(("arbitrary",) * 3 if comm else ("parallel", "arbitrary", "arbitrary")),
    )(qf, kf, v, lse, do, delta, *c_ins)


def _mla_bwd_prep(dqf, dkf, dv, cos, sin):
    def fn(dqf_v, dkf_v, dv_v, cos_v, sin_v):
        dq = jnp.concatenate([dqf_v[h][:, :QK_NOPE] for h in range(MLA_HEADS)]
                             + [_rope_t(dqf_v[h][:, QK_NOPE:], cos_v[:, :LANES], sin_v[:, :LANES]) for h in range(MLA_HEADS)], axis=1)
        dkv = jnp.concatenate([dkf_v[h][:, :QK_NOPE] for h in range(MLA_HEADS)] + [dv_v], axis=1)
        dkpe = dkf_v[0][:, QK_NOPE:] + dkf_v[1][:, QK_NOPE:] + dkf_v[2][:, QK_NOPE:] + dkf_v[3][:, QK_NOPE:]
        return dq, dkv, dkpe

    n_rows = dv.shape[0]
    return _rowwise("mla_bwd_prep", fn, [dqf, dkf, dv, cos, sin],
                    [_sds((n_rows, UQ_PAD), BF16), _sds((n_rows, 1024), BF16), _sds((n_rows, LANES))], ts=256)


def _assemble_dh(dqkvs, dcqn, cq, dckvn, ckv, dkpe, cos, sin, gq, gkv, comm=None):
    def fn(g1, g2, g3, dcqn_v, cq_v, dckvn_v, ckv_v, dkpe_v, cos_v, sin_v, gq_v, gkv_v):
        g = g1 + g2 + g3
        dqa = _rope_t(g[:, :A_WIDTH], cos_v, sin_v) * A_SCALE
        dka = _rope_t(g[:, A_WIDTH:2 * A_WIDTH], cos_v, sin_v)
        dcq, dgq = _rms_bwd(cq_v, gq_v, dcqn_v)
        dckv, dgkv = _rms_bwd(ckv_v, gkv_v, dckvn_v)
        dkr = _rope_t(dkpe_v, cos_v[:, :LANES], sin_v[:, :LANES])
        return jnp.concatenate([dqa, dka, g[:, 2 * A_WIDTH:], dcq, dkr, dckv], axis=1), dgq, dgkv

    n_rows = cq.shape[0]
    return _rowwise("assemble_dh", fn, [*dqkvs, dcqn, cq, dckvn, ckv, dkpe, cos, sin],
                    [_sds((n_rows, IN_PAD), BF16)], consts=[gq, gkv],
                    reds=[_sds((1, Q_LORA)), _sds((1, KV_LORA))], ts=256,
                    in_dils=[dil for _, dil in PATTERNS] + [1] * 7, comm=comm)


def _layer_fwd(x_f32, x_bf, w, sm, cos, sin, behind=(None, None), rest=None):
    *qkv_views, cq, kpe, ckv = _in_proj(x_bf, w["w_in"], cos, sin)
    ol_views, brought = [], []
    for i, (view, (_, dil)) in enumerate(zip(qkv_views, PATTERNS)):
        ol, *got = _dil_fwd(view, dil, comm=rest[0][i] if rest else None)
        ol_views.append(ol)
        brought.append(got)
    if rest:
        w = {**w, **rest[1](brought)}
    cqn, qf = _mla_q_prep(cq, sm["q_a_norm"], w["w_uq"], cos, sin)
    ckvn, kf, v = _mla_kv_prep(ckv, kpe, sm["kv_a_norm"], w["w_ukv"])
    b_out, b_lse, *behind_attn = _mla_fwd(qf, kf, v, comm=behind[0])
    a_out, mixed = _mix_fwd(ol_views, b_out, sm["a_out_norm"], sm["b_out_norm"])
    x1, x1_bf, r1 = _mm_res_ln("wo_ln1", mixed, w["w_o"], x_f32, sm["ln1_g"], sm["ln1_b"])
    f, *behind_ff1 = _mm_relu2(x1_bf, w["w_ff1"], comm=behind[1])
    x2, x2_bf, r2 = _mm_res_ln("ff2_ln2", f, w["w_ff2"], x1, sm["ln2_g"], sm["ln2_b"])
    saved = dict(w=w, x_bf=x_bf, qkv_views=qkv_views, ol_views=ol_views, cq=cq, ckv=ckv, cqn=cqn, qf=qf, ckvn=ckvn,
                 kf=kf, v=v, b_out=b_out, b_lse=b_lse, a_out=a_out, mixed=mixed, x1_bf=x1_bf, r1=r1, f=f, r2=r2)
    return x2, x2_bf, saved, (behind_attn, behind_ff1)


def _layer_bwd(dx2, w, sm, sv, cos, sin, reduce_of, pending=None):
    dr2, dr2_bf, dg2, db2 = _ln_bwd("ln2_bwd", dx2, sv["r2"], sm["ln2_g"])
    (dw_ff2,) = _mm("dw_ff2", sv["f"], dr2_bf, (F32,), ta=True, tm=1024, tn=1024, tk=2048)
    du, *swapped = _mm("d_u", dr2_bf, w["w_ff2"], (BF16,), tb=True, tm=1024, tn=2048, tk=D_MODEL,
                       epilogue=lambda acc, f: (acc * (2.0 * jnp.sqrt(f.astype(F32))),), extras=[sv["f"]],
                       comm=_swap_comm(pending.pieces) if pending else None)
    if pending:
        pending.after_swap(swapped)
    (dw_ff1,) = _mm("dw_ff1", sv["x1_bf"], du, (F32,), ta=True, tm=1024, tn=1024, tk=2048, out_pieces=True)
    (dy1,) = _mm("d_x1", du, w["w_ff1"], (F32,), tb=True, tm=1024, tn=1024, tk=2048,
                 epilogue=lambda acc, d: (acc + ALPHA * d,), extras=[dr2], b_pieces=True)
    dr1, dr1_bf, dg1, db1 = _ln_bwd("ln1_bwd", dy1, sv["r1"], sm["ln1_g"])
    (dw_o,) = _mm("dw_o", sv["mixed"], dr1_bf, (F32,), ta=True, tm=1024, tn=1024, tk=2048)
    early = reduce_of(dict(w_o=dw_o, w_ff1=dw_ff1, w_ff2=dw_ff2))
    (dmixed,) = _mm("d_mixed", dr1_bf, w["w_o"], (F32,), tb=True, tm=1024, tn=1024, tk=D_MODEL)
    dod1, dod2, dod3, do_b, delta_b, dga, dgb = _mix_bwd(
        dmixed, sv["a_out"], sv["b_out"], sv["ol_views"], sm["a_out_norm"], sm["b_out_norm"])
    dqkvs = []
    for i, ((_, dil), qkv_view, ol_view, dod_view) in enumerate(zip(PATTERNS, sv["qkv_views"], sv["ol_views"], (dod1, dod2, dod3))):
        swap_here = early is not None and i == 0
        dqkv, *swapped = _dil_bwd(qkv_view, ol_view, dod_view, dil, comm=_swap_comm(early.pieces) if swap_here else None)
        dqkvs.append(dqkv)
        if swap_here:
            early.after_swap(swapped)
    to_scatter = (early.s16 if early else []) + (pending.s16 if pending else [])
    dqf, dkf, dv_b, *scattered = _mla_bwd(sv["qf"], sv["kf"], sv["v"], sv["b_lse"], do_b, delta_b,
                                          comm=_scatter_comm(to_scatter) if to_scatter else None)
    if early:
        early.after_scatter(scattered[:len(early.s16)])
    if pending:
        pending.after_scatter(scattered[len(scattered) - len(pending.s16):])
    dq_b, dkv_b, dkpe = _mla_bwd_prep(dqf, dkf, dv_b, cos, sin)
    (dw_uq,) = _mm("dw_uq", sv["cqn"], dq_b, (F32,), ta=True, tm=Q_LORA, tn=1024, tk=2048)
    (dcqn,) = _mm("d_cqn", dq_b, w["w_uq"], (F32,), tb=True, tm=1024, tn=Q_LORA, tk=UQ_PAD)
    (dw_ukv,) = _mm("dw_ukv", sv["ckvn"], dkv_b, (F32,), ta=True, tm=KV_LORA, tn=1024, tk=2048)
    (dckvn,) = _mm("d_ckvn", dkv_b, w["w_ukv"], (F32,), tb=True, tm=1024, tn=KV_LORA, tk=1024)
    to_share = (early.mine if early else []) + (pending.mine if pending else [])
    dh, dgq, dgkv, *shared = _assemble_dh(dqkvs, dcqn, sv["cq"], dckvn, sv["ckv"], dkpe, cos, sin,
                                          sm["q_a_norm"], sm["kv_a_norm"], comm=_share_comm(to_share) if to_share else None)
    if early:
        early.after_share(shared[:len(early.mine)])
    if pending:
        pending.after_share(shared[len(shared) - len(pending.mine):])
    (dw_in,) = _mm("dw_in", sv["x_bf"], dh, (F32,), ta=True, tm=1024, tn=1024, tk=2048)
    (dx,) = _mm("d_x", dh, w["w_in"], (F32,), tb=True, tm=1024, tn=1024, tk=2048,
                epilogue=lambda acc, d: (acc + ALPHA * d,), extras=[dr1])
    late = reduce_of(dict(w_in=dw_in, w_uq=dw_uq, w_ukv=dw_ukv))
    dsm = dict(q_a_norm=dgq, kv_a_norm=dgkv, a_out_norm=dga, b_out_norm=dgb, ln1_g=dg1, ln1_b=db1, ln2_g=dg2, ln2_b=db2)
    return dx, dsm, late


def _pad_w_in(w):
    return jnp.concatenate([w[:, :1792], w[:, 1920:1984], jnp.zeros((w.shape[0], 64), w.dtype), w[:, 1792:1920]], axis=1)


def _unpad_w_in(w):
    return jnp.concatenate([w[:, :1792], w[:, 1920:2048], w[:, 1792:1856]], axis=1)


def _rope_tables(n_rows):
    half = A_HEAD_DIM // 2
    inv_freq = ROPE_THETA ** (-jnp.arange(half, dtype=F32) / half)
    ang = jnp.arange(n_rows, dtype=F32)[:, None] * inv_freq[None, :]
    cos = jnp.tile(jnp.cos(ang), (1, 2 * A_HEADS))
    sin = jnp.tile(jnp.concatenate([-jnp.sin(ang), jnp.sin(ang)], axis=1), (1, A_HEADS))
    return cos, sin


W_NAMES = ("w_in", "w_uq", "w_ukv", "w_o", "w_ff1", "w_ff2")
PIECE = dict(w_in=(512, 512), w_uq=(128, 256), w_ukv=(64, 256), w_o=(128, 1024), w_ff1=(512, 1024), w_ff2=(512, 1024))
SHARD_COLS = dict(w_in=496, w_uq=192, w_ukv=256, w_o=1024, w_ff1=1024, w_ff2=1024)


def _col_pieces(t, pad_to):
    rows, _, cols = t.shape
    t = jnp.pad(t, ((0, 0), (0, 0), (0, pad_to - cols)))
    return t.reshape(2, rows // 2, N_CHIPS, pad_to).transpose(2, 0, 1, 3)


def _grad_pieces(dws):
    def heads(t, rows):
        return jnp.concatenate([t[:, :512].reshape(rows, MLA_HEADS, LANES), t[:, 512:].reshape(rows, MLA_HEADS, LANES)], axis=2)

    make = dict(
        w_in=lambda t: _col_pieces(_unpad_w_in(t).reshape(D_MODEL, N_CHIPS, SHARD_COLS["w_in"]), 512),
        w_uq=lambda t: _col_pieces(heads(t, Q_LORA), 256),
        w_ukv=lambda t: _col_pieces(heads(t, KV_LORA), 256),
        w_o=lambda t: t.reshape(N_CHIPS, 2, 128, D_MODEL),
        w_ff1=lambda t: t,
        w_ff2=lambda t: t.reshape(N_CHIPS, 2, 512, D_MODEL))
    return {n: make[n](t) for n, t in dws.items()}


def _weights_from_pieces(p):
    def cols(t):
        return t.transpose(1, 2, 0, 3).reshape(2 * t.shape[2], N_CHIPS, t.shape[3])

    def heads(t, rows):
        return jnp.concatenate([t[:, :, :LANES].reshape(rows, 512), t[:, :, LANES:].reshape(rows, 512)], axis=1)

    make = dict(
        w_in=lambda t: _pad_w_in(cols(t)[:, :, :SHARD_COLS["w_in"]].reshape(D_MODEL, IN_COLS)),
        w_uq=lambda t: heads(cols(t), Q_LORA),
        w_ukv=lambda t: heads(cols(t), KV_LORA),
        w_o=lambda t: t.reshape(D_MODEL, D_MODEL),
        w_ff1=lambda t: t,
        w_ff2=lambda t: t.reshape(D_FF, D_MODEL))
    return {n: make[n](t) for n, t in p.items()}


def _my_piece(name, shard, half):
    rows, cols = shard.shape[0] // 2, shard.shape[1]
    t = lax.dynamic_slice_in_dim(shard, half * rows, rows, axis=0).astype(BF16)
    return jnp.pad(t, ((0, 0), (0, PIECE[name][1] - cols)))


MESH = pl.DeviceIdType.MESH
ANY = pl.BlockSpec(memory_space=pl.ANY)


class _Comm:
    def __init__(self, ins, out_shapes, n_sems, start, finish):
        self.ins, self.out_shapes, self.n_sems, self.start, self.finish = list(ins), list(out_shapes), n_sems, start, finish

    def sem_scratch(self):
        return [pltpu.SemaphoreType.DMA((self.n_sems,)), pltpu.SemaphoreType.DMA((self.n_sems,))]


def _comm_call(name, comm):
    n_in, n_out = len(comm.ins), len(comm.out_shapes)

    def body(*refs):
        ins, outs, sems = refs[:n_in], refs[n_in:n_in + n_out], refs[n_in + n_out:]
        comm.start(ins, outs, *sems)
        comm.finish(ins, outs, *sems)

    return pl.pallas_call(body, name=name, out_shape=comm.out_shapes, in_specs=[ANY] * n_in, out_specs=[ANY] * n_out,
                          scratch_shapes=comm.sem_scratch())(*comm.ins)


def _all_gather_comm(blocks):
    n = len(blocks)

    def plan(x_refs, out_refs, send_sems, recv_sems):
        x, y, c = lax.axis_index("x"), lax.axis_index("y"), lax.axis_index("c")
        me, sibling = (x, y, c), (x, y, 1 - c)
        chips = [(1 - x, y), (x, 1 - y), (1 - x, 1 - y)]

        def copy(t, k, blk, to, src=None):
            px, py, pc = blk
            slot = out_refs[t].at[4 * px + 2 * py + pc]
            return pltpu.make_async_remote_copy(
                src_ref=slot if src is None else src, dst_ref=slot,
                send_sem=send_sems.at[7 * t + k], recv_sem=recv_sems.at[7 * t + k], device_id=to, device_id_type=MESH)

        first = []
        for t in range(n):
            first.append(copy(t, 0, me, sibling, src=x_refs[t]))
            first += [copy(t, 1 + j, me, (*chip, c), src=x_refs[t]) for j, chip in enumerate(chips)]
        return me, sibling, chips, c, copy, first

    def start(x_refs, out_refs, send_sems, recv_sems):
        for cp in plan(x_refs, out_refs, send_sems, recv_sems)[-1]:
            cp.start()

    def finish(x_refs, out_refs, send_sems, recv_sems):
        me, sibling, chips, c, copy, sent = plan(x_refs, out_refs, send_sems, recv_sems)
        for j, chip in enumerate(chips):
            for t in range(n):
                copy(t, 1 + j, (*chip, c), me).wait_recv()
                sent.append(copy(t, 4 + j, (*chip, c), sibling))
                sent[-1].start()
        for t in range(n):
            copy(t, 0, sibling, me).wait_recv()
        for j, chip in enumerate(chips):
            for t in range(n):
                copy(t, 4 + j, (*chip, 1 - c), me).wait_recv()
        for cp in sent:
            cp.wait_send()

    return _Comm(blocks, [_sds((N_DEV,) + b.shape, b.dtype) for b in blocks], 7 * n, start, finish)


def _simple_comm(ins, out_shapes, n_sems, copies):
    def start(in_refs, out_refs, send_sems, recv_sems):
        for cp in copies(in_refs, out_refs, send_sems, recv_sems):
            cp.start()

    def finish(in_refs, out_refs, send_sems, recv_sems):
        for cp in copies(in_refs, out_refs, send_sems, recv_sems):
            cp.wait()

    return _Comm(ins, out_shapes, n_sems, start, finish)


def _swap_comm(gs):
    def copies(g_refs, got_refs, send_sems, recv_sems):
        c = lax.axis_index("c")
        sibling = (lax.axis_index("x"), lax.axis_index("y"), 1 - c)
        return [pltpu.make_async_remote_copy(src_ref=g_refs[t].at[k, 1 - c], dst_ref=got_refs[t].at[k],
                                             send_sem=send_sems.at[N_CHIPS * t + k], recv_sem=recv_sems.at[N_CHIPS * t + k],
                                             device_id=sibling, device_id_type=MESH)
                for t in range(len(gs)) for k in range(N_CHIPS)]

    return _simple_comm(gs, [_sds((N_CHIPS,) + g.shape[2:], g.dtype) for g in gs], N_CHIPS * len(gs), copies)


def _scatter_comm(hs):
    def copies(h_refs, got_refs, send_sems, recv_sems):
        x, y, c = lax.axis_index("x"), lax.axis_index("y"), lax.axis_index("c")
        chips = [(1 - x, y), (x, 1 - y), (1 - x, 1 - y)]
        return [pltpu.make_async_remote_copy(src_ref=h_refs[t].at[2 * px + py], dst_ref=got_refs[t].at[rel],
                                             send_sem=send_sems.at[3 * t + rel], recv_sem=recv_sems.at[3 * t + rel],
                                             device_id=(px, py, c), device_id_type=MESH)
                for t in range(len(hs)) for rel, (px, py) in enumerate(chips)]

    return _simple_comm(hs, [_sds((3,) + h.shape[1:], h.dtype) for h in hs], 3 * len(hs), copies)


def _share_comm(rs):
    def copies(r_refs, got_refs, send_sems, recv_sems):
        sibling = (lax.axis_index("x"), lax.axis_index("y"), 1 - lax.axis_index("c"))
        return [pltpu.make_async_remote_copy(src_ref=r_refs[t], dst_ref=got_refs[t], send_sem=send_sems.at[t],
                                             recv_sem=recv_sems.at[t], device_id=sibling, device_id_type=MESH)
                for t in range(len(rs))]

    return _simple_comm(rs, [_sds(r.shape, r.dtype) for r in rs], len(rs), copies)


SUM_STEPS = 4


def _pair_sums(name, gs, gots, c_arr):
    n = len(gs)
    dims = [(g.shape[2] // SUM_STEPS, g.shape[3]) for g in gs]

    def body(c_ref, *refs):
        for t in range(n):
            s = refs[t][...] + refs[n + t][...]
            refs[2 * n + t][...] = s
            refs[3 * n + t][...] = s.astype(BF16)

    def part(tr, pc):
        return pl.BlockSpec((None, tr, pc), lambda k, i, c_ref: (k, i, 0))

    def kept(tr, pc):
        return pl.BlockSpec((None, None, tr, pc), lambda k, i, c_ref: (k, c_ref[0], i, 0))

    outs = pl.pallas_call(
        body, name=name,
        out_shape=[_sds((N_CHIPS,) + g.shape[2:]) for g in gs] + [_sds((N_CHIPS,) + g.shape[2:], BF16) for g in gs],
        grid_spec=pltpu.PrefetchScalarGridSpec(
            num_scalar_prefetch=1, grid=(N_CHIPS, SUM_STEPS),
            in_specs=[kept(*d) for d in dims] + [part(*d) for d in dims],
            out_specs=[part(*d) for d in dims] * 2),
        compiler_params=_params(("parallel", "parallel")),
    )(c_arr, *gs, *gots)
    return list(outs[:n]), list(outs[n:])


def _chips_sums(name, h32s, gots, chip_arr):
    n = len(h32s)
    dims = [(h.shape[1] // SUM_STEPS, h.shape[2]) for h in h32s]

    def body(chip_ref, *refs):
        for t in range(n):
            got_ref = refs[n + t]
            refs[2 * n + t][...] = refs[t][...] + got_ref[0] + got_ref[1] + got_ref[2]

    return pl.pallas_call(
        body, name=name, out_shape=[_sds(h.shape[1:]) for h in h32s],
        grid_spec=pltpu.PrefetchScalarGridSpec(
            num_scalar_prefetch=1, grid=(SUM_STEPS,),
            in_specs=[pl.BlockSpec((None, tr, pc), lambda i, chip_ref: (chip_ref[0], i, 0)) for tr, pc in dims]
            + [pl.BlockSpec((3, tr, pc), lambda i, chip_ref: (0, i, 0)) for tr, pc in dims],
            out_specs=[pl.BlockSpec((tr, pc), lambda i, chip_ref: (i, 0)) for tr, pc in dims]),
        compiler_params=_params(("parallel",)),
    )(chip_arr, *h32s, *gots)


class _Reduce:
    def __init__(self, tag, dws, c, chip):
        gp = _grad_pieces(dws)
        self.tag, self.c, self.names, self.pieces = tag, c, list(gp), list(gp.values())
        self.c_arr, self.chip_arr = jnp.reshape(c, (1,)).astype(jnp.int32), jnp.reshape(chip, (1,)).astype(jnp.int32)
        self.shards = None

    def after_swap(self, gots):
        self.s32, self.s16 = _pair_sums(f"rs_pair_sum_{self.tag}", self.pieces, gots, self.c_arr)

    def after_scatter(self, parts):
        self.mine = list(_chips_sums(f"rs_chips_sum_{self.tag}", self.s32, parts, self.chip_arr))

    def after_share(self, theirs):
        self.shards = {}
        for n, a, b in zip(self.names, self.mine, theirs):
            both = jnp.where(self.c == 0, jnp.concatenate([a, b], axis=0), jnp.concatenate([b, a], axis=0))
            self.shards[n] = both[:, :SHARD_COLS[n]]

    def run_alone(self):
        self.after_swap(_comm_call(f"rs_pair_{self.tag}", _swap_comm(self.pieces)))
        self.after_scatter(_comm_call(f"rs_chips_{self.tag}", _scatter_comm(self.s16)))
        self.after_share(_comm_call(f"rs_share_{self.tag}", _share_comm(self.mine)))


class _LayerWeights:
    def __init__(self, shards, c, dev):
        self.shards, self.c, self.dev = shards, c, dev

    def comm(self, l, names=W_NAMES):
        return _all_gather_comm([_my_piece(n, self.shards[l][n], self.c) for n in names])

    def take(self, l, gathered, names=W_NAMES):
        pieces = {}
        for n, g in zip(names, gathered):
            mine = _my_piece(n, self.shards[l][n], self.c)
            pieces[n] = lax.dynamic_update_index_in_dim(g, mine, self.dev, 0).reshape((N_CHIPS, 2) + mine.shape)
        return _weights_from_pieces(pieces)

    def first(self):
        now, later = ("w_in", "w_uq", "w_ukv"), ("w_ff1", "w_ff2", "w_o")
        weights = self.take(0, _comm_call("ag_l0", self.comm(0, now)), now)
        comms = [self.comm(0, (n,)) for n in later]

        def take_rest(results):
            rest = {}
            for n, gathered in zip(later, results):
                rest.update(self.take(0, gathered, (n,)))
            return rest

        return weights, (comms, take_rest)


def _all_reduce_small(vec):
    rows, lanes = vec.shape

    def gather_body(x_ref, out_ref, send_sems, recv_sems):
        x, y, c = lax.axis_index("x"), lax.axis_index("y"), lax.axis_index("c")
        me, sibling = (x, y, c), (x, y, 1 - c)
        chips = [(1 - x, y), (x, 1 - y), (1 - x, 1 - y)]

        def slot(px, py, pc):
            return out_ref.at[4 * px + 2 * py + pc]

        def copy(k, blk, to, src=None):
            return pltpu.make_async_remote_copy(
                src_ref=slot(*blk) if src is None else src, dst_ref=slot(*blk),
                send_sem=send_sems.at[k], recv_sem=recv_sems.at[k], device_id=to, device_id_type=MESH)

        out_ref[4 * x + 2 * y + c] = x_ref[...]
        first = [copy(0, me, sibling, src=x_ref)]
        first += [copy(1 + j, me, (*chip, c), src=x_ref) for j, chip in enumerate(chips)]
        for cp in first:
            cp.start()
        passed = [copy(4 + j, (*chip, c), sibling) for j, chip in enumerate(chips)]
        for j, chip in enumerate(chips):
            copy(1 + j, (*chip, c), me).wait_recv()
            passed[j].start()
        copy(0, sibling, me).wait_recv()
        for j, chip in enumerate(chips):
            copy(4 + j, (*chip, 1 - c), me).wait_recv()
        for cp in first + passed:
            cp.wait_send()

    vmem = pl.BlockSpec(memory_space=pltpu.VMEM)
    allv = pl.pallas_call(
        gather_body, name="small_all_gather", out_shape=_sds((N_DEV, rows, lanes)),
        in_specs=[vmem], out_specs=vmem,
        scratch_shapes=[pltpu.SemaphoreType.DMA((7,)), pltpu.SemaphoreType.DMA((7,))],
    )(vec)

    def sum_body(a_ref, o_ref):
        acc = a_ref[0]
        for d in range(1, N_DEV):
            acc = acc + a_ref[d]
        o_ref[...] = acc

    return pl.pallas_call(sum_body, name="small_sum", out_shape=_sds((rows, lanes)), in_specs=[vmem], out_specs=vmem)(allv)


def _adamw(name, w, g, m, v, ts=512, comm=None):
    def fn(w_v, g_v, m_v, v_v):
        m_n = ADAM_B1 * m_v + (1.0 - ADAM_B1) * g_v
        v_n = ADAM_B2 * v_v + (1.0 - ADAM_B2) * (g_v * g_v)
        m_hat = m_n / (1.0 - ADAM_B1 ** ADAM_STEP)
        v_hat = v_n / (1.0 - ADAM_B2 ** ADAM_STEP)
        delta = -ADAM_LR * (m_hat / (jnp.sqrt(v_hat) + ADAM_EPS) + ADAM_WD * w_v)
        return delta, m_n, v_n

    return _rowwise(name, fn, [w, g, m, v], [_sds(w.shape)] * 3, ts=ts, comm=comm)


def _pack_small(per_layer):
    flat = jnp.concatenate([per_layer[l][n].reshape(-1) for l in range(DEPTH) for n in SMALL_NAMES])
    return jnp.pad(flat, (0, SMALL_ROWS * LANES - flat.shape[0])).reshape(SMALL_ROWS, LANES)


def _unpack_small(packed):
    flat = packed.reshape(-1)
    per = sum(SMALL_SIZES)
    out = {}
    for n, size, off in zip(SMALL_NAMES, SMALL_SIZES, [sum(SMALL_SIZES[:i]) for i in range(len(SMALL_SIZES))]):
        out[n] = jnp.stack([flat[l * per + off:l * per + off + size] for l in range(DEPTH)])
    return out


NEXT_WEIGHTS_BEHIND = (("w_ff1", "w_ff2"), ("w_in", "w_uq", "w_ukv", "w_o"))


def _fwd_bwd(x, target, layer_weights, smalls, on_layer_grads):
    depth = len(smalls)
    cos, sin = _rope_tables(x.shape[0])
    h_f32, h_bf = x, x.astype(BF16)
    saved = []
    w, rest = layer_weights.first()
    for l in range(depth):
        more = l + 1 < depth
        behind = [layer_weights.comm(l + 1, names) if more else None for names in NEXT_WEIGHTS_BEHIND]
        h_f32, h_bf, sv, brought = _layer_fwd(h_f32, h_bf, w, smalls[l], cos, sin, behind, rest)
        saved.append(sv)
        if more:
            w, rest = {}, None
            for names, got in zip(NEXT_WEIGHTS_BEHIND, brought):
                w.update(layer_weights.take(l + 1, got, names))
    loss_part, dh = _loss_fn(h_f32, target)
    small_grads, pending = [None] * depth, None
    for l in reversed(range(depth)):
        dh, small_grads[l], pending = _layer_bwd(dh, saved[l]["w"], smalls[l], saved[l], cos, sin,
                                                 lambda dws, l=l: on_layer_grads(l, dws), pending)
    return loss_part, dh, small_grads, pending


def kernel(x, w_in, q_a_norm, kv_a_norm, w_uq, w_ukv, a_out_norm, b_out_norm, w_o, ln1_g, ln1_b, w_ff1, w_ff2, ln2_g, ln2_b, loss_target, m_w_in, m_q_a_norm, m_kv_a_norm, m_w_uq, m_w_ukv, m_a_out_norm, m_b_out_norm, m_w_o, m_ln1_g, m_ln1_b, m_w_ff1, m_w_ff2, m_ln2_g, m_ln2_b, v_w_in, v_q_a_norm, v_kv_a_norm, v_w_uq, v_w_ukv, v_a_out_norm, v_b_out_norm, v_w_o, v_ln1_g, v_ln1_b, v_w_ff1, v_w_ff2, v_ln2_g, v_ln2_b):
    c = lax.axis_index("c")
    chip = 2 * lax.axis_index("x") + lax.axis_index("y")
    big = dict(w_in=w_in, w_uq=w_uq, w_ukv=w_ukv, w_o=w_o, w_ff1=w_ff1, w_ff2=w_ff2)
    big_m = dict(w_in=m_w_in, w_uq=m_w_uq, w_ukv=m_w_ukv, w_o=m_w_o, w_ff1=m_w_ff1, w_ff2=m_w_ff2)
    big_v = dict(w_in=v_w_in, w_uq=v_w_uq, w_ukv=v_w_ukv, w_o=v_w_o, w_ff1=v_w_ff1, w_ff2=v_w_ff2)
    small = dict(q_a_norm=q_a_norm, kv_a_norm=kv_a_norm, a_out_norm=a_out_norm, b_out_norm=b_out_norm,
                 ln1_g=ln1_g, ln1_b=ln1_b, ln2_g=ln2_g, ln2_b=ln2_b)
    small_m = dict(q_a_norm=m_q_a_norm, kv_a_norm=m_kv_a_norm, a_out_norm=m_a_out_norm, b_out_norm=m_b_out_norm,
                   ln1_g=m_ln1_g, ln1_b=m_ln1_b, ln2_g=m_ln2_g, ln2_b=m_ln2_b)
    small_v = dict(q_a_norm=v_q_a_norm, kv_a_norm=v_kv_a_norm, a_out_norm=v_a_out_norm, b_out_norm=v_b_out_norm,
                   ln1_g=v_ln1_g, ln1_b=v_ln1_b, ln2_g=v_ln2_g, ln2_b=v_ln2_b)

    layer_weights = _LayerWeights([{n: big[n][l] for n in W_NAMES} for l in range(DEPTH)], c, 2 * chip + c)
    smalls = [{n: small[n][l][None, :] for n in SMALL_NAMES} for l in range(DEPTH)]
    reductions = [[] for _ in range(DEPTH)]

    def reduce_layer(l, dws):
        reductions[l].append(_Reduce(f"l{l}_{len(reductions[l])}", dws, c, chip))
        return reductions[l][-1]

    loss_part, grad_x, small_grads, last = _fwd_bwd(x[0], loss_target[0], layer_weights, smalls, reduce_layer)
    loss = lax.psum(loss_part[0, 0], ("x", "y", "c"))
    grad_x = grad_x[None]

    g_small_packed = _all_reduce_small(_pack_small(small_grads))
    pack_in = lambda d: _pack_small([{n: d[n][l] for n in SMALL_NAMES} for l in range(DEPTH)])
    sd, sm_, sv_ = _adamw("adamw_small", pack_in(small), g_small_packed, pack_in(small_m), pack_in(small_v), ts=SMALL_ROWS)
    g_small, d_small, m_small, v_small = (_unpack_small(t) for t in (g_small_packed, sd, sm_, sv_))

    last.after_swap(_comm_call(f"rs_pair_{last.tag}", _swap_comm(last.pieces)))
    g_big, d_big, m_big, v_big = {}, {}, {}, {}
    for n in ("w_ff2", "w_ff1", "w_o", "w_in", "w_uq", "w_ukv"):
        behind = _scatter_comm(last.s16) if n == "w_ff2" else _share_comm(last.mine) if n == "w_ff1" else None
        g = jnp.stack([next(r.shards[n] for r in reductions[l] if n in r.names) for l in range(DEPTH)])
        shape = g.shape
        flat = lambda t: t.reshape(shape[0] * shape[1], shape[2])
        d_, m_, v_, *brought = _adamw(f"adamw_{n}", flat(big[n]), flat(g), flat(big_m[n]), flat(big_v[n]), comm=behind)
        g_big[n], d_big[n], m_big[n], v_big[n] = g, d_.reshape(shape), m_.reshape(shape), v_.reshape(shape)
        if n == "w_ff2":
            last.after_scatter(brought)
        elif n == "w_ff1":
            last.after_share(brought)

    order = ("w_in", "q_a_norm", "kv_a_norm", "w_uq", "w_ukv", "a_out_norm", "b_out_norm", "w_o", "ln1_g", "ln1_b",
             "w_ff1", "w_ff2", "ln2_g", "ln2_b")
    pick = lambda bigd, smalld: [bigd[n] if n in bigd else smalld[n] for n in order]
    return (loss, grad_x, *pick(g_big, g_small), *pick(d_big, d_small), *pick(m_big, m_small), *pick(v_big, v_small))
```

```python
import math

import jax
import jax.numpy as jnp
from jax import lax
from jax.experimental import pallas as pl
from jax.experimental.pallas import tpu as pltpu

F32, BF16 = jnp.float32, jnp.bfloat16

D_MODEL = 1024
DEPTH = 4
A_HEAD_DIM = 64
A_HEADS = 8
A_WIDTH = 512
PATTERNS = ((128, 1), (512, 4), (2048, 16))
SPAN = 128
MLA_HEADS = 4
QK_NOPE = 128
QK_ROPE = 64
V_DIM = 128
Q_LORA = 256
KV_LORA = 128
MLA_WIDTH = 512
QK_PAD = 256
IN_COLS = 1984
IN_PAD = 2048
UQ_PAD = 1024
D_FF = 4096
ROPE_THETA = 10000.0
ALPHA = (2.0 * DEPTH) ** 0.25
LN_EPS = 1e-5
RMS_EPS = 1e-6
MLA_SCALE = (QK_NOPE + QK_ROPE) ** -0.5
A_SCALE = A_HEAD_DIM ** -0.5

ADAM_LR, ADAM_B1, ADAM_B2, ADAM_EPS, ADAM_WD, ADAM_STEP = 0.001, 0.9, 0.999, 1e-08, 0.01, 10

VMEM_LIMIT_BYTES = 56 * 1024 * 1024
NEG = -1e30
N_CHIPS, N_DEV = 4, 8
LANES = 128

SMALL_NAMES = ("q_a_norm", "kv_a_norm", "a_out_norm", "b_out_norm", "ln1_g", "ln1_b", "ln2_g", "ln2_b")
SMALL_SIZES = (256, 128, 512, 512, 1024, 1024, 1024, 1024)
SMALL_ROWS = 176


def _params(sem):
    return pltpu.CompilerParams(dimension_semantics=sem, vmem_limit_bytes=VMEM_LIMIT_BYTES)


FAST_STRIDE = 4


def _interleave(parts, scratch):
    d, (n, width) = len(parts), parts[0].shape
    if d > FAST_STRIDE and d % FAST_STRIDE == 0:
        inner = [_interleave(parts[r::FAST_STRIDE], scratch) for r in range(FAST_STRIDE)]
        return _interleave(inner, scratch)
    for r in range(d):
        for s in range(width // LANES):
            scratch.at[s][pl.ds(r, n, stride=d), :] = parts[r][:, s * LANES:(s + 1) * LANES]
    return jnp.concatenate([scratch.at[s][pl.ds(0, n * d), :] for s in range(width // LANES)], axis=1)


def _deinterleave(x, d, scratch):
    n, width = x.shape[0] // d, x.shape[1]
    if d > FAST_STRIDE and d % FAST_STRIDE == 0:
        outer = _deinterleave(x, FAST_STRIDE, scratch)
        inner = [_deinterleave(y, d // FAST_STRIDE, scratch) for y in outer]
        return [inner[r % FAST_STRIDE][r // FAST_STRIDE] for r in range(d)]
    for s in range(width // LANES):
        scratch.at[s][pl.ds(0, n * d), :] = x[:, s * LANES:(s + 1) * LANES]
    return [jnp.concatenate([scratch.at[s][pl.ds(r, n, stride=d), :] for s in range(width // LANES)], axis=1) for r in range(d)]


def _rowwise(name, fn, ins, outs, consts=(), reds=(), ts=512, in_dils=None, out_dils=None, comm=None):
    in_dils = list(in_dils or [1] * len(ins))
    out_dils = list(out_dils or [1] * len(outs))
    n_rows = ins[0].shape[-2] * in_dils[0]
    ts = min(ts, n_rows)
    assert n_rows % ts == 0
    n_in, n_c, n_o, n_r = len(ins), len(consts), len(outs), len(reds)
    viewed = [(a.shape, d) for a, d in zip(list(ins) + list(outs), in_dils + out_dils) if d > 1]
    grid = (n_rows // ts,)
    h_ins, h_in_specs, h_outs, h_out_specs, begin, end = _hosted(comm, grid)
    n_hi, n_ho = len(h_ins), len(h_outs)

    def tile_spec(shape, d=1):
        if len(shape) == 2:
            return pl.BlockSpec((ts // d, shape[1]), lambda i: (i, 0))
        return pl.BlockSpec((shape[0], ts, shape[2]), lambda i: (0, i, 0))

    def whole_spec(shape):
        return pl.BlockSpec(shape, lambda i: (0,) * len(shape))

    def body(*refs):
        in_refs, c_refs, hi_refs = refs[:n_in], refs[n_in:n_in + n_c], refs[n_in + n_c:n_in + n_c + n_hi]
        refs = refs[n_in + n_c + n_hi:]
        o_refs, r_refs, ho_refs = refs[:n_o], refs[n_o:n_o + n_r], refs[n_o + n_r:n_o + n_r + n_ho]
        scratches = list(refs[n_o + n_r + n_ho:])
        sems = [scratches.pop(), scratches.pop()][::-1] if comm else []
        begin(hi_refs, ho_refs, sems)
        vals = []
        for r, d in zip(in_refs, in_dils):
            if d == 1:
                vals.append(r[...])
            else:
                width = r.shape[1] // d
                vals.append(_interleave([r[:, k * width:(k + 1) * width] for k in range(d)], scratches.pop(0)))
        res = fn(*vals, *[r[...] for r in c_refs])
        res = tuple(res) if isinstance(res, (tuple, list)) else (res,)
        for r, v, d in zip(o_refs, res[:n_o], out_dils):
            if len(r.shape) == 3:
                for g in range(r.shape[0]):
                    r[g] = v[g].astype(r.dtype)
            elif d == 1:
                r[...] = v.astype(r.dtype)
            else:
                width = r.shape[1] // d
                for k, part in enumerate(_deinterleave(v, d, scratches.pop(0))):
                    r[:, k * width:(k + 1) * width] = part.astype(r.dtype)
        if n_r:
            i = pl.program_id(0)

            @pl.when(i == 0)
            def _():
                for r, v in zip(r_refs, res[n_o:]):
                    r[...] = v

            @pl.when(i > 0)
            def _():
                for r, v in zip(r_refs, res[n_o:]):
                    r[...] += v
        end(hi_refs, ho_refs, sems)

    out = pl.pallas_call(
        body, name=name, grid=grid,
        in_specs=[tile_spec(a.shape, d) for a, d in zip(ins, in_dils)] + [whole_spec(c.shape) for c in consts] + h_in_specs,
        out_specs=[tile_spec(o.shape, d) for o, d in zip(outs, out_dils)] + [whole_spec(r.shape) for r in reds] + h_out_specs,
        out_shape=list(outs) + list(reds) + h_outs,
        scratch_shapes=[pltpu.VMEM((shape[1] // d // LANES, ts, LANES), F32) for shape, d in viewed]
        + (comm.sem_scratch() if comm else []),
        compiler_params=_params(("arbitrary",)),
    )(*ins, *consts, *h_ins)
    return out


def _sds(shape, dtype=F32):
    return jax.ShapeDtypeStruct(tuple(shape), dtype)


def _hosted(comm, grid):
    if comm is None:
        return [], [], [], [], (lambda *a: None), (lambda *a: None)

    def edge(which, at, ins, outs, sems):
        ids = [pl.program_id(d) for d in range(len(grid))]
        hit = ids[0] == at[0]
        for i, v in zip(ids[1:], at[1:]):
            hit = hit & (i == v)

        @pl.when(hit)
        def _():
            which(ins, outs, *sems)

    begin = lambda ins, outs, sems: edge(comm.start, [0] * len(grid), ins, outs, sems)
    end = lambda ins, outs, sems: edge(comm.finish, [g - 1 for g in grid], ins, outs, sems)
    return comm.ins, [ANY] * len(comm.ins), comm.out_shapes, [ANY] * len(comm.out_shapes), begin, end


def _mm(name, a, b, out_dtypes, *, ta=False, tb=False, tm=512, tn=512, tk=512, epilogue=None, extras=(), vecs=(),
        comm=None, b_pieces=False, out_pieces=False):
    (k_dim, m_dim) = a.shape if ta else a.shape[::-1]
    if b_pieces:
        b_rows, b_cols = 2 * b.shape[2], N_CHIPS * b.shape[3]
        (n_dim, k2) = (b_rows, b_cols) if tb else (b_cols, b_rows)
    else:
        (n_dim, k2) = b.shape if tb else b.shape[::-1]
    assert k_dim == k2
    tm, tn, tk = min(tm, m_dim), min(tn, n_dim), min(tk, k_dim)
    assert m_dim % tm == 0 and n_dim % tn == 0 and k_dim % tk == 0
    nk = k_dim // tk
    grid = (m_dim // tm, n_dim // tn, nk)
    a_spec = pl.BlockSpec((tk, tm), lambda i, j, k: (k, i)) if ta else pl.BlockSpec((tm, tk), lambda i, j, k: (i, k))
    if b_pieces:
        assert tn == b_rows and tk % b.shape[3] == 0 if tb else (tk, tn) == (b_rows, b.shape[3])
        n_bp = tk // b.shape[3] if tb else 1
        b_spec = pl.BlockSpec((n_bp,) + b.shape[1:], (lambda i, j, k: (k, 0, 0, 0)) if tb else (lambda i, j, k: (j, 0, 0, 0)))
    else:
        b_spec = pl.BlockSpec((tn, tk), lambda i, j, k: (j, k)) if tb else pl.BlockSpec((tk, tn), lambda i, j, k: (k, j))
    assert not out_pieces or (tm == m_dim and len(out_dtypes) == 1)
    dims = (((0 if ta else 1,), (1 if tb else 0,)), ((), ()))
    n_e, n_v, n_o = len(extras), len(vecs), len(out_dtypes)
    c_ins, c_in_specs, c_outs, c_out_specs, begin, end = _hosted(comm, grid)
    n_ci, n_co = len(c_ins), len(c_outs)

    def body(a_ref, b_ref, *refs):
        e_refs, v_refs, ci_refs = refs[:n_e], refs[n_e:n_e + n_v], refs[n_e + n_v:n_e + n_v + n_ci]
        o_refs = refs[n_e + n_v + n_ci:n_e + n_v + n_ci + n_o]
        co_refs = refs[n_e + n_v + n_ci + n_o:n_e + n_v + n_ci + n_o + n_co]
        scratch = refs[n_e + n_v + n_ci + n_o + n_co:]
        sems = scratch[-2:]
        begin(ci_refs, co_refs, sems)
        if b_pieces:
            pc = b.shape[3]
            part = None
            for cc in range(n_bp):
                a_val = a_ref[:, cc * pc:(cc + 1) * pc] if tb else a_ref[...]
                term = lax.dot_general(a_val.astype(BF16), b_ref[cc].reshape(b_rows, pc).astype(BF16), dims,
                                       preferred_element_type=F32)
                part = term if part is None else part + term
        else:
            part = lax.dot_general(a_ref[...].astype(BF16), b_ref[...].astype(BF16), dims, preferred_element_type=F32)

        def finish(acc):
            outs = epilogue(acc, *[r[...] for r in e_refs], *[r[...] for r in v_refs]) if epilogue else (acc,)
            for r, v in zip(o_refs, outs):
                r[...] = (v.reshape(2, tm // 2, tn) if out_pieces else v).astype(r.dtype)

        if nk == 1:
            finish(part)
        else:
            acc_ref = scratch[0]
            k = pl.program_id(2)

            @pl.when(k == 0)
            def _():
                acc_ref[...] = part

            @pl.when(k > 0)
            def _():
                acc_ref[...] += part

            @pl.when(k == nk - 1)
            def _():
                finish(acc_ref[...])
        end(ci_refs, co_refs, sems)

    tile = pl.BlockSpec((tm, tn), lambda i, j, k: (i, j))
    if out_pieces:
        out_tile, out_shape = pl.BlockSpec((None, 2, tm // 2, tn), lambda i, j, k: (j, 0, 0, 0)), (n_dim // tn, 2, tm // 2, tn)
    else:
        out_tile, out_shape = tile, (m_dim, n_dim)
    out = pl.pallas_call(
        body, name=name, grid=grid,
        in_specs=[a_spec, b_spec] + [tile] * n_e + [pl.BlockSpec((1, tn), lambda i, j, k: (0, j))] * n_v + c_in_specs,
        out_specs=[out_tile] * n_o + c_out_specs,
        out_shape=[_sds(out_shape, d) for d in out_dtypes] + c_outs,
        scratch_shapes=([pltpu.VMEM((tm, tn), F32)] if nk > 1 else []) + (comm.sem_scratch() if comm else []),
        compiler_params=_params(("arbitrary",) * 3 if comm else ("parallel", "parallel", "arbitrary")),
    )(a, b, *extras, *vecs, *c_ins)
    return out


def _swap_halves(x):
    width = x.shape[1]
    lane = lax.broadcasted_iota(jnp.int32, x.shape, 1)
    return jnp.where((lane % 64) < 32, pltpu.roll(x, width - 32, 1), pltpu.roll(x, 32, 1))


def _rope(x, cos, sin_signed):
    return x * cos + _swap_halves(x) * sin_signed


def _rope_t(d, cos, sin_signed):
    return d * cos - _swap_halves(d) * sin_signed


def _rms(x, g):
    rstd = lax.rsqrt(jnp.mean(x * x, axis=-1, keepdims=True) + RMS_EPS)
    return x * rstd * g


def _rms_bwd(x, g, dy):
    rstd = lax.rsqrt(jnp.mean(x * x, axis=-1, keepdims=True) + RMS_EPS)
    xh = x * rstd
    dyg = dy * g
    dx = rstd * (dyg - xh * jnp.mean(dyg * xh, axis=-1, keepdims=True))
    return dx, jnp.sum(dy * xh, axis=0, keepdims=True)


def _layer_norm(r, g, b):
    mu = jnp.mean(r, axis=-1, keepdims=True)
    xc = r - mu
    var = jnp.mean(xc * xc, axis=-1, keepdims=True)
    return xc * lax.rsqrt(var + LN_EPS) * g + b


def _dot(a, b, dims):
    return lax.dot_general(a, b, (dims, ((), ())), preferred_element_type=F32)


_NN, _NT, _TN = ((1,), (0,)), ((1,), (1,)), ((0,), (0,))


def _in_proj(x_bf, w_in_p, cos, sin):
    n_rows = x_bf.shape[0]
    ts = 512
    dils = [dil for _, dil in PATTERNS]
    n_p = len(dils)

    def body(x_ref, w_ref, cos_ref, sin_ref, *refs):
        view_refs, (cq_ref, kpe_ref, ckv_ref, scratch) = refs[:n_p], refs[n_p:]
        xb = x_ref[...]
        cos_v, sin_v = cos_ref[...], sin_ref[...]

        def seg(lo, hi):
            return _dot(xb, w_ref[:, lo:hi], _NN)

        def put(part, val):
            for ref, dil in zip(view_refs, dils):
                pieces = [val] if dil == 1 else _deinterleave(val, dil, scratch)
                for r, piece in enumerate(pieces):
                    lo = (3 * r + part) * A_WIDTH
                    ref[:, lo:lo + A_WIDTH] = piece.astype(BF16)

        put(0, _rope(seg(0, 512), cos_v, sin_v) * A_SCALE)
        put(1, _rope(seg(512, 1024), cos_v, sin_v))
        put(2, seg(1024, 1536))
        cq_ref[...] = seg(1536, 1792)
        kpe_ref[...] = _rope(seg(1792, 1920), cos_v[:, :LANES], sin_v[:, :LANES]).astype(BF16)
        ckv_ref[...] = seg(1920, 2048)

    row = lambda c, d=1: pl.BlockSpec((ts // d, c * d), lambda i: (i, 0))
    return pl.pallas_call(
        body, name="in_proj", grid=(n_rows // ts,),
        in_specs=[row(D_MODEL), pl.BlockSpec((D_MODEL, IN_PAD), lambda i: (0, 0)), row(A_WIDTH), row(A_WIDTH)],
        out_specs=[row(3 * A_WIDTH, d) for d in dils] + [row(Q_LORA), row(LANES), row(KV_LORA)],
        out_shape=[_sds((n_rows // d, d * 3 * A_WIDTH), BF16) for d in dils]
        + [_sds((n_rows, Q_LORA)), _sds((n_rows, LANES), BF16), _sds((n_rows, KV_LORA))],
        scratch_shapes=[pltpu.VMEM((A_WIDTH // LANES, ts, LANES), F32)],
        compiler_params=_params(("parallel",)),
    )(x_bf, w_in_p, cos, sin)


def _band_mask(m, heads):
    qi = lax.broadcasted_iota(jnp.int32, (heads * SPAN, 2 * SPAN), 0) % SPAN
    kj = lax.broadcasted_iota(jnp.int32, (heads * SPAN, 2 * SPAN), 1)
    return ((kj < SPAN) & (kj >= qi) & (m > 0)) | ((kj >= SPAN) & ((kj - SPAN) <= qi))


def _even_lanes(rows):
    return lax.broadcasted_iota(jnp.int32, (rows, LANES), 1) < A_HEAD_DIM


DIL_GROUP = 4
HEAD_SLOT = LANES // A_HEADS
OL_WIDTH = A_WIDTH + LANES


def _dil_fwd(qkv_view, dil, comm=None):
    nb = qkv_view.shape[0] // SPAN
    group = min(dil, DIL_GROUP)
    grid = (dil // group, nb)
    c_ins, c_in_specs, c_outs, c_out_specs, begin, end = _hosted(comm, grid)
    n_ci, n_co = len(c_ins), len(c_outs)

    def body(cur_ref, prev_ref, *refs):
        ci_refs, ol_ref, co_refs, sems = refs[:n_ci], refs[n_ci], refs[n_ci + 1:n_ci + 1 + n_co], refs[n_ci + 1 + n_co:]
        begin(ci_refs, co_refs, sems)
        mask = _band_mask(pl.program_id(1), 1)
        even, even2 = _even_lanes(SPAN), _even_lanes(2 * SPAN)
        slot = lax.broadcasted_iota(jnp.int32, (SPAN, LANES), 1) // HEAD_SLOT
        for g in range(group):
            lse_c = jnp.zeros((SPAN, LANES), F32)
            for p in range(A_HEADS // 2):
                q_sl, k_sl, v_sl = (slice((3 * g + t) * A_WIDTH + p * LANES, (3 * g + t) * A_WIDTH + (p + 1) * LANES)
                                    for t in range(3))
                q2 = cur_ref[:, q_sl]
                kcat = jnp.concatenate([prev_ref[:, k_sl], cur_ref[:, k_sl]], axis=0)
                vcat = jnp.concatenate([prev_ref[:, v_sl], cur_ref[:, v_sl]], axis=0)
                zero, one = jnp.zeros_like(q2), jnp.ones_like(vcat)
                res, lses = [], []
                for first in (True, False):
                    qh = jnp.where(even, q2, zero) if first else jnp.where(even, zero, q2)
                    vh = jnp.where(even2, vcat, one) if first else jnp.where(even2, one, vcat)
                    s = jnp.where(mask, _dot(qh, kcat, _NT), NEG)
                    mx = jnp.max(s, axis=-1, keepdims=True)
                    r = _dot(jnp.exp(s - mx).astype(BF16), vh, _NN)
                    den = pltpu.roll(r, A_HEAD_DIM, 1)
                    res.append(r / den)
                    lses.append(mx + jnp.log(den))
                o_lo = OL_WIDTH * g + p * LANES
                ol_ref[:, o_lo:o_lo + LANES] = jnp.where(even, res[0], res[1])
                pair = jnp.where(even, lses[0], lses[1])
                lse_c = jnp.where(slot == 2 * p, pltpu.roll(pair, (2 * p * HEAD_SLOT) % LANES, 1), lse_c)
                lse_c = jnp.where(slot == 2 * p + 1, pltpu.roll(pair, ((2 * p + 1) * HEAD_SLOT - A_HEAD_DIM) % LANES, 1), lse_c)
            ol_ref[:, OL_WIDTH * g + A_WIDTH:OL_WIDTH * (g + 1)] = lse_c
        end(ci_refs, co_refs, sems)

    return pl.pallas_call(
        body, name=f"dil_fwd_d{dil}", grid=grid,
        in_specs=[pl.BlockSpec((SPAN, group * 3 * A_WIDTH), lambda r, m: (m, r)),
                  pl.BlockSpec((SPAN, group * 3 * A_WIDTH), lambda r, m: (jnp.maximum(m - 1, 0), r))] + c_in_specs,
        out_specs=[pl.BlockSpec((SPAN, group * OL_WIDTH), lambda r, m: (m, r))] + c_out_specs,
        out_shape=[_sds((qkv_view.shape[0], dil * OL_WIDTH))] + c_outs,
        scratch_shapes=comm.sem_scratch() if comm else [],
        compiler_params=_params(("arbitrary", "arbitrary") if comm else ("parallel", "arbitrary")),
    )(qkv_view, qkv_view, *c_ins)


def _slot_spread_matrix():
    i = lax.broadcasted_iota(jnp.int32, (LANES, A_WIDTH), 0)
    j = lax.broadcasted_iota(jnp.int32, (LANES, A_WIDTH), 1)
    return (i == HEAD_SLOT * (j // A_HEAD_DIM)).astype(BF16)


def _spread_slots(c, spread):
    hi = c.astype(BF16)
    r1 = c - hi.astype(F32)
    mid = r1.astype(BF16)
    lo = (r1 - mid.astype(F32)).astype(BF16)
    return _dot(hi, spread, _NN) + _dot(mid, spread, _NN) + _dot(lo, spread, _NN)


def _pattern_weights(ol1, ol2, ol3):
    l1, l2, l3 = ol1[:, A_WIDTH:], ol2[:, A_WIDTH:], ol3[:, A_WIDTH:]
    mx = jnp.maximum(jnp.maximum(l1, l2), l3)
    e1, e2, e3 = jnp.exp(l1 - mx), jnp.exp(l2 - mx), jnp.exp(l3 - mx)
    inv = 1.0 / (e1 + e2 + e3)
    return e1 * inv, e2 * inv, e3 * inv


def _mix_fwd(ols, b_out, a_g, b_g):
    def fn(ol1, ol2, ol3, b, ag, bg, spread):
        ws = [_spread_slots(w, spread) for w in _pattern_weights(ol1, ol2, ol3)]
        a = ws[0] * ol1[:, :A_WIDTH] + ws[1] * ol2[:, :A_WIDTH] + ws[2] * ol3[:, :A_WIDTH]
        return a, jnp.concatenate([_rms(a, ag), _rms(b, bg)], axis=1)

    n_rows = b_out.shape[0]
    return _rowwise("mix_fwd", fn, [*ols, b_out], [_sds((n_rows, A_WIDTH)), _sds((n_rows, 2 * A_WIDTH), BF16)],
                    consts=[a_g, b_g, _slot_spread_matrix()], in_dils=[dil for _, dil in PATTERNS] + [1])


def _mla_q_prep(cq, g, w_uq_p, cos, sin):
    def fn(cq_v, cos_v, sin_v, g_v, w_v):
        cqn = _rms(cq_v, g_v).astype(BF16)
        q = _dot(cqn, w_v, _NN)
        qf = [jnp.concatenate([q[:, h * QK_NOPE:(h + 1) * QK_NOPE],
                               _rope(q[:, 512 + h * LANES:512 + (h + 1) * LANES], cos_v[:, :LANES], sin_v[:, :LANES])], axis=1)
              for h in range(MLA_HEADS)]
        return cqn, qf

    n_rows = cq.shape[0]
    return _rowwise("mla_q_prep", fn, [cq, cos, sin], [_sds((n_rows, Q_LORA), BF16), _sds((MLA_HEADS, n_rows, QK_PAD), BF16)],
                    consts=[g, w_uq_p])


def _mla_kv_prep(ckv, kpe, g, w_ukv_p):
    def fn(ckv_v, kpe_v, g_v, w_v):
        ckvn = _rms(ckv_v, g_v).astype(BF16)
        kvv = _dot(ckvn, w_v, _NN)
        kf = [jnp.concatenate([kvv[:, h * QK_NOPE:(h + 1) * QK_NOPE], kpe_v.astype(F32)], axis=1) for h in range(MLA_HEADS)]
        return ckvn, kf, kvv[:, 512:]

    n_rows = ckv.shape[0]
    return _rowwise("mla_kv_prep", fn, [ckv, kpe],
                    [_sds((n_rows, KV_LORA), BF16), _sds((MLA_HEADS, n_rows, QK_PAD), BF16), _sds((n_rows, MLA_WIDTH), BF16)],
                    consts=[g, w_ukv_p])


MLA_FWD_TILES = (1024, 2048)
MLA_BWD_TILES = (1024, 1024)


def _mla_fwd(qf, kf, v, comm=None):
    n_rows = v.shape[0]
    tq, tk = min(MLA_FWD_TILES[0], n_rows), min(MLA_FWD_TILES[1], n_rows)
    nq, nk = n_rows // tq, n_rows // tk
    grid = (MLA_HEADS, nq, nk)
    exp2_scale = MLA_SCALE * math.log2(math.e)
    c_ins, c_in_specs, c_outs, c_out_specs, begin, end = _hosted(comm, grid)
    n_ci, n_co = len(c_ins), len(c_outs)

    def last_k(i):
        return jnp.right_shift(i * tq + tq - 1, int(math.log2(tk)))

    def body(q_ref, k_ref, v_ref, *refs):
        ci_refs, (o_ref, lse_ref), co_refs = refs[:n_ci], refs[n_ci:n_ci + 2], refs[n_ci + 2:n_ci + 2 + n_co]
        m_sc, l_sc, acc_sc = refs[n_ci + 2 + n_co:n_ci + 5 + n_co]
        sems = refs[n_ci + 5 + n_co:]
        i, j = pl.program_id(1), pl.program_id(2)
        begin(ci_refs, co_refs, sems)

        @pl.when(j == 0)
        def _():
            m_sc[...] = jnp.full(m_sc.shape, NEG, F32)
            l_sc[...] = jnp.zeros(l_sc.shape, F32)
            acc_sc[...] = jnp.zeros(acc_sc.shape, F32)

        def step(masked):
            s = _dot(q_ref[...], k_ref[...], _NT)
            if masked:
                row = lax.broadcasted_iota(jnp.int32, (tq, tk), 0) + i * tq
                col = lax.broadcasted_iota(jnp.int32, (tq, tk), 1) + j * tk
                s = jnp.where(col <= row, s, NEG)
            m_prev = m_sc[...]
            m_new = jnp.maximum(m_prev, jnp.max(s, axis=-1, keepdims=True))
            alpha = jnp.exp2((m_prev - m_new) * exp2_scale)
            p = jnp.exp2((s - m_new) * exp2_scale)
            l_sc[...] = alpha * l_sc[...] + jnp.sum(p, axis=-1, keepdims=True)
            acc_sc[...] = alpha * acc_sc[...] + _dot(p.astype(BF16), v_ref[...], _NN)
            m_sc[...] = m_new

        active = j * tk <= i * tq + tq - 1
        crosses = (j + 1) * tk - 1 > i * tq

        @pl.when(active & jnp.logical_not(crosses))
        def _():
            step(False)

        @pl.when(active & crosses)
        def _():
            step(True)

        @pl.when(j == last_k(i))
        def _():
            o_ref[...] = acc_sc[...] / l_sc[...]
            lse_ref[...] = jnp.broadcast_to(m_sc[...] * MLA_SCALE + jnp.log(l_sc[...]), (tq, V_DIM))

        end(ci_refs, co_refs, sems)

    return pl.pallas_call(
        body, name="mla_fwd", grid=grid,
        in_specs=[pl.BlockSpec((None, tq, QK_PAD), lambda h, i, j: (h, i, 0)),
                  pl.BlockSpec((None, tk, QK_PAD), lambda h, i, j: (h, jnp.minimum(j, last_k(i)), 0)),
                  pl.BlockSpec((tk, V_DIM), lambda h, i, j: (jnp.minimum(j, last_k(i)), h))] + c_in_specs,
        out_specs=[pl.BlockSpec((tq, V_DIM), lambda h, i, j: (i, h)), pl.BlockSpec((tq, V_DIM), lambda h, i, j: (i, h))]
        + c_out_specs,
        out_shape=[_sds((n_rows, MLA_WIDTH)), _sds((n_rows, MLA_WIDTH))] + c_outs,
        scratch_shapes=[pltpu.VMEM((tq, 1), F32), pltpu.VMEM((tq, 1), F32), pltpu.VMEM((tq, V_DIM), F32)]
        + (comm.sem_scratch() if comm else []),
        compiler_params=_params(("arbitrary",) * 3 if comm else ("parallel", "parallel", "arbitrary")),
    )(qf, kf, v, *c_ins)


def _mm_res_ln(name, a, w, xres, g, b):
    def epi(acc, xr, g_v, b_v):
        r = ALPHA * xr + acc
        y = _layer_norm(r, g_v, b_v)
        return y, y, r

    return _mm(name, a, w, (F32, BF16, F32), tm=1024, tn=D_MODEL, tk=1024, epilogue=epi, extras=[xres], vecs=[g, b])


def _mm_relu2(x_bf, w, comm=None):
    def epi(acc):
        r = jnp.maximum(acc, 0.0)
        return (r * r,)

    return _mm("ff1", x_bf, w, (BF16,), tm=1024, tn=1024, tk=D_MODEL, epilogue=epi, b_pieces=True, comm=comm)


def _loss_fn(y, t):
    def fn(y_v, t_v):
        d = y_v - t_v
        part = jnp.sum(jnp.sum(d * d, axis=1, keepdims=True), axis=0, keepdims=True)
        return d * (1.0 / D_MODEL), part

    dy, part = _rowwise("loss", fn, [y, t], [_sds(y.shape)], reds=[_sds((1, 1))])
    return part * (0.5 / D_MODEL), dy


def _ln_bwd(name, dy, r, g):
    def fn(dy_v, r_v, g_v):
        mu = jnp.mean(r_v, axis=-1, keepdims=True)
        xc = r_v - mu
        rstd = lax.rsqrt(jnp.mean(xc * xc, axis=-1, keepdims=True) + LN_EPS)
        xh = xc * rstd
        dxh = dy_v * g_v
        dr = rstd * (dxh - jnp.mean(dxh, axis=-1, keepdims=True) - xh * jnp.mean(dxh * xh, axis=-1, keepdims=True))
        return dr, dr, jnp.sum(dy_v * xh, axis=0, keepdims=True), jnp.sum(dy_v, axis=0, keepdims=True)

    return _rowwise(name, fn, [dy, r], [_sds(dy.shape), _sds(dy.shape, BF16)], consts=[g],
                    reds=[_sds((1, D_MODEL)), _sds((1, D_MODEL))])


def _head_sum_matrix():
    i = lax.broadcasted_iota(jnp.int32, (A_WIDTH, LANES), 0) // A_HEAD_DIM
    j = lax.broadcasted_iota(jnp.int32, (A_WIDTH, LANES), 1) // HEAD_SLOT
    return (i == j).astype(BF16)


def _mix_bwd(dmixed, a_out, b_out, ols, a_g, b_g):
    def fn(dm, a, b, ol1, ol2, ol3, ag, bg, sum_mat, spread):
        da, dga = _rms_bwd(a, ag, dm[:, :A_WIDTH])
        db, dgb = _rms_bwd(b, bg, dm[:, A_WIDTH:])
        t = da * a
        t_hi = t.astype(BF16)
        t_lo = (t - t_hi.astype(F32)).astype(BF16)
        tsum = _dot(t_hi, sum_mat, _NN) + _dot(t_lo, sum_mat, _NN)
        tb = db * b
        delta_b = jnp.concatenate(
            [jnp.broadcast_to(jnp.sum(tb[:, h * V_DIM:(h + 1) * V_DIM], axis=-1, keepdims=True), (tb.shape[0], V_DIM))
             for h in range(MLA_HEADS)], axis=1)
        dods = [jnp.concatenate([_spread_slots(w, spread) * da, w * tsum], axis=1) for w in _pattern_weights(ol1, ol2, ol3)]
        return (*dods, db, delta_b, dga, dgb)

    n_rows = a_out.shape[0]
    wide = (n_rows, A_WIDTH)
    dils = [dil for _, dil in PATTERNS]
    return _rowwise("mix_bwd", fn, [dmixed, a_out, b_out, *ols],
                    [_sds((n_rows // d, d * OL_WIDTH)) for d in dils] + [_sds(wide, BF16), _sds(wide)],
                    consts=[a_g, b_g, _head_sum_matrix(), _slot_spread_matrix()],
                    reds=[_sds((1, A_WIDTH)), _sds((1, MLA_WIDTH))], ts=256,
                    in_dils=[1, 1, 1] + dils, out_dils=dils + [1, 1])


def _dil_bwd(qkv_view, ol_view, dod_view, dil, comm=None):
    nb = qkv_view.shape[0] // SPAN
    group = min(dil, DIL_GROUP)
    grid = (dil // group, nb + 1)
    c_ins, c_in_specs, c_outs, c_out_specs, begin, end = _hosted(comm, grid)
    n_ci, n_co = len(c_ins), len(c_outs)

    def body(cur_ref, prev_ref, ol_ref, dod_ref, *refs):
        ci_refs, out_ref, co_refs = refs[:n_ci], refs[n_ci], refs[n_ci + 1:n_ci + 1 + n_co]
        carry, sems = refs[n_ci + 1 + n_co], refs[n_ci + 2 + n_co:]
        m = pl.program_id(1)
        begin(ci_refs, co_refs, sems)

        @pl.when(m == 0)
        def _():
            carry[...] = jnp.zeros(carry.shape, F32)

        @pl.when(m < nb)
        def _():
            mask, even = _band_mask(m, 2), _even_lanes(SPAN)
            for g in range(group):
                for p in range(A_HEADS // 2):
                    q_sl, k_sl, v_sl = (slice((3 * g + t) * A_WIDTH + p * LANES, (3 * g + t) * A_WIDTH + (p + 1) * LANES)
                                        for t in range(3))
                    do_sl = slice(OL_WIDTH * g + p * LANES, OL_WIDTH * g + (p + 1) * LANES)
                    s_lo = OL_WIDTH * g + A_WIDTH + 2 * p * HEAD_SLOT
                    q2, do2 = cur_ref[:, q_sl], dod_ref[:, do_sl].astype(BF16)
                    zero = jnp.zeros_like(q2)
                    qcat = jnp.concatenate([jnp.where(even, q2, zero), jnp.where(even, zero, q2)], axis=0)
                    docat = jnp.concatenate([jnp.where(even, do2, zero), jnp.where(even, zero, do2)], axis=0)
                    kcat = jnp.concatenate([prev_ref[:, k_sl], cur_ref[:, k_sl]], axis=0)
                    vcat = jnp.concatenate([prev_ref[:, v_sl], cur_ref[:, v_sl]], axis=0)
                    lse_c = jnp.concatenate([ol_ref[:, s_lo:s_lo + 1], ol_ref[:, s_lo + HEAD_SLOT:s_lo + HEAD_SLOT + 1]], axis=0)
                    dl_c = jnp.concatenate([dod_ref[:, s_lo:s_lo + 1], dod_ref[:, s_lo + HEAD_SLOT:s_lo + HEAD_SLOT + 1]], axis=0)
                    pr = jnp.exp(jnp.where(mask, _dot(qcat, kcat, _NT), NEG) - lse_c)
                    ds = (pr * (_dot(docat, vcat, _NT) - dl_c)).astype(BF16)
                    dq = _dot(ds, kcat, _NN)
                    dk2 = _dot(ds, qcat, _TN)
                    dv2 = _dot(pr.astype(BF16), docat, _TN)
                    out_ref[:, q_sl] = carry[:, q_sl]
                    out_ref[:, k_sl] = carry[:, k_sl] + dk2[:SPAN]
                    out_ref[:, v_sl] = carry[:, v_sl] + dv2[:SPAN]
                    carry[:, q_sl] = jnp.where(even, dq[:SPAN], dq[SPAN:])
                    carry[:, k_sl] = dk2[SPAN:]
                    carry[:, v_sl] = dv2[SPAN:]

        @pl.when(m == nb)
        def _():
            out_ref[...] = carry[...]

        end(ci_refs, co_refs, sems)

    last = nb - 1

    def cur(width):
        return pl.BlockSpec((SPAN, group * width), lambda r, m: (jnp.minimum(m, last), r))

    def prev(width):
        return pl.BlockSpec((SPAN, group * width), lambda r, m: (jnp.clip(m - 1, 0, last), r))

    return pl.pallas_call(
        body, name=f"dil_bwd_d{dil}", grid=grid,
        in_specs=[cur(3 * A_WIDTH), prev(3 * A_WIDTH), cur(OL_WIDTH), cur(OL_WIDTH)] + c_in_specs,
        out_specs=[prev(3 * A_WIDTH)] + c_out_specs,
        out_shape=[_sds(qkv_view.shape)] + c_outs,
        scratch_shapes=[pltpu.VMEM((SPAN, group * 3 * A_WIDTH), F32)] + (comm.sem_scratch() if comm else []),
        compiler_params=_params(("arbitrary", "arbitrary") if comm else ("parallel", "arbitrary")),
    )(qkv_view, qkv_view, ol_view, dod_view, *c_ins)


def _mla_bwd(qf, kf, v, lse, do, delta, comm=None):
    n_rows = v.shape[0]
    tq, tk = min(MLA_BWD_TILES[0], n_rows), min(MLA_BWD_TILES[1], n_rows)
    nq, nk = n_rows // tq, n_rows // tk

    def first_q(j):
        return jnp.right_shift(j * tk, int(math.log2(tq)))

    grid = (MLA_HEADS, nk, nq)
    c_ins, c_in_specs, c_outs, c_out_specs, begin, end = _hosted(comm, grid)
    n_ci, n_co = len(c_ins), len(c_outs)

    def body(q_ref, k_ref, v_ref, lse_ref, do_ref, dl_ref, *refs):
        ci_refs, (dq_ref, dk_ref, dv_ref), co_refs = refs[:n_ci], refs[n_ci:n_ci + 3], refs[n_ci + 3:n_ci + 3 + n_co]
        dk_sc, dv_sc = refs[n_ci + 3 + n_co:n_ci + 5 + n_co]
        sems = refs[n_ci + 5 + n_co:]
        j, i = pl.program_id(1), pl.program_id(2)
        begin(ci_refs, co_refs, sems)

        @pl.when((j == 0) & (i == 0))
        def _():
            dq_ref[...] = jnp.zeros(dq_ref.shape, F32)

        @pl.when(i == first_q(j))
        def _():
            dk_sc[...] = jnp.zeros(dk_sc.shape, F32)
            dv_sc[...] = jnp.zeros(dv_sc.shape, F32)

        def step(masked):
            q, k, dob = q_ref[...], k_ref[...], do_ref[...]
            s = _dot(q, k, _NT) * MLA_SCALE
            if masked:
                row = lax.broadcasted_iota(jnp.int32, (tq, tk), 0) + i * tq
                col = lax.broadcasted_iota(jnp.int32, (tq, tk), 1) + j * tk
                s = jnp.where(col <= row, s, NEG)
            p = jnp.exp(s - lse_ref[:, :1])
            dp = _dot(dob, v_ref[...], _NT)
            ds = (p * (dp - dl_ref[:, :1]) * MLA_SCALE).astype(BF16)
            dv_sc[...] += _dot(p.astype(BF16), dob, _TN)
            dk_sc[...] += _dot(ds, q, _TN)
            rows = pl.ds(pl.multiple_of(i * tq, tq), tq)
            dq_ref[rows, :] += _dot(ds, k, _NN)

        active = i >= first_q(j)
        crosses = (j + 1) * tk - 1 > i * tq

        @pl.when(active & jnp.logical_not(crosses))
        def _():
            step(False)

        @pl.when(active & crosses)
        def _():
            step(True)

        @pl.when(i == nq - 1)
        def _():
            dk_ref[...] = dk_sc[...]
            dv_ref[...] = dv_sc[...]

        end(ci_refs, co_refs, sems)

    qrow = lambda h, j, i: (jnp.maximum(i, first_q(j)), h)
    return pl.pallas_call(
        body, name="mla_bwd", grid=grid,
        in_specs=[pl.BlockSpec((None, tq, QK_PAD), lambda h, j, i: (h, jnp.maximum(i, first_q(j)), 0)),
                  pl.BlockSpec((None, tk, QK_PAD), lambda h, j, i: (h, j, 0)),
                  pl.BlockSpec((tk, V_DIM), lambda h, j, i: (j, h)),
                  pl.BlockSpec((tq, V_DIM), qrow), pl.BlockSpec((tq, V_DIM), qrow), pl.BlockSpec((tq, V_DIM), qrow)]
        + c_in_specs,
        out_specs=[pl.BlockSpec((None, n_rows, QK_PAD), lambda h, j, i: (h, 0, 0)),
                   pl.BlockSpec((None, tk, QK_PAD), lambda h, j, i: (h, j, 0)),
                   pl.BlockSpec((tk, V_DIM), lambda h, j, i: (j, h))] + c_out_specs,
        out_shape=[_sds((MLA_HEADS, n_rows, QK_PAD)), _sds((MLA_HEADS, n_rows, QK_PAD)), _sds((n_rows, MLA_WIDTH))] + c_outs,
        scratch_shapes=[pltpu.VMEM((tk, QK_PAD), F32), pltpu.VMEM((tk, V_DIM), F32)] + (comm.sem_scratch() if comm else []),
        compiler_params=_params(("arbitrary",) * 3 if comm else ("parallel", "arbitrary", "arbitrary")),
    )(qf, kf, v, lse, do, delta, *c_ins)


def _mla_bwd_prep(dqf, dkf, dv, cos, sin):
    def fn(dqf_v, dkf_v, dv_v, cos_v, sin_v):
        dq = jnp.concatenate([dqf_v[h][:, :QK_NOPE] for h in range(MLA_HEADS)]
                             + [_rope_t(dqf_v[h][:, QK_NOPE:], cos_v[:, :LANES], sin_v[:, :LANES]) for h in range(MLA_HEADS)], axis=1)
        dkv = jnp.concatenate([dkf_v[h][:, :QK_NOPE] for h in range(MLA_HEADS)] + [dv_v], axis=1)
        dkpe = dkf_v[0][:, QK_NOPE:] + dkf_v[1][:, QK_NOPE:] + dkf_v[2][:, QK_NOPE:] + dkf_v[3][:, QK_NOPE:]
        return dq, dkv, dkpe

    n_rows = dv.shape[0]
    return _rowwise("mla_bwd_prep", fn, [dqf, dkf, dv, cos, sin],
                    [_sds((n_rows, UQ_PAD), BF16), _sds((n_rows, 1024), BF16), _sds((n_rows, LANES))], ts=256)


def _assemble_dh(dqkvs, dcqn, cq, dckvn, ckv, dkpe, cos, sin, gq, gkv, comm=None):
    def fn(g1, g2, g3, dcqn_v, cq_v, dckvn_v, ckv_v, dkpe_v, cos_v, sin_v, gq_v, gkv_v):
        g = g1 + g2 + g3
        dqa = _rope_t(g[:, :A_WIDTH], cos_v, sin_v) * A_SCALE
        dka = _rope_t(g[:, A_WIDTH:2 * A_WIDTH], cos_v, sin_v)
        dcq, dgq = _rms_bwd(cq_v, gq_v, dcqn_v)
        dckv, dgkv = _rms_bwd(ckv_v, gkv_v, dckvn_v)
        dkr = _rope_t(dkpe_v, cos_v[:, :LANES], sin_v[:, :LANES])
        return jnp.concatenate([dqa, dka, g[:, 2 * A_WIDTH:], dcq, dkr, dckv], axis=1), dgq, dgkv

    n_rows = cq.shape[0]
    return _rowwise("assemble_dh", fn, [*dqkvs, dcqn, cq, dckvn, ckv, dkpe, cos, sin],
                    [_sds((n_rows, IN_PAD), BF16)], consts=[gq, gkv],
                    reds=[_sds((1, Q_LORA)), _sds((1, KV_LORA))], ts=256,
                    in_dils=[dil for _, dil in PATTERNS] + [1] * 7, comm=comm)


def _layer_fwd(x_f32, x_bf, w, sm, cos, sin, behind=(None, None), rest=None):
    *qkv_views, cq, kpe, ckv = _in_proj(x_bf, w["w_in"], cos, sin)
    ol_views, brought = [], []
    for i, (view, (_, dil)) in enumerate(zip(qkv_views, PATTERNS)):
        ol, *got = _dil_fwd(view, dil, comm=rest[0][i] if rest else None)
        ol_views.append(ol)
        brought.append(got)
    if rest:
        w = {**w, **rest[1](brought)}
    cqn, qf = _mla_q_prep(cq, sm["q_a_norm"], w["w_uq"], cos, sin)
    ckvn, kf, v = _mla_kv_prep(ckv, kpe, sm["kv_a_norm"], w["w_ukv"])
    b_out, b_lse, *behind_attn = _mla_fwd(qf, kf, v, comm=behind[0])
    a_out, mixed = _mix_fwd(ol_views, b_out, sm["a_out_norm"], sm["b_out_norm"])
    x1, x1_bf, r1 = _mm_res_ln("wo_ln1", mixed, w["w_o"], x_f32, sm["ln1_g"], sm["ln1_b"])
    f, *behind_ff1 = _mm_relu2(x1_bf, w["w_ff1"], comm=behind[1])
    x2, x2_bf, r2 = _mm_res_ln("ff2_ln2", f, w["w_ff2"], x1, sm["ln2_g"], sm["ln2_b"])
    saved = dict(w=w, x_bf=x_bf, qkv_views=qkv_views, ol_views=ol_views, cq=cq, ckv=ckv, cqn=cqn, qf=qf, ckvn=ckvn,
                 kf=kf, v=v, b_out=b_out, b_lse=b_lse, a_out=a_out, mixed=mixed, x1_bf=x1_bf, r1=r1, f=f, r2=r2)
    return x2, x2_bf, saved, (behind_attn, behind_ff1)


def _layer_bwd(dx2, w, sm, sv, cos, sin, reduce_of, pending=None):
    dr2, dr2_bf, dg2, db2 = _ln_bwd("ln2_bwd", dx2, sv["r2"], sm["ln2_g"])
    (dw_ff2,) = _mm("dw_ff2", sv["f"], dr2_bf, (F32,), ta=True, tm=1024, tn=1024, tk=2048)
    du, *swapped = _mm("d_u", dr2_bf, w["w_ff2"], (BF16,), tb=True, tm=1024, tn=2048, tk=D_MODEL,
                       epilogue=lambda acc, f: (acc * (2.0 * jnp.sqrt(f.astype(F32))),), extras=[sv["f"]],
                       comm=_swap_comm(pending.pieces) if pending else None)
    if pending:
        pending.after_swap(swapped)
    (dw_ff1,) = _mm("dw_ff1", sv["x1_bf"], du, (F32,), ta=True, tm=1024, tn=1024, tk=2048, out_pieces=True)
    (dy1,) = _mm("d_x1", du, w["w_ff1"], (F32,), tb=True, tm=1024, tn=1024, tk=2048,
                 epilogue=lambda acc, d: (acc + ALPHA * d,), extras=[dr2], b_pieces=True)
    dr1, dr1_bf, dg1, db1 = _ln_bwd("ln1_bwd", dy1, sv["r1"], sm["ln1_g"])
    (dw_o,) = _mm("dw_o", sv["mixed"], dr1_bf, (F32,), ta=True, tm=1024, tn=1024, tk=2048)
    early = reduce_of(dict(w_o=dw_o, w_ff1=dw_ff1, w_ff2=dw_ff2))
    (dmixed,) = _mm("d_mixed", dr1_bf, w["w_o"], (F32,), tb=True, tm=1024, tn=1024, tk=D_MODEL)
    dod1, dod2, dod3, do_b, delta_b, dga, dgb = _mix_bwd(
        dmixed, sv["a_out"], sv["b_out"], sv["ol_views"], sm["a_out_norm"], sm["b_out_norm"])
    dqkvs = []
    for i, ((_, dil), qkv_view, ol_view, dod_view) in enumerate(zip(PATTERNS, sv["qkv_views"], sv["ol_views"], (dod1, dod2, dod3))):
        swap_here = early is not None and i == 0
        dqkv, *swapped = _dil_bwd(qkv_view, ol_view, dod_view, dil, comm=_swap_comm(early.pieces) if swap_here else None)
        dqkvs.append(dqkv)
        if swap_here:
            early.after_swap(swapped)
    to_scatter = (early.s16 if early else []) + (pending.s16 if pending else [])
    dqf, dkf, dv_b, *scattered = _mla_bwd(sv["qf"], sv["kf"], sv["v"], sv["b_lse"], do_b, delta_b,
                                          comm=_scatter_comm(to_scatter) if to_scatter else None)
    if early:
        early.after_scatter(scattered[:len(early.s16)])
    if pending:
        pending.after_scatter(scattered[len(scattered) - len(pending.s16):])
    dq_b, dkv_b, dkpe = _mla_bwd_prep(dqf, dkf, dv_b, cos, sin)
    (dw_uq,) = _mm("dw_uq", sv["cqn"], dq_b, (F32,), ta=True, tm=Q_LORA, tn=1024, tk=2048)
    (dcqn,) = _mm("d_cqn", dq_b, w["w_uq"], (F32,), tb=True, tm=1024, tn=Q_LORA, tk=UQ_PAD)
    (dw_ukv,) = _mm("dw_ukv", sv["ckvn"], dkv_b, (F32,), ta=True, tm=KV_LORA, tn=1024, tk=2048)
    (dckvn,) = _mm("d_ckvn", dkv_b, w["w_ukv"], (F32,), tb=True, tm=1024, tn=KV_LORA, tk=1024)
    to_share = (early.mine if early else []) + (pending.mine if pending else [])
    dh, dgq, dgkv, *shared = _assemble_dh(dqkvs, dcqn, sv["cq"], dckvn, sv["ckv"], dkpe, cos, sin,
                                          sm["q_a_norm"], sm["kv_a_norm"], comm=_share_comm(to_share) if to_share else None)
    if early:
        early.after_share(shared[:len(early.mine)])
    if pending:
        pending.after_share(shared[len(shared) - len(pending.mine):])
    (dw_in,) = _mm("dw_in", sv["x_bf"], dh, (F32,), ta=True, tm=1024, tn=1024, tk=2048)
    (dx,) = _mm("d_x", dh, w["w_in"], (F32,), tb=True, tm=1024, tn=1024, tk=2048,
                epilogue=lambda acc, d: (acc + ALPHA * d,), extras=[dr1])
    late = reduce_of(dict(w_in=dw_in, w_uq=dw_uq, w_ukv=dw_ukv))
    dsm = dict(q_a_norm=dgq, kv_a_norm=dgkv, a_out_norm=dga, b_out_norm=dgb, ln1_g=dg1, ln1_b=db1, ln2_g=dg2, ln2_b=db2)
    return dx, dsm, late


def _pad_w_in(w):
    return jnp.concatenate([w[:, :1792], w[:, 1920:1984], jnp.zeros((w.shape[0], 64), w.dtype), w[:, 1792:1920]], axis=1)


def _unpad_w_in(w):
    return jnp.concatenate([w[:, :1792], w[:, 1920:2048], w[:, 1792:1856]], axis=1)


def _rope_tables(n_rows):
    half = A_HEAD_DIM // 2
    inv_freq = ROPE_THETA ** (-jnp.arange(half, dtype=F32) / half)
    ang = jnp.arange(n_rows, dtype=F32)[:, None] * inv_freq[None, :]
    cos = jnp.tile(jnp.cos(ang), (1, 2 * A_HEADS))
    sin = jnp.tile(jnp.concatenate([-jnp.sin(ang), jnp.sin(ang)], axis=1), (1, A_HEADS))
    return cos, sin


W_NAMES = ("w_in", "w_uq", "w_ukv", "w_o", "w_ff1", "w_ff2")
PIECE = dict(w_in=(512, 512), w_uq=(128, 256), w_ukv=(64, 256), w_o=(128, 1024), w_ff1=(512, 1024), w_ff2=(512, 1024))
SHARD_COLS = dict(w_in=496, w_uq=192, w_ukv=256, w_o=1024, w_ff1=1024, w_ff2=1024)


def _col_pieces(t, pad_to):
    rows, _, cols = t.shape
    t = jnp.pad(t, ((0, 0), (0, 0), (0, pad_to - cols)))
    return t.reshape(2, rows // 2, N_CHIPS, pad_to).transpose(2, 0, 1, 3)


def _grad_pieces(dws):
    def heads(t, rows):
        return jnp.concatenate([t[:, :512].reshape(rows, MLA_HEADS, LANES), t[:, 512:].reshape(rows, MLA_HEADS, LANES)], axis=2)

    make = dict(
        w_in=lambda t: _col_pieces(_unpad_w_in(t).reshape(D_MODEL, N_CHIPS, SHARD_COLS["w_in"]), 512),
        w_uq=lambda t: _col_pieces(heads(t, Q_LORA), 256),
        w_ukv=lambda t: _col_pieces(heads(t, KV_LORA), 256),
        w_o=lambda t: t.reshape(N_CHIPS, 2, 128, D_MODEL),
        w_ff1=lambda t: t,
        w_ff2=lambda t: t.reshape(N_CHIPS, 2, 512, D_MODEL))
    return {n: make[n](t) for n, t in dws.items()}


def _weights_from_pieces(p):
    def cols(t):
        return t.transpose(1, 2, 0, 3).reshape(2 * t.shape[2], N_CHIPS, t.shape[3])

    def heads(t, rows):
        return jnp.concatenate([t[:, :, :LANES].reshape(rows, 512), t[:, :, LANES:].reshape(rows, 512)], axis=1)

    make = dict(
        w_in=lambda t: _pad_w_in(cols(t)[:, :, :SHARD_COLS["w_in"]].reshape(D_MODEL, IN_COLS)),
        w_uq=lambda t: heads(cols(t), Q_LORA),
        w_ukv=lambda t: heads(cols(t), KV_LORA),
        w_o=lambda t: t.reshape(D_MODEL, D_MODEL),
        w_ff1=lambda t: t,
        w_ff2=lambda t: t.reshape(D_FF, D_MODEL))
    return {n: make[n](t) for n, t in p.items()}


def _my_piece(name, shard, half):
    rows, cols = shard.shape[0] // 2, shard.shape[1]
    t = lax.dynamic_slice_in_dim(shard, half * rows, rows, axis=0).astype(BF16)
    return jnp.pad(t, ((0, 0), (0, PIECE[name][1] - cols)))


MESH = pl.DeviceIdType.MESH
ANY = pl.BlockSpec(memory_space=pl.ANY)


class _Comm:
    def __init__(self, ins, out_shapes, n_sems, start, finish):
        self.ins, self.out_shapes, self.n_sems, self.start, self.finish = list(ins), list(out_shapes), n_sems, start, finish

    def sem_scratch(self):
        return [pltpu.SemaphoreType.DMA((self.n_sems,)), pltpu.SemaphoreType.DMA((self.n_sems,))]


def _comm_call(name, comm):
    n_in, n_out = len(comm.ins), len(comm.out_shapes)

    def body(*refs):
        ins, outs, sems = refs[:n_in], refs[n_in:n_in + n_out], refs[n_in + n_out:]
        comm.start(ins, outs, *sems)
        comm.finish(ins, outs, *sems)

    return pl.pallas_call(body, name=name, out_shape=comm.out_shapes, in_specs=[ANY] * n_in, out_specs=[ANY] * n_out,
                          scratch_shapes=comm.sem_scratch())(*comm.ins)


def _all_gather_comm(blocks):
    n = len(blocks)

    def plan(x_refs, out_refs, send_sems, recv_sems):
        x, y, c = lax.axis_index("x"), lax.axis_index("y"), lax.axis_index("c")
        me, sibling = (x, y, c), (x, y, 1 - c)
        chips = [(1 - x, y), (x, 1 - y), (1 - x, 1 - y)]

        def copy(t, k, blk, to, src=None):
            px, py, pc = blk
            slot = out_refs[t].at[4 * px + 2 * py + pc]
            return pltpu.make_async_remote_copy(
                src_ref=slot if src is None else src, dst_ref=slot,
                send_sem=send_sems.at[7 * t + k], recv_sem=recv_sems.at[7 * t + k], device_id=to, device_id_type=MESH)

        first = []
        for t in range(n):
            first.append(copy(t, 0, me, sibling, src=x_refs[t]))
            first += [copy(t, 1 + j, me, (*chip, c), src=x_refs[t]) for j, chip in enumerate(chips)]
        return me, sibling, chips, c, copy, first

    def start(x_refs, out_refs, send_sems, recv_sems):
        for cp in plan(x_refs, out_refs, send_sems, recv_sems)[-1]:
            cp.start()

    def finish(x_refs, out_refs, send_sems, recv_sems):
        me, sibling, chips, c, copy, sent = plan(x_refs, out_refs, send_sems, recv_sems)
        for j, chip in enumerate(chips):
            for t in range(n):
                copy(t, 1 + j, (*chip, c), me).wait_recv()
                sent.append(copy(t, 4 + j, (*chip, c), sibling))
                sent[-1].start()
        for t in range(n):
            copy(t, 0, sibling, me).wait_recv()
        for j, chip in enumerate(chips):
            for t in range(n):
                copy(t, 4 + j, (*chip, 1 - c), me).wait_recv()
        for cp in sent:
            cp.wait_send()

    return _Comm(blocks, [_sds((N_DEV,) + b.shape, b.dtype) for b in blocks], 7 * n, start, finish)


def _simple_comm(ins, out_shapes, n_sems, copies):
    def start(in_refs, out_refs, send_sems, recv_sems):
        for cp in copies(in_refs, out_refs, send_sems, recv_sems):
            cp.start()

    def finish(in_refs, out_refs, send_sems, recv_sems):
        for cp in copies(in_refs, out_refs, send_sems, recv_sems):
            cp.wait()

    return _Comm(ins, out_shapes, n_sems, start, finish)


def _swap_comm(gs):
    def copies(g_refs, got_refs, send_sems, recv_sems):
        c = lax.axis_index("c")
        sibling = (lax.axis_index("x"), lax.axis_index("y"), 1 - c)
        return [pltpu.make_async_remote_copy(src_ref=g_refs[t].at[k, 1 - c], dst_ref=got_refs[t].at[k],
                                             send_sem=send_sems.at[N_CHIPS * t + k], recv_sem=recv_sems.at[N_CHIPS * t + k],
                                             device_id=sibling, device_id_type=MESH)
                for t in range(len(gs)) for k in range(N_CHIPS)]

    return _simple_comm(gs, [_sds((N_CHIPS,) + g.shape[2:], g.dtype) for g in gs], N_CHIPS * len(gs), copies)


def _scatter_comm(hs):
    def copies(h_refs, got_refs, send_sems, recv_sems):
        x, y, c = lax.axis_index("x"), lax.axis_index("y"), lax.axis_index("c")
        chips = [(1 - x, y), (x, 1 - y), (1 - x, 1 - y)]
        return [pltpu.make_async_remote_copy(src_ref=h_refs[t].at[2 * px + py], dst_ref=got_refs[t].at[rel],
                                             send_sem=send_sems.at[3 * t + rel], recv_sem=recv_sems.at[3 * t + rel],
                                             device_id=(px, py, c), device_id_type=MESH)
                for t in range(len(hs)) for rel, (px, py) in enumerate(chips)]

    return _simple_comm(hs, [_sds((3,) + h.shape[1:], h.dtype) for h in hs], 3 * len(hs), copies)


def _share_comm(rs):
    def copies(r_refs, got_refs, send_sems, recv_sems):
        sibling = (lax.axis_index("x"), lax.axis_index("y"), 1 - lax.axis_index("c"))
        return [pltpu.make_async_remote_copy(src_ref=r_refs[t], dst_ref=got_refs[t], send_sem=send_sems.at[t],
                                             recv_sem=recv_sems.at[t], device_id=sibling, device_id_type=MESH)
                for t in range(len(rs))]

    return _simple_comm(rs, [_sds(r.shape, r.dtype) for r in rs], len(rs), copies)


SUM_STEPS = 4


def _pair_sums(name, gs, gots, c_arr):
    n = len(gs)
    dims = [(g.shape[2] // SUM_STEPS, g.shape[3]) for g in gs]

    def body(c_ref, *refs):
        for t in range(n):
            s = refs[t][...] + refs[n + t][...]
            refs[2 * n + t][...] = s
            refs[3 * n + t][...] = s.astype(BF16)

    def part(tr, pc):
        return pl.BlockSpec((None, tr, pc), lambda k, i, c_ref: (k, i, 0))

    def kept(tr, pc):
        return pl.BlockSpec((None, None, tr, pc), lambda k, i, c_ref: (k, c_ref[0], i, 0))

    outs = pl.pallas_call(
        body, name=name,
        out_shape=[_sds((N_CHIPS,) + g.shape[2:]) for g in gs] + [_sds((N_CHIPS,) + g.shape[2:], BF16) for g in gs],
        grid_spec=pltpu.PrefetchScalarGridSpec(
            num_scalar_prefetch=1, grid=(N_CHIPS, SUM_STEPS),
            in_specs=[kept(*d) for d in dims] + [part(*d) for d in dims],
            out_specs=[part(*d) for d in dims] * 2),
        compiler_params=_params(("parallel", "parallel")),
    )(c_arr, *gs, *gots)
    return list(outs[:n]), list(outs[n:])


def _chips_sums(name, h32s, gots, chip_arr):
    n = len(h32s)
    dims = [(h.shape[1] // SUM_STEPS, h.shape[2]) for h in h32s]

    def body(chip_ref, *refs):
        for t in range(n):
            got_ref = refs[n + t]
            refs[2 * n + t][...] = refs[t][...] + got_ref[0] + got_ref[1] + got_ref[2]

    return pl.pallas_call(
        body, name=name, out_shape=[_sds(h.shape[1:]) for h in h32s],
        grid_spec=pltpu.PrefetchScalarGridSpec(
            num_scalar_prefetch=1, grid=(SUM_STEPS,),
            in_specs=[pl.BlockSpec((None, tr, pc), lambda i, chip_ref: (chip_ref[0], i, 0)) for tr, pc in dims]
            + [pl.BlockSpec((3, tr, pc), lambda i, chip_ref: (0, i, 0)) for tr, pc in dims],
            out_specs=[pl.BlockSpec((tr, pc), lambda i, chip_ref: (i, 0)) for tr, pc in dims]),
        compiler_params=_params(("parallel",)),
    )(chip_arr, *h32s, *gots)


class _Reduce:
    def __init__(self, tag, dws, c, chip):
        gp = _grad_pieces(dws)
        self.tag, self.c, self.names, self.pieces = tag, c, list(gp), list(gp.values())
        self.c_arr, self.chip_arr = jnp.reshape(c, (1,)).astype(jnp.int32), jnp.reshape(chip, (1,)).astype(jnp.int32)
        self.shards = None

    def after_swap(self, gots):
        self.s32, self.s16 = _pair_sums(f"rs_pair_sum_{self.tag}", self.pieces, gots, self.c_arr)

    def after_scatter(self, parts):
        self.mine = list(_chips_sums(f"rs_chips_sum_{self.tag}", self.s32, parts, self.chip_arr))

    def after_share(self, theirs):
        self.shards = {}
        for n, a, b in zip(self.names, self.mine, theirs):
            both = jnp.where(self.c == 0, jnp.concatenate([a, b], axis=0), jnp.concatenate([b, a], axis=0))
            self.shards[n] = both[:, :SHARD_COLS[n]]

    def run_alone(self):
        self.after_swap(_comm_call(f"rs_pair_{self.tag}", _swap_comm(self.pieces)))
        self.after_scatter(_comm_call(f"rs_chips_{self.tag}", _scatter_comm(self.s16)))
        self.after_share(_comm_call(f"rs_share_{self.tag}", _share_comm(self.mine)))


class _LayerWeights:
    def __init__(self, shards, c, dev):
        self.shards, self.c, self.dev = shards, c, dev

    def comm(self, l, names=W_NAMES):
        return _all_gather_comm([_my_piece(n, self.shards[l][n], self.c) for n in names])

    def take(self, l, gathered, names=W_NAMES):
        pieces = {}
        for n, g in zip(names, gathered):
            mine = _my_piece(n, self.shards[l][n], self.c)
            pieces[n] = lax.dynamic_update_index_in_dim(g, mine, self.dev, 0).reshape((N_CHIPS, 2) + mine.shape)
        return _weights_from_pieces(pieces)

    def first(self):
        now, later = ("w_in", "w_uq", "w_ukv"), ("w_ff1", "w_ff2", "w_o")
        weights = self.take(0, _comm_call("ag_l0", self.comm(0, now)), now)
        comms = [self.comm(0, (n,)) for n in later]

        def take_rest(results):
            rest = {}
            for n, gathered in zip(later, results):
                rest.update(self.take(0, gathered, (n,)))
            return rest

        return weights, (comms, take_rest)


def _all_reduce_small(vec):
    rows, lanes = vec.shape

    def gather_body(x_ref, out_ref, send_sems, recv_sems):
        x, y, c = lax.axis_index("x"), lax.axis_index("y"), lax.axis_index("c")
        me, sibling = (x, y, c), (x, y, 1 - c)
        chips = [(1 - x, y), (x, 1 - y), (1 - x, 1 - y)]

        def slot(px, py, pc):
            return out_ref.at[4 * px + 2 * py + pc]

        def copy(k, blk, to, src=None):
            return pltpu.make_async_remote_copy(
                src_ref=slot(*blk) if src is None else src, dst_ref=slot(*blk),
                send_sem=send_sems.at[k], recv_sem=recv_sems.at[k], device_id=to, device_id_type=MESH)

        out_ref[4 * x + 2 * y + c] = x_ref[...]
        first = [copy(0, me, sibling, src=x_ref)]
        first += [copy(1 + j, me, (*chip, c), src=x_ref) for j, chip in enumerate(chips)]
        for cp in first:
            cp.start()
        passed = [copy(4 + j, (*chip, c), sibling) for j, chip in enumerate(chips)]
        for j, chip in enumerate(chips):
            copy(1 + j, (*chip, c), me).wait_recv()
            passed[j].start()
        copy(0, sibling, me).wait_recv()
        for j, chip in enumerate(chips):
            copy(4 + j, (*chip, 1 - c), me).wait_recv()
        for cp in first + passed:
            cp.wait_send()

    vmem = pl.BlockSpec(memory_space=pltpu.VMEM)
    allv = pl.pallas_call(
        gather_body, name="small_all_gather", out_shape=_sds((N_DEV, rows, lanes)),
        in_specs=[vmem], out_specs=vmem,
        scratch_shapes=[pltpu.SemaphoreType.DMA((7,)), pltpu.SemaphoreType.DMA((7,))],
    )(vec)

    def sum_body(a_ref, o_ref):
        acc = a_ref[0]
        for d in range(1, N_DEV):
            acc = acc + a_ref[d]
        o_ref[...] = acc

    return pl.pallas_call(sum_body, name="small_sum", out_shape=_sds((rows, lanes)), in_specs=[vmem], out_specs=vmem)(allv)


def _adamw(name, w, g, m, v, ts=512, comm=None):
    def fn(w_v, g_v, m_v, v_v):
        m_n = ADAM_B1 * m_v + (1.0 - ADAM_B1) * g_v
        v_n = ADAM_B2 * v_v + (1.0 - ADAM_B2) * (g_v * g_v)
        m_hat = m_n / (1.0 - ADAM_B1 ** ADAM_STEP)
        v_hat = v_n / (1.0 - ADAM_B2 ** ADAM_STEP)
        delta = -ADAM_LR * (m_hat / (jnp.sqrt(v_hat) + ADAM_EPS) + ADAM_WD * w_v)
        return delta, m_n, v_n

    return _rowwise(name, fn, [w, g, m, v], [_sds(w.shape)] * 3, ts=ts, comm=comm)


def _pack_small(per_layer):
    flat = jnp.concatenate([per_layer[l][n].reshape(-1) for l in range(DEPTH) for n in SMALL_NAMES])
    return jnp.pad(flat, (0, SMALL_ROWS * LANES - flat.shape[0])).reshape(SMALL_ROWS, LANES)


def _unpack_small(packed):
    flat = packed.reshape(-1)
    per = sum(SMALL_SIZES)
    out = {}
    for n, size, off in zip(SMALL_NAMES, SMALL_SIZES, [sum(SMALL_SIZES[:i]) for i in range(len(SMALL_SIZES))]):
        out[n] = jnp.stack([flat[l * per + off:l * per + off + size] for l in range(DEPTH)])
    return out


NEXT_WEIGHTS_BEHIND = (("w_ff1", "w_ff2"), ("w_in", "w_uq", "w_ukv", "w_o"))


def _fwd_bwd(x, target, layer_weights, smalls, on_layer_grads):
    depth = len(smalls)
    cos, sin = _rope_tables(x.shape[0])
    h_f32, h_bf = x, x.astype(BF16)
    saved = []
    w, rest = layer_weights.first()
    for l in range(depth):
        more = l + 1 < depth
        behind = [layer_weights.comm(l + 1, names) if more else None for names in NEXT_WEIGHTS_BEHIND]
        h_f32, h_bf, sv, brought = _layer_fwd(h_f32, h_bf, w, smalls[l], cos, sin, behind, rest)
        saved.append(sv)
        if more:
            w, rest = {}, None
            for names, got in zip(NEXT_WEIGHTS_BEHIND, brought):
                w.update(layer_weights.take(l + 1, got, names))
    loss_part, dh = _loss_fn(h_f32, target)
    small_grads, pending = [None] * depth, None
    for l in reversed(range(depth)):
        dh, small_grads[l], pending = _layer_bwd(dh, saved[l]["w"], smalls[l], saved[l], cos, sin,
                                                 lambda dws, l=l: on_layer_grads(l, dws), pending)
    return loss_part, dh, small_grads, pending


def kernel(x, w_in, q_a_norm, kv_a_norm, w_uq, w_ukv, a_out_norm, b_out_norm, w_o, ln1_g, ln1_b, w_ff1, w_ff2, ln2_g, ln2_b, loss_target, m_w_in, m_q_a_norm, m_kv_a_norm, m_w_uq, m_w_ukv, m_a_out_norm, m_b_out_norm, m_w_o, m_ln1_g, m_ln1_b, m_w_ff1, m_w_ff2, m_ln2_g, m_ln2_b, v_w_in, v_q_a_norm, v_kv_a_norm, v_w_uq, v_w_ukv, v_a_out_norm, v_b_out_norm, v_w_o, v_ln1_g, v_ln1_b, v_w_ff1, v_w_ff2, v_ln2_g, v_ln2_b):
    c = lax.axis_index("c")
    chip = 2 * lax.axis_index("x") + lax.axis_index("y")
    big = dict(w_in=w_in, w_uq=w_uq, w_ukv=w_ukv, w_o=w_o, w_ff1=w_ff1, w_ff2=w_ff2)
    big_m = dict(w_in=m_w_in, w_uq=m_w_uq, w_ukv=m_w_ukv, w_o=m_w_o, w_ff1=m_w_ff1, w_ff2=m_w_ff2)
    big_v = dict(w_in=v_w_in, w_uq=v_w_uq, w_ukv=v_w_ukv, w_o=v_w_o, w_ff1=v_w_ff1, w_ff2=v_w_ff2)
    small = dict(q_a_norm=q_a_norm, kv_a_norm=kv_a_norm, a_out_norm=a_out_norm, b_out_norm=b_out_norm,
                 ln1_g=ln1_g, ln1_b=ln1_b, ln2_g=ln2_g, ln2_b=ln2_b)
    small_m = dict(q_a_norm=m_q_a_norm, kv_a_norm=m_kv_a_norm, a_out_norm=m_a_out_norm, b_out_norm=m_b_out_norm,
                   ln1_g=m_ln1_g, ln1_b=m_ln1_b, ln2_g=m_ln2_g, ln2_b=m_ln2_b)
    small_v = dict(q_a_norm=v_q_a_norm, kv_a_norm=v_kv_a_norm, a_out_norm=v_a_out_norm, b_out_norm=v_b_out_norm,
                   ln1_g=v_ln1_g, ln1_b=v_ln1_b, ln2_g=v_ln2_g, ln2_b=v_ln2_b)

    layer_weights = _LayerWeights([{n: big[n][l] for n in W_NAMES} for l in range(DEPTH)], c, 2 * chip + c)
    smalls = [{n: small[n][l][None, :] for n in SMALL_NAMES} for l in range(DEPTH)]
    reductions = [[] for _ in range(DEPTH)]

    def reduce_layer(l, dws):
        reductions[l].append(_Reduce(f"l{l}_{len(reductions[l])}", dws, c, chip))
        return reductions[l][-1]

    loss_part, grad_x, small_grads, last = _fwd_bwd(x[0], loss_target[0], layer_weights, smalls, reduce_layer)
    loss = lax.psum(loss_part[0, 0], ("x", "y", "c"))
    grad_x = grad_x[None]

    g_small_packed = _all_reduce_small(_pack_small(small_grads))
    pack_in = lambda d: _pack_small([{n: d[n][l] for n in SMALL_NAMES} for l in range(DEPTH)])
    sd, sm_, sv_ = _adamw("adamw_small", pack_in(small), g_small_packed, pack_in(small_m), pack_in(small_v), ts=SMALL_ROWS)
    g_small, d_small, m_small, v_small = (_unpack_small(t) for t in (g_small_packed, sd, sm_, sv_))

    last.run_alone()
    g_big, d_big, m_big, v_big = {}, {}, {}, {}
    for n in W_NAMES:
        g = jnp.stack([next(r.shards[n] for r in reductions[l] if n in r.names) for l in range(DEPTH)])
        shape = g.shape
        flat = lambda t: t.reshape(shape[0] * shape[1], shape[2])
        d_, m_, v_ = _adamw(f"adamw_{n}", flat(big[n]), flat(g), flat(big_m[n]), flat(big_v[n]))
        g_big[n], d_big[n], m_big[n], v_big[n] = g, d_.reshape(shape), m_.reshape(shape), v_.reshape(shape)

    order = ("w_in", "q_a_norm", "kv_a_norm", "w_uq", "w_ukv", "a_out_norm", "b_out_norm", "w_o", "ln1_g", "ln1_b",
             "w_ff1", "w_ff2", "ln2_g", "ln2_b")
    pick = lambda bigd, smalld: [bigd[n] if n in bigd else smalld[n] for n in order]
    return (loss, grad_x, *pick(g_big, g_small), *pick(d_big, d_small), *pick(m_big, m_small), *pick(v_big, v_small))
```

```python
import math

import jax
import jax.numpy as jnp
from jax import lax
from jax.experimental import pallas as pl
from jax.experimental.pallas import tpu as pltpu

F32, BF16 = jnp.float32, jnp.bfloat16

D_MODEL = 1024
DEPTH = 4
A_HEAD_DIM = 64
A_HEADS = 8
A_WIDTH = 512
PATTERNS = ((128, 1), (512, 4), (2048, 16))
SPAN = 128
MLA_HEADS = 4
QK_NOPE = 128
QK_ROPE = 64
V_DIM = 128
Q_LORA = 256
KV_LORA = 128
MLA_WIDTH = 512
QK_PAD = 256
IN_COLS = 1984
IN_PAD = 2048
UQ_PAD = 1024
D_FF = 4096
ROPE_THETA = 10000.0
ALPHA = (2.0 * DEPTH) ** 0.25
LN_EPS = 1e-5
RMS_EPS = 1e-6
MLA_SCALE = (QK_NOPE + QK_ROPE) ** -0.5
A_SCALE = A_HEAD_DIM ** -0.5

ADAM_LR, ADAM_B1, ADAM_B2, ADAM_EPS, ADAM_WD, ADAM_STEP = 0.001, 0.9, 0.999, 1e-08, 0.01, 10

VMEM_LIMIT_BYTES = 56 * 1024 * 1024
NEG = -1e30
N_CHIPS, N_DEV = 4, 8
LANES = 128

SMALL_NAMES = ("q_a_norm", "kv_a_norm", "a_out_norm", "b_out_norm", "ln1_g", "ln1_b", "ln2_g", "ln2_b")
SMALL_SIZES = (256, 128, 512, 512, 1024, 1024, 1024, 1024)
SMALL_ROWS = 176


def _params(sem):
    return pltpu.CompilerParams(dimension_semantics=sem, vmem_limit_bytes=VMEM_LIMIT_BYTES)


FAST_STRIDE = 4


def _interleave(parts, scratch):
    d, (n, width) = len(parts), parts[0].shape
    if d > FAST_STRIDE and d % FAST_STRIDE == 0:
        inner = [_interleave(parts[r::FAST_STRIDE], scratch) for r in range(FAST_STRIDE)]
        return _interleave(inner, scratch)
    for r in range(d):
        for s in range(width // LANES):
            scratch.at[s][pl.ds(r, n, stride=d), :] = parts[r][:, s * LANES:(s + 1) * LANES]
    return jnp.concatenate([scratch.at[s][pl.ds(0, n * d), :] for s in range(width // LANES)], axis=1)


def _deinterleave(x, d, scratch):
    n, width = x.shape[0] // d, x.shape[1]
    if d > FAST_STRIDE and d % FAST_STRIDE == 0:
        outer = _deinterleave(x, FAST_STRIDE, scratch)
        inner = [_deinterleave(y, d // FAST_STRIDE, scratch) for y in outer]
        return [inner[r % FAST_STRIDE][r // FAST_STRIDE] for r in range(d)]
    for s in range(width // LANES):
        scratch.at[s][pl.ds(0, n * d), :] = x[:, s * LANES:(s + 1) * LANES]
    return [jnp.concatenate([scratch.at[s][pl.ds(r, n, stride=d), :] for s in range(width // LANES)], axis=1) for r in range(d)]


def _rowwise(name, fn, ins, outs, consts=(), reds=(), ts=512, in_dils=None, out_dils=None, comm=None):
    in_dils = list(in_dils or [1] * len(ins))
    out_dils = list(out_dils or [1] * len(outs))
    n_rows = ins[0].shape[-2] * in_dils[0]
    ts = min(ts, n_rows)
    assert n_rows % ts == 0
    n_in, n_c, n_o, n_r = len(ins), len(consts), len(outs), len(reds)
    viewed = [(a.shape, d) for a, d in zip(list(ins) + list(outs), in_dils + out_dils) if d > 1]
    grid = (n_rows // ts,)
    h_ins, h_in_specs, h_outs, h_out_specs, begin, end = _hosted(comm, grid)
    n_hi, n_ho = len(h_ins), len(h_outs)

    def tile_spec(shape, d=1):
        if len(shape) == 2:
            return pl.BlockSpec((ts // d, shape[1]), lambda i: (i, 0))
        return pl.BlockSpec((shape[0], ts, shape[2]), lambda i: (0, i, 0))

    def whole_spec(shape):
        return pl.BlockSpec(shape, lambda i: (0,) * len(shape))

    def body(*refs):
        in_refs, c_refs, hi_refs = refs[:n_in], refs[n_in:n_in + n_c], refs[n_in + n_c:n_in + n_c + n_hi]
        refs = refs[n_in + n_c + n_hi:]
        o_refs, r_refs, ho_refs = refs[:n_o], refs[n_o:n_o + n_r], refs[n_o + n_r:n_o + n_r + n_ho]
        scratches = list(refs[n_o + n_r + n_ho:])
        sems = [scratches.pop(), scratches.pop()][::-1] if comm else []
        begin(hi_refs, ho_refs, sems)
        vals = []
        for r, d in zip(in_refs, in_dils):
            if d == 1:
                vals.append(r[...])
            else:
                width = r.shape[1] // d
                vals.append(_interleave([r[:, k * width:(k + 1) * width] for k in range(d)], scratches.pop(0)))
        res = fn(*vals, *[r[...] for r in c_refs])
        res = tuple(res) if isinstance(res, (tuple, list)) else (res,)
        for r, v, d in zip(o_refs, res[:n_o], out_dils):
            if len(r.shape) == 3:
                for g in range(r.shape[0]):
                    r[g] = v[g].astype(r.dtype)
            elif d == 1:
                r[...] = v.astype(r.dtype)
            else:
                width = r.shape[1] // d
                for k, part in enumerate(_deinterleave(v, d, scratches.pop(0))):
                    r[:, k * width:(k + 1) * width] = part.astype(r.dtype)
        if n_r:
            i = pl.program_id(0)

            @pl.when(i == 0)
            def _():
                for r, v in zip(r_refs, res[n_o:]):
                    r[...] = v

            @pl.when(i > 0)
            def _():
                for r, v in zip(r_refs, res[n_o:]):
                    r[...] += v
        end(hi_refs, ho_refs, sems)

    out = pl.pallas_call(
        body, name=name, grid=grid,
        in_specs=[tile_spec(a.shape, d) for a, d in zip(ins, in_dils)] + [whole_spec(c.shape) for c in consts] + h_in_specs,
        out_specs=[tile_spec(o.shape, d) for o, d in zip(outs, out_dils)] + [whole_spec(r.shape) for r in reds] + h_out_specs,
        out_shape=list(outs) + list(reds) + h_outs,
        scratch_shapes=[pltpu.VMEM((shape[1] // d // LANES, ts, LANES), F32) for shape, d in viewed]
        + (comm.sem_scratch() if comm else []),
        compiler_params=_params(("arbitrary",)),
    )(*ins, *consts, *h_ins)
    return out


def _sds(shape, dtype=F32):
    return jax.ShapeDtypeStruct(tuple(shape), dtype)


def _hosted(comm, grid):
    if comm is None:
        return [], [], [], [], (lambda *a: None), (lambda *a: None)

    def edge(which, at, ins, outs, sems):
        ids = [pl.program_id(d) for d in range(len(grid))]
        hit = ids[0] == at[0]
        for i, v in zip(ids[1:], at[1:]):
            hit = hit & (i == v)

        @pl.when(hit)
        def _():
            which(ins, outs, *sems)

    begin = lambda ins, outs, sems: edge(comm.start, [0] * len(grid), ins, outs, sems)
    end = lambda ins, outs, sems: edge(comm.finish, [g - 1 for g in grid], ins, outs, sems)
    return comm.ins, [ANY] * len(comm.ins), comm.out_shapes, [ANY] * len(comm.out_shapes), begin, end


def _mm(name, a, b, out_dtypes, *, ta=False, tb=False, tm=512, tn=512, tk=512, epilogue=None, extras=(), vecs=(),
        comm=None, b_pieces=False, out_pieces=False):
    (k_dim, m_dim) = a.shape if ta else a.shape[::-1]
    if b_pieces:
        b_rows, b_cols = 2 * b.shape[2], N_CHIPS * b.shape[3]
        (n_dim, k2) = (b_rows, b_cols) if tb else (b_cols, b_rows)
    else:
        (n_dim, k2) = b.shape if tb else b.shape[::-1]
    assert k_dim == k2
    tm, tn, tk = min(tm, m_dim), min(tn, n_dim), min(tk, k_dim)
    assert m_dim % tm == 0 and n_dim % tn == 0 and k_dim % tk == 0
    nk = k_dim // tk
    grid = (m_dim // tm, n_dim // tn, nk)
    a_spec = pl.BlockSpec((tk, tm), lambda i, j, k: (k, i)) if ta else pl.BlockSpec((tm, tk), lambda i, j, k: (i, k))
    if b_pieces:
        assert tn == b_rows and tk % b.shape[3] == 0 if tb else (tk, tn) == (b_rows, b.shape[3])
        n_bp = tk // b.shape[3] if tb else 1
        b_spec = pl.BlockSpec((n_bp,) + b.shape[1:], (lambda i, j, k: (k, 0, 0, 0)) if tb else (lambda i, j, k: (j, 0, 0, 0)))
    else:
        b_spec = pl.BlockSpec((tn, tk), lambda i, j, k: (j, k)) if tb else pl.BlockSpec((tk, tn), lambda i, j, k: (k, j))
    assert not out_pieces or (tm == m_dim and len(out_dtypes) == 1)
    dims = (((0 if ta else 1,), (1 if tb else 0,)), ((), ()))
    n_e, n_v, n_o = len(extras), len(vecs), len(out_dtypes)
    c_ins, c_in_specs, c_outs, c_out_specs, begin, end = _hosted(comm, grid)
    n_ci, n_co = len(c_ins), len(c_outs)

    def body(a_ref, b_ref, *refs):
        e_refs, v_refs, ci_refs = refs[:n_e], refs[n_e:n_e + n_v], refs[n_e + n_v:n_e + n_v + n_ci]
        o_refs = refs[n_e + n_v + n_ci:n_e + n_v + n_ci + n_o]
        co_refs = refs[n_e + n_v + n_ci + n_o:n_e + n_v + n_ci + n_o + n_co]
        scratch = refs[n_e + n_v + n_ci + n_o + n_co:]
        sems = scratch[-2:]
        begin(ci_refs, co_refs, sems)
        if b_pieces:
            pc = b.shape[3]
            part = None
            for cc in range(n_bp):
                a_val = a_ref[:, cc * pc:(cc + 1) * pc] if tb else a_ref[...]
                term = lax.dot_general(a_val.astype(BF16), b_ref[cc].reshape(b_rows, pc).astype(BF16), dims,
                                       preferred_element_type=F32)
                part = term if part is None else part + term
        else:
            part = lax.dot_general(a_ref[...].astype(BF16), b_ref[...].astype(BF16), dims, preferred_element_type=F32)

        def finish(acc):
            outs = epilogue(acc, *[r[...] for r in e_refs], *[r[...] for r in v_refs]) if epilogue else (acc,)
            for r, v in zip(o_refs, outs):
                r[...] = (v.reshape(2, tm // 2, tn) if out_pieces else v).astype(r.dtype)

        if nk == 1:
            finish(part)
        else:
            acc_ref = scratch[0]
            k = pl.program_id(2)

            @pl.when(k == 0)
            def _():
                acc_ref[...] = part

            @pl.when(k > 0)
            def _():
                acc_ref[...] += part

            @pl.when(k == nk - 1)
            def _():
                finish(acc_ref[...])
        end(ci_refs, co_refs, sems)

    tile = pl.BlockSpec((tm, tn), lambda i, j, k: (i, j))
    if out_pieces:
        out_tile, out_shape = pl.BlockSpec((None, 2, tm // 2, tn), lambda i, j, k: (j, 0, 0, 0)), (n_dim // tn, 2, tm // 2, tn)
    else:
        out_tile, out_shape = tile, (m_dim, n_dim)
    out = pl.pallas_call(
        body, name=name, grid=grid,
        in_specs=[a_spec, b_spec] + [tile] * n_e + [pl.BlockSpec((1, tn), lambda i, j, k: (0, j))] * n_v + c_in_specs,
        out_specs=[out_tile] * n_o + c_out_specs,
        out_shape=[_sds(out_shape, d) for d in out_dtypes] + c_outs,
        scratch_shapes=([pltpu.VMEM((tm, tn), F32)] if nk > 1 else []) + (comm.sem_scratch() if comm else []),
        compiler_params=_params(("arbitrary",) * 3 if comm else ("parallel", "parallel", "arbitrary")),
    )(a, b, *extras, *vecs, *c_ins)
    return out


def _swap_halves(x):
    width = x.shape[1]
    lane = lax.broadcasted_iota(jnp.int32, x.shape, 1)
    return jnp.where((lane % 64) < 32, pltpu.roll(x, width - 32, 1), pltpu.roll(x, 32, 1))


def _rope(x, cos, sin_signed):
    return x * cos + _swap_halves(x) * sin_signed


def _rope_t(d, cos, sin_signed):
    return d * cos - _swap_halves(d) * sin_signed


def _rms(x, g):
    rstd = lax.rsqrt(jnp.mean(x * x, axis=-1, keepdims=True) + RMS_EPS)
    return x * rstd * g


def _rms_bwd(x, g, dy):
    rstd = lax.rsqrt(jnp.mean(x * x, axis=-1, keepdims=True) + RMS_EPS)
    xh = x * rstd
    dyg = dy * g
    dx = rstd * (dyg - xh * jnp.mean(dyg * xh, axis=-1, keepdims=True))
    return dx, jnp.sum(dy * xh, axis=0, keepdims=True)


def _layer_norm(r, g, b):
    mu = jnp.mean(r, axis=-1, keepdims=True)
    xc = r - mu
    var = jnp.mean(xc * xc, axis=-1, keepdims=True)
    return xc * lax.rsqrt(var + LN_EPS) * g + b


def _dot(a, b, dims):
    return lax.dot_general(a, b, (dims, ((), ())), preferred_element_type=F32)


_NN, _NT, _TN = ((1,), (0,)), ((1,), (1,)), ((0,), (0,))


def _in_proj(x_bf, w_in_p, cos, sin):
    n_rows = x_bf.shape[0]
    ts = 512
    dils = [dil for _, dil in PATTERNS]
    n_p = len(dils)

    def body(x_ref, w_ref, cos_ref, sin_ref, *refs):
        view_refs, (cq_ref, kpe_ref, ckv_ref, scratch) = refs[:n_p], refs[n_p:]
        xb = x_ref[...]
        cos_v, sin_v = cos_ref[...], sin_ref[...]

        def seg(lo, hi):
            return _dot(xb, w_ref[:, lo:hi], _NN)

        def put(part, val):
            for ref, dil in zip(view_refs, dils):
                pieces = [val] if dil == 1 else _deinterleave(val, dil, scratch)
                for r, piece in enumerate(pieces):
                    lo = (3 * r + part) * A_WIDTH
                    ref[:, lo:lo + A_WIDTH] = piece.astype(BF16)

        put(0, _rope(seg(0, 512), cos_v, sin_v) * A_SCALE)
        put(1, _rope(seg(512, 1024), cos_v, sin_v))
        put(2, seg(1024, 1536))
        cq_ref[...] = seg(1536, 1792)
        kpe_ref[...] = _rope(seg(1792, 1920), cos_v[:, :LANES], sin_v[:, :LANES]).astype(BF16)
        ckv_ref[...] = seg(1920, 2048)

    row = lambda c, d=1: pl.BlockSpec((ts // d, c * d), lambda i: (i, 0))
    return pl.pallas_call(
        body, name="in_proj", grid=(n_rows // ts,),
        in_specs=[row(D_MODEL), pl.BlockSpec((D_MODEL, IN_PAD), lambda i: (0, 0)), row(A_WIDTH), row(A_WIDTH)],
        out_specs=[row(3 * A_WIDTH, d) for d in dils] + [row(Q_LORA), row(LANES), row(KV_LORA)],
        out_shape=[_sds((n_rows // d, d * 3 * A_WIDTH), BF16) for d in dils]
        + [_sds((n_rows, Q_LORA)), _sds((n_rows, LANES), BF16), _sds((n_rows, KV_LORA))],
        scratch_shapes=[pltpu.VMEM((A_WIDTH // LANES, ts, LANES), F32)],
        compiler_params=_params(("parallel",)),
    )(x_bf, w_in_p, cos, sin)


def _band_mask(m, heads):
    qi = lax.broadcasted_iota(jnp.int32, (heads * SPAN, 2 * SPAN), 0) % SPAN
    kj = lax.broadcasted_iota(jnp.int32, (heads * SPAN, 2 * SPAN), 1)
    return ((kj < SPAN) & (kj >= qi) & (m > 0)) | ((kj >= SPAN) & ((kj - SPAN) <= qi))


def _even_lanes(rows):
    return lax.broadcasted_iota(jnp.int32, (rows, LANES), 1) < A_HEAD_DIM


DIL_GROUP = 4
HEAD_SLOT = LANES // A_HEADS
OL_WIDTH = A_WIDTH + LANES


def _slot_lane(h):
    return HEAD_SLOT * (h // 2) + A_HEAD_DIM * (h % 2)


def _slot_head(lane):
    return 2 * (jnp.right_shift(lane, 4) & 3) + jnp.right_shift(lane, 6)


def _dil_fwd(qkv_view, dil, comm=None):
    nb = qkv_view.shape[0] // SPAN
    group = min(dil, DIL_GROUP)
    grid = (dil // group, nb)
    c_ins, c_in_specs, c_outs, c_out_specs, begin, end = _hosted(comm, grid)
    n_ci, n_co = len(c_ins), len(c_outs)

    def body(cur_ref, prev_ref, *refs):
        ci_refs, ol_ref, co_refs, sems = refs[:n_ci], refs[n_ci], refs[n_ci + 1:n_ci + 1 + n_co], refs[n_ci + 1 + n_co:]
        begin(ci_refs, co_refs, sems)
        mask = _band_mask(pl.program_id(1), 1)
        even, even2 = _even_lanes(SPAN), _even_lanes(2 * SPAN)
        pair_of_lane = jnp.right_shift(lax.broadcasted_iota(jnp.int32, (SPAN, LANES), 1), 4) & 3
        for g in range(group):
            lse_c = jnp.zeros((SPAN, LANES), F32)
            for p in range(A_HEADS // 2):
                q_sl, k_sl, v_sl = (slice((3 * g + t) * A_WIDTH + p * LANES, (3 * g + t) * A_WIDTH + (p + 1) * LANES)
                                    for t in range(3))
                q2 = cur_ref[:, q_sl]
                kcat = jnp.concatenate([prev_ref[:, k_sl], cur_ref[:, k_sl]], axis=0)
                vcat = jnp.concatenate([prev_ref[:, v_sl], cur_ref[:, v_sl]], axis=0)
                zero, one = jnp.zeros_like(q2), jnp.ones_like(vcat)
                res, lses = [], []
                for first in (True, False):
                    qh = jnp.where(even, q2, zero) if first else jnp.where(even, zero, q2)
                    vh = jnp.where(even2, vcat, one) if first else jnp.where(even2, one, vcat)
                    s = jnp.where(mask, _dot(qh, kcat, _NT), NEG)
                    mx = jnp.max(s, axis=-1, keepdims=True)
                    r = _dot(jnp.exp(s - mx).astype(BF16), vh, _NN)
                    den = pltpu.roll(r, A_HEAD_DIM, 1)
                    res.append(r / den)
                    lses.append(mx + jnp.log(den))
                o_lo = OL_WIDTH * g + p * LANES
                ol_ref[:, o_lo:o_lo + LANES] = jnp.where(even, res[0], res[1])
                lse_c = jnp.where(pair_of_lane == p, jnp.where(even, lses[0], lses[1]), lse_c)
            ol_ref[:, OL_WIDTH * g + A_WIDTH:OL_WIDTH * (g + 1)] = lse_c
        end(ci_refs, co_refs, sems)

    return pl.pallas_call(
        body, name=f"dil_fwd_d{dil}", grid=grid,
        in_specs=[pl.BlockSpec((SPAN, group * 3 * A_WIDTH), lambda r, m: (m, r)),
                  pl.BlockSpec((SPAN, group * 3 * A_WIDTH), lambda r, m: (jnp.maximum(m - 1, 0), r))] + c_in_specs,
        out_specs=[pl.BlockSpec((SPAN, group * OL_WIDTH), lambda r, m: (m, r))] + c_out_specs,
        out_shape=[_sds((qkv_view.shape[0], dil * OL_WIDTH))] + c_outs,
        scratch_shapes=comm.sem_scratch() if comm else [],
        compiler_params=_params(("arbitrary", "arbitrary") if comm else ("parallel", "arbitrary")),
    )(qkv_view, qkv_view, *c_ins)


def _slot_spread_matrix():
    i = lax.broadcasted_iota(jnp.int32, (LANES, A_WIDTH), 0)
    h = lax.broadcasted_iota(jnp.int32, (LANES, A_WIDTH), 1) // A_HEAD_DIM
    return (i == HEAD_SLOT * (h // 2) + A_HEAD_DIM * (h % 2)).astype(BF16)


def _spread_slots(c, spread):
    hi = c.astype(BF16)
    r1 = c - hi.astype(F32)
    mid = r1.astype(BF16)
    lo = (r1 - mid.astype(F32)).astype(BF16)
    return _dot(hi, spread, _NN) + _dot(mid, spread, _NN) + _dot(lo, spread, _NN)


def _pattern_weights(ol1, ol2, ol3):
    l1, l2, l3 = ol1[:, A_WIDTH:], ol2[:, A_WIDTH:], ol3[:, A_WIDTH:]
    mx = jnp.maximum(jnp.maximum(l1, l2), l3)
    e1, e2, e3 = jnp.exp(l1 - mx), jnp.exp(l2 - mx), jnp.exp(l3 - mx)
    inv = 1.0 / (e1 + e2 + e3)
    return e1 * inv, e2 * inv, e3 * inv


def _mix_fwd(ols, b_out, a_g, b_g):
    def fn(ol1, ol2, ol3, b, ag, bg, spread):
        ws = [_spread_slots(w, spread) for w in _pattern_weights(ol1, ol2, ol3)]
        a = ws[0] * ol1[:, :A_WIDTH] + ws[1] * ol2[:, :A_WIDTH] + ws[2] * ol3[:, :A_WIDTH]
        return a, jnp.concatenate([_rms(a, ag), _rms(b, bg)], axis=1)

    n_rows = b_out.shape[0]
    return _rowwise("mix_fwd", fn, [*ols, b_out], [_sds((n_rows, A_WIDTH)), _sds((n_rows, 2 * A_WIDTH), BF16)],
                    consts=[a_g, b_g, _slot_spread_matrix()], in_dils=[dil for _, dil in PATTERNS] + [1])


def _mla_q_prep(cq, g, w_uq_p, cos, sin):
    def fn(cq_v, cos_v, sin_v, g_v, w_v):
        cqn = _rms(cq_v, g_v).astype(BF16)
        q = _dot(cqn, w_v, _NN)
        qf = [jnp.concatenate([q[:, h * QK_NOPE:(h + 1) * QK_NOPE],
                               _rope(q[:, 512 + h * LANES:512 + (h + 1) * LANES], cos_v[:, :LANES], sin_v[:, :LANES])], axis=1)
              for h in range(MLA_HEADS)]
        return cqn, qf

    n_rows = cq.shape[0]
    return _rowwise("mla_q_prep", fn, [cq, cos, sin], [_sds((n_rows, Q_LORA), BF16), _sds((MLA_HEADS, n_rows, QK_PAD), BF16)],
                    consts=[g, w_uq_p])


def _mla_kv_prep(ckv, kpe, g, w_ukv_p):
    def fn(ckv_v, kpe_v, g_v, w_v):
        ckvn = _rms(ckv_v, g_v).astype(BF16)
        kvv = _dot(ckvn, w_v, _NN)
        kf = [jnp.concatenate([kvv[:, h * QK_NOPE:(h + 1) * QK_NOPE], kpe_v.astype(F32)], axis=1) for h in range(MLA_HEADS)]
        return ckvn, kf, kvv[:, 512:]

    n_rows = ckv.shape[0]
    return _rowwise("mla_kv_prep", fn, [ckv, kpe],
                    [_sds((n_rows, KV_LORA), BF16), _sds((MLA_HEADS, n_rows, QK_PAD), BF16), _sds((n_rows, MLA_WIDTH), BF16)],
                    consts=[g, w_ukv_p])


MLA_FWD_TILES = (1024, 2048)
MLA_BWD_TILES = (1024, 1024)


def _mla_fwd(qf, kf, v, comm=None):
    n_rows = v.shape[0]
    tq, tk = min(MLA_FWD_TILES[0], n_rows), min(MLA_FWD_TILES[1], n_rows)
    nq, nk = n_rows // tq, n_rows // tk
    grid = (MLA_HEADS, nq, nk)
    exp2_scale = MLA_SCALE * math.log2(math.e)
    c_ins, c_in_specs, c_outs, c_out_specs, begin, end = _hosted(comm, grid)
    n_ci, n_co = len(c_ins), len(c_outs)

    def last_k(i):
        return jnp.right_shift(i * tq + tq - 1, int(math.log2(tk)))

    def body(q_ref, k_ref, v_ref, *refs):
        ci_refs, (o_ref, lse_ref), co_refs = refs[:n_ci], refs[n_ci:n_ci + 2], refs[n_ci + 2:n_ci + 2 + n_co]
        m_sc, l_sc, acc_sc = refs[n_ci + 2 + n_co:n_ci + 5 + n_co]
        sems = refs[n_ci + 5 + n_co:]
        i, j = pl.program_id(1), pl.program_id(2)
        begin(ci_refs, co_refs, sems)

        @pl.when(j == 0)
        def _():
            m_sc[...] = jnp.full(m_sc.shape, NEG, F32)
            l_sc[...] = jnp.zeros(l_sc.shape, F32)
            acc_sc[...] = jnp.zeros(acc_sc.shape, F32)

        def step(masked):
            s = _dot(q_ref[...], k_ref[...], _NT)
            if masked:
                row = lax.broadcasted_iota(jnp.int32, (tq, tk), 0) + i * tq
                col = lax.broadcasted_iota(jnp.int32, (tq, tk), 1) + j * tk
                s = jnp.where(col <= row, s, NEG)
            m_prev = m_sc[...]
            m_new = jnp.maximum(m_prev, jnp.max(s, axis=-1, keepdims=True))
            alpha = jnp.exp2((m_prev - m_new) * exp2_scale)
            p = jnp.exp2((s - m_new) * exp2_scale)
            l_sc[...] = alpha * l_sc[...] + jnp.sum(p, axis=-1, keepdims=True)
            acc_sc[...] = alpha * acc_sc[...] + _dot(p.astype(BF16), v_ref[...], _NN)
            m_sc[...] = m_new

        active = j * tk <= i * tq + tq - 1
        crosses = (j + 1) * tk - 1 > i * tq

        @pl.when(active & jnp.logical_not(crosses))
        def _():
            step(False)

        @pl.when(active & crosses)
        def _():
            step(True)

        @pl.when(j == last_k(i))
        def _():
            o_ref[...] = acc_sc[...] / l_sc[...]
            lse_ref[...] = jnp.broadcast_to(m_sc[...] * MLA_SCALE + jnp.log(l_sc[...]), (tq, V_DIM))

        end(ci_refs, co_refs, sems)

    return pl.pallas_call(
        body, name="mla_fwd", grid=grid,
        in_specs=[pl.BlockSpec((None, tq, QK_PAD), lambda h, i, j: (h, i, 0)),
                  pl.BlockSpec((None, tk, QK_PAD), lambda h, i, j: (h, jnp.minimum(j, last_k(i)), 0)),
                  pl.BlockSpec((tk, V_DIM), lambda h, i, j: (jnp.minimum(j, last_k(i)), h))] + c_in_specs,
        out_specs=[pl.BlockSpec((tq, V_DIM), lambda h, i, j: (i, h)), pl.BlockSpec((tq, V_DIM), lambda h, i, j: (i, h))]
        + c_out_specs,
        out_shape=[_sds((n_rows, MLA_WIDTH)), _sds((n_rows, MLA_WIDTH))] + c_outs,
        scratch_shapes=[pltpu.VMEM((tq, 1), F32), pltpu.VMEM((tq, 1), F32), pltpu.VMEM((tq, V_DIM), F32)]
        + (comm.sem_scratch() if comm else []),
        compiler_params=_params(("arbitrary",) * 3 if comm else ("parallel", "parallel", "arbitrary")),
    )(qf, kf, v, *c_ins)


def _mm_res_ln(name, a, w, xres, g, b):
    def epi(acc, xr, g_v, b_v):
        r = ALPHA * xr + acc
        y = _layer_norm(r, g_v, b_v)
        return y, y, r

    return _mm(name, a, w, (F32, BF16, F32), tm=1024, tn=D_MODEL, tk=1024, epilogue=epi, extras=[xres], vecs=[g, b])


def _mm_relu2(x_bf, w, comm=None):
    def epi(acc):
        r = jnp.maximum(acc, 0.0)
        return (r * r,)

    return _mm("ff1", x_bf, w, (BF16,), tm=1024, tn=1024, tk=D_MODEL, epilogue=epi, b_pieces=True, comm=comm)


def _loss_fn(y, t):
    def fn(y_v, t_v):
        d = y_v - t_v
        part = jnp.sum(jnp.sum(d * d, axis=1, keepdims=True), axis=0, keepdims=True)
        return d * (1.0 / D_MODEL), part

    dy, part = _rowwise("loss", fn, [y, t], [_sds(y.shape)], reds=[_sds((1, 1))])
    return part * (0.5 / D_MODEL), dy


def _ln_bwd(name, dy, r, g):
    def fn(dy_v, r_v, g_v):
        mu = jnp.mean(r_v, axis=-1, keepdims=True)
        xc = r_v - mu
        rstd = lax.rsqrt(jnp.mean(xc * xc, axis=-1, keepdims=True) + LN_EPS)
        xh = xc * rstd
        dxh = dy_v * g_v
        dr = rstd * (dxh - jnp.mean(dxh, axis=-1, keepdims=True) - xh * jnp.mean(dxh * xh, axis=-1, keepdims=True))
        return dr, dr, jnp.sum(dy_v * xh, axis=0, keepdims=True), jnp.sum(dy_v, axis=0, keepdims=True)

    return _rowwise(name, fn, [dy, r], [_sds(dy.shape), _sds(dy.shape, BF16)], consts=[g],
                    reds=[_sds((1, D_MODEL)), _sds((1, D_MODEL))])


def _head_sum_matrix():
    i = lax.broadcasted_iota(jnp.int32, (A_WIDTH, LANES), 0) // A_HEAD_DIM
    return (i == _slot_head(lax.broadcasted_iota(jnp.int32, (A_WIDTH, LANES), 1))).astype(BF16)


def _mix_bwd(dmixed, a_out, b_out, ols, a_g, b_g):
    def fn(dm, a, b, ol1, ol2, ol3, ag, bg, sum_mat, spread):
        da, dga = _rms_bwd(a, ag, dm[:, :A_WIDTH])
        db, dgb = _rms_bwd(b, bg, dm[:, A_WIDTH:])
        t = da * a
        t_hi = t.astype(BF16)
        t_lo = (t - t_hi.astype(F32)).astype(BF16)
        tsum = _dot(t_hi, sum_mat, _NN) + _dot(t_lo, sum_mat, _NN)
        tb = db * b
        delta_b = jnp.concatenate(
            [jnp.broadcast_to(jnp.sum(tb[:, h * V_DIM:(h + 1) * V_DIM], axis=-1, keepdims=True), (tb.shape[0], V_DIM))
             for h in range(MLA_HEADS)], axis=1)
        dods = [jnp.concatenate([_spread_slots(w, spread) * da, w * tsum], axis=1) for w in _pattern_weights(ol1, ol2, ol3)]
        return (*dods, db, delta_b, dga, dgb)

    n_rows = a_out.shape[0]
    wide = (n_rows, A_WIDTH)
    dils = [dil for _, dil in PATTERNS]
    return _rowwise("mix_bwd", fn, [dmixed, a_out, b_out, *ols],
                    [_sds((n_rows // d, d * OL_WIDTH)) for d in dils] + [_sds(wide, BF16), _sds(wide)],
                    consts=[a_g, b_g, _head_sum_matrix(), _slot_spread_matrix()],
                    reds=[_sds((1, A_WIDTH)), _sds((1, MLA_WIDTH))], ts=256,
                    in_dils=[1, 1, 1] + dils, out_dils=dils + [1, 1])


def _dil_bwd(qkv_view, ol_view, dod_view, dil, comm=None):
    nb = qkv_view.shape[0] // SPAN
    group = min(dil, DIL_GROUP)
    grid = (dil // group, nb + 1)
    c_ins, c_in_specs, c_outs, c_out_specs, begin, end = _hosted(comm, grid)
    n_ci, n_co = len(c_ins), len(c_outs)

    def body(cur_ref, prev_ref, ol_ref, dod_ref, *refs):
        ci_refs, out_ref, co_refs = refs[:n_ci], refs[n_ci], refs[n_ci + 1:n_ci + 1 + n_co]
        carry, sems = refs[n_ci + 1 + n_co], refs[n_ci + 2 + n_co:]
        m = pl.program_id(1)
        begin(ci_refs, co_refs, sems)

        @pl.when(m == 0)
        def _():
            carry[...] = jnp.zeros(carry.shape, F32)

        @pl.when(m < nb)
        def _():
            mask, even = _band_mask(m, 2), _even_lanes(SPAN)
            for g in range(group):
                for p in range(A_HEADS // 2):
                    q_sl, k_sl, v_sl = (slice((3 * g + t) * A_WIDTH + p * LANES, (3 * g + t) * A_WIDTH + (p + 1) * LANES)
                                        for t in range(3))
                    do_sl = slice(OL_WIDTH * g + p * LANES, OL_WIDTH * g + (p + 1) * LANES)
                    s_e, s_o = (OL_WIDTH * g + A_WIDTH + _slot_lane(2 * p + t) for t in range(2))
                    q2, do2 = cur_ref[:, q_sl], dod_ref[:, do_sl].astype(BF16)
                    zero = jnp.zeros_like(q2)
                    qcat = jnp.concatenate([jnp.where(even, q2, zero), jnp.where(even, zero, q2)], axis=0)
                    docat = jnp.concatenate([jnp.where(even, do2, zero), jnp.where(even, zero, do2)], axis=0)
                    kcat = jnp.concatenate([prev_ref[:, k_sl], cur_ref[:, k_sl]], axis=0)
                    vcat = jnp.concatenate([prev_ref[:, v_sl], cur_ref[:, v_sl]], axis=0)
                    lse_c = jnp.concatenate([ol_ref[:, s_e:s_e + 1], ol_ref[:, s_o:s_o + 1]], axis=0)
                    dl_c = jnp.concatenate([dod_ref[:, s_e:s_e + 1], dod_ref[:, s_o:s_o + 1]], axis=0)
                    pr = jnp.exp(jnp.where(mask, _dot(qcat, kcat, _NT), NEG) - lse_c)
                    ds = (pr * (_dot(docat, vcat, _NT) - dl_c)).astype(BF16)
                    dq = _dot(ds, kcat, _NN)
                    dk2 = _dot(ds, qcat, _TN)
                    dv2 = _dot(pr.astype(BF16), docat, _TN)
                    out_ref[:, q_sl] = carry[:, q_sl]
                    out_ref[:, k_sl] = carry[:, k_sl] + dk2[:SPAN]
                    out_ref[:, v_sl] = carry[:, v_sl] + dv2[:SPAN]
                    carry[:, q_sl] = jnp.where(even, dq[:SPAN], dq[SPAN:])
                    carry[:, k_sl] = dk2[SPAN:]
                    carry[:, v_sl] = dv2[SPAN:]

        @pl.when(m == nb)
        def _():
            out_ref[...] = carry[...]

        end(ci_refs, co_refs, sems)

    last = nb - 1

    def cur(width):
        return pl.BlockSpec((SPAN, group * width), lambda r, m: (jnp.minimum(m, last), r))

    def prev(width):
        return pl.BlockSpec((SPAN, group * width), lambda r, m: (jnp.clip(m - 1, 0, last), r))

    return pl.pallas_call(
        body, name=f"dil_bwd_d{dil}", grid=grid,
        in_specs=[cur(3 * A_WIDTH), prev(3 * A_WIDTH), cur(OL_WIDTH), cur(OL_WIDTH)] + c_in_specs,
        out_specs=[prev(3 * A_WIDTH)] + c_out_specs,
        out_shape=[_sds(qkv_view.shape)] + c_outs,
        scratch_shapes=[pltpu.VMEM((SPAN, group * 3 * A_WIDTH), F32)] + (comm.sem_scratch() if comm else []),
        compiler_params=_params(("arbitrary", "arbitrary") if comm else ("parallel", "arbitrary")),
    )(qkv_view, qkv_view, ol_view, dod_view, *c_ins)


def _mla_bwd(qf, kf, v, lse, do, delta, comm=None):
    n_rows = v.shape[0]
    tq, tk = min(MLA_BWD_TILES[0], n_rows), min(MLA_BWD_TILES[1], n_rows)
    nq, nk = n_rows // tq, n_rows // tk

    def first_q(j):
        return jnp.right_shift(j * tk, int(math.log2(tq)))

    grid = (MLA_HEADS, nk, nq)
    c_ins, c_in_specs, c_outs, c_out_specs, begin, end = _hosted(comm, grid)
    n_ci, n_co = len(c_ins), len(c_outs)

    def body(q_ref, k_ref, v_ref, lse_ref, do_ref, dl_ref, *refs):
        ci_refs, (dq_ref, dk_ref, dv_ref), co_refs = refs[:n_ci], refs[n_ci:n_ci + 3], refs[n_ci + 3:n_ci + 3 + n_co]
        dk_sc, dv_sc = refs[n_ci + 3 + n_co:n_ci + 5 + n_co]
        sems = refs[n_ci + 5 + n_co:]
        j, i = pl.program_id(1), pl.program_id(2)
        begin(ci_refs, co_refs, sems)

        @pl.when((j == 0) & (i == 0))
        def _():
            dq_ref[...] = jnp.zeros(dq_ref.shape, F32)

        @pl.when(i == first_q(j))
        def _():
            dk_sc[...] = jnp.zeros(dk_sc.shape, F32)
            dv_sc[...] = jnp.zeros(dv_sc.shape, F32)

        def step(masked):
            q, k, dob = q_ref[...], k_ref[...], do_ref[...]
            s = _dot(q, k, _NT) * MLA_SCALE
            if masked:
                row = lax.broadcasted_iota(jnp.int32, (tq, tk), 0) + i * tq
                col = lax.broadcasted_iota(jnp.int32, (tq, tk), 1) + j * tk
                s = jnp.where(col <= row, s, NEG)
            p = jnp.exp(s - lse_ref[:, :1])
            dp = _dot(dob, v_ref[...], _NT)
            ds = (p * (dp - dl_ref[:, :1]) * MLA_SCALE).astype(BF16)
            dv_sc[...] += _dot(p.astype(BF16), dob, _TN)
            dk_sc[...] += _dot(ds, q, _TN)
            rows = pl.ds(pl.multiple_of(i * tq, tq), tq)
            dq_ref[rows, :] += _dot(ds, k, _NN)

        active = i >= first_q(j)
        crosses = (j + 1) * tk - 1 > i * tq

        @pl.when(active & jnp.logical_not(crosses))
        def _():
            step(False)

        @pl.when(active & crosses)
        def _():
            step(True)

        @pl.when(i == nq - 1)
        def _():
            dk_ref[...] = dk_sc[...]
            dv_ref[...] = dv_sc[...]

        end(ci_refs, co_refs, sems)

    qrow = lambda h, j, i: (jnp.maximum(i, first_q(j)), h)
    return pl.pallas_call(
        body, name="mla_bwd", grid=grid,
        in_specs=[pl.BlockSpec((None, tq, QK_PAD), lambda h, j, i: (h, jnp.maximum(i, first_q(j)), 0)),
                  pl.BlockSpec((None, tk, QK_PAD), lambda h, j, i: (h, j, 0)),
                  pl.BlockSpec((tk, V_DIM), lambda h, j, i: (j, h)),
                  pl.BlockSpec((tq, V_DIM), qrow), pl.BlockSpec((tq, V_DIM), qrow), pl.BlockSpec((tq, V_DIM), qrow)]
        + c_in_specs,
        out_specs=[pl.BlockSpec((None, n_rows, QK_PAD), lambda h, j, i: (h, 0, 0)),
                   pl.BlockSpec((None, tk, QK_PAD), lambda h, j, i: (h, j, 0)),
                   pl.BlockSpec((tk, V_DIM), lambda h, j, i: (j, h))] + c_out_specs,
        out_shape=[_sds((MLA_HEADS, n_rows, QK_PAD)), _sds((MLA_HEADS, n_rows, QK_PAD)), _sds((n_rows, MLA_WIDTH))] + c_outs,
        scratch_shapes=[pltpu.VMEM((tk, QK_PAD), F32), pltpu.VMEM((tk, V_DIM), F32)] + (comm.sem_scratch() if comm else []),
        compiler_params=_params(("arbitrary",) * 3 if comm else ("parallel", "arbitrary", "arbitrary")),
    )(qf, kf, v, lse, do, delta, *c_ins)


def _mla_bwd_prep(dqf, dkf, dv, cos, sin):
    def fn(dqf_v, dkf_v, dv_v, cos_v, sin_v):
        dq = jnp.concatenate([dqf_v[h][:, :QK_NOPE] for h in range(MLA_HEADS)]
                             + [_rope_t(dqf_v[h][:, QK_NOPE:], cos_v[:, :LANES], sin_v[:, :LANES]) for h in range(MLA_HEADS)], axis=1)
        dkv = jnp.concatenate([dkf_v[h][:, :QK_NOPE] for h in range(MLA_HEADS)] + [dv_v], axis=1)
        dkpe = dkf_v[0][:, QK_NOPE:] + dkf_v[1][:, QK_NOPE:] + dkf_v[2][:, QK_NOPE:] + dkf_v[3][:, QK_NOPE:]
        return dq, dkv, dkpe

    n_rows = dv.shape[0]
    return _rowwise("mla_bwd_prep", fn, [dqf, dkf, dv, cos, sin],
                    [_sds((n_rows, UQ_PAD), BF16), _sds((n_rows, 1024), BF16), _sds((n_rows, LANES))], ts=256)


def _assemble_dh(dqkvs, dcqn, cq, dckvn, ckv, dkpe, cos, sin, gq, gkv, comm=None):
    def fn(g1, g2, g3, dcqn_v, cq_v, dckvn_v, ckv_v, dkpe_v, cos_v, sin_v, gq_v, gkv_v):
        g = g1 + g2 + g3
        dqa = _rope_t(g[:, :A_WIDTH], cos_v, sin_v) * A_SCALE
        dka = _rope_t(g[:, A_WIDTH:2 * A_WIDTH], cos_v, sin_v)
        dcq, dgq = _rms_bwd(cq_v, gq_v, dcqn_v)
        dckv, dgkv = _rms_bwd(ckv_v, gkv_v, dckvn_v)
        dkr = _rope_t(dkpe_v, cos_v[:, :LANES], sin_v[:, :LANES])
        return jnp.concatenate([dqa, dka, g[:, 2 * A_WIDTH:], dcq, dkr, dckv], axis=1), dgq, dgkv

    n_rows = cq.shape[0]
    return _rowwise("assemble_dh", fn, [*dqkvs, dcqn, cq, dckvn, ckv, dkpe, cos, sin],
                    [_sds((n_rows, IN_PAD), BF16)], consts=[gq, gkv],
                    reds=[_sds((1, Q_LORA)), _sds((1, KV_LORA))], ts=256,
                    in_dils=[dil for _, dil in PATTERNS] + [1] * 7, comm=comm)


def _layer_fwd(x_f32, x_bf, w, sm, cos, sin, behind=(None, None), rest=None):
    *qkv_views, cq, kpe, ckv = _in_proj(x_bf, w["w_in"], cos, sin)
    ol_views, brought = [], []
    for i, (view, (_, dil)) in enumerate(zip(qkv_views, PATTERNS)):
        ol, *got = _dil_fwd(view, dil, comm=rest[0][i] if rest else None)
        ol_views.append(ol)
        brought.append(got)
    if rest:
        w = {**w, **rest[1](brought)}
    cqn, qf = _mla_q_prep(cq, sm["q_a_norm"], w["w_uq"], cos, sin)
    ckvn, kf, v = _mla_kv_prep(ckv, kpe, sm["kv_a_norm"], w["w_ukv"])
    b_out, b_lse, *behind_attn = _mla_fwd(qf, kf, v, comm=behind[0])
    a_out, mixed = _mix_fwd(ol_views, b_out, sm["a_out_norm"], sm["b_out_norm"])
    x1, x1_bf, r1 = _mm_res_ln("wo_ln1", mixed, w["w_o"], x_f32, sm["ln1_g"], sm["ln1_b"])
    f, *behind_ff1 = _mm_relu2(x1_bf, w["w_ff1"], comm=behind[1])
    x2, x2_bf, r2 = _mm_res_ln("ff2_ln2", f, w["w_ff2"], x1, sm["ln2_g"], sm["ln2_b"])
    saved = dict(w=w, x_bf=x_bf, qkv_views=qkv_views, ol_views=ol_views, cq=cq, ckv=ckv, cqn=cqn, qf=qf, ckvn=ckvn,
                 kf=kf, v=v, b_out=b_out, b_lse=b_lse, a_out=a_out, mixed=mixed, x1_bf=x1_bf, r1=r1, f=f, r2=r2)
    return x2, x2_bf, saved, (behind_attn, behind_ff1)


def _layer_bwd(dx2, w, sm, sv, cos, sin, reduce_of, pending=None):
    dr2, dr2_bf, dg2, db2 = _ln_bwd("ln2_bwd", dx2, sv["r2"], sm["ln2_g"])
    (dw_ff2,) = _mm("dw_ff2", sv["f"], dr2_bf, (F32,), ta=True, tm=1024, tn=1024, tk=2048)
    du, *swapped = _mm("d_u", dr2_bf, w["w_ff2"], (BF16,), tb=True, tm=1024, tn=2048, tk=D_MODEL,
                       epilogue=lambda acc, f: (acc * (2.0 * jnp.sqrt(f.astype(F32))),), extras=[sv["f"]],
                       comm=_swap_comm(pending.pieces) if pending else None)
    if pending:
        pending.after_swap(swapped)
    (dw_ff1,) = _mm("dw_ff1", sv["x1_bf"], du, (F32,), ta=True, tm=1024, tn=1024, tk=2048, out_pieces=True)
    (dy1,) = _mm("d_x1", du, w["w_ff1"], (F32,), tb=True, tm=1024, tn=1024, tk=2048,
                 epilogue=lambda acc, d: (acc + ALPHA * d,), extras=[dr2], b_pieces=True)
    dr1, dr1_bf, dg1, db1 = _ln_bwd("ln1_bwd", dy1, sv["r1"], sm["ln1_g"])
    (dw_o,) = _mm("dw_o", sv["mixed"], dr1_bf, (F32,), ta=True, tm=1024, tn=1024, tk=2048)
    early = reduce_of(dict(w_o=dw_o, w_ff1=dw_ff1, w_ff2=dw_ff2))
    (dmixed,) = _mm("d_mixed", dr1_bf, w["w_o"], (F32,), tb=True, tm=1024, tn=1024, tk=D_MODEL)
    dod1, dod2, dod3, do_b, delta_b, dga, dgb = _mix_bwd(
        dmixed, sv["a_out"], sv["b_out"], sv["ol_views"], sm["a_out_norm"], sm["b_out_norm"])
    dqkvs = []
    for i, ((_, dil), qkv_view, ol_view, dod_view) in enumerate(zip(PATTERNS, sv["qkv_views"], sv["ol_views"], (dod1, dod2, dod3))):
        swap_here = early is not None and i == 0
        dqkv, *swapped = _dil_bwd(qkv_view, ol_view, dod_view, dil, comm=_swap_comm(early.pieces) if swap_here else None)
        dqkvs.append(dqkv)
        if swap_here:
            early.after_swap(swapped)
    to_scatter = (early.s16 if early else []) + (pending.s16 if pending else [])
    dqf, dkf, dv_b, *scattered = _mla_bwd(sv["qf"], sv["kf"], sv["v"], sv["b_lse"], do_b, delta_b,
                                          comm=_scatter_comm(to_scatter) if to_scatter else None)
    if early:
        early.after_scatter(scattered[:len(early.s16)])
    if pending:
        pending.after_scatter(scattered[len(scattered) - len(pending.s16):])
    dq_b, dkv_b, dkpe = _mla_bwd_prep(dqf, dkf, dv_b, cos, sin)
    (dw_uq,) = _mm("dw_uq", sv["cqn"], dq_b, (F32,), ta=True, tm=Q_LORA, tn=1024, tk=2048)
    (dcqn,) = _mm("d_cqn", dq_b, w["w_uq"], (F32,), tb=True, tm=1024, tn=Q_LORA, tk=UQ_PAD)
    (dw_ukv,) = _mm("dw_ukv", sv["ckvn"], dkv_b, (F32,), ta=True, tm=KV_LORA, tn=1024, tk=2048)
    (dckvn,) = _mm("d_ckvn", dkv_b, w["w_ukv"], (F32,), tb=True, tm=1024, tn=KV_LORA, tk=1024)
    to_share = (early.mine if early else []) + (pending.mine if pending else [])
    dh, dgq, dgkv, *shared = _assemble_dh(dqkvs, dcqn, sv["cq"], dckvn, sv["ckv"], dkpe, cos, sin,
                                          sm["q_a_norm"], sm["kv_a_norm"], comm=_share_comm(to_share) if to_share else None)
    if early:
        early.after_share(shared[:len(early.mine)])
    if pending:
        pending.after_share(shared[len(shared) - len(pending.mine):])
    (dw_in,) = _mm("dw_in", sv["x_bf"], dh, (F32,), ta=True, tm=1024, tn=1024, tk=2048)
    (dx,) = _mm("d_x", dh, w["w_in"], (F32,), tb=True, tm=1024, tn=1024, tk=2048,
                epilogue=lambda acc, d: (acc + ALPHA * d,), extras=[dr1])
    late = reduce_of(dict(w_in=dw_in, w_uq=dw_uq, w_ukv=dw_ukv))
    dsm = dict(q_a_norm=dgq, kv_a_norm=dgkv, a_out_norm=dga, b_out_norm=dgb, ln1_g=dg1, ln1_b=db1, ln2_g=dg2, ln2_b=db2)
    return dx, dsm, late


def _pad_w_in(w):
    return jnp.concatenate([w[:, :1792], w[:, 1920:1984], jnp.zeros((w.shape[0], 64), w.dtype), w[:, 1792:1920]], axis=1)


def _unpad_w_in(w):
    return jnp.concatenate([w[:, :1792], w[:, 1920:2048], w[:, 1792:1856]], axis=1)


def _rope_tables(n_rows):
    half = A_HEAD_DIM // 2
    inv_freq = ROPE_THETA ** (-jnp.arange(half, dtype=F32) / half)
    ang = jnp.arange(n_rows, dtype=F32)[:, None] * inv_freq[None, :]
    cos = jnp.tile(jnp.cos(ang), (1, 2 * A_HEADS))
    sin = jnp.tile(jnp.concatenate([-jnp.sin(ang), jnp.sin(ang)], axis=1), (1, A_HEADS))
    return cos, sin


W_NAMES = ("w_in", "w_uq", "w_ukv", "w_o", "w_ff1", "w_ff2")
PIECE = dict(w_in=(512, 512), w_uq=(128, 256), w_ukv=(64, 256), w_o=(128, 1024), w_ff1=(512, 1024), w_ff2=(512, 1024))
SHARD_COLS = dict(w_in=496, w_uq=192, w_ukv=256, w_o=1024, w_ff1=1024, w_ff2=1024)


def _col_pieces(t, pad_to):
    rows, _, cols = t.shape
    t = jnp.pad(t, ((0, 0), (0, 0), (0, pad_to - cols)))
    return t.reshape(2, rows // 2, N_CHIPS, pad_to).transpose(2, 0, 1, 3)


def _grad_pieces(dws):
    def heads(t, rows):
        return jnp.concatenate([t[:, :512].reshape(rows, MLA_HEADS, LANES), t[:, 512:].reshape(rows, MLA_HEADS, LANES)], axis=2)

    make = dict(
        w_in=lambda t: _col_pieces(_unpad_w_in(t).reshape(D_MODEL, N_CHIPS, SHARD_COLS["w_in"]), 512),
        w_uq=lambda t: _col_pieces(heads(t, Q_LORA), 256),
        w_ukv=lambda t: _col_pieces(heads(t, KV_LORA), 256),
        w_o=lambda t: t.reshape(N_CHIPS, 2, 128, D_MODEL),
        w_ff1=lambda t: t,
        w_ff2=lambda t: t.reshape(N_CHIPS, 2, 512, D_MODEL))
    return {n: make[n](t) for n, t in dws.items()}


def _weights_from_pieces(p):
    def cols(t):
        return t.transpose(1, 2, 0, 3).reshape(2 * t.shape[2], N_CHIPS, t.shape[3])

    def heads(t, rows):
        return jnp.concatenate([t[:, :, :LANES].reshape(rows, 512), t[:, :, LANES:].reshape(rows, 512)], axis=1)

    make = dict(
        w_in=lambda t: _pad_w_in(cols(t)[:, :, :SHARD_COLS["w_in"]].reshape(D_MODEL, IN_COLS)),
        w_uq=lambda t: heads(cols(t), Q_LORA),
        w_ukv=lambda t: heads(cols(t), KV_LORA),
        w_o=lambda t: t.reshape(D_MODEL, D_MODEL),
        w_ff1=lambda t: t,
        w_ff2=lambda t: t.reshape(D_FF, D_MODEL))
    return {n: make[n](t) for n, t in p.items()}


def _my_piece(name, shard, half):
    rows, cols = shard.shape[0] // 2, shard.shape[1]
    t = lax.dynamic_slice_in_dim(shard, half * rows, rows, axis=0).astype(BF16)
    return jnp.pad(t, ((0, 0), (0, PIECE[name][1] - cols)))


MESH = pl.DeviceIdType.MESH
ANY = pl.BlockSpec(memory_space=pl.ANY)


class _Comm:
    def __init__(self, ins, out_shapes, n_sems, start, finish):
        self.ins, self.out_shapes, self.n_sems, self.start, self.finish = list(ins), list(out_shapes), n_sems, start, finish

    def sem_scratch(self):
        return [pltpu.SemaphoreType.DMA((self.n_sems,)), pltpu.SemaphoreType.DMA((self.n_sems,))]


def _comm_call(name, comm):
    n_in, n_out = len(comm.ins), len(comm.out_shapes)

    def body(*refs):
        ins, outs, sems = refs[:n_in], refs[n_in:n_in + n_out], refs[n_in + n_out:]
        comm.start(ins, outs, *sems)
        comm.finish(ins, outs, *sems)

    return pl.pallas_call(body, name=name, out_shape=comm.out_shapes, in_specs=[ANY] * n_in, out_specs=[ANY] * n_out,
                          scratch_shapes=comm.sem_scratch())(*comm.ins)


def _all_gather_comm(blocks):
    n = len(blocks)

    def plan(x_refs, out_refs, send_sems, recv_sems):
        x, y, c = lax.axis_index("x"), lax.axis_index("y"), lax.axis_index("c")
        me, sibling = (x, y, c), (x, y, 1 - c)
        chips = [(1 - x, y), (x, 1 - y), (1 - x, 1 - y)]

        def copy(t, k, blk, to, src=None):
            px, py, pc = blk
            slot = out_refs[t].at[4 * px + 2 * py + pc]
            return pltpu.make_async_remote_copy(
                src_ref=slot if src is None else src, dst_ref=slot,
                send_sem=send_sems.at[7 * t + k], recv_sem=recv_sems.at[7 * t + k], device_id=to, device_id_type=MESH)

        first = []
        for t in range(n):
            first.append(copy(t, 0, me, sibling, src=x_refs[t]))
            first += [copy(t, 1 + j, me, (*chip, c), src=x_refs[t]) for j, chip in enumerate(chips)]
        return me, sibling, chips, c, copy, first

    def start(x_refs, out_refs, send_sems, recv_sems):
        for cp in plan(x_refs, out_refs, send_sems, recv_sems)[-1]:
            cp.start()

    def finish(x_refs, out_refs, send_sems, recv_sems):
        me, sibling, chips, c, copy, sent = plan(x_refs, out_refs, send_sems, recv_sems)
        for j, chip in enumerate(chips):
            for t in range(n):
                copy(t, 1 + j, (*chip, c), me).wait_recv()
                sent.append(copy(t, 4 + j, (*chip, c), sibling))
                sent[-1].start()
        for t in range(n):
            copy(t, 0, sibling, me).wait_recv()
        for j, chip in enumerate(chips):
            for t in range(n):
                copy(t, 4 + j, (*chip, 1 - c), me).wait_recv()
        for cp in sent:
            cp.wait_send()

    return _Comm(blocks, [_sds((N_DEV,) + b.shape, b.dtype) for b in blocks], 7 * n, start, finish)


def _simple_comm(ins, out_shapes, n_sems, copies):
    def start(in_refs, out_refs, send_sems, recv_sems):
        for cp in copies(in_refs, out_refs, send_sems, recv_sems):
            cp.start()

    def finish(in_refs, out_refs, send_sems, recv_sems):
        for cp in copies(in_refs, out_refs, send_sems, recv_sems):
            cp.wait()

    return _Comm(ins, out_shapes, n_sems, start, finish)


def _swap_comm(gs):
    def copies(g_refs, got_refs, send_sems, recv_sems):
        c = lax.axis_index("c")
        sibling = (lax.axis_index("x"), lax.axis_index("y"), 1 - c)
        return [pltpu.make_async_remote_copy(src_ref=g_refs[t].at[k, 1 - c], dst_ref=got_refs[t].at[k],
                                             send_sem=send_sems.at[N_CHIPS * t + k], recv_sem=recv_sems.at[N_CHIPS * t + k],
                                             device_id=sibling, device_id_type=MESH)
                for t in range(len(gs)) for k in range(N_CHIPS)]

    return _simple_comm(gs, [_sds((N_CHIPS,) + g.shape[2:], g.dtype) for g in gs], N_CHIPS * len(gs), copies)


def _scatter_comm(hs):
    def copies(h_refs, got_refs, send_sems, recv_sems):
        x, y, c = lax.axis_index("x"), lax.axis_index("y"), lax.axis_index("c")
        chips = [(1 - x, y), (x, 1 - y), (1 - x, 1 - y)]
        return [pltpu.make_async_remote_copy(src_ref=h_refs[t].at[2 * px + py], dst_ref=got_refs[t].at[rel],
                                             send_sem=send_sems.at[3 * t + rel], recv_sem=recv_sems.at[3 * t + rel],
                                             device_id=(px, py, c), device_id_type=MESH)
                for t in range(len(hs)) for rel, (px, py) in enumerate(chips)]

    return _simple_comm(hs, [_sds((3,) + h.shape[1:], h.dtype) for h in hs], 3 * len(hs), copies)


def _share_comm(rs):
    def copies(r_refs, got_refs, send_sems, recv_sems):
        sibling = (lax.axis_index("x"), lax.axis_index("y"), 1 - lax.axis_index("c"))
        return [pltpu.make_async_remote_copy(src_ref=r_refs[t], dst_ref=got_refs[t], send_sem=send_sems.at[t],
                                             recv_sem=recv_sems.at[t], device_id=sibling, device_id_type=MESH)
                for t in range(len(rs))]

    return _simple_comm(rs, [_sds(r.shape, r.dtype) for r in rs], len(rs), copies)


SUM_STEPS = 4


def _pair_sums(name, gs, gots, c_arr):
    n = len(gs)
    dims = [(g.shape[2] // SUM_STEPS, g.shape[3]) for g in gs]

    def body(c_ref, *refs):
        for t in range(n):
            s = refs[t][...] + refs[n + t][...]
            refs[2 * n + t][...] = s
            refs[3 * n + t][...] = s.astype(BF16)

    def part(tr, pc):
        return pl.BlockSpec((None, tr, pc), lambda k, i, c_ref: (k, i, 0))

    def kept(tr, pc):
        return pl.BlockSpec((None, None, tr, pc), lambda k, i, c_ref: (k, c_ref[0], i, 0))

    outs = pl.pallas_call(
        body, name=name,
        out_shape=[_sds((N_CHIPS,) + g.shape[2:]) for g in gs] + [_sds((N_CHIPS,) + g.shape[2:], BF16) for g in gs],
        grid_spec=pltpu.PrefetchScalarGridSpec(
            num_scalar_prefetch=1, grid=(N_CHIPS, SUM_STEPS),
            in_specs=[kept(*d) for d in dims] + [part(*d) for d in dims],
            out_specs=[part(*d) for d in dims] * 2),
        compiler_params=_params(("parallel", "parallel")),
    )(c_arr, *gs, *gots)
    return list(outs[:n]), list(outs[n:])


def _chips_sums(name, h32s, gots, chip_arr):
    n = len(h32s)
    dims = [(h.shape[1] // SUM_STEPS, h.shape[2]) for h in h32s]

    def body(chip_ref, *refs):
        for t in range(n):
            got_ref = refs[n + t]
            refs[2 * n + t][...] = refs[t][...] + got_ref[0] + got_ref[1] + got_ref[2]

    return pl.pallas_call(
        body, name=name, out_shape=[_sds(h.shape[1:]) for h in h32s],
        grid_spec=pltpu.PrefetchScalarGridSpec(
            num_scalar_prefetch=1, grid=(SUM_STEPS,),
            in_specs=[pl.BlockSpec((None, tr, pc), lambda i, chip_ref: (chip_ref[0], i, 0)) for tr, pc in dims]
            + [pl.BlockSpec((3, tr, pc), lambda i, chip_ref: (0, i, 0)) for tr, pc in dims],
            out_specs=[pl.BlockSpec((tr, pc), lambda i, chip_ref: (i, 0)) for tr, pc in dims]),
        compiler_params=_params(("parallel",)),
    )(chip_arr, *h32s, *gots)


class _Reduce:
    def __init__(self, tag, dws, c, chip):
        gp = _grad_pieces(dws)
        self.tag, self.c, self.names, self.pieces = tag, c, list(gp), list(gp.values())
        self.c_arr, self.chip_arr = jnp.reshape(c, (1,)).astype(jnp.int32), jnp.reshape(chip, (1,)).astype(jnp.int32)
        self.shards = None

    def after_swap(self, gots):
        self.s32, self.s16 = _pair_sums(f"rs_pair_sum_{self.tag}", self.pieces, gots, self.c_arr)

    def after_scatter(self, parts):
        self.mine = list(_chips_sums(f"rs_chips_sum_{self.tag}", self.s32, parts, self.chip_arr))

    def after_share(self, theirs):
        self.shards = {}
        for n, a, b in zip(self.names, self.mine, theirs):
            both = jnp.where(self.c == 0, jnp.concatenate([a, b], axis=0), jnp.concatenate([b, a], axis=0))
            self.shards[n] = both[:, :SHARD_COLS[n]]

    def run_alone(self):
        self.after_swap(_comm_call(f"rs_pair_{self.tag}", _swap_comm(self.pieces)))
        self.after_scatter(_comm_call(f"rs_chips_{self.tag}", _scatter_comm(self.s16)))
        self.after_share(_comm_call(f"rs_share_{self.tag}", _share_comm(self.mine)))


class _LayerWeights:
    def __init__(self, shards, c, dev):
        self.shards, self.c, self.dev = shards, c, dev

    def comm(self, l, names=W_NAMES):
        return _all_gather_comm([_my_piece(n, self.shards[l][n], self.c) for n in names])

    def take(self, l, gathered, names=W_NAMES):
        pieces = {}
        for n, g in zip(names, gathered):
            mine = _my_piece(n, self.shards[l][n], self.c)
            pieces[n] = lax.dynamic_update_index_in_dim(g, mine, self.dev, 0).reshape((N_CHIPS, 2) + mine.shape)
        return _weights_from_pieces(pieces)

    def first(self):
        now, later = ("w_in", "w_uq", "w_ukv"), ("w_ff1", "w_ff2", "w_o")
        weights = self.take(0, _comm_call("ag_l0", self.comm(0, now)), now)
        comms = [self.comm(0, (n,)) for n in later]

        def take_rest(results):
            rest = {}
            for n, gathered in zip(later, results):
                rest.update(self.take(0, gathered, (n,)))
            return rest

        return weights, (comms, take_rest)


def _all_reduce_small(vec):
    rows, lanes = vec.shape

    def gather_body(x_ref, out_ref, send_sems, recv_sems):
        x, y, c = lax.axis_index("x"), lax.axis_index("y"), lax.axis_index("c")
        me, sibling = (x, y, c), (x, y, 1 - c)
        chips = [(1 - x, y), (x, 1 - y), (1 - x, 1 - y)]

        def slot(px, py, pc):
            return out_ref.at[4 * px + 2 * py + pc]

        def copy(k, blk, to, src=None):
            return pltpu.make_async_remote_copy(
                src_ref=slot(*blk) if src is None else src, dst_ref=slot(*blk),
                send_sem=send_sems.at[k], recv_sem=recv_sems.at[k], device_id=to, device_id_type=MESH)

        out_ref[4 * x + 2 * y + c] = x_ref[...]
        first = [copy(0, me, sibling, src=x_ref)]
        first += [copy(1 + j, me, (*chip, c), src=x_ref) for j, chip in enumerate(chips)]
        for cp in first:
            cp.start()
        passed = [copy(4 + j, (*chip, c), sibling) for j, chip in enumerate(chips)]
        for j, chip in enumerate(chips):
            copy(1 + j, (*chip, c), me).wait_recv()
            passed[j].start()
        copy(0, sibling, me).wait_recv()
        for j, chip in enumerate(chips):
            copy(4 + j, (*chip, 1 - c), me).wait_recv()
        for cp in first + passed:
            cp.wait_send()

    vmem = pl.BlockSpec(memory_space=pltpu.VMEM)
    allv = pl.pallas_call(
        gather_body, name="small_all_gather", out_shape=_sds((N_DEV, rows, lanes)),
        in_specs=[vmem], out_specs=vmem,
        scratch_shapes=[pltpu.SemaphoreType.DMA((7,)), pltpu.SemaphoreType.DMA((7,))],
    )(vec)

    def sum_body(a_ref, o_ref):
        acc = a_ref[0]
        for d in range(1, N_DEV):
            acc = acc + a_ref[d]
        o_ref[...] = acc

    return pl.pallas_call(sum_body, name="small_sum", out_shape=_sds((rows, lanes)), in_specs=[vmem], out_specs=vmem)(allv)


def _adamw(name, w, g, m, v, ts=512, comm=None):
    def fn(w_v, g_v, m_v, v_v):
        m_n = ADAM_B1 * m_v + (1.0 - ADAM_B1) * g_v
        v_n = ADAM_B2 * v_v + (1.0 - ADAM_B2) * (g_v * g_v)
        m_hat = m_n / (1.0 - ADAM_B1 ** ADAM_STEP)
        v_hat = v_n / (1.0 - ADAM_B2 ** ADAM_STEP)
        delta = -ADAM_LR * (m_hat / (jnp.sqrt(v_hat) + ADAM_EPS) + ADAM_WD * w_v)
        return delta, m_n, v_n

    return _rowwise(name, fn, [w, g, m, v], [_sds(w.shape)] * 3, ts=ts, comm=comm)


def _pack_small(per_layer):
    flat = jnp.concatenate([per_layer[l][n].reshape(-1) for l in range(DEPTH) for n in SMALL_NAMES])
    return jnp.pad(flat, (0, SMALL_ROWS * LANES - flat.shape[0])).reshape(SMALL_ROWS, LANES)


def _unpack_small(packed):
    flat = packed.reshape(-1)
    per = sum(SMALL_SIZES)
    out = {}
    for n, size, off in zip(SMALL_NAMES, SMALL_SIZES, [sum(SMALL_SIZES[:i]) for i in range(len(SMALL_SIZES))]):
        out[n] = jnp.stack([flat[l * per + off:l * per + off + size] for l in range(DEPTH)])
    return out


NEXT_WEIGHTS_BEHIND = (("w_ff1", "w_ff2"), ("w_in", "w_uq", "w_ukv", "w_o"))


def _fwd_bwd(x, target, layer_weights, smalls, on_layer_grads):
    depth = len(smalls)
    cos, sin = _rope_tables(x.shape[0])
    h_f32, h_bf = x, x.astype(BF16)
    saved = []
    w, rest = layer_weights.first()
    for l in range(depth):
        more = l + 1 < depth
        behind = [layer_weights.comm(l + 1, names) if more else None for names in NEXT_WEIGHTS_BEHIND]
        h_f32, h_bf, sv, brought = _layer_fwd(h_f32, h_bf, w, smalls[l], cos, sin, behind, rest)
        saved.append(sv)
        if more:
            w, rest = {}, None
            for names, got in zip(NEXT_WEIGHTS_BEHIND, brought):
                w.update(layer_weights.take(l + 1, got, names))
    loss_part, dh = _loss_fn(h_f32, target)
    small_grads, pending = [None] * depth, None
    for l in reversed(range(depth)):
        dh, small_grads[l], pending = _layer_bwd(dh, saved[l]["w"], smalls[l], saved[l], cos, sin,
                                                 lambda dws, l=l: on_layer_grads(l, dws), pending)
    return loss_part, dh, small_grads, pending


def kernel(x, w_in, q_a_norm, kv_a_norm, w_uq, w_ukv, a_out_norm, b_out_norm, w_o, ln1_g, ln1_b, w_ff1, w_ff2, ln2_g, ln2_b, loss_target, m_w_in, m_q_a_norm, m_kv_a_norm, m_w_uq, m_w_ukv, m_a_out_norm, m_b_out_norm, m_w_o, m_ln1_g, m_ln1_b, m_w_ff1, m_w_ff2, m_ln2_g, m_ln2_b, v_w_in, v_q_a_norm, v_kv_a_norm, v_w_uq, v_w_ukv, v_a_out_norm, v_b_out_norm, v_w_o, v_ln1_g, v_ln1_b, v_w_ff1, v_w_ff2, v_ln2_g, v_ln2_b):
    c = lax.axis_index("c")
    chip = 2 * lax.axis_index("x") + lax.axis_index("y")
    big = dict(w_in=w_in, w_uq=w_uq, w_ukv=w_ukv, w_o=w_o, w_ff1=w_ff1, w_ff2=w_ff2)
    big_m = dict(w_in=m_w_in, w_uq=m_w_uq, w_ukv=m_w_ukv, w_o=m_w_o, w_ff1=m_w_ff1, w_ff2=m_w_ff2)
    big_v = dict(w_in=v_w_in, w_uq=v_w_uq, w_ukv=v_w_ukv, w_o=v_w_o, w_ff1=v_w_ff1, w_ff2=v_w_ff2)
    small = dict(q_a_norm=q_a_norm, kv_a_norm=kv_a_norm, a_out_norm=a_out_norm, b_out_norm=b_out_norm,
                 ln1_g=ln1_g, ln1_b=ln1_b, ln2_g=ln2_g, ln2_b=ln2_b)
    small_m = dict(q_a_norm=m_q_a_norm, kv_a_norm=m_kv_a_norm, a_out_norm=m_a_out_norm, b_out_norm=m_b_out_norm,
                   ln1_g=m_ln1_g, ln1_b=m_ln1_b, ln2_g=m_ln2_g, ln2_b=m_ln2_b)
    small_v = dict(q_a_norm=v_q_a_norm, kv_a_norm=v_kv_a_norm, a_out_norm=v_a_out_norm, b_out_norm=v_b_out_norm,
                   ln1_g=v_ln1_g, ln1_b=v_ln1_b, ln2_g=v_ln2_g, ln2_b=v_ln2_b)

    layer_weights = _LayerWeights([{n: big[n][l] for n in W_NAMES} for l in range(DEPTH)], c, 2 * chip + c)
    smalls = [{n: small[n][l][None, :] for n in SMALL_NAMES} for l in range(DEPTH)]
    reductions = [[] for _ in range(DEPTH)]

    def reduce_layer(l, dws):
        reductions[l].append(_Reduce(f"l{l}_{len(reductions[l])}", dws, c, chip))
        return reductions[l][-1]

    loss_part, grad_x, small_grads, last = _fwd_bwd(x[0], loss_target[0], layer_weights, smalls, reduce_layer)
    loss = lax.psum(loss_part[0, 0], ("x", "y", "c"))
    grad_x = grad_x[None]

    g_small_packed = _all_reduce_small(_pack_small(small_grads))
    pack_in = lambda d: _pack_small([{n: d[n][l] for n in SMALL_NAMES} for l in range(DEPTH)])
    sd, sm_, sv_ = _adamw("adamw_small", pack_in(small), g_small_packed, pack_in(small_m), pack_in(small_v), ts=SMALL_ROWS)
    g_small, d_small, m_small, v_small = (_unpack_small(t) for t in (g_small_packed, sd, sm_, sv_))

    last.run_alone()
    g_big, d_big, m_big, v_big = {}, {}, {}, {}
    for n in W_NAMES:
        g = jnp.stack([next(r.shards[n] for r in reductions[l] if n in r.names) for l in range(DEPTH)])
        shape = g.shape
        flat = lambda t: t.reshape(shape[0] * shape[1], shape[2])
        d_, m_, v_ = _adamw(f"adamw_{n}", flat(big[n]), flat(g), flat(big_m[n]), flat(big_v[n]))
        g_big[n], d_big[n], m_big[n], v_big[n] = g, d_.reshape(shape), m_.reshape(shape), v_.reshape(shape)

    order = ("w_in", "q_a_norm", "kv_a_norm", "w_uq", "w_ukv", "a_out_norm", "b_out_norm", "w_o", "ln1_g", "ln1_b",
             "w_ff1", "w_ff2", "ln2_g", "ln2_b")
    pick = lambda bigd, smalld: [bigd[n] if n in bigd else smalld[n] for n in order]
    return (loss, grad_x, *pick(g_big, g_small), *pick(d_big, d_small), *pick(m_big, m_small), *pick(v_big, v_small))
```

```python
import math

import jax
import jax.numpy as jnp
from jax import lax
from jax.experimental import pallas as pl
from jax.experimental.pallas import tpu as pltpu

F32, BF16 = jnp.float32, jnp.bfloat16

D_MODEL = 1024
DEPTH = 4
A_HEAD_DIM = 64
A_HEADS = 8
A_WIDTH = 512
PATTERNS = ((128, 1), (512, 4), (2048, 16))
SPAN = 128
MLA_HEADS = 4
QK_NOPE = 128
QK_ROPE = 64
V_DIM = 128
Q_LORA = 256
KV_LORA = 128
MLA_WIDTH = 512
QK_PAD = 256
IN_COLS = 1984
IN_PAD = 2048
UQ_PAD = 1024
D_FF = 4096
ROPE_THETA = 10000.0
ALPHA = (2.0 * DEPTH) ** 0.25
LN_EPS = 1e-5
RMS_EPS = 1e-6
MLA_SCALE = (QK_NOPE + QK_ROPE) ** -0.5
A_SCALE = A_HEAD_DIM ** -0.5

ADAM_LR, ADAM_B1, ADAM_B2, ADAM_EPS, ADAM_WD, ADAM_STEP = 0.001, 0.9, 0.999, 1e-08, 0.01, 10

VMEM_LIMIT_BYTES = 56 * 1024 * 1024
NEG = -1e30
N_CHIPS, N_DEV = 4, 8
LANES = 128

SMALL_NAMES = ("q_a_norm", "kv_a_norm", "a_out_norm", "b_out_norm", "ln1_g", "ln1_b", "ln2_g", "ln2_b")
SMALL_SIZES = (256, 128, 512, 512, 1024, 1024, 1024, 1024)
SMALL_ROWS = 176


def _params(sem):
    return pltpu.CompilerParams(dimension_semantics=sem, vmem_limit_bytes=VMEM_LIMIT_BYTES)


FAST_STRIDE = 4


def _interleave(parts, scratch):
    d, (n, width) = len(parts), parts[0].shape
    if d > FAST_STRIDE and d % FAST_STRIDE == 0:
        inner = [_interleave(parts[r::FAST_STRIDE], scratch) for r in range(FAST_STRIDE)]
        return _interleave(inner, scratch)
    for r in range(d):
        for s in range(width // LANES):
            scratch.at[s][pl.ds(r, n, stride=d), :] = parts[r][:, s * LANES:(s + 1) * LANES]
    return jnp.concatenate([scratch.at[s][pl.ds(0, n * d), :] for s in range(width // LANES)], axis=1)


def _deinterleave(x, d, scratch):
    n, width = x.shape[0] // d, x.shape[1]
    if d > FAST_STRIDE and d % FAST_STRIDE == 0:
        outer = _deinterleave(x, FAST_STRIDE, scratch)
        inner = [_deinterleave(y, d // FAST_STRIDE, scratch) for y in outer]
        return [inner[r % FAST_STRIDE][r // FAST_STRIDE] for r in range(d)]
    for s in range(width // LANES):
        scratch.at[s][pl.ds(0, n * d), :] = x[:, s * LANES:(s + 1) * LANES]
    return [jnp.concatenate([scratch.at[s][pl.ds(r, n, stride=d), :] for s in range(width // LANES)], axis=1) for r in range(d)]


def _rowwise(name, fn, ins, outs, consts=(), reds=(), ts=512, in_dils=None, out_dils=None, comm=None):
    in_dils = list(in_dils or [1] * len(ins))
    out_dils = list(out_dils or [1] * len(outs))
    n_rows = ins[0].shape[-2] * in_dils[0]
    ts = min(ts, n_rows)
    assert n_rows % ts == 0
    n_in, n_c, n_o, n_r = len(ins), len(consts), len(outs), len(reds)
    viewed = [(a.shape, d) for a, d in zip(list(ins) + list(outs), in_dils + out_dils) if d > 1]
    grid = (n_rows // ts,)
    h_ins, h_in_specs, h_outs, h_out_specs, begin, end = _hosted(comm, grid)
    n_hi, n_ho = len(h_ins), len(h_outs)

    def tile_spec(shape, d=1):
        if len(shape) == 2:
            return pl.BlockSpec((ts // d, shape[1]), lambda i: (i, 0))
        return pl.BlockSpec((shape[0], ts, shape[2]), lambda i: (0, i, 0))

    def whole_spec(shape):
        return pl.BlockSpec(shape, lambda i: (0,) * len(shape))

    def body(*refs):
        in_refs, c_refs, hi_refs = refs[:n_in], refs[n_in:n_in + n_c], refs[n_in + n_c:n_in + n_c + n_hi]
        refs = refs[n_in + n_c + n_hi:]
        o_refs, r_refs, ho_refs = refs[:n_o], refs[n_o:n_o + n_r], refs[n_o + n_r:n_o + n_r + n_ho]
        scratches = list(refs[n_o + n_r + n_ho:])
        sems = [scratches.pop(), scratches.pop()][::-1] if comm else []
        begin(hi_refs, ho_refs, sems)
        vals = []
        for r, d in zip(in_refs, in_dils):
            if d == 1:
                vals.append(r[...])
            else:
                width = r.shape[1] // d
                vals.append(_interleave([r[:, k * width:(k + 1) * width] for k in range(d)], scratches.pop(0)))
        res = fn(*vals, *[r[...] for r in c_refs])
        res = tuple(res) if isinstance(res, (tuple, list)) else (res,)
        for r, v, d in zip(o_refs, res[:n_o], out_dils):
            if len(r.shape) == 3:
                for g in range(r.shape[0]):
                    r[g] = v[g].astype(r.dtype)
            elif d == 1:
                r[...] = v.astype(r.dtype)
            else:
                width = r.shape[1] // d
                for k, part in enumerate(_deinterleave(v, d, scratches.pop(0))):
                    r[:, k * width:(k + 1) * width] = part.astype(r.dtype)
        if n_r:
            i = pl.program_id(0)

            @pl.when(i == 0)
            def _():
                for r, v in zip(r_refs, res[n_o:]):
                    r[...] = v

            @pl.when(i > 0)
            def _():
                for r, v in zip(r_refs, res[n_o:]):
                    r[...] += v
        end(hi_refs, ho_refs, sems)

    out = pl.pallas_call(
        body, name=name, grid=grid,
        in_specs=[tile_spec(a.shape, d) for a, d in zip(ins, in_dils)] + [whole_spec(c.shape) for c in consts] + h_in_specs,
        out_specs=[tile_spec(o.shape, d) for o, d in zip(outs, out_dils)] + [whole_spec(r.shape) for r in reds] + h_out_specs,
        out_shape=list(outs) + list(reds) + h_outs,
        scratch_shapes=[pltpu.VMEM((shape[1] // d // LANES, ts, LANES), F32) for shape, d in viewed]
        + (comm.sem_scratch() if comm else []),
        compiler_params=_params(("arbitrary",)),
    )(*ins, *consts, *h_ins)
    return out


def _sds(shape, dtype=F32):
    return jax.ShapeDtypeStruct(tuple(shape), dtype)


def _hosted(comm, grid):
    if comm is None:
        return [], [], [], [], (lambda *a: None), (lambda *a: None)

    def edge(which, at, ins, outs, sems):
        ids = [pl.program_id(d) for d in range(len(grid))]
        hit = ids[0] == at[0]
        for i, v in zip(ids[1:], at[1:]):
            hit = hit & (i == v)

        @pl.when(hit)
        def _():
            which(ins, outs, *sems)

    begin = lambda ins, outs, sems: edge(comm.start, [0] * len(grid), ins, outs, sems)
    end = lambda ins, outs, sems: edge(comm.finish, [g - 1 for g in grid], ins, outs, sems)
    return comm.ins, [ANY] * len(comm.ins), comm.out_shapes, [ANY] * len(comm.out_shapes), begin, end


def _mm(name, a, b, out_dtypes, *, ta=False, tb=False, tm=512, tn=512, tk=512, epilogue=None, extras=(), vecs=(),
        comm=None, b_pieces=False, out_pieces=False):
    (k_dim, m_dim) = a.shape if ta else a.shape[::-1]
    if b_pieces:
        b_rows, b_cols = 2 * b.shape[2], N_CHIPS * b.shape[3]
        (n_dim, k2) = (b_rows, b_cols) if tb else (b_cols, b_rows)
    else:
        (n_dim, k2) = b.shape if tb else b.shape[::-1]
    assert k_dim == k2
    tm, tn, tk = min(tm, m_dim), min(tn, n_dim), min(tk, k_dim)
    assert m_dim % tm == 0 and n_dim % tn == 0 and k_dim % tk == 0
    nk = k_dim // tk
    grid = (m_dim // tm, n_dim // tn, nk)
    a_spec = pl.BlockSpec((tk, tm), lambda i, j, k: (k, i)) if ta else pl.BlockSpec((tm, tk), lambda i, j, k: (i, k))
    if b_pieces:
        assert tn == b_rows and tk % b.shape[3] == 0 if tb else (tk, tn) == (b_rows, b.shape[3])
        n_bp = tk // b.shape[3] if tb else 1
        b_spec = pl.BlockSpec((n_bp,) + b.shape[1:], (lambda i, j, k: (k, 0, 0, 0)) if tb else (lambda i, j, k: (j, 0, 0, 0)))
    else:
        b_spec = pl.BlockSpec((tn, tk), lambda i, j, k: (j, k)) if tb else pl.BlockSpec((tk, tn), lambda i, j, k: (k, j))
    assert not out_pieces or (tm == m_dim and len(out_dtypes) == 1)
    dims = (((0 if ta else 1,), (1 if tb else 0,)), ((), ()))
    n_e, n_v, n_o = len(extras), len(vecs), len(out_dtypes)
    c_ins, c_in_specs, c_outs, c_out_specs, begin, end = _hosted(comm, grid)
    n_ci, n_co = len(c_ins), len(c_outs)

    def body(a_ref, b_ref, *refs):
        e_refs, v_refs, ci_refs = refs[:n_e], refs[n_e:n_e + n_v], refs[n_e + n_v:n_e + n_v + n_ci]
        o_refs = refs[n_e + n_v + n_ci:n_e + n_v + n_ci + n_o]
        co_refs = refs[n_e + n_v + n_ci + n_o:n_e + n_v + n_ci + n_o + n_co]
        scratch = refs[n_e + n_v + n_ci + n_o + n_co:]
        sems = scratch[-2:]
        begin(ci_refs, co_refs, sems)
        if b_pieces:
            pc = b.shape[3]
            part = None
            for cc in range(n_bp):
                a_val = a_ref[:, cc * pc:(cc + 1) * pc] if tb else a_ref[...]
                term = lax.dot_general(a_val.astype(BF16), b_ref[cc].reshape(b_rows, pc).astype(BF16), dims,
                                       preferred_element_type=F32)
                part = term if part is None else part + term
        else:
            part = lax.dot_general(a_ref[...].astype(BF16), b_ref[...].astype(BF16), dims, preferred_element_type=F32)

        def finish(acc):
            outs = epilogue(acc, *[r[...] for r in e_refs], *[r[...] for r in v_refs]) if epilogue else (acc,)
            for r, v in zip(o_refs, outs):
                r[...] = (v.reshape(2, tm // 2, tn) if out_pieces else v).astype(r.dtype)

        if nk == 1:
            finish(part)
        else:
            acc_ref = scratch[0]
            k = pl.program_id(2)

            @pl.when(k == 0)
            def _():
                acc_ref[...] = part

            @pl.when(k > 0)
            def _():
                acc_ref[...] += part

            @pl.when(k == nk - 1)
            def _():
                finish(acc_ref[...])
        end(ci_refs, co_refs, sems)

    tile = pl.BlockSpec((tm, tn), lambda i, j, k: (i, j))
    if out_pieces:
        out_tile, out_shape = pl.BlockSpec((None, 2, tm // 2, tn), lambda i, j, k: (j, 0, 0, 0)), (n_dim // tn, 2, tm // 2, tn)
    else:
        out_tile, out_shape = tile, (m_dim, n_dim)
    out = pl.pallas_call(
        body, name=name, grid=grid,
        in_specs=[a_spec, b_spec] + [tile] * n_e + [pl.BlockSpec((1, tn), lambda i, j, k: (0, j))] * n_v + c_in_specs,
        out_specs=[out_tile] * n_o + c_out_specs,
        out_shape=[_sds(out_shape, d) for d in out_dtypes] + c_outs,
        scratch_shapes=([pltpu.VMEM((tm, tn), F32)] if nk > 1 else []) + (comm.sem_scratch() if comm else []),
        compiler_params=_params(("arbitrary",) * 3 if comm else ("parallel", "parallel", "arbitrary")),
    )(a, b, *extras, *vecs, *c_ins)
    return out


def _swap_halves(x):
    width = x.shape[1]
    lane = lax.broadcasted_iota(jnp.int32, x.shape, 1)
    return jnp.where((lane % 64) < 32, pltpu.roll(x, width - 32, 1), pltpu.roll(x, 32, 1))


def _rope(x, cos, sin_signed):
    return x * cos + _swap_halves(x) * sin_signed


def _rope_t(d, cos, sin_signed):
    return d * cos - _swap_halves(d) * sin_signed


def _rms(x, g):
    rstd = lax.rsqrt(jnp.mean(x * x, axis=-1, keepdims=True) + RMS_EPS)
    return x * rstd * g


def _rms_bwd(x, g, dy):
    rstd = lax.rsqrt(jnp.mean(x * x, axis=-1, keepdims=True) + RMS_EPS)
    xh = x * rstd
    dyg = dy * g
    dx = rstd * (dyg - xh * jnp.mean(dyg * xh, axis=-1, keepdims=True))
    return dx, jnp.sum(dy * xh, axis=0, keepdims=True)


def _layer_norm(r, g, b):
    mu = jnp.mean(r, axis=-1, keepdims=True)
    xc = r - mu
    var = jnp.mean(xc * xc, axis=-1, keepdims=True)
    return xc * lax.rsqrt(var + LN_EPS) * g + b


def _dot(a, b, dims):
    return lax.dot_general(a, b, (dims, ((), ())), preferred_element_type=F32)


_NN, _NT, _TN = ((1,), (0,)), ((1,), (1,)), ((0,), (0,))


def _in_proj(x_bf, w_in_p, cos, sin):
    n_rows = x_bf.shape[0]
    ts = 512
    dils = [dil for _, dil in PATTERNS]
    n_p = len(dils)

    def body(x_ref, w_ref, cos_ref, sin_ref, *refs):
        view_refs, (cq_ref, kpe_ref, ckv_ref, scratch) = refs[:n_p], refs[n_p:]
        xb = x_ref[...]
        cos_v, sin_v = cos_ref[...], sin_ref[...]

        def seg(lo, hi):
            return _dot(xb, w_ref[:, lo:hi], _NN)

        def put(part, val):
            for ref, dil in zip(view_refs, dils):
                pieces = [val] if dil == 1 else _deinterleave(val, dil, scratch)
                for r, piece in enumerate(pieces):
                    lo = (3 * r + part) * A_WIDTH
                    ref[:, lo:lo + A_WIDTH] = piece.astype(BF16)

        put(0, _rope(seg(0, 512), cos_v, sin_v) * A_SCALE)
        put(1, _rope(seg(512, 1024), cos_v, sin_v))
        put(2, seg(1024, 1536))
        cq_ref[...] = seg(1536, 1792)
        kpe_ref[...] = _rope(seg(1792, 1920), cos_v[:, :LANES], sin_v[:, :LANES]).astype(BF16)
        ckv_ref[...] = seg(1920, 2048)

    row = lambda c, d=1: pl.BlockSpec((ts // d, c * d), lambda i: (i, 0))
    return pl.pallas_call(
        body, name="in_proj", grid=(n_rows // ts,),
        in_specs=[row(D_MODEL), pl.BlockSpec((D_MODEL, IN_PAD), lambda i: (0, 0)), row(A_WIDTH), row(A_WIDTH)],
        out_specs=[row(3 * A_WIDTH, d) for d in dils] + [row(Q_LORA), row(LANES), row(KV_LORA)],
        out_shape=[_sds((n_rows // d, d * 3 * A_WIDTH), BF16) for d in dils]
        + [_sds((n_rows, Q_LORA)), _sds((n_rows, LANES), BF16), _sds((n_rows, KV_LORA))],
        scratch_shapes=[pltpu.VMEM((A_WIDTH // LANES, ts, LANES), F32)],
        compiler_params=_params(("parallel",)),
    )(x_bf, w_in_p, cos, sin)


def _band_mask(m, heads):
    qi = lax.broadcasted_iota(jnp.int32, (heads * SPAN, 2 * SPAN), 0) % SPAN
    kj = lax.broadcasted_iota(jnp.int32, (heads * SPAN, 2 * SPAN), 1)
    return ((kj < SPAN) & (kj >= qi) & (m > 0)) | ((kj >= SPAN) & ((kj - SPAN) <= qi))


def _even_lanes(rows):
    return lax.broadcasted_iota(jnp.int32, (rows, LANES), 1) < A_HEAD_DIM


DIL_GROUP = 4
HEAD_SLOT = LANES // A_HEADS
OL_WIDTH = A_WIDTH + LANES


DQKV_WIDTH = 3 * A_WIDTH // 2


def _pack2(a, b):
    hi = lax.bitcast_convert_type(a.astype(jnp.bfloat16).astype(F32), jnp.uint32) & jnp.uint32(0xFFFF0000)
    lo = lax.shift_right_logical(lax.bitcast_convert_type(b.astype(jnp.bfloat16).astype(F32), jnp.uint32), jnp.uint32(16))
    return lax.bitcast_convert_type(hi | lo, F32)


def _unpack2(p):
    u = lax.bitcast_convert_type(p, jnp.uint32)
    return (lax.bitcast_convert_type(u & jnp.uint32(0xFFFF0000), F32),
            lax.bitcast_convert_type(lax.shift_left(u, jnp.uint32(16)), F32))


def _slot_lane(h):
    return HEAD_SLOT * (h // 2) + A_HEAD_DIM * (h % 2)


def _slot_head(lane):
    return 2 * (jnp.right_shift(lane, 4) & 3) + jnp.right_shift(lane, 6)


def _dil_fwd(qkv_view, dil, comm=None):
    nb = qkv_view.shape[0] // SPAN
    group = min(dil, DIL_GROUP)
    grid = (dil // group, nb)
    c_ins, c_in_specs, c_outs, c_out_specs, begin, end = _hosted(comm, grid)
    n_ci, n_co = len(c_ins), len(c_outs)

    def body(cur_ref, prev_ref, *refs):
        ci_refs, ol_ref, co_refs, sems = refs[:n_ci], refs[n_ci], refs[n_ci + 1:n_ci + 1 + n_co], refs[n_ci + 1 + n_co:]
        begin(ci_refs, co_refs, sems)
        mask = _band_mask(pl.program_id(1), 1)
        even, even2 = _even_lanes(SPAN), _even_lanes(2 * SPAN)
        pair_of_lane = jnp.right_shift(lax.broadcasted_iota(jnp.int32, (SPAN, LANES), 1), 4) & 3
        for g in range(group):
            lse_c = jnp.zeros((SPAN, LANES), F32)
            for p in range(A_HEADS // 2):
                q_sl, k_sl, v_sl = (slice((3 * g + t) * A_WIDTH + p * LANES, (3 * g + t) * A_WIDTH + (p + 1) * LANES)
                                    for t in range(3))
                q2 = cur_ref[:, q_sl]
                kcat = jnp.concatenate([prev_ref[:, k_sl], cur_ref[:, k_sl]], axis=0)
                vcat = jnp.concatenate([prev_ref[:, v_sl], cur_ref[:, v_sl]], axis=0)
                zero, one = jnp.zeros_like(q2), jnp.ones_like(vcat)
                res, lses = [], []
                for first in (True, False):
                    qh = jnp.where(even, q2, zero) if first else jnp.where(even, zero, q2)
                    vh = jnp.where(even2, vcat, one) if first else jnp.where(even2, one, vcat)
                    s = jnp.where(mask, _dot(qh, kcat, _NT), NEG)
                    mx = jnp.max(s, axis=-1, keepdims=True)
                    r = _dot(jnp.exp(s - mx).astype(BF16), vh, _NN)
                    den = pltpu.roll(r, A_HEAD_DIM, 1)
                    res.append(r / den)
                    lses.append(mx + jnp.log(den))
                o_lo = OL_WIDTH * g + p * LANES
                ol_ref[:, o_lo:o_lo + LANES] = jnp.where(even, res[0], res[1])
                lse_c = jnp.where(pair_of_lane == p, jnp.where(even, lses[0], lses[1]), lse_c)
            ol_ref[:, OL_WIDTH * g + A_WIDTH:OL_WIDTH * (g + 1)] = lse_c
        end(ci_refs, co_refs, sems)

    return pl.pallas_call(
        body, name=f"dil_fwd_d{dil}", grid=grid,
        in_specs=[pl.BlockSpec((SPAN, group * 3 * A_WIDTH), lambda r, m: (m, r)),
                  pl.BlockSpec((SPAN, group * 3 * A_WIDTH), lambda r, m: (jnp.maximum(m - 1, 0), r))] + c_in_specs,
        out_specs=[pl.BlockSpec((SPAN, group * OL_WIDTH), lambda r, m: (m, r))] + c_out_specs,
        out_shape=[_sds((qkv_view.shape[0], dil * OL_WIDTH))] + c_outs,
        scratch_shapes=comm.sem_scratch() if comm else [],
        compiler_params=_params(("arbitrary", "arbitrary") if comm else ("parallel", "arbitrary")),
    )(qkv_view, qkv_view, *c_ins)


def _slot_spread_matrix():
    i = lax.broadcasted_iota(jnp.int32, (LANES, A_WIDTH), 0)
    h = lax.broadcasted_iota(jnp.int32, (LANES, A_WIDTH), 1) // A_HEAD_DIM
    return (i == HEAD_SLOT * (h // 2) + A_HEAD_DIM * (h % 2)).astype(BF16)


def _spread_slots(c, spread):
    hi = c.astype(BF16)
    r1 = c - hi.astype(F32)
    mid = r1.astype(BF16)
    lo = (r1 - mid.astype(F32)).astype(BF16)
    return _dot(hi, spread, _NN) + _dot(mid, spread, _NN) + _dot(lo, spread, _NN)


def _pattern_weights(ol1, ol2, ol3):
    l1, l2, l3 = ol1[:, A_WIDTH:], ol2[:, A_WIDTH:], ol3[:, A_WIDTH:]
    mx = jnp.maximum(jnp.maximum(l1, l2), l3)
    e1, e2, e3 = jnp.exp(l1 - mx), jnp.exp(l2 - mx), jnp.exp(l3 - mx)
    inv = 1.0 / (e1 + e2 + e3)
    return e1 * inv, e2 * inv, e3 * inv


def _mix_fwd(ols, b_out, a_g, b_g):
    def fn(ol1, ol2, ol3, b, ag, bg, spread):
        ws = [_spread_slots(w, spread) for w in _pattern_weights(ol1, ol2, ol3)]
        a = ws[0] * ol1[:, :A_WIDTH] + ws[1] * ol2[:, :A_WIDTH] + ws[2] * ol3[:, :A_WIDTH]
        return a, jnp.concatenate([_rms(a, ag), _rms(b, bg)], axis=1)

    n_rows = b_out.shape[0]
    return _rowwise("mix_fwd", fn, [*ols, b_out], [_sds((n_rows, A_WIDTH)), _sds((n_rows, 2 * A_WIDTH), BF16)],
                    consts=[a_g, b_g, _slot_spread_matrix()], in_dils=[dil for _, dil in PATTERNS] + [1])


def _mla_q_prep(cq, g, w_uq_p, cos, sin):
    def fn(cq_v, cos_v, sin_v, g_v, w_v):
        cqn = _rms(cq_v, g_v).astype(BF16)
        q = _dot(cqn, w_v, _NN)
        qf = [jnp.concatenate([q[:, h * QK_NOPE:(h + 1) * QK_NOPE],
                               _rope(q[:, 512 + h * LANES:512 + (h + 1) * LANES], cos_v[:, :LANES], sin_v[:, :LANES])], axis=1)
              for h in range(MLA_HEADS)]
        return cqn, qf

    n_rows = cq.shape[0]
    return _rowwise("mla_q_prep", fn, [cq, cos, sin], [_sds((n_rows, Q_LORA), BF16), _sds((MLA_HEADS, n_rows, QK_PAD), BF16)],
                    consts=[g, w_uq_p])


def _mla_kv_prep(ckv, kpe, g, w_ukv_p):
    def fn(ckv_v, kpe_v, g_v, w_v):
        ckvn = _rms(ckv_v, g_v).astype(BF16)
        kvv = _dot(ckvn, w_v, _NN)
        kf = [jnp.concatenate([kvv[:, h * QK_NOPE:(h + 1) * QK_NOPE], kpe_v.astype(F32)], axis=1) for h in range(MLA_HEADS)]
        return ckvn, kf, kvv[:, 512:]

    n_rows = ckv.shape[0]
    return _rowwise("mla_kv_prep", fn, [ckv, kpe],
                    [_sds((n_rows, KV_LORA), BF16), _sds((MLA_HEADS, n_rows, QK_PAD), BF16), _sds((n_rows, MLA_WIDTH), BF16)],
                    consts=[g, w_ukv_p])


MLA_FWD_TILES = (1024, 2048)
MLA_BWD_TILES = (1024, 1024)


def _mla_fwd(qf, kf, v, comm=None):
    n_rows = v.shape[0]
    tq, tk = min(MLA_FWD_TILES[0], n_rows), min(MLA_FWD_TILES[1], n_rows)
    nq, nk = n_rows // tq, n_rows // tk
    grid = (MLA_HEADS, nq, nk)
    exp2_scale = MLA_SCALE * math.log2(math.e)
    c_ins, c_in_specs, c_outs, c_out_specs, begin, end = _hosted(comm, grid)
    n_ci, n_co = len(c_ins), len(c_outs)

    def last_k(i):
        return jnp.right_shift(i * tq + tq - 1, int(math.log2(tk)))

    def body(q_ref, k_ref, v_ref, *refs):
        ci_refs, (o_ref, lse_ref), co_refs = refs[:n_ci], refs[n_ci:n_ci + 2], refs[n_ci + 2:n_ci + 2 + n_co]
        m_sc, l_sc, acc_sc = refs[n_ci + 2 + n_co:n_ci + 5 + n_co]
        sems = refs[n_ci + 5 + n_co:]
        i, j = pl.program_id(1), pl.program_id(2)
        begin(ci_refs, co_refs, sems)

        @pl.when(j == 0)
        def _():
            m_sc[...] = jnp.full(m_sc.shape, NEG, F32)
            l_sc[...] = jnp.zeros(l_sc.shape, F32)
            acc_sc[...] = jnp.zeros(acc_sc.shape, F32)

        def step(masked):
            s = _dot(q_ref[...], k_ref[...], _NT)
            if masked:
                row = lax.broadcasted_iota(jnp.int32, (tq, tk), 0) + i * tq
                col = lax.broadcasted_iota(jnp.int32, (tq, tk), 1) + j * tk
                s = jnp.where(col <= row, s, NEG)
            m_prev = m_sc[...]
            m_new = jnp.maximum(m_prev, jnp.max(s, axis=-1, keepdims=True))
            alpha = jnp.exp2((m_prev - m_new) * exp2_scale)
            p = jnp.exp2((s - m_new) * exp2_scale)
            l_sc[...] = alpha * l_sc[...] + jnp.sum(p, axis=-1, keepdims=True)
            acc_sc[...] = alpha * acc_sc[...] + _dot(p.astype(BF16), v_ref[...], _NN)
            m_sc[...] = m_new

        active = j * tk <= i * tq + tq - 1
        crosses = (j + 1) * tk - 1 > i * tq

        @pl.when(active & jnp.logical_not(crosses))
        def _():
            step(False)

        @pl.when(active & crosses)
        def _():
            step(True)

        @pl.when(j == last_k(i))
        def _():
            o_ref[...] = acc_sc[...] / l_sc[...]
            lse_ref[...] = jnp.broadcast_to(m_sc[...] * MLA_SCALE + jnp.log(l_sc[...]), (tq, V_DIM))

        end(ci_refs, co_refs, sems)

    return pl.pallas_call(
        body, name="mla_fwd", grid=grid,
        in_specs=[pl.BlockSpec((None, tq, QK_PAD), lambda h, i, j: (h, i, 0)),
                  pl.BlockSpec((None, tk, QK_PAD), lambda h, i, j: (h, jnp.minimum(j, last_k(i)), 0)),
                  pl.BlockSpec((tk, V_DIM), lambda h, i, j: (jnp.minimum(j, last_k(i)), h))] + c_in_specs,
        out_specs=[pl.BlockSpec((tq, V_DIM), lambda h, i, j: (i, h)), pl.BlockSpec((tq, V_DIM), lambda h, i, j: (i, h))]
        + c_out_specs,
        out_shape=[_sds((n_rows, MLA_WIDTH)), _sds((n_rows, MLA_WIDTH))] + c_outs,
        scratch_shapes=[pltpu.VMEM((tq, 1), F32), pltpu.VMEM((tq, 1), F32), pltpu.VMEM((tq, V_DIM), F32)]
        + (comm.sem_scratch() if comm else []),
        compiler_params=_params(("arbitrary",) * 3 if comm else ("parallel", "parallel", "arbitrary")),
    )(qf, kf, v, *c_ins)


def _mm_res_ln(name, a, w, xres, g, b):
    def epi(acc, xr, g_v, b_v):
        r = ALPHA * xr + acc
        y = _layer_norm(r, g_v, b_v)
        return y, y, r

    return _mm(name, a, w, (F32, BF16, F32), tm=1024, tn=D_MODEL, tk=1024, epilogue=epi, extras=[xres], vecs=[g, b])


def _mm_relu2(x_bf, w, comm=None):
    def epi(acc):
        r = jnp.maximum(acc, 0.0)
        return (r * r,)

    return _mm("ff1", x_bf, w, (BF16,), tm=1024, tn=1024, tk=D_MODEL, epilogue=epi, b_pieces=True, comm=comm)


def _loss_fn(y, t):
    def fn(y_v, t_v):
        d = y_v - t_v
        part = jnp.sum(jnp.sum(d * d, axis=1, keepdims=True), axis=0, keepdims=True)
        return d * (1.0 / D_MODEL), part

    dy, part = _rowwise("loss", fn, [y, t], [_sds(y.shape)], reds=[_sds((1, 1))])
    return part * (0.5 / D_MODEL), dy


def _ln_bwd(name, dy, r, g):
    def fn(dy_v, r_v, g_v):
        mu = jnp.mean(r_v, axis=-1, keepdims=True)
        xc = r_v - mu
        rstd = lax.rsqrt(jnp.mean(xc * xc, axis=-1, keepdims=True) + LN_EPS)
        xh = xc * rstd
        dxh = dy_v * g_v
        dr = rstd * (dxh - jnp.mean(dxh, axis=-1, keepdims=True) - xh * jnp.mean(dxh * xh, axis=-1, keepdims=True))
        return dr, dr, jnp.sum(dy_v * xh, axis=0, keepdims=True), jnp.sum(dy_v, axis=0, keepdims=True)

    return _rowwise(name, fn, [dy, r], [_sds(dy.shape), _sds(dy.shape, BF16)], consts=[g],
                    reds=[_sds((1, D_MODEL)), _sds((1, D_MODEL))])


def _head_sum_matrix():
    i = lax.broadcasted_iota(jnp.int32, (A_WIDTH, LANES), 0) // A_HEAD_DIM
    return (i == _slot_head(lax.broadcasted_iota(jnp.int32, (A_WIDTH, LANES), 1))).astype(BF16)


def _mix_bwd(dmixed, a_out, b_out, ols, a_g, b_g):
    def fn(dm, a, b, ol1, ol2, ol3, ag, bg, sum_mat, spread):
        da, dga = _rms_bwd(a, ag, dm[:, :A_WIDTH])
        db, dgb = _rms_bwd(b, bg, dm[:, A_WIDTH:])
        t = da * a
        t_hi = t.astype(BF16)
        t_lo = (t - t_hi.astype(F32)).astype(BF16)
        tsum = _dot(t_hi, sum_mat, _NN) + _dot(t_lo, sum_mat, _NN)
        tb = db * b
        delta_b = jnp.concatenate(
            [jnp.broadcast_to(jnp.sum(tb[:, h * V_DIM:(h + 1) * V_DIM], axis=-1, keepdims=True), (tb.shape[0], V_DIM))
             for h in range(MLA_HEADS)], axis=1)
        dods = [jnp.concatenate([_spread_slots(w, spread) * da, w * tsum], axis=1) for w in _pattern_weights(ol1, ol2, ol3)]
        return (*dods, db, delta_b, dga, dgb)

    n_rows = a_out.shape[0]
    wide = (n_rows, A_WIDTH)
    dils = [dil for _, dil in PATTERNS]
    return _rowwise("mix_bwd", fn, [dmixed, a_out, b_out, *ols],
                    [_sds((n_rows // d, d * OL_WIDTH)) for d in dils] + [_sds(wide, BF16), _sds(wide)],
                    consts=[a_g, b_g, _head_sum_matrix(), _slot_spread_matrix()],
                    reds=[_sds((1, A_WIDTH)), _sds((1, MLA_WIDTH))], ts=256,
                    in_dils=[1, 1, 1] + dils, out_dils=dils + [1, 1])


def _dil_bwd(qkv_view, ol_view, dod_view, dil, comm=None):
    nb = qkv_view.shape[0] // SPAN
    group = min(dil, DIL_GROUP)
    grid = (dil // group, nb + 1)
    c_ins, c_in_specs, c_outs, c_out_specs, begin, end = _hosted(comm, grid)
    n_ci, n_co = len(c_ins), len(c_outs)

    def body(cur_ref, prev_ref, ol_ref, dod_ref, *refs):
        ci_refs, out_ref, co_refs = refs[:n_ci], refs[n_ci], refs[n_ci + 1:n_ci + 1 + n_co]
        carry, sems = refs[n_ci + 1 + n_co], refs[n_ci + 2 + n_co:]
        m = pl.program_id(1)
        begin(ci_refs, co_refs, sems)

        def write_packed(g, pairs):
            lo = DQKV_WIDTH * g
            for p, (dq_p, dk_p, _) in enumerate(pairs):
                out_ref[:, lo + p * LANES:lo + (p + 1) * LANES] = _pack2(dq_p, dk_p)
            for p in range(2):
                out_ref[:, lo + A_WIDTH + p * LANES:lo + A_WIDTH + (p + 1) * LANES] = _pack2(pairs[p][2], pairs[p + 2][2])

        @pl.when(m == 0)
        def _():
            carry[...] = jnp.zeros(carry.shape, F32)

        @pl.when(m < nb)
        def _():
            mask, even = _band_mask(m, 2), _even_lanes(SPAN)
            for g in range(group):
                done = []
                for p in range(A_HEADS // 2):
                    q_sl, k_sl, v_sl = (slice((3 * g + t) * A_WIDTH + p * LANES, (3 * g + t) * A_WIDTH + (p + 1) * LANES)
                                        for t in range(3))
                    do_sl = slice(OL_WIDTH * g + p * LANES, OL_WIDTH * g + (p + 1) * LANES)
                    s_e, s_o = (OL_WIDTH * g + A_WIDTH + _slot_lane(2 * p + t) for t in range(2))
                    q2, do2 = cur_ref[:, q_sl], dod_ref[:, do_sl].astype(BF16)
                    zero = jnp.zeros_like(q2)
                    qcat = jnp.concatenate([jnp.where(even, q2, zero), jnp.where(even, zero, q2)], axis=0)
                    docat = jnp.concatenate([jnp.where(even, do2, zero), jnp.where(even, zero, do2)], axis=0)
                    kcat = jnp.concatenate([prev_ref[:, k_sl], cur_ref[:, k_sl]], axis=0)
                    vcat = jnp.concatenate([prev_ref[:, v_sl], cur_ref[:, v_sl]], axis=0)
                    lse_c = jnp.concatenate([ol_ref[:, s_e:s_e + 1], ol_ref[:, s_o:s_o + 1]], axis=0)
                    dl_c = jnp.concatenate([dod_ref[:, s_e:s_e + 1], dod_ref[:, s_o:s_o + 1]], axis=0)
                    pr = jnp.exp(jnp.where(mask, _dot(qcat, kcat, _NT), NEG) - lse_c)
                    ds = (pr * (_dot(docat, vcat, _NT) - dl_c)).astype(BF16)
                    dq = _dot(ds, kcat, _NN)
                    dk2 = _dot(ds, qcat, _TN)
                    dv2 = _dot(pr.astype(BF16), docat, _TN)
                    done.append((carry[:, q_sl], carry[:, k_sl] + dk2[:SPAN], carry[:, v_sl] + dv2[:SPAN]))
                    carry[:, q_sl] = jnp.where(even, dq[:SPAN], dq[SPAN:])
                    carry[:, k_sl] = dk2[SPAN:]
                    carry[:, v_sl] = dv2[SPAN:]
                write_packed(g, done)

        @pl.when(m == nb)
        def _():
            for g in range(group):
                write_packed(g, [tuple(carry[:, (3 * g + t) * A_WIDTH + p * LANES:(3 * g + t) * A_WIDTH + (p + 1) * LANES]
                                       for t in range(3)) for p in range(A_HEADS // 2)])

        end(ci_refs, co_refs, sems)

    last = nb - 1

    def cur(width):
        return pl.BlockSpec((SPAN, group * width), lambda r, m: (jnp.minimum(m, last), r))

    def prev(width):
        return pl.BlockSpec((SPAN, group * width), lambda r, m: (jnp.clip(m - 1, 0, last), r))

    return pl.pallas_call(
        body, name=f"dil_bwd_d{dil}", grid=grid,
        in_specs=[cur(3 * A_WIDTH), prev(3 * A_WIDTH), cur(OL_WIDTH), cur(OL_WIDTH)] + c_in_specs,
        out_specs=[prev(DQKV_WIDTH)] + c_out_specs,
        out_shape=[_sds((qkv_view.shape[0], dil * DQKV_WIDTH))] + c_outs,
        scratch_shapes=[pltpu.VMEM((SPAN, group * 3 * A_WIDTH), F32)] + (comm.sem_scratch() if comm else []),
        compiler_params=_params(("arbitrary", "arbitrary") if comm else ("parallel", "arbitrary")),
    )(qkv_view, qkv_view, ol_view, dod_view, *c_ins)


def _mla_bwd(qf, kf, v, lse, do, delta, comm=None):
    n_rows = v.shape[0]
    tq, tk = min(MLA_BWD_TILES[0], n_rows), min(MLA_BWD_TILES[1], n_rows)
    nq, nk = n_rows // tq, n_rows // tk

    def first_q(j):
        return jnp.right_shift(j * tk, int(math.log2(tq)))

    grid = (MLA_HEADS, nk, nq)
    c_ins, c_in_specs, c_outs, c_out_specs, begin, end = _hosted(comm, grid)
    n_ci, n_co = len(c_ins), len(c_outs)

    def body(q_ref, k_ref, v_ref, lse_ref, do_ref, dl_ref, *refs):
        ci_refs, (dq_ref, dk_ref, dv_ref), co_refs = refs[:n_ci], refs[n_ci:n_ci + 3], refs[n_ci + 3:n_ci + 3 + n_co]
        dk_sc, dv_sc = refs[n_ci + 3 + n_co:n_ci + 5 + n_co]
        sems = refs[n_ci + 5 + n_co:]
        j, i = pl.program_id(1), pl.program_id(2)
        begin(ci_refs, co_refs, sems)

        @pl.when((j == 0) & (i == 0))
        def _():
            dq_ref[...] = jnp.zeros(dq_ref.shape, F32)

        @pl.when(i == first_q(j))
        def _():
            dk_sc[...] = jnp.zeros(dk_sc.shape, F32)
            dv_sc[...] = jnp.zeros(dv_sc.shape, F32)

        def step(masked):
            q, k, dob = q_ref[...], k_ref[...], do_ref[...]
            s = _dot(q, k, _NT) * MLA_SCALE
            if masked:
                row = lax.broadcasted_iota(jnp.int32, (tq, tk), 0) + i * tq
                col = lax.broadcasted_iota(jnp.int32, (tq, tk), 1) + j * tk
                s = jnp.where(col <= row, s, NEG)
            p = jnp.exp(s - lse_ref[:, :1])
            dp = _dot(dob, v_ref[...], _NT)
            ds = (p * (dp - dl_ref[:, :1]) * MLA_SCALE).astype(BF16)
            dv_sc[...] += _dot(p.astype(BF16), dob, _TN)
            dk_sc[...] += _dot(ds, q, _TN)
            rows = pl.ds(pl.multiple_of(i * tq, tq), tq)
            dq_ref[rows, :] += _dot(ds, k, _NN)

        active = i >= first_q(j)
        crosses = (j + 1) * tk - 1 > i * tq

        @pl.when(active & jnp.logical_not(crosses))
        def _():
            step(False)

        @pl.when(active & crosses)
        def _():
            step(True)

        @pl.when(i == nq - 1)
        def _():
            dk_ref[...] = dk_sc[...]
            dv_ref[...] = dv_sc[...]

        end(ci_refs, co_refs, sems)

    qrow = lambda h, j, i: (jnp.maximum(i, first_q(j)), h)
    return pl.pallas_call(
        body, name="mla_bwd", grid=grid,
        in_specs=[pl.BlockSpec((None, tq, QK_PAD), lambda h, j, i: (h, jnp.maximum(i, first_q(j)), 0)),
                  pl.BlockSpec((None, tk, QK_PAD), lambda h, j, i: (h, j, 0)),
                  pl.BlockSpec((tk, V_DIM), lambda h, j, i: (j, h)),
                  pl.BlockSpec((tq, V_DIM), qrow), pl.BlockSpec((tq, V_DIM), qrow), pl.BlockSpec((tq, V_DIM), qrow)]
        + c_in_specs,
        out_specs=[pl.BlockSpec((None, n_rows, QK_PAD), lambda h, j, i: (h, 0, 0)),
                   pl.BlockSpec((None, tk, QK_PAD), lambda h, j, i: (h, j, 0)),
                   pl.BlockSpec((tk, V_DIM), lambda h, j, i: (j, h))] + c_out_specs,
        out_shape=[_sds((MLA_HEADS, n_rows, QK_PAD)), _sds((MLA_HEADS, n_rows, QK_PAD)), _sds((n_rows, MLA_WIDTH))] + c_outs,
        scratch_shapes=[pltpu.VMEM((tk, QK_PAD), F32), pltpu.VMEM((tk, V_DIM), F32)] + (comm.sem_scratch() if comm else []),
        compiler_params=_params(("arbitrary",) * 3 if comm else ("parallel", "arbitrary", "arbitrary")),
    )(qf, kf, v, lse, do, delta, *c_ins)


def _mla_bwd_prep(dqf, dkf, dv, cos, sin):
    def fn(dqf_v, dkf_v, dv_v, cos_v, sin_v):
        dq = jnp.concatenate([dqf_v[h][:, :QK_NOPE] for h in range(MLA_HEADS)]
                             + [_rope_t(dqf_v[h][:, QK_NOPE:], cos_v[:, :LANES], sin_v[:, :LANES]) for h in range(MLA_HEADS)], axis=1)
        dkv = jnp.concatenate([dkf_v[h][:, :QK_NOPE] for h in range(MLA_HEADS)] + [dv_v], axis=1)
        dkpe = dkf_v[0][:, QK_NOPE:] + dkf_v[1][:, QK_NOPE:] + dkf_v[2][:, QK_NOPE:] + dkf_v[3][:, QK_NOPE:]
        return dq, dkv, dkpe

    n_rows = dv.shape[0]
    return _rowwise("mla_bwd_prep", fn, [dqf, dkf, dv, cos, sin],
                    [_sds((n_rows, UQ_PAD), BF16), _sds((n_rows, 1024), BF16), _sds((n_rows, LANES))], ts=256)


def _assemble_dh(dqkvs, dcqn, cq, dckvn, ckv, dkpe, cos, sin, gq, gkv, comm=None):
    def fn(g1, g2, g3, dcqn_v, cq_v, dckvn_v, ckv_v, dkpe_v, cos_v, sin_v, gq_v, gkv_v):
        dq = dk = dv = 0.0
        for packed in (g1, g2, g3):
            dq_p, dk_p = _unpack2(packed[:, :A_WIDTH])
            dv_p = jnp.concatenate(_unpack2(packed[:, A_WIDTH:]), axis=1)
            dq, dk, dv = dq + dq_p, dk + dk_p, dv + dv_p
        dqa = _rope_t(dq, cos_v, sin_v) * A_SCALE
        dka = _rope_t(dk, cos_v, sin_v)
        dcq, dgq = _rms_bwd(cq_v, gq_v, dcqn_v)
        dckv, dgkv = _rms_bwd(ckv_v, gkv_v, dckvn_v)
        dkr = _rope_t(dkpe_v, cos_v[:, :LANES], sin_v[:, :LANES])
        return jnp.concatenate([dqa, dka, dv, dcq, dkr, dckv], axis=1), dgq, dgkv

    n_rows = cq.shape[0]
    return _rowwise("assemble_dh", fn, [*dqkvs, dcqn, cq, dckvn, ckv, dkpe, cos, sin],
                    [_sds((n_rows, IN_PAD), BF16)], consts=[gq, gkv],
                    reds=[_sds((1, Q_LORA)), _sds((1, KV_LORA))], ts=256,
                    in_dils=[dil for _, dil in PATTERNS] + [1] * 7, comm=comm)


def _layer_fwd(x_f32, x_bf, w, sm, cos, sin, behind=(None, None), rest=None):
    *qkv_views, cq, kpe, ckv = _in_proj(x_bf, w["w_in"], cos, sin)
    ol_views, brought = [], []
    for i, (view, (_, dil)) in enumerate(zip(qkv_views, PATTERNS)):
        ol, *got = _dil_fwd(view, dil, comm=rest[0][i] if rest else None)
        ol_views.append(ol)
        brought.append(got)
    if rest:
        w = {**w, **rest[1](brought)}
    cqn, qf = _mla_q_prep(cq, sm["q_a_norm"], w["w_uq"], cos, sin)
    ckvn, kf, v = _mla_kv_prep(ckv, kpe, sm["kv_a_norm"], w["w_ukv"])
    b_out, b_lse, *behind_attn = _mla_fwd(qf, kf, v, comm=behind[0])
    a_out, mixed = _mix_fwd(ol_views, b_out, sm["a_out_norm"], sm["b_out_norm"])
    x1, x1_bf, r1 = _mm_res_ln("wo_ln1", mixed, w["w_o"], x_f32, sm["ln1_g"], sm["ln1_b"])
    f, *behind_ff1 = _mm_relu2(x1_bf, w["w_ff1"], comm=behind[1])
    x2, x2_bf, r2 = _mm_res_ln("ff2_ln2", f, w["w_ff2"], x1, sm["ln2_g"], sm["ln2_b"])
    saved = dict(w=w, x_bf=x_bf, qkv_views=qkv_views, ol_views=ol_views, cq=cq, ckv=ckv, cqn=cqn, qf=qf, ckvn=ckvn,
                 kf=kf, v=v, b_out=b_out, b_lse=b_lse, a_out=a_out, mixed=mixed, x1_bf=x1_bf, r1=r1, f=f, r2=r2)
    return x2, x2_bf, saved, (behind_attn, behind_ff1)


def _layer_bwd(dx2, w, sm, sv, cos, sin, reduce_of, pending=None):
    dr2, dr2_bf, dg2, db2 = _ln_bwd("ln2_bwd", dx2, sv["r2"], sm["ln2_g"])
    (dw_ff2,) = _mm("dw_ff2", sv["f"], dr2_bf, (F32,), ta=True, tm=1024, tn=1024, tk=2048)
    du, *swapped = _mm("d_u", dr2_bf, w["w_ff2"], (BF16,), tb=True, tm=1024, tn=2048, tk=D_MODEL,
                       epilogue=lambda acc, f: (acc * (2.0 * jnp.sqrt(f.astype(F32))),), extras=[sv["f"]],
                       comm=_swap_comm(pending.pieces) if pending else None)
    if pending:
        pending.after_swap(swapped)
    (dw_ff1,) = _mm("dw_ff1", sv["x1_bf"], du, (F32,), ta=True, tm=1024, tn=1024, tk=2048, out_pieces=True)
    (dy1,) = _mm("d_x1", du, w["w_ff1"], (F32,), tb=True, tm=1024, tn=1024, tk=2048,
                 epilogue=lambda acc, d: (acc + ALPHA * d,), extras=[dr2], b_pieces=True)
    dr1, dr1_bf, dg1, db1 = _ln_bwd("ln1_bwd", dy1, sv["r1"], sm["ln1_g"])
    (dw_o,) = _mm("dw_o", sv["mixed"], dr1_bf, (F32,), ta=True, tm=1024, tn=1024, tk=2048)
    early = reduce_of(dict(w_o=dw_o, w_ff1=dw_ff1, w_ff2=dw_ff2))
    (dmixed,) = _mm("d_mixed", dr1_bf, w["w_o"], (F32,), tb=True, tm=1024, tn=1024, tk=D_MODEL)
    dod1, dod2, dod3, do_b, delta_b, dga, dgb = _mix_bwd(
        dmixed, sv["a_out"], sv["b_out"], sv["ol_views"], sm["a_out_norm"], sm["b_out_norm"])
    dqkvs = []
    for i, ((_, dil), qkv_view, ol_view, dod_view) in enumerate(zip(PATTERNS, sv["qkv_views"], sv["ol_views"], (dod1, dod2, dod3))):
        swap_here = early is not None and i == 0
        dqkv, *swapped = _dil_bwd(qkv_view, ol_view, dod_view, dil, comm=_swap_comm(early.pieces) if swap_here else None)
        dqkvs.append(dqkv)
        if swap_here:
            early.after_swap(swapped)
    to_scatter = (early.s16 if early else []) + (pending.s16 if pending else [])
    dqf, dkf, dv_b, *scattered = _mla_bwd(sv["qf"], sv["kf"], sv["v"], sv["b_lse"], do_b, delta_b,
                                          comm=_scatter_comm(to_scatter) if to_scatter else None)
    if early:
        early.after_scatter(scattered[:len(early.s16)])
    if pending:
        pending.after_scatter(scattered[len(scattered) - len(pending.s16):])
    dq_b, dkv_b, dkpe = _mla_bwd_prep(dqf, dkf, dv_b, cos, sin)
    (dw_uq,) = _mm("dw_uq", sv["cqn"], dq_b, (F32,), ta=True, tm=Q_LORA, tn=1024, tk=2048)
    (dcqn,) = _mm("d_cqn", dq_b, w["w_uq"], (F32,), tb=True, tm=1024, tn=Q_LORA, tk=UQ_PAD)
    (dw_ukv,) = _mm("dw_ukv", sv["ckvn"], dkv_b, (F32,), ta=True, tm=KV_LORA, tn=1024, tk=2048)
    (dckvn,) = _mm("d_ckvn", dkv_b, w["w_ukv"], (F32,), tb=True, tm=1024, tn=KV_LORA, tk=1024)
    to_share = (early.mine if early else []) + (pending.mine if pending else [])
    dh, dgq, dgkv, *shared = _assemble_dh(dqkvs, dcqn, sv["cq"], dckvn, sv["ckv"], dkpe, cos, sin,
                                          sm["q_a_norm"], sm["kv_a_norm"], comm=_share_comm(to_share) if to_share else None)
    if early:
        early.after_share(shared[:len(early.mine)])
    if pending:
        pending.after_share(shared[len(shared) - len(pending.mine):])
    (dw_in,) = _mm("dw_in", sv["x_bf"], dh, (F32,), ta=True, tm=1024, tn=1024, tk=2048)
    (dx,) = _mm("d_x", dh, w["w_in"], (F32,), tb=True, tm=1024, tn=1024, tk=2048,
                epilogue=lambda acc, d: (acc + ALPHA * d,), extras=[dr1])
    late = reduce_of(dict(w_in=dw_in, w_uq=dw_uq, w_ukv=dw_ukv))
    dsm = dict(q_a_norm=dgq, kv_a_norm=dgkv, a_out_norm=dga, b_out_norm=dgb, ln1_g=dg1, ln1_b=db1, ln2_g=dg2, ln2_b=db2)
    return dx, dsm, late


def _pad_w_in(w):
    return jnp.concatenate([w[:, :1792], w[:, 1920:1984], jnp.zeros((w.shape[0], 64), w.dtype), w[:, 1792:1920]], axis=1)


def _unpad_w_in(w):
    return jnp.concatenate([w[:, :1792], w[:, 1920:2048], w[:, 1792:1856]], axis=1)


def _rope_tables(n_rows):
    half = A_HEAD_DIM // 2
    inv_freq = ROPE_THETA ** (-jnp.arange(half, dtype=F32) / half)
    ang = jnp.arange(n_rows, dtype=F32)[:, None] * inv_freq[None, :]
    cos = jnp.tile(jnp.cos(ang), (1, 2 * A_HEADS))
    sin = jnp.tile(jnp.concatenate([-jnp.sin(ang), jnp.sin(ang)], axis=1), (1, A_HEADS))
    return cos, sin


W_NAMES = ("w_in", "w_uq", "w_ukv", "w_o", "w_ff1", "w_ff2")
PIECE = dict(w_in=(512, 512), w_uq=(128, 256), w_ukv=(64, 256), w_o=(128, 1024), w_ff1=(512, 1024), w_ff2=(512, 1024))
SHARD_COLS = dict(w_in=496, w_uq=192, w_ukv=256, w_o=1024, w_ff1=1024, w_ff2=1024)


def _col_pieces(t, pad_to):
    rows, _, cols = t.shape
    t = jnp.pad(t, ((0, 0), (0, 0), (0, pad_to - cols)))
    return t.reshape(2, rows // 2, N_CHIPS, pad_to).transpose(2, 0, 1, 3)


def _grad_pieces(dws):
    def heads(t, rows):
        return jnp.concatenate([t[:, :512].reshape(rows, MLA_HEADS, LANES), t[:, 512:].reshape(rows, MLA_HEADS, LANES)], axis=2)

    make = dict(
        w_in=lambda t: _col_pieces(_unpad_w_in(t).reshape(D_MODEL, N_CHIPS, SHARD_COLS["w_in"]), 512),
        w_uq=lambda t: _col_pieces(heads(t, Q_LORA), 256),
        w_ukv=lambda t: _col_pieces(heads(t, KV_LORA), 256),
        w_o=lambda t: t.reshape(N_CHIPS, 2, 128, D_MODEL),
        w_ff1=lambda t: t,
        w_ff2=lambda t: t.reshape(N_CHIPS, 2, 512, D_MODEL))
    return {n: make[n](t) for n, t in dws.items()}


def _weights_from_pieces(p):
    def cols(t):
        return t.transpose(1, 2, 0, 3).reshape(2 * t.shape[2], N_CHIPS, t.shape[3])

    def heads(t, rows):
        return jnp.concatenate([t[:, :, :LANES].reshape(rows, 512), t[:, :, LANES:].reshape(rows, 512)], axis=1)

    make = dict(
        w_in=lambda t: _pad_w_in(cols(t)[:, :, :SHARD_COLS["w_in"]].reshape(D_MODEL, IN_COLS)),
        w_uq=lambda t: heads(cols(t), Q_LORA),
        w_ukv=lambda t: heads(cols(t), KV_LORA),
        w_o=lambda t: t.reshape(D_MODEL, D_MODEL),
        w_ff1=lambda t: t,
        w_ff2=lambda t: t.reshape(D_FF, D_MODEL))
    return {n: make[n](t) for n, t in p.items()}


def _my_piece(name, shard, half):
    rows, cols = shard.shape[0] // 2, shard.shape[1]
    t = lax.dynamic_slice_in_dim(shard, half * rows, rows, axis=0).astype(BF16)
    return jnp.pad(t, ((0, 0), (0, PIECE[name][1] - cols)))


MESH = pl.DeviceIdType.MESH
ANY = pl.BlockSpec(memory_space=pl.ANY)


class _Comm:
    def __init__(self, ins, out_shapes, n_sems, start, finish):
        self.ins, self.out_shapes, self.n_sems, self.start, self.finish = list(ins), list(out_shapes), n_sems, start, finish

    def sem_scratch(self):
        return [pltpu.SemaphoreType.DMA((self.n_sems,)), pltpu.SemaphoreType.DMA((self.n_sems,))]


def _comm_call(name, comm):
    n_in, n_out = len(comm.ins), len(comm.out_shapes)

    def body(*refs):
        ins, outs, sems = refs[:n_in], refs[n_in:n_in + n_out], refs[n_in + n_out:]
        comm.start(ins, outs, *sems)
        comm.finish(ins, outs, *sems)

    return pl.pallas_call(body, name=name, out_shape=comm.out_shapes, in_specs=[ANY] * n_in, out_specs=[ANY] * n_out,
                          scratch_shapes=comm.sem_scratch())(*comm.ins)


def _all_gather_comm(blocks):
    n = len(blocks)

    def plan(x_refs, out_refs, send_sems, recv_sems):
        x, y, c = lax.axis_index("x"), lax.axis_index("y"), lax.axis_index("c")
        me, sibling = (x, y, c), (x, y, 1 - c)
        chips = [(1 - x, y), (x, 1 - y), (1 - x, 1 - y)]

        def copy(t, k, blk, to, src=None):
            px, py, pc = blk
            slot = out_refs[t].at[4 * px + 2 * py + pc]
            return pltpu.make_async_remote_copy(
                src_ref=slot if src is None else src, dst_ref=slot,
                send_sem=send_sems.at[7 * t + k], recv_sem=recv_sems.at[7 * t + k], device_id=to, device_id_type=MESH)

        first = []
        for t in range(n):
            first.append(copy(t, 0, me, sibling, src=x_refs[t]))
            first += [copy(t, 1 + j, me, (*chip, c), src=x_refs[t]) for j, chip in enumerate(chips)]
        return me, sibling, chips, c, copy, first

    def start(x_refs, out_refs, send_sems, recv_sems):
        for cp in plan(x_refs, out_refs, send_sems, recv_sems)[-1]:
            cp.start()

    def finish(x_refs, out_refs, send_sems, recv_sems):
        me, sibling, chips, c, copy, sent = plan(x_refs, out_refs, send_sems, recv_sems)
        for j, chip in enumerate(chips):
            for t in range(n):
                copy(t, 1 + j, (*chip, c), me).wait_recv()
                sent.append(copy(t, 4 + j, (*chip, c), sibling))
                sent[-1].start()
        for t in range(n):
            copy(t, 0, sibling, me).wait_recv()
        for j, chip in enumerate(chips):
            for t in range(n):
                copy(t, 4 + j, (*chip, 1 - c), me).wait_recv()
        for cp in sent:
            cp.wait_send()

    return _Comm(blocks, [_sds((N_DEV,) + b.shape, b.dtype) for b in blocks], 7 * n, start, finish)


def _simple_comm(ins, out_shapes, n_sems, copies):
    def start(in_refs, out_refs, send_sems, recv_sems):
        for cp in copies(in_refs, out_refs, send_sems, recv_sems):
            cp.start()

    def finish(in_refs, out_refs, send_sems, recv_sems):
        for cp in copies(in_refs, out_refs, send_sems, recv_sems):
            cp.wait()

    return _Comm(ins, out_shapes, n_sems, start, finish)


def _swap_comm(gs):
    def copies(g_refs, got_refs, send_sems, recv_sems):
        c = lax.axis_index("c")
        sibling = (lax.axis_index("x"), lax.axis_index("y"), 1 - c)
        return [pltpu.make_async_remote_copy(src_ref=g_refs[t].at[k, 1 - c], dst_ref=got_refs[t].at[k],
                                             send_sem=send_sems.at[N_CHIPS * t + k], recv_sem=recv_sems.at[N_CHIPS * t + k],
                                             device_id=sibling, device_id_type=MESH)
                for t in range(len(gs)) for k in range(N_CHIPS)]

    return _simple_comm(gs, [_sds((N_CHIPS,) + g.shape[2:], g.dtype) for g in gs], N_CHIPS * len(gs), copies)


def _scatter_comm(hs):
    def copies(h_refs, got_refs, send_sems, recv_sems):
        x, y, c = lax.axis_index("x"), lax.axis_index("y"), lax.axis_index("c")
        chips = [(1 - x, y), (x, 1 - y), (1 - x, 1 - y)]
        return [pltpu.make_async_remote_copy(src_ref=h_refs[t].at[2 * px + py], dst_ref=got_refs[t].at[rel],
                                             send_sem=send_sems.at[3 * t + rel], recv_sem=recv_sems.at[3 * t + rel],
                                             device_id=(px, py, c), device_id_type=MESH)
                for t in range(len(hs)) for rel, (px, py) in enumerate(chips)]

    return _simple_comm(hs, [_sds((3,) + h.shape[1:], h.dtype) for h in hs], 3 * len(hs), copies)


def _share_comm(rs):
    def copies(r_refs, got_refs, send_sems, recv_sems):
        sibling = (lax.axis_index("x"), lax.axis_index("y"), 1 - lax.axis_index("c"))
        return [pltpu.make_async_remote_copy(src_ref=r_refs[t], dst_ref=got_refs[t], send_sem=send_sems.at[t],
                                             recv_sem=recv_sems.at[t], device_id=sibling, device_id_type=MESH)
                for t in range(len(rs))]

    return _simple_comm(rs, [_sds(r.shape, r.dtype) for r in rs], len(rs), copies)


SUM_STEPS = 4


def _pair_sums(name, gs, gots, c_arr):
    n = len(gs)
    dims = [(g.shape[2] // SUM_STEPS, g.shape[3]) for g in gs]

    def body(c_ref, *refs):
        for t in range(n):
            s = refs[t][...] + refs[n + t][...]
            refs[2 * n + t][...] = s
            refs[3 * n + t][...] = s.astype(BF16)

    def part(tr, pc):
        return pl.BlockSpec((None, tr, pc), lambda k, i, c_ref: (k, i, 0))

    def kept(tr, pc):
        return pl.BlockSpec((None, None, tr, pc), lambda k, i, c_ref: (k, c_ref[0], i, 0))

    outs = pl.pallas_call(
        body, name=name,
        out_shape=[_sds((N_CHIPS,) + g.shape[2:]) for g in gs] + [_sds((N_CHIPS,) + g.shape[2:], BF16) for g in gs],
        grid_spec=pltpu.PrefetchScalarGridSpec(
            num_scalar_prefetch=1, grid=(N_CHIPS, SUM_STEPS),
            in_specs=[kept(*d) for d in dims] + [part(*d) for d in dims],
            out_specs=[part(*d) for d in dims] * 2),
        compiler_params=_params(("parallel", "parallel")),
    )(c_arr, *gs, *gots)
    return list(outs[:n]), list(outs[n:])


def _chips_sums(name, h32s, gots, chip_arr):
    n = len(h32s)
    dims = [(h.shape[1] // SUM_STEPS, h.shape[2]) for h in h32s]

    def body(chip_ref, *refs):
        for t in range(n):
            got_ref = refs[n + t]
            refs[2 * n + t][...] = refs[t][...] + got_ref[0] + got_ref[1] + got_ref[2]

    return pl.pallas_call(
        body, name=name, out_shape=[_sds(h.shape[1:]) for h in h32s],
        grid_spec=pltpu.PrefetchScalarGridSpec(
            num_scalar_prefetch=1, grid=(SUM_STEPS,),
            in_specs=[pl.BlockSpec((None, tr, pc), lambda i, chip_ref: (chip_ref[0], i, 0)) for tr, pc in dims]
            + [pl.BlockSpec((3, tr, pc), lambda i, chip_ref: (0, i, 0)) for tr, pc in dims],
            out_specs=[pl.BlockSpec((tr, pc), lambda i, chip_ref: (i, 0)) for tr, pc in dims]),
        compiler_params=_params(("parallel",)),
    )(chip_arr, *h32s, *gots)


class _Reduce:
    def __init__(self, tag, dws, c, chip):
        gp = _grad_pieces(dws)
        self.tag, self.c, self.names, self.pieces = tag, c, list(gp), list(gp.values())
        self.c_arr, self.chip_arr = jnp.reshape(c, (1,)).astype(jnp.int32), jnp.reshape(chip, (1,)).astype(jnp.int32)
        self.shards = None

    def after_swap(self, gots):
        self.s32, self.s16 = _pair_sums(f"rs_pair_sum_{self.tag}", self.pieces, gots, self.c_arr)

    def after_scatter(self, parts):
        self.mine = list(_chips_sums(f"rs_chips_sum_{self.tag}", self.s32, parts, self.chip_arr))

    def after_share(self, theirs):
        self.shards = {}
        for n, a, b in zip(self.names, self.mine, theirs):
            both = jnp.where(self.c == 0, jnp.concatenate([a, b], axis=0), jnp.concatenate([b, a], axis=0))
            self.shards[n] = both[:, :SHARD_COLS[n]]

    def run_alone(self):
        self.after_swap(_comm_call(f"rs_pair_{self.tag}", _swap_comm(self.pieces)))
        self.after_scatter(_comm_call(f"rs_chips_{self.tag}", _scatter_comm(self.s16)))
        self.after_share(_comm_call(f"rs_share_{self.tag}", _share_comm(self.mine)))


class _LayerWeights:
    def __init__(self, shards, c, dev):
        self.shards, self.c, self.dev = shards, c, dev

    def comm(self, l, names=W_NAMES):
        return _all_gather_comm([_my_piece(n, self.shards[l][n], self.c) for n in names])

    def take(self, l, gathered, names=W_NAMES):
        pieces = {}
        for n, g in zip(names, gathered):
            mine = _my_piece(n, self.shards[l][n], self.c)
            pieces[n] = lax.dynamic_update_index_in_dim(g, mine, self.dev, 0).reshape((N_CHIPS, 2) + mine.shape)
        return _weights_from_pieces(pieces)

    def first(self):
        now, later = ("w_in", "w_uq", "w_ukv"), ("w_ff1", "w_ff2", "w_o")
        weights = self.take(0, _comm_call("ag_l0", self.comm(0, now)), now)
        comms = [self.comm(0, (n,)) for n in later]

        def take_rest(results):
            rest = {}
            for n, gathered in zip(later, results):
                rest.update(self.take(0, gathered, (n,)))
            return rest

        return weights, (comms, take_rest)


def _all_reduce_small(vec):
    rows, lanes = vec.shape

    def gather_body(x_ref, out_ref, send_sems, recv_sems):
        x, y, c = lax.axis_index("x"), lax.axis_index("y"), lax.axis_index("c")
        me, sibling = (x, y, c), (x, y, 1 - c)
        chips = [(1 - x, y), (x, 1 - y), (1 - x, 1 - y)]

        def slot(px, py, pc):
            return out_ref.at[4 * px + 2 * py + pc]

        def copy(k, blk, to, src=None):
            return pltpu.make_async_remote_copy(
                src_ref=slot(*blk) if src is None else src, dst_ref=slot(*blk),
                send_sem=send_sems.at[k], recv_sem=recv_sems.at[k], device_id=to, device_id_type=MESH)

        out_ref[4 * x + 2 * y + c] = x_ref[...]
        first = [copy(0, me, sibling, src=x_ref)]
        first += [copy(1 + j, me, (*chip, c), src=x_ref) for j, chip in enumerate(chips)]
        for cp in first:
            cp.start()
        passed = [copy(4 + j, (*chip, c), sibling) for j, chip in enumerate(chips)]
        for j, chip in enumerate(chips):
            copy(1 + j, (*chip, c), me).wait_recv()
            passed[j].start()
        copy(0, sibling, me).wait_recv()
        for j, chip in enumerate(chips):
            copy(4 + j, (*chip, 1 - c), me).wait_recv()
        for cp in first + passed:
            cp.wait_send()

    vmem = pl.BlockSpec(memory_space=pltpu.VMEM)
    allv = pl.pallas_call(
        gather_body, name="small_all_gather", out_shape=_sds((N_DEV, rows, lanes)),
        in_specs=[vmem], out_specs=vmem,
        scratch_shapes=[pltpu.SemaphoreType.DMA((7,)), pltpu.SemaphoreType.DMA((7,))],
    )(vec)

    def sum_body(a_ref, o_ref):
        acc = a_ref[0]
        for d in range(1, N_DEV):
            acc = acc + a_ref[d]
        o_ref[...] = acc

    return pl.pallas_call(sum_body, name="small_sum", out_shape=_sds((rows, lanes)), in_specs=[vmem], out_specs=vmem)(allv)


def _adamw(name, w, g, m, v, ts=512, comm=None):
    def fn(w_v, g_v, m_v, v_v):
        m_n = ADAM_B1 * m_v + (1.0 - ADAM_B1) * g_v
        v_n = ADAM_B2 * v_v + (1.0 - ADAM_B2) * (g_v * g_v)
        m_hat = m_n / (1.0 - ADAM_B1 ** ADAM_STEP)
        v_hat = v_n / (1.0 - ADAM_B2 ** ADAM_STEP)
        delta = -ADAM_LR * (m_hat / (jnp.sqrt(v_hat) + ADAM_EPS) + ADAM_WD * w_v)
        return delta, m_n, v_n

    return _rowwise(name, fn, [w, g, m, v], [_sds(w.shape)] * 3, ts=ts, comm=comm)


def _pack_small(per_layer):
    flat = jnp.concatenate([per_layer[l][n].reshape(-1) for l in range(DEPTH) for n in SMALL_NAMES])
    return jnp.pad(flat, (0, SMALL_ROWS * LANES - flat.shape[0])).reshape(SMALL_ROWS, LANES)


def _unpack_small(packed):
    flat = packed.reshape(-1)
    per = sum(SMALL_SIZES)
    out = {}
    for n, size, off in zip(SMALL_NAMES, SMALL_SIZES, [sum(SMALL_SIZES[:i]) for i in range(len(SMALL_SIZES))]):
        out[n] = jnp.stack([flat[l * per + off:l * per + off + size] for l in range(DEPTH)])
    return out


NEXT_WEIGHTS_BEHIND = (("w_ff1", "w_ff2"), ("w_in", "w_uq", "w_ukv", "w_o"))


def _fwd_bwd(x, target, layer_weights, smalls, on_layer_grads):
    depth = len(smalls)
    cos, sin = _rope_tables(x.shape[0])
    h_f32, h_bf = x, x.astype(BF16)
    saved = []
    w, rest = layer_weights.first()
    for l in range(depth):
        more = l + 1 < depth
        behind = [layer_weights.comm(l + 1, names) if more else None for names in NEXT_WEIGHTS_BEHIND]
        h_f32, h_bf, sv, brought = _layer_fwd(h_f32, h_bf, w, smalls[l], cos, sin, behind, rest)
        saved.append(sv)
        if more:
            w, rest = {}, None
            for names, got in zip(NEXT_WEIGHTS_BEHIND, brought):
                w.update(layer_weights.take(l + 1, got, names))
    loss_part, dh = _loss_fn(h_f32, target)
    small_grads, pending = [None] * depth, None
    for l in reversed(range(depth)):
        dh, small_grads[l], pending = _layer_bwd(dh, saved[l]["w"], smalls[l], saved[l], cos, sin,
                                                 lambda dws, l=l: on_layer_grads(l, dws), pending)
    return loss_part, dh, small_grads, pending


def kernel(x, w_in, q_a_norm, kv_a_norm, w_uq, w_ukv, a_out_norm, b_out_norm, w_o, ln1_g, ln1_b, w_ff1, w_ff2, ln2_g, ln2_b, loss_target, m_w_in, m_q_a_norm, m_kv_a_norm, m_w_uq, m_w_ukv, m_a_out_norm, m_b_out_norm, m_w_o, m_ln1_g, m_ln1_b, m_w_ff1, m_w_ff2, m_ln2_g, m_ln2_b, v_w_in, v_q_a_norm, v_kv_a_norm, v_w_uq, v_w_ukv, v_a_out_norm, v_b_out_norm, v_w_o, v_ln1_g, v_ln1_b, v_w_ff1, v_w_ff2, v_ln2_g, v_ln2_b):
    c = lax.axis_index("c")
    chip = 2 * lax.axis_index("x") + lax.axis_index("y")
    big = dict(w_in=w_in, w_uq=w_uq, w_ukv=w_ukv, w_o=w_o, w_ff1=w_ff1, w_ff2=w_ff2)
    big_m = dict(w_in=m_w_in, w_uq=m_w_uq, w_ukv=m_w_ukv, w_o=m_w_o, w_ff1=m_w_ff1, w_ff2=m_w_ff2)
    big_v = dict(w_in=v_w_in, w_uq=v_w_uq, w_ukv=v_w_ukv, w_o=v_w_o, w_ff1=v_w_ff1, w_ff2=v_w_ff2)
    small = dict(q_a_norm=q_a_norm, kv_a_norm=kv_a_norm, a_out_norm=a_out_norm, b_out_norm=b_out_norm,
                 ln1_g=ln1_g, ln1_b=ln1_b, ln2_g=ln2_g, ln2_b=ln2_b)
    small_m = dict(q_a_norm=m_q_a_norm, kv_a_norm=m_kv_a_norm, a_out_norm=m_a_out_norm, b_out_norm=m_b_out_norm,
                   ln1_g=m_ln1_g, ln1_b=m_ln1_b, ln2_g=m_ln2_g, ln2_b=m_ln2_b)
    small_v = dict(q_a_norm=v_q_a_norm, kv_a_norm=v_kv_a_norm, a_out_norm=v_a_out_norm, b_out_norm=v_b_out_norm,
                   ln1_g=v_ln1_g, ln1_b=v_ln1_b, ln2_g=v_ln2_g, ln2_b=v_ln2_b)

    layer_weights = _LayerWeights([{n: big[n][l] for n in W_NAMES} for l in range(DEPTH)], c, 2 * chip + c)
    smalls = [{n: small[n][l][None, :] for n in SMALL_NAMES} for l in range(DEPTH)]
    reductions = [[] for _ in range(DEPTH)]

    def reduce_layer(l, dws):
        reductions[l].append(_Reduce(f"l{l}_{len(reductions[l])}", dws, c, chip))
        return reductions[l][-1]

    loss_part, grad_x, small_grads, last = _fwd_bwd(x[0], loss_target[0], layer_weights, smalls, reduce_layer)
    loss = lax.psum(loss_part[0, 0], ("x", "y", "c"))
    grad_x = grad_x[None]

    g_small_packed = _all_reduce_small(_pack_small(small_grads))
    pack_in = lambda d: _pack_small([{n: d[n][l] for n in SMALL_NAMES} for l in range(DEPTH)])
    sd, sm_, sv_ = _adamw("adamw_small", pack_in(small), g_small_packed, pack_in(small_m), pack_in(small_v), ts=SMALL_ROWS)
    g_small, d_small, m_small, v_small = (_unpack_small(t) for t in (g_small_packed, sd, sm_, sv_))

    last.run_alone()
    g_big, d_big, m_big, v_big = {}, {}, {}, {}
    for n in W_NAMES:
        g = jnp.stack([next(r.shards[n] for r in reductions[l] if n in r.names) for l in range(DEPTH)])
        shape = g.shape
        flat = lambda t: t.reshape(shape[0] * shape[1], shape[2])
        d_, m_, v_ = _adamw(f"adamw_{n}", flat(big[n]), flat(g), flat(big_m[n]), flat(big_v[n]))
        g_big[n], d_big[n], m_big[n], v_big[n] = g, d_.reshape(shape), m_.reshape(shape), v_.reshape(shape)

    order = ("w_in", "q_a_norm", "kv_a_norm", "w_uq", "w_ukv", "a_out_norm", "b_out_norm", "w_o", "ln1_g", "ln1_b",
             "w_ff1", "w_ff2", "ln2_g", "ln2_b")
    pick = lambda bigd, smalld: [bigd[n] if n in bigd else smalld[n] for n in order]
    return (loss, grad_x, *pick(g_big, g_small), *pick(d_big, d_small), *pick(m_big, m_small), *pick(v_big, v_small))
```

```python
import math

import jax
import jax.numpy as jnp
from jax import lax
from jax.experimental import pallas as pl
from jax.experimental.pallas import tpu as pltpu

F32, BF16 = jnp.float32, jnp.bfloat16

D_MODEL = 1024
DEPTH = 4
A_HEAD_DIM = 64
A_HEADS = 8
A_WIDTH = 512
PATTERNS = ((128, 1), (512, 4), (2048, 16))
SPAN = 128
MLA_HEADS = 4
QK_NOPE = 128
QK_ROPE = 64
V_DIM = 128
Q_LORA = 256
KV_LORA = 128
MLA_WIDTH = 512
QK_PAD = 256
IN_COLS = 1984
IN_PAD = 2048
UQ_PAD = 1024
D_FF = 4096
ROPE_THETA = 10000.0
ALPHA = (2.0 * DEPTH) ** 0.25
LN_EPS = 1e-5
RMS_EPS = 1e-6
MLA_SCALE = (QK_NOPE + QK_ROPE) ** -0.5
A_SCALE = A_HEAD_DIM ** -0.5

ADAM_LR, ADAM_B1, ADAM_B2, ADAM_EPS, ADAM_WD, ADAM_STEP = 0.001, 0.9, 0.999, 1e-08, 0.01, 10

VMEM_LIMIT_BYTES = 56 * 1024 * 1024
NEG = -1e30
N_CHIPS, N_DEV = 4, 8
LANES = 128

SMALL_NAMES = ("q_a_norm", "kv_a_norm", "a_out_norm", "b_out_norm", "ln1_g", "ln1_b", "ln2_g", "ln2_b")
SMALL_SIZES = (256, 128, 512, 512, 1024, 1024, 1024, 1024)
SMALL_ROWS = 176


def _params(sem):
    return pltpu.CompilerParams(dimension_semantics=sem, vmem_limit_bytes=VMEM_LIMIT_BYTES)


FAST_STRIDE = 4


def _interleave(parts, scratch):
    d, (n, width) = len(parts), parts[0].shape
    if d > FAST_STRIDE and d % FAST_STRIDE == 0:
        inner = [_interleave(parts[r::FAST_STRIDE], scratch) for r in range(FAST_STRIDE)]
        return _interleave(inner, scratch)
    for r in range(d):
        for s in range(width // LANES):
            scratch.at[s][pl.ds(r, n, stride=d), :] = parts[r][:, s * LANES:(s + 1) * LANES]
    return jnp.concatenate([scratch.at[s][pl.ds(0, n * d), :] for s in range(width // LANES)], axis=1)


def _deinterleave(x, d, scratch):
    n, width = x.shape[0] // d, x.shape[1]
    if d > FAST_STRIDE and d % FAST_STRIDE == 0:
        outer = _deinterleave(x, FAST_STRIDE, scratch)
        inner = [_deinterleave(y, d // FAST_STRIDE, scratch) for y in outer]
        return [inner[r % FAST_STRIDE][r // FAST_STRIDE] for r in range(d)]
    for s in range(width // LANES):
        scratch.at[s][pl.ds(0, n * d), :] = x[:, s * LANES:(s + 1) * LANES]
    return [jnp.concatenate([scratch.at[s][pl.ds(r, n, stride=d), :] for s in range(width // LANES)], axis=1) for r in range(d)]


def _rowwise(name, fn, ins, outs, consts=(), reds=(), ts=512, in_dils=None, out_dils=None, comm=None):
    in_dils = list(in_dils or [1] * len(ins))
    out_dils = list(out_dils or [1] * len(outs))
    n_rows = ins[0].shape[-2] * in_dils[0]
    ts = min(ts, n_rows)
    assert n_rows % ts == 0
    n_in, n_c, n_o, n_r = len(ins), len(consts), len(outs), len(reds)
    viewed = [(a.shape, d) for a, d in zip(list(ins) + list(outs), in_dils + out_dils) if d > 1]
    grid = (n_rows // ts,)
    h_ins, h_in_specs, h_outs, h_out_specs, begin, end = _hosted(comm, grid)
    n_hi, n_ho = len(h_ins), len(h_outs)

    def tile_spec(shape, d=1):
        if len(shape) == 2:
            return pl.BlockSpec((ts // d, shape[1]), lambda i: (i, 0))
        return pl.BlockSpec((shape[0], ts, shape[2]), lambda i: (0, i, 0))

    def whole_spec(shape):
        return pl.BlockSpec(shape, lambda i: (0,) * len(shape))

    def body(*refs):
        in_refs, c_refs, hi_refs = refs[:n_in], refs[n_in:n_in + n_c], refs[n_in + n_c:n_in + n_c + n_hi]
        refs = refs[n_in + n_c + n_hi:]
        o_refs, r_refs, ho_refs = refs[:n_o], refs[n_o:n_o + n_r], refs[n_o + n_r:n_o + n_r + n_ho]
        scratches = list(refs[n_o + n_r + n_ho:])
        sems = [scratches.pop(), scratches.pop()][::-1] if comm else []
        begin(hi_refs, ho_refs, sems)
        vals = []
        for r, d in zip(in_refs, in_dils):
            if d == 1:
                vals.append(r[...])
            else:
                width = r.shape[1] // d
                vals.append(_interleave([r[:, k * width:(k + 1) * width] for k in range(d)], scratches.pop(0)))
        res = fn(*vals, *[r[...] for r in c_refs])
        res = tuple(res) if isinstance(res, (tuple, list)) else (res,)
        for r, v, d in zip(o_refs, res[:n_o], out_dils):
            if len(r.shape) == 3:
                for g in range(r.shape[0]):
                    r[g] = v[g].astype(r.dtype)
            elif d == 1:
                r[...] = v.astype(r.dtype)
            else:
                width = r.shape[1] // d
                for k, part in enumerate(_deinterleave(v, d, scratches.pop(0))):
                    r[:, k * width:(k + 1) * width] = part.astype(r.dtype)
        if n_r:
            i = pl.program_id(0)

            @pl.when(i == 0)
            def _():
                for r, v in zip(r_refs, res[n_o:]):
                    r[...] = v

            @pl.when(i > 0)
            def _():
                for r, v in zip(r_refs, res[n_o:]):
                    r[...] += v
        end(hi_refs, ho_refs, sems)

    out = pl.pallas_call(
        body, name=name, grid=grid,
        in_specs=[tile_spec(a.shape, d) for a, d in zip(ins, in_dils)] + [whole_spec(c.shape) for c in consts] + h_in_specs,
        out_specs=[tile_spec(o.shape, d) for o, d in zip(outs, out_dils)] + [whole_spec(r.shape) for r in reds] + h_out_specs,
        out_shape=list(outs) + list(reds) + h_outs,
        scratch_shapes=[pltpu.VMEM((shape[1] // d // LANES, ts, LANES), F32) for shape, d in viewed]
        + (comm.sem_scratch() if comm else []),
        compiler_params=_params(("arbitrary",)),
    )(*ins, *consts, *h_ins)
    return out


def _sds(shape, dtype=F32):
    return jax.ShapeDtypeStruct(tuple(shape), dtype)


def _hosted(comm, grid):
    if comm is None:
        return [], [], [], [], (lambda *a: None), (lambda *a: None)

    def edge(which, at, ins, outs, sems):
        ids = [pl.program_id(d) for d in range(len(grid))]
        hit = ids[0] == at[0]
        for i, v in zip(ids[1:], at[1:]):
            hit = hit & (i == v)

        @pl.when(hit)
        def _():
            which(ins, outs, *sems)

    begin = lambda ins, outs, sems: edge(comm.start, [0] * len(grid), ins, outs, sems)
    end = lambda ins, outs, sems: edge(comm.finish, [g - 1 for g in grid], ins, outs, sems)
    return comm.ins, [ANY] * len(comm.ins), comm.out_shapes, [ANY] * len(comm.out_shapes), begin, end


def _mm(name, a, b, out_dtypes, *, ta=False, tb=False, tm=512, tn=512, tk=512, epilogue=None, extras=(), vecs=(),
        comm=None, b_pieces=False, out_pieces=False):
    (k_dim, m_dim) = a.shape if ta else a.shape[::-1]
    if b_pieces:
        b_rows, b_cols = 2 * b.shape[2], N_CHIPS * b.shape[3]
        (n_dim, k2) = (b_rows, b_cols) if tb else (b_cols, b_rows)
    else:
        (n_dim, k2) = b.shape if tb else b.shape[::-1]
    assert k_dim == k2
    tm, tn, tk = min(tm, m_dim), min(tn, n_dim), min(tk, k_dim)
    assert m_dim % tm == 0 and n_dim % tn == 0 and k_dim % tk == 0
    nk = k_dim // tk
    grid = (m_dim // tm, n_dim // tn, nk)
    a_spec = pl.BlockSpec((tk, tm), lambda i, j, k: (k, i)) if ta else pl.BlockSpec((tm, tk), lambda i, j, k: (i, k))
    if b_pieces:
        assert tn == b_rows and tk % b.shape[3] == 0 if tb else (tk, tn) == (b_rows, b.shape[3])
        n_bp = tk // b.shape[3] if tb else 1
        b_spec = pl.BlockSpec((n_bp,) + b.shape[1:], (lambda i, j, k: (k, 0, 0, 0)) if tb else (lambda i, j, k: (j, 0, 0, 0)))
    else:
        b_spec = pl.BlockSpec((tn, tk), lambda i, j, k: (j, k)) if tb else pl.BlockSpec((tk, tn), lambda i, j, k: (k, j))
    assert not out_pieces or (tm == m_dim and len(out_dtypes) == 1)
    dims = (((0 if ta else 1,), (1 if tb else 0,)), ((), ()))
    n_e, n_v, n_o = len(extras), len(vecs), len(out_dtypes)
    c_ins, c_in_specs, c_outs, c_out_specs, begin, end = _hosted(comm, grid)
    n_ci, n_co = len(c_ins), len(c_outs)

    def body(a_ref, b_ref, *refs):
        e_refs, v_refs, ci_refs = refs[:n_e], refs[n_e:n_e + n_v], refs[n_e + n_v:n_e + n_v + n_ci]
        o_refs = refs[n_e + n_v + n_ci:n_e + n_v + n_ci + n_o]
        co_refs = refs[n_e + n_v + n_ci + n_o:n_e + n_v + n_ci + n_o + n_co]
        scratch = refs[n_e + n_v + n_ci + n_o + n_co:]
        sems = scratch[-2:]
        begin(ci_refs, co_refs, sems)
        if b_pieces:
            pc = b.shape[3]
            part = None
            for cc in range(n_bp):
                a_val = a_ref[:, cc * pc:(cc + 1) * pc] if tb else a_ref[...]
                term = lax.dot_general(a_val.astype(BF16), b_ref[cc].reshape(b_rows, pc).astype(BF16), dims,
                                       preferred_element_type=F32)
                part = term if part is None else part + term
        else:
            part = lax.dot_general(a_ref[...].astype(BF16), b_ref[...].astype(BF16), dims, preferred_element_type=F32)

        def finish(acc):
            outs = epilogue(acc, *[r[...] for r in e_refs], *[r[...] for r in v_refs]) if epilogue else (acc,)
            for r, v in zip(o_refs, outs):
                r[...] = (v.reshape(2, tm // 2, tn) if out_pieces else v).astype(r.dtype)

        if nk == 1:
            finish(part)
        else:
            acc_ref = scratch[0]
            k = pl.program_id(2)

            @pl.when(k == 0)
            def _():
                acc_ref[...] = part

            @pl.when(k > 0)
            def _():
                acc_ref[...] += part

            @pl.when(k == nk - 1)
            def _():
                finish(acc_ref[...])
        end(ci_refs, co_refs, sems)

    tile = pl.BlockSpec((tm, tn), lambda i, j, k: (i, j))
    if out_pieces:
        out_tile, out_shape = pl.BlockSpec((None, 2, tm // 2, tn), lambda i, j, k: (j, 0, 0, 0)), (n_dim // tn, 2, tm // 2, tn)
    else:
        out_tile, out_shape = tile, (m_dim, n_dim)
    out = pl.pallas_call(
        body, name=name, grid=grid,
        in_specs=[a_spec, b_spec] + [tile] * n_e + [pl.BlockSpec((1, tn), lambda i, j, k: (0, j))] * n_v + c_in_specs,
        out_specs=[out_tile] * n_o + c_out_specs,
        out_shape=[_sds(out_shape, d) for d in out_dtypes] + c_outs,
        scratch_shapes=([pltpu.VMEM((tm, tn), F32)] if nk > 1 else []) + (comm.sem_scratch() if comm else []),
        compiler_params=_params(("arbitrary",) * 3 if comm else ("parallel", "parallel", "arbitrary")),
    )(a, b, *extras, *vecs, *c_ins)
    return out


def _swap_halves(x):
    width = x.shape[1]
    lane = lax.broadcasted_iota(jnp.int32, x.shape, 1)
    return jnp.where((lane % 64) < 32, pltpu.roll(x, width - 32, 1), pltpu.roll(x, 32, 1))


def _rope(x, cos, sin_signed):
    return x * cos + _swap_halves(x) * sin_signed


def _rope_t(d, cos, sin_signed):
    return d * cos - _swap_halves(d) * sin_signed


def _rms(x, g):
    rstd = lax.rsqrt(jnp.mean(x * x, axis=-1, keepdims=True) + RMS_EPS)
    return x * rstd * g


def _rms_bwd(x, g, dy):
    rstd = lax.rsqrt(jnp.mean(x * x, axis=-1, keepdims=True) + RMS_EPS)
    xh = x * rstd
    dyg = dy * g
    dx = rstd * (dyg - xh * jnp.mean(dyg * xh, axis=-1, keepdims=True))
    return dx, jnp.sum(dy * xh, axis=0, keepdims=True)


def _layer_norm(r, g, b):
    mu = jnp.mean(r, axis=-1, keepdims=True)
    xc = r - mu
    var = jnp.mean(xc * xc, axis=-1, keepdims=True)
    return xc * lax.rsqrt(var + LN_EPS) * g + b


def _dot(a, b, dims):
    return lax.dot_general(a, b, (dims, ((), ())), preferred_element_type=F32)


_NN, _NT, _TN = ((1,), (0,)), ((1,), (1,)), ((0,), (0,))


def _in_proj(x_bf, w_in_p, cos, sin):
    n_rows = x_bf.shape[0]
    ts = 512
    dils = [dil for _, dil in PATTERNS]
    n_p = len(dils)

    def body(x_ref, w_ref, cos_ref, sin_ref, *refs):
        view_refs, (cq_ref, kpe_ref, ckv_ref, scratch) = refs[:n_p], refs[n_p:]
        xb = x_ref[...]
        cos_v, sin_v = cos_ref[...], sin_ref[...]

        def seg(lo, hi):
            return _dot(xb, w_ref[:, lo:hi], _NN)

        def put(part, val):
            for ref, dil in zip(view_refs, dils):
                pieces = [val] if dil == 1 else _deinterleave(val, dil, scratch)
                for r, piece in enumerate(pieces):
                    lo = (3 * r + part) * A_WIDTH
                    ref[:, lo:lo + A_WIDTH] = piece.astype(BF16)

        put(0, _rope(seg(0, 512), cos_v, sin_v) * A_SCALE)
        put(1, _rope(seg(512, 1024), cos_v, sin_v))
        put(2, seg(1024, 1536))
        cq_ref[...] = seg(1536, 1792)
        kpe_ref[...] = _rope(seg(1792, 1920), cos_v[:, :LANES], sin_v[:, :LANES]).astype(BF16)
        ckv_ref[...] = seg(1920, 2048)

    row = lambda c, d=1: pl.BlockSpec((ts // d, c * d), lambda i: (i, 0))
    return pl.pallas_call(
        body, name="in_proj", grid=(n_rows // ts,),
        in_specs=[row(D_MODEL), pl.BlockSpec((D_MODEL, IN_PAD), lambda i: (0, 0)), row(A_WIDTH), row(A_WIDTH)],
        out_specs=[row(3 * A_WIDTH, d) for d in dils] + [row(Q_LORA), row(LANES), row(KV_LORA)],
        out_shape=[_sds((n_rows // d, d * 3 * A_WIDTH), BF16) for d in dils]
        + [_sds((n_rows, Q_LORA)), _sds((n_rows, LANES), BF16), _sds((n_rows, KV_LORA))],
        scratch_shapes=[pltpu.VMEM((A_WIDTH // LANES, ts, LANES), F32)],
        compiler_params=_params(("parallel",)),
    )(x_bf, w_in_p, cos, sin)


def _band_mask(m, heads):
    qi = lax.broadcasted_iota(jnp.int32, (heads * SPAN, 2 * SPAN), 0) % SPAN
    kj = lax.broadcasted_iota(jnp.int32, (heads * SPAN, 2 * SPAN), 1)
    return ((kj < SPAN) & (kj >= qi) & (m > 0)) | ((kj >= SPAN) & ((kj - SPAN) <= qi))


def _even_lanes(rows):
    return lax.broadcasted_iota(jnp.int32, (rows, LANES), 1) < A_HEAD_DIM


DIL_GROUP = 4
HEAD_SLOT = LANES // A_HEADS
O_WORDS = A_WIDTH // 2
OL_WIDTH = O_WORDS + LANES


def _pack_halves(t):
    return _pack2(t[:, :O_WORDS], t[:, O_WORDS:])


def _unpack_halves(p):
    return jnp.concatenate(_unpack2(p), axis=1)


DQKV_WIDTH = 3 * A_WIDTH // 2


def _pack2(a, b):
    hi = lax.bitcast_convert_type(a.astype(jnp.bfloat16).astype(F32), jnp.uint32) & jnp.uint32(0xFFFF0000)
    lo = lax.shift_right_logical(lax.bitcast_convert_type(b.astype(jnp.bfloat16).astype(F32), jnp.uint32), jnp.uint32(16))
    return lax.bitcast_convert_type(hi | lo, F32)


def _unpack2(p):
    u = lax.bitcast_convert_type(p, jnp.uint32)
    return (lax.bitcast_convert_type(u & jnp.uint32(0xFFFF0000), F32),
            lax.bitcast_convert_type(lax.shift_left(u, jnp.uint32(16)), F32))


def _slot_lane(h):
    return HEAD_SLOT * (h // 2) + A_HEAD_DIM * (h % 2)


def _slot_head(lane):
    return 2 * (jnp.right_shift(lane, 4) & 3) + jnp.right_shift(lane, 6)


def _dil_fwd(qkv_view, dil, comm=None):
    nb = qkv_view.shape[0] // SPAN
    group = min(dil, DIL_GROUP)
    grid = (dil // group, nb)
    c_ins, c_in_specs, c_outs, c_out_specs, begin, end = _hosted(comm, grid)
    n_ci, n_co = len(c_ins), len(c_outs)

    def body(cur_ref, prev_ref, *refs):
        ci_refs, ol_ref, co_refs, sems = refs[:n_ci], refs[n_ci], refs[n_ci + 1:n_ci + 1 + n_co], refs[n_ci + 1 + n_co:]
        begin(ci_refs, co_refs, sems)
        mask = _band_mask(pl.program_id(1), 1)
        even, even2 = _even_lanes(SPAN), _even_lanes(2 * SPAN)
        pair_of_lane = jnp.right_shift(lax.broadcasted_iota(jnp.int32, (SPAN, LANES), 1), 4) & 3
        for g in range(group):
            lse_c, o_pairs = jnp.zeros((SPAN, LANES), F32), []
            for p in range(A_HEADS // 2):
                q_sl, k_sl, v_sl = (slice((3 * g + t) * A_WIDTH + p * LANES, (3 * g + t) * A_WIDTH + (p + 1) * LANES)
                                    for t in range(3))
                q2 = cur_ref[:, q_sl]
                kcat = jnp.concatenate([prev_ref[:, k_sl], cur_ref[:, k_sl]], axis=0)
                vcat = jnp.concatenate([prev_ref[:, v_sl], cur_ref[:, v_sl]], axis=0)
                zero, one = jnp.zeros_like(q2), jnp.ones_like(vcat)
                res, lses = [], []
                for first in (True, False):
                    qh = jnp.where(even, q2, zero) if first else jnp.where(even, zero, q2)
                    vh = jnp.where(even2, vcat, one) if first else jnp.where(even2, one, vcat)
                    s = jnp.where(mask, _dot(qh, kcat, _NT), NEG)
                    mx = jnp.max(s, axis=-1, keepdims=True)
                    r = _dot(jnp.exp(s - mx).astype(BF16), vh, _NN)
                    den = pltpu.roll(r, A_HEAD_DIM, 1)
                    res.append(r / den)
                    lses.append(mx + jnp.log(den))
                o_pairs.append(jnp.where(even, res[0], res[1]))
                lse_c = jnp.where(pair_of_lane == p, jnp.where(even, lses[0], lses[1]), lse_c)
            for p in range(2):
                ol_ref[:, OL_WIDTH * g + p * LANES:OL_WIDTH * g + (p + 1) * LANES] = _pack2(o_pairs[p], o_pairs[p + 2])
            ol_ref[:, OL_WIDTH * g + O_WORDS:OL_WIDTH * (g + 1)] = lse_c
        end(ci_refs, co_refs, sems)

    return pl.pallas_call(
        body, name=f"dil_fwd_d{dil}", grid=grid,
        in_specs=[pl.BlockSpec((SPAN, group * 3 * A_WIDTH), lambda r, m: (m, r)),
                  pl.BlockSpec((SPAN, group * 3 * A_WIDTH), lambda r, m: (jnp.maximum(m - 1, 0), r))] + c_in_specs,
        out_specs=[pl.BlockSpec((SPAN, group * OL_WIDTH), lambda r, m: (m, r))] + c_out_specs,
        out_shape=[_sds((qkv_view.shape[0], dil * OL_WIDTH))] + c_outs,
        scratch_shapes=comm.sem_scratch() if comm else [],
        compiler_params=_params(("arbitrary", "arbitrary") if comm else ("parallel", "arbitrary")),
    )(qkv_view, qkv_view, *c_ins)


def _slot_spread_matrix():
    i = lax.broadcasted_iota(jnp.int32, (LANES, A_WIDTH), 0)
    h = lax.broadcasted_iota(jnp.int32, (LANES, A_WIDTH), 1) // A_HEAD_DIM
    return (i == HEAD_SLOT * (h // 2) + A_HEAD_DIM * (h % 2)).astype(BF16)


def _spread_slots(c, spread):
    hi = c.astype(BF16)
    r1 = c - hi.astype(F32)
    mid = r1.astype(BF16)
    lo = (r1 - mid.astype(F32)).astype(BF16)
    return _dot(hi, spread, _NN) + _dot(mid, spread, _NN) + _dot(lo, spread, _NN)


def _pattern_weights(ol1, ol2, ol3):
    l1, l2, l3 = ol1[:, O_WORDS:], ol2[:, O_WORDS:], ol3[:, O_WORDS:]
    mx = jnp.maximum(jnp.maximum(l1, l2), l3)
    e1, e2, e3 = jnp.exp(l1 - mx), jnp.exp(l2 - mx), jnp.exp(l3 - mx)
    inv = 1.0 / (e1 + e2 + e3)
    return e1 * inv, e2 * inv, e3 * inv


def _mix_fwd(ols, b_out, a_g, b_g):
    def fn(ol1, ol2, ol3, b, ag, bg, spread):
        ws = [_spread_slots(w, spread) for w in _pattern_weights(ol1, ol2, ol3)]
        a = sum(w * _unpack_halves(ol[:, :O_WORDS]) for w, ol in zip(ws, (ol1, ol2, ol3)))
        return a, jnp.concatenate([_rms(a, ag), _rms(b, bg)], axis=1)

    n_rows = b_out.shape[0]
    return _rowwise("mix_fwd", fn, [*ols, b_out], [_sds((n_rows, A_WIDTH)), _sds((n_rows, 2 * A_WIDTH), BF16)],
                    consts=[a_g, b_g, _slot_spread_matrix()], in_dils=[dil for _, dil in PATTERNS] + [1])


def _mla_q_prep(cq, g, w_uq_p, cos, sin):
    def fn(cq_v, cos_v, sin_v, g_v, w_v):
        cqn = _rms(cq_v, g_v).astype(BF16)
        q = _dot(cqn, w_v, _NN)
        qf = [jnp.concatenate([q[:, h * QK_NOPE:(h + 1) * QK_NOPE],
                               _rope(q[:, 512 + h * LANES:512 + (h + 1) * LANES], cos_v[:, :LANES], sin_v[:, :LANES])], axis=1)
              for h in range(MLA_HEADS)]
        return cqn, qf

    n_rows = cq.shape[0]
    return _rowwise("mla_q_prep", fn, [cq, cos, sin], [_sds((n_rows, Q_LORA), BF16), _sds((MLA_HEADS, n_rows, QK_PAD), BF16)],
                    consts=[g, w_uq_p])


def _mla_kv_prep(ckv, kpe, g, w_ukv_p):
    def fn(ckv_v, kpe_v, g_v, w_v):
        ckvn = _rms(ckv_v, g_v).astype(BF16)
        kvv = _dot(ckvn, w_v, _NN)
        kf = [jnp.concatenate([kvv[:, h * QK_NOPE:(h + 1) * QK_NOPE], kpe_v.astype(F32)], axis=1) for h in range(MLA_HEADS)]
        return ckvn, kf, kvv[:, 512:]

    n_rows = ckv.shape[0]
    return _rowwise("mla_kv_prep", fn, [ckv, kpe],
                    [_sds((n_rows, KV_LORA), BF16), _sds((MLA_HEADS, n_rows, QK_PAD), BF16), _sds((n_rows, MLA_WIDTH), BF16)],
                    consts=[g, w_ukv_p])


MLA_FWD_TILES = (1024, 2048)
MLA_BWD_TILES = (1024, 1024)


def _mla_fwd(qf, kf, v, comm=None):
    n_rows = v.shape[0]
    tq, tk = min(MLA_FWD_TILES[0], n_rows), min(MLA_FWD_TILES[1], n_rows)
    nq, nk = n_rows // tq, n_rows // tk
    grid = (MLA_HEADS, nq, nk)
    exp2_scale = MLA_SCALE * math.log2(math.e)
    c_ins, c_in_specs, c_outs, c_out_specs, begin, end = _hosted(comm, grid)
    n_ci, n_co = len(c_ins), len(c_outs)

    def last_k(i):
        return jnp.right_shift(i * tq + tq - 1, int(math.log2(tk)))

    def body(q_ref, k_ref, v_ref, *refs):
        ci_refs, (o_ref, lse_ref), co_refs = refs[:n_ci], refs[n_ci:n_ci + 2], refs[n_ci + 2:n_ci + 2 + n_co]
        m_sc, l_sc, acc_sc = refs[n_ci + 2 + n_co:n_ci + 5 + n_co]
        sems = refs[n_ci + 5 + n_co:]
        i, j = pl.program_id(1), pl.program_id(2)
        begin(ci_refs, co_refs, sems)

        @pl.when(j == 0)
        def _():
            m_sc[...] = jnp.full(m_sc.shape, NEG, F32)
            l_sc[...] = jnp.zeros(l_sc.shape, F32)
            acc_sc[...] = jnp.zeros(acc_sc.shape, F32)

        def step(masked):
            s = _dot(q_ref[...], k_ref[...], _NT)
            if masked:
                row = lax.broadcasted_iota(jnp.int32, (tq, tk), 0) + i * tq
                col = lax.broadcasted_iota(jnp.int32, (tq, tk), 1) + j * tk
                s = jnp.where(col <= row, s, NEG)
            m_prev = m_sc[...]
            m_new = jnp.maximum(m_prev, jnp.max(s, axis=-1, keepdims=True))
            alpha = jnp.exp2((m_prev - m_new) * exp2_scale)
            p = jnp.exp2((s - m_new) * exp2_scale)
            l_sc[...] = alpha * l_sc[...] + jnp.sum(p, axis=-1, keepdims=True)
            acc_sc[...] = alpha * acc_sc[...] + _dot(p.astype(BF16), v_ref[...], _NN)
            m_sc[...] = m_new

        active = j * tk <= i * tq + tq - 1
        crosses = (j + 1) * tk - 1 > i * tq

        @pl.when(active & jnp.logical_not(crosses))
        def _():
            step(False)

        @pl.when(active & crosses)
        def _():
            step(True)

        @pl.when(j == last_k(i))
        def _():
            o_ref[...] = acc_sc[...] / l_sc[...]
            lse_ref[...] = jnp.broadcast_to(m_sc[...] * MLA_SCALE + jnp.log(l_sc[...]), (tq, V_DIM))

        end(ci_refs, co_refs, sems)

    return pl.pallas_call(
        body, name="mla_fwd", grid=grid,
        in_specs=[pl.BlockSpec((None, tq, QK_PAD), lambda h, i, j: (h, i, 0)),
                  pl.BlockSpec((None, tk, QK_PAD), lambda h, i, j: (h, jnp.minimum(j, last_k(i)), 0)),
                  pl.BlockSpec((tk, V_DIM), lambda h, i, j: (jnp.minimum(j, last_k(i)), h))] + c_in_specs,
        out_specs=[pl.BlockSpec((tq, V_DIM), lambda h, i, j: (i, h)), pl.BlockSpec((tq, V_DIM), lambda h, i, j: (i, h))]
        + c_out_specs,
        out_shape=[_sds((n_rows, MLA_WIDTH)), _sds((n_rows, MLA_WIDTH))] + c_outs,
        scratch_shapes=[pltpu.VMEM((tq, 1), F32), pltpu.VMEM((tq, 1), F32), pltpu.VMEM((tq, V_DIM), F32)]
        + (comm.sem_scratch() if comm else []),
        compiler_params=_params(("arbitrary",) * 3 if comm else ("parallel", "parallel", "arbitrary")),
    )(qf, kf, v, *c_ins)


def _mm_res_ln(name, a, w, xres, g, b):
    def epi(acc, xr, g_v, b_v):
        r = ALPHA * xr + acc
        y = _layer_norm(r, g_v, b_v)
        return y, y, r

    return _mm(name, a, w, (F32, BF16, F32), tm=1024, tn=D_MODEL, tk=1024, epilogue=epi, extras=[xres], vecs=[g, b])


def _mm_relu2(x_bf, w, comm=None):
    def epi(acc):
        r = jnp.maximum(acc, 0.0)
        return (r * r,)

    return _mm("ff1", x_bf, w, (BF16,), tm=1024, tn=1024, tk=D_MODEL, epilogue=epi, b_pieces=True, comm=comm)


def _loss_fn(y, t):
    def fn(y_v, t_v):
        d = y_v - t_v
        part = jnp.sum(jnp.sum(d * d, axis=1, keepdims=True), axis=0, keepdims=True)
        return d * (1.0 / D_MODEL), part

    dy, part = _rowwise("loss", fn, [y, t], [_sds(y.shape)], reds=[_sds((1, 1))])
    return part * (0.5 / D_MODEL), dy


def _ln_bwd(name, dy, r, g):
    def fn(dy_v, r_v, g_v):
        mu = jnp.mean(r_v, axis=-1, keepdims=True)
        xc = r_v - mu
        rstd = lax.rsqrt(jnp.mean(xc * xc, axis=-1, keepdims=True) + LN_EPS)
        xh = xc * rstd
        dxh = dy_v * g_v
        dr = rstd * (dxh - jnp.mean(dxh, axis=-1, keepdims=True) - xh * jnp.mean(dxh * xh, axis=-1, keepdims=True))
        return dr, dr, jnp.sum(dy_v * xh, axis=0, keepdims=True), jnp.sum(dy_v, axis=0, keepdims=True)

    return _rowwise(name, fn, [dy, r], [_sds(dy.shape), _sds(dy.shape, BF16)], consts=[g],
                    reds=[_sds((1, D_MODEL)), _sds((1, D_MODEL))])


def _head_sum_matrix():
    i = lax.broadcasted_iota(jnp.int32, (A_WIDTH, LANES), 0) // A_HEAD_DIM
    return (i == _slot_head(lax.broadcasted_iota(jnp.int32, (A_WIDTH, LANES), 1))).astype(BF16)


def _mix_bwd(dmixed, a_out, b_out, ols, a_g, b_g):
    def fn(dm, a, b, ol1, ol2, ol3, ag, bg, sum_mat, spread):
        da, dga = _rms_bwd(a, ag, dm[:, :A_WIDTH])
        db, dgb = _rms_bwd(b, bg, dm[:, A_WIDTH:])
        t = da * a
        t_hi = t.astype(BF16)
        t_lo = (t - t_hi.astype(F32)).astype(BF16)
        tsum = _dot(t_hi, sum_mat, _NN) + _dot(t_lo, sum_mat, _NN)
        tb = db * b
        delta_b = jnp.concatenate(
            [jnp.broadcast_to(jnp.sum(tb[:, h * V_DIM:(h + 1) * V_DIM], axis=-1, keepdims=True), (tb.shape[0], V_DIM))
             for h in range(MLA_HEADS)], axis=1)
        dods = [jnp.concatenate([_pack_halves(_spread_slots(w, spread) * da), w * tsum], axis=1)
                for w in _pattern_weights(ol1, ol2, ol3)]
        return (*dods, db, delta_b, dga, dgb)

    n_rows = a_out.shape[0]
    wide = (n_rows, A_WIDTH)
    dils = [dil for _, dil in PATTERNS]
    return _rowwise("mix_bwd", fn, [dmixed, a_out, b_out, *ols],
                    [_sds((n_rows // d, d * OL_WIDTH)) for d in dils] + [_sds(wide, BF16), _sds(wide)],
                    consts=[a_g, b_g, _head_sum_matrix(), _slot_spread_matrix()],
                    reds=[_sds((1, A_WIDTH)), _sds((1, MLA_WIDTH))], ts=256,
                    in_dils=[1, 1, 1] + dils, out_dils=dils + [1, 1])


def _dil_bwd(qkv_view, ol_view, dod_view, dil, comm=None):
    nb = qkv_view.shape[0] // SPAN
    group = min(dil, DIL_GROUP)
    grid = (dil // group, nb + 1)
    c_ins, c_in_specs, c_outs, c_out_specs, begin, end = _hosted(comm, grid)
    n_ci, n_co = len(c_ins), len(c_outs)

    def body(cur_ref, prev_ref, ol_ref, dod_ref, *refs):
        ci_refs, out_ref, co_refs = refs[:n_ci], refs[n_ci], refs[n_ci + 1:n_ci + 1 + n_co]
        carry, sems = refs[n_ci + 1 + n_co], refs[n_ci + 2 + n_co:]
        m = pl.program_id(1)
        begin(ci_refs, co_refs, sems)

        def write_packed(g, pairs):
            lo = DQKV_WIDTH * g
            for p, (dq_p, dk_p, _) in enumerate(pairs):
                out_ref[:, lo + p * LANES:lo + (p + 1) * LANES] = _pack2(dq_p, dk_p)
            for p in range(2):
                out_ref[:, lo + A_WIDTH + p * LANES:lo + A_WIDTH + (p + 1) * LANES] = _pack2(pairs[p][2], pairs[p + 2][2])

        @pl.when(m == 0)
        def _():
            carry[...] = jnp.zeros(carry.shape, F32)

        @pl.when(m < nb)
        def _():
            mask, even = _band_mask(m, 2), _even_lanes(SPAN)
            for g in range(group):
                done = []
                for p in range(A_HEADS // 2):
                    q_sl, k_sl, v_sl = (slice((3 * g + t) * A_WIDTH + p * LANES, (3 * g + t) * A_WIDTH + (p + 1) * LANES)
                                        for t in range(3))
                    do_sl = slice(OL_WIDTH * g + (p % 2) * LANES, OL_WIDTH * g + (p % 2 + 1) * LANES)
                    s_e, s_o = (OL_WIDTH * g + O_WORDS + _slot_lane(2 * p + t) for t in range(2))
                    q2, do2 = cur_ref[:, q_sl], _unpack2(dod_ref[:, do_sl])[p // 2].astype(BF16)
                    zero = jnp.zeros_like(q2)
                    qcat = jnp.concatenate([jnp.where(even, q2, zero), jnp.where(even, zero, q2)], axis=0)
                    docat = jnp.concatenate([jnp.where(even, do2, zero), jnp.where(even, zero, do2)], axis=0)
                    kcat = jnp.concatenate([prev_ref[:, k_sl], cur_ref[:, k_sl]], axis=0)
                    vcat = jnp.concatenate([prev_ref[:, v_sl], cur_ref[:, v_sl]], axis=0)
                    lse_c = jnp.concatenate([ol_ref[:, s_e:s_e + 1], ol_ref[:, s_o:s_o + 1]], axis=0)
                    dl_c = jnp.concatenate([dod_ref[:, s_e:s_e + 1], dod_ref[:, s_o:s_o + 1]], axis=0)
                    pr = jnp.exp(jnp.where(mask, _dot(qcat, kcat, _NT), NEG) - lse_c)
                    ds = (pr * (_dot(docat, vcat, _NT) - dl_c)).astype(BF16)
                    dq = _dot(ds, kcat, _NN)
                    dk2 = _dot(ds, qcat, _TN)
                    dv2 = _dot(pr.astype(BF16), docat, _TN)
                    done.append((carry[:, q_sl], carry[:, k_sl] + dk2[:SPAN], carry[:, v_sl] + dv2[:SPAN]))
                    carry[:, q_sl] = jnp.where(even, dq[:SPAN], dq[SPAN:])
                    carry[:, k_sl] = dk2[SPAN:]
                    carry[:, v_sl] = dv2[SPAN:]
                write_packed(g, done)

        @pl.when(m == nb)
        def _():
            for g in range(group):
                write_packed(g, [tuple(carry[:, (3 * g + t) * A_WIDTH + p * LANES:(3 * g + t) * A_WIDTH + (p + 1) * LANES]
                                       for t in range(3)) for p in range(A_HEADS // 2)])

        end(ci_refs, co_refs, sems)

    last = nb - 1

    def cur(width):
        return pl.BlockSpec((SPAN, group * width), lambda r, m: (jnp.minimum(m, last), r))

    def prev(width):
        return pl.BlockSpec((SPAN, group * width), lambda r, m: (jnp.clip(m - 1, 0, last), r))

    return pl.pallas_call(
        body, name=f"dil_bwd_d{dil}", grid=grid,
        in_specs=[cur(3 * A_WIDTH), prev(3 * A_WIDTH), cur(OL_WIDTH), cur(OL_WIDTH)] + c_in_specs,
        out_specs=[prev(DQKV_WIDTH)] + c_out_specs,
        out_shape=[_sds((qkv_view.shape[0], dil * DQKV_WIDTH))] + c_outs,
        scratch_shapes=[pltpu.VMEM((SPAN, group * 3 * A_WIDTH), F32)] + (comm.sem_scratch() if comm else []),
        compiler_params=_params(("arbitrary", "arbitrary") if comm else ("parallel", "arbitrary")),
    )(qkv_view, qkv_view, ol_view, dod_view, *c_ins)


def _mla_bwd(qf, kf, v, lse, do, delta, comm=None):
    n_rows = v.shape[0]
    tq, tk = min(MLA_BWD_TILES[0], n_rows), min(MLA_BWD_TILES[1], n_rows)
    nq, nk = n_rows // tq, n_rows // tk

    def first_q(j):
        return jnp.right_shift(j * tk, int(math.log2(tq)))

    grid = (MLA_HEADS, nk, nq)
    c_ins, c_in_specs, c_outs, c_out_specs, begin, end = _hosted(comm, grid)
    n_ci, n_co = len(c_ins), len(c_outs)

    def body(q_ref, k_ref, v_ref, lse_ref, do_ref, dl_ref, *refs):
        ci_refs, (dq_ref, dk_ref, dv_ref), co_refs = refs[:n_ci], refs[n_ci:n_ci + 3], refs[n_ci + 3:n_ci + 3 + n_co]
        dk_sc, dv_sc = refs[n_ci + 3 + n_co:n_ci + 5 + n_co]
        sems = refs[n_ci + 5 + n_co:]
        j, i = pl.program_id(1), pl.program_id(2)
        begin(ci_refs, co_refs, sems)

        @pl.when((j == 0) & (i == 0))
        def _():
            dq_ref[...] = jnp.zeros(dq_ref.shape, F32)

        @pl.when(i == first_q(j))
        def _():
            dk_sc[...] = jnp.zeros(dk_sc.shape, F32)
            dv_sc[...] = jnp.zeros(dv_sc.shape, F32)

        def step(masked):
            q, k, dob = q_ref[...], k_ref[...], do_ref[...]
            s = _dot(q, k, _NT) * MLA_SCALE
            if masked:
                row = lax.broadcasted_iota(jnp.int32, (tq, tk), 0) + i * tq
                col = lax.broadcasted_iota(jnp.int32, (tq, tk), 1) + j * tk
                s = jnp.where(col <= row, s, NEG)
            p = jnp.exp(s - lse_ref[:, :1])
            dp = _dot(dob, v_ref[...], _NT)
            ds = (p * (dp - dl_ref[:, :1]) * MLA_SCALE).astype(BF16)
            dv_sc[...] += _dot(p.astype(BF16), dob, _TN)
            dk_sc[...] += _dot(ds, q, _TN)
            rows = pl.ds(pl.multiple_of(i * tq, tq), tq)
            dq_ref[rows, :] += _dot(ds, k, _NN)

        active = i >= first_q(j)
        crosses = (j + 1) * tk - 1 > i * tq

        @pl.when(active & jnp.logical_not(crosses))
        def _():
            step(False)

        @pl.when(active & crosses)
        def _():
            step(True)

        @pl.when(i == nq - 1)
        def _():
            dk_ref[...] = dk_sc[...]
            dv_ref[...] = dv_sc[...]

        end(ci_refs, co_refs, sems)

    qrow = lambda h, j, i: (jnp.maximum(i, first_q(j)), h)
    return pl.pallas_call(
        body, name="mla_bwd", grid=grid,
        in_specs=[pl.BlockSpec((None, tq, QK_PAD), lambda h, j, i: (h, jnp.maximum(i, first_q(j)), 0)),
                  pl.BlockSpec((None, tk, QK_PAD), lambda h, j, i: (h, j, 0)),
                  pl.BlockSpec((tk, V_DIM), lambda h, j, i: (j, h)),
                  pl.BlockSpec((tq, V_DIM), qrow), pl.BlockSpec((tq, V_DIM), qrow), pl.BlockSpec((tq, V_DIM), qrow)]
        + c_in_specs,
        out_specs=[pl.BlockSpec((None, n_rows, QK_PAD), lambda h, j, i: (h, 0, 0)),
                   pl.BlockSpec((None, tk, QK_PAD), lambda h, j, i: (h, j, 0)),
                   pl.BlockSpec((tk, V_DIM), lambda h, j, i: (j, h))] + c_out_specs,
        out_shape=[_sds((MLA_HEADS, n_rows, QK_PAD)), _sds((MLA_HEADS, n_rows, QK_PAD)), _sds((n_rows, MLA_WIDTH))] + c_outs,
        scratch_shapes=[pltpu.VMEM((tk, QK_PAD), F32), pltpu.VMEM((tk, V_DIM), F32)] + (comm.sem_scratch() if comm else []),
        compiler_params=_params(("arbitrary",) * 3 if comm else ("parallel", "arbitrary", "arbitrary")),
    )(qf, kf, v, lse, do, delta, *c_ins)


def _mla_bwd_prep(dqf, dkf, dv, cos, sin):
    def fn(dqf_v, dkf_v, dv_v, cos_v, sin_v):
        dq = jnp.concatenate([dqf_v[h][:, :QK_NOPE] for h in range(MLA_HEADS)]
                             + [_rope_t(dqf_v[h][:, QK_NOPE:], cos_v[:, :LANES], sin_v[:, :LANES]) for h in range(MLA_HEADS)], axis=1)
        dkv = jnp.concatenate([dkf_v[h][:, :QK_NOPE] for h in range(MLA_HEADS)] + [dv_v], axis=1)
        dkpe = dkf_v[0][:, QK_NOPE:] + dkf_v[1][:, QK_NOPE:] + dkf_v[2][:, QK_NOPE:] + dkf_v[3][:, QK_NOPE:]
        return dq, dkv, dkpe

    n_rows = dv.shape[0]
    return _rowwise("mla_bwd_prep", fn, [dqf, dkf, dv, cos, sin],
                    [_sds((n_rows, UQ_PAD), BF16), _sds((n_rows, 1024), BF16), _sds((n_rows, LANES))], ts=256)


def _assemble_dh(dqkvs, dcqn, cq, dckvn, ckv, dkpe, cos, sin, gq, gkv, comm=None):
    def fn(g1, g2, g3, dcqn_v, cq_v, dckvn_v, ckv_v, dkpe_v, cos_v, sin_v, gq_v, gkv_v):
        dq = dk = dv = 0.0
        for packed in (g1, g2, g3):
            dq_p, dk_p = _unpack2(packed[:, :A_WIDTH])
            dv_p = jnp.concatenate(_unpack2(packed[:, A_WIDTH:]), axis=1)
            dq, dk, dv = dq + dq_p, dk + dk_p, dv + dv_p
        dqa = _rope_t(dq, cos_v, sin_v) * A_SCALE
        dka = _rope_t(dk, cos_v, sin_v)
        dcq, dgq = _rms_bwd(cq_v, gq_v, dcqn_v)
        dckv, dgkv = _rms_bwd(ckv_v, gkv_v, dckvn_v)
        dkr = _rope_t(dkpe_v, cos_v[:, :LANES], sin_v[:, :LANES])
        return jnp.concatenate([dqa, dka, dv, dcq, dkr, dckv], axis=1), dgq, dgkv

    n_rows = cq.shape[0]
    return _rowwise("assemble_dh", fn, [*dqkvs, dcqn, cq, dckvn, ckv, dkpe, cos, sin],
                    [_sds((n_rows, IN_PAD), BF16)], consts=[gq, gkv],
                    reds=[_sds((1, Q_LORA)), _sds((1, KV_LORA))], ts=256,
                    in_dils=[dil for _, dil in PATTERNS] + [1] * 7, comm=comm)


def _layer_fwd(x_f32, x_bf, w, sm, cos, sin, behind=(None, None), rest=None):
    *qkv_views, cq, kpe, ckv = _in_proj(x_bf, w["w_in"], cos, sin)
    ol_views, brought = [], []
    for i, (view, (_, dil)) in enumerate(zip(qkv_views, PATTERNS)):
        ol, *got = _dil_fwd(view, dil, comm=rest[0][i] if rest else None)
        ol_views.append(ol)
        brought.append(got)
    if rest:
        w = {**w, **rest[1](brought)}
    cqn, qf = _mla_q_prep(cq, sm["q_a_norm"], w["w_uq"], cos, sin)
    ckvn, kf, v = _mla_kv_prep(ckv, kpe, sm["kv_a_norm"], w["w_ukv"])
    b_out, b_lse, *behind_attn = _mla_fwd(qf, kf, v, comm=behind[0])
    a_out, mixed = _mix_fwd(ol_views, b_out, sm["a_out_norm"], sm["b_out_norm"])
    x1, x1_bf, r1 = _mm_res_ln("wo_ln1", mixed, w["w_o"], x_f32, sm["ln1_g"], sm["ln1_b"])
    f, *behind_ff1 = _mm_relu2(x1_bf, w["w_ff1"], comm=behind[1])
    x2, x2_bf, r2 = _mm_res_ln("ff2_ln2", f, w["w_ff2"], x1, sm["ln2_g"], sm["ln2_b"])
    saved = dict(w=w, x_bf=x_bf, qkv_views=qkv_views, ol_views=ol_views, cq=cq, ckv=ckv, cqn=cqn, qf=qf, ckvn=ckvn,
                 kf=kf, v=v, b_out=b_out, b_lse=b_lse, a_out=a_out, mixed=mixed, x1_bf=x1_bf, r1=r1, f=f, r2=r2)
    return x2, x2_bf, saved, (behind_attn, behind_ff1)


def _layer_bwd(dx2, w, sm, sv, cos, sin, reduce_of, pending=None):
    dr2, dr2_bf, dg2, db2 = _ln_bwd("ln2_bwd", dx2, sv["r2"], sm["ln2_g"])
    (dw_ff2,) = _mm("dw_ff2", sv["f"], dr2_bf, (F32,), ta=True, tm=1024, tn=1024, tk=2048)
    du, *swapped = _mm("d_u", dr2_bf, w["w_ff2"], (BF16,), tb=True, tm=1024, tn=2048, tk=D_MODEL,
                       epilogue=lambda acc, f: (acc * (2.0 * jnp.sqrt(f.astype(F32))),), extras=[sv["f"]],
                       comm=_swap_comm(pending.pieces) if pending else None)
    if pending:
        pending.after_swap(swapped)
    (dw_ff1,) = _mm("dw_ff1", sv["x1_bf"], du, (F32,), ta=True, tm=1024, tn=1024, tk=2048, out_pieces=True)
    (dy1,) = _mm("d_x1", du, w["w_ff1"], (F32,), tb=True, tm=1024, tn=1024, tk=2048,
                 epilogue=lambda acc, d: (acc + ALPHA * d,), extras=[dr2], b_pieces=True)
    dr1, dr1_bf, dg1, db1 = _ln_bwd("ln1_bwd", dy1, sv["r1"], sm["ln1_g"])
    (dw_o,) = _mm("dw_o", sv["mixed"], dr1_bf, (F32,), ta=True, tm=1024, tn=1024, tk=2048)
    early = reduce_of(dict(w_o=dw_o, w_ff1=dw_ff1, w_ff2=dw_ff2))
    (dmixed,) = _mm("d_mixed", dr1_bf, w["w_o"], (F32,), tb=True, tm=1024, tn=1024, tk=D_MODEL)
    dod1, dod2, dod3, do_b, delta_b, dga, dgb = _mix_bwd(
        dmixed, sv["a_out"], sv["b_out"], sv["ol_views"], sm["a_out_norm"], sm["b_out_norm"])
    dqkvs = []
    for i, ((_, dil), qkv_view, ol_view, dod_view) in enumerate(zip(PATTERNS, sv["qkv_views"], sv["ol_views"], (dod1, dod2, dod3))):
        swap_here = early is not None and i == 0
        dqkv, *swapped = _dil_bwd(qkv_view, ol_view, dod_view, dil, comm=_swap_comm(early.pieces) if swap_here else None)
        dqkvs.append(dqkv)
        if swap_here:
            early.after_swap(swapped)
    to_scatter = (early.s16 if early else []) + (pending.s16 if pending else [])
    dqf, dkf, dv_b, *scattered = _mla_bwd(sv["qf"], sv["kf"], sv["v"], sv["b_lse"], do_b, delta_b,
                                          comm=_scatter_comm(to_scatter) if to_scatter else None)
    if early:
        early.after_scatter(scattered[:len(early.s16)])
    if pending:
        pending.after_scatter(scattered[len(scattered) - len(pending.s16):])
    dq_b, dkv_b, dkpe = _mla_bwd_prep(dqf, dkf, dv_b, cos, sin)
    (dw_uq,) = _mm("dw_uq", sv["cqn"], dq_b, (F32,), ta=True, tm=Q_LORA, tn=1024, tk=2048)
    (dcqn,) = _mm("d_cqn", dq_b, w["w_uq"], (F32,), tb=True, tm=1024, tn=Q_LORA, tk=UQ_PAD)
    (dw_ukv,) = _mm("dw_ukv", sv["ckvn"], dkv_b, (F32,), ta=True, tm=KV_LORA, tn=1024, tk=2048)
    (dckvn,) = _mm("d_ckvn", dkv_b, w["w_ukv"], (F32,), tb=True, tm=1024, tn=KV_LORA, tk=1024)
    to_share = (early.mine if early else []) + (pending.mine if pending else [])
    dh, dgq, dgkv, *shared = _assemble_dh(dqkvs, dcqn, sv["cq"], dckvn, sv["ckv"], dkpe, cos, sin,
                                          sm["q_a_norm"], sm["kv_a_norm"], comm=_share_comm(to_share) if to_share else None)
    if early:
        early.after_share(shared[:len(early.mine)])
    if pending:
        pending.after_share(shared[len(shared) - len(pending.mine):])
    (dw_in,) = _mm("dw_in", sv["x_bf"], dh, (F32,), ta=True, tm=1024, tn=1024, tk=2048)
    (dx,) = _mm("d_x", dh, w["w_in"], (F32,), tb=True, tm=1024, tn=1024, tk=2048,
                epilogue=lambda acc, d: (acc + ALPHA * d,), extras=[dr1])
    late = reduce_of(dict(w_in=dw_in, w_uq=dw_uq, w_ukv=dw_ukv))
    dsm = dict(q_a_norm=dgq, kv_a_norm=dgkv, a_out_norm=dga, b_out_norm=dgb, ln1_g=dg1, ln1_b=db1, ln2_g=dg2, ln2_b=db2)
    return dx, dsm, late


def _pad_w_in(w):
    return jnp.concatenate([w[:, :1792], w[:, 1920:1984], jnp.zeros((w.shape[0], 64), w.dtype), w[:, 1792:1920]], axis=1)


def _unpad_w_in(w):
    return jnp.concatenate([w[:, :1792], w[:, 1920:2048], w[:, 1792:1856]], axis=1)


def _rope_tables(n_rows):
    half = A_HEAD_DIM // 2
    inv_freq = ROPE_THETA ** (-jnp.arange(half, dtype=F32) / half)
    ang = jnp.arange(n_rows, dtype=F32)[:, None] * inv_freq[None, :]
    cos = jnp.tile(jnp.cos(ang), (1, 2 * A_HEADS))
    sin = jnp.tile(jnp.concatenate([-jnp.sin(ang), jnp.sin(ang)], axis=1), (1, A_HEADS))
    return cos, sin


W_NAMES = ("w_in", "w_uq", "w_ukv", "w_o", "w_ff1", "w_ff2")
PIECE = dict(w_in=(512, 512), w_uq=(128, 256), w_ukv=(64, 256), w_o=(128, 1024), w_ff1=(512, 1024), w_ff2=(512, 1024))
SHARD_COLS = dict(w_in=496, w_uq=192, w_ukv=256, w_o=1024, w_ff1=1024, w_ff2=1024)


def _col_pieces(t, pad_to):
    rows, _, cols = t.shape
    t = jnp.pad(t, ((0, 0), (0, 0), (0, pad_to - cols)))
    return t.reshape(2, rows // 2, N_CHIPS, pad_to).transpose(2, 0, 1, 3)


def _grad_pieces(dws):
    def heads(t, rows):
        return jnp.concatenate([t[:, :512].reshape(rows, MLA_HEADS, LANES), t[:, 512:].reshape(rows, MLA_HEADS, LANES)], axis=2)

    make = dict(
        w_in=lambda t: _col_pieces(_unpad_w_in(t).reshape(D_MODEL, N_CHIPS, SHARD_COLS["w_in"]), 512),
        w_uq=lambda t: _col_pieces(heads(t, Q_LORA), 256),
        w_ukv=lambda t: _col_pieces(heads(t, KV_LORA), 256),
        w_o=lambda t: t.reshape(N_CHIPS, 2, 128, D_MODEL),
        w_ff1=lambda t: t,
        w_ff2=lambda t: t.reshape(N_CHIPS, 2, 512, D_MODEL))
    return {n: make[n](t) for n, t in dws.items()}


def _weights_from_pieces(p):
    def cols(t):
        return t.transpose(1, 2, 0, 3).reshape(2 * t.shape[2], N_CHIPS, t.shape[3])

    def heads(t, rows):
        return jnp.concatenate([t[:, :, :LANES].reshape(rows, 512), t[:, :, LANES:].reshape(rows, 512)], axis=1)

    make = dict(
        w_in=lambda t: _pad_w_in(cols(t)[:, :, :SHARD_COLS["w_in"]].reshape(D_MODEL, IN_COLS)),
        w_uq=lambda t: heads(cols(t), Q_LORA),
        w_ukv=lambda t: heads(cols(t), KV_LORA),
        w_o=lambda t: t.reshape(D_MODEL, D_MODEL),
        w_ff1=lambda t: t,
        w_ff2=lambda t: t.reshape(D_FF, D_MODEL))
    return {n: make[n](t) for n, t in p.items()}


def _my_piece(name, shard, half):
    rows, cols = shard.shape[0] // 2, shard.shape[1]
    t = lax.dynamic_slice_in_dim(shard, half * rows, rows, axis=0).astype(BF16)
    return jnp.pad(t, ((0, 0), (0, PIECE[name][1] - cols)))


MESH = pl.DeviceIdType.MESH
ANY = pl.BlockSpec(memory_space=pl.ANY)


class _Comm:
    def __init__(self, ins, out_shapes, n_sems, start, finish):
        self.ins, self.out_shapes, self.n_sems, self.start, self.finish = list(ins), list(out_shapes), n_sems, start, finish

    def sem_scratch(self):
        return [pltpu.SemaphoreType.DMA((self.n_sems,)), pltpu.SemaphoreType.DMA((self.n_sems,))]


def _comm_call(name, comm):
    n_in, n_out = len(comm.ins), len(comm.out_shapes)

    def body(*refs):
        ins, outs, sems = refs[:n_in], refs[n_in:n_in + n_out], refs[n_in + n_out:]
        comm.start(ins, outs, *sems)
        comm.finish(ins, outs, *sems)

    return pl.pallas_call(body, name=name, out_shape=comm.out_shapes, in_specs=[ANY] * n_in, out_specs=[ANY] * n_out,
                          scratch_shapes=comm.sem_scratch())(*comm.ins)


def _all_gather_comm(blocks):
    n = len(blocks)

    def plan(x_refs, out_refs, send_sems, recv_sems):
        x, y, c = lax.axis_index("x"), lax.axis_index("y"), lax.axis_index("c")
        me, sibling = (x, y, c), (x, y, 1 - c)
        chips = [(1 - x, y), (x, 1 - y), (1 - x, 1 - y)]

        def copy(t, k, blk, to, src=None):
            px, py, pc = blk
            slot = out_refs[t].at[4 * px + 2 * py + pc]
            return pltpu.make_async_remote_copy(
                src_ref=slot if src is None else src, dst_ref=slot,
                send_sem=send_sems.at[7 * t + k], recv_sem=recv_sems.at[7 * t + k], device_id=to, device_id_type=MESH)

        first = []
        for t in range(n):
            first.append(copy(t, 0, me, sibling, src=x_refs[t]))
            first += [copy(t, 1 + j, me, (*chip, c), src=x_refs[t]) for j, chip in enumerate(chips)]
        return me, sibling, chips, c, copy, first

    def start(x_refs, out_refs, send_sems, recv_sems):
        for cp in plan(x_refs, out_refs, send_sems, recv_sems)[-1]:
            cp.start()

    def finish(x_refs, out_refs, send_sems, recv_sems):
        me, sibling, chips, c, copy, sent = plan(x_refs, out_refs, send_sems, recv_sems)
        for j, chip in enumerate(chips):
            for t in range(n):
                copy(t, 1 + j, (*chip, c), me).wait_recv()
                sent.append(copy(t, 4 + j, (*chip, c), sibling))
                sent[-1].start()
        for t in range(n):
            copy(t, 0, sibling, me).wait_recv()
        for j, chip in enumerate(chips):
            for t in range(n):
                copy(t, 4 + j, (*chip, 1 - c), me).wait_recv()
        for cp in sent:
            cp.wait_send()

    return _Comm(blocks, [_sds((N_DEV,) + b.shape, b.dtype) for b in blocks], 7 * n, start, finish)


def _simple_comm(ins, out_shapes, n_sems, copies):
    def start(in_refs, out_refs, send_sems, recv_sems):
        for cp in copies(in_refs, out_refs, send_sems, recv_sems):
            cp.start()

    def finish(in_refs, out_refs, send_sems, recv_sems):
        for cp in copies(in_refs, out_refs, send_sems, recv_sems):
            cp.wait()

    return _Comm(ins, out_shapes, n_sems, start, finish)


def _swap_comm(gs):
    def copies(g_refs, got_refs, send_sems, recv_sems):
        c = lax.axis_index("c")
        sibling = (lax.axis_index("x"), lax.axis_index("y"), 1 - c)
        return [pltpu.make_async_remote_copy(src_ref=g_refs[t].at[k, 1 - c], dst_ref=got_refs[t].at[k],
                                             send_sem=send_sems.at[N_CHIPS * t + k], recv_sem=recv_sems.at[N_CHIPS * t + k],
                                             device_id=sibling, device_id_type=MESH)
                for t in range(len(gs)) for k in range(N_CHIPS)]

    return _simple_comm(gs, [_sds((N_CHIPS,) + g.shape[2:], g.dtype) for g in gs], N_CHIPS * len(gs), copies)


def _scatter_comm(hs):
    def copies(h_refs, got_refs, send_sems, recv_sems):
        x, y, c = lax.axis_index("x"), lax.axis_index("y"), lax.axis_index("c")
        chips = [(1 - x, y), (x, 1 - y), (1 - x, 1 - y)]
        return [pltpu.make_async_remote_copy(src_ref=h_refs[t].at[2 * px + py], dst_ref=got_refs[t].at[rel],
                                             send_sem=send_sems.at[3 * t + rel], recv_sem=recv_sems.at[3 * t + rel],
                                             device_id=(px, py, c), device_id_type=MESH)
                for t in range(len(hs)) for rel, (px, py) in enumerate(chips)]

    return _simple_comm(hs, [_sds((3,) + h.shape[1:], h.dtype) for h in hs], 3 * len(hs), copies)


def _share_comm(rs):
    def copies(r_refs, got_refs, send_sems, recv_sems):
        sibling = (lax.axis_index("x"), lax.axis_index("y"), 1 - lax.axis_index("c"))
        return [pltpu.make_async_remote_copy(src_ref=r_refs[t], dst_ref=got_refs[t], send_sem=send_sems.at[t],
                                             recv_sem=recv_sems.at[t], device_id=sibling, device_id_type=MESH)
                for t in range(len(rs))]

    return _simple_comm(rs, [_sds(r.shape, r.dtype) for r in rs], len(rs), copies)


SUM_STEPS = 4


def _pair_sums(name, gs, gots, c_arr):
    n = len(gs)
    dims = [(g.shape[2] // SUM_STEPS, g.shape[3]) for g in gs]

    def body(c_ref, *refs):
        for t in range(n):
            s = refs[t][...] + refs[n + t][...]
            refs[2 * n + t][...] = s
            refs[3 * n + t][...] = s.astype(BF16)

    def part(tr, pc):
        return pl.BlockSpec((None, tr, pc), lambda k, i, c_ref: (k, i, 0))

    def kept(tr, pc):
        return pl.BlockSpec((None, None, tr, pc), lambda k, i, c_ref: (k, c_ref[0], i, 0))

    outs = pl.pallas_call(
        body, name=name,
        out_shape=[_sds((N_CHIPS,) + g.shape[2:]) for g in gs] + [_sds((N_CHIPS,) + g.shape[2:], BF16) for g in gs],
        grid_spec=pltpu.PrefetchScalarGridSpec(
            num_scalar_prefetch=1, grid=(N_CHIPS, SUM_STEPS),
            in_specs=[kept(*d) for d in dims] + [part(*d) for d in dims],
            out_specs=[part(*d) for d in dims] * 2),
        compiler_params=_params(("parallel", "parallel")),
    )(c_arr, *gs, *gots)
    return list(outs[:n]), list(outs[n:])


def _chips_sums(name, h32s, gots, chip_arr):
    n = len(h32s)
    dims = [(h.shape[1] // SUM_STEPS, h.shape[2]) for h in h32s]

    def body(chip_ref, *refs):
        for t in range(n):
            got_ref = refs[n + t]
            refs[2 * n + t][...] = refs[t][...] + got_ref[0] + got_ref[1] + got_ref[2]

    return pl.pallas_call(
        body, name=name, out_shape=[_sds(h.shape[1:]) for h in h32s],
        grid_spec=pltpu.PrefetchScalarGridSpec(
            num_scalar_prefetch=1, grid=(SUM_STEPS,),
            in_specs=[pl.BlockSpec((None, tr, pc), lambda i, chip_ref: (chip_ref[0], i, 0)) for tr, pc in dims]
            + [pl.BlockSpec((3, tr, pc), lambda i, chip_ref: (0, i, 0)) for tr, pc in dims],
            out_specs=[pl.BlockSpec((tr, pc), lambda i, chip_ref: (i, 0)) for tr, pc in dims]),
        compiler_params=_params(("parallel",)),
    )(chip_arr, *h32s, *gots)


class _Reduce:
    def __init__(self, tag, dws, c, chip):
        gp = _grad_pieces(dws)
        self.tag, self.c, self.names, self.pieces = tag, c, list(gp), list(gp.values())
        self.c_arr, self.chip_arr = jnp.reshape(c, (1,)).astype(jnp.int32), jnp.reshape(chip, (1,)).astype(jnp.int32)
        self.shards = None

    def after_swap(self, gots):
        self.s32, self.s16 = _pair_sums(f"rs_pair_sum_{self.tag}", self.pieces, gots, self.c_arr)

    def after_scatter(self, parts):
        self.mine = list(_chips_sums(f"rs_chips_sum_{self.tag}", self.s32, parts, self.chip_arr))

    def after_share(self, theirs):
        self.shards = {}
        for n, a, b in zip(self.names, self.mine, theirs):
            both = jnp.where(self.c == 0, jnp.concatenate([a, b], axis=0), jnp.concatenate([b, a], axis=0))
            self.shards[n] = both[:, :SHARD_COLS[n]]

    def run_alone(self):
        self.after_swap(_comm_call(f"rs_pair_{self.tag}", _swap_comm(self.pieces)))
        self.after_scatter(_comm_call(f"rs_chips_{self.tag}", _scatter_comm(self.s16)))
        self.after_share(_comm_call(f"rs_share_{self.tag}", _share_comm(self.mine)))


class _LayerWeights:
    def __init__(self, shards, c, dev):
        self.shards, self.c, self.dev = shards, c, dev

    def comm(self, l, names=W_NAMES):
        return _all_gather_comm([_my_piece(n, self.shards[l][n], self.c) for n in names])

    def take(self, l, gathered, names=W_NAMES):
        pieces = {}
        for n, g in zip(names, gathered):
            mine = _my_piece(n, self.shards[l][n], self.c)
            pieces[n] = lax.dynamic_update_index_in_dim(g, mine, self.dev, 0).reshape((N_CHIPS, 2) + mine.shape)
        return _weights_from_pieces(pieces)

    def first(self):
        now, later = ("w_in", "w_uq", "w_ukv"), ("w_ff1", "w_ff2", "w_o")
        weights = self.take(0, _comm_call("ag_l0", self.comm(0, now)), now)
        comms = [self.comm(0, (n,)) for n in later]

        def take_rest(results):
            rest = {}
            for n, gathered in zip(later, results):
                rest.update(self.take(0, gathered, (n,)))
            return rest

        return weights, (comms, take_rest)


def _all_reduce_small(vec):
    rows, lanes = vec.shape

    def gather_body(x_ref, out_ref, send_sems, recv_sems):
        x, y, c = lax.axis_index("x"), lax.axis_index("y"), lax.axis_index("c")
        me, sibling = (x, y, c), (x, y, 1 - c)
        chips = [(1 - x, y), (x, 1 - y), (1 - x, 1 - y)]

        def slot(px, py, pc):
            return out_ref.at[4 * px + 2 * py + pc]

        def copy(k, blk, to, src=None):
            return pltpu.make_async_remote_copy(
                src_ref=slot(*blk) if src is None else src, dst_ref=slot(*blk),
                send_sem=send_sems.at[k], recv_sem=recv_sems.at[k], device_id=to, device_id_type=MESH)

        out_ref[4 * x + 2 * y + c] = x_ref[...]
        first = [copy(0, me, sibling, src=x_ref)]
        first += [copy(1 + j, me, (*chip, c), src=x_ref) for j, chip in enumerate(chips)]
        for cp in first:
            cp.start()
        passed = [copy(4 + j, (*chip, c), sibling) for j, chip in enumerate(chips)]
        for j, chip in enumerate(chips):
            copy(1 + j, (*chip, c), me).wait_recv()
            passed[j].start()
        copy(0, sibling, me).wait_recv()
        for j, chip in enumerate(chips):
            copy(4 + j, (*chip, 1 - c), me).wait_recv()
        for cp in first + passed:
            cp.wait_send()

    vmem = pl.BlockSpec(memory_space=pltpu.VMEM)
    allv = pl.pallas_call(
        gather_body, name="small_all_gather", out_shape=_sds((N_DEV, rows, lanes)),
        in_specs=[vmem], out_specs=vmem,
        scratch_shapes=[pltpu.SemaphoreType.DMA((7,)), pltpu.SemaphoreType.DMA((7,))],
    )(vec)

    def sum_body(a_ref, o_ref):
        acc = a_ref[0]
        for d in range(1, N_DEV):
            acc = acc + a_ref[d]
        o_ref[...] = acc

    return pl.pallas_call(sum_body, name="small_sum", out_shape=_sds((rows, lanes)), in_specs=[vmem], out_specs=vmem)(allv)


def _adamw(name, w, g, m, v, ts=512, comm=None):
    def fn(w_v, g_v, m_v, v_v):
        m_n = ADAM_B1 * m_v + (1.0 - ADAM_B1) * g_v
        v_n = ADAM_B2 * v_v + (1.0 - ADAM_B2) * (g_v * g_v)
        m_hat = m_n / (1.0 - ADAM_B1 ** ADAM_STEP)
        v_hat = v_n / (1.0 - ADAM_B2 ** ADAM_STEP)
        delta = -ADAM_LR * (m_hat / (jnp.sqrt(v_hat) + ADAM_EPS) + ADAM_WD * w_v)
        return delta, m_n, v_n

    return _rowwise(name, fn, [w, g, m, v], [_sds(w.shape)] * 3, ts=ts, comm=comm)


def _pack_small(per_layer):
    flat = jnp.concatenate([per_layer[l][n].reshape(-1) for l in range(DEPTH) for n in SMALL_NAMES])
    return jnp.pad(flat, (0, SMALL_ROWS * LANES - flat.shape[0])).reshape(SMALL_ROWS, LANES)


def _unpack_small(packed):
    flat = packed.reshape(-1)
    per = sum(SMALL_SIZES)
    out = {}
    for n, size, off in zip(SMALL_NAMES, SMALL_SIZES, [sum(SMALL_SIZES[:i]) for i in range(len(SMALL_SIZES))]):
        out[n] = jnp.stack([flat[l * per + off:l * per + off + size] for l in range(DEPTH)])
    return out


NEXT_WEIGHTS_BEHIND = (("w_ff1", "w_ff2"), ("w_in", "w_uq", "w_ukv", "w_o"))


def _fwd_bwd(x, target, layer_weights, smalls, on_layer_grads):
    depth = len(smalls)
    cos, sin = _rope_tables(x.shape[0])
    h_f32, h_bf = x, x.astype(BF16)
    saved = []
    w, rest = layer_weights.first()
    for l in range(depth):
        more = l + 1 < depth
        behind = [layer_weights.comm(l + 1, names) if more else None for names in NEXT_WEIGHTS_BEHIND]
        h_f32, h_bf, sv, brought = _layer_fwd(h_f32, h_bf, w, smalls[l], cos, sin, behind, rest)
        saved.append(sv)
        if more:
            w, rest = {}, None
            for names, got in zip(NEXT_WEIGHTS_BEHIND, brought):
                w.update(layer_weights.take(l + 1, got, names))
    loss_part, dh = _loss_fn(h_f32, target)
    small_grads, pending = [None] * depth, None
    for l in reversed(range(depth)):
        dh, small_grads[l], pending = _layer_bwd(dh, saved[l]["w"], smalls[l], saved[l], cos, sin,
                                                 lambda dws, l=l: on_layer_grads(l, dws), pending)
    return loss_part, dh, small_grads, pending


def kernel(x, w_in, q_a_norm, kv_a_norm, w_uq, w_ukv, a_out_norm, b_out_norm, w_o, ln1_g, ln1_b, w_ff1, w_ff2, ln2_g, ln2_b, loss_target, m_w_in, m_q_a_norm, m_kv_a_norm, m_w_uq, m_w_ukv, m_a_out_norm, m_b_out_norm, m_w_o, m_ln1_g, m_ln1_b, m_w_ff1, m_w_ff2, m_ln2_g, m_ln2_b, v_w_in, v_q_a_norm, v_kv_a_norm, v_w_uq, v_w_ukv, v_a_out_norm, v_b_out_norm, v_w_o, v_ln1_g, v_ln1_b, v_w_ff1, v_w_ff2, v_ln2_g, v_ln2_b):
    c = lax.axis_index("c")
    chip = 2 * lax.axis_index("x") + lax.axis_index("y")
    big = dict(w_in=w_in, w_uq=w_uq, w_ukv=w_ukv, w_o=w_o, w_ff1=w_ff1, w_ff2=w_ff2)
    big_m = dict(w_in=m_w_in, w_uq=m_w_uq, w_ukv=m_w_ukv, w_o=m_w_o, w_ff1=m_w_ff1, w_ff2=m_w_ff2)
    big_v = dict(w_in=v_w_in, w_uq=v_w_uq, w_ukv=v_w_ukv, w_o=v_w_o, w_ff1=v_w_ff1, w_ff2=v_w_ff2)
    small = dict(q_a_norm=q_a_norm, kv_a_norm=kv_a_norm, a_out_norm=a_out_norm, b_out_norm=b_out_norm,
                 ln1_g=ln1_g, ln1_b=ln1_b, ln2_g=ln2_g, ln2_b=ln2_b)
    small_m = dict(q_a_norm=m_q_a_norm, kv_a_norm=m_kv_a_norm, a_out_norm=m_a_out_norm, b_out_norm=m_b_out_norm,
                   ln1_g=m_ln1_g, ln1_b=m_ln1_b, ln2_g=m_ln2_g, ln2_b=m_ln2_b)
    small_v = dict(q_a_norm=v_q_a_norm, kv_a_norm=v_kv_a_norm, a_out_norm=v_a_out_norm, b_out_norm=v_b_out_norm,
                   ln1_g=v_ln1_g, ln1_b=v_ln1_b, ln2_g=v_ln2_g, ln2_b=v_ln2_b)

    layer_weights = _LayerWeights([{n: big[n][l] for n in W_NAMES} for l in range(DEPTH)], c, 2 * chip + c)
    smalls = [{n: small[n][l][None, :] for n in SMALL_NAMES} for l in range(DEPTH)]
    reductions = [[] for _ in range(DEPTH)]

    def reduce_layer(l, dws):
        reductions[l].append(_Reduce(f"l{l}_{len(reductions[l])}", dws, c, chip))
        return reductions[l][-1]

    loss_part, grad_x, small_grads, last = _fwd_bwd(x[0], loss_target[0], layer_weights, smalls, reduce_layer)
    loss = lax.psum(loss_part[0, 0], ("x", "y", "c"))
    grad_x = grad_x[None]

    g_small_packed = _all_reduce_small(_pack_small(small_grads))
    pack_in = lambda d: _pack_small([{n: d[n][l] for n in SMALL_NAMES} for l in range(DEPTH)])
    sd, sm_, sv_ = _adamw("adamw_small", pack_in(small), g_small_packed, pack_in(small_m), pack_in(small_v), ts=SMALL_ROWS)
    g_small, d_small, m_small, v_small = (_unpack_small(t) for t in (g_small_packed, sd, sm_, sv_))

    last.run_alone()
    g_big, d_big, m_big, v_big = {}, {}, {}, {}
    for n in W_NAMES:
        g = jnp.stack([next(r.shards[n] for r in reductions[l] if n in r.names) for l in range(DEPTH)])
        shape = g.shape
        flat = lambda t: t.reshape(shape[0] * shape[1], shape[2])
        d_, m_, v_ = _adamw(f"adamw_{n}", flat(big[n]), flat(g), flat(big_m[n]), flat(big_v[n]))
        g_big[n], d_big[n], m_big[n], v_big[n] = g, d_.reshape(shape), m_.reshape(shape), v_.reshape(shape)

    order = ("w_in", "q_a_norm", "kv_a_norm", "w_uq", "w_ukv", "a_out_norm", "b_out_norm", "w_o", "ln1_g", "ln1_b",
             "w_ff1", "w_ff2", "ln2_g", "ln2_b")
    pick = lambda bigd, smalld: [bigd[n] if n in bigd else smalld[n] for n in order]
    return (loss, grad_x, *pick(g_big, g_small), *pick(d_big, d_small), *pick(m_big, m_small), *pick(v_big, v_small))
```

```python
import math

import jax
import jax.numpy as jnp
from jax import lax
from jax.experimental import pallas as pl
from jax.experimental.pallas import tpu as pltpu

F32, BF16 = jnp.float32, jnp.bfloat16

D_MODEL = 1024
DEPTH = 4
A_HEAD_DIM = 64
A_HEADS = 8
A_WIDTH = 512
PATTERNS = ((128, 1), (512, 4), (2048, 16))
SPAN = 128
MLA_HEADS = 4
QK_NOPE = 128
QK_ROPE = 64
V_DIM = 128
Q_LORA = 256
KV_LORA = 128
MLA_WIDTH = 512
QK_PAD = 256
IN_COLS = 1984
IN_PAD = 2048
UQ_PAD = 1024
D_FF = 4096
ROPE_THETA = 10000.0
ALPHA = (2.0 * DEPTH) ** 0.25
LN_EPS = 1e-5
RMS_EPS = 1e-6
MLA_SCALE = (QK_NOPE + QK_ROPE) ** -0.5
A_SCALE = A_HEAD_DIM ** -0.5

ADAM_LR, ADAM_B1, ADAM_B2, ADAM_EPS, ADAM_WD, ADAM_STEP = 0.001, 0.9, 0.999, 1e-08, 0.01, 10

VMEM_LIMIT_BYTES = 56 * 1024 * 1024
NEG = -1e30
N_CHIPS, N_DEV = 4, 8
LANES = 128

SMALL_NAMES = ("q_a_norm", "kv_a_norm", "a_out_norm", "b_out_norm", "ln1_g", "ln1_b", "ln2_g", "ln2_b")
SMALL_SIZES = (256, 128, 512, 512, 1024, 1024, 1024, 1024)
SMALL_ROWS = 176


def _params(sem):
    return pltpu.CompilerParams(dimension_semantics=sem, vmem_limit_bytes=VMEM_LIMIT_BYTES)


FAST_STRIDE = 4


def _interleave(parts, scratch):
    d, (n, width) = len(parts), parts[0].shape
    if d > FAST_STRIDE and d % FAST_STRIDE == 0:
        inner = [_interleave(parts[r::FAST_STRIDE], scratch) for r in range(FAST_STRIDE)]
        return _interleave(inner, scratch)
    for r in range(d):
        for s in range(width // LANES):
            scratch.at[s][pl.ds(r, n, stride=d), :] = parts[r][:, s * LANES:(s + 1) * LANES]
    return jnp.concatenate([scratch.at[s][pl.ds(0, n * d), :] for s in range(width // LANES)], axis=1)


def _deinterleave(x, d, scratch):
    n, width = x.shape[0] // d, x.shape[1]
    if d > FAST_STRIDE and d % FAST_STRIDE == 0:
        outer = _deinterleave(x, FAST_STRIDE, scratch)
        inner = [_deinterleave(y, d // FAST_STRIDE, scratch) for y in outer]
        return [inner[r % FAST_STRIDE][r // FAST_STRIDE] for r in range(d)]
    for s in range(width // LANES):
        scratch.at[s][pl.ds(0, n * d), :] = x[:, s * LANES:(s + 1) * LANES]
    return [jnp.concatenate([scratch.at[s][pl.ds(r, n, stride=d), :] for s in range(width // LANES)], axis=1) for r in range(d)]


def _rowwise(name, fn, ins, outs, consts=(), reds=(), ts=512, in_dils=None, out_dils=None, comm=None):
    in_dils = list(in_dils or [1] * len(ins))
    out_dils = list(out_dils or [1] * len(outs))
    n_rows = ins[0].shape[-2] * in_dils[0]
    ts = min(ts, n_rows)
    assert n_rows % ts == 0
    n_in, n_c, n_o, n_r = len(ins), len(consts), len(outs), len(reds)
    viewed = [(a.shape, d) for a, d in zip(list(ins) + list(outs), in_dils + out_dils) if d > 1]
    grid = (n_rows // ts,)
    h_ins, h_in_specs, h_outs, h_out_specs, begin, end = _hosted(comm, grid)
    n_hi, n_ho = len(h_ins), len(h_outs)

    def tile_spec(shape, d=1):
        if len(shape) == 2:
            return pl.BlockSpec((ts // d, shape[1]), lambda i: (i, 0))
        return pl.BlockSpec((shape[0], ts, shape[2]), lambda i: (0, i, 0))

    def whole_spec(shape):
        return pl.BlockSpec(shape, lambda i: (0,) * len(shape))

    def body(*refs):
        in_refs, c_refs, hi_refs = refs[:n_in], refs[n_in:n_in + n_c], refs[n_in + n_c:n_in + n_c + n_hi]
        refs = refs[n_in + n_c + n_hi:]
        o_refs, r_refs, ho_refs = refs[:n_o], refs[n_o:n_o + n_r], refs[n_o + n_r:n_o + n_r + n_ho]
        scratches = list(refs[n_o + n_r + n_ho:])
        sems = [scratches.pop(), scratches.pop()][::-1] if comm else []
        begin(hi_refs, ho_refs, sems)
        vals = []
        for r, d in zip(in_refs, in_dils):
            if d == 1:
                vals.append(r[...])
            else:
                width = r.shape[1] // d
                vals.append(_interleave([r[:, k * width:(k + 1) * width] for k in range(d)], scratches.pop(0)))
        res = fn(*vals, *[r[...] for r in c_refs])
        res = tuple(res) if isinstance(res, (tuple, list)) else (res,)
        for r, v, d in zip(o_refs, res[:n_o], out_dils):
            if len(r.shape) == 3:
                for g in range(r.shape[0]):
                    r[g] = v[g].astype(r.dtype)
            elif d == 1:
                r[...] = v.astype(r.dtype)
            else:
                width = r.shape[1] // d
                for k, part in enumerate(_deinterleave(v, d, scratches.pop(0))):
                    r[:, k * width:(k + 1) * width] = part.astype(r.dtype)
        if n_r:
            i = pl.program_id(0)

            @pl.when(i == 0)
            def _():
                for r, v in zip(r_refs, res[n_o:]):
                    r[...] = v

            @pl.when(i > 0)
            def _():
                for r, v in zip(r_refs, res[n_o:]):
                    r[...] += v
        end(hi_refs, ho_refs, sems)

    out = pl.pallas_call(
        body, name=name, grid=grid,
        in_specs=[tile_spec(a.shape, d) for a, d in zip(ins, in_dils)] + [whole_spec(c.shape) for c in consts] + h_in_specs,
        out_specs=[tile_spec(o.shape, d) for o, d in zip(outs, out_dils)] + [whole_spec(r.shape) for r in reds] + h_out_specs,
        out_shape=list(outs) + list(reds) + h_outs,
        scratch_shapes=[pltpu.VMEM((shape[1] // d // LANES, ts, LANES), F32) for shape, d in viewed]
        + (comm.sem_scratch() if comm else []),
        compiler_params=_params(("arbitrary",)),
    )(*ins, *consts, *h_ins)
    return out


def _sds(shape, dtype=F32):
    return jax.ShapeDtypeStruct(tuple(shape), dtype)


def _hosted(comm, grid):
    if comm is None:
        return [], [], [], [], (lambda *a: None), (lambda *a: None)

    def edge(which, at, ins, outs, sems):
        ids = [pl.program_id(d) for d in range(len(grid))]
        hit = ids[0] == at[0]
        for i, v in zip(ids[1:], at[1:]):
            hit = hit & (i == v)

        @pl.when(hit)
        def _():
            which(ins, outs, *sems)

    begin = lambda ins, outs, sems: edge(comm.start, [0] * len(grid), ins, outs, sems)
    end = lambda ins, outs, sems: edge(comm.finish, [g - 1 for g in grid], ins, outs, sems)
    return comm.ins, [ANY] * len(comm.ins), comm.out_shapes, [ANY] * len(comm.out_shapes), begin, end


def _mm(name, a, b, out_dtypes, *, ta=False, tb=False, tm=512, tn=512, tk=512, epilogue=None, extras=(), vecs=(),
        comm=None, b_pieces=False, out_pieces=False, reds=()):
    (k_dim, m_dim) = a.shape if ta else a.shape[::-1]
    if b_pieces:
        b_rows, b_cols = 2 * b.shape[2], N_CHIPS * b.shape[3]
        (n_dim, k2) = (b_rows, b_cols) if tb else (b_cols, b_rows)
    else:
        (n_dim, k2) = b.shape if tb else b.shape[::-1]
    assert k_dim == k2
    tm, tn, tk = min(tm, m_dim), min(tn, n_dim), min(tk, k_dim)
    assert m_dim % tm == 0 and n_dim % tn == 0 and k_dim % tk == 0
    nk = k_dim // tk
    grid = (m_dim // tm, n_dim // tn, nk)
    a_spec = pl.BlockSpec((tk, tm), lambda i, j, k: (k, i)) if ta else pl.BlockSpec((tm, tk), lambda i, j, k: (i, k))
    if b_pieces:
        assert tn == b_rows and tk % b.shape[3] == 0 if tb else (tk, tn) == (b_rows, b.shape[3])
        n_bp = tk // b.shape[3] if tb else 1
        b_spec = pl.BlockSpec((n_bp,) + b.shape[1:], (lambda i, j, k: (k, 0, 0, 0)) if tb else (lambda i, j, k: (j, 0, 0, 0)))
    else:
        b_spec = pl.BlockSpec((tn, tk), lambda i, j, k: (j, k)) if tb else pl.BlockSpec((tk, tn), lambda i, j, k: (k, j))
    assert not out_pieces or (tm == m_dim and len(out_dtypes) == 1)
    assert not reds or tn == n_dim
    dims = (((0 if ta else 1,), (1 if tb else 0,)), ((), ()))
    n_e, n_v, n_o, n_r = len(extras), len(vecs), len(out_dtypes), len(reds)
    c_ins, c_in_specs, c_outs, c_out_specs, begin, end = _hosted(comm, grid)
    n_ci, n_co = len(c_ins), len(c_outs)

    def body(a_ref, b_ref, *refs):
        e_refs, v_refs, ci_refs = refs[:n_e], refs[n_e:n_e + n_v], refs[n_e + n_v:n_e + n_v + n_ci]
        refs = refs[n_e + n_v + n_ci:]
        o_refs, r_refs, co_refs = refs[:n_o], refs[n_o:n_o + n_r], refs[n_o + n_r:n_o + n_r + n_co]
        scratch = refs[n_o + n_r + n_co:]
        sems = scratch[-2:]
        first_rows = pl.program_id(0) == 0
        begin(ci_refs, co_refs, sems)
        if b_pieces:
            pc = b.shape[3]
            part = None
            for cc in range(n_bp):
                a_val = a_ref[:, cc * pc:(cc + 1) * pc] if tb else a_ref[...]
                term = lax.dot_general(a_val.astype(BF16), b_ref[cc].reshape(b_rows, pc).astype(BF16), dims,
                                       preferred_element_type=F32)
                part = term if part is None else part + term
        else:
            part = lax.dot_general(a_ref[...].astype(BF16), b_ref[...].astype(BF16), dims, preferred_element_type=F32)

        def finish(acc):
            outs = epilogue(acc, *[r[...] for r in e_refs], *[r[...] for r in v_refs]) if epilogue else (acc,)
            for r, v in zip(o_refs, outs):
                r[...] = (v.reshape(2, tm // 2, tn) if out_pieces else v).astype(r.dtype)
            if n_r:
                @pl.when(first_rows)
                def _():
                    for r, v in zip(r_refs, outs[n_o:]):
                        r[...] = v

                @pl.when(jnp.logical_not(first_rows))
                def _():
                    for r, v in zip(r_refs, outs[n_o:]):
                        r[...] += v

        if nk == 1:
            finish(part)
        else:
            acc_ref = scratch[0]
            k = pl.program_id(2)

            @pl.when(k == 0)
            def _():
                acc_ref[...] = part

            @pl.when(k > 0)
            def _():
                acc_ref[...] += part

            @pl.when(k == nk - 1)
            def _():
                finish(acc_ref[...])
        end(ci_refs, co_refs, sems)

    tile = pl.BlockSpec((tm, tn), lambda i, j, k: (i, j))
    if out_pieces:
        out_tile, out_shape = pl.BlockSpec((None, 2, tm // 2, tn), lambda i, j, k: (j, 0, 0, 0)), (n_dim // tn, 2, tm // 2, tn)
    else:
        out_tile, out_shape = tile, (m_dim, n_dim)
    out = pl.pallas_call(
        body, name=name, grid=grid,
        in_specs=[a_spec, b_spec] + [tile] * n_e + [pl.BlockSpec((1, tn), lambda i, j, k: (0, j))] * n_v + c_in_specs,
        out_specs=[out_tile] * n_o + [pl.BlockSpec((1, tn), lambda i, j, k: (0, j))] * n_r + c_out_specs,
        out_shape=[_sds(out_shape, d) for d in out_dtypes] + list(reds) + c_outs,
        scratch_shapes=([pltpu.VMEM((tm, tn), F32)] if nk > 1 else []) + (comm.sem_scratch() if comm else []),
        compiler_params=_params(("arbitrary",) * 3 if comm or reds else ("parallel", "parallel", "arbitrary")),
    )(a, b, *extras, *vecs, *c_ins)
    return out


def _swap_halves(x):
    width = x.shape[1]
    lane = lax.broadcasted_iota(jnp.int32, x.shape, 1)
    return jnp.where((lane % 64) < 32, pltpu.roll(x, width - 32, 1), pltpu.roll(x, 32, 1))


def _rope(x, cos, sin_signed):
    return x * cos + _swap_halves(x) * sin_signed


def _rope_t(d, cos, sin_signed):
    return d * cos - _swap_halves(d) * sin_signed


def _rms(x, g):
    rstd = lax.rsqrt(jnp.mean(x * x, axis=-1, keepdims=True) + RMS_EPS)
    return x * rstd * g


def _rms_bwd(x, g, dy):
    rstd = lax.rsqrt(jnp.mean(x * x, axis=-1, keepdims=True) + RMS_EPS)
    xh = x * rstd
    dyg = dy * g
    dx = rstd * (dyg - xh * jnp.mean(dyg * xh, axis=-1, keepdims=True))
    return dx, jnp.sum(dy * xh, axis=0, keepdims=True)


def _layer_norm(r, g, b):
    mu = jnp.mean(r, axis=-1, keepdims=True)
    xc = r - mu
    var = jnp.mean(xc * xc, axis=-1, keepdims=True)
    return xc * lax.rsqrt(var + LN_EPS) * g + b


def _dot(a, b, dims):
    return lax.dot_general(a, b, (dims, ((), ())), preferred_element_type=F32)


_NN, _NT, _TN = ((1,), (0,)), ((1,), (1,)), ((0,), (0,))


def _in_proj(x_bf, w_in_p, cos, sin):
    n_rows = x_bf.shape[0]
    ts = 512
    dils = [dil for _, dil in PATTERNS]
    n_p = len(dils)

    def body(x_ref, w_ref, cos_ref, sin_ref, *refs):
        view_refs, (cq_ref, kpe_ref, ckv_ref, scratch) = refs[:n_p], refs[n_p:]
        xb = x_ref[...]
        cos_v, sin_v = cos_ref[...], sin_ref[...]

        def seg(lo, hi):
            return _dot(xb, w_ref[:, lo:hi], _NN)

        def put(part, val):
            by_dil = {1: [val]}
            for ref, dil in zip(view_refs, dils):
                if dil not in by_dil:
                    if dil > FAST_STRIDE and dil % FAST_STRIDE == 0:
                        if FAST_STRIDE not in by_dil:
                            by_dil[FAST_STRIDE] = _deinterleave(val, FAST_STRIDE, scratch)
                        inner = [_deinterleave(y, dil // FAST_STRIDE, scratch) for y in by_dil[FAST_STRIDE]]
                        by_dil[dil] = [inner[r % FAST_STRIDE][r // FAST_STRIDE] for r in range(dil)]
                    else:
                        by_dil[dil] = _deinterleave(val, dil, scratch)
                for r, piece in enumerate(by_dil[dil]):
                    lo = (3 * r + part) * A_WIDTH
                    ref[:, lo:lo + A_WIDTH] = piece.astype(BF16)

        put(0, _rope(seg(0, 512), cos_v, sin_v) * A_SCALE)
        put(1, _rope(seg(512, 1024), cos_v, sin_v))
        put(2, seg(1024, 1536))
        cq_ref[...] = seg(1536, 1792)
        kpe_ref[...] = _rope(seg(1792, 1920), cos_v[:, :LANES], sin_v[:, :LANES]).astype(BF16)
        ckv_ref[...] = seg(1920, 2048)

    row = lambda c, d=1: pl.BlockSpec((ts // d, c * d), lambda i: (i, 0))
    return pl.pallas_call(
        body, name="in_proj", grid=(n_rows // ts,),
        in_specs=[row(D_MODEL), pl.BlockSpec((D_MODEL, IN_PAD), lambda i: (0, 0)), row(A_WIDTH), row(A_WIDTH)],
        out_specs=[row(3 * A_WIDTH, d) for d in dils] + [row(Q_LORA), row(LANES), row(KV_LORA)],
        out_shape=[_sds((n_rows // d, d * 3 * A_WIDTH), BF16) for d in dils]
        + [_sds((n_rows, Q_LORA)), _sds((n_rows, LANES), BF16), _sds((n_rows, KV_LORA))],
        scratch_shapes=[pltpu.VMEM((A_WIDTH // LANES, ts, LANES), F32)],
        compiler_params=_params(("parallel",)),
    )(x_bf, w_in_p, cos, sin)


def _band_mask(m, heads):
    qi = lax.broadcasted_iota(jnp.int32, (heads * SPAN, 2 * SPAN), 0) % SPAN
    kj = lax.broadcasted_iota(jnp.int32, (heads * SPAN, 2 * SPAN), 1)
    return ((kj < SPAN) & (kj >= qi) & (m > 0)) | ((kj >= SPAN) & ((kj - SPAN) <= qi))


def _even_lanes(rows):
    return lax.broadcasted_iota(jnp.int32, (rows, LANES), 1) < A_HEAD_DIM


DIL_GROUP = 4
HEAD_SLOT = LANES // A_HEADS
O_WORDS = A_WIDTH // 2
OL_WIDTH = O_WORDS + LANES


def _pack_halves(t):
    return _pack2(t[:, :O_WORDS], t[:, O_WORDS:])


def _unpack_halves(p):
    return jnp.concatenate(_unpack2(p), axis=1)


DQKV_WIDTH = 3 * A_WIDTH // 2


def _pack2(a, b):
    hi = lax.bitcast_convert_type(a.astype(jnp.bfloat16).astype(F32), jnp.uint32) & jnp.uint32(0xFFFF0000)
    lo = lax.shift_right_logical(lax.bitcast_convert_type(b.astype(jnp.bfloat16).astype(F32), jnp.uint32), jnp.uint32(16))
    return lax.bitcast_convert_type(hi | lo, F32)


def _unpack2(p):
    u = lax.bitcast_convert_type(p, jnp.uint32)
    return (lax.bitcast_convert_type(u & jnp.uint32(0xFFFF0000), F32),
            lax.bitcast_convert_type(lax.shift_left(u, jnp.uint32(16)), F32))


def _slot_lane(h):
    return HEAD_SLOT * (h // 2) + A_HEAD_DIM * (h % 2)


def _slot_head(lane):
    return 2 * (jnp.right_shift(lane, 4) & 3) + jnp.right_shift(lane, 6)


def _dil_fwd(qkv_view, dil, comm=None):
    nb = qkv_view.shape[0] // SPAN
    group = min(dil, DIL_GROUP)
    grid = (dil // group, nb)
    c_ins, c_in_specs, c_outs, c_out_specs, begin, end = _hosted(comm, grid)
    n_ci, n_co = len(c_ins), len(c_outs)

    def body(cur_ref, prev_ref, *refs):
        ci_refs, ol_ref, co_refs, sems = refs[:n_ci], refs[n_ci], refs[n_ci + 1:n_ci + 1 + n_co], refs[n_ci + 1 + n_co:]
        begin(ci_refs, co_refs, sems)
        mask = _band_mask(pl.program_id(1), 1)
        even, even2 = _even_lanes(SPAN), _even_lanes(2 * SPAN)
        pair_of_lane = jnp.right_shift(lax.broadcasted_iota(jnp.int32, (SPAN, LANES), 1), 4) & 3
        for g in range(group):
            lse_c, o_pairs = jnp.zeros((SPAN, LANES), F32), []
            for p in range(A_HEADS // 2):
                q_sl, k_sl, v_sl = (slice((3 * g + t) * A_WIDTH + p * LANES, (3 * g + t) * A_WIDTH + (p + 1) * LANES)
                                    for t in range(3))
                q2 = cur_ref[:, q_sl]
                kcat = jnp.concatenate([prev_ref[:, k_sl], cur_ref[:, k_sl]], axis=0)
                vcat = jnp.concatenate([prev_ref[:, v_sl], cur_ref[:, v_sl]], axis=0)
                zero, one = jnp.zeros_like(q2), jnp.ones_like(vcat)
                res, lses = [], []
                for first in (True, False):
                    qh = jnp.where(even, q2, zero) if first else jnp.where(even, zero, q2)
                    vh = jnp.where(even2, vcat, one) if first else jnp.where(even2, one, vcat)
                    s = jnp.where(mask, _dot(qh, kcat, _NT), NEG)
                    mx = jnp.max(s, axis=-1, keepdims=True)
                    r = _dot(jnp.exp(s - mx).astype(BF16), vh, _NN)
                    den = pltpu.roll(r, A_HEAD_DIM, 1)
                    res.append(r / den)
                    lses.append(mx + jnp.log(den))
                o_pairs.append(jnp.where(even, res[0], res[1]))
                lse_c = jnp.where(pair_of_lane == p, jnp.where(even, lses[0], lses[1]), lse_c)
            for p in range(2):
                ol_ref[:, OL_WIDTH * g + p * LANES:OL_WIDTH * g + (p + 1) * LANES] = _pack2(o_pairs[p], o_pairs[p + 2])
            ol_ref[:, OL_WIDTH * g + O_WORDS:OL_WIDTH * (g + 1)] = lse_c
        end(ci_refs, co_refs, sems)

    return pl.pallas_call(
        body, name=f"dil_fwd_d{dil}", grid=grid,
        in_specs=[pl.BlockSpec((SPAN, group * 3 * A_WIDTH), lambda r, m: (m, r)),
                  pl.BlockSpec((SPAN, group * 3 * A_WIDTH), lambda r, m: (jnp.maximum(m - 1, 0), r))] + c_in_specs,
        out_specs=[pl.BlockSpec((SPAN, group * OL_WIDTH), lambda r, m: (m, r))] + c_out_specs,
        out_shape=[_sds((qkv_view.shape[0], dil * OL_WIDTH))] + c_outs,
        scratch_shapes=comm.sem_scratch() if comm else [],
        compiler_params=_params(("arbitrary", "arbitrary") if comm else ("parallel", "arbitrary")),
    )(qkv_view, qkv_view, *c_ins)


def _slot_spread_matrix():
    i = lax.broadcasted_iota(jnp.int32, (LANES, A_WIDTH), 0)
    h = lax.broadcasted_iota(jnp.int32, (LANES, A_WIDTH), 1) // A_HEAD_DIM
    return (i == HEAD_SLOT * (h // 2) + A_HEAD_DIM * (h % 2)).astype(BF16)


def _spread_slots(c, spread):
    hi = c.astype(BF16)
    r1 = c - hi.astype(F32)
    mid = r1.astype(BF16)
    lo = (r1 - mid.astype(F32)).astype(BF16)
    return _dot(hi, spread, _NN) + _dot(mid, spread, _NN) + _dot(lo, spread, _NN)


def _pattern_weights(ol1, ol2, ol3):
    l1, l2, l3 = ol1[:, O_WORDS:], ol2[:, O_WORDS:], ol3[:, O_WORDS:]
    mx = jnp.maximum(jnp.maximum(l1, l2), l3)
    e1, e2, e3 = jnp.exp(l1 - mx), jnp.exp(l2 - mx), jnp.exp(l3 - mx)
    inv = 1.0 / (e1 + e2 + e3)
    return e1 * inv, e2 * inv, e3 * inv


def _mix_fwd(ols, b_out, a_g, b_g):
    def fn(ol1, ol2, ol3, b, ag, bg, spread):
        ws = [_spread_slots(w, spread) for w in _pattern_weights(ol1, ol2, ol3)]
        a = sum(w * _unpack_halves(ol[:, :O_WORDS]) for w, ol in zip(ws, (ol1, ol2, ol3)))
        return a, jnp.concatenate([_rms(a, ag), _rms(b, bg)], axis=1)

    n_rows = b_out.shape[0]
    return _rowwise("mix_fwd", fn, [*ols, b_out], [_sds((n_rows, A_WIDTH)), _sds((n_rows, 2 * A_WIDTH), BF16)],
                    consts=[a_g, b_g, _slot_spread_matrix()], in_dils=[dil for _, dil in PATTERNS] + [1])


def _mla_q_prep(cq, g, w_uq_p, cos, sin):
    def fn(cq_v, cos_v, sin_v, g_v, w_v):
        cqn = _rms(cq_v, g_v).astype(BF16)
        q = _dot(cqn, w_v, _NN)
        qf = [jnp.concatenate([q[:, h * QK_NOPE:(h + 1) * QK_NOPE],
                               _rope(q[:, 512 + h * LANES:512 + (h + 1) * LANES], cos_v[:, :LANES], sin_v[:, :LANES])], axis=1)
              for h in range(MLA_HEADS)]
        return cqn, qf

    n_rows = cq.shape[0]
    return _rowwise("mla_q_prep", fn, [cq, cos, sin], [_sds((n_rows, Q_LORA), BF16), _sds((MLA_HEADS, n_rows, QK_PAD), BF16)],
                    consts=[g, w_uq_p])


def _mla_kv_prep(ckv, kpe, g, w_ukv_p):
    def fn(ckv_v, kpe_v, g_v, w_v):
        ckvn = _rms(ckv_v, g_v).astype(BF16)
        kvv = _dot(ckvn, w_v, _NN)
        kf = [jnp.concatenate([kvv[:, h * QK_NOPE:(h + 1) * QK_NOPE], kpe_v.astype(F32)], axis=1) for h in range(MLA_HEADS)]
        return ckvn, kf, kvv[:, 512:]

    n_rows = ckv.shape[0]
    return _rowwise("mla_kv_prep", fn, [ckv, kpe],
                    [_sds((n_rows, KV_LORA), BF16), _sds((MLA_HEADS, n_rows, QK_PAD), BF16), _sds((n_rows, MLA_WIDTH), BF16)],
                    consts=[g, w_ukv_p])


MLA_FWD_TILES = (1024, 2048)
MLA_BWD_TILES = (1024, 1024)


def _mla_fwd(qf, kf, v, comm=None):
    n_rows = v.shape[0]
    tq, tk = min(MLA_FWD_TILES[0], n_rows), min(MLA_FWD_TILES[1], n_rows)
    nq, nk = n_rows // tq, n_rows // tk
    grid = (MLA_HEADS, nq, nk)
    exp2_scale = MLA_SCALE * math.log2(math.e)
    c_ins, c_in_specs, c_outs, c_out_specs, begin, end = _hosted(comm, grid)
    n_ci, n_co = len(c_ins), len(c_outs)

    def last_k(i):
        return jnp.right_shift(i * tq + tq - 1, int(math.log2(tk)))

    def body(q_ref, k_ref, v_ref, *refs):
        ci_refs, (o_ref, lse_ref), co_refs = refs[:n_ci], refs[n_ci:n_ci + 2], refs[n_ci + 2:n_ci + 2 + n_co]
        m_sc, l_sc, acc_sc = refs[n_ci + 2 + n_co:n_ci + 5 + n_co]
        sems = refs[n_ci + 5 + n_co:]
        i, j = pl.program_id(1), pl.program_id(2)
        begin(ci_refs, co_refs, sems)

        @pl.when(j == 0)
        def _():
            m_sc[...] = jnp.full(m_sc.shape, NEG, F32)
            l_sc[...] = jnp.zeros(l_sc.shape, F32)
            acc_sc[...] = jnp.zeros(acc_sc.shape, F32)

        def step(masked):
            s = _dot(q_ref[...], k_ref[...], _NT)
            if masked:
                row = lax.broadcasted_iota(jnp.int32, (tq, tk), 0) + i * tq
                col = lax.broadcasted_iota(jnp.int32, (tq, tk), 1) + j * tk
                s = jnp.where(col <= row, s, NEG)
            m_prev = m_sc[...]
            m_new = jnp.maximum(m_prev, jnp.max(s, axis=-1, keepdims=True))
            alpha = jnp.exp2((m_prev - m_new) * exp2_scale)
            p = jnp.exp2((s - m_new) * exp2_scale)
            l_sc[...] = alpha * l_sc[...] + jnp.sum(p, axis=-1, keepdims=True)
            acc_sc[...] = alpha * acc_sc[...] + _dot(p.astype(BF16), v_ref[...], _NN)
            m_sc[...] = m_new

        active = j * tk <= i * tq + tq - 1
        crosses = (j + 1) * tk - 1 > i * tq

        @pl.when(active & jnp.logical_not(crosses))
        def _():
            step(False)

        @pl.when(active & crosses)
        def _():
            step(True)

        @pl.when(j == last_k(i))
        def _():
            o_ref[...] = acc_sc[...] / l_sc[...]
            lse_ref[...] = jnp.broadcast_to(m_sc[...] * MLA_SCALE + jnp.log(l_sc[...]), (tq, V_DIM))

        end(ci_refs, co_refs, sems)

    return pl.pallas_call(
        body, name="mla_fwd", grid=grid,
        in_specs=[pl.BlockSpec((None, tq, QK_PAD), lambda h, i, j: (h, i, 0)),
                  pl.BlockSpec((None, tk, QK_PAD), lambda h, i, j: (h, jnp.minimum(j, last_k(i)), 0)),
                  pl.BlockSpec((tk, V_DIM), lambda h, i, j: (jnp.minimum(j, last_k(i)), h))] + c_in_specs,
        out_specs=[pl.BlockSpec((tq, V_DIM), lambda h, i, j: (i, h)), pl.BlockSpec((tq, V_DIM), lambda h, i, j: (i, h))]
        + c_out_specs,
        out_shape=[_sds((n_rows, MLA_WIDTH)), _sds((n_rows, MLA_WIDTH))] + c_outs,
        scratch_shapes=[pltpu.VMEM((tq, 1), F32), pltpu.VMEM((tq, 1), F32), pltpu.VMEM((tq, V_DIM), F32)]
        + (comm.sem_scratch() if comm else []),
        compiler_params=_params(("arbitrary",) * 3 if comm else ("parallel", "parallel", "arbitrary")),
    )(qf, kf, v, *c_ins)


def _mm_res_ln(name, a, w, xres, g, b):
    def epi(acc, xr, g_v, b_v):
        r = ALPHA * xr + acc
        y = _layer_norm(r, g_v, b_v)
        return y, y, r

    return _mm(name, a, w, (F32, BF16, F32), tm=1024, tn=D_MODEL, tk=1024, epilogue=epi, extras=[xres], vecs=[g, b])


def _mm_relu2(x_bf, w, comm=None):
    def epi(acc):
        r = jnp.maximum(acc, 0.0)
        return (r * r,)

    return _mm("ff1", x_bf, w, (BF16,), tm=1024, tn=1024, tk=D_MODEL, epilogue=epi, b_pieces=True, comm=comm)


def _loss_fn(y, t):
    def fn(y_v, t_v):
        d = y_v - t_v
        part = jnp.sum(jnp.sum(d * d, axis=1, keepdims=True), axis=0, keepdims=True)
        return d * (1.0 / D_MODEL), part

    dy, part = _rowwise("loss", fn, [y, t], [_sds(y.shape)], reds=[_sds((1, 1))])
    return part * (0.5 / D_MODEL), dy


def _ln_bwd_rows(dy, r, g):
    mu = jnp.mean(r, axis=-1, keepdims=True)
    xc = r - mu
    rstd = lax.rsqrt(jnp.mean(xc * xc, axis=-1, keepdims=True) + LN_EPS)
    xh = xc * rstd
    dxh = dy * g
    dr = rstd * (dxh - jnp.mean(dxh, axis=-1, keepdims=True) - xh * jnp.mean(dxh * xh, axis=-1, keepdims=True))
    return dr, dr, jnp.sum(dy * xh, axis=0, keepdims=True), jnp.sum(dy, axis=0, keepdims=True)


def _ln_bwd(name, dy, r, g):
    return _rowwise(name, _ln_bwd_rows, [dy, r], [_sds(dy.shape), _sds(dy.shape, BF16)], consts=[g],
                    reds=[_sds((1, D_MODEL)), _sds((1, D_MODEL))])


def _mm_res_ln_bwd(name, a, w, res_grad, r, g, **mm_args):
    return _mm(name, a, w, (F32, BF16), tb=True, tm=512, tn=D_MODEL, tk=1024,
               epilogue=lambda acc, d, r_v, g_v: _ln_bwd_rows(acc + ALPHA * d, r_v, g_v),
               extras=[res_grad, r], vecs=[g], reds=[_sds((1, D_MODEL)), _sds((1, D_MODEL))], **mm_args)


def _head_sum_matrix():
    i = lax.broadcasted_iota(jnp.int32, (A_WIDTH, LANES), 0) // A_HEAD_DIM
    return (i == _slot_head(lax.broadcasted_iota(jnp.int32, (A_WIDTH, LANES), 1))).astype(BF16)


def _mix_bwd(dmixed, a_out, b_out, ols, a_g, b_g):
    def fn(dm, a, b, ol1, ol2, ol3, ag, bg, sum_mat, spread):
        da, dga = _rms_bwd(a, ag, dm[:, :A_WIDTH])
        db, dgb = _rms_bwd(b, bg, dm[:, A_WIDTH:])
        t = da * a
        t_hi = t.astype(BF16)
        t_lo = (t - t_hi.astype(F32)).astype(BF16)
        tsum = _dot(t_hi, sum_mat, _NN) + _dot(t_lo, sum_mat, _NN)
        tb = db * b
        delta_b = jnp.concatenate(
            [jnp.broadcast_to(jnp.sum(tb[:, h * V_DIM:(h + 1) * V_DIM], axis=-1, keepdims=True), (tb.shape[0], V_DIM))
             for h in range(MLA_HEADS)], axis=1)
        dods = [jnp.concatenate([_pack_halves(_spread_slots(w, spread) * da), w * tsum], axis=1)
                for w in _pattern_weights(ol1, ol2, ol3)]
        return (*dods, db, delta_b, dga, dgb)

    n_rows = a_out.shape[0]
    wide = (n_rows, A_WIDTH)
    dils = [dil for _, dil in PATTERNS]
    return _rowwise("mix_bwd", fn, [dmixed, a_out, b_out, *ols],
                    [_sds((n_rows // d, d * OL_WIDTH)) for d in dils] + [_sds(wide, BF16), _sds(wide)],
                    consts=[a_g, b_g, _head_sum_matrix(), _slot_spread_matrix()],
                    reds=[_sds((1, A_WIDTH)), _sds((1, MLA_WIDTH))], ts=256,
                    in_dils=[1, 1, 1] + dils, out_dils=dils + [1, 1])


def _dil_bwd(qkv_view, ol_view, dod_view, dil, comm=None):
    nb = qkv_view.shape[0] // SPAN
    group = min(dil, DIL_GROUP)
    grid = (dil // group, nb + 1)
    c_ins, c_in_specs, c_outs, c_out_specs, begin, end = _hosted(comm, grid)
    n_ci, n_co = len(c_ins), len(c_outs)

    def body(cur_ref, prev_ref, ol_ref, dod_ref, *refs):
        ci_refs, out_ref, co_refs = refs[:n_ci], refs[n_ci], refs[n_ci + 1:n_ci + 1 + n_co]
        carry, sems = refs[n_ci + 1 + n_co], refs[n_ci + 2 + n_co:]
        m = pl.program_id(1)
        begin(ci_refs, co_refs, sems)

        def write_packed(g, pairs):
            lo = DQKV_WIDTH * g
            for p, (dq_p, dk_p, _) in enumerate(pairs):
                out_ref[:, lo + p * LANES:lo + (p + 1) * LANES] = _pack2(dq_p, dk_p)
            for p in range(2):
                out_ref[:, lo + A_WIDTH + p * LANES:lo + A_WIDTH + (p + 1) * LANES] = _pack2(pairs[p][2], pairs[p + 2][2])

        @pl.when(m == 0)
        def _():
            carry[...] = jnp.zeros(carry.shape, F32)

        @pl.when(m < nb)
        def _():
            mask, even = _band_mask(m, 2), _even_lanes(SPAN)
            for g in range(group):
                done = []
                for p in range(A_HEADS // 2):
                    q_sl, k_sl, v_sl = (slice((3 * g + t) * A_WIDTH + p * LANES, (3 * g + t) * A_WIDTH + (p + 1) * LANES)
                                        for t in range(3))
                    do_sl = slice(OL_WIDTH * g + (p % 2) * LANES, OL_WIDTH * g + (p % 2 + 1) * LANES)
                    s_e, s_o = (OL_WIDTH * g + O_WORDS + _slot_lane(2 * p + t) for t in range(2))
                    q2, do2 = cur_ref[:, q_sl], _unpack2(dod_ref[:, do_sl])[p // 2].astype(BF16)
                    zero = jnp.zeros_like(q2)
                    qcat = jnp.concatenate([jnp.where(even, q2, zero), jnp.where(even, zero, q2)], axis=0)
                    docat = jnp.concatenate([jnp.where(even, do2, zero), jnp.where(even, zero, do2)], axis=0)
                    kcat = jnp.concatenate([prev_ref[:, k_sl], cur_ref[:, k_sl]], axis=0)
                    vcat = jnp.concatenate([prev_ref[:, v_sl], cur_ref[:, v_sl]], axis=0)
                    lse_c = jnp.concatenate([ol_ref[:, s_e:s_e + 1], ol_ref[:, s_o:s_o + 1]], axis=0)
                    dl_c = jnp.concatenate([dod_ref[:, s_e:s_e + 1], dod_ref[:, s_o:s_o + 1]], axis=0)
                    pr = jnp.exp(jnp.where(mask, _dot(qcat, kcat, _NT), NEG) - lse_c)
                    ds = (pr * (_dot(docat, vcat, _NT) - dl_c)).astype(BF16)
                    dq = _dot(ds, kcat, _NN)
                    dk2 = _dot(ds, qcat, _TN)
                    dv2 = _dot(pr.astype(BF16), docat, _TN)
                    done.append((carry[:, q_sl], carry[:, k_sl] + dk2[:SPAN], carry[:, v_sl] + dv2[:SPAN]))
                    carry[:, q_sl] = jnp.where(even, dq[:SPAN], dq[SPAN:])
                    carry[:, k_sl] = dk2[SPAN:]
                    carry[:, v_sl] = dv2[SPAN:]
                write_packed(g, done)

        @pl.when(m == nb)
        def _():
            for g in range(group):
                write_packed(g, [tuple(carry[:, (3 * g + t) * A_WIDTH + p * LANES:(3 * g + t) * A_WIDTH + (p + 1) * LANES]
                                       for t in range(3)) for p in range(A_HEADS // 2)])

        end(ci_refs, co_refs, sems)

    last = nb - 1

    def cur(width):
        return pl.BlockSpec((SPAN, group * width), lambda r, m: (jnp.minimum(m, last), r))

    def prev(width):
        return pl.BlockSpec((SPAN, group * width), lambda r, m: (jnp.clip(m - 1, 0, last), r))

    return pl.pallas_call(
        body, name=f"dil_bwd_d{dil}", grid=grid,
        in_specs=[cur(3 * A_WIDTH), prev(3 * A_WIDTH), cur(OL_WIDTH), cur(OL_WIDTH)] + c_in_specs,
        out_specs=[prev(DQKV_WIDTH)] + c_out_specs,
        out_shape=[_sds((qkv_view.shape[0], dil * DQKV_WIDTH))] + c_outs,
        scratch_shapes=[pltpu.VMEM((SPAN, group * 3 * A_WIDTH), F32)] + (comm.sem_scratch() if comm else []),
        compiler_params=_params(("arbitrary", "arbitrary") if comm else ("parallel", "arbitrary")),
    )(qkv_view, qkv_view, ol_view, dod_view, *c_ins)


def _mla_bwd(qf, kf, v, lse, do, delta, comm=None):
    n_rows = v.shape[0]
    tq, tk = min(MLA_BWD_TILES[0], n_rows), min(MLA_BWD_TILES[1], n_rows)
    nq, nk = n_rows // tq, n_rows // tk

    def first_q(j):
        return jnp.right_shift(j * tk, int(math.log2(tq)))

    grid = (MLA_HEADS, nk, nq)
    c_ins, c_in_specs, c_outs, c_out_specs, begin, end = _hosted(comm, grid)
    n_ci, n_co = len(c_ins), len(c_outs)

    def body(q_ref, k_ref, v_ref, lse_ref, do_ref, dl_ref, *refs):
        ci_refs, (dq_ref, dk_ref, dv_ref), co_refs = refs[:n_ci], refs[n_ci:n_ci + 3], refs[n_ci + 3:n_ci + 3 + n_co]
        dk_sc, dv_sc = refs[n_ci + 3 + n_co:n_ci + 5 + n_co]
        sems = refs[n_ci + 5 + n_co:]
        j, i = pl.program_id(1), pl.program_id(2)
        begin(ci_refs, co_refs, sems)

        @pl.when((j == 0) & (i == 0))
        def _():
            dq_ref[...] = jnp.zeros(dq_ref.shape, F32)

        @pl.when(i == first_q(j))
        def _():
            dk_sc[...] = jnp.zeros(dk_sc.shape, F32)
            dv_sc[...] = jnp.zeros(dv_sc.shape, F32)

        def step(masked):
            q, k, dob = q_ref[...], k_ref[...], do_ref[...]
            s = _dot(q, k, _NT) * MLA_SCALE
            if masked:
                row = lax.broadcasted_iota(jnp.int32, (tq, tk), 0) + i * tq
                col = lax.broadcasted_iota(jnp.int32, (tq, tk), 1) + j * tk
                s = jnp.where(col <= row, s, NEG)
            p = jnp.exp(s - lse_ref[:, :1])
            dp = _dot(dob, v_ref[...], _NT)
            ds = (p * (dp - dl_ref[:, :1]) * MLA_SCALE).astype(BF16)
            dv_sc[...] += _dot(p.astype(BF16), dob, _TN)
            dk_sc[...] += _dot(ds, q, _TN)
            rows = pl.ds(pl.multiple_of(i * tq, tq), tq)
            dq_ref[rows, :] += _dot(ds, k, _NN)

        active = i >= first_q(j)
        crosses = (j + 1) * tk - 1 > i * tq

        @pl.when(active & jnp.logical_not(crosses))
        def _():
            step(False)

        @pl.when(active & crosses)
        def _():
            step(True)

        @pl.when(i == nq - 1)
        def _():
            dk_ref[...] = dk_sc[...]
            dv_ref[...] = dv_sc[...]

        end(ci_refs, co_refs, sems)

    qrow = lambda h, j, i: (jnp.maximum(i, first_q(j)), h)
    return pl.pallas_call(
        body, name="mla_bwd", grid=grid,
        in_specs=[pl.BlockSpec((None, tq, QK_PAD), lambda h, j, i: (h, jnp.maximum(i, first_q(j)), 0)),
                  pl.BlockSpec((None, tk, QK_PAD), lambda h, j, i: (h, j, 0)),
                  pl.BlockSpec((tk, V_DIM), lambda h, j, i: (j, h)),
                  pl.BlockSpec((tq, V_DIM), qrow), pl.BlockSpec((tq, V_DIM), qrow), pl.BlockSpec((tq, V_DIM), qrow)]
        + c_in_specs,
        out_specs=[pl.BlockSpec((None, n_rows, QK_PAD), lambda h, j, i: (h, 0, 0)),
                   pl.BlockSpec((None, tk, QK_PAD), lambda h, j, i: (h, j, 0)),
                   pl.BlockSpec((tk, V_DIM), lambda h, j, i: (j, h))] + c_out_specs,
        out_shape=[_sds((MLA_HEADS, n_rows, QK_PAD)), _sds((MLA_HEADS, n_rows, QK_PAD)), _sds((n_rows, MLA_WIDTH))] + c_outs,
        scratch_shapes=[pltpu.VMEM((tk, QK_PAD), F32), pltpu.VMEM((tk, V_DIM), F32)] + (comm.sem_scratch() if comm else []),
        compiler_params=_params(("arbitrary",) * 3 if comm else ("parallel", "arbitrary", "arbitrary")),
    )(qf, kf, v, lse, do, delta, *c_ins)


def _mla_bwd_prep(dqf, dkf, dv, cos, sin):
    def fn(dqf_v, dkf_v, dv_v, cos_v, sin_v):
        dq = jnp.concatenate([dqf_v[h][:, :QK_NOPE] for h in range(MLA_HEADS)]
                             + [_rope_t(dqf_v[h][:, QK_NOPE:], cos_v[:, :LANES], sin_v[:, :LANES]) for h in range(MLA_HEADS)], axis=1)
        dkv = jnp.concatenate([dkf_v[h][:, :QK_NOPE] for h in range(MLA_HEADS)] + [dv_v], axis=1)
        dkpe = dkf_v[0][:, QK_NOPE:] + dkf_v[1][:, QK_NOPE:] + dkf_v[2][:, QK_NOPE:] + dkf_v[3][:, QK_NOPE:]
        return dq, dkv, dkpe

    n_rows = dv.shape[0]
    return _rowwise("mla_bwd_prep", fn, [dqf, dkf, dv, cos, sin],
                    [_sds((n_rows, UQ_PAD), BF16), _sds((n_rows, 1024), BF16), _sds((n_rows, LANES))], ts=256)


def _assemble_dh(dqkvs, dcqn, cq, dckvn, ckv, dkpe, cos, sin, gq, gkv, comm=None):
    def fn(g1, g2, g3, dcqn_v, cq_v, dckvn_v, ckv_v, dkpe_v, cos_v, sin_v, gq_v, gkv_v):
        dq = dk = dv = 0.0
        for packed in (g1, g2, g3):
            dq_p, dk_p = _unpack2(packed[:, :A_WIDTH])
            dv_p = jnp.concatenate(_unpack2(packed[:, A_WIDTH:]), axis=1)
            dq, dk, dv = dq + dq_p, dk + dk_p, dv + dv_p
        dqa = _rope_t(dq, cos_v, sin_v) * A_SCALE
        dka = _rope_t(dk, cos_v, sin_v)
        dcq, dgq = _rms_bwd(cq_v, gq_v, dcqn_v)
        dckv, dgkv = _rms_bwd(ckv_v, gkv_v, dckvn_v)
        dkr = _rope_t(dkpe_v, cos_v[:, :LANES], sin_v[:, :LANES])
        return jnp.concatenate([dqa, dka, dv, dcq, dkr, dckv], axis=1), dgq, dgkv

    n_rows = cq.shape[0]
    return _rowwise("assemble_dh", fn, [*dqkvs, dcqn, cq, dckvn, ckv, dkpe, cos, sin],
                    [_sds((n_rows, IN_PAD), BF16)], consts=[gq, gkv],
                    reds=[_sds((1, Q_LORA)), _sds((1, KV_LORA))], ts=256,
                    in_dils=[dil for _, dil in PATTERNS] + [1] * 7, comm=comm)


def _layer_fwd(x_f32, x_bf, w, sm, cos, sin, behind=(None, None), rest=None):
    *qkv_views, cq, kpe, ckv = _in_proj(x_bf, w["w_in"], cos, sin)
    ol_views, brought = [], []
    for i, (view, (_, dil)) in enumerate(zip(qkv_views, PATTERNS)):
        ol, *got = _dil_fwd(view, dil, comm=rest[0][i] if rest else None)
        ol_views.append(ol)
        brought.append(got)
    if rest:
        w = {**w, **rest[1](brought)}
    cqn, qf = _mla_q_prep(cq, sm["q_a_norm"], w["w_uq"], cos, sin)
    ckvn, kf, v = _mla_kv_prep(ckv, kpe, sm["kv_a_norm"], w["w_ukv"])
    b_out, b_lse, *behind_attn = _mla_fwd(qf, kf, v, comm=behind[0])
    a_out, mixed = _mix_fwd(ol_views, b_out, sm["a_out_norm"], sm["b_out_norm"])
    x1, x1_bf, r1 = _mm_res_ln("wo_ln1", mixed, w["w_o"], x_f32, sm["ln1_g"], sm["ln1_b"])
    f, *behind_ff1 = _mm_relu2(x1_bf, w["w_ff1"], comm=behind[1])
    x2, x2_bf, r2 = _mm_res_ln("ff2_ln2", f, w["w_ff2"], x1, sm["ln2_g"], sm["ln2_b"])
    saved = dict(w=w, x_bf=x_bf, qkv_views=qkv_views, ol_views=ol_views, cq=cq, ckv=ckv, cqn=cqn, qf=qf, ckvn=ckvn,
                 kf=kf, v=v, b_out=b_out, b_lse=b_lse, a_out=a_out, mixed=mixed, x1_bf=x1_bf, r1=r1, f=f, r2=r2)
    return x2, x2_bf, saved, (behind_attn, behind_ff1)


def _layer_bwd(top, w, sm, sv, cos, sin, reduce_of, pending=None, below=None):
    dr2, dr2_bf, dg2, db2 = top
    (dw_ff2,) = _mm("dw_ff2", sv["f"], dr2_bf, (F32,), ta=True, tm=1024, tn=1024, tk=2048)
    du, *swapped = _mm("d_u", dr2_bf, w["w_ff2"], (BF16,), tb=True, tm=1024, tn=2048, tk=D_MODEL,
                       epilogue=lambda acc, f: (acc * (2.0 * jnp.sqrt(f.astype(F32))),), extras=[sv["f"]],
                       comm=_swap_comm(pending.pieces) if pending else None)
    if pending:
        pending.after_swap(swapped)
    (dw_ff1,) = _mm("dw_ff1", sv["x1_bf"], du, (F32,), ta=True, tm=1024, tn=1024, tk=2048, out_pieces=True)
    dr1, dr1_bf, dg1, db1 = _mm_res_ln_bwd("d_x1", du, w["w_ff1"], dr2, sv["r1"], sm["ln1_g"], b_pieces=True)
    (dw_o,) = _mm("dw_o", sv["mixed"], dr1_bf, (F32,), ta=True, tm=1024, tn=1024, tk=2048)
    early = reduce_of(dict(w_o=dw_o, w_ff1=dw_ff1, w_ff2=dw_ff2))
    (dmixed,) = _mm("d_mixed", dr1_bf, w["w_o"], (F32,), tb=True, tm=1024, tn=1024, tk=D_MODEL)
    dod1, dod2, dod3, do_b, delta_b, dga, dgb = _mix_bwd(
        dmixed, sv["a_out"], sv["b_out"], sv["ol_views"], sm["a_out_norm"], sm["b_out_norm"])
    dqkvs = []
    for i, ((_, dil), qkv_view, ol_view, dod_view) in enumerate(zip(PATTERNS, sv["qkv_views"], sv["ol_views"], (dod1, dod2, dod3))):
        swap_here = early is not None and i == 0
        dqkv, *swapped = _dil_bwd(qkv_view, ol_view, dod_view, dil, comm=_swap_comm(early.pieces) if swap_here else None)
        dqkvs.append(dqkv)
        if swap_here:
            early.after_swap(swapped)
    to_scatter = (early.s16 if early else []) + (pending.s16 if pending else [])
    dqf, dkf, dv_b, *scattered = _mla_bwd(sv["qf"], sv["kf"], sv["v"], sv["b_lse"], do_b, delta_b,
                                          comm=_scatter_comm(to_scatter) if to_scatter else None)
    if early:
        early.after_scatter(scattered[:len(early.s16)])
    if pending:
        pending.after_scatter(scattered[len(scattered) - len(pending.s16):])
    dq_b, dkv_b, dkpe = _mla_bwd_prep(dqf, dkf, dv_b, cos, sin)
    (dw_uq,) = _mm("dw_uq", sv["cqn"], dq_b, (F32,), ta=True, tm=Q_LORA, tn=1024, tk=2048)
    (dcqn,) = _mm("d_cqn", dq_b, w["w_uq"], (F32,), tb=True, tm=1024, tn=Q_LORA, tk=UQ_PAD)
    (dw_ukv,) = _mm("dw_ukv", sv["ckvn"], dkv_b, (F32,), ta=True, tm=KV_LORA, tn=1024, tk=2048)
    (dckvn,) = _mm("d_ckvn", dkv_b, w["w_ukv"], (F32,), tb=True, tm=1024, tn=KV_LORA, tk=1024)
    to_share = (early.mine if early else []) + (pending.mine if pending else [])
    dh, dgq, dgkv, *shared = _assemble_dh(dqkvs, dcqn, sv["cq"], dckvn, sv["ckv"], dkpe, cos, sin,
                                          sm["q_a_norm"], sm["kv_a_norm"], comm=_share_comm(to_share) if to_share else None)
    if early:
        early.after_share(shared[:len(early.mine)])
    if pending:
        pending.after_share(shared[len(shared) - len(pending.mine):])
    (dw_in,) = _mm("dw_in", sv["x_bf"], dh, (F32,), ta=True, tm=1024, tn=1024, tk=2048)
    if below is None:
        down = _mm("d_x", dh, w["w_in"], (F32,), tb=True, tm=1024, tn=1024, tk=2048,
                   epilogue=lambda acc, d: (acc + ALPHA * d,), extras=[dr1])
    else:
        down = _mm_res_ln_bwd("d_x", dh, w["w_in"], dr1, *below)
    late = reduce_of(dict(w_in=dw_in, w_uq=dw_uq, w_ukv=dw_ukv))
    dsm = dict(q_a_norm=dgq, kv_a_norm=dgkv, a_out_norm=dga, b_out_norm=dgb, ln1_g=dg1, ln1_b=db1, ln2_g=dg2, ln2_b=db2)
    return down, dsm, late


def _pad_w_in(w):
    return jnp.concatenate([w[:, :1792], w[:, 1920:1984], jnp.zeros((w.shape[0], 64), w.dtype), w[:, 1792:1920]], axis=1)


def _unpad_w_in(w):
    return jnp.concatenate([w[:, :1792], w[:, 1920:2048], w[:, 1792:1856]], axis=1)


def _rope_tables(n_rows):
    half = A_HEAD_DIM // 2
    inv_freq = ROPE_THETA ** (-jnp.arange(half, dtype=F32) / half)
    ang = jnp.arange(n_rows, dtype=F32)[:, None] * inv_freq[None, :]
    cos = jnp.tile(jnp.cos(ang), (1, 2 * A_HEADS))
    sin = jnp.tile(jnp.concatenate([-jnp.sin(ang), jnp.sin(ang)], axis=1), (1, A_HEADS))
    return cos, sin


W_NAMES = ("w_in", "w_uq", "w_ukv", "w_o", "w_ff1", "w_ff2")
PIECE = dict(w_in=(512, 512), w_uq=(128, 256), w_ukv=(64, 256), w_o=(128, 1024), w_ff1=(512, 1024), w_ff2=(512, 1024))
SHARD_COLS = dict(w_in=496, w_uq=192, w_ukv=256, w_o=1024, w_ff1=1024, w_ff2=1024)


def _col_pieces(t, pad_to):
    rows, _, cols = t.shape
    t = jnp.pad(t, ((0, 0), (0, 0), (0, pad_to - cols)))
    return t.reshape(2, rows // 2, N_CHIPS, pad_to).transpose(2, 0, 1, 3)


def _grad_pieces(dws):
    def heads(t, rows):
        return jnp.concatenate([t[:, :512].reshape(rows, MLA_HEADS, LANES), t[:, 512:].reshape(rows, MLA_HEADS, LANES)], axis=2)

    make = dict(
        w_in=lambda t: _col_pieces(_unpad_w_in(t).reshape(D_MODEL, N_CHIPS, SHARD_COLS["w_in"]), 512),
        w_uq=lambda t: _col_pieces(heads(t, Q_LORA), 256),
        w_ukv=lambda t: _col_pieces(heads(t, KV_LORA), 256),
        w_o=lambda t: t.reshape(N_CHIPS, 2, 128, D_MODEL),
        w_ff1=lambda t: t,
        w_ff2=lambda t: t.reshape(N_CHIPS, 2, 512, D_MODEL))
    return {n: make[n](t) for n, t in dws.items()}


def _weights_from_pieces(p):
    def cols(t):
        return t.transpose(1, 2, 0, 3).reshape(2 * t.shape[2], N_CHIPS, t.shape[3])

    def heads(t, rows):
        return jnp.concatenate([t[:, :, :LANES].reshape(rows, 512), t[:, :, LANES:].reshape(rows, 512)], axis=1)

    make = dict(
        w_in=lambda t: _pad_w_in(cols(t)[:, :, :SHARD_COLS["w_in"]].reshape(D_MODEL, IN_COLS)),
        w_uq=lambda t: heads(cols(t), Q_LORA),
        w_ukv=lambda t: heads(cols(t), KV_LORA),
        w_o=lambda t: t.reshape(D_MODEL, D_MODEL),
        w_ff1=lambda t: t,
        w_ff2=lambda t: t.reshape(D_FF, D_MODEL))
    return {n: make[n](t) for n, t in p.items()}


def _my_piece(name, shard, half):
    rows, cols = shard.shape[0] // 2, shard.shape[1]
    t = lax.dynamic_slice_in_dim(shard, half * rows, rows, axis=0).astype(BF16)
    return jnp.pad(t, ((0, 0), (0, PIECE[name][1] - cols)))


MESH = pl.DeviceIdType.MESH
ANY = pl.BlockSpec(memory_space=pl.ANY)


class _Comm:
    def __init__(self, ins, out_shapes, n_sems, start, finish):
        self.ins, self.out_shapes, self.n_sems, self.start, self.finish = list(ins), list(out_shapes), n_sems, start, finish

    def sem_scratch(self):
        return [pltpu.SemaphoreType.DMA((self.n_sems,)), pltpu.SemaphoreType.DMA((self.n_sems,))]


def _comm_call(name, comm):
    n_in, n_out = len(comm.ins), len(comm.out_shapes)

    def body(*refs):
        ins, outs, sems = refs[:n_in], refs[n_in:n_in + n_out], refs[n_in + n_out:]
        comm.start(ins, outs, *sems)
        comm.finish(ins, outs, *sems)

    return pl.pallas_call(body, name=name, out_shape=comm.out_shapes, in_specs=[ANY] * n_in, out_specs=[ANY] * n_out,
                          scratch_shapes=comm.sem_scratch())(*comm.ins)


def _all_gather_comm(blocks):
    n = len(blocks)

    def plan(x_refs, out_refs, send_sems, recv_sems):
        x, y, c = lax.axis_index("x"), lax.axis_index("y"), lax.axis_index("c")
        me, sibling = (x, y, c), (x, y, 1 - c)
        chips = [(1 - x, y), (x, 1 - y), (1 - x, 1 - y)]

        def copy(t, k, blk, to, src=None):
            px, py, pc = blk
            slot = out_refs[t].at[4 * px + 2 * py + pc]
            return pltpu.make_async_remote_copy(
                src_ref=slot if src is None else src, dst_ref=slot,
                send_sem=send_sems.at[7 * t + k], recv_sem=recv_sems.at[7 * t + k], device_id=to, device_id_type=MESH)

        first = []
        for t in range(n):
            first.append(copy(t, 0, me, sibling, src=x_refs[t]))
            first += [copy(t, 1 + j, me, (*chip, c), src=x_refs[t]) for j, chip in enumerate(chips)]
        return me, sibling, chips, c, copy, first

    def start(x_refs, out_refs, send_sems, recv_sems):
        for cp in plan(x_refs, out_refs, send_sems, recv_sems)[-1]:
            cp.start()

    def finish(x_refs, out_refs, send_sems, recv_sems):
        me, sibling, chips, c, copy, sent = plan(x_refs, out_refs, send_sems, recv_sems)
        for j, chip in enumerate(chips):
            for t in range(n):
                copy(t, 1 + j, (*chip, c), me).wait_recv()
                sent.append(copy(t, 4 + j, (*chip, c), sibling))
                sent[-1].start()
        for t in range(n):
            copy(t, 0, sibling, me).wait_recv()
        for j, chip in enumerate(chips):
            for t in range(n):
                copy(t, 4 + j, (*chip, 1 - c), me).wait_recv()
        for cp in sent:
            cp.wait_send()

    return _Comm(blocks, [_sds((N_DEV,) + b.shape, b.dtype) for b in blocks], 7 * n, start, finish)


def _simple_comm(ins, out_shapes, n_sems, copies):
    def start(in_refs, out_refs, send_sems, recv_sems):
        for cp in copies(in_refs, out_refs, send_sems, recv_sems):
            cp.start()

    def finish(in_refs, out_refs, send_sems, recv_sems):
        for cp in copies(in_refs, out_refs, send_sems, recv_sems):
            cp.wait()

    return _Comm(ins, out_shapes, n_sems, start, finish)


def _swap_comm(gs):
    def copies(g_refs, got_refs, send_sems, recv_sems):
        c = lax.axis_index("c")
        sibling = (lax.axis_index("x"), lax.axis_index("y"), 1 - c)
        return [pltpu.make_async_remote_copy(src_ref=g_refs[t].at[k, 1 - c], dst_ref=got_refs[t].at[k],
                                             send_sem=send_sems.at[N_CHIPS * t + k], recv_sem=recv_sems.at[N_CHIPS * t + k],
                                             device_id=sibling, device_id_type=MESH)
                for t in range(len(gs)) for k in range(N_CHIPS)]

    return _simple_comm(gs, [_sds((N_CHIPS,) + g.shape[2:], g.dtype) for g in gs], N_CHIPS * len(gs), copies)


def _scatter_comm(hs):
    def copies(h_refs, got_refs, send_sems, recv_sems):
        x, y, c = lax.axis_index("x"), lax.axis_index("y"), lax.axis_index("c")
        chips = [(1 - x, y), (x, 1 - y), (1 - x, 1 - y)]
        return [pltpu.make_async_remote_copy(src_ref=h_refs[t].at[2 * px + py], dst_ref=got_refs[t].at[rel],
                                             send_sem=send_sems.at[3 * t + rel], recv_sem=recv_sems.at[3 * t + rel],
                                             device_id=(px, py, c), device_id_type=MESH)
                for t in range(len(hs)) for rel, (px, py) in enumerate(chips)]

    return _simple_comm(hs, [_sds((3,) + h.shape[1:], h.dtype) for h in hs], 3 * len(hs), copies)


def _share_comm(rs):
    def copies(r_refs, got_refs, send_sems, recv_sems):
        sibling = (lax.axis_index("x"), lax.axis_index("y"), 1 - lax.axis_index("c"))
        return [pltpu.make_async_remote_copy(src_ref=r_refs[t], dst_ref=got_refs[t], send_sem=send_sems.at[t],
                                             recv_sem=recv_sems.at[t], device_id=sibling, device_id_type=MESH)
                for t in range(len(rs))]

    return _simple_comm(rs, [_sds(r.shape, r.dtype) for r in rs], len(rs), copies)


SUM_STEPS = 4


def _pair_sums(name, gs, gots, c_arr):
    n = len(gs)
    dims = [(g.shape[2] // SUM_STEPS, g.shape[3]) for g in gs]

    def body(c_ref, *refs):
        for t in range(n):
            s = refs[t][...] + refs[n + t][...]
            refs[2 * n + t][...] = s
            refs[3 * n + t][...] = s.astype(BF16)

    def part(tr, pc):
        return pl.BlockSpec((None, tr, pc), lambda k, i, c_ref: (k, i, 0))

    def kept(tr, pc):
        return pl.BlockSpec((None, None, tr, pc), lambda k, i, c_ref: (k, c_ref[0], i, 0))

    outs = pl.pallas_call(
        body, name=name,
        out_shape=[_sds((N_CHIPS,) + g.shape[2:]) for g in gs] + [_sds((N_CHIPS,) + g.shape[2:], BF16) for g in gs],
        grid_spec=pltpu.PrefetchScalarGridSpec(
            num_scalar_prefetch=1, grid=(N_CHIPS, SUM_STEPS),
            in_specs=[kept(*d) for d in dims] + [part(*d) for d in dims],
            out_specs=[part(*d) for d in dims] * 2),
        compiler_params=_params(("parallel", "parallel")),
    )(c_arr, *gs, *gots)
    return list(outs[:n]), list(outs[n:])


def _chips_sums(name, h32s, gots, chip_arr):
    n = len(h32s)
    dims = [(h.shape[1] // SUM_STEPS, h.shape[2]) for h in h32s]

    def body(chip_ref, *refs):
        for t in range(n):
            got_ref = refs[n + t]
            refs[2 * n + t][...] = refs[t][...] + got_ref[0] + got_ref[1] + got_ref[2]

    return pl.pallas_call(
        body, name=name, out_shape=[_sds(h.shape[1:]) for h in h32s],
        grid_spec=pltpu.PrefetchScalarGridSpec(
            num_scalar_prefetch=1, grid=(SUM_STEPS,),
            in_specs=[pl.BlockSpec((None, tr, pc), lambda i, chip_ref: (chip_ref[0], i, 0)) for tr, pc in dims]
            + [pl.BlockSpec((3, tr, pc), lambda i, chip_ref: (0, i, 0)) for tr, pc in dims],
            out_specs=[pl.BlockSpec((tr, pc), lambda i, chip_ref: (i, 0)) for tr, pc in dims]),
        compiler_params=_params(("parallel",)),
    )(chip_arr, *h32s, *gots)


class _Reduce:
    def __init__(self, tag, dws, c, chip):
        gp = _grad_pieces(dws)
        self.tag, self.c, self.names, self.pieces = tag, c, list(gp), list(gp.values())
        self.c_arr, self.chip_arr = jnp.reshape(c, (1,)).astype(jnp.int32), jnp.reshape(chip, (1,)).astype(jnp.int32)
        self.shards = None

    def after_swap(self, gots):
        self.s32, self.s16 = _pair_sums(f"rs_pair_sum_{self.tag}", self.pieces, gots, self.c_arr)

    def after_scatter(self, parts):
        self.mine = list(_chips_sums(f"rs_chips_sum_{self.tag}", self.s32, parts, self.chip_arr))

    def after_share(self, theirs):
        self.shards = {}
        for n, a, b in zip(self.names, self.mine, theirs):
            both = jnp.where(self.c == 0, jnp.concatenate([a, b], axis=0), jnp.concatenate([b, a], axis=0))
            self.shards[n] = both[:, :SHARD_COLS[n]]

    def run_alone(self):
        self.after_swap(_comm_call(f"rs_pair_{self.tag}", _swap_comm(self.pieces)))
        self.after_scatter(_comm_call(f"rs_chips_{self.tag}", _scatter_comm(self.s16)))
        self.after_share(_comm_call(f"rs_share_{self.tag}", _share_comm(self.mine)))


class _LayerWeights:
    def __init__(self, shards, c, dev):
        self.shards, self.c, self.dev = shards, c, dev

    def comm(self, l, names=W_NAMES):
        return _all_gather_comm([_my_piece(n, self.shards[l][n], self.c) for n in names])

    def take(self, l, gathered, names=W_NAMES):
        pieces = {}
        for n, g in zip(names, gathered):
            mine = _my_piece(n, self.shards[l][n], self.c)
            pieces[n] = lax.dynamic_update_index_in_dim(g, mine, self.dev, 0).reshape((N_CHIPS, 2) + mine.shape)
        return _weights_from_pieces(pieces)

    def first(self):
        now, later = ("w_in", "w_uq", "w_ukv"), ("w_ff1", "w_ff2", "w_o")
        weights = self.take(0, _comm_call("ag_l0", self.comm(0, now)), now)
        comms = [self.comm(0, (n,)) for n in later]

        def take_rest(results):
            rest = {}
            for n, gathered in zip(later, results):
                rest.update(self.take(0, gathered, (n,)))
            return rest

        return weights, (comms, take_rest)


def _all_reduce_small(vec):
    rows, lanes = vec.shape

    def gather_body(x_ref, out_ref, send_sems, recv_sems):
        x, y, c = lax.axis_index("x"), lax.axis_index("y"), lax.axis_index("c")
        me, sibling = (x, y, c), (x, y, 1 - c)
        chips = [(1 - x, y), (x, 1 - y), (1 - x, 1 - y)]

        def slot(px, py, pc):
            return out_ref.at[4 * px + 2 * py + pc]

        def copy(k, blk, to, src=None):
            return pltpu.make_async_remote_copy(
                src_ref=slot(*blk) if src is None else src, dst_ref=slot(*blk),
                send_sem=send_sems.at[k], recv_sem=recv_sems.at[k], device_id=to, device_id_type=MESH)

        out_ref[4 * x + 2 * y + c] = x_ref[...]
        first = [copy(0, me, sibling, src=x_ref)]
        first += [copy(1 + j, me, (*chip, c), src=x_ref) for j, chip in enumerate(chips)]
        for cp in first:
            cp.start()
        passed = [copy(4 + j, (*chip, c), sibling) for j, chip in enumerate(chips)]
        for j, chip in enumerate(chips):
            copy(1 + j, (*chip, c), me).wait_recv()
            passed[j].start()
        copy(0, sibling, me).wait_recv()
        for j, chip in enumerate(chips):
            copy(4 + j, (*chip, 1 - c), me).wait_recv()
        for cp in first + passed:
            cp.wait_send()

    vmem = pl.BlockSpec(memory_space=pltpu.VMEM)
    allv = pl.pallas_call(
        gather_body, name="small_all_gather", out_shape=_sds((N_DEV, rows, lanes)),
        in_specs=[vmem], out_specs=vmem,
        scratch_shapes=[pltpu.SemaphoreType.DMA((7,)), pltpu.SemaphoreType.DMA((7,))],
    )(vec)

    def sum_body(a_ref, o_ref):
        acc = a_ref[0]
        for d in range(1, N_DEV):
            acc = acc + a_ref[d]
        o_ref[...] = acc

    return pl.pallas_call(sum_body, name="small_sum", out_shape=_sds((rows, lanes)), in_specs=[vmem], out_specs=vmem)(allv)


def _adamw(name, w, g, m, v, ts=512, comm=None):
    def fn(w_v, g_v, m_v, v_v):
        m_n = ADAM_B1 * m_v + (1.0 - ADAM_B1) * g_v
        v_n = ADAM_B2 * v_v + (1.0 - ADAM_B2) * (g_v * g_v)
        m_hat = m_n / (1.0 - ADAM_B1 ** ADAM_STEP)
        v_hat = v_n / (1.0 - ADAM_B2 ** ADAM_STEP)
        delta = -ADAM_LR * (m_hat / (jnp.sqrt(v_hat) + ADAM_EPS) + ADAM_WD * w_v)
        return delta, m_n, v_n

    return _rowwise(name, fn, [w, g, m, v], [_sds(w.shape)] * 3, ts=ts, comm=comm)


def _pack_small(per_layer):
    flat = jnp.concatenate([per_layer[l][n].reshape(-1) for l in range(DEPTH) for n in SMALL_NAMES])
    return jnp.pad(flat, (0, SMALL_ROWS * LANES - flat.shape[0])).reshape(SMALL_ROWS, LANES)


def _unpack_small(packed):
    flat = packed.reshape(-1)
    per = sum(SMALL_SIZES)
    out = {}
    for n, size, off in zip(SMALL_NAMES, SMALL_SIZES, [sum(SMALL_SIZES[:i]) for i in range(len(SMALL_SIZES))]):
        out[n] = jnp.stack([flat[l * per + off:l * per + off + size] for l in range(DEPTH)])
    return out


NEXT_WEIGHTS_BEHIND = (("w_ff1", "w_ff2"), ("w_in", "w_uq", "w_ukv", "w_o"))


def _fwd_bwd(x, target, layer_weights, smalls, on_layer_grads):
    depth = len(smalls)
    cos, sin = _rope_tables(x.shape[0])
    h_f32, h_bf = x, x.astype(BF16)
    saved = []
    w, rest = layer_weights.first()
    for l in range(depth):
        more = l + 1 < depth
        behind = [layer_weights.comm(l + 1, names) if more else None for names in NEXT_WEIGHTS_BEHIND]
        h_f32, h_bf, sv, brought = _layer_fwd(h_f32, h_bf, w, smalls[l], cos, sin, behind, rest)
        saved.append(sv)
        if more:
            w, rest = {}, None
            for names, got in zip(NEXT_WEIGHTS_BEHIND, brought):
                w.update(layer_weights.take(l + 1, got, names))
    loss_part, dh = _loss_fn(h_f32, target)
    small_grads, pending = [None] * depth, None
    top = _ln_bwd("ln2_bwd", dh, saved[-1]["r2"], smalls[-1]["ln2_g"])
    for l in reversed(range(depth)):
        below = (saved[l - 1]["r2"], smalls[l - 1]["ln2_g"]) if l > 0 else None
        top, small_grads[l], pending = _layer_bwd(top, saved[l]["w"], smalls[l], saved[l], cos, sin,
                                                  lambda dws, l=l: on_layer_grads(l, dws), pending, below)
    return loss_part, top[0], small_grads, pending


def kernel(x, w_in, q_a_norm, kv_a_norm, w_uq, w_ukv, a_out_norm, b_out_norm, w_o, ln1_g, ln1_b, w_ff1, w_ff2, ln2_g, ln2_b, loss_target, m_w_in, m_q_a_norm, m_kv_a_norm, m_w_uq, m_w_ukv, m_a_out_norm, m_b_out_norm, m_w_o, m_ln1_g, m_ln1_b, m_w_ff1, m_w_ff2, m_ln2_g, m_ln2_b, v_w_in, v_q_a_norm, v_kv_a_norm, v_w_uq, v_w_ukv, v_a_out_norm, v_b_out_norm, v_w_o, v_ln1_g, v_ln1_b, v_w_ff1, v_w_ff2, v_ln2_g, v_ln2_b):
    c = lax.axis_index("c")
    chip = 2 * lax.axis_index("x") + lax.axis_index("y")
    big = dict(w_in=w_in, w_uq=w_uq, w_ukv=w_ukv, w_o=w_o, w_ff1=w_ff1, w_ff2=w_ff2)
    big_m = dict(w_in=m_w_in, w_uq=m_w_uq, w_ukv=m_w_ukv, w_o=m_w_o, w_ff1=m_w_ff1, w_ff2=m_w_ff2)
    big_v = dict(w_in=v_w_in, w_uq=v_w_uq, w_ukv=v_w_ukv, w_o=v_w_o, w_ff1=v_w_ff1, w_ff2=v_w_ff2)
    small = dict(q_a_norm=q_a_norm, kv_a_norm=kv_a_norm, a_out_norm=a_out_norm, b_out_norm=b_out_norm,
                 ln1_g=ln1_g, ln1_b=ln1_b, ln2_g=ln2_g, ln2_b=ln2_b)
    small_m = dict(q_a_norm=m_q_a_norm, kv_a_norm=m_kv_a_norm, a_out_norm=m_a_out_norm, b_out_norm=m_b_out_norm,
                   ln1_g=m_ln1_g, ln1_b=m_ln1_b, ln2_g=m_ln2_g, ln2_b=m_ln2_b)
    small_v = dict(q_a_norm=v_q_a_norm, kv_a_norm=v_kv_a_norm, a_out_norm=v_a_out_norm, b_out_norm=v_b_out_norm,
                   ln1_g=v_ln1_g, ln1_b=v_ln1_b, ln2_g=v_ln2_g, ln2_b=v_ln2_b)

    layer_weights = _LayerWeights([{n: big[n][l] for n in W_NAMES} for l in range(DEPTH)], c, 2 * chip + c)
    smalls = [{n: small[n][l][None, :] for n in SMALL_NAMES} for l in range(DEPTH)]
    reductions = [[] for _ in range(DEPTH)]

    def reduce_layer(l, dws):
        reductions[l].append(_Reduce(f"l{l}_{len(reductions[l])}", dws, c, chip))
        return reductions[l][-1]

    loss_part, grad_x, small_grads, last = _fwd_bwd(x[0], loss_target[0], layer_weights, smalls, reduce_layer)
    loss = lax.psum(loss_part[0, 0], ("x", "y", "c"))
    grad_x = grad_x[None]

    g_small_packed = _all_reduce_small(_pack_small(small_grads))
    pack_in = lambda d: _pack_small([{n: d[n][l] for n in SMALL_NAMES} for l in range(DEPTH)])
    sd, sm_, sv_ = _adamw("adamw_small", pack_in(small), g_small_packed, pack_in(small_m), pack_in(small_v), ts=SMALL_ROWS)
    g_small, d_small, m_small, v_small = (_unpack_small(t) for t in (g_small_packed, sd, sm_, sv_))

    last.run_alone()
    g_big, d_big, m_big, v_big = {}, {}, {}, {}
    for n in W_NAMES:
        g = jnp.stack([next(r.shards[n] for r in reductions[l] if n in r.names) for l in range(DEPTH)])
        shape = g.shape
        flat = lambda t: t.reshape(shape[0] * shape[1], shape[2])
        d_, m_, v_ = _adamw(f"adamw_{n}", flat(big[n]), flat(g), flat(big_m[n]), flat(big_v[n]))
        g_big[n], d_big[n], m_big[n], v_big[n] = g, d_.reshape(shape), m_.reshape(shape), v_.reshape(shape)

    order = ("w_in", "q_a_norm", "kv_a_norm", "w_uq", "w_ukv", "a_out_norm", "b_out_norm", "w_o", "ln1_g", "ln1_b",
             "w_ff1", "w_ff2", "ln2_g", "ln2_b")
    pick = lambda bigd, smalld: [bigd[n] if n in bigd else smalld[n] for n in order]
    return (loss, grad_x, *pick(g_big, g_small), *pick(d_big, d_small), *pick(m_big, m_small), *pick(v_big, v_small))
```

```python
import math

import jax
import jax.numpy as jnp
from jax import lax
from jax.experimental import pallas as pl
from jax.experimental.pallas import tpu as pltpu

F32, BF16 = jnp.float32, jnp.bfloat16

D_MODEL = 1024
DEPTH = 4
A_HEAD_DIM = 64
A_HEADS = 8
A_WIDTH = 512
PATTERNS = ((128, 1), (512, 4), (2048, 16))
SPAN = 128
MLA_HEADS = 4
QK_NOPE = 128
QK_ROPE = 64
V_DIM = 128
Q_LORA = 256
KV_LORA = 128
MLA_WIDTH = 512
QK_PAD = 256
IN_COLS = 1984
IN_PAD = 2048
UQ_PAD = 1024
D_FF = 4096
ROPE_THETA = 10000.0
ALPHA = (2.0 * DEPTH) ** 0.25
LN_EPS = 1e-5
RMS_EPS = 1e-6
MLA_SCALE = (QK_NOPE + QK_ROPE) ** -0.5
A_SCALE = A_HEAD_DIM ** -0.5

ADAM_LR, ADAM_B1, ADAM_B2, ADAM_EPS, ADAM_WD, ADAM_STEP = 0.001, 0.9, 0.999, 1e-08, 0.01, 10

VMEM_LIMIT_BYTES = 56 * 1024 * 1024
NEG = -1e30
N_CHIPS, N_DEV = 4, 8
LANES = 128

SMALL_NAMES = ("q_a_norm", "kv_a_norm", "a_out_norm", "b_out_norm", "ln1_g", "ln1_b", "ln2_g", "ln2_b")
SMALL_SIZES = (256, 128, 512, 512, 1024, 1024, 1024, 1024)
SMALL_ROWS = 176


def _params(sem):
    return pltpu.CompilerParams(dimension_semantics=sem, vmem_limit_bytes=VMEM_LIMIT_BYTES)


FAST_STRIDE = 4


def _interleave(parts, scratch):
    d, (n, width) = len(parts), parts[0].shape
    if d > FAST_STRIDE and d % FAST_STRIDE == 0:
        inner = [_interleave(parts[r::FAST_STRIDE], scratch) for r in range(FAST_STRIDE)]
        return _interleave(inner, scratch)
    for r in range(d):
        for s in range(width // LANES):
            scratch.at[s][pl.ds(r, n, stride=d), :] = parts[r][:, s * LANES:(s + 1) * LANES]
    return jnp.concatenate([scratch.at[s][pl.ds(0, n * d), :] for s in range(width // LANES)], axis=1)


def _deinterleave(x, d, scratch):
    n, width = x.shape[0] // d, x.shape[1]
    if d > FAST_STRIDE and d % FAST_STRIDE == 0:
        outer = _deinterleave(x, FAST_STRIDE, scratch)
        inner = [_deinterleave(y, d // FAST_STRIDE, scratch) for y in outer]
        return [inner[r % FAST_STRIDE][r // FAST_STRIDE] for r in range(d)]
    for s in range(width // LANES):
        scratch.at[s][pl.ds(0, n * d), :] = x[:, s * LANES:(s + 1) * LANES]
    return [jnp.concatenate([scratch.at[s][pl.ds(r, n, stride=d), :] for s in range(width // LANES)], axis=1) for r in range(d)]


def _rowwise(name, fn, ins, outs, consts=(), reds=(), ts=512, in_dils=None, out_dils=None, comm=None):
    in_dils = list(in_dils or [1] * len(ins))
    out_dils = list(out_dils or [1] * len(outs))
    n_rows = ins[0].shape[-2] * in_dils[0]
    ts = min(ts, n_rows)
    assert n_rows % ts == 0
    n_in, n_c, n_o, n_r = len(ins), len(consts), len(outs), len(reds)
    viewed = [(a.shape, d) for a, d in zip(list(ins) + list(outs), in_dils + out_dils) if d > 1]
    grid = (n_rows // ts,)
    h_ins, h_in_specs, h_outs, h_out_specs, begin, end = _hosted(comm, grid)
    n_hi, n_ho = len(h_ins), len(h_outs)

    def tile_spec(shape, d=1):
        if len(shape) == 2:
            return pl.BlockSpec((ts // d, shape[1]), lambda i: (i, 0))
        return pl.BlockSpec((shape[0], ts, shape[2]), lambda i: (0, i, 0))

    def whole_spec(shape):
        return pl.BlockSpec(shape, lambda i: (0,) * len(shape))

    def body(*refs):
        in_refs, c_refs, hi_refs = refs[:n_in], refs[n_in:n_in + n_c], refs[n_in + n_c:n_in + n_c + n_hi]
        refs = refs[n_in + n_c + n_hi:]
        o_refs, r_refs, ho_refs = refs[:n_o], refs[n_o:n_o + n_r], refs[n_o + n_r:n_o + n_r + n_ho]
        scratches = list(refs[n_o + n_r + n_ho:])
        sems = [scratches.pop(), scratches.pop()][::-1] if comm else []
        begin(hi_refs, ho_refs, sems)
        vals = []
        for r, d in zip(in_refs, in_dils):
            if d == 1:
                vals.append(r[...])
            else:
                width = r.shape[1] // d
                vals.append(_interleave([r[:, k * width:(k + 1) * width] for k in range(d)], scratches.pop(0)))
        res = fn(*vals, *[r[...] for r in c_refs])
        res = tuple(res) if isinstance(res, (tuple, list)) else (res,)
        for r, v, d in zip(o_refs, res[:n_o], out_dils):
            if len(r.shape) == 3:
                for g in range(r.shape[0]):
                    r[g] = v[g].astype(r.dtype)
            elif d == 1:
                r[...] = v.astype(r.dtype)
            else:
                width = r.shape[1] // d
                for k, part in enumerate(_deinterleave(v, d, scratches.pop(0))):
                    r[:, k * width:(k + 1) * width] = part.astype(r.dtype)
        if n_r:
            i = pl.program_id(0)

            @pl.when(i == 0)
            def _():
                for r, v in zip(r_refs, res[n_o:]):
                    r[...] = v

            @pl.when(i > 0)
            def _():
                for r, v in zip(r_refs, res[n_o:]):
                    r[...] += v
        end(hi_refs, ho_refs, sems)

    out = pl.pallas_call(
        body, name=name, grid=grid,
        in_specs=[tile_spec(a.shape, d) for a, d in zip(ins, in_dils)] + [whole_spec(c.shape) for c in consts] + h_in_specs,
        out_specs=[tile_spec(o.shape, d) for o, d in zip(outs, out_dils)] + [whole_spec(r.shape) for r in reds] + h_out_specs,
        out_shape=list(outs) + list(reds) + h_outs,
        scratch_shapes=[pltpu.VMEM((shape[1] // d // LANES, ts, LANES), F32) for shape, d in viewed]
        + (comm.sem_scratch() if comm else []),
        compiler_params=_params(("arbitrary",)),
    )(*ins, *consts, *h_ins)
    return out


def _sds(shape, dtype=F32):
    return jax.ShapeDtypeStruct(tuple(shape), dtype)


def _hosted(comm, grid):
    if comm is None:
        return [], [], [], [], (lambda *a: None), (lambda *a: None)

    def edge(which, at, ins, outs, sems):
        ids = [pl.program_id(d) for d in range(len(grid))]
        hit = ids[0] == at[0]
        for i, v in zip(ids[1:], at[1:]):
            hit = hit & (i == v)

        @pl.when(hit)
        def _():
            which(ins, outs, *sems)

    begin = lambda ins, outs, sems: edge(comm.start, [0] * len(grid), ins, outs, sems)
    end = lambda ins, outs, sems: edge(comm.finish, [g - 1 for g in grid], ins, outs, sems)
    return comm.ins, [ANY] * len(comm.ins), comm.out_shapes, [ANY] * len(comm.out_shapes), begin, end


def _mm(name, a, b, out_dtypes, *, ta=False, tb=False, tm=512, tn=512, tk=512, epilogue=None, extras=(), vecs=(),
        comm=None, b_pieces=False, out_pieces=False):
    (k_dim, m_dim) = a.shape if ta else a.shape[::-1]
    if b_pieces:
        b_rows, b_cols = 2 * b.shape[2], N_CHIPS * b.shape[3]
        (n_dim, k2) = (b_rows, b_cols) if tb else (b_cols, b_rows)
    else:
        (n_dim, k2) = b.shape if tb else b.shape[::-1]
    assert k_dim == k2
    tm, tn, tk = min(tm, m_dim), min(tn, n_dim), min(tk, k_dim)
    assert m_dim % tm == 0 and n_dim % tn == 0 and k_dim % tk == 0
    nk = k_dim // tk
    grid = (m_dim // tm, n_dim // tn, nk)
    a_spec = pl.BlockSpec((tk, tm), lambda i, j, k: (k, i)) if ta else pl.BlockSpec((tm, tk), lambda i, j, k: (i, k))
    if b_pieces:
        assert tn == b_rows and tk % b.shape[3] == 0 if tb else (tk, tn) == (b_rows, b.shape[3])
        n_bp = tk // b.shape[3] if tb else 1
        b_spec = pl.BlockSpec((n_bp,) + b.shape[1:], (lambda i, j, k: (k, 0, 0, 0)) if tb else (lambda i, j, k: (j, 0, 0, 0)))
    else:
        b_spec = pl.BlockSpec((tn, tk), lambda i, j, k: (j, k)) if tb else pl.BlockSpec((tk, tn), lambda i, j, k: (k, j))
    assert not out_pieces or (tm == m_dim and len(out_dtypes) == 1)
    dims = (((0 if ta else 1,), (1 if tb else 0,)), ((), ()))
    n_e, n_v, n_o = len(extras), len(vecs), len(out_dtypes)
    c_ins, c_in_specs, c_outs, c_out_specs, begin, end = _hosted(comm, grid)
    n_ci, n_co = len(c_ins), len(c_outs)

    def body(a_ref, b_ref, *refs):
        e_refs, v_refs, ci_refs = refs[:n_e], refs[n_e:n_e + n_v], refs[n_e + n_v:n_e + n_v + n_ci]
        refs = refs[n_e + n_v + n_ci:]
        o_refs, co_refs = refs[:n_o], refs[n_o:n_o + n_co]
        scratch = refs[n_o + n_co:]
        sems = scratch[-2:]
        begin(ci_refs, co_refs, sems)
        if b_pieces:
            pc = b.shape[3]
            part = None
            for cc in range(n_bp):
                a_val = a_ref[:, cc * pc:(cc + 1) * pc] if tb else a_ref[...]
                term = lax.dot_general(a_val.astype(BF16), b_ref[cc].reshape(b_rows, pc).astype(BF16), dims,
                                       preferred_element_type=F32)
                part = term if part is None else part + term
        else:
            part = lax.dot_general(a_ref[...].astype(BF16), b_ref[...].astype(BF16), dims, preferred_element_type=F32)

        def finish(acc):
            outs = epilogue(acc, *[r[...] for r in e_refs], *[r[...] for r in v_refs]) if epilogue else (acc,)
            for r, v in zip(o_refs, outs):
                r[...] = (v.reshape(2, tm // 2, tn) if out_pieces else v).astype(r.dtype)

        if nk == 1:
            finish(part)
        else:
            acc_ref = scratch[0]
            k = pl.program_id(2)

            @pl.when(k == 0)
            def _():
                acc_ref[...] = part

            @pl.when(k > 0)
            def _():
                acc_ref[...] += part

            @pl.when(k == nk - 1)
            def _():
                finish(acc_ref[...])
        end(ci_refs, co_refs, sems)

    tile = pl.BlockSpec((tm, tn), lambda i, j, k: (i, j))
    if out_pieces:
        out_tile, out_shape = pl.BlockSpec((None, 2, tm // 2, tn), lambda i, j, k: (j, 0, 0, 0)), (n_dim // tn, 2, tm // 2, tn)
    else:
        out_tile, out_shape = tile, (m_dim, n_dim)
    out = pl.pallas_call(
        body, name=name, grid=grid,
        in_specs=[a_spec, b_spec] + [tile] * n_e + [pl.BlockSpec((1, tn), lambda i, j, k: (0, j))] * n_v + c_in_specs,
        out_specs=[out_tile] * n_o + c_out_specs,
        out_shape=[_sds(out_shape, d) for d in out_dtypes] + c_outs,
        scratch_shapes=([pltpu.VMEM((tm, tn), F32)] if nk > 1 else []) + (comm.sem_scratch() if comm else []),
        compiler_params=_params(("arbitrary",) * 3 if comm else ("parallel", "parallel", "arbitrary")),
    )(a, b, *extras, *vecs, *c_ins)
    return out


def _swap_halves(x):
    width = x.shape[1]
    lane = lax.broadcasted_iota(jnp.int32, x.shape, 1)
    return jnp.where((lane % 64) < 32, pltpu.roll(x, width - 32, 1), pltpu.roll(x, 32, 1))


def _rope(x, cos, sin_signed):
    return x * cos + _swap_halves(x) * sin_signed


def _rope_t(d, cos, sin_signed):
    return d * cos - _swap_halves(d) * sin_signed


def _rms(x, g):
    rstd = lax.rsqrt(jnp.mean(x * x, axis=-1, keepdims=True) + RMS_EPS)
    return x * rstd * g


def _rms_bwd(x, g, dy):
    rstd = lax.rsqrt(jnp.mean(x * x, axis=-1, keepdims=True) + RMS_EPS)
    xh = x * rstd
    dyg = dy * g
    dx = rstd * (dyg - xh * jnp.mean(dyg * xh, axis=-1, keepdims=True))
    return dx, jnp.sum(dy * xh, axis=0, keepdims=True)


def _layer_norm(r, g, b):
    mu = jnp.mean(r, axis=-1, keepdims=True)
    xc = r - mu
    var = jnp.mean(xc * xc, axis=-1, keepdims=True)
    return xc * lax.rsqrt(var + LN_EPS) * g + b


def _dot(a, b, dims):
    return lax.dot_general(a, b, (dims, ((), ())), preferred_element_type=F32)


_NN, _NT, _TN = ((1,), (0,)), ((1,), (1,)), ((0,), (0,))


def _in_proj(x_bf, w_in_p, cos, sin):
    n_rows = x_bf.shape[0]
    ts = 512
    dils = [dil for _, dil in PATTERNS]
    n_p = len(dils)

    def body(x_ref, w_ref, cos_ref, sin_ref, *refs):
        view_refs, (cq_ref, kpe_ref, ckv_ref, scratch) = refs[:n_p], refs[n_p:]
        xb = x_ref[...]
        cos_v, sin_v = cos_ref[...], sin_ref[...]

        def seg(lo, hi):
            return _dot(xb, w_ref[:, lo:hi], _NN)

        def put(part, val):
            by_dil = {1: [val]}
            for ref, dil in zip(view_refs, dils):
                if dil not in by_dil:
                    if dil > FAST_STRIDE and dil % FAST_STRIDE == 0:
                        if FAST_STRIDE not in by_dil:
                            by_dil[FAST_STRIDE] = _deinterleave(val, FAST_STRIDE, scratch)
                        inner = [_deinterleave(y, dil // FAST_STRIDE, scratch) for y in by_dil[FAST_STRIDE]]
                        by_dil[dil] = [inner[r % FAST_STRIDE][r // FAST_STRIDE] for r in range(dil)]
                    else:
                        by_dil[dil] = _deinterleave(val, dil, scratch)
                for r, piece in enumerate(by_dil[dil]):
                    lo = (3 * r + part) * A_WIDTH
                    ref[:, lo:lo + A_WIDTH] = piece.astype(BF16)

        put(0, _rope(seg(0, 512), cos_v, sin_v) * A_SCALE)
        put(1, _rope(seg(512, 1024), cos_v, sin_v))
        put(2, seg(1024, 1536))
        cq_ref[...] = seg(1536, 1792)
        kpe_ref[...] = _rope(seg(1792, 1920), cos_v[:, :LANES], sin_v[:, :LANES]).astype(BF16)
        ckv_ref[...] = seg(1920, 2048)

    row = lambda c, d=1: pl.BlockSpec((ts // d, c * d), lambda i: (i, 0))
    return pl.pallas_call(
        body, name="in_proj", grid=(n_rows // ts,),
        in_specs=[row(D_MODEL), pl.BlockSpec((D_MODEL, IN_PAD), lambda i: (0, 0)), row(A_WIDTH), row(A_WIDTH)],
        out_specs=[row(3 * A_WIDTH, d) for d in dils] + [row(Q_LORA), row(LANES), row(KV_LORA)],
        out_shape=[_sds((n_rows // d, d * 3 * A_WIDTH), BF16) for d in dils]
        + [_sds((n_rows, Q_LORA)), _sds((n_rows, LANES), BF16), _sds((n_rows, KV_LORA))],
        scratch_shapes=[pltpu.VMEM((A_WIDTH // LANES, ts, LANES), F32)],
        compiler_params=_params(("parallel",)),
    )(x_bf, w_in_p, cos, sin)


def _band_mask(m, heads):
    qi = lax.broadcasted_iota(jnp.int32, (heads * SPAN, 2 * SPAN), 0) % SPAN
    kj = lax.broadcasted_iota(jnp.int32, (heads * SPAN, 2 * SPAN), 1)
    return ((kj < SPAN) & (kj >= qi) & (m > 0)) | ((kj >= SPAN) & ((kj - SPAN) <= qi))


def _even_lanes(rows):
    return lax.broadcasted_iota(jnp.int32, (rows, LANES), 1) < A_HEAD_DIM


DIL_GROUP = 4
HEAD_SLOT = LANES // A_HEADS
O_WORDS = A_WIDTH // 2
OL_WIDTH = O_WORDS + LANES


def _pack_halves(t):
    return _pack2(t[:, :O_WORDS], t[:, O_WORDS:])


def _unpack_halves(p):
    return jnp.concatenate(_unpack2(p), axis=1)


DQKV_WIDTH = 3 * A_WIDTH // 2


def _pack2(a, b):
    hi = lax.bitcast_convert_type(a.astype(jnp.bfloat16).astype(F32), jnp.uint32) & jnp.uint32(0xFFFF0000)
    lo = lax.shift_right_logical(lax.bitcast_convert_type(b.astype(jnp.bfloat16).astype(F32), jnp.uint32), jnp.uint32(16))
    return lax.bitcast_convert_type(hi | lo, F32)


def _unpack2(p):
    u = lax.bitcast_convert_type(p, jnp.uint32)
    return (lax.bitcast_convert_type(u & jnp.uint32(0xFFFF0000), F32),
            lax.bitcast_convert_type(lax.shift_left(u, jnp.uint32(16)), F32))


def _slot_lane(h):
    return HEAD_SLOT * (h // 2) + A_HEAD_DIM * (h % 2)


def _slot_head(lane):
    return 2 * (jnp.right_shift(lane, 4) & 3) + jnp.right_shift(lane, 6)


def _dil_fwd(qkv_view, dil, comm=None):
    nb = qkv_view.shape[0] // SPAN
    group = min(dil, DIL_GROUP)
    grid = (dil // group, nb)
    c_ins, c_in_specs, c_outs, c_out_specs, begin, end = _hosted(comm, grid)
    n_ci, n_co = len(c_ins), len(c_outs)

    def body(cur_ref, prev_ref, *refs):
        ci_refs, ol_ref, co_refs, sems = refs[:n_ci], refs[n_ci], refs[n_ci + 1:n_ci + 1 + n_co], refs[n_ci + 1 + n_co:]
        begin(ci_refs, co_refs, sems)
        mask = _band_mask(pl.program_id(1), 1)
        even, even2 = _even_lanes(SPAN), _even_lanes(2 * SPAN)
        pair_of_lane = jnp.right_shift(lax.broadcasted_iota(jnp.int32, (SPAN, LANES), 1), 4) & 3
        for g in range(group):
            lse_c, o_pairs = jnp.zeros((SPAN, LANES), F32), []
            for p in range(A_HEADS // 2):
                q_sl, k_sl, v_sl = (slice((3 * g + t) * A_WIDTH + p * LANES, (3 * g + t) * A_WIDTH + (p + 1) * LANES)
                                    for t in range(3))
                q2 = cur_ref[:, q_sl]
                kcat = jnp.concatenate([prev_ref[:, k_sl], cur_ref[:, k_sl]], axis=0)
                vcat = jnp.concatenate([prev_ref[:, v_sl], cur_ref[:, v_sl]], axis=0)
                zero, one = jnp.zeros_like(q2), jnp.ones_like(vcat)
                res, lses = [], []
                for first in (True, False):
                    qh = jnp.where(even, q2, zero) if first else jnp.where(even, zero, q2)
                    vh = jnp.where(even2, vcat, one) if first else jnp.where(even2, one, vcat)
                    s = jnp.where(mask, _dot(qh, kcat, _NT), NEG)
                    mx = jnp.max(s, axis=-1, keepdims=True)
                    r = _dot(jnp.exp(s - mx).astype(BF16), vh, _NN)
                    den = pltpu.roll(r, A_HEAD_DIM, 1)
                    res.append(r / den)
                    lses.append(mx + jnp.log(den))
                o_pairs.append(jnp.where(even, res[0], res[1]))
                lse_c = jnp.where(pair_of_lane == p, jnp.where(even, lses[0], lses[1]), lse_c)
            for p in range(2):
                ol_ref[:, OL_WIDTH * g + p * LANES:OL_WIDTH * g + (p + 1) * LANES] = _pack2(o_pairs[p], o_pairs[p + 2])
            ol_ref[:, OL_WIDTH * g + O_WORDS:OL_WIDTH * (g + 1)] = lse_c
        end(ci_refs, co_refs, sems)

    return pl.pallas_call(
        body, name=f"dil_fwd_d{dil}", grid=grid,
        in_specs=[pl.BlockSpec((SPAN, group * 3 * A_WIDTH), lambda r, m: (m, r)),
                  pl.BlockSpec((SPAN, group * 3 * A_WIDTH), lambda r, m: (jnp.maximum(m - 1, 0), r))] + c_in_specs,
        out_specs=[pl.BlockSpec((SPAN, group * OL_WIDTH), lambda r, m: (m, r))] + c_out_specs,
        out_shape=[_sds((qkv_view.shape[0], dil * OL_WIDTH))] + c_outs,
        scratch_shapes=comm.sem_scratch() if comm else [],
        compiler_params=_params(("arbitrary", "arbitrary") if comm else ("parallel", "arbitrary")),
    )(qkv_view, qkv_view, *c_ins)


def _slot_spread_matrix():
    i = lax.broadcasted_iota(jnp.int32, (LANES, A_WIDTH), 0)
    h = lax.broadcasted_iota(jnp.int32, (LANES, A_WIDTH), 1) // A_HEAD_DIM
    return (i == HEAD_SLOT * (h // 2) + A_HEAD_DIM * (h % 2)).astype(BF16)


def _spread_slots(c, spread):
    hi = c.astype(BF16)
    r1 = c - hi.astype(F32)
    mid = r1.astype(BF16)
    lo = (r1 - mid.astype(F32)).astype(BF16)
    return _dot(hi, spread, _NN) + _dot(mid, spread, _NN) + _dot(lo, spread, _NN)


def _pattern_weights(ol1, ol2, ol3):
    l1, l2, l3 = ol1[:, O_WORDS:], ol2[:, O_WORDS:], ol3[:, O_WORDS:]
    mx = jnp.maximum(jnp.maximum(l1, l2), l3)
    e1, e2, e3 = jnp.exp(l1 - mx), jnp.exp(l2 - mx), jnp.exp(l3 - mx)
    inv = 1.0 / (e1 + e2 + e3)
    return e1 * inv, e2 * inv, e3 * inv


def _mix_fwd(ols, b_out, a_g, b_g):
    def fn(ol1, ol2, ol3, b, ag, bg, spread):
        ws = [_spread_slots(w, spread) for w in _pattern_weights(ol1, ol2, ol3)]
        a = sum(w * _unpack_halves(ol[:, :O_WORDS]) for w, ol in zip(ws, (ol1, ol2, ol3)))
        return a, jnp.concatenate([_rms(a, ag), _rms(b, bg)], axis=1)

    n_rows = b_out.shape[0]
    return _rowwise("mix_fwd", fn, [*ols, b_out], [_sds((n_rows, A_WIDTH)), _sds((n_rows, 2 * A_WIDTH), BF16)],
                    consts=[a_g, b_g, _slot_spread_matrix()], in_dils=[dil for _, dil in PATTERNS] + [1])


def _mla_q_prep(cq, g, w_uq_p, cos, sin):
    def fn(cq_v, cos_v, sin_v, g_v, w_v):
        cqn = _rms(cq_v, g_v).astype(BF16)
        q = _dot(cqn, w_v, _NN)
        qf = [jnp.concatenate([q[:, h * QK_NOPE:(h + 1) * QK_NOPE],
                               _rope(q[:, 512 + h * LANES:512 + (h + 1) * LANES], cos_v[:, :LANES], sin_v[:, :LANES])], axis=1)
              for h in range(MLA_HEADS)]
        return cqn, qf

    n_rows = cq.shape[0]
    return _rowwise("mla_q_prep", fn, [cq, cos, sin], [_sds((n_rows, Q_LORA), BF16), _sds((MLA_HEADS, n_rows, QK_PAD), BF16)],
                    consts=[g, w_uq_p])


def _mla_kv_prep(ckv, kpe, g, w_ukv_p):
    def fn(ckv_v, kpe_v, g_v, w_v):
        ckvn = _rms(ckv_v, g_v).astype(BF16)
        kvv = _dot(ckvn, w_v, _NN)
        kf = [jnp.concatenate([kvv[:, h * QK_NOPE:(h + 1) * QK_NOPE], kpe_v.astype(F32)], axis=1) for h in range(MLA_HEADS)]
        return ckvn, kf, kvv[:, 512:]

    n_rows = ckv.shape[0]
    return _rowwise("mla_kv_prep", fn, [ckv, kpe],
                    [_sds((n_rows, KV_LORA), BF16), _sds((MLA_HEADS, n_rows, QK_PAD), BF16), _sds((n_rows, MLA_WIDTH), BF16)],
                    consts=[g, w_ukv_p])


MLA_FWD_TILES = (1024, 2048)
MLA_BWD_TILES = (1024, 1024)


def _mla_fwd(qf, kf, v, comm=None):
    n_rows = v.shape[0]
    tq, tk = min(MLA_FWD_TILES[0], n_rows), min(MLA_FWD_TILES[1], n_rows)
    nq, nk = n_rows // tq, n_rows // tk
    grid = (MLA_HEADS, nq, nk)
    exp2_scale = MLA_SCALE * math.log2(math.e)
    c_ins, c_in_specs, c_outs, c_out_specs, begin, end = _hosted(comm, grid)
    n_ci, n_co = len(c_ins), len(c_outs)

    def last_k(i):
        return jnp.right_shift(i * tq + tq - 1, int(math.log2(tk)))

    def body(q_ref, k_ref, v_ref, *refs):
        ci_refs, (o_ref, lse_ref), co_refs = refs[:n_ci], refs[n_ci:n_ci + 2], refs[n_ci + 2:n_ci + 2 + n_co]
        m_sc, l_sc, acc_sc = refs[n_ci + 2 + n_co:n_ci + 5 + n_co]
        sems = refs[n_ci + 5 + n_co:]
        i, j = pl.program_id(1), pl.program_id(2)
        begin(ci_refs, co_refs, sems)

        @pl.when(j == 0)
        def _():
            m_sc[...] = jnp.full(m_sc.shape, NEG, F32)
            l_sc[...] = jnp.zeros(l_sc.shape, F32)
            acc_sc[...] = jnp.zeros(acc_sc.shape, F32)

        def step(masked):
            s = _dot(q_ref[...], k_ref[...], _NT)
            if masked:
                row = lax.broadcasted_iota(jnp.int32, (tq, tk), 0) + i * tq
                col = lax.broadcasted_iota(jnp.int32, (tq, tk), 1) + j * tk
                s = jnp.where(col <= row, s, NEG)
            m_prev = m_sc[...]
            m_new = jnp.maximum(m_prev, jnp.max(s, axis=-1, keepdims=True))
            alpha = jnp.exp2((m_prev - m_new) * exp2_scale)
            p = jnp.exp2((s - m_new) * exp2_scale)
            l_sc[...] = alpha * l_sc[...] + jnp.sum(p, axis=-1, keepdims=True)
            acc_sc[...] = alpha * acc_sc[...] + _dot(p.astype(BF16), v_ref[...], _NN)
            m_sc[...] = m_new

        active = j * tk <= i * tq + tq - 1
        crosses = (j + 1) * tk - 1 > i * tq

        @pl.when(active & jnp.logical_not(crosses))
        def _():
            step(False)

        @pl.when(active & crosses)
        def _():
            step(True)

        @pl.when(j == last_k(i))
        def _():
            o_ref[...] = acc_sc[...] / l_sc[...]
            lse_ref[...] = jnp.broadcast_to(m_sc[...] * MLA_SCALE + jnp.log(l_sc[...]), (tq, V_DIM))

        end(ci_refs, co_refs, sems)

    return pl.pallas_call(
        body, name="mla_fwd", grid=grid,
        in_specs=[pl.BlockSpec((None, tq, QK_PAD), lambda h, i, j: (h, i, 0)),
                  pl.BlockSpec((None, tk, QK_PAD), lambda h, i, j: (h, jnp.minimum(j, last_k(i)), 0)),
                  pl.BlockSpec((tk, V_DIM), lambda h, i, j: (jnp.minimum(j, last_k(i)), h))] + c_in_specs,
        out_specs=[pl.BlockSpec((tq, V_DIM), lambda h, i, j: (i, h)), pl.BlockSpec((tq, V_DIM), lambda h, i, j: (i, h))]
        + c_out_specs,
        out_shape=[_sds((n_rows, MLA_WIDTH)), _sds((n_rows, MLA_WIDTH))] + c_outs,
        scratch_shapes=[pltpu.VMEM((tq, 1), F32), pltpu.VMEM((tq, 1), F32), pltpu.VMEM((tq, V_DIM), F32)]
        + (comm.sem_scratch() if comm else []),
        compiler_params=_params(("arbitrary",) * 3 if comm else ("parallel", "parallel", "arbitrary")),
    )(qf, kf, v, *c_ins)


def _mm_res_ln(name, a, w, xres, g, b):
    def epi(acc, xr, g_v, b_v):
        r = ALPHA * xr + acc
        y = _layer_norm(r, g_v, b_v)
        return y, y, r

    return _mm(name, a, w, (F32, BF16, F32), tm=1024, tn=D_MODEL, tk=1024, epilogue=epi, extras=[xres], vecs=[g, b])


def _mm_relu2(x_bf, w, comm=None):
    def epi(acc):
        r = jnp.maximum(acc, 0.0)
        return (r * r,)

    return _mm("ff1", x_bf, w, (BF16,), tm=1024, tn=1024, tk=D_MODEL, epilogue=epi, b_pieces=True, comm=comm)


def _loss_fn(y, t):
    def fn(y_v, t_v):
        d = y_v - t_v
        part = jnp.sum(jnp.sum(d * d, axis=1, keepdims=True), axis=0, keepdims=True)
        return d * (1.0 / D_MODEL), part

    dy, part = _rowwise("loss", fn, [y, t], [_sds(y.shape)], reds=[_sds((1, 1))])
    return part * (0.5 / D_MODEL), dy


def _ln_bwd_rows(dy, r, g):
    mu = jnp.mean(r, axis=-1, keepdims=True)
    xc = r - mu
    rstd = lax.rsqrt(jnp.mean(xc * xc, axis=-1, keepdims=True) + LN_EPS)
    xh = xc * rstd
    dxh = dy * g
    dr = rstd * (dxh - jnp.mean(dxh, axis=-1, keepdims=True) - xh * jnp.mean(dxh * xh, axis=-1, keepdims=True))
    return dr, dr, jnp.sum(dy * xh, axis=0, keepdims=True), jnp.sum(dy, axis=0, keepdims=True)


def _ln_bwd(name, dy, r, g):
    return _rowwise(name, _ln_bwd_rows, [dy, r], [_sds(dy.shape), _sds(dy.shape, BF16)], consts=[g],
                    reds=[_sds((1, D_MODEL)), _sds((1, D_MODEL))])


def _head_sum_matrix():
    i = lax.broadcasted_iota(jnp.int32, (A_WIDTH, LANES), 0) // A_HEAD_DIM
    return (i == _slot_head(lax.broadcasted_iota(jnp.int32, (A_WIDTH, LANES), 1))).astype(BF16)


def _mix_bwd(dmixed, a_out, b_out, ols, a_g, b_g):
    def fn(dm, a, b, ol1, ol2, ol3, ag, bg, sum_mat, spread):
        da, dga = _rms_bwd(a, ag, dm[:, :A_WIDTH])
        db, dgb = _rms_bwd(b, bg, dm[:, A_WIDTH:])
        t = da * a
        t_hi = t.astype(BF16)
        t_lo = (t - t_hi.astype(F32)).astype(BF16)
        tsum = _dot(t_hi, sum_mat, _NN) + _dot(t_lo, sum_mat, _NN)
        tb = db * b
        delta_b = jnp.concatenate(
            [jnp.broadcast_to(jnp.sum(tb[:, h * V_DIM:(h + 1) * V_DIM], axis=-1, keepdims=True), (tb.shape[0], V_DIM))
             for h in range(MLA_HEADS)], axis=1)
        dods = [jnp.concatenate([_pack_halves(_spread_slots(w, spread) * da), w * tsum], axis=1)
                for w in _pattern_weights(ol1, ol2, ol3)]
        return (*dods, db, delta_b, dga, dgb)

    n_rows = a_out.shape[0]
    wide = (n_rows, A_WIDTH)
    dils = [dil for _, dil in PATTERNS]
    return _rowwise("mix_bwd", fn, [dmixed, a_out, b_out, *ols],
                    [_sds((n_rows // d, d * OL_WIDTH)) for d in dils] + [_sds(wide, BF16), _sds(wide)],
                    consts=[a_g, b_g, _head_sum_matrix(), _slot_spread_matrix()],
                    reds=[_sds((1, A_WIDTH)), _sds((1, MLA_WIDTH))], ts=256,
                    in_dils=[1, 1, 1] + dils, out_dils=dils + [1, 1])


def _dil_bwd(qkv_view, ol_view, dod_view, dil, comm=None):
    nb = qkv_view.shape[0] // SPAN
    group = min(dil, DIL_GROUP)
    grid = (dil // group, nb + 1)
    c_ins, c_in_specs, c_outs, c_out_specs, begin, end = _hosted(comm, grid)
    n_ci, n_co = len(c_ins), len(c_outs)

    def body(cur_ref, prev_ref, ol_ref, dod_ref, *refs):
        ci_refs, out_ref, co_refs = refs[:n_ci], refs[n_ci], refs[n_ci + 1:n_ci + 1 + n_co]
        carry, sems = refs[n_ci + 1 + n_co], refs[n_ci + 2 + n_co:]
        m = pl.program_id(1)
        begin(ci_refs, co_refs, sems)

        def write_packed(g, pairs):
            lo = DQKV_WIDTH * g
            for p, (dq_p, dk_p, _) in enumerate(pairs):
                out_ref[:, lo + p * LANES:lo + (p + 1) * LANES] = _pack2(dq_p, dk_p)
            for p in range(2):
                out_ref[:, lo + A_WIDTH + p * LANES:lo + A_WIDTH + (p + 1) * LANES] = _pack2(pairs[p][2], pairs[p + 2][2])

        @pl.when(m == 0)
        def _():
            carry[...] = jnp.zeros(carry.shape, F32)

        @pl.when(m < nb)
        def _():
            mask, even = _band_mask(m, 2), _even_lanes(SPAN)
            for g in range(group):
                done = []
                for p in range(A_HEADS // 2):
                    q_sl, k_sl, v_sl = (slice((3 * g + t) * A_WIDTH + p * LANES, (3 * g + t) * A_WIDTH + (p + 1) * LANES)
                                        for t in range(3))
                    do_sl = slice(OL_WIDTH * g + (p % 2) * LANES, OL_WIDTH * g + (p % 2 + 1) * LANES)
                    s_e, s_o = (OL_WIDTH * g + O_WORDS + _slot_lane(2 * p + t) for t in range(2))
                    q2, do2 = cur_ref[:, q_sl], _unpack2(dod_ref[:, do_sl])[p // 2].astype(BF16)
                    zero = jnp.zeros_like(q2)
                    qcat = jnp.concatenate([jnp.where(even, q2, zero), jnp.where(even, zero, q2)], axis=0)
                    docat = jnp.concatenate([jnp.where(even, do2, zero), jnp.where(even, zero, do2)], axis=0)
                    kcat = jnp.concatenate([prev_ref[:, k_sl], cur_ref[:, k_sl]], axis=0)
                    vcat = jnp.concatenate([prev_ref[:, v_sl], cur_ref[:, v_sl]], axis=0)
                    lse_c = jnp.concatenate([ol_ref[:, s_e:s_e + 1], ol_ref[:, s_o:s_o + 1]], axis=0)
                    dl_c = jnp.concatenate([dod_ref[:, s_e:s_e + 1], dod_ref[:, s_o:s_o + 1]], axis=0)
                    pr = jnp.exp(jnp.where(mask, _dot(qcat, kcat, _NT), NEG) - lse_c)
                    ds = (pr * (_dot(docat, vcat, _NT) - dl_c)).astype(BF16)
                    dq = _dot(ds, kcat, _NN)
                    dk2 = _dot(ds, qcat, _TN)
                    dv2 = _dot(pr.astype(BF16), docat, _TN)
                    done.append((carry[:, q_sl], carry[:, k_sl] + dk2[:SPAN], carry[:, v_sl] + dv2[:SPAN]))
                    carry[:, q_sl] = jnp.where(even, dq[:SPAN], dq[SPAN:])
                    carry[:, k_sl] = dk2[SPAN:]
                    carry[:, v_sl] = dv2[SPAN:]
                write_packed(g, done)

        @pl.when(m == nb)
        def _():
            for g in range(group):
                write_packed(g, [tuple(carry[:, (3 * g + t) * A_WIDTH + p * LANES:(3 * g + t) * A_WIDTH + (p + 1) * LANES]
                                       for t in range(3)) for p in range(A_HEADS // 2)])

        end(ci_refs, co_refs, sems)

    last = nb - 1

    def cur(width):
        return pl.BlockSpec((SPAN, group * width), lambda r, m: (jnp.minimum(m, last), r))

    def prev(width):
        return pl.BlockSpec((SPAN, group * width), lambda r, m: (jnp.clip(m - 1, 0, last), r))

    return pl.pallas_call(
        body, name=f"dil_bwd_d{dil}", grid=grid,
        in_specs=[cur(3 * A_WIDTH), prev(3 * A_WIDTH), cur(OL_WIDTH), cur(OL_WIDTH)] + c_in_specs,
        out_specs=[prev(DQKV_WIDTH)] + c_out_specs,
        out_shape=[_sds((qkv_view.shape[0], dil * DQKV_WIDTH))] + c_outs,
        scratch_shapes=[pltpu.VMEM((SPAN, group * 3 * A_WIDTH), F32)] + (comm.sem_scratch() if comm else []),
        compiler_params=_params(("arbitrary", "arbitrary") if comm else ("parallel", "arbitrary")),
    )(qkv_view, qkv_view, ol_view, dod_view, *c_ins)


def _mla_bwd(qf, kf, v, lse, do, delta, comm=None):
    n_rows = v.shape[0]
    tq, tk = min(MLA_BWD_TILES[0], n_rows), min(MLA_BWD_TILES[1], n_rows)
    nq, nk = n_rows // tq, n_rows // tk

    def first_q(j):
        return jnp.right_shift(j * tk, int(math.log2(tq)))

    grid = (MLA_HEADS, nk, nq)
    c_ins, c_in_specs, c_outs, c_out_specs, begin, end = _hosted(comm, grid)
    n_ci, n_co = len(c_ins), len(c_outs)

    def body(q_ref, k_ref, v_ref, lse_ref, do_ref, dl_ref, *refs):
        ci_refs, (dq_ref, dk_ref, dv_ref), co_refs = refs[:n_ci], refs[n_ci:n_ci + 3], refs[n_ci + 3:n_ci + 3 + n_co]
        dk_sc, dv_sc = refs[n_ci + 3 + n_co:n_ci + 5 + n_co]
        sems = refs[n_ci + 5 + n_co:]
        j, i = pl.program_id(1), pl.program_id(2)
        begin(ci_refs, co_refs, sems)

        @pl.when((j == 0) & (i == 0))
        def _():
            dq_ref[...] = jnp.zeros(dq_ref.shape, F32)

        @pl.when(i == first_q(j))
        def _():
            dk_sc[...] = jnp.zeros(dk_sc.shape, F32)
            dv_sc[...] = jnp.zeros(dv_sc.shape, F32)

        def step(masked):
            q, k, dob = q_ref[...], k_ref[...], do_ref[...]
            s = _dot(q, k, _NT) * MLA_SCALE
            if masked:
                row = lax.broadcasted_iota(jnp.int32, (tq, tk), 0) + i * tq
                col = lax.broadcasted_iota(jnp.int32, (tq, tk), 1) + j * tk
                s = jnp.where(col <= row, s, NEG)
            p = jnp.exp(s - lse_ref[:, :1])
            dp = _dot(dob, v_ref[...], _NT)
            ds = (p * (dp - dl_ref[:, :1]) * MLA_SCALE).astype(BF16)
            dv_sc[...] += _dot(p.astype(BF16), dob, _TN)
            dk_sc[...] += _dot(ds, q, _TN)
            rows = pl.ds(pl.multiple_of(i * tq, tq), tq)
            dq_ref[rows, :] += _dot(ds, k, _NN)

        active = i >= first_q(j)
        crosses = (j + 1) * tk - 1 > i * tq

        @pl.when(active & jnp.logical_not(crosses))
        def _():
            step(False)

        @pl.when(active & crosses)
        def _():
            step(True)

        @pl.when(i == nq - 1)
        def _():
            dk_ref[...] = dk_sc[...]
            dv_ref[...] = dv_sc[...]

        end(ci_refs, co_refs, sems)

    qrow = lambda h, j, i: (jnp.maximum(i, first_q(j)), h)
    return pl.pallas_call(
        body, name="mla_bwd", grid=grid,
        in_specs=[pl.BlockSpec((None, tq, QK_PAD), lambda h, j, i: (h, jnp.maximum(i, first_q(j)), 0)),
                  pl.BlockSpec((None, tk, QK_PAD), lambda h, j, i: (h, j, 0)),
                  pl.BlockSpec((tk, V_DIM), lambda h, j, i: (j, h)),
                  pl.BlockSpec((tq, V_DIM), qrow), pl.BlockSpec((tq, V_DIM), qrow), pl.BlockSpec((tq, V_DIM), qrow)]
        + c_in_specs,
        out_specs=[pl.BlockSpec((None, n_rows, QK_PAD), lambda h, j, i: (h, 0, 0)),
                   pl.BlockSpec((None, tk, QK_PAD), lambda h, j, i: (h, j, 0)),
                   pl.BlockSpec((tk, V_DIM), lambda h, j, i: (j, h))] + c_out_specs,
        out_shape=[_sds((MLA_HEADS, n_rows, QK_PAD)), _sds((MLA_HEADS, n_rows, QK_PAD)), _sds((n_rows, MLA_WIDTH))] + c_outs,
        scratch_shapes=[pltpu.VMEM((tk, QK_PAD), F32), pltpu.VMEM((tk, V_DIM), F32)] + (comm.sem_scratch() if comm else []),
        compiler_params=_params(("arbitrary",) * 3 if comm else ("parallel", "arbitrary", "arbitrary")),
    )(qf, kf, v, lse, do, delta, *c_ins)


def _mla_bwd_prep(dqf, dkf, dv, cos, sin):
    def fn(dqf_v, dkf_v, dv_v, cos_v, sin_v):
        dq = jnp.concatenate([dqf_v[h][:, :QK_NOPE] for h in range(MLA_HEADS)]
                             + [_rope_t(dqf_v[h][:, QK_NOPE:], cos_v[:, :LANES], sin_v[:, :LANES]) for h in range(MLA_HEADS)], axis=1)
        dkv = jnp.concatenate([dkf_v[h][:, :QK_NOPE] for h in range(MLA_HEADS)] + [dv_v], axis=1)
        dkpe = dkf_v[0][:, QK_NOPE:] + dkf_v[1][:, QK_NOPE:] + dkf_v[2][:, QK_NOPE:] + dkf_v[3][:, QK_NOPE:]
        return dq, dkv, dkpe

    n_rows = dv.shape[0]
    return _rowwise("mla_bwd_prep", fn, [dqf, dkf, dv, cos, sin],
                    [_sds((n_rows, UQ_PAD), BF16), _sds((n_rows, 1024), BF16), _sds((n_rows, LANES))], ts=256)


def _assemble_dh(dqkvs, dcqn, cq, dckvn, ckv, dkpe, cos, sin, gq, gkv, comm=None):
    def fn(g1, g2, g3, dcqn_v, cq_v, dckvn_v, ckv_v, dkpe_v, cos_v, sin_v, gq_v, gkv_v):
        dq = dk = dv = 0.0
        for packed in (g1, g2, g3):
            dq_p, dk_p = _unpack2(packed[:, :A_WIDTH])
            dv_p = jnp.concatenate(_unpack2(packed[:, A_WIDTH:]), axis=1)
            dq, dk, dv = dq + dq_p, dk + dk_p, dv + dv_p
        dqa = _rope_t(dq, cos_v, sin_v) * A_SCALE
        dka = _rope_t(dk, cos_v, sin_v)
        dcq, dgq = _rms_bwd(cq_v, gq_v, dcqn_v)
        dckv, dgkv = _rms_bwd(ckv_v, gkv_v, dckvn_v)
        dkr = _rope_t(dkpe_v, cos_v[:, :LANES], sin_v[:, :LANES])
        return jnp.concatenate([dqa, dka, dv, dcq, dkr, dckv], axis=1), dgq, dgkv

    n_rows = cq.shape[0]
    return _rowwise("assemble_dh", fn, [*dqkvs, dcqn, cq, dckvn, ckv, dkpe, cos, sin],
                    [_sds((n_rows, IN_PAD), BF16)], consts=[gq, gkv],
                    reds=[_sds((1, Q_LORA)), _sds((1, KV_LORA))], ts=256,
                    in_dils=[dil for _, dil in PATTERNS] + [1] * 7, comm=comm)


def _layer_fwd(x_f32, x_bf, w, sm, cos, sin, behind=(None, None), rest=None):
    *qkv_views, cq, kpe, ckv = _in_proj(x_bf, w["w_in"], cos, sin)
    ol_views, brought = [], []
    for i, (view, (_, dil)) in enumerate(zip(qkv_views, PATTERNS)):
        ol, *got = _dil_fwd(view, dil, comm=rest[0][i] if rest else None)
        ol_views.append(ol)
        brought.append(got)
    if rest:
        w = {**w, **rest[1](brought)}
    cqn, qf = _mla_q_prep(cq, sm["q_a_norm"], w["w_uq"], cos, sin)
    ckvn, kf, v = _mla_kv_prep(ckv, kpe, sm["kv_a_norm"], w["w_ukv"])
    b_out, b_lse, *behind_attn = _mla_fwd(qf, kf, v, comm=behind[0])
    a_out, mixed = _mix_fwd(ol_views, b_out, sm["a_out_norm"], sm["b_out_norm"])
    x1, x1_bf, r1 = _mm_res_ln("wo_ln1", mixed, w["w_o"], x_f32, sm["ln1_g"], sm["ln1_b"])
    f, *behind_ff1 = _mm_relu2(x1_bf, w["w_ff1"], comm=behind[1])
    x2, x2_bf, r2 = _mm_res_ln("ff2_ln2", f, w["w_ff2"], x1, sm["ln2_g"], sm["ln2_b"])
    saved = dict(w=w, x_bf=x_bf, qkv_views=qkv_views, ol_views=ol_views, cq=cq, ckv=ckv, cqn=cqn, qf=qf, ckvn=ckvn,
                 kf=kf, v=v, b_out=b_out, b_lse=b_lse, a_out=a_out, mixed=mixed, x1_bf=x1_bf, r1=r1, f=f, r2=r2)
    return x2, x2_bf, saved, (behind_attn, behind_ff1)


def _layer_bwd(dx2, w, sm, sv, cos, sin, reduce_of, pending=None):
    dr2, dr2_bf, dg2, db2 = _ln_bwd("ln2_bwd", dx2, sv["r2"], sm["ln2_g"])
    (dw_ff2,) = _mm("dw_ff2", sv["f"], dr2_bf, (F32,), ta=True, tm=1024, tn=1024, tk=2048)
    du, *swapped = _mm("d_u", dr2_bf, w["w_ff2"], (BF16,), tb=True, tm=1024, tn=2048, tk=D_MODEL,
                       epilogue=lambda acc, f: (acc * (2.0 * jnp.sqrt(f.astype(F32))),), extras=[sv["f"]],
                       comm=_swap_comm(pending.pieces) if pending else None)
    if pending:
        pending.after_swap(swapped)
    (dw_ff1,) = _mm("dw_ff1", sv["x1_bf"], du, (F32,), ta=True, tm=1024, tn=1024, tk=2048, out_pieces=True)
    (dy1,) = _mm("d_x1", du, w["w_ff1"], (F32,), tb=True, tm=1024, tn=1024, tk=2048,
                 epilogue=lambda acc, d: (acc + ALPHA * d,), extras=[dr2], b_pieces=True)
    dr1, dr1_bf, dg1, db1 = _ln_bwd("ln1_bwd", dy1, sv["r1"], sm["ln1_g"])
    (dw_o,) = _mm("dw_o", sv["mixed"], dr1_bf, (F32,), ta=True, tm=1024, tn=1024, tk=2048)
    early = reduce_of(dict(w_o=dw_o, w_ff1=dw_ff1, w_ff2=dw_ff2))
    (dmixed,) = _mm("d_mixed", dr1_bf, w["w_o"], (F32,), tb=True, tm=1024, tn=1024, tk=D_MODEL)
    dod1, dod2, dod3, do_b, delta_b, dga, dgb = _mix_bwd(
        dmixed, sv["a_out"], sv["b_out"], sv["ol_views"], sm["a_out_norm"], sm["b_out_norm"])
    dqkvs = []
    for i, ((_, dil), qkv_view, ol_view, dod_view) in enumerate(zip(PATTERNS, sv["qkv_views"], sv["ol_views"], (dod1, dod2, dod3))):
        swap_here = early is not None and i == 0
        dqkv, *swapped = _dil_bwd(qkv_view, ol_view, dod_view, dil, comm=_swap_comm(early.pieces) if swap_here else None)
        dqkvs.append(dqkv)
        if swap_here:
            early.after_swap(swapped)
    to_scatter = (early.s16 if early else []) + (pending.s16 if pending else [])
    dqf, dkf, dv_b, *scattered = _mla_bwd(sv["qf"], sv["kf"], sv["v"], sv["b_lse"], do_b, delta_b,
                                          comm=_scatter_comm(to_scatter) if to_scatter else None)
    if early:
        early.after_scatter(scattered[:len(early.s16)])
    if pending:
        pending.after_scatter(scattered[len(scattered) - len(pending.s16):])
    dq_b, dkv_b, dkpe = _mla_bwd_prep(dqf, dkf, dv_b, cos, sin)
    (dw_uq,) = _mm("dw_uq", sv["cqn"], dq_b, (F32,), ta=True, tm=Q_LORA, tn=1024, tk=2048)
    (dcqn,) = _mm("d_cqn", dq_b, w["w_uq"], (F32,), tb=True, tm=1024, tn=Q_LORA, tk=UQ_PAD)
    (dw_ukv,) = _mm("dw_ukv", sv["ckvn"], dkv_b, (F32,), ta=True, tm=KV_LORA, tn=1024, tk=2048)
    (dckvn,) = _mm("d_ckvn", dkv_b, w["w_ukv"], (F32,), tb=True, tm=1024, tn=KV_LORA, tk=1024)
    to_share = (early.mine if early else []) + (pending.mine if pending else [])
    dh, dgq, dgkv, *shared = _assemble_dh(dqkvs, dcqn, sv["cq"], dckvn, sv["ckv"], dkpe, cos, sin,
                                          sm["q_a_norm"], sm["kv_a_norm"], comm=_share_comm(to_share) if to_share else None)
    if early:
        early.after_share(shared[:len(early.mine)])
    if pending:
        pending.after_share(shared[len(shared) - len(pending.mine):])
    (dw_in,) = _mm("dw_in", sv["x_bf"], dh, (F32,), ta=True, tm=1024, tn=1024, tk=2048)
    (dx,) = _mm("d_x", dh, w["w_in"], (F32,), tb=True, tm=1024, tn=1024, tk=2048,
                epilogue=lambda acc, d: (acc + ALPHA * d,), extras=[dr1])
    late = reduce_of(dict(w_in=dw_in, w_uq=dw_uq, w_ukv=dw_ukv))
    dsm = dict(q_a_norm=dgq, kv_a_norm=dgkv, a_out_norm=dga, b_out_norm=dgb, ln1_g=dg1, ln1_b=db1, ln2_g=dg2, ln2_b=db2)
    return dx, dsm, late


def _pad_w_in(w):
    return jnp.concatenate([w[:, :1792], w[:, 1920:1984], jnp.zeros((w.shape[0], 64), w.dtype), w[:, 1792:1920]], axis=1)


def _unpad_w_in(w):
    return jnp.concatenate([w[:, :1792], w[:, 1920:2048], w[:, 1792:1856]], axis=1)


def _rope_tables(n_rows):
    half = A_HEAD_DIM // 2
    inv_freq = ROPE_THETA ** (-jnp.arange(half, dtype=F32) / half)
    ang = jnp.arange(n_rows, dtype=F32)[:, None] * inv_freq[None, :]
    cos = jnp.tile(jnp.cos(ang), (1, 2 * A_HEADS))
    sin = jnp.tile(jnp.concatenate([-jnp.sin(ang), jnp.sin(ang)], axis=1), (1, A_HEADS))
    return cos, sin


W_NAMES = ("w_in", "w_uq", "w_ukv", "w_o", "w_ff1", "w_ff2")
PIECE = dict(w_in=(512, 512), w_uq=(128, 256), w_ukv=(64, 256), w_o=(128, 1024), w_ff1=(512, 1024), w_ff2=(512, 1024))
SHARD_COLS = dict(w_in=496, w_uq=192, w_ukv=256, w_o=1024, w_ff1=1024, w_ff2=1024)


def _col_pieces(t, pad_to):
    rows, _, cols = t.shape
    t = jnp.pad(t, ((0, 0), (0, 0), (0, pad_to - cols)))
    return t.reshape(2, rows // 2, N_CHIPS, pad_to).transpose(2, 0, 1, 3)


def _grad_pieces(dws):
    def heads(t, rows):
        return jnp.concatenate([t[:, :512].reshape(rows, MLA_HEADS, LANES), t[:, 512:].reshape(rows, MLA_HEADS, LANES)], axis=2)

    make = dict(
        w_in=lambda t: _col_pieces(_unpad_w_in(t).reshape(D_MODEL, N_CHIPS, SHARD_COLS["w_in"]), 512),
        w_uq=lambda t: _col_pieces(heads(t, Q_LORA), 256),
        w_ukv=lambda t: _col_pieces(heads(t, KV_LORA), 256),
        w_o=lambda t: t.reshape(N_CHIPS, 2, 128, D_MODEL),
        w_ff1=lambda t: t,
        w_ff2=lambda t: t.reshape(N_CHIPS, 2, 512, D_MODEL))
    return {n: make[n](t) for n, t in dws.items()}


def _weights_from_pieces(p):
    def cols(t):
        return t.transpose(1, 2, 0, 3).reshape(2 * t.shape[2], N_CHIPS, t.shape[3])

    def heads(t, rows):
        return jnp.concatenate([t[:, :, :LANES].reshape(rows, 512), t[:, :, LANES:].reshape(rows, 512)], axis=1)

    make = dict(
        w_in=lambda t: _pad_w_in(cols(t)[:, :, :SHARD_COLS["w_in"]].reshape(D_MODEL, IN_COLS)),
        w_uq=lambda t: heads(cols(t), Q_LORA),
        w_ukv=lambda t: heads(cols(t), KV_LORA),
        w_o=lambda t: t.reshape(D_MODEL, D_MODEL),
        w_ff1=lambda t: t,
        w_ff2=lambda t: t.reshape(D_FF, D_MODEL))
    return {n: make[n](t) for n, t in p.items()}


def _my_piece(name, shard, half):
    rows, cols = shard.shape[0] // 2, shard.shape[1]
    t = lax.dynamic_slice_in_dim(shard, half * rows, rows, axis=0).astype(BF16)
    return jnp.pad(t, ((0, 0), (0, PIECE[name][1] - cols)))


MESH = pl.DeviceIdType.MESH
ANY = pl.BlockSpec(memory_space=pl.ANY)


class _Comm:
    def __init__(self, ins, out_shapes, n_sems, start, finish):
        self.ins, self.out_shapes, self.n_sems, self.start, self.finish = list(ins), list(out_shapes), n_sems, start, finish

    def sem_scratch(self):
        return [pltpu.SemaphoreType.DMA((self.n_sems,)), pltpu.SemaphoreType.DMA((self.n_sems,))]


def _comm_call(name, comm):
    n_in, n_out = len(comm.ins), len(comm.out_shapes)

    def body(*refs):
        ins, outs, sems = refs[:n_in], refs[n_in:n_in + n_out], refs[n_in + n_out:]
        comm.start(ins, outs, *sems)
        comm.finish(ins, outs, *sems)

    return pl.pallas_call(body, name=name, out_shape=comm.out_shapes, in_specs=[ANY] * n_in, out_specs=[ANY] * n_out,
                          scratch_shapes=comm.sem_scratch())(*comm.ins)


def _all_gather_comm(blocks):
    n = len(blocks)

    def plan(x_refs, out_refs, send_sems, recv_sems):
        x, y, c = lax.axis_index("x"), lax.axis_index("y"), lax.axis_index("c")
        me, sibling = (x, y, c), (x, y, 1 - c)
        chips = [(1 - x, y), (x, 1 - y), (1 - x, 1 - y)]

        def copy(t, k, blk, to, src=None):
            px, py, pc = blk
            slot = out_refs[t].at[4 * px + 2 * py + pc]
            return pltpu.make_async_remote_copy(
                src_ref=slot if src is None else src, dst_ref=slot,
                send_sem=send_sems.at[7 * t + k], recv_sem=recv_sems.at[7 * t + k], device_id=to, device_id_type=MESH)

        first = []
        for t in range(n):
            first.append(copy(t, 0, me, sibling, src=x_refs[t]))
            first += [copy(t, 1 + j, me, (*chip, c), src=x_refs[t]) for j, chip in enumerate(chips)]
        return me, sibling, chips, c, copy, first

    def start(x_refs, out_refs, send_sems, recv_sems):
        for cp in plan(x_refs, out_refs, send_sems, recv_sems)[-1]:
            cp.start()

    def finish(x_refs, out_refs, send_sems, recv_sems):
        me, sibling, chips, c, copy, sent = plan(x_refs, out_refs, send_sems, recv_sems)
        for j, chip in enumerate(chips):
            for t in range(n):
                copy(t, 1 + j, (*chip, c), me).wait_recv()
                sent.append(copy(t, 4 + j, (*chip, c), sibling))
                sent[-1].start()
        for t in range(n):
            copy(t, 0, sibling, me).wait_recv()
        for j, chip in enumerate(chips):
            for t in range(n):
                copy(t, 4 + j, (*chip, 1 - c), me).wait_recv()
        for cp in sent:
            cp.wait_send()

    return _Comm(blocks, [_sds((N_DEV,) + b.shape, b.dtype) for b in blocks], 7 * n, start, finish)


def _simple_comm(ins, out_shapes, n_sems, copies):
    def start(in_refs, out_refs, send_sems, recv_sems):
        for cp in copies(in_refs, out_refs, send_sems, recv_sems):
            cp.start()

    def finish(in_refs, out_refs, send_sems, recv_sems):
        for cp in copies(in_refs, out_refs, send_sems, recv_sems):
            cp.wait()

    return _Comm(ins, out_shapes, n_sems, start, finish)


def _swap_comm(gs):
    def copies(g_refs, got_refs, send_sems, recv_sems):
        c = lax.axis_index("c")
        sibling = (lax.axis_index("x"), lax.axis_index("y"), 1 - c)
        return [pltpu.make_async_remote_copy(src_ref=g_refs[t].at[k, 1 - c], dst_ref=got_refs[t].at[k],
                                             send_sem=send_sems.at[N_CHIPS * t + k], recv_sem=recv_sems.at[N_CHIPS * t + k],
                                             device_id=sibling, device_id_type=MESH)
                for t in range(len(gs)) for k in range(N_CHIPS)]

    return _simple_comm(gs, [_sds((N_CHIPS,) + g.shape[2:], g.dtype) for g in gs], N_CHIPS * len(gs), copies)


def _scatter_comm(hs):
    def copies(h_refs, got_refs, send_sems, recv_sems):
        x, y, c = lax.axis_index("x"), lax.axis_index("y"), lax.axis_index("c")
        chips = [(1 - x, y), (x, 1 - y), (1 - x, 1 - y)]
        return [pltpu.make_async_remote_copy(src_ref=h_refs[t].at[2 * px + py], dst_ref=got_refs[t].at[rel],
                                             send_sem=send_sems.at[3 * t + rel], recv_sem=recv_sems.at[3 * t + rel],
                                             device_id=(px, py, c), device_id_type=MESH)
                for t in range(len(hs)) for rel, (px, py) in enumerate(chips)]

    return _simple_comm(hs, [_sds((3,) + h.shape[1:], h.dtype) for h in hs], 3 * len(hs), copies)


def _share_comm(rs):
    def copies(r_refs, got_refs, send_sems, recv_sems):
        sibling = (lax.axis_index("x"), lax.axis_index("y"), 1 - lax.axis_index("c"))
        return [pltpu.make_async_remote_copy(src_ref=r_refs[t], dst_ref=got_refs[t], send_sem=send_sems.at[t],
                                             recv_sem=recv_sems.at[t], device_id=sibling, device_id_type=MESH)
                for t in range(len(rs))]

    return _simple_comm(rs, [_sds(r.shape, r.dtype) for r in rs], len(rs), copies)


SUM_STEPS = 4


def _pair_sums(name, gs, gots, where):
    n = len(gs)
    dims = [(g.shape[2] // SUM_STEPS, g.shape[3]) for g in gs]

    def body(where_ref, *refs):
        own = pl.program_id(1) == where_ref[1]
        for t in range(n):
            s = refs[t][...] + refs[n + t][...]
            refs[3 * n + t][...] = s.astype(BF16)

            @pl.when(own)
            def _():
                refs[2 * n + t][...] = s

    def part(tr, pc):
        return pl.BlockSpec((None, tr, pc), lambda i, k, where_ref: (k, i, 0))

    def kept(tr, pc):
        return pl.BlockSpec((None, None, tr, pc), lambda i, k, where_ref: (k, where_ref[0], i, 0))

    outs = pl.pallas_call(
        body, name=name,
        out_shape=[_sds(g.shape[2:]) for g in gs] + [_sds((N_CHIPS,) + g.shape[2:], BF16) for g in gs],
        grid_spec=pltpu.PrefetchScalarGridSpec(
            num_scalar_prefetch=1, grid=(SUM_STEPS, N_CHIPS),
            in_specs=[kept(*d) for d in dims] + [part(*d) for d in dims],
            out_specs=[pl.BlockSpec(d, lambda i, k, where_ref: (i, 0)) for d in dims] + [part(*d) for d in dims]),
        compiler_params=_params(("parallel", "arbitrary")),
    )(where, *gs, *gots)
    return list(outs[:n]), list(outs[n:])


def _chips_sums(name, owns, gots):
    n = len(owns)
    dims = [(h.shape[0] // SUM_STEPS, h.shape[1]) for h in owns]

    def body(*refs):
        for t in range(n):
            got_ref = refs[n + t]
            refs[2 * n + t][...] = refs[t][...] + got_ref[0] + got_ref[1] + got_ref[2]

    row = [pl.BlockSpec((tr, pc), lambda i: (i, 0)) for tr, pc in dims]
    return pl.pallas_call(
        body, name=name, out_shape=[_sds(h.shape) for h in owns], grid=(SUM_STEPS,),
        in_specs=row + [pl.BlockSpec((3, tr, pc), lambda i: (0, i, 0)) for tr, pc in dims], out_specs=row,
        compiler_params=_params(("parallel",)),
    )(*owns, *gots)


class _Reduce:
    def __init__(self, tag, dws, c, chip):
        gp = _grad_pieces(dws)
        self.tag, self.c, self.names, self.pieces = tag, c, list(gp), list(gp.values())
        self.where = jnp.stack([c, chip]).astype(jnp.int32)
        self.shards = None

    def after_swap(self, gots):
        self.s32, self.s16 = _pair_sums(f"rs_pair_sum_{self.tag}", self.pieces, gots, self.where)

    def after_scatter(self, parts):
        self.mine = list(_chips_sums(f"rs_chips_sum_{self.tag}", self.s32, parts))

    def after_share(self, theirs):
        self.shards = {}
        for n, a, b in zip(self.names, self.mine, theirs):
            both = jnp.where(self.c == 0, jnp.concatenate([a, b], axis=0), jnp.concatenate([b, a], axis=0))
            self.shards[n] = both[:, :SHARD_COLS[n]]

    def run_alone(self):
        self.after_swap(_comm_call(f"rs_pair_{self.tag}", _swap_comm(self.pieces)))
        self.after_scatter(_comm_call(f"rs_chips_{self.tag}", _scatter_comm(self.s16)))
        self.after_share(_comm_call(f"rs_share_{self.tag}", _share_comm(self.mine)))


class _LayerWeights:
    def __init__(self, shards, c, dev):
        self.shards, self.c, self.dev = shards, c, dev

    def comm(self, l, names=W_NAMES):
        return _all_gather_comm([_my_piece(n, self.shards[l][n], self.c) for n in names])

    def take(self, l, gathered, names=W_NAMES):
        pieces = {}
        for n, g in zip(names, gathered):
            mine = _my_piece(n, self.shards[l][n], self.c)
            pieces[n] = lax.dynamic_update_index_in_dim(g, mine, self.dev, 0).reshape((N_CHIPS, 2) + mine.shape)
        return _weights_from_pieces(pieces)

    def first(self):
        now, later = ("w_in", "w_uq", "w_ukv"), ("w_ff1", "w_ff2", "w_o")
        weights = self.take(0, _comm_call("ag_l0", self.comm(0, now)), now)
        comms = [self.comm(0, (n,)) for n in later]

        def take_rest(results):
            rest = {}
            for n, gathered in zip(later, results):
                rest.update(self.take(0, gathered, (n,)))
            return rest

        return weights, (comms, take_rest)


def _all_reduce_small(vec):
    rows, lanes = vec.shape

    def gather_body(x_ref, out_ref, send_sems, recv_sems):
        x, y, c = lax.axis_index("x"), lax.axis_index("y"), lax.axis_index("c")
        me, sibling = (x, y, c), (x, y, 1 - c)
        chips = [(1 - x, y), (x, 1 - y), (1 - x, 1 - y)]

        def slot(px, py, pc):
            return out_ref.at[4 * px + 2 * py + pc]

        def copy(k, blk, to, src=None):
            return pltpu.make_async_remote_copy(
                src_ref=slot(*blk) if src is None else src, dst_ref=slot(*blk),
                send_sem=send_sems.at[k], recv_sem=recv_sems.at[k], device_id=to, device_id_type=MESH)

        out_ref[4 * x + 2 * y + c] = x_ref[...]
        first = [copy(0, me, sibling, src=x_ref)]
        first += [copy(1 + j, me, (*chip, c), src=x_ref) for j, chip in enumerate(chips)]
        for cp in first:
            cp.start()
        passed = [copy(4 + j, (*chip, c), sibling) for j, chip in enumerate(chips)]
        for j, chip in enumerate(chips):
            copy(1 + j, (*chip, c), me).wait_recv()
            passed[j].start()
        copy(0, sibling, me).wait_recv()
        for j, chip in enumerate(chips):
            copy(4 + j, (*chip, 1 - c), me).wait_recv()
        for cp in first + passed:
            cp.wait_send()

    vmem = pl.BlockSpec(memory_space=pltpu.VMEM)
    allv = pl.pallas_call(
        gather_body, name="small_all_gather", out_shape=_sds((N_DEV, rows, lanes)),
        in_specs=[vmem], out_specs=vmem,
        scratch_shapes=[pltpu.SemaphoreType.DMA((7,)), pltpu.SemaphoreType.DMA((7,))],
    )(vec)

    def sum_body(a_ref, o_ref):
        acc = a_ref[0]
        for d in range(1, N_DEV):
            acc = acc + a_ref[d]
        o_ref[...] = acc

    return pl.pallas_call(sum_body, name="small_sum", out_shape=_sds((rows, lanes)), in_specs=[vmem], out_specs=vmem)(allv)


def _adamw(name, w, g, m, v, ts=512, comm=None):
    def fn(w_v, g_v, m_v, v_v):
        m_n = ADAM_B1 * m_v + (1.0 - ADAM_B1) * g_v
        v_n = ADAM_B2 * v_v + (1.0 - ADAM_B2) * (g_v * g_v)
        m_hat = m_n / (1.0 - ADAM_B1 ** ADAM_STEP)
        v_hat = v_n / (1.0 - ADAM_B2 ** ADAM_STEP)
        delta = -ADAM_LR * (m_hat / (jnp.sqrt(v_hat) + ADAM_EPS) + ADAM_WD * w_v)
        return delta, m_n, v_n

    return _rowwise(name, fn, [w, g, m, v], [_sds(w.shape)] * 3, ts=ts, comm=comm)


def _pack_small(per_layer):
    flat = jnp.concatenate([per_layer[l][n].reshape(-1) for l in range(DEPTH) for n in SMALL_NAMES])
    return jnp.pad(flat, (0, SMALL_ROWS * LANES - flat.shape[0])).reshape(SMALL_ROWS, LANES)


def _unpack_small(packed):
    flat = packed.reshape(-1)
    per = sum(SMALL_SIZES)
    out = {}
    for n, size, off in zip(SMALL_NAMES, SMALL_SIZES, [sum(SMALL_SIZES[:i]) for i in range(len(SMALL_SIZES))]):
        out[n] = jnp.stack([flat[l * per + off:l * per + off + size] for l in range(DEPTH)])
    return out


NEXT_WEIGHTS_BEHIND = (("w_ff1", "w_ff2"), ("w_in", "w_uq", "w_ukv", "w_o"))


def _fwd_bwd(x, target, layer_weights, smalls, on_layer_grads):
    depth = len(smalls)
    cos, sin = _rope_tables(x.shape[0])
    h_f32, h_bf = x, x.astype(BF16)
    saved = []
    w, rest = layer_weights.first()
    for l in range(depth):
        more = l + 1 < depth
        behind = [layer_weights.comm(l + 1, names) if more else None for names in NEXT_WEIGHTS_BEHIND]
        h_f32, h_bf, sv, brought = _layer_fwd(h_f32, h_bf, w, smalls[l], cos, sin, behind, rest)
        saved.append(sv)
        if more:
            w, rest = {}, None
            for names, got in zip(NEXT_WEIGHTS_BEHIND, brought):
                w.update(layer_weights.take(l + 1, got, names))
    loss_part, dh = _loss_fn(h_f32, target)
    small_grads, pending = [None] * depth, None
    for l in reversed(range(depth)):
        dh, small_grads[l], pending = _layer_bwd(dh, saved[l]["w"], smalls[l], saved[l], cos, sin,
                                                 lambda dws, l=l: on_layer_grads(l, dws), pending)
    return loss_part, dh, small_grads, pending


def kernel(x, w_in, q_a_norm, kv_a_norm, w_uq, w_ukv, a_out_norm, b_out_norm, w_o, ln1_g, ln1_b, w_ff1, w_ff2, ln2_g, ln2_b, loss_target, m_w_in, m_q_a_norm, m_kv_a_norm, m_w_uq, m_w_ukv, m_a_out_norm, m_b_out_norm, m_w_o, m_ln1_g, m_ln1_b, m_w_ff1, m_w_ff2, m_ln2_g, m_ln2_b, v_w_in, v_q_a_norm, v_kv_a_norm, v_w_uq, v_w_ukv, v_a_out_norm, v_b_out_norm, v_w_o, v_ln1_g, v_ln1_b, v_w_ff1, v_w_ff2, v_ln2_g, v_ln2_b):
    c = lax.axis_index("c")
    chip = 2 * lax.axis_index("x") + lax.axis_index("y")
    big = dict(w_in=w_in, w_uq=w_uq, w_ukv=w_ukv, w_o=w_o, w_ff1=w_ff1, w_ff2=w_ff2)
    big_m = dict(w_in=m_w_in, w_uq=m_w_uq, w_ukv=m_w_ukv, w_o=m_w_o, w_ff1=m_w_ff1, w_ff2=m_w_ff2)
    big_v = dict(w_in=v_w_in, w_uq=v_w_uq, w_ukv=v_w_ukv, w_o=v_w_o, w_ff1=v_w_ff1, w_ff2=v_w_ff2)
    small = dict(q_a_norm=q_a_norm, kv_a_norm=kv_a_norm, a_out_norm=a_out_norm, b_out_norm=b_out_norm,
                 ln1_g=ln1_g, ln1_b=ln1_b, ln2_g=ln2_g, ln2_b=ln2_b)
    small_m = dict(q_a_norm=m_q_a_norm, kv_a_norm=m_kv_a_norm, a_out_norm=m_a_out_norm, b_out_norm=m_b_out_norm,
                   ln1_g=m_ln1_g, ln1_b=m_ln1_b, ln2_g=m_ln2_g, ln2_b=m_ln2_b)
    small_v = dict(q_a_norm=v_q_a_norm, kv_a_norm=v_kv_a_norm, a_out_norm=v_a_out_norm, b_out_norm=v_b_out_norm,
                   ln1_g=v_ln1_g, ln1_b=v_ln1_b, ln2_g=v_ln2_g, ln2_b=v_ln2_b)

    layer_weights = _LayerWeights([{n: big[n][l] for n in W_NAMES} for l in range(DEPTH)], c, 2 * chip + c)
    smalls = [{n: small[n][l][None, :] for n in SMALL_NAMES} for l in range(DEPTH)]
    reductions = [[] for _ in range(DEPTH)]

    def reduce_layer(l, dws):
        reductions[l].append(_Reduce(f"l{l}_{len(reductions[l])}", dws, c, chip))
        return reductions[l][-1]

    loss_part, grad_x, small_grads, last = _fwd_bwd(x[0], loss_target[0], layer_weights, smalls, reduce_layer)
    loss = lax.psum(loss_part[0, 0], ("x", "y", "c"))
    grad_x = grad_x[None]

    g_small_packed = _all_reduce_small(_pack_small(small_grads))
    pack_in = lambda d: _pack_small([{n: d[n][l] for n in SMALL_NAMES} for l in range(DEPTH)])
    sd, sm_, sv_ = _adamw("adamw_small", pack_in(small), g_small_packed, pack_in(small_m), pack_in(small_v), ts=SMALL_ROWS)
    g_small, d_small, m_small, v_small = (_unpack_small(t) for t in (g_small_packed, sd, sm_, sv_))

    last.run_alone()
    g_big, d_big, m_big, v_big = {}, {}, {}, {}
    for n in W_NAMES:
        g = jnp.stack([next(r.shards[n] for r in reductions[l] if n in r.names) for l in range(DEPTH)])
        shape = g.shape
        flat = lambda t: t.reshape(shape[0] * shape[1], shape[2])
        d_, m_, v_ = _adamw(f"adamw_{n}", flat(big[n]), flat(g), flat(big_m[n]), flat(big_v[n]))
        g_big[n], d_big[n], m_big[n], v_big[n] = g, d_.reshape(shape), m_.reshape(shape), v_.reshape(shape)

    order = ("w_in", "q_a_norm", "kv_a_norm", "w_uq", "w_ukv", "a_out_norm", "b_out_norm", "w_o", "ln1_g", "ln1_b",
             "w_ff1", "w_ff2", "ln2_g", "ln2_b")
    pick = lambda bigd, smalld: [bigd[n] if n in bigd else smalld[n] for n in order]
    return (loss, grad_x, *pick(g_big, g_small), *pick(d_big, d_small), *pick(m_big, m_small), *pick(v_big, v_small))
```

```python
import math

import jax
import jax.numpy as jnp
from jax import lax
from jax.experimental import pallas as pl
from jax.experimental.pallas import tpu as pltpu

F32, BF16 = jnp.float32, jnp.bfloat16

D_MODEL = 1024
DEPTH = 4
A_HEAD_DIM = 64
A_HEADS = 8
A_WIDTH = 512
PATTERNS = ((128, 1), (512, 4), (2048, 16))
SPAN = 128
MLA_HEADS = 4
QK_NOPE = 128
QK_ROPE = 64
V_DIM = 128
Q_LORA = 256
KV_LORA = 128
MLA_WIDTH = 512
QK_PAD = 256
IN_COLS = 1984
IN_PAD = 2048
UQ_PAD = 1024
D_FF = 4096
ROPE_THETA = 10000.0
ALPHA = (2.0 * DEPTH) ** 0.25
LN_EPS = 1e-5
RMS_EPS = 1e-6
MLA_SCALE = (QK_NOPE + QK_ROPE) ** -0.5
A_SCALE = A_HEAD_DIM ** -0.5

ADAM_LR, ADAM_B1, ADAM_B2, ADAM_EPS, ADAM_WD, ADAM_STEP = 0.001, 0.9, 0.999, 1e-08, 0.01, 10

VMEM_LIMIT_BYTES = 56 * 1024 * 1024
NEG = -1e30
N_CHIPS, N_DEV = 4, 8
LANES = 128

SMALL_NAMES = ("q_a_norm", "kv_a_norm", "a_out_norm", "b_out_norm", "ln1_g", "ln1_b", "ln2_g", "ln2_b")
SMALL_SIZES = (256, 128, 512, 512, 1024, 1024, 1024, 1024)
SMALL_ROWS = 176


def _params(sem):
    return pltpu.CompilerParams(dimension_semantics=sem, vmem_limit_bytes=VMEM_LIMIT_BYTES)


FAST_STRIDE = 4


def _interleave(parts, scratch):
    d, (n, width) = len(parts), parts[0].shape
    if d > FAST_STRIDE and d % FAST_STRIDE == 0:
        inner = [_interleave(parts[r::FAST_STRIDE], scratch) for r in range(FAST_STRIDE)]
        return _interleave(inner, scratch)
    for r in range(d):
        for s in range(width // LANES):
            scratch.at[s][pl.ds(r, n, stride=d), :] = parts[r][:, s * LANES:(s + 1) * LANES]
    return jnp.concatenate([scratch.at[s][pl.ds(0, n * d), :] for s in range(width // LANES)], axis=1)


def _deinterleave(x, d, scratch):
    n, width = x.shape[0] // d, x.shape[1]
    if d > FAST_STRIDE and d % FAST_STRIDE == 0:
        outer = _deinterleave(x, FAST_STRIDE, scratch)
        inner = [_deinterleave(y, d // FAST_STRIDE, scratch) for y in outer]
        return [inner[r % FAST_STRIDE][r // FAST_STRIDE] for r in range(d)]
    for s in range(width // LANES):
        scratch.at[s][pl.ds(0, n * d), :] = x[:, s * LANES:(s + 1) * LANES]
    return [jnp.concatenate([scratch.at[s][pl.ds(r, n, stride=d), :] for s in range(width // LANES)], axis=1) for r in range(d)]


def _rowwise(name, fn, ins, outs, consts=(), reds=(), ts=512, in_dils=None, out_dils=None, comm=None):
    in_dils = list(in_dils or [1] * len(ins))
    out_dils = list(out_dils or [1] * len(outs))
    n_rows = ins[0].shape[-2] * in_dils[0]
    ts = min(ts, n_rows)
    assert n_rows % ts == 0
    n_in, n_c, n_o, n_r = len(ins), len(consts), len(outs), len(reds)
    viewed = [(a.shape, d) for a, d in zip(list(ins) + list(outs), in_dils + out_dils) if d > 1]
    grid = (n_rows // ts,)
    h_ins, h_in_specs, h_outs, h_out_specs, begin, end = _hosted(comm, grid)
    n_hi, n_ho = len(h_ins), len(h_outs)

    def tile_spec(shape, d=1):
        if len(shape) == 2:
            return pl.BlockSpec((ts // d, shape[1]), lambda i: (i, 0))
        return pl.BlockSpec((shape[0], ts, shape[2]), lambda i: (0, i, 0))

    def whole_spec(shape):
        return pl.BlockSpec(shape, lambda i: (0,) * len(shape))

    def body(*refs):
        in_refs, c_refs, hi_refs = refs[:n_in], refs[n_in:n_in + n_c], refs[n_in + n_c:n_in + n_c + n_hi]
        refs = refs[n_in + n_c + n_hi:]
        o_refs, r_refs, ho_refs = refs[:n_o], refs[n_o:n_o + n_r], refs[n_o + n_r:n_o + n_r + n_ho]
        scratches = list(refs[n_o + n_r + n_ho:])
        sems = [scratches.pop(), scratches.pop()][::-1] if comm else []
        begin(hi_refs, ho_refs, sems)
        vals = []
        for r, d in zip(in_refs, in_dils):
            if d == 1:
                vals.append(r[...])
            else:
                width = r.shape[1] // d
                vals.append(_interleave([r[:, k * width:(k + 1) * width] for k in range(d)], scratches.pop(0)))
        res = fn(*vals, *[r[...] for r in c_refs])
        res = tuple(res) if isinstance(res, (tuple, list)) else (res,)
        for r, v, d in zip(o_refs, res[:n_o], out_dils):
            if len(r.shape) == 3:
                for g in range(r.shape[0]):
                    r[g] = v[g].astype(r.dtype)
            elif d == 1:
                r[...] = v.astype(r.dtype)
            else:
                width = r.shape[1] // d
                for k, part in enumerate(_deinterleave(v, d, scratches.pop(0))):
                    r[:, k * width:(k + 1) * width] = part.astype(r.dtype)
        if n_r:
            i = pl.program_id(0)

            @pl.when(i == 0)
            def _():
                for r, v in zip(r_refs, res[n_o:]):
                    r[...] = v

            @pl.when(i > 0)
            def _():
                for r, v in zip(r_refs, res[n_o:]):
                    r[...] += v
        end(hi_refs, ho_refs, sems)

    out = pl.pallas_call(
        body, name=name, grid=grid,
        in_specs=[tile_spec(a.shape, d) for a, d in zip(ins, in_dils)] + [whole_spec(c.shape) for c in consts] + h_in_specs,
        out_specs=[tile_spec(o.shape, d) for o, d in zip(outs, out_dils)] + [whole_spec(r.shape) for r in reds] + h_out_specs,
        out_shape=list(outs) + list(reds) + h_outs,
        scratch_shapes=[pltpu.VMEM((shape[1] // d // LANES, ts, LANES), F32) for shape, d in viewed]
        + (comm.sem_scratch() if comm else []),
        compiler_params=_params(("arbitrary",)),
    )(*ins, *consts, *h_ins)
    return out


def _sds(shape, dtype=F32):
    return jax.ShapeDtypeStruct(tuple(shape), dtype)


def _hosted(comm, grid):
    if comm is None:
        return [], [], [], [], (lambda *a: None), (lambda *a: None)

    def edge(which, at, ins, outs, sems):
        ids = [pl.program_id(d) for d in range(len(grid))]
        hit = ids[0] == at[0]
        for i, v in zip(ids[1:], at[1:]):
            hit = hit & (i == v)

        @pl.when(hit)
        def _():
            which(ins, outs, *sems)

    begin = lambda ins, outs, sems: edge(comm.start, [0] * len(grid), ins, outs, sems)
    end = lambda ins, outs, sems: edge(comm.finish, [g - 1 for g in grid], ins, outs, sems)
    return comm.ins, [ANY] * len(comm.ins), comm.out_shapes, [ANY] * len(comm.out_shapes), begin, end


def _mm(name, a, b, out_dtypes, *, ta=False, tb=False, tm=512, tn=512, tk=512, epilogue=None, extras=(), vecs=(),
        comm=None, b_pieces=False, out_pieces=False):
    (k_dim, m_dim) = a.shape if ta else a.shape[::-1]
    if b_pieces:
        b_rows, b_cols = 2 * b.shape[2], N_CHIPS * b.shape[3]
        (n_dim, k2) = (b_rows, b_cols) if tb else (b_cols, b_rows)
    else:
        (n_dim, k2) = b.shape if tb else b.shape[::-1]
    assert k_dim == k2
    tm, tn, tk = min(tm, m_dim), min(tn, n_dim), min(tk, k_dim)
    assert m_dim % tm == 0 and n_dim % tn == 0 and k_dim % tk == 0
    nk = k_dim // tk
    grid = (m_dim // tm, n_dim // tn, nk)
    a_spec = pl.BlockSpec((tk, tm), lambda i, j, k: (k, i)) if ta else pl.BlockSpec((tm, tk), lambda i, j, k: (i, k))
    if b_pieces:
        assert tn == b_rows and tk % b.shape[3] == 0 if tb else (tk, tn) == (b_rows, b.shape[3])
        n_bp = tk // b.shape[3] if tb else 1
        b_spec = pl.BlockSpec((n_bp,) + b.shape[1:], (lambda i, j, k: (k, 0, 0, 0)) if tb else (lambda i, j, k: (j, 0, 0, 0)))
    else:
        b_spec = pl.BlockSpec((tn, tk), lambda i, j, k: (j, k)) if tb else pl.BlockSpec((tk, tn), lambda i, j, k: (k, j))
    assert not out_pieces or (tm == m_dim and len(out_dtypes) == 1)
    dims = (((0 if ta else 1,), (1 if tb else 0,)), ((), ()))
    n_e, n_v, n_o = len(extras), len(vecs), len(out_dtypes)
    c_ins, c_in_specs, c_outs, c_out_specs, begin, end = _hosted(comm, grid)
    n_ci, n_co = len(c_ins), len(c_outs)

    def body(a_ref, b_ref, *refs):
        e_refs, v_refs, ci_refs = refs[:n_e], refs[n_e:n_e + n_v], refs[n_e + n_v:n_e + n_v + n_ci]
        refs = refs[n_e + n_v + n_ci:]
        o_refs, co_refs = refs[:n_o], refs[n_o:n_o + n_co]
        scratch = refs[n_o + n_co:]
        sems = scratch[-2:]
        begin(ci_refs, co_refs, sems)
        if b_pieces:
            pc = b.shape[3]
            part = None
            for cc in range(n_bp):
                a_val = a_ref[:, cc * pc:(cc + 1) * pc] if tb else a_ref[...]
                term = lax.dot_general(a_val.astype(BF16), b_ref[cc].reshape(b_rows, pc).astype(BF16), dims,
                                       preferred_element_type=F32)
                part = term if part is None else part + term
        else:
            part = lax.dot_general(a_ref[...].astype(BF16), b_ref[...].astype(BF16), dims, preferred_element_type=F32)

        def finish(acc):
            outs = epilogue(acc, *[r[...] for r in e_refs], *[r[...] for r in v_refs]) if epilogue else (acc,)
            for r, v in zip(o_refs, outs):
                r[...] = (v.reshape(2, tm // 2, tn) if out_pieces else v).astype(r.dtype)

        if nk == 1:
            finish(part)
        else:
            acc_ref = scratch[0]
            k = pl.program_id(2)

            @pl.when(k == 0)
            def _():
                acc_ref[...] = part

            @pl.when(k > 0)
            def _():
                acc_ref[...] += part

            @pl.when(k == nk - 1)
            def _():
                finish(acc_ref[...])
        end(ci_refs, co_refs, sems)

    tile = pl.BlockSpec((tm, tn), lambda i, j, k: (i, j))
    if out_pieces:
        out_tile, out_shape = pl.BlockSpec((None, 2, tm // 2, tn), lambda i, j, k: (j, 0, 0, 0)), (n_dim // tn, 2, tm // 2, tn)
    else:
        out_tile, out_shape = tile, (m_dim, n_dim)
    out = pl.pallas_call(
        body, name=name, grid=grid,
        in_specs=[a_spec, b_spec] + [tile] * n_e + [pl.BlockSpec((1, tn), lambda i, j, k: (0, j))] * n_v + c_in_specs,
        out_specs=[out_tile] * n_o + c_out_specs,
        out_shape=[_sds(out_shape, d) for d in out_dtypes] + c_outs,
        scratch_shapes=([pltpu.VMEM((tm, tn), F32)] if nk > 1 else []) + (comm.sem_scratch() if comm else []),
        compiler_params=_params(("arbitrary",) * 3 if comm else ("parallel", "parallel", "arbitrary")),
    )(a, b, *extras, *vecs, *c_ins)
    return out


def _swap_halves(x):
    width = x.shape[1]
    lane = lax.broadcasted_iota(jnp.int32, x.shape, 1)
    return jnp.where((lane % 64) < 32, pltpu.roll(x, width - 32, 1), pltpu.roll(x, 32, 1))


def _rope(x, cos, sin_signed):
    return x * cos + _swap_halves(x) * sin_signed


def _rope_t(d, cos, sin_signed):
    return d * cos - _swap_halves(d) * sin_signed


def _rms(x, g):
    rstd = lax.rsqrt(jnp.mean(x * x, axis=-1, keepdims=True) + RMS_EPS)
    return x * rstd * g


def _rms_bwd(x, g, dy):
    rstd = lax.rsqrt(jnp.mean(x * x, axis=-1, keepdims=True) + RMS_EPS)
    xh = x * rstd
    dyg = dy * g
    dx = rstd * (dyg - xh * jnp.mean(dyg * xh, axis=-1, keepdims=True))
    return dx, jnp.sum(dy * xh, axis=0, keepdims=True)


def _layer_norm(r, g, b):
    mu = jnp.mean(r, axis=-1, keepdims=True)
    xc = r - mu
    var = jnp.mean(xc * xc, axis=-1, keepdims=True)
    return xc * lax.rsqrt(var + LN_EPS) * g + b


def _dot(a, b, dims):
    return lax.dot_general(a, b, (dims, ((), ())), preferred_element_type=F32)


_NN, _NT, _TN = ((1,), (0,)), ((1,), (1,)), ((0,), (0,))


def _in_proj(x_bf, w_in_p, cos, sin):
    n_rows = x_bf.shape[0]
    ts = 512
    dils = [dil for _, dil in PATTERNS]
    n_p = len(dils)

    def body(x_ref, w_ref, cos_ref, sin_ref, *refs):
        view_refs, (cq_ref, kpe_ref, ckv_ref, scratch) = refs[:n_p], refs[n_p:]
        xb = x_ref[...]
        cos_v, sin_v = cos_ref[...], sin_ref[...]

        def seg(lo, hi):
            return _dot(xb, w_ref[:, lo:hi], _NN)

        def put(part, val):
            by_dil = {1: [val]}
            for ref, dil in zip(view_refs, dils):
                if dil not in by_dil:
                    if dil > FAST_STRIDE and dil % FAST_STRIDE == 0:
                        if FAST_STRIDE not in by_dil:
                            by_dil[FAST_STRIDE] = _deinterleave(val, FAST_STRIDE, scratch)
                        inner = [_deinterleave(y, dil // FAST_STRIDE, scratch) for y in by_dil[FAST_STRIDE]]
                        by_dil[dil] = [inner[r % FAST_STRIDE][r // FAST_STRIDE] for r in range(dil)]
                    else:
                        by_dil[dil] = _deinterleave(val, dil, scratch)
                for r, piece in enumerate(by_dil[dil]):
                    lo = (3 * r + part) * A_WIDTH
                    ref[:, lo:lo + A_WIDTH] = piece.astype(BF16)

        put(0, _rope(seg(0, 512), cos_v, sin_v) * A_SCALE)
        put(1, _rope(seg(512, 1024), cos_v, sin_v))
        put(2, seg(1024, 1536))
        cq_ref[...] = seg(1536, 1792)
        kpe_ref[...] = _rope(seg(1792, 1920), cos_v[:, :LANES], sin_v[:, :LANES]).astype(BF16)
        ckv_ref[...] = seg(1920, 2048)

    row = lambda c, d=1: pl.BlockSpec((ts // d, c * d), lambda i: (i, 0))
    return pl.pallas_call(
        body, name="in_proj", grid=(n_rows // ts,),
        in_specs=[row(D_MODEL), pl.BlockSpec((D_MODEL, IN_PAD), lambda i: (0, 0)), row(A_WIDTH), row(A_WIDTH)],
        out_specs=[row(3 * A_WIDTH, d) for d in dils] + [row(Q_LORA), row(LANES), row(KV_LORA)],
        out_shape=[_sds((n_rows // d, d * 3 * A_WIDTH), BF16) for d in dils]
        + [_sds((n_rows, Q_LORA)), _sds((n_rows, LANES), BF16), _sds((n_rows, KV_LORA))],
        scratch_shapes=[pltpu.VMEM((A_WIDTH // LANES, ts, LANES), F32)],
        compiler_params=_params(("parallel",)),
    )(x_bf, w_in_p, cos, sin)


def _band_mask(m, heads):
    qi = lax.broadcasted_iota(jnp.int32, (heads * SPAN, 2 * SPAN), 0) % SPAN
    kj = lax.broadcasted_iota(jnp.int32, (heads * SPAN, 2 * SPAN), 1)
    return ((kj < SPAN) & (kj >= qi) & (m > 0)) | ((kj >= SPAN) & ((kj - SPAN) <= qi))


def _even_lanes(rows):
    return lax.broadcasted_iota(jnp.int32, (rows, LANES), 1) < A_HEAD_DIM


DIL_GROUP = 4
HEAD_SLOT = LANES // A_HEADS
O_WORDS = A_WIDTH // 2
OL_WIDTH = O_WORDS + LANES


def _pack_halves(t):
    return _pack2(t[:, :O_WORDS], t[:, O_WORDS:])


def _unpack_halves(p):
    return jnp.concatenate(_unpack2(p), axis=1)


DQKV_WIDTH = 3 * A_WIDTH // 2


def _pack2(a, b):
    hi = lax.bitcast_convert_type(a.astype(jnp.bfloat16).astype(F32), jnp.uint32) & jnp.uint32(0xFFFF0000)
    lo = lax.shift_right_logical(lax.bitcast_convert_type(b.astype(jnp.bfloat16).astype(F32), jnp.uint32), jnp.uint32(16))
    return lax.bitcast_convert_type(hi | lo, F32)


def _unpack2(p):
    u = lax.bitcast_convert_type(p, jnp.uint32)
    return (lax.bitcast_convert_type(u & jnp.uint32(0xFFFF0000), F32),
            lax.bitcast_convert_type(lax.shift_left(u, jnp.uint32(16)), F32))


def _slot_lane(h):
    return HEAD_SLOT * (h // 2) + A_HEAD_DIM * (h % 2)


def _slot_head(lane):
    return 2 * (jnp.right_shift(lane, 4) & 3) + jnp.right_shift(lane, 6)


def _dil_fwd(qkv_view, dil, comm=None):
    nb = qkv_view.shape[0] // SPAN
    group = min(dil, DIL_GROUP)
    grid = (dil // group, nb)
    c_ins, c_in_specs, c_outs, c_out_specs, begin, end = _hosted(comm, grid)
    n_ci, n_co = len(c_ins), len(c_outs)

    def body(cur_ref, prev_ref, *refs):
        ci_refs, ol_ref, co_refs, sems = refs[:n_ci], refs[n_ci], refs[n_ci + 1:n_ci + 1 + n_co], refs[n_ci + 1 + n_co:]
        begin(ci_refs, co_refs, sems)
        mask = _band_mask(pl.program_id(1), 1)
        even, even2 = _even_lanes(SPAN), _even_lanes(2 * SPAN)
        pair_of_lane = jnp.right_shift(lax.broadcasted_iota(jnp.int32, (SPAN, LANES), 1), 4) & 3
        for g in range(group):
            lse_c, o_pairs = jnp.zeros((SPAN, LANES), F32), []
            for p in range(A_HEADS // 2):
                q_sl, k_sl, v_sl = (slice((3 * g + t) * A_WIDTH + p * LANES, (3 * g + t) * A_WIDTH + (p + 1) * LANES)
                                    for t in range(3))
                q2 = cur_ref[:, q_sl]
                kcat = jnp.concatenate([prev_ref[:, k_sl], cur_ref[:, k_sl]], axis=0)
                vcat = jnp.concatenate([prev_ref[:, v_sl], cur_ref[:, v_sl]], axis=0)
                zero, one = jnp.zeros_like(q2), jnp.ones_like(vcat)
                res, lses = [], []
                for first in (True, False):
                    qh = jnp.where(even, q2, zero) if first else jnp.where(even, zero, q2)
                    vh = jnp.where(even2, vcat, one) if first else jnp.where(even2, one, vcat)
                    s = jnp.where(mask, _dot(qh, kcat, _NT), NEG)
                    mx = jnp.max(s, axis=-1, keepdims=True)
                    r = _dot(jnp.exp(s - mx).astype(BF16), vh, _NN)
                    den = pltpu.roll(r, A_HEAD_DIM, 1)
                    res.append(r / den)
                    lses.append(mx + jnp.log(den))
                o_pairs.append(jnp.where(even, res[0], res[1]))
                lse_c = jnp.where(pair_of_lane == p, jnp.where(even, lses[0], lses[1]), lse_c)
            for p in range(2):
                ol_ref[:, OL_WIDTH * g + p * LANES:OL_WIDTH * g + (p + 1) * LANES] = _pack2(o_pairs[p], o_pairs[p + 2])
            ol_ref[:, OL_WIDTH * g + O_WORDS:OL_WIDTH * (g + 1)] = lse_c
        end(ci_refs, co_refs, sems)

    return pl.pallas_call(
        body, name=f"dil_fwd_d{dil}", grid=grid,
        in_specs=[pl.BlockSpec((SPAN, group * 3 * A_WIDTH), lambda r, m: (m, r)),
                  pl.BlockSpec((SPAN, group * 3 * A_WIDTH), lambda r, m: (jnp.maximum(m - 1, 0), r))] + c_in_specs,
        out_specs=[pl.BlockSpec((SPAN, group * OL_WIDTH), lambda r, m: (m, r))] + c_out_specs,
        out_shape=[_sds((qkv_view.shape[0], dil * OL_WIDTH))] + c_outs,
        scratch_shapes=comm.sem_scratch() if comm else [],
        compiler_params=_params(("arbitrary", "arbitrary") if comm else ("parallel", "arbitrary")),
    )(qkv_view, qkv_view, *c_ins)


def _slot_spread_matrix():
    i = lax.broadcasted_iota(jnp.int32, (LANES, A_WIDTH), 0)
    h = lax.broadcasted_iota(jnp.int32, (LANES, A_WIDTH), 1) // A_HEAD_DIM
    return (i == HEAD_SLOT * (h // 2) + A_HEAD_DIM * (h % 2)).astype(BF16)


def _spread_slots(c, spread):
    hi = c.astype(BF16)
    r1 = c - hi.astype(F32)
    mid = r1.astype(BF16)
    lo = (r1 - mid.astype(F32)).astype(BF16)
    return _dot(hi, spread, _NN) + _dot(mid, spread, _NN) + _dot(lo, spread, _NN)


def _pattern_weights(ol1, ol2, ol3):
    l1, l2, l3 = ol1[:, O_WORDS:], ol2[:, O_WORDS:], ol3[:, O_WORDS:]
    mx = jnp.maximum(jnp.maximum(l1, l2), l3)
    e1, e2, e3 = jnp.exp(l1 - mx), jnp.exp(l2 - mx), jnp.exp(l3 - mx)
    inv = 1.0 / (e1 + e2 + e3)
    return e1 * inv, e2 * inv, e3 * inv


def _mix_fwd(ols, b_out, a_g, b_g):
    def fn(ol1, ol2, ol3, b, ag, bg, spread):
        ws = [_spread_slots(w, spread) for w in _pattern_weights(ol1, ol2, ol3)]
        a = sum(w * _unpack_halves(ol[:, :O_WORDS]) for w, ol in zip(ws, (ol1, ol2, ol3)))
        return a, jnp.concatenate([_rms(a, ag), _rms(b, bg)], axis=1)

    n_rows = b_out.shape[0]
    return _rowwise("mix_fwd", fn, [*ols, b_out], [_sds((n_rows, A_WIDTH)), _sds((n_rows, 2 * A_WIDTH), BF16)],
                    consts=[a_g, b_g, _slot_spread_matrix()], in_dils=[dil for _, dil in PATTERNS] + [1])


def _mla_prep(cq, ckv, kpe, cos, sin, gq, w_uq_p, gkv, w_ukv_p):
    def fn(cq_v, ckv_v, kpe_v, cos_v, sin_v, gq_v, wq_v, gkv_v, wkv_v):
        cqn = _rms(cq_v, gq_v).astype(BF16)
        q = _dot(cqn, wq_v, _NN)
        qf = [jnp.concatenate([q[:, h * QK_NOPE:(h + 1) * QK_NOPE],
                               _rope(q[:, 512 + h * LANES:512 + (h + 1) * LANES], cos_v[:, :LANES], sin_v[:, :LANES])], axis=1)
              for h in range(MLA_HEADS)]
        ckvn = _rms(ckv_v, gkv_v).astype(BF16)
        kvv = _dot(ckvn, wkv_v, _NN)
        kf = [jnp.concatenate([kvv[:, h * QK_NOPE:(h + 1) * QK_NOPE], kpe_v.astype(F32)], axis=1) for h in range(MLA_HEADS)]
        return cqn, qf, ckvn, kf, kvv[:, 512:]

    n_rows = cq.shape[0]
    heads = _sds((MLA_HEADS, n_rows, QK_PAD), BF16)
    return _rowwise("mla_prep", fn, [cq, ckv, kpe, cos, sin],
                    [_sds((n_rows, Q_LORA), BF16), heads, _sds((n_rows, KV_LORA), BF16), heads, _sds((n_rows, MLA_WIDTH), BF16)],
                    consts=[gq, w_uq_p, gkv, w_ukv_p])


MLA_FWD_TILES = (1024, 2048)
MLA_BWD_TILES = (1024, 1024)


def _mla_fwd(qf, kf, v, comm=None):
    n_rows = v.shape[0]
    tq, tk = min(MLA_FWD_TILES[0], n_rows), min(MLA_FWD_TILES[1], n_rows)
    nq, nk = n_rows // tq, n_rows // tk
    grid = (MLA_HEADS, nq, nk)
    exp2_scale = MLA_SCALE * math.log2(math.e)
    c_ins, c_in_specs, c_outs, c_out_specs, begin, end = _hosted(comm, grid)
    n_ci, n_co = len(c_ins), len(c_outs)

    def last_k(i):
        return jnp.right_shift(i * tq + tq - 1, int(math.log2(tk)))

    def body(q_ref, k_ref, v_ref, *refs):
        ci_refs, (o_ref, lse_ref), co_refs = refs[:n_ci], refs[n_ci:n_ci + 2], refs[n_ci + 2:n_ci + 2 + n_co]
        m_sc, l_sc, acc_sc = refs[n_ci + 2 + n_co:n_ci + 5 + n_co]
        sems = refs[n_ci + 5 + n_co:]
        i, j = pl.program_id(1), pl.program_id(2)
        begin(ci_refs, co_refs, sems)

        @pl.when(j == 0)
        def _():
            m_sc[...] = jnp.full(m_sc.shape, NEG, F32)
            l_sc[...] = jnp.zeros(l_sc.shape, F32)
            acc_sc[...] = jnp.zeros(acc_sc.shape, F32)

        def step(masked):
            s = _dot(q_ref[...], k_ref[...], _NT)
            if masked:
                row = lax.broadcasted_iota(jnp.int32, (tq, tk), 0) + i * tq
                col = lax.broadcasted_iota(jnp.int32, (tq, tk), 1) + j * tk
                s = jnp.where(col <= row, s, NEG)
            m_prev = m_sc[...]
            m_new = jnp.maximum(m_prev, jnp.max(s, axis=-1, keepdims=True))
            alpha = jnp.exp2((m_prev - m_new) * exp2_scale)
            p = jnp.exp2((s - m_new) * exp2_scale)
            l_sc[...] = alpha * l_sc[...] + jnp.sum(p, axis=-1, keepdims=True)
            acc_sc[...] = alpha * acc_sc[...] + _dot(p.astype(BF16), v_ref[...], _NN)
            m_sc[...] = m_new

        active = j * tk <= i * tq + tq - 1
        crosses = (j + 1) * tk - 1 > i * tq

        @pl.when(active & jnp.logical_not(crosses))
        def _():
            step(False)

        @pl.when(active & crosses)
        def _():
            step(True)

        @pl.when(j == last_k(i))
        def _():
            o_ref[...] = acc_sc[...] / l_sc[...]
            lse_ref[...] = jnp.broadcast_to(m_sc[...] * MLA_SCALE + jnp.log(l_sc[...]), (tq, V_DIM))

        end(ci_refs, co_refs, sems)

    return pl.pallas_call(
        body, name="mla_fwd", grid=grid,
        in_specs=[pl.BlockSpec((None, tq, QK_PAD), lambda h, i, j: (h, i, 0)),
                  pl.BlockSpec((None, tk, QK_PAD), lambda h, i, j: (h, jnp.minimum(j, last_k(i)), 0)),
                  pl.BlockSpec((tk, V_DIM), lambda h, i, j: (jnp.minimum(j, last_k(i)), h))] + c_in_specs,
        out_specs=[pl.BlockSpec((tq, V_DIM), lambda h, i, j: (i, h)), pl.BlockSpec((tq, V_DIM), lambda h, i, j: (i, h))]
        + c_out_specs,
        out_shape=[_sds((n_rows, MLA_WIDTH)), _sds((n_rows, MLA_WIDTH))] + c_outs,
        scratch_shapes=[pltpu.VMEM((tq, 1), F32), pltpu.VMEM((tq, 1), F32), pltpu.VMEM((tq, V_DIM), F32)]
        + (comm.sem_scratch() if comm else []),
        compiler_params=_params(("arbitrary",) * 3 if comm else ("parallel", "parallel", "arbitrary")),
    )(qf, kf, v, *c_ins)


def _mm_res_ln(name, a, w, xres, g, b):
    def epi(acc, xr, g_v, b_v):
        r = ALPHA * xr + acc
        y = _layer_norm(r, g_v, b_v)
        return y, y, r

    return _mm(name, a, w, (F32, BF16, F32), tm=1024, tn=D_MODEL, tk=1024, epilogue=epi, extras=[xres], vecs=[g, b])


def _mm_relu2(x_bf, w, comm=None):
    def epi(acc):
        r = jnp.maximum(acc, 0.0)
        return (r * r,)

    return _mm("ff1", x_bf, w, (BF16,), tm=2048, tn=1024, tk=D_MODEL, epilogue=epi, b_pieces=True, comm=comm)


def _loss_fn(y, t):
    def fn(y_v, t_v):
        d = y_v - t_v
        part = jnp.sum(jnp.sum(d * d, axis=1, keepdims=True), axis=0, keepdims=True)
        return d * (1.0 / D_MODEL), part

    dy, part = _rowwise("loss", fn, [y, t], [_sds(y.shape)], reds=[_sds((1, 1))])
    return part * (0.5 / D_MODEL), dy


def _ln_bwd_rows(dy, r, g):
    mu = jnp.mean(r, axis=-1, keepdims=True)
    xc = r - mu
    rstd = lax.rsqrt(jnp.mean(xc * xc, axis=-1, keepdims=True) + LN_EPS)
    xh = xc * rstd
    dxh = dy * g
    dr = rstd * (dxh - jnp.mean(dxh, axis=-1, keepdims=True) - xh * jnp.mean(dxh * xh, axis=-1, keepdims=True))
    return dr, dr, jnp.sum(dy * xh, axis=0, keepdims=True), jnp.sum(dy, axis=0, keepdims=True)


def _ln_bwd(name, dy, r, g):
    return _rowwise(name, _ln_bwd_rows, [dy, r], [_sds(dy.shape), _sds(dy.shape, BF16)], consts=[g],
                    reds=[_sds((1, D_MODEL)), _sds((1, D_MODEL))])


def _head_sum_matrix():
    i = lax.broadcasted_iota(jnp.int32, (A_WIDTH, LANES), 0) // A_HEAD_DIM
    return (i == _slot_head(lax.broadcasted_iota(jnp.int32, (A_WIDTH, LANES), 1))).astype(BF16)


def _mix_bwd(dmixed, a_out, b_out, ols, a_g, b_g):
    def fn(dm, a, b, ol1, ol2, ol3, ag, bg, sum_mat, spread):
        da, dga = _rms_bwd(a, ag, dm[:, :A_WIDTH])
        db, dgb = _rms_bwd(b, bg, dm[:, A_WIDTH:])
        t = da * a
        t_hi = t.astype(BF16)
        t_lo = (t - t_hi.astype(F32)).astype(BF16)
        tsum = _dot(t_hi, sum_mat, _NN) + _dot(t_lo, sum_mat, _NN)
        tb = db * b
        delta_b = jnp.concatenate(
            [jnp.broadcast_to(jnp.sum(tb[:, h * V_DIM:(h + 1) * V_DIM], axis=-1, keepdims=True), (tb.shape[0], V_DIM))
             for h in range(MLA_HEADS)], axis=1)
        dods = [jnp.concatenate([_pack_halves(_spread_slots(w, spread) * da), w * tsum], axis=1)
                for w in _pattern_weights(ol1, ol2, ol3)]
        return (*dods, db, delta_b, dga, dgb)

    n_rows = a_out.shape[0]
    wide = (n_rows, A_WIDTH)
    dils = [dil for _, dil in PATTERNS]
    return _rowwise("mix_bwd", fn, [dmixed, a_out, b_out, *ols],
                    [_sds((n_rows // d, d * OL_WIDTH)) for d in dils] + [_sds(wide, BF16), _sds(wide)],
                    consts=[a_g, b_g, _head_sum_matrix(), _slot_spread_matrix()],
                    reds=[_sds((1, A_WIDTH)), _sds((1, MLA_WIDTH))], ts=256,
                    in_dils=[1, 1, 1] + dils, out_dils=dils + [1, 1])


def _dil_bwd(qkv_view, ol_view, dod_view, dil, comm=None):
    nb = qkv_view.shape[0] // SPAN
    group = min(dil, DIL_GROUP)
    grid = (dil // group, nb + 1)
    c_ins, c_in_specs, c_outs, c_out_specs, begin, end = _hosted(comm, grid)
    n_ci, n_co = len(c_ins), len(c_outs)

    def body(cur_ref, prev_ref, ol_ref, dod_ref, *refs):
        ci_refs, out_ref, co_refs = refs[:n_ci], refs[n_ci], refs[n_ci + 1:n_ci + 1 + n_co]
        carry, sems = refs[n_ci + 1 + n_co], refs[n_ci + 2 + n_co:]
        m = pl.program_id(1)
        begin(ci_refs, co_refs, sems)

        def write_packed(g, pairs):
            lo = DQKV_WIDTH * g
            for p, (dq_p, dk_p, _) in enumerate(pairs):
                out_ref[:, lo + p * LANES:lo + (p + 1) * LANES] = _pack2(dq_p, dk_p)
            for p in range(2):
                out_ref[:, lo + A_WIDTH + p * LANES:lo + A_WIDTH + (p + 1) * LANES] = _pack2(pairs[p][2], pairs[p + 2][2])

        @pl.when(m == 0)
        def _():
            carry[...] = jnp.zeros(carry.shape, F32)

        @pl.when(m < nb)
        def _():
            mask, even = _band_mask(m, 2), _even_lanes(SPAN)
            for g in range(group):
                done = []
                for p in range(A_HEADS // 2):
                    q_sl, k_sl, v_sl = (slice((3 * g + t) * A_WIDTH + p * LANES, (3 * g + t) * A_WIDTH + (p + 1) * LANES)
                                        for t in range(3))
                    do_sl = slice(OL_WIDTH * g + (p % 2) * LANES, OL_WIDTH * g + (p % 2 + 1) * LANES)
                    s_e, s_o = (OL_WIDTH * g + O_WORDS + _slot_lane(2 * p + t) for t in range(2))
                    q2, do2 = cur_ref[:, q_sl], _unpack2(dod_ref[:, do_sl])[p // 2].astype(BF16)
                    zero = jnp.zeros_like(q2)
                    qcat = jnp.concatenate([jnp.where(even, q2, zero), jnp.where(even, zero, q2)], axis=0)
                    docat = jnp.concatenate([jnp.where(even, do2, zero), jnp.where(even, zero, do2)], axis=0)
                    kcat = jnp.concatenate([prev_ref[:, k_sl], cur_ref[:, k_sl]], axis=0)
                    vcat = jnp.concatenate([prev_ref[:, v_sl], cur_ref[:, v_sl]], axis=0)
                    lse_c = jnp.concatenate([ol_ref[:, s_e:s_e + 1], ol_ref[:, s_o:s_o + 1]], axis=0)
                    dl_c = jnp.concatenate([dod_ref[:, s_e:s_e + 1], dod_ref[:, s_o:s_o + 1]], axis=0)
                    pr = jnp.exp(jnp.where(mask, _dot(qcat, kcat, _NT), NEG) - lse_c)
                    ds = (pr * (_dot(docat, vcat, _NT) - dl_c)).astype(BF16)
                    dq = _dot(ds, kcat, _NN)
                    dk2 = _dot(ds, qcat, _TN)
                    dv2 = _dot(pr.astype(BF16), docat, _TN)
                    done.append((carry[:, q_sl], carry[:, k_sl] + dk2[:SPAN], carry[:, v_sl] + dv2[:SPAN]))
                    carry[:, q_sl] = jnp.where(even, dq[:SPAN], dq[SPAN:])
                    carry[:, k_sl] = dk2[SPAN:]
                    carry[:, v_sl] = dv2[SPAN:]
                write_packed(g, done)

        @pl.when(m == nb)
        def _():
            for g in range(group):
                write_packed(g, [tuple(carry[:, (3 * g + t) * A_WIDTH + p * LANES:(3 * g + t) * A_WIDTH + (p + 1) * LANES]
                                       for t in range(3)) for p in range(A_HEADS // 2)])

        end(ci_refs, co_refs, sems)

    last = nb - 1

    def cur(width):
        return pl.BlockSpec((SPAN, group * width), lambda r, m: (jnp.minimum(m, last), r))

    def prev(width):
        return pl.BlockSpec((SPAN, group * width), lambda r, m: (jnp.clip(m - 1, 0, last), r))

    return pl.pallas_call(
        body, name=f"dil_bwd_d{dil}", grid=grid,
        in_specs=[cur(3 * A_WIDTH), prev(3 * A_WIDTH), cur(OL_WIDTH), cur(OL_WIDTH)] + c_in_specs,
        out_specs=[prev(DQKV_WIDTH)] + c_out_specs,
        out_shape=[_sds((qkv_view.shape[0], dil * DQKV_WIDTH))] + c_outs,
        scratch_shapes=[pltpu.VMEM((SPAN, group * 3 * A_WIDTH), F32)] + (comm.sem_scratch() if comm else []),
        compiler_params=_params(("arbitrary", "arbitrary") if comm else ("parallel", "arbitrary")),
    )(qkv_view, qkv_view, ol_view, dod_view, *c_ins)


def _mla_bwd(qf, kf, v, lse, do, delta, comm=None):
    n_rows = v.shape[0]
    tq, tk = min(MLA_BWD_TILES[0], n_rows), min(MLA_BWD_TILES[1], n_rows)
    nq, nk = n_rows // tq, n_rows // tk

    def first_q(j):
        return jnp.right_shift(j * tk, int(math.log2(tq)))

    grid = (MLA_HEADS, nk, nq)
    c_ins, c_in_specs, c_outs, c_out_specs, begin, end = _hosted(comm, grid)
    n_ci, n_co = len(c_ins), len(c_outs)

    def body(q_ref, k_ref, v_ref, lse_ref, do_ref, dl_ref, *refs):
        ci_refs, (dq_ref, dk_ref, dv_ref), co_refs = refs[:n_ci], refs[n_ci:n_ci + 3], refs[n_ci + 3:n_ci + 3 + n_co]
        dk_sc, dv_sc = refs[n_ci + 3 + n_co:n_ci + 5 + n_co]
        sems = refs[n_ci + 5 + n_co:]
        j, i = pl.program_id(1), pl.program_id(2)
        begin(ci_refs, co_refs, sems)

        @pl.when((j == 0) & (i == 0))
        def _():
            dq_ref[...] = jnp.zeros(dq_ref.shape, F32)

        @pl.when(i == first_q(j))
        def _():
            dk_sc[...] = jnp.zeros(dk_sc.shape, F32)
            dv_sc[...] = jnp.zeros(dv_sc.shape, F32)

        def step(masked):
            q, k, dob = q_ref[...], k_ref[...], do_ref[...]
            s = _dot(q, k, _NT) * MLA_SCALE
            if masked:
                row = lax.broadcasted_iota(jnp.int32, (tq, tk), 0) + i * tq
                col = lax.broadcasted_iota(jnp.int32, (tq, tk), 1) + j * tk
                s = jnp.where(col <= row, s, NEG)
            p = jnp.exp(s - lse_ref[:, :1])
            dp = _dot(dob, v_ref[...], _NT)
            ds = (p * (dp - dl_ref[:, :1]) * MLA_SCALE).astype(BF16)
            dv_sc[...] += _dot(p.astype(BF16), dob, _TN)
            dk_sc[...] += _dot(ds, q, _TN)
            rows = pl.ds(pl.multiple_of(i * tq, tq), tq)
            dq_ref[rows, :] += _dot(ds, k, _NN)

        active = i >= first_q(j)
        crosses = (j + 1) * tk - 1 > i * tq

        @pl.when(active & jnp.logical_not(crosses))
        def _():
            step(False)

        @pl.when(active & crosses)
        def _():
            step(True)

        @pl.when(i == nq - 1)
        def _():
            dk_ref[...] = dk_sc[...]
            dv_ref[...] = dv_sc[...]

        end(ci_refs, co_refs, sems)

    qrow = lambda h, j, i: (jnp.maximum(i, first_q(j)), h)
    return pl.pallas_call(
        body, name="mla_bwd", grid=grid,
        in_specs=[pl.BlockSpec((None, tq, QK_PAD), lambda h, j, i: (h, jnp.maximum(i, first_q(j)), 0)),
                  pl.BlockSpec((None, tk, QK_PAD), lambda h, j, i: (h, j, 0)),
                  pl.BlockSpec((tk, V_DIM), lambda h, j, i: (j, h)),
                  pl.BlockSpec((tq, V_DIM), qrow), pl.BlockSpec((tq, V_DIM), qrow), pl.BlockSpec((tq, V_DIM), qrow)]
        + c_in_specs,
        out_specs=[pl.BlockSpec((None, n_rows, QK_PAD), lambda h, j, i: (h, 0, 0)),
                   pl.BlockSpec((None, tk, QK_PAD), lambda h, j, i: (h, j, 0)),
                   pl.BlockSpec((tk, V_DIM), lambda h, j, i: (j, h))] + c_out_specs,
        out_shape=[_sds((MLA_HEADS, n_rows, QK_PAD)), _sds((MLA_HEADS, n_rows, QK_PAD)), _sds((n_rows, MLA_WIDTH))] + c_outs,
        scratch_shapes=[pltpu.VMEM((tk, QK_PAD), F32), pltpu.VMEM((tk, V_DIM), F32)] + (comm.sem_scratch() if comm else []),
        compiler_params=_params(("arbitrary",) * 3 if comm else ("parallel", "arbitrary", "arbitrary")),
    )(qf, kf, v, lse, do, delta, *c_ins)


def _mla_bwd_prep(dqf, dkf, dv, cos, sin):
    def fn(dqf_v, dkf_v, dv_v, cos_v, sin_v):
        dq = jnp.concatenate([dqf_v[h][:, :QK_NOPE] for h in range(MLA_HEADS)]
                             + [_rope_t(dqf_v[h][:, QK_NOPE:], cos_v[:, :LANES], sin_v[:, :LANES]) for h in range(MLA_HEADS)], axis=1)
        dkv = jnp.concatenate([dkf_v[h][:, :QK_NOPE] for h in range(MLA_HEADS)] + [dv_v], axis=1)
        dkpe = dkf_v[0][:, QK_NOPE:] + dkf_v[1][:, QK_NOPE:] + dkf_v[2][:, QK_NOPE:] + dkf_v[3][:, QK_NOPE:]
        return dq, dkv, dkpe

    n_rows = dv.shape[0]
    return _rowwise("mla_bwd_prep", fn, [dqf, dkf, dv, cos, sin],
                    [_sds((n_rows, UQ_PAD), BF16), _sds((n_rows, 1024), BF16), _sds((n_rows, LANES))], ts=256)


def _assemble_dh(dqkvs, dcqn, cq, dckvn, ckv, dkpe, cos, sin, gq, gkv, comm=None):
    def fn(g1, g2, g3, dcqn_v, cq_v, dckvn_v, ckv_v, dkpe_v, cos_v, sin_v, gq_v, gkv_v):
        dq = dk = dv = 0.0
        for packed in (g1, g2, g3):
            dq_p, dk_p = _unpack2(packed[:, :A_WIDTH])
            dv_p = jnp.concatenate(_unpack2(packed[:, A_WIDTH:]), axis=1)
            dq, dk, dv = dq + dq_p, dk + dk_p, dv + dv_p
        dqa = _rope_t(dq, cos_v, sin_v) * A_SCALE
        dka = _rope_t(dk, cos_v, sin_v)
        dcq, dgq = _rms_bwd(cq_v, gq_v, dcqn_v)
        dckv, dgkv = _rms_bwd(ckv_v, gkv_v, dckvn_v)
        dkr = _rope_t(dkpe_v, cos_v[:, :LANES], sin_v[:, :LANES])
        return jnp.concatenate([dqa, dka, dv, dcq, dkr, dckv], axis=1), dgq, dgkv

    n_rows = cq.shape[0]
    return _rowwise("assemble_dh", fn, [*dqkvs, dcqn, cq, dckvn, ckv, dkpe, cos, sin],
                    [_sds((n_rows, IN_PAD), BF16)], consts=[gq, gkv],
                    reds=[_sds((1, Q_LORA)), _sds((1, KV_LORA))], ts=256,
                    in_dils=[dil for _, dil in PATTERNS] + [1] * 7, comm=comm)


def _layer_fwd(x_f32, x_bf, w, sm, cos, sin, behind=(None, None), rest=None):
    *qkv_views, cq, kpe, ckv = _in_proj(x_bf, w["w_in"], cos, sin)
    ol_views, brought = [], []
    for i, (view, (_, dil)) in enumerate(zip(qkv_views, PATTERNS)):
        ol, *got = _dil_fwd(view, dil, comm=rest[0][i] if rest else None)
        ol_views.append(ol)
        brought.append(got)
    if rest:
        w = {**w, **rest[1](brought)}
    cqn, qf, ckvn, kf, v = _mla_prep(cq, ckv, kpe, cos, sin, sm["q_a_norm"], w["w_uq"], sm["kv_a_norm"], w["w_ukv"])
    b_out, b_lse, *behind_attn = _mla_fwd(qf, kf, v, comm=behind[0])
    a_out, mixed = _mix_fwd(ol_views, b_out, sm["a_out_norm"], sm["b_out_norm"])
    x1, x1_bf, r1 = _mm_res_ln("wo_ln1", mixed, w["w_o"], x_f32, sm["ln1_g"], sm["ln1_b"])
    f, *behind_ff1 = _mm_relu2(x1_bf, w["w_ff1"], comm=behind[1])
    x2, x2_bf, r2 = _mm_res_ln("ff2_ln2", f, w["w_ff2"], x1, sm["ln2_g"], sm["ln2_b"])
    saved = dict(w=w, x_bf=x_bf, qkv_views=qkv_views, ol_views=ol_views, cq=cq, ckv=ckv, cqn=cqn, qf=qf, ckvn=ckvn,
                 kf=kf, v=v, b_out=b_out, b_lse=b_lse, a_out=a_out, mixed=mixed, x1_bf=x1_bf, r1=r1, f=f, r2=r2)
    return x2, x2_bf, saved, (behind_attn, behind_ff1)


def _layer_bwd(dx2, w, sm, sv, cos, sin, reduce_of, pending=None):
    dr2, dr2_bf, dg2, db2 = _ln_bwd("ln2_bwd", dx2, sv["r2"], sm["ln2_g"])
    (dw_ff2,) = _mm("dw_ff2", sv["f"], dr2_bf, (F32,), ta=True, tm=1024, tn=1024, tk=2048)
    du, *swapped = _mm("d_u", dr2_bf, w["w_ff2"], (BF16,), tb=True, tm=1024, tn=2048, tk=D_MODEL,
                       epilogue=lambda acc, f: (acc * (2.0 * jnp.sqrt(f.astype(F32))),), extras=[sv["f"]],
                       comm=_swap_comm(pending.pieces) if pending else None)
    if pending:
        pending.after_swap(swapped)
    (dw_ff1,) = _mm("dw_ff1", sv["x1_bf"], du, (F32,), ta=True, tm=1024, tn=1024, tk=2048, out_pieces=True)
    (dy1,) = _mm("d_x1", du, w["w_ff1"], (F32,), tb=True, tm=1024, tn=1024, tk=2048,
                 epilogue=lambda acc, d: (acc + ALPHA * d,), extras=[dr2], b_pieces=True)
    dr1, dr1_bf, dg1, db1 = _ln_bwd("ln1_bwd", dy1, sv["r1"], sm["ln1_g"])
    (dw_o,) = _mm("dw_o", sv["mixed"], dr1_bf, (F32,), ta=True, tm=1024, tn=1024, tk=2048)
    early = reduce_of(dict(w_o=dw_o, w_ff1=dw_ff1, w_ff2=dw_ff2))
    (dmixed,) = _mm("d_mixed", dr1_bf, w["w_o"], (F32,), tb=True, tm=1024, tn=1024, tk=D_MODEL)
    dod1, dod2, dod3, do_b, delta_b, dga, dgb = _mix_bwd(
        dmixed, sv["a_out"], sv["b_out"], sv["ol_views"], sm["a_out_norm"], sm["b_out_norm"])
    dqkvs = []
    for i, ((_, dil), qkv_view, ol_view, dod_view) in enumerate(zip(PATTERNS, sv["qkv_views"], sv["ol_views"], (dod1, dod2, dod3))):
        swap_here = early is not None and i == 0
        dqkv, *swapped = _dil_bwd(qkv_view, ol_view, dod_view, dil, comm=_swap_comm(early.pieces) if swap_here else None)
        dqkvs.append(dqkv)
        if swap_here:
            early.after_swap(swapped)
    to_scatter = (early.s16 if early else []) + (pending.s16 if pending else [])
    dqf, dkf, dv_b, *scattered = _mla_bwd(sv["qf"], sv["kf"], sv["v"], sv["b_lse"], do_b, delta_b,
                                          comm=_scatter_comm(to_scatter) if to_scatter else None)
    if early:
        early.after_scatter(scattered[:len(early.s16)])
    if pending:
        pending.after_scatter(scattered[len(scattered) - len(pending.s16):])
    dq_b, dkv_b, dkpe = _mla_bwd_prep(dqf, dkf, dv_b, cos, sin)
    (dw_uq,) = _mm("dw_uq", sv["cqn"], dq_b, (F32,), ta=True, tm=Q_LORA, tn=1024, tk=2048)
    (dcqn,) = _mm("d_cqn", dq_b, w["w_uq"], (F32,), tb=True, tm=1024, tn=Q_LORA, tk=UQ_PAD)
    (dw_ukv,) = _mm("dw_ukv", sv["ckvn"], dkv_b, (F32,), ta=True, tm=KV_LORA, tn=1024, tk=2048)
    (dckvn,) = _mm("d_ckvn", dkv_b, w["w_ukv"], (F32,), tb=True, tm=1024, tn=KV_LORA, tk=1024)
    to_share = (early.mine if early else []) + (pending.mine if pending else [])
    dh, dgq, dgkv, *shared = _assemble_dh(dqkvs, dcqn, sv["cq"], dckvn, sv["ckv"], dkpe, cos, sin,
                                          sm["q_a_norm"], sm["kv_a_norm"], comm=_share_comm(to_share) if to_share else None)
    if early:
        early.after_share(shared[:len(early.mine)])
    if pending:
        pending.after_share(shared[len(shared) - len(pending.mine):])
    (dw_in,) = _mm("dw_in", sv["x_bf"], dh, (F32,), ta=True, tm=1024, tn=1024, tk=2048)
    (dx,) = _mm("d_x", dh, w["w_in"], (F32,), tb=True, tm=1024, tn=1024, tk=2048,
                epilogue=lambda acc, d: (acc + ALPHA * d,), extras=[dr1])
    late = reduce_of(dict(w_in=dw_in, w_uq=dw_uq, w_ukv=dw_ukv))
    dsm = dict(q_a_norm=dgq, kv_a_norm=dgkv, a_out_norm=dga, b_out_norm=dgb, ln1_g=dg1, ln1_b=db1, ln2_g=dg2, ln2_b=db2)
    return dx, dsm, late


def _pad_w_in(w):
    return jnp.concatenate([w[:, :1792], w[:, 1920:1984], jnp.zeros((w.shape[0], 64), w.dtype), w[:, 1792:1920]], axis=1)


def _unpad_w_in(w):
    return jnp.concatenate([w[:, :1792], w[:, 1920:2048], w[:, 1792:1856]], axis=1)


def _rope_tables(n_rows):
    half = A_HEAD_DIM // 2
    inv_freq = ROPE_THETA ** (-jnp.arange(half, dtype=F32) / half)
    ang = jnp.arange(n_rows, dtype=F32)[:, None] * inv_freq[None, :]
    cos = jnp.tile(jnp.cos(ang), (1, 2 * A_HEADS))
    sin = jnp.tile(jnp.concatenate([-jnp.sin(ang), jnp.sin(ang)], axis=1), (1, A_HEADS))
    return cos, sin


W_NAMES = ("w_in", "w_uq", "w_ukv", "w_o", "w_ff1", "w_ff2")
PIECE = dict(w_in=(512, 512), w_uq=(128, 256), w_ukv=(64, 256), w_o=(128, 1024), w_ff1=(512, 1024), w_ff2=(512, 1024))
SHARD_COLS = dict(w_in=496, w_uq=192, w_ukv=256, w_o=1024, w_ff1=1024, w_ff2=1024)


def _col_pieces(t, pad_to):
    rows, _, cols = t.shape
    t = jnp.pad(t, ((0, 0), (0, 0), (0, pad_to - cols)))
    return t.reshape(2, rows // 2, N_CHIPS, pad_to).transpose(2, 0, 1, 3)


def _grad_pieces(dws):
    def heads(t, rows):
        return jnp.concatenate([t[:, :512].reshape(rows, MLA_HEADS, LANES), t[:, 512:].reshape(rows, MLA_HEADS, LANES)], axis=2)

    make = dict(
        w_in=lambda t: _col_pieces(_unpad_w_in(t).reshape(D_MODEL, N_CHIPS, SHARD_COLS["w_in"]), 512),
        w_uq=lambda t: _col_pieces(heads(t, Q_LORA), 256),
        w_ukv=lambda t: _col_pieces(heads(t, KV_LORA), 256),
        w_o=lambda t: t.reshape(N_CHIPS, 2, 128, D_MODEL),
        w_ff1=lambda t: t,
        w_ff2=lambda t: t.reshape(N_CHIPS, 2, 512, D_MODEL))
    return {n: make[n](t) for n, t in dws.items()}


def _weights_from_pieces(p):
    def cols(t):
        return t.transpose(1, 2, 0, 3).reshape(2 * t.shape[2], N_CHIPS, t.shape[3])

    def heads(t, rows):
        return jnp.concatenate([t[:, :, :LANES].reshape(rows, 512), t[:, :, LANES:].reshape(rows, 512)], axis=1)

    make = dict(
        w_in=lambda t: _pad_w_in(cols(t)[:, :, :SHARD_COLS["w_in"]].reshape(D_MODEL, IN_COLS)),
        w_uq=lambda t: heads(cols(t), Q_LORA),
        w_ukv=lambda t: heads(cols(t), KV_LORA),
        w_o=lambda t: t.reshape(D_MODEL, D_MODEL),
        w_ff1=lambda t: t,
        w_ff2=lambda t: t.reshape(D_FF, D_MODEL))
    return {n: make[n](t) for n, t in p.items()}


def _my_piece(name, shard, half):
    rows, cols = shard.shape[0] // 2, shard.shape[1]
    t = lax.dynamic_slice_in_dim(shard, half * rows, rows, axis=0).astype(BF16)
    return jnp.pad(t, ((0, 0), (0, PIECE[name][1] - cols)))


MESH = pl.DeviceIdType.MESH
ANY = pl.BlockSpec(memory_space=pl.ANY)


class _Comm:
    def __init__(self, ins, out_shapes, n_sems, start, finish):
        self.ins, self.out_shapes, self.n_sems, self.start, self.finish = list(ins), list(out_shapes), n_sems, start, finish

    def sem_scratch(self):
        return [pltpu.SemaphoreType.DMA((self.n_sems,)), pltpu.SemaphoreType.DMA((self.n_sems,))]


def _comm_call(name, comm):
    n_in, n_out = len(comm.ins), len(comm.out_shapes)

    def body(*refs):
        ins, outs, sems = refs[:n_in], refs[n_in:n_in + n_out], refs[n_in + n_out:]
        comm.start(ins, outs, *sems)
        comm.finish(ins, outs, *sems)

    return pl.pallas_call(body, name=name, out_shape=comm.out_shapes, in_specs=[ANY] * n_in, out_specs=[ANY] * n_out,
                          scratch_shapes=comm.sem_scratch())(*comm.ins)


def _all_gather_comm(blocks):
    n = len(blocks)

    def plan(x_refs, out_refs, send_sems, recv_sems):
        x, y, c = lax.axis_index("x"), lax.axis_index("y"), lax.axis_index("c")
        me, sibling = (x, y, c), (x, y, 1 - c)
        chips = [(1 - x, y), (x, 1 - y), (1 - x, 1 - y)]

        def copy(t, k, blk, to, src=None):
            px, py, pc = blk
            slot = out_refs[t].at[4 * px + 2 * py + pc]
            return pltpu.make_async_remote_copy(
                src_ref=slot if src is None else src, dst_ref=slot,
                send_sem=send_sems.at[7 * t + k], recv_sem=recv_sems.at[7 * t + k], device_id=to, device_id_type=MESH)

        first = []
        for t in range(n):
            first.append(copy(t, 0, me, sibling, src=x_refs[t]))
            first += [copy(t, 1 + j, me, (*chip, c), src=x_refs[t]) for j, chip in enumerate(chips)]
        return me, sibling, chips, c, copy, first

    def start(x_refs, out_refs, send_sems, recv_sems):
        for cp in plan(x_refs, out_refs, send_sems, recv_sems)[-1]:
            cp.start()

    def finish(x_refs, out_refs, send_sems, recv_sems):
        me, sibling, chips, c, copy, sent = plan(x_refs, out_refs, send_sems, recv_sems)
        for j, chip in enumerate(chips):
            for t in range(n):
                copy(t, 1 + j, (*chip, c), me).wait_recv()
                sent.append(copy(t, 4 + j, (*chip, c), sibling))
                sent[-1].start()
        for t in range(n):
            copy(t, 0, sibling, me).wait_recv()
        for j, chip in enumerate(chips):
            for t in range(n):
                copy(t, 4 + j, (*chip, 1 - c), me).wait_recv()
        for cp in sent:
            cp.wait_send()

    return _Comm(blocks, [_sds((N_DEV,) + b.shape, b.dtype) for b in blocks], 7 * n, start, finish)


def _simple_comm(ins, out_shapes, n_sems, copies):
    def start(in_refs, out_refs, send_sems, recv_sems):
        for cp in copies(in_refs, out_refs, send_sems, recv_sems):
            cp.start()

    def finish(in_refs, out_refs, send_sems, recv_sems):
        for cp in copies(in_refs, out_refs, send_sems, recv_sems):
            cp.wait()

    return _Comm(ins, out_shapes, n_sems, start, finish)


def _swap_comm(gs):
    def copies(g_refs, got_refs, send_sems, recv_sems):
        c = lax.axis_index("c")
        sibling = (lax.axis_index("x"), lax.axis_index("y"), 1 - c)
        return [pltpu.make_async_remote_copy(src_ref=g_refs[t].at[k, 1 - c], dst_ref=got_refs[t].at[k],
                                             send_sem=send_sems.at[N_CHIPS * t + k], recv_sem=recv_sems.at[N_CHIPS * t + k],
                                             device_id=sibling, device_id_type=MESH)
                for t in range(len(gs)) for k in range(N_CHIPS)]

    return _simple_comm(gs, [_sds((N_CHIPS,) + g.shape[2:], g.dtype) for g in gs], N_CHIPS * len(gs), copies)


def _scatter_comm(hs):
    def copies(h_refs, got_refs, send_sems, recv_sems):
        x, y, c = lax.axis_index("x"), lax.axis_index("y"), lax.axis_index("c")
        chips = [(1 - x, y), (x, 1 - y), (1 - x, 1 - y)]
        return [pltpu.make_async_remote_copy(src_ref=h_refs[t].at[2 * px + py], dst_ref=got_refs[t].at[rel],
                                             send_sem=send_sems.at[3 * t + rel], recv_sem=recv_sems.at[3 * t + rel],
                                             device_id=(px, py, c), device_id_type=MESH)
                for t in range(len(hs)) for rel, (px, py) in enumerate(chips)]

    return _simple_comm(hs, [_sds((3,) + h.shape[1:], h.dtype) for h in hs], 3 * len(hs), copies)


def _share_comm(rs):
    def copies(r_refs, got_refs, send_sems, recv_sems):
        sibling = (lax.axis_index("x"), lax.axis_index("y"), 1 - lax.axis_index("c"))
        return [pltpu.make_async_remote_copy(src_ref=r_refs[t], dst_ref=got_refs[t], send_sem=send_sems.at[t],
                                             recv_sem=recv_sems.at[t], device_id=sibling, device_id_type=MESH)
                for t in range(len(rs))]

    return _simple_comm(rs, [_sds(r.shape, r.dtype) for r in rs], len(rs), copies)


SUM_STEPS = 4


def _pair_sums(name, gs, gots, where):
    n = len(gs)
    dims = [(g.shape[2] // SUM_STEPS, g.shape[3]) for g in gs]

    def body(where_ref, *refs):
        own = pl.program_id(1) == where_ref[1]
        for t in range(n):
            s = refs[t][...] + refs[n + t][...]
            refs[3 * n + t][...] = s.astype(BF16)

            @pl.when(own)
            def _():
                refs[2 * n + t][...] = s

    def part(tr, pc):
        return pl.BlockSpec((None, tr, pc), lambda i, k, where_ref: (k, i, 0))

    def kept(tr, pc):
        return pl.BlockSpec((None, None, tr, pc), lambda i, k, where_ref: (k, where_ref[0], i, 0))

    outs = pl.pallas_call(
        body, name=name,
        out_shape=[_sds(g.shape[2:]) for g in gs] + [_sds((N_CHIPS,) + g.shape[2:], BF16) for g in gs],
        grid_spec=pltpu.PrefetchScalarGridSpec(
            num_scalar_prefetch=1, grid=(SUM_STEPS, N_CHIPS),
            in_specs=[kept(*d) for d in dims] + [part(*d) for d in dims],
            out_specs=[pl.BlockSpec(d, lambda i, k, where_ref: (i, 0)) for d in dims] + [part(*d) for d in dims]),
        compiler_params=_params(("parallel", "arbitrary")),
    )(where, *gs, *gots)
    return list(outs[:n]), list(outs[n:])


def _chips_sums(name, owns, gots):
    n = len(owns)
    dims = [(h.shape[0] // SUM_STEPS, h.shape[1]) for h in owns]

    def body(*refs):
        for t in range(n):
            got_ref = refs[n + t]
            refs[2 * n + t][...] = refs[t][...] + got_ref[0] + got_ref[1] + got_ref[2]

    row = [pl.BlockSpec((tr, pc), lambda i: (i, 0)) for tr, pc in dims]
    return pl.pallas_call(
        body, name=name, out_shape=[_sds(h.shape) for h in owns], grid=(SUM_STEPS,),
        in_specs=row + [pl.BlockSpec((3, tr, pc), lambda i: (0, i, 0)) for tr, pc in dims], out_specs=row,
        compiler_params=_params(("parallel",)),
    )(*owns, *gots)


class _Reduce:
    def __init__(self, tag, dws, c, chip):
        gp = _grad_pieces(dws)
        self.tag, self.c, self.names, self.pieces = tag, c, list(gp), list(gp.values())
        self.where = jnp.stack([c, chip]).astype(jnp.int32)
        self.shards = None

    def after_swap(self, gots):
        self.s32, self.s16 = _pair_sums(f"rs_pair_sum_{self.tag}", self.pieces, gots, self.where)

    def after_scatter(self, parts):
        self.mine = list(_chips_sums(f"rs_chips_sum_{self.tag}", self.s32, parts))

    def after_share(self, theirs):
        self.shards = {}
        for n, a, b in zip(self.names, self.mine, theirs):
            both = jnp.where(self.c == 0, jnp.concatenate([a, b], axis=0), jnp.concatenate([b, a], axis=0))
            self.shards[n] = both[:, :SHARD_COLS[n]]

    def run_alone(self):
        self.after_swap(_comm_call(f"rs_pair_{self.tag}", _swap_comm(self.pieces)))
        self.after_scatter(_comm_call(f"rs_chips_{self.tag}", _scatter_comm(self.s16)))
        self.after_share(_comm_call(f"rs_share_{self.tag}", _share_comm(self.mine)))


class _LayerWeights:
    def __init__(self, shards, c, dev):
        self.shards, self.c, self.dev = shards, c, dev

    def comm(self, l, names=W_NAMES):
        return _all_gather_comm([_my_piece(n, self.shards[l][n], self.c) for n in names])

    def take(self, l, gathered, names=W_NAMES):
        pieces = {}
        for n, g in zip(names, gathered):
            mine = _my_piece(n, self.shards[l][n], self.c)
            pieces[n] = lax.dynamic_update_index_in_dim(g, mine, self.dev, 0).reshape((N_CHIPS, 2) + mine.shape)
        return _weights_from_pieces(pieces)

    def first(self):
        now, later = ("w_in", "w_uq", "w_ukv"), ("w_ff1", "w_ff2", "w_o")
        weights = self.take(0, _comm_call("ag_l0", self.comm(0, now)), now)
        comms = [self.comm(0, (n,)) for n in later]

        def take_rest(results):
            rest = {}
            for n, gathered in zip(later, results):
                rest.update(self.take(0, gathered, (n,)))
            return rest

        return weights, (comms, take_rest)


def _all_reduce_small(vec):
    rows, lanes = vec.shape

    def gather_body(x_ref, out_ref, send_sems, recv_sems):
        x, y, c = lax.axis_index("x"), lax.axis_index("y"), lax.axis_index("c")
        me, sibling = (x, y, c), (x, y, 1 - c)
        chips = [(1 - x, y), (x, 1 - y), (1 - x, 1 - y)]

        def slot(px, py, pc):
            return out_ref.at[4 * px + 2 * py + pc]

        def copy(k, blk, to, src=None):
            return pltpu.make_async_remote_copy(
                src_ref=slot(*blk) if src is None else src, dst_ref=slot(*blk),
                send_sem=send_sems.at[k], recv_sem=recv_sems.at[k], device_id=to, device_id_type=MESH)

        out_ref[4 * x + 2 * y + c] = x_ref[...]
        first = [copy(0, me, sibling, src=x_ref)]
        first += [copy(1 + j, me, (*chip, c), src=x_ref) for j, chip in enumerate(chips)]
        for cp in first:
            cp.start()
        passed = [copy(4 + j, (*chip, c), sibling) for j, chip in enumerate(chips)]
        for j, chip in enumerate(chips):
            copy(1 + j, (*chip, c), me).wait_recv()
            passed[j].start()
        copy(0, sibling, me).wait_recv()
        for j, chip in enumerate(chips):
            copy(4 + j, (*chip, 1 - c), me).wait_recv()
        for cp in first + passed:
            cp.wait_send()

    vmem = pl.BlockSpec(memory_space=pltpu.VMEM)
    allv = pl.pallas_call(
        gather_body, name="small_all_gather", out_shape=_sds((N_DEV, rows, lanes)),
        in_specs=[vmem], out_specs=vmem,
        scratch_shapes=[pltpu.SemaphoreType.DMA((7,)), pltpu.SemaphoreType.DMA((7,))],
    )(vec)

    def sum_body(a_ref, o_ref):
        acc = a_ref[0]
        for d in range(1, N_DEV):
            acc = acc + a_ref[d]
        o_ref[...] = acc

    return pl.pallas_call(sum_body, name="small_sum", out_shape=_sds((rows, lanes)), in_specs=[vmem], out_specs=vmem)(allv)


def _adamw(name, w, g, m, v, ts=512, comm=None):
    def fn(w_v, g_v, m_v, v_v):
        m_n = ADAM_B1 * m_v + (1.0 - ADAM_B1) * g_v
        v_n = ADAM_B2 * v_v + (1.0 - ADAM_B2) * (g_v * g_v)
        m_hat = m_n / (1.0 - ADAM_B1 ** ADAM_STEP)
        v_hat = v_n / (1.0 - ADAM_B2 ** ADAM_STEP)
        delta = -ADAM_LR * (m_hat / (jnp.sqrt(v_hat) + ADAM_EPS) + ADAM_WD * w_v)
        return delta, m_n, v_n

    return _rowwise(name, fn, [w, g, m, v], [_sds(w.shape)] * 3, ts=ts, comm=comm)


def _pack_small(per_layer):
    flat = jnp.concatenate([per_layer[l][n].reshape(-1) for l in range(DEPTH) for n in SMALL_NAMES])
    return jnp.pad(flat, (0, SMALL_ROWS * LANES - flat.shape[0])).reshape(SMALL_ROWS, LANES)


def _unpack_small(packed):
    flat = packed.reshape(-1)
    per = sum(SMALL_SIZES)
    out = {}
    for n, size, off in zip(SMALL_NAMES, SMALL_SIZES, [sum(SMALL_SIZES[:i]) for i in range(len(SMALL_SIZES))]):
        out[n] = jnp.stack([flat[l * per + off:l * per + off + size] for l in range(DEPTH)])
    return out


NEXT_WEIGHTS_BEHIND = (("w_ff1", "w_ff2"), ("w_in", "w_uq", "w_ukv", "w_o"))


def _fwd_bwd(x, target, layer_weights, smalls, on_layer_grads):
    depth = len(smalls)
    cos, sin = _rope_tables(x.shape[0])
    h_f32, h_bf = x, x.astype(BF16)
    saved = []
    w, rest = layer_weights.first()
    for l in range(depth):
        more = l + 1 < depth
        behind = [layer_weights.comm(l + 1, names) if more else None for names in NEXT_WEIGHTS_BEHIND]
        h_f32, h_bf, sv, brought = _layer_fwd(h_f32, h_bf, w, smalls[l], cos, sin, behind, rest)
        saved.append(sv)
        if more:
            w, rest = {}, None
            for names, got in zip(NEXT_WEIGHTS_BEHIND, brought):
                w.update(layer_weights.take(l + 1, got, names))
    loss_part, dh = _loss_fn(h_f32, target)
    small_grads, pending = [None] * depth, None
    for l in reversed(range(depth)):
        dh, small_grads[l], pending = _layer_bwd(dh, saved[l]["w"], smalls[l], saved[l], cos, sin,
                                                 lambda dws, l=l: on_layer_grads(l, dws), pending)
    return loss_part, dh, small_grads, pending


def kernel(x, w_in, q_a_norm, kv_a_norm, w_uq, w_ukv, a_out_norm, b_out_norm, w_o, ln1_g, ln1_b, w_ff1, w_ff2, ln2_g, ln2_b, loss_target, m_w_in, m_q_a_norm, m_kv_a_norm, m_w_uq, m_w_ukv, m_a_out_norm, m_b_out_norm, m_w_o, m_ln1_g, m_ln1_b, m_w_ff1, m_w_ff2, m_ln2_g, m_ln2_b, v_w_in, v_q_a_norm, v_kv_a_norm, v_w_uq, v_w_ukv, v_a_out_norm, v_b_out_norm, v_w_o, v_ln1_g, v_ln1_b, v_w_ff1, v_w_ff2, v_ln2_g, v_ln2_b):
    c = lax.axis_index("c")
    chip = 2 * lax.axis_index("x") + lax.axis_index("y")
    big = dict(w_in=w_in, w_uq=w_uq, w_ukv=w_ukv, w_o=w_o, w_ff1=w_ff1, w_ff2=w_ff2)
    big_m = dict(w_in=m_w_in, w_uq=m_w_uq, w_ukv=m_w_ukv, w_o=m_w_o, w_ff1=m_w_ff1, w_ff2=m_w_ff2)
    big_v = dict(w_in=v_w_in, w_uq=v_w_uq, w_ukv=v_w_ukv, w_o=v_w_o, w_ff1=v_w_ff1, w_ff2=v_w_ff2)
    small = dict(q_a_norm=q_a_norm, kv_a_norm=kv_a_norm, a_out_norm=a_out_norm, b_out_norm=b_out_norm,
                 ln1_g=ln1_g, ln1_b=ln1_b, ln2_g=ln2_g, ln2_b=ln2_b)
    small_m = dict(q_a_norm=m_q_a_norm, kv_a_norm=m_kv_a_norm, a_out_norm=m_a_out_norm, b_out_norm=m_b_out_norm,
                   ln1_g=m_ln1_g, ln1_b=m_ln1_b, ln2_g=m_ln2_g, ln2_b=m_ln2_b)
    small_v = dict(q_a_norm=v_q_a_norm, kv_a_norm=v_kv_a_norm, a_out_norm=v_a_out_norm, b_out_norm=v_b_out_norm,
                   ln1_g=v_ln1_g, ln1_b=v_ln1_b, ln2_g=v_ln2_g, ln2_b=v_ln2_b)

    layer_weights = _LayerWeights([{n: big[n][l] for n in W_NAMES} for l in range(DEPTH)], c, 2 * chip + c)
    smalls = [{n: small[n][l][None, :] for n in SMALL_NAMES} for l in range(DEPTH)]
    reductions = [[] for _ in range(DEPTH)]

    def reduce_layer(l, dws):
        reductions[l].append(_Reduce(f"l{l}_{len(reductions[l])}", dws, c, chip))
        return reductions[l][-1]

    loss_part, grad_x, small_grads, last = _fwd_bwd(x[0], loss_target[0], layer_weights, smalls, reduce_layer)
    loss = lax.psum(loss_part[0, 0], ("x", "y", "c"))
    grad_x = grad_x[None]

    g_small_packed = _all_reduce_small(_pack_small(small_grads))
    pack_in = lambda d: _pack_small([{n: d[n][l] for n in SMALL_NAMES} for l in range(DEPTH)])
    sd, sm_, sv_ = _adamw("adamw_small", pack_in(small), g_small_packed, pack_in(small_m), pack_in(small_v), ts=SMALL_ROWS)
    g_small, d_small, m_small, v_small = (_unpack_small(t) for t in (g_small_packed, sd, sm_, sv_))

    last.run_alone()
    g_big, d_big, m_big, v_big = {}, {}, {}, {}
    for n in W_NAMES:
        g = jnp.stack([next(r.shards[n] for r in reductions[l] if n in r.names) for l in range(DEPTH)])
        shape = g.shape
        flat = lambda t: t.reshape(shape[0] * shape[1], shape[2])
        d_, m_, v_ = _adamw(f"adamw_{n}", flat(big[n]), flat(g), flat(big_m[n]), flat(big_v[n]))
        g_big[n], d_big[n], m_big[n], v_big[n] = g, d_.reshape(shape), m_.reshape(shape), v_.reshape(shape)

    order = ("w_in", "q_a_norm", "kv_a_norm", "w_uq", "w_ukv", "a_out_norm", "b_out_norm", "w_o", "ln1_g", "ln1_b",
             "w_ff1", "w_ff2", "ln2_g", "ln2_b")
    pick = lambda bigd, smalld: [bigd[n] if n in bigd else smalld[n] for n in order]
    return (loss, grad_x, *pick(g_big, g_small), *pick(d_big, d_small), *pick(m_big, m_small), *pick(v_big, v_small))
```

```python
import math

import jax
import jax.numpy as jnp
from jax import lax
from jax.experimental import pallas as pl
from jax.experimental.pallas import tpu as pltpu

F32, BF16 = jnp.float32, jnp.bfloat16

D_MODEL = 1024
DEPTH = 4
A_HEAD_DIM = 64
A_HEADS = 8
A_WIDTH = 512
PATTERNS = ((128, 1), (512, 4), (2048, 16))
SPAN = 128
MLA_HEADS = 4
QK_NOPE = 128
QK_ROPE = 64
V_DIM = 128
Q_LORA = 256
KV_LORA = 128
MLA_WIDTH = 512
QK_PAD = 256
IN_COLS = 1984
IN_PAD = 2048
UQ_PAD = 1024
D_FF = 4096
ROPE_THETA = 10000.0
ALPHA = (2.0 * DEPTH) ** 0.25
LN_EPS = 1e-5
RMS_EPS = 1e-6
MLA_SCALE = (QK_NOPE + QK_ROPE) ** -0.5
A_SCALE = A_HEAD_DIM ** -0.5

ADAM_LR, ADAM_B1, ADAM_B2, ADAM_EPS, ADAM_WD, ADAM_STEP = 0.001, 0.9, 0.999, 1e-08, 0.01, 10

VMEM_LIMIT_BYTES = 56 * 1024 * 1024
NEG = -1e30
N_CHIPS, N_DEV = 4, 8
LANES = 128

SMALL_NAMES = ("q_a_norm", "kv_a_norm", "a_out_norm", "b_out_norm", "ln1_g", "ln1_b", "ln2_g", "ln2_b")
SMALL_SIZES = (256, 128, 512, 512, 1024, 1024, 1024, 1024)
SMALL_ROWS = 176


def _params(sem):
    return pltpu.CompilerParams(dimension_semantics=sem, vmem_limit_bytes=VMEM_LIMIT_BYTES)


FAST_STRIDE = 4


def _interleave(parts, scratch):
    d, (n, width) = len(parts), parts[0].shape
    if d > FAST_STRIDE and d % FAST_STRIDE == 0:
        inner = [_interleave(parts[r::FAST_STRIDE], scratch) for r in range(FAST_STRIDE)]
        return _interleave(inner, scratch)
    for r in range(d):
        for s in range(width // LANES):
            scratch.at[s][pl.ds(r, n, stride=d), :] = parts[r][:, s * LANES:(s + 1) * LANES]
    return jnp.concatenate([scratch.at[s][pl.ds(0, n * d), :] for s in range(width // LANES)], axis=1)


def _deinterleave(x, d, scratch):
    n, width = x.shape[0] // d, x.shape[1]
    if d > FAST_STRIDE and d % FAST_STRIDE == 0:
        outer = _deinterleave(x, FAST_STRIDE, scratch)
        inner = [_deinterleave(y, d // FAST_STRIDE, scratch) for y in outer]
        return [inner[r % FAST_STRIDE][r // FAST_STRIDE] for r in range(d)]
    for s in range(width // LANES):
        scratch.at[s][pl.ds(0, n * d), :] = x[:, s * LANES:(s + 1) * LANES]
    return [jnp.concatenate([scratch.at[s][pl.ds(r, n, stride=d), :] for s in range(width // LANES)], axis=1) for r in range(d)]


def _rowwise(name, fn, ins, outs, consts=(), reds=(), ts=512, in_dils=None, out_dils=None, comm=None):
    in_dils = list(in_dils or [1] * len(ins))
    out_dils = list(out_dils or [1] * len(outs))
    n_rows = ins[0].shape[-2] * in_dils[0]
    ts = min(ts, n_rows)
    assert n_rows % ts == 0
    n_in, n_c, n_o, n_r = len(ins), len(consts), len(outs), len(reds)
    viewed = [(a.shape, d) for a, d in zip(list(ins) + list(outs), in_dils + out_dils) if d > 1]
    grid = (n_rows // ts,)
    h_ins, h_in_specs, h_outs, h_out_specs, begin, end = _hosted(comm, grid)
    n_hi, n_ho = len(h_ins), len(h_outs)

    def tile_spec(shape, d=1):
        if len(shape) == 2:
            return pl.BlockSpec((ts // d, shape[1]), lambda i: (i, 0))
        return pl.BlockSpec((shape[0], ts, shape[2]), lambda i: (0, i, 0))

    def whole_spec(shape):
        return pl.BlockSpec(shape, lambda i: (0,) * len(shape))

    def body(*refs):
        in_refs, c_refs, hi_refs = refs[:n_in], refs[n_in:n_in + n_c], refs[n_in + n_c:n_in + n_c + n_hi]
        refs = refs[n_in + n_c + n_hi:]
        o_refs, r_refs, ho_refs = refs[:n_o], refs[n_o:n_o + n_r], refs[n_o + n_r:n_o + n_r + n_ho]
        scratches = list(refs[n_o + n_r + n_ho:])
        sems = [scratches.pop(), scratches.pop()][::-1] if comm else []
        begin(hi_refs, ho_refs, sems)
        vals = []
        for r, d in zip(in_refs, in_dils):
            if d == 1:
                vals.append(r[...])
            else:
                width = r.shape[1] // d
                vals.append(_interleave([r[:, k * width:(k + 1) * width] for k in range(d)], scratches.pop(0)))
        res = fn(*vals, *[r[...] for r in c_refs])
        res = tuple(res) if isinstance(res, (tuple, list)) else (res,)
        for r, v, d in zip(o_refs, res[:n_o], out_dils):
            if len(r.shape) == 3:
                for g in range(r.shape[0]):
                    r[g] = v[g].astype(r.dtype)
            elif d == 1:
                r[...] = v.astype(r.dtype)
            else:
                width = r.shape[1] // d
                for k, part in enumerate(_deinterleave(v, d, scratches.pop(0))):
                    r[:, k * width:(k + 1) * width] = part.astype(r.dtype)
        if n_r:
            i = pl.program_id(0)

            @pl.when(i == 0)
            def _():
                for r, v in zip(r_refs, res[n_o:]):
                    r[...] = v

            @pl.when(i > 0)
            def _():
                for r, v in zip(r_refs, res[n_o:]):
                    r[...] += v
        end(hi_refs, ho_refs, sems)

    out = pl.pallas_call(
        body, name=name, grid=grid,
        in_specs=[tile_spec(a.shape, d) for a, d in zip(ins, in_dils)] + [whole_spec(c.shape) for c in consts] + h_in_specs,
        out_specs=[tile_spec(o.shape, d) for o, d in zip(outs, out_dils)] + [whole_spec(r.shape) for r in reds] + h_out_specs,
        out_shape=list(outs) + list(reds) + h_outs,
        scratch_shapes=[pltpu.VMEM((shape[1] // d // LANES, ts, LANES), F32) for shape, d in viewed]
        + (comm.sem_scratch() if comm else []),
        compiler_params=_params(("arbitrary",)),
    )(*ins, *consts, *h_ins)
    return out


def _sds(shape, dtype=F32):
    return jax.ShapeDtypeStruct(tuple(shape), dtype)


def _hosted(comm, grid):
    if comm is None:
        return [], [], [], [], (lambda *a: None), (lambda *a: None)

    def edge(which, at, ins, outs, sems):
        ids = [pl.program_id(d) for d in range(len(grid))]
        hit = ids[0] == at[0]
        for i, v in zip(ids[1:], at[1:]):
            hit = hit & (i == v)

        @pl.when(hit)
        def _():
            which(ins, outs, *sems)

    begin = lambda ins, outs, sems: edge(comm.start, [0] * len(grid), ins, outs, sems)
    end = lambda ins, outs, sems: edge(comm.finish, [g - 1 for g in grid], ins, outs, sems)
    return comm.ins, [ANY] * len(comm.ins), comm.out_shapes, [ANY] * len(comm.out_shapes), begin, end


def _mm(name, a, b, out_dtypes, *, ta=False, tb=False, tm=512, tn=512, tk=512, epilogue=None, extras=(), vecs=(),
        comm=None, b_pieces=False, out_pieces=False):
    (k_dim, m_dim) = a.shape if ta else a.shape[::-1]
    if b_pieces:
        b_rows, b_cols = 2 * b.shape[2], N_CHIPS * b.shape[3]
        (n_dim, k2) = (b_rows, b_cols) if tb else (b_cols, b_rows)
    else:
        (n_dim, k2) = b.shape if tb else b.shape[::-1]
    assert k_dim == k2
    tm, tn, tk = min(tm, m_dim), min(tn, n_dim), min(tk, k_dim)
    assert m_dim % tm == 0 and n_dim % tn == 0 and k_dim % tk == 0
    nk = k_dim // tk
    grid = (m_dim // tm, n_dim // tn, nk)
    a_spec = pl.BlockSpec((tk, tm), lambda i, j, k: (k, i)) if ta else pl.BlockSpec((tm, tk), lambda i, j, k: (i, k))
    if b_pieces:
        assert tn == b_rows and tk % b.shape[3] == 0 if tb else (tk, tn) == (b_rows, b.shape[3])
        n_bp = tk // b.shape[3] if tb else 1
        b_spec = pl.BlockSpec((n_bp,) + b.shape[1:], (lambda i, j, k: (k, 0, 0, 0)) if tb else (lambda i, j, k: (j, 0, 0, 0)))
    else:
        b_spec = pl.BlockSpec((tn, tk), lambda i, j, k: (j, k)) if tb else pl.BlockSpec((tk, tn), lambda i, j, k: (k, j))
    assert not out_pieces or (tm == m_dim and len(out_dtypes) == 1)
    dims = (((0 if ta else 1,), (1 if tb else 0,)), ((), ()))
    n_e, n_v, n_o = len(extras), len(vecs), len(out_dtypes)
    c_ins, c_in_specs, c_outs, c_out_specs, begin, end = _hosted(comm, grid)
    n_ci, n_co = len(c_ins), len(c_outs)

    def body(a_ref, b_ref, *refs):
        e_refs, v_refs, ci_refs = refs[:n_e], refs[n_e:n_e + n_v], refs[n_e + n_v:n_e + n_v + n_ci]
        refs = refs[n_e + n_v + n_ci:]
        o_refs, co_refs = refs[:n_o], refs[n_o:n_o + n_co]
        scratch = refs[n_o + n_co:]
        sems = scratch[-2:]
        begin(ci_refs, co_refs, sems)
        if b_pieces:
            pc = b.shape[3]
            part = None
            for cc in range(n_bp):
                a_val = a_ref[:, cc * pc:(cc + 1) * pc] if tb else a_ref[...]
                term = lax.dot_general(a_val.astype(BF16), b_ref[cc].reshape(b_rows, pc).astype(BF16), dims,
                                       preferred_element_type=F32)
                part = term if part is None else part + term
        else:
            part = lax.dot_general(a_ref[...].astype(BF16), b_ref[...].astype(BF16), dims, preferred_element_type=F32)

        def finish(acc):
            outs = epilogue(acc, *[r[...] for r in e_refs], *[r[...] for r in v_refs]) if epilogue else (acc,)
            for r, v in zip(o_refs, outs):
                r[...] = (v.reshape(2, tm // 2, tn) if out_pieces else v).astype(r.dtype)

        if nk == 1:
            finish(part)
        else:
            acc_ref = scratch[0]
            k = pl.program_id(2)

            @pl.when(k == 0)
            def _():
                acc_ref[...] = part

            @pl.when(k > 0)
            def _():
                acc_ref[...] += part

            @pl.when(k == nk - 1)
            def _():
                finish(acc_ref[...])
        end(ci_refs, co_refs, sems)

    tile = pl.BlockSpec((tm, tn), lambda i, j, k: (i, j))
    if out_pieces:
        out_tile, out_shape = pl.BlockSpec((None, 2, tm // 2, tn), lambda i, j, k: (j, 0, 0, 0)), (n_dim // tn, 2, tm // 2, tn)
    else:
        out_tile, out_shape = tile, (m_dim, n_dim)
    out = pl.pallas_call(
        body, name=name, grid=grid,
        in_specs=[a_spec, b_spec] + [tile] * n_e + [pl.BlockSpec((1, tn), lambda i, j, k: (0, j))] * n_v + c_in_specs,
        out_specs=[out_tile] * n_o + c_out_specs,
        out_shape=[_sds(out_shape, d) for d in out_dtypes] + c_outs,
        scratch_shapes=([pltpu.VMEM((tm, tn), F32)] if nk > 1 else []) + (comm.sem_scratch() if comm else []),
        compiler_params=_params(("arbitrary",) * 3 if comm else ("parallel", "parallel", "arbitrary")),
    )(a, b, *extras, *vecs, *c_ins)
    return out


def _swap_halves(x):
    width = x.shape[1]
    lane = lax.broadcasted_iota(jnp.int32, x.shape, 1)
    return jnp.where((lane % 64) < 32, pltpu.roll(x, width - 32, 1), pltpu.roll(x, 32, 1))


def _rope(x, cos, sin_signed):
    return x * cos + _swap_halves(x) * sin_signed


def _rope_t(d, cos, sin_signed):
    return d * cos - _swap_halves(d) * sin_signed


def _rms(x, g):
    rstd = lax.rsqrt(jnp.mean(x * x, axis=-1, keepdims=True) + RMS_EPS)
    return x * rstd * g


def _rms_bwd(x, g, dy):
    rstd = lax.rsqrt(jnp.mean(x * x, axis=-1, keepdims=True) + RMS_EPS)
    xh = x * rstd
    dyg = dy * g
    dx = rstd * (dyg - xh * jnp.mean(dyg * xh, axis=-1, keepdims=True))
    return dx, jnp.sum(dy * xh, axis=0, keepdims=True)


def _layer_norm(r, g, b):
    mu = jnp.mean(r, axis=-1, keepdims=True)
    xc = r - mu
    var = jnp.mean(xc * xc, axis=-1, keepdims=True)
    return xc * lax.rsqrt(var + LN_EPS) * g + b


def _dot(a, b, dims):
    return lax.dot_general(a, b, (dims, ((), ())), preferred_element_type=F32)


_NN, _NT, _TN = ((1,), (0,)), ((1,), (1,)), ((0,), (0,))


def _in_proj(x_bf, w_in_p, cos, sin):
    n_rows = x_bf.shape[0]
    ts = 512
    dils = [dil for _, dil in PATTERNS]
    n_p = len(dils)

    def body(x_ref, w_ref, cos_ref, sin_ref, *refs):
        view_refs, (cq_ref, kpe_ref, ckv_ref, scratch) = refs[:n_p], refs[n_p:]
        xb = x_ref[...]
        cos_v, sin_v = cos_ref[...], sin_ref[...]

        def seg(lo, hi):
            return _dot(xb, w_ref[:, lo:hi], _NN)

        def put(part, val):
            by_dil = {1: [val]}
            for ref, dil in zip(view_refs, dils):
                if dil not in by_dil:
                    if dil > FAST_STRIDE and dil % FAST_STRIDE == 0:
                        if FAST_STRIDE not in by_dil:
                            by_dil[FAST_STRIDE] = _deinterleave(val, FAST_STRIDE, scratch)
                        inner = [_deinterleave(y, dil // FAST_STRIDE, scratch) for y in by_dil[FAST_STRIDE]]
                        by_dil[dil] = [inner[r % FAST_STRIDE][r // FAST_STRIDE] for r in range(dil)]
                    else:
                        by_dil[dil] = _deinterleave(val, dil, scratch)
                for r, piece in enumerate(by_dil[dil]):
                    lo = (3 * r + part) * A_WIDTH
                    ref[:, lo:lo + A_WIDTH] = piece.astype(BF16)

        put(0, _rope(seg(0, 512), cos_v, sin_v) * A_SCALE)
        put(1, _rope(seg(512, 1024), cos_v, sin_v))
        put(2, seg(1024, 1536))
        cq_ref[...] = seg(1536, 1792)
        kpe_ref[...] = _rope(seg(1792, 1920), cos_v[:, :LANES], sin_v[:, :LANES]).astype(BF16)
        ckv_ref[...] = seg(1920, 2048)

    row = lambda c, d=1: pl.BlockSpec((ts // d, c * d), lambda i: (i, 0))
    return pl.pallas_call(
        body, name="in_proj", grid=(n_rows // ts,),
        in_specs=[row(D_MODEL), pl.BlockSpec((D_MODEL, IN_PAD), lambda i: (0, 0)), row(A_WIDTH), row(A_WIDTH)],
        out_specs=[row(3 * A_WIDTH, d) for d in dils] + [row(Q_LORA), row(LANES), row(KV_LORA)],
        out_shape=[_sds((n_rows // d, d * 3 * A_WIDTH), BF16) for d in dils]
        + [_sds((n_rows, Q_LORA)), _sds((n_rows, LANES), BF16), _sds((n_rows, KV_LORA))],
        scratch_shapes=[pltpu.VMEM((A_WIDTH // LANES, ts, LANES), F32)],
        compiler_params=_params(("parallel",)),
    )(x_bf, w_in_p, cos, sin)


def _band_mask(m, heads):
    qi = lax.broadcasted_iota(jnp.int32, (heads * SPAN, 2 * SPAN), 0) % SPAN
    kj = lax.broadcasted_iota(jnp.int32, (heads * SPAN, 2 * SPAN), 1)
    return ((kj < SPAN) & (kj >= qi) & (m > 0)) | ((kj >= SPAN) & ((kj - SPAN) <= qi))


def _even_lanes(rows):
    return lax.broadcasted_iota(jnp.int32, (rows, LANES), 1) < A_HEAD_DIM


DIL_GROUP = 4
HEAD_SLOT = LANES // A_HEADS
O_WORDS = A_WIDTH // 2
OL_WIDTH = O_WORDS + LANES


def _pack_halves(t):
    return _pack2(t[:, :O_WORDS], t[:, O_WORDS:])


def _unpack_halves(p):
    return jnp.concatenate(_unpack2(p), axis=1)


DQKV_WIDTH = 3 * A_WIDTH // 2


def _pack2(a, b):
    hi = lax.bitcast_convert_type(a.astype(jnp.bfloat16).astype(F32), jnp.uint32) & jnp.uint32(0xFFFF0000)
    lo = lax.shift_right_logical(lax.bitcast_convert_type(b.astype(jnp.bfloat16).astype(F32), jnp.uint32), jnp.uint32(16))
    return lax.bitcast_convert_type(hi | lo, F32)


def _unpack2(p):
    u = lax.bitcast_convert_type(p, jnp.uint32)
    return (lax.bitcast_convert_type(u & jnp.uint32(0xFFFF0000), F32),
            lax.bitcast_convert_type(lax.shift_left(u, jnp.uint32(16)), F32))


def _slot_lane(h):
    return HEAD_SLOT * (h // 2) + A_HEAD_DIM * (h % 2)


def _slot_head(lane):
    return 2 * (jnp.right_shift(lane, 4) & 3) + jnp.right_shift(lane, 6)


def _dil_fwd(qkv_view, dil, comm=None):
    nb = qkv_view.shape[0] // SPAN
    group = min(dil, DIL_GROUP)
    grid = (dil // group, nb)
    c_ins, c_in_specs, c_outs, c_out_specs, begin, end = _hosted(comm, grid)
    n_ci, n_co = len(c_ins), len(c_outs)

    def body(cur_ref, prev_ref, *refs):
        ci_refs, ol_ref, co_refs, sems = refs[:n_ci], refs[n_ci], refs[n_ci + 1:n_ci + 1 + n_co], refs[n_ci + 1 + n_co:]
        begin(ci_refs, co_refs, sems)
        mask = _band_mask(pl.program_id(1), 1)
        even, even2 = _even_lanes(SPAN), _even_lanes(2 * SPAN)
        pair_of_lane = jnp.right_shift(lax.broadcasted_iota(jnp.int32, (SPAN, LANES), 1), 4) & 3
        for g in range(group):
            lse_c, o_pairs = jnp.zeros((SPAN, LANES), F32), []
            for p in range(A_HEADS // 2):
                q_sl, k_sl, v_sl = (slice((3 * g + t) * A_WIDTH + p * LANES, (3 * g + t) * A_WIDTH + (p + 1) * LANES)
                                    for t in range(3))
                q2 = cur_ref[:, q_sl]
                kcat = jnp.concatenate([prev_ref[:, k_sl], cur_ref[:, k_sl]], axis=0)
                vcat = jnp.concatenate([prev_ref[:, v_sl], cur_ref[:, v_sl]], axis=0)
                zero, one = jnp.zeros_like(q2), jnp.ones_like(vcat)
                res, lses = [], []
                for first in (True, False):
                    qh = jnp.where(even, q2, zero) if first else jnp.where(even, zero, q2)
                    vh = jnp.where(even2, vcat, one) if first else jnp.where(even2, one, vcat)
                    s = jnp.where(mask, _dot(qh, kcat, _NT), NEG)
                    mx = jnp.max(s, axis=-1, keepdims=True)
                    r = _dot(jnp.exp(s - mx).astype(BF16), vh, _NN)
                    den = pltpu.roll(r, A_HEAD_DIM, 1)
                    res.append(r / den)
                    lses.append(mx + jnp.log(den))
                o_pairs.append(jnp.where(even, res[0], res[1]))
                lse_c = jnp.where(pair_of_lane == p, jnp.where(even, lses[0], lses[1]), lse_c)
            for p in range(2):
                ol_ref[:, OL_WIDTH * g + p * LANES:OL_WIDTH * g + (p + 1) * LANES] = _pack2(o_pairs[p], o_pairs[p + 2])
            ol_ref[:, OL_WIDTH * g + O_WORDS:OL_WIDTH * (g + 1)] = lse_c
        end(ci_refs, co_refs, sems)

    return pl.pallas_call(
        body, name=f"dil_fwd_d{dil}", grid=grid,
        in_specs=[pl.BlockSpec((SPAN, group * 3 * A_WIDTH), lambda r, m: (m, r)),
                  pl.BlockSpec((SPAN, group * 3 * A_WIDTH), lambda r, m: (jnp.maximum(m - 1, 0), r))] + c_in_specs,
        out_specs=[pl.BlockSpec((SPAN, group * OL_WIDTH), lambda r, m: (m, r))] + c_out_specs,
        out_shape=[_sds((qkv_view.shape[0], dil * OL_WIDTH))] + c_outs,
        scratch_shapes=comm.sem_scratch() if comm else [],
        compiler_params=_params(("arbitrary", "arbitrary") if comm else ("parallel", "arbitrary")),
    )(qkv_view, qkv_view, *c_ins)


def _slot_spread_matrix():
    i = lax.broadcasted_iota(jnp.int32, (LANES, A_WIDTH), 0)
    h = lax.broadcasted_iota(jnp.int32, (LANES, A_WIDTH), 1) // A_HEAD_DIM
    return (i == HEAD_SLOT * (h // 2) + A_HEAD_DIM * (h % 2)).astype(BF16)


def _spread_slots(c, spread):
    hi = c.astype(BF16)
    r1 = c - hi.astype(F32)
    mid = r1.astype(BF16)
    lo = (r1 - mid.astype(F32)).astype(BF16)
    return _dot(hi, spread, _NN) + _dot(mid, spread, _NN) + _dot(lo, spread, _NN)


def _pattern_weights(ol1, ol2, ol3):
    l1, l2, l3 = ol1[:, O_WORDS:], ol2[:, O_WORDS:], ol3[:, O_WORDS:]
    mx = jnp.maximum(jnp.maximum(l1, l2), l3)
    e1, e2, e3 = jnp.exp(l1 - mx), jnp.exp(l2 - mx), jnp.exp(l3 - mx)
    inv = 1.0 / (e1 + e2 + e3)
    return e1 * inv, e2 * inv, e3 * inv


def _mix_fwd(ols, b_out, a_g, b_g):
    def fn(ol1, ol2, ol3, b, ag, bg, spread):
        ws = [_spread_slots(w, spread) for w in _pattern_weights(ol1, ol2, ol3)]
        a = sum(w * _unpack_halves(ol[:, :O_WORDS]) for w, ol in zip(ws, (ol1, ol2, ol3)))
        return a, jnp.concatenate([_rms(a, ag), _rms(b, bg)], axis=1)

    n_rows = b_out.shape[0]
    return _rowwise("mix_fwd", fn, [*ols, b_out], [_sds((n_rows, A_WIDTH)), _sds((n_rows, 2 * A_WIDTH), BF16)],
                    consts=[a_g, b_g, _slot_spread_matrix()], in_dils=[dil for _, dil in PATTERNS] + [1])


def _mla_prep(cq, ckv, kpe, cos, sin, gq, w_uq_p, gkv, w_ukv_p):
    def fn(cq_v, ckv_v, kpe_v, cos_v, sin_v, gq_v, wq_v, gkv_v, wkv_v):
        cqn = _rms(cq_v, gq_v).astype(BF16)
        q = _dot(cqn, wq_v, _NN)
        qf = [jnp.concatenate([q[:, h * QK_NOPE:(h + 1) * QK_NOPE],
                               _rope(q[:, 512 + h * LANES:512 + (h + 1) * LANES], cos_v[:, :LANES], sin_v[:, :LANES])], axis=1)
              for h in range(MLA_HEADS)]
        ckvn = _rms(ckv_v, gkv_v).astype(BF16)
        kvv = _dot(ckvn, wkv_v, _NN)
        kf = [jnp.concatenate([kvv[:, h * QK_NOPE:(h + 1) * QK_NOPE], kpe_v.astype(F32)], axis=1) for h in range(MLA_HEADS)]
        return cqn, qf, ckvn, kf, kvv[:, 512:]

    n_rows = cq.shape[0]
    heads = _sds((MLA_HEADS, n_rows, QK_PAD), BF16)
    return _rowwise("mla_prep", fn, [cq, ckv, kpe, cos, sin],
                    [_sds((n_rows, Q_LORA), BF16), heads, _sds((n_rows, KV_LORA), BF16), heads, _sds((n_rows, MLA_WIDTH), BF16)],
                    consts=[gq, w_uq_p, gkv, w_ukv_p])


MLA_FWD_TILES = (1024, 2048)
MLA_BWD_TILES = (1024, 1024)


def _mla_fwd(qf, kf, v, comm=None):
    n_rows = v.shape[0]
    tq, tk = min(MLA_FWD_TILES[0], n_rows), min(MLA_FWD_TILES[1], n_rows)
    nq, nk = n_rows // tq, n_rows // tk
    grid = (MLA_HEADS, nq, nk)
    exp2_scale = MLA_SCALE * math.log2(math.e)
    c_ins, c_in_specs, c_outs, c_out_specs, begin, end = _hosted(comm, grid)
    n_ci, n_co = len(c_ins), len(c_outs)

    def last_k(i):
        return jnp.right_shift(i * tq + tq - 1, int(math.log2(tk)))

    def body(q_ref, k_ref, v_ref, *refs):
        ci_refs, (o_ref, lse_ref), co_refs = refs[:n_ci], refs[n_ci:n_ci + 2], refs[n_ci + 2:n_ci + 2 + n_co]
        m_sc, l_sc, acc_sc = refs[n_ci + 2 + n_co:n_ci + 5 + n_co]
        sems = refs[n_ci + 5 + n_co:]
        i, j = pl.program_id(1), pl.program_id(2)
        begin(ci_refs, co_refs, sems)

        @pl.when(j == 0)
        def _():
            m_sc[...] = jnp.full(m_sc.shape, NEG, F32)
            l_sc[...] = jnp.zeros(l_sc.shape, F32)
            acc_sc[...] = jnp.zeros(acc_sc.shape, F32)

        def step(masked):
            s = _dot(q_ref[...], k_ref[...], _NT)
            if masked:
                row = lax.broadcasted_iota(jnp.int32, (tq, tk), 0) + i * tq
                col = lax.broadcasted_iota(jnp.int32, (tq, tk), 1) + j * tk
                s = jnp.where(col <= row, s, NEG)
            m_prev = m_sc[...]
            m_new = jnp.maximum(m_prev, jnp.max(s, axis=-1, keepdims=True))
            alpha = jnp.exp2((m_prev - m_new) * exp2_scale)
            p = jnp.exp2((s - m_new) * exp2_scale)
            l_sc[...] = alpha * l_sc[...] + jnp.sum(p, axis=-1, keepdims=True)
            acc_sc[...] = alpha * acc_sc[...] + _dot(p.astype(BF16), v_ref[...], _NN)
            m_sc[...] = m_new

        active = j * tk <= i * tq + tq - 1
        crosses = (j + 1) * tk - 1 > i * tq

        @pl.when(active & jnp.logical_not(crosses))
        def _():
            step(False)

        @pl.when(active & crosses)
        def _():
            step(True)

        @pl.when(j == last_k(i))
        def _():
            o_ref[...] = acc_sc[...] / l_sc[...]
            lse_ref[...] = jnp.broadcast_to(m_sc[...] * MLA_SCALE + jnp.log(l_sc[...]), (tq, V_DIM))

        end(ci_refs, co_refs, sems)

    return pl.pallas_call(
        body, name="mla_fwd", grid=grid,
        in_specs=[pl.BlockSpec((None, tq, QK_PAD), lambda h, i, j: (h, i, 0)),
                  pl.BlockSpec((None, tk, QK_PAD), lambda h, i, j: (h, jnp.minimum(j, last_k(i)), 0)),
                  pl.BlockSpec((tk, V_DIM), lambda h, i, j: (jnp.minimum(j, last_k(i)), h))] + c_in_specs,
        out_specs=[pl.BlockSpec((tq, V_DIM), lambda h, i, j: (i, h)), pl.BlockSpec((tq, V_DIM), lambda h, i, j: (i, h))]
        + c_out_specs,
        out_shape=[_sds((n_rows, MLA_WIDTH)), _sds((n_rows, MLA_WIDTH))] + c_outs,
        scratch_shapes=[pltpu.VMEM((tq, 1), F32), pltpu.VMEM((tq, 1), F32), pltpu.VMEM((tq, V_DIM), F32)]
        + (comm.sem_scratch() if comm else []),
        compiler_params=_params(("arbitrary",) * 3 if comm else ("parallel", "parallel", "arbitrary")),
    )(qf, kf, v, *c_ins)


def _mm_res_ln(name, a, w, xres, g, b, tm):
    def epi(acc, xr, g_v, b_v):
        r = ALPHA * xr + acc
        y = _layer_norm(r, g_v, b_v)
        return y, y, r

    return _mm(name, a, w, (F32, BF16, F32), tm=tm, tn=D_MODEL, tk=a.shape[1], epilogue=epi, extras=[xres], vecs=[g, b])


def _mm_relu2(x_bf, w, comm=None):
    def epi(acc):
        r = jnp.maximum(acc, 0.0)
        return (r * r,)

    return _mm("ff1", x_bf, w, (BF16,), tm=2048, tn=1024, tk=D_MODEL, epilogue=epi, b_pieces=True, comm=comm)


def _loss_fn(y, t):
    def fn(y_v, t_v):
        d = y_v - t_v
        part = jnp.sum(jnp.sum(d * d, axis=1, keepdims=True), axis=0, keepdims=True)
        return d * (1.0 / D_MODEL), part

    dy, part = _rowwise("loss", fn, [y, t], [_sds(y.shape)], reds=[_sds((1, 1))])
    return part * (0.5 / D_MODEL), dy


def _ln_bwd_rows(dy, r, g):
    mu = jnp.mean(r, axis=-1, keepdims=True)
    xc = r - mu
    rstd = lax.rsqrt(jnp.mean(xc * xc, axis=-1, keepdims=True) + LN_EPS)
    xh = xc * rstd
    dxh = dy * g
    dr = rstd * (dxh - jnp.mean(dxh, axis=-1, keepdims=True) - xh * jnp.mean(dxh * xh, axis=-1, keepdims=True))
    return dr, dr, jnp.sum(dy * xh, axis=0, keepdims=True), jnp.sum(dy, axis=0, keepdims=True)


def _ln_bwd(name, dy, r, g):
    return _rowwise(name, _ln_bwd_rows, [dy, r], [_sds(dy.shape), _sds(dy.shape, BF16)], consts=[g],
                    reds=[_sds((1, D_MODEL)), _sds((1, D_MODEL))])


def _head_sum_matrix():
    i = lax.broadcasted_iota(jnp.int32, (A_WIDTH, LANES), 0) // A_HEAD_DIM
    return (i == _slot_head(lax.broadcasted_iota(jnp.int32, (A_WIDTH, LANES), 1))).astype(BF16)


def _mix_bwd(dmixed, a_out, b_out, ols, a_g, b_g):
    def fn(dm, a, b, ol1, ol2, ol3, ag, bg, sum_mat, spread):
        da, dga = _rms_bwd(a, ag, dm[:, :A_WIDTH])
        db, dgb = _rms_bwd(b, bg, dm[:, A_WIDTH:])
        t = da * a
        t_hi = t.astype(BF16)
        t_lo = (t - t_hi.astype(F32)).astype(BF16)
        tsum = _dot(t_hi, sum_mat, _NN) + _dot(t_lo, sum_mat, _NN)
        tb = db * b
        delta_b = jnp.concatenate(
            [jnp.broadcast_to(jnp.sum(tb[:, h * V_DIM:(h + 1) * V_DIM], axis=-1, keepdims=True), (tb.shape[0], V_DIM))
             for h in range(MLA_HEADS)], axis=1)
        dods = [jnp.concatenate([_pack_halves(_spread_slots(w, spread) * da), w * tsum], axis=1)
                for w in _pattern_weights(ol1, ol2, ol3)]
        return (*dods, db, delta_b, dga, dgb)

    n_rows = a_out.shape[0]
    wide = (n_rows, A_WIDTH)
    dils = [dil for _, dil in PATTERNS]
    return _rowwise("mix_bwd", fn, [dmixed, a_out, b_out, *ols],
                    [_sds((n_rows // d, d * OL_WIDTH)) for d in dils] + [_sds(wide, BF16), _sds(wide)],
                    consts=[a_g, b_g, _head_sum_matrix(), _slot_spread_matrix()],
                    reds=[_sds((1, A_WIDTH)), _sds((1, MLA_WIDTH))], ts=256,
                    in_dils=[1, 1, 1] + dils, out_dils=dils + [1, 1])


def _dil_bwd(qkv_view, ol_view, dod_view, dil, comm=None):
    nb = qkv_view.shape[0] // SPAN
    group = min(dil, DIL_GROUP)
    grid = (dil // group, nb + 1)
    c_ins, c_in_specs, c_outs, c_out_specs, begin, end = _hosted(comm, grid)
    n_ci, n_co = len(c_ins), len(c_outs)

    def body(cur_ref, prev_ref, ol_ref, dod_ref, *refs):
        ci_refs, out_ref, co_refs = refs[:n_ci], refs[n_ci], refs[n_ci + 1:n_ci + 1 + n_co]
        carry, sems = refs[n_ci + 1 + n_co], refs[n_ci + 2 + n_co:]
        m = pl.program_id(1)
        begin(ci_refs, co_refs, sems)

        def write_packed(g, pairs):
            lo = DQKV_WIDTH * g
            for p, (dq_p, dk_p, _) in enumerate(pairs):
                out_ref[:, lo + p * LANES:lo + (p + 1) * LANES] = _pack2(dq_p, dk_p)
            for p in range(2):
                out_ref[:, lo + A_WIDTH + p * LANES:lo + A_WIDTH + (p + 1) * LANES] = _pack2(pairs[p][2], pairs[p + 2][2])

        @pl.when(m == 0)
        def _():
            carry[...] = jnp.zeros(carry.shape, F32)

        @pl.when(m < nb)
        def _():
            mask, even = _band_mask(m, 2), _even_lanes(SPAN)
            for g in range(group):
                done = []
                for p in range(A_HEADS // 2):
                    q_sl, k_sl, v_sl = (slice((3 * g + t) * A_WIDTH + p * LANES, (3 * g + t) * A_WIDTH + (p + 1) * LANES)
                                        for t in range(3))
                    do_sl = slice(OL_WIDTH * g + (p % 2) * LANES, OL_WIDTH * g + (p % 2 + 1) * LANES)
                    s_e, s_o = (OL_WIDTH * g + O_WORDS + _slot_lane(2 * p + t) for t in range(2))
                    q2, do2 = cur_ref[:, q_sl], _unpack2(dod_ref[:, do_sl])[p // 2].astype(BF16)
                    zero = jnp.zeros_like(q2)
                    qcat = jnp.concatenate([jnp.where(even, q2, zero), jnp.where(even, zero, q2)], axis=0)
                    docat = jnp.concatenate([jnp.where(even, do2, zero), jnp.where(even, zero, do2)], axis=0)
                    kcat = jnp.concatenate([prev_ref[:, k_sl], cur_ref[:, k_sl]], axis=0)
                    vcat = jnp.concatenate([prev_ref[:, v_sl], cur_ref[:, v_sl]], axis=0)
                    lse_c = jnp.concatenate([ol_ref[:, s_e:s_e + 1], ol_ref[:, s_o:s_o + 1]], axis=0)
                    dl_c = jnp.concatenate([dod_ref[:, s_e:s_e + 1], dod_ref[:, s_o:s_o + 1]], axis=0)
                    pr = jnp.exp(jnp.where(mask, _dot(qcat, kcat, _NT), NEG) - lse_c)
                    ds = (pr * (_dot(docat, vcat, _NT) - dl_c)).astype(BF16)
                    dq = _dot(ds, kcat, _NN)
                    dk2 = _dot(ds, qcat, _TN)
                    dv2 = _dot(pr.astype(BF16), docat, _TN)
                    done.append((carry[:, q_sl], carry[:, k_sl] + dk2[:SPAN], carry[:, v_sl] + dv2[:SPAN]))
                    carry[:, q_sl] = jnp.where(even, dq[:SPAN], dq[SPAN:])
                    carry[:, k_sl] = dk2[SPAN:]
                    carry[:, v_sl] = dv2[SPAN:]
                write_packed(g, done)

        @pl.when(m == nb)
        def _():
            for g in range(group):
                write_packed(g, [tuple(carry[:, (3 * g + t) * A_WIDTH + p * LANES:(3 * g + t) * A_WIDTH + (p + 1) * LANES]
                                       for t in range(3)) for p in range(A_HEADS // 2)])

        end(ci_refs, co_refs, sems)

    last = nb - 1

    def cur(width):
        return pl.BlockSpec((SPAN, group * width), lambda r, m: (jnp.minimum(m, last), r))

    def prev(width):
        return pl.BlockSpec((SPAN, group * width), lambda r, m: (jnp.clip(m - 1, 0, last), r))

    return pl.pallas_call(
        body, name=f"dil_bwd_d{dil}", grid=grid,
        in_specs=[cur(3 * A_WIDTH), prev(3 * A_WIDTH), cur(OL_WIDTH), cur(OL_WIDTH)] + c_in_specs,
        out_specs=[prev(DQKV_WIDTH)] + c_out_specs,
        out_shape=[_sds((qkv_view.shape[0], dil * DQKV_WIDTH))] + c_outs,
        scratch_shapes=[pltpu.VMEM((SPAN, group * 3 * A_WIDTH), F32)] + (comm.sem_scratch() if comm else []),
        compiler_params=_params(("arbitrary", "arbitrary") if comm else ("parallel", "arbitrary")),
    )(qkv_view, qkv_view, ol_view, dod_view, *c_ins)


def _mla_bwd(qf, kf, v, lse, do, delta, comm=None):
    n_rows = v.shape[0]
    tq, tk = min(MLA_BWD_TILES[0], n_rows), min(MLA_BWD_TILES[1], n_rows)
    nq, nk = n_rows // tq, n_rows // tk

    def first_q(j):
        return jnp.right_shift(j * tk, int(math.log2(tq)))

    grid = (MLA_HEADS, nk, nq)
    c_ins, c_in_specs, c_outs, c_out_specs, begin, end = _hosted(comm, grid)
    n_ci, n_co = len(c_ins), len(c_outs)

    def body(q_ref, k_ref, v_ref, lse_ref, do_ref, dl_ref, *refs):
        ci_refs, (dq_ref, dk_ref, dv_ref), co_refs = refs[:n_ci], refs[n_ci:n_ci + 3], refs[n_ci + 3:n_ci + 3 + n_co]
        dk_sc, dv_sc = refs[n_ci + 3 + n_co:n_ci + 5 + n_co]
        sems = refs[n_ci + 5 + n_co:]
        j, i = pl.program_id(1), pl.program_id(2)
        begin(ci_refs, co_refs, sems)

        @pl.when((j == 0) & (i == 0))
        def _():
            dq_ref[...] = jnp.zeros(dq_ref.shape, F32)

        @pl.when(i == first_q(j))
        def _():
            dk_sc[...] = jnp.zeros(dk_sc.shape, F32)
            dv_sc[...] = jnp.zeros(dv_sc.shape, F32)

        def step(masked):
            q, k, dob = q_ref[...], k_ref[...], do_ref[...]
            s = _dot(q, k, _NT) * MLA_SCALE
            if masked:
                row = lax.broadcasted_iota(jnp.int32, (tq, tk), 0) + i * tq
                col = lax.broadcasted_iota(jnp.int32, (tq, tk), 1) + j * tk
                s = jnp.where(col <= row, s, NEG)
            p = jnp.exp(s - lse_ref[:, :1])
            dp = _dot(dob, v_ref[...], _NT)
            ds = (p * (dp - dl_ref[:, :1]) * MLA_SCALE).astype(BF16)
            dv_sc[...] += _dot(p.astype(BF16), dob, _TN)
            dk_sc[...] += _dot(ds, q, _TN)
            rows = pl.ds(pl.multiple_of(i * tq, tq), tq)
            dq_ref[rows, :] += _dot(ds, k, _NN)

        active = i >= first_q(j)
        crosses = (j + 1) * tk - 1 > i * tq

        @pl.when(active & jnp.logical_not(crosses))
        def _():
            step(False)

        @pl.when(active & crosses)
        def _():
            step(True)

        @pl.when(i == nq - 1)
        def _():
            dk_ref[...] = dk_sc[...]
            dv_ref[...] = dv_sc[...]

        end(ci_refs, co_refs, sems)

    qrow = lambda h, j, i: (jnp.maximum(i, first_q(j)), h)
    return pl.pallas_call(
        body, name="mla_bwd", grid=grid,
        in_specs=[pl.BlockSpec((None, tq, QK_PAD), lambda h, j, i: (h, jnp.maximum(i, first_q(j)), 0)),
                  pl.BlockSpec((None, tk, QK_PAD), lambda h, j, i: (h, j, 0)),
                  pl.BlockSpec((tk, V_DIM), lambda h, j, i: (j, h)),
                  pl.BlockSpec((tq, V_DIM), qrow), pl.BlockSpec((tq, V_DIM), qrow), pl.BlockSpec((tq, V_DIM), qrow)]
        + c_in_specs,
        out_specs=[pl.BlockSpec((None, n_rows, QK_PAD), lambda h, j, i: (h, 0, 0)),
                   pl.BlockSpec((None, tk, QK_PAD), lambda h, j, i: (h, j, 0)),
                   pl.BlockSpec((tk, V_DIM), lambda h, j, i: (j, h))] + c_out_specs,
        out_shape=[_sds((MLA_HEADS, n_rows, QK_PAD)), _sds((MLA_HEADS, n_rows, QK_PAD)), _sds((n_rows, MLA_WIDTH))] + c_outs,
        scratch_shapes=[pltpu.VMEM((tk, QK_PAD), F32), pltpu.VMEM((tk, V_DIM), F32)] + (comm.sem_scratch() if comm else []),
        compiler_params=_params(("arbitrary",) * 3 if comm else ("parallel", "arbitrary", "arbitrary")),
    )(qf, kf, v, lse, do, delta, *c_ins)


def _mla_bwd_prep(dqf, dkf, dv, cos, sin):
    def fn(dqf_v, dkf_v, dv_v, cos_v, sin_v):
        dq = jnp.concatenate([dqf_v[h][:, :QK_NOPE] for h in range(MLA_HEADS)]
                             + [_rope_t(dqf_v[h][:, QK_NOPE:], cos_v[:, :LANES], sin_v[:, :LANES]) for h in range(MLA_HEADS)], axis=1)
        dkv = jnp.concatenate([dkf_v[h][:, :QK_NOPE] for h in range(MLA_HEADS)] + [dv_v], axis=1)
        dkpe = dkf_v[0][:, QK_NOPE:] + dkf_v[1][:, QK_NOPE:] + dkf_v[2][:, QK_NOPE:] + dkf_v[3][:, QK_NOPE:]
        return dq, dkv, dkpe

    n_rows = dv.shape[0]
    return _rowwise("mla_bwd_prep", fn, [dqf, dkf, dv, cos, sin],
                    [_sds((n_rows, UQ_PAD), BF16), _sds((n_rows, 1024), BF16), _sds((n_rows, LANES))], ts=256)


def _assemble_dh(dqkvs, dcqn, cq, dckvn, ckv, dkpe, cos, sin, gq, gkv, comm=None):
    def fn(g1, g2, g3, dcqn_v, cq_v, dckvn_v, ckv_v, dkpe_v, cos_v, sin_v, gq_v, gkv_v):
        dq = dk = dv = 0.0
        for packed in (g1, g2, g3):
            dq_p, dk_p = _unpack2(packed[:, :A_WIDTH])
            dv_p = jnp.concatenate(_unpack2(packed[:, A_WIDTH:]), axis=1)
            dq, dk, dv = dq + dq_p, dk + dk_p, dv + dv_p
        dqa = _rope_t(dq, cos_v, sin_v) * A_SCALE
        dka = _rope_t(dk, cos_v, sin_v)
        dcq, dgq = _rms_bwd(cq_v, gq_v, dcqn_v)
        dckv, dgkv = _rms_bwd(ckv_v, gkv_v, dckvn_v)
        dkr = _rope_t(dkpe_v, cos_v[:, :LANES], sin_v[:, :LANES])
        return jnp.concatenate([dqa, dka, dv, dcq, dkr, dckv], axis=1), dgq, dgkv

    n_rows = cq.shape[0]
    return _rowwise("assemble_dh", fn, [*dqkvs, dcqn, cq, dckvn, ckv, dkpe, cos, sin],
                    [_sds((n_rows, IN_PAD), BF16)], consts=[gq, gkv],
                    reds=[_sds((1, Q_LORA)), _sds((1, KV_LORA))], ts=256,
                    in_dils=[dil for _, dil in PATTERNS] + [1] * 7, comm=comm)


def _layer_fwd(x_f32, x_bf, w, sm, cos, sin, behind=(None, None), rest=None):
    *qkv_views, cq, kpe, ckv = _in_proj(x_bf, w["w_in"], cos, sin)
    ol_views, brought = [], []
    for i, (view, (_, dil)) in enumerate(zip(qkv_views, PATTERNS)):
        ol, *got = _dil_fwd(view, dil, comm=rest[0][i] if rest else None)
        ol_views.append(ol)
        brought.append(got)
    if rest:
        w = {**w, **rest[1](brought)}
    cqn, qf, ckvn, kf, v = _mla_prep(cq, ckv, kpe, cos, sin, sm["q_a_norm"], w["w_uq"], sm["kv_a_norm"], w["w_ukv"])
    b_out, b_lse, *behind_attn = _mla_fwd(qf, kf, v, comm=behind[0])
    a_out, mixed = _mix_fwd(ol_views, b_out, sm["a_out_norm"], sm["b_out_norm"])
    x1, x1_bf, r1 = _mm_res_ln("wo_ln1", mixed, w["w_o"], x_f32, sm["ln1_g"], sm["ln1_b"], tm=1024)
    f, *behind_ff1 = _mm_relu2(x1_bf, w["w_ff1"], comm=behind[1])
    x2, x2_bf, r2 = _mm_res_ln("ff2_ln2", f, w["w_ff2"], x1, sm["ln2_g"], sm["ln2_b"], tm=512)
    saved = dict(w=w, x_bf=x_bf, qkv_views=qkv_views, ol_views=ol_views, cq=cq, ckv=ckv, cqn=cqn, qf=qf, ckvn=ckvn,
                 kf=kf, v=v, b_out=b_out, b_lse=b_lse, a_out=a_out, mixed=mixed, x1_bf=x1_bf, r1=r1, f=f, r2=r2)
    return x2, x2_bf, saved, (behind_attn, behind_ff1)


def _layer_bwd(dx2, w, sm, sv, cos, sin, reduce_of, pending=None):
    dr2, dr2_bf, dg2, db2 = _ln_bwd("ln2_bwd", dx2, sv["r2"], sm["ln2_g"])
    (dw_ff2,) = _mm("dw_ff2", sv["f"], dr2_bf, (F32,), ta=True, tm=1024, tn=1024, tk=dx2.shape[0])
    du, *swapped = _mm("d_u", dr2_bf, w["w_ff2"], (BF16,), tb=True, tm=1024, tn=2048, tk=D_MODEL,
                       epilogue=lambda acc, f: (acc * (2.0 * jnp.sqrt(f.astype(F32))),), extras=[sv["f"]],
                       comm=_swap_comm(pending.pieces) if pending else None)
    if pending:
        pending.after_swap(swapped)
    (dw_ff1,) = _mm("dw_ff1", sv["x1_bf"], du, (F32,), ta=True, tm=1024, tn=1024, tk=dx2.shape[0], out_pieces=True)
    (dy1,) = _mm("d_x1", du, w["w_ff1"], (F32,), tb=True, tm=512, tn=1024, tk=D_FF,
                 epilogue=lambda acc, d: (acc + ALPHA * d,), extras=[dr2], b_pieces=True)
    dr1, dr1_bf, dg1, db1 = _ln_bwd("ln1_bwd", dy1, sv["r1"], sm["ln1_g"])
    (dw_o,) = _mm("dw_o", sv["mixed"], dr1_bf, (F32,), ta=True, tm=1024, tn=1024, tk=2048)
    early = reduce_of(dict(w_o=dw_o, w_ff1=dw_ff1, w_ff2=dw_ff2))
    (dmixed,) = _mm("d_mixed", dr1_bf, w["w_o"], (F32,), tb=True, tm=1024, tn=1024, tk=D_MODEL)
    dod1, dod2, dod3, do_b, delta_b, dga, dgb = _mix_bwd(
        dmixed, sv["a_out"], sv["b_out"], sv["ol_views"], sm["a_out_norm"], sm["b_out_norm"])
    dqkvs = []
    for i, ((_, dil), qkv_view, ol_view, dod_view) in enumerate(zip(PATTERNS, sv["qkv_views"], sv["ol_views"], (dod1, dod2, dod3))):
        swap_here = early is not None and i == 0
        dqkv, *swapped = _dil_bwd(qkv_view, ol_view, dod_view, dil, comm=_swap_comm(early.pieces) if swap_here else None)
        dqkvs.append(dqkv)
        if swap_here:
            early.after_swap(swapped)
    to_scatter = (early.s16 if early else []) + (pending.s16 if pending else [])
    dqf, dkf, dv_b, *scattered = _mla_bwd(sv["qf"], sv["kf"], sv["v"], sv["b_lse"], do_b, delta_b,
                                          comm=_scatter_comm(to_scatter) if to_scatter else None)
    if early:
        early.after_scatter(scattered[:len(early.s16)])
    if pending:
        pending.after_scatter(scattered[len(scattered) - len(pending.s16):])
    dq_b, dkv_b, dkpe = _mla_bwd_prep(dqf, dkf, dv_b, cos, sin)
    (dw_uq,) = _mm("dw_uq", sv["cqn"], dq_b, (F32,), ta=True, tm=Q_LORA, tn=1024, tk=2048)
    (dcqn,) = _mm("d_cqn", dq_b, w["w_uq"], (F32,), tb=True, tm=1024, tn=Q_LORA, tk=UQ_PAD)
    (dw_ukv,) = _mm("dw_ukv", sv["ckvn"], dkv_b, (F32,), ta=True, tm=KV_LORA, tn=1024, tk=2048)
    (dckvn,) = _mm("d_ckvn", dkv_b, w["w_ukv"], (F32,), tb=True, tm=1024, tn=KV_LORA, tk=1024)
    to_share = (early.mine if early else []) + (pending.mine if pending else [])
    dh, dgq, dgkv, *shared = _assemble_dh(dqkvs, dcqn, sv["cq"], dckvn, sv["ckv"], dkpe, cos, sin,
                                          sm["q_a_norm"], sm["kv_a_norm"], comm=_share_comm(to_share) if to_share else None)
    if early:
        early.after_share(shared[:len(early.mine)])
    if pending:
        pending.after_share(shared[len(shared) - len(pending.mine):])
    (dw_in,) = _mm("dw_in", sv["x_bf"], dh, (F32,), ta=True, tm=1024, tn=1024, tk=dx2.shape[0])
    (dx,) = _mm("d_x", dh, w["w_in"], (F32,), tb=True, tm=1024, tn=1024, tk=2048,
                epilogue=lambda acc, d: (acc + ALPHA * d,), extras=[dr1])
    late = reduce_of(dict(w_in=dw_in, w_uq=dw_uq, w_ukv=dw_ukv))
    dsm = dict(q_a_norm=dgq, kv_a_norm=dgkv, a_out_norm=dga, b_out_norm=dgb, ln1_g=dg1, ln1_b=db1, ln2_g=dg2, ln2_b=db2)
    return dx, dsm, late


def _pad_w_in(w):
    return jnp.concatenate([w[:, :1792], w[:, 1920:1984], jnp.zeros((w.shape[0], 64), w.dtype), w[:, 1792:1920]], axis=1)


def _unpad_w_in(w):
    return jnp.concatenate([w[:, :1792], w[:, 1920:2048], w[:, 1792:1856]], axis=1)


def _rope_tables(n_rows):
    half = A_HEAD_DIM // 2
    inv_freq = ROPE_THETA ** (-jnp.arange(half, dtype=F32) / half)
    ang = jnp.arange(n_rows, dtype=F32)[:, None] * inv_freq[None, :]
    cos = jnp.tile(jnp.cos(ang), (1, 2 * A_HEADS))
    sin = jnp.tile(jnp.concatenate([-jnp.sin(ang), jnp.sin(ang)], axis=1), (1, A_HEADS))
    return cos, sin


W_NAMES = ("w_in", "w_uq", "w_ukv", "w_o", "w_ff1", "w_ff2")
PIECE = dict(w_in=(512, 512), w_uq=(128, 256), w_ukv=(64, 256), w_o=(128, 1024), w_ff1=(512, 1024), w_ff2=(512, 1024))
SHARD_COLS = dict(w_in=496, w_uq=192, w_ukv=256, w_o=1024, w_ff1=1024, w_ff2=1024)


def _col_pieces(t, pad_to):
    rows, _, cols = t.shape
    t = jnp.pad(t, ((0, 0), (0, 0), (0, pad_to - cols)))
    return t.reshape(2, rows // 2, N_CHIPS, pad_to).transpose(2, 0, 1, 3)


def _grad_pieces(dws):
    def heads(t, rows):
        return jnp.concatenate([t[:, :512].reshape(rows, MLA_HEADS, LANES), t[:, 512:].reshape(rows, MLA_HEADS, LANES)], axis=2)

    make = dict(
        w_in=lambda t: _col_pieces(_unpad_w_in(t).reshape(D_MODEL, N_CHIPS, SHARD_COLS["w_in"]), 512),
        w_uq=lambda t: _col_pieces(heads(t, Q_LORA), 256),
        w_ukv=lambda t: _col_pieces(heads(t, KV_LORA), 256),
        w_o=lambda t: t.reshape(N_CHIPS, 2, 128, D_MODEL),
        w_ff1=lambda t: t,
        w_ff2=lambda t: t.reshape(N_CHIPS, 2, 512, D_MODEL))
    return {n: make[n](t) for n, t in dws.items()}


def _weights_from_pieces(p):
    def cols(t):
        return t.transpose(1, 2, 0, 3).reshape(2 * t.shape[2], N_CHIPS, t.shape[3])

    def heads(t, rows):
        return jnp.concatenate([t[:, :, :LANES].reshape(rows, 512), t[:, :, LANES:].reshape(rows, 512)], axis=1)

    make = dict(
        w_in=lambda t: _pad_w_in(cols(t)[:, :, :SHARD_COLS["w_in"]].reshape(D_MODEL, IN_COLS)),
        w_uq=lambda t: heads(cols(t), Q_LORA),
        w_ukv=lambda t: heads(cols(t), KV_LORA),
        w_o=lambda t: t.reshape(D_MODEL, D_MODEL),
        w_ff1=lambda t: t,
        w_ff2=lambda t: t.reshape(D_FF, D_MODEL))
    return {n: make[n](t) for n, t in p.items()}


def _my_piece(name, shard, half):
    rows, cols = shard.shape[0] // 2, shard.shape[1]
    t = lax.dynamic_slice_in_dim(shard, half * rows, rows, axis=0).astype(BF16)
    return jnp.pad(t, ((0, 0), (0, PIECE[name][1] - cols)))


MESH = pl.DeviceIdType.MESH
ANY = pl.BlockSpec(memory_space=pl.ANY)


class _Comm:
    def __init__(self, ins, out_shapes, n_sems, start, finish):
        self.ins, self.out_shapes, self.n_sems, self.start, self.finish = list(ins), list(out_shapes), n_sems, start, finish

    def sem_scratch(self):
        return [pltpu.SemaphoreType.DMA((self.n_sems,)), pltpu.SemaphoreType.DMA((self.n_sems,))]


def _comm_call(name, comm):
    n_in, n_out = len(comm.ins), len(comm.out_shapes)

    def body(*refs):
        ins, outs, sems = refs[:n_in], refs[n_in:n_in + n_out], refs[n_in + n_out:]
        comm.start(ins, outs, *sems)
        comm.finish(ins, outs, *sems)

    return pl.pallas_call(body, name=name, out_shape=comm.out_shapes, in_specs=[ANY] * n_in, out_specs=[ANY] * n_out,
                          scratch_shapes=comm.sem_scratch())(*comm.ins)


def _all_gather_comm(blocks):
    n = len(blocks)

    def plan(x_refs, out_refs, send_sems, recv_sems):
        x, y, c = lax.axis_index("x"), lax.axis_index("y"), lax.axis_index("c")
        me, sibling = (x, y, c), (x, y, 1 - c)
        chips = [(1 - x, y), (x, 1 - y), (1 - x, 1 - y)]

        def copy(t, k, blk, to, src=None):
            px, py, pc = blk
            slot = out_refs[t].at[4 * px + 2 * py + pc]
            return pltpu.make_async_remote_copy(
                src_ref=slot if src is None else src, dst_ref=slot,
                send_sem=send_sems.at[7 * t + k], recv_sem=recv_sems.at[7 * t + k], device_id=to, device_id_type=MESH)

        first = []
        for t in range(n):
            first.append(copy(t, 0, me, sibling, src=x_refs[t]))
            first += [copy(t, 1 + j, me, (*chip, c), src=x_refs[t]) for j, chip in enumerate(chips)]
        return me, sibling, chips, c, copy, first

    def start(x_refs, out_refs, send_sems, recv_sems):
        for cp in plan(x_refs, out_refs, send_sems, recv_sems)[-1]:
            cp.start()

    def finish(x_refs, out_refs, send_sems, recv_sems):
        me, sibling, chips, c, copy, sent = plan(x_refs, out_refs, send_sems, recv_sems)
        for j, chip in enumerate(chips):
            for t in range(n):
                copy(t, 1 + j, (*chip, c), me).wait_recv()
                sent.append(copy(t, 4 + j, (*chip, c), sibling))
                sent[-1].start()
        for t in range(n):
            copy(t, 0, sibling, me).wait_recv()
        for j, chip in enumerate(chips):
            for t in range(n):
                copy(t, 4 + j, (*chip, 1 - c), me).wait_recv()
        for cp in sent:
            cp.wait_send()

    return _Comm(blocks, [_sds((N_DEV,) + b.shape, b.dtype) for b in blocks], 7 * n, start, finish)


def _simple_comm(ins, out_shapes, n_sems, copies):
    def start(in_refs, out_refs, send_sems, recv_sems):
        for cp in copies(in_refs, out_refs, send_sems, recv_sems):
            cp.start()

    def finish(in_refs, out_refs, send_sems, recv_sems):
        for cp in copies(in_refs, out_refs, send_sems, recv_sems):
            cp.wait()

    return _Comm(ins, out_shapes, n_sems, start, finish)


def _swap_comm(gs):
    def copies(g_refs, got_refs, send_sems, recv_sems):
        c = lax.axis_index("c")
        sibling = (lax.axis_index("x"), lax.axis_index("y"), 1 - c)
        return [pltpu.make_async_remote_copy(src_ref=g_refs[t].at[k, 1 - c], dst_ref=got_refs[t].at[k],
                                             send_sem=send_sems.at[N_CHIPS * t + k], recv_sem=recv_sems.at[N_CHIPS * t + k],
                                             device_id=sibling, device_id_type=MESH)
                for t in range(len(gs)) for k in range(N_CHIPS)]

    return _simple_comm(gs, [_sds((N_CHIPS,) + g.shape[2:], g.dtype) for g in gs], N_CHIPS * len(gs), copies)


def _scatter_comm(hs):
    def copies(h_refs, got_refs, send_sems, recv_sems):
        x, y, c = lax.axis_index("x"), lax.axis_index("y"), lax.axis_index("c")
        chips = [(1 - x, y), (x, 1 - y), (1 - x, 1 - y)]
        return [pltpu.make_async_remote_copy(src_ref=h_refs[t].at[2 * px + py], dst_ref=got_refs[t].at[rel],
                                             send_sem=send_sems.at[3 * t + rel], recv_sem=recv_sems.at[3 * t + rel],
                                             device_id=(px, py, c), device_id_type=MESH)
                for t in range(len(hs)) for rel, (px, py) in enumerate(chips)]

    return _simple_comm(hs, [_sds((3,) + h.shape[1:], h.dtype) for h in hs], 3 * len(hs), copies)


def _share_comm(rs):
    def copies(r_refs, got_refs, send_sems, recv_sems):
        sibling = (lax.axis_index("x"), lax.axis_index("y"), 1 - lax.axis_index("c"))
        return [pltpu.make_async_remote_copy(src_ref=r_refs[t], dst_ref=got_refs[t], send_sem=send_sems.at[t],
                                             recv_sem=recv_sems.at[t], device_id=sibling, device_id_type=MESH)
                for t in range(len(rs))]

    return _simple_comm(rs, [_sds(r.shape, r.dtype) for r in rs], len(rs), copies)


SUM_STEPS = 4


def _pair_sums(name, gs, gots, where):
    n = len(gs)
    dims = [(g.shape[2] // SUM_STEPS, g.shape[3]) for g in gs]

    def body(where_ref, *refs):
        own = pl.program_id(1) == where_ref[1]
        for t in range(n):
            s = refs[t][...] + refs[n + t][...]
            refs[3 * n + t][...] = s.astype(BF16)

            @pl.when(own)
            def _():
                refs[2 * n + t][...] = s

    def part(tr, pc):
        return pl.BlockSpec((None, tr, pc), lambda i, k, where_ref: (k, i, 0))

    def kept(tr, pc):
        return pl.BlockSpec((None, None, tr, pc), lambda i, k, where_ref: (k, where_ref[0], i, 0))

    outs = pl.pallas_call(
        body, name=name,
        out_shape=[_sds(g.shape[2:]) for g in gs] + [_sds((N_CHIPS,) + g.shape[2:], BF16) for g in gs],
        grid_spec=pltpu.PrefetchScalarGridSpec(
            num_scalar_prefetch=1, grid=(SUM_STEPS, N_CHIPS),
            in_specs=[kept(*d) for d in dims] + [part(*d) for d in dims],
            out_specs=[pl.BlockSpec(d, lambda i, k, where_ref: (i, 0)) for d in dims] + [part(*d) for d in dims]),
        compiler_params=_params(("parallel", "arbitrary")),
    )(where, *gs, *gots)
    return list(outs[:n]), list(outs[n:])


def _chips_sums(name, owns, gots):
    n = len(owns)
    dims = [(h.shape[0] // SUM_STEPS, h.shape[1]) for h in owns]

    def body(*refs):
        for t in range(n):
            got_ref = refs[n + t]
            refs[2 * n + t][...] = refs[t][...] + got_ref[0] + got_ref[1] + got_ref[2]

    row = [pl.BlockSpec((tr, pc), lambda i: (i, 0)) for tr, pc in dims]
    return pl.pallas_call(
        body, name=name, out_shape=[_sds(h.shape) for h in owns], grid=(SUM_STEPS,),
        in_specs=row + [pl.BlockSpec((3, tr, pc), lambda i: (0, i, 0)) for tr, pc in dims], out_specs=row,
        compiler_params=_params(("parallel",)),
    )(*owns, *gots)


class _Reduce:
    def __init__(self, tag, dws, c, chip):
        gp = _grad_pieces(dws)
        self.tag, self.c, self.names, self.pieces = tag, c, list(gp), list(gp.values())
        self.where = jnp.stack([c, chip]).astype(jnp.int32)
        self.shards = None

    def after_swap(self, gots):
        self.s32, self.s16 = _pair_sums(f"rs_pair_sum_{self.tag}", self.pieces, gots, self.where)

    def after_scatter(self, parts):
        self.mine = list(_chips_sums(f"rs_chips_sum_{self.tag}", self.s32, parts))

    def after_share(self, theirs):
        self.shards = {}
        for n, a, b in zip(self.names, self.mine, theirs):
            both = jnp.where(self.c == 0, jnp.concatenate([a, b], axis=0), jnp.concatenate([b, a], axis=0))
            self.shards[n] = both[:, :SHARD_COLS[n]]

    def run_alone(self):
        self.after_swap(_comm_call(f"rs_pair_{self.tag}", _swap_comm(self.pieces)))
        self.after_scatter(_comm_call(f"rs_chips_{self.tag}", _scatter_comm(self.s16)))
        self.after_share(_comm_call(f"rs_share_{self.tag}", _share_comm(self.mine)))


class _LayerWeights:
    def __init__(self, shards, c, dev):
        self.shards, self.c, self.dev = shards, c, dev

    def comm(self, l, names=W_NAMES):
        return _all_gather_comm([_my_piece(n, self.shards[l][n], self.c) for n in names])

    def take(self, l, gathered, names=W_NAMES):
        pieces = {}
        for n, g in zip(names, gathered):
            mine = _my_piece(n, self.shards[l][n], self.c)
            pieces[n] = lax.dynamic_update_index_in_dim(g, mine, self.dev, 0).reshape((N_CHIPS, 2) + mine.shape)
        return _weights_from_pieces(pieces)

    def first(self):
        now, later = ("w_in", "w_uq", "w_ukv"), ("w_ff1", "w_ff2", "w_o")
        weights = self.take(0, _comm_call("ag_l0", self.comm(0, now)), now)
        comms = [self.comm(0, (n,)) for n in later]

        def take_rest(results):
            rest = {}
            for n, gathered in zip(later, results):
                rest.update(self.take(0, gathered, (n,)))
            return rest

        return weights, (comms, take_rest)


def _all_reduce_small(vec):
    rows, lanes = vec.shape

    def gather_body(x_ref, out_ref, send_sems, recv_sems):
        x, y, c = lax.axis_index("x"), lax.axis_index("y"), lax.axis_index("c")
        me, sibling = (x, y, c), (x, y, 1 - c)
        chips = [(1 - x, y), (x, 1 - y), (1 - x, 1 - y)]

        def slot(px, py, pc):
            return out_ref.at[4 * px + 2 * py + pc]

        def copy(k, blk, to, src=None):
            return pltpu.make_async_remote_copy(
                src_ref=slot(*blk) if src is None else src, dst_ref=slot(*blk),
                send_sem=send_sems.at[k], recv_sem=recv_sems.at[k], device_id=to, device_id_type=MESH)

        out_ref[4 * x + 2 * y + c] = x_ref[...]
        first = [copy(0, me, sibling, src=x_ref)]
        first += [copy(1 + j, me, (*chip, c), src=x_ref) for j, chip in enumerate(chips)]
        for cp in first:
            cp.start()
        passed = [copy(4 + j, (*chip, c), sibling) for j, chip in enumerate(chips)]
        for j, chip in enumerate(chips):
            copy(1 + j, (*chip, c), me).wait_recv()
            passed[j].start()
        copy(0, sibling, me).wait_recv()
        for j, chip in enumerate(chips):
            copy(4 + j, (*chip, 1 - c), me).wait_recv()
        for cp in first + passed:
            cp.wait_send()

    vmem = pl.BlockSpec(memory_space=pltpu.VMEM)
    allv = pl.pallas_call(
        gather_body, name="small_all_gather", out_shape=_sds((N_DEV, rows, lanes)),
        in_specs=[vmem], out_specs=vmem,
        scratch_shapes=[pltpu.SemaphoreType.DMA((7,)), pltpu.SemaphoreType.DMA((7,))],
    )(vec)

    def sum_body(a_ref, o_ref):
        acc = a_ref[0]
        for d in range(1, N_DEV):
            acc = acc + a_ref[d]
        o_ref[...] = acc

    return pl.pallas_call(sum_body, name="small_sum", out_shape=_sds((rows, lanes)), in_specs=[vmem], out_specs=vmem)(allv)


def _adamw(name, w, g, m, v, ts=512, comm=None):
    def fn(w_v, g_v, m_v, v_v):
        m_n = ADAM_B1 * m_v + (1.0 - ADAM_B1) * g_v
        v_n = ADAM_B2 * v_v + (1.0 - ADAM_B2) * (g_v * g_v)
        m_hat = m_n / (1.0 - ADAM_B1 ** ADAM_STEP)
        v_hat = v_n / (1.0 - ADAM_B2 ** ADAM_STEP)
        delta = -ADAM_LR * (m_hat / (jnp.sqrt(v_hat) + ADAM_EPS) + ADAM_WD * w_v)
        return delta, m_n, v_n

    return _rowwise(name, fn, [w, g, m, v], [_sds(w.shape)] * 3, ts=ts, comm=comm)


def _pack_small(per_layer):
    flat = jnp.concatenate([per_layer[l][n].reshape(-1) for l in range(DEPTH) for n in SMALL_NAMES])
    return jnp.pad(flat, (0, SMALL_ROWS * LANES - flat.shape[0])).reshape(SMALL_ROWS, LANES)


def _unpack_small(packed):
    flat = packed.reshape(-1)
    per = sum(SMALL_SIZES)
    out = {}
    for n, size, off in zip(SMALL_NAMES, SMALL_SIZES, [sum(SMALL_SIZES[:i]) for i in range(len(SMALL_SIZES))]):
        out[n] = jnp.stack([flat[l * per + off:l * per + off + size] for l in range(DEPTH)])
    return out


NEXT_WEIGHTS_BEHIND = (("w_ff1", "w_ff2"), ("w_in", "w_uq", "w_ukv", "w_o"))


def _fwd_bwd(x, target, layer_weights, smalls, on_layer_grads):
    depth = len(smalls)
    cos, sin = _rope_tables(x.shape[0])
    h_f32, h_bf = x, x.astype(BF16)
    saved = []
    w, rest = layer_weights.first()
    for l in range(depth):
        more = l + 1 < depth
        behind = [layer_weights.comm(l + 1, names) if more else None for names in NEXT_WEIGHTS_BEHIND]
        h_f32, h_bf, sv, brought = _layer_fwd(h_f32, h_bf, w, smalls[l], cos, sin, behind, rest)
        saved.append(sv)
        if more:
            w, rest = {}, None
            for names, got in zip(NEXT_WEIGHTS_BEHIND, brought):
                w.update(layer_weights.take(l + 1, got, names))
    loss_part, dh = _loss_fn(h_f32, target)
    small_grads, pending = [None] * depth, None
    for l in reversed(range(depth)):
        dh, small_grads[l], pending = _layer_bwd(dh, saved[l]["w"], smalls[l], saved[l], cos, sin,
                                                 lambda dws, l=l: on_layer_grads(l, dws), pending)
    return loss_part, dh, small_grads, pending


def kernel(x, w_in, q_a_norm, kv_a_norm, w_uq, w_ukv, a_out_norm, b_out_norm, w_o, ln1_g, ln1_b, w_ff1, w_ff2, ln2_g, ln2_b, loss_target, m_w_in, m_q_a_norm, m_kv_a_norm, m_w_uq, m_w_ukv, m_a_out_norm, m_b_out_norm, m_w_o, m_ln1_g, m_ln1_b, m_w_ff1, m_w_ff2, m_ln2_g, m_ln2_b, v_w_in, v_q_a_norm, v_kv_a_norm, v_w_uq, v_w_ukv, v_a_out_norm, v_b_out_norm, v_w_o, v_ln1_g, v_ln1_b, v_w_ff1, v_w_ff2, v_ln2_g, v_ln2_b):
    c = lax.axis_index("c")
    chip = 2 * lax.axis_index("x") + lax.axis_index("y")
    big = dict(w_in=w_in, w_uq=w_uq, w_ukv=w_ukv, w_o=w_o, w_ff1=w_ff1, w_ff2=w_ff2)
    big_m = dict(w_in=m_w_in, w_uq=m_w_uq, w_ukv=m_w_ukv, w_o=m_w_o, w_ff1=m_w_ff1, w_ff2=m_w_ff2)
    big_v = dict(w_in=v_w_in, w_uq=v_w_uq, w_ukv=v_w_ukv, w_o=v_w_o, w_ff1=v_w_ff1, w_ff2=v_w_ff2)
    small = dict(q_a_norm=q_a_norm, kv_a_norm=kv_a_norm, a_out_norm=a_out_norm, b_out_norm=b_out_norm,
                 ln1_g=ln1_g, ln1_b=ln1_b, ln2_g=ln2_g, ln2_b=ln2_b)
    small_m = dict(q_a_norm=m_q_a_norm, kv_a_norm=m_kv_a_norm, a_out_norm=m_a_out_norm, b_out_norm=m_b_out_norm,
                   ln1_g=m_ln1_g, ln1_b=m_ln1_b, ln2_g=m_ln2_g, ln2_b=m_ln2_b)
    small_v = dict(q_a_norm=v_q_a_norm, kv_a_norm=v_kv_a_norm, a_out_norm=v_a_out_norm, b_out_norm=v_b_out_norm,
                   ln1_g=v_ln1_g, ln1_b=v_ln1_b, ln2_g=v_ln2_g, ln2_b=v_ln2_b)

    layer_weights = _LayerWeights([{n: big[n][l] for n in W_NAMES} for l in range(DEPTH)], c, 2 * chip + c)
    smalls = [{n: small[n][l][None, :] for n in SMALL_NAMES} for l in range(DEPTH)]
    reductions = [[] for _ in range(DEPTH)]

    def reduce_layer(l, dws):
        reductions[l].append(_Reduce(f"l{l}_{len(reductions[l])}", dws, c, chip))
        return reductions[l][-1]

    loss_part, grad_x, small_grads, last = _fwd_bwd(x[0], loss_target[0], layer_weights, smalls, reduce_layer)
    loss = lax.psum(loss_part[0, 0], ("x", "y", "c"))
    grad_x = grad_x[None]

    g_small_packed = _all_reduce_small(_pack_small(small_grads))
    pack_in = lambda d: _pack_small([{n: d[n][l] for n in SMALL_NAMES} for l in range(DEPTH)])
    sd, sm_, sv_ = _adamw("adamw_small", pack_in(small), g_small_packed, pack_in(small_m), pack_in(small_v), ts=SMALL_ROWS)
    g_small, d_small, m_small, v_small = (_unpack_small(t) for t in (g_small_packed, sd, sm_, sv_))

    last.run_alone()
    g_big, d_big, m_big, v_big = {}, {}, {}, {}
    for n in W_NAMES:
        g = jnp.stack([next(r.shards[n] for r in reductions[l] if n in r.names) for l in range(DEPTH)])
        shape = g.shape
        flat = lambda t: t.reshape(shape[0] * shape[1], shape[2])
        d_, m_, v_ = _adamw(f"adamw_{n}", flat(big[n]), flat(g), flat(big_m[n]), flat(big_v[n]))
        g_big[n], d_big[n], m_big[n], v_big[n] = g, d_.reshape(shape), m_.reshape(shape), v_.reshape(shape)

    order = ("w_in", "q_a_norm", "kv_a_norm", "w_uq", "w_ukv", "a_out_norm", "b_out_norm", "w_o", "ln1_g", "ln1_b",
             "w_ff1", "w_ff2", "ln2_g", "ln2_b")
    pick = lambda bigd, smalld: [bigd[n] if n in bigd else smalld[n] for n in order]
    return (loss, grad_x, *pick(g_big, g_small), *pick(d_big, d_small), *pick(m_big, m_small), *pick(v_big, v_small))
```

```python
import math

import jax
import jax.numpy as jnp
from jax import lax
from jax.experimental import pallas as pl
from jax.experimental.pallas import tpu as pltpu

F32, BF16 = jnp.float32, jnp.bfloat16

D_MODEL = 1024
DEPTH = 4
A_HEAD_DIM = 64
A_HEADS = 8
A_WIDTH = 512
PATTERNS = ((128, 1), (512, 4), (2048, 16))
SPAN = 128
MLA_HEADS = 4
QK_NOPE = 128
QK_ROPE = 64
V_DIM = 128
Q_LORA = 256
KV_LORA = 128
MLA_WIDTH = 512
QK_PAD = 256
IN_COLS = 1984
IN_PAD = 2048
UQ_PAD = 1024
D_FF = 4096
ROPE_THETA = 10000.0
ALPHA = (2.0 * DEPTH) ** 0.25
LN_EPS = 1e-5
RMS_EPS = 1e-6
MLA_SCALE = (QK_NOPE + QK_ROPE) ** -0.5
A_SCALE = A_HEAD_DIM ** -0.5

ADAM_LR, ADAM_B1, ADAM_B2, ADAM_EPS, ADAM_WD, ADAM_STEP = 0.001, 0.9, 0.999, 1e-08, 0.01, 10

VMEM_LIMIT_BYTES = 56 * 1024 * 1024
NEG = -1e30
N_CHIPS, N_DEV = 4, 8
LANES = 128

SMALL_NAMES = ("q_a_norm", "kv_a_norm", "a_out_norm", "b_out_norm", "ln1_g", "ln1_b", "ln2_g", "ln2_b")
SMALL_SIZES = (256, 128, 512, 512, 1024, 1024, 1024, 1024)
SMALL_ROWS = 176


def _params(sem):
    return pltpu.CompilerParams(dimension_semantics=sem, vmem_limit_bytes=VMEM_LIMIT_BYTES)


FAST_STRIDE = 4


def _interleave(parts, scratch):
    d, (n, width) = len(parts), parts[0].shape
    if d > FAST_STRIDE and d % FAST_STRIDE == 0:
        inner = [_interleave(parts[r::FAST_STRIDE], scratch) for r in range(FAST_STRIDE)]
        return _interleave(inner, scratch)
    for r in range(d):
        for s in range(width // LANES):
            scratch.at[s][pl.ds(r, n, stride=d), :] = parts[r][:, s * LANES:(s + 1) * LANES]
    return jnp.concatenate([scratch.at[s][pl.ds(0, n * d), :] for s in range(width // LANES)], axis=1)


def _deinterleave(x, d, scratch):
    n, width = x.shape[0] // d, x.shape[1]
    if d > FAST_STRIDE and d % FAST_STRIDE == 0:
        outer = _deinterleave(x, FAST_STRIDE, scratch)
        inner = [_deinterleave(y, d // FAST_STRIDE, scratch) for y in outer]
        return [inner[r % FAST_STRIDE][r // FAST_STRIDE] for r in range(d)]
    for s in range(width // LANES):
        scratch.at[s][pl.ds(0, n * d), :] = x[:, s * LANES:(s + 1) * LANES]
    return [jnp.concatenate([scratch.at[s][pl.ds(r, n, stride=d), :] for s in range(width // LANES)], axis=1) for r in range(d)]


def _rowwise(name, fn, ins, outs, consts=(), reds=(), ts=512, in_dils=None, out_dils=None, comm=None):
    in_dils = list(in_dils or [1] * len(ins))
    out_dils = list(out_dils or [1] * len(outs))
    n_rows = ins[0].shape[-2] * in_dils[0]
    ts = min(ts, n_rows)
    assert n_rows % ts == 0
    n_in, n_c, n_o, n_r = len(ins), len(consts), len(outs), len(reds)
    viewed = [(a.shape, d) for a, d in zip(list(ins) + list(outs), in_dils + out_dils) if d > 1]
    grid = (n_rows // ts,)
    h_ins, h_in_specs, h_outs, h_out_specs, begin, end = _hosted(comm, grid)
    n_hi, n_ho = len(h_ins), len(h_outs)

    def tile_spec(shape, d=1):
        if len(shape) == 2:
            return pl.BlockSpec((ts // d, shape[1]), lambda i: (i, 0))
        return pl.BlockSpec((shape[0], ts, shape[2]), lambda i: (0, i, 0))

    def whole_spec(shape):
        return pl.BlockSpec(shape, lambda i: (0,) * len(shape))

    def body(*refs):
        in_refs, c_refs, hi_refs = refs[:n_in], refs[n_in:n_in + n_c], refs[n_in + n_c:n_in + n_c + n_hi]
        refs = refs[n_in + n_c + n_hi:]
        o_refs, r_refs, ho_refs = refs[:n_o], refs[n_o:n_o + n_r], refs[n_o + n_r:n_o + n_r + n_ho]
        scratches = list(refs[n_o + n_r + n_ho:])
        sems = [scratches.pop(), scratches.pop()][::-1] if comm else []
        begin(hi_refs, ho_refs, sems)
        vals = []
        for r, d in zip(in_refs, in_dils):
            if d == 1:
                vals.append(r[...])
            else:
                width = r.shape[1] // d
                vals.append(_interleave([r[:, k * width:(k + 1) * width] for k in range(d)], scratches.pop(0)))
        res = fn(*vals, *[r[...] for r in c_refs])
        res = tuple(res) if isinstance(res, (tuple, list)) else (res,)
        for r, v, d in zip(o_refs, res[:n_o], out_dils):
            if len(r.shape) == 3:
                for g in range(r.shape[0]):
                    r[g] = v[g].astype(r.dtype)
            elif d == 1:
                r[...] = v.astype(r.dtype)
            else:
                width = r.shape[1] // d
                for k, part in enumerate(_deinterleave(v, d, scratches.pop(0))):
                    r[:, k * width:(k + 1) * width] = part.astype(r.dtype)
        if n_r:
            i = pl.program_id(0)

            @pl.when(i == 0)
            def _():
                for r, v in zip(r_refs, res[n_o:]):
                    r[...] = v

            @pl.when(i > 0)
            def _():
                for r, v in zip(r_refs, res[n_o:]):
                    r[...] += v
        end(hi_refs, ho_refs, sems)

    out = pl.pallas_call(
        body, name=name, grid=grid,
        in_specs=[tile_spec(a.shape, d) for a, d in zip(ins, in_dils)] + [whole_spec(c.shape) for c in consts] + h_in_specs,
        out_specs=[tile_spec(o.shape, d) for o, d in zip(outs, out_dils)] + [whole_spec(r.shape) for r in reds] + h_out_specs,
        out_shape=list(outs) + list(reds) + h_outs,
        scratch_shapes=[pltpu.VMEM((shape[1] // d // LANES, ts, LANES), F32) for shape, d in viewed]
        + (comm.sem_scratch() if comm else []),
        compiler_params=_params(("arbitrary",)),
    )(*ins, *consts, *h_ins)
    return out


def _sds(shape, dtype=F32):
    return jax.ShapeDtypeStruct(tuple(shape), dtype)


def _hosted(comm, grid):
    if comm is None:
        return [], [], [], [], (lambda *a: None), (lambda *a: None)

    def edge(which, at, ins, outs, sems):
        ids = [pl.program_id(d) for d in range(len(grid))]
        hit = ids[0] == at[0]
        for i, v in zip(ids[1:], at[1:]):
            hit = hit & (i == v)

        @pl.when(hit)
        def _():
            which(ins, outs, *sems)

    begin = lambda ins, outs, sems: edge(comm.start, [0] * len(grid), ins, outs, sems)
    end = lambda ins, outs, sems: edge(comm.finish, [g - 1 for g in grid], ins, outs, sems)
    return comm.ins, [ANY] * len(comm.ins), comm.out_shapes, [ANY] * len(comm.out_shapes), begin, end


def _mm(name, a, b, out_dtypes, *, ta=False, tb=False, tm=512, tn=512, tk=512, epilogue=None, extras=(), vecs=(),
        comm=None, b_pieces=False, out_pieces=False):
    (k_dim, m_dim) = a.shape if ta else a.shape[::-1]
    if b_pieces:
        b_rows, b_cols = 2 * b.shape[2], N_CHIPS * b.shape[3]
        (n_dim, k2) = (b_rows, b_cols) if tb else (b_cols, b_rows)
    else:
        (n_dim, k2) = b.shape if tb else b.shape[::-1]
    assert k_dim == k2
    tm, tn, tk = min(tm, m_dim), min(tn, n_dim), min(tk, k_dim)
    assert m_dim % tm == 0 and n_dim % tn == 0 and k_dim % tk == 0
    nk = k_dim // tk
    grid = (m_dim // tm, n_dim // tn, nk)
    a_spec = pl.BlockSpec((tk, tm), lambda i, j, k: (k, i)) if ta else pl.BlockSpec((tm, tk), lambda i, j, k: (i, k))
    if b_pieces:
        assert tn == b_rows and tk % b.shape[3] == 0 if tb else (tk, tn) == (b_rows, b.shape[3])
        n_bp = tk // b.shape[3] if tb else 1
        b_spec = pl.BlockSpec((n_bp,) + b.shape[1:], (lambda i, j, k: (k, 0, 0, 0)) if tb else (lambda i, j, k: (j, 0, 0, 0)))
    else:
        b_spec = pl.BlockSpec((tn, tk), lambda i, j, k: (j, k)) if tb else pl.BlockSpec((tk, tn), lambda i, j, k: (k, j))
    assert not out_pieces or (tm == m_dim and len(out_dtypes) == 1)
    dims = (((0 if ta else 1,), (1 if tb else 0,)), ((), ()))
    n_e, n_v, n_o = len(extras), len(vecs), len(out_dtypes)
    c_ins, c_in_specs, c_outs, c_out_specs, begin, end = _hosted(comm, grid)
    n_ci, n_co = len(c_ins), len(c_outs)

    def body(a_ref, b_ref, *refs):
        e_refs, v_refs, ci_refs = refs[:n_e], refs[n_e:n_e + n_v], refs[n_e + n_v:n_e + n_v + n_ci]
        refs = refs[n_e + n_v + n_ci:]
        o_refs, co_refs = refs[:n_o], refs[n_o:n_o + n_co]
        scratch = refs[n_o + n_co:]
        sems = scratch[-2:]
        begin(ci_refs, co_refs, sems)
        if b_pieces:
            pc = b.shape[3]
            part = None
            for cc in range(n_bp):
                a_val = a_ref[:, cc * pc:(cc + 1) * pc] if tb else a_ref[...]
                term = lax.dot_general(a_val.astype(BF16), b_ref[cc].reshape(b_rows, pc).astype(BF16), dims,
                                       preferred_element_type=F32)
                part = term if part is None else part + term
        else:
            part = lax.dot_general(a_ref[...].astype(BF16), b_ref[...].astype(BF16), dims, preferred_element_type=F32)

        def finish(acc):
            outs = epilogue(acc, *[r[...] for r in e_refs], *[r[...] for r in v_refs]) if epilogue else (acc,)
            for r, v in zip(o_refs, outs):
                r[...] = (v.reshape(2, tm // 2, tn) if out_pieces else v).astype(r.dtype)

        if nk == 1:
            finish(part)
        else:
            acc_ref = scratch[0]
            k = pl.program_id(2)

            @pl.when(k == 0)
            def _():
                acc_ref[...] = part

            @pl.when(k > 0)
            def _():
                acc_ref[...] += part

            @pl.when(k == nk - 1)
            def _():
                finish(acc_ref[...])
        end(ci_refs, co_refs, sems)

    tile = pl.BlockSpec((tm, tn), lambda i, j, k: (i, j))
    if out_pieces:
        out_tile, out_shape = pl.BlockSpec((None, 2, tm // 2, tn), lambda i, j, k: (j, 0, 0, 0)), (n_dim // tn, 2, tm // 2, tn)
    else:
        out_tile, out_shape = tile, (m_dim, n_dim)
    out = pl.pallas_call(
        body, name=name, grid=grid,
        in_specs=[a_spec, b_spec] + [tile] * n_e + [pl.BlockSpec((1, tn), lambda i, j, k: (0, j))] * n_v + c_in_specs,
        out_specs=[out_tile] * n_o + c_out_specs,
        out_shape=[_sds(out_shape, d) for d in out_dtypes] + c_outs,
        scratch_shapes=([pltpu.VMEM((tm, tn), F32)] if nk > 1 else []) + (comm.sem_scratch() if comm else []),
        compiler_params=_params(("arbitrary",) * 3 if comm else ("parallel", "parallel", "arbitrary")),
    )(a, b, *extras, *vecs, *c_ins)
    return out


def _swap_halves(x):
    width = x.shape[1]
    lane = lax.broadcasted_iota(jnp.int32, x.shape, 1)
    return jnp.where((lane % 64) < 32, pltpu.roll(x, width - 32, 1), pltpu.roll(x, 32, 1))


def _rope(x, cos, sin_signed):
    return x * cos + _swap_halves(x) * sin_signed


def _rope_t(d, cos, sin_signed):
    return d * cos - _swap_halves(d) * sin_signed


def _rms(x, g):
    rstd = lax.rsqrt(jnp.mean(x * x, axis=-1, keepdims=True) + RMS_EPS)
    return x * rstd * g


def _rms_bwd(x, g, dy):
    rstd = lax.rsqrt(jnp.mean(x * x, axis=-1, keepdims=True) + RMS_EPS)
    xh = x * rstd
    dyg = dy * g
    dx = rstd * (dyg - xh * jnp.mean(dyg * xh, axis=-1, keepdims=True))
    return dx, jnp.sum(dy * xh, axis=0, keepdims=True)


def _layer_norm(r, g, b):
    mu = jnp.mean(r, axis=-1, keepdims=True)
    xc = r - mu
    var = jnp.mean(xc * xc, axis=-1, keepdims=True)
    return xc * lax.rsqrt(var + LN_EPS) * g + b


def _dot(a, b, dims):
    return lax.dot_general(a, b, (dims, ((), ())), preferred_element_type=F32)


_NN, _NT, _TN = ((1,), (0,)), ((1,), (1,)), ((0,), (0,))


def _in_proj(x_bf, w_in_p, cos, sin):
    n_rows = x_bf.shape[0]
    ts = 512
    dils = [dil for _, dil in PATTERNS]
    n_p = len(dils)

    def body(x_ref, w_ref, cos_ref, sin_ref, *refs):
        view_refs, (cq_ref, kpe_ref, ckv_ref, scratch) = refs[:n_p], refs[n_p:]
        xb = x_ref[...]
        cos_v, sin_v = cos_ref[...], sin_ref[...]

        def seg(lo, hi):
            return _dot(xb, w_ref[:, lo:hi], _NN)

        def put(part, val):
            by_dil = {1: [val]}
            for ref, dil in zip(view_refs, dils):
                if dil not in by_dil:
                    if dil > FAST_STRIDE and dil % FAST_STRIDE == 0:
                        if FAST_STRIDE not in by_dil:
                            by_dil[FAST_STRIDE] = _deinterleave(val, FAST_STRIDE, scratch)
                        inner = [_deinterleave(y, dil // FAST_STRIDE, scratch) for y in by_dil[FAST_STRIDE]]
                        by_dil[dil] = [inner[r % FAST_STRIDE][r // FAST_STRIDE] for r in range(dil)]
                    else:
                        by_dil[dil] = _deinterleave(val, dil, scratch)
                for r, piece in enumerate(by_dil[dil]):
                    lo = (3 * r + part) * A_WIDTH
                    ref[:, lo:lo + A_WIDTH] = piece.astype(BF16)

        put(0, _rope(seg(0, 512), cos_v, sin_v) * A_SCALE)
        put(1, _rope(seg(512, 1024), cos_v, sin_v))
        put(2, seg(1024, 1536))
        cq_ref[...] = seg(1536, 1792)
        kpe_ref[...] = _rope(seg(1792, 1920), cos_v[:, :LANES], sin_v[:, :LANES]).astype(BF16)
        ckv_ref[...] = seg(1920, 2048)

    row = lambda c, d=1: pl.BlockSpec((ts // d, c * d), lambda i: (i, 0))
    return pl.pallas_call(
        body, name="in_proj", grid=(n_rows // ts,),
        in_specs=[row(D_MODEL), pl.BlockSpec((D_MODEL, IN_PAD), lambda i: (0, 0)), row(A_WIDTH), row(A_WIDTH)],
        out_specs=[row(3 * A_WIDTH, d) for d in dils] + [row(Q_LORA), row(LANES), row(KV_LORA)],
        out_shape=[_sds((n_rows // d, d * 3 * A_WIDTH), BF16) for d in dils]
        + [_sds((n_rows, Q_LORA)), _sds((n_rows, LANES), BF16), _sds((n_rows, KV_LORA))],
        scratch_shapes=[pltpu.VMEM((A_WIDTH // LANES, ts, LANES), F32)],
        compiler_params=_params(("parallel",)),
    )(x_bf, w_in_p, cos, sin)


def _band_mask(m, heads):
    qi = lax.broadcasted_iota(jnp.int32, (heads * SPAN, 2 * SPAN), 0) % SPAN
    kj = lax.broadcasted_iota(jnp.int32, (heads * SPAN, 2 * SPAN), 1)
    return ((kj < SPAN) & (kj >= qi) & (m > 0)) | ((kj >= SPAN) & ((kj - SPAN) <= qi))


def _even_lanes(rows):
    return lax.broadcasted_iota(jnp.int32, (rows, LANES), 1) < A_HEAD_DIM


DIL_GROUP = 4
HEAD_SLOT = LANES // A_HEADS
O_WORDS = A_WIDTH // 2
OL_WIDTH = O_WORDS + LANES


def _pack_halves(t):
    return _pack2(t[:, :O_WORDS], t[:, O_WORDS:])


def _unpack_halves(p):
    return jnp.concatenate(_unpack2(p), axis=1)


DQKV_WIDTH = 3 * A_WIDTH // 2


def _pack2(a, b):
    hi = lax.bitcast_convert_type(a.astype(jnp.bfloat16).astype(F32), jnp.uint32) & jnp.uint32(0xFFFF0000)
    lo = lax.shift_right_logical(lax.bitcast_convert_type(b.astype(jnp.bfloat16).astype(F32), jnp.uint32), jnp.uint32(16))
    return lax.bitcast_convert_type(hi | lo, F32)


def _unpack2(p):
    u = lax.bitcast_convert_type(p, jnp.uint32)
    return (lax.bitcast_convert_type(u & jnp.uint32(0xFFFF0000), F32),
            lax.bitcast_convert_type(lax.shift_left(u, jnp.uint32(16)), F32))


def _slot_lane(h):
    return HEAD_SLOT * (h // 2) + A_HEAD_DIM * (h % 2)


def _slot_head(lane):
    return 2 * (jnp.right_shift(lane, 4) & 3) + jnp.right_shift(lane, 6)


def _dil_fwd(qkv_view, dil, comm=None):
    nb = qkv_view.shape[0] // SPAN
    group = min(dil, DIL_GROUP)
    grid = (dil // group, nb)
    c_ins, c_in_specs, c_outs, c_out_specs, begin, end = _hosted(comm, grid)
    n_ci, n_co = len(c_ins), len(c_outs)

    def body(cur_ref, prev_ref, *refs):
        ci_refs, ol_ref, co_refs, sems = refs[:n_ci], refs[n_ci], refs[n_ci + 1:n_ci + 1 + n_co], refs[n_ci + 1 + n_co:]
        begin(ci_refs, co_refs, sems)
        mask = _band_mask(pl.program_id(1), 1)
        even, even2 = _even_lanes(SPAN), _even_lanes(2 * SPAN)
        pair_of_lane = jnp.right_shift(lax.broadcasted_iota(jnp.int32, (SPAN, LANES), 1), 4) & 3
        for g in range(group):
            lse_c, o_pairs = jnp.zeros((SPAN, LANES), F32), []
            for p in range(A_HEADS // 2):
                q_sl, k_sl, v_sl = (slice((3 * g + t) * A_WIDTH + p * LANES, (3 * g + t) * A_WIDTH + (p + 1) * LANES)
                                    for t in range(3))
                q2 = cur_ref[:, q_sl]
                kcat = jnp.concatenate([prev_ref[:, k_sl], cur_ref[:, k_sl]], axis=0)
                vcat = jnp.concatenate([prev_ref[:, v_sl], cur_ref[:, v_sl]], axis=0)
                zero, one = jnp.zeros_like(q2), jnp.ones_like(vcat)
                res, lses = [], []
                for first in (True, False):
                    qh = jnp.where(even, q2, zero) if first else jnp.where(even, zero, q2)
                    vh = jnp.where(even2, vcat, one) if first else jnp.where(even2, one, vcat)
                    s = jnp.where(mask, _dot(qh, kcat, _NT), NEG)
                    mx = jnp.max(s, axis=-1, keepdims=True)
                    r = _dot(jnp.exp(s - mx).astype(BF16), vh, _NN)
                    den = pltpu.roll(r, A_HEAD_DIM, 1)
                    res.append(r / den)
                    lses.append(mx + jnp.log(den))
                o_pairs.append(jnp.where(even, res[0], res[1]))
                lse_c = jnp.where(pair_of_lane == p, jnp.where(even, lses[0], lses[1]), lse_c)
            for p in range(2):
                ol_ref[:, OL_WIDTH * g + p * LANES:OL_WIDTH * g + (p + 1) * LANES] = _pack2(o_pairs[p], o_pairs[p + 2])
            ol_ref[:, OL_WIDTH * g + O_WORDS:OL_WIDTH * (g + 1)] = lse_c
        end(ci_refs, co_refs, sems)

    return pl.pallas_call(
        body, name=f"dil_fwd_d{dil}", grid=grid,
        in_specs=[pl.BlockSpec((SPAN, group * 3 * A_WIDTH), lambda r, m: (m, r)),
                  pl.BlockSpec((SPAN, group * 3 * A_WIDTH), lambda r, m: (jnp.maximum(m - 1, 0), r))] + c_in_specs,
        out_specs=[pl.BlockSpec((SPAN, group * OL_WIDTH), lambda r, m: (m, r))] + c_out_specs,
        out_shape=[_sds((qkv_view.shape[0], dil * OL_WIDTH))] + c_outs,
        scratch_shapes=comm.sem_scratch() if comm else [],
        compiler_params=_params(("arbitrary", "arbitrary") if comm else ("parallel", "arbitrary")),
    )(qkv_view, qkv_view, *c_ins)


def _slot_spread_matrix():
    i = lax.broadcasted_iota(jnp.int32, (LANES, A_WIDTH), 0)
    h = lax.broadcasted_iota(jnp.int32, (LANES, A_WIDTH), 1) // A_HEAD_DIM
    return (i == HEAD_SLOT * (h // 2) + A_HEAD_DIM * (h % 2)).astype(BF16)


def _spread_slots(c, spread):
    hi = c.astype(BF16)
    r1 = c - hi.astype(F32)
    mid = r1.astype(BF16)
    lo = (r1 - mid.astype(F32)).astype(BF16)
    return _dot(hi, spread, _NN) + _dot(mid, spread, _NN) + _dot(lo, spread, _NN)


def _pattern_weights(ol1, ol2, ol3):
    l1, l2, l3 = ol1[:, O_WORDS:], ol2[:, O_WORDS:], ol3[:, O_WORDS:]
    mx = jnp.maximum(jnp.maximum(l1, l2), l3)
    e1, e2, e3 = jnp.exp(l1 - mx), jnp.exp(l2 - mx), jnp.exp(l3 - mx)
    inv = 1.0 / (e1 + e2 + e3)
    return e1 * inv, e2 * inv, e3 * inv


def _mix_fwd(ols, b_out, a_g, b_g):
    def fn(ol1, ol2, ol3, b, ag, bg, spread):
        ws = [_spread_slots(w, spread) for w in _pattern_weights(ol1, ol2, ol3)]
        a = sum(w * _unpack_halves(ol[:, :O_WORDS]) for w, ol in zip(ws, (ol1, ol2, ol3)))
        return a, jnp.concatenate([_rms(a, ag), _rms(b, bg)], axis=1)

    n_rows = b_out.shape[0]
    return _rowwise("mix_fwd", fn, [*ols, b_out], [_sds((n_rows, A_WIDTH)), _sds((n_rows, 2 * A_WIDTH), BF16)],
                    consts=[a_g, b_g, _slot_spread_matrix()], in_dils=[dil for _, dil in PATTERNS] + [1])


def _mla_prep(cq, ckv, kpe, cos, sin, gq, w_uq_p, gkv, w_ukv_p):
    def fn(cq_v, ckv_v, kpe_v, cos_v, sin_v, gq_v, wq_v, gkv_v, wkv_v):
        cqn = _rms(cq_v, gq_v).astype(BF16)
        q = _dot(cqn, wq_v, _NN)
        qf = [jnp.concatenate([q[:, h * QK_NOPE:(h + 1) * QK_NOPE],
                               _rope(q[:, 512 + h * LANES:512 + (h + 1) * LANES], cos_v[:, :LANES], sin_v[:, :LANES])], axis=1)
              for h in range(MLA_HEADS)]
        ckvn = _rms(ckv_v, gkv_v).astype(BF16)
        kvv = _dot(ckvn, wkv_v, _NN)
        kf = [jnp.concatenate([kvv[:, h * QK_NOPE:(h + 1) * QK_NOPE], kpe_v.astype(F32)], axis=1) for h in range(MLA_HEADS)]
        return cqn, qf, ckvn, kf, kvv[:, 512:]

    n_rows = cq.shape[0]
    heads = _sds((MLA_HEADS, n_rows, QK_PAD), BF16)
    return _rowwise("mla_prep", fn, [cq, ckv, kpe, cos, sin],
                    [_sds((n_rows, Q_LORA), BF16), heads, _sds((n_rows, KV_LORA), BF16), heads, _sds((n_rows, MLA_WIDTH), BF16)],
                    consts=[gq, w_uq_p, gkv, w_ukv_p])


MLA_FWD_TILES = (1024, 2048)
MLA_BWD_TILES = (1024, 1024)


def _mla_fwd(qf, kf, v, comm=None):
    n_rows = v.shape[0]
    tq, tk = min(MLA_FWD_TILES[0], n_rows), min(MLA_FWD_TILES[1], n_rows)
    nq, nk = n_rows // tq, n_rows // tk
    grid = (MLA_HEADS, nq, nk)
    exp2_scale = MLA_SCALE * math.log2(math.e)
    c_ins, c_in_specs, c_outs, c_out_specs, begin, end = _hosted(comm, grid)
    n_ci, n_co = len(c_ins), len(c_outs)

    def last_k(i):
        return jnp.right_shift(i * tq + tq - 1, int(math.log2(tk)))

    def body(q_ref, k_ref, v_ref, *refs):
        ci_refs, (o_ref, lse_ref), co_refs = refs[:n_ci], refs[n_ci:n_ci + 2], refs[n_ci + 2:n_ci + 2 + n_co]
        m_sc, l_sc, acc_sc = refs[n_ci + 2 + n_co:n_ci + 5 + n_co]
        sems = refs[n_ci + 5 + n_co:]
        i, j = pl.program_id(1), pl.program_id(2)
        begin(ci_refs, co_refs, sems)

        @pl.when(j == 0)
        def _():
            m_sc[...] = jnp.full(m_sc.shape, NEG, F32)
            l_sc[...] = jnp.zeros(l_sc.shape, F32)
            acc_sc[...] = jnp.zeros(acc_sc.shape, F32)

        def step(masked):
            s = _dot(q_ref[...], k_ref[...], _NT)
            if masked:
                row = lax.broadcasted_iota(jnp.int32, (tq, tk), 0) + i * tq
                col = lax.broadcasted_iota(jnp.int32, (tq, tk), 1) + j * tk
                s = jnp.where(col <= row, s, NEG)
            m_prev = m_sc[...]
            m_new = jnp.maximum(m_prev, jnp.max(s, axis=-1, keepdims=True))
            alpha = jnp.exp2((m_prev - m_new) * exp2_scale)
            p = jnp.exp2((s - m_new) * exp2_scale)
            l_sc[...] = alpha * l_sc[...] + jnp.sum(p, axis=-1, keepdims=True)
            acc_sc[...] = alpha * acc_sc[...] + _dot(p.astype(BF16), v_ref[...], _NN)
            m_sc[...] = m_new

        active = j * tk <= i * tq + tq - 1
        crosses = (j + 1) * tk - 1 > i * tq

        @pl.when(active & jnp.logical_not(crosses))
        def _():
            step(False)

        @pl.when(active & crosses)
        def _():
            step(True)

        @pl.when(j == last_k(i))
        def _():
            o_ref[...] = acc_sc[...] / l_sc[...]
            lse_ref[...] = jnp.broadcast_to(m_sc[...] * MLA_SCALE + jnp.log(l_sc[...]), (tq, V_DIM))

        end(ci_refs, co_refs, sems)

    return pl.pallas_call(
        body, name="mla_fwd", grid=grid,
        in_specs=[pl.BlockSpec((None, tq, QK_PAD), lambda h, i, j: (h, i, 0)),
                  pl.BlockSpec((None, tk, QK_PAD), lambda h, i, j: (h, jnp.minimum(j, last_k(i)), 0)),
                  pl.BlockSpec((tk, V_DIM), lambda h, i, j: (jnp.minimum(j, last_k(i)), h))] + c_in_specs,
        out_specs=[pl.BlockSpec((tq, V_DIM), lambda h, i, j: (i, h)), pl.BlockSpec((tq, V_DIM), lambda h, i, j: (i, h))]
        + c_out_specs,
        out_shape=[_sds((n_rows, MLA_WIDTH)), _sds((n_rows, MLA_WIDTH))] + c_outs,
        scratch_shapes=[pltpu.VMEM((tq, 1), F32), pltpu.VMEM((tq, 1), F32), pltpu.VMEM((tq, V_DIM), F32)]
        + (comm.sem_scratch() if comm else []),
        compiler_params=_params(("arbitrary",) * 3 if comm else ("parallel", "parallel", "arbitrary")),
    )(qf, kf, v, *c_ins)


def _mm_res_ln(name, a, w, xres, g, b, tm):
    def epi(acc, xr, g_v, b_v):
        r = ALPHA * xr + acc
        y = _layer_norm(r, g_v, b_v)
        return y, y, r

    return _mm(name, a, w, (F32, BF16, F32), tm=tm, tn=D_MODEL, tk=a.shape[1], epilogue=epi, extras=[xres], vecs=[g, b])


def _mm_relu2(x_bf, w, comm=None):
    def epi(acc):
        r = jnp.maximum(acc, 0.0)
        return (r * r,)

    return _mm("ff1", x_bf, w, (BF16,), tm=2048, tn=1024, tk=D_MODEL, epilogue=epi, b_pieces=True, comm=comm)


def _loss_fn(y, t):
    def fn(y_v, t_v):
        d = y_v - t_v
        part = jnp.sum(jnp.sum(d * d, axis=1, keepdims=True), axis=0, keepdims=True)
        return d * (1.0 / D_MODEL), part

    dy, part = _rowwise("loss", fn, [y, t], [_sds(y.shape)], reds=[_sds((1, 1))])
    return part * (0.5 / D_MODEL), dy


def _ln_bwd_rows(dy, r, g):
    mu = jnp.mean(r, axis=-1, keepdims=True)
    xc = r - mu
    rstd = lax.rsqrt(jnp.mean(xc * xc, axis=-1, keepdims=True) + LN_EPS)
    xh = xc * rstd
    dxh = dy * g
    dr = rstd * (dxh - jnp.mean(dxh, axis=-1, keepdims=True) - xh * jnp.mean(dxh * xh, axis=-1, keepdims=True))
    return dr, dr, jnp.sum(dy * xh, axis=0, keepdims=True), jnp.sum(dy, axis=0, keepdims=True)


def _ln_bwd(name, dy, r, g):
    return _rowwise(name, _ln_bwd_rows, [dy, r], [_sds(dy.shape), _sds(dy.shape, BF16)], consts=[g],
                    reds=[_sds((1, D_MODEL)), _sds((1, D_MODEL))])


def _head_sum_matrix():
    i = lax.broadcasted_iota(jnp.int32, (A_WIDTH, LANES), 0) // A_HEAD_DIM
    return (i == _slot_head(lax.broadcasted_iota(jnp.int32, (A_WIDTH, LANES), 1))).astype(BF16)


def _mix_bwd(dmixed, a_out, b_out, ols, a_g, b_g):
    def fn(dm, a, b, ol1, ol2, ol3, ag, bg, sum_mat, spread):
        da, dga = _rms_bwd(a, ag, dm[:, :A_WIDTH])
        db, dgb = _rms_bwd(b, bg, dm[:, A_WIDTH:])
        t = da * a
        t_hi = t.astype(BF16)
        t_lo = (t - t_hi.astype(F32)).astype(BF16)
        tsum = _dot(t_hi, sum_mat, _NN) + _dot(t_lo, sum_mat, _NN)
        tb = db * b
        delta_b = jnp.concatenate(
            [jnp.broadcast_to(jnp.sum(tb[:, h * V_DIM:(h + 1) * V_DIM], axis=-1, keepdims=True), (tb.shape[0], V_DIM))
             for h in range(MLA_HEADS)], axis=1)
        dods = [jnp.concatenate([_pack_halves(_spread_slots(w, spread) * da), w * tsum], axis=1)
                for w in _pattern_weights(ol1, ol2, ol3)]
        return (*dods, db, delta_b, dga, dgb)

    n_rows = a_out.shape[0]
    wide = (n_rows, A_WIDTH)
    dils = [dil for _, dil in PATTERNS]
    return _rowwise("mix_bwd", fn, [dmixed, a_out, b_out, *ols],
                    [_sds((n_rows // d, d * OL_WIDTH)) for d in dils] + [_sds(wide, BF16), _sds(wide)],
                    consts=[a_g, b_g, _head_sum_matrix(), _slot_spread_matrix()],
                    reds=[_sds((1, A_WIDTH)), _sds((1, MLA_WIDTH))], ts=256,
                    in_dils=[1, 1, 1] + dils, out_dils=dils + [1, 1])


def _dil_bwd(qkv_view, ol_view, dod_view, dil, comm=None):
    nb = qkv_view.shape[0] // SPAN
    group = min(dil, DIL_GROUP)
    grid = (dil // group, nb + 1)
    c_ins, c_in_specs, c_outs, c_out_specs, begin, end = _hosted(comm, grid)
    n_ci, n_co = len(c_ins), len(c_outs)

    def body(cur_ref, prev_ref, ol_ref, dod_ref, *refs):
        ci_refs, out_ref, co_refs = refs[:n_ci], refs[n_ci], refs[n_ci + 1:n_ci + 1 + n_co]
        carry, sems = refs[n_ci + 1 + n_co], refs[n_ci + 2 + n_co:]
        m = pl.program_id(1)
        begin(ci_refs, co_refs, sems)

        def write_packed(g, pairs):
            lo = DQKV_WIDTH * g
            for p, (dq_p, dk_p, _) in enumerate(pairs):
                out_ref[:, lo + p * LANES:lo + (p + 1) * LANES] = _pack2(dq_p, dk_p)
            for p in range(2):
                out_ref[:, lo + A_WIDTH + p * LANES:lo + A_WIDTH + (p + 1) * LANES] = _pack2(pairs[p][2], pairs[p + 2][2])

        @pl.when(m == 0)
        def _():
            carry[...] = jnp.zeros(carry.shape, F32)

        @pl.when(m < nb)
        def _():
            mask, even = _band_mask(m, 2), _even_lanes(SPAN)
            for g in range(group):
                done = []
                for p in range(A_HEADS // 2):
                    q_sl, k_sl, v_sl = (slice((3 * g + t) * A_WIDTH + p * LANES, (3 * g + t) * A_WIDTH + (p + 1) * LANES)
                                        for t in range(3))
                    do_sl = slice(OL_WIDTH * g + (p % 2) * LANES, OL_WIDTH * g + (p % 2 + 1) * LANES)
                    s_e, s_o = (OL_WIDTH * g + O_WORDS + _slot_lane(2 * p + t) for t in range(2))
                    q2, do2 = cur_ref[:, q_sl], _unpack2(dod_ref[:, do_sl])[p // 2].astype(BF16)
                    zero = jnp.zeros_like(q2)
                    qcat = jnp.concatenate([jnp.where(even, q2, zero), jnp.where(even, zero, q2)], axis=0)
                    docat = jnp.concatenate([jnp.where(even, do2, zero), jnp.where(even, zero, do2)], axis=0)
                    kcat = jnp.concatenate([prev_ref[:, k_sl], cur_ref[:, k_sl]], axis=0)
                    vcat = jnp.concatenate([prev_ref[:, v_sl], cur_ref[:, v_sl]], axis=0)
                    lse_c = jnp.concatenate([ol_ref[:, s_e:s_e + 1], ol_ref[:, s_o:s_o + 1]], axis=0)
                    dl_c = jnp.concatenate([dod_ref[:, s_e:s_e + 1], dod_ref[:, s_o:s_o + 1]], axis=0)
                    pr = jnp.exp(jnp.where(mask, _dot(qcat, kcat, _NT), NEG) - lse_c)
                    ds = (pr * (_dot(docat, vcat, _NT) - dl_c)).astype(BF16)
                    dq = _dot(ds, kcat, _NN)
                    dk2 = _dot(ds, qcat, _TN)
                    dv2 = _dot(pr.astype(BF16), docat, _TN)
                    done.append((carry[:, q_sl], carry[:, k_sl] + dk2[:SPAN], carry[:, v_sl] + dv2[:SPAN]))
                    carry[:, q_sl] = jnp.where(even, dq[:SPAN], dq[SPAN:])
                    carry[:, k_sl] = dk2[SPAN:]
                    carry[:, v_sl] = dv2[SPAN:]
                write_packed(g, done)

        @pl.when(m == nb)
        def _():
            for g in range(group):
                write_packed(g, [tuple(carry[:, (3 * g + t) * A_WIDTH + p * LANES:(3 * g + t) * A_WIDTH + (p + 1) * LANES]
                                       for t in range(3)) for p in range(A_HEADS // 2)])

        end(ci_refs, co_refs, sems)

    last = nb - 1

    def cur(width):
        return pl.BlockSpec((SPAN, group * width), lambda r, m: (jnp.minimum(m, last), r))

    def prev(width):
        return pl.BlockSpec((SPAN, group * width), lambda r, m: (jnp.clip(m - 1, 0, last), r))

    return pl.pallas_call(
        body, name=f"dil_bwd_d{dil}", grid=grid,
        in_specs=[cur(3 * A_WIDTH), prev(3 * A_WIDTH), cur(OL_WIDTH), cur(OL_WIDTH)] + c_in_specs,
        out_specs=[prev(DQKV_WIDTH)] + c_out_specs,
        out_shape=[_sds((qkv_view.shape[0], dil * DQKV_WIDTH))] + c_outs,
        scratch_shapes=[pltpu.VMEM((SPAN, group * 3 * A_WIDTH), F32)] + (comm.sem_scratch() if comm else []),
        compiler_params=_params(("arbitrary", "arbitrary") if comm else ("parallel", "arbitrary")),
    )(qkv_view, qkv_view, ol_view, dod_view, *c_ins)


def _mla_bwd(qf, kf, v, lse, do, delta, comm=None):
    n_rows = v.shape[0]
    tq, tk = min(MLA_BWD_TILES[0], n_rows), min(MLA_BWD_TILES[1], n_rows)
    nq, nk = n_rows // tq, n_rows // tk

    def first_q(j):
        return jnp.right_shift(j * tk, int(math.log2(tq)))

    grid = (MLA_HEADS, nk, nq)
    c_ins, c_in_specs, c_outs, c_out_specs, begin, end = _hosted(comm, grid)
    n_ci, n_co = len(c_ins), len(c_outs)

    def body(q_ref, k_ref, v_ref, lse_ref, do_ref, dl_ref, *refs):
        ci_refs, (dq_ref, dk_ref, dv_ref), co_refs = refs[:n_ci], refs[n_ci:n_ci + 3], refs[n_ci + 3:n_ci + 3 + n_co]
        dk_sc, dv_sc = refs[n_ci + 3 + n_co:n_ci + 5 + n_co]
        sems = refs[n_ci + 5 + n_co:]
        j, i = pl.program_id(1), pl.program_id(2)
        begin(ci_refs, co_refs, sems)

        @pl.when((j == 0) & (i == 0))
        def _():
            dq_ref[...] = jnp.zeros(dq_ref.shape, F32)

        @pl.when(i == first_q(j))
        def _():
            dk_sc[...] = jnp.zeros(dk_sc.shape, F32)
            dv_sc[...] = jnp.zeros(dv_sc.shape, F32)

        def step(masked):
            q, k, dob = q_ref[...], k_ref[...], do_ref[...]
            s = _dot(q, k, _NT) * MLA_SCALE
            if masked:
                row = lax.broadcasted_iota(jnp.int32, (tq, tk), 0) + i * tq
                col = lax.broadcasted_iota(jnp.int32, (tq, tk), 1) + j * tk
                s = jnp.where(col <= row, s, NEG)
            p = jnp.exp(s - lse_ref[:, :1])
            dp = _dot(dob, v_ref[...], _NT)
            ds = (p * (dp - dl_ref[:, :1]) * MLA_SCALE).astype(BF16)
            dv_sc[...] += _dot(p.astype(BF16), dob, _TN)
            dk_sc[...] += _dot(ds, q, _TN)
            rows = pl.ds(pl.multiple_of(i * tq, tq), tq)
            dq_ref[rows, :] += _dot(ds, k, _NN)

        active = i >= first_q(j)
        crosses = (j + 1) * tk - 1 > i * tq

        @pl.when(active & jnp.logical_not(crosses))
        def _():
            step(False)

        @pl.when(active & crosses)
        def _():
            step(True)

        @pl.when(i == nq - 1)
        def _():
            dk_ref[...] = dk_sc[...]
            dv_ref[...] = dv_sc[...]

        end(ci_refs, co_refs, sems)

    qrow = lambda h, j, i: (jnp.maximum(i, first_q(j)), h)
    return pl.pallas_call(
        body, name="mla_bwd", grid=grid,
        in_specs=[pl.BlockSpec((None, tq, QK_PAD), lambda h, j, i: (h, jnp.maximum(i, first_q(j)), 0)),
                  pl.BlockSpec((None, tk, QK_PAD), lambda h, j, i: (h, j, 0)),
                  pl.BlockSpec((tk, V_DIM), lambda h, j, i: (j, h)),
                  pl.BlockSpec((tq, V_DIM), qrow), pl.BlockSpec((tq, V_DIM), qrow), pl.BlockSpec((tq, V_DIM), qrow)]
        + c_in_specs,
        out_specs=[pl.BlockSpec((None, n_rows, QK_PAD), lambda h, j, i: (h, 0, 0)),
                   pl.BlockSpec((None, tk, QK_PAD), lambda h, j, i: (h, j, 0)),
                   pl.BlockSpec((tk, V_DIM), lambda h, j, i: (j, h))] + c_out_specs,
        out_shape=[_sds((MLA_HEADS, n_rows, QK_PAD)), _sds((MLA_HEADS, n_rows, QK_PAD)), _sds((n_rows, MLA_WIDTH))] + c_outs,
        scratch_shapes=[pltpu.VMEM((tk, QK_PAD), F32), pltpu.VMEM((tk, V_DIM), F32)] + (comm.sem_scratch() if comm else []),
        compiler_params=_params(("arbitrary",) * 3 if comm else ("parallel", "arbitrary", "arbitrary")),
    )(qf, kf, v, lse, do, delta, *c_ins)


def _mla_bwd_prep(dqf, dkf, dv, cos, sin):
    def fn(dqf_v, dkf_v, dv_v, cos_v, sin_v):
        dq = jnp.concatenate([dqf_v[h][:, :QK_NOPE] for h in range(MLA_HEADS)]
                             + [_rope_t(dqf_v[h][:, QK_NOPE:], cos_v[:, :LANES], sin_v[:, :LANES]) for h in range(MLA_HEADS)], axis=1)
        dkv = jnp.concatenate([dkf_v[h][:, :QK_NOPE] for h in range(MLA_HEADS)] + [dv_v], axis=1)
        dkpe = dkf_v[0][:, QK_NOPE:] + dkf_v[1][:, QK_NOPE:] + dkf_v[2][:, QK_NOPE:] + dkf_v[3][:, QK_NOPE:]
        return dq, dkv, dkpe

    n_rows = dv.shape[0]
    return _rowwise("mla_bwd_prep", fn, [dqf, dkf, dv, cos, sin],
                    [_sds((n_rows, UQ_PAD), BF16), _sds((n_rows, 1024), BF16), _sds((n_rows, LANES))], ts=256)


def _assemble_dh(dqkvs, dcqn, cq, dckvn, ckv, dkpe, cos, sin, gq, gkv, comm=None):
    def fn(g1, g2, g3, dcqn_v, cq_v, dckvn_v, ckv_v, dkpe_v, cos_v, sin_v, gq_v, gkv_v):
        dq = dk = dv = 0.0
        for packed in (g1, g2, g3):
            dq_p, dk_p = _unpack2(packed[:, :A_WIDTH])
            dv_p = jnp.concatenate(_unpack2(packed[:, A_WIDTH:]), axis=1)
            dq, dk, dv = dq + dq_p, dk + dk_p, dv + dv_p
        dqa = _rope_t(dq, cos_v, sin_v) * A_SCALE
        dka = _rope_t(dk, cos_v, sin_v)
        dcq, dgq = _rms_bwd(cq_v, gq_v, dcqn_v)
        dckv, dgkv = _rms_bwd(ckv_v, gkv_v, dckvn_v)
        dkr = _rope_t(dkpe_v, cos_v[:, :LANES], sin_v[:, :LANES])
        return jnp.concatenate([dqa, dka, dv, dcq, dkr, dckv], axis=1), dgq, dgkv

    n_rows = cq.shape[0]
    return _rowwise("assemble_dh", fn, [*dqkvs, dcqn, cq, dckvn, ckv, dkpe, cos, sin],
                    [_sds((n_rows, IN_PAD), BF16)], consts=[gq, gkv],
                    reds=[_sds((1, Q_LORA)), _sds((1, KV_LORA))], ts=256,
                    in_dils=[dil for _, dil in PATTERNS] + [1] * 7, comm=comm)


def _layer_fwd(x_f32, x_bf, w, sm, cos, sin, behind=(None, None), rest=None):
    *qkv_views, cq, kpe, ckv = _in_proj(x_bf, w["w_in"], cos, sin)
    ol_views, brought = [], []
    for i, (view, (_, dil)) in enumerate(zip(qkv_views, PATTERNS)):
        ol, *got = _dil_fwd(view, dil, comm=rest[0][i] if rest else None)
        ol_views.append(ol)
        brought.append(got)
    if rest:
        w = {**w, **rest[1](brought)}
    cqn, qf, ckvn, kf, v = _mla_prep(cq, ckv, kpe, cos, sin, sm["q_a_norm"], w["w_uq"], sm["kv_a_norm"], w["w_ukv"])
    b_out, b_lse, *behind_attn = _mla_fwd(qf, kf, v, comm=behind[0])
    a_out, mixed = _mix_fwd(ol_views, b_out, sm["a_out_norm"], sm["b_out_norm"])
    x1, x1_bf, r1 = _mm_res_ln("wo_ln1", mixed, w["w_o"], x_f32, sm["ln1_g"], sm["ln1_b"], tm=1024)
    f, *behind_ff1 = _mm_relu2(x1_bf, w["w_ff1"], comm=behind[1])
    x2, x2_bf, r2 = _mm_res_ln("ff2_ln2", f, w["w_ff2"], x1, sm["ln2_g"], sm["ln2_b"], tm=512)
    saved = dict(w=w, x_bf=x_bf, qkv_views=qkv_views, ol_views=ol_views, cq=cq, ckv=ckv, cqn=cqn, qf=qf, ckvn=ckvn,
                 kf=kf, v=v, b_out=b_out, b_lse=b_lse, a_out=a_out, mixed=mixed, x1_bf=x1_bf, r1=r1, f=f, r2=r2)
    return x2, x2_bf, saved, (behind_attn, behind_ff1)


def _layer_bwd(dx2, w, sm, sv, cos, sin, reduce_of, pending=None):
    dr2, dr2_bf, dg2, db2 = _ln_bwd("ln2_bwd", dx2, sv["r2"], sm["ln2_g"])
    (dw_ff2,) = _mm("dw_ff2", sv["f"], dr2_bf, (F32,), ta=True, tm=1024, tn=1024, tk=dx2.shape[0])
    du, *swapped = _mm("d_u", dr2_bf, w["w_ff2"], (BF16,), tb=True, tm=512, tn=D_FF, tk=D_MODEL,
                       epilogue=lambda acc, f: (acc * (2.0 * jnp.sqrt(f.astype(F32))),), extras=[sv["f"]],
                       comm=_swap_comm(pending.pieces) if pending else None)
    if pending:
        pending.after_swap(swapped)
    (dw_ff1,) = _mm("dw_ff1", sv["x1_bf"], du, (F32,), ta=True, tm=1024, tn=1024, tk=dx2.shape[0], out_pieces=True)
    (dy1,) = _mm("d_x1", du, w["w_ff1"], (F32,), tb=True, tm=512, tn=1024, tk=D_FF,
                 epilogue=lambda acc, d: (acc + ALPHA * d,), extras=[dr2], b_pieces=True)
    dr1, dr1_bf, dg1, db1 = _ln_bwd("ln1_bwd", dy1, sv["r1"], sm["ln1_g"])
    (dw_o,) = _mm("dw_o", sv["mixed"], dr1_bf, (F32,), ta=True, tm=256, tn=1024, tk=dx2.shape[0])
    early = reduce_of(dict(w_o=dw_o, w_ff1=dw_ff1, w_ff2=dw_ff2))
    (dmixed,) = _mm("d_mixed", dr1_bf, w["w_o"], (F32,), tb=True, tm=512, tn=1024, tk=D_MODEL)
    dod1, dod2, dod3, do_b, delta_b, dga, dgb = _mix_bwd(
        dmixed, sv["a_out"], sv["b_out"], sv["ol_views"], sm["a_out_norm"], sm["b_out_norm"])
    dqkvs = []
    for i, ((_, dil), qkv_view, ol_view, dod_view) in enumerate(zip(PATTERNS, sv["qkv_views"], sv["ol_views"], (dod1, dod2, dod3))):
        swap_here = early is not None and i == 0
        dqkv, *swapped = _dil_bwd(qkv_view, ol_view, dod_view, dil, comm=_swap_comm(early.pieces) if swap_here else None)
        dqkvs.append(dqkv)
        if swap_here:
            early.after_swap(swapped)
    to_scatter = (early.s16 if early else []) + (pending.s16 if pending else [])
    dqf, dkf, dv_b, *scattered = _mla_bwd(sv["qf"], sv["kf"], sv["v"], sv["b_lse"], do_b, delta_b,
                                          comm=_scatter_comm(to_scatter) if to_scatter else None)
    if early:
        early.after_scatter(scattered[:len(early.s16)])
    if pending:
        pending.after_scatter(scattered[len(scattered) - len(pending.s16):])
    dq_b, dkv_b, dkpe = _mla_bwd_prep(dqf, dkf, dv_b, cos, sin)
    (dw_uq,) = _mm("dw_uq", sv["cqn"], dq_b, (F32,), ta=True, tm=Q_LORA, tn=1024, tk=2048)
    (dcqn,) = _mm("d_cqn", dq_b, w["w_uq"], (F32,), tb=True, tm=1024, tn=Q_LORA, tk=UQ_PAD)
    (dw_ukv,) = _mm("dw_ukv", sv["ckvn"], dkv_b, (F32,), ta=True, tm=KV_LORA, tn=1024, tk=2048)
    (dckvn,) = _mm("d_ckvn", dkv_b, w["w_ukv"], (F32,), tb=True, tm=1024, tn=KV_LORA, tk=1024)
    to_share = (early.mine if early else []) + (pending.mine if pending else [])
    dh, dgq, dgkv, *shared = _assemble_dh(dqkvs, dcqn, sv["cq"], dckvn, sv["ckv"], dkpe, cos, sin,
                                          sm["q_a_norm"], sm["kv_a_norm"], comm=_share_comm(to_share) if to_share else None)
    if early:
        early.after_share(shared[:len(early.mine)])
    if pending:
        pending.after_share(shared[len(shared) - len(pending.mine):])
    (dw_in,) = _mm("dw_in", sv["x_bf"], dh, (F32,), ta=True, tm=1024, tn=1024, tk=dx2.shape[0])
    (dx,) = _mm("d_x", dh, w["w_in"], (F32,), tb=True, tm=512, tn=1024, tk=IN_PAD,
                epilogue=lambda acc, d: (acc + ALPHA * d,), extras=[dr1])
    late = reduce_of(dict(w_in=dw_in, w_uq=dw_uq, w_ukv=dw_ukv))
    dsm = dict(q_a_norm=dgq, kv_a_norm=dgkv, a_out_norm=dga, b_out_norm=dgb, ln1_g=dg1, ln1_b=db1, ln2_g=dg2, ln2_b=db2)
    return dx, dsm, late


def _pad_w_in(w):
    return jnp.concatenate([w[:, :1792], w[:, 1920:1984], jnp.zeros((w.shape[0], 64), w.dtype), w[:, 1792:1920]], axis=1)


def _unpad_w_in(w):
    return jnp.concatenate([w[:, :1792], w[:, 1920:2048], w[:, 1792:1856]], axis=1)


def _rope_tables(n_rows):
    half = A_HEAD_DIM // 2
    inv_freq = ROPE_THETA ** (-jnp.arange(half, dtype=F32) / half)
    ang = jnp.arange(n_rows, dtype=F32)[:, None] * inv_freq[None, :]
    cos = jnp.tile(jnp.cos(ang), (1, 2 * A_HEADS))
    sin = jnp.tile(jnp.concatenate([-jnp.sin(ang), jnp.sin(ang)], axis=1), (1, A_HEADS))
    return cos, sin


W_NAMES = ("w_in", "w_uq", "w_ukv", "w_o", "w_ff1", "w_ff2")
PIECE = dict(w_in=(512, 512), w_uq=(128, 256), w_ukv=(64, 256), w_o=(128, 1024), w_ff1=(512, 1024), w_ff2=(512, 1024))
SHARD_COLS = dict(w_in=496, w_uq=192, w_ukv=256, w_o=1024, w_ff1=1024, w_ff2=1024)


def _col_pieces(t, pad_to):
    rows, _, cols = t.shape
    t = jnp.pad(t, ((0, 0), (0, 0), (0, pad_to - cols)))
    return t.reshape(2, rows // 2, N_CHIPS, pad_to).transpose(2, 0, 1, 3)


def _grad_pieces(dws):
    def heads(t, rows):
        return jnp.concatenate([t[:, :512].reshape(rows, MLA_HEADS, LANES), t[:, 512:].reshape(rows, MLA_HEADS, LANES)], axis=2)

    make = dict(
        w_in=lambda t: _col_pieces(_unpad_w_in(t).reshape(D_MODEL, N_CHIPS, SHARD_COLS["w_in"]), 512),
        w_uq=lambda t: _col_pieces(heads(t, Q_LORA), 256),
        w_ukv=lambda t: _col_pieces(heads(t, KV_LORA), 256),
        w_o=lambda t: t.reshape(N_CHIPS, 2, 128, D_MODEL),
        w_ff1=lambda t: t,
        w_ff2=lambda t: t.reshape(N_CHIPS, 2, 512, D_MODEL))
    return {n: make[n](t) for n, t in dws.items()}


def _weights_from_pieces(p):
    def cols(t):
        return t.transpose(1, 2, 0, 3).reshape(2 * t.shape[2], N_CHIPS, t.shape[3])

    def heads(t, rows):
        return jnp.concatenate([t[:, :, :LANES].reshape(rows, 512), t[:, :, LANES:].reshape(rows, 512)], axis=1)

    make = dict(
        w_in=lambda t: _pad_w_in(cols(t)[:, :, :SHARD_COLS["w_in"]].reshape(D_MODEL, IN_COLS)),
        w_uq=lambda t: heads(cols(t), Q_LORA),
        w_ukv=lambda t: heads(cols(t), KV_LORA),
        w_o=lambda t: t.reshape(D_MODEL, D_MODEL),
        w_ff1=lambda t: t,
        w_ff2=lambda t: t.reshape(D_FF, D_MODEL))
    return {n: make[n](t) for n, t in p.items()}


def _my_piece(name, shard, half):
    rows, cols = shard.shape[0] // 2, shard.shape[1]
    t = lax.dynamic_slice_in_dim(shard, half * rows, rows, axis=0).astype(BF16)
    return jnp.pad(t, ((0, 0), (0, PIECE[name][1] - cols)))


MESH = pl.DeviceIdType.MESH
ANY = pl.BlockSpec(memory_space=pl.ANY)


class _Comm:
    def __init__(self, ins, out_shapes, n_sems, start, finish):
        self.ins, self.out_shapes, self.n_sems, self.start, self.finish = list(ins), list(out_shapes), n_sems, start, finish

    def sem_scratch(self):
        return [pltpu.SemaphoreType.DMA((self.n_sems,)), pltpu.SemaphoreType.DMA((self.n_sems,))]


def _comm_call(name, comm):
    n_in, n_out = len(comm.ins), len(comm.out_shapes)

    def body(*refs):
        ins, outs, sems = refs[:n_in], refs[n_in:n_in + n_out], refs[n_in + n_out:]
        comm.start(ins, outs, *sems)
        comm.finish(ins, outs, *sems)

    return pl.pallas_call(body, name=name, out_shape=comm.out_shapes, in_specs=[ANY] * n_in, out_specs=[ANY] * n_out,
                          scratch_shapes=comm.sem_scratch())(*comm.ins)


def _all_gather_comm(blocks):
    n = len(blocks)

    def plan(x_refs, out_refs, send_sems, recv_sems):
        x, y, c = lax.axis_index("x"), lax.axis_index("y"), lax.axis_index("c")
        me, sibling = (x, y, c), (x, y, 1 - c)
        chips = [(1 - x, y), (x, 1 - y), (1 - x, 1 - y)]

        def copy(t, k, blk, to, src=None):
            px, py, pc = blk
            slot = out_refs[t].at[4 * px + 2 * py + pc]
            return pltpu.make_async_remote_copy(
                src_ref=slot if src is None else src, dst_ref=slot,
                send_sem=send_sems.at[7 * t + k], recv_sem=recv_sems.at[7 * t + k], device_id=to, device_id_type=MESH)

        first = []
        for t in range(n):
            first.append(copy(t, 0, me, sibling, src=x_refs[t]))
            first += [copy(t, 1 + j, me, (*chip, c), src=x_refs[t]) for j, chip in enumerate(chips)]
        return me, sibling, chips, c, copy, first

    def start(x_refs, out_refs, send_sems, recv_sems):
        for cp in plan(x_refs, out_refs, send_sems, recv_sems)[-1]:
            cp.start()

    def finish(x_refs, out_refs, send_sems, recv_sems):
        me, sibling, chips, c, copy, sent = plan(x_refs, out_refs, send_sems, recv_sems)
        for j, chip in enumerate(chips):
            for t in range(n):
                copy(t, 1 + j, (*chip, c), me).wait_recv()
                sent.append(copy(t, 4 + j, (*chip, c), sibling))
                sent[-1].start()
        for t in range(n):
            copy(t, 0, sibling, me).wait_recv()
        for j, chip in enumerate(chips):
            for t in range(n):
                copy(t, 4 + j, (*chip, 1 - c), me).wait_recv()
        for cp in sent:
            cp.wait_send()

    return _Comm(blocks, [_sds((N_DEV,) + b.shape, b.dtype) for b in blocks], 7 * n, start, finish)


def _simple_comm(ins, out_shapes, n_sems, copies):
    def start(in_refs, out_refs, send_sems, recv_sems):
        for cp in copies(in_refs, out_refs, send_sems, recv_sems):
            cp.start()

    def finish(in_refs, out_refs, send_sems, recv_sems):
        for cp in copies(in_refs, out_refs, send_sems, recv_sems):
            cp.wait()

    return _Comm(ins, out_shapes, n_sems, start, finish)


def _swap_comm(gs):
    def copies(g_refs, got_refs, send_sems, recv_sems):
        c = lax.axis_index("c")
        sibling = (lax.axis_index("x"), lax.axis_index("y"), 1 - c)
        return [pltpu.make_async_remote_copy(src_ref=g_refs[t].at[k, 1 - c], dst_ref=got_refs[t].at[k],
                                             send_sem=send_sems.at[N_CHIPS * t + k], recv_sem=recv_sems.at[N_CHIPS * t + k],
                                             device_id=sibling, device_id_type=MESH)
                for t in range(len(gs)) for k in range(N_CHIPS)]

    return _simple_comm(gs, [_sds((N_CHIPS,) + g.shape[2:], g.dtype) for g in gs], N_CHIPS * len(gs), copies)


def _scatter_comm(hs):
    def copies(h_refs, got_refs, send_sems, recv_sems):
        x, y, c = lax.axis_index("x"), lax.axis_index("y"), lax.axis_index("c")
        chips = [(1 - x, y), (x, 1 - y), (1 - x, 1 - y)]
        return [pltpu.make_async_remote_copy(src_ref=h_refs[t].at[2 * px + py], dst_ref=got_refs[t].at[rel],
                                             send_sem=send_sems.at[3 * t + rel], recv_sem=recv_sems.at[3 * t + rel],
                                             device_id=(px, py, c), device_id_type=MESH)
                for t in range(len(hs)) for rel, (px, py) in enumerate(chips)]

    return _simple_comm(hs, [_sds((3,) + h.shape[1:], h.dtype) for h in hs], 3 * len(hs), copies)


def _share_comm(rs):
    def copies(r_refs, got_refs, send_sems, recv_sems):
        sibling = (lax.axis_index("x"), lax.axis_index("y"), 1 - lax.axis_index("c"))
        return [pltpu.make_async_remote_copy(src_ref=r_refs[t], dst_ref=got_refs[t], send_sem=send_sems.at[t],
                                             recv_sem=recv_sems.at[t], device_id=sibling, device_id_type=MESH)
                for t in range(len(rs))]

    return _simple_comm(rs, [_sds(r.shape, r.dtype) for r in rs], len(rs), copies)


SUM_STEPS = 4


def _pair_sums(name, gs, gots, where):
    n = len(gs)
    dims = [(g.shape[2] // SUM_STEPS, g.shape[3]) for g in gs]

    def body(where_ref, *refs):
        own = pl.program_id(1) == where_ref[1]
        for t in range(n):
            s = refs[t][...] + refs[n + t][...]
            refs[3 * n + t][...] = s.astype(BF16)

            @pl.when(own)
            def _():
                refs[2 * n + t][...] = s

    def part(tr, pc):
        return pl.BlockSpec((None, tr, pc), lambda i, k, where_ref: (k, i, 0))

    def kept(tr, pc):
        return pl.BlockSpec((None, None, tr, pc), lambda i, k, where_ref: (k, where_ref[0], i, 0))

    outs = pl.pallas_call(
        body, name=name,
        out_shape=[_sds(g.shape[2:]) for g in gs] + [_sds((N_CHIPS,) + g.shape[2:], BF16) for g in gs],
        grid_spec=pltpu.PrefetchScalarGridSpec(
            num_scalar_prefetch=1, grid=(SUM_STEPS, N_CHIPS),
            in_specs=[kept(*d) for d in dims] + [part(*d) for d in dims],
            out_specs=[pl.BlockSpec(d, lambda i, k, where_ref: (i, 0)) for d in dims] + [part(*d) for d in dims]),
        compiler_params=_params(("parallel", "arbitrary")),
    )(where, *gs, *gots)
    return list(outs[:n]), list(outs[n:])


def _chips_sums(name, owns, gots):
    n = len(owns)
    dims = [(h.shape[0] // SUM_STEPS, h.shape[1]) for h in owns]

    def body(*refs):
        for t in range(n):
            got_ref = refs[n + t]
            refs[2 * n + t][...] = refs[t][...] + got_ref[0] + got_ref[1] + got_ref[2]

    row = [pl.BlockSpec((tr, pc), lambda i: (i, 0)) for tr, pc in dims]
    return pl.pallas_call(
        body, name=name, out_shape=[_sds(h.shape) for h in owns], grid=(SUM_STEPS,),
        in_specs=row + [pl.BlockSpec((3, tr, pc), lambda i: (0, i, 0)) for tr, pc in dims], out_specs=row,
        compiler_params=_params(("parallel",)),
    )(*owns, *gots)


class _Reduce:
    def __init__(self, tag, dws, c, chip):
        gp = _grad_pieces(dws)
        self.tag, self.c, self.names, self.pieces = tag, c, list(gp), list(gp.values())
        self.where = jnp.stack([c, chip]).astype(jnp.int32)
        self.shards = None

    def after_swap(self, gots):
        self.s32, self.s16 = _pair_sums(f"rs_pair_sum_{self.tag}", self.pieces, gots, self.where)

    def after_scatter(self, parts):
        self.mine = list(_chips_sums(f"rs_chips_sum_{self.tag}", self.s32, parts))

    def after_share(self, theirs):
        self.shards = {}
        for n, a, b in zip(self.names, self.mine, theirs):
            both = jnp.where(self.c == 0, jnp.concatenate([a, b], axis=0), jnp.concatenate([b, a], axis=0))
            self.shards[n] = both[:, :SHARD_COLS[n]]

    def run_alone(self):
        self.after_swap(_comm_call(f"rs_pair_{self.tag}", _swap_comm(self.pieces)))
        self.after_scatter(_comm_call(f"rs_chips_{self.tag}", _scatter_comm(self.s16)))
        self.after_share(_comm_call(f"rs_share_{self.tag}", _share_comm(self.mine)))


class _LayerWeights:
    def __init__(self, shards, c, dev):
        self.shards, self.c, self.dev = shards, c, dev

    def comm(self, l, names=W_NAMES):
        return _all_gather_comm([_my_piece(n, self.shards[l][n], self.c) for n in names])

    def take(self, l, gathered, names=W_NAMES):
        pieces = {}
        for n, g in zip(names, gathered):
            mine = _my_piece(n, self.shards[l][n], self.c)
            pieces[n] = lax.dynamic_update_index_in_dim(g, mine, self.dev, 0).reshape((N_CHIPS, 2) + mine.shape)
        return _weights_from_pieces(pieces)

    def first(self):
        now, later = ("w_in", "w_uq", "w_ukv"), ("w_ff1", "w_ff2", "w_o")
        weights = self.take(0, _comm_call("ag_l0", self.comm(0, now)), now)
        comms = [self.comm(0, (n,)) for n in later]

        def take_rest(results):
            rest = {}
            for n, gathered in zip(later, results):
                rest.update(self.take(0, gathered, (n,)))
            return rest

        return weights, (comms, take_rest)


def _all_reduce_small(vec):
    rows, lanes = vec.shape

    def gather_body(x_ref, out_ref, send_sems, recv_sems):
        x, y, c = lax.axis_index("x"), lax.axis_index("y"), lax.axis_index("c")
        me, sibling = (x, y, c), (x, y, 1 - c)
        chips = [(1 - x, y), (x, 1 - y), (1 - x, 1 - y)]

        def slot(px, py, pc):
            return out_ref.at[4 * px + 2 * py + pc]

        def copy(k, blk, to, src=None):
            return pltpu.make_async_remote_copy(
                src_ref=slot(*blk) if src is None else src, dst_ref=slot(*blk),
                send_sem=send_sems.at[k], recv_sem=recv_sems.at[k], device_id=to, device_id_type=MESH)

        out_ref[4 * x + 2 * y + c] = x_ref[...]
        first = [copy(0, me, sibling, src=x_ref)]
        first += [copy(1 + j, me, (*chip, c), src=x_ref) for j, chip in enumerate(chips)]
        for cp in first:
            cp.start()
        passed = [copy(4 + j, (*chip, c), sibling) for j, chip in enumerate(chips)]
        for j, chip in enumerate(chips):
            copy(1 + j, (*chip, c), me).wait_recv()
            passed[j].start()
        copy(0, sibling, me).wait_recv()
        for j, chip in enumerate(chips):
            copy(4 + j, (*chip, 1 - c), me).wait_recv()
        for cp in first + passed:
            cp.wait_send()

    vmem = pl.BlockSpec(memory_space=pltpu.VMEM)
    allv = pl.pallas_call(
        gather_body, name="small_all_gather", out_shape=_sds((N_DEV, rows, lanes)),
        in_specs=[vmem], out_specs=vmem,
        scratch_shapes=[pltpu.SemaphoreType.DMA((7,)), pltpu.SemaphoreType.DMA((7,))],
    )(vec)

    def sum_body(a_ref, o_ref):
        acc = a_ref[0]
        for d in range(1, N_DEV):
            acc = acc + a_ref[d]
        o_ref[...] = acc

    return pl.pallas_call(sum_body, name="small_sum", out_shape=_sds((rows, lanes)), in_specs=[vmem], out_specs=vmem)(allv)


def _adamw(name, w, g, m, v, ts=512, comm=None):
    def fn(w_v, g_v, m_v, v_v):
        m_n = ADAM_B1 * m_v + (1.0 - ADAM_B1) * g_v
        v_n = ADAM_B2 * v_v + (1.0 - ADAM_B2) * (g_v * g_v)
        m_hat = m_n / (1.0 - ADAM_B1 ** ADAM_STEP)
        v_hat = v_n / (1.0 - ADAM_B2 ** ADAM_STEP)
        delta = -ADAM_LR * (m_hat / (jnp.sqrt(v_hat) + ADAM_EPS) + ADAM_WD * w_v)
        return delta, m_n, v_n

    return _rowwise(name, fn, [w, g, m, v], [_sds(w.shape)] * 3, ts=ts, comm=comm)


def _pack_small(per_layer):
    flat = jnp.concatenate([per_layer[l][n].reshape(-1) for l in range(DEPTH) for n in SMALL_NAMES])
    return jnp.pad(flat, (0, SMALL_ROWS * LANES - flat.shape[0])).reshape(SMALL_ROWS, LANES)


def _unpack_small(packed):
    flat = packed.reshape(-1)
    per = sum(SMALL_SIZES)
    out = {}
    for n, size, off in zip(SMALL_NAMES, SMALL_SIZES, [sum(SMALL_SIZES[:i]) for i in range(len(SMALL_SIZES))]):
        out[n] = jnp.stack([flat[l * per + off:l * per + off + size] for l in range(DEPTH)])
    return out


NEXT_WEIGHTS_BEHIND = (("w_ff1", "w_ff2"), ("w_in", "w_uq", "w_ukv", "w_o"))


def _fwd_bwd(x, target, layer_weights, smalls, on_layer_grads):
    depth = len(smalls)
    cos, sin = _rope_tables(x.shape[0])
    h_f32, h_bf = x, x.astype(BF16)
    saved = []
    w, rest = layer_weights.first()
    for l in range(depth):
        more = l + 1 < depth
        behind = [layer_weights.comm(l + 1, names) if more else None for names in NEXT_WEIGHTS_BEHIND]
        h_f32, h_bf, sv, brought = _layer_fwd(h_f32, h_bf, w, smalls[l], cos, sin, behind, rest)
        saved.append(sv)
        if more:
            w, rest = {}, None
            for names, got in zip(NEXT_WEIGHTS_BEHIND, brought):
                w.update(layer_weights.take(l + 1, got, names))
    loss_part, dh = _loss_fn(h_f32, target)
    small_grads, pending = [None] * depth, None
    for l in reversed(range(depth)):
        dh, small_grads[l], pending = _layer_bwd(dh, saved[l]["w"], smalls[l], saved[l], cos, sin,
                                                 lambda dws, l=l: on_layer_grads(l, dws), pending)
    return loss_part, dh, small_grads, pending


def kernel(x, w_in, q_a_norm, kv_a_norm, w_uq, w_ukv, a_out_norm, b_out_norm, w_o, ln1_g, ln1_b, w_ff1, w_ff2, ln2_g, ln2_b, loss_target, m_w_in, m_q_a_norm, m_kv_a_norm, m_w_uq, m_w_ukv, m_a_out_norm, m_b_out_norm, m_w_o, m_ln1_g, m_ln1_b, m_w_ff1, m_w_ff2, m_ln2_g, m_ln2_b, v_w_in, v_q_a_norm, v_kv_a_norm, v_w_uq, v_w_ukv, v_a_out_norm, v_b_out_norm, v_w_o, v_ln1_g, v_ln1_b, v_w_ff1, v_w_ff2, v_ln2_g, v_ln2_b):
    c = lax.axis_index("c")
    chip = 2 * lax.axis_index("x") + lax.axis_index("y")
    big = dict(w_in=w_in, w_uq=w_uq, w_ukv=w_ukv, w_o=w_o, w_ff1=w_ff1, w_ff2=w_ff2)
    big_m = dict(w_in=m_w_in, w_uq=m_w_uq, w_ukv=m_w_ukv, w_o=m_w_o, w_ff1=m_w_ff1, w_ff2=m_w_ff2)
    big_v = dict(w_in=v_w_in, w_uq=v_w_uq, w_ukv=v_w_ukv, w_o=v_w_o, w_ff1=v_w_ff1, w_ff2=v_w_ff2)
    small = dict(q_a_norm=q_a_norm, kv_a_norm=kv_a_norm, a_out_norm=a_out_norm, b_out_norm=b_out_norm,
                 ln1_g=ln1_g, ln1_b=ln1_b, ln2_g=ln2_g, ln2_b=ln2_b)
    small_m = dict(q_a_norm=m_q_a_norm, kv_a_norm=m_kv_a_norm, a_out_norm=m_a_out_norm, b_out_norm=m_b_out_norm,
                   ln1_g=m_ln1_g, ln1_b=m_ln1_b, ln2_g=m_ln2_g, ln2_b=m_ln2_b)
    small_v = dict(q_a_norm=v_q_a_norm, kv_a_norm=v_kv_a_norm, a_out_norm=v_a_out_norm, b_out_norm=v_b_out_norm,
                   ln1_g=v_ln1_g, ln1_b=v_ln1_b, ln2_g=v_ln2_g, ln2_b=v_ln2_b)

    layer_weights = _LayerWeights([{n: big[n][l] for n in W_NAMES} for l in range(DEPTH)], c, 2 * chip + c)
    smalls = [{n: small[n][l][None, :] for n in SMALL_NAMES} for l in range(DEPTH)]
    reductions = [[] for _ in range(DEPTH)]

    def reduce_layer(l, dws):
        reductions[l].append(_Reduce(f"l{l}_{len(reductions[l])}", dws, c, chip))
        return reductions[l][-1]

    loss_part, grad_x, small_grads, last = _fwd_bwd(x[0], loss_target[0], layer_weights, smalls, reduce_layer)
    loss = lax.psum(loss_part[0, 0], ("x", "y", "c"))
    grad_x = grad_x[None]

    g_small_packed = _all_reduce_small(_pack_small(small_grads))
    pack_in = lambda d: _pack_small([{n: d[n][l] for n in SMALL_NAMES} for l in range(DEPTH)])
    sd, sm_, sv_ = _adamw("adamw_small", pack_in(small), g_small_packed, pack_in(small_m), pack_in(small_v), ts=SMALL_ROWS)
    g_small, d_small, m_small, v_small = (_unpack_small(t) for t in (g_small_packed, sd, sm_, sv_))

    last.run_alone()
    g_big, d_big, m_big, v_big = {}, {}, {}, {}
    for n in W_NAMES:
        g = jnp.stack([next(r.shards[n] for r in reductions[l] if n in r.names) for l in range(DEPTH)])
        shape = g.shape
        flat = lambda t: t.reshape(shape[0] * shape[1], shape[2])
        d_, m_, v_ = _adamw(f"adamw_{n}", flat(big[n]), flat(g), flat(big_m[n]), flat(big_v[n]))
        g_big[n], d_big[n], m_big[n], v_big[n] = g, d_.reshape(shape), m_.reshape(shape), v_.reshape(shape)

    order = ("w_in", "q_a_norm", "kv_a_norm", "w_uq", "w_ukv", "a_out_norm", "b_out_norm", "w_o", "ln1_g", "ln1_b",
             "w_ff1", "w_ff2", "ln2_g", "ln2_b")
    pick = lambda bigd, smalld: [bigd[n] if n in bigd else smalld[n] for n in order]
    return (loss, grad_x, *pick(g_big, g_small), *pick(d_big, d_small), *pick(m_big, m_small), *pick(v_big, v_small))
```

```python
import math

import jax
import jax.numpy as jnp
from jax import lax
from jax.experimental import pallas as pl
from jax.experimental.pallas import tpu as pltpu

F32, BF16 = jnp.float32, jnp.bfloat16

D_MODEL = 1024
DEPTH = 4
A_HEAD_DIM = 64
A_HEADS = 8
A_WIDTH = 512
PATTERNS = ((128, 1), (512, 4), (2048, 16))
SPAN = 128
MLA_HEADS = 4
QK_NOPE = 128
QK_ROPE = 64
V_DIM = 128
Q_LORA = 256
KV_LORA = 128
MLA_WIDTH = 512
QK_PAD = 256
IN_COLS = 1984
IN_PAD = 2048
UQ_PAD = 1024
D_FF = 4096
ROPE_THETA = 10000.0
ALPHA = (2.0 * DEPTH) ** 0.25
LN_EPS = 1e-5
RMS_EPS = 1e-6
MLA_SCALE = (QK_NOPE + QK_ROPE) ** -0.5
A_SCALE = A_HEAD_DIM ** -0.5

ADAM_LR, ADAM_B1, ADAM_B2, ADAM_EPS, ADAM_WD, ADAM_STEP = 0.001, 0.9, 0.999, 1e-08, 0.01, 10

VMEM_LIMIT_BYTES = 56 * 1024 * 1024
NEG = -1e30
N_CHIPS, N_DEV = 4, 8
LANES = 128

SMALL_NAMES = ("q_a_norm", "kv_a_norm", "a_out_norm", "b_out_norm", "ln1_g", "ln1_b", "ln2_g", "ln2_b")
SMALL_SIZES = (256, 128, 512, 512, 1024, 1024, 1024, 1024)
SMALL_ROWS = 176


def _params(sem):
    return pltpu.CompilerParams(dimension_semantics=sem, vmem_limit_bytes=VMEM_LIMIT_BYTES)


FAST_STRIDE = 4


def _interleave(parts, scratch):
    d, (n, width) = len(parts), parts[0].shape
    if d > FAST_STRIDE and d % FAST_STRIDE == 0:
        inner = [_interleave(parts[r::FAST_STRIDE], scratch) for r in range(FAST_STRIDE)]
        return _interleave(inner, scratch)
    for r in range(d):
        for s in range(width // LANES):
            scratch.at[s][pl.ds(r, n, stride=d), :] = parts[r][:, s * LANES:(s + 1) * LANES]
    return jnp.concatenate([scratch.at[s][pl.ds(0, n * d), :] for s in range(width // LANES)], axis=1)


def _deinterleave(x, d, scratch):
    n, width = x.shape[0] // d, x.shape[1]
    if d > FAST_STRIDE and d % FAST_STRIDE == 0:
        outer = _deinterleave(x, FAST_STRIDE, scratch)
        inner = [_deinterleave(y, d // FAST_STRIDE, scratch) for y in outer]
        return [inner[r % FAST_STRIDE][r // FAST_STRIDE] for r in range(d)]
    for s in range(width // LANES):
        scratch.at[s][pl.ds(0, n * d), :] = x[:, s * LANES:(s + 1) * LANES]
    return [jnp.concatenate([scratch.at[s][pl.ds(r, n, stride=d), :] for s in range(width // LANES)], axis=1) for r in range(d)]


def _rowwise(name, fn, ins, outs, consts=(), reds=(), ts=512, in_dils=None, out_dils=None, comm=None):
    in_dils = list(in_dils or [1] * len(ins))
    out_dils = list(out_dils or [1] * len(outs))
    n_rows = ins[0].shape[-2] * in_dils[0]
    ts = min(ts, n_rows)
    assert n_rows % ts == 0
    n_in, n_c, n_o, n_r = len(ins), len(consts), len(outs), len(reds)
    viewed = [(a.shape, d) for a, d in zip(list(ins) + list(outs), in_dils + out_dils) if d > 1]
    grid = (n_rows // ts,)
    h_ins, h_in_specs, h_outs, h_out_specs, begin, end = _hosted(comm, grid)
    n_hi, n_ho = len(h_ins), len(h_outs)

    def tile_spec(shape, d=1):
        if len(shape) == 2:
            return pl.BlockSpec((ts // d, shape[1]), lambda i: (i, 0))
        return pl.BlockSpec((shape[0], ts, shape[2]), lambda i: (0, i, 0))

    def whole_spec(shape):
        return pl.BlockSpec(shape, lambda i: (0,) * len(shape))

    def body(*refs):
        in_refs, c_refs, hi_refs = refs[:n_in], refs[n_in:n_in + n_c], refs[n_in + n_c:n_in + n_c + n_hi]
        refs = refs[n_in + n_c + n_hi:]
        o_refs, r_refs, ho_refs = refs[:n_o], refs[n_o:n_o + n_r], refs[n_o + n_r:n_o + n_r + n_ho]
        scratches = list(refs[n_o + n_r + n_ho:])
        sems = [scratches.pop(), scratches.pop()][::-1] if comm else []
        begin(hi_refs, ho_refs, sems)
        vals = []
        for r, d in zip(in_refs, in_dils):
            if d == 1:
                vals.append(r[...])
            else:
                width = r.shape[1] // d
                vals.append(_interleave([r[:, k * width:(k + 1) * width] for k in range(d)], scratches.pop(0)))
        res = fn(*vals, *[r[...] for r in c_refs])
        res = tuple(res) if isinstance(res, (tuple, list)) else (res,)
        for r, v, d in zip(o_refs, res[:n_o], out_dils):
            if len(r.shape) == 3:
                for g in range(r.shape[0]):
                    r[g] = v[g].astype(r.dtype)
            elif d == 1:
                r[...] = v.astype(r.dtype)
            else:
                width = r.shape[1] // d
                for k, part in enumerate(_deinterleave(v, d, scratches.pop(0))):
                    r[:, k * width:(k + 1) * width] = part.astype(r.dtype)
        if n_r:
            i = pl.program_id(0)

            @pl.when(i == 0)
            def _():
                for r, v in zip(r_refs, res[n_o:]):
                    r[...] = v

            @pl.when(i > 0)
            def _():
                for r, v in zip(r_refs, res[n_o:]):
                    r[...] += v
        end(hi_refs, ho_refs, sems)

    out = pl.pallas_call(
        body, name=name, grid=grid,
        in_specs=[tile_spec(a.shape, d) for a, d in zip(ins, in_dils)] + [whole_spec(c.shape) for c in consts] + h_in_specs,
        out_specs=[tile_spec(o.shape, d) for o, d in zip(outs, out_dils)] + [whole_spec(r.shape) for r in reds] + h_out_specs,
        out_shape=list(outs) + list(reds) + h_outs,
        scratch_shapes=[pltpu.VMEM((shape[1] // d // LANES, ts, LANES), F32) for shape, d in viewed]
        + (comm.sem_scratch() if comm else []),
        compiler_params=_params(("arbitrary",)),
    )(*ins, *consts, *h_ins)
    return out


def _sds(shape, dtype=F32):
    return jax.ShapeDtypeStruct(tuple(shape), dtype)


def _hosted(comm, grid):
    if comm is None:
        return [], [], [], [], (lambda *a: None), (lambda *a: None)

    def edge(which, at, ins, outs, sems):
        ids = [pl.program_id(d) for d in range(len(grid))]
        hit = ids[0] == at[0]
        for i, v in zip(ids[1:], at[1:]):
            hit = hit & (i == v)

        @pl.when(hit)
        def _():
            which(ins, outs, *sems)

    late, at_late = (3 * math.prod(grid)) // 4, []
    for g in reversed(grid):
        late, r = divmod(late, g)
        at_late.insert(0, r)

    def begin(ins, outs, sems):
        edge(comm.start, [0] * len(grid), ins, outs, sems)
        if comm.middle:
            edge(comm.middle, at_late, ins, outs, sems)

    end = lambda ins, outs, sems: edge(comm.finish, [g - 1 for g in grid], ins, outs, sems)
    return comm.ins, [ANY] * len(comm.ins), comm.out_shapes, [ANY] * len(comm.out_shapes), begin, end


def _mm(name, a, b, out_dtypes, *, ta=False, tb=False, tm=512, tn=512, tk=512, epilogue=None, extras=(), vecs=(),
        comm=None, b_pieces=False, out_pieces=False):
    (k_dim, m_dim) = a.shape if ta else a.shape[::-1]
    if b_pieces:
        b_rows, b_cols = 2 * b.shape[2], N_CHIPS * b.shape[3]
        (n_dim, k2) = (b_rows, b_cols) if tb else (b_cols, b_rows)
    else:
        (n_dim, k2) = b.shape if tb else b.shape[::-1]
    assert k_dim == k2
    tm, tn, tk = min(tm, m_dim), min(tn, n_dim), min(tk, k_dim)
    assert m_dim % tm == 0 and n_dim % tn == 0 and k_dim % tk == 0
    nk = k_dim // tk
    grid = (m_dim // tm, n_dim // tn, nk)
    a_spec = pl.BlockSpec((tk, tm), lambda i, j, k: (k, i)) if ta else pl.BlockSpec((tm, tk), lambda i, j, k: (i, k))
    if b_pieces:
        assert tn == b_rows and tk % b.shape[3] == 0 if tb else (tk, tn) == (b_rows, b.shape[3])
        n_bp = tk // b.shape[3] if tb else 1
        b_spec = pl.BlockSpec((n_bp,) + b.shape[1:], (lambda i, j, k: (k, 0, 0, 0)) if tb else (lambda i, j, k: (j, 0, 0, 0)))
    else:
        b_spec = pl.BlockSpec((tn, tk), lambda i, j, k: (j, k)) if tb else pl.BlockSpec((tk, tn), lambda i, j, k: (k, j))
    assert not out_pieces or (tm == m_dim and len(out_dtypes) == 1)
    dims = (((0 if ta else 1,), (1 if tb else 0,)), ((), ()))
    n_e, n_v, n_o = len(extras), len(vecs), len(out_dtypes)
    c_ins, c_in_specs, c_outs, c_out_specs, begin, end = _hosted(comm, grid)
    n_ci, n_co = len(c_ins), len(c_outs)

    def body(a_ref, b_ref, *refs):
        e_refs, v_refs, ci_refs = refs[:n_e], refs[n_e:n_e + n_v], refs[n_e + n_v:n_e + n_v + n_ci]
        refs = refs[n_e + n_v + n_ci:]
        o_refs, co_refs = refs[:n_o], refs[n_o:n_o + n_co]
        scratch = refs[n_o + n_co:]
        sems = scratch[-2:]
        begin(ci_refs, co_refs, sems)
        if b_pieces:
            pc = b.shape[3]
            part = None
            for cc in range(n_bp):
                a_val = a_ref[:, cc * pc:(cc + 1) * pc] if tb else a_ref[...]
                term = lax.dot_general(a_val.astype(BF16), b_ref[cc].reshape(b_rows, pc).astype(BF16), dims,
                                       preferred_element_type=F32)
                part = term if part is None else part + term
        else:
            part = lax.dot_general(a_ref[...].astype(BF16), b_ref[...].astype(BF16), dims, preferred_element_type=F32)

        def finish(acc):
            outs = epilogue(acc, *[r[...] for r in e_refs], *[r[...] for r in v_refs]) if epilogue else (acc,)
            for r, v in zip(o_refs, outs):
                r[...] = (v.reshape(2, tm // 2, tn) if out_pieces else v).astype(r.dtype)

        if nk == 1:
            finish(part)
        else:
            acc_ref = scratch[0]
            k = pl.program_id(2)

            @pl.when(k == 0)
            def _():
                acc_ref[...] = part

            @pl.when(k > 0)
            def _():
                acc_ref[...] += part

            @pl.when(k == nk - 1)
            def _():
                finish(acc_ref[...])
        end(ci_refs, co_refs, sems)

    tile = pl.BlockSpec((tm, tn), lambda i, j, k: (i, j))
    if out_pieces:
        out_tile, out_shape = pl.BlockSpec((None, 2, tm // 2, tn), lambda i, j, k: (j, 0, 0, 0)), (n_dim // tn, 2, tm // 2, tn)
    else:
        out_tile, out_shape = tile, (m_dim, n_dim)
    out = pl.pallas_call(
        body, name=name, grid=grid,
        in_specs=[a_spec, b_spec] + [tile] * n_e + [pl.BlockSpec((1, tn), lambda i, j, k: (0, j))] * n_v + c_in_specs,
        out_specs=[out_tile] * n_o + c_out_specs,
        out_shape=[_sds(out_shape, d) for d in out_dtypes] + c_outs,
        scratch_shapes=([pltpu.VMEM((tm, tn), F32)] if nk > 1 else []) + (comm.sem_scratch() if comm else []),
        compiler_params=_params(("arbitrary",) * 3 if comm else ("parallel", "parallel", "arbitrary")),
    )(a, b, *extras, *vecs, *c_ins)
    return out


def _swap_halves(x):
    width = x.shape[1]
    lane = lax.broadcasted_iota(jnp.int32, x.shape, 1)
    return jnp.where((lane % 64) < 32, pltpu.roll(x, width - 32, 1), pltpu.roll(x, 32, 1))


def _rope(x, cos, sin_signed):
    return x * cos + _swap_halves(x) * sin_signed


def _rope_t(d, cos, sin_signed):
    return d * cos - _swap_halves(d) * sin_signed


def _rms(x, g):
    rstd = lax.rsqrt(jnp.mean(x * x, axis=-1, keepdims=True) + RMS_EPS)
    return x * rstd * g


def _rms_bwd(x, g, dy):
    rstd = lax.rsqrt(jnp.mean(x * x, axis=-1, keepdims=True) + RMS_EPS)
    xh = x * rstd
    dyg = dy * g
    dx = rstd * (dyg - xh * jnp.mean(dyg * xh, axis=-1, keepdims=True))
    return dx, jnp.sum(dy * xh, axis=0, keepdims=True)


def _layer_norm(r, g, b):
    mu = jnp.mean(r, axis=-1, keepdims=True)
    xc = r - mu
    var = jnp.mean(xc * xc, axis=-1, keepdims=True)
    return xc * lax.rsqrt(var + LN_EPS) * g + b


def _dot(a, b, dims):
    return lax.dot_general(a, b, (dims, ((), ())), preferred_element_type=F32)


_NN, _NT, _TN = ((1,), (0,)), ((1,), (1,)), ((0,), (0,))


def _in_proj(x_bf, w_in_p, cos, sin):
    n_rows = x_bf.shape[0]
    ts = 512
    dils = [dil for _, dil in PATTERNS]
    n_p = len(dils)

    def body(x_ref, w_ref, cos_ref, sin_ref, *refs):
        view_refs, (cq_ref, kpe_ref, ckv_ref, scratch) = refs[:n_p], refs[n_p:]
        xb = x_ref[...]
        cos_v, sin_v = cos_ref[...], sin_ref[...]

        def seg(lo, hi):
            return _dot(xb, w_ref[:, lo:hi], _NN)

        def put(part, val):
            by_dil = {1: [val]}
            for ref, dil in zip(view_refs, dils):
                if dil not in by_dil:
                    if dil > FAST_STRIDE and dil % FAST_STRIDE == 0:
                        if FAST_STRIDE not in by_dil:
                            by_dil[FAST_STRIDE] = _deinterleave(val, FAST_STRIDE, scratch)
                        inner = [_deinterleave(y, dil // FAST_STRIDE, scratch) for y in by_dil[FAST_STRIDE]]
                        by_dil[dil] = [inner[r % FAST_STRIDE][r // FAST_STRIDE] for r in range(dil)]
                    else:
                        by_dil[dil] = _deinterleave(val, dil, scratch)
                for r, piece in enumerate(by_dil[dil]):
                    lo = (3 * r + part) * A_WIDTH
                    ref[:, lo:lo + A_WIDTH] = piece.astype(BF16)

        put(0, _rope(seg(0, 512), cos_v, sin_v) * A_SCALE)
        put(1, _rope(seg(512, 1024), cos_v, sin_v))
        put(2, seg(1024, 1536))
        cq_ref[...] = seg(1536, 1792)
        kpe_ref[...] = _rope(seg(1792, 1920), cos_v[:, :LANES], sin_v[:, :LANES]).astype(BF16)
        ckv_ref[...] = seg(1920, 2048)

    row = lambda c, d=1: pl.BlockSpec((ts // d, c * d), lambda i: (i, 0))
    return pl.pallas_call(
        body, name="in_proj", grid=(n_rows // ts,),
        in_specs=[row(D_MODEL), pl.BlockSpec((D_MODEL, IN_PAD), lambda i: (0, 0)), row(A_WIDTH), row(A_WIDTH)],
        out_specs=[row(3 * A_WIDTH, d) for d in dils] + [row(Q_LORA), row(LANES), row(KV_LORA)],
        out_shape=[_sds((n_rows // d, d * 3 * A_WIDTH), BF16) for d in dils]
        + [_sds((n_rows, Q_LORA)), _sds((n_rows, LANES), BF16), _sds((n_rows, KV_LORA))],
        scratch_shapes=[pltpu.VMEM((A_WIDTH // LANES, ts, LANES), F32)],
        compiler_params=_params(("parallel",)),
    )(x_bf, w_in_p, cos, sin)


def _band_mask(m, heads):
    qi = lax.broadcasted_iota(jnp.int32, (heads * SPAN, 2 * SPAN), 0) % SPAN
    kj = lax.broadcasted_iota(jnp.int32, (heads * SPAN, 2 * SPAN), 1)
    return ((kj < SPAN) & (kj >= qi) & (m > 0)) | ((kj >= SPAN) & ((kj - SPAN) <= qi))


def _even_lanes(rows):
    return lax.broadcasted_iota(jnp.int32, (rows, LANES), 1) < A_HEAD_DIM


DIL_GROUP = 4
HEAD_SLOT = LANES // A_HEADS
O_WORDS = A_WIDTH // 2
OL_WIDTH = O_WORDS + LANES


def _pack_halves(t):
    return _pack2(t[:, :O_WORDS], t[:, O_WORDS:])


def _unpack_halves(p):
    return jnp.concatenate(_unpack2(p), axis=1)


DQKV_WIDTH = 3 * A_WIDTH // 2


def _pack2(a, b):
    hi = lax.bitcast_convert_type(a.astype(jnp.bfloat16).astype(F32), jnp.uint32) & jnp.uint32(0xFFFF0000)
    lo = lax.shift_right_logical(lax.bitcast_convert_type(b.astype(jnp.bfloat16).astype(F32), jnp.uint32), jnp.uint32(16))
    return lax.bitcast_convert_type(hi | lo, F32)


def _unpack2(p):
    u = lax.bitcast_convert_type(p, jnp.uint32)
    return (lax.bitcast_convert_type(u & jnp.uint32(0xFFFF0000), F32),
            lax.bitcast_convert_type(lax.shift_left(u, jnp.uint32(16)), F32))


def _slot_lane(h):
    return HEAD_SLOT * (h // 2) + A_HEAD_DIM * (h % 2)


def _slot_head(lane):
    return 2 * (jnp.right_shift(lane, 4) & 3) + jnp.right_shift(lane, 6)


def _dil_fwd(qkv_view, dil, comm=None):
    nb = qkv_view.shape[0] // SPAN
    group = min(dil, DIL_GROUP)
    grid = (dil // group, nb)
    c_ins, c_in_specs, c_outs, c_out_specs, begin, end = _hosted(comm, grid)
    n_ci, n_co = len(c_ins), len(c_outs)

    def body(cur_ref, prev_ref, *refs):
        ci_refs, ol_ref, co_refs, sems = refs[:n_ci], refs[n_ci], refs[n_ci + 1:n_ci + 1 + n_co], refs[n_ci + 1 + n_co:]
        begin(ci_refs, co_refs, sems)
        mask = _band_mask(pl.program_id(1), 1)
        even, even2 = _even_lanes(SPAN), _even_lanes(2 * SPAN)
        pair_of_lane = jnp.right_shift(lax.broadcasted_iota(jnp.int32, (SPAN, LANES), 1), 4) & 3
        for g in range(group):
            lse_c, o_pairs = jnp.zeros((SPAN, LANES), F32), []
            for p in range(A_HEADS // 2):
                q_sl, k_sl, v_sl = (slice((3 * g + t) * A_WIDTH + p * LANES, (3 * g + t) * A_WIDTH + (p + 1) * LANES)
                                    for t in range(3))
                q2 = cur_ref[:, q_sl]
                kcat = jnp.concatenate([prev_ref[:, k_sl], cur_ref[:, k_sl]], axis=0)
                vcat = jnp.concatenate([prev_ref[:, v_sl], cur_ref[:, v_sl]], axis=0)
                zero, one = jnp.zeros_like(q2), jnp.ones_like(vcat)
                res, lses = [], []
                for first in (True, False):
                    qh = jnp.where(even, q2, zero) if first else jnp.where(even, zero, q2)
                    vh = jnp.where(even2, vcat, one) if first else jnp.where(even2, one, vcat)
                    s = jnp.where(mask, _dot(qh, kcat, _NT), NEG)
                    mx = jnp.max(s, axis=-1, keepdims=True)
                    r = _dot(jnp.exp(s - mx).astype(BF16), vh, _NN)
                    den = pltpu.roll(r, A_HEAD_DIM, 1)
                    res.append(r / den)
                    lses.append(mx + jnp.log(den))
                o_pairs.append(jnp.where(even, res[0], res[1]))
                lse_c = jnp.where(pair_of_lane == p, jnp.where(even, lses[0], lses[1]), lse_c)
            for p in range(2):
                ol_ref[:, OL_WIDTH * g + p * LANES:OL_WIDTH * g + (p + 1) * LANES] = _pack2(o_pairs[p], o_pairs[p + 2])
            ol_ref[:, OL_WIDTH * g + O_WORDS:OL_WIDTH * (g + 1)] = lse_c
        end(ci_refs, co_refs, sems)

    return pl.pallas_call(
        body, name=f"dil_fwd_d{dil}", grid=grid,
        in_specs=[pl.BlockSpec((SPAN, group * 3 * A_WIDTH), lambda r, m: (m, r)),
                  pl.BlockSpec((SPAN, group * 3 * A_WIDTH), lambda r, m: (jnp.maximum(m - 1, 0), r))] + c_in_specs,
        out_specs=[pl.BlockSpec((SPAN, group * OL_WIDTH), lambda r, m: (m, r))] + c_out_specs,
        out_shape=[_sds((qkv_view.shape[0], dil * OL_WIDTH))] + c_outs,
        scratch_shapes=comm.sem_scratch() if comm else [],
        compiler_params=_params(("arbitrary", "arbitrary") if comm else ("parallel", "arbitrary")),
    )(qkv_view, qkv_view, *c_ins)


def _slot_spread_matrix():
    i = lax.broadcasted_iota(jnp.int32, (LANES, A_WIDTH), 0)
    h = lax.broadcasted_iota(jnp.int32, (LANES, A_WIDTH), 1) // A_HEAD_DIM
    return (i == HEAD_SLOT * (h // 2) + A_HEAD_DIM * (h % 2)).astype(BF16)


def _spread_slots(c, spread):
    hi = c.astype(BF16)
    r1 = c - hi.astype(F32)
    mid = r1.astype(BF16)
    lo = (r1 - mid.astype(F32)).astype(BF16)
    return _dot(hi, spread, _NN) + _dot(mid, spread, _NN) + _dot(lo, spread, _NN)


def _pattern_weights(ol1, ol2, ol3):
    l1, l2, l3 = ol1[:, O_WORDS:], ol2[:, O_WORDS:], ol3[:, O_WORDS:]
    mx = jnp.maximum(jnp.maximum(l1, l2), l3)
    e1, e2, e3 = jnp.exp(l1 - mx), jnp.exp(l2 - mx), jnp.exp(l3 - mx)
    inv = 1.0 / (e1 + e2 + e3)
    return e1 * inv, e2 * inv, e3 * inv


def _mix_fwd(ols, b_out, a_g, b_g):
    def fn(ol1, ol2, ol3, b, ag, bg, spread):
        ws = [_spread_slots(w, spread) for w in _pattern_weights(ol1, ol2, ol3)]
        a = sum(w * _unpack_halves(ol[:, :O_WORDS]) for w, ol in zip(ws, (ol1, ol2, ol3)))
        return a, jnp.concatenate([_rms(a, ag), _rms(b, bg)], axis=1)

    n_rows = b_out.shape[0]
    return _rowwise("mix_fwd", fn, [*ols, b_out], [_sds((n_rows, A_WIDTH)), _sds((n_rows, 2 * A_WIDTH), BF16)],
                    consts=[a_g, b_g, _slot_spread_matrix()], in_dils=[dil for _, dil in PATTERNS] + [1])


def _mla_prep(cq, ckv, kpe, cos, sin, gq, w_uq_p, gkv, w_ukv_p):
    def fn(cq_v, ckv_v, kpe_v, cos_v, sin_v, gq_v, wq_v, gkv_v, wkv_v):
        cqn = _rms(cq_v, gq_v).astype(BF16)
        q = _dot(cqn, wq_v, _NN)
        qf = [jnp.concatenate([q[:, h * QK_NOPE:(h + 1) * QK_NOPE],
                               _rope(q[:, 512 + h * LANES:512 + (h + 1) * LANES], cos_v[:, :LANES], sin_v[:, :LANES])], axis=1)
              for h in range(MLA_HEADS)]
        ckvn = _rms(ckv_v, gkv_v).astype(BF16)
        kvv = _dot(ckvn, wkv_v, _NN)
        kf = [jnp.concatenate([kvv[:, h * QK_NOPE:(h + 1) * QK_NOPE], kpe_v.astype(F32)], axis=1) for h in range(MLA_HEADS)]
        return cqn, qf, ckvn, kf, kvv[:, 512:]

    n_rows = cq.shape[0]
    heads = _sds((MLA_HEADS, n_rows, QK_PAD), BF16)
    return _rowwise("mla_prep", fn, [cq, ckv, kpe, cos, sin],
                    [_sds((n_rows, Q_LORA), BF16), heads, _sds((n_rows, KV_LORA), BF16), heads, _sds((n_rows, MLA_WIDTH), BF16)],
                    consts=[gq, w_uq_p, gkv, w_ukv_p])


MLA_FWD_TILES = (1024, 2048)
MLA_BWD_TILES = (1024, 1024)


def _mla_fwd(qf, kf, v, comm=None):
    n_rows = v.shape[0]
    tq, tk = min(MLA_FWD_TILES[0], n_rows), min(MLA_FWD_TILES[1], n_rows)
    nq, nk = n_rows // tq, n_rows // tk
    grid = (MLA_HEADS, nq, nk)
    exp2_scale = MLA_SCALE * math.log2(math.e)
    c_ins, c_in_specs, c_outs, c_out_specs, begin, end = _hosted(comm, grid)
    n_ci, n_co = len(c_ins), len(c_outs)

    def last_k(i):
        return jnp.right_shift(i * tq + tq - 1, int(math.log2(tk)))

    def body(q_ref, k_ref, v_ref, *refs):
        ci_refs, (o_ref, lse_ref), co_refs = refs[:n_ci], refs[n_ci:n_ci + 2], refs[n_ci + 2:n_ci + 2 + n_co]
        m_sc, l_sc, acc_sc = refs[n_ci + 2 + n_co:n_ci + 5 + n_co]
        sems = refs[n_ci + 5 + n_co:]
        i, j = pl.program_id(1), pl.program_id(2)
        begin(ci_refs, co_refs, sems)

        @pl.when(j == 0)
        def _():
            m_sc[...] = jnp.full(m_sc.shape, NEG, F32)
            l_sc[...] = jnp.zeros(l_sc.shape, F32)
            acc_sc[...] = jnp.zeros(acc_sc.shape, F32)

        def step(masked):
            s = _dot(q_ref[...], k_ref[...], _NT)
            if masked:
                row = lax.broadcasted_iota(jnp.int32, (tq, tk), 0) + i * tq
                col = lax.broadcasted_iota(jnp.int32, (tq, tk), 1) + j * tk
                s = jnp.where(col <= row, s, NEG)
            m_prev = m_sc[...]
            m_new = jnp.maximum(m_prev, jnp.max(s, axis=-1, keepdims=True))
            alpha = jnp.exp2((m_prev - m_new) * exp2_scale)
            p = jnp.exp2((s - m_new) * exp2_scale)
            l_sc[...] = alpha * l_sc[...] + jnp.sum(p, axis=-1, keepdims=True)
            acc_sc[...] = alpha * acc_sc[...] + _dot(p.astype(BF16), v_ref[...], _NN)
            m_sc[...] = m_new

        active = j * tk <= i * tq + tq - 1
        crosses = (j + 1) * tk - 1 > i * tq

        @pl.when(active & jnp.logical_not(crosses))
        def _():
            step(False)

        @pl.when(active & crosses)
        def _():
            step(True)

        @pl.when(j == last_k(i))
        def _():
            o_ref[...] = acc_sc[...] / l_sc[...]
            lse_ref[...] = jnp.broadcast_to(m_sc[...] * MLA_SCALE + jnp.log(l_sc[...]), (tq, V_DIM))

        end(ci_refs, co_refs, sems)

    return pl.pallas_call(
        body, name="mla_fwd", grid=grid,
        in_specs=[pl.BlockSpec((None, tq, QK_PAD), lambda h, i, j: (h, i, 0)),
                  pl.BlockSpec((None, tk, QK_PAD), lambda h, i, j: (h, jnp.minimum(j, last_k(i)), 0)),
                  pl.BlockSpec((tk, V_DIM), lambda h, i, j: (jnp.minimum(j, last_k(i)), h))] + c_in_specs,
        out_specs=[pl.BlockSpec((tq, V_DIM), lambda h, i, j: (i, h)), pl.BlockSpec((tq, V_DIM), lambda h, i, j: (i, h))]
        + c_out_specs,
        out_shape=[_sds((n_rows, MLA_WIDTH)), _sds((n_rows, MLA_WIDTH))] + c_outs,
        scratch_shapes=[pltpu.VMEM((tq, 1), F32), pltpu.VMEM((tq, 1), F32), pltpu.VMEM((tq, V_DIM), F32)]
        + (comm.sem_scratch() if comm else []),
        compiler_params=_params(("arbitrary",) * 3 if comm else ("parallel", "parallel", "arbitrary")),
    )(qf, kf, v, *c_ins)


def _mm_res_ln(name, a, w, xres, g, b, tm):
    def epi(acc, xr, g_v, b_v):
        r = ALPHA * xr + acc
        y = _layer_norm(r, g_v, b_v)
        return y, y, r

    return _mm(name, a, w, (F32, BF16, F32), tm=tm, tn=D_MODEL, tk=a.shape[1], epilogue=epi, extras=[xres], vecs=[g, b])


def _mm_relu2(x_bf, w, comm=None):
    def epi(acc):
        r = jnp.maximum(acc, 0.0)
        return (r * r,)

    return _mm("ff1", x_bf, w, (BF16,), tm=2048, tn=1024, tk=D_MODEL, epilogue=epi, b_pieces=True, comm=comm)


def _loss_fn(y, t):
    def fn(y_v, t_v):
        d = y_v - t_v
        part = jnp.sum(jnp.sum(d * d, axis=1, keepdims=True), axis=0, keepdims=True)
        return d * (1.0 / D_MODEL), part

    dy, part = _rowwise("loss", fn, [y, t], [_sds(y.shape)], reds=[_sds((1, 1))])
    return part * (0.5 / D_MODEL), dy


def _ln_bwd_rows(dy, r, g):
    mu = jnp.mean(r, axis=-1, keepdims=True)
    xc = r - mu
    rstd = lax.rsqrt(jnp.mean(xc * xc, axis=-1, keepdims=True) + LN_EPS)
    xh = xc * rstd
    dxh = dy * g
    dr = rstd * (dxh - jnp.mean(dxh, axis=-1, keepdims=True) - xh * jnp.mean(dxh * xh, axis=-1, keepdims=True))
    return dr, dr, jnp.sum(dy * xh, axis=0, keepdims=True), jnp.sum(dy, axis=0, keepdims=True)


def _ln_bwd(name, dy, r, g):
    return _rowwise(name, _ln_bwd_rows, [dy, r], [_sds(dy.shape), _sds(dy.shape, BF16)], consts=[g],
                    reds=[_sds((1, D_MODEL)), _sds((1, D_MODEL))])


def _head_sum_matrix():
    i = lax.broadcasted_iota(jnp.int32, (A_WIDTH, LANES), 0) // A_HEAD_DIM
    return (i == _slot_head(lax.broadcasted_iota(jnp.int32, (A_WIDTH, LANES), 1))).astype(BF16)


def _mix_bwd(dmixed, a_out, b_out, ols, a_g, b_g):
    def fn(dm, a, b, ol1, ol2, ol3, ag, bg, sum_mat, spread):
        da, dga = _rms_bwd(a, ag, dm[:, :A_WIDTH])
        db, dgb = _rms_bwd(b, bg, dm[:, A_WIDTH:])
        t = da * a
        t_hi = t.astype(BF16)
        t_lo = (t - t_hi.astype(F32)).astype(BF16)
        tsum = _dot(t_hi, sum_mat, _NN) + _dot(t_lo, sum_mat, _NN)
        tb = db * b
        delta_b = jnp.concatenate(
            [jnp.broadcast_to(jnp.sum(tb[:, h * V_DIM:(h + 1) * V_DIM], axis=-1, keepdims=True), (tb.shape[0], V_DIM))
             for h in range(MLA_HEADS)], axis=1)
        dods = [jnp.concatenate([_pack_halves(_spread_slots(w, spread) * da), w * tsum], axis=1)
                for w in _pattern_weights(ol1, ol2, ol3)]
        return (*dods, db, delta_b, dga, dgb)

    n_rows = a_out.shape[0]
    wide = (n_rows, A_WIDTH)
    dils = [dil for _, dil in PATTERNS]
    return _rowwise("mix_bwd", fn, [dmixed, a_out, b_out, *ols],
                    [_sds((n_rows // d, d * OL_WIDTH)) for d in dils] + [_sds(wide, BF16), _sds(wide)],
                    consts=[a_g, b_g, _head_sum_matrix(), _slot_spread_matrix()],
                    reds=[_sds((1, A_WIDTH)), _sds((1, MLA_WIDTH))], ts=256,
                    in_dils=[1, 1, 1] + dils, out_dils=dils + [1, 1])


def _dil_bwd(qkv_view, ol_view, dod_view, dil, comm=None):
    nb = qkv_view.shape[0] // SPAN
    group = min(dil, DIL_GROUP)
    grid = (dil // group, nb + 1)
    c_ins, c_in_specs, c_outs, c_out_specs, begin, end = _hosted(comm, grid)
    n_ci, n_co = len(c_ins), len(c_outs)

    def body(cur_ref, prev_ref, ol_ref, dod_ref, *refs):
        ci_refs, out_ref, co_refs = refs[:n_ci], refs[n_ci], refs[n_ci + 1:n_ci + 1 + n_co]
        carry, sems = refs[n_ci + 1 + n_co], refs[n_ci + 2 + n_co:]
        m = pl.program_id(1)
        begin(ci_refs, co_refs, sems)

        def write_packed(g, pairs):
            lo = DQKV_WIDTH * g
            for p, (dq_p, dk_p, _) in enumerate(pairs):
                out_ref[:, lo + p * LANES:lo + (p + 1) * LANES] = _pack2(dq_p, dk_p)
            for p in range(2):
                out_ref[:, lo + A_WIDTH + p * LANES:lo + A_WIDTH + (p + 1) * LANES] = _pack2(pairs[p][2], pairs[p + 2][2])

        @pl.when(m == 0)
        def _():
            carry[...] = jnp.zeros(carry.shape, F32)

        @pl.when(m < nb)
        def _():
            mask, even = _band_mask(m, 2), _even_lanes(SPAN)
            for g in range(group):
                done = []
                for p in range(A_HEADS // 2):
                    q_sl, k_sl, v_sl = (slice((3 * g + t) * A_WIDTH + p * LANES, (3 * g + t) * A_WIDTH + (p + 1) * LANES)
                                        for t in range(3))
                    do_sl = slice(OL_WIDTH * g + (p % 2) * LANES, OL_WIDTH * g + (p % 2 + 1) * LANES)
                    s_e, s_o = (OL_WIDTH * g + O_WORDS + _slot_lane(2 * p + t) for t in range(2))
                    q2, do2 = cur_ref[:, q_sl], _unpack2(dod_ref[:, do_sl])[p // 2].astype(BF16)
                    zero = jnp.zeros_like(q2)
                    qcat = jnp.concatenate([jnp.where(even, q2, zero), jnp.where(even, zero, q2)], axis=0)
                    docat = jnp.concatenate([jnp.where(even, do2, zero), jnp.where(even, zero, do2)], axis=0)
                    kcat = jnp.concatenate([prev_ref[:, k_sl], cur_ref[:, k_sl]], axis=0)
                    vcat = jnp.concatenate([prev_ref[:, v_sl], cur_ref[:, v_sl]], axis=0)
                    lse_c = jnp.concatenate([ol_ref[:, s_e:s_e + 1], ol_ref[:, s_o:s_o + 1]], axis=0)
                    dl_c = jnp.concatenate([dod_ref[:, s_e:s_e + 1], dod_ref[:, s_o:s_o + 1]], axis=0)
                    pr = jnp.exp(jnp.where(mask, _dot(qcat, kcat, _NT), NEG) - lse_c)
                    ds = (pr * (_dot(docat, vcat, _NT) - dl_c)).astype(BF16)
                    dq = _dot(ds, kcat, _NN)
                    dk2 = _dot(ds, qcat, _TN)
                    dv2 = _dot(pr.astype(BF16), docat, _TN)
                    done.append((carry[:, q_sl], carry[:, k_sl] + dk2[:SPAN], carry[:, v_sl] + dv2[:SPAN]))
                    carry[:, q_sl] = jnp.where(even, dq[:SPAN], dq[SPAN:])
                    carry[:, k_sl] = dk2[SPAN:]
                    carry[:, v_sl] = dv2[SPAN:]
                write_packed(g, done)

        @pl.when(m == nb)
        def _():
            for g in range(group):
                write_packed(g, [tuple(carry[:, (3 * g + t) * A_WIDTH + p * LANES:(3 * g + t) * A_WIDTH + (p + 1) * LANES]
                                       for t in range(3)) for p in range(A_HEADS // 2)])

        end(ci_refs, co_refs, sems)

    last = nb - 1

    def cur(width):
        return pl.BlockSpec((SPAN, group * width), lambda r, m: (jnp.minimum(m, last), r))

    def prev(width):
        return pl.BlockSpec((SPAN, group * width), lambda r, m: (jnp.clip(m - 1, 0, last), r))

    return pl.pallas_call(
        body, name=f"dil_bwd_d{dil}", grid=grid,
        in_specs=[cur(3 * A_WIDTH), prev(3 * A_WIDTH), cur(OL_WIDTH), cur(OL_WIDTH)] + c_in_specs,
        out_specs=[prev(DQKV_WIDTH)] + c_out_specs,
        out_shape=[_sds((qkv_view.shape[0], dil * DQKV_WIDTH))] + c_outs,
        scratch_shapes=[pltpu.VMEM((SPAN, group * 3 * A_WIDTH), F32)] + (comm.sem_scratch() if comm else []),
        compiler_params=_params(("arbitrary", "arbitrary") if comm else ("parallel", "arbitrary")),
    )(qkv_view, qkv_view, ol_view, dod_view, *c_ins)


def _mla_bwd(qf, kf, v, lse, do, delta, comm=None):
    n_rows = v.shape[0]
    tq, tk = min(MLA_BWD_TILES[0], n_rows), min(MLA_BWD_TILES[1], n_rows)
    nq, nk = n_rows // tq, n_rows // tk

    def first_q(j):
        return jnp.right_shift(j * tk, int(math.log2(tq)))

    grid = (MLA_HEADS, nk, nq)
    c_ins, c_in_specs, c_outs, c_out_specs, begin, end = _hosted(comm, grid)
    n_ci, n_co = len(c_ins), len(c_outs)

    def body(q_ref, k_ref, v_ref, lse_ref, do_ref, dl_ref, *refs):
        ci_refs, (dq_ref, dk_ref, dv_ref), co_refs = refs[:n_ci], refs[n_ci:n_ci + 3], refs[n_ci + 3:n_ci + 3 + n_co]
        dk_sc, dv_sc = refs[n_ci + 3 + n_co:n_ci + 5 + n_co]
        sems = refs[n_ci + 5 + n_co:]
        j, i = pl.program_id(1), pl.program_id(2)
        begin(ci_refs, co_refs, sems)

        @pl.when((j == 0) & (i == 0))
        def _():
            dq_ref[...] = jnp.zeros(dq_ref.shape, F32)

        @pl.when(i == first_q(j))
        def _():
            dk_sc[...] = jnp.zeros(dk_sc.shape, F32)
            dv_sc[...] = jnp.zeros(dv_sc.shape, F32)

        def step(masked):
            q, k, dob = q_ref[...], k_ref[...], do_ref[...]
            s = _dot(q, k, _NT) * MLA_SCALE
            if masked:
                row = lax.broadcasted_iota(jnp.int32, (tq, tk), 0) + i * tq
                col = lax.broadcasted_iota(jnp.int32, (tq, tk), 1) + j * tk
                s = jnp.where(col <= row, s, NEG)
            p = jnp.exp(s - lse_ref[:, :1])
            dp = _dot(dob, v_ref[...], _NT)
            ds = (p * (dp - dl_ref[:, :1]) * MLA_SCALE).astype(BF16)
            dv_sc[...] += _dot(p.astype(BF16), dob, _TN)
            dk_sc[...] += _dot(ds, q, _TN)
            rows = pl.ds(pl.multiple_of(i * tq, tq), tq)
            dq_ref[rows, :] += _dot(ds, k, _NN)

        active = i >= first_q(j)
        crosses = (j + 1) * tk - 1 > i * tq

        @pl.when(active & jnp.logical_not(crosses))
        def _():
            step(False)

        @pl.when(active & crosses)
        def _():
            step(True)

        @pl.when(i == nq - 1)
        def _():
            dk_ref[...] = dk_sc[...]
            dv_ref[...] = dv_sc[...]

        end(ci_refs, co_refs, sems)

    qrow = lambda h, j, i: (jnp.maximum(i, first_q(j)), h)
    return pl.pallas_call(
        body, name="mla_bwd", grid=grid,
        in_specs=[pl.BlockSpec((None, tq, QK_PAD), lambda h, j, i: (h, jnp.maximum(i, first_q(j)), 0)),
                  pl.BlockSpec((None, tk, QK_PAD), lambda h, j, i: (h, j, 0)),
                  pl.BlockSpec((tk, V_DIM), lambda h, j, i: (j, h)),
                  pl.BlockSpec((tq, V_DIM), qrow), pl.BlockSpec((tq, V_DIM), qrow), pl.BlockSpec((tq, V_DIM), qrow)]
        + c_in_specs,
        out_specs=[pl.BlockSpec((None, n_rows, QK_PAD), lambda h, j, i: (h, 0, 0)),
                   pl.BlockSpec((None, tk, QK_PAD), lambda h, j, i: (h, j, 0)),
                   pl.BlockSpec((tk, V_DIM), lambda h, j, i: (j, h))] + c_out_specs,
        out_shape=[_sds((MLA_HEADS, n_rows, QK_PAD)), _sds((MLA_HEADS, n_rows, QK_PAD)), _sds((n_rows, MLA_WIDTH))] + c_outs,
        scratch_shapes=[pltpu.VMEM((tk, QK_PAD), F32), pltpu.VMEM((tk, V_DIM), F32)] + (comm.sem_scratch() if comm else []),
        compiler_params=_params(("arbitrary",) * 3 if comm else ("parallel", "arbitrary", "arbitrary")),
    )(qf, kf, v, lse, do, delta, *c_ins)


def _mla_bwd_prep(dqf, dkf, dv, cos, sin):
    def fn(dqf_v, dkf_v, dv_v, cos_v, sin_v):
        dq = jnp.concatenate([dqf_v[h][:, :QK_NOPE] for h in range(MLA_HEADS)]
                             + [_rope_t(dqf_v[h][:, QK_NOPE:], cos_v[:, :LANES], sin_v[:, :LANES]) for h in range(MLA_HEADS)], axis=1)
        dkv = jnp.concatenate([dkf_v[h][:, :QK_NOPE] for h in range(MLA_HEADS)] + [dv_v], axis=1)
        dkpe = dkf_v[0][:, QK_NOPE:] + dkf_v[1][:, QK_NOPE:] + dkf_v[2][:, QK_NOPE:] + dkf_v[3][:, QK_NOPE:]
        return dq, dkv, dkpe

    n_rows = dv.shape[0]
    return _rowwise("mla_bwd_prep", fn, [dqf, dkf, dv, cos, sin],
                    [_sds((n_rows, UQ_PAD), BF16), _sds((n_rows, 1024), BF16), _sds((n_rows, LANES))], ts=256)


def _assemble_dh(dqkvs, dcqn, cq, dckvn, ckv, dkpe, cos, sin, gq, gkv, comm=None):
    def fn(g1, g2, g3, dcqn_v, cq_v, dckvn_v, ckv_v, dkpe_v, cos_v, sin_v, gq_v, gkv_v):
        dq = dk = dv = 0.0
        for packed in (g1, g2, g3):
            dq_p, dk_p = _unpack2(packed[:, :A_WIDTH])
            dv_p = jnp.concatenate(_unpack2(packed[:, A_WIDTH:]), axis=1)
            dq, dk, dv = dq + dq_p, dk + dk_p, dv + dv_p
        dqa = _rope_t(dq, cos_v, sin_v) * A_SCALE
        dka = _rope_t(dk, cos_v, sin_v)
        dcq, dgq = _rms_bwd(cq_v, gq_v, dcqn_v)
        dckv, dgkv = _rms_bwd(ckv_v, gkv_v, dckvn_v)
        dkr = _rope_t(dkpe_v, cos_v[:, :LANES], sin_v[:, :LANES])
        return jnp.concatenate([dqa, dka, dv, dcq, dkr, dckv], axis=1), dgq, dgkv

    n_rows = cq.shape[0]
    return _rowwise("assemble_dh", fn, [*dqkvs, dcqn, cq, dckvn, ckv, dkpe, cos, sin],
                    [_sds((n_rows, IN_PAD), BF16)], consts=[gq, gkv],
                    reds=[_sds((1, Q_LORA)), _sds((1, KV_LORA))], ts=256,
                    in_dils=[dil for _, dil in PATTERNS] + [1] * 7, comm=comm)


def _layer_fwd(x_f32, x_bf, w, sm, cos, sin, behind=(None, None), rest=None):
    *qkv_views, cq, kpe, ckv = _in_proj(x_bf, w["w_in"], cos, sin)
    ol_views, brought = [], []
    for i, (view, (_, dil)) in enumerate(zip(qkv_views, PATTERNS)):
        ol, *got = _dil_fwd(view, dil, comm=rest[0][i] if rest else None)
        ol_views.append(ol)
        brought.append(got)
    if rest:
        w = {**w, **rest[1](brought)}
    cqn, qf, ckvn, kf, v = _mla_prep(cq, ckv, kpe, cos, sin, sm["q_a_norm"], w["w_uq"], sm["kv_a_norm"], w["w_ukv"])
    b_out, b_lse, *behind_attn = _mla_fwd(qf, kf, v, comm=behind[0])
    a_out, mixed = _mix_fwd(ol_views, b_out, sm["a_out_norm"], sm["b_out_norm"])
    x1, x1_bf, r1 = _mm_res_ln("wo_ln1", mixed, w["w_o"], x_f32, sm["ln1_g"], sm["ln1_b"], tm=1024)
    f, *behind_ff1 = _mm_relu2(x1_bf, w["w_ff1"], comm=behind[1])
    x2, x2_bf, r2 = _mm_res_ln("ff2_ln2", f, w["w_ff2"], x1, sm["ln2_g"], sm["ln2_b"], tm=512)
    saved = dict(w=w, x_bf=x_bf, qkv_views=qkv_views, ol_views=ol_views, cq=cq, ckv=ckv, cqn=cqn, qf=qf, ckvn=ckvn,
                 kf=kf, v=v, b_out=b_out, b_lse=b_lse, a_out=a_out, mixed=mixed, x1_bf=x1_bf, r1=r1, f=f, r2=r2)
    return x2, x2_bf, saved, (behind_attn, behind_ff1)


def _layer_bwd(dx2, w, sm, sv, cos, sin, reduce_of, pending=None):
    dr2, dr2_bf, dg2, db2 = _ln_bwd("ln2_bwd", dx2, sv["r2"], sm["ln2_g"])
    (dw_ff2,) = _mm("dw_ff2", sv["f"], dr2_bf, (F32,), ta=True, tm=1024, tn=1024, tk=dx2.shape[0])
    du, *swapped = _mm("d_u", dr2_bf, w["w_ff2"], (BF16,), tb=True, tm=1024, tn=2048, tk=D_MODEL,
                       epilogue=lambda acc, f: (acc * (2.0 * jnp.sqrt(f.astype(F32))),), extras=[sv["f"]],
                       comm=_swap_comm(pending.pieces) if pending else None)
    if pending:
        pending.after_swap(swapped)
    (dw_ff1,) = _mm("dw_ff1", sv["x1_bf"], du, (F32,), ta=True, tm=1024, tn=1024, tk=dx2.shape[0], out_pieces=True)
    (dy1,) = _mm("d_x1", du, w["w_ff1"], (F32,), tb=True, tm=512, tn=1024, tk=D_FF,
                 epilogue=lambda acc, d: (acc + ALPHA * d,), extras=[dr2], b_pieces=True)
    dr1, dr1_bf, dg1, db1 = _ln_bwd("ln1_bwd", dy1, sv["r1"], sm["ln1_g"])
    (dw_o,) = _mm("dw_o", sv["mixed"], dr1_bf, (F32,), ta=True, tm=1024, tn=1024, tk=2048)
    early = reduce_of(dict(w_o=dw_o, w_ff1=dw_ff1, w_ff2=dw_ff2))
    (dmixed,) = _mm("d_mixed", dr1_bf, w["w_o"], (F32,), tb=True, tm=1024, tn=1024, tk=D_MODEL)
    dod1, dod2, dod3, do_b, delta_b, dga, dgb = _mix_bwd(
        dmixed, sv["a_out"], sv["b_out"], sv["ol_views"], sm["a_out_norm"], sm["b_out_norm"])
    dqkvs = []
    for i, ((_, dil), qkv_view, ol_view, dod_view) in enumerate(zip(PATTERNS, sv["qkv_views"], sv["ol_views"], (dod1, dod2, dod3))):
        swap_here = early is not None and i == 0
        dqkv, *swapped = _dil_bwd(qkv_view, ol_view, dod_view, dil, comm=_swap_comm(early.pieces) if swap_here else None)
        dqkvs.append(dqkv)
        if swap_here:
            early.after_swap(swapped)
    to_scatter = (early.s16 if early else []) + (pending.s16 if pending else [])
    dqf, dkf, dv_b, *scattered = _mla_bwd(sv["qf"], sv["kf"], sv["v"], sv["b_lse"], do_b, delta_b,
                                          comm=_scatter_comm(to_scatter) if to_scatter else None)
    if early:
        early.after_scatter(scattered[:len(early.s16)])
    if pending:
        pending.after_scatter(scattered[len(scattered) - len(pending.s16):])
    dq_b, dkv_b, dkpe = _mla_bwd_prep(dqf, dkf, dv_b, cos, sin)
    (dw_uq,) = _mm("dw_uq", sv["cqn"], dq_b, (F32,), ta=True, tm=Q_LORA, tn=1024, tk=2048)
    (dcqn,) = _mm("d_cqn", dq_b, w["w_uq"], (F32,), tb=True, tm=1024, tn=Q_LORA, tk=UQ_PAD)
    (dw_ukv,) = _mm("dw_ukv", sv["ckvn"], dkv_b, (F32,), ta=True, tm=KV_LORA, tn=1024, tk=2048)
    (dckvn,) = _mm("d_ckvn", dkv_b, w["w_ukv"], (F32,), tb=True, tm=1024, tn=KV_LORA, tk=1024)
    to_share = (early.mine if early else []) + (pending.mine if pending else [])
    dh, dgq, dgkv, *shared = _assemble_dh(dqkvs, dcqn, sv["cq"], dckvn, sv["ckv"], dkpe, cos, sin,
                                          sm["q_a_norm"], sm["kv_a_norm"], comm=_share_comm(to_share) if to_share else None)
    if early:
        early.after_share(shared[:len(early.mine)])
    if pending:
        pending.after_share(shared[len(shared) - len(pending.mine):])
    (dw_in,) = _mm("dw_in", sv["x_bf"], dh, (F32,), ta=True, tm=1024, tn=1024, tk=dx2.shape[0])
    (dx,) = _mm("d_x", dh, w["w_in"], (F32,), tb=True, tm=1024, tn=1024, tk=2048,
                epilogue=lambda acc, d: (acc + ALPHA * d,), extras=[dr1])
    late = reduce_of(dict(w_in=dw_in, w_uq=dw_uq, w_ukv=dw_ukv))
    dsm = dict(q_a_norm=dgq, kv_a_norm=dgkv, a_out_norm=dga, b_out_norm=dgb, ln1_g=dg1, ln1_b=db1, ln2_g=dg2, ln2_b=db2)
    return dx, dsm, late


def _pad_w_in(w):
    return jnp.concatenate([w[:, :1792], w[:, 1920:1984], jnp.zeros((w.shape[0], 64), w.dtype), w[:, 1792:1920]], axis=1)


def _unpad_w_in(w):
    return jnp.concatenate([w[:, :1792], w[:, 1920:2048], w[:, 1792:1856]], axis=1)


def _rope_tables(n_rows):
    half = A_HEAD_DIM // 2
    inv_freq = ROPE_THETA ** (-jnp.arange(half, dtype=F32) / half)
    ang = jnp.arange(n_rows, dtype=F32)[:, None] * inv_freq[None, :]
    cos = jnp.tile(jnp.cos(ang), (1, 2 * A_HEADS))
    sin = jnp.tile(jnp.concatenate([-jnp.sin(ang), jnp.sin(ang)], axis=1), (1, A_HEADS))
    return cos, sin


W_NAMES = ("w_in", "w_uq", "w_ukv", "w_o", "w_ff1", "w_ff2")
PIECE = dict(w_in=(512, 512), w_uq=(128, 256), w_ukv=(64, 256), w_o=(128, 1024), w_ff1=(512, 1024), w_ff2=(512, 1024))
SHARD_COLS = dict(w_in=496, w_uq=192, w_ukv=256, w_o=1024, w_ff1=1024, w_ff2=1024)


def _col_pieces(t, pad_to):
    rows, _, cols = t.shape
    t = jnp.pad(t, ((0, 0), (0, 0), (0, pad_to - cols)))
    return t.reshape(2, rows // 2, N_CHIPS, pad_to).transpose(2, 0, 1, 3)


def _grad_pieces(dws):
    def heads(t, rows):
        return jnp.concatenate([t[:, :512].reshape(rows, MLA_HEADS, LANES), t[:, 512:].reshape(rows, MLA_HEADS, LANES)], axis=2)

    make = dict(
        w_in=lambda t: _col_pieces(_unpad_w_in(t).reshape(D_MODEL, N_CHIPS, SHARD_COLS["w_in"]), 512),
        w_uq=lambda t: _col_pieces(heads(t, Q_LORA), 256),
        w_ukv=lambda t: _col_pieces(heads(t, KV_LORA), 256),
        w_o=lambda t: t.reshape(N_CHIPS, 2, 128, D_MODEL),
        w_ff1=lambda t: t,
        w_ff2=lambda t: t.reshape(N_CHIPS, 2, 512, D_MODEL))
    return {n: make[n](t) for n, t in dws.items()}


def _weights_from_pieces(p):
    def cols(t):
        return t.transpose(1, 2, 0, 3).reshape(2 * t.shape[2], N_CHIPS, t.shape[3])

    def heads(t, rows):
        return jnp.concatenate([t[:, :, :LANES].reshape(rows, 512), t[:, :, LANES:].reshape(rows, 512)], axis=1)

    make = dict(
        w_in=lambda t: _pad_w_in(cols(t)[:, :, :SHARD_COLS["w_in"]].reshape(D_MODEL, IN_COLS)),
        w_uq=lambda t: heads(cols(t), Q_LORA),
        w_ukv=lambda t: heads(cols(t), KV_LORA),
        w_o=lambda t: t.reshape(D_MODEL, D_MODEL),
        w_ff1=lambda t: t,
        w_ff2=lambda t: t.reshape(D_FF, D_MODEL))
    return {n: make[n](t) for n, t in p.items()}


def _my_piece(name, shard, half):
    rows, cols = shard.shape[0] // 2, shard.shape[1]
    t = lax.dynamic_slice_in_dim(shard, half * rows, rows, axis=0).astype(BF16)
    return jnp.pad(t, ((0, 0), (0, PIECE[name][1] - cols)))


MESH = pl.DeviceIdType.MESH
ANY = pl.BlockSpec(memory_space=pl.ANY)


class _Comm:
    def __init__(self, ins, out_shapes, n_sems, start, finish, middle=None):
        self.ins, self.out_shapes, self.n_sems, self.start, self.finish = list(ins), list(out_shapes), n_sems, start, finish
        self.middle = middle

    def sem_scratch(self):
        return [pltpu.SemaphoreType.DMA((self.n_sems,)), pltpu.SemaphoreType.DMA((self.n_sems,))]


def _comm_call(name, comm):
    n_in, n_out = len(comm.ins), len(comm.out_shapes)

    def body(*refs):
        ins, outs, sems = refs[:n_in], refs[n_in:n_in + n_out], refs[n_in + n_out:]
        comm.start(ins, outs, *sems)
        if comm.middle:
            comm.middle(ins, outs, *sems)
        comm.finish(ins, outs, *sems)

    return pl.pallas_call(body, name=name, out_shape=comm.out_shapes, in_specs=[ANY] * n_in, out_specs=[ANY] * n_out,
                          scratch_shapes=comm.sem_scratch())(*comm.ins)


def _all_gather_comm(blocks):
    n = len(blocks)

    def plan(x_refs, out_refs, send_sems, recv_sems):
        x, y, c = lax.axis_index("x"), lax.axis_index("y"), lax.axis_index("c")
        me, sibling = (x, y, c), (x, y, 1 - c)
        chips = [(1 - x, y), (x, 1 - y), (1 - x, 1 - y)]

        def copy(t, k, blk, to, src=None):
            px, py, pc = blk
            slot = out_refs[t].at[4 * px + 2 * py + pc]
            return pltpu.make_async_remote_copy(
                src_ref=slot if src is None else src, dst_ref=slot,
                send_sem=send_sems.at[7 * t + k], recv_sem=recv_sems.at[7 * t + k], device_id=to, device_id_type=MESH)

        first = []
        for t in range(n):
            first.append(copy(t, 0, me, sibling, src=x_refs[t]))
            first += [copy(t, 1 + j, me, (*chip, c), src=x_refs[t]) for j, chip in enumerate(chips)]
        return me, sibling, chips, c, copy, first

    def start(x_refs, out_refs, send_sems, recv_sems):
        for cp in plan(x_refs, out_refs, send_sems, recv_sems)[-1]:
            cp.start()

    def pass_on(x_refs, out_refs, send_sems, recv_sems):
        me, sibling, chips, c, copy, _ = plan(x_refs, out_refs, send_sems, recv_sems)
        for j, chip in enumerate(chips):
            for t in range(n):
                copy(t, 1 + j, (*chip, c), me).wait_recv()
                copy(t, 4 + j, (*chip, c), sibling).start()

    def finish(x_refs, out_refs, send_sems, recv_sems):
        me, sibling, chips, c, copy, sent = plan(x_refs, out_refs, send_sems, recv_sems)
        for t in range(n):
            copy(t, 0, sibling, me).wait_recv()
        for j, chip in enumerate(chips):
            for t in range(n):
                copy(t, 4 + j, (*chip, 1 - c), me).wait_recv()
                sent.append(copy(t, 4 + j, (*chip, c), sibling))
        for cp in sent:
            cp.wait_send()

    return _Comm(blocks, [_sds((N_DEV,) + b.shape, b.dtype) for b in blocks], 7 * n, start, finish, middle=pass_on)


def _simple_comm(ins, out_shapes, n_sems, copies):
    def start(in_refs, out_refs, send_sems, recv_sems):
        for cp in copies(in_refs, out_refs, send_sems, recv_sems):
            cp.start()

    def finish(in_refs, out_refs, send_sems, recv_sems):
        for cp in copies(in_refs, out_refs, send_sems, recv_sems):
            cp.wait()

    return _Comm(ins, out_shapes, n_sems, start, finish)


def _swap_comm(gs):
    def copies(g_refs, got_refs, send_sems, recv_sems):
        c = lax.axis_index("c")
        sibling = (lax.axis_index("x"), lax.axis_index("y"), 1 - c)
        return [pltpu.make_async_remote_copy(src_ref=g_refs[t].at[k, 1 - c], dst_ref=got_refs[t].at[k],
                                             send_sem=send_sems.at[N_CHIPS * t + k], recv_sem=recv_sems.at[N_CHIPS * t + k],
                                             device_id=sibling, device_id_type=MESH)
                for t in range(len(gs)) for k in range(N_CHIPS)]

    return _simple_comm(gs, [_sds((N_CHIPS,) + g.shape[2:], g.dtype) for g in gs], N_CHIPS * len(gs), copies)


def _scatter_comm(hs):
    def copies(h_refs, got_refs, send_sems, recv_sems):
        x, y, c = lax.axis_index("x"), lax.axis_index("y"), lax.axis_index("c")
        chips = [(1 - x, y), (x, 1 - y), (1 - x, 1 - y)]
        return [pltpu.make_async_remote_copy(src_ref=h_refs[t].at[2 * px + py], dst_ref=got_refs[t].at[rel],
                                             send_sem=send_sems.at[3 * t + rel], recv_sem=recv_sems.at[3 * t + rel],
                                             device_id=(px, py, c), device_id_type=MESH)
                for t in range(len(hs)) for rel, (px, py) in enumerate(chips)]

    return _simple_comm(hs, [_sds((3,) + h.shape[1:], h.dtype) for h in hs], 3 * len(hs), copies)


def _share_comm(rs):
    def copies(r_refs, got_refs, send_sems, recv_sems):
        sibling = (lax.axis_index("x"), lax.axis_index("y"), 1 - lax.axis_index("c"))
        return [pltpu.make_async_remote_copy(src_ref=r_refs[t], dst_ref=got_refs[t], send_sem=send_sems.at[t],
                                             recv_sem=recv_sems.at[t], device_id=sibling, device_id_type=MESH)
                for t in range(len(rs))]

    return _simple_comm(rs, [_sds(r.shape, r.dtype) for r in rs], len(rs), copies)


SUM_STEPS = 4


def _pair_sums(name, gs, gots, where):
    n = len(gs)
    dims = [(g.shape[2] // SUM_STEPS, g.shape[3]) for g in gs]

    def body(where_ref, *refs):
        own = pl.program_id(1) == where_ref[1]
        for t in range(n):
            s = refs[t][...] + refs[n + t][...]
            refs[3 * n + t][...] = s.astype(BF16)

            @pl.when(own)
            def _():
                refs[2 * n + t][...] = s

    def part(tr, pc):
        return pl.BlockSpec((None, tr, pc), lambda i, k, where_ref: (k, i, 0))

    def kept(tr, pc):
        return pl.BlockSpec((None, None, tr, pc), lambda i, k, where_ref: (k, where_ref[0], i, 0))

    outs = pl.pallas_call(
        body, name=name,
        out_shape=[_sds(g.shape[2:]) for g in gs] + [_sds((N_CHIPS,) + g.shape[2:], BF16) for g in gs],
        grid_spec=pltpu.PrefetchScalarGridSpec(
            num_scalar_prefetch=1, grid=(SUM_STEPS, N_CHIPS),
            in_specs=[kept(*d) for d in dims] + [part(*d) for d in dims],
            out_specs=[pl.BlockSpec(d, lambda i, k, where_ref: (i, 0)) for d in dims] + [part(*d) for d in dims]),
        compiler_params=_params(("parallel", "arbitrary")),
    )(where, *gs, *gots)
    return list(outs[:n]), list(outs[n:])


def _chips_sums(name, owns, gots):
    n = len(owns)
    dims = [(h.shape[0] // SUM_STEPS, h.shape[1]) for h in owns]

    def body(*refs):
        for t in range(n):
            got_ref = refs[n + t]
            refs[2 * n + t][...] = refs[t][...] + got_ref[0] + got_ref[1] + got_ref[2]

    row = [pl.BlockSpec((tr, pc), lambda i: (i, 0)) for tr, pc in dims]
    return pl.pallas_call(
        body, name=name, out_shape=[_sds(h.shape) for h in owns], grid=(SUM_STEPS,),
        in_specs=row + [pl.BlockSpec((3, tr, pc), lambda i: (0, i, 0)) for tr, pc in dims], out_specs=row,
        compiler_params=_params(("parallel",)),
    )(*owns, *gots)


class _Reduce:
    def __init__(self, tag, dws, c, chip):
        gp = _grad_pieces(dws)
        self.tag, self.c, self.names, self.pieces = tag, c, list(gp), list(gp.values())
        self.where = jnp.stack([c, chip]).astype(jnp.int32)
        self.shards = None

    def after_swap(self, gots):
        self.s32, self.s16 = _pair_sums(f"rs_pair_sum_{self.tag}", self.pieces, gots, self.where)

    def after_scatter(self, parts):
        self.mine = list(_chips_sums(f"rs_chips_sum_{self.tag}", self.s32, parts))

    def after_share(self, theirs):
        self.shards = {}
        for n, a, b in zip(self.names, self.mine, theirs):
            both = jnp.where(self.c == 0, jnp.concatenate([a, b], axis=0), jnp.concatenate([b, a], axis=0))
            self.shards[n] = both[:, :SHARD_COLS[n]]

    def run_alone(self):
        self.after_swap(_comm_call(f"rs_pair_{self.tag}", _swap_comm(self.pieces)))
        self.after_scatter(_comm_call(f"rs_chips_{self.tag}", _scatter_comm(self.s16)))
        self.after_share(_comm_call(f"rs_share_{self.tag}", _share_comm(self.mine)))


class _LayerWeights:
    def __init__(self, shards, c, dev):
        self.shards, self.c, self.dev = shards, c, dev

    def comm(self, l, names=W_NAMES):
        return _all_gather_comm([_my_piece(n, self.shards[l][n], self.c) for n in names])

    def take(self, l, gathered, names=W_NAMES):
        pieces = {}
        for n, g in zip(names, gathered):
            mine = _my_piece(n, self.shards[l][n], self.c)
            pieces[n] = lax.dynamic_update_index_in_dim(g, mine, self.dev, 0).reshape((N_CHIPS, 2) + mine.shape)
        return _weights_from_pieces(pieces)

    def first(self):
        now, later = ("w_in", "w_uq", "w_ukv"), ("w_ff1", "w_ff2", "w_o")
        weights = self.take(0, _comm_call("ag_l0", self.comm(0, now)), now)
        comms = [self.comm(0, (n,)) for n in later]

        def take_rest(results):
            rest = {}
            for n, gathered in zip(later, results):
                rest.update(self.take(0, gathered, (n,)))
            return rest

        return weights, (comms, take_rest)


def _all_reduce_small(vec):
    rows, lanes = vec.shape

    def gather_body(x_ref, out_ref, send_sems, recv_sems):
        x, y, c = lax.axis_index("x"), lax.axis_index("y"), lax.axis_index("c")
        me, sibling = (x, y, c), (x, y, 1 - c)
        chips = [(1 - x, y), (x, 1 - y), (1 - x, 1 - y)]

        def slot(px, py, pc):
            return out_ref.at[4 * px + 2 * py + pc]

        def copy(k, blk, to, src=None):
            return pltpu.make_async_remote_copy(
                src_ref=slot(*blk) if src is None else src, dst_ref=slot(*blk),
                send_sem=send_sems.at[k], recv_sem=recv_sems.at[k], device_id=to, device_id_type=MESH)

        out_ref[4 * x + 2 * y + c] = x_ref[...]
        first = [copy(0, me, sibling, src=x_ref)]
        first += [copy(1 + j, me, (*chip, c), src=x_ref) for j, chip in enumerate(chips)]
        for cp in first:
            cp.start()
        passed = [copy(4 + j, (*chip, c), sibling) for j, chip in enumerate(chips)]
        for j, chip in enumerate(chips):
            copy(1 + j, (*chip, c), me).wait_recv()
            passed[j].start()
        copy(0, sibling, me).wait_recv()
        for j, chip in enumerate(chips):
            copy(4 + j, (*chip, 1 - c), me).wait_recv()
        for cp in first + passed:
            cp.wait_send()

    vmem = pl.BlockSpec(memory_space=pltpu.VMEM)
    allv = pl.pallas_call(
        gather_body, name="small_all_gather", out_shape=_sds((N_DEV, rows, lanes)),
        in_specs=[vmem], out_specs=vmem,
        scratch_shapes=[pltpu.SemaphoreType.DMA((7,)), pltpu.SemaphoreType.DMA((7,))],
    )(vec)

    def sum_body(a_ref, o_ref):
        acc = a_ref[0]
        for d in range(1, N_DEV):
            acc = acc + a_ref[d]
        o_ref[...] = acc

    return pl.pallas_call(sum_body, name="small_sum", out_shape=_sds((rows, lanes)), in_specs=[vmem], out_specs=vmem)(allv)


def _adamw(name, w, g, m, v, ts=512, comm=None):
    def fn(w_v, g_v, m_v, v_v):
        m_n = ADAM_B1 * m_v + (1.0 - ADAM_B1) * g_v
        v_n = ADAM_B2 * v_v + (1.0 - ADAM_B2) * (g_v * g_v)
        m_hat = m_n / (1.0 - ADAM_B1 ** ADAM_STEP)
        v_hat = v_n / (1.0 - ADAM_B2 ** ADAM_STEP)
        delta = -ADAM_LR * (m_hat / (jnp.sqrt(v_hat) + ADAM_EPS) + ADAM_WD * w_v)
        return delta, m_n, v_n

    return _rowwise(name, fn, [w, g, m, v], [_sds(w.shape)] * 3, ts=ts, comm=comm)


def _pack_small(per_layer):
    flat = jnp.concatenate([per_layer[l][n].reshape(-1) for l in range(DEPTH) for n in SMALL_NAMES])
    return jnp.pad(flat, (0, SMALL_ROWS * LANES - flat.shape[0])).reshape(SMALL_ROWS, LANES)


def _unpack_small(packed):
    flat = packed.reshape(-1)
    per = sum(SMALL_SIZES)
    out = {}
    for n, size, off in zip(SMALL_NAMES, SMALL_SIZES, [sum(SMALL_SIZES[:i]) for i in range(len(SMALL_SIZES))]):
        out[n] = jnp.stack([flat[l * per + off:l * per + off + size] for l in range(DEPTH)])
    return out


NEXT_WEIGHTS_BEHIND = (("w_ff1", "w_ff2"), ("w_in", "w_uq", "w_ukv", "w_o"))


def _fwd_bwd(x, target, layer_weights, smalls, on_layer_grads):
    depth = len(smalls)
    cos, sin = _rope_tables(x.shape[0])
    h_f32, h_bf = x, x.astype(BF16)
    saved = []
    w, rest = layer_weights.first()
    for l in range(depth):
        more = l + 1 < depth
        behind = [layer_weights.comm(l + 1, names) if more else None for names in NEXT_WEIGHTS_BEHIND]
        h_f32, h_bf, sv, brought = _layer_fwd(h_f32, h_bf, w, smalls[l], cos, sin, behind, rest)
        saved.append(sv)
        if more:
            w, rest = {}, None
            for names, got in zip(NEXT_WEIGHTS_BEHIND, brought):
                w.update(layer_weights.take(l + 1, got, names))
    loss_part, dh = _loss_fn(h_f32, target)
    small_grads, pending = [None] * depth, None
    for l in reversed(range(depth)):
        dh, small_grads[l], pending = _layer_bwd(dh, saved[l]["w"], smalls[l], saved[l], cos, sin,
                                                 lambda dws, l=l: on_layer_grads(l, dws), pending)
    return loss_part, dh, small_grads, pending


def kernel(x, w_in, q_a_norm, kv_a_norm, w_uq, w_ukv, a_out_norm, b_out_norm, w_o, ln1_g, ln1_b, w_ff1, w_ff2, ln2_g, ln2_b, loss_target, m_w_in, m_q_a_norm, m_kv_a_norm, m_w_uq, m_w_ukv, m_a_out_norm, m_b_out_norm, m_w_o, m_ln1_g, m_ln1_b, m_w_ff1, m_w_ff2, m_ln2_g, m_ln2_b, v_w_in, v_q_a_norm, v_kv_a_norm, v_w_uq, v_w_ukv, v_a_out_norm, v_b_out_norm, v_w_o, v_ln1_g, v_ln1_b, v_w_ff1, v_w_ff2, v_ln2_g, v_ln2_b):
    c = lax.axis_index("c")
    chip = 2 * lax.axis_index("x") + lax.axis_index("y")
    big = dict(w_in=w_in, w_uq=w_uq, w_ukv=w_ukv, w_o=w_o, w_ff1=w_ff1, w_ff2=w_ff2)
    big_m = dict(w_in=m_w_in, w_uq=m_w_uq, w_ukv=m_w_ukv, w_o=m_w_o, w_ff1=m_w_ff1, w_ff2=m_w_ff2)
    big_v = dict(w_in=v_w_in, w_uq=v_w_uq, w_ukv=v_w_ukv, w_o=v_w_o, w_ff1=v_w_ff1, w_ff2=v_w_ff2)
    small = dict(q_a_norm=q_a_norm, kv_a_norm=kv_a_norm, a_out_norm=a_out_norm, b_out_norm=b_out_norm,
                 ln1_g=ln1_g, ln1_b=ln1_b, ln2_g=ln2_g, ln2_b=ln2_b)
    small_m = dict(q_a_norm=m_q_a_norm, kv_a_norm=m_kv_a_norm, a_out_norm=m_a_out_norm, b_out_norm=m_b_out_norm,
                   ln1_g=m_ln1_g, ln1_b=m_ln1_b, ln2_g=m_ln2_g, ln2_b=m_ln2_b)
    small_v = dict(q_a_norm=v_q_a_norm, kv_a_norm=v_kv_a_norm, a_out_norm=v_a_out_norm, b_out_norm=v_b_out_norm,
                   ln1_g=v_ln1_g, ln1_b=v_ln1_b, ln2_g=v_ln2_g, ln2_b=v_ln2_b)

    layer_weights = _LayerWeights([{n: big[n][l] for n in W_NAMES} for l in range(DEPTH)], c, 2 * chip + c)
    smalls = [{n: small[n][l][None, :] for n in SMALL_NAMES} for l in range(DEPTH)]
    reductions = [[] for _ in range(DEPTH)]

    def reduce_layer(l, dws):
        reductions[l].append(_Reduce(f"l{l}_{len(reductions[l])}", dws, c, chip))
        return reductions[l][-1]

    loss_part, grad_x, small_grads, last = _fwd_bwd(x[0], loss_target[0], layer_weights, smalls, reduce_layer)
    loss = lax.psum(loss_part[0, 0], ("x", "y", "c"))
    grad_x = grad_x[None]

    g_small_packed = _all_reduce_small(_pack_small(small_grads))
    pack_in = lambda d: _pack_small([{n: d[n][l] for n in SMALL_NAMES} for l in range(DEPTH)])
    sd, sm_, sv_ = _adamw("adamw_small", pack_in(small), g_small_packed, pack_in(small_m), pack_in(small_v), ts=SMALL_ROWS)
    g_small, d_small, m_small, v_small = (_unpack_small(t) for t in (g_small_packed, sd, sm_, sv_))

    last.run_alone()
    g_big, d_big, m_big, v_big = {}, {}, {}, {}
    for n in W_NAMES:
        g = jnp.stack([next(r.shards[n] for r in reductions[l] if n in r.names) for l in range(DEPTH)])
        shape = g.shape
        flat = lambda t: t.reshape(shape[0] * shape[1], shape[2])
        d_, m_, v_ = _adamw(f"adamw_{n}", flat(big[n]), flat(g), flat(big_m[n]), flat(big_v[n]))
        g_big[n], d_big[n], m_big[n], v_big[n] = g, d_.reshape(shape), m_.reshape(shape), v_.reshape(shape)

    order = ("w_in", "q_a_norm", "kv_a_norm", "w_uq", "w_ukv", "a_out_norm", "b_out_norm", "w_o", "ln1_g", "ln1_b",
             "w_ff1", "w_ff2", "ln2_g", "ln2_b")
    pick = lambda bigd, smalld: [bigd[n] if n in bigd else smalld[n] for n in order]
    return (loss, grad_x, *pick(g_big, g_small), *pick(d_big, d_small), *pick(m_big, m_small), *pick(v_big, v_small))
```

```python
import math

import jax
import jax.numpy as jnp
from jax import lax
from jax.experimental import pallas as pl
from jax.experimental.pallas import tpu as pltpu

F32, BF16 = jnp.float32, jnp.bfloat16

D_MODEL = 1024
DEPTH = 4
A_HEAD_DIM = 64
A_HEADS = 8
A_WIDTH = 512
PATTERNS = ((128, 1), (512, 4), (2048, 16))
SPAN = 128
MLA_HEADS = 4
QK_NOPE = 128
QK_ROPE = 64
V_DIM = 128
Q_LORA = 256
KV_LORA = 128
MLA_WIDTH = 512
QK_PAD = 256
IN_COLS = 1984
IN_PAD = 2048
UQ_PAD = 1024
D_FF = 4096
ROPE_THETA = 10000.0
ALPHA = (2.0 * DEPTH) ** 0.25
LN_EPS = 1e-5
RMS_EPS = 1e-6
MLA_SCALE = (QK_NOPE + QK_ROPE) ** -0.5
A_SCALE = A_HEAD_DIM ** -0.5

ADAM_LR, ADAM_B1, ADAM_B2, ADAM_EPS, ADAM_WD, ADAM_STEP = 0.001, 0.9, 0.999, 1e-08, 0.01, 10

VMEM_LIMIT_BYTES = 56 * 1024 * 1024
NEG = -1e30
N_CHIPS, N_DEV = 4, 8
LANES = 128

SMALL_NAMES = ("q_a_norm", "kv_a_norm", "a_out_norm", "b_out_norm", "ln1_g", "ln1_b", "ln2_g", "ln2_b")
SMALL_SIZES = (256, 128, 512, 512, 1024, 1024, 1024, 1024)
SMALL_ROWS = 176


def _params(sem):
    return pltpu.CompilerParams(dimension_semantics=sem, vmem_limit_bytes=VMEM_LIMIT_BYTES)


FAST_STRIDE = 4


def _interleave(parts, scratch):
    d, (n, width) = len(parts), parts[0].shape
    if d > FAST_STRIDE and d % FAST_STRIDE == 0:
        inner = [_interleave(parts[r::FAST_STRIDE], scratch) for r in range(FAST_STRIDE)]
        return _interleave(inner, scratch)
    for r in range(d):
        for s in range(width // LANES):
            scratch.at[s][pl.ds(r, n, stride=d), :] = parts[r][:, s * LANES:(s + 1) * LANES]
    return jnp.concatenate([scratch.at[s][pl.ds(0, n * d), :] for s in range(width // LANES)], axis=1)


def _deinterleave(x, d, scratch):
    n, width = x.shape[0] // d, x.shape[1]
    if d > FAST_STRIDE and d % FAST_STRIDE == 0:
        outer = _deinterleave(x, FAST_STRIDE, scratch)
        inner = [_deinterleave(y, d // FAST_STRIDE, scratch) for y in outer]
        return [inner[r % FAST_STRIDE][r // FAST_STRIDE] for r in range(d)]
    for s in range(width // LANES):
        scratch.at[s][pl.ds(0, n * d), :] = x[:, s * LANES:(s + 1) * LANES]
    return [jnp.concatenate([scratch.at[s][pl.ds(r, n, stride=d), :] for s in range(width // LANES)], axis=1) for r in range(d)]


def _rowwise(name, fn, ins, outs, consts=(), reds=(), ts=512, in_dils=None, out_dils=None, comm=None):
    in_dils = list(in_dils or [1] * len(ins))
    out_dils = list(out_dils or [1] * len(outs))
    n_rows = ins[0].shape[-2] * in_dils[0]
    ts = min(ts, n_rows)
    assert n_rows % ts == 0
    n_in, n_c, n_o, n_r = len(ins), len(consts), len(outs), len(reds)
    viewed = [(a.shape, d) for a, d in zip(list(ins) + list(outs), in_dils + out_dils) if d > 1]
    grid = (n_rows // ts,)
    h_ins, h_in_specs, h_outs, h_out_specs, begin, end = _hosted(comm, grid)
    n_hi, n_ho = len(h_ins), len(h_outs)

    def tile_spec(shape, d=1):
        if len(shape) == 2:
            return pl.BlockSpec((ts // d, shape[1]), lambda i: (i, 0))
        return pl.BlockSpec((shape[0], ts, shape[2]), lambda i: (0, i, 0))

    def whole_spec(shape):
        return pl.BlockSpec(shape, lambda i: (0,) * len(shape))

    def body(*refs):
        in_refs, c_refs, hi_refs = refs[:n_in], refs[n_in:n_in + n_c], refs[n_in + n_c:n_in + n_c + n_hi]
        refs = refs[n_in + n_c + n_hi:]
        o_refs, r_refs, ho_refs = refs[:n_o], refs[n_o:n_o + n_r], refs[n_o + n_r:n_o + n_r + n_ho]
        scratches = list(refs[n_o + n_r + n_ho:])
        sems = [scratches.pop(), scratches.pop()][::-1] if comm else []
        begin(hi_refs, ho_refs, sems)
        vals = []
        for r, d in zip(in_refs, in_dils):
            if d == 1:
                vals.append(r[...])
            else:
                width = r.shape[1] // d
                vals.append(_interleave([r[:, k * width:(k + 1) * width] for k in range(d)], scratches.pop(0)))
        res = fn(*vals, *[r[...] for r in c_refs])
        res = tuple(res) if isinstance(res, (tuple, list)) else (res,)
        for r, v, d in zip(o_refs, res[:n_o], out_dils):
            if len(r.shape) == 3:
                for g in range(r.shape[0]):
                    r[g] = v[g].astype(r.dtype)
            elif d == 1:
                r[...] = v.astype(r.dtype)
            else:
                width = r.shape[1] // d
                for k, part in enumerate(_deinterleave(v, d, scratches.pop(0))):
                    r[:, k * width:(k + 1) * width] = part.astype(r.dtype)
        if n_r:
            i = pl.program_id(0)

            @pl.when(i == 0)
            def _():
                for r, v in zip(r_refs, res[n_o:]):
                    r[...] = v

            @pl.when(i > 0)
            def _():
                for r, v in zip(r_refs, res[n_o:]):
                    r[...] += v
        end(hi_refs, ho_refs, sems)

    out = pl.pallas_call(
        body, name=name, grid=grid,
        in_specs=[tile_spec(a.shape, d) for a, d in zip(ins, in_dils)] + [whole_spec(c.shape) for c in consts] + h_in_specs,
        out_specs=[tile_spec(o.shape, d) for o, d in zip(outs, out_dils)] + [whole_spec(r.shape) for r in reds] + h_out_specs,
        out_shape=list(outs) + list(reds) + h_outs,
        scratch_shapes=[pltpu.VMEM((shape[1] // d // LANES, ts, LANES), F32) for shape, d in viewed]
        + (comm.sem_scratch() if comm else []),
        compiler_params=_params(("arbitrary",)),
    )(*ins, *consts, *h_ins)
    return out


def _sds(shape, dtype=F32):
    return jax.ShapeDtypeStruct(tuple(shape), dtype)


def _hosted(comm, grid, long_host=False):
    if comm is None:
        return [], [], [], [], (lambda *a: None), (lambda *a: None)

    def edge(which, at, ins, outs, sems):
        ids = [pl.program_id(d) for d in range(len(grid))]
        hit = ids[0] == at[0]
        for i, v in zip(ids[1:], at[1:]):
            hit = hit & (i == v)

        @pl.when(hit)
        def _():
            which(ins, outs, *sems)

    last = [g - 1 for g in grid]
    late, at_late = (3 * math.prod(grid)) // 4, []
    for g in reversed(grid):
        late, r = divmod(late, g)
        at_late.insert(0, r)

    def begin(ins, outs, sems):
        edge(comm.start, [0] * len(grid), ins, outs, sems)
        if comm.middle and long_host:
            edge(comm.middle, at_late, ins, outs, sems)

    def end(ins, outs, sems):
        if comm.middle and not long_host:
            edge(comm.middle, last, ins, outs, sems)
        edge(comm.finish, last, ins, outs, sems)
    return comm.ins, [ANY] * len(comm.ins), comm.out_shapes, [ANY] * len(comm.out_shapes), begin, end


def _mm(name, a, b, out_dtypes, *, ta=False, tb=False, tm=512, tn=512, tk=512, epilogue=None, extras=(), vecs=(),
        comm=None, b_pieces=False, out_pieces=False):
    (k_dim, m_dim) = a.shape if ta else a.shape[::-1]
    if b_pieces:
        b_rows, b_cols = 2 * b.shape[2], N_CHIPS * b.shape[3]
        (n_dim, k2) = (b_rows, b_cols) if tb else (b_cols, b_rows)
    else:
        (n_dim, k2) = b.shape if tb else b.shape[::-1]
    assert k_dim == k2
    tm, tn, tk = min(tm, m_dim), min(tn, n_dim), min(tk, k_dim)
    assert m_dim % tm == 0 and n_dim % tn == 0 and k_dim % tk == 0
    nk = k_dim // tk
    grid = (m_dim // tm, n_dim // tn, nk)
    a_spec = pl.BlockSpec((tk, tm), lambda i, j, k: (k, i)) if ta else pl.BlockSpec((tm, tk), lambda i, j, k: (i, k))
    if b_pieces:
        assert tn == b_rows and tk % b.shape[3] == 0 if tb else (tk, tn) == (b_rows, b.shape[3])
        n_bp = tk // b.shape[3] if tb else 1
        b_spec = pl.BlockSpec((n_bp,) + b.shape[1:], (lambda i, j, k: (k, 0, 0, 0)) if tb else (lambda i, j, k: (j, 0, 0, 0)))
    else:
        b_spec = pl.BlockSpec((tn, tk), lambda i, j, k: (j, k)) if tb else pl.BlockSpec((tk, tn), lambda i, j, k: (k, j))
    assert not out_pieces or (tm == m_dim and len(out_dtypes) == 1)
    dims = (((0 if ta else 1,), (1 if tb else 0,)), ((), ()))
    n_e, n_v, n_o = len(extras), len(vecs), len(out_dtypes)
    c_ins, c_in_specs, c_outs, c_out_specs, begin, end = _hosted(comm, grid)
    n_ci, n_co = len(c_ins), len(c_outs)

    def body(a_ref, b_ref, *refs):
        e_refs, v_refs, ci_refs = refs[:n_e], refs[n_e:n_e + n_v], refs[n_e + n_v:n_e + n_v + n_ci]
        refs = refs[n_e + n_v + n_ci:]
        o_refs, co_refs = refs[:n_o], refs[n_o:n_o + n_co]
        scratch = refs[n_o + n_co:]
        sems = scratch[-2:]
        begin(ci_refs, co_refs, sems)
        if b_pieces:
            pc = b.shape[3]
            part = None
            for cc in range(n_bp):
                a_val = a_ref[:, cc * pc:(cc + 1) * pc] if tb else a_ref[...]
                term = lax.dot_general(a_val.astype(BF16), b_ref[cc].reshape(b_rows, pc).astype(BF16), dims,
                                       preferred_element_type=F32)
                part = term if part is None else part + term
        else:
            part = lax.dot_general(a_ref[...].astype(BF16), b_ref[...].astype(BF16), dims, preferred_element_type=F32)

        def finish(acc):
            outs = epilogue(acc, *[r[...] for r in e_refs], *[r[...] for r in v_refs]) if epilogue else (acc,)
            for r, v in zip(o_refs, outs):
                r[...] = (v.reshape(2, tm // 2, tn) if out_pieces else v).astype(r.dtype)

        if nk == 1:
            finish(part)
        else:
            acc_ref = scratch[0]
            k = pl.program_id(2)

            @pl.when(k == 0)
            def _():
                acc_ref[...] = part

            @pl.when(k > 0)
            def _():
                acc_ref[...] += part

            @pl.when(k == nk - 1)
            def _():
                finish(acc_ref[...])
        end(ci_refs, co_refs, sems)

    tile = pl.BlockSpec((tm, tn), lambda i, j, k: (i, j))
    if out_pieces:
        out_tile, out_shape = pl.BlockSpec((None, 2, tm // 2, tn), lambda i, j, k: (j, 0, 0, 0)), (n_dim // tn, 2, tm // 2, tn)
    else:
        out_tile, out_shape = tile, (m_dim, n_dim)
    out = pl.pallas_call(
        body, name=name, grid=grid,
        in_specs=[a_spec, b_spec] + [tile] * n_e + [pl.BlockSpec((1, tn), lambda i, j, k: (0, j))] * n_v + c_in_specs,
        out_specs=[out_tile] * n_o + c_out_specs,
        out_shape=[_sds(out_shape, d) for d in out_dtypes] + c_outs,
        scratch_shapes=([pltpu.VMEM((tm, tn), F32)] if nk > 1 else []) + (comm.sem_scratch() if comm else []),
        compiler_params=_params(("arbitrary",) * 3 if comm else ("parallel", "parallel", "arbitrary")),
    )(a, b, *extras, *vecs, *c_ins)
    return out


def _swap_halves(x):
    width = x.shape[1]
    lane = lax.broadcasted_iota(jnp.int32, x.shape, 1)
    return jnp.where((lane % 64) < 32, pltpu.roll(x, width - 32, 1), pltpu.roll(x, 32, 1))


def _rope(x, cos, sin_signed):
    return x * cos + _swap_halves(x) * sin_signed


def _rope_t(d, cos, sin_signed):
    return d * cos - _swap_halves(d) * sin_signed


def _rms(x, g):
    rstd = lax.rsqrt(jnp.mean(x * x, axis=-1, keepdims=True) + RMS_EPS)
    return x * rstd * g


def _rms_bwd(x, g, dy):
    rstd = lax.rsqrt(jnp.mean(x * x, axis=-1, keepdims=True) + RMS_EPS)
    xh = x * rstd
    dyg = dy * g
    dx = rstd * (dyg - xh * jnp.mean(dyg * xh, axis=-1, keepdims=True))
    return dx, jnp.sum(dy * xh, axis=0, keepdims=True)


def _layer_norm(r, g, b):
    mu = jnp.mean(r, axis=-1, keepdims=True)
    xc = r - mu
    var = jnp.mean(xc * xc, axis=-1, keepdims=True)
    return xc * lax.rsqrt(var + LN_EPS) * g + b


def _dot(a, b, dims):
    return lax.dot_general(a, b, (dims, ((), ())), preferred_element_type=F32)


_NN, _NT, _TN = ((1,), (0,)), ((1,), (1,)), ((0,), (0,))


def _in_proj(x_bf, w_in_p, cos, sin):
    n_rows = x_bf.shape[0]
    ts = 512
    dils = [dil for _, dil in PATTERNS]
    n_p = len(dils)

    def body(x_ref, w_ref, cos_ref, sin_ref, *refs):
        view_refs, (cq_ref, kpe_ref, ckv_ref, scratch) = refs[:n_p], refs[n_p:]
        xb = x_ref[...]
        cos_v, sin_v = cos_ref[...], sin_ref[...]

        def seg(lo, hi):
            return _dot(xb, w_ref[:, lo:hi], _NN)

        def put(part, val):
            by_dil = {1: [val]}
            for ref, dil in zip(view_refs, dils):
                if dil not in by_dil:
                    if dil > FAST_STRIDE and dil % FAST_STRIDE == 0:
                        if FAST_STRIDE not in by_dil:
                            by_dil[FAST_STRIDE] = _deinterleave(val, FAST_STRIDE, scratch)
                        inner = [_deinterleave(y, dil // FAST_STRIDE, scratch) for y in by_dil[FAST_STRIDE]]
                        by_dil[dil] = [inner[r % FAST_STRIDE][r // FAST_STRIDE] for r in range(dil)]
                    else:
                        by_dil[dil] = _deinterleave(val, dil, scratch)
                for r, piece in enumerate(by_dil[dil]):
                    lo = (3 * r + part) * A_WIDTH
                    ref[:, lo:lo + A_WIDTH] = piece.astype(BF16)

        put(0, _rope(seg(0, 512), cos_v, sin_v) * A_SCALE)
        put(1, _rope(seg(512, 1024), cos_v, sin_v))
        put(2, seg(1024, 1536))
        cq_ref[...] = seg(1536, 1792)
        kpe_ref[...] = _rope(seg(1792, 1920), cos_v[:, :LANES], sin_v[:, :LANES]).astype(BF16)
        ckv_ref[...] = seg(1920, 2048)

    row = lambda c, d=1: pl.BlockSpec((ts // d, c * d), lambda i: (i, 0))
    return pl.pallas_call(
        body, name="in_proj", grid=(n_rows // ts,),
        in_specs=[row(D_MODEL), pl.BlockSpec((D_MODEL, IN_PAD), lambda i: (0, 0)), row(A_WIDTH), row(A_WIDTH)],
        out_specs=[row(3 * A_WIDTH, d) for d in dils] + [row(Q_LORA), row(LANES), row(KV_LORA)],
        out_shape=[_sds((n_rows // d, d * 3 * A_WIDTH), BF16) for d in dils]
        + [_sds((n_rows, Q_LORA)), _sds((n_rows, LANES), BF16), _sds((n_rows, KV_LORA))],
        scratch_shapes=[pltpu.VMEM((A_WIDTH // LANES, ts, LANES), F32)],
        compiler_params=_params(("parallel",)),
    )(x_bf, w_in_p, cos, sin)


def _band_mask(m, heads):
    qi = lax.broadcasted_iota(jnp.int32, (heads * SPAN, 2 * SPAN), 0) % SPAN
    kj = lax.broadcasted_iota(jnp.int32, (heads * SPAN, 2 * SPAN), 1)
    return ((kj < SPAN) & (kj >= qi) & (m > 0)) | ((kj >= SPAN) & ((kj - SPAN) <= qi))


def _even_lanes(rows):
    return lax.broadcasted_iota(jnp.int32, (rows, LANES), 1) < A_HEAD_DIM


DIL_GROUP = 4
HEAD_SLOT = LANES // A_HEADS
O_WORDS = A_WIDTH // 2
OL_WIDTH = O_WORDS + LANES


def _pack_halves(t):
    return _pack2(t[:, :O_WORDS], t[:, O_WORDS:])


def _unpack_halves(p):
    return jnp.concatenate(_unpack2(p), axis=1)


DQKV_WIDTH = 3 * A_WIDTH // 2


def _pack2(a, b):
    hi = lax.bitcast_convert_type(a.astype(jnp.bfloat16).astype(F32), jnp.uint32) & jnp.uint32(0xFFFF0000)
    lo = lax.shift_right_logical(lax.bitcast_convert_type(b.astype(jnp.bfloat16).astype(F32), jnp.uint32), jnp.uint32(16))
    return lax.bitcast_convert_type(hi | lo, F32)


def _unpack2(p):
    u = lax.bitcast_convert_type(p, jnp.uint32)
    return (lax.bitcast_convert_type(u & jnp.uint32(0xFFFF0000), F32),
            lax.bitcast_convert_type(lax.shift_left(u, jnp.uint32(16)), F32))


def _slot_lane(h):
    return HEAD_SLOT * (h // 2) + A_HEAD_DIM * (h % 2)


def _slot_head(lane):
    return 2 * (jnp.right_shift(lane, 4) & 3) + jnp.right_shift(lane, 6)


def _dil_fwd(qkv_view, dil, comm=None):
    nb = qkv_view.shape[0] // SPAN
    group = min(dil, DIL_GROUP)
    grid = (dil // group, nb)
    c_ins, c_in_specs, c_outs, c_out_specs, begin, end = _hosted(comm, grid)
    n_ci, n_co = len(c_ins), len(c_outs)

    def body(cur_ref, prev_ref, *refs):
        ci_refs, ol_ref, co_refs, sems = refs[:n_ci], refs[n_ci], refs[n_ci + 1:n_ci + 1 + n_co], refs[n_ci + 1 + n_co:]
        begin(ci_refs, co_refs, sems)
        mask = _band_mask(pl.program_id(1), 1)
        even, even2 = _even_lanes(SPAN), _even_lanes(2 * SPAN)
        pair_of_lane = jnp.right_shift(lax.broadcasted_iota(jnp.int32, (SPAN, LANES), 1), 4) & 3
        for g in range(group):
            lse_c, o_pairs = jnp.zeros((SPAN, LANES), F32), []
            for p in range(A_HEADS // 2):
                q_sl, k_sl, v_sl = (slice((3 * g + t) * A_WIDTH + p * LANES, (3 * g + t) * A_WIDTH + (p + 1) * LANES)
                                    for t in range(3))
                q2 = cur_ref[:, q_sl]
                kcat = jnp.concatenate([prev_ref[:, k_sl], cur_ref[:, k_sl]], axis=0)
                vcat = jnp.concatenate([prev_ref[:, v_sl], cur_ref[:, v_sl]], axis=0)
                zero, one = jnp.zeros_like(q2), jnp.ones_like(vcat)
                res, lses = [], []
                for first in (True, False):
                    qh = jnp.where(even, q2, zero) if first else jnp.where(even, zero, q2)
                    vh = jnp.where(even2, vcat, one) if first else jnp.where(even2, one, vcat)
                    s = jnp.where(mask, _dot(qh, kcat, _NT), NEG)
                    mx = jnp.max(s, axis=-1, keepdims=True)
                    r = _dot(jnp.exp(s - mx).astype(BF16), vh, _NN)
                    den = pltpu.roll(r, A_HEAD_DIM, 1)
                    res.append(r / den)
                    lses.append(mx + jnp.log(den))
                o_pairs.append(jnp.where(even, res[0], res[1]))
                lse_c = jnp.where(pair_of_lane == p, jnp.where(even, lses[0], lses[1]), lse_c)
            for p in range(2):
                ol_ref[:, OL_WIDTH * g + p * LANES:OL_WIDTH * g + (p + 1) * LANES] = _pack2(o_pairs[p], o_pairs[p + 2])
            ol_ref[:, OL_WIDTH * g + O_WORDS:OL_WIDTH * (g + 1)] = lse_c
        end(ci_refs, co_refs, sems)

    return pl.pallas_call(
        body, name=f"dil_fwd_d{dil}", grid=grid,
        in_specs=[pl.BlockSpec((SPAN, group * 3 * A_WIDTH), lambda r, m: (m, r)),
                  pl.BlockSpec((SPAN, group * 3 * A_WIDTH), lambda r, m: (jnp.maximum(m - 1, 0), r))] + c_in_specs,
        out_specs=[pl.BlockSpec((SPAN, group * OL_WIDTH), lambda r, m: (m, r))] + c_out_specs,
        out_shape=[_sds((qkv_view.shape[0], dil * OL_WIDTH))] + c_outs,
        scratch_shapes=comm.sem_scratch() if comm else [],
        compiler_params=_params(("arbitrary", "arbitrary") if comm else ("parallel", "arbitrary")),
    )(qkv_view, qkv_view, *c_ins)


def _slot_spread_matrix():
    i = lax.broadcasted_iota(jnp.int32, (LANES, A_WIDTH), 0)
    h = lax.broadcasted_iota(jnp.int32, (LANES, A_WIDTH), 1) // A_HEAD_DIM
    return (i == HEAD_SLOT * (h // 2) + A_HEAD_DIM * (h % 2)).astype(BF16)


def _spread_slots(c, spread):
    hi = c.astype(BF16)
    r1 = c - hi.astype(F32)
    mid = r1.astype(BF16)
    lo = (r1 - mid.astype(F32)).astype(BF16)
    return _dot(hi, spread, _NN) + _dot(mid, spread, _NN) + _dot(lo, spread, _NN)


def _pattern_weights(ol1, ol2, ol3):
    l1, l2, l3 = ol1[:, O_WORDS:], ol2[:, O_WORDS:], ol3[:, O_WORDS:]
    mx = jnp.maximum(jnp.maximum(l1, l2), l3)
    e1, e2, e3 = jnp.exp(l1 - mx), jnp.exp(l2 - mx), jnp.exp(l3 - mx)
    inv = 1.0 / (e1 + e2 + e3)
    return e1 * inv, e2 * inv, e3 * inv


def _mix_fwd(ols, b_out, a_g, b_g):
    def fn(ol1, ol2, ol3, b, ag, bg, spread):
        ws = [_spread_slots(w, spread) for w in _pattern_weights(ol1, ol2, ol3)]
        a = sum(w * _unpack_halves(ol[:, :O_WORDS]) for w, ol in zip(ws, (ol1, ol2, ol3)))
        return a, jnp.concatenate([_rms(a, ag), _rms(b, bg)], axis=1)

    n_rows = b_out.shape[0]
    return _rowwise("mix_fwd", fn, [*ols, b_out], [_sds((n_rows, A_WIDTH)), _sds((n_rows, 2 * A_WIDTH), BF16)],
                    consts=[a_g, b_g, _slot_spread_matrix()], in_dils=[dil for _, dil in PATTERNS] + [1])


def _mla_prep(cq, ckv, kpe, cos, sin, gq, w_uq_p, gkv, w_ukv_p):
    def fn(cq_v, ckv_v, kpe_v, cos_v, sin_v, gq_v, wq_v, gkv_v, wkv_v):
        cqn = _rms(cq_v, gq_v).astype(BF16)
        q = _dot(cqn, wq_v, _NN)
        qf = [jnp.concatenate([q[:, h * QK_NOPE:(h + 1) * QK_NOPE],
                               _rope(q[:, 512 + h * LANES:512 + (h + 1) * LANES], cos_v[:, :LANES], sin_v[:, :LANES])], axis=1)
              for h in range(MLA_HEADS)]
        ckvn = _rms(ckv_v, gkv_v).astype(BF16)
        kvv = _dot(ckvn, wkv_v, _NN)
        kf = [jnp.concatenate([kvv[:, h * QK_NOPE:(h + 1) * QK_NOPE], kpe_v.astype(F32)], axis=1) for h in range(MLA_HEADS)]
        return cqn, qf, ckvn, kf, kvv[:, 512:]

    n_rows = cq.shape[0]
    heads = _sds((MLA_HEADS, n_rows, QK_PAD), BF16)
    return _rowwise("mla_prep", fn, [cq, ckv, kpe, cos, sin],
                    [_sds((n_rows, Q_LORA), BF16), heads, _sds((n_rows, KV_LORA), BF16), heads, _sds((n_rows, MLA_WIDTH), BF16)],
                    consts=[gq, w_uq_p, gkv, w_ukv_p])


MLA_FWD_TILES = (1024, 2048)
MLA_BWD_TILES = (1024, 1024)


def _mla_fwd(qf, kf, v, comm=None):
    n_rows = v.shape[0]
    tq, tk = min(MLA_FWD_TILES[0], n_rows), min(MLA_FWD_TILES[1], n_rows)
    nq, nk = n_rows // tq, n_rows // tk
    grid = (MLA_HEADS, nq, nk)
    exp2_scale = MLA_SCALE * math.log2(math.e)
    c_ins, c_in_specs, c_outs, c_out_specs, begin, end = _hosted(comm, grid, long_host=True)
    n_ci, n_co = len(c_ins), len(c_outs)

    def last_k(i):
        return jnp.right_shift(i * tq + tq - 1, int(math.log2(tk)))

    def body(q_ref, k_ref, v_ref, *refs):
        ci_refs, (o_ref, lse_ref), co_refs = refs[:n_ci], refs[n_ci:n_ci + 2], refs[n_ci + 2:n_ci + 2 + n_co]
        m_sc, l_sc, acc_sc = refs[n_ci + 2 + n_co:n_ci + 5 + n_co]
        sems = refs[n_ci + 5 + n_co:]
        i, j = pl.program_id(1), pl.program_id(2)
        begin(ci_refs, co_refs, sems)

        @pl.when(j == 0)
        def _():
            m_sc[...] = jnp.full(m_sc.shape, NEG, F32)
            l_sc[...] = jnp.zeros(l_sc.shape, F32)
            acc_sc[...] = jnp.zeros(acc_sc.shape, F32)

        def step(masked):
            s = _dot(q_ref[...], k_ref[...], _NT)
            if masked:
                row = lax.broadcasted_iota(jnp.int32, (tq, tk), 0) + i * tq
                col = lax.broadcasted_iota(jnp.int32, (tq, tk), 1) + j * tk
                s = jnp.where(col <= row, s, NEG)
            m_prev = m_sc[...]
            m_new = jnp.maximum(m_prev, jnp.max(s, axis=-1, keepdims=True))
            alpha = jnp.exp2((m_prev - m_new) * exp2_scale)
            p = jnp.exp2((s - m_new) * exp2_scale)
            l_sc[...] = alpha * l_sc[...] + jnp.sum(p, axis=-1, keepdims=True)
            acc_sc[...] = alpha * acc_sc[...] + _dot(p.astype(BF16), v_ref[...], _NN)
            m_sc[...] = m_new

        active = j * tk <= i * tq + tq - 1
        crosses = (j + 1) * tk - 1 > i * tq

        @pl.when(active & jnp.logical_not(crosses))
        def _():
            step(False)

        @pl.when(active & crosses)
        def _():
            step(True)

        @pl.when(j == last_k(i))
        def _():
            o_ref[...] = acc_sc[...] / l_sc[...]
            lse_ref[...] = jnp.broadcast_to(m_sc[...] * MLA_SCALE + jnp.log(l_sc[...]), (tq, V_DIM))

        end(ci_refs, co_refs, sems)

    return pl.pallas_call(
        body, name="mla_fwd", grid=grid,
        in_specs=[pl.BlockSpec((None, tq, QK_PAD), lambda h, i, j: (h, i, 0)),
                  pl.BlockSpec((None, tk, QK_PAD), lambda h, i, j: (h, jnp.minimum(j, last_k(i)), 0)),
                  pl.BlockSpec((tk, V_DIM), lambda h, i, j: (jnp.minimum(j, last_k(i)), h))] + c_in_specs,
        out_specs=[pl.BlockSpec((tq, V_DIM), lambda h, i, j: (i, h)), pl.BlockSpec((tq, V_DIM), lambda h, i, j: (i, h))]
        + c_out_specs,
        out_shape=[_sds((n_rows, MLA_WIDTH)), _sds((n_rows, MLA_WIDTH))] + c_outs,
        scratch_shapes=[pltpu.VMEM((tq, 1), F32), pltpu.VMEM((tq, 1), F32), pltpu.VMEM((tq, V_DIM), F32)]
        + (comm.sem_scratch() if comm else []),
        compiler_params=_params(("arbitrary",) * 3 if comm else ("parallel", "parallel", "arbitrary")),
    )(qf, kf, v, *c_ins)


def _mm_res_ln(name, a, w, xres, g, b, tm):
    def epi(acc, xr, g_v, b_v):
        r = ALPHA * xr + acc
        y = _layer_norm(r, g_v, b_v)
        return y, y, r

    return _mm(name, a, w, (F32, BF16, F32), tm=tm, tn=D_MODEL, tk=a.shape[1], epilogue=epi, extras=[xres], vecs=[g, b])


def _mm_relu2(x_bf, w, comm=None):
    def epi(acc):
        r = jnp.maximum(acc, 0.0)
        return (r * r,)

    return _mm("ff1", x_bf, w, (BF16,), tm=2048, tn=1024, tk=D_MODEL, epilogue=epi, b_pieces=True, comm=comm)


def _loss_fn(y, t):
    def fn(y_v, t_v):
        d = y_v - t_v
        part = jnp.sum(jnp.sum(d * d, axis=1, keepdims=True), axis=0, keepdims=True)
        return d * (1.0 / D_MODEL), part

    dy, part = _rowwise("loss", fn, [y, t], [_sds(y.shape)], reds=[_sds((1, 1))])
    return part * (0.5 / D_MODEL), dy


def _ln_bwd_rows(dy, r, g):
    mu = jnp.mean(r, axis=-1, keepdims=True)
    xc = r - mu
    rstd = lax.rsqrt(jnp.mean(xc * xc, axis=-1, keepdims=True) + LN_EPS)
    xh = xc * rstd
    dxh = dy * g
    dr = rstd * (dxh - jnp.mean(dxh, axis=-1, keepdims=True) - xh * jnp.mean(dxh * xh, axis=-1, keepdims=True))
    return dr, dr, jnp.sum(dy * xh, axis=0, keepdims=True), jnp.sum(dy, axis=0, keepdims=True)


def _ln_bwd(name, dy, r, g):
    return _rowwise(name, _ln_bwd_rows, [dy, r], [_sds(dy.shape), _sds(dy.shape, BF16)], consts=[g],
                    reds=[_sds((1, D_MODEL)), _sds((1, D_MODEL))])


def _head_sum_matrix():
    i = lax.broadcasted_iota(jnp.int32, (A_WIDTH, LANES), 0) // A_HEAD_DIM
    return (i == _slot_head(lax.broadcasted_iota(jnp.int32, (A_WIDTH, LANES), 1))).astype(BF16)


def _mix_bwd(dmixed, a_out, b_out, ols, a_g, b_g):
    def fn(dm, a, b, ol1, ol2, ol3, ag, bg, sum_mat, spread):
        da, dga = _rms_bwd(a, ag, dm[:, :A_WIDTH])
        db, dgb = _rms_bwd(b, bg, dm[:, A_WIDTH:])
        t = da * a
        t_hi = t.astype(BF16)
        t_lo = (t - t_hi.astype(F32)).astype(BF16)
        tsum = _dot(t_hi, sum_mat, _NN) + _dot(t_lo, sum_mat, _NN)
        tb = db * b
        delta_b = jnp.concatenate(
            [jnp.broadcast_to(jnp.sum(tb[:, h * V_DIM:(h + 1) * V_DIM], axis=-1, keepdims=True), (tb.shape[0], V_DIM))
             for h in range(MLA_HEADS)], axis=1)
        dods = [jnp.concatenate([_pack_halves(_spread_slots(w, spread) * da), w * tsum], axis=1)
                for w in _pattern_weights(ol1, ol2, ol3)]
        return (*dods, db, delta_b, dga, dgb)

    n_rows = a_out.shape[0]
    wide = (n_rows, A_WIDTH)
    dils = [dil for _, dil in PATTERNS]
    return _rowwise("mix_bwd", fn, [dmixed, a_out, b_out, *ols],
                    [_sds((n_rows // d, d * OL_WIDTH)) for d in dils] + [_sds(wide, BF16), _sds(wide)],
                    consts=[a_g, b_g, _head_sum_matrix(), _slot_spread_matrix()],
                    reds=[_sds((1, A_WIDTH)), _sds((1, MLA_WIDTH))], ts=256,
                    in_dils=[1, 1, 1] + dils, out_dils=dils + [1, 1])


def _dil_bwd(qkv_view, ol_view, dod_view, dil, comm=None):
    nb = qkv_view.shape[0] // SPAN
    group = min(dil, DIL_GROUP)
    grid = (dil // group, nb + 1)
    c_ins, c_in_specs, c_outs, c_out_specs, begin, end = _hosted(comm, grid)
    n_ci, n_co = len(c_ins), len(c_outs)

    def body(cur_ref, prev_ref, ol_ref, dod_ref, *refs):
        ci_refs, out_ref, co_refs = refs[:n_ci], refs[n_ci], refs[n_ci + 1:n_ci + 1 + n_co]
        carry, sems = refs[n_ci + 1 + n_co], refs[n_ci + 2 + n_co:]
        m = pl.program_id(1)
        begin(ci_refs, co_refs, sems)

        def write_packed(g, pairs):
            lo = DQKV_WIDTH * g
            for p, (dq_p, dk_p, _) in enumerate(pairs):
                out_ref[:, lo + p * LANES:lo + (p + 1) * LANES] = _pack2(dq_p, dk_p)
            for p in range(2):
                out_ref[:, lo + A_WIDTH + p * LANES:lo + A_WIDTH + (p + 1) * LANES] = _pack2(pairs[p][2], pairs[p + 2][2])

        @pl.when(m == 0)
        def _():
            carry[...] = jnp.zeros(carry.shape, F32)

        @pl.when(m < nb)
        def _():
            mask, even = _band_mask(m, 2), _even_lanes(SPAN)
            for g in range(group):
                done = []
                for p in range(A_HEADS // 2):
                    q_sl, k_sl, v_sl = (slice((3 * g + t) * A_WIDTH + p * LANES, (3 * g + t) * A_WIDTH + (p + 1) * LANES)
                                        for t in range(3))
                    do_sl = slice(OL_WIDTH * g + (p % 2) * LANES, OL_WIDTH * g + (p % 2 + 1) * LANES)
                    s_e, s_o = (OL_WIDTH * g + O_WORDS + _slot_lane(2 * p + t) for t in range(2))
                    q2, do2 = cur_ref[:, q_sl], _unpack2(dod_ref[:, do_sl])[p // 2].astype(BF16)
                    zero = jnp.zeros_like(q2)
                    qcat = jnp.concatenate([jnp.where(even, q2, zero), jnp.where(even, zero, q2)], axis=0)
                    docat = jnp.concatenate([jnp.where(even, do2, zero), jnp.where(even, zero, do2)], axis=0)
                    kcat = jnp.concatenate([prev_ref[:, k_sl], cur_ref[:, k_sl]], axis=0)
                    vcat = jnp.concatenate([prev_ref[:, v_sl], cur_ref[:, v_sl]], axis=0)
                    lse_c = jnp.concatenate([ol_ref[:, s_e:s_e + 1], ol_ref[:, s_o:s_o + 1]], axis=0)
                    dl_c = jnp.concatenate([dod_ref[:, s_e:s_e + 1], dod_ref[:, s_o:s_o + 1]], axis=0)
                    pr = jnp.exp(jnp.where(mask, _dot(qcat, kcat, _NT), NEG) - lse_c)
                    ds = (pr * (_dot(docat, vcat, _NT) - dl_c)).astype(BF16)
                    dq = _dot(ds, kcat, _NN)
                    dk2 = _dot(ds, qcat, _TN)
                    dv2 = _dot(pr.astype(BF16), docat, _TN)
                    done.append((carry[:, q_sl], carry[:, k_sl] + dk2[:SPAN], carry[:, v_sl] + dv2[:SPAN]))
                    carry[:, q_sl] = jnp.where(even, dq[:SPAN], dq[SPAN:])
                    carry[:, k_sl] = dk2[SPAN:]
                    carry[:, v_sl] = dv2[SPAN:]
                write_packed(g, done)

        @pl.when(m == nb)
        def _():
            for g in range(group):
                write_packed(g, [tuple(carry[:, (3 * g + t) * A_WIDTH + p * LANES:(3 * g + t) * A_WIDTH + (p + 1) * LANES]
                                       for t in range(3)) for p in range(A_HEADS // 2)])

        end(ci_refs, co_refs, sems)

    last = nb - 1

    def cur(width):
        return pl.BlockSpec((SPAN, group * width), lambda r, m: (jnp.minimum(m, last), r))

    def prev(width):
        return pl.BlockSpec((SPAN, group * width), lambda r, m: (jnp.clip(m - 1, 0, last), r))

    return pl.pallas_call(
        body, name=f"dil_bwd_d{dil}", grid=grid,
        in_specs=[cur(3 * A_WIDTH), prev(3 * A_WIDTH), cur(OL_WIDTH), cur(OL_WIDTH)] + c_in_specs,
        out_specs=[prev(DQKV_WIDTH)] + c_out_specs,
        out_shape=[_sds((qkv_view.shape[0], dil * DQKV_WIDTH))] + c_outs,
        scratch_shapes=[pltpu.VMEM((SPAN, group * 3 * A_WIDTH), F32)] + (comm.sem_scratch() if comm else []),
        compiler_params=_params(("arbitrary", "arbitrary") if comm else ("parallel", "arbitrary")),
    )(qkv_view, qkv_view, ol_view, dod_view, *c_ins)


def _mla_bwd(qf, kf, v, lse, do, delta, comm=None):
    n_rows = v.shape[0]
    tq, tk = min(MLA_BWD_TILES[0], n_rows), min(MLA_BWD_TILES[1], n_rows)
    nq, nk = n_rows // tq, n_rows // tk

    def first_q(j):
        return jnp.right_shift(j * tk, int(math.log2(tq)))

    grid = (MLA_HEADS, nk, nq)
    c_ins, c_in_specs, c_outs, c_out_specs, begin, end = _hosted(comm, grid)
    n_ci, n_co = len(c_ins), len(c_outs)

    def body(q_ref, k_ref, v_ref, lse_ref, do_ref, dl_ref, *refs):
        ci_refs, (dq_ref, dk_ref, dv_ref), co_refs = refs[:n_ci], refs[n_ci:n_ci + 3], refs[n_ci + 3:n_ci + 3 + n_co]
        dk_sc, dv_sc = refs[n_ci + 3 + n_co:n_ci + 5 + n_co]
        sems = refs[n_ci + 5 + n_co:]
        j, i = pl.program_id(1), pl.program_id(2)
        begin(ci_refs, co_refs, sems)

        @pl.when((j == 0) & (i == 0))
        def _():
            dq_ref[...] = jnp.zeros(dq_ref.shape, F32)

        @pl.when(i == first_q(j))
        def _():
            dk_sc[...] = jnp.zeros(dk_sc.shape, F32)
            dv_sc[...] = jnp.zeros(dv_sc.shape, F32)

        def step(masked):
            q, k, dob = q_ref[...], k_ref[...], do_ref[...]
            s = _dot(q, k, _NT) * MLA_SCALE
            if masked:
                row = lax.broadcasted_iota(jnp.int32, (tq, tk), 0) + i * tq
                col = lax.broadcasted_iota(jnp.int32, (tq, tk), 1) + j * tk
                s = jnp.where(col <= row, s, NEG)
            p = jnp.exp(s - lse_ref[:, :1])
            dp = _dot(dob, v_ref[...], _NT)
            ds = (p * (dp - dl_ref[:, :1]) * MLA_SCALE).astype(BF16)
            dv_sc[...] += _dot(p.astype(BF16), dob, _TN)
            dk_sc[...] += _dot(ds, q, _TN)
            rows = pl.ds(pl.multiple_of(i * tq, tq), tq)
            dq_ref[rows, :] += _dot(ds, k, _NN)

        active = i >= first_q(j)
        crosses = (j + 1) * tk - 1 > i * tq

        @pl.when(active & jnp.logical_not(crosses))
        def _():
            step(False)

        @pl.when(active & crosses)
        def _():
            step(True)

        @pl.when(i == nq - 1)
        def _():
            dk_ref[...] = dk_sc[...]
            dv_ref[...] = dv_sc[...]

        end(ci_refs, co_refs, sems)

    qrow = lambda h, j, i: (jnp.maximum(i, first_q(j)), h)
    return pl.pallas_call(
        body, name="mla_bwd", grid=grid,
        in_specs=[pl.BlockSpec((None, tq, QK_PAD), lambda h, j, i: (h, jnp.maximum(i, first_q(j)), 0)),
                  pl.BlockSpec((None, tk, QK_PAD), lambda h, j, i: (h, j, 0)),
                  pl.BlockSpec((tk, V_DIM), lambda h, j, i: (j, h)),
                  pl.BlockSpec((tq, V_DIM), qrow), pl.BlockSpec((tq, V_DIM), qrow), pl.BlockSpec((tq, V_DIM), qrow)]
        + c_in_specs,
        out_specs=[pl.BlockSpec((None, n_rows, QK_PAD), lambda h, j, i: (h, 0, 0)),
                   pl.BlockSpec((None, tk, QK_PAD), lambda h, j, i: (h, j, 0)),
                   pl.BlockSpec((tk, V_DIM), lambda h, j, i: (j, h))] + c_out_specs,
        out_shape=[_sds((MLA_HEADS, n_rows, QK_PAD)), _sds((MLA_HEADS, n_rows, QK_PAD)), _sds((n_rows, MLA_WIDTH))] + c_outs,
        scratch_shapes=[pltpu.VMEM((tk, QK_PAD), F32), pltpu.VMEM((tk, V_DIM), F32)] + (comm.sem_scratch() if comm else []),
        compiler_params=_params(("arbitrary",) * 3 if comm else ("parallel", "arbitrary", "arbitrary")),
    )(qf, kf, v, lse, do, delta, *c_ins)


def _mla_bwd_prep(dqf, dkf, dv, cos, sin):
    def fn(dqf_v, dkf_v, dv_v, cos_v, sin_v):
        dq = jnp.concatenate([dqf_v[h][:, :QK_NOPE] for h in range(MLA_HEADS)]
                             + [_rope_t(dqf_v[h][:, QK_NOPE:], cos_v[:, :LANES], sin_v[:, :LANES]) for h in range(MLA_HEADS)], axis=1)
        dkv = jnp.concatenate([dkf_v[h][:, :QK_NOPE] for h in range(MLA_HEADS)] + [dv_v], axis=1)
        dkpe = dkf_v[0][:, QK_NOPE:] + dkf_v[1][:, QK_NOPE:] + dkf_v[2][:, QK_NOPE:] + dkf_v[3][:, QK_NOPE:]
        return dq, dkv, dkpe

    n_rows = dv.shape[0]
    return _rowwise("mla_bwd_prep", fn, [dqf, dkf, dv, cos, sin],
                    [_sds((n_rows, UQ_PAD), BF16), _sds((n_rows, 1024), BF16), _sds((n_rows, LANES))], ts=256)


def _assemble_dh(dqkvs, dcqn, cq, dckvn, ckv, dkpe, cos, sin, gq, gkv, comm=None):
    def fn(g1, g2, g3, dcqn_v, cq_v, dckvn_v, ckv_v, dkpe_v, cos_v, sin_v, gq_v, gkv_v):
        dq = dk = dv = 0.0
        for packed in (g1, g2, g3):
            dq_p, dk_p = _unpack2(packed[:, :A_WIDTH])
            dv_p = jnp.concatenate(_unpack2(packed[:, A_WIDTH:]), axis=1)
            dq, dk, dv = dq + dq_p, dk + dk_p, dv + dv_p
        dqa = _rope_t(dq, cos_v, sin_v) * A_SCALE
        dka = _rope_t(dk, cos_v, sin_v)
        dcq, dgq = _rms_bwd(cq_v, gq_v, dcqn_v)
        dckv, dgkv = _rms_bwd(ckv_v, gkv_v, dckvn_v)
        dkr = _rope_t(dkpe_v, cos_v[:, :LANES], sin_v[:, :LANES])
        return jnp.concatenate([dqa, dka, dv, dcq, dkr, dckv], axis=1), dgq, dgkv

    n_rows = cq.shape[0]
    return _rowwise("assemble_dh", fn, [*dqkvs, dcqn, cq, dckvn, ckv, dkpe, cos, sin],
                    [_sds((n_rows, IN_PAD), BF16)], consts=[gq, gkv],
                    reds=[_sds((1, Q_LORA)), _sds((1, KV_LORA))], ts=256,
                    in_dils=[dil for _, dil in PATTERNS] + [1] * 7, comm=comm)


def _layer_fwd(x_f32, x_bf, w, sm, cos, sin, behind=(None, None), rest=None):
    *qkv_views, cq, kpe, ckv = _in_proj(x_bf, w["w_in"], cos, sin)
    ol_views, brought = [], []
    for i, (view, (_, dil)) in enumerate(zip(qkv_views, PATTERNS)):
        ol, *got = _dil_fwd(view, dil, comm=rest[0][i] if rest else None)
        ol_views.append(ol)
        brought.append(got)
    if rest:
        w = {**w, **rest[1](brought)}
    cqn, qf, ckvn, kf, v = _mla_prep(cq, ckv, kpe, cos, sin, sm["q_a_norm"], w["w_uq"], sm["kv_a_norm"], w["w_ukv"])
    b_out, b_lse, *behind_attn = _mla_fwd(qf, kf, v, comm=behind[0])
    a_out, mixed = _mix_fwd(ol_views, b_out, sm["a_out_norm"], sm["b_out_norm"])
    x1, x1_bf, r1 = _mm_res_ln("wo_ln1", mixed, w["w_o"], x_f32, sm["ln1_g"], sm["ln1_b"], tm=1024)
    f, *behind_ff1 = _mm_relu2(x1_bf, w["w_ff1"], comm=behind[1])
    x2, x2_bf, r2 = _mm_res_ln("ff2_ln2", f, w["w_ff2"], x1, sm["ln2_g"], sm["ln2_b"], tm=512)
    saved = dict(w=w, x_bf=x_bf, qkv_views=qkv_views, ol_views=ol_views, cq=cq, ckv=ckv, cqn=cqn, qf=qf, ckvn=ckvn,
                 kf=kf, v=v, b_out=b_out, b_lse=b_lse, a_out=a_out, mixed=mixed, x1_bf=x1_bf, r1=r1, f=f, r2=r2)
    return x2, x2_bf, saved, (behind_attn, behind_ff1)


def _layer_bwd(dx2, w, sm, sv, cos, sin, reduce_of, pending=None):
    dr2, dr2_bf, dg2, db2 = _ln_bwd("ln2_bwd", dx2, sv["r2"], sm["ln2_g"])
    (dw_ff2,) = _mm("dw_ff2", sv["f"], dr2_bf, (F32,), ta=True, tm=1024, tn=1024, tk=dx2.shape[0])
    du, *swapped = _mm("d_u", dr2_bf, w["w_ff2"], (BF16,), tb=True, tm=1024, tn=2048, tk=D_MODEL,
                       epilogue=lambda acc, f: (acc * (2.0 * jnp.sqrt(f.astype(F32))),), extras=[sv["f"]],
                       comm=_swap_comm(pending.pieces) if pending else None)
    if pending:
        pending.after_swap(swapped)
    (dw_ff1,) = _mm("dw_ff1", sv["x1_bf"], du, (F32,), ta=True, tm=1024, tn=1024, tk=dx2.shape[0], out_pieces=True)
    (dy1,) = _mm("d_x1", du, w["w_ff1"], (F32,), tb=True, tm=512, tn=1024, tk=D_FF,
                 epilogue=lambda acc, d: (acc + ALPHA * d,), extras=[dr2], b_pieces=True)
    dr1, dr1_bf, dg1, db1 = _ln_bwd("ln1_bwd", dy1, sv["r1"], sm["ln1_g"])
    (dw_o,) = _mm("dw_o", sv["mixed"], dr1_bf, (F32,), ta=True, tm=1024, tn=1024, tk=2048)
    early = reduce_of(dict(w_o=dw_o, w_ff1=dw_ff1, w_ff2=dw_ff2))
    (dmixed,) = _mm("d_mixed", dr1_bf, w["w_o"], (F32,), tb=True, tm=1024, tn=1024, tk=D_MODEL)
    dod1, dod2, dod3, do_b, delta_b, dga, dgb = _mix_bwd(
        dmixed, sv["a_out"], sv["b_out"], sv["ol_views"], sm["a_out_norm"], sm["b_out_norm"])
    dqkvs = []
    for i, ((_, dil), qkv_view, ol_view, dod_view) in enumerate(zip(PATTERNS, sv["qkv_views"], sv["ol_views"], (dod1, dod2, dod3))):
        swap_here = early is not None and i == 0
        dqkv, *swapped = _dil_bwd(qkv_view, ol_view, dod_view, dil, comm=_swap_comm(early.pieces) if swap_here else None)
        dqkvs.append(dqkv)
        if swap_here:
            early.after_swap(swapped)
    to_scatter = (early.s16 if early else []) + (pending.s16 if pending else [])
    dqf, dkf, dv_b, *scattered = _mla_bwd(sv["qf"], sv["kf"], sv["v"], sv["b_lse"], do_b, delta_b,
                                          comm=_scatter_comm(to_scatter) if to_scatter else None)
    if early:
        early.after_scatter(scattered[:len(early.s16)])
    if pending:
        pending.after_scatter(scattered[len(scattered) - len(pending.s16):])
    dq_b, dkv_b, dkpe = _mla_bwd_prep(dqf, dkf, dv_b, cos, sin)
    (dw_uq,) = _mm("dw_uq", sv["cqn"], dq_b, (F32,), ta=True, tm=Q_LORA, tn=1024, tk=2048)
    (dcqn,) = _mm("d_cqn", dq_b, w["w_uq"], (F32,), tb=True, tm=1024, tn=Q_LORA, tk=UQ_PAD)
    (dw_ukv,) = _mm("dw_ukv", sv["ckvn"], dkv_b, (F32,), ta=True, tm=KV_LORA, tn=1024, tk=2048)
    (dckvn,) = _mm("d_ckvn", dkv_b, w["w_ukv"], (F32,), tb=True, tm=1024, tn=KV_LORA, tk=1024)
    to_share = (early.mine if early else []) + (pending.mine if pending else [])
    dh, dgq, dgkv, *shared = _assemble_dh(dqkvs, dcqn, sv["cq"], dckvn, sv["ckv"], dkpe, cos, sin,
                                          sm["q_a_norm"], sm["kv_a_norm"], comm=_share_comm(to_share) if to_share else None)
    if early:
        early.after_share(shared[:len(early.mine)])
    if pending:
        pending.after_share(shared[len(shared) - len(pending.mine):])
    (dw_in,) = _mm("dw_in", sv["x_bf"], dh, (F32,), ta=True, tm=1024, tn=1024, tk=dx2.shape[0])
    (dx,) = _mm("d_x", dh, w["w_in"], (F32,), tb=True, tm=1024, tn=1024, tk=2048,
                epilogue=lambda acc, d: (acc + ALPHA * d,), extras=[dr1])
    late = reduce_of(dict(w_in=dw_in, w_uq=dw_uq, w_ukv=dw_ukv))
    dsm = dict(q_a_norm=dgq, kv_a_norm=dgkv, a_out_norm=dga, b_out_norm=dgb, ln1_g=dg1, ln1_b=db1, ln2_g=dg2, ln2_b=db2)
    return dx, dsm, late


def _pad_w_in(w):
    return jnp.concatenate([w[:, :1792], w[:, 1920:1984], jnp.zeros((w.shape[0], 64), w.dtype), w[:, 1792:1920]], axis=1)


def _unpad_w_in(w):
    return jnp.concatenate([w[:, :1792], w[:, 1920:2048], w[:, 1792:1856]], axis=1)


def _rope_tables(n_rows):
    half = A_HEAD_DIM // 2
    inv_freq = ROPE_THETA ** (-jnp.arange(half, dtype=F32) / half)
    ang = jnp.arange(n_rows, dtype=F32)[:, None] * inv_freq[None, :]
    cos = jnp.tile(jnp.cos(ang), (1, 2 * A_HEADS))
    sin = jnp.tile(jnp.concatenate([-jnp.sin(ang), jnp.sin(ang)], axis=1), (1, A_HEADS))
    return cos, sin


W_NAMES = ("w_in", "w_uq", "w_ukv", "w_o", "w_ff1", "w_ff2")
PIECE = dict(w_in=(512, 512), w_uq=(128, 256), w_ukv=(64, 256), w_o=(128, 1024), w_ff1=(512, 1024), w_ff2=(512, 1024))
SHARD_COLS = dict(w_in=496, w_uq=192, w_ukv=256, w_o=1024, w_ff1=1024, w_ff2=1024)


def _col_pieces(t, pad_to):
    rows, _, cols = t.shape
    t = jnp.pad(t, ((0, 0), (0, 0), (0, pad_to - cols)))
    return t.reshape(2, rows // 2, N_CHIPS, pad_to).transpose(2, 0, 1, 3)


def _grad_pieces(dws):
    def heads(t, rows):
        return jnp.concatenate([t[:, :512].reshape(rows, MLA_HEADS, LANES), t[:, 512:].reshape(rows, MLA_HEADS, LANES)], axis=2)

    make = dict(
        w_in=lambda t: _col_pieces(_unpad_w_in(t).reshape(D_MODEL, N_CHIPS, SHARD_COLS["w_in"]), 512),
        w_uq=lambda t: _col_pieces(heads(t, Q_LORA), 256),
        w_ukv=lambda t: _col_pieces(heads(t, KV_LORA), 256),
        w_o=lambda t: t.reshape(N_CHIPS, 2, 128, D_MODEL),
        w_ff1=lambda t: t,
        w_ff2=lambda t: t.reshape(N_CHIPS, 2, 512, D_MODEL))
    return {n: make[n](t) for n, t in dws.items()}


def _weights_from_pieces(p):
    def cols(t):
        return t.transpose(1, 2, 0, 3).reshape(2 * t.shape[2], N_CHIPS, t.shape[3])

    def heads(t, rows):
        return jnp.concatenate([t[:, :, :LANES].reshape(rows, 512), t[:, :, LANES:].reshape(rows, 512)], axis=1)

    make = dict(
        w_in=lambda t: _pad_w_in(cols(t)[:, :, :SHARD_COLS["w_in"]].reshape(D_MODEL, IN_COLS)),
        w_uq=lambda t: heads(cols(t), Q_LORA),
        w_ukv=lambda t: heads(cols(t), KV_LORA),
        w_o=lambda t: t.reshape(D_MODEL, D_MODEL),
        w_ff1=lambda t: t,
        w_ff2=lambda t: t.reshape(D_FF, D_MODEL))
    return {n: make[n](t) for n, t in p.items()}


def _my_piece(name, shard, half):
    rows, cols = shard.shape[0] // 2, shard.shape[1]
    t = lax.dynamic_slice_in_dim(shard, half * rows, rows, axis=0).astype(BF16)
    return jnp.pad(t, ((0, 0), (0, PIECE[name][1] - cols)))


MESH = pl.DeviceIdType.MESH
ANY = pl.BlockSpec(memory_space=pl.ANY)


class _Comm:
    def __init__(self, ins, out_shapes, n_sems, start, finish, middle=None):
        self.ins, self.out_shapes, self.n_sems, self.start, self.finish = list(ins), list(out_shapes), n_sems, start, finish
        self.middle = middle

    def sem_scratch(self):
        return [pltpu.SemaphoreType.DMA((self.n_sems,)), pltpu.SemaphoreType.DMA((self.n_sems,))]


def _comm_call(name, comm):
    n_in, n_out = len(comm.ins), len(comm.out_shapes)

    def body(*refs):
        ins, outs, sems = refs[:n_in], refs[n_in:n_in + n_out], refs[n_in + n_out:]
        comm.start(ins, outs, *sems)
        if comm.middle:
            comm.middle(ins, outs, *sems)
        comm.finish(ins, outs, *sems)

    return pl.pallas_call(body, name=name, out_shape=comm.out_shapes, in_specs=[ANY] * n_in, out_specs=[ANY] * n_out,
                          scratch_shapes=comm.sem_scratch())(*comm.ins)


def _all_gather_comm(blocks):
    n = len(blocks)

    def plan(x_refs, out_refs, send_sems, recv_sems):
        x, y, c = lax.axis_index("x"), lax.axis_index("y"), lax.axis_index("c")
        me, sibling = (x, y, c), (x, y, 1 - c)
        chips = [(1 - x, y), (x, 1 - y), (1 - x, 1 - y)]

        def copy(t, k, blk, to, src=None):
            px, py, pc = blk
            slot = out_refs[t].at[4 * px + 2 * py + pc]
            return pltpu.make_async_remote_copy(
                src_ref=slot if src is None else src, dst_ref=slot,
                send_sem=send_sems.at[7 * t + k], recv_sem=recv_sems.at[7 * t + k], device_id=to, device_id_type=MESH)

        first = []
        for t in range(n):
            first.append(copy(t, 0, me, sibling, src=x_refs[t]))
            first += [copy(t, 1 + j, me, (*chip, c), src=x_refs[t]) for j, chip in enumerate(chips)]
        return me, sibling, chips, c, copy, first

    def start(x_refs, out_refs, send_sems, recv_sems):
        for cp in plan(x_refs, out_refs, send_sems, recv_sems)[-1]:
            cp.start()

    def pass_on(x_refs, out_refs, send_sems, recv_sems):
        me, sibling, chips, c, copy, _ = plan(x_refs, out_refs, send_sems, recv_sems)
        for j, chip in enumerate(chips):
            for t in range(n):
                copy(t, 1 + j, (*chip, c), me).wait_recv()
                copy(t, 4 + j, (*chip, c), sibling).start()

    def finish(x_refs, out_refs, send_sems, recv_sems):
        me, sibling, chips, c, copy, sent = plan(x_refs, out_refs, send_sems, recv_sems)
        for t in range(n):
            copy(t, 0, sibling, me).wait_recv()
        for j, chip in enumerate(chips):
            for t in range(n):
                copy(t, 4 + j, (*chip, 1 - c), me).wait_recv()
                sent.append(copy(t, 4 + j, (*chip, c), sibling))
        for cp in sent:
            cp.wait_send()

    return _Comm(blocks, [_sds((N_DEV,) + b.shape, b.dtype) for b in blocks], 7 * n, start, finish, middle=pass_on)


def _simple_comm(ins, out_shapes, n_sems, copies):
    def start(in_refs, out_refs, send_sems, recv_sems):
        for cp in copies(in_refs, out_refs, send_sems, recv_sems):
            cp.start()

    def finish(in_refs, out_refs, send_sems, recv_sems):
        for cp in copies(in_refs, out_refs, send_sems, recv_sems):
            cp.wait()

    return _Comm(ins, out_shapes, n_sems, start, finish)


def _swap_comm(gs):
    def copies(g_refs, got_refs, send_sems, recv_sems):
        c = lax.axis_index("c")
        sibling = (lax.axis_index("x"), lax.axis_index("y"), 1 - c)
        return [pltpu.make_async_remote_copy(src_ref=g_refs[t].at[k, 1 - c], dst_ref=got_refs[t].at[k],
                                             send_sem=send_sems.at[N_CHIPS * t + k], recv_sem=recv_sems.at[N_CHIPS * t + k],
                                             device_id=sibling, device_id_type=MESH)
                for t in range(len(gs)) for k in range(N_CHIPS)]

    return _simple_comm(gs, [_sds((N_CHIPS,) + g.shape[2:], g.dtype) for g in gs], N_CHIPS * len(gs), copies)


def _scatter_comm(hs):
    def copies(h_refs, got_refs, send_sems, recv_sems):
        x, y, c = lax.axis_index("x"), lax.axis_index("y"), lax.axis_index("c")
        chips = [(1 - x, y), (x, 1 - y), (1 - x, 1 - y)]
        return [pltpu.make_async_remote_copy(src_ref=h_refs[t].at[2 * px + py], dst_ref=got_refs[t].at[rel],
                                             send_sem=send_sems.at[3 * t + rel], recv_sem=recv_sems.at[3 * t + rel],
                                             device_id=(px, py, c), device_id_type=MESH)
                for t in range(len(hs)) for rel, (px, py) in enumerate(chips)]

    return _simple_comm(hs, [_sds((3,) + h.shape[1:], h.dtype) for h in hs], 3 * len(hs), copies)


def _share_comm(rs):
    def copies(r_refs, got_refs, send_sems, recv_sems):
        sibling = (lax.axis_index("x"), lax.axis_index("y"), 1 - lax.axis_index("c"))
        return [pltpu.make_async_remote_copy(src_ref=r_refs[t], dst_ref=got_refs[t], send_sem=send_sems.at[t],
                                             recv_sem=recv_sems.at[t], device_id=sibling, device_id_type=MESH)
                for t in range(len(rs))]

    return _simple_comm(rs, [_sds(r.shape, r.dtype) for r in rs], len(rs), copies)


SUM_STEPS = 4


def _pair_sums(name, gs, gots, where):
    n = len(gs)
    dims = [(g.shape[2] // SUM_STEPS, g.shape[3]) for g in gs]

    def body(where_ref, *refs):
        own = pl.program_id(1) == where_ref[1]
        for t in range(n):
            s = refs[t][...] + refs[n + t][...]
            refs[3 * n + t][...] = s.astype(BF16)

            @pl.when(own)
            def _():
                refs[2 * n + t][...] = s

    def part(tr, pc):
        return pl.BlockSpec((None, tr, pc), lambda i, k, where_ref: (k, i, 0))

    def kept(tr, pc):
        return pl.BlockSpec((None, None, tr, pc), lambda i, k, where_ref: (k, where_ref[0], i, 0))

    outs = pl.pallas_call(
        body, name=name,
        out_shape=[_sds(g.shape[2:]) for g in gs] + [_sds((N_CHIPS,) + g.shape[2:], BF16) for g in gs],
        grid_spec=pltpu.PrefetchScalarGridSpec(
            num_scalar_prefetch=1, grid=(SUM_STEPS, N_CHIPS),
            in_specs=[kept(*d) for d in dims] + [part(*d) for d in dims],
            out_specs=[pl.BlockSpec(d, lambda i, k, where_ref: (i, 0)) for d in dims] + [part(*d) for d in dims]),
        compiler_params=_params(("parallel", "arbitrary")),
    )(where, *gs, *gots)
    return list(outs[:n]), list(outs[n:])


def _chips_sums(name, owns, gots):
    n = len(owns)
    dims = [(h.shape[0] // SUM_STEPS, h.shape[1]) for h in owns]

    def body(*refs):
        for t in range(n):
            got_ref = refs[n + t]
            refs[2 * n + t][...] = refs[t][...] + got_ref[0] + got_ref[1] + got_ref[2]

    row = [pl.BlockSpec((tr, pc), lambda i: (i, 0)) for tr, pc in dims]
    return pl.pallas_call(
        body, name=name, out_shape=[_sds(h.shape) for h in owns], grid=(SUM_STEPS,),
        in_specs=row + [pl.BlockSpec((3, tr, pc), lambda i: (0, i, 0)) for tr, pc in dims], out_specs=row,
        compiler_params=_params(("parallel",)),
    )(*owns, *gots)


class _Reduce:
    def __init__(self, tag, dws, c, chip):
        gp = _grad_pieces(dws)
        self.tag, self.c, self.names, self.pieces = tag, c, list(gp), list(gp.values())
        self.where = jnp.stack([c, chip]).astype(jnp.int32)
        self.shards = None

    def after_swap(self, gots):
        self.s32, self.s16 = _pair_sums(f"rs_pair_sum_{self.tag}", self.pieces, gots, self.where)

    def after_scatter(self, parts):
        self.mine = list(_chips_sums(f"rs_chips_sum_{self.tag}", self.s32, parts))

    def after_share(self, theirs):
        self.shards = {}
        for n, a, b in zip(self.names, self.mine, theirs):
            both = jnp.where(self.c == 0, jnp.concatenate([a, b], axis=0), jnp.concatenate([b, a], axis=0))
            self.shards[n] = both[:, :SHARD_COLS[n]]

    def run_alone(self):
        self.after_swap(_comm_call(f"rs_pair_{self.tag}", _swap_comm(self.pieces)))
        self.after_scatter(_comm_call(f"rs_chips_{self.tag}", _scatter_comm(self.s16)))
        self.after_share(_comm_call(f"rs_share_{self.tag}", _share_comm(self.mine)))


class _LayerWeights:
    def __init__(self, shards, c, dev):
        self.shards, self.c, self.dev = shards, c, dev

    def comm(self, l, names=W_NAMES):
        return _all_gather_comm([_my_piece(n, self.shards[l][n], self.c) for n in names])

    def take(self, l, gathered, names=W_NAMES):
        pieces = {}
        for n, g in zip(names, gathered):
            mine = _my_piece(n, self.shards[l][n], self.c)
            pieces[n] = lax.dynamic_update_index_in_dim(g, mine, self.dev, 0).reshape((N_CHIPS, 2) + mine.shape)
        return _weights_from_pieces(pieces)

    def first(self):
        now, later = ("w_in", "w_uq", "w_ukv"), ("w_ff1", "w_ff2", "w_o")
        weights = self.take(0, _comm_call("ag_l0", self.comm(0, now)), now)
        comms = [self.comm(0, (n,)) for n in later]

        def take_rest(results):
            rest = {}
            for n, gathered in zip(later, results):
                rest.update(self.take(0, gathered, (n,)))
            return rest

        return weights, (comms, take_rest)


def _all_reduce_small(vec):
    rows, lanes = vec.shape

    def gather_body(x_ref, out_ref, send_sems, recv_sems):
        x, y, c = lax.axis_index("x"), lax.axis_index("y"), lax.axis_index("c")
        me, sibling = (x, y, c), (x, y, 1 - c)
        chips = [(1 - x, y), (x, 1 - y), (1 - x, 1 - y)]

        def slot(px, py, pc):
            return out_ref.at[4 * px + 2 * py + pc]

        def copy(k, blk, to, src=None):
            return pltpu.make_async_remote_copy(
                src_ref=slot(*blk) if src is None else src, dst_ref=slot(*blk),
                send_sem=send_sems.at[k], recv_sem=recv_sems.at[k], device_id=to, device_id_type=MESH)

        out_ref[4 * x + 2 * y + c] = x_ref[...]
        first = [copy(0, me, sibling, src=x_ref)]
        first += [copy(1 + j, me, (*chip, c), src=x_ref) for j, chip in enumerate(chips)]
        for cp in first:
            cp.start()
        passed = [copy(4 + j, (*chip, c), sibling) for j, chip in enumerate(chips)]
        for j, chip in enumerate(chips):
            copy(1 + j, (*chip, c), me).wait_recv()
            passed[j].start()
        copy(0, sibling, me).wait_recv()
        for j, chip in enumerate(chips):
            copy(4 + j, (*chip, 1 - c), me).wait_recv()
        for cp in first + passed:
            cp.wait_send()

    vmem = pl.BlockSpec(memory_space=pltpu.VMEM)
    allv = pl.pallas_call(
        gather_body, name="small_all_gather", out_shape=_sds((N_DEV, rows, lanes)),
        in_specs=[vmem], out_specs=vmem,
        scratch_shapes=[pltpu.SemaphoreType.DMA((7,)), pltpu.SemaphoreType.DMA((7,))],
    )(vec)

    def sum_body(a_ref, o_ref):
        acc = a_ref[0]
        for d in range(1, N_DEV):
            acc = acc + a_ref[d]
        o_ref[...] = acc

    return pl.pallas_call(sum_body, name="small_sum", out_shape=_sds((rows, lanes)), in_specs=[vmem], out_specs=vmem)(allv)


def _adamw(name, w, g, m, v, ts=512, comm=None):
    def fn(w_v, g_v, m_v, v_v):
        m_n = ADAM_B1 * m_v + (1.0 - ADAM_B1) * g_v
        v_n = ADAM_B2 * v_v + (1.0 - ADAM_B2) * (g_v * g_v)
        m_hat = m_n / (1.0 - ADAM_B1 ** ADAM_STEP)
        v_hat = v_n / (1.0 - ADAM_B2 ** ADAM_STEP)
        delta = -ADAM_LR * (m_hat / (jnp.sqrt(v_hat) + ADAM_EPS) + ADAM_WD * w_v)
        return delta, m_n, v_n

    return _rowwise(name, fn, [w, g, m, v], [_sds(w.shape)] * 3, ts=ts, comm=comm)


def _pack_small(per_layer):
    flat = jnp.concatenate([per_layer[l][n].reshape(-1) for l in range(DEPTH) for n in SMALL_NAMES])
    return jnp.pad(flat, (0, SMALL_ROWS * LANES - flat.shape[0])).reshape(SMALL_ROWS, LANES)


def _unpack_small(packed):
    flat = packed.reshape(-1)
    per = sum(SMALL_SIZES)
    out = {}
    for n, size, off in zip(SMALL_NAMES, SMALL_SIZES, [sum(SMALL_SIZES[:i]) for i in range(len(SMALL_SIZES))]):
        out[n] = jnp.stack([flat[l * per + off:l * per + off + size] for l in range(DEPTH)])
    return out


NEXT_WEIGHTS_BEHIND = (("w_ff1", "w_ff2"), ("w_in", "w_uq", "w_ukv", "w_o"))


def _fwd_bwd(x, target, layer_weights, smalls, on_layer_grads):
    depth = len(smalls)
    cos, sin = _rope_tables(x.shape[0])
    h_f32, h_bf = x, x.astype(BF16)
    saved = []
    w, rest = layer_weights.first()
    for l in range(depth):
        more = l + 1 < depth
        behind = [layer_weights.comm(l + 1, names) if more else None for names in NEXT_WEIGHTS_BEHIND]
        h_f32, h_bf, sv, brought = _layer_fwd(h_f32, h_bf, w, smalls[l], cos, sin, behind, rest)
        saved.append(sv)
        if more:
            w, rest = {}, None
            for names, got in zip(NEXT_WEIGHTS_BEHIND, brought):
                w.update(layer_weights.take(l + 1, got, names))
    loss_part, dh = _loss_fn(h_f32, target)
    small_grads, pending = [None] * depth, None
    for l in reversed(range(depth)):
        dh, small_grads[l], pending = _layer_bwd(dh, saved[l]["w"], smalls[l], saved[l], cos, sin,
                                                 lambda dws, l=l: on_layer_grads(l, dws), pending)
    return loss_part, dh, small_grads, pending


def kernel(x, w_in, q_a_norm, kv_a_norm, w_uq, w_ukv, a_out_norm, b_out_norm, w_o, ln1_g, ln1_b, w_ff1, w_ff2, ln2_g, ln2_b, loss_target, m_w_in, m_q_a_norm, m_kv_a_norm, m_w_uq, m_w_ukv, m_a_out_norm, m_b_out_norm, m_w_o, m_ln1_g, m_ln1_b, m_w_ff1, m_w_ff2, m_ln2_g, m_ln2_b, v_w_in, v_q_a_norm, v_kv_a_norm, v_w_uq, v_w_ukv, v_a_out_norm, v_b_out_norm, v_w_o, v_ln1_g, v_ln1_b, v_w_ff1, v_w_ff2, v_ln2_g, v_ln2_b):
    c = lax.axis_index("c")
    chip = 2 * lax.axis_index("x") + lax.axis_index("y")
    big = dict(w_in=w_in, w_uq=w_uq, w_ukv=w_ukv, w_o=w_o, w_ff1=w_ff1, w_ff2=w_ff2)
    big_m = dict(w_in=m_w_in, w_uq=m_w_uq, w_ukv=m_w_ukv, w_o=m_w_o, w_ff1=m_w_ff1, w_ff2=m_w_ff2)
    big_v = dict(w_in=v_w_in, w_uq=v_w_uq, w_ukv=v_w_ukv, w_o=v_w_o, w_ff1=v_w_ff1, w_ff2=v_w_ff2)
    small = dict(q_a_norm=q_a_norm, kv_a_norm=kv_a_norm, a_out_norm=a_out_norm, b_out_norm=b_out_norm,
                 ln1_g=ln1_g, ln1_b=ln1_b, ln2_g=ln2_g, ln2_b=ln2_b)
    small_m = dict(q_a_norm=m_q_a_norm, kv_a_norm=m_kv_a_norm, a_out_norm=m_a_out_norm, b_out_norm=m_b_out_norm,
                   ln1_g=m_ln1_g, ln1_b=m_ln1_b, ln2_g=m_ln2_g, ln2_b=m_ln2_b)
    small_v = dict(q_a_norm=v_q_a_norm, kv_a_norm=v_kv_a_norm, a_out_norm=v_a_out_norm, b_out_norm=v_b_out_norm,
                   ln1_g=v_ln1_g, ln1_b=v_ln1_b, ln2_g=v_ln2_g, ln2_b=v_ln2_b)

    layer_weights = _LayerWeights([{n: big[n][l] for n in W_NAMES} for l in range(DEPTH)], c, 2 * chip + c)
    smalls = [{n: small[n][l][None, :] for n in SMALL_NAMES} for l in range(DEPTH)]
    reductions = [[] for _ in range(DEPTH)]

    def reduce_layer(l, dws):
        reductions[l].append(_Reduce(f"l{l}_{len(reductions[l])}", dws, c, chip))
        return reductions[l][-1]

    loss_part, grad_x, small_grads, last = _fwd_bwd(x[0], loss_target[0], layer_weights, smalls, reduce_layer)
    loss = lax.psum(loss_part[0, 0], ("x", "y", "c"))
    grad_x = grad_x[None]

    g_small_packed = _all_reduce_small(_pack_small(small_grads))
    pack_in = lambda d: _pack_small([{n: d[n][l] for n in SMALL_NAMES} for l in range(DEPTH)])
    sd, sm_, sv_ = _adamw("adamw_small", pack_in(small), g_small_packed, pack_in(small_m), pack_in(small_v), ts=SMALL_ROWS)
    g_small, d_small, m_small, v_small = (_unpack_small(t) for t in (g_small_packed, sd, sm_, sv_))

    last.run_alone()
    g_big, d_big, m_big, v_big = {}, {}, {}, {}
    for n in W_NAMES:
        g = jnp.stack([next(r.shards[n] for r in reductions[l] if n in r.names) for l in range(DEPTH)])
        shape = g.shape
        flat = lambda t: t.reshape(shape[0] * shape[1], shape[2])
        d_, m_, v_ = _adamw(f"adamw_{n}", flat(big[n]), flat(g), flat(big_m[n]), flat(big_v[n]))
        g_big[n], d_big[n], m_big[n], v_big[n] = g, d_.reshape(shape), m_.reshape(shape), v_.reshape(shape)

    order = ("w_in", "q_a_norm", "kv_a_norm", "w_uq", "w_ukv", "a_out_norm", "b_out_norm", "w_o", "ln1_g", "ln1_b",
             "w_ff1", "w_ff2", "ln2_g", "ln2_b")
    pick = lambda bigd, smalld: [bigd[n] if n in bigd else smalld[n] for n in order]
    return (loss, grad_x, *pick(g_big, g_small), *pick(d_big, d_small), *pick(m_big, m_small), *pick(v_big, v_small))
```

```python
import math

import jax
import jax.numpy as jnp
from jax import lax
from jax.experimental import pallas as pl
from jax.experimental.pallas import tpu as pltpu

F32, BF16 = jnp.float32, jnp.bfloat16

D_MODEL = 1024
DEPTH = 4
A_HEAD_DIM = 64
A_HEADS = 8
A_WIDTH = 512
PATTERNS = ((128, 1), (512, 4), (2048, 16))
SPAN = 128
MLA_HEADS = 4
QK_NOPE = 128
QK_ROPE = 64
V_DIM = 128
Q_LORA = 256
KV_LORA = 128
MLA_WIDTH = 512
QK_PAD = 256
IN_COLS = 1984
IN_PAD = 2048
UQ_PAD = 1024
D_FF = 4096
ROPE_THETA = 10000.0
ALPHA = (2.0 * DEPTH) ** 0.25
LN_EPS = 1e-5
RMS_EPS = 1e-6
MLA_SCALE = (QK_NOPE + QK_ROPE) ** -0.5
A_SCALE = A_HEAD_DIM ** -0.5

ADAM_LR, ADAM_B1, ADAM_B2, ADAM_EPS, ADAM_WD, ADAM_STEP = 0.001, 0.9, 0.999, 1e-08, 0.01, 10

VMEM_LIMIT_BYTES = 56 * 1024 * 1024
NEG = -1e30
N_CHIPS, N_DEV = 4, 8
LANES = 128

SMALL_NAMES = ("q_a_norm", "kv_a_norm", "a_out_norm", "b_out_norm", "ln1_g", "ln1_b", "ln2_g", "ln2_b")
SMALL_SIZES = (256, 128, 512, 512, 1024, 1024, 1024, 1024)
SMALL_ROWS = 176


def _params(sem):
    return pltpu.CompilerParams(dimension_semantics=sem, vmem_limit_bytes=VMEM_LIMIT_BYTES)


FAST_STRIDE = 4


def _interleave(parts, scratch):
    d, (n, width) = len(parts), parts[0].shape
    if d > FAST_STRIDE and d % FAST_STRIDE == 0:
        inner = [_interleave(parts[r::FAST_STRIDE], scratch) for r in range(FAST_STRIDE)]
        return _interleave(inner, scratch)
    for r in range(d):
        for s in range(width // LANES):
            scratch.at[s][pl.ds(r, n, stride=d), :] = parts[r][:, s * LANES:(s + 1) * LANES]
    return jnp.concatenate([scratch.at[s][pl.ds(0, n * d), :] for s in range(width // LANES)], axis=1)


def _deinterleave(x, d, scratch):
    n, width = x.shape[0] // d, x.shape[1]
    if d > FAST_STRIDE and d % FAST_STRIDE == 0:
        outer = _deinterleave(x, FAST_STRIDE, scratch)
        inner = [_deinterleave(y, d // FAST_STRIDE, scratch) for y in outer]
        return [inner[r % FAST_STRIDE][r // FAST_STRIDE] for r in range(d)]
    for s in range(width // LANES):
        scratch.at[s][pl.ds(0, n * d), :] = x[:, s * LANES:(s + 1) * LANES]
    return [jnp.concatenate([scratch.at[s][pl.ds(r, n, stride=d), :] for s in range(width // LANES)], axis=1) for r in range(d)]


RING_SLOTS = 3


def _rowwise(name, fn, ins, outs, consts=(), reds=(), ts=512, in_dils=None, out_dils=None, comm=None, ring=False):
    in_dils = list(in_dils or [1] * len(ins))
    out_dils = list(out_dils or [1] * len(outs))
    n_rows = ins[0].shape[-2] * in_dils[0]
    ts = min(ts, n_rows)
    assert n_rows % ts == 0
    assert not ring or (comm is None and all(d == 1 and len(a.shape) == 2 for a, d in zip(ins, in_dils)))
    n_in, n_c, n_o, n_r = len(ins), len(consts), len(outs), len(reds)
    viewed = [(a.shape, d) for a, d in zip(list(ins) + list(outs), in_dils + out_dils) if d > 1]
    grid = (n_rows // ts,)
    h_ins, h_in_specs, h_outs, h_out_specs, begin, end = _hosted(comm, grid)
    n_hi, n_ho = len(h_ins), len(h_outs)

    def tile_spec(shape, d=1):
        if len(shape) == 2:
            return pl.BlockSpec((ts // d, shape[1]), lambda i: (i, 0))
        return pl.BlockSpec((shape[0], ts, shape[2]), lambda i: (0, i, 0))

    def whole_spec(shape):
        return pl.BlockSpec(shape, lambda i: (0,) * len(shape))

    def body(*refs):
        in_refs, c_refs, hi_refs = refs[:n_in], refs[n_in:n_in + n_c], refs[n_in + n_c:n_in + n_c + n_hi]
        refs = refs[n_in + n_c + n_hi:]
        o_refs, r_refs, ho_refs = refs[:n_o], refs[n_o:n_o + n_r], refs[n_o + n_r:n_o + n_r + n_ho]
        scratches = list(refs[n_o + n_r + n_ho:])
        sems = [scratches.pop(), scratches.pop()][::-1] if comm else []
        begin(hi_refs, ho_refs, sems)
        vals = []
        if ring:
            ring_sem = scratches.pop()
            bufs = [scratches.pop() for _ in in_refs][::-1]
            step = pl.program_id(0)

            def fetch(s, slot):
                rows = pl.ds(pl.multiple_of(s * ts, ts), ts)
                return [pltpu.make_async_copy(r.at[rows, :], b.at[slot], ring_sem.at[k, slot])
                        for k, (r, b) in enumerate(zip(in_refs, bufs))]

            @pl.when(step == 0)
            def _():
                for s in range(min(RING_SLOTS - 1, grid[0])):
                    for cp in fetch(s, s):
                        cp.start()

            @pl.when(step + (RING_SLOTS - 1) < grid[0])
            def _():
                for cp in fetch(step + (RING_SLOTS - 1), lax.rem(step + (RING_SLOTS - 1), RING_SLOTS)):
                    cp.start()

            slot = lax.rem(step, RING_SLOTS)
            for cp in fetch(step, slot):
                cp.wait()
            vals = [b[slot] for b in bufs]
        for r, d in zip(() if ring else in_refs, in_dils):
            if d == 1:
                vals.append(r[...])
            else:
                width = r.shape[1] // d
                vals.append(_interleave([r[:, k * width:(k + 1) * width] for k in range(d)], scratches.pop(0)))
        res = fn(*vals, *[r[...] for r in c_refs])
        res = tuple(res) if isinstance(res, (tuple, list)) else (res,)
        for r, v, d in zip(o_refs, res[:n_o], out_dils):
            if len(r.shape) == 3:
                for g in range(r.shape[0]):
                    r[g] = v[g].astype(r.dtype)
            elif d == 1:
                r[...] = v.astype(r.dtype)
            else:
                width = r.shape[1] // d
                for k, part in enumerate(_deinterleave(v, d, scratches.pop(0))):
                    r[:, k * width:(k + 1) * width] = part.astype(r.dtype)
        if n_r:
            i = pl.program_id(0)

            @pl.when(i == 0)
            def _():
                for r, v in zip(r_refs, res[n_o:]):
                    r[...] = v

            @pl.when(i > 0)
            def _():
                for r, v in zip(r_refs, res[n_o:]):
                    r[...] += v
        end(hi_refs, ho_refs, sems)

    out = pl.pallas_call(
        body, name=name, grid=grid,
        in_specs=([ANY] * n_in if ring else [tile_spec(a.shape, d) for a, d in zip(ins, in_dils)])
        + [whole_spec(c.shape) for c in consts] + h_in_specs,
        out_specs=[tile_spec(o.shape, d) for o, d in zip(outs, out_dils)] + [whole_spec(r.shape) for r in reds] + h_out_specs,
        out_shape=list(outs) + list(reds) + h_outs,
        scratch_shapes=[pltpu.VMEM((shape[1] // d // LANES, ts, LANES), F32) for shape, d in viewed]
        + ([pltpu.VMEM((RING_SLOTS, ts, a.shape[1]), a.dtype) for a in ins]
           + [pltpu.SemaphoreType.DMA((n_in, RING_SLOTS))] if ring else [])
        + (comm.sem_scratch() if comm else []),
        compiler_params=_params(("arbitrary",)),
    )(*ins, *consts, *h_ins)
    return out


def _sds(shape, dtype=F32):
    return jax.ShapeDtypeStruct(tuple(shape), dtype)


def _hosted(comm, grid, long_host=False):
    if comm is None:
        return [], [], [], [], (lambda *a: None), (lambda *a: None)

    def edge(which, at, ins, outs, sems):
        ids = [pl.program_id(d) for d in range(len(grid))]
        hit = ids[0] == at[0]
        for i, v in zip(ids[1:], at[1:]):
            hit = hit & (i == v)

        @pl.when(hit)
        def _():
            which(ins, outs, *sems)

    last = [g - 1 for g in grid]
    late, at_late = (3 * math.prod(grid)) // 4, []
    for g in reversed(grid):
        late, r = divmod(late, g)
        at_late.insert(0, r)

    def begin(ins, outs, sems):
        edge(comm.start, [0] * len(grid), ins, outs, sems)
        if comm.middle and long_host:
            edge(comm.middle, at_late, ins, outs, sems)

    def end(ins, outs, sems):
        if comm.middle and not long_host:
            edge(comm.middle, last, ins, outs, sems)
        edge(comm.finish, last, ins, outs, sems)
    return comm.ins, [ANY] * len(comm.ins), comm.out_shapes, [ANY] * len(comm.out_shapes), begin, end


def _mm(name, a, b, out_dtypes, *, ta=False, tb=False, tm=512, tn=512, tk=512, epilogue=None, extras=(), vecs=(),
        comm=None, b_pieces=False, out_pieces=False):
    (k_dim, m_dim) = a.shape if ta else a.shape[::-1]
    if b_pieces:
        b_rows, b_cols = 2 * b.shape[2], N_CHIPS * b.shape[3]
        (n_dim, k2) = (b_rows, b_cols) if tb else (b_cols, b_rows)
    else:
        (n_dim, k2) = b.shape if tb else b.shape[::-1]
    assert k_dim == k2
    tm, tn, tk = min(tm, m_dim), min(tn, n_dim), min(tk, k_dim)
    assert m_dim % tm == 0 and n_dim % tn == 0 and k_dim % tk == 0
    nk = k_dim // tk
    grid = (m_dim // tm, n_dim // tn, nk)
    a_spec = pl.BlockSpec((tk, tm), lambda i, j, k: (k, i)) if ta else pl.BlockSpec((tm, tk), lambda i, j, k: (i, k))
    if b_pieces:
        assert tn == b_rows and tk % b.shape[3] == 0 if tb else (tk, tn) == (b_rows, b.shape[3])
        n_bp = tk // b.shape[3] if tb else 1
        b_spec = pl.BlockSpec((n_bp,) + b.shape[1:], (lambda i, j, k: (k, 0, 0, 0)) if tb else (lambda i, j, k: (j, 0, 0, 0)))
    else:
        b_spec = pl.BlockSpec((tn, tk), lambda i, j, k: (j, k)) if tb else pl.BlockSpec((tk, tn), lambda i, j, k: (k, j))
    assert not out_pieces or (tm == m_dim and len(out_dtypes) == 1)
    dims = (((0 if ta else 1,), (1 if tb else 0,)), ((), ()))
    n_e, n_v, n_o = len(extras), len(vecs), len(out_dtypes)
    c_ins, c_in_specs, c_outs, c_out_specs, begin, end = _hosted(comm, grid)
    n_ci, n_co = len(c_ins), len(c_outs)

    def body(a_ref, b_ref, *refs):
        e_refs, v_refs, ci_refs = refs[:n_e], refs[n_e:n_e + n_v], refs[n_e + n_v:n_e + n_v + n_ci]
        refs = refs[n_e + n_v + n_ci:]
        o_refs, co_refs = refs[:n_o], refs[n_o:n_o + n_co]
        scratch = refs[n_o + n_co:]
        sems = scratch[-2:]
        begin(ci_refs, co_refs, sems)
        if b_pieces:
            pc = b.shape[3]
            part = None
            for cc in range(n_bp):
                a_val = a_ref[:, cc * pc:(cc + 1) * pc] if tb else a_ref[...]
                term = lax.dot_general(a_val.astype(BF16), b_ref[cc].reshape(b_rows, pc).astype(BF16), dims,
                                       preferred_element_type=F32)
                part = term if part is None else part + term
        else:
            part = lax.dot_general(a_ref[...].astype(BF16), b_ref[...].astype(BF16), dims, preferred_element_type=F32)

        def finish(acc):
            outs = epilogue(acc, *[r[...] for r in e_refs], *[r[...] for r in v_refs]) if epilogue else (acc,)
            for r, v in zip(o_refs, outs):
                r[...] = (v.reshape(2, tm // 2, tn) if out_pieces else v).astype(r.dtype)

        if nk == 1:
            finish(part)
        else:
            acc_ref = scratch[0]
            k = pl.program_id(2)

            @pl.when(k == 0)
            def _():
                acc_ref[...] = part

            @pl.when(k > 0)
            def _():
                acc_ref[...] += part

            @pl.when(k == nk - 1)
            def _():
                finish(acc_ref[...])
        end(ci_refs, co_refs, sems)

    tile = pl.BlockSpec((tm, tn), lambda i, j, k: (i, j))
    if out_pieces:
        out_tile, out_shape = pl.BlockSpec((None, 2, tm // 2, tn), lambda i, j, k: (j, 0, 0, 0)), (n_dim // tn, 2, tm // 2, tn)
    else:
        out_tile, out_shape = tile, (m_dim, n_dim)
    out = pl.pallas_call(
        body, name=name, grid=grid,
        in_specs=[a_spec, b_spec] + [tile] * n_e + [pl.BlockSpec((1, tn), lambda i, j, k: (0, j))] * n_v + c_in_specs,
        out_specs=[out_tile] * n_o + c_out_specs,
        out_shape=[_sds(out_shape, d) for d in out_dtypes] + c_outs,
        scratch_shapes=([pltpu.VMEM((tm, tn), F32)] if nk > 1 else []) + (comm.sem_scratch() if comm else []),
        compiler_params=_params(("arbitrary",) * 3 if comm else ("parallel", "parallel", "arbitrary")),
    )(a, b, *extras, *vecs, *c_ins)
    return out


def _swap_halves(x):
    width = x.shape[1]
    lane = lax.broadcasted_iota(jnp.int32, x.shape, 1)
    return jnp.where((lane % 64) < 32, pltpu.roll(x, width - 32, 1), pltpu.roll(x, 32, 1))


def _rope(x, cos, sin_signed):
    return x * cos + _swap_halves(x) * sin_signed


def _rope_t(d, cos, sin_signed):
    return d * cos - _swap_halves(d) * sin_signed


def _rms(x, g):
    rstd = lax.rsqrt(jnp.mean(x * x, axis=-1, keepdims=True) + RMS_EPS)
    return x * rstd * g


def _rms_bwd(x, g, dy):
    rstd = lax.rsqrt(jnp.mean(x * x, axis=-1, keepdims=True) + RMS_EPS)
    xh = x * rstd
    dyg = dy * g
    dx = rstd * (dyg - xh * jnp.mean(dyg * xh, axis=-1, keepdims=True))
    return dx, jnp.sum(dy * xh, axis=0, keepdims=True)


def _layer_norm(r, g, b):
    mu = jnp.mean(r, axis=-1, keepdims=True)
    xc = r - mu
    var = jnp.mean(xc * xc, axis=-1, keepdims=True)
    return xc * lax.rsqrt(var + LN_EPS) * g + b


def _dot(a, b, dims):
    return lax.dot_general(a, b, (dims, ((), ())), preferred_element_type=F32)


_NN, _NT, _TN = ((1,), (0,)), ((1,), (1,)), ((0,), (0,))


def _in_proj(x_bf, w_in_p, cos, sin):
    n_rows = x_bf.shape[0]
    ts = 512
    dils = [dil for _, dil in PATTERNS]
    n_p = len(dils)

    def body(x_ref, w_ref, cos_ref, sin_ref, *refs):
        view_refs, (cq_ref, kpe_ref, ckv_ref, scratch) = refs[:n_p], refs[n_p:]
        xb = x_ref[...]
        cos_v, sin_v = cos_ref[...], sin_ref[...]

        def seg(lo, hi):
            return _dot(xb, w_ref[:, lo:hi], _NN)

        def put(part, val):
            by_dil = {1: [val]}
            for ref, dil in zip(view_refs, dils):
                if dil not in by_dil:
                    if dil > FAST_STRIDE and dil % FAST_STRIDE == 0:
                        if FAST_STRIDE not in by_dil:
                            by_dil[FAST_STRIDE] = _deinterleave(val, FAST_STRIDE, scratch)
                        inner = [_deinterleave(y, dil // FAST_STRIDE, scratch) for y in by_dil[FAST_STRIDE]]
                        by_dil[dil] = [inner[r % FAST_STRIDE][r // FAST_STRIDE] for r in range(dil)]
                    else:
                        by_dil[dil] = _deinterleave(val, dil, scratch)
                for r, piece in enumerate(by_dil[dil]):
                    lo = (3 * r + part) * A_WIDTH
                    ref[:, lo:lo + A_WIDTH] = piece.astype(BF16)

        put(0, _rope(seg(0, 512), cos_v, sin_v) * A_SCALE)
        put(1, _rope(seg(512, 1024), cos_v, sin_v))
        put(2, seg(1024, 1536))
        cq_ref[...] = seg(1536, 1792)
        kpe_ref[...] = _rope(seg(1792, 1920), cos_v[:, :LANES], sin_v[:, :LANES]).astype(BF16)
        ckv_ref[...] = seg(1920, 2048)

    row = lambda c, d=1: pl.BlockSpec((ts // d, c * d), lambda i: (i, 0))
    return pl.pallas_call(
        body, name="in_proj", grid=(n_rows // ts,),
        in_specs=[row(D_MODEL), pl.BlockSpec((D_MODEL, IN_PAD), lambda i: (0, 0)), row(A_WIDTH), row(A_WIDTH)],
        out_specs=[row(3 * A_WIDTH, d) for d in dils] + [row(Q_LORA), row(LANES), row(KV_LORA)],
        out_shape=[_sds((n_rows // d, d * 3 * A_WIDTH), BF16) for d in dils]
        + [_sds((n_rows, Q_LORA)), _sds((n_rows, LANES), BF16), _sds((n_rows, KV_LORA))],
        scratch_shapes=[pltpu.VMEM((A_WIDTH // LANES, ts, LANES), F32)],
        compiler_params=_params(("parallel",)),
    )(x_bf, w_in_p, cos, sin)


def _band_mask(m, heads):
    qi = lax.broadcasted_iota(jnp.int32, (heads * SPAN, 2 * SPAN), 0) % SPAN
    kj = lax.broadcasted_iota(jnp.int32, (heads * SPAN, 2 * SPAN), 1)
    return ((kj < SPAN) & (kj >= qi) & (m > 0)) | ((kj >= SPAN) & ((kj - SPAN) <= qi))


def _even_lanes(rows):
    return lax.broadcasted_iota(jnp.int32, (rows, LANES), 1) < A_HEAD_DIM


DIL_GROUP = 4
HEAD_SLOT = LANES // A_HEADS
O_WORDS = A_WIDTH // 2
OL_WIDTH = O_WORDS + LANES


def _pack_halves(t):
    return _pack2(t[:, :O_WORDS], t[:, O_WORDS:])


def _unpack_halves(p):
    return jnp.concatenate(_unpack2(p), axis=1)


DQKV_WIDTH = 3 * A_WIDTH // 2


def _pack2(a, b):
    hi = lax.bitcast_convert_type(a.astype(jnp.bfloat16).astype(F32), jnp.uint32) & jnp.uint32(0xFFFF0000)
    lo = lax.shift_right_logical(lax.bitcast_convert_type(b.astype(jnp.bfloat16).astype(F32), jnp.uint32), jnp.uint32(16))
    return lax.bitcast_convert_type(hi | lo, F32)


def _unpack2(p):
    u = lax.bitcast_convert_type(p, jnp.uint32)
    return (lax.bitcast_convert_type(u & jnp.uint32(0xFFFF0000), F32),
            lax.bitcast_convert_type(lax.shift_left(u, jnp.uint32(16)), F32))


def _slot_lane(h):
    return HEAD_SLOT * (h // 2) + A_HEAD_DIM * (h % 2)


def _slot_head(lane):
    return 2 * (jnp.right_shift(lane, 4) & 3) + jnp.right_shift(lane, 6)


def _dil_fwd(qkv_view, dil, comm=None):
    nb = qkv_view.shape[0] // SPAN
    group = min(dil, DIL_GROUP)
    grid = (dil // group, nb)
    c_ins, c_in_specs, c_outs, c_out_specs, begin, end = _hosted(comm, grid)
    n_ci, n_co = len(c_ins), len(c_outs)

    def body(cur_ref, prev_ref, *refs):
        ci_refs, ol_ref, co_refs, sems = refs[:n_ci], refs[n_ci], refs[n_ci + 1:n_ci + 1 + n_co], refs[n_ci + 1 + n_co:]
        begin(ci_refs, co_refs, sems)
        mask = _band_mask(pl.program_id(1), 1)
        even, even2 = _even_lanes(SPAN), _even_lanes(2 * SPAN)
        pair_of_lane = jnp.right_shift(lax.broadcasted_iota(jnp.int32, (SPAN, LANES), 1), 4) & 3
        for g in range(group):
            lse_c, o_pairs = jnp.zeros((SPAN, LANES), F32), []
            for p in range(A_HEADS // 2):
                q_sl, k_sl, v_sl = (slice((3 * g + t) * A_WIDTH + p * LANES, (3 * g + t) * A_WIDTH + (p + 1) * LANES)
                                    for t in range(3))
                q2 = cur_ref[:, q_sl]
                kcat = jnp.concatenate([prev_ref[:, k_sl], cur_ref[:, k_sl]], axis=0)
                vcat = jnp.concatenate([prev_ref[:, v_sl], cur_ref[:, v_sl]], axis=0)
                zero, one = jnp.zeros_like(q2), jnp.ones_like(vcat)
                res, lses = [], []
                for first in (True, False):
                    qh = jnp.where(even, q2, zero) if first else jnp.where(even, zero, q2)
                    vh = jnp.where(even2, vcat, one) if first else jnp.where(even2, one, vcat)
                    s = jnp.where(mask, _dot(qh, kcat, _NT), NEG)
                    mx = jnp.max(s, axis=-1, keepdims=True)
                    r = _dot(jnp.exp(s - mx).astype(BF16), vh, _NN)
                    den = pltpu.roll(r, A_HEAD_DIM, 1)
                    res.append(r / den)
                    lses.append(mx + jnp.log(den))
                o_pairs.append(jnp.where(even, res[0], res[1]))
                lse_c = jnp.where(pair_of_lane == p, jnp.where(even, lses[0], lses[1]), lse_c)
            for p in range(2):
                ol_ref[:, OL_WIDTH * g + p * LANES:OL_WIDTH * g + (p + 1) * LANES] = _pack2(o_pairs[p], o_pairs[p + 2])
            ol_ref[:, OL_WIDTH * g + O_WORDS:OL_WIDTH * (g + 1)] = lse_c
        end(ci_refs, co_refs, sems)

    return pl.pallas_call(
        body, name=f"dil_fwd_d{dil}", grid=grid,
        in_specs=[pl.BlockSpec((SPAN, group * 3 * A_WIDTH), lambda r, m: (m, r)),
                  pl.BlockSpec((SPAN, group * 3 * A_WIDTH), lambda r, m: (jnp.maximum(m - 1, 0), r))] + c_in_specs,
        out_specs=[pl.BlockSpec((SPAN, group * OL_WIDTH), lambda r, m: (m, r))] + c_out_specs,
        out_shape=[_sds((qkv_view.shape[0], dil * OL_WIDTH))] + c_outs,
        scratch_shapes=comm.sem_scratch() if comm else [],
        compiler_params=_params(("arbitrary", "arbitrary") if comm else ("parallel", "arbitrary")),
    )(qkv_view, qkv_view, *c_ins)


def _slot_spread_matrix():
    i = lax.broadcasted_iota(jnp.int32, (LANES, A_WIDTH), 0)
    h = lax.broadcasted_iota(jnp.int32, (LANES, A_WIDTH), 1) // A_HEAD_DIM
    return (i == HEAD_SLOT * (h // 2) + A_HEAD_DIM * (h % 2)).astype(BF16)


def _spread_slots(c, spread):
    hi = c.astype(BF16)
    r1 = c - hi.astype(F32)
    mid = r1.astype(BF16)
    lo = (r1 - mid.astype(F32)).astype(BF16)
    return _dot(hi, spread, _NN) + _dot(mid, spread, _NN) + _dot(lo, spread, _NN)


def _pattern_weights(ol1, ol2, ol3):
    l1, l2, l3 = ol1[:, O_WORDS:], ol2[:, O_WORDS:], ol3[:, O_WORDS:]
    mx = jnp.maximum(jnp.maximum(l1, l2), l3)
    e1, e2, e3 = jnp.exp(l1 - mx), jnp.exp(l2 - mx), jnp.exp(l3 - mx)
    inv = 1.0 / (e1 + e2 + e3)
    return e1 * inv, e2 * inv, e3 * inv


def _mix_fwd(ols, b_out, a_g, b_g):
    def fn(ol1, ol2, ol3, b, ag, bg, spread):
        ws = [_spread_slots(w, spread) for w in _pattern_weights(ol1, ol2, ol3)]
        a = sum(w * _unpack_halves(ol[:, :O_WORDS]) for w, ol in zip(ws, (ol1, ol2, ol3)))
        return a, jnp.concatenate([_rms(a, ag), _rms(b, bg)], axis=1)

    n_rows = b_out.shape[0]
    return _rowwise("mix_fwd", fn, [*ols, b_out], [_sds((n_rows, A_WIDTH)), _sds((n_rows, 2 * A_WIDTH), BF16)],
                    consts=[a_g, b_g, _slot_spread_matrix()], in_dils=[dil for _, dil in PATTERNS] + [1])


def _mla_prep(cq, ckv, kpe, cos, sin, gq, w_uq_p, gkv, w_ukv_p):
    def fn(cq_v, ckv_v, kpe_v, cos_v, sin_v, gq_v, wq_v, gkv_v, wkv_v):
        cqn = _rms(cq_v, gq_v).astype(BF16)
        q = _dot(cqn, wq_v, _NN)
        qf = [jnp.concatenate([q[:, h * QK_NOPE:(h + 1) * QK_NOPE],
                               _rope(q[:, 512 + h * LANES:512 + (h + 1) * LANES], cos_v[:, :LANES], sin_v[:, :LANES])], axis=1)
              for h in range(MLA_HEADS)]
        ckvn = _rms(ckv_v, gkv_v).astype(BF16)
        kvv = _dot(ckvn, wkv_v, _NN)
        kf = [jnp.concatenate([kvv[:, h * QK_NOPE:(h + 1) * QK_NOPE], kpe_v.astype(F32)], axis=1) for h in range(MLA_HEADS)]
        return cqn, qf, ckvn, kf, kvv[:, 512:]

    n_rows = cq.shape[0]
    heads = _sds((MLA_HEADS, n_rows, QK_PAD), BF16)
    return _rowwise("mla_prep", fn, [cq, ckv, kpe, cos, sin],
                    [_sds((n_rows, Q_LORA), BF16), heads, _sds((n_rows, KV_LORA), BF16), heads, _sds((n_rows, MLA_WIDTH), BF16)],
                    consts=[gq, w_uq_p, gkv, w_ukv_p])


MLA_FWD_TILES = (1024, 2048)
MLA_BWD_TILES = (1024, 1024)


def _mla_fwd(qf, kf, v, comm=None):
    n_rows = v.shape[0]
    tq, tk = min(MLA_FWD_TILES[0], n_rows), min(MLA_FWD_TILES[1], n_rows)
    nq, nk = n_rows // tq, n_rows // tk
    grid = (MLA_HEADS, nq, nk)
    exp2_scale = MLA_SCALE * math.log2(math.e)
    c_ins, c_in_specs, c_outs, c_out_specs, begin, end = _hosted(comm, grid, long_host=True)
    n_ci, n_co = len(c_ins), len(c_outs)

    def last_k(i):
        return jnp.right_shift(i * tq + tq - 1, int(math.log2(tk)))

    def body(q_ref, k_ref, v_ref, *refs):
        ci_refs, (o_ref, lse_ref), co_refs = refs[:n_ci], refs[n_ci:n_ci + 2], refs[n_ci + 2:n_ci + 2 + n_co]
        m_sc, l_sc, acc_sc = refs[n_ci + 2 + n_co:n_ci + 5 + n_co]
        sems = refs[n_ci + 5 + n_co:]
        i, j = pl.program_id(1), pl.program_id(2)
        begin(ci_refs, co_refs, sems)

        @pl.when(j == 0)
        def _():
            m_sc[...] = jnp.full(m_sc.shape, NEG, F32)
            l_sc[...] = jnp.zeros(l_sc.shape, F32)
            acc_sc[...] = jnp.zeros(acc_sc.shape, F32)

        def step(masked):
            s = _dot(q_ref[...], k_ref[...], _NT)
            if masked:
                row = lax.broadcasted_iota(jnp.int32, (tq, tk), 0) + i * tq
                col = lax.broadcasted_iota(jnp.int32, (tq, tk), 1) + j * tk
                s = jnp.where(col <= row, s, NEG)
            m_prev = m_sc[...]
            m_new = jnp.maximum(m_prev, jnp.max(s, axis=-1, keepdims=True))
            alpha = jnp.exp2((m_prev - m_new) * exp2_scale)
            p = jnp.exp2((s - m_new) * exp2_scale)
            l_sc[...] = alpha * l_sc[...] + jnp.sum(p, axis=-1, keepdims=True)
            acc_sc[...] = alpha * acc_sc[...] + _dot(p.astype(BF16), v_ref[...], _NN)
            m_sc[...] = m_new

        active = j * tk <= i * tq + tq - 1
        crosses = (j + 1) * tk - 1 > i * tq

        @pl.when(active & jnp.logical_not(crosses))
        def _():
            step(False)

        @pl.when(active & crosses)
        def _():
            step(True)

        @pl.when(j == last_k(i))
        def _():
            o_ref[...] = acc_sc[...] / l_sc[...]
            lse_ref[...] = jnp.broadcast_to(m_sc[...] * MLA_SCALE + jnp.log(l_sc[...]), (tq, V_DIM))

        end(ci_refs, co_refs, sems)

    return pl.pallas_call(
        body, name="mla_fwd", grid=grid,
        in_specs=[pl.BlockSpec((None, tq, QK_PAD), lambda h, i, j: (h, i, 0)),
                  pl.BlockSpec((None, tk, QK_PAD), lambda h, i, j: (h, jnp.minimum(j, last_k(i)), 0)),
                  pl.BlockSpec((tk, V_DIM), lambda h, i, j: (jnp.minimum(j, last_k(i)), h))] + c_in_specs,
        out_specs=[pl.BlockSpec((tq, V_DIM), lambda h, i, j: (i, h)), pl.BlockSpec((tq, V_DIM), lambda h, i, j: (i, h))]
        + c_out_specs,
        out_shape=[_sds((n_rows, MLA_WIDTH)), _sds((n_rows, MLA_WIDTH))] + c_outs,
        scratch_shapes=[pltpu.VMEM((tq, 1), F32), pltpu.VMEM((tq, 1), F32), pltpu.VMEM((tq, V_DIM), F32)]
        + (comm.sem_scratch() if comm else []),
        compiler_params=_params(("arbitrary",) * 3 if comm else ("parallel", "parallel", "arbitrary")),
    )(qf, kf, v, *c_ins)


def _mm_res_ln(name, a, w, xres, g, b, tm):
    def epi(acc, xr, g_v, b_v):
        r = ALPHA * xr + acc
        y = _layer_norm(r, g_v, b_v)
        return y, y, r

    return _mm(name, a, w, (F32, BF16, F32), tm=tm, tn=D_MODEL, tk=a.shape[1], epilogue=epi, extras=[xres], vecs=[g, b])


def _mm_relu2(x_bf, w, comm=None):
    def epi(acc):
        r = jnp.maximum(acc, 0.0)
        return (r * r,)

    return _mm("ff1", x_bf, w, (BF16,), tm=2048, tn=1024, tk=D_MODEL, epilogue=epi, b_pieces=True, comm=comm)


def _loss_fn(y, t):
    def fn(y_v, t_v):
        d = y_v - t_v
        part = jnp.sum(jnp.sum(d * d, axis=1, keepdims=True), axis=0, keepdims=True)
        return d * (1.0 / D_MODEL), part

    dy, part = _rowwise("loss", fn, [y, t], [_sds(y.shape)], reds=[_sds((1, 1))])
    return part * (0.5 / D_MODEL), dy


def _ln_bwd_rows(dy, r, g):
    mu = jnp.mean(r, axis=-1, keepdims=True)
    xc = r - mu
    rstd = lax.rsqrt(jnp.mean(xc * xc, axis=-1, keepdims=True) + LN_EPS)
    xh = xc * rstd
    dxh = dy * g
    dr = rstd * (dxh - jnp.mean(dxh, axis=-1, keepdims=True) - xh * jnp.mean(dxh * xh, axis=-1, keepdims=True))
    return dr, dr, jnp.sum(dy * xh, axis=0, keepdims=True), jnp.sum(dy, axis=0, keepdims=True)


def _ln_bwd(name, dy, r, g):
    return _rowwise(name, _ln_bwd_rows, [dy, r], [_sds(dy.shape), _sds(dy.shape, BF16)], consts=[g],
                    reds=[_sds((1, D_MODEL)), _sds((1, D_MODEL))], ring=True)


def _head_sum_matrix():
    i = lax.broadcasted_iota(jnp.int32, (A_WIDTH, LANES), 0) // A_HEAD_DIM
    return (i == _slot_head(lax.broadcasted_iota(jnp.int32, (A_WIDTH, LANES), 1))).astype(BF16)


def _mix_bwd(dmixed, a_out, b_out, ols, a_g, b_g):
    def fn(dm, a, b, ol1, ol2, ol3, ag, bg, sum_mat, spread):
        da, dga = _rms_bwd(a, ag, dm[:, :A_WIDTH])
        db, dgb = _rms_bwd(b, bg, dm[:, A_WIDTH:])
        t = da * a
        t_hi = t.astype(BF16)
        t_lo = (t - t_hi.astype(F32)).astype(BF16)
        tsum = _dot(t_hi, sum_mat, _NN) + _dot(t_lo, sum_mat, _NN)
        tb = db * b
        delta_b = jnp.concatenate(
            [jnp.broadcast_to(jnp.sum(tb[:, h * V_DIM:(h + 1) * V_DIM], axis=-1, keepdims=True), (tb.shape[0], V_DIM))
             for h in range(MLA_HEADS)], axis=1)
        dods = [jnp.concatenate([_pack_halves(_spread_slots(w, spread) * da), w * tsum], axis=1)
                for w in _pattern_weights(ol1, ol2, ol3)]
        return (*dods, db, delta_b, dga, dgb)

    n_rows = a_out.shape[0]
    wide = (n_rows, A_WIDTH)
    dils = [dil for _, dil in PATTERNS]
    return _rowwise("mix_bwd", fn, [dmixed, a_out, b_out, *ols],
                    [_sds((n_rows // d, d * OL_WIDTH)) for d in dils] + [_sds(wide, BF16), _sds(wide)],
                    consts=[a_g, b_g, _head_sum_matrix(), _slot_spread_matrix()],
                    reds=[_sds((1, A_WIDTH)), _sds((1, MLA_WIDTH))], ts=256,
                    in_dils=[1, 1, 1] + dils, out_dils=dils + [1, 1])


def _dil_bwd(qkv_view, ol_view, dod_view, dil, comm=None):
    nb = qkv_view.shape[0] // SPAN
    group = min(dil, DIL_GROUP)
    grid = (dil // group, nb + 1)
    c_ins, c_in_specs, c_outs, c_out_specs, begin, end = _hosted(comm, grid)
    n_ci, n_co = len(c_ins), len(c_outs)

    def body(cur_ref, prev_ref, ol_ref, dod_ref, *refs):
        ci_refs, out_ref, co_refs = refs[:n_ci], refs[n_ci], refs[n_ci + 1:n_ci + 1 + n_co]
        carry, sems = refs[n_ci + 1 + n_co], refs[n_ci + 2 + n_co:]
        m = pl.program_id(1)
        begin(ci_refs, co_refs, sems)

        def write_packed(g, pairs):
            lo = DQKV_WIDTH * g
            for p, (dq_p, dk_p, _) in enumerate(pairs):
                out_ref[:, lo + p * LANES:lo + (p + 1) * LANES] = _pack2(dq_p, dk_p)
            for p in range(2):
                out_ref[:, lo + A_WIDTH + p * LANES:lo + A_WIDTH + (p + 1) * LANES] = _pack2(pairs[p][2], pairs[p + 2][2])

        @pl.when(m == 0)
        def _():
            carry[...] = jnp.zeros(carry.shape, F32)

        @pl.when(m < nb)
        def _():
            mask, even = _band_mask(m, 2), _even_lanes(SPAN)
            for g in range(group):
                done = []
                for p in range(A_HEADS // 2):
                    q_sl, k_sl, v_sl = (slice((3 * g + t) * A_WIDTH + p * LANES, (3 * g + t) * A_WIDTH + (p + 1) * LANES)
                                        for t in range(3))
                    do_sl = slice(OL_WIDTH * g + (p % 2) * LANES, OL_WIDTH * g + (p % 2 + 1) * LANES)
                    s_e, s_o = (OL_WIDTH * g + O_WORDS + _slot_lane(2 * p + t) for t in range(2))
                    q2, do2 = cur_ref[:, q_sl], _unpack2(dod_ref[:, do_sl])[p // 2].astype(BF16)
                    zero = jnp.zeros_like(q2)
                    qcat = jnp.concatenate([jnp.where(even, q2, zero), jnp.where(even, zero, q2)], axis=0)
                    docat = jnp.concatenate([jnp.where(even, do2, zero), jnp.where(even, zero, do2)], axis=0)
                    kcat = jnp.concatenate([prev_ref[:, k_sl], cur_ref[:, k_sl]], axis=0)
                    vcat = jnp.concatenate([prev_ref[:, v_sl], cur_ref[:, v_sl]], axis=0)
                    lse_c = jnp.concatenate([ol_ref[:, s_e:s_e + 1], ol_ref[:, s_o:s_o + 1]], axis=0)
                    dl_c = jnp.concatenate([dod_ref[:, s_e:s_e + 1], dod_ref[:, s_o:s_o + 1]], axis=0)
                    pr = jnp.exp(jnp.where(mask, _dot(qcat, kcat, _NT), NEG) - lse_c)
                    ds = (pr * (_dot(docat, vcat, _NT) - dl_c)).astype(BF16)
                    dq = _dot(ds, kcat, _NN)
                    dk2 = _dot(ds, qcat, _TN)
                    dv2 = _dot(pr.astype(BF16), docat, _TN)
                    done.append((carry[:, q_sl], carry[:, k_sl] + dk2[:SPAN], carry[:, v_sl] + dv2[:SPAN]))
                    carry[:, q_sl] = jnp.where(even, dq[:SPAN], dq[SPAN:])
                    carry[:, k_sl] = dk2[SPAN:]
                    carry[:, v_sl] = dv2[SPAN:]
                write_packed(g, done)

        @pl.when(m == nb)
        def _():
            for g in range(group):
                write_packed(g, [tuple(carry[:, (3 * g + t) * A_WIDTH + p * LANES:(3 * g + t) * A_WIDTH + (p + 1) * LANES]
                                       for t in range(3)) for p in range(A_HEADS // 2)])

        end(ci_refs, co_refs, sems)

    last = nb - 1

    def cur(width):
        return pl.BlockSpec((SPAN, group * width), lambda r, m: (jnp.minimum(m, last), r))

    def prev(width):
        return pl.BlockSpec((SPAN, group * width), lambda r, m: (jnp.clip(m - 1, 0, last), r))

    return pl.pallas_call(
        body, name=f"dil_bwd_d{dil}", grid=grid,
        in_specs=[cur(3 * A_WIDTH), prev(3 * A_WIDTH), cur(OL_WIDTH), cur(OL_WIDTH)] + c_in_specs,
        out_specs=[prev(DQKV_WIDTH)] + c_out_specs,
        out_shape=[_sds((qkv_view.shape[0], dil * DQKV_WIDTH))] + c_outs,
        scratch_shapes=[pltpu.VMEM((SPAN, group * 3 * A_WIDTH), F32)] + (comm.sem_scratch() if comm else []),
        compiler_params=_params(("arbitrary", "arbitrary") if comm else ("parallel", "arbitrary")),
    )(qkv_view, qkv_view, ol_view, dod_view, *c_ins)


def _mla_bwd(qf, kf, v, lse, do, delta, comm=None):
    n_rows = v.shape[0]
    tq, tk = min(MLA_BWD_TILES[0], n_rows), min(MLA_BWD_TILES[1], n_rows)
    nq, nk = n_rows // tq, n_rows // tk

    def first_q(j):
        return jnp.right_shift(j * tk, int(math.log2(tq)))

    grid = (MLA_HEADS, nk, nq)
    c_ins, c_in_specs, c_outs, c_out_specs, begin, end = _hosted(comm, grid)
    n_ci, n_co = len(c_ins), len(c_outs)

    def body(q_ref, k_ref, v_ref, lse_ref, do_ref, dl_ref, *refs):
        ci_refs, (dq_ref, dk_ref, dv_ref), co_refs = refs[:n_ci], refs[n_ci:n_ci + 3], refs[n_ci + 3:n_ci + 3 + n_co]
        dk_sc, dv_sc = refs[n_ci + 3 + n_co:n_ci + 5 + n_co]
        sems = refs[n_ci + 5 + n_co:]
        j, i = pl.program_id(1), pl.program_id(2)
        begin(ci_refs, co_refs, sems)

        @pl.when((j == 0) & (i == 0))
        def _():
            dq_ref[...] = jnp.zeros(dq_ref.shape, F32)

        @pl.when(i == first_q(j))
        def _():
            dk_sc[...] = jnp.zeros(dk_sc.shape, F32)
            dv_sc[...] = jnp.zeros(dv_sc.shape, F32)

        def step(masked):
            q, k, dob = q_ref[...], k_ref[...], do_ref[...]
            s = _dot(q, k, _NT) * MLA_SCALE
            if masked:
                row = lax.broadcasted_iota(jnp.int32, (tq, tk), 0) + i * tq
                col = lax.broadcasted_iota(jnp.int32, (tq, tk), 1) + j * tk
                s = jnp.where(col <= row, s, NEG)
            p = jnp.exp(s - lse_ref[:, :1])
            dp = _dot(dob, v_ref[...], _NT)
            ds = (p * (dp - dl_ref[:, :1]) * MLA_SCALE).astype(BF16)
            dv_sc[...] += _dot(p.astype(BF16), dob, _TN)
            dk_sc[...] += _dot(ds, q, _TN)
            rows = pl.ds(pl.multiple_of(i * tq, tq), tq)
            dq_ref[rows, :] += _dot(ds, k, _NN)

        active = i >= first_q(j)
        crosses = (j + 1) * tk - 1 > i * tq

        @pl.when(active & jnp.logical_not(crosses))
        def _():
            step(False)

        @pl.when(active & crosses)
        def _():
            step(True)

        @pl.when(i == nq - 1)
        def _():
            dk_ref[...] = dk_sc[...]
            dv_ref[...] = dv_sc[...]

        end(ci_refs, co_refs, sems)

    qrow = lambda h, j, i: (jnp.maximum(i, first_q(j)), h)
    return pl.pallas_call(
        body, name="mla_bwd", grid=grid,
        in_specs=[pl.BlockSpec((None, tq, QK_PAD), lambda h, j, i: (h, jnp.maximum(i, first_q(j)), 0)),
                  pl.BlockSpec((None, tk, QK_PAD), lambda h, j, i: (h, j, 0)),
                  pl.BlockSpec((tk, V_DIM), lambda h, j, i: (j, h)),
                  pl.BlockSpec((tq, V_DIM), qrow), pl.BlockSpec((tq, V_DIM), qrow), pl.BlockSpec((tq, V_DIM), qrow)]
        + c_in_specs,
        out_specs=[pl.BlockSpec((None, n_rows, QK_PAD), lambda h, j, i: (h, 0, 0)),
                   pl.BlockSpec((None, tk, QK_PAD), lambda h, j, i: (h, j, 0)),
                   pl.BlockSpec((tk, V_DIM), lambda h, j, i: (j, h))] + c_out_specs,
        out_shape=[_sds((MLA_HEADS, n_rows, QK_PAD)), _sds((MLA_HEADS, n_rows, QK_PAD)), _sds((n_rows, MLA_WIDTH))] + c_outs,
        scratch_shapes=[pltpu.VMEM((tk, QK_PAD), F32), pltpu.VMEM((tk, V_DIM), F32)] + (comm.sem_scratch() if comm else []),
        compiler_params=_params(("arbitrary",) * 3 if comm else ("parallel", "arbitrary", "arbitrary")),
    )(qf, kf, v, lse, do, delta, *c_ins)


def _mla_bwd_prep(dqf, dkf, dv, cos, sin):
    def fn(dqf_v, dkf_v, dv_v, cos_v, sin_v):
        dq = jnp.concatenate([dqf_v[h][:, :QK_NOPE] for h in range(MLA_HEADS)]
                             + [_rope_t(dqf_v[h][:, QK_NOPE:], cos_v[:, :LANES], sin_v[:, :LANES]) for h in range(MLA_HEADS)], axis=1)
        dkv = jnp.concatenate([dkf_v[h][:, :QK_NOPE] for h in range(MLA_HEADS)] + [dv_v], axis=1)
        dkpe = dkf_v[0][:, QK_NOPE:] + dkf_v[1][:, QK_NOPE:] + dkf_v[2][:, QK_NOPE:] + dkf_v[3][:, QK_NOPE:]
        return dq, dkv, dkpe

    n_rows = dv.shape[0]
    return _rowwise("mla_bwd_prep", fn, [dqf, dkf, dv, cos, sin],
                    [_sds((n_rows, UQ_PAD), BF16), _sds((n_rows, 1024), BF16), _sds((n_rows, LANES))], ts=256)


def _assemble_dh(dqkvs, dcqn, cq, dckvn, ckv, dkpe, cos, sin, gq, gkv, comm=None):
    def fn(g1, g2, g3, dcqn_v, cq_v, dckvn_v, ckv_v, dkpe_v, cos_v, sin_v, gq_v, gkv_v):
        dq = dk = dv = 0.0
        for packed in (g1, g2, g3):
            dq_p, dk_p = _unpack2(packed[:, :A_WIDTH])
            dv_p = jnp.concatenate(_unpack2(packed[:, A_WIDTH:]), axis=1)
            dq, dk, dv = dq + dq_p, dk + dk_p, dv + dv_p
        dqa = _rope_t(dq, cos_v, sin_v) * A_SCALE
        dka = _rope_t(dk, cos_v, sin_v)
        dcq, dgq = _rms_bwd(cq_v, gq_v, dcqn_v)
        dckv, dgkv = _rms_bwd(ckv_v, gkv_v, dckvn_v)
        dkr = _rope_t(dkpe_v, cos_v[:, :LANES], sin_v[:, :LANES])
        return jnp.concatenate([dqa, dka, dv, dcq, dkr, dckv], axis=1), dgq, dgkv

    n_rows = cq.shape[0]
    return _rowwise("assemble_dh", fn, [*dqkvs, dcqn, cq, dckvn, ckv, dkpe, cos, sin],
                    [_sds((n_rows, IN_PAD), BF16)], consts=[gq, gkv],
                    reds=[_sds((1, Q_LORA)), _sds((1, KV_LORA))], ts=256,
                    in_dils=[dil for _, dil in PATTERNS] + [1] * 7, comm=comm)


def _layer_fwd(x_f32, x_bf, w, sm, cos, sin, behind=(None, None), rest=None):
    *qkv_views, cq, kpe, ckv = _in_proj(x_bf, w["w_in"], cos, sin)
    ol_views, brought = [], []
    for i, (view, (_, dil)) in enumerate(zip(qkv_views, PATTERNS)):
        ol, *got = _dil_fwd(view, dil, comm=rest[0][i] if rest else None)
        ol_views.append(ol)
        brought.append(got)
    if rest:
        w = {**w, **rest[1](brought)}
    cqn, qf, ckvn, kf, v = _mla_prep(cq, ckv, kpe, cos, sin, sm["q_a_norm"], w["w_uq"], sm["kv_a_norm"], w["w_ukv"])
    b_out, b_lse, *behind_attn = _mla_fwd(qf, kf, v, comm=behind[0])
    a_out, mixed = _mix_fwd(ol_views, b_out, sm["a_out_norm"], sm["b_out_norm"])
    x1, x1_bf, r1 = _mm_res_ln("wo_ln1", mixed, w["w_o"], x_f32, sm["ln1_g"], sm["ln1_b"], tm=1024)
    f, *behind_ff1 = _mm_relu2(x1_bf, w["w_ff1"], comm=behind[1])
    x2, x2_bf, r2 = _mm_res_ln("ff2_ln2", f, w["w_ff2"], x1, sm["ln2_g"], sm["ln2_b"], tm=512)
    saved = dict(w=w, x_bf=x_bf, qkv_views=qkv_views, ol_views=ol_views, cq=cq, ckv=ckv, cqn=cqn, qf=qf, ckvn=ckvn,
                 kf=kf, v=v, b_out=b_out, b_lse=b_lse, a_out=a_out, mixed=mixed, x1_bf=x1_bf, r1=r1, f=f, r2=r2)
    return x2, x2_bf, saved, (behind_attn, behind_ff1)


def _layer_bwd(dx2, w, sm, sv, cos, sin, reduce_of, pending=None):
    dr2, dr2_bf, dg2, db2 = _ln_bwd("ln2_bwd", dx2, sv["r2"], sm["ln2_g"])
    (dw_ff2,) = _mm("dw_ff2", sv["f"], dr2_bf, (F32,), ta=True, tm=1024, tn=1024, tk=dx2.shape[0])
    du, *swapped = _mm("d_u", dr2_bf, w["w_ff2"], (BF16,), tb=True, tm=1024, tn=2048, tk=D_MODEL,
                       epilogue=lambda acc, f: (acc * (2.0 * jnp.sqrt(f.astype(F32))),), extras=[sv["f"]],
                       comm=_swap_comm(pending.pieces) if pending else None)
    if pending:
        pending.after_swap(swapped)
    (dw_ff1,) = _mm("dw_ff1", sv["x1_bf"], du, (F32,), ta=True, tm=1024, tn=1024, tk=dx2.shape[0], out_pieces=True)
    (dy1,) = _mm("d_x1", du, w["w_ff1"], (F32,), tb=True, tm=512, tn=1024, tk=D_FF,
                 epilogue=lambda acc, d: (acc + ALPHA * d,), extras=[dr2], b_pieces=True)
    dr1, dr1_bf, dg1, db1 = _ln_bwd("ln1_bwd", dy1, sv["r1"], sm["ln1_g"])
    (dw_o,) = _mm("dw_o", sv["mixed"], dr1_bf, (F32,), ta=True, tm=1024, tn=1024, tk=2048)
    early = reduce_of(dict(w_o=dw_o, w_ff1=dw_ff1, w_ff2=dw_ff2))
    (dmixed,) = _mm("d_mixed", dr1_bf, w["w_o"], (F32,), tb=True, tm=1024, tn=1024, tk=D_MODEL)
    dod1, dod2, dod3, do_b, delta_b, dga, dgb = _mix_bwd(
        dmixed, sv["a_out"], sv["b_out"], sv["ol_views"], sm["a_out_norm"], sm["b_out_norm"])
    dqkvs = []
    for i, ((_, dil), qkv_view, ol_view, dod_view) in enumerate(zip(PATTERNS, sv["qkv_views"], sv["ol_views"], (dod1, dod2, dod3))):
        swap_here = early is not None and i == 0
        dqkv, *swapped = _dil_bwd(qkv_view, ol_view, dod_view, dil, comm=_swap_comm(early.pieces) if swap_here else None)
        dqkvs.append(dqkv)
        if swap_here:
            early.after_swap(swapped)
    to_scatter = (early.s16 if early else []) + (pending.s16 if pending else [])
    dqf, dkf, dv_b, *scattered = _mla_bwd(sv["qf"], sv["kf"], sv["v"], sv["b_lse"], do_b, delta_b,
                                          comm=_scatter_comm(to_scatter) if to_scatter else None)
    if early:
        early.after_scatter(scattered[:len(early.s16)])
    if pending:
        pending.after_scatter(scattered[len(scattered) - len(pending.s16):])
    dq_b, dkv_b, dkpe = _mla_bwd_prep(dqf, dkf, dv_b, cos, sin)
    (dw_uq,) = _mm("dw_uq", sv["cqn"], dq_b, (F32,), ta=True, tm=Q_LORA, tn=1024, tk=2048)
    (dcqn,) = _mm("d_cqn", dq_b, w["w_uq"], (F32,), tb=True, tm=1024, tn=Q_LORA, tk=UQ_PAD)
    (dw_ukv,) = _mm("dw_ukv", sv["ckvn"], dkv_b, (F32,), ta=True, tm=KV_LORA, tn=1024, tk=2048)
    (dckvn,) = _mm("d_ckvn", dkv_b, w["w_ukv"], (F32,), tb=True, tm=1024, tn=KV_LORA, tk=1024)
    to_share = (early.mine if early else []) + (pending.mine if pending else [])
    dh, dgq, dgkv, *shared = _assemble_dh(dqkvs, dcqn, sv["cq"], dckvn, sv["ckv"], dkpe, cos, sin,
                                          sm["q_a_norm"], sm["kv_a_norm"], comm=_share_comm(to_share) if to_share else None)
    if early:
        early.after_share(shared[:len(early.mine)])
    if pending:
        pending.after_share(shared[len(shared) - len(pending.mine):])
    (dw_in,) = _mm("dw_in", sv["x_bf"], dh, (F32,), ta=True, tm=1024, tn=1024, tk=dx2.shape[0])
    (dx,) = _mm("d_x", dh, w["w_in"], (F32,), tb=True, tm=1024, tn=1024, tk=2048,
                epilogue=lambda acc, d: (acc + ALPHA * d,), extras=[dr1])
    late = reduce_of(dict(w_in=dw_in, w_uq=dw_uq, w_ukv=dw_ukv))
    dsm = dict(q_a_norm=dgq, kv_a_norm=dgkv, a_out_norm=dga, b_out_norm=dgb, ln1_g=dg1, ln1_b=db1, ln2_g=dg2, ln2_b=db2)
    return dx, dsm, late


def _pad_w_in(w):
    return jnp.concatenate([w[:, :1792], w[:, 1920:1984], jnp.zeros((w.shape[0], 64), w.dtype), w[:, 1792:1920]], axis=1)


def _unpad_w_in(w):
    return jnp.concatenate([w[:, :1792], w[:, 1920:2048], w[:, 1792:1856]], axis=1)


def _rope_tables(n_rows):
    half = A_HEAD_DIM // 2
    inv_freq = ROPE_THETA ** (-jnp.arange(half, dtype=F32) / half)
    ang = jnp.arange(n_rows, dtype=F32)[:, None] * inv_freq[None, :]
    cos = jnp.tile(jnp.cos(ang), (1, 2 * A_HEADS))
    sin = jnp.tile(jnp.concatenate([-jnp.sin(ang), jnp.sin(ang)], axis=1), (1, A_HEADS))
    return cos, sin


W_NAMES = ("w_in", "w_uq", "w_ukv", "w_o", "w_ff1", "w_ff2")
PIECE = dict(w_in=(512, 512), w_uq=(128, 256), w_ukv=(64, 256), w_o=(128, 1024), w_ff1=(512, 1024), w_ff2=(512, 1024))
SHARD_COLS = dict(w_in=496, w_uq=192, w_ukv=256, w_o=1024, w_ff1=1024, w_ff2=1024)


def _col_pieces(t, pad_to):
    rows, _, cols = t.shape
    t = jnp.pad(t, ((0, 0), (0, 0), (0, pad_to - cols)))
    return t.reshape(2, rows // 2, N_CHIPS, pad_to).transpose(2, 0, 1, 3)


def _grad_pieces(dws):
    def heads(t, rows):
        return jnp.concatenate([t[:, :512].reshape(rows, MLA_HEADS, LANES), t[:, 512:].reshape(rows, MLA_HEADS, LANES)], axis=2)

    make = dict(
        w_in=lambda t: _col_pieces(_unpad_w_in(t).reshape(D_MODEL, N_CHIPS, SHARD_COLS["w_in"]), 512),
        w_uq=lambda t: _col_pieces(heads(t, Q_LORA), 256),
        w_ukv=lambda t: _col_pieces(heads(t, KV_LORA), 256),
        w_o=lambda t: t.reshape(N_CHIPS, 2, 128, D_MODEL),
        w_ff1=lambda t: t,
        w_ff2=lambda t: t.reshape(N_CHIPS, 2, 512, D_MODEL))
    return {n: make[n](t) for n, t in dws.items()}


def _weights_from_pieces(p):
    def cols(t):
        return t.transpose(1, 2, 0, 3).reshape(2 * t.shape[2], N_CHIPS, t.shape[3])

    def heads(t, rows):
        return jnp.concatenate([t[:, :, :LANES].reshape(rows, 512), t[:, :, LANES:].reshape(rows, 512)], axis=1)

    make = dict(
        w_in=lambda t: _pad_w_in(cols(t)[:, :, :SHARD_COLS["w_in"]].reshape(D_MODEL, IN_COLS)),
        w_uq=lambda t: heads(cols(t), Q_LORA),
        w_ukv=lambda t: heads(cols(t), KV_LORA),
        w_o=lambda t: t.reshape(D_MODEL, D_MODEL),
        w_ff1=lambda t: t,
        w_ff2=lambda t: t.reshape(D_FF, D_MODEL))
    return {n: make[n](t) for n, t in p.items()}


def _my_piece(name, shard, half):
    rows, cols = shard.shape[0] // 2, shard.shape[1]
    t = lax.dynamic_slice_in_dim(shard, half * rows, rows, axis=0).astype(BF16)
    return jnp.pad(t, ((0, 0), (0, PIECE[name][1] - cols)))


MESH = pl.DeviceIdType.MESH
ANY = pl.BlockSpec(memory_space=pl.ANY)


class _Comm:
    def __init__(self, ins, out_shapes, n_sems, start, finish, middle=None):
        self.ins, self.out_shapes, self.n_sems, self.start, self.finish = list(ins), list(out_shapes), n_sems, start, finish
        self.middle = middle

    def sem_scratch(self):
        return [pltpu.SemaphoreType.DMA((self.n_sems,)), pltpu.SemaphoreType.DMA((self.n_sems,))]


def _comm_call(name, comm):
    n_in, n_out = len(comm.ins), len(comm.out_shapes)

    def body(*refs):
        ins, outs, sems = refs[:n_in], refs[n_in:n_in + n_out], refs[n_in + n_out:]
        comm.start(ins, outs, *sems)
        if comm.middle:
            comm.middle(ins, outs, *sems)
        comm.finish(ins, outs, *sems)

    return pl.pallas_call(body, name=name, out_shape=comm.out_shapes, in_specs=[ANY] * n_in, out_specs=[ANY] * n_out,
                          scratch_shapes=comm.sem_scratch())(*comm.ins)


def _all_gather_comm(blocks):
    n = len(blocks)

    def plan(x_refs, out_refs, send_sems, recv_sems):
        x, y, c = lax.axis_index("x"), lax.axis_index("y"), lax.axis_index("c")
        me, sibling = (x, y, c), (x, y, 1 - c)
        chips = [(1 - x, y), (x, 1 - y), (1 - x, 1 - y)]

        def copy(t, k, blk, to, src=None):
            px, py, pc = blk
            slot = out_refs[t].at[4 * px + 2 * py + pc]
            return pltpu.make_async_remote_copy(
                src_ref=slot if src is None else src, dst_ref=slot,
                send_sem=send_sems.at[7 * t + k], recv_sem=recv_sems.at[7 * t + k], device_id=to, device_id_type=MESH)

        first = []
        for t in range(n):
            first.append(copy(t, 0, me, sibling, src=x_refs[t]))
            first += [copy(t, 1 + j, me, (*chip, c), src=x_refs[t]) for j, chip in enumerate(chips)]
        return me, sibling, chips, c, copy, first

    def start(x_refs, out_refs, send_sems, recv_sems):
        for cp in plan(x_refs, out_refs, send_sems, recv_sems)[-1]:
            cp.start()

    def pass_on(x_refs, out_refs, send_sems, recv_sems):
        me, sibling, chips, c, copy, _ = plan(x_refs, out_refs, send_sems, recv_sems)
        for j, chip in enumerate(chips):
            for t in range(n):
                copy(t, 1 + j, (*chip, c), me).wait_recv()
                copy(t, 4 + j, (*chip, c), sibling).start()

    def finish(x_refs, out_refs, send_sems, recv_sems):
        me, sibling, chips, c, copy, sent = plan(x_refs, out_refs, send_sems, recv_sems)
        for t in range(n):
            copy(t, 0, sibling, me).wait_recv()
        for j, chip in enumerate(chips):
            for t in range(n):
                copy(t, 4 + j, (*chip, 1 - c), me).wait_recv()
                sent.append(copy(t, 4 + j, (*chip, c), sibling))
        for cp in sent:
            cp.wait_send()

    return _Comm(blocks, [_sds((N_DEV,) + b.shape, b.dtype) for b in blocks], 7 * n, start, finish, middle=pass_on)


def _simple_comm(ins, out_shapes, n_sems, copies):
    def start(in_refs, out_refs, send_sems, recv_sems):
        for cp in copies(in_refs, out_refs, send_sems, recv_sems):
            cp.start()

    def finish(in_refs, out_refs, send_sems, recv_sems):
        for cp in copies(in_refs, out_refs, send_sems, recv_sems):
            cp.wait()

    return _Comm(ins, out_shapes, n_sems, start, finish)


def _swap_comm(gs):
    def copies(g_refs, got_refs, send_sems, recv_sems):
        c = lax.axis_index("c")
        sibling = (lax.axis_index("x"), lax.axis_index("y"), 1 - c)
        return [pltpu.make_async_remote_copy(src_ref=g_refs[t].at[k, 1 - c], dst_ref=got_refs[t].at[k],
                                             send_sem=send_sems.at[N_CHIPS * t + k], recv_sem=recv_sems.at[N_CHIPS * t + k],
                                             device_id=sibling, device_id_type=MESH)
                for t in range(len(gs)) for k in range(N_CHIPS)]

    return _simple_comm(gs, [_sds((N_CHIPS,) + g.shape[2:], g.dtype) for g in gs], N_CHIPS * len(gs), copies)


def _scatter_comm(hs):
    def copies(h_refs, got_refs, send_sems, recv_sems):
        x, y, c = lax.axis_index("x"), lax.axis_index("y"), lax.axis_index("c")
        chips = [(1 - x, y), (x, 1 - y), (1 - x, 1 - y)]
        return [pltpu.make_async_remote_copy(src_ref=h_refs[t].at[2 * px + py], dst_ref=got_refs[t].at[rel],
                                             send_sem=send_sems.at[3 * t + rel], recv_sem=recv_sems.at[3 * t + rel],
                                             device_id=(px, py, c), device_id_type=MESH)
                for t in range(len(hs)) for rel, (px, py) in enumerate(chips)]

    return _simple_comm(hs, [_sds((3,) + h.shape[1:], h.dtype) for h in hs], 3 * len(hs), copies)


def _share_comm(rs):
    def copies(r_refs, got_refs, send_sems, recv_sems):
        sibling = (lax.axis_index("x"), lax.axis_index("y"), 1 - lax.axis_index("c"))
        return [pltpu.make_async_remote_copy(src_ref=r_refs[t], dst_ref=got_refs[t], send_sem=send_sems.at[t],
                                             recv_sem=recv_sems.at[t], device_id=sibling, device_id_type=MESH)
                for t in range(len(rs))]

    return _simple_comm(rs, [_sds(r.shape, r.dtype) for r in rs], len(rs), copies)


SUM_STEPS = 4


def _pair_sums(name, gs, gots, where):
    n = len(gs)
    dims = [(g.shape[2] // SUM_STEPS, g.shape[3]) for g in gs]

    def body(where_ref, *refs):
        own = pl.program_id(1) == where_ref[1]
        for t in range(n):
            s = refs[t][...] + refs[n + t][...]
            refs[3 * n + t][...] = s.astype(BF16)

            @pl.when(own)
            def _():
                refs[2 * n + t][...] = s

    def part(tr, pc):
        return pl.BlockSpec((None, tr, pc), lambda i, k, where_ref: (k, i, 0))

    def kept(tr, pc):
        return pl.BlockSpec((None, None, tr, pc), lambda i, k, where_ref: (k, where_ref[0], i, 0))

    outs = pl.pallas_call(
        body, name=name,
        out_shape=[_sds(g.shape[2:]) for g in gs] + [_sds((N_CHIPS,) + g.shape[2:], BF16) for g in gs],
        grid_spec=pltpu.PrefetchScalarGridSpec(
            num_scalar_prefetch=1, grid=(SUM_STEPS, N_CHIPS),
            in_specs=[kept(*d) for d in dims] + [part(*d) for d in dims],
            out_specs=[pl.BlockSpec(d, lambda i, k, where_ref: (i, 0)) for d in dims] + [part(*d) for d in dims]),
        compiler_params=_params(("parallel", "arbitrary")),
    )(where, *gs, *gots)
    return list(outs[:n]), list(outs[n:])


def _chips_sums(name, owns, gots):
    n = len(owns)
    dims = [(h.shape[0] // SUM_STEPS, h.shape[1]) for h in owns]

    def body(*refs):
        for t in range(n):
            got_ref = refs[n + t]
            refs[2 * n + t][...] = refs[t][...] + got_ref[0] + got_ref[1] + got_ref[2]

    row = [pl.BlockSpec((tr, pc), lambda i: (i, 0)) for tr, pc in dims]
    return pl.pallas_call(
        body, name=name, out_shape=[_sds(h.shape) for h in owns], grid=(SUM_STEPS,),
        in_specs=row + [pl.BlockSpec((3, tr, pc), lambda i: (0, i, 0)) for tr, pc in dims], out_specs=row,
        compiler_params=_params(("parallel",)),
    )(*owns, *gots)


class _Reduce:
    def __init__(self, tag, dws, c, chip):
        gp = _grad_pieces(dws)
        self.tag, self.c, self.names, self.pieces = tag, c, list(gp), list(gp.values())
        self.where = jnp.stack([c, chip]).astype(jnp.int32)
        self.shards = None

    def after_swap(self, gots):
        self.s32, self.s16 = _pair_sums(f"rs_pair_sum_{self.tag}", self.pieces, gots, self.where)

    def after_scatter(self, parts):
        self.mine = list(_chips_sums(f"rs_chips_sum_{self.tag}", self.s32, parts))

    def after_share(self, theirs):
        self.shards = {}
        for n, a, b in zip(self.names, self.mine, theirs):
            both = jnp.where(self.c == 0, jnp.concatenate([a, b], axis=0), jnp.concatenate([b, a], axis=0))
            self.shards[n] = both[:, :SHARD_COLS[n]]

    def run_alone(self):
        self.after_swap(_comm_call(f"rs_pair_{self.tag}", _swap_comm(self.pieces)))
        self.after_scatter(_comm_call(f"rs_chips_{self.tag}", _scatter_comm(self.s16)))
        self.after_share(_comm_call(f"rs_share_{self.tag}", _share_comm(self.mine)))


class _LayerWeights:
    def __init__(self, shards, c, dev):
        self.shards, self.c, self.dev = shards, c, dev

    def comm(self, l, names=W_NAMES):
        return _all_gather_comm([_my_piece(n, self.shards[l][n], self.c) for n in names])

    def take(self, l, gathered, names=W_NAMES):
        pieces = {}
        for n, g in zip(names, gathered):
            mine = _my_piece(n, self.shards[l][n], self.c)
            pieces[n] = lax.dynamic_update_index_in_dim(g, mine, self.dev, 0).reshape((N_CHIPS, 2) + mine.shape)
        return _weights_from_pieces(pieces)

    def first(self):
        now, later = ("w_in", "w_uq", "w_ukv"), ("w_ff1", "w_ff2", "w_o")
        weights = self.take(0, _comm_call("ag_l0", self.comm(0, now)), now)
        comms = [self.comm(0, (n,)) for n in later]

        def take_rest(results):
            rest = {}
            for n, gathered in zip(later, results):
                rest.update(self.take(0, gathered, (n,)))
            return rest

        return weights, (comms, take_rest)


def _all_reduce_small(vec):
    rows, lanes = vec.shape

    def gather_body(x_ref, out_ref, send_sems, recv_sems):
        x, y, c = lax.axis_index("x"), lax.axis_index("y"), lax.axis_index("c")
        me, sibling = (x, y, c), (x, y, 1 - c)
        chips = [(1 - x, y), (x, 1 - y), (1 - x, 1 - y)]

        def slot(px, py, pc):
            return out_ref.at[4 * px + 2 * py + pc]

        def copy(k, blk, to, src=None):
            return pltpu.make_async_remote_copy(
                src_ref=slot(*blk) if src is None else src, dst_ref=slot(*blk),
                send_sem=send_sems.at[k], recv_sem=recv_sems.at[k], device_id=to, device_id_type=MESH)

        out_ref[4 * x + 2 * y + c] = x_ref[...]
        first = [copy(0, me, sibling, src=x_ref)]
        first += [copy(1 + j, me, (*chip, c), src=x_ref) for j, chip in enumerate(chips)]
        for cp in first:
            cp.start()
        passed = [copy(4 + j, (*chip, c), sibling) for j, chip in enumerate(chips)]
        for j, chip in enumerate(chips):
            copy(1 + j, (*chip, c), me).wait_recv()
            passed[j].start()
        copy(0, sibling, me).wait_recv()
        for j, chip in enumerate(chips):
            copy(4 + j, (*chip, 1 - c), me).wait_recv()
        for cp in first + passed:
            cp.wait_send()

    vmem = pl.BlockSpec(memory_space=pltpu.VMEM)
    allv = pl.pallas_call(
        gather_body, name="small_all_gather", out_shape=_sds((N_DEV, rows, lanes)),
        in_specs=[vmem], out_specs=vmem,
        scratch_shapes=[pltpu.SemaphoreType.DMA((7,)), pltpu.SemaphoreType.DMA((7,))],
    )(vec)

    def sum_body(a_ref, o_ref):
        acc = a_ref[0]
        for d in range(1, N_DEV):
            acc = acc + a_ref[d]
        o_ref[...] = acc

    return pl.pallas_call(sum_body, name="small_sum", out_shape=_sds((rows, lanes)), in_specs=[vmem], out_specs=vmem)(allv)


def _adamw(name, w, g, m, v, ts=512, comm=None):
    def fn(w_v, g_v, m_v, v_v):
        m_n = ADAM_B1 * m_v + (1.0 - ADAM_B1) * g_v
        v_n = ADAM_B2 * v_v + (1.0 - ADAM_B2) * (g_v * g_v)
        m_hat = m_n / (1.0 - ADAM_B1 ** ADAM_STEP)
        v_hat = v_n / (1.0 - ADAM_B2 ** ADAM_STEP)
        delta = -ADAM_LR * (m_hat / (jnp.sqrt(v_hat) + ADAM_EPS) + ADAM_WD * w_v)
        return delta, m_n, v_n

    return _rowwise(name, fn, [w, g, m, v], [_sds(w.shape)] * 3, ts=ts, comm=comm)


def _pack_small(per_layer):
    flat = jnp.concatenate([per_layer[l][n].reshape(-1) for l in range(DEPTH) for n in SMALL_NAMES])
    return jnp.pad(flat, (0, SMALL_ROWS * LANES - flat.shape[0])).reshape(SMALL_ROWS, LANES)


def _unpack_small(packed):
    flat = packed.reshape(-1)
    per = sum(SMALL_SIZES)
    out = {}
    for n, size, off in zip(SMALL_NAMES, SMALL_SIZES, [sum(SMALL_SIZES[:i]) for i in range(len(SMALL_SIZES))]):
        out[n] = jnp.stack([flat[l * per + off:l * per + off + size] for l in range(DEPTH)])
    return out


NEXT_WEIGHTS_BEHIND = (("w_ff1", "w_ff2"), ("w_in", "w_uq", "w_ukv", "w_o"))


def _fwd_bwd(x, target, layer_weights, smalls, on_layer_grads):
    depth = len(smalls)
    cos, sin = _rope_tables(x.shape[0])
    h_f32, h_bf = x, x.astype(BF16)
    saved = []
    w, rest = layer_weights.first()
    for l in range(depth):
        more = l + 1 < depth
        behind = [layer_weights.comm(l + 1, names) if more else None for names in NEXT_WEIGHTS_BEHIND]
        h_f32, h_bf, sv, brought = _layer_fwd(h_f32, h_bf, w, smalls[l], cos, sin, behind, rest)
        saved.append(sv)
        if more:
            w, rest = {}, None
            for names, got in zip(NEXT_WEIGHTS_BEHIND, brought):
                w.update(layer_weights.take(l + 1, got, names))
    loss_part, dh = _loss_fn(h_f32, target)
    small_grads, pending = [None] * depth, None
    for l in reversed(range(depth)):
        dh, small_grads[l], pending = _layer_bwd(dh, saved[l]["w"], smalls[l], saved[l], cos, sin,
                                                 lambda dws, l=l: on_layer_grads(l, dws), pending)
    return loss_part, dh, small_grads, pending


def kernel(x, w_in, q_a_norm, kv_a_norm, w_uq, w_ukv, a_out_norm, b_out_norm, w_o, ln1_g, ln1_b, w_ff1, w_ff2, ln2_g, ln2_b, loss_target, m_w_in, m_q_a_norm, m_kv_a_norm, m_w_uq, m_w_ukv, m_a_out_norm, m_b_out_norm, m_w_o, m_ln1_g, m_ln1_b, m_w_ff1, m_w_ff2, m_ln2_g, m_ln2_b, v_w_in, v_q_a_norm, v_kv_a_norm, v_w_uq, v_w_ukv, v_a_out_norm, v_b_out_norm, v_w_o, v_ln1_g, v_ln1_b, v_w_ff1, v_w_ff2, v_ln2_g, v_ln2_b):
    c = lax.axis_index("c")
    chip = 2 * lax.axis_index("x") + lax.axis_index("y")
    big = dict(w_in=w_in, w_uq=w_uq, w_ukv=w_ukv, w_o=w_o, w_ff1=w_ff1, w_ff2=w_ff2)
    big_m = dict(w_in=m_w_in, w_uq=m_w_uq, w_ukv=m_w_ukv, w_o=m_w_o, w_ff1=m_w_ff1, w_ff2=m_w_ff2)
    big_v = dict(w_in=v_w_in, w_uq=v_w_uq, w_ukv=v_w_ukv, w_o=v_w_o, w_ff1=v_w_ff1, w_ff2=v_w_ff2)
    small = dict(q_a_norm=q_a_norm, kv_a_norm=kv_a_norm, a_out_norm=a_out_norm, b_out_norm=b_out_norm,
                 ln1_g=ln1_g, ln1_b=ln1_b, ln2_g=ln2_g, ln2_b=ln2_b)
    small_m = dict(q_a_norm=m_q_a_norm, kv_a_norm=m_kv_a_norm, a_out_norm=m_a_out_norm, b_out_norm=m_b_out_norm,
                   ln1_g=m_ln1_g, ln1_b=m_ln1_b, ln2_g=m_ln2_g, ln2_b=m_ln2_b)
    small_v = dict(q_a_norm=v_q_a_norm, kv_a_norm=v_kv_a_norm, a_out_norm=v_a_out_norm, b_out_norm=v_b_out_norm,
                   ln1_g=v_ln1_g, ln1_b=v_ln1_b, ln2_g=v_ln2_g, ln2_b=v_ln2_b)

    layer_weights = _LayerWeights([{n: big[n][l] for n in W_NAMES} for l in range(DEPTH)], c, 2 * chip + c)
    smalls = [{n: small[n][l][None, :] for n in SMALL_NAMES} for l in range(DEPTH)]
    reductions = [[] for _ in range(DEPTH)]

    def reduce_layer(l, dws):
        reductions[l].append(_Reduce(f"l{l}_{len(reductions[l])}", dws, c, chip))
        return reductions[l][-1]

    loss_part, grad_x, small_grads, last = _fwd_bwd(x[0], loss_target[0], layer_weights, smalls, reduce_layer)
    loss = lax.psum(loss_part[0, 0], ("x", "y", "c"))
    grad_x = grad_x[None]

    g_small_packed = _all_reduce_small(_pack_small(small_grads))
    pack_in = lambda d: _pack_small([{n: d[n][l] for n in SMALL_NAMES} for l in range(DEPTH)])
    sd, sm_, sv_ = _adamw("adamw_small", pack_in(small), g_small_packed, pack_in(small_m), pack_in(small_v), ts=SMALL_ROWS)
    g_small, d_small, m_small, v_small = (_unpack_small(t) for t in (g_small_packed, sd, sm_, sv_))

    last.run_alone()
    g_big, d_big, m_big, v_big = {}, {}, {}, {}
    for n in W_NAMES:
        g = jnp.stack([next(r.shards[n] for r in reductions[l] if n in r.names) for l in range(DEPTH)])
        shape = g.shape
        flat = lambda t: t.reshape(shape[0] * shape[1], shape[2])
        d_, m_, v_ = _adamw(f"adamw_{n}", flat(big[n]), flat(g), flat(big_m[n]), flat(big_v[n]))
        g_big[n], d_big[n], m_big[n], v_big[n] = g, d_.reshape(shape), m_.reshape(shape), v_.reshape(shape)

    order = ("w_in", "q_a_norm", "kv_a_norm", "w_uq", "w_ukv", "a_out_norm", "b_out_norm", "w_o", "ln1_g", "ln1_b",
             "w_ff1", "w_ff2", "ln2_g", "ln2_b")
    pick = lambda bigd, smalld: [bigd[n] if n in bigd else smalld[n] for n in order]
    return (loss, grad_x, *pick(g_big, g_small), *pick(d_big, d_small), *pick(m_big, m_small), *pick(v_big, v_small))
```
